```python
import math
import jax, jax.numpy as jnp
from jax import lax
import numpy as np

D_MODEL = 1024
BATCH = 8
SEQ = 4096
DEPTH = 1

N_META = 16
CONV_WIDTH = D_MODEL
CONV_KERNEL = 31
POOL_WIDTH = D_MODEL
POOL_WINDOWS = (2, 4, 8, 16)
N_POOL_GROUPS = len(POOL_WINDOWS)
POOL_GROUP_DIM = POOL_WIDTH // N_POOL_GROUPS
N_BRANCHES = 2
D_FF = int(math.ceil(8 * D_MODEL / 3 / 256) * 256)
D_IN = 2 * CONV_WIDTH + POOL_WIDTH + N_BRANCHES * D_MODEL
RMS_EPS = 1e-6
LN_EPS = 1e-5

kernel_name = "hybrid_conv_pool_gated_encoder_block"


def rms_norm(x, g):
    xf = x.astype(jnp.float32)
    y = xf * lax.rsqrt(jnp.mean(xf * xf, axis=-1, keepdims=True) + RMS_EPS)
    return (y * g.astype(jnp.float32)).astype(x.dtype)


def layer_norm(x, g, b):
    xf = x.astype(jnp.float32)
    mu = jnp.mean(xf, axis=-1, keepdims=True)
    xc = xf - mu
    var = jnp.mean(xc * xc, axis=-1, keepdims=True)
    y = xc * lax.rsqrt(var + LN_EPS)
    return (y * g.astype(jnp.float32) + b.astype(jnp.float32)).astype(x.dtype)


def conformer_conv(a_val, a_gate, w_dw, b_dw, ln_g, ln_b, w_conv_out):
    a = a_val * jax.nn.sigmoid(a_gate)
    pad = CONV_KERNEL // 2
    a = lax.conv_general_dilated(
        a, w_dw[:, None, :].astype(a.dtype),
        window_strides=(1,), padding=[(pad, pad)],
        dimension_numbers=("NWC", "WIO", "NWC"),
        feature_group_count=a.shape[-1]) + b_dw
    a = jax.nn.silu(layer_norm(a, ln_g, ln_b))
    return a @ w_conv_out


def multi_scale_pool(p, w_pool, pool_scale, w_pool_out):
    B, T, C = p.shape
    pf = p.astype(jnp.float32)
    cs = jnp.concatenate([jnp.zeros((B, 1, C), jnp.float32), jnp.cumsum(pf, axis=1)], axis=1)
    t = jnp.arange(T)
    outs = []
    for k, w in enumerate(POOL_WINDOWS):
        left = w // 2
        right = w - 1 - left
        lo = jnp.clip(t - left, 0, T)
        hi = jnp.clip(t + right + 1, 0, T)
        sl = slice(k * POOL_GROUP_DIM, (k + 1) * POOL_GROUP_DIM)
        csg = cs[..., sl]
        s = jnp.take(csg, hi, axis=1) - jnp.take(csg, lo, axis=1)
        cnt = (hi - lo).astype(jnp.float32)[None, :, None]
        outs.append(s / cnt - pf[..., sl])
    m = jnp.stack(outs, axis=2).astype(p.dtype)
    m = jnp.einsum("btgc,gcd->btgd", m, w_pool).reshape(B, T, C)
    return (m * pool_scale) @ w_pool_out


def _fwd_setup_inputs(seed: int = 0) -> dict:
    key = jax.random.key(seed)
    ks = jax.random.split(key, 24)
    f32 = jnp.float32
    L, D = DEPTH, D_MODEL

    def nrm(k, shape, fan_in):
        return jax.random.normal(k, shape, f32) * (fan_in ** -0.5)

    def gain(k, shape):
        return 1.0 + 0.05 * jax.random.normal(k, shape, f32)

    return {
        "x": jax.random.normal(ks[0], (BATCH, SEQ, D), f32),
        "meta_tokens": jax.random.normal(ks[1], (N_META, D), f32),
        "g_mix": gain(ks[2], (L, D)),
        "w_in": nrm(ks[3], (L, D, D_IN), D),
        "b_gate": 0.02 * jax.random.normal(ks[4], (L, N_BRANCHES * D), f32),
        "w_dw": nrm(ks[5], (L, CONV_KERNEL, CONV_WIDTH), CONV_KERNEL),
        "b_dw": 0.02 * jax.random.normal(ks[6], (L, CONV_WIDTH), f32),
        "ln_g": gain(ks[7], (L, CONV_WIDTH)),
        "ln_b": 0.02 * jax.random.normal(ks[8], (L, CONV_WIDTH), f32),
        "w_conv_out": nrm(ks[9], (L, CONV_WIDTH, D), CONV_WIDTH),
        "w_pool": nrm(ks[10], (L, N_POOL_GROUPS, POOL_GROUP_DIM, POOL_GROUP_DIM), POOL_GROUP_DIM),
        "pool_scale": gain(ks[11], (L, POOL_WIDTH)),
        "w_pool_out": nrm(ks[12], (L, POOL_WIDTH, D), POOL_WIDTH),
        "w_o": nrm(ks[13], (L, D, D), D),
        "g_ffn": gain(ks[14], (L, D)),
        "w_ffn_gate": nrm(ks[15], (L, D, D_FF), D),
        "w_ffn_up": nrm(ks[16], (L, D, D_FF), D),
        "w_ffn_down": nrm(ks[17], (L, D_FF, D), D_FF),
        "g_final": gain(ks[18], (D,)),
    }


def _fwd_reference(x, meta_tokens, g_mix, w_in, b_gate, w_dw, b_dw, ln_g, ln_b, w_conv_out,
              w_pool, pool_scale, w_pool_out, w_o, g_ffn, w_ffn_gate, w_ffn_up,
              w_ffn_down, g_final):
    B = x.shape[0]
    meta = jnp.broadcast_to(meta_tokens[None].astype(x.dtype), (B, N_META, D_MODEL))
    h = jnp.concatenate([meta, x], axis=1)
    c1 = CONV_WIDTH
    c2 = 2 * CONV_WIDTH
    c3 = c2 + POOL_WIDTH
    c4 = c3 + D_MODEL
    for l in range(DEPTH):
        u = rms_norm(h, g_mix[l])
        z = u @ w_in[l]
        gates = jax.nn.sigmoid(z[..., c3:] + b_gate[l])
        y_conv = conformer_conv(z[..., :c1], z[..., c1:c2], w_dw[l], b_dw[l],
                                ln_g[l], ln_b[l], w_conv_out[l])
        y_pool = multi_scale_pool(z[..., c2:c3], w_pool[l], pool_scale[l], w_pool_out[l])
        merged = gates[..., :D_MODEL] * y_conv + gates[..., D_MODEL:] * y_pool
        h = h + merged @ w_o[l]
        v = rms_norm(h, g_ffn[l])
        f = jax.nn.silu(v @ w_ffn_gate[l]) * (v @ w_ffn_up[l])
        h = h + f @ w_ffn_down[l]
    h = rms_norm(h, g_final)
    return h[:, N_META:, :]


import jax as _jax
import jax.numpy as _jnp

TWIN_FORMAT = 'train_step'
FWD_PARAMS = ['x', 'meta_tokens', 'g_mix', 'w_in', 'b_gate', 'w_dw', 'b_dw', 'ln_g', 'ln_b', 'w_conv_out', 'w_pool', 'pool_scale', 'w_pool_out', 'w_o', 'g_ffn', 'w_ffn_gate', 'w_ffn_up', 'w_ffn_down', 'g_final']
TWIN_WEIGHTS = ['meta_tokens', 'g_mix', 'w_in', 'b_gate', 'w_dw', 'b_dw', 'ln_g', 'ln_b', 'w_conv_out', 'w_pool', 'pool_scale', 'w_pool_out', 'w_o', 'g_ffn', 'w_ffn_gate', 'w_ffn_up', 'w_ffn_down', 'g_final']
TWIN_DIFF_INPUT = 'x'
TWIN_INPUTS = ['x', 'meta_tokens', 'g_mix', 'w_in', 'b_gate', 'w_dw', 'b_dw', 'ln_g', 'ln_b', 'w_conv_out', 'w_pool', 'pool_scale', 'w_pool_out', 'w_o', 'g_ffn', 'w_ffn_gate', 'w_ffn_up', 'w_ffn_down', 'g_final', 'loss_target', 'm_meta_tokens', 'm_g_mix', 'm_w_in', 'm_b_gate', 'm_w_dw', 'm_b_dw', 'm_ln_g', 'm_ln_b', 'm_w_conv_out', 'm_w_pool', 'm_pool_scale', 'm_w_pool_out', 'm_w_o', 'm_g_ffn', 'm_w_ffn_gate', 'm_w_ffn_up', 'm_w_ffn_down', 'm_g_final', 'v_meta_tokens', 'v_g_mix', 'v_w_in', 'v_b_gate', 'v_w_dw', 'v_b_dw', 'v_ln_g', 'v_ln_b', 'v_w_conv_out', 'v_w_pool', 'v_pool_scale', 'v_w_pool_out', 'v_w_o', 'v_g_ffn', 'v_w_ffn_gate', 'v_w_ffn_up', 'v_w_ffn_down', 'v_g_final']
TWIN_OUTPUTS = ['loss', 'grad_x', 'grad_meta_tokens', 'grad_g_mix', 'grad_w_in', 'grad_b_gate', 'grad_w_dw', 'grad_b_dw', 'grad_ln_g', 'grad_ln_b', 'grad_w_conv_out', 'grad_w_pool', 'grad_pool_scale', 'grad_w_pool_out', 'grad_w_o', 'grad_g_ffn', 'grad_w_ffn_gate', 'grad_w_ffn_up', 'grad_w_ffn_down', 'grad_g_final', 'delta_meta_tokens', 'delta_g_mix', 'delta_w_in', 'delta_b_gate', 'delta_w_dw', 'delta_b_dw', 'delta_ln_g', 'delta_ln_b', 'delta_w_conv_out', 'delta_w_pool', 'delta_pool_scale', 'delta_w_pool_out', 'delta_w_o', 'delta_g_ffn', 'delta_w_ffn_gate', 'delta_w_ffn_up', 'delta_w_ffn_down', 'delta_g_final', 'new_m_meta_tokens', 'new_m_g_mix', 'new_m_w_in', 'new_m_b_gate', 'new_m_w_dw', 'new_m_b_dw', 'new_m_ln_g', 'new_m_ln_b', 'new_m_w_conv_out', 'new_m_w_pool', 'new_m_pool_scale', 'new_m_w_pool_out', 'new_m_w_o', 'new_m_g_ffn', 'new_m_w_ffn_gate', 'new_m_w_ffn_up', 'new_m_w_ffn_down', 'new_m_g_final', 'new_v_meta_tokens', 'new_v_g_mix', 'new_v_w_in', 'new_v_b_gate', 'new_v_w_dw', 'new_v_b_dw', 'new_v_ln_g', 'new_v_ln_b', 'new_v_w_conv_out', 'new_v_w_pool', 'new_v_pool_scale', 'new_v_w_pool_out', 'new_v_w_o', 'new_v_g_ffn', 'new_v_w_ffn_gate', 'new_v_w_ffn_up', 'new_v_w_ffn_down', 'new_v_g_final']
TWIN_LEAF_KINDS = {'loss': 'loss', 'grad_x': 'grad_x', 'grad_meta_tokens': 'grad_w', 'grad_g_mix': 'grad_w', 'grad_w_in': 'grad_w', 'grad_b_gate': 'grad_w', 'grad_w_dw': 'grad_w', 'grad_b_dw': 'grad_w', 'grad_ln_g': 'grad_w', 'grad_ln_b': 'grad_w', 'grad_w_conv_out': 'grad_w', 'grad_w_pool': 'grad_w', 'grad_pool_scale': 'grad_w', 'grad_w_pool_out': 'grad_w', 'grad_w_o': 'grad_w', 'grad_g_ffn': 'grad_w', 'grad_w_ffn_gate': 'grad_w', 'grad_w_ffn_up': 'grad_w', 'grad_w_ffn_down': 'grad_w', 'grad_g_final': 'grad_w', 'delta_meta_tokens': 'delta_w', 'delta_g_mix': 'delta_w', 'delta_w_in': 'delta_w', 'delta_b_gate': 'delta_w', 'delta_w_dw': 'delta_w', 'delta_b_dw': 'delta_w', 'delta_ln_g': 'delta_w', 'delta_ln_b': 'delta_w', 'delta_w_conv_out': 'delta_w', 'delta_w_pool': 'delta_w', 'delta_pool_scale': 'delta_w', 'delta_w_pool_out': 'delta_w', 'delta_w_o': 'delta_w', 'delta_g_ffn': 'delta_w', 'delta_w_ffn_gate': 'delta_w', 'delta_w_ffn_up': 'delta_w', 'delta_w_ffn_down': 'delta_w', 'delta_g_final': 'delta_w', 'new_m_meta_tokens': 'new_m', 'new_m_g_mix': 'new_m', 'new_m_w_in': 'new_m', 'new_m_b_gate': 'new_m', 'new_m_w_dw': 'new_m', 'new_m_b_dw': 'new_m', 'new_m_ln_g': 'new_m', 'new_m_ln_b': 'new_m', 'new_m_w_conv_out': 'new_m', 'new_m_w_pool': 'new_m', 'new_m_pool_scale': 'new_m', 'new_m_w_pool_out': 'new_m', 'new_m_w_o': 'new_m', 'new_m_g_ffn': 'new_m', 'new_m_w_ffn_gate': 'new_m', 'new_m_w_ffn_up': 'new_m', 'new_m_w_ffn_down': 'new_m', 'new_m_g_final': 'new_m', 'new_v_meta_tokens': 'new_v', 'new_v_g_mix': 'new_v', 'new_v_w_in': 'new_v', 'new_v_b_gate': 'new_v', 'new_v_w_dw': 'new_v', 'new_v_b_dw': 'new_v', 'new_v_ln_g': 'new_v', 'new_v_ln_b': 'new_v', 'new_v_w_conv_out': 'new_v', 'new_v_w_pool': 'new_v', 'new_v_pool_scale': 'new_v', 'new_v_w_pool_out': 'new_v', 'new_v_w_o': 'new_v', 'new_v_g_ffn': 'new_v', 'new_v_w_ffn_gate': 'new_v', 'new_v_w_ffn_up': 'new_v', 'new_v_w_ffn_down': 'new_v', 'new_v_g_final': 'new_v'}


def _forward(args):
    return _fwd_reference(*[args[k] for k in FWD_PARAMS])


def _output_shape():
    def fwd():
        inp = _fwd_setup_inputs(0)
        return _fwd_reference(*[inp[k] for k in FWD_PARAMS])
    out = _jax.eval_shape(fwd)
    return out.shape, out.dtype

N_MICROBATCH = 1
ADAM_LR = 0.001
ADAM_B1 = 0.9
ADAM_B2 = 0.999
ADAM_EPS = 1e-08
ADAM_WD = 0.01
ADAM_STEP = 10
PER_EXAMPLE_BATCH_AXIS = {'x': 0, 'loss_target': 0}
SHARED_INPUTS = []
_WEIGHT_DTYPES = {'meta_tokens': _jnp.float32, 'g_mix': _jnp.float32, 'w_in': _jnp.float32, 'b_gate': _jnp.float32, 'w_dw': _jnp.float32, 'b_dw': _jnp.float32, 'ln_g': _jnp.float32, 'ln_b': _jnp.float32, 'w_conv_out': _jnp.float32, 'w_pool': _jnp.float32, 'pool_scale': _jnp.float32, 'w_pool_out': _jnp.float32, 'w_o': _jnp.float32, 'g_ffn': _jnp.float32, 'w_ffn_gate': _jnp.float32, 'w_ffn_up': _jnp.float32, 'w_ffn_down': _jnp.float32, 'g_final': _jnp.float32}
MOMENT_SCALE = {'meta_tokens': 2.021343e-03, 'g_mix': 1.116090e-01, 'w_in': 4.934132e-02, 'b_gate': 2.851630e-02, 'w_dw': 5.772599e-02, 'b_dw': 1.283222e-01, 'ln_g': 8.017598e-02, 'ln_b': 7.054623e-02, 'w_conv_out': 5.711149e-02, 'w_pool': 8.307347e-02, 'pool_scale': 8.251336e-02, 'w_pool_out': 8.379146e-02, 'w_o': 1.024417e-01, 'g_ffn': 1.219685e-01, 'w_ffn_gate': 5.112737e-02, 'w_ffn_up': 4.995319e-02, 'w_ffn_down': 8.313621e-02, 'g_final': 3.207642e+01}


def _to_microbatches(a, axis):
    t = _jnp.moveaxis(a, axis, 0)
    t = t.reshape((N_MICROBATCH, t.shape[0] // N_MICROBATCH) + t.shape[1:])
    return _jnp.moveaxis(t, 1, axis + 1)


def setup_inputs(seed: int = 0) -> dict:
    inp = _fwd_setup_inputs(seed)
    key = _jax.random.fold_in(_jax.random.key(seed), 7919)
    shape, _ = _output_shape()
    out = dict(inp)
    out["loss_target"] = _jax.random.normal(_jax.random.fold_in(key, 0), shape, _jnp.float32)
    for i, name in enumerate(TWIN_WEIGHTS):
        w = inp[name].astype(_jnp.float32)
        if MOMENT_SCALE is None:
            s = _jnp.sqrt(_jnp.mean(_jnp.square(w)) + 1e-30)
        else:
            s = MOMENT_SCALE[name]
        km, kv = _jax.random.split(_jax.random.fold_in(key, i + 1))
        out[name] = w
        out["m_" + name] = s * _jax.random.normal(km, w.shape, _jnp.float32)
        out["v_" + name] = (s * s) * _jax.random.uniform(kv, w.shape, _jnp.float32, 0.5, 1.5)
    if N_MICROBATCH > 1:
        for name, axis in PER_EXAMPLE_BATCH_AXIS.items():
            out[name] = _to_microbatches(out[name], axis)
    return {'x': out['x'], 'meta_tokens': out['meta_tokens'], 'g_mix': out['g_mix'], 'w_in': out['w_in'], 'b_gate': out['b_gate'], 'w_dw': out['w_dw'], 'b_dw': out['b_dw'], 'ln_g': out['ln_g'], 'ln_b': out['ln_b'], 'w_conv_out': out['w_conv_out'], 'w_pool': out['w_pool'], 'pool_scale': out['pool_scale'], 'w_pool_out': out['w_pool_out'], 'w_o': out['w_o'], 'g_ffn': out['g_ffn'], 'w_ffn_gate': out['w_ffn_gate'], 'w_ffn_up': out['w_ffn_up'], 'w_ffn_down': out['w_ffn_down'], 'g_final': out['g_final'], 'loss_target': out['loss_target'], 'm_meta_tokens': out['m_meta_tokens'], 'm_g_mix': out['m_g_mix'], 'm_w_in': out['m_w_in'], 'm_b_gate': out['m_b_gate'], 'm_w_dw': out['m_w_dw'], 'm_b_dw': out['m_b_dw'], 'm_ln_g': out['m_ln_g'], 'm_ln_b': out['m_ln_b'], 'm_w_conv_out': out['m_w_conv_out'], 'm_w_pool': out['m_w_pool'], 'm_pool_scale': out['m_pool_scale'], 'm_w_pool_out': out['m_w_pool_out'], 'm_w_o': out['m_w_o'], 'm_g_ffn': out['m_g_ffn'], 'm_w_ffn_gate': out['m_w_ffn_gate'], 'm_w_ffn_up': out['m_w_ffn_up'], 'm_w_ffn_down': out['m_w_ffn_down'], 'm_g_final': out['m_g_final'], 'v_meta_tokens': out['v_meta_tokens'], 'v_g_mix': out['v_g_mix'], 'v_w_in': out['v_w_in'], 'v_b_gate': out['v_b_gate'], 'v_w_dw': out['v_w_dw'], 'v_b_dw': out['v_b_dw'], 'v_ln_g': out['v_ln_g'], 'v_ln_b': out['v_ln_b'], 'v_w_conv_out': out['v_w_conv_out'], 'v_w_pool': out['v_w_pool'], 'v_pool_scale': out['v_pool_scale'], 'v_w_pool_out': out['v_w_pool_out'], 'v_w_o': out['v_w_o'], 'v_g_ffn': out['v_g_ffn'], 'v_w_ffn_gate': out['v_w_ffn_gate'], 'v_w_ffn_up': out['v_w_ffn_up'], 'v_w_ffn_down': out['v_w_ffn_down'], 'v_g_final': out['v_g_final']}


def _loss(weights, diff, rest, loss_target):
    with _jax.named_scope("forward"):
        args = {**rest, TWIN_DIFF_INPUT: diff, **{k: w.astype(_WEIGHT_DTYPES[k]) for k, w in weights.items()}}
        y = _forward(args)
    with _jax.named_scope("loss_head"):
        err = _jnp.square(y.astype(_jnp.float32) - loss_target)
        return 0.5 * _jnp.sum(_jnp.mean(err, axis=-1)) if err.ndim else 0.5 * err


def _adamw(w, g, m, v):
    m = ADAM_B1 * m + (1.0 - ADAM_B1) * g
    v = ADAM_B2 * v + (1.0 - ADAM_B2) * _jnp.square(g)
    m_hat = m / (1.0 - ADAM_B1 ** ADAM_STEP)
    v_hat = v / (1.0 - ADAM_B2 ** ADAM_STEP)
    delta = -ADAM_LR * (m_hat / (_jnp.sqrt(v_hat) + ADAM_EPS) + ADAM_WD * w)
    return delta, m, v


def reference(x, meta_tokens, g_mix, w_in, b_gate, w_dw, b_dw, ln_g, ln_b, w_conv_out, w_pool, pool_scale, w_pool_out, w_o, g_ffn, w_ffn_gate, w_ffn_up, w_ffn_down, g_final, loss_target, m_meta_tokens, m_g_mix, m_w_in, m_b_gate, m_w_dw, m_b_dw, m_ln_g, m_ln_b, m_w_conv_out, m_w_pool, m_pool_scale, m_w_pool_out, m_w_o, m_g_ffn, m_w_ffn_gate, m_w_ffn_up, m_w_ffn_down, m_g_final, v_meta_tokens, v_g_mix, v_w_in, v_b_gate, v_w_dw, v_b_dw, v_ln_g, v_ln_b, v_w_conv_out, v_w_pool, v_pool_scale, v_w_pool_out, v_w_o, v_g_ffn, v_w_ffn_gate, v_w_ffn_up, v_w_ffn_down, v_g_final):
    given = dict(x=x, meta_tokens=meta_tokens, g_mix=g_mix, w_in=w_in, b_gate=b_gate, w_dw=w_dw, b_dw=b_dw, ln_g=ln_g, ln_b=ln_b, w_conv_out=w_conv_out, w_pool=w_pool, pool_scale=pool_scale, w_pool_out=w_pool_out, w_o=w_o, g_ffn=g_ffn, w_ffn_gate=w_ffn_gate, w_ffn_up=w_ffn_up, w_ffn_down=w_ffn_down, g_final=g_final, loss_target=loss_target, m_meta_tokens=m_meta_tokens, m_g_mix=m_g_mix, m_w_in=m_w_in, m_b_gate=m_b_gate, m_w_dw=m_w_dw, m_b_dw=m_b_dw, m_ln_g=m_ln_g, m_ln_b=m_ln_b, m_w_conv_out=m_w_conv_out, m_w_pool=m_w_pool, m_pool_scale=m_pool_scale, m_w_pool_out=m_w_pool_out, m_w_o=m_w_o, m_g_ffn=m_g_ffn, m_w_ffn_gate=m_w_ffn_gate, m_w_ffn_up=m_w_ffn_up, m_w_ffn_down=m_w_ffn_down, m_g_final=m_g_final, v_meta_tokens=v_meta_tokens, v_g_mix=v_g_mix, v_w_in=v_w_in, v_b_gate=v_b_gate, v_w_dw=v_w_dw, v_b_dw=v_b_dw, v_ln_g=v_ln_g, v_ln_b=v_ln_b, v_w_conv_out=v_w_conv_out, v_w_pool=v_w_pool, v_pool_scale=v_pool_scale, v_w_pool_out=v_w_pool_out, v_w_o=v_w_o, v_g_ffn=v_g_ffn, v_w_ffn_gate=v_w_ffn_gate, v_w_ffn_up=v_w_ffn_up, v_w_ffn_down=v_w_ffn_down, v_g_final=v_g_final)
    weights = {n: given[n] for n in TWIN_WEIGHTS}
    shared = {n: given[n] for n in SHARED_INPUTS}
    per_example = {n: given[n] for n in ['x']}
    grad_fn = _jax.value_and_grad(_loss, argnums=(0, 1))

    def one_microbatch(ex, loss_target):
        ex = dict(ex)
        diff = ex.pop(TWIN_DIFF_INPUT)
        return grad_fn(weights, diff, {**shared, **ex}, loss_target)

    if N_MICROBATCH == 1:
        loss, (grad_w, grad_x) = one_microbatch(per_example, given["loss_target"])
    else:
        def body(carry, xs):
            loss_sum, grad_sum = carry
            l_k, (gw_k, gx_k) = one_microbatch(xs[0], xs[1])
            with _jax.named_scope("update"):
                return (loss_sum + l_k, _jax.tree.map(_jnp.add, grad_sum, gw_k)), gx_k

        init = (_jnp.zeros((), _jnp.float32), _jax.tree.map(_jnp.zeros_like, weights))
        (loss, grad_w), grad_x = _jax.lax.scan(body, init, (per_example, given["loss_target"]))
    with _jax.named_scope("update"):
        delta_w, new_m, new_v = {}, {}, {}
        for n in TWIN_WEIGHTS:
            delta_w[n], new_m[n], new_v[n] = _adamw(weights[n], grad_w[n], given["m_" + n], given["v_" + n])
    return (loss, grad_x, *[grad_w[n] for n in TWIN_WEIGHTS], *[delta_w[n] for n in TWIN_WEIGHTS],
            *[new_m[n] for n in TWIN_WEIGHTS], *[new_v[n] for n in TWIN_WEIGHTS])
```

```python
import functools

import jax
import jax.numpy as jnp
from jax import lax
from jax.experimental import pallas as pl
from jax.experimental.pallas import tpu as pltpu

F32, BF16 = jnp.float32, jnp.bfloat16
MESH_ID = pl.DeviceIdType.MESH
AXES = ("x", "y", "c")
NDEV = 8

D = 1024
N_META = 16
CONV_K = 31
HALO = 16
POOL_WINDOWS = (2, 4, 8, 16)
PG = 256
DIN = 5 * D
DFF = 2816
FFB = DFF // NDEV
FFC = DFF // 2
INB = DIN // NDEV
RMS_EPS = 1e-6
LN_EPS = 1e-5
ADAM_LR, ADAM_B1, ADAM_B2, ADAM_EPS, ADAM_WD, ADAM_STEP = 0.001, 0.9, 0.999, 1e-08, 0.01, 10

TM = 384
TMS = 192
RB, CW = 32, 256
MIB = 2 ** 20


def _sig(x):
    return 1.0 / (1.0 + jnp.exp(-x))


def _dot(a, b):
    return jnp.dot(a, b, preferred_element_type=F32)


def _dot_nt(a, b):
    return lax.dot_general(a, b, (((1,), (1,)), ((), ())), preferred_element_type=F32)


def _dot_tn(a, b):
    return lax.dot_general(a, b, (((0,), (0,)), ((), ())), preferred_element_type=F32)


def _params(sem, vmem_mib):
    return pltpu.CompilerParams(dimension_semantics=sem, vmem_limit_bytes=vmem_mib * MIB)


def _load_once(first, pairs, sems):
    @pl.when(first)
    def _():
        cps = [pltpu.make_async_copy(s, d, sems.at[k]) for k, (s, d) in enumerate(pairs)]
        for cp in cps:
            cp.start()
        for cp in cps:
            cp.wait()


def _place():
    x, y, c = lax.axis_index("x"), lax.axis_index("y"), lax.axis_index("c")
    return x, y, c


def _all_gather(shards, dtypes):
    n = len(shards)

    def body(*refs):
        ins, outs, stages = refs[:n], refs[n:2 * n], refs[2 * n:3 * n]
        send_sems, recv_sems, local_sems = refs[3 * n:]
        x, y, c = _place()
        me, sibling = (x, y, c), (x, y, 1 - c)
        chips = [(1 - x, y), (x, 1 - y), (1 - x, 1 - y)]

        def idx(p):
            return 4 * p[0] + 2 * p[1] + p[2]

        def copy(w, k, block, to, src=None):
            dst = outs[w].at[idx(block)]
            return pltpu.make_async_remote_copy(
                src_ref=dst if src is None else src, dst_ref=dst,
                send_sem=send_sems.at[7 * w + k], recv_sem=recv_sems.at[7 * w + k],
                device_id=to, device_id_type=MESH_ID)

        mine, first = [], []
        for w in range(n):
            stages[w][...] = ins[w][...].astype(stages[w].dtype)
            cp = pltpu.make_async_copy(stages[w], outs[w].at[idx(me)], local_sems.at[w])
            cp.start()
            mine.append(cp)
            cps = [copy(w, 0, me, sibling, src=stages[w])]
            cps += [copy(w, 1 + j, me, (*chip, c), src=stages[w]) for j, chip in enumerate(chips)]
            for cp in cps:
                cp.start()
            first += cps
        passed = []
        for w in range(n):
            for j, chip in enumerate(chips):
                copy(w, 1 + j, (*chip, c), me).wait_recv()
                cp = copy(w, 4 + j, (*chip, c), sibling)
                cp.start()
                passed.append(cp)
        for w in range(n):
            copy(w, 0, sibling, me).wait_recv()
            for j, chip in enumerate(chips):
                copy(w, 4 + j, (*chip, 1 - c), me).wait_recv()
        for cp in first + passed:
            cp.wait_send()
        for cp in mine:
            cp.wait()

    return pl.pallas_call(
        body, name="ag_weights",
        out_shape=[jax.ShapeDtypeStruct((NDEV, *s.shape), dt) for s, dt in zip(shards, dtypes)],
        in_specs=[pl.BlockSpec(memory_space=pltpu.VMEM)] * n,
        out_specs=[pl.BlockSpec(memory_space=pl.ANY)] * n,
        scratch_shapes=[pltpu.VMEM(s.shape, dt) for s, dt in zip(shards, dtypes)]
        + [pltpu.SemaphoreType.DMA((7 * n,)), pltpu.SemaphoreType.DMA((7 * n,)), pltpu.SemaphoreType.DMA((n,))],
        compiler_params=pltpu.CompilerParams(vmem_limit_bytes=40 * MIB),
    )(*shards)


def _reduce_scatter(parts, small):
    n = len(parts)
    blks = [p.shape[1:] for p in parts]

    def body(*refs):
        ps, small_ref = refs[:n], refs[n]
        o = n + 1
        owns, sibs, rels, small_out = refs[o:o + n], refs[o + n:o + 2 * n], refs[o + 2 * n:o + 3 * n], refs[o + 3 * n]
        o += 3 * n + 1
        pa, pb, qst = refs[o:o + n], refs[o + n:o + 2 * n], refs[o + 2 * n:o + 3 * n]
        s1_send, s1_recv, s2_send, s2_recv, sm_send, sm_recv, lsem = refs[o + 3 * n:]
        x, y, c = _place()
        me = 4 * x + 2 * y + c
        sibling = (x, y, 1 - c)
        chips = [(1 - x, y), (x, 1 - y), (1 - x, 1 - y)]
        all_chips = [(x, y)] + chips

        own_cps = []
        for w in range(n):
            cp = pltpu.make_async_copy(ps[w].at[me], owns[w], lsem.at[w])
            cp.start()
            own_cps.append(cp)
        sm_own = pltpu.make_async_copy(small_ref, small_out.at[me], lsem.at[n])
        sm_own.start()

        def small_copy(r):
            peer = ((x + (r >> 2)) % 2, (y + ((r >> 1) & 1)) % 2, (c + (r & 1)) % 2)
            return pltpu.make_async_remote_copy(
                src_ref=small_ref, dst_ref=small_out.at[me], send_sem=sm_send.at[r - 1], recv_sem=sm_recv.at[r - 1],
                device_id=peer, device_id_type=MESH_ID)

        sm_cps = [small_copy(r) for r in range(1, NDEV)]
        for cp in sm_cps:
            cp.start()

        def pair_copy(w, rel):
            cx, cy = all_chips[rel]
            return pltpu.make_async_remote_copy(
                src_ref=ps[w].at[4 * cx + 2 * cy + (1 - c)], dst_ref=sibs[w].at[rel],
                send_sem=s1_send.at[4 * w + rel], recv_sem=s1_recv.at[4 * w + rel],
                device_id=sibling, device_id_type=MESH_ID)

        def chip_copy(w, j):
            return pltpu.make_async_remote_copy(
                src_ref=qst[w].at[j], dst_ref=rels[w].at[j],
                send_sem=s2_send.at[3 * w + j], recv_sem=s2_recv.at[3 * w + j],
                device_id=(*chips[j], c), device_id_type=MESH_ID)

        pair_cps = [pair_copy(w, rel) for w in range(n) for rel in (1, 2, 3, 0)]
        for cp in pair_cps:
            cp.start()
        chip_cps = []
        for w in range(n):
            for j, (cx, cy) in enumerate(chips):
                pair_copy(w, 1 + j).wait_recv()
                la = pltpu.make_async_copy(ps[w].at[4 * cx + 2 * cy + c], pa[w], lsem.at[n + 1])
                lb = pltpu.make_async_copy(sibs[w].at[1 + j], pb[w], lsem.at[n + 2])
                la.start()
                lb.start()
                la.wait()
                lb.wait()
                qst[w][j] = (pa[w][...].astype(F32) + pb[w][...].astype(F32)).astype(BF16)
                cp = chip_copy(w, j)
                cp.start()
                chip_cps.append(cp)
        for w in range(n):
            pair_copy(w, 0).wait_recv()
            for j in range(3):
                chip_copy(w, j).wait_recv()
        for cp in sm_cps:
            cp.wait_recv()
        for cp in pair_cps + chip_cps + sm_cps:
            cp.wait_send()
        for cp in own_cps:
            cp.wait()
        sm_own.wait()

    any_spec = pl.BlockSpec(memory_space=pl.ANY)
    outs = pl.pallas_call(
        body, name="rs_grads",
        out_shape=[jax.ShapeDtypeStruct(b, BF16) for b in blks]
        + [jax.ShapeDtypeStruct((4, *b), BF16) for b in blks]
        + [jax.ShapeDtypeStruct((3, *b), BF16) for b in blks]
        + [jax.ShapeDtypeStruct((NDEV, *small.shape), F32)],
        in_specs=[any_spec] * (n + 1),
        out_specs=[any_spec] * (3 * n + 1),
        scratch_shapes=[pltpu.VMEM(b, BF16) for b in blks] + [pltpu.VMEM(b, BF16) for b in blks]
        + [pltpu.VMEM((3, *b), BF16) for b in blks]
        + [pltpu.SemaphoreType.DMA((4 * n,)), pltpu.SemaphoreType.DMA((4 * n,)),
           pltpu.SemaphoreType.DMA((3 * n,)), pltpu.SemaphoreType.DMA((3 * n,)),
           pltpu.SemaphoreType.DMA((NDEV - 1,)), pltpu.SemaphoreType.DMA((NDEV - 1,)),
           pltpu.SemaphoreType.DMA((n + 3,))],
        compiler_params=pltpu.CompilerParams(vmem_limit_bytes=40 * MIB),
    )(*parts, small)
    return outs[:n], outs[n:2 * n], outs[2 * n:3 * n], outs[3 * n]


def _adamw_math(g, w, m, v):
    m = ADAM_B1 * m + (1.0 - ADAM_B1) * g
    v = ADAM_B2 * v + (1.0 - ADAM_B2) * (g * g)
    m_hat = m / (1.0 - ADAM_B1 ** ADAM_STEP)
    v_hat = v / (1.0 - ADAM_B2 ** ADAM_STEP)
    delta = -ADAM_LR * (m_hat / (jnp.sqrt(v_hat) + ADAM_EPS) + ADAM_WD * w)
    return delta, m, v


def _adamw_shard(name, own, sib, rel, w, m, v, block, grid):
    nd = len(block)

    def body(own_ref, sib_ref, r0_ref, r1_ref, r2_ref, w_ref, m_ref, v_ref, g_out, d_out, m_out, v_out):
        g = own_ref[...].astype(F32) + sib_ref[...].astype(F32)
        g = g + r0_ref[...].astype(F32)
        g = g + r1_ref[...].astype(F32)
        g = g + r2_ref[...].astype(F32)
        delta, mm, vv = _adamw_math(g, w_ref[...], m_ref[...], v_ref[...])
        g_out[...] = g
        d_out[...] = delta
        m_out[...] = mm
        v_out[...] = vv

    def base(*g):
        return tuple(g) + (0,) * (nd - len(g))

    plain = pl.BlockSpec(block, base)

    def lead(k):
        return pl.BlockSpec((None, *block), lambda *g: (k, *base(*g)))

    shp = jax.ShapeDtypeStruct(w.shape, F32)
    return pl.pallas_call(
        body, name=name, grid=grid,
        in_specs=[plain, lead(0), lead(0), lead(1), lead(2), plain, plain, plain],
        out_specs=[plain] * 4, out_shape=[shp] * 4,
        compiler_params=_params(("arbitrary",) * len(grid), 40),
    )(own, sib, rel, rel, rel, w, m, v)


def _adamw_small(gathered, w, m, v):
    def body(g_ref, w_ref, m_ref, v_ref, g_out, d_out, m_out, v_out):
        g = g_ref[0]
        for d in range(1, NDEV):
            g = g + g_ref[d]
        delta, mm, vv = _adamw_math(g, w_ref[...], m_ref[...], v_ref[...])
        g_out[...] = g
        d_out[...] = delta
        m_out[...] = mm
        v_out[...] = vv

    shp = jax.ShapeDtypeStruct(w.shape, F32)
    return pl.pallas_call(body, name="adamw_small", out_shape=[shp] * 4)(gathered, w, m, v)


def _repack_gu(g_gu):
    def body(x_ref, o_ref):
        for i in range(2):
            for d in range(NDEV):
                o_ref[i, d // 4, :, pl.ds(FFB * (d % 4), FFB)] = x_ref[d, i]

    return pl.pallas_call(body, name="repack_gu", out_shape=jax.ShapeDtypeStruct((2, 2, D, FFC), BF16),
                          compiler_params=pltpu.CompilerParams(vmem_limit_bytes=40 * MIB))(g_gu)


def _fwd_in(h0, g_mix, w_g):
    tp = h0.shape[0]
    nt = tp // TM

    def body(h_ref, g_ref, w_hbm, z_ref, u_ref, w_vm, sems):
        _load_once(pl.program_id(0) == 0, [(w_hbm, w_vm)], sems)
        xv = h_ref[...]
        r = lax.rsqrt(jnp.mean(xv * xv, axis=-1, keepdims=True) + RMS_EPS)
        u = (xv * r * g_ref[...]).astype(BF16)
        u_ref[...] = u
        for d in range(NDEV):
            z_ref[:, INB * d:INB * (d + 1)] = _dot(u, w_vm[d])

    return pl.pallas_call(
        body, name="fwd_in", grid=(nt,),
        in_specs=[pl.BlockSpec((TM, D), lambda i: (i, 0)), pl.BlockSpec((1, D), lambda i: (0, 0)),
                  pl.BlockSpec(memory_space=pl.ANY)],
        out_specs=[pl.BlockSpec((TM, DIN), lambda i: (i, 0)), pl.BlockSpec((TM, D), lambda i: (i, 0))],
        out_shape=[jax.ShapeDtypeStruct((tp, DIN), F32), jax.ShapeDtypeStruct((tp, D), BF16)],
        scratch_shapes=[pltpu.VMEM((NDEV, D, INB), BF16), pltpu.SemaphoreType.DMA((1,))],
        compiler_params=_params(("arbitrary",), 48),
    )(h0, g_mix, w_g)


def _halo_specs(col, nt, width=D):
    r = TM // HALO
    last = nt * r - 1
    return [pl.BlockSpec((HALO, width), lambda i: (jnp.maximum(i * r - 1, 0), col)),
            pl.BlockSpec((TM, width), lambda i: (i, col)),
            pl.BlockSpec((HALO, width), lambda i: (jnp.minimum((i + 1) * r, last), col))]


def _fill_ext(ext_ref, left, cur, right, i, nt):
    ext_ref[pl.ds(0, HALO), :] = jnp.where(i > 0, left, 0.0)
    ext_ref[pl.ds(HALO, TM), :] = cur
    ext_ref[pl.ds(HALO + TM, HALO), :] = jnp.where(i < nt - 1, right, 0.0)


def _shift_copies(ext_ref, sh_ref, c0):
    for r in range(8):
        sh_ref[r] = ext_ref[pl.ds(r, TM + 24), pl.ds(c0, CW)]


def _pool_cnt(i, t_real, left, right, rows, row0):
    t = i * TM + row0 + lax.broadcasted_iota(jnp.int32, (rows, 1), 0)
    lo = jnp.maximum(t - left, 0)
    hi = jnp.minimum(t + right + 1, t_real)
    return jnp.maximum(hi - lo, 1).astype(F32)


def _seq_fwd(z, w_dw, b_dw, t_real):
    tp = z.shape[0]
    nt = tp // TM

    def body(av_l, av, av_r, ag_l, ag, ag_r, p_l, p, p_r, w_ref, b_ref, ac_ref, m_ref, a_ext, p_ext, sh):
        i = pl.program_id(0)
        _fill_ext(a_ext, av_l[...] * _sig(ag_l[...]), av[...] * _sig(ag[...]), av_r[...] * _sig(ag_r[...]), i, nt)
        _fill_ext(p_ext, p_l[...], p[...], p_r[...], i, nt)
        for c0 in range(0, D, CW):
            _shift_copies(a_ext, sh, c0)

            def rows(j, carry):
                base = pl.multiple_of(j * RB, RB)
                acc = jnp.broadcast_to(b_ref[:, pl.ds(c0, CW)], (RB, CW))
                for k in range(CONV_K):
                    q, r = divmod(k + 1, 8)
                    acc = acc + sh[r, pl.ds(pl.multiple_of(base + 8 * q, 8), RB), :] * w_ref[pl.ds(k, 1), pl.ds(c0, CW)]
                ac_ref[pl.ds(base, RB), pl.ds(c0, CW)] = acc
                return carry

            lax.fori_loop(0, TM // RB, rows, 0)
        for g, win in enumerate(POOL_WINDOWS):
            left = win // 2
            right = win - 1 - left
            cols = pl.ds(g * PG, PG)
            s = p_ext[pl.ds(HALO - left, TM), cols]
            for off in range(-left + 1, right + 1):
                s = s + p_ext[pl.ds(HALO + off, TM), cols]
            cnt = _pool_cnt(i, t_real, left, right, TM, 0)
            m_ref[:, cols] = (s / cnt - p_ext[pl.ds(HALO, TM), cols]).astype(BF16)

    return pl.pallas_call(
        body, name="seq_fwd", grid=(nt,),
        in_specs=_halo_specs(0, nt) + _halo_specs(1, nt) + _halo_specs(2, nt)
        + [pl.BlockSpec((32, D), lambda i: (0, 0)), pl.BlockSpec((1, D), lambda i: (0, 0))],
        out_specs=[pl.BlockSpec((TM, D), lambda i: (i, 0))] * 2,
        out_shape=[jax.ShapeDtypeStruct((tp, D), F32), jax.ShapeDtypeStruct((tp, D), BF16)],
        scratch_shapes=[pltpu.VMEM((TM + 2 * HALO, D), F32), pltpu.VMEM((TM + 2 * HALO, D), F32),
                        pltpu.VMEM((8, TM + 24, CW), F32)],
        compiler_params=_params(("arbitrary",), 40),
    )(z, z, z, z, z, z, z, z, z, w_dw, b_dw)


def _ln_stats(ac):
    mu = jnp.mean(ac, axis=-1, keepdims=True)
    xc = ac - mu
    rl = lax.rsqrt(jnp.mean(xc * xc, axis=-1, keepdims=True) + LN_EPS)
    return xc * rl, rl


def _pool_mix(m, wp_ref):
    return jnp.concatenate(
        [_dot(m[:, g * PG:(g + 1) * PG], wp_ref[:, g].reshape(PG, PG)) for g in range(4)], axis=1)


def _mix_fwd(ac, m, z, h0, b_gate, ln_g, ln_b, pool_scale, g_mixw, g_pool):
    tp = h0.shape[0]
    nt = tp // TMS

    def body(ac_ref, m_ref, zga, zgb, h_ref, bg_ref, lg_ref, lb_ref, ps_ref, wm_hbm, wp_hbm,
             h1_ref, s_ref, yc_ref, yp_ref, mg_ref, q_ref, wm, wp, sems):
        _load_once(pl.program_id(0) == 0, [(wm_hbm, wm), (wp_hbm, wp)], sems)
        n, _ = _ln_stats(ac_ref[...])
        l = n * lg_ref[...] + lb_ref[...]
        s = (l * _sig(l)).astype(BF16)
        s_ref[...] = s
        yc = _dot(s, wm[:, 0].reshape(D, D))
        q = (_pool_mix(m_ref[...], wp) * ps_ref[...]).astype(BF16)
        q_ref[...] = q
        yp = _dot(q, wm[:, 1].reshape(D, D))
        ga = _sig(zga[...] + bg_ref[:, :D])
        gb = _sig(zgb[...] + bg_ref[:, D:])
        merged = (ga * yc + gb * yp).astype(BF16)
        yc_ref[...] = yc
        yp_ref[...] = yp
        mg_ref[...] = merged
        h1_ref[...] = h_ref[...] + _dot(merged, wm[:, 2].reshape(D, D))

    def tile(col=0):
        return pl.BlockSpec((TMS, D), lambda i: (i, col))

    def vec(w):
        return pl.BlockSpec((1, w), lambda i: (0, 0))

    anys = pl.BlockSpec(memory_space=pl.ANY)
    f32o, b16o = jax.ShapeDtypeStruct((tp, D), F32), jax.ShapeDtypeStruct((tp, D), BF16)
    return pl.pallas_call(
        body, name="mix_fwd", grid=(nt,),
        in_specs=[tile(), tile(), tile(3), tile(4), tile(), vec(2 * D), vec(D), vec(D), vec(D), anys, anys],
        out_specs=[tile()] * 6,
        out_shape=[f32o, b16o, f32o, f32o, b16o, b16o],
        scratch_shapes=[pltpu.VMEM((NDEV, 3, D // NDEV, D), BF16), pltpu.VMEM((NDEV, 4, PG // NDEV, PG), BF16),
                        pltpu.SemaphoreType.DMA((2,))],
        compiler_params=_params(("arbitrary",), 48),
    )(ac, m, z, z, h0, b_gate, ln_g, ln_b, pool_scale, g_mixw, g_pool)


def _ffn_fwd(h1, tgt, g_ffn, g_final, w_gu, w_dn, t_real):
    tp = h1.shape[0]
    nt = tp // TM

    def body(h_ref, t_ref, gf_ref, gl_ref, wgu_hbm, wdn_hbm,
             fg_ref, fu_ref, v_ref, f_ref, dh2_ref, acc_ref, wgu, wdn, v_sc, h2_sc, sems):
        i, j = pl.program_id(0), pl.program_id(1)
        _load_once((i == 0) & (j == 0), [(wgu_hbm, wgu), (wdn_hbm, wdn)], sems)

        @pl.when((i == 0) & (j == 0))
        def _():
            acc_ref[...] = jnp.zeros_like(acc_ref)

        @pl.when(j == 0)
        def _():
            h = h_ref[...]
            r = lax.rsqrt(jnp.mean(h * h, axis=-1, keepdims=True) + RMS_EPS)
            v = (h * r * gf_ref[...]).astype(BF16)
            v_sc[...] = v
            v_ref[...] = v
            h2_sc[...] = h

        v = v_sc[...]
        fg = _dot(v, wgu[0, j])
        fu = _dot(v, wgu[1, j])
        fg_ref[...] = fg
        fu_ref[...] = fu
        f = ((fg * _sig(fg)) * fu).astype(BF16)
        f_ref[...] = f
        h2_sc[...] += _dot(f, wdn[j])

        @pl.when(j == 1)
        def _():
            h2 = h2_sc[...]
            r = lax.rsqrt(jnp.mean(h2 * h2, axis=-1, keepdims=True) + RMS_EPS)
            n2 = h2 * r
            t = i * TM + lax.broadcasted_iota(jnp.int32, (TM, 1), 0)
            valid = (t >= N_META) & (t < t_real)
            diff = jnp.where(valid, n2 * gl_ref[...] - t_ref[...], 0.0)
            dy = diff * (1.0 / D)
            acc_ref[0:1, :] += jnp.sum(diff * diff, axis=0, keepdims=True)
            acc_ref[1:2, :] += jnp.sum(dy * n2, axis=0, keepdims=True)
            dn = dy * gl_ref[...]
            dh2_ref[...] = r * (dn - n2 * jnp.mean(dn * n2, axis=-1, keepdims=True))

    def tile():
        return pl.BlockSpec((TM, D), lambda i, j: (i, 0))

    def chunk():
        return pl.BlockSpec((TM, FFC), lambda i, j: (i, j))

    def vec():
        return pl.BlockSpec((1, D), lambda i, j: (0, 0))

    anys = pl.BlockSpec(memory_space=pl.ANY)
    hid32, hid16 = jax.ShapeDtypeStruct((tp, DFF), F32), jax.ShapeDtypeStruct((tp, DFF), BF16)
    return pl.pallas_call(
        body, name="ffn_fwd", grid=(nt, 2),
        in_specs=[tile(), tile(), vec(), vec(), anys, anys],
        out_specs=[chunk(), chunk(), tile(), chunk(), tile(), pl.BlockSpec((8, D), lambda i, j: (0, 0))],
        out_shape=[hid32, hid32, jax.ShapeDtypeStruct((tp, D), BF16), hid16, jax.ShapeDtypeStruct((tp, D), F32),
                   jax.ShapeDtypeStruct((8, D), F32)],
        scratch_shapes=[pltpu.VMEM((2, 2, D, FFC), BF16), pltpu.VMEM((2, FFC, D), BF16),
                        pltpu.VMEM((TM, D), BF16), pltpu.VMEM((TM, D), F32), pltpu.SemaphoreType.DMA((2,))],
        compiler_params=_params(("arbitrary", "arbitrary"), 56),
    )(h1, tgt, g_ffn, g_final, w_gu, w_dn)


def _ffn_bwd(dh2, fg, fu, h1, g_ffn, w_gu, w_dn):
    tp = h1.shape[0]
    nt = tp // TM

    def body(dh2_ref, fg_ref, fu_ref, h_ref, gf_ref, wgu_hbm, wdn_hbm,
             dfg_ref, dfu_ref, dh1_ref, acc_ref, wgu, wdn, d_sc, dv_sc, sems):
        i, j = pl.program_id(0), pl.program_id(1)
        _load_once((i == 0) & (j == 0), [(wgu_hbm, wgu), (wdn_hbm, wdn)], sems)

        @pl.when((i == 0) & (j == 0))
        def _():
            acc_ref[...] = jnp.zeros_like(acc_ref)

        @pl.when(j == 0)
        def _():
            d_sc[...] = dh2_ref[...].astype(BF16)
            dv_sc[...] = jnp.zeros_like(dv_sc)

        df = _dot_nt(d_sc[...], wdn[j])
        fg = fg_ref[...]
        sg = _sig(fg)
        dfu = (df * (fg * sg)).astype(BF16)
        dfg = (df * fu_ref[...] * (sg * (1.0 + fg * (1.0 - sg)))).astype(BF16)
        dfg_ref[...] = dfg
        dfu_ref[...] = dfu
        dv_sc[...] += _dot_nt(dfg, wgu[0, j]) + _dot_nt(dfu, wgu[1, j])

        @pl.when(j == 1)
        def _():
            h = h_ref[...]
            r = lax.rsqrt(jnp.mean(h * h, axis=-1, keepdims=True) + RMS_EPS)
            n1 = h * r
            dv = dv_sc[...]
            acc_ref[0:1, :] += jnp.sum(dv * n1, axis=0, keepdims=True)
            dn = dv * gf_ref[...]
            dh1_ref[...] = dh2_ref[...] + r * (dn - n1 * jnp.mean(dn * n1, axis=-1, keepdims=True))

    def tile():
        return pl.BlockSpec((TM, D), lambda i, j: (i, 0))

    def chunk():
        return pl.BlockSpec((TM, FFC), lambda i, j: (i, j))

    anys = pl.BlockSpec(memory_space=pl.ANY)
    hid16 = jax.ShapeDtypeStruct((tp, DFF), BF16)
    return pl.pallas_call(
        body, name="ffn_bwd", grid=(nt, 2),
        in_specs=[tile(), chunk(), chunk(), tile(), pl.BlockSpec((1, D), lambda i, j: (0, 0)), anys, anys],
        out_specs=[chunk(), chunk(), tile(), pl.BlockSpec((8, D), lambda i, j: (0, 0))],
        out_shape=[hid16, hid16, jax.ShapeDtypeStruct((tp, D), F32), jax.ShapeDtypeStruct((8, D), F32)],
        scratch_shapes=[pltpu.VMEM((2, 2, D, FFC), BF16), pltpu.VMEM((2, FFC, D), BF16),
                        pltpu.VMEM((TM, D), BF16), pltpu.VMEM((TM, D), F32), pltpu.SemaphoreType.DMA((2,))],
        compiler_params=_params(("arbitrary", "arbitrary"), 56),
    )(dh2, fg, fu, h1, g_ffn, w_gu, w_dn)


def _mix_bwd(dh1, z, yc, yp, ac, m, b_gate, ln_g, ln_b, pool_scale, g_mixw, g_pool):
    tp = dh1.shape[0]
    nt = tp // TMS

    def body(dh1_ref, zga, zgb, yc_ref, yp_ref, ac_ref, m_ref, bg_ref, lg_ref, lb_ref, ps_ref, wm_hbm, wp_hbm,
             dac_ref, dm_ref, dzg_ref, dyc_ref, dyp_ref, dm2_ref, acc_ref, wm, wp, sems):
        first = pl.program_id(0) == 0
        _load_once(first, [(wm_hbm, wm), (wp_hbm, wp)], sems)

        @pl.when(first)
        def _():
            acc_ref[...] = jnp.zeros_like(acc_ref)

        dmerged = _dot_nt(dh1_ref[...].astype(BF16), wm[:, 2].reshape(D, D))
        ga = _sig(zga[...] + bg_ref[:, :D])
        gb = _sig(zgb[...] + bg_ref[:, D:])
        dyc = dmerged * ga
        dyp = dmerged * gb
        dza = (dmerged * yc_ref[...]) * (ga * (1.0 - ga))
        dzb = (dmerged * yp_ref[...]) * (gb * (1.0 - gb))
        dzg_ref[:, :D] = dza.astype(BF16)
        dzg_ref[:, D:] = dzb.astype(BF16)
        acc_ref[0:1, :D] += jnp.sum(dza, axis=0, keepdims=True)
        acc_ref[0:1, D:] += jnp.sum(dzb, axis=0, keepdims=True)
        dyc_b = dyc.astype(BF16)
        dyp_b = dyp.astype(BF16)
        dyc_ref[...] = dyc_b
        dyp_ref[...] = dyp_b
        ds = _dot_nt(dyc_b, wm[:, 0].reshape(D, D))
        n, rl = _ln_stats(ac_ref[...])
        l = n * lg_ref[...] + lb_ref[...]
        sg = _sig(l)
        dl = ds * (sg * (1.0 + l * (1.0 - sg)))
        acc_ref[1:2, :D] += jnp.sum(dl * n, axis=0, keepdims=True)
        acc_ref[1:2, D:] += jnp.sum(dl, axis=0, keepdims=True)
        dn = dl * lg_ref[...]
        dac_ref[...] = rl * (dn - jnp.mean(dn, axis=-1, keepdims=True) - n * jnp.mean(dn * n, axis=-1, keepdims=True))
        dq = _dot_nt(dyp_b, wm[:, 1].reshape(D, D))
        mv = m_ref[...]
        acc_ref[2:3, :D] += jnp.sum(dq * _pool_mix(mv, wp), axis=0, keepdims=True)
        dm2 = (dq * ps_ref[...]).astype(BF16)
        dm2_ref[...] = dm2
        dm_ref[...] = jnp.concatenate(
            [_dot_nt(dm2[:, g * PG:(g + 1) * PG], wp[:, g].reshape(PG, PG)) for g in range(4)], axis=1)

    def tile(col=0):
        return pl.BlockSpec((TMS, D), lambda i: (i, col))

    def vec(w):
        return pl.BlockSpec((1, w), lambda i: (0, 0))

    anys = pl.BlockSpec(memory_space=pl.ANY)
    f32o, b16o = jax.ShapeDtypeStruct((tp, D), F32), jax.ShapeDtypeStruct((tp, D), BF16)
    return pl.pallas_call(
        body, name="mix_bwd", grid=(nt,),
        in_specs=[tile(), tile(3), tile(4), tile(), tile(), tile(), tile(), vec(2 * D), vec(D), vec(D), vec(D), anys, anys],
        out_specs=[tile(), tile(), pl.BlockSpec((TMS, 2 * D), lambda i: (i, 0)), tile(), tile(), tile(),
                   pl.BlockSpec((8, 2 * D), lambda i: (0, 0))],
        out_shape=[f32o, f32o, jax.ShapeDtypeStruct((tp, 2 * D), BF16), b16o, b16o, b16o,
                   jax.ShapeDtypeStruct((8, 2 * D), F32)],
        scratch_shapes=[pltpu.VMEM((NDEV, 3, D // NDEV, D), BF16), pltpu.VMEM((NDEV, 4, PG // NDEV, PG), BF16),
                        pltpu.SemaphoreType.DMA((2,))],
        compiler_params=_params(("arbitrary",), 48),
    )(dh1, z, z, yc, yp, ac, m, b_gate, ln_g, ln_b, pool_scale, g_mixw, g_pool)


def _seq_bwd(dac, dm, dzg, z, w_dw, t_real):
    tp = z.shape[0]
    nt = tp // TM

    def body(dac_l, dac_c, dac_r, dm_l, dm_c, dm_r, av_l, av, av_r, ag_l, ag, ag_r, dzg_ref, w_ref,
             dz_ref, acc_ref, a_ext, d_ext, m_ext, sha, shd, da_sc, dw_sc):
        i = pl.program_id(0)

        @pl.when(i == 0)
        def _():
            dw_sc[...] = jnp.zeros_like(dw_sc)
            acc_ref[...] = jnp.zeros_like(acc_ref)

        sg = _sig(ag[...])
        _fill_ext(a_ext, av_l[...] * _sig(ag_l[...]), av[...] * sg, av_r[...] * _sig(ag_r[...]), i, nt)
        _fill_ext(d_ext, dac_l[...], dac_c[...], dac_r[...], i, nt)
        for g, win in enumerate(POOL_WINDOWS):
            left = win // 2
            right = win - 1 - left
            cols = pl.ds(g * PG, PG)
            m_ext[pl.ds(0, HALO), cols] = jnp.where(i > 0, dm_l[:, cols] / _pool_cnt(i, t_real, left, right, HALO, -HALO), 0.0)
            m_ext[pl.ds(HALO, TM), cols] = dm_c[:, cols] / _pool_cnt(i, t_real, left, right, TM, 0)
            m_ext[pl.ds(HALO + TM, HALO), cols] = jnp.where(
                i < nt - 1, dm_r[:, cols] / _pool_cnt(i, t_real, left, right, HALO, TM), 0.0)
        for c0 in range(0, D, CW):
            _shift_copies(a_ext, sha, c0)
            _shift_copies(d_ext, shd, c0)

            def rows(j, carry):
                base = pl.multiple_of(j * RB, RB)
                dcur = d_ext[pl.ds(pl.multiple_of(base + HALO, 8), RB), pl.ds(c0, CW)]
                acc = jnp.zeros((RB, CW), F32)
                for k in range(CONV_K):
                    q, r = divmod(CONV_K - k, 8)
                    acc = acc + shd[r, pl.ds(pl.multiple_of(base + 8 * q, 8), RB), :] * w_ref[pl.ds(k, 1), pl.ds(c0, CW)]
                    q, r = divmod(k + 1, 8)
                    prod = dcur * sha[r, pl.ds(pl.multiple_of(base + 8 * q, 8), RB), :]
                    dw_sc[k, :, pl.ds(c0, CW)] += jnp.sum(prod.reshape(RB // 8, 8, CW), axis=0)
                da_sc[pl.ds(base, RB), pl.ds(c0, CW)] = acc
                return carry

            lax.fori_loop(0, TM // RB, rows, 0)
        da = da_sc[...]
        dz_ref[:, 0:D] = (da * sg).astype(BF16)
        dz_ref[:, D:2 * D] = (da * av[...] * (sg * (1.0 - sg))).astype(BF16)
        for g, win in enumerate(POOL_WINDOWS):
            left = win // 2
            right = win - 1 - left
            cols = pl.ds(g * PG, PG)
            s = m_ext[pl.ds(HALO - right, TM), cols]
            for off in range(-right + 1, left + 1):
                s = s + m_ext[pl.ds(HALO + off, TM), cols]
            dz_ref[:, pl.ds(2 * D + g * PG, PG)] = (s - dm_c[:, cols]).astype(BF16)
        dz_ref[:, 3 * D:] = dzg_ref[...]

        @pl.when(i == nt - 1)
        def _():
            for k in range(CONV_K):
                acc_ref[k:k + 1, :] = jnp.sum(dw_sc[k], axis=0, keepdims=True)

        acc_ref[CONV_K:CONV_K + 1, :] += jnp.sum(dac_c[...], axis=0, keepdims=True)

    ext = pltpu.VMEM((TM + 2 * HALO, D), F32)
    shs = pltpu.VMEM((8, TM + 24, CW), F32)
    return pl.pallas_call(
        body, name="seq_bwd", grid=(nt,),
        in_specs=_halo_specs(0, nt) + _halo_specs(0, nt) + _halo_specs(0, nt) + _halo_specs(1, nt)
        + [pl.BlockSpec((TM, 2 * D), lambda i: (i, 0)), pl.BlockSpec((32, D), lambda i: (0, 0))],
        out_specs=[pl.BlockSpec((TM, DIN), lambda i: (i, 0)), pl.BlockSpec((32, D), lambda i: (0, 0))],
        out_shape=[jax.ShapeDtypeStruct((tp, DIN), BF16), jax.ShapeDtypeStruct((32, D), F32)],
        scratch_shapes=[ext, ext, ext, shs, shs, pltpu.VMEM((TM, D), F32), pltpu.VMEM((CONV_K, 8, D), F32)],
        compiler_params=_params(("arbitrary",), 48),
    )(dac, dac, dac, dm, dm, dm, z, z, z, z, z, z, dzg, w_dw)


def _in_bwd(dz, h0, dh1, g_mix, w_g):
    tp = h0.shape[0]
    nt = tp // TM

    def body(dz_ref, h_ref, dh1_ref, g_ref, w_hbm, dh0_ref, acc_ref, w_vm, sems):
        first = pl.program_id(0) == 0
        _load_once(first, [(w_hbm, w_vm)], sems)

        @pl.when(first)
        def _():
            acc_ref[...] = jnp.zeros_like(acc_ref)

        du = _dot_nt(dz_ref[:, 0:INB], w_vm[0])
        for d in range(1, NDEV):
            du = du + _dot_nt(dz_ref[:, INB * d:INB * (d + 1)], w_vm[d])
        h = h_ref[...]
        r = lax.rsqrt(jnp.mean(h * h, axis=-1, keepdims=True) + RMS_EPS)
        n0 = h * r
        acc_ref[0:1, :] += jnp.sum(du * n0, axis=0, keepdims=True)
        dn = du * g_ref[...]
        dh0_ref[...] = dh1_ref[...] + r * (dn - n0 * jnp.mean(dn * n0, axis=-1, keepdims=True))

    tile = pl.BlockSpec((TM, D), lambda i: (i, 0))
    return pl.pallas_call(
        body, name="in_bwd", grid=(nt,),
        in_specs=[pl.BlockSpec((TM, DIN), lambda i: (i, 0)), tile, tile, pl.BlockSpec((1, D), lambda i: (0, 0)),
                  pl.BlockSpec(memory_space=pl.ANY)],
        out_specs=[tile, pl.BlockSpec((8, D), lambda i: (0, 0))],
        out_shape=[jax.ShapeDtypeStruct((tp, D), F32), jax.ShapeDtypeStruct((8, D), F32)],
        scratch_shapes=[pltpu.VMEM((NDEV, D, INB), BF16), pltpu.SemaphoreType.DMA((1,))],
        compiler_params=_params(("arbitrary",), 48),
    )(dz, h0, dh1, g_mix, w_g)


def _wgrad_in(u, dz):
    tp = u.shape[0]
    nt = tp // TM
    half = DIN // 2

    def body(u_ref, dz_ref, o_ref, acc):
        t = pl.program_id(1)

        @pl.when(t == 0)
        def _():
            acc[...] = jnp.zeros_like(acc)

        acc[...] += _dot_tn(u_ref[...], dz_ref[...])

        @pl.when(t == nt - 1)
        def _():
            for d in range(4):
                o_ref[d] = acc[:, INB * d:INB * (d + 1)].astype(BF16)

    return pl.pallas_call(
        body, name="wgrad_in", grid=(2, nt),
        in_specs=[pl.BlockSpec((TM, D), lambda h, t: (t, 0)), pl.BlockSpec((TM, half), lambda h, t: (t, h))],
        out_specs=pl.BlockSpec((4, D, INB), lambda h, t: (h, 0, 0)),
        out_shape=jax.ShapeDtypeStruct((NDEV, D, INB), BF16),
        scratch_shapes=[pltpu.VMEM((D, half), F32)],
        compiler_params=_params(("arbitrary", "arbitrary"), 48),
    )(u, dz)


def _wgrad_mix(s, dyc, q, dyp, merged, dh1, m, dm2):
    tp = s.shape[0]
    nt = tp // TM
    rb = D // NDEV

    def body(s_ref, dyc_ref, q_ref, dyp_ref, mg_ref, dh1_ref, m_ref, dm2_ref, o_ref, op_ref, acc, accp):
        t = pl.program_id(0)

        @pl.when(t == 0)
        def _():
            acc[...] = jnp.zeros_like(acc)
            accp[...] = jnp.zeros_like(accp)

        acc[0] += _dot_tn(s_ref[...], dyc_ref[...])
        acc[1] += _dot_tn(q_ref[...], dyp_ref[...])
        acc[2] += _dot_tn(mg_ref[...], dh1_ref[...].astype(BF16))
        for g in range(4):
            accp[g] += _dot_tn(m_ref[:, g * PG:(g + 1) * PG], dm2_ref[:, g * PG:(g + 1) * PG])

        @pl.when(t == nt - 1)
        def _():
            for d in range(NDEV):
                for k in range(3):
                    o_ref[d, k] = acc[k, rb * d:rb * (d + 1), :].astype(BF16)
                for g in range(4):
                    op_ref[d, g] = accp[g, 32 * d:32 * (d + 1), :].astype(BF16)

    tile = pl.BlockSpec((TM, D), lambda t: (t, 0))
    return pl.pallas_call(
        body, name="wgrad_mix", grid=(nt,),
        in_specs=[tile] * 8,
        out_specs=[pl.BlockSpec((NDEV, 3, rb, D), lambda t: (0, 0, 0, 0)),
                   pl.BlockSpec((NDEV, 4, 32, PG), lambda t: (0, 0, 0, 0))],
        out_shape=[jax.ShapeDtypeStruct((NDEV, 3, rb, D), BF16), jax.ShapeDtypeStruct((NDEV, 4, 32, PG), BF16)],
        scratch_shapes=[pltpu.VMEM((3, D, D), F32), pltpu.VMEM((4, PG, PG), F32)],
        compiler_params=_params(("arbitrary",), 56),
    )(s, dyc, q, dyp, merged, dh1, m, dm2)


def _wgrad_gu(v, dfg, dfu):
    tp = v.shape[0]
    nt = tp // TM

    def body(v_ref, dg_ref, du_ref, o_ref, acc):
        k, t = pl.program_id(0), pl.program_id(2)

        @pl.when(t == 0)
        def _():
            acc[...] = jnp.zeros_like(acc)

        @pl.when(k == 0)
        def _():
            acc[...] += _dot_tn(v_ref[...], dg_ref[...])

        @pl.when(k == 1)
        def _():
            acc[...] += _dot_tn(v_ref[...], du_ref[...])

        @pl.when(t == nt - 1)
        def _():
            for d in range(4):
                o_ref[d] = acc[:, pl.ds(FFB * d, FFB)].astype(BF16)

    return pl.pallas_call(
        body, name="wgrad_gu", grid=(2, 2, nt),
        in_specs=[pl.BlockSpec((TM, D), lambda k, h, t: (t, 0)),
                  pl.BlockSpec((TM, FFC), lambda k, h, t: (t * (1 - k), h * (1 - k))),
                  pl.BlockSpec((TM, FFC), lambda k, h, t: (t * k, h * k))],
        out_specs=pl.BlockSpec((4, None, D, FFB), lambda k, h, t: (h, k, 0, 0)),
        out_shape=jax.ShapeDtypeStruct((NDEV, 2, D, FFB), BF16),
        scratch_shapes=[pltpu.VMEM((D, FFC), F32)],
        compiler_params=_params(("arbitrary",) * 3, 40),
    )(v, dfg, dfu)


def _wgrad_down(f, dh2):
    tp = f.shape[0]
    nt = tp // TM

    def body(f_ref, d_ref, o_ref, acc):
        t = pl.program_id(1)

        @pl.when(t == 0)
        def _():
            acc[...] = jnp.zeros_like(acc)

        acc[...] += _dot_tn(f_ref[...], d_ref[...].astype(BF16))

        @pl.when(t == nt - 1)
        def _():
            for d in range(4):
                o_ref[d] = acc[FFB * d:FFB * (d + 1), :].astype(BF16)

    return pl.pallas_call(
        body, name="wgrad_down", grid=(2, nt),
        in_specs=[pl.BlockSpec((TM, FFC), lambda h, t: (t, h)), pl.BlockSpec((TM, D), lambda h, t: (t, 0))],
        out_specs=pl.BlockSpec((4, FFB, D), lambda h, t: (h, 0, 0)),
        out_shape=jax.ShapeDtypeStruct((NDEV, FFB, D), BF16),
        scratch_shapes=[pltpu.VMEM((FFC, D), F32)],
        compiler_params=_params(("arbitrary", "arbitrary"), 40),
    )(f, dh2)


def kernel(x, meta_tokens, g_mix, w_in, b_gate, w_dw, b_dw, ln_g, ln_b, w_conv_out, w_pool, pool_scale, w_pool_out, w_o, g_ffn, w_ffn_gate, w_ffn_up, w_ffn_down, g_final, loss_target, m_meta_tokens, m_g_mix, m_w_in, m_b_gate, m_w_dw, m_b_dw, m_ln_g, m_ln_b, m_w_conv_out, m_w_pool, m_pool_scale, m_w_pool_out, m_w_o, m_g_ffn, m_w_ffn_gate, m_w_ffn_up, m_w_ffn_down, m_g_final, v_meta_tokens, v_g_mix, v_w_in, v_b_gate, v_w_dw, v_b_dw, v_ln_g, v_ln_b, v_w_conv_out, v_w_pool, v_pool_scale, v_w_pool_out, v_w_o, v_g_ffn, v_w_ffn_gate, v_w_ffn_up, v_w_ffn_down, v_g_final):
    seq = x.shape[1]
    t_real = N_META + seq
    tp = -(-t_real // TM) * TM
    zrow = jnp.zeros((1, D // NDEV), F32)

    def small_pack(meta, dw):
        return jnp.concatenate([meta, dw[0], zrow], axis=0)

    def mix_pack(a, b, c):
        return jnp.concatenate([a, b, c], axis=0)

    def gu_pack(a, b):
        return jnp.concatenate([a, b], axis=0)

    shards = [w_in[0], small_pack(meta_tokens, w_dw), mix_pack(w_conv_out, w_pool_out, w_o), w_pool[0],
              gu_pack(w_ffn_gate, w_ffn_up), w_ffn_down[0]]
    g_in, g_small, g_mixw, g_pool, g_gu, g_down = _all_gather(shards, [BF16, F32, BF16, BF16, BF16, BF16])
    small_full = g_small.transpose(1, 0, 2).reshape(48, D)
    wdw_full = small_full[N_META:]
    h0 = jnp.concatenate([small_full[:N_META], x[0], jnp.zeros((tp - t_real, D), F32)], axis=0)
    tgt = jnp.concatenate([jnp.zeros((N_META, D), F32), loss_target[0], jnp.zeros((tp - t_real, D), F32)], axis=0)
    w_gu = _repack_gu(g_gu)
    w_dn = g_down.reshape(2, FFC, D)

    z, u = _fwd_in(h0, g_mix, g_in)
    ac, m = _seq_fwd(z, wdw_full, b_dw, t_real)
    h1, s, yc, yp, merged, q = _mix_fwd(ac, m, z, h0, b_gate, ln_g, ln_b, pool_scale, g_mixw, g_pool)
    fg, fu, v, f, dh2, head_acc = _ffn_fwd(h1, tgt, g_ffn, g_final.reshape(1, D), w_gu, w_dn, t_real)
    loss = lax.psum((0.5 / D) * jnp.sum(head_acc[0]), AXES)

    dfg, dfu, dh1, ffn_acc = _ffn_bwd(dh2, fg, fu, h1, g_ffn, w_gu, w_dn)
    dac, dm, dzg, dyc, dyp, dm2, mix_acc = _mix_bwd(dh1, z, yc, yp, ac, m, b_gate, ln_g, ln_b, pool_scale, g_mixw, g_pool)
    dz, seq_acc = _seq_bwd(dac, dm, dzg, z, wdw_full, t_real)
    dh0, in_acc = _in_bwd(dz, h0, dh1, g_mix, g_in)
    grad_x = dh0[N_META:t_real][None]

    p_in = _wgrad_in(u, dz)
    p_mix, p_pool = _wgrad_mix(s, dyc, q, dyp, merged, dh1, m, dm2)
    p_gu = _wgrad_gu(v, dfg, dfu)
    p_down = _wgrad_down(f, dh2)
    small_g = jnp.concatenate([dh0[:N_META], seq_acc[:CONV_K], jnp.zeros((1, D), F32)], axis=0)
    p_small = small_g.reshape(48, NDEV, D // NDEV).transpose(1, 0, 2).astype(BF16)
    rep_g = jnp.concatenate([
        in_acc[0:1], mix_acc[0:1, :D], mix_acc[0:1, D:], seq_acc[CONV_K:CONV_K + 1], mix_acc[1:2, :D], mix_acc[1:2, D:],
        mix_acc[2:3, :D], ffn_acc[0:1], head_acc[1:2], jnp.zeros((7, D), F32)], axis=0)

    owns, sibs, rels, rep_all = _reduce_scatter([p_in, p_small, p_mix, p_pool, p_gu, p_down], rep_g)

    def upd(k, name, w, m_, v_, block, grid):
        return _adamw_shard(name, owns[k], sibs[k], rels[k], w, m_, v_, block, grid)

    r_in = upd(0, "adamw_in", w_in[0], m_w_in[0], v_w_in[0], (256, INB), (4,))
    r_small = upd(1, "adamw_meta_dw", small_pack(meta_tokens, w_dw), small_pack(m_meta_tokens, m_w_dw),
                  small_pack(v_meta_tokens, v_w_dw), (48, D // NDEV), (1,))
    r_mix = upd(2, "adamw_mix", mix_pack(w_conv_out, w_pool_out, w_o), mix_pack(m_w_conv_out, m_w_pool_out, m_w_o),
                mix_pack(v_w_conv_out, v_w_pool_out, v_w_o), (1, D // NDEV, D), (3,))
    r_pool = upd(3, "adamw_pool", w_pool[0], m_w_pool[0], v_w_pool[0], (4, 32, PG), (1,))
    r_gu = upd(4, "adamw_gu", gu_pack(w_ffn_gate, w_ffn_up), gu_pack(m_w_ffn_gate, m_w_ffn_up),
               gu_pack(v_w_ffn_gate, v_w_ffn_up), (1, 256, FFB), (2, 4))
    r_down = upd(5, "adamw_down", w_ffn_down[0], m_w_ffn_down[0], v_w_ffn_down[0], (FFB // 2, D), (2,))

    def rep_pack(a_mix, a_bg, a_bdw, a_lg, a_lb, a_ps, a_gf, a_gl):
        return jnp.concatenate([a_mix, a_bg[:, :D], a_bg[:, D:], a_bdw, a_lg, a_lb, a_ps, a_gf, a_gl.reshape(1, D),
                                jnp.zeros((7, D), F32)], axis=0)

    r_rep = _adamw_small(
        rep_all,
        rep_pack(g_mix, b_gate, b_dw, ln_g, ln_b, pool_scale, g_ffn, g_final),
        rep_pack(m_g_mix, m_b_gate, m_b_dw, m_ln_g, m_ln_b, m_pool_scale, m_g_ffn, m_g_final),
        rep_pack(v_g_mix, v_b_gate, v_b_dw, v_ln_g, v_ln_b, v_pool_scale, v_g_ffn, v_g_final))

    def per_weight(kind):
        sm, rp, mx, gu = r_small[kind], r_rep[kind], r_mix[kind], r_gu[kind]
        return [
            sm[:N_META],
            rp[0:1],
            r_in[kind][None],
            jnp.concatenate([rp[1:2], rp[2:3]], axis=1),
            sm[N_META:N_META + CONV_K][None],
            rp[3:4], rp[4:5], rp[5:6],
            mx[0:1],
            r_pool[kind][None],
            rp[6:7],
            mx[1:2], mx[2:3],
            rp[7:8],
            gu[0:1], gu[1:2],
            r_down[kind][None],
            rp[8],
        ]

    return (loss, grad_x, *per_weight(0), *per_weight(1), *per_weight(2), *per_weight(3))
```

```python
import math

import jax
import jax.numpy as jnp
from jax import lax
from jax.experimental import pallas as pl
from jax.experimental.pallas import tpu as pltpu

F32, BF16 = jnp.float32, jnp.bfloat16
MESH_ID = pl.DeviceIdType.MESH
NDEV = 8

D = 1024
N_META = 16
CONV_K = 31
HALO = 16
POOL_WINDOWS = (2, 4, 8, 16)
PG = 256
DIN = 5 * D
DFF = 2816
FFB = DFF // NDEV
FFC = DFF // 2
INB = DIN // NDEV
RMS_EPS = 1e-6
LN_EPS = 1e-5
ADAM_LR, ADAM_B1, ADAM_B2, ADAM_EPS, ADAM_WD, ADAM_STEP = 0.001, 0.9, 0.999, 1e-08, 0.01, 10

TM = 384
TMS = 192
RB, CW = 32, 256
MIB = 2 ** 20


def _sig(x):
    return 1.0 / (1.0 + jnp.exp(-x))


def _dot(a, b):
    return jnp.dot(a, b, preferred_element_type=F32)


def _dot_nt(a, b):
    return lax.dot_general(a, b, (((1,), (1,)), ((), ())), preferred_element_type=F32)


def _dot_tn(a, b):
    return lax.dot_general(a, b, (((0,), (0,)), ((), ())), preferred_element_type=F32)


def _params(sem, vmem_mib):
    return pltpu.CompilerParams(dimension_semantics=sem, vmem_limit_bytes=vmem_mib * MIB)


def _load_once(first, pairs, sems):
    @pl.when(first)
    def _():
        cps = [pltpu.make_async_copy(s, d, sems.at[k]) for k, (s, d) in enumerate(pairs)]
        for cp in cps:
            cp.start()
        for cp in cps:
            cp.wait()


def _place():
    x, y, c = lax.axis_index("x"), lax.axis_index("y"), lax.axis_index("c")
    return x, y, c


def _all_gather(groups, dtypes):
    n = len(groups)
    arrays = [a for _, parts in groups for a, _, _ in parts]

    def body(*refs):
        na = len(arrays)
        ins, outs, stages = refs[:na], refs[na:na + n], refs[na + n:na + 2 * n]
        send_sems, recv_sems, local_sems = refs[na + 2 * n:]
        x, y, c = _place()
        me, sibling = (x, y, c), (x, y, 1 - c)
        chips = [(1 - x, y), (x, 1 - y), (1 - x, 1 - y)]

        def idx(p):
            return 4 * p[0] + 2 * p[1] + p[2]

        def copy(w, k, block, to, src=None):
            dst = outs[w].at[idx(block)]
            return pltpu.make_async_remote_copy(
                src_ref=dst if src is None else src, dst_ref=dst,
                send_sem=send_sems.at[7 * w + k], recv_sem=recv_sems.at[7 * w + k],
                device_id=to, device_id_type=MESH_ID)

        mine, first = [], []
        a = 0
        for w in range(n):
            shape, parts = groups[w]
            if sum(arr.size for arr, _, _ in parts) < math.prod(shape):
                stages[w][...] = jnp.zeros(shape, dtypes[w])
            for _, dst, src in parts:
                stages[w][dst] = ins[a][src].astype(dtypes[w])
                a += 1
            cp = pltpu.make_async_copy(stages[w], outs[w].at[idx(me)], local_sems.at[w])
            cp.start()
            mine.append(cp)
            cps = [copy(w, 0, me, sibling, src=stages[w])]
            cps += [copy(w, 1 + j, me, (*chip, c), src=stages[w]) for j, chip in enumerate(chips)]
            for cp in cps:
                cp.start()
            first += cps
        passed = []
        for w in range(n):
            for j, chip in enumerate(chips):
                copy(w, 1 + j, (*chip, c), me).wait_recv()
                cp = copy(w, 4 + j, (*chip, c), sibling)
                cp.start()
                passed.append(cp)
        for w in range(n):
            copy(w, 0, sibling, me).wait_recv()
            for j, chip in enumerate(chips):
                copy(w, 4 + j, (*chip, 1 - c), me).wait_recv()
        for cp in first + passed:
            cp.wait_send()
        for cp in mine:
            cp.wait()

    return pl.pallas_call(
        body, name="ag_weights",
        out_shape=[jax.ShapeDtypeStruct((NDEV, *s), dt) for (s, _), dt in zip(groups, dtypes)],
        in_specs=[pl.BlockSpec(memory_space=pltpu.VMEM)] * len(arrays),
        out_specs=[pl.BlockSpec(memory_space=pl.ANY)] * n,
        scratch_shapes=[pltpu.VMEM(s, dt) for (s, _), dt in zip(groups, dtypes)]
        + [pltpu.SemaphoreType.DMA((7 * n,)), pltpu.SemaphoreType.DMA((7 * n,)), pltpu.SemaphoreType.DMA((n,))],
        compiler_params=pltpu.CompilerParams(vmem_limit_bytes=40 * MIB),
    )(*arrays)


def _reduce_scatter(parts, small):
    n = len(parts)
    blks = [p.shape[1:] for p in parts]

    def body(*refs):
        ps, small_ref = refs[:n], refs[n]
        o = n + 1
        owns, sibs, rels, small_out = refs[o:o + n], refs[o + n:o + 2 * n], refs[o + 2 * n:o + 3 * n], refs[o + 3 * n]
        o += 3 * n + 1
        pa, pb, qst = refs[o:o + n], refs[o + n:o + 2 * n], refs[o + 2 * n:o + 3 * n]
        s1_send, s1_recv, s2_send, s2_recv, sm_send, sm_recv, lsem = refs[o + 3 * n:]
        x, y, c = _place()
        me = 4 * x + 2 * y + c
        sibling = (x, y, 1 - c)
        chips = [(1 - x, y), (x, 1 - y), (1 - x, 1 - y)]
        all_chips = [(x, y)] + chips

        own_cps = []
        for w in range(n):
            cp = pltpu.make_async_copy(ps[w].at[me], owns[w], lsem.at[w])
            cp.start()
            own_cps.append(cp)
        sm_own = pltpu.make_async_copy(small_ref, small_out.at[me], lsem.at[n])
        sm_own.start()

        def small_copy(r):
            peer = ((x + (r >> 2)) % 2, (y + ((r >> 1) & 1)) % 2, (c + (r & 1)) % 2)
            return pltpu.make_async_remote_copy(
                src_ref=small_ref, dst_ref=small_out.at[me], send_sem=sm_send.at[r - 1], recv_sem=sm_recv.at[r - 1],
                device_id=peer, device_id_type=MESH_ID)

        sm_cps = [small_copy(r) for r in range(1, NDEV)]
        for cp in sm_cps:
            cp.start()

        def pair_copy(w, rel):
            cx, cy = all_chips[rel]
            return pltpu.make_async_remote_copy(
                src_ref=ps[w].at[4 * cx + 2 * cy + (1 - c)], dst_ref=sibs[w].at[rel],
                send_sem=s1_send.at[4 * w + rel], recv_sem=s1_recv.at[4 * w + rel],
                device_id=sibling, device_id_type=MESH_ID)

        def chip_copy(w, j):
            return pltpu.make_async_remote_copy(
                src_ref=qst[w].at[j], dst_ref=rels[w].at[j],
                send_sem=s2_send.at[3 * w + j], recv_sem=s2_recv.at[3 * w + j],
                device_id=(*chips[j], c), device_id_type=MESH_ID)

        pair_cps = [pair_copy(w, rel) for w in range(n) for rel in (1, 2, 3, 0)]
        for cp in pair_cps:
            cp.start()
        chip_cps = []
        for w in range(n):
            for j, (cx, cy) in enumerate(chips):
                pair_copy(w, 1 + j).wait_recv()
                la = pltpu.make_async_copy(ps[w].at[4 * cx + 2 * cy + c], pa[w], lsem.at[n + 1])
                lb = pltpu.make_async_copy(sibs[w].at[1 + j], pb[w], lsem.at[n + 2])
                la.start()
                lb.start()
                la.wait()
                lb.wait()
                qst[w][j] = (pa[w][...].astype(F32) + pb[w][...].astype(F32)).astype(BF16)
                cp = chip_copy(w, j)
                cp.start()
                chip_cps.append(cp)
        for w in range(n):
            pair_copy(w, 0).wait_recv()
            for j in range(3):
                chip_copy(w, j).wait_recv()
        for cp in sm_cps:
            cp.wait_recv()
        for cp in pair_cps + chip_cps + sm_cps:
            cp.wait_send()
        for cp in own_cps:
            cp.wait()
        sm_own.wait()

    any_spec = pl.BlockSpec(memory_space=pl.ANY)
    outs = pl.pallas_call(
        body, name="rs_grads",
        out_shape=[jax.ShapeDtypeStruct(b, BF16) for b in blks]
        + [jax.ShapeDtypeStruct((4, *b), BF16) for b in blks]
        + [jax.ShapeDtypeStruct((3, *b), BF16) for b in blks]
        + [jax.ShapeDtypeStruct((NDEV, *small.shape), F32)],
        in_specs=[any_spec] * (n + 1),
        out_specs=[any_spec] * (3 * n + 1),
        scratch_shapes=[pltpu.VMEM(b, BF16) for b in blks] + [pltpu.VMEM(b, BF16) for b in blks]
        + [pltpu.VMEM((3, *b), BF16) for b in blks]
        + [pltpu.SemaphoreType.DMA((4 * n,)), pltpu.SemaphoreType.DMA((4 * n,)),
           pltpu.SemaphoreType.DMA((3 * n,)), pltpu.SemaphoreType.DMA((3 * n,)),
           pltpu.SemaphoreType.DMA((NDEV - 1,)), pltpu.SemaphoreType.DMA((NDEV - 1,)),
           pltpu.SemaphoreType.DMA((n + 3,))],
        compiler_params=pltpu.CompilerParams(vmem_limit_bytes=40 * MIB),
    )(*parts, small)
    return outs[:n], outs[n:2 * n], outs[2 * n:3 * n], outs[3 * n]


def _adamw_math(g, w, m, v):
    m = ADAM_B1 * m + (1.0 - ADAM_B1) * g
    v = ADAM_B2 * v + (1.0 - ADAM_B2) * (g * g)
    m_hat = m / (1.0 - ADAM_B1 ** ADAM_STEP)
    v_hat = v / (1.0 - ADAM_B2 ** ADAM_STEP)
    delta = -ADAM_LR * (m_hat / (jnp.sqrt(v_hat) + ADAM_EPS) + ADAM_WD * w)
    return delta, m, v


def _adamw_multi(name, own, sib, rel, ws, ms, vs, row_grid):
    k_n, r_n, c_n = own.shape
    rbk = r_n // row_grid

    def body(*refs):
        own_ref, sib_ref, r0_ref, r1_ref, r2_ref = refs[:5]
        w_refs, m_refs, v_refs = refs[5:5 + k_n], refs[5 + k_n:5 + 2 * k_n], refs[5 + 2 * k_n:5 + 3 * k_n]
        outs = refs[5 + 3 * k_n:]
        for k in range(k_n):
            g = own_ref[k].astype(F32) + sib_ref[k].astype(F32)
            g = g + r0_ref[k].astype(F32)
            g = g + r1_ref[k].astype(F32)
            g = g + r2_ref[k].astype(F32)
            delta, mm, vv = _adamw_math(g, w_refs[k][0], m_refs[k][0], v_refs[k][0])
            outs[4 * k][0] = g
            outs[4 * k + 1][0] = delta
            outs[4 * k + 2][0] = mm
            outs[4 * k + 3][0] = vv

    def lead(j):
        return pl.BlockSpec((None, k_n, rbk, c_n), lambda g: (j, 0, g, 0))

    wspec = pl.BlockSpec((1, rbk, c_n), lambda g: (0, g, 0))
    shp = jax.ShapeDtypeStruct((1, r_n, c_n), F32)
    res = pl.pallas_call(
        body, name=name, grid=(row_grid,),
        in_specs=[pl.BlockSpec((k_n, rbk, c_n), lambda g: (0, g, 0)), lead(0), lead(0), lead(1), lead(2)] + [wspec] * (3 * k_n),
        out_specs=[wspec] * (4 * k_n), out_shape=[shp] * (4 * k_n),
        compiler_params=_params(("arbitrary",), 40),
    )(own, sib, rel, rel, rel, *ws, *ms, *vs)
    return [tuple(res[4 * k:4 * k + 4]) for k in range(k_n)]


def _adamw_meta_dw(own, sib, rel, meta, dw):
    def body(own_ref, sib_ref, rel_ref, wm, mm, vm, wd, md, vd, *outs):
        def gsum(rows):
            g = own_ref[rows, :].astype(F32) + sib_ref[0, rows, :].astype(F32)
            for j in range(3):
                g = g + rel_ref[j, rows, :].astype(F32)
            return g

        g = gsum(pl.ds(0, N_META))
        delta, m2, v2 = _adamw_math(g, wm[...], mm[...], vm[...])
        for o, val in zip(outs[:4], (g, delta, m2, v2)):
            o[...] = val
        g = gsum(pl.ds(N_META, CONV_K))
        delta, m2, v2 = _adamw_math(g, wd[0], md[0], vd[0])
        for o, val in zip(outs[4:], (g, delta, m2, v2)):
            o[0] = val

    s_meta = jax.ShapeDtypeStruct(meta[0].shape, F32)
    s_dw = jax.ShapeDtypeStruct(dw[0].shape, F32)
    res = pl.pallas_call(body, name="adamw_meta_dw", out_shape=[s_meta] * 4 + [s_dw] * 4)(own, sib, rel, *meta, *dw)
    return tuple(res[:4]), tuple(res[4:])


REP_ROWS = 16


def _adamw_rep(gathered, ws, ms, vs):
    rows = [(0, 1), (1, 2), (3, 1), (4, 1), (5, 1), (6, 1), (7, 1), (8, 1)]

    def body(g_ref, *refs):
        w_refs, m_refs, v_refs = refs[:8], refs[8:16], refs[16:24]
        loss_ref, outs, acc = refs[24], refs[25:57], refs[57]
        g = g_ref[0]
        for d in range(1, NDEV):
            g = g + g_ref[d]
        acc[...] = g
        loss_ref[...] = (0.5 / D) * jnp.sum(acc[pl.ds(9, 1), :], axis=1, keepdims=True)
        for p, (r0, nr) in enumerate(rows):
            for h in range(nr):
                cols = pl.ds(h * D, D)
                gp = acc[pl.ds(r0 + h, 1), :]
                delta, mm, vv = _adamw_math(gp, w_refs[p][:, cols], m_refs[p][:, cols], v_refs[p][:, cols])
                for o, val in zip(outs[4 * p:4 * p + 4], (gp, delta, mm, vv)):
                    o[:, cols] = val

    shapes = [jax.ShapeDtypeStruct(w.shape, F32) for w in ws]
    res = pl.pallas_call(
        body, name="adamw_rep",
        out_shape=[jax.ShapeDtypeStruct((1, 1), F32)] + [s for s in shapes for _ in range(4)],
        scratch_shapes=[pltpu.VMEM((REP_ROWS, D), F32)],
    )(gathered, *ws, *ms, *vs)
    return res[0], [tuple(res[1 + 4 * p:5 + 4 * p]) for p in range(8)]


def _repack_gu(g_gu):
    def body(x_ref, o_ref):
        for i in range(2):
            for d in range(NDEV):
                o_ref[i, d // 4, :, pl.ds(FFB * (d % 4), FFB)] = x_ref[d, i]

    return pl.pallas_call(body, name="repack_gu", out_shape=jax.ShapeDtypeStruct((2, 2, D, FFC), BF16),
                          compiler_params=pltpu.CompilerParams(vmem_limit_bytes=40 * MIB))(g_gu)


def _fwd_in(x2, tail, g_mix, w_g, tp):
    nt = tp // TM
    nx_last = TM - tail.shape[0]

    def body(x_ref, tail_ref, g_ref, w_hbm, h_ref, z_ref, u_ref, w_vm, sems):
        i = pl.program_id(0)
        _load_once(i == 0, [(w_hbm, w_vm)], sems)

        @pl.when(i < nt - 1)
        def _():
            h_ref[...] = x_ref[...]

        @pl.when(i == nt - 1)
        def _():
            h_ref[pl.ds(0, nx_last), :] = x_ref[pl.ds(0, nx_last), :]
            h_ref[pl.ds(nx_last, TM - nx_last), :] = tail_ref[...]

        xv = h_ref[...]
        r = lax.rsqrt(jnp.mean(xv * xv, axis=-1, keepdims=True) + RMS_EPS)
        u = (xv * r * g_ref[...]).astype(BF16)
        u_ref[...] = u
        for d in range(NDEV):
            z_ref[:, INB * d:INB * (d + 1)] = _dot(u, w_vm[d])

    tile = pl.BlockSpec((TM, D), lambda i: (i, 0))
    return pl.pallas_call(
        body, name="fwd_in", grid=(nt,),
        in_specs=[tile, pl.BlockSpec(tail.shape, lambda i: (0, 0)), pl.BlockSpec((1, D), lambda i: (0, 0)),
                  pl.BlockSpec(memory_space=pl.ANY)],
        out_specs=[tile, pl.BlockSpec((TM, DIN), lambda i: (i, 0)), tile],
        out_shape=[jax.ShapeDtypeStruct((tp, D), F32), jax.ShapeDtypeStruct((tp, DIN), F32),
                   jax.ShapeDtypeStruct((tp, D), BF16)],
        scratch_shapes=[pltpu.VMEM((NDEV, D, INB), BF16), pltpu.SemaphoreType.DMA((1,))],
        compiler_params=_params(("arbitrary",), 52),
    )(x2, tail, g_mix, w_g)


def _halo_specs(col, nt, width=D):
    r = TM // HALO
    nb = nt * r
    return [pl.BlockSpec((HALO, width), lambda i: ((i * r + nb - 1) % nb, col)),
            pl.BlockSpec((TM, width), lambda i: (i, col)),
            pl.BlockSpec((HALO, width), lambda i: (((i + 1) * r) % nb, col))]


def _fill_ext(ext_ref, left, cur, right):
    ext_ref[pl.ds(0, HALO), :] = left
    ext_ref[pl.ds(HALO, TM), :] = cur
    ext_ref[pl.ds(HALO + TM, HALO), :] = right


def _shift_copies(ext_ref, sh_ref, c0):
    for r in range(8):
        sh_ref[r] = ext_ref[pl.ds(r, TM + 24), pl.ds(c0, CW)]


def _tap_rows(w_ref, w8):
    for k in range(CONV_K):
        w8[k] = jnp.broadcast_to(w_ref[pl.ds(k, 1), :], (8, D))


def _pool_cnt(i, seq, tp, left, right, rows, row0):
    b = i * TM + row0 + lax.broadcasted_iota(jnp.int32, (rows, 1), 0)
    b = jnp.where(b < 0, b + tp, b)
    b = jnp.where(b >= tp, b - tp, b)
    t = jnp.where(b < seq, b + N_META, b - (tp - N_META))
    lo = jnp.maximum(t - left, 0)
    hi = jnp.minimum(t + right + 1, seq + N_META)
    return jnp.maximum(hi - lo, 1).astype(F32)


def _seq_fwd(z, w_dw, b_dw, seq):
    tp = z.shape[0]
    nt = tp // TM

    def body(av_l, av, av_r, ag_l, ag, ag_r, p_l, p, p_r, w_ref, b_ref, ac_ref, m_ref, a_ext, p_ext, sh, w8):
        i = pl.program_id(0)

        @pl.when(i == 0)
        def _():
            _tap_rows(w_ref, w8)

        _fill_ext(a_ext, av_l[...] * _sig(ag_l[...]), av[...] * _sig(ag[...]), av_r[...] * _sig(ag_r[...]))
        _fill_ext(p_ext, p_l[...], p[...], p_r[...])
        for c0 in range(0, D, CW):
            _shift_copies(a_ext, sh, c0)

            def rows(j, carry):
                base = pl.multiple_of(j * RB, RB)
                acc = jnp.broadcast_to(b_ref[:, pl.ds(c0, CW)], (RB // 8, 8, CW))
                for k in range(CONV_K):
                    q, r = divmod(k + 1, 8)
                    slab = sh[r, pl.ds(pl.multiple_of(base + 8 * q, 8), RB), :].reshape(RB // 8, 8, CW)
                    acc = acc + slab * w8[k, :, pl.ds(c0, CW)]
                ac_ref[pl.ds(base, RB), pl.ds(c0, CW)] = acc.reshape(RB, CW)
                return carry

            lax.fori_loop(0, TM // RB, rows, 0)
        for g, win in enumerate(POOL_WINDOWS):
            left = win // 2
            right = win - 1 - left
            cols = pl.ds(g * PG, PG)
            s = p_ext[pl.ds(HALO - left, TM), cols]
            for off in range(-left + 1, right + 1):
                s = s + p_ext[pl.ds(HALO + off, TM), cols]
            cnt = _pool_cnt(i, seq, tp, left, right, TM, 0)
            m_ref[:, cols] = (s / cnt - p_ext[pl.ds(HALO, TM), cols]).astype(BF16)

    return pl.pallas_call(
        body, name="seq_fwd", grid=(nt,),
        in_specs=_halo_specs(0, nt) + _halo_specs(1, nt) + _halo_specs(2, nt)
        + [pl.BlockSpec((32, D), lambda i: (0, 0)), pl.BlockSpec((1, D), lambda i: (0, 0))],
        out_specs=[pl.BlockSpec((TM, D), lambda i: (i, 0))] * 2,
        out_shape=[jax.ShapeDtypeStruct((tp, D), F32), jax.ShapeDtypeStruct((tp, D), BF16)],
        scratch_shapes=[pltpu.VMEM((TM + 2 * HALO, D), F32), pltpu.VMEM((TM + 2 * HALO, D), F32),
                        pltpu.VMEM((8, TM + 24, CW), F32), pltpu.VMEM((CONV_K, 8, D), F32)],
        compiler_params=_params(("arbitrary",), 40),
    )(z, z, z, z, z, z, z, z, z, w_dw, b_dw)


def _ln_stats(ac):
    mu = jnp.mean(ac, axis=-1, keepdims=True)
    xc = ac - mu
    rl = lax.rsqrt(jnp.mean(xc * xc, axis=-1, keepdims=True) + LN_EPS)
    return xc * rl, rl


def _pool_mix(m, wp_ref):
    return jnp.concatenate(
        [_dot(m[:, g * PG:(g + 1) * PG], wp_ref[:, g].reshape(PG, PG)) for g in range(4)], axis=1)


def _mix_fwd(ac, m, z, h0, b_gate, ln_g, ln_b, pool_scale, g_mixw, g_pool):
    tp = h0.shape[0]
    nt = tp // TMS

    def body(ac_ref, m_ref, zga, zgb, h_ref, bg_ref, lg_ref, lb_ref, ps_ref, wm_hbm, wp_hbm,
             h1_ref, s_ref, yc_ref, yp_ref, mg_ref, q_ref, wm, wp, sems):
        _load_once(pl.program_id(0) == 0, [(wm_hbm, wm), (wp_hbm, wp)], sems)
        n, _ = _ln_stats(ac_ref[...])
        l = n * lg_ref[...] + lb_ref[...]
        s = (l * _sig(l)).astype(BF16)
        s_ref[...] = s
        yc = _dot(s, wm[:, 0].reshape(D, D))
        q = (_pool_mix(m_ref[...], wp) * ps_ref[...]).astype(BF16)
        q_ref[...] = q
        yp = _dot(q, wm[:, 1].reshape(D, D))
        ga = _sig(zga[...] + bg_ref[:, :D])
        gb = _sig(zgb[...] + bg_ref[:, D:])
        merged = (ga * yc + gb * yp).astype(BF16)
        yc_ref[...] = yc
        yp_ref[...] = yp
        mg_ref[...] = merged
        h1_ref[...] = h_ref[...] + _dot(merged, wm[:, 2].reshape(D, D))

    def tile(col=0):
        return pl.BlockSpec((TMS, D), lambda i: (i, col))

    def vec(w):
        return pl.BlockSpec((1, w), lambda i: (0, 0))

    anys = pl.BlockSpec(memory_space=pl.ANY)
    f32o, b16o = jax.ShapeDtypeStruct((tp, D), F32), jax.ShapeDtypeStruct((tp, D), BF16)
    return pl.pallas_call(
        body, name="mix_fwd", grid=(nt,),
        in_specs=[tile(), tile(), tile(3), tile(4), tile(), vec(2 * D), vec(D), vec(D), vec(D), anys, anys],
        out_specs=[tile()] * 6,
        out_shape=[f32o, b16o, f32o, f32o, b16o, b16o],
        scratch_shapes=[pltpu.VMEM((NDEV, 3, D // NDEV, D), BF16), pltpu.VMEM((NDEV, 4, PG // NDEV, PG), BF16),
                        pltpu.SemaphoreType.DMA((2,))],
        compiler_params=_params(("arbitrary",), 48),
    )(ac, m, z, z, h0, b_gate, ln_g, ln_b, pool_scale, g_mixw, g_pool)


def _ffn_fwd(h1, tgt, g_ffn, g_final, w_gu, w_dn):
    tp = h1.shape[0]
    nt = tp // TM
    nx_last = tgt.shape[0] - (nt - 1) * TM

    def body(h_ref, t_ref, gf_ref, gl_ref, wgu_hbm, wdn_hbm,
             fg_ref, fu_ref, v_ref, f_ref, dh2_ref, acc_ref, wgu, wdn, v_sc, h2_sc, diff_sc, sems):
        i, j = pl.program_id(0), pl.program_id(1)
        _load_once((i == 0) & (j == 0), [(wgu_hbm, wgu), (wdn_hbm, wdn)], sems)

        @pl.when((i == 0) & (j == 0))
        def _():
            acc_ref[...] = jnp.zeros_like(acc_ref)

        @pl.when(j == 0)
        def _():
            h = h_ref[...]
            r = lax.rsqrt(jnp.mean(h * h, axis=-1, keepdims=True) + RMS_EPS)
            v = (h * r * gf_ref[...]).astype(BF16)
            v_sc[...] = v
            v_ref[...] = v
            h2_sc[...] = h

        v = v_sc[...]
        fg = _dot(v, wgu[0, j])
        fu = _dot(v, wgu[1, j])
        fg_ref[...] = fg
        fu_ref[...] = fu
        f = ((fg * _sig(fg)) * fu).astype(BF16)
        f_ref[...] = f
        h2_sc[...] += _dot(f, wdn[j])

        @pl.when(j == 1)
        def _():
            h2 = h2_sc[...]
            r = lax.rsqrt(jnp.mean(h2 * h2, axis=-1, keepdims=True) + RMS_EPS)
            n2 = h2 * r
            y = n2 * gl_ref[...]

            @pl.when(i < nt - 1)
            def _():
                diff_sc[...] = y - t_ref[...]

            @pl.when(i == nt - 1)
            def _():
                diff_sc[pl.ds(0, nx_last), :] = y[:nx_last] - t_ref[pl.ds(0, nx_last), :]
                diff_sc[pl.ds(nx_last, TM - nx_last), :] = jnp.zeros((TM - nx_last, D), F32)

            diff = diff_sc[...]
            dy = diff * (1.0 / D)
            acc_ref[0:1, :] += jnp.sum(diff * diff, axis=0, keepdims=True)
            acc_ref[1:2, :] += jnp.sum(dy * n2, axis=0, keepdims=True)
            dn = dy * gl_ref[...]
            dh2_ref[...] = r * (dn - n2 * jnp.mean(dn * n2, axis=-1, keepdims=True))

    def tile():
        return pl.BlockSpec((TM, D), lambda i, j: (i, 0))

    def chunk():
        return pl.BlockSpec((TM, FFC), lambda i, j: (i, j))

    def vec():
        return pl.BlockSpec((1, D), lambda i, j: (0, 0))

    anys = pl.BlockSpec(memory_space=pl.ANY)
    hid32, hid16 = jax.ShapeDtypeStruct((tp, DFF), F32), jax.ShapeDtypeStruct((tp, DFF), BF16)
    return pl.pallas_call(
        body, name="ffn_fwd", grid=(nt, 2),
        in_specs=[tile(), tile(), vec(), vec(), anys, anys],
        out_specs=[chunk(), chunk(), tile(), chunk(), tile(), pl.BlockSpec((8, D), lambda i, j: (0, 0))],
        out_shape=[hid32, hid32, jax.ShapeDtypeStruct((tp, D), BF16), hid16, jax.ShapeDtypeStruct((tp, D), F32),
                   jax.ShapeDtypeStruct((8, D), F32)],
        scratch_shapes=[pltpu.VMEM((2, 2, D, FFC), BF16), pltpu.VMEM((2, FFC, D), BF16),
                        pltpu.VMEM((TM, D), BF16), pltpu.VMEM((TM, D), F32), pltpu.VMEM((TM, D), F32),
                        pltpu.SemaphoreType.DMA((2,))],
        compiler_params=_params(("arbitrary", "arbitrary"), 56),
    )(h1, tgt, g_ffn, g_final, w_gu, w_dn)


def _ffn_bwd(dh2, fg, fu, h1, g_ffn, w_gu, w_dn):
    tp = h1.shape[0]
    nt = tp // TM

    def body(dh2_ref, fg_ref, fu_ref, h_ref, gf_ref, wgu_hbm, wdn_hbm,
             dfg_ref, dfu_ref, dh1_ref, acc_ref, wgu, wdn, d_sc, dv_sc, sems):
        i, j = pl.program_id(0), pl.program_id(1)
        _load_once((i == 0) & (j == 0), [(wgu_hbm, wgu), (wdn_hbm, wdn)], sems)

        @pl.when((i == 0) & (j == 0))
        def _():
            acc_ref[...] = jnp.zeros_like(acc_ref)

        @pl.when(j == 0)
        def _():
            d_sc[...] = dh2_ref[...].astype(BF16)
            dv_sc[...] = jnp.zeros_like(dv_sc)

        df = _dot_nt(d_sc[...], wdn[j])
        fg = fg_ref[...]
        sg = _sig(fg)
        dfu = (df * (fg * sg)).astype(BF16)
        dfg = (df * fu_ref[...] * (sg * (1.0 + fg * (1.0 - sg)))).astype(BF16)
        dfg_ref[...] = dfg
        dfu_ref[...] = dfu
        dv_sc[...] += _dot_nt(dfg, wgu[0, j]) + _dot_nt(dfu, wgu[1, j])

        @pl.when(j == 1)
        def _():
            h = h_ref[...]
            r = lax.rsqrt(jnp.mean(h * h, axis=-1, keepdims=True) + RMS_EPS)
            n1 = h * r
            dv = dv_sc[...]
            acc_ref[0:1, :] += jnp.sum(dv * n1, axis=0, keepdims=True)
            dn = dv * gf_ref[...]
            dh1_ref[...] = dh2_ref[...] + r * (dn - n1 * jnp.mean(dn * n1, axis=-1, keepdims=True))

    def tile():
        return pl.BlockSpec((TM, D), lambda i, j: (i, 0))

    def chunk():
        return pl.BlockSpec((TM, FFC), lambda i, j: (i, j))

    anys = pl.BlockSpec(memory_space=pl.ANY)
    hid16 = jax.ShapeDtypeStruct((tp, DFF), BF16)
    return pl.pallas_call(
        body, name="ffn_bwd", grid=(nt, 2),
        in_specs=[tile(), chunk(), chunk(), tile(), pl.BlockSpec((1, D), lambda i, j: (0, 0)), anys, anys],
        out_specs=[chunk(), chunk(), tile(), pl.BlockSpec((8, D), lambda i, j: (0, 0))],
        out_shape=[hid16, hid16, jax.ShapeDtypeStruct((tp, D), F32), jax.ShapeDtypeStruct((8, D), F32)],
        scratch_shapes=[pltpu.VMEM((2, 2, D, FFC), BF16), pltpu.VMEM((2, FFC, D), BF16),
                        pltpu.VMEM((TM, D), BF16), pltpu.VMEM((TM, D), F32), pltpu.SemaphoreType.DMA((2,))],
        compiler_params=_params(("arbitrary", "arbitrary"), 56),
    )(dh2, fg, fu, h1, g_ffn, w_gu, w_dn)


def _mix_bwd(dh1, z, yc, yp, ac, m, b_gate, ln_g, ln_b, pool_scale, g_mixw, g_pool):
    tp = dh1.shape[0]
    nt = tp // TMS

    def body(dh1_ref, zga, zgb, yc_ref, yp_ref, ac_ref, m_ref, bg_ref, lg_ref, lb_ref, ps_ref, wm_hbm, wp_hbm,
             dac_ref, dm_ref, dzg_ref, dyc_ref, dyp_ref, dm2_ref, acc_ref, wm, wp, sems):
        first = pl.program_id(0) == 0
        _load_once(first, [(wm_hbm, wm), (wp_hbm, wp)], sems)

        @pl.when(first)
        def _():
            acc_ref[...] = jnp.zeros_like(acc_ref)

        dmerged = _dot_nt(dh1_ref[...].astype(BF16), wm[:, 2].reshape(D, D))
        ga = _sig(zga[...] + bg_ref[:, :D])
        gb = _sig(zgb[...] + bg_ref[:, D:])
        dyc = dmerged * ga
        dyp = dmerged * gb
        dza = (dmerged * yc_ref[...]) * (ga * (1.0 - ga))
        dzb = (dmerged * yp_ref[...]) * (gb * (1.0 - gb))
        dzg_ref[:, :D] = dza.astype(BF16)
        dzg_ref[:, D:] = dzb.astype(BF16)
        acc_ref[0:1, :D] += jnp.sum(dza, axis=0, keepdims=True)
        acc_ref[0:1, D:] += jnp.sum(dzb, axis=0, keepdims=True)
        dyc_b = dyc.astype(BF16)
        dyp_b = dyp.astype(BF16)
        dyc_ref[...] = dyc_b
        dyp_ref[...] = dyp_b
        ds = _dot_nt(dyc_b, wm[:, 0].reshape(D, D))
        n, rl = _ln_stats(ac_ref[...])
        l = n * lg_ref[...] + lb_ref[...]
        sg = _sig(l)
        dl = ds * (sg * (1.0 + l * (1.0 - sg)))
        acc_ref[1:2, :D] += jnp.sum(dl * n, axis=0, keepdims=True)
        acc_ref[1:2, D:] += jnp.sum(dl, axis=0, keepdims=True)
        dn = dl * lg_ref[...]
        dac_ref[...] = rl * (dn - jnp.mean(dn, axis=-1, keepdims=True) - n * jnp.mean(dn * n, axis=-1, keepdims=True))
        dq = _dot_nt(dyp_b, wm[:, 1].reshape(D, D))
        mv = m_ref[...]
        acc_ref[2:3, :D] += jnp.sum(dq * _pool_mix(mv, wp), axis=0, keepdims=True)
        dm2 = (dq * ps_ref[...]).astype(BF16)
        dm2_ref[...] = dm2
        dm_ref[...] = jnp.concatenate(
            [_dot_nt(dm2[:, g * PG:(g + 1) * PG], wp[:, g].reshape(PG, PG)) for g in range(4)], axis=1)

    def tile(col=0):
        return pl.BlockSpec((TMS, D), lambda i: (i, col))

    def vec(w):
        return pl.BlockSpec((1, w), lambda i: (0, 0))

    anys = pl.BlockSpec(memory_space=pl.ANY)
    f32o, b16o = jax.ShapeDtypeStruct((tp, D), F32), jax.ShapeDtypeStruct((tp, D), BF16)
    return pl.pallas_call(
        body, name="mix_bwd", grid=(nt,),
        in_specs=[tile(), tile(3), tile(4), tile(), tile(), tile(), tile(), vec(2 * D), vec(D), vec(D), vec(D), anys, anys],
        out_specs=[tile(), tile(), pl.BlockSpec((TMS, 2 * D), lambda i: (i, 0)), tile(), tile(), tile(),
                   pl.BlockSpec((8, 2 * D), lambda i: (0, 0))],
        out_shape=[f32o, f32o, jax.ShapeDtypeStruct((tp, 2 * D), BF16), b16o, b16o, b16o,
                   jax.ShapeDtypeStruct((8, 2 * D), F32)],
        scratch_shapes=[pltpu.VMEM((NDEV, 3, D // NDEV, D), BF16), pltpu.VMEM((NDEV, 4, PG // NDEV, PG), BF16),
                        pltpu.SemaphoreType.DMA((2,))],
        compiler_params=_params(("arbitrary",), 48),
    )(dh1, z, z, yc, yp, ac, m, b_gate, ln_g, ln_b, pool_scale, g_mixw, g_pool)


def _seq_bwd(dac, dm, dzg, z, w_dw, seq):
    tp = z.shape[0]
    nt = tp // TM

    def body(dac_l, dac_c, dac_r, dm_l, dm_c, dm_r, av_l, av, av_r, ag_l, ag, ag_r, dzg_ref, w_ref,
             dz_ref, acc_ref, a_ext, d_ext, m_ext, sha, shd, da_sc, dw_sc, w8):
        i = pl.program_id(0)

        @pl.when(i == 0)
        def _():
            dw_sc[...] = jnp.zeros_like(dw_sc)
            acc_ref[...] = jnp.zeros_like(acc_ref)
            _tap_rows(w_ref, w8)

        sg = _sig(ag[...])
        _fill_ext(a_ext, av_l[...] * _sig(ag_l[...]), av[...] * sg, av_r[...] * _sig(ag_r[...]))
        _fill_ext(d_ext, dac_l[...], dac_c[...], dac_r[...])
        for g, win in enumerate(POOL_WINDOWS):
            left = win // 2
            right = win - 1 - left
            cols = pl.ds(g * PG, PG)
            m_ext[pl.ds(0, HALO), cols] = dm_l[:, cols] / _pool_cnt(i, seq, tp, left, right, HALO, -HALO)
            m_ext[pl.ds(HALO, TM), cols] = dm_c[:, cols] / _pool_cnt(i, seq, tp, left, right, TM, 0)
            m_ext[pl.ds(HALO + TM, HALO), cols] = dm_r[:, cols] / _pool_cnt(i, seq, tp, left, right, HALO, TM)
        for c0 in range(0, D, CW):
            _shift_copies(a_ext, sha, c0)
            _shift_copies(d_ext, shd, c0)

            def rows(j, carry):
                base = pl.multiple_of(j * RB, RB)
                dcur = d_ext[pl.ds(pl.multiple_of(base + HALO, 8), RB), pl.ds(c0, CW)].reshape(RB // 8, 8, CW)
                acc = jnp.zeros((RB // 8, 8, CW), F32)
                for k in range(CONV_K):
                    q, r = divmod(CONV_K - k, 8)
                    slab = shd[r, pl.ds(pl.multiple_of(base + 8 * q, 8), RB), :].reshape(RB // 8, 8, CW)
                    acc = acc + slab * w8[k, :, pl.ds(c0, CW)]
                    q, r = divmod(k + 1, 8)
                    slab = sha[r, pl.ds(pl.multiple_of(base + 8 * q, 8), RB), :].reshape(RB // 8, 8, CW)
                    dw_sc[k, :, pl.ds(c0, CW)] += jnp.sum(dcur * slab, axis=0)
                da_sc[pl.ds(base, RB), pl.ds(c0, CW)] = acc.reshape(RB, CW)
                return carry

            lax.fori_loop(0, TM // RB, rows, 0)
        da = da_sc[...]
        dz_ref[:, 0:D] = (da * sg).astype(BF16)
        dz_ref[:, D:2 * D] = (da * av[...] * (sg * (1.0 - sg))).astype(BF16)
        for g, win in enumerate(POOL_WINDOWS):
            left = win // 2
            right = win - 1 - left
            cols = pl.ds(g * PG, PG)
            s = m_ext[pl.ds(HALO - right, TM), cols]
            for off in range(-right + 1, left + 1):
                s = s + m_ext[pl.ds(HALO + off, TM), cols]
            dz_ref[:, pl.ds(2 * D + g * PG, PG)] = (s - dm_c[:, cols]).astype(BF16)
        dz_ref[:, 3 * D:] = dzg_ref[...]

        @pl.when(i == nt - 1)
        def _():
            for k in range(CONV_K):
                acc_ref[k:k + 1, :] = jnp.sum(dw_sc[k], axis=0, keepdims=True)

        acc_ref[CONV_K:CONV_K + 1, :] += jnp.sum(dac_c[...], axis=0, keepdims=True)

    ext = pltpu.VMEM((TM + 2 * HALO, D), F32)
    shs = pltpu.VMEM((8, TM + 24, CW), F32)
    return pl.pallas_call(
        body, name="seq_bwd", grid=(nt,),
        in_specs=_halo_specs(0, nt) + _halo_specs(0, nt) + _halo_specs(0, nt) + _halo_specs(1, nt)
        + [pl.BlockSpec((TM, 2 * D), lambda i: (i, 0)), pl.BlockSpec((32, D), lambda i: (0, 0))],
        out_specs=[pl.BlockSpec((TM, DIN), lambda i: (i, 0)), pl.BlockSpec((32, D), lambda i: (0, 0))],
        out_shape=[jax.ShapeDtypeStruct((tp, DIN), BF16), jax.ShapeDtypeStruct((32, D), F32)],
        scratch_shapes=[ext, ext, ext, shs, shs, pltpu.VMEM((TM, D), F32), pltpu.VMEM((CONV_K, 8, D), F32),
                        pltpu.VMEM((CONV_K, 8, D), F32)],
        compiler_params=_params(("arbitrary",), 48),
    )(dac, dac, dac, dm, dm, dm, z, z, z, z, z, z, dzg, w_dw)


def _in_bwd(dz, h0, dh1, g_mix, w_g, seq):
    tp = h0.shape[0]
    nt = tp // TM

    def body(dz_ref, h_ref, dh1_ref, g_ref, w_hbm, gx_ref, gmeta_ref, acc_ref, w_vm, sems):
        i = pl.program_id(0)
        _load_once(i == 0, [(w_hbm, w_vm)], sems)

        @pl.when(i == 0)
        def _():
            acc_ref[...] = jnp.zeros_like(acc_ref)

        du = _dot_nt(dz_ref[:, 0:INB], w_vm[0])
        for d in range(1, NDEV):
            du = du + _dot_nt(dz_ref[:, INB * d:INB * (d + 1)], w_vm[d])
        h = h_ref[...]
        r = lax.rsqrt(jnp.mean(h * h, axis=-1, keepdims=True) + RMS_EPS)
        n0 = h * r
        acc_ref[0:1, :] += jnp.sum(du * n0, axis=0, keepdims=True)
        dn = du * g_ref[...]
        gx_ref[...] = dh1_ref[...] + r * (dn - n0 * jnp.mean(dn * n0, axis=-1, keepdims=True))

        @pl.when(i == nt - 1)
        def _():
            gmeta_ref[...] = gx_ref[pl.ds(TM - N_META, N_META), :]

    tile = pl.BlockSpec((TM, D), lambda i: (i, 0))
    return pl.pallas_call(
        body, name="in_bwd", grid=(nt,),
        in_specs=[pl.BlockSpec((TM, DIN), lambda i: (i, 0)), tile, tile, pl.BlockSpec((1, D), lambda i: (0, 0)),
                  pl.BlockSpec(memory_space=pl.ANY)],
        out_specs=[tile, pl.BlockSpec((N_META, D), lambda i: (0, 0)), pl.BlockSpec((8, D), lambda i: (0, 0))],
        out_shape=[jax.ShapeDtypeStruct((seq, D), F32), jax.ShapeDtypeStruct((N_META, D), F32),
                   jax.ShapeDtypeStruct((8, D), F32)],
        scratch_shapes=[pltpu.VMEM((NDEV, D, INB), BF16), pltpu.SemaphoreType.DMA((1,))],
        compiler_params=_params(("arbitrary",), 48),
    )(dz, h0, dh1, g_mix, w_g)


def _wgrad_in(u, dz):
    tp = u.shape[0]
    nt = tp // TM
    half = DIN // 2

    def body(u_ref, dz_ref, o_ref, acc):
        t = pl.program_id(1)

        @pl.when(t == 0)
        def _():
            acc[...] = jnp.zeros_like(acc)

        acc[...] += _dot_tn(u_ref[...], dz_ref[...])

        @pl.when(t == nt - 1)
        def _():
            for d in range(4):
                o_ref[d] = acc[:, INB * d:INB * (d + 1)].astype(BF16)

    return pl.pallas_call(
        body, name="wgrad_in", grid=(2, nt),
        in_specs=[pl.BlockSpec((TM, D), lambda h, t: (t, 0)), pl.BlockSpec((TM, half), lambda h, t: (t, h))],
        out_specs=pl.BlockSpec((4, D, INB), lambda h, t: (h, 0, 0)),
        out_shape=jax.ShapeDtypeStruct((NDEV, D, INB), BF16),
        scratch_shapes=[pltpu.VMEM((D, half), F32)],
        compiler_params=_params(("arbitrary", "arbitrary"), 48),
    )(u, dz)


def _wgrad_mix(s, dyc, q, dyp, merged, dh1, m, dm2):
    tp = s.shape[0]
    nt = tp // TM
    rb = D // NDEV

    def body(s_ref, dyc_ref, q_ref, dyp_ref, mg_ref, dh1_ref, m_ref, dm2_ref, o_ref, op_ref, acc, accp):
        t = pl.program_id(0)

        @pl.when(t == 0)
        def _():
            acc[...] = jnp.zeros_like(acc)
            accp[...] = jnp.zeros_like(accp)

        acc[0] += _dot_tn(s_ref[...], dyc_ref[...])
        acc[1] += _dot_tn(q_ref[...], dyp_ref[...])
        acc[2] += _dot_tn(mg_ref[...], dh1_ref[...].astype(BF16))
        for g in range(4):
            accp[g] += _dot_tn(m_ref[:, g * PG:(g + 1) * PG], dm2_ref[:, g * PG:(g + 1) * PG])

        @pl.when(t == nt - 1)
        def _():
            for d in range(NDEV):
                for k in range(3):
                    o_ref[d, k] = acc[k, rb * d:rb * (d + 1), :].astype(BF16)
                for g in range(4):
                    op_ref[d, g] = accp[g, 32 * d:32 * (d + 1), :].astype(BF16)

    tile = pl.BlockSpec((TM, D), lambda t: (t, 0))
    return pl.pallas_call(
        body, name="wgrad_mix", grid=(nt,),
        in_specs=[tile] * 8,
        out_specs=[pl.BlockSpec((NDEV, 3, rb, D), lambda t: (0, 0, 0, 0)),
                   pl.BlockSpec((NDEV, 4, 32, PG), lambda t: (0, 0, 0, 0))],
        out_shape=[jax.ShapeDtypeStruct((NDEV, 3, rb, D), BF16), jax.ShapeDtypeStruct((NDEV, 4, 32, PG), BF16)],
        scratch_shapes=[pltpu.VMEM((3, D, D), F32), pltpu.VMEM((4, PG, PG), F32)],
        compiler_params=_params(("arbitrary",), 56),
    )(s, dyc, q, dyp, merged, dh1, m, dm2)


def _wgrad_gu(v, dfg, dfu):
    tp = v.shape[0]
    nt = tp // TM

    def body(v_ref, dg_ref, du_ref, o_ref, acc):
        k, t = pl.program_id(0), pl.program_id(2)

        @pl.when(t == 0)
        def _():
            acc[...] = jnp.zeros_like(acc)

        @pl.when(k == 0)
        def _():
            acc[...] += _dot_tn(v_ref[...], dg_ref[...])

        @pl.when(k == 1)
        def _():
            acc[...] += _dot_tn(v_ref[...], du_ref[...])

        @pl.when(t == nt - 1)
        def _():
            for d in range(4):
                o_ref[d] = acc[:, pl.ds(FFB * d, FFB)].astype(BF16)

    return pl.pallas_call(
        body, name="wgrad_gu", grid=(2, 2, nt),
        in_specs=[pl.BlockSpec((TM, D), lambda k, h, t: (t, 0)),
                  pl.BlockSpec((TM, FFC), lambda k, h, t: (t * (1 - k), h * (1 - k))),
                  pl.BlockSpec((TM, FFC), lambda k, h, t: (t * k, h * k))],
        out_specs=pl.BlockSpec((4, None, D, FFB), lambda k, h, t: (h, k, 0, 0)),
        out_shape=jax.ShapeDtypeStruct((NDEV, 2, D, FFB), BF16),
        scratch_shapes=[pltpu.VMEM((D, FFC), F32)],
        compiler_params=_params(("arbitrary",) * 3, 40),
    )(v, dfg, dfu)


def _wgrad_down(f, dh2):
    tp = f.shape[0]
    nt = tp // TM

    def body(f_ref, d_ref, o_ref, acc):
        t = pl.program_id(1)

        @pl.when(t == 0)
        def _():
            acc[...] = jnp.zeros_like(acc)

        acc[...] += _dot_tn(f_ref[...], d_ref[...].astype(BF16))

        @pl.when(t == nt - 1)
        def _():
            for d in range(4):
                o_ref[d] = acc[FFB * d:FFB * (d + 1), :].astype(BF16)

    return pl.pallas_call(
        body, name="wgrad_down", grid=(2, nt),
        in_specs=[pl.BlockSpec((TM, FFC), lambda h, t: (t, h)), pl.BlockSpec((TM, D), lambda h, t: (t, 0))],
        out_specs=pl.BlockSpec((4, FFB, D), lambda h, t: (h, 0, 0)),
        out_shape=jax.ShapeDtypeStruct((NDEV, FFB, D), BF16),
        scratch_shapes=[pltpu.VMEM((FFC, D), F32)],
        compiler_params=_params(("arbitrary", "arbitrary"), 40),
    )(f, dh2)


def kernel(x, meta_tokens, g_mix, w_in, b_gate, w_dw, b_dw, ln_g, ln_b, w_conv_out, w_pool, pool_scale, w_pool_out, w_o, g_ffn, w_ffn_gate, w_ffn_up, w_ffn_down, g_final, loss_target, m_meta_tokens, m_g_mix, m_w_in, m_b_gate, m_w_dw, m_b_dw, m_ln_g, m_ln_b, m_w_conv_out, m_w_pool, m_pool_scale, m_w_pool_out, m_w_o, m_g_ffn, m_w_ffn_gate, m_w_ffn_up, m_w_ffn_down, m_g_final, v_meta_tokens, v_g_mix, v_w_in, v_b_gate, v_w_dw, v_b_dw, v_ln_g, v_ln_b, v_w_conv_out, v_w_pool, v_pool_scale, v_w_pool_out, v_w_o, v_g_ffn, v_w_ffn_gate, v_w_ffn_up, v_w_ffn_down, v_g_final):
    seq = x.shape[1]
    tp = -(-(seq + 2 * HALO) // TM) * TM
    nx_last = seq - (tp // TM - 1) * TM
    assert 0 < nx_last <= TM - 2 * HALO and nx_last % 8 == 0

    whole = (Ellipsis,)
    groups = [((D, INB), [(w_in, whole, 0)]),
              ((48, D // NDEV), [(meta_tokens, pl.ds(0, N_META), whole), (w_dw, pl.ds(N_META, CONV_K), 0)]),
              ((3, D // NDEV, D), [(w_conv_out, 0, 0), (w_pool_out, 1, 0), (w_o, 2, 0)]),
              ((4, PG // NDEV, PG), [(w_pool, whole, 0)]),
              ((2, D, FFB), [(w_ffn_gate, 0, 0), (w_ffn_up, 1, 0)]),
              ((FFB, D), [(w_ffn_down, whole, 0)])]
    g_in, g_small, g_mixw, g_pool, g_gu, g_down = _all_gather(groups, [BF16, F32, BF16, BF16, BF16, BF16])
    small_full = g_small.transpose(1, 0, 2).reshape(48, D)
    wdw_full = small_full[N_META:]
    tail = jnp.concatenate([jnp.zeros((TM - nx_last - N_META, D), F32), small_full[:N_META]], axis=0)
    w_gu = _repack_gu(g_gu)
    w_dn = g_down.reshape(2, FFC, D)

    h0, z, u = _fwd_in(x[0], tail, g_mix, g_in, tp)
    ac, m = _seq_fwd(z, wdw_full, b_dw, seq)
    h1, s, yc, yp, merged, q = _mix_fwd(ac, m, z, h0, b_gate, ln_g, ln_b, pool_scale, g_mixw, g_pool)
    fg, fu, v, f, dh2, head_acc = _ffn_fwd(h1, loss_target[0], g_ffn, g_final.reshape(1, D), w_gu, w_dn)

    dfg, dfu, dh1, ffn_acc = _ffn_bwd(dh2, fg, fu, h1, g_ffn, w_gu, w_dn)
    dac, dm, dzg, dyc, dyp, dm2, mix_acc = _mix_bwd(dh1, z, yc, yp, ac, m, b_gate, ln_g, ln_b, pool_scale, g_mixw, g_pool)
    dz, seq_acc = _seq_bwd(dac, dm, dzg, z, wdw_full, seq)
    grad_x, g_meta, in_acc = _in_bwd(dz, h0, dh1, g_mix, g_in, seq)

    p_in = _wgrad_in(u, dz)
    p_mix, p_pool = _wgrad_mix(s, dyc, q, dyp, merged, dh1, m, dm2)
    p_gu = _wgrad_gu(v, dfg, dfu)
    p_down = _wgrad_down(f, dh2)
    small_g = jnp.concatenate([g_meta, seq_acc[:CONV_K], jnp.zeros((1, D), F32)], axis=0)
    p_small = small_g.reshape(48, NDEV, D // NDEV).transpose(1, 0, 2).astype(BF16)
    rep_g = jnp.concatenate([
        in_acc[0:1], mix_acc[0:1, :D], mix_acc[0:1, D:], seq_acc[CONV_K:CONV_K + 1], mix_acc[1:2, :D], mix_acc[1:2, D:],
        mix_acc[2:3, :D], ffn_acc[0:1], head_acc[1:2], head_acc[0:1], jnp.zeros((REP_ROWS - 10, D), F32)], axis=0)

    owns, sibs, rels, rep_all = _reduce_scatter([p_in, p_small, p_mix, p_pool, p_gu, p_down], rep_g)

    def lead(a):
        return a.reshape(1, *a.shape)

    def stack4(a, lead_dims):
        return a.reshape(*lead_dims, 1, 4 * 32, PG)

    (r_in,) = _adamw_multi("adamw_in", lead(owns[0]), sibs[0][:, None], rels[0][:, None], [w_in], [m_w_in], [v_w_in], 4)
    r_meta, r_dw = _adamw_meta_dw(owns[1], sibs[1], rels[1], (meta_tokens, m_meta_tokens, v_meta_tokens),
                                  (w_dw, m_w_dw, v_w_dw))
    r_conv, r_pout, r_o = _adamw_multi("adamw_mix", owns[2], sibs[2], rels[2], [w_conv_out, w_pool_out, w_o],
                                       [m_w_conv_out, m_w_pool_out, m_w_o], [v_w_conv_out, v_w_pool_out, v_w_o], 1)
    (r_pool,) = _adamw_multi("adamw_pool", stack4(owns[3], ()), stack4(sibs[3], (4,)), stack4(rels[3], (3,)),
                             [w_pool.reshape(1, 128, PG)], [m_w_pool.reshape(1, 128, PG)], [v_w_pool.reshape(1, 128, PG)], 1)
    r_pool = tuple(a.reshape(w_pool.shape) for a in r_pool)
    r_gate, r_up = _adamw_multi("adamw_gu", owns[4], sibs[4], rels[4], [w_ffn_gate, w_ffn_up],
                                [m_w_ffn_gate, m_w_ffn_up], [v_w_ffn_gate, v_w_ffn_up], 4)
    (r_down,) = _adamw_multi("adamw_down", lead(owns[5]), sibs[5][:, None], rels[5][:, None],
                             [w_ffn_down], [m_w_ffn_down], [v_w_ffn_down], 2)
    row = (1, D)
    loss, reps = _adamw_rep(
        rep_all,
        [g_mix, b_gate, b_dw, ln_g, ln_b, pool_scale, g_ffn, g_final.reshape(row)],
        [m_g_mix, m_b_gate, m_b_dw, m_ln_g, m_ln_b, m_pool_scale, m_g_ffn, m_g_final.reshape(row)],
        [v_g_mix, v_b_gate, v_b_dw, v_ln_g, v_ln_b, v_pool_scale, v_g_ffn, v_g_final.reshape(row)])
    r_gmix, r_bg, r_bdw, r_lg, r_lb, r_ps, r_gffn, r_gfin = reps
    r_gfin = tuple(a.reshape(D) for a in r_gfin)

    in_order = [r_meta, r_gmix, r_in, r_bg, r_dw, r_bdw, r_lg, r_lb, r_conv, r_pool, r_ps, r_pout, r_o, r_gffn,
                r_gate, r_up, r_down, r_gfin]
    return (loss.reshape(()), grad_x[None], *[r[0] for r in in_order], *[r[1] for r in in_order],
            *[r[2] for r in in_order], *[r[3] for r in in_order])
```

```python
import math

import jax
import jax.numpy as jnp
from jax import lax
from jax.experimental import pallas as pl
from jax.experimental.pallas import tpu as pltpu

F32, BF16 = jnp.float32, jnp.bfloat16
MESH_ID = pl.DeviceIdType.MESH
NDEV = 8

D = 1024
N_META = 16
CONV_K = 31
HALO = 16
POOL_WINDOWS = (2, 4, 8, 16)
PG = 256
DIN = 5 * D
DFF = 2816
FFB = DFF // NDEV
FFC = DFF // 2
INB = DIN // NDEV
RMS_EPS = 1e-6
LN_EPS = 1e-5
ADAM_LR, ADAM_B1, ADAM_B2, ADAM_EPS, ADAM_WD, ADAM_STEP = 0.001, 0.9, 0.999, 1e-08, 0.01, 10

TM = 384
TMS = 192
RB, CW = 64, 128
MIB = 2 ** 20


def _sig(x):
    return 1.0 / (1.0 + jnp.exp(-x))


def _dot(a, b):
    return jnp.dot(a, b, preferred_element_type=F32)


def _dot_nt(a, b):
    return lax.dot_general(a, b, (((1,), (1,)), ((), ())), preferred_element_type=F32)


def _dot_tn(a, b):
    return lax.dot_general(a, b, (((0,), (0,)), ((), ())), preferred_element_type=F32)


def _params(sem, vmem_mib):
    return pltpu.CompilerParams(dimension_semantics=sem, vmem_limit_bytes=vmem_mib * MIB)


def _load_once(first, pairs, sems):
    @pl.when(first)
    def _():
        cps = [pltpu.make_async_copy(s, d, sems.at[k]) for k, (s, d) in enumerate(pairs)]
        for cp in cps:
            cp.start()
        for cp in cps:
            cp.wait()


def _place():
    x, y, c = lax.axis_index("x"), lax.axis_index("y"), lax.axis_index("c")
    return x, y, c


class _Gather:
    def __init__(self, groups, dtypes):
        self.groups, self.dtypes, self.n = groups, dtypes, len(groups)
        self.arrays = [a for _, parts in groups for a, _, _ in parts]
        self.out_shape = [jax.ShapeDtypeStruct((NDEV, *s), dt) for (s, _), dt in zip(groups, dtypes)]
        self.scratch = [pltpu.VMEM(s, dt) for (s, _), dt in zip(groups, dtypes)] + [
            pltpu.SemaphoreType.DMA((7 * self.n,)), pltpu.SemaphoreType.DMA((7 * self.n,)),
            pltpu.SemaphoreType.DMA((self.n,))]

    def bind(self, ins, outs, scratch):
        self.ins, self.outs, self.stages = ins, outs, scratch[:self.n]
        self.send_sems, self.recv_sems, self.local_sems = scratch[self.n:]
        return self

    def _copy(self, w, k, block, to, src=None):
        dst = self.outs[w].at[4 * block[0] + 2 * block[1] + block[2]]
        return pltpu.make_async_remote_copy(
            src_ref=dst if src is None else src, dst_ref=dst,
            send_sem=self.send_sems.at[7 * w + k], recv_sem=self.recv_sems.at[7 * w + k],
            device_id=to, device_id_type=MESH_ID)

    def _first(self):
        x, y, c = _place()
        me, sibling = (x, y, c), (x, y, 1 - c)
        chips = [(1 - x, y), (x, 1 - y), (1 - x, 1 - y)]
        mine, first = [], []
        for w in range(self.n):
            mine.append(pltpu.make_async_copy(self.stages[w], self.outs[w].at[4 * x + 2 * y + c], self.local_sems.at[w]))
            first.append(self._copy(w, 0, me, sibling, src=self.stages[w]))
            first += [self._copy(w, 1 + j, me, (*chip, c), src=self.stages[w]) for j, chip in enumerate(chips)]
        return mine, first

    def _passed(self):
        x, y, c = _place()
        chips = [(1 - x, y), (x, 1 - y), (1 - x, 1 - y)]
        return [self._copy(w, 4 + j, (*chip, c), (x, y, 1 - c)) for w in range(self.n) for j, chip in enumerate(chips)]

    def issue(self):
        a = 0
        for w in range(self.n):
            shape, parts = self.groups[w]
            if sum(arr.size for arr, _, _ in parts) < math.prod(shape):
                self.stages[w][...] = jnp.zeros(shape, self.dtypes[w])
            for _, dst, src in parts:
                self.stages[w][dst] = self.ins[a][src].astype(self.dtypes[w])
                a += 1
        mine, first = self._first()
        for cp in mine + first:
            cp.start()

    def forward(self):
        x, y, c = _place()
        chips = [(1 - x, y), (x, 1 - y), (1 - x, 1 - y)]
        passed = self._passed()
        for w in range(self.n):
            for j, chip in enumerate(chips):
                self._copy(w, 1 + j, (*chip, c), (x, y, c)).wait_recv()
                passed[3 * w + j].start()

    def finish(self):
        x, y, c = _place()
        chips = [(1 - x, y), (x, 1 - y), (1 - x, 1 - y)]
        for w in range(self.n):
            self._copy(w, 0, (x, y, 1 - c), (x, y, c)).wait_recv()
            for j, chip in enumerate(chips):
                self._copy(w, 4 + j, (*chip, 1 - c), (x, y, c)).wait_recv()
        mine, first = self._first()
        for cp in first + self._passed():
            cp.wait_send()
        for cp in mine:
            cp.wait()


def _all_gather(groups, dtypes):
    ag = _Gather(groups, dtypes)
    na, n = len(ag.arrays), ag.n

    def body(*refs):
        ag.bind(refs[:na], refs[na:na + n], refs[na + n:])
        ag.issue()
        ag.forward()
        ag.finish()

    return pl.pallas_call(
        body, name="ag_weights", out_shape=ag.out_shape,
        in_specs=[pl.BlockSpec(memory_space=pltpu.VMEM)] * na,
        out_specs=[pl.BlockSpec(memory_space=pl.ANY)] * n,
        scratch_shapes=ag.scratch,
        compiler_params=pltpu.CompilerParams(vmem_limit_bytes=40 * MIB),
    )(*ag.arrays)


class _ChipExchange:
    def __init__(self, qs):
        self.n = len(qs)
        self.out_shape = [jax.ShapeDtypeStruct(q.shape, q.dtype) for q in qs]
        self.scratch = [pltpu.SemaphoreType.DMA((3 * self.n,)), pltpu.SemaphoreType.DMA((3 * self.n,))]

    def bind(self, qs, rels, scratch):
        self.qs, self.rels = qs, rels
        self.send_sems, self.recv_sems = scratch
        return self

    def _copies(self):
        x, y, c = _place()
        chips = [(1 - x, y), (x, 1 - y), (1 - x, 1 - y)]
        return [pltpu.make_async_remote_copy(
            src_ref=self.qs[w].at[j], dst_ref=self.rels[w].at[j],
            send_sem=self.send_sems.at[3 * w + j], recv_sem=self.recv_sems.at[3 * w + j],
            device_id=(*chips[j], c), device_id_type=MESH_ID) for w in range(self.n) for j in range(3)]

    def issue(self):
        for cp in self._copies():
            cp.start()

    def finish(self):
        cps = self._copies()
        for cp in cps:
            cp.wait_recv()
        for cp in cps:
            cp.wait_send()


def _reduce_scatter(parts, small):
    n = len(parts)
    blks = [p.shape[1:] for p in parts]

    def body(*refs):
        ps, small_ref = refs[:n], refs[n]
        o = n + 1
        owns, sibs, rels, small_out = refs[o:o + n], refs[o + n:o + 2 * n], refs[o + 2 * n:o + 3 * n], refs[o + 3 * n]
        o += 3 * n + 1
        pa, pb, qst = refs[o:o + n], refs[o + n:o + 2 * n], refs[o + 2 * n:o + 3 * n]
        s1_send, s1_recv, s2_send, s2_recv, sm_send, sm_recv, lsem = refs[o + 3 * n:]
        x, y, c = _place()
        me = 4 * x + 2 * y + c
        sibling = (x, y, 1 - c)
        chips = [(1 - x, y), (x, 1 - y), (1 - x, 1 - y)]
        all_chips = [(x, y)] + chips

        own_cps = []
        for w in range(n):
            cp = pltpu.make_async_copy(ps[w].at[me], owns[w], lsem.at[w])
            cp.start()
            own_cps.append(cp)
        sm_own = pltpu.make_async_copy(small_ref, small_out.at[me], lsem.at[n])
        sm_own.start()

        def small_copy(r):
            peer = ((x + (r >> 2)) % 2, (y + ((r >> 1) & 1)) % 2, (c + (r & 1)) % 2)
            return pltpu.make_async_remote_copy(
                src_ref=small_ref, dst_ref=small_out.at[me], send_sem=sm_send.at[r - 1], recv_sem=sm_recv.at[r - 1],
                device_id=peer, device_id_type=MESH_ID)

        sm_cps = [small_copy(r) for r in range(1, NDEV)]
        for cp in sm_cps:
            cp.start()

        def pair_copy(w, rel):
            cx, cy = all_chips[rel]
            return pltpu.make_async_remote_copy(
                src_ref=ps[w].at[4 * cx + 2 * cy + (1 - c)], dst_ref=sibs[w].at[rel],
                send_sem=s1_send.at[4 * w + rel], recv_sem=s1_recv.at[4 * w + rel],
                device_id=sibling, device_id_type=MESH_ID)

        def chip_copy(w, j):
            return pltpu.make_async_remote_copy(
                src_ref=qst[w].at[j], dst_ref=rels[w].at[j],
                send_sem=s2_send.at[3 * w + j], recv_sem=s2_recv.at[3 * w + j],
                device_id=(*chips[j], c), device_id_type=MESH_ID)

        pair_cps = [pair_copy(w, rel) for w in range(n) for rel in (1, 2, 3, 0)]
        for cp in pair_cps:
            cp.start()
        chip_cps = []
        for w in range(n):
            for j, (cx, cy) in enumerate(chips):
                pair_copy(w, 1 + j).wait_recv()
                la = pltpu.make_async_copy(ps[w].at[4 * cx + 2 * cy + c], pa[w], lsem.at[n + 1])
                lb = pltpu.make_async_copy(sibs[w].at[1 + j], pb[w], lsem.at[n + 2])
                la.start()
                lb.start()
                la.wait()
                lb.wait()
                qst[w][j] = (pa[w][...].astype(F32) + pb[w][...].astype(F32)).astype(BF16)
                cp = chip_copy(w, j)
                cp.start()
                chip_cps.append(cp)
        for w in range(n):
            pair_copy(w, 0).wait_recv()
            for j in range(3):
                chip_copy(w, j).wait_recv()
        for cp in sm_cps:
            cp.wait_recv()
        for cp in pair_cps + chip_cps + sm_cps:
            cp.wait_send()
        for cp in own_cps:
            cp.wait()
        sm_own.wait()

    any_spec = pl.BlockSpec(memory_space=pl.ANY)
    outs = pl.pallas_call(
        body, name="rs_grads",
        out_shape=[jax.ShapeDtypeStruct(b, BF16) for b in blks]
        + [jax.ShapeDtypeStruct((4, *b), BF16) for b in blks]
        + [jax.ShapeDtypeStruct((3, *b), BF16) for b in blks]
        + [jax.ShapeDtypeStruct((NDEV, *small.shape), F32)],
        in_specs=[any_spec] * (n + 1),
        out_specs=[any_spec] * (3 * n + 1),
        scratch_shapes=[pltpu.VMEM(b, BF16) for b in blks] + [pltpu.VMEM(b, BF16) for b in blks]
        + [pltpu.VMEM((3, *b), BF16) for b in blks]
        + [pltpu.SemaphoreType.DMA((4 * n,)), pltpu.SemaphoreType.DMA((4 * n,)),
           pltpu.SemaphoreType.DMA((3 * n,)), pltpu.SemaphoreType.DMA((3 * n,)),
           pltpu.SemaphoreType.DMA((NDEV - 1,)), pltpu.SemaphoreType.DMA((NDEV - 1,)),
           pltpu.SemaphoreType.DMA((n + 3,))],
        compiler_params=pltpu.CompilerParams(vmem_limit_bytes=40 * MIB),
    )(*parts, small)
    return outs[:n], outs[n:2 * n], outs[2 * n:3 * n], outs[3 * n]


def _rs_pair(name, parts):
    n = len(parts)
    blks = [p.shape[1:] for p in parts]

    def body(*refs):
        ps = refs[:n]
        owns, sibs, qs = refs[n:2 * n], refs[2 * n:3 * n], refs[3 * n:4 * n]
        pa, pb, qst = refs[4 * n:5 * n], refs[5 * n:6 * n], refs[6 * n:7 * n]
        s_send, s_recv, lsem = refs[7 * n:]
        x, y, c = _place()
        chips = [(1 - x, y), (x, 1 - y), (1 - x, 1 - y)]
        all_chips = [(x, y)] + chips

        own_cps = [pltpu.make_async_copy(ps[w].at[4 * x + 2 * y + c], owns[w], lsem.at[w]) for w in range(n)]
        for cp in own_cps:
            cp.start()

        def pair_copy(w, rel):
            cx, cy = all_chips[rel]
            return pltpu.make_async_remote_copy(
                src_ref=ps[w].at[4 * cx + 2 * cy + (1 - c)], dst_ref=sibs[w].at[rel],
                send_sem=s_send.at[4 * w + rel], recv_sem=s_recv.at[4 * w + rel],
                device_id=(x, y, 1 - c), device_id_type=MESH_ID)

        pair_cps = [pair_copy(w, rel) for w in range(n) for rel in (1, 2, 3, 0)]
        for cp in pair_cps:
            cp.start()
        q_cps = []
        for w in range(n):
            for j, (cx, cy) in enumerate(chips):
                la = pltpu.make_async_copy(ps[w].at[4 * cx + 2 * cy + c], pa[w], lsem.at[n])
                lb = pltpu.make_async_copy(sibs[w].at[1 + j], pb[w], lsem.at[n + 1])
                la.start()
                pair_copy(w, 1 + j).wait_recv()
                lb.start()
                la.wait()
                lb.wait()
                qst[w][j] = (pa[w][...].astype(F32) + pb[w][...].astype(F32)).astype(BF16)
            cp = pltpu.make_async_copy(qst[w], qs[w], lsem.at[n + 2 + w])
            cp.start()
            q_cps.append(cp)
        for w in range(n):
            pair_copy(w, 0).wait_recv()
        for cp in pair_cps:
            cp.wait_send()
        for cp in own_cps + q_cps:
            cp.wait()

    any_spec = pl.BlockSpec(memory_space=pl.ANY)
    outs = pl.pallas_call(
        body, name=name,
        out_shape=[jax.ShapeDtypeStruct(b, BF16) for b in blks]
        + [jax.ShapeDtypeStruct((4, *b), BF16) for b in blks]
        + [jax.ShapeDtypeStruct((3, *b), BF16) for b in blks],
        in_specs=[any_spec] * n,
        out_specs=[any_spec] * (3 * n),
        scratch_shapes=[pltpu.VMEM(b, BF16) for b in blks] + [pltpu.VMEM(b, BF16) for b in blks]
        + [pltpu.VMEM((3, *b), BF16) for b in blks]
        + [pltpu.SemaphoreType.DMA((4 * n,)), pltpu.SemaphoreType.DMA((4 * n,)), pltpu.SemaphoreType.DMA((2 * n + 2,))],
        compiler_params=pltpu.CompilerParams(vmem_limit_bytes=40 * MIB),
    )(*parts)
    return outs[:n], outs[n:2 * n], outs[2 * n:3 * n]


def _adamw_math(g, w, m, v):
    m = ADAM_B1 * m + (1.0 - ADAM_B1) * g
    v = ADAM_B2 * v + (1.0 - ADAM_B2) * (g * g)
    m_hat = m / (1.0 - ADAM_B1 ** ADAM_STEP)
    v_hat = v / (1.0 - ADAM_B2 ** ADAM_STEP)
    delta = -ADAM_LR * (m_hat / (jnp.sqrt(v_hat) + ADAM_EPS) + ADAM_WD * w)
    return delta, m, v


def _adamw_multi(name, own, sib, rel, ws, ms, vs, row_grid):
    k_n, r_n, c_n = own.shape
    rbk = r_n // row_grid

    def body(*refs):
        own_ref, sib_ref, r0_ref, r1_ref, r2_ref = refs[:5]
        w_refs, m_refs, v_refs = refs[5:5 + k_n], refs[5 + k_n:5 + 2 * k_n], refs[5 + 2 * k_n:5 + 3 * k_n]
        outs = refs[5 + 3 * k_n:]
        for k in range(k_n):
            g = own_ref[k].astype(F32) + sib_ref[k].astype(F32)
            g = g + r0_ref[k].astype(F32)
            g = g + r1_ref[k].astype(F32)
            g = g + r2_ref[k].astype(F32)
            delta, mm, vv = _adamw_math(g, w_refs[k][0], m_refs[k][0], v_refs[k][0])
            outs[4 * k][0] = g
            outs[4 * k + 1][0] = delta
            outs[4 * k + 2][0] = mm
            outs[4 * k + 3][0] = vv

    def lead(j):
        return pl.BlockSpec((None, k_n, rbk, c_n), lambda g: (j, 0, g, 0))

    wspec = pl.BlockSpec((1, rbk, c_n), lambda g: (0, g, 0))
    shp = jax.ShapeDtypeStruct((1, r_n, c_n), F32)
    res = pl.pallas_call(
        body, name=name, grid=(row_grid,),
        in_specs=[pl.BlockSpec((k_n, rbk, c_n), lambda g: (0, g, 0)), lead(0), lead(0), lead(1), lead(2)] + [wspec] * (3 * k_n),
        out_specs=[wspec] * (4 * k_n), out_shape=[shp] * (4 * k_n),
        compiler_params=_params(("arbitrary",), 40),
    )(own, sib, rel, rel, rel, *ws, *ms, *vs)
    return [tuple(res[4 * k:4 * k + 4]) for k in range(k_n)]


def _adamw_meta_dw(own, sib, rel, meta, dw):
    def body(own_ref, sib_ref, rel_ref, wm, mm, vm, wd, md, vd, *outs):
        def gsum(rows):
            g = own_ref[rows, :].astype(F32) + sib_ref[0, rows, :].astype(F32)
            for j in range(3):
                g = g + rel_ref[j, rows, :].astype(F32)
            return g

        g = gsum(pl.ds(0, N_META))
        delta, m2, v2 = _adamw_math(g, wm[...], mm[...], vm[...])
        for o, val in zip(outs[:4], (g, delta, m2, v2)):
            o[...] = val
        g = gsum(pl.ds(N_META, CONV_K))
        delta, m2, v2 = _adamw_math(g, wd[0], md[0], vd[0])
        for o, val in zip(outs[4:], (g, delta, m2, v2)):
            o[0] = val

    s_meta = jax.ShapeDtypeStruct(meta[0].shape, F32)
    s_dw = jax.ShapeDtypeStruct(dw[0].shape, F32)
    res = pl.pallas_call(body, name="adamw_meta_dw", out_shape=[s_meta] * 4 + [s_dw] * 4)(own, sib, rel, *meta, *dw)
    return tuple(res[:4]), tuple(res[4:])


REP_ROWS = 16


def _adamw_rep(gathered, ws, ms, vs):
    rows = [(0, 1), (1, 2), (3, 1), (4, 1), (5, 1), (6, 1), (7, 1), (8, 1)]

    def body(g_ref, *refs):
        w_refs, m_refs, v_refs = refs[:8], refs[8:16], refs[16:24]
        loss_ref, outs, acc = refs[24], refs[25:57], refs[57]
        g = g_ref[0]
        for d in range(1, NDEV):
            g = g + g_ref[d]
        acc[...] = g
        loss_ref[...] = (0.5 / D) * jnp.sum(acc[pl.ds(9, 1), :], axis=1, keepdims=True)
        for p, (r0, nr) in enumerate(rows):
            for h in range(nr):
                cols = pl.ds(h * D, D)
                gp = acc[pl.ds(r0 + h, 1), :]
                delta, mm, vv = _adamw_math(gp, w_refs[p][:, cols], m_refs[p][:, cols], v_refs[p][:, cols])
                for o, val in zip(outs[4 * p:4 * p + 4], (gp, delta, mm, vv)):
                    o[:, cols] = val

    shapes = [jax.ShapeDtypeStruct(w.shape, F32) for w in ws]
    res = pl.pallas_call(
        body, name="adamw_rep",
        out_shape=[jax.ShapeDtypeStruct((1, 1), F32)] + [s for s in shapes for _ in range(4)],
        scratch_shapes=[pltpu.VMEM((REP_ROWS, D), F32)],
    )(gathered, *ws, *ms, *vs)
    return res[0], [tuple(res[1 + 4 * p:5 + 4 * p]) for p in range(8)]


def _repack_gu(g_gu):
    def body(x_ref, o_ref):
        for i in range(2):
            for d in range(NDEV):
                o_ref[i, d // 4, :, pl.ds(FFB * (d % 4), FFB)] = x_ref[d, i]

    return pl.pallas_call(body, name="repack_gu", out_shape=jax.ShapeDtypeStruct((2, 2, D, FFC), BF16),
                          compiler_params=pltpu.CompilerParams(vmem_limit_bytes=40 * MIB))(g_gu)


def _whole(a):
    nd = a.ndim
    return pl.BlockSpec(a.shape, lambda *g: (0,) * nd)


def _fwd_in(x2, tail, g_mix, w_g, tp, ag):
    nt = tp // TM
    nx_last = TM - tail.shape[0]
    na, ng = len(ag.arrays), ag.n

    def body(*refs):
        x_ref, tail_ref, g_ref, w_hbm = refs[:4]
        h_ref, z_ref, u_ref = refs[4 + na:7 + na]
        w_vm, sems = refs[7 + na + ng:9 + na + ng]
        ag.bind(refs[4:4 + na], refs[7 + na:7 + na + ng], refs[9 + na + ng:])
        i = pl.program_id(0)

        @pl.when(i == 0)
        def _():
            ag.issue()

        @pl.when(i == max(nt - 3, 0))
        def _():
            ag.forward()

        _load_once(i == 0, [(w_hbm, w_vm)], sems)

        @pl.when(i < nt - 1)
        def _():
            h_ref[...] = x_ref[...]

        @pl.when(i == nt - 1)
        def _():
            h_ref[pl.ds(0, nx_last), :] = x_ref[pl.ds(0, nx_last), :]
            h_ref[pl.ds(nx_last, TM - nx_last), :] = tail_ref[...]

        xv = h_ref[...]
        r = lax.rsqrt(jnp.mean(xv * xv, axis=-1, keepdims=True) + RMS_EPS)
        u = (xv * r * g_ref[...]).astype(BF16)
        u_ref[...] = u
        for d in range(NDEV):
            z_ref[:, INB * d:INB * (d + 1)] = _dot(u, w_vm[d])

        @pl.when(i == nt - 1)
        def _():
            ag.finish()

    tile = pl.BlockSpec((TM, D), lambda i: (i, 0))
    res = pl.pallas_call(
        body, name="fwd_in", grid=(nt,),
        in_specs=[tile, pl.BlockSpec(tail.shape, lambda i: (0, 0)), pl.BlockSpec((1, D), lambda i: (0, 0)),
                  pl.BlockSpec(memory_space=pl.ANY)] + [_whole(a) for a in ag.arrays],
        out_specs=[tile, pl.BlockSpec((TM, DIN), lambda i: (i, 0)), tile] + [pl.BlockSpec(memory_space=pl.ANY)] * ng,
        out_shape=[jax.ShapeDtypeStruct((tp, D), F32), jax.ShapeDtypeStruct((tp, DIN), F32),
                   jax.ShapeDtypeStruct((tp, D), BF16)] + ag.out_shape,
        scratch_shapes=[pltpu.VMEM((NDEV, D, INB), BF16), pltpu.SemaphoreType.DMA((1,))] + ag.scratch,
        compiler_params=_params(("arbitrary",), 56),
    )(x2, tail, g_mix, w_g, *ag.arrays)
    return res[:3], res[3:]


def _halo_specs(col, nt, width=D):
    r = TM // HALO
    nb = nt * r
    return [pl.BlockSpec((HALO, width), lambda i: ((i * r + nb - 1) % nb, col)),
            pl.BlockSpec((TM, width), lambda i: (i, col)),
            pl.BlockSpec((HALO, width), lambda i: (((i + 1) * r) % nb, col))]


def _fill_ext(ext_ref, left, cur, right):
    ext_ref[pl.ds(0, HALO), :] = left
    ext_ref[pl.ds(HALO, TM), :] = cur
    ext_ref[pl.ds(HALO + TM, HALO), :] = right


def _shift_copies(ext_ref, sh_ref, c0):
    for r in range(8):
        sh_ref[r] = ext_ref[pl.ds(r, TM + 24), pl.ds(c0, CW)]


def _tap_rows(w_ref, w8):
    for k in range(CONV_K):
        w8[k] = jnp.broadcast_to(w_ref[pl.ds(k, 1), :], (8, D))


def _pool_cnt(i, seq, tp, left, right, rows, row0):
    b = i * TM + row0 + lax.broadcasted_iota(jnp.int32, (rows, 1), 0)
    b = jnp.where(b < 0, b + tp, b)
    b = jnp.where(b >= tp, b - tp, b)
    t = jnp.where(b < seq, b + N_META, b - (tp - N_META))
    lo = jnp.maximum(t - left, 0)
    hi = jnp.minimum(t + right + 1, seq + N_META)
    return jnp.maximum(hi - lo, 1).astype(F32)


def _seq_fwd(z, w_dw, b_dw, seq, gat):
    tp = z.shape[0]
    nt = tp // TM
    na, ng = len(gat.arrays), gat.n

    def body(*refs):
        av_l, av, av_r, ag_l, ag, ag_r, p_l, p, p_r, w_ref, b_ref = refs[:11]
        ac_ref, m_ref = refs[11 + na:13 + na]
        a_ext, p_ext, sh, w8 = refs[13 + na + ng:17 + na + ng]
        gat.bind(refs[11:11 + na], refs[13 + na:13 + na + ng], refs[17 + na + ng:])
        i = pl.program_id(0)

        @pl.when(i == 0)
        def _():
            gat.issue()
            _tap_rows(w_ref, w8)

        @pl.when(i == max(nt - 2, 0))
        def _():
            gat.forward()

        _fill_ext(a_ext, av_l[...] * _sig(ag_l[...]), av[...] * _sig(ag[...]), av_r[...] * _sig(ag_r[...]))
        _fill_ext(p_ext, p_l[...], p[...], p_r[...])
        for c0 in range(0, D, CW):
            _shift_copies(a_ext, sh, c0)

            def rows(j, carry):
                base = pl.multiple_of(j * RB, RB)
                acc = jnp.broadcast_to(b_ref[:, pl.ds(c0, CW)], (RB // 8, 8, CW))
                for k in range(CONV_K):
                    q, r = divmod(k + 1, 8)
                    slab = sh[r, pl.ds(pl.multiple_of(base + 8 * q, 8), RB), :].reshape(RB // 8, 8, CW)
                    acc = acc + slab * w8[k, :, pl.ds(c0, CW)]
                ac_ref[pl.ds(base, RB), pl.ds(c0, CW)] = acc.reshape(RB, CW)
                return carry

            lax.fori_loop(0, TM // RB, rows, 0)
        for g, win in enumerate(POOL_WINDOWS):
            left = win // 2
            right = win - 1 - left
            cols = pl.ds(g * PG, PG)
            s = p_ext[pl.ds(HALO - left, TM), cols]
            for off in range(-left + 1, right + 1):
                s = s + p_ext[pl.ds(HALO + off, TM), cols]
            cnt = _pool_cnt(i, seq, tp, left, right, TM, 0)
            m_ref[:, cols] = (s / cnt - p_ext[pl.ds(HALO, TM), cols]).astype(BF16)

        @pl.when(i == nt - 1)
        def _():
            gat.finish()

    res = pl.pallas_call(
        body, name="seq_fwd", grid=(nt,),
        in_specs=_halo_specs(0, nt) + _halo_specs(1, nt) + _halo_specs(2, nt)
        + [pl.BlockSpec((32, D), lambda i: (0, 0)), pl.BlockSpec((1, D), lambda i: (0, 0))] + [_whole(a) for a in gat.arrays],
        out_specs=[pl.BlockSpec((TM, D), lambda i: (i, 0))] * 2 + [pl.BlockSpec(memory_space=pl.ANY)] * ng,
        out_shape=[jax.ShapeDtypeStruct((tp, D), F32), jax.ShapeDtypeStruct((tp, D), BF16)] + gat.out_shape,
        scratch_shapes=[pltpu.VMEM((TM + 2 * HALO, D), F32), pltpu.VMEM((TM + 2 * HALO, D), F32),
                        pltpu.VMEM((8, TM + 24, CW), F32), pltpu.VMEM((CONV_K, 8, D), F32)] + gat.scratch,
        compiler_params=_params(("arbitrary",), 52),
    )(z, z, z, z, z, z, z, z, z, w_dw, b_dw, *gat.arrays)
    return res[:2], res[2:]


def _ln_stats(ac):
    mu = jnp.mean(ac, axis=-1, keepdims=True)
    xc = ac - mu
    rl = lax.rsqrt(jnp.mean(xc * xc, axis=-1, keepdims=True) + LN_EPS)
    return xc * rl, rl


def _pool_mix(m, wp_ref):
    return jnp.concatenate(
        [_dot(m[:, g * PG:(g + 1) * PG], wp_ref[:, g].reshape(PG, PG)) for g in range(4)], axis=1)


def _mix_fwd(ac, m, z, h0, b_gate, ln_g, ln_b, pool_scale, g_mixw, g_pool):
    tp = h0.shape[0]
    nt = tp // TMS

    def body(ac_ref, m_ref, zga, zgb, h_ref, bg_ref, lg_ref, lb_ref, ps_ref, wm_hbm, wp_hbm,
             h1_ref, s_ref, yc_ref, yp_ref, mg_ref, q_ref, wm, wp, sems):
        _load_once(pl.program_id(0) == 0, [(wm_hbm, wm), (wp_hbm, wp)], sems)
        n, _ = _ln_stats(ac_ref[...])
        l = n * lg_ref[...] + lb_ref[...]
        s = (l * _sig(l)).astype(BF16)
        s_ref[...] = s
        yc = _dot(s, wm[:, 0].reshape(D, D))
        q = (_pool_mix(m_ref[...], wp) * ps_ref[...]).astype(BF16)
        q_ref[...] = q
        yp = _dot(q, wm[:, 1].reshape(D, D))
        ga = _sig(zga[...] + bg_ref[:, :D])
        gb = _sig(zgb[...] + bg_ref[:, D:])
        merged = (ga * yc + gb * yp).astype(BF16)
        yc_ref[...] = yc
        yp_ref[...] = yp
        mg_ref[...] = merged
        h1_ref[...] = h_ref[...] + _dot(merged, wm[:, 2].reshape(D, D))

    def tile(col=0):
        return pl.BlockSpec((TMS, D), lambda i: (i, col))

    def vec(w):
        return pl.BlockSpec((1, w), lambda i: (0, 0))

    anys = pl.BlockSpec(memory_space=pl.ANY)
    f32o, b16o = jax.ShapeDtypeStruct((tp, D), F32), jax.ShapeDtypeStruct((tp, D), BF16)
    return pl.pallas_call(
        body, name="mix_fwd", grid=(nt,),
        in_specs=[tile(), tile(), tile(3), tile(4), tile(), vec(2 * D), vec(D), vec(D), vec(D), anys, anys],
        out_specs=[tile()] * 6,
        out_shape=[f32o, b16o, f32o, f32o, b16o, b16o],
        scratch_shapes=[pltpu.VMEM((NDEV, 3, D // NDEV, D), BF16), pltpu.VMEM((NDEV, 4, PG // NDEV, PG), BF16),
                        pltpu.SemaphoreType.DMA((2,))],
        compiler_params=_params(("arbitrary",), 48),
    )(ac, m, z, z, h0, b_gate, ln_g, ln_b, pool_scale, g_mixw, g_pool)


def _ffn_fwd(h1, tgt, g_ffn, g_final, w_gu, w_dn):
    tp = h1.shape[0]
    nt = tp // TM
    nx_last = tgt.shape[0] - (nt - 1) * TM

    def body(h_ref, t_ref, gf_ref, gl_ref, wgu_hbm, wdn_hbm,
             fg_ref, fu_ref, v_ref, f_ref, dh2_ref, acc_ref, wgu, wdn, v_sc, h2_sc, diff_sc, sems):
        i, j = pl.program_id(0), pl.program_id(1)
        _load_once((i == 0) & (j == 0), [(wgu_hbm, wgu), (wdn_hbm, wdn)], sems)

        @pl.when((i == 0) & (j == 0))
        def _():
            acc_ref[...] = jnp.zeros_like(acc_ref)

        @pl.when(j == 0)
        def _():
            h = h_ref[...]
            r = lax.rsqrt(jnp.mean(h * h, axis=-1, keepdims=True) + RMS_EPS)
            v = (h * r * gf_ref[...]).astype(BF16)
            v_sc[...] = v
            v_ref[...] = v
            h2_sc[...] = h

        v = v_sc[...]
        fg = _dot(v, wgu[0, j])
        fu = _dot(v, wgu[1, j])
        fg_ref[...] = fg
        fu_ref[...] = fu
        f = ((fg * _sig(fg)) * fu).astype(BF16)
        f_ref[...] = f
        h2_sc[...] += _dot(f, wdn[j])

        @pl.when(j == 1)
        def _():
            h2 = h2_sc[...]
            r = lax.rsqrt(jnp.mean(h2 * h2, axis=-1, keepdims=True) + RMS_EPS)
            n2 = h2 * r
            y = n2 * gl_ref[...]

            @pl.when(i < nt - 1)
            def _():
                diff_sc[...] = y - t_ref[...]

            @pl.when(i == nt - 1)
            def _():
                diff_sc[pl.ds(0, nx_last), :] = y[:nx_last] - t_ref[pl.ds(0, nx_last), :]
                diff_sc[pl.ds(nx_last, TM - nx_last), :] = jnp.zeros((TM - nx_last, D), F32)

            diff = diff_sc[...]
            dy = diff * (1.0 / D)
            acc_ref[0:1, :] += jnp.sum(diff * diff, axis=0, keepdims=True)
            acc_ref[1:2, :] += jnp.sum(dy * n2, axis=0, keepdims=True)
            dn = dy * gl_ref[...]
            dh2_ref[...] = r * (dn - n2 * jnp.mean(dn * n2, axis=-1, keepdims=True))

    def tile():
        return pl.BlockSpec((TM, D), lambda i, j: (i, 0))

    def chunk():
        return pl.BlockSpec((TM, FFC), lambda i, j: (i, j))

    def vec():
        return pl.BlockSpec((1, D), lambda i, j: (0, 0))

    anys = pl.BlockSpec(memory_space=pl.ANY)
    hid32, hid16 = jax.ShapeDtypeStruct((tp, DFF), F32), jax.ShapeDtypeStruct((tp, DFF), BF16)
    return pl.pallas_call(
        body, name="ffn_fwd", grid=(nt, 2),
        in_specs=[tile(), tile(), vec(), vec(), anys, anys],
        out_specs=[chunk(), chunk(), tile(), chunk(), tile(), pl.BlockSpec((8, D), lambda i, j: (0, 0))],
        out_shape=[hid32, hid32, jax.ShapeDtypeStruct((tp, D), BF16), hid16, jax.ShapeDtypeStruct((tp, D), F32),
                   jax.ShapeDtypeStruct((8, D), F32)],
        scratch_shapes=[pltpu.VMEM((2, 2, D, FFC), BF16), pltpu.VMEM((2, FFC, D), BF16),
                        pltpu.VMEM((TM, D), BF16), pltpu.VMEM((TM, D), F32), pltpu.VMEM((TM, D), F32),
                        pltpu.SemaphoreType.DMA((2,))],
        compiler_params=_params(("arbitrary", "arbitrary"), 56),
    )(h1, tgt, g_ffn, g_final, w_gu, w_dn)


def _ffn_bwd(dh2, fg, fu, h1, g_ffn, w_gu, w_dn):
    tp = h1.shape[0]
    nt = tp // TM

    def body(dh2_ref, fg_ref, fu_ref, h_ref, gf_ref, wgu_hbm, wdn_hbm,
             dfg_ref, dfu_ref, dh1_ref, acc_ref, wgu, wdn, d_sc, dv_sc, sems):
        i, j = pl.program_id(0), pl.program_id(1)
        _load_once((i == 0) & (j == 0), [(wgu_hbm, wgu), (wdn_hbm, wdn)], sems)

        @pl.when((i == 0) & (j == 0))
        def _():
            acc_ref[...] = jnp.zeros_like(acc_ref)

        @pl.when(j == 0)
        def _():
            d_sc[...] = dh2_ref[...].astype(BF16)
            dv_sc[...] = jnp.zeros_like(dv_sc)

        df = _dot_nt(d_sc[...], wdn[j])
        fg = fg_ref[...]
        sg = _sig(fg)
        dfu = (df * (fg * sg)).astype(BF16)
        dfg = (df * fu_ref[...] * (sg * (1.0 + fg * (1.0 - sg)))).astype(BF16)
        dfg_ref[...] = dfg
        dfu_ref[...] = dfu
        dv_sc[...] += _dot_nt(dfg, wgu[0, j]) + _dot_nt(dfu, wgu[1, j])

        @pl.when(j == 1)
        def _():
            h = h_ref[...]
            r = lax.rsqrt(jnp.mean(h * h, axis=-1, keepdims=True) + RMS_EPS)
            n1 = h * r
            dv = dv_sc[...]
            acc_ref[0:1, :] += jnp.sum(dv * n1, axis=0, keepdims=True)
            dn = dv * gf_ref[...]
            dh1_ref[...] = dh2_ref[...] + r * (dn - n1 * jnp.mean(dn * n1, axis=-1, keepdims=True))

    def tile():
        return pl.BlockSpec((TM, D), lambda i, j: (i, 0))

    def chunk():
        return pl.BlockSpec((TM, FFC), lambda i, j: (i, j))

    anys = pl.BlockSpec(memory_space=pl.ANY)
    hid16 = jax.ShapeDtypeStruct((tp, DFF), BF16)
    return pl.pallas_call(
        body, name="ffn_bwd", grid=(nt, 2),
        in_specs=[tile(), chunk(), chunk(), tile(), pl.BlockSpec((1, D), lambda i, j: (0, 0)), anys, anys],
        out_specs=[chunk(), chunk(), tile(), pl.BlockSpec((8, D), lambda i, j: (0, 0))],
        out_shape=[hid16, hid16, jax.ShapeDtypeStruct((tp, D), F32), jax.ShapeDtypeStruct((8, D), F32)],
        scratch_shapes=[pltpu.VMEM((2, 2, D, FFC), BF16), pltpu.VMEM((2, FFC, D), BF16),
                        pltpu.VMEM((TM, D), BF16), pltpu.VMEM((TM, D), F32), pltpu.SemaphoreType.DMA((2,))],
        compiler_params=_params(("arbitrary", "arbitrary"), 56),
    )(dh2, fg, fu, h1, g_ffn, w_gu, w_dn)


def _mix_bwd(dh1, z, yc, yp, ac, m, b_gate, ln_g, ln_b, pool_scale, g_mixw, g_pool, qs):
    tp = dh1.shape[0]
    nt = tp // TMS
    ex = _ChipExchange(qs)
    nq = ex.n

    def body(*refs):
        dh1_ref, zga, zgb, yc_ref, yp_ref, ac_ref, m_ref, bg_ref, lg_ref, lb_ref, ps_ref, wm_hbm, wp_hbm = refs[:13]
        dac_ref, dm_ref, dzg_ref, dyc_ref, dyp_ref, dm2_ref, acc_ref = refs[13 + nq:20 + nq]
        wm, wp, sems = refs[20 + 2 * nq:23 + 2 * nq]
        ex.bind(refs[13:13 + nq], refs[20 + nq:20 + 2 * nq], refs[23 + 2 * nq:])
        first = pl.program_id(0) == 0

        @pl.when(first)
        def _():
            ex.issue()
            acc_ref[...] = jnp.zeros_like(acc_ref)

        _load_once(first, [(wm_hbm, wm), (wp_hbm, wp)], sems)

        dmerged = _dot_nt(dh1_ref[...].astype(BF16), wm[:, 2].reshape(D, D))
        ga = _sig(zga[...] + bg_ref[:, :D])
        gb = _sig(zgb[...] + bg_ref[:, D:])
        dyc = dmerged * ga
        dyp = dmerged * gb
        dza = (dmerged * yc_ref[...]) * (ga * (1.0 - ga))
        dzb = (dmerged * yp_ref[...]) * (gb * (1.0 - gb))
        dzg_ref[:, :D] = dza.astype(BF16)
        dzg_ref[:, D:] = dzb.astype(BF16)
        acc_ref[0:1, :D] += jnp.sum(dza, axis=0, keepdims=True)
        acc_ref[0:1, D:] += jnp.sum(dzb, axis=0, keepdims=True)
        dyc_b = dyc.astype(BF16)
        dyp_b = dyp.astype(BF16)
        dyc_ref[...] = dyc_b
        dyp_ref[...] = dyp_b
        ds = _dot_nt(dyc_b, wm[:, 0].reshape(D, D))
        n, rl = _ln_stats(ac_ref[...])
        l = n * lg_ref[...] + lb_ref[...]
        sg = _sig(l)
        dl = ds * (sg * (1.0 + l * (1.0 - sg)))
        acc_ref[1:2, :D] += jnp.sum(dl * n, axis=0, keepdims=True)
        acc_ref[1:2, D:] += jnp.sum(dl, axis=0, keepdims=True)
        dn = dl * lg_ref[...]
        dac_ref[...] = rl * (dn - jnp.mean(dn, axis=-1, keepdims=True) - n * jnp.mean(dn * n, axis=-1, keepdims=True))
        dq = _dot_nt(dyp_b, wm[:, 1].reshape(D, D))
        mv = m_ref[...]
        acc_ref[2:3, :D] += jnp.sum(dq * _pool_mix(mv, wp), axis=0, keepdims=True)
        dm2 = (dq * ps_ref[...]).astype(BF16)
        dm2_ref[...] = dm2
        dm_ref[...] = jnp.concatenate(
            [_dot_nt(dm2[:, g * PG:(g + 1) * PG], wp[:, g].reshape(PG, PG)) for g in range(4)], axis=1)

        @pl.when(pl.program_id(0) == nt - 1)
        def _():
            ex.finish()

    def tile(col=0):
        return pl.BlockSpec((TMS, D), lambda i: (i, col))

    def vec(w):
        return pl.BlockSpec((1, w), lambda i: (0, 0))

    anys = pl.BlockSpec(memory_space=pl.ANY)
    f32o, b16o = jax.ShapeDtypeStruct((tp, D), F32), jax.ShapeDtypeStruct((tp, D), BF16)
    res = pl.pallas_call(
        body, name="mix_bwd", grid=(nt,),
        in_specs=[tile(), tile(3), tile(4), tile(), tile(), tile(), tile(), vec(2 * D), vec(D), vec(D), vec(D), anys, anys]
        + [anys] * nq,
        out_specs=[tile(), tile(), pl.BlockSpec((TMS, 2 * D), lambda i: (i, 0)), tile(), tile(), tile(),
                   pl.BlockSpec((8, 2 * D), lambda i: (0, 0))] + [anys] * nq,
        out_shape=[f32o, f32o, jax.ShapeDtypeStruct((tp, 2 * D), BF16), b16o, b16o, b16o,
                   jax.ShapeDtypeStruct((8, 2 * D), F32)] + ex.out_shape,
        scratch_shapes=[pltpu.VMEM((NDEV, 3, D // NDEV, D), BF16), pltpu.VMEM((NDEV, 4, PG // NDEV, PG), BF16),
                        pltpu.SemaphoreType.DMA((2,))] + ex.scratch,
        compiler_params=_params(("arbitrary",), 48),
    )(dh1, z, z, yc, yp, ac, m, b_gate, ln_g, ln_b, pool_scale, g_mixw, g_pool, *qs)
    return res[:7], res[7:]


def _seq_bwd(dac, dm, dzg, z, w_dw, seq, qs):
    tp = z.shape[0]
    nt = tp // TM
    ex = _ChipExchange(qs)
    nq = ex.n

    def body(*refs):
        dac_l, dac_c, dac_r, dm_l, dm_c, dm_r, av_l, av, av_r, ag_l, ag, ag_r, dzg_ref, w_ref = refs[:14]
        dz_ref, acc_ref = refs[14 + nq:16 + nq]
        a_ext, d_ext, m_ext, sha, shd, da_sc, dw_sc, w8 = refs[16 + 2 * nq:24 + 2 * nq]
        ex.bind(refs[14:14 + nq], refs[16 + nq:16 + 2 * nq], refs[24 + 2 * nq:])
        i = pl.program_id(0)

        @pl.when(i == 0)
        def _():
            ex.issue()
            dw_sc[...] = jnp.zeros_like(dw_sc)
            acc_ref[...] = jnp.zeros_like(acc_ref)
            _tap_rows(w_ref, w8)

        sg = _sig(ag[...])
        _fill_ext(a_ext, av_l[...] * _sig(ag_l[...]), av[...] * sg, av_r[...] * _sig(ag_r[...]))
        _fill_ext(d_ext, dac_l[...], dac_c[...], dac_r[...])
        for g, win in enumerate(POOL_WINDOWS):
            left = win // 2
            right = win - 1 - left
            cols = pl.ds(g * PG, PG)
            m_ext[pl.ds(0, HALO), cols] = dm_l[:, cols] / _pool_cnt(i, seq, tp, left, right, HALO, -HALO)
            m_ext[pl.ds(HALO, TM), cols] = dm_c[:, cols] / _pool_cnt(i, seq, tp, left, right, TM, 0)
            m_ext[pl.ds(HALO + TM, HALO), cols] = dm_r[:, cols] / _pool_cnt(i, seq, tp, left, right, HALO, TM)
        for c0 in range(0, D, CW):
            _shift_copies(a_ext, sha, c0)
            _shift_copies(d_ext, shd, c0)

            def rows(j, carry):
                base = pl.multiple_of(j * RB, RB)
                dcur = d_ext[pl.ds(pl.multiple_of(base + HALO, 8), RB), pl.ds(c0, CW)].reshape(RB // 8, 8, CW)
                acc = jnp.zeros((RB // 8, 8, CW), F32)
                for k in range(CONV_K):
                    q, r = divmod(CONV_K - k, 8)
                    slab = shd[r, pl.ds(pl.multiple_of(base + 8 * q, 8), RB), :].reshape(RB // 8, 8, CW)
                    acc = acc + slab * w8[k, :, pl.ds(c0, CW)]
                    q, r = divmod(k + 1, 8)
                    slab = sha[r, pl.ds(pl.multiple_of(base + 8 * q, 8), RB), :].reshape(RB // 8, 8, CW)
                    dw_sc[k, :, pl.ds(c0, CW)] += jnp.sum(dcur * slab, axis=0)
                da_sc[pl.ds(base, RB), pl.ds(c0, CW)] = acc.reshape(RB, CW)
                return carry

            lax.fori_loop(0, TM // RB, rows, 0)
        da = da_sc[...]
        dz_ref[:, 0:D] = (da * sg).astype(BF16)
        dz_ref[:, D:2 * D] = (da * av[...] * (sg * (1.0 - sg))).astype(BF16)
        for g, win in enumerate(POOL_WINDOWS):
            left = win // 2
            right = win - 1 - left
            cols = pl.ds(g * PG, PG)
            s = m_ext[pl.ds(HALO - right, TM), cols]
            for off in range(-right + 1, left + 1):
                s = s + m_ext[pl.ds(HALO + off, TM), cols]
            dz_ref[:, pl.ds(2 * D + g * PG, PG)] = (s - dm_c[:, cols]).astype(BF16)
        dz_ref[:, 3 * D:] = dzg_ref[...]

        @pl.when(i == nt - 1)
        def _():
            for k in range(CONV_K):
                acc_ref[k:k + 1, :] = jnp.sum(dw_sc[k], axis=0, keepdims=True)

        acc_ref[CONV_K:CONV_K + 1, :] += jnp.sum(dac_c[...], axis=0, keepdims=True)

        @pl.when(i == nt - 1)
        def _():
            ex.finish()

    ext = pltpu.VMEM((TM + 2 * HALO, D), F32)
    shs = pltpu.VMEM((8, TM + 24, CW), F32)
    anys = pl.BlockSpec(memory_space=pl.ANY)
    res = pl.pallas_call(
        body, name="seq_bwd", grid=(nt,),
        in_specs=_halo_specs(0, nt) + _halo_specs(0, nt) + _halo_specs(0, nt) + _halo_specs(1, nt)
        + [pl.BlockSpec((TM, 2 * D), lambda i: (i, 0)), pl.BlockSpec((32, D), lambda i: (0, 0))] + [anys] * nq,
        out_specs=[pl.BlockSpec((TM, DIN), lambda i: (i, 0)), pl.BlockSpec((32, D), lambda i: (0, 0))] + [anys] * nq,
        out_shape=[jax.ShapeDtypeStruct((tp, DIN), BF16), jax.ShapeDtypeStruct((32, D), F32)] + ex.out_shape,
        scratch_shapes=[ext, ext, ext, shs, shs, pltpu.VMEM((TM, D), F32), pltpu.VMEM((CONV_K, 8, D), F32),
                        pltpu.VMEM((CONV_K, 8, D), F32)] + ex.scratch,
        compiler_params=_params(("arbitrary",), 48),
    )(dac, dac, dac, dm, dm, dm, z, z, z, z, z, z, dzg, w_dw, *qs)
    return res[:2], res[2:]


def _in_bwd(dz, h0, dh1, g_mix, w_g, seq, qs):
    tp = h0.shape[0]
    nt = tp // TM
    ex = _ChipExchange(qs)
    nq = ex.n

    def body(*refs):
        dz_ref, h_ref, dh1_ref, g_ref, w_hbm = refs[:5]
        gx_ref, gmeta_ref, acc_ref = refs[5 + nq:8 + nq]
        w_vm, sems = refs[8 + 2 * nq:10 + 2 * nq]
        ex.bind(refs[5:5 + nq], refs[8 + nq:8 + 2 * nq], refs[10 + 2 * nq:])
        i = pl.program_id(0)

        @pl.when(i == 0)
        def _():
            ex.issue()
            acc_ref[...] = jnp.zeros_like(acc_ref)

        _load_once(i == 0, [(w_hbm, w_vm)], sems)

        du = _dot_nt(dz_ref[:, 0:INB], w_vm[0])
        for d in range(1, NDEV):
            du = du + _dot_nt(dz_ref[:, INB * d:INB * (d + 1)], w_vm[d])
        h = h_ref[...]
        r = lax.rsqrt(jnp.mean(h * h, axis=-1, keepdims=True) + RMS_EPS)
        n0 = h * r
        acc_ref[0:1, :] += jnp.sum(du * n0, axis=0, keepdims=True)
        dn = du * g_ref[...]
        gx_ref[...] = dh1_ref[...] + r * (dn - n0 * jnp.mean(dn * n0, axis=-1, keepdims=True))

        @pl.when(i == nt - 1)
        def _():
            gmeta_ref[...] = gx_ref[pl.ds(TM - N_META, N_META), :]
            ex.finish()

    tile = pl.BlockSpec((TM, D), lambda i: (i, 0))
    anys = pl.BlockSpec(memory_space=pl.ANY)
    res = pl.pallas_call(
        body, name="in_bwd", grid=(nt,),
        in_specs=[pl.BlockSpec((TM, DIN), lambda i: (i, 0)), tile, tile, pl.BlockSpec((1, D), lambda i: (0, 0)), anys]
        + [anys] * nq,
        out_specs=[tile, pl.BlockSpec((N_META, D), lambda i: (0, 0)), pl.BlockSpec((8, D), lambda i: (0, 0))] + [anys] * nq,
        out_shape=[jax.ShapeDtypeStruct((seq, D), F32), jax.ShapeDtypeStruct((N_META, D), F32),
                   jax.ShapeDtypeStruct((8, D), F32)] + ex.out_shape,
        scratch_shapes=[pltpu.VMEM((NDEV, D, INB), BF16), pltpu.SemaphoreType.DMA((1,))] + ex.scratch,
        compiler_params=_params(("arbitrary",), 48),
    )(dz, h0, dh1, g_mix, w_g, *qs)
    return res[:3], res[3:]


def _wgrad_in(u, dz):
    tp = u.shape[0]
    nt = tp // TM
    half = DIN // 2

    def body(u_ref, dz_ref, o_ref, acc):
        t = pl.program_id(1)

        @pl.when(t == 0)
        def _():
            acc[...] = jnp.zeros_like(acc)

        acc[...] += _dot_tn(u_ref[...], dz_ref[...])

        @pl.when(t == nt - 1)
        def _():
            for d in range(4):
                o_ref[d] = acc[:, INB * d:INB * (d + 1)].astype(BF16)

    return pl.pallas_call(
        body, name="wgrad_in", grid=(2, nt),
        in_specs=[pl.BlockSpec((TM, D), lambda h, t: (t, 0)), pl.BlockSpec((TM, half), lambda h, t: (t, h))],
        out_specs=pl.BlockSpec((4, D, INB), lambda h, t: (h, 0, 0)),
        out_shape=jax.ShapeDtypeStruct((NDEV, D, INB), BF16),
        scratch_shapes=[pltpu.VMEM((D, half), F32)],
        compiler_params=_params(("arbitrary", "arbitrary"), 48),
    )(u, dz)


def _wgrad_mix(s, dyc, q, dyp, merged, dh1, m, dm2):
    tp = s.shape[0]
    nt = tp // TM
    rb = D // NDEV

    def body(s_ref, dyc_ref, q_ref, dyp_ref, mg_ref, dh1_ref, m_ref, dm2_ref, o_ref, op_ref, acc, accp):
        t = pl.program_id(0)

        @pl.when(t == 0)
        def _():
            acc[...] = jnp.zeros_like(acc)
            accp[...] = jnp.zeros_like(accp)

        acc[0] += _dot_tn(s_ref[...], dyc_ref[...])
        acc[1] += _dot_tn(q_ref[...], dyp_ref[...])
        acc[2] += _dot_tn(mg_ref[...], dh1_ref[...].astype(BF16))
        for g in range(4):
            accp[g] += _dot_tn(m_ref[:, g * PG:(g + 1) * PG], dm2_ref[:, g * PG:(g + 1) * PG])

        @pl.when(t == nt - 1)
        def _():
            for d in range(NDEV):
                for k in range(3):
                    o_ref[d, k] = acc[k, rb * d:rb * (d + 1), :].astype(BF16)
                for g in range(4):
                    op_ref[d, g] = accp[g, 32 * d:32 * (d + 1), :].astype(BF16)

    tile = pl.BlockSpec((TM, D), lambda t: (t, 0))
    return pl.pallas_call(
        body, name="wgrad_mix", grid=(nt,),
        in_specs=[tile] * 8,
        out_specs=[pl.BlockSpec((NDEV, 3, rb, D), lambda t: (0, 0, 0, 0)),
                   pl.BlockSpec((NDEV, 4, 32, PG), lambda t: (0, 0, 0, 0))],
        out_shape=[jax.ShapeDtypeStruct((NDEV, 3, rb, D), BF16), jax.ShapeDtypeStruct((NDEV, 4, 32, PG), BF16)],
        scratch_shapes=[pltpu.VMEM((3, D, D), F32), pltpu.VMEM((4, PG, PG), F32)],
        compiler_params=_params(("arbitrary",), 56),
    )(s, dyc, q, dyp, merged, dh1, m, dm2)


def _wgrad_gu(v, dfg, dfu):
    tp = v.shape[0]
    nt = tp // TM

    def body(v_ref, dg_ref, du_ref, o_ref, acc):
        k, t = pl.program_id(0), pl.program_id(2)

        @pl.when(t == 0)
        def _():
            acc[...] = jnp.zeros_like(acc)

        @pl.when(k == 0)
        def _():
            acc[...] += _dot_tn(v_ref[...], dg_ref[...])

        @pl.when(k == 1)
        def _():
            acc[...] += _dot_tn(v_ref[...], du_ref[...])

        @pl.when(t == nt - 1)
        def _():
            for d in range(4):
                o_ref[d] = acc[:, pl.ds(FFB * d, FFB)].astype(BF16)

    return pl.pallas_call(
        body, name="wgrad_gu", grid=(2, 2, nt),
        in_specs=[pl.BlockSpec((TM, D), lambda k, h, t: (t, 0)),
                  pl.BlockSpec((TM, FFC), lambda k, h, t: (t * (1 - k), h * (1 - k))),
                  pl.BlockSpec((TM, FFC), lambda k, h, t: (t * k, h * k))],
        out_specs=pl.BlockSpec((4, None, D, FFB), lambda k, h, t: (h, k, 0, 0)),
        out_shape=jax.ShapeDtypeStruct((NDEV, 2, D, FFB), BF16),
        scratch_shapes=[pltpu.VMEM((D, FFC), F32)],
        compiler_params=_params(("arbitrary",) * 3, 40),
    )(v, dfg, dfu)


def _wgrad_down(f, dh2):
    tp = f.shape[0]
    nt = tp // TM

    def body(f_ref, d_ref, o_ref, acc):
        t = pl.program_id(1)

        @pl.when(t == 0)
        def _():
            acc[...] = jnp.zeros_like(acc)

        acc[...] += _dot_tn(f_ref[...], d_ref[...].astype(BF16))

        @pl.when(t == nt - 1)
        def _():
            for d in range(4):
                o_ref[d] = acc[FFB * d:FFB * (d + 1), :].astype(BF16)

    return pl.pallas_call(
        body, name="wgrad_down", grid=(2, nt),
        in_specs=[pl.BlockSpec((TM, FFC), lambda h, t: (t, h)), pl.BlockSpec((TM, D), lambda h, t: (t, 0))],
        out_specs=pl.BlockSpec((4, FFB, D), lambda h, t: (h, 0, 0)),
        out_shape=jax.ShapeDtypeStruct((NDEV, FFB, D), BF16),
        scratch_shapes=[pltpu.VMEM((FFC, D), F32)],
        compiler_params=_params(("arbitrary", "arbitrary"), 40),
    )(f, dh2)


def kernel(x, meta_tokens, g_mix, w_in, b_gate, w_dw, b_dw, ln_g, ln_b, w_conv_out, w_pool, pool_scale, w_pool_out, w_o, g_ffn, w_ffn_gate, w_ffn_up, w_ffn_down, g_final, loss_target, m_meta_tokens, m_g_mix, m_w_in, m_b_gate, m_w_dw, m_b_dw, m_ln_g, m_ln_b, m_w_conv_out, m_w_pool, m_pool_scale, m_w_pool_out, m_w_o, m_g_ffn, m_w_ffn_gate, m_w_ffn_up, m_w_ffn_down, m_g_final, v_meta_tokens, v_g_mix, v_w_in, v_b_gate, v_w_dw, v_b_dw, v_ln_g, v_ln_b, v_w_conv_out, v_w_pool, v_pool_scale, v_w_pool_out, v_w_o, v_g_ffn, v_w_ffn_gate, v_w_ffn_up, v_w_ffn_down, v_g_final):
    seq = x.shape[1]
    tp = -(-(seq + 2 * HALO) // TM) * TM
    nx_last = seq - (tp // TM - 1) * TM
    assert 0 < nx_last <= TM - 2 * HALO and nx_last % 8 == 0

    whole = (Ellipsis,)
    g_in, g_small = _all_gather(
        [((D, INB), [(w_in, whole, 0)]),
         ((48, D // NDEV), [(meta_tokens, pl.ds(0, N_META), whole), (w_dw, pl.ds(N_META, CONV_K), 0)])], [BF16, F32])
    ag_mix = _Gather([((3, D // NDEV, D), [(w_conv_out, 0, 0), (w_pool_out, 1, 0), (w_o, 2, 0)]),
                      ((4, PG // NDEV, PG), [(w_pool, whole, 0)])], [BF16, BF16])
    ag_ffn = _Gather([((2, D, FFB), [(w_ffn_gate, 0, 0), (w_ffn_up, 1, 0)]),
                      ((FFB, D), [(w_ffn_down, whole, 0)])], [BF16, BF16])
    small_full = g_small.transpose(1, 0, 2).reshape(48, D)
    wdw_full = small_full[N_META:]
    tail = jnp.concatenate([jnp.zeros((TM - nx_last - N_META, D), F32), small_full[:N_META]], axis=0)

    (h0, z, u), (g_mixw, g_pool) = _fwd_in(x[0], tail, g_mix, g_in, tp, ag_mix)
    (ac, m), (g_gu, g_down) = _seq_fwd(z, wdw_full, b_dw, seq, ag_ffn)
    w_gu = _repack_gu(g_gu)
    w_dn = g_down.reshape(2, FFC, D)
    h1, s, yc, yp, merged, q = _mix_fwd(ac, m, z, h0, b_gate, ln_g, ln_b, pool_scale, g_mixw, g_pool)
    fg, fu, v, f, dh2, head_acc = _ffn_fwd(h1, loss_target[0], g_ffn, g_final.reshape(1, D), w_gu, w_dn)

    dfg, dfu, dh1, ffn_acc = _ffn_bwd(dh2, fg, fu, h1, g_ffn, w_gu, w_dn)
    own_f, sib_f, q_f = _rs_pair("rs_pair_ffn", [_wgrad_gu(v, dfg, dfu), _wgrad_down(f, dh2)])
    (dac, dm, dzg, dyc, dyp, dm2, mix_acc), rel_f = _mix_bwd(
        dh1, z, yc, yp, ac, m, b_gate, ln_g, ln_b, pool_scale, g_mixw, g_pool, q_f)
    own_m, sib_m, q_m = _rs_pair("rs_pair_mix", list(_wgrad_mix(s, dyc, q, dyp, merged, dh1, m, dm2)))
    (dz, seq_acc), rel_m = _seq_bwd(dac, dm, dzg, z, wdw_full, seq, q_m)
    own_i, sib_i, q_i = _rs_pair("rs_pair_in", [_wgrad_in(u, dz)])
    (grad_x, g_meta, in_acc), rel_i = _in_bwd(dz, h0, dh1, g_mix, g_in, seq, q_i)
    small_g = jnp.concatenate([g_meta, seq_acc[:CONV_K], jnp.zeros((1, D), F32)], axis=0)
    p_small = small_g.reshape(48, NDEV, D // NDEV).transpose(1, 0, 2).astype(BF16)
    rep_g = jnp.concatenate([
        in_acc[0:1], mix_acc[0:1, :D], mix_acc[0:1, D:], seq_acc[CONV_K:CONV_K + 1], mix_acc[1:2, :D], mix_acc[1:2, D:],
        mix_acc[2:3, :D], ffn_acc[0:1], head_acc[1:2], head_acc[0:1], jnp.zeros((REP_ROWS - 10, D), F32)], axis=0)
    own_s, sib_s, rel_s, rep_all = _reduce_scatter([p_small], rep_g)
    owns = [own_i[0], own_s[0], own_m[0], own_m[1], own_f[0], own_f[1]]
    sibs = [sib_i[0], sib_s[0], sib_m[0], sib_m[1], sib_f[0], sib_f[1]]
    rels = [rel_i[0], rel_s[0], rel_m[0], rel_m[1], rel_f[0], rel_f[1]]

    def lead(a):
        return a.reshape(1, *a.shape)

    def stack4(a, lead_dims):
        return a.reshape(*lead_dims, 1, 4 * 32, PG)

    (r_in,) = _adamw_multi("adamw_in", lead(owns[0]), sibs[0][:, None], rels[0][:, None], [w_in], [m_w_in], [v_w_in], 4)
    r_meta, r_dw = _adamw_meta_dw(owns[1], sibs[1], rels[1], (meta_tokens, m_meta_tokens, v_meta_tokens),
                                  (w_dw, m_w_dw, v_w_dw))
    r_conv, r_pout, r_o = _adamw_multi("adamw_mix", owns[2], sibs[2], rels[2], [w_conv_out, w_pool_out, w_o],
                                       [m_w_conv_out, m_w_pool_out, m_w_o], [v_w_conv_out, v_w_pool_out, v_w_o], 1)
    (r_pool,) = _adamw_multi("adamw_pool", stack4(owns[3], ()), stack4(sibs[3], (4,)), stack4(rels[3], (3,)),
                             [w_pool.reshape(1, 128, PG)], [m_w_pool.reshape(1, 128, PG)], [v_w_pool.reshape(1, 128, PG)], 1)
    r_pool = tuple(a.reshape(w_pool.shape) for a in r_pool)
    r_gate, r_up = _adamw_multi("adamw_gu", owns[4], sibs[4], rels[4], [w_ffn_gate, w_ffn_up],
                                [m_w_ffn_gate, m_w_ffn_up], [v_w_ffn_gate, v_w_ffn_up], 4)
    (r_down,) = _adamw_multi("adamw_down", lead(owns[5]), sibs[5][:, None], rels[5][:, None],
                             [w_ffn_down], [m_w_ffn_down], [v_w_ffn_down], 2)
    row = (1, D)
    loss, reps = _adamw_rep(
        rep_all,
        [g_mix, b_gate, b_dw, ln_g, ln_b, pool_scale, g_ffn, g_final.reshape(row)],
        [m_g_mix, m_b_gate, m_b_dw, m_ln_g, m_ln_b, m_pool_scale, m_g_ffn, m_g_final.reshape(row)],
        [v_g_mix, v_b_gate, v_b_dw, v_ln_g, v_ln_b, v_pool_scale, v_g_ffn, v_g_final.reshape(row)])
    r_gmix, r_bg, r_bdw, r_lg, r_lb, r_ps, r_gffn, r_gfin = reps
    r_gfin = tuple(a.reshape(D) for a in r_gfin)

    in_order = [r_meta, r_gmix, r_in, r_bg, r_dw, r_bdw, r_lg, r_lb, r_conv, r_pool, r_ps, r_pout, r_o, r_gffn,
                r_gate, r_up, r_down, r_gfin]
    return (loss.reshape(()), grad_x[None], *[r[0] for r in in_order], *[r[1] for r in in_order],
            *[r[2] for r in in_order], *[r[3] for r in in_order])
```

```python
import math

import jax
import jax.numpy as jnp
from jax import lax
from jax.experimental import pallas as pl
from jax.experimental.pallas import tpu as pltpu

F32, BF16 = jnp.float32, jnp.bfloat16
MESH_ID = pl.DeviceIdType.MESH
NDEV = 8

D = 1024
N_META = 16
CONV_K = 31
HALO = 16
POOL_WINDOWS = (2, 4, 8, 16)
PG = 256
DIN = 5 * D
DFF = 2816
FFB = DFF // NDEV
FFC = DFF // 2
INB = DIN // NDEV
RMS_EPS = 1e-6
LN_EPS = 1e-5
ADAM_LR, ADAM_B1, ADAM_B2, ADAM_EPS, ADAM_WD, ADAM_STEP = 0.001, 0.9, 0.999, 1e-08, 0.01, 10

TM = 384
TMS = 192
TM_IO = 704
TM_WG = 1056
TM_WM = 528
RB, CW = 64, 128
MIB = 2 ** 20


def _sig(x):
    return 1.0 / (1.0 + jnp.exp(-x))


def _dot(a, b):
    return jnp.dot(a, b, preferred_element_type=F32)


def _dot_nt(a, b):
    return lax.dot_general(a, b, (((1,), (1,)), ((), ())), preferred_element_type=F32)


def _dot_tn(a, b):
    return lax.dot_general(a, b, (((0,), (0,)), ((), ())), preferred_element_type=F32)


def _pick(tp, pref):
    return pref if tp % pref == 0 else TM


def _params(sem, vmem_mib):
    return pltpu.CompilerParams(dimension_semantics=sem, vmem_limit_bytes=vmem_mib * MIB)


def _load_once(first, pairs, sems):
    @pl.when(first)
    def _():
        cps = [pltpu.make_async_copy(s, d, sems.at[k]) for k, (s, d) in enumerate(pairs)]
        for cp in cps:
            cp.start()
        for cp in cps:
            cp.wait()


def _place():
    x, y, c = lax.axis_index("x"), lax.axis_index("y"), lax.axis_index("c")
    return x, y, c


class _Gather:
    def __init__(self, groups, dtypes):
        self.groups, self.dtypes, self.n = groups, dtypes, len(groups)
        self.arrays = [a for _, parts in groups for a, _, _ in parts]
        self.out_shape = [jax.ShapeDtypeStruct((NDEV, *s), dt) for (s, _), dt in zip(groups, dtypes)]
        self.scratch = [pltpu.VMEM(s, dt) for (s, _), dt in zip(groups, dtypes)] + [
            pltpu.SemaphoreType.DMA((7 * self.n,)), pltpu.SemaphoreType.DMA((7 * self.n,)),
            pltpu.SemaphoreType.DMA((self.n,))]

    def bind(self, ins, outs, scratch):
        self.ins, self.outs, self.stages = ins, outs, scratch[:self.n]
        self.send_sems, self.recv_sems, self.local_sems = scratch[self.n:]
        return self

    def _copy(self, w, k, block, to, src=None):
        dst = self.outs[w].at[4 * block[0] + 2 * block[1] + block[2]]
        return pltpu.make_async_remote_copy(
            src_ref=dst if src is None else src, dst_ref=dst,
            send_sem=self.send_sems.at[7 * w + k], recv_sem=self.recv_sems.at[7 * w + k],
            device_id=to, device_id_type=MESH_ID)

    def _first(self):
        x, y, c = _place()
        me, sibling = (x, y, c), (x, y, 1 - c)
        chips = [(1 - x, y), (x, 1 - y), (1 - x, 1 - y)]
        mine, first = [], []
        for w in range(self.n):
            mine.append(pltpu.make_async_copy(self.stages[w], self.outs[w].at[4 * x + 2 * y + c], self.local_sems.at[w]))
            first.append(self._copy(w, 0, me, sibling, src=self.stages[w]))
            first += [self._copy(w, 1 + j, me, (*chip, c), src=self.stages[w]) for j, chip in enumerate(chips)]
        return mine, first

    def _passed(self):
        x, y, c = _place()
        chips = [(1 - x, y), (x, 1 - y), (1 - x, 1 - y)]
        return [self._copy(w, 4 + j, (*chip, c), (x, y, 1 - c)) for w in range(self.n) for j, chip in enumerate(chips)]

    def issue(self):
        a = 0
        for w in range(self.n):
            shape, parts = self.groups[w]
            if sum(arr.size for arr, _, _ in parts) < math.prod(shape):
                self.stages[w][...] = jnp.zeros(shape, self.dtypes[w])
            for _, dst, src in parts:
                self.stages[w][dst] = self.ins[a][src].astype(self.dtypes[w])
                a += 1
        mine, first = self._first()
        for cp in mine + first:
            cp.start()

    def forward(self):
        x, y, c = _place()
        chips = [(1 - x, y), (x, 1 - y), (1 - x, 1 - y)]
        passed = self._passed()
        for w in range(self.n):
            for j, chip in enumerate(chips):
                self._copy(w, 1 + j, (*chip, c), (x, y, c)).wait_recv()
                passed[3 * w + j].start()

    def finish(self):
        x, y, c = _place()
        chips = [(1 - x, y), (x, 1 - y), (1 - x, 1 - y)]
        for w in range(self.n):
            self._copy(w, 0, (x, y, 1 - c), (x, y, c)).wait_recv()
            for j, chip in enumerate(chips):
                self._copy(w, 4 + j, (*chip, 1 - c), (x, y, c)).wait_recv()
        mine, first = self._first()
        for cp in first + self._passed():
            cp.wait_send()
        for cp in mine:
            cp.wait()


def _all_gather(groups, dtypes):
    ag = _Gather(groups, dtypes)
    na, n = len(ag.arrays), ag.n

    def body(*refs):
        ag.bind(refs[:na], refs[na:na + n], refs[na + n:])
        ag.issue()
        ag.forward()
        ag.finish()

    return pl.pallas_call(
        body, name="ag_weights", out_shape=ag.out_shape,
        in_specs=[pl.BlockSpec(memory_space=pltpu.VMEM)] * na,
        out_specs=[pl.BlockSpec(memory_space=pl.ANY)] * n,
        scratch_shapes=ag.scratch,
        compiler_params=pltpu.CompilerParams(vmem_limit_bytes=40 * MIB),
    )(*ag.arrays)


class _ChipExchange:
    def __init__(self, qs):
        self.n = len(qs)
        self.out_shape = [jax.ShapeDtypeStruct(q.shape, q.dtype) for q in qs]
        self.scratch = [pltpu.SemaphoreType.DMA((3 * self.n,)), pltpu.SemaphoreType.DMA((3 * self.n,))]

    def bind(self, qs, rels, scratch):
        self.qs, self.rels = qs, rels
        self.send_sems, self.recv_sems = scratch
        return self

    def _copies(self):
        x, y, c = _place()
        chips = [(1 - x, y), (x, 1 - y), (1 - x, 1 - y)]
        return [pltpu.make_async_remote_copy(
            src_ref=self.qs[w].at[j], dst_ref=self.rels[w].at[j],
            send_sem=self.send_sems.at[3 * w + j], recv_sem=self.recv_sems.at[3 * w + j],
            device_id=(*chips[j], c), device_id_type=MESH_ID) for w in range(self.n) for j in range(3)]

    def issue(self):
        for cp in self._copies():
            cp.start()

    def finish(self):
        cps = self._copies()
        for cp in cps:
            cp.wait_recv()
        for cp in cps:
            cp.wait_send()


def _reduce_scatter(parts, small):
    n = len(parts)
    blks = [p.shape[1:] for p in parts]

    def body(*refs):
        ps, small_ref = refs[:n], refs[n]
        o = n + 1
        owns, sibs, rels, small_out = refs[o:o + n], refs[o + n:o + 2 * n], refs[o + 2 * n:o + 3 * n], refs[o + 3 * n]
        o += 3 * n + 1
        pa, pb, qst = refs[o:o + n], refs[o + n:o + 2 * n], refs[o + 2 * n:o + 3 * n]
        s1_send, s1_recv, s2_send, s2_recv, sm_send, sm_recv, lsem = refs[o + 3 * n:]
        x, y, c = _place()
        me = 4 * x + 2 * y + c
        sibling = (x, y, 1 - c)
        chips = [(1 - x, y), (x, 1 - y), (1 - x, 1 - y)]
        all_chips = [(x, y)] + chips

        own_cps = []
        for w in range(n):
            cp = pltpu.make_async_copy(ps[w].at[me], owns[w], lsem.at[w])
            cp.start()
            own_cps.append(cp)
        sm_own = pltpu.make_async_copy(small_ref, small_out.at[me], lsem.at[n])
        sm_own.start()

        def small_copy(r):
            peer = ((x + (r >> 2)) % 2, (y + ((r >> 1) & 1)) % 2, (c + (r & 1)) % 2)
            return pltpu.make_async_remote_copy(
                src_ref=small_ref, dst_ref=small_out.at[me], send_sem=sm_send.at[r - 1], recv_sem=sm_recv.at[r - 1],
                device_id=peer, device_id_type=MESH_ID)

        sm_cps = [small_copy(r) for r in range(1, NDEV)]
        for cp in sm_cps:
            cp.start()

        def pair_copy(w, rel):
            cx, cy = all_chips[rel]
            return pltpu.make_async_remote_copy(
                src_ref=ps[w].at[4 * cx + 2 * cy + (1 - c)], dst_ref=sibs[w].at[rel],
                send_sem=s1_send.at[4 * w + rel], recv_sem=s1_recv.at[4 * w + rel],
                device_id=sibling, device_id_type=MESH_ID)

        def chip_copy(w, j):
            return pltpu.make_async_remote_copy(
                src_ref=qst[w].at[j], dst_ref=rels[w].at[j],
                send_sem=s2_send.at[3 * w + j], recv_sem=s2_recv.at[3 * w + j],
                device_id=(*chips[j], c), device_id_type=MESH_ID)

        pair_cps = [pair_copy(w, rel) for w in range(n) for rel in (1, 2, 3, 0)]
        for cp in pair_cps:
            cp.start()
        chip_cps = []
        for w in range(n):
            for j, (cx, cy) in enumerate(chips):
                pair_copy(w, 1 + j).wait_recv()
                la = pltpu.make_async_copy(ps[w].at[4 * cx + 2 * cy + c], pa[w], lsem.at[n + 1])
                lb = pltpu.make_async_copy(sibs[w].at[1 + j], pb[w], lsem.at[n + 2])
                la.start()
                lb.start()
                la.wait()
                lb.wait()
                qst[w][j] = (pa[w][...].astype(F32) + pb[w][...].astype(F32)).astype(BF16)
                cp = chip_copy(w, j)
                cp.start()
                chip_cps.append(cp)
        for w in range(n):
            pair_copy(w, 0).wait_recv()
            for j in range(3):
                chip_copy(w, j).wait_recv()
        for cp in sm_cps:
            cp.wait_recv()
        for cp in pair_cps + chip_cps + sm_cps:
            cp.wait_send()
        for cp in own_cps:
            cp.wait()
        sm_own.wait()

    any_spec = pl.BlockSpec(memory_space=pl.ANY)
    outs = pl.pallas_call(
        body, name="rs_grads",
        out_shape=[jax.ShapeDtypeStruct(b, BF16) for b in blks]
        + [jax.ShapeDtypeStruct((4, *b), BF16) for b in blks]
        + [jax.ShapeDtypeStruct((3, *b), BF16) for b in blks]
        + [jax.ShapeDtypeStruct((NDEV, *small.shape), F32)],
        in_specs=[any_spec] * (n + 1),
        out_specs=[any_spec] * (3 * n + 1),
        scratch_shapes=[pltpu.VMEM(b, BF16) for b in blks] + [pltpu.VMEM(b, BF16) for b in blks]
        + [pltpu.VMEM((3, *b), BF16) for b in blks]
        + [pltpu.SemaphoreType.DMA((4 * n,)), pltpu.SemaphoreType.DMA((4 * n,)),
           pltpu.SemaphoreType.DMA((3 * n,)), pltpu.SemaphoreType.DMA((3 * n,)),
           pltpu.SemaphoreType.DMA((NDEV - 1,)), pltpu.SemaphoreType.DMA((NDEV - 1,)),
           pltpu.SemaphoreType.DMA((n + 3,))],
        compiler_params=pltpu.CompilerParams(vmem_limit_bytes=40 * MIB),
    )(*parts, small)
    return outs[:n], outs[n:2 * n], outs[2 * n:3 * n], outs[3 * n]


def _rs_pair(name, parts):
    n = len(parts)
    blks = [p.shape[1:] for p in parts]

    def body(*refs):
        ps = refs[:n]
        owns, sibs, qs = refs[n:2 * n], refs[2 * n:3 * n], refs[3 * n:4 * n]
        pa, pb, qst = refs[4 * n:5 * n], refs[5 * n:6 * n], refs[6 * n:7 * n]
        s_send, s_recv, lsem = refs[7 * n:]
        x, y, c = _place()
        chips = [(1 - x, y), (x, 1 - y), (1 - x, 1 - y)]
        all_chips = [(x, y)] + chips

        own_cps = [pltpu.make_async_copy(ps[w].at[4 * x + 2 * y + c], owns[w], lsem.at[w]) for w in range(n)]
        for cp in own_cps:
            cp.start()

        def pair_copy(w, rel):
            cx, cy = all_chips[rel]
            return pltpu.make_async_remote_copy(
                src_ref=ps[w].at[4 * cx + 2 * cy + (1 - c)], dst_ref=sibs[w].at[rel],
                send_sem=s_send.at[4 * w + rel], recv_sem=s_recv.at[4 * w + rel],
                device_id=(x, y, 1 - c), device_id_type=MESH_ID)

        pair_cps = [pair_copy(w, rel) for w in range(n) for rel in (1, 2, 3, 0)]
        for cp in pair_cps:
            cp.start()
        q_cps = []
        for w in range(n):
            for j, (cx, cy) in enumerate(chips):
                la = pltpu.make_async_copy(ps[w].at[4 * cx + 2 * cy + c], pa[w], lsem.at[n])
                lb = pltpu.make_async_copy(sibs[w].at[1 + j], pb[w], lsem.at[n + 1])
                la.start()
                pair_copy(w, 1 + j).wait_recv()
                lb.start()
                la.wait()
                lb.wait()
                qst[w][j] = (pa[w][...].astype(F32) + pb[w][...].astype(F32)).astype(BF16)
            cp = pltpu.make_async_copy(qst[w], qs[w], lsem.at[n + 2 + w])
            cp.start()
            q_cps.append(cp)
        for w in range(n):
            pair_copy(w, 0).wait_recv()
        for cp in pair_cps:
            cp.wait_send()
        for cp in own_cps + q_cps:
            cp.wait()

    any_spec = pl.BlockSpec(memory_space=pl.ANY)
    outs = pl.pallas_call(
        body, name=name,
        out_shape=[jax.ShapeDtypeStruct(b, BF16) for b in blks]
        + [jax.ShapeDtypeStruct((4, *b), BF16) for b in blks]
        + [jax.ShapeDtypeStruct((3, *b), BF16) for b in blks],
        in_specs=[any_spec] * n,
        out_specs=[any_spec] * (3 * n),
        scratch_shapes=[pltpu.VMEM(b, BF16) for b in blks] + [pltpu.VMEM(b, BF16) for b in blks]
        + [pltpu.VMEM((3, *b), BF16) for b in blks]
        + [pltpu.SemaphoreType.DMA((4 * n,)), pltpu.SemaphoreType.DMA((4 * n,)), pltpu.SemaphoreType.DMA((2 * n + 2,))],
        compiler_params=pltpu.CompilerParams(vmem_limit_bytes=40 * MIB),
    )(*parts)
    return outs[:n], outs[n:2 * n], outs[2 * n:3 * n]


def _adamw_math(g, w, m, v):
    m = ADAM_B1 * m + (1.0 - ADAM_B1) * g
    v = ADAM_B2 * v + (1.0 - ADAM_B2) * (g * g)
    m_hat = m / (1.0 - ADAM_B1 ** ADAM_STEP)
    v_hat = v / (1.0 - ADAM_B2 ** ADAM_STEP)
    delta = -ADAM_LR * (m_hat / (jnp.sqrt(v_hat) + ADAM_EPS) + ADAM_WD * w)
    return delta, m, v


def _adamw_multi(name, own, sib, rel, ws, ms, vs, row_grid):
    k_n, r_n, c_n = own.shape
    rbk = r_n // row_grid

    def body(*refs):
        own_ref, sib_ref, r0_ref, r1_ref, r2_ref = refs[:5]
        w_refs, m_refs, v_refs = refs[5:5 + k_n], refs[5 + k_n:5 + 2 * k_n], refs[5 + 2 * k_n:5 + 3 * k_n]
        outs = refs[5 + 3 * k_n:]
        for k in range(k_n):
            g = own_ref[k].astype(F32) + sib_ref[k].astype(F32)
            g = g + r0_ref[k].astype(F32)
            g = g + r1_ref[k].astype(F32)
            g = g + r2_ref[k].astype(F32)
            delta, mm, vv = _adamw_math(g, w_refs[k][0], m_refs[k][0], v_refs[k][0])
            outs[4 * k][0] = g
            outs[4 * k + 1][0] = delta
            outs[4 * k + 2][0] = mm
            outs[4 * k + 3][0] = vv

    def lead(j):
        return pl.BlockSpec((None, k_n, rbk, c_n), lambda g: (j, 0, g, 0))

    wspec = pl.BlockSpec((1, rbk, c_n), lambda g: (0, g, 0))
    shp = jax.ShapeDtypeStruct((1, r_n, c_n), F32)
    res = pl.pallas_call(
        body, name=name, grid=(row_grid,),
        in_specs=[pl.BlockSpec((k_n, rbk, c_n), lambda g: (0, g, 0)), lead(0), lead(0), lead(1), lead(2)] + [wspec] * (3 * k_n),
        out_specs=[wspec] * (4 * k_n), out_shape=[shp] * (4 * k_n),
        compiler_params=_params(("arbitrary",), 40),
    )(own, sib, rel, rel, rel, *ws, *ms, *vs)
    return [tuple(res[4 * k:4 * k + 4]) for k in range(k_n)]


def _adamw_meta_dw(own, sib, rel, meta, dw):
    def body(own_ref, sib_ref, rel_ref, wm, mm, vm, wd, md, vd, *outs):
        def gsum(rows):
            g = own_ref[rows, :].astype(F32) + sib_ref[0, rows, :].astype(F32)
            for j in range(3):
                g = g + rel_ref[j, rows, :].astype(F32)
            return g

        g = gsum(pl.ds(0, N_META))
        delta, m2, v2 = _adamw_math(g, wm[...], mm[...], vm[...])
        for o, val in zip(outs[:4], (g, delta, m2, v2)):
            o[...] = val
        g = gsum(pl.ds(N_META, CONV_K))
        delta, m2, v2 = _adamw_math(g, wd[0], md[0], vd[0])
        for o, val in zip(outs[4:], (g, delta, m2, v2)):
            o[0] = val

    s_meta = jax.ShapeDtypeStruct(meta[0].shape, F32)
    s_dw = jax.ShapeDtypeStruct(dw[0].shape, F32)
    res = pl.pallas_call(body, name="adamw_meta_dw", out_shape=[s_meta] * 4 + [s_dw] * 4)(own, sib, rel, *meta, *dw)
    return tuple(res[:4]), tuple(res[4:])


REP_ROWS = 16


def _adamw_rep(gathered, ws, ms, vs):
    rows = [(0, 1), (1, 2), (3, 1), (4, 1), (5, 1), (6, 1), (7, 1), (8, 1)]

    def body(g_ref, *refs):
        w_refs, m_refs, v_refs = refs[:8], refs[8:16], refs[16:24]
        loss_ref, outs, acc = refs[24], refs[25:57], refs[57]
        g = g_ref[0]
        for d in range(1, NDEV):
            g = g + g_ref[d]
        acc[...] = g
        loss_ref[...] = (0.5 / D) * jnp.sum(acc[pl.ds(9, 1), :], axis=1, keepdims=True)
        for p, (r0, nr) in enumerate(rows):
            for h in range(nr):
                cols = pl.ds(h * D, D)
                gp = acc[pl.ds(r0 + h, 1), :]
                delta, mm, vv = _adamw_math(gp, w_refs[p][:, cols], m_refs[p][:, cols], v_refs[p][:, cols])
                for o, val in zip(outs[4 * p:4 * p + 4], (gp, delta, mm, vv)):
                    o[:, cols] = val

    shapes = [jax.ShapeDtypeStruct(w.shape, F32) for w in ws]
    res = pl.pallas_call(
        body, name="adamw_rep",
        out_shape=[jax.ShapeDtypeStruct((1, 1), F32)] + [s for s in shapes for _ in range(4)],
        scratch_shapes=[pltpu.VMEM((REP_ROWS, D), F32)],
    )(gathered, *ws, *ms, *vs)
    return res[0], [tuple(res[1 + 4 * p:5 + 4 * p]) for p in range(8)]


def _repack_gu(g_gu):
    def body(x_ref, o_ref):
        for i in range(2):
            for d in range(NDEV):
                o_ref[i, d // 4, :, pl.ds(FFB * (d % 4), FFB)] = x_ref[d, i]

    return pl.pallas_call(body, name="repack_gu", out_shape=jax.ShapeDtypeStruct((2, 2, D, FFC), BF16),
                          compiler_params=pltpu.CompilerParams(vmem_limit_bytes=40 * MIB))(g_gu)


def _whole(a):
    nd = a.ndim
    return pl.BlockSpec(a.shape, lambda *g: (0,) * nd)


def _fwd_in(x2, tail, g_mix, w_g, tp, ag):
    tm = _pick(tp, TM_IO)
    nt = tp // tm
    nx_last = tm - tail.shape[0]
    na, ng = len(ag.arrays), ag.n
    half = NDEV // 2

    def body(*refs):
        x_ref, tail_ref, g_ref, w_hbm = refs[:4]
        h_ref, z_ref, u_ref = refs[4 + na:7 + na]
        w_vm, sems = refs[7 + na + ng:9 + na + ng]
        ag.bind(refs[4:4 + na], refs[7 + na:7 + na + ng], refs[9 + na + ng:])
        i, j = pl.program_id(0), pl.program_id(1)
        first = (i == 0) & (j == 0)

        @pl.when(first)
        def _():
            ag.issue()

        @pl.when((i == max(nt - 2, 0)) & (j == 0))
        def _():
            ag.forward()

        _load_once(first, [(w_hbm, w_vm)], sems)

        @pl.when((j == 0) & (i < nt - 1))
        def _():
            h_ref[...] = x_ref[...]

        @pl.when((j == 0) & (i == nt - 1))
        def _():
            h_ref[pl.ds(0, nx_last), :] = x_ref[pl.ds(0, nx_last), :]
            h_ref[pl.ds(nx_last, tm - nx_last), :] = tail_ref[...]

        @pl.when(j == 0)
        def _():
            xv = h_ref[...]
            r = lax.rsqrt(jnp.mean(xv * xv, axis=-1, keepdims=True) + RMS_EPS)
            u_ref[...] = (xv * r * g_ref[...]).astype(BF16)

        u = u_ref[...]
        for d in range(half):
            z_ref[:, INB * d:INB * (d + 1)] = _dot(u, w_vm[half * j + d])

        @pl.when((i == nt - 1) & (j == 1))
        def _():
            ag.finish()

    tile = pl.BlockSpec((tm, D), lambda i, j: (i, 0))
    res = pl.pallas_call(
        body, name="fwd_in", grid=(nt, 2),
        in_specs=[tile, pl.BlockSpec(tail.shape, lambda i, j: (0, 0)), pl.BlockSpec((1, D), lambda i, j: (0, 0)),
                  pl.BlockSpec(memory_space=pl.ANY)] + [_whole(a) for a in ag.arrays],
        out_specs=[tile, pl.BlockSpec((tm, DIN // 2), lambda i, j: (i, j)), tile] + [pl.BlockSpec(memory_space=pl.ANY)] * ng,
        out_shape=[jax.ShapeDtypeStruct((tp, D), F32), jax.ShapeDtypeStruct((tp, DIN), F32),
                   jax.ShapeDtypeStruct((tp, D), BF16)] + ag.out_shape,
        scratch_shapes=[pltpu.VMEM((NDEV, D, INB), BF16), pltpu.SemaphoreType.DMA((1,))] + ag.scratch,
        compiler_params=_params(("arbitrary", "arbitrary"), 56),
    )(x2, tail, g_mix, w_g, *ag.arrays)
    return res[:3], res[3:]


def _halo_specs(col, nt, width=D):
    r = TM // HALO
    nb = nt * r
    return [pl.BlockSpec((HALO, width), lambda i: ((i * r + nb - 1) % nb, col)),
            pl.BlockSpec((TM, width), lambda i: (i, col)),
            pl.BlockSpec((HALO, width), lambda i: (((i + 1) * r) % nb, col))]


def _fill_ext(ext_ref, left, cur, right):
    ext_ref[pl.ds(0, HALO), :] = left
    ext_ref[pl.ds(HALO, TM), :] = cur
    ext_ref[pl.ds(HALO + TM, HALO), :] = right


def _shift_copies(ext_ref, sh_ref, c0):
    for r in range(8):
        sh_ref[r] = ext_ref[pl.ds(r, TM + 24), pl.ds(c0, CW)]


def _tap_rows(w_ref, w8):
    for k in range(CONV_K):
        w8[k] = jnp.broadcast_to(w_ref[pl.ds(k, 1), :], (8, D))


def _pool_cnt(i, seq, tp, left, right, rows, row0):
    b = i * TM + row0 + lax.broadcasted_iota(jnp.int32, (rows, 1), 0)
    b = jnp.where(b < 0, b + tp, b)
    b = jnp.where(b >= tp, b - tp, b)
    t = jnp.where(b < seq, b + N_META, b - (tp - N_META))
    lo = jnp.maximum(t - left, 0)
    hi = jnp.minimum(t + right + 1, seq + N_META)
    return jnp.maximum(hi - lo, 1).astype(F32)


def _seq_fwd(z, w_dw, b_dw, seq, gat):
    tp = z.shape[0]
    nt = tp // TM
    na, ng = len(gat.arrays), gat.n

    def body(*refs):
        av_l, av, av_r, ag_l, ag, ag_r, p_l, p, p_r, w_ref, b_ref = refs[:11]
        ac_ref, m_ref = refs[11 + na:13 + na]
        a_ext, p_ext, sh, w8 = refs[13 + na + ng:17 + na + ng]
        gat.bind(refs[11:11 + na], refs[13 + na:13 + na + ng], refs[17 + na + ng:])
        i = pl.program_id(0)

        @pl.when(i == 0)
        def _():
            gat.issue()
            _tap_rows(w_ref, w8)

        @pl.when(i == max(nt - 2, 0))
        def _():
            gat.forward()

        _fill_ext(a_ext, av_l[...] * _sig(ag_l[...]), av[...] * _sig(ag[...]), av_r[...] * _sig(ag_r[...]))
        _fill_ext(p_ext, p_l[...], p[...], p_r[...])
        for c0 in range(0, D, CW):
            _shift_copies(a_ext, sh, c0)

            def rows(j, carry):
                base = pl.multiple_of(j * RB, RB)
                acc = jnp.broadcast_to(b_ref[:, pl.ds(c0, CW)], (RB // 8, 8, CW))
                for k in range(CONV_K):
                    q, r = divmod(k + 1, 8)
                    slab = sh[r, pl.ds(pl.multiple_of(base + 8 * q, 8), RB), :].reshape(RB // 8, 8, CW)
                    acc = acc + slab * w8[k, :, pl.ds(c0, CW)]
                ac_ref[pl.ds(base, RB), pl.ds(c0, CW)] = acc.reshape(RB, CW)
                return carry

            lax.fori_loop(0, TM // RB, rows, 0)
        for g, win in enumerate(POOL_WINDOWS):
            left = win // 2
            right = win - 1 - left
            cols = pl.ds(g * PG, PG)
            s = p_ext[pl.ds(HALO - left, TM), cols]
            for off in range(-left + 1, right + 1):
                s = s + p_ext[pl.ds(HALO + off, TM), cols]
            cnt = _pool_cnt(i, seq, tp, left, right, TM, 0)
            m_ref[:, cols] = (s / cnt - p_ext[pl.ds(HALO, TM), cols]).astype(BF16)

        @pl.when(i == nt - 1)
        def _():
            gat.finish()

    res = pl.pallas_call(
        body, name="seq_fwd", grid=(nt,),
        in_specs=_halo_specs(0, nt) + _halo_specs(1, nt) + _halo_specs(2, nt)
        + [pl.BlockSpec((32, D), lambda i: (0, 0)), pl.BlockSpec((1, D), lambda i: (0, 0))] + [_whole(a) for a in gat.arrays],
        out_specs=[pl.BlockSpec((TM, D), lambda i: (i, 0))] * 2 + [pl.BlockSpec(memory_space=pl.ANY)] * ng,
        out_shape=[jax.ShapeDtypeStruct((tp, D), F32), jax.ShapeDtypeStruct((tp, D), BF16)] + gat.out_shape,
        scratch_shapes=[pltpu.VMEM((TM + 2 * HALO, D), F32), pltpu.VMEM((TM + 2 * HALO, D), F32),
                        pltpu.VMEM((8, TM + 24, CW), F32), pltpu.VMEM((CONV_K, 8, D), F32)] + gat.scratch,
        compiler_params=_params(("arbitrary",), 52),
    )(z, z, z, z, z, z, z, z, z, w_dw, b_dw, *gat.arrays)
    return res[:2], res[2:]


def _ln_stats(ac):
    mu = jnp.mean(ac, axis=-1, keepdims=True)
    xc = ac - mu
    rl = lax.rsqrt(jnp.mean(xc * xc, axis=-1, keepdims=True) + LN_EPS)
    return xc * rl, rl


def _pool_mix(m, wp_ref):
    return jnp.concatenate(
        [_dot(m[:, g * PG:(g + 1) * PG], wp_ref[:, g].reshape(PG, PG)) for g in range(4)], axis=1)


def _mix_fwd(ac, m, z, h0, b_gate, ln_g, ln_b, pool_scale, g_mixw, g_pool):
    tp = h0.shape[0]
    nt = tp // TMS

    def body(ac_ref, m_ref, zga, zgb, h_ref, bg_ref, lg_ref, lb_ref, ps_ref, wm_hbm, wp_hbm,
             h1_ref, s_ref, yc_ref, yp_ref, mg_ref, q_ref, wm, wp, sems):
        _load_once(pl.program_id(0) == 0, [(wm_hbm, wm), (wp_hbm, wp)], sems)
        n, _ = _ln_stats(ac_ref[...])
        l = n * lg_ref[...] + lb_ref[...]
        s = (l * _sig(l)).astype(BF16)
        s_ref[...] = s
        yc = _dot(s, wm[:, 0].reshape(D, D))
        q = (_pool_mix(m_ref[...], wp) * ps_ref[...]).astype(BF16)
        q_ref[...] = q
        yp = _dot(q, wm[:, 1].reshape(D, D))
        ga = _sig(zga[...] + bg_ref[:, :D])
        gb = _sig(zgb[...] + bg_ref[:, D:])
        merged = (ga * yc + gb * yp).astype(BF16)
        yc_ref[...] = yc
        yp_ref[...] = yp
        mg_ref[...] = merged
        h1_ref[...] = h_ref[...] + _dot(merged, wm[:, 2].reshape(D, D))

    def tile(col=0):
        return pl.BlockSpec((TMS, D), lambda i: (i, col))

    def vec(w):
        return pl.BlockSpec((1, w), lambda i: (0, 0))

    anys = pl.BlockSpec(memory_space=pl.ANY)
    f32o, b16o = jax.ShapeDtypeStruct((tp, D), F32), jax.ShapeDtypeStruct((tp, D), BF16)
    return pl.pallas_call(
        body, name="mix_fwd", grid=(nt,),
        in_specs=[tile(), tile(), tile(3), tile(4), tile(), vec(2 * D), vec(D), vec(D), vec(D), anys, anys],
        out_specs=[tile()] * 6,
        out_shape=[f32o, b16o, f32o, f32o, b16o, b16o],
        scratch_shapes=[pltpu.VMEM((NDEV, 3, D // NDEV, D), BF16), pltpu.VMEM((NDEV, 4, PG // NDEV, PG), BF16),
                        pltpu.SemaphoreType.DMA((2,))],
        compiler_params=_params(("arbitrary",), 48),
    )(ac, m, z, z, h0, b_gate, ln_g, ln_b, pool_scale, g_mixw, g_pool)


def _ffn_fwd(h1, tgt, g_ffn, g_final, w_gu, w_dn):
    tp = h1.shape[0]
    nt = tp // TM
    nx_last = tgt.shape[0] - (nt - 1) * TM

    def body(h_ref, t_ref, gf_ref, gl_ref, wgu_hbm, wdn_hbm,
             fg_ref, fu_ref, v_ref, f_ref, dh2_ref, acc_ref, wgu, wdn, v_sc, h2_sc, diff_sc, sems):
        i, j = pl.program_id(0), pl.program_id(1)
        _load_once((i == 0) & (j == 0), [(wgu_hbm, wgu), (wdn_hbm, wdn)], sems)

        @pl.when((i == 0) & (j == 0))
        def _():
            acc_ref[...] = jnp.zeros_like(acc_ref)

        @pl.when(j == 0)
        def _():
            h = h_ref[...]
            r = lax.rsqrt(jnp.mean(h * h, axis=-1, keepdims=True) + RMS_EPS)
            v = (h * r * gf_ref[...]).astype(BF16)
            v_sc[...] = v
            v_ref[...] = v
            h2_sc[...] = h

        v = v_sc[...]
        fg = _dot(v, wgu[0, j])
        fu = _dot(v, wgu[1, j])
        fg_ref[...] = fg
        fu_ref[...] = fu
        f = ((fg * _sig(fg)) * fu).astype(BF16)
        f_ref[...] = f
        h2_sc[...] += _dot(f, wdn[j])

        @pl.when(j == 1)
        def _():
            h2 = h2_sc[...]
            r = lax.rsqrt(jnp.mean(h2 * h2, axis=-1, keepdims=True) + RMS_EPS)
            n2 = h2 * r
            y = n2 * gl_ref[...]

            @pl.when(i < nt - 1)
            def _():
                diff_sc[...] = y - t_ref[...]

            @pl.when(i == nt - 1)
            def _():
                diff_sc[pl.ds(0, nx_last), :] = y[:nx_last] - t_ref[pl.ds(0, nx_last), :]
                diff_sc[pl.ds(nx_last, TM - nx_last), :] = jnp.zeros((TM - nx_last, D), F32)

            diff = diff_sc[...]
            dy = diff * (1.0 / D)
            acc_ref[0:1, :] += jnp.sum(diff * diff, axis=0, keepdims=True)
            acc_ref[1:2, :] += jnp.sum(dy * n2, axis=0, keepdims=True)
            dn = dy * gl_ref[...]
            dh2_ref[...] = r * (dn - n2 * jnp.mean(dn * n2, axis=-1, keepdims=True))

    def tile():
        return pl.BlockSpec((TM, D), lambda i, j: (i, 0))

    def chunk():
        return pl.BlockSpec((TM, FFC), lambda i, j: (i, j))

    def vec():
        return pl.BlockSpec((1, D), lambda i, j: (0, 0))

    anys = pl.BlockSpec(memory_space=pl.ANY)
    hid32, hid16 = jax.ShapeDtypeStruct((tp, DFF), F32), jax.ShapeDtypeStruct((tp, DFF), BF16)
    return pl.pallas_call(
        body, name="ffn_fwd", grid=(nt, 2),
        in_specs=[tile(), tile(), vec(), vec(), anys, anys],
        out_specs=[chunk(), chunk(), tile(), chunk(), tile(), pl.BlockSpec((8, D), lambda i, j: (0, 0))],
        out_shape=[hid32, hid32, jax.ShapeDtypeStruct((tp, D), BF16), hid16, jax.ShapeDtypeStruct((tp, D), F32),
                   jax.ShapeDtypeStruct((8, D), F32)],
        scratch_shapes=[pltpu.VMEM((2, 2, D, FFC), BF16), pltpu.VMEM((2, FFC, D), BF16),
                        pltpu.VMEM((TM, D), BF16), pltpu.VMEM((TM, D), F32), pltpu.VMEM((TM, D), F32),
                        pltpu.SemaphoreType.DMA((2,))],
        compiler_params=_params(("arbitrary", "arbitrary"), 56),
    )(h1, tgt, g_ffn, g_final, w_gu, w_dn)


def _ffn_bwd(dh2, fg, fu, h1, g_ffn, w_gu, w_dn):
    tp = h1.shape[0]
    nt = tp // TM

    def body(dh2_ref, fg_ref, fu_ref, h_ref, gf_ref, wgu_hbm, wdn_hbm,
             dfg_ref, dfu_ref, dh1_ref, acc_ref, wgu, wdn, d_sc, dv_sc, sems):
        i, j = pl.program_id(0), pl.program_id(1)
        _load_once((i == 0) & (j == 0), [(wgu_hbm, wgu), (wdn_hbm, wdn)], sems)

        @pl.when((i == 0) & (j == 0))
        def _():
            acc_ref[...] = jnp.zeros_like(acc_ref)

        @pl.when(j == 0)
        def _():
            d_sc[...] = dh2_ref[...].astype(BF16)
            dv_sc[...] = jnp.zeros_like(dv_sc)

        df = _dot_nt(d_sc[...], wdn[j])
        fg = fg_ref[...]
        sg = _sig(fg)
        dfu = (df * (fg * sg)).astype(BF16)
        dfg = (df * fu_ref[...] * (sg * (1.0 + fg * (1.0 - sg)))).astype(BF16)
        dfg_ref[...] = dfg
        dfu_ref[...] = dfu
        dv_sc[...] += _dot_nt(dfg, wgu[0, j]) + _dot_nt(dfu, wgu[1, j])

        @pl.when(j == 1)
        def _():
            h = h_ref[...]
            r = lax.rsqrt(jnp.mean(h * h, axis=-1, keepdims=True) + RMS_EPS)
            n1 = h * r
            dv = dv_sc[...]
            acc_ref[0:1, :] += jnp.sum(dv * n1, axis=0, keepdims=True)
            dn = dv * gf_ref[...]
            dh1_ref[...] = dh2_ref[...] + r * (dn - n1 * jnp.mean(dn * n1, axis=-1, keepdims=True))

    def tile():
        return pl.BlockSpec((TM, D), lambda i, j: (i, 0))

    def chunk():
        return pl.BlockSpec((TM, FFC), lambda i, j: (i, j))

    anys = pl.BlockSpec(memory_space=pl.ANY)
    hid16 = jax.ShapeDtypeStruct((tp, DFF), BF16)
    return pl.pallas_call(
        body, name="ffn_bwd", grid=(nt, 2),
        in_specs=[tile(), chunk(), chunk(), tile(), pl.BlockSpec((1, D), lambda i, j: (0, 0)), anys, anys],
        out_specs=[chunk(), chunk(), tile(), pl.BlockSpec((8, D), lambda i, j: (0, 0))],
        out_shape=[hid16, hid16, jax.ShapeDtypeStruct((tp, D), F32), jax.ShapeDtypeStruct((8, D), F32)],
        scratch_shapes=[pltpu.VMEM((2, 2, D, FFC), BF16), pltpu.VMEM((2, FFC, D), BF16),
                        pltpu.VMEM((TM, D), BF16), pltpu.VMEM((TM, D), F32), pltpu.SemaphoreType.DMA((2,))],
        compiler_params=_params(("arbitrary", "arbitrary"), 56),
    )(dh2, fg, fu, h1, g_ffn, w_gu, w_dn)


def _mix_bwd(dh1, z, yc, yp, ac, m, b_gate, ln_g, ln_b, pool_scale, g_mixw, g_pool, qs):
    tp = dh1.shape[0]
    nt = tp // TMS
    ex = _ChipExchange(qs)
    nq = ex.n

    def body(*refs):
        dh1_ref, zga, zgb, yc_ref, yp_ref, ac_ref, m_ref, bg_ref, lg_ref, lb_ref, ps_ref, wm_hbm, wp_hbm = refs[:13]
        dac_ref, dm_ref, dzg_ref, dyc_ref, dyp_ref, dm2_ref, acc_ref = refs[13 + nq:20 + nq]
        wm, wp, sems = refs[20 + 2 * nq:23 + 2 * nq]
        ex.bind(refs[13:13 + nq], refs[20 + nq:20 + 2 * nq], refs[23 + 2 * nq:])
        first = pl.program_id(0) == 0

        @pl.when(first)
        def _():
            ex.issue()
            acc_ref[...] = jnp.zeros_like(acc_ref)

        _load_once(first, [(wm_hbm, wm), (wp_hbm, wp)], sems)

        dmerged = _dot_nt(dh1_ref[...].astype(BF16), wm[:, 2].reshape(D, D))
        ga = _sig(zga[...] + bg_ref[:, :D])
        gb = _sig(zgb[...] + bg_ref[:, D:])
        dyc = dmerged * ga
        dyp = dmerged * gb
        dza = (dmerged * yc_ref[...]) * (ga * (1.0 - ga))
        dzb = (dmerged * yp_ref[...]) * (gb * (1.0 - gb))
        dzg_ref[:, :D] = dza.astype(BF16)
        dzg_ref[:, D:] = dzb.astype(BF16)
        acc_ref[0:1, :D] += jnp.sum(dza, axis=0, keepdims=True)
        acc_ref[0:1, D:] += jnp.sum(dzb, axis=0, keepdims=True)
        dyc_b = dyc.astype(BF16)
        dyp_b = dyp.astype(BF16)
        dyc_ref[...] = dyc_b
        dyp_ref[...] = dyp_b
        ds = _dot_nt(dyc_b, wm[:, 0].reshape(D, D))
        n, rl = _ln_stats(ac_ref[...])
        l = n * lg_ref[...] + lb_ref[...]
        sg = _sig(l)
        dl = ds * (sg * (1.0 + l * (1.0 - sg)))
        acc_ref[1:2, :D] += jnp.sum(dl * n, axis=0, keepdims=True)
        acc_ref[1:2, D:] += jnp.sum(dl, axis=0, keepdims=True)
        dn = dl * lg_ref[...]
        dac_ref[...] = rl * (dn - jnp.mean(dn, axis=-1, keepdims=True) - n * jnp.mean(dn * n, axis=-1, keepdims=True))
        dq = _dot_nt(dyp_b, wm[:, 1].reshape(D, D))
        mv = m_ref[...]
        acc_ref[2:3, :D] += jnp.sum(dq * _pool_mix(mv, wp), axis=0, keepdims=True)
        dm2 = (dq * ps_ref[...]).astype(BF16)
        dm2_ref[...] = dm2
        dm_ref[...] = jnp.concatenate(
            [_dot_nt(dm2[:, g * PG:(g + 1) * PG], wp[:, g].reshape(PG, PG)) for g in range(4)], axis=1)

        @pl.when(pl.program_id(0) == nt - 1)
        def _():
            ex.finish()

    def tile(col=0):
        return pl.BlockSpec((TMS, D), lambda i: (i, col))

    def vec(w):
        return pl.BlockSpec((1, w), lambda i: (0, 0))

    anys = pl.BlockSpec(memory_space=pl.ANY)
    f32o, b16o = jax.ShapeDtypeStruct((tp, D), F32), jax.ShapeDtypeStruct((tp, D), BF16)
    res = pl.pallas_call(
        body, name="mix_bwd", grid=(nt,),
        in_specs=[tile(), tile(3), tile(4), tile(), tile(), tile(), tile(), vec(2 * D), vec(D), vec(D), vec(D), anys, anys]
        + [anys] * nq,
        out_specs=[tile(), tile(), pl.BlockSpec((TMS, 2 * D), lambda i: (i, 0)), tile(), tile(), tile(),
                   pl.BlockSpec((8, 2 * D), lambda i: (0, 0))] + [anys] * nq,
        out_shape=[f32o, f32o, jax.ShapeDtypeStruct((tp, 2 * D), BF16), b16o, b16o, b16o,
                   jax.ShapeDtypeStruct((8, 2 * D), F32)] + ex.out_shape,
        scratch_shapes=[pltpu.VMEM((NDEV, 3, D // NDEV, D), BF16), pltpu.VMEM((NDEV, 4, PG // NDEV, PG), BF16),
                        pltpu.SemaphoreType.DMA((2,))] + ex.scratch,
        compiler_params=_params(("arbitrary",), 48),
    )(dh1, z, z, yc, yp, ac, m, b_gate, ln_g, ln_b, pool_scale, g_mixw, g_pool, *qs)
    return res[:7], res[7:]


def _seq_bwd(dac, dm, dzg, z, w_dw, seq, qs):
    tp = z.shape[0]
    nt = tp // TM
    ex = _ChipExchange(qs)
    nq = ex.n

    def body(*refs):
        dac_l, dac_c, dac_r, dm_l, dm_c, dm_r, av_l, av, av_r, ag_l, ag, ag_r, dzg_ref, w_ref = refs[:14]
        dz_ref, acc_ref = refs[14 + nq:16 + nq]
        a_ext, d_ext, m_ext, sha, shd, da_sc, dw_sc, w8 = refs[16 + 2 * nq:24 + 2 * nq]
        ex.bind(refs[14:14 + nq], refs[16 + nq:16 + 2 * nq], refs[24 + 2 * nq:])
        i = pl.program_id(0)

        @pl.when(i == 0)
        def _():
            ex.issue()
            dw_sc[...] = jnp.zeros_like(dw_sc)
            acc_ref[...] = jnp.zeros_like(acc_ref)
            _tap_rows(w_ref, w8)

        sg = _sig(ag[...])
        _fill_ext(a_ext, av_l[...] * _sig(ag_l[...]), av[...] * sg, av_r[...] * _sig(ag_r[...]))
        _fill_ext(d_ext, dac_l[...], dac_c[...], dac_r[...])
        for g, win in enumerate(POOL_WINDOWS):
            left = win // 2
            right = win - 1 - left
            cols = pl.ds(g * PG, PG)
            m_ext[pl.ds(0, HALO), cols] = dm_l[:, cols] / _pool_cnt(i, seq, tp, left, right, HALO, -HALO)
            m_ext[pl.ds(HALO, TM), cols] = dm_c[:, cols] / _pool_cnt(i, seq, tp, left, right, TM, 0)
            m_ext[pl.ds(HALO + TM, HALO), cols] = dm_r[:, cols] / _pool_cnt(i, seq, tp, left, right, HALO, TM)
        for c0 in range(0, D, CW):
            _shift_copies(a_ext, sha, c0)
            _shift_copies(d_ext, shd, c0)

            def rows(j, carry):
                base = pl.multiple_of(j * RB, RB)
                dcur = d_ext[pl.ds(pl.multiple_of(base + HALO, 8), RB), pl.ds(c0, CW)].reshape(RB // 8, 8, CW)
                acc = jnp.zeros((RB // 8, 8, CW), F32)
                for k in range(CONV_K):
                    q, r = divmod(CONV_K - k, 8)
                    slab = shd[r, pl.ds(pl.multiple_of(base + 8 * q, 8), RB), :].reshape(RB // 8, 8, CW)
                    acc = acc + slab * w8[k, :, pl.ds(c0, CW)]
                    q, r = divmod(k + 1, 8)
                    slab = sha[r, pl.ds(pl.multiple_of(base + 8 * q, 8), RB), :].reshape(RB // 8, 8, CW)
                    dw_sc[k, :, pl.ds(c0, CW)] += jnp.sum(dcur * slab, axis=0)
                da_sc[pl.ds(base, RB), pl.ds(c0, CW)] = acc.reshape(RB, CW)
                return carry

            lax.fori_loop(0, TM // RB, rows, 0)
        da = da_sc[...]
        dz_ref[:, 0:D] = (da * sg).astype(BF16)
        dz_ref[:, D:2 * D] = (da * av[...] * (sg * (1.0 - sg))).astype(BF16)
        for g, win in enumerate(POOL_WINDOWS):
            left = win // 2
            right = win - 1 - left
            cols = pl.ds(g * PG, PG)
            s = m_ext[pl.ds(HALO - right, TM), cols]
            for off in range(-right + 1, left + 1):
                s = s + m_ext[pl.ds(HALO + off, TM), cols]
            dz_ref[:, pl.ds(2 * D + g * PG, PG)] = (s - dm_c[:, cols]).astype(BF16)
        dz_ref[:, 3 * D:] = dzg_ref[...]

        @pl.when(i == nt - 1)
        def _():
            for k in range(CONV_K):
                acc_ref[k:k + 1, :] = jnp.sum(dw_sc[k], axis=0, keepdims=True)

        acc_ref[CONV_K:CONV_K + 1, :] += jnp.sum(dac_c[...], axis=0, keepdims=True)

        @pl.when(i == nt - 1)
        def _():
            ex.finish()

    ext = pltpu.VMEM((TM + 2 * HALO, D), F32)
    shs = pltpu.VMEM((8, TM + 24, CW), F32)
    anys = pl.BlockSpec(memory_space=pl.ANY)
    res = pl.pallas_call(
        body, name="seq_bwd", grid=(nt,),
        in_specs=_halo_specs(0, nt) + _halo_specs(0, nt) + _halo_specs(0, nt) + _halo_specs(1, nt)
        + [pl.BlockSpec((TM, 2 * D), lambda i: (i, 0)), pl.BlockSpec((32, D), lambda i: (0, 0))] + [anys] * nq,
        out_specs=[pl.BlockSpec((TM, DIN), lambda i: (i, 0)), pl.BlockSpec((32, D), lambda i: (0, 0))] + [anys] * nq,
        out_shape=[jax.ShapeDtypeStruct((tp, DIN), BF16), jax.ShapeDtypeStruct((32, D), F32)] + ex.out_shape,
        scratch_shapes=[ext, ext, ext, shs, shs, pltpu.VMEM((TM, D), F32), pltpu.VMEM((CONV_K, 8, D), F32),
                        pltpu.VMEM((CONV_K, 8, D), F32)] + ex.scratch,
        compiler_params=_params(("arbitrary",), 48),
    )(dac, dac, dac, dm, dm, dm, z, z, z, z, z, z, dzg, w_dw, *qs)
    return res[:2], res[2:]


def _in_bwd(dz, h0, dh1, g_mix, w_g, seq, qs):
    tp = h0.shape[0]
    tm = _pick(tp, TM_IO)
    nt = tp // tm
    ex = _ChipExchange(qs)
    nq = ex.n

    def body(*refs):
        dz_ref, h_ref, dh1_ref, g_ref, w_hbm = refs[:5]
        gx_ref, gmeta_ref, acc_ref = refs[5 + nq:8 + nq]
        w_vm, sems = refs[8 + 2 * nq:10 + 2 * nq]
        ex.bind(refs[5:5 + nq], refs[8 + nq:8 + 2 * nq], refs[10 + 2 * nq:])
        i = pl.program_id(0)

        @pl.when(i == 0)
        def _():
            ex.issue()
            acc_ref[...] = jnp.zeros_like(acc_ref)

        _load_once(i == 0, [(w_hbm, w_vm)], sems)

        du = _dot_nt(dz_ref[:, 0:INB], w_vm[0])
        for d in range(1, NDEV):
            du = du + _dot_nt(dz_ref[:, INB * d:INB * (d + 1)], w_vm[d])
        h = h_ref[...]
        r = lax.rsqrt(jnp.mean(h * h, axis=-1, keepdims=True) + RMS_EPS)
        n0 = h * r
        acc_ref[0:1, :] += jnp.sum(du * n0, axis=0, keepdims=True)
        dn = du * g_ref[...]
        gx_ref[...] = dh1_ref[...] + r * (dn - n0 * jnp.mean(dn * n0, axis=-1, keepdims=True))

        @pl.when(i == nt - 1)
        def _():
            gmeta_ref[...] = gx_ref[pl.ds(tm - N_META, N_META), :]
            ex.finish()

    tile = pl.BlockSpec((tm, D), lambda i: (i, 0))
    anys = pl.BlockSpec(memory_space=pl.ANY)
    res = pl.pallas_call(
        body, name="in_bwd", grid=(nt,),
        in_specs=[pl.BlockSpec((tm, DIN), lambda i: (i, 0)), tile, tile, pl.BlockSpec((1, D), lambda i: (0, 0)), anys]
        + [anys] * nq,
        out_specs=[tile, pl.BlockSpec((N_META, D), lambda i: (0, 0)), pl.BlockSpec((8, D), lambda i: (0, 0))] + [anys] * nq,
        out_shape=[jax.ShapeDtypeStruct((seq, D), F32), jax.ShapeDtypeStruct((N_META, D), F32),
                   jax.ShapeDtypeStruct((8, D), F32)] + ex.out_shape,
        scratch_shapes=[pltpu.VMEM((NDEV, D, INB), BF16), pltpu.SemaphoreType.DMA((1,))] + ex.scratch,
        compiler_params=_params(("arbitrary",), 58),
    )(dz, h0, dh1, g_mix, w_g, *qs)
    return res[:3], res[3:]


def _wgrad_in(u, dz):
    tp = u.shape[0]
    tm = _pick(tp, TM_WG)
    nt = tp // tm
    half = DIN // 2

    def body(u_ref, dz_ref, o_ref, acc):
        t = pl.program_id(1)

        @pl.when(t == 0)
        def _():
            acc[...] = jnp.zeros_like(acc)

        acc[...] += _dot_tn(u_ref[...], dz_ref[...])

        @pl.when(t == nt - 1)
        def _():
            for d in range(4):
                o_ref[d] = acc[:, INB * d:INB * (d + 1)].astype(BF16)

    return pl.pallas_call(
        body, name="wgrad_in", grid=(2, nt),
        in_specs=[pl.BlockSpec((tm, D), lambda h, t: (t, 0)), pl.BlockSpec((tm, half), lambda h, t: (t, h))],
        out_specs=pl.BlockSpec((4, D, INB), lambda h, t: (h, 0, 0)),
        out_shape=jax.ShapeDtypeStruct((NDEV, D, INB), BF16),
        scratch_shapes=[pltpu.VMEM((D, half), F32)],
        compiler_params=_params(("arbitrary", "arbitrary"), 48),
    )(u, dz)


def _wgrad_mix(s, dyc, q, dyp, merged, dh1, m, dm2):
    tp = s.shape[0]
    tm = _pick(tp, TM_WM)
    nt = tp // tm
    rb = D // NDEV

    def body(s_ref, dyc_ref, q_ref, dyp_ref, mg_ref, dh1_ref, m_ref, dm2_ref, o_ref, op_ref, acc, accp):
        t = pl.program_id(0)

        @pl.when(t == 0)
        def _():
            acc[...] = jnp.zeros_like(acc)
            accp[...] = jnp.zeros_like(accp)

        acc[0] += _dot_tn(s_ref[...], dyc_ref[...])
        acc[1] += _dot_tn(q_ref[...], dyp_ref[...])
        acc[2] += _dot_tn(mg_ref[...], dh1_ref[...].astype(BF16))
        for g in range(4):
            accp[g] += _dot_tn(m_ref[:, g * PG:(g + 1) * PG], dm2_ref[:, g * PG:(g + 1) * PG])

        @pl.when(t == nt - 1)
        def _():
            for d in range(NDEV):
                for k in range(3):
                    o_ref[d, k] = acc[k, rb * d:rb * (d + 1), :].astype(BF16)
                for g in range(4):
                    op_ref[d, g] = accp[g, 32 * d:32 * (d + 1), :].astype(BF16)

    tile = pl.BlockSpec((tm, D), lambda t: (t, 0))
    return pl.pallas_call(
        body, name="wgrad_mix", grid=(nt,),
        in_specs=[tile] * 8,
        out_specs=[pl.BlockSpec((NDEV, 3, rb, D), lambda t: (0, 0, 0, 0)),
                   pl.BlockSpec((NDEV, 4, 32, PG), lambda t: (0, 0, 0, 0))],
        out_shape=[jax.ShapeDtypeStruct((NDEV, 3, rb, D), BF16), jax.ShapeDtypeStruct((NDEV, 4, 32, PG), BF16)],
        scratch_shapes=[pltpu.VMEM((3, D, D), F32), pltpu.VMEM((4, PG, PG), F32)],
        compiler_params=_params(("arbitrary",), 56),
    )(s, dyc, q, dyp, merged, dh1, m, dm2)


def _wgrad_gu(v, dfg, dfu):
    tp = v.shape[0]
    tm = _pick(tp, TM_WG)
    nt = tp // tm

    def body(v_ref, dg_ref, du_ref, o_ref, acc):
        k, t = pl.program_id(0), pl.program_id(2)

        @pl.when(t == 0)
        def _():
            acc[...] = jnp.zeros_like(acc)

        @pl.when(k == 0)
        def _():
            acc[...] += _dot_tn(v_ref[...], dg_ref[...])

        @pl.when(k == 1)
        def _():
            acc[...] += _dot_tn(v_ref[...], du_ref[...])

        @pl.when(t == nt - 1)
        def _():
            for d in range(4):
                o_ref[d] = acc[:, pl.ds(FFB * d, FFB)].astype(BF16)

    return pl.pallas_call(
        body, name="wgrad_gu", grid=(2, 2, nt),
        in_specs=[pl.BlockSpec((tm, D), lambda k, h, t: (t, 0)),
                  pl.BlockSpec((tm, FFC), lambda k, h, t: (t * (1 - k), h * (1 - k))),
                  pl.BlockSpec((tm, FFC), lambda k, h, t: (t * k, h * k))],
        out_specs=pl.BlockSpec((4, None, D, FFB), lambda k, h, t: (h, k, 0, 0)),
        out_shape=jax.ShapeDtypeStruct((NDEV, 2, D, FFB), BF16),
        scratch_shapes=[pltpu.VMEM((D, FFC), F32)],
        compiler_params=_params(("arbitrary",) * 3, 40),
    )(v, dfg, dfu)


def _wgrad_down(f, dh2):
    tp = f.shape[0]
    tm = _pick(tp, TM_WG)
    nt = tp // tm

    def body(f_ref, d_ref, o_ref, acc):
        t = pl.program_id(1)

        @pl.when(t == 0)
        def _():
            acc[...] = jnp.zeros_like(acc)

        acc[...] += _dot_tn(f_ref[...], d_ref[...].astype(BF16))

        @pl.when(t == nt - 1)
        def _():
            for d in range(4):
                o_ref[d] = acc[FFB * d:FFB * (d + 1), :].astype(BF16)

    return pl.pallas_call(
        body, name="wgrad_down", grid=(2, nt),
        in_specs=[pl.BlockSpec((tm, FFC), lambda h, t: (t, h)), pl.BlockSpec((tm, D), lambda h, t: (t, 0))],
        out_specs=pl.BlockSpec((4, FFB, D), lambda h, t: (h, 0, 0)),
        out_shape=jax.ShapeDtypeStruct((NDEV, FFB, D), BF16),
        scratch_shapes=[pltpu.VMEM((FFC, D), F32)],
        compiler_params=_params(("arbitrary", "arbitrary"), 40),
    )(f, dh2)


def kernel(x, meta_tokens, g_mix, w_in, b_gate, w_dw, b_dw, ln_g, ln_b, w_conv_out, w_pool, pool_scale, w_pool_out, w_o, g_ffn, w_ffn_gate, w_ffn_up, w_ffn_down, g_final, loss_target, m_meta_tokens, m_g_mix, m_w_in, m_b_gate, m_w_dw, m_b_dw, m_ln_g, m_ln_b, m_w_conv_out, m_w_pool, m_pool_scale, m_w_pool_out, m_w_o, m_g_ffn, m_w_ffn_gate, m_w_ffn_up, m_w_ffn_down, m_g_final, v_meta_tokens, v_g_mix, v_w_in, v_b_gate, v_w_dw, v_b_dw, v_ln_g, v_ln_b, v_w_conv_out, v_w_pool, v_pool_scale, v_w_pool_out, v_w_o, v_g_ffn, v_w_ffn_gate, v_w_ffn_up, v_w_ffn_down, v_g_final):
    seq = x.shape[1]
    tp = -(-(seq + 2 * HALO) // TM) * TM
    tm_in = _pick(tp, TM_IO)
    nx_last = seq - (tp // tm_in - 1) * tm_in
    assert 0 < nx_last <= tm_in - 2 * HALO and nx_last % 8 == 0 and 0 < seq - (tp // TM - 1) * TM

    whole = (Ellipsis,)
    g_in, g_small = _all_gather(
        [((D, INB), [(w_in, whole, 0)]),
         ((48, D // NDEV), [(meta_tokens, pl.ds(0, N_META), whole), (w_dw, pl.ds(N_META, CONV_K), 0)])], [BF16, F32])
    ag_mix = _Gather([((3, D // NDEV, D), [(w_conv_out, 0, 0), (w_pool_out, 1, 0), (w_o, 2, 0)]),
                      ((4, PG // NDEV, PG), [(w_pool, whole, 0)])], [BF16, BF16])
    ag_ffn = _Gather([((2, D, FFB), [(w_ffn_gate, 0, 0), (w_ffn_up, 1, 0)]),
                      ((FFB, D), [(w_ffn_down, whole, 0)])], [BF16, BF16])
    small_full = g_small.transpose(1, 0, 2).reshape(48, D)
    wdw_full = small_full[N_META:]
    tail = jnp.concatenate([jnp.zeros((tm_in - nx_last - N_META, D), F32), small_full[:N_META]], axis=0)

    (h0, z, u), (g_mixw, g_pool) = _fwd_in(x[0], tail, g_mix, g_in, tp, ag_mix)
    (ac, m), (g_gu, g_down) = _seq_fwd(z, wdw_full, b_dw, seq, ag_ffn)
    w_gu = _repack_gu(g_gu)
    w_dn = g_down.reshape(2, FFC, D)
    h1, s, yc, yp, merged, q = _mix_fwd(ac, m, z, h0, b_gate, ln_g, ln_b, pool_scale, g_mixw, g_pool)
    fg, fu, v, f, dh2, head_acc = _ffn_fwd(h1, loss_target[0], g_ffn, g_final.reshape(1, D), w_gu, w_dn)

    dfg, dfu, dh1, ffn_acc = _ffn_bwd(dh2, fg, fu, h1, g_ffn, w_gu, w_dn)
    own_f, sib_f, q_f = _rs_pair("rs_pair_ffn", [_wgrad_gu(v, dfg, dfu), _wgrad_down(f, dh2)])
    (dac, dm, dzg, dyc, dyp, dm2, mix_acc), rel_f = _mix_bwd(
        dh1, z, yc, yp, ac, m, b_gate, ln_g, ln_b, pool_scale, g_mixw, g_pool, q_f)
    own_m, sib_m, q_m = _rs_pair("rs_pair_mix", list(_wgrad_mix(s, dyc, q, dyp, merged, dh1, m, dm2)))
    (dz, seq_acc), rel_m = _seq_bwd(dac, dm, dzg, z, wdw_full, seq, q_m)
    own_i, sib_i, q_i = _rs_pair("rs_pair_in", [_wgrad_in(u, dz)])
    (grad_x, g_meta, in_acc), rel_i = _in_bwd(dz, h0, dh1, g_mix, g_in, seq, q_i)
    small_g = jnp.concatenate([g_meta, seq_acc[:CONV_K], jnp.zeros((1, D), F32)], axis=0)
    p_small = small_g.reshape(48, NDEV, D // NDEV).transpose(1, 0, 2).astype(BF16)
    rep_g = jnp.concatenate([
        in_acc[0:1], mix_acc[0:1, :D], mix_acc[0:1, D:], seq_acc[CONV_K:CONV_K + 1], mix_acc[1:2, :D], mix_acc[1:2, D:],
        mix_acc[2:3, :D], ffn_acc[0:1], head_acc[1:2], head_acc[0:1], jnp.zeros((REP_ROWS - 10, D), F32)], axis=0)
    own_s, sib_s, rel_s, rep_all = _reduce_scatter([p_small], rep_g)
    owns = [own_i[0], own_s[0], own_m[0], own_m[1], own_f[0], own_f[1]]
    sibs = [sib_i[0], sib_s[0], sib_m[0], sib_m[1], sib_f[0], sib_f[1]]
    rels = [rel_i[0], rel_s[0], rel_m[0], rel_m[1], rel_f[0], rel_f[1]]

    def lead(a):
        return a.reshape(1, *a.shape)

    def stack4(a, lead_dims):
        return a.reshape(*lead_dims, 1, 4 * 32, PG)

    (r_in,) = _adamw_multi("adamw_in", lead(owns[0]), sibs[0][:, None], rels[0][:, None], [w_in], [m_w_in], [v_w_in], 4)
    r_meta, r_dw = _adamw_meta_dw(owns[1], sibs[1], rels[1], (meta_tokens, m_meta_tokens, v_meta_tokens),
                                  (w_dw, m_w_dw, v_w_dw))
    r_conv, r_pout, r_o = _adamw_multi("adamw_mix", owns[2], sibs[2], rels[2], [w_conv_out, w_pool_out, w_o],
                                       [m_w_conv_out, m_w_pool_out, m_w_o], [v_w_conv_out, v_w_pool_out, v_w_o], 1)
    (r_pool,) = _adamw_multi("adamw_pool", stack4(owns[3], ()), stack4(sibs[3], (4,)), stack4(rels[3], (3,)),
                             [w_pool.reshape(1, 128, PG)], [m_w_pool.reshape(1, 128, PG)], [v_w_pool.reshape(1, 128, PG)], 1)
    r_pool = tuple(a.reshape(w_pool.shape) for a in r_pool)
    r_gate, r_up = _adamw_multi("adamw_gu", owns[4], sibs[4], rels[4], [w_ffn_gate, w_ffn_up],
                                [m_w_ffn_gate, m_w_ffn_up], [v_w_ffn_gate, v_w_ffn_up], 4)
    (r_down,) = _adamw_multi("adamw_down", lead(owns[5]), sibs[5][:, None], rels[5][:, None],
                             [w_ffn_down], [m_w_ffn_down], [v_w_ffn_down], 2)
    row = (1, D)
    loss, reps = _adamw_rep(
        rep_all,
        [g_mix, b_gate, b_dw, ln_g, ln_b, pool_scale, g_ffn, g_final.reshape(row)],
        [m_g_mix, m_b_gate, m_b_dw, m_ln_g, m_ln_b, m_pool_scale, m_g_ffn, m_g_final.reshape(row)],
        [v_g_mix, v_b_gate, v_b_dw, v_ln_g, v_ln_b, v_pool_scale, v_g_ffn, v_g_final.reshape(row)])
    r_gmix, r_bg, r_bdw, r_lg, r_lb, r_ps, r_gffn, r_gfin = reps
    r_gfin = tuple(a.reshape(D) for a in r_gfin)

    in_order = [r_meta, r_gmix, r_in, r_bg, r_dw, r_bdw, r_lg, r_lb, r_conv, r_pool, r_ps, r_pout, r_o, r_gffn,
                r_gate, r_up, r_down, r_gfin]
    return (loss.reshape(()), grad_x[None], *[r[0] for r in in_order], *[r[1] for r in in_order],
            *[r[2] for r in in_order], *[r[3] for r in in_order])
```

```python
import math

import jax
import jax.numpy as jnp
from jax import lax
from jax.experimental import pallas as pl
from jax.experimental.pallas import tpu as pltpu

F32, BF16 = jnp.float32, jnp.bfloat16
MESH_ID = pl.DeviceIdType.MESH
NDEV = 8

D = 1024
N_META = 16
CONV_K = 31
HALO = 16
POOL_WINDOWS = (2, 4, 8, 16)
PG = 256
DIN = 5 * D
DFF = 2816
FFB = DFF // NDEV
FFC = DFF // 2
INB = DIN // NDEV
RMS_EPS = 1e-6
LN_EPS = 1e-5
ADAM_LR, ADAM_B1, ADAM_B2, ADAM_EPS, ADAM_WD, ADAM_STEP = 0.001, 0.9, 0.999, 1e-08, 0.01, 10

TM = 384
TMS = 192
TM_IO = 704
TM_WG = 1408
TM_WM = 704
RB, CW = 64, 128
MIB = 2 ** 20


def _sig(x):
    return 1.0 / (1.0 + jnp.exp(-x))


def _dot(a, b):
    return jnp.dot(a, b, preferred_element_type=F32)


def _dot_nt(a, b):
    return lax.dot_general(a, b, (((1,), (1,)), ((), ())), preferred_element_type=F32)


def _dot_tn(a, b):
    return lax.dot_general(a, b, (((0,), (0,)), ((), ())), preferred_element_type=F32)


def _pick(tp, pref):
    return pref if tp % pref == 0 else TM


def _params(sem, vmem_mib):
    return pltpu.CompilerParams(dimension_semantics=sem, vmem_limit_bytes=vmem_mib * MIB)


def _load_once(first, pairs, sems):
    @pl.when(first)
    def _():
        cps = [pltpu.make_async_copy(s, d, sems.at[k]) for k, (s, d) in enumerate(pairs)]
        for cp in cps:
            cp.start()
        for cp in cps:
            cp.wait()


def _place():
    x, y, c = lax.axis_index("x"), lax.axis_index("y"), lax.axis_index("c")
    return x, y, c


class _Gather:
    def __init__(self, groups, dtypes):
        self.groups, self.dtypes, self.n = groups, dtypes, len(groups)
        self.arrays = [a for _, parts in groups for a, _, _ in parts]
        self.out_shape = [jax.ShapeDtypeStruct((NDEV, *s), dt) for (s, _), dt in zip(groups, dtypes)]
        self.scratch = [pltpu.VMEM(s, dt) for (s, _), dt in zip(groups, dtypes)] + [
            pltpu.SemaphoreType.DMA((7 * self.n,)), pltpu.SemaphoreType.DMA((7 * self.n,)),
            pltpu.SemaphoreType.DMA((self.n,))]

    def bind(self, ins, outs, scratch):
        self.ins, self.outs, self.stages = ins, outs, scratch[:self.n]
        self.send_sems, self.recv_sems, self.local_sems = scratch[self.n:]
        return self

    def _copy(self, w, k, block, to, src=None):
        dst = self.outs[w].at[4 * block[0] + 2 * block[1] + block[2]]
        return pltpu.make_async_remote_copy(
            src_ref=dst if src is None else src, dst_ref=dst,
            send_sem=self.send_sems.at[7 * w + k], recv_sem=self.recv_sems.at[7 * w + k],
            device_id=to, device_id_type=MESH_ID)

    def _first(self):
        x, y, c = _place()
        me, sibling = (x, y, c), (x, y, 1 - c)
        chips = [(1 - x, y), (x, 1 - y), (1 - x, 1 - y)]
        mine, first = [], []
        for w in range(self.n):
            mine.append(pltpu.make_async_copy(self.stages[w], self.outs[w].at[4 * x + 2 * y + c], self.local_sems.at[w]))
            first.append(self._copy(w, 0, me, sibling, src=self.stages[w]))
            first += [self._copy(w, 1 + j, me, (*chip, c), src=self.stages[w]) for j, chip in enumerate(chips)]
        return mine, first

    def _passed(self):
        x, y, c = _place()
        chips = [(1 - x, y), (x, 1 - y), (1 - x, 1 - y)]
        return [self._copy(w, 4 + j, (*chip, c), (x, y, 1 - c)) for w in range(self.n) for j, chip in enumerate(chips)]

    def issue(self):
        a = 0
        for w in range(self.n):
            shape, parts = self.groups[w]
            if sum(arr.size for arr, _, _ in parts) < math.prod(shape):
                self.stages[w][...] = jnp.zeros(shape, self.dtypes[w])
            for _, dst, src in parts:
                self.stages[w][dst] = self.ins[a][src].astype(self.dtypes[w])
                a += 1
        mine, first = self._first()
        for cp in mine + first:
            cp.start()

    def forward(self):
        x, y, c = _place()
        chips = [(1 - x, y), (x, 1 - y), (1 - x, 1 - y)]
        passed = self._passed()
        for w in range(self.n):
            for j, chip in enumerate(chips):
                self._copy(w, 1 + j, (*chip, c), (x, y, c)).wait_recv()
                passed[3 * w + j].start()

    def finish(self):
        x, y, c = _place()
        chips = [(1 - x, y), (x, 1 - y), (1 - x, 1 - y)]
        for w in range(self.n):
            self._copy(w, 0, (x, y, 1 - c), (x, y, c)).wait_recv()
            for j, chip in enumerate(chips):
                self._copy(w, 4 + j, (*chip, 1 - c), (x, y, c)).wait_recv()
        mine, first = self._first()
        for cp in first + self._passed():
            cp.wait_send()
        for cp in mine:
            cp.wait()


def _all_gather(groups, dtypes):
    ag = _Gather(groups, dtypes)
    na, n = len(ag.arrays), ag.n

    def body(*refs):
        ag.bind(refs[:na], refs[na:na + n], refs[na + n:])
        ag.issue()
        ag.forward()
        ag.finish()

    return pl.pallas_call(
        body, name="ag_weights", out_shape=ag.out_shape,
        in_specs=[pl.BlockSpec(memory_space=pltpu.VMEM)] * na,
        out_specs=[pl.BlockSpec(memory_space=pl.ANY)] * n,
        scratch_shapes=ag.scratch,
        compiler_params=pltpu.CompilerParams(vmem_limit_bytes=40 * MIB),
    )(*ag.arrays)


class _ChipExchange:
    def __init__(self, qs):
        self.n = len(qs)
        self.out_shape = [jax.ShapeDtypeStruct(q.shape, q.dtype) for q in qs]
        self.scratch = [pltpu.SemaphoreType.DMA((3 * self.n,)), pltpu.SemaphoreType.DMA((3 * self.n,))]

    def bind(self, qs, rels, scratch):
        self.qs, self.rels = qs, rels
        self.send_sems, self.recv_sems = scratch
        return self

    def _copies(self):
        x, y, c = _place()
        chips = [(1 - x, y), (x, 1 - y), (1 - x, 1 - y)]
        return [pltpu.make_async_remote_copy(
            src_ref=self.qs[w].at[j], dst_ref=self.rels[w].at[j],
            send_sem=self.send_sems.at[3 * w + j], recv_sem=self.recv_sems.at[3 * w + j],
            device_id=(*chips[j], c), device_id_type=MESH_ID) for w in range(self.n) for j in range(3)]

    def issue(self):
        for cp in self._copies():
            cp.start()

    def finish(self):
        cps = self._copies()
        for cp in cps:
            cp.wait_recv()
        for cp in cps:
            cp.wait_send()


def _reduce_scatter(parts, small):
    n = len(parts)
    blks = [p.shape[1:] for p in parts]

    def body(*refs):
        ps, small_ref = refs[:n], refs[n]
        o = n + 1
        owns, sibs, rels, small_out = refs[o:o + n], refs[o + n:o + 2 * n], refs[o + 2 * n:o + 3 * n], refs[o + 3 * n]
        o += 3 * n + 1
        pa, pb, qst = refs[o:o + n], refs[o + n:o + 2 * n], refs[o + 2 * n:o + 3 * n]
        s1_send, s1_recv, s2_send, s2_recv, sm_send, sm_recv, lsem = refs[o + 3 * n:]
        x, y, c = _place()
        me = 4 * x + 2 * y + c
        sibling = (x, y, 1 - c)
        chips = [(1 - x, y), (x, 1 - y), (1 - x, 1 - y)]
        all_chips = [(x, y)] + chips

        own_cps = []
        for w in range(n):
            cp = pltpu.make_async_copy(ps[w].at[me], owns[w], lsem.at[w])
            cp.start()
            own_cps.append(cp)
        sm_own = pltpu.make_async_copy(small_ref, small_out.at[me], lsem.at[n])
        sm_own.start()

        def small_copy(r):
            peer = ((x + (r >> 2)) % 2, (y + ((r >> 1) & 1)) % 2, (c + (r & 1)) % 2)
            return pltpu.make_async_remote_copy(
                src_ref=small_ref, dst_ref=small_out.at[me], send_sem=sm_send.at[r - 1], recv_sem=sm_recv.at[r - 1],
                device_id=peer, device_id_type=MESH_ID)

        sm_cps = [small_copy(r) for r in range(1, NDEV)]
        for cp in sm_cps:
            cp.start()

        def pair_copy(w, rel):
            cx, cy = all_chips[rel]
            return pltpu.make_async_remote_copy(
                src_ref=ps[w].at[4 * cx + 2 * cy + (1 - c)], dst_ref=sibs[w].at[rel],
                send_sem=s1_send.at[4 * w + rel], recv_sem=s1_recv.at[4 * w + rel],
                device_id=sibling, device_id_type=MESH_ID)

        def chip_copy(w, j):
            return pltpu.make_async_remote_copy(
                src_ref=qst[w].at[j], dst_ref=rels[w].at[j],
                send_sem=s2_send.at[3 * w + j], recv_sem=s2_recv.at[3 * w + j],
                device_id=(*chips[j], c), device_id_type=MESH_ID)

        pair_cps = [pair_copy(w, rel) for w in range(n) for rel in (1, 2, 3, 0)]
        for cp in pair_cps:
            cp.start()
        chip_cps = []
        for w in range(n):
            for j, (cx, cy) in enumerate(chips):
                pair_copy(w, 1 + j).wait_recv()
                la = pltpu.make_async_copy(ps[w].at[4 * cx + 2 * cy + c], pa[w], lsem.at[n + 1])
                lb = pltpu.make_async_copy(sibs[w].at[1 + j], pb[w], lsem.at[n + 2])
                la.start()
                lb.start()
                la.wait()
                lb.wait()
                qst[w][j] = (pa[w][...].astype(F32) + pb[w][...].astype(F32)).astype(BF16)
                cp = chip_copy(w, j)
                cp.start()
                chip_cps.append(cp)
        for w in range(n):
            pair_copy(w, 0).wait_recv()
            for j in range(3):
                chip_copy(w, j).wait_recv()
        for cp in sm_cps:
            cp.wait_recv()
        for cp in pair_cps + chip_cps + sm_cps:
            cp.wait_send()
        for cp in own_cps:
            cp.wait()
        sm_own.wait()

    any_spec = pl.BlockSpec(memory_space=pl.ANY)
    outs = pl.pallas_call(
        body, name="rs_grads",
        out_shape=[jax.ShapeDtypeStruct(b, BF16) for b in blks]
        + [jax.ShapeDtypeStruct((4, *b), BF16) for b in blks]
        + [jax.ShapeDtypeStruct((3, *b), BF16) for b in blks]
        + [jax.ShapeDtypeStruct((NDEV, *small.shape), F32)],
        in_specs=[any_spec] * (n + 1),
        out_specs=[any_spec] * (3 * n + 1),
        scratch_shapes=[pltpu.VMEM(b, BF16) for b in blks] + [pltpu.VMEM(b, BF16) for b in blks]
        + [pltpu.VMEM((3, *b), BF16) for b in blks]
        + [pltpu.SemaphoreType.DMA((4 * n,)), pltpu.SemaphoreType.DMA((4 * n,)),
           pltpu.SemaphoreType.DMA((3 * n,)), pltpu.SemaphoreType.DMA((3 * n,)),
           pltpu.SemaphoreType.DMA((NDEV - 1,)), pltpu.SemaphoreType.DMA((NDEV - 1,)),
           pltpu.SemaphoreType.DMA((n + 3,))],
        compiler_params=pltpu.CompilerParams(vmem_limit_bytes=40 * MIB),
    )(*parts, small)
    return outs[:n], outs[n:2 * n], outs[2 * n:3 * n], outs[3 * n]


def _rs_pair(name, parts):
    n = len(parts)
    blks = [p.shape[1:] for p in parts]

    def body(*refs):
        ps = refs[:n]
        owns, sibs, qs = refs[n:2 * n], refs[2 * n:3 * n], refs[3 * n:4 * n]
        pa, pb, qst = refs[4 * n:5 * n], refs[5 * n:6 * n], refs[6 * n:7 * n]
        s_send, s_recv, lsem = refs[7 * n:]
        x, y, c = _place()
        chips = [(1 - x, y), (x, 1 - y), (1 - x, 1 - y)]
        all_chips = [(x, y)] + chips

        own_cps = [pltpu.make_async_copy(ps[w].at[4 * x + 2 * y + c], owns[w], lsem.at[w]) for w in range(n)]
        for cp in own_cps:
            cp.start()

        def pair_copy(w, rel):
            cx, cy = all_chips[rel]
            return pltpu.make_async_remote_copy(
                src_ref=ps[w].at[4 * cx + 2 * cy + (1 - c)], dst_ref=sibs[w].at[rel],
                send_sem=s_send.at[4 * w + rel], recv_sem=s_recv.at[4 * w + rel],
                device_id=(x, y, 1 - c), device_id_type=MESH_ID)

        pair_cps = [pair_copy(w, rel) for w in range(n) for rel in (1, 2, 3, 0)]
        for cp in pair_cps:
            cp.start()
        q_cps = []
        for w in range(n):
            for j, (cx, cy) in enumerate(chips):
                la = pltpu.make_async_copy(ps[w].at[4 * cx + 2 * cy + c], pa[w], lsem.at[n])
                lb = pltpu.make_async_copy(sibs[w].at[1 + j], pb[w], lsem.at[n + 1])
                la.start()
                pair_copy(w, 1 + j).wait_recv()
                lb.start()
                la.wait()
                lb.wait()
                qst[w][j] = (pa[w][...].astype(F32) + pb[w][...].astype(F32)).astype(BF16)
            cp = pltpu.make_async_copy(qst[w], qs[w], lsem.at[n + 2 + w])
            cp.start()
            q_cps.append(cp)
        for w in range(n):
            pair_copy(w, 0).wait_recv()
        for cp in pair_cps:
            cp.wait_send()
        for cp in own_cps + q_cps:
            cp.wait()

    any_spec = pl.BlockSpec(memory_space=pl.ANY)
    outs = pl.pallas_call(
        body, name=name,
        out_shape=[jax.ShapeDtypeStruct(b, BF16) for b in blks]
        + [jax.ShapeDtypeStruct((4, *b), BF16) for b in blks]
        + [jax.ShapeDtypeStruct((3, *b), BF16) for b in blks],
        in_specs=[any_spec] * n,
        out_specs=[any_spec] * (3 * n),
        scratch_shapes=[pltpu.VMEM(b, BF16) for b in blks] + [pltpu.VMEM(b, BF16) for b in blks]
        + [pltpu.VMEM((3, *b), BF16) for b in blks]
        + [pltpu.SemaphoreType.DMA((4 * n,)), pltpu.SemaphoreType.DMA((4 * n,)), pltpu.SemaphoreType.DMA((2 * n + 2,))],
        compiler_params=pltpu.CompilerParams(vmem_limit_bytes=40 * MIB),
    )(*parts)
    return outs[:n], outs[n:2 * n], outs[2 * n:3 * n]


def _adamw_math(g, w, m, v):
    m = ADAM_B1 * m + (1.0 - ADAM_B1) * g
    v = ADAM_B2 * v + (1.0 - ADAM_B2) * (g * g)
    m_hat = m / (1.0 - ADAM_B1 ** ADAM_STEP)
    v_hat = v / (1.0 - ADAM_B2 ** ADAM_STEP)
    delta = -ADAM_LR * (m_hat / (jnp.sqrt(v_hat) + ADAM_EPS) + ADAM_WD * w)
    return delta, m, v


def _adamw_multi(name, own, sib, rel, ws, ms, vs, row_grid):
    k_n, r_n, c_n = own.shape
    rbk = r_n // row_grid

    def body(*refs):
        own_ref, sib_ref, r0_ref, r1_ref, r2_ref = refs[:5]
        w_refs, m_refs, v_refs = refs[5:5 + k_n], refs[5 + k_n:5 + 2 * k_n], refs[5 + 2 * k_n:5 + 3 * k_n]
        outs = refs[5 + 3 * k_n:]
        for k in range(k_n):
            g = own_ref[k].astype(F32) + sib_ref[k].astype(F32)
            g = g + r0_ref[k].astype(F32)
            g = g + r1_ref[k].astype(F32)
            g = g + r2_ref[k].astype(F32)
            delta, mm, vv = _adamw_math(g, w_refs[k][0], m_refs[k][0], v_refs[k][0])
            outs[4 * k][0] = g
            outs[4 * k + 1][0] = delta
            outs[4 * k + 2][0] = mm
            outs[4 * k + 3][0] = vv

    def lead(j):
        return pl.BlockSpec((None, k_n, rbk, c_n), lambda g: (j, 0, g, 0))

    wspec = pl.BlockSpec((1, rbk, c_n), lambda g: (0, g, 0))
    shp = jax.ShapeDtypeStruct((1, r_n, c_n), F32)
    res = pl.pallas_call(
        body, name=name, grid=(row_grid,),
        in_specs=[pl.BlockSpec((k_n, rbk, c_n), lambda g: (0, g, 0)), lead(0), lead(0), lead(1), lead(2)] + [wspec] * (3 * k_n),
        out_specs=[wspec] * (4 * k_n), out_shape=[shp] * (4 * k_n),
        compiler_params=_params(("arbitrary",), 40),
    )(own, sib, rel, rel, rel, *ws, *ms, *vs)
    return [tuple(res[4 * k:4 * k + 4]) for k in range(k_n)]


def _adamw_meta_dw(own, sib, rel, meta, dw):
    def body(own_ref, sib_ref, rel_ref, wm, mm, vm, wd, md, vd, *outs):
        def gsum(rows):
            g = own_ref[rows, :].astype(F32) + sib_ref[0, rows, :].astype(F32)
            for j in range(3):
                g = g + rel_ref[j, rows, :].astype(F32)
            return g

        g = gsum(pl.ds(0, N_META))
        delta, m2, v2 = _adamw_math(g, wm[...], mm[...], vm[...])
        for o, val in zip(outs[:4], (g, delta, m2, v2)):
            o[...] = val
        g = gsum(pl.ds(N_META, CONV_K))
        delta, m2, v2 = _adamw_math(g, wd[0], md[0], vd[0])
        for o, val in zip(outs[4:], (g, delta, m2, v2)):
            o[0] = val

    s_meta = jax.ShapeDtypeStruct(meta[0].shape, F32)
    s_dw = jax.ShapeDtypeStruct(dw[0].shape, F32)
    res = pl.pallas_call(body, name="adamw_meta_dw", out_shape=[s_meta] * 4 + [s_dw] * 4)(own, sib, rel, *meta, *dw)
    return tuple(res[:4]), tuple(res[4:])


REP_ROWS = 16


def _adamw_rep(gathered, ws, ms, vs):
    rows = [(0, 1), (1, 2), (3, 1), (4, 1), (5, 1), (6, 1), (7, 1), (8, 1)]

    def body(g_ref, *refs):
        w_refs, m_refs, v_refs = refs[:8], refs[8:16], refs[16:24]
        loss_ref, outs, acc = refs[24], refs[25:57], refs[57]
        g = g_ref[0]
        for d in range(1, NDEV):
            g = g + g_ref[d]
        acc[...] = g
        loss_ref[...] = (0.5 / D) * jnp.sum(acc[pl.ds(9, 1), :], axis=1, keepdims=True)
        for p, (r0, nr) in enumerate(rows):
            for h in range(nr):
                cols = pl.ds(h * D, D)
                gp = acc[pl.ds(r0 + h, 1), :]
                delta, mm, vv = _adamw_math(gp, w_refs[p][:, cols], m_refs[p][:, cols], v_refs[p][:, cols])
                for o, val in zip(outs[4 * p:4 * p + 4], (gp, delta, mm, vv)):
                    o[:, cols] = val

    shapes = [jax.ShapeDtypeStruct(w.shape, F32) for w in ws]
    res = pl.pallas_call(
        body, name="adamw_rep",
        out_shape=[jax.ShapeDtypeStruct((1, 1), F32)] + [s for s in shapes for _ in range(4)],
        scratch_shapes=[pltpu.VMEM((REP_ROWS, D), F32)],
    )(gathered, *ws, *ms, *vs)
    return res[0], [tuple(res[1 + 4 * p:5 + 4 * p]) for p in range(8)]


def _repack_gu(g_gu):
    def body(x_ref, o_ref):
        for i in range(2):
            for d in range(NDEV):
                o_ref[i, d // 4, :, pl.ds(FFB * (d % 4), FFB)] = x_ref[d, i]

    return pl.pallas_call(body, name="repack_gu", out_shape=jax.ShapeDtypeStruct((2, 2, D, FFC), BF16),
                          compiler_params=pltpu.CompilerParams(vmem_limit_bytes=40 * MIB))(g_gu)


def _win_pairs(w_hbm, w_vm):
    half = NDEV // 2
    return [(w_hbm.at[d], w_vm.at[d // half, :, pl.ds(INB * (d % half), INB)]) for d in range(NDEV)]


def _whole(a):
    nd = a.ndim
    return pl.BlockSpec(a.shape, lambda *g: (0,) * nd)


def _fwd_in(x2, tail, g_mix, w_g, tp, ag):
    tm = _pick(tp, TM_IO)
    nt = tp // tm
    nx_last = tm - tail.shape[0]
    na, ng = len(ag.arrays), ag.n
    half = NDEV // 2

    def body(*refs):
        x_ref, tail_ref, g_ref, w_hbm = refs[:4]
        h_ref, z_ref, u_ref = refs[4 + na:7 + na]
        w_vm, sems = refs[7 + na + ng:9 + na + ng]
        ag.bind(refs[4:4 + na], refs[7 + na:7 + na + ng], refs[9 + na + ng:])
        i, j = pl.program_id(0), pl.program_id(1)
        first = (i == 0) & (j == 0)

        @pl.when(first)
        def _():
            ag.issue()

        @pl.when((i == max(nt - 2, 0)) & (j == 0))
        def _():
            ag.forward()

        _load_once(first, _win_pairs(w_hbm, w_vm), sems)

        @pl.when((j == 0) & (i < nt - 1))
        def _():
            h_ref[...] = x_ref[...]

        @pl.when((j == 0) & (i == nt - 1))
        def _():
            h_ref[pl.ds(0, nx_last), :] = x_ref[pl.ds(0, nx_last), :]
            h_ref[pl.ds(nx_last, tm - nx_last), :] = tail_ref[...]

        @pl.when(j == 0)
        def _():
            xv = h_ref[...]
            r = lax.rsqrt(jnp.mean(xv * xv, axis=-1, keepdims=True) + RMS_EPS)
            u_ref[...] = (xv * r * g_ref[...]).astype(BF16)

        z_ref[...] = _dot(u_ref[...], w_vm[j])

        @pl.when((i == nt - 1) & (j == 1))
        def _():
            ag.finish()

    tile = pl.BlockSpec((tm, D), lambda i, j: (i, 0))
    res = pl.pallas_call(
        body, name="fwd_in", grid=(nt, 2),
        in_specs=[tile, pl.BlockSpec(tail.shape, lambda i, j: (0, 0)), pl.BlockSpec((1, D), lambda i, j: (0, 0)),
                  pl.BlockSpec(memory_space=pl.ANY)] + [_whole(a) for a in ag.arrays],
        out_specs=[tile, pl.BlockSpec((tm, DIN // 2), lambda i, j: (i, j)), tile] + [pl.BlockSpec(memory_space=pl.ANY)] * ng,
        out_shape=[jax.ShapeDtypeStruct((tp, D), F32), jax.ShapeDtypeStruct((tp, DIN), F32),
                   jax.ShapeDtypeStruct((tp, D), BF16)] + ag.out_shape,
        scratch_shapes=[pltpu.VMEM((2, D, DIN // 2), BF16), pltpu.SemaphoreType.DMA((NDEV,))] + ag.scratch,
        compiler_params=_params(("arbitrary", "arbitrary"), 56),
    )(x2, tail, g_mix, w_g, *ag.arrays)
    return res[:3], res[3:]


def _halo_specs(col, nt, width=D):
    r = TM // HALO
    nb = nt * r
    return [pl.BlockSpec((HALO, width), lambda i: ((i * r + nb - 1) % nb, col)),
            pl.BlockSpec((TM, width), lambda i: (i, col)),
            pl.BlockSpec((HALO, width), lambda i: (((i + 1) * r) % nb, col))]


def _fill_ext(ext_ref, left, cur, right):
    ext_ref[pl.ds(0, HALO), :] = left
    ext_ref[pl.ds(HALO, TM), :] = cur
    ext_ref[pl.ds(HALO + TM, HALO), :] = right


def _shift_copies(ext_ref, sh_ref, c0):
    for r in range(8):
        sh_ref[r] = ext_ref[pl.ds(r, TM + 24), pl.ds(c0, CW)]


def _tap_rows(w_ref, w8):
    for k in range(CONV_K):
        w8[k] = jnp.broadcast_to(w_ref[pl.ds(k, 1), :], (8, D))


def _pool_cnt(i, seq, tp, left, right, rows, row0):
    b = i * TM + row0 + lax.broadcasted_iota(jnp.int32, (rows, 1), 0)
    b = jnp.where(b < 0, b + tp, b)
    b = jnp.where(b >= tp, b - tp, b)
    t = jnp.where(b < seq, b + N_META, b - (tp - N_META))
    lo = jnp.maximum(t - left, 0)
    hi = jnp.minimum(t + right + 1, seq + N_META)
    return jnp.maximum(hi - lo, 1).astype(F32)


def _seq_fwd(z, w_dw, b_dw, seq, gat):
    tp = z.shape[0]
    nt = tp // TM
    na, ng = len(gat.arrays), gat.n

    def body(*refs):
        av_l, av, av_r, ag_l, ag, ag_r, p_l, p, p_r, w_ref, b_ref = refs[:11]
        ac_ref, m_ref = refs[11 + na:13 + na]
        a_ext, p_ext, sh, w8 = refs[13 + na + ng:17 + na + ng]
        gat.bind(refs[11:11 + na], refs[13 + na:13 + na + ng], refs[17 + na + ng:])
        i = pl.program_id(0)

        @pl.when(i == 0)
        def _():
            gat.issue()
            _tap_rows(w_ref, w8)

        @pl.when(i == max(nt - 2, 0))
        def _():
            gat.forward()

        _fill_ext(a_ext, av_l[...] * _sig(ag_l[...]), av[...] * _sig(ag[...]), av_r[...] * _sig(ag_r[...]))
        _fill_ext(p_ext, p_l[...], p[...], p_r[...])
        for c0 in range(0, D, CW):
            _shift_copies(a_ext, sh, c0)

            def rows(j, carry):
                base = pl.multiple_of(j * RB, RB)
                acc = jnp.broadcast_to(b_ref[:, pl.ds(c0, CW)], (RB // 8, 8, CW))
                for k in range(CONV_K):
                    q, r = divmod(k + 1, 8)
                    slab = sh[r, pl.ds(pl.multiple_of(base + 8 * q, 8), RB), :].reshape(RB // 8, 8, CW)
                    acc = acc + slab * w8[k, :, pl.ds(c0, CW)]
                ac_ref[pl.ds(base, RB), pl.ds(c0, CW)] = acc.reshape(RB, CW)
                return carry

            lax.fori_loop(0, TM // RB, rows, 0)
        for g, win in enumerate(POOL_WINDOWS):
            left = win // 2
            right = win - 1 - left
            cols = pl.ds(g * PG, PG)
            s = p_ext[pl.ds(HALO - left, TM), cols]
            for off in range(-left + 1, right + 1):
                s = s + p_ext[pl.ds(HALO + off, TM), cols]
            cnt = _pool_cnt(i, seq, tp, left, right, TM, 0)
            m_ref[:, cols] = (s / cnt - p_ext[pl.ds(HALO, TM), cols]).astype(BF16)

        @pl.when(i == nt - 1)
        def _():
            gat.finish()

    res = pl.pallas_call(
        body, name="seq_fwd", grid=(nt,),
        in_specs=_halo_specs(0, nt) + _halo_specs(1, nt) + _halo_specs(2, nt)
        + [pl.BlockSpec((32, D), lambda i: (0, 0)), pl.BlockSpec((1, D), lambda i: (0, 0))] + [_whole(a) for a in gat.arrays],
        out_specs=[pl.BlockSpec((TM, D), lambda i: (i, 0))] * 2 + [pl.BlockSpec(memory_space=pl.ANY)] * ng,
        out_shape=[jax.ShapeDtypeStruct((tp, D), F32), jax.ShapeDtypeStruct((tp, D), BF16)] + gat.out_shape,
        scratch_shapes=[pltpu.VMEM((TM + 2 * HALO, D), F32), pltpu.VMEM((TM + 2 * HALO, D), F32),
                        pltpu.VMEM((8, TM + 24, CW), F32), pltpu.VMEM((CONV_K, 8, D), F32)] + gat.scratch,
        compiler_params=_params(("arbitrary",), 52),
    )(z, z, z, z, z, z, z, z, z, w_dw, b_dw, *gat.arrays)
    return res[:2], res[2:]


def _ln_stats(ac):
    mu = jnp.mean(ac, axis=-1, keepdims=True)
    xc = ac - mu
    rl = lax.rsqrt(jnp.mean(xc * xc, axis=-1, keepdims=True) + LN_EPS)
    return xc * rl, rl


def _pool_mix(m, wp_ref):
    return jnp.concatenate(
        [_dot(m[:, g * PG:(g + 1) * PG], wp_ref[:, g].reshape(PG, PG)) for g in range(4)], axis=1)


def _mix_fwd(ac, m, z, h0, b_gate, ln_g, ln_b, pool_scale, g_mixw, g_pool):
    tp = h0.shape[0]
    nt = tp // TMS

    def body(ac_ref, m_ref, zga, zgb, h_ref, bg_ref, lg_ref, lb_ref, ps_ref, wm_hbm, wp_hbm,
             h1_ref, s_ref, yc_ref, yp_ref, mg_ref, q_ref, wm, wp, sems):
        _load_once(pl.program_id(0) == 0, [(wm_hbm, wm), (wp_hbm, wp)], sems)
        n, _ = _ln_stats(ac_ref[...])
        l = n * lg_ref[...] + lb_ref[...]
        s = (l * _sig(l)).astype(BF16)
        s_ref[...] = s
        yc = _dot(s, wm[:, 0].reshape(D, D))
        q = (_pool_mix(m_ref[...], wp) * ps_ref[...]).astype(BF16)
        q_ref[...] = q
        yp = _dot(q, wm[:, 1].reshape(D, D))
        ga = _sig(zga[...] + bg_ref[:, :D])
        gb = _sig(zgb[...] + bg_ref[:, D:])
        merged = (ga * yc + gb * yp).astype(BF16)
        yc_ref[...] = yc
        yp_ref[...] = yp
        mg_ref[...] = merged
        h1_ref[...] = h_ref[...] + _dot(merged, wm[:, 2].reshape(D, D))

    def tile(col=0):
        return pl.BlockSpec((TMS, D), lambda i: (i, col))

    def vec(w):
        return pl.BlockSpec((1, w), lambda i: (0, 0))

    anys = pl.BlockSpec(memory_space=pl.ANY)
    f32o, b16o = jax.ShapeDtypeStruct((tp, D), F32), jax.ShapeDtypeStruct((tp, D), BF16)
    return pl.pallas_call(
        body, name="mix_fwd", grid=(nt,),
        in_specs=[tile(), tile(), tile(3), tile(4), tile(), vec(2 * D), vec(D), vec(D), vec(D), anys, anys],
        out_specs=[tile()] * 6,
        out_shape=[f32o, b16o, f32o, f32o, b16o, b16o],
        scratch_shapes=[pltpu.VMEM((NDEV, 3, D // NDEV, D), BF16), pltpu.VMEM((NDEV, 4, PG // NDEV, PG), BF16),
                        pltpu.SemaphoreType.DMA((2,))],
        compiler_params=_params(("arbitrary",), 48),
    )(ac, m, z, z, h0, b_gate, ln_g, ln_b, pool_scale, g_mixw, g_pool)


def _ffn_fwd(h1, tgt, g_ffn, g_final, w_gu, w_dn):
    tp = h1.shape[0]
    nt = tp // TM
    nx_last = tgt.shape[0] - (nt - 1) * TM

    def body(h_ref, t_ref, gf_ref, gl_ref, wgu_hbm, wdn_hbm,
             fg_ref, fu_ref, v_ref, f_ref, dh2_ref, acc_ref, wgu, wdn, v_sc, h2_sc, diff_sc, sems):
        i, j = pl.program_id(0), pl.program_id(1)
        _load_once((i == 0) & (j == 0), [(wgu_hbm, wgu), (wdn_hbm, wdn)], sems)

        @pl.when((i == 0) & (j == 0))
        def _():
            acc_ref[...] = jnp.zeros_like(acc_ref)

        @pl.when(j == 0)
        def _():
            h = h_ref[...]
            r = lax.rsqrt(jnp.mean(h * h, axis=-1, keepdims=True) + RMS_EPS)
            v = (h * r * gf_ref[...]).astype(BF16)
            v_sc[...] = v
            v_ref[...] = v
            h2_sc[...] = h

        v = v_sc[...]
        fg = _dot(v, wgu[0, j])
        fu = _dot(v, wgu[1, j])
        fg_ref[...] = fg
        fu_ref[...] = fu
        f = ((fg * _sig(fg)) * fu).astype(BF16)
        f_ref[...] = f
        h2_sc[...] += _dot(f, wdn[j])

        @pl.when(j == 1)
        def _():
            h2 = h2_sc[...]
            r = lax.rsqrt(jnp.mean(h2 * h2, axis=-1, keepdims=True) + RMS_EPS)
            n2 = h2 * r
            y = n2 * gl_ref[...]

            @pl.when(i < nt - 1)
            def _():
                diff_sc[...] = y - t_ref[...]

            @pl.when(i == nt - 1)
            def _():
                diff_sc[pl.ds(0, nx_last), :] = y[:nx_last] - t_ref[pl.ds(0, nx_last), :]
                diff_sc[pl.ds(nx_last, TM - nx_last), :] = jnp.zeros((TM - nx_last, D), F32)

            diff = diff_sc[...]
            dy = diff * (1.0 / D)
            acc_ref[0:1, :] += jnp.sum(diff * diff, axis=0, keepdims=True)
            acc_ref[1:2, :] += jnp.sum(dy * n2, axis=0, keepdims=True)
            dn = dy * gl_ref[...]
            dh2_ref[...] = r * (dn - n2 * jnp.mean(dn * n2, axis=-1, keepdims=True))

    def tile():
        return pl.BlockSpec((TM, D), lambda i, j: (i, 0))

    def chunk():
        return pl.BlockSpec((TM, FFC), lambda i, j: (i, j))

    def vec():
        return pl.BlockSpec((1, D), lambda i, j: (0, 0))

    anys = pl.BlockSpec(memory_space=pl.ANY)
    hid32, hid16 = jax.ShapeDtypeStruct((tp, DFF), F32), jax.ShapeDtypeStruct((tp, DFF), BF16)
    return pl.pallas_call(
        body, name="ffn_fwd", grid=(nt, 2),
        in_specs=[tile(), tile(), vec(), vec(), anys, anys],
        out_specs=[chunk(), chunk(), tile(), chunk(), tile(), pl.BlockSpec((8, D), lambda i, j: (0, 0))],
        out_shape=[hid32, hid32, jax.ShapeDtypeStruct((tp, D), BF16), hid16, jax.ShapeDtypeStruct((tp, D), F32),
                   jax.ShapeDtypeStruct((8, D), F32)],
        scratch_shapes=[pltpu.VMEM((2, 2, D, FFC), BF16), pltpu.VMEM((2, FFC, D), BF16),
                        pltpu.VMEM((TM, D), BF16), pltpu.VMEM((TM, D), F32), pltpu.VMEM((TM, D), F32),
                        pltpu.SemaphoreType.DMA((2,))],
        compiler_params=_params(("arbitrary", "arbitrary"), 56),
    )(h1, tgt, g_ffn, g_final, w_gu, w_dn)


def _ffn_bwd(dh2, fg, fu, h1, g_ffn, w_gu, w_dn):
    tp = h1.shape[0]
    nt = tp // TM

    def body(dh2_ref, fg_ref, fu_ref, h_ref, gf_ref, wgu_hbm, wdn_hbm,
             dfg_ref, dfu_ref, dh1_ref, acc_ref, wgu, wdn, d_sc, dv_sc, sems):
        i, j = pl.program_id(0), pl.program_id(1)
        _load_once((i == 0) & (j == 0), [(wgu_hbm, wgu), (wdn_hbm, wdn)], sems)

        @pl.when((i == 0) & (j == 0))
        def _():
            acc_ref[...] = jnp.zeros_like(acc_ref)

        @pl.when(j == 0)
        def _():
            d_sc[...] = dh2_ref[...].astype(BF16)
            dv_sc[...] = jnp.zeros_like(dv_sc)

        df = _dot_nt(d_sc[...], wdn[j])
        fg = fg_ref[...]
        sg = _sig(fg)
        dfu = (df * (fg * sg)).astype(BF16)
        dfg = (df * fu_ref[...] * (sg * (1.0 + fg * (1.0 - sg)))).astype(BF16)
        dfg_ref[...] = dfg
        dfu_ref[...] = dfu
        dv_sc[...] += _dot_nt(dfg, wgu[0, j]) + _dot_nt(dfu, wgu[1, j])

        @pl.when(j == 1)
        def _():
            h = h_ref[...]
            r = lax.rsqrt(jnp.mean(h * h, axis=-1, keepdims=True) + RMS_EPS)
            n1 = h * r
            dv = dv_sc[...]
            acc_ref[0:1, :] += jnp.sum(dv * n1, axis=0, keepdims=True)
            dn = dv * gf_ref[...]
            dh1_ref[...] = dh2_ref[...] + r * (dn - n1 * jnp.mean(dn * n1, axis=-1, keepdims=True))

    def tile():
        return pl.BlockSpec((TM, D), lambda i, j: (i, 0))

    def chunk():
        return pl.BlockSpec((TM, FFC), lambda i, j: (i, j))

    anys = pl.BlockSpec(memory_space=pl.ANY)
    hid16 = jax.ShapeDtypeStruct((tp, DFF), BF16)
    return pl.pallas_call(
        body, name="ffn_bwd", grid=(nt, 2),
        in_specs=[tile(), chunk(), chunk(), tile(), pl.BlockSpec((1, D), lambda i, j: (0, 0)), anys, anys],
        out_specs=[chunk(), chunk(), tile(), pl.BlockSpec((8, D), lambda i, j: (0, 0))],
        out_shape=[hid16, hid16, jax.ShapeDtypeStruct((tp, D), F32), jax.ShapeDtypeStruct((8, D), F32)],
        scratch_shapes=[pltpu.VMEM((2, 2, D, FFC), BF16), pltpu.VMEM((2, FFC, D), BF16),
                        pltpu.VMEM((TM, D), BF16), pltpu.VMEM((TM, D), F32), pltpu.SemaphoreType.DMA((2,))],
        compiler_params=_params(("arbitrary", "arbitrary"), 56),
    )(dh2, fg, fu, h1, g_ffn, w_gu, w_dn)


def _mix_bwd(dh1, z, yc, yp, ac, m, b_gate, ln_g, ln_b, pool_scale, g_mixw, g_pool, qs):
    tp = dh1.shape[0]
    nt = tp // TMS
    ex = _ChipExchange(qs)
    nq = ex.n

    def body(*refs):
        dh1_ref, zga, zgb, yc_ref, yp_ref, ac_ref, m_ref, bg_ref, lg_ref, lb_ref, ps_ref, wm_hbm, wp_hbm = refs[:13]
        dac_ref, dm_ref, dzg_ref, dyc_ref, dyp_ref, dm2_ref, acc_ref = refs[13 + nq:20 + nq]
        wm, wp, sems = refs[20 + 2 * nq:23 + 2 * nq]
        ex.bind(refs[13:13 + nq], refs[20 + nq:20 + 2 * nq], refs[23 + 2 * nq:])
        first = pl.program_id(0) == 0

        @pl.when(first)
        def _():
            ex.issue()
            acc_ref[...] = jnp.zeros_like(acc_ref)

        _load_once(first, [(wm_hbm, wm), (wp_hbm, wp)], sems)

        dmerged = _dot_nt(dh1_ref[...].astype(BF16), wm[:, 2].reshape(D, D))
        ga = _sig(zga[...] + bg_ref[:, :D])
        gb = _sig(zgb[...] + bg_ref[:, D:])
        dyc = dmerged * ga
        dyp = dmerged * gb
        dza = (dmerged * yc_ref[...]) * (ga * (1.0 - ga))
        dzb = (dmerged * yp_ref[...]) * (gb * (1.0 - gb))
        dzg_ref[:, :D] = dza.astype(BF16)
        dzg_ref[:, D:] = dzb.astype(BF16)
        acc_ref[0:1, :D] += jnp.sum(dza, axis=0, keepdims=True)
        acc_ref[0:1, D:] += jnp.sum(dzb, axis=0, keepdims=True)
        dyc_b = dyc.astype(BF16)
        dyp_b = dyp.astype(BF16)
        dyc_ref[...] = dyc_b
        dyp_ref[...] = dyp_b
        ds = _dot_nt(dyc_b, wm[:, 0].reshape(D, D))
        n, rl = _ln_stats(ac_ref[...])
        l = n * lg_ref[...] + lb_ref[...]
        sg = _sig(l)
        dl = ds * (sg * (1.0 + l * (1.0 - sg)))
        acc_ref[1:2, :D] += jnp.sum(dl * n, axis=0, keepdims=True)
        acc_ref[1:2, D:] += jnp.sum(dl, axis=0, keepdims=True)
        dn = dl * lg_ref[...]
        dac_ref[...] = rl * (dn - jnp.mean(dn, axis=-1, keepdims=True) - n * jnp.mean(dn * n, axis=-1, keepdims=True))
        dq = _dot_nt(dyp_b, wm[:, 1].reshape(D, D))
        mv = m_ref[...]
        acc_ref[2:3, :D] += jnp.sum(dq * _pool_mix(mv, wp), axis=0, keepdims=True)
        dm2 = (dq * ps_ref[...]).astype(BF16)
        dm2_ref[...] = dm2
        dm_ref[...] = jnp.concatenate(
            [_dot_nt(dm2[:, g * PG:(g + 1) * PG], wp[:, g].reshape(PG, PG)) for g in range(4)], axis=1)

        @pl.when(pl.program_id(0) == nt - 1)
        def _():
            ex.finish()

    def tile(col=0):
        return pl.BlockSpec((TMS, D), lambda i: (i, col))

    def vec(w):
        return pl.BlockSpec((1, w), lambda i: (0, 0))

    anys = pl.BlockSpec(memory_space=pl.ANY)
    f32o, b16o = jax.ShapeDtypeStruct((tp, D), F32), jax.ShapeDtypeStruct((tp, D), BF16)
    res = pl.pallas_call(
        body, name="mix_bwd", grid=(nt,),
        in_specs=[tile(), tile(3), tile(4), tile(), tile(), tile(), tile(), vec(2 * D), vec(D), vec(D), vec(D), anys, anys]
        + [anys] * nq,
        out_specs=[tile(), tile(), pl.BlockSpec((TMS, 2 * D), lambda i: (i, 0)), tile(), tile(), tile(),
                   pl.BlockSpec((8, 2 * D), lambda i: (0, 0))] + [anys] * nq,
        out_shape=[f32o, f32o, jax.ShapeDtypeStruct((tp, 2 * D), BF16), b16o, b16o, b16o,
                   jax.ShapeDtypeStruct((8, 2 * D), F32)] + ex.out_shape,
        scratch_shapes=[pltpu.VMEM((NDEV, 3, D // NDEV, D), BF16), pltpu.VMEM((NDEV, 4, PG // NDEV, PG), BF16),
                        pltpu.SemaphoreType.DMA((2,))] + ex.scratch,
        compiler_params=_params(("arbitrary",), 48),
    )(dh1, z, z, yc, yp, ac, m, b_gate, ln_g, ln_b, pool_scale, g_mixw, g_pool, *qs)
    return res[:7], res[7:]


def _seq_bwd(dac, dm, dzg, z, w_dw, seq, qs):
    tp = z.shape[0]
    nt = tp // TM
    ex = _ChipExchange(qs)
    nq = ex.n

    def body(*refs):
        dac_l, dac_c, dac_r, dm_l, dm_c, dm_r, av_l, av, av_r, ag_l, ag, ag_r, dzg_ref, w_ref = refs[:14]
        dz_ref, acc_ref = refs[14 + nq:16 + nq]
        a_ext, d_ext, m_ext, sha, shd, da_sc, dw_sc, w8 = refs[16 + 2 * nq:24 + 2 * nq]
        ex.bind(refs[14:14 + nq], refs[16 + nq:16 + 2 * nq], refs[24 + 2 * nq:])
        i = pl.program_id(0)

        @pl.when(i == 0)
        def _():
            ex.issue()
            dw_sc[...] = jnp.zeros_like(dw_sc)
            acc_ref[...] = jnp.zeros_like(acc_ref)
            _tap_rows(w_ref, w8)

        sg = _sig(ag[...])
        _fill_ext(a_ext, av_l[...] * _sig(ag_l[...]), av[...] * sg, av_r[...] * _sig(ag_r[...]))
        _fill_ext(d_ext, dac_l[...], dac_c[...], dac_r[...])
        for g, win in enumerate(POOL_WINDOWS):
            left = win // 2
            right = win - 1 - left
            cols = pl.ds(g * PG, PG)
            m_ext[pl.ds(0, HALO), cols] = dm_l[:, cols] / _pool_cnt(i, seq, tp, left, right, HALO, -HALO)
            m_ext[pl.ds(HALO, TM), cols] = dm_c[:, cols] / _pool_cnt(i, seq, tp, left, right, TM, 0)
            m_ext[pl.ds(HALO + TM, HALO), cols] = dm_r[:, cols] / _pool_cnt(i, seq, tp, left, right, HALO, TM)
        for c0 in range(0, D, CW):
            _shift_copies(a_ext, sha, c0)
            _shift_copies(d_ext, shd, c0)

            def rows(j, carry):
                base = pl.multiple_of(j * RB, RB)
                dcur = d_ext[pl.ds(pl.multiple_of(base + HALO, 8), RB), pl.ds(c0, CW)].reshape(RB // 8, 8, CW)
                acc = jnp.zeros((RB // 8, 8, CW), F32)
                for k in range(CONV_K):
                    q, r = divmod(CONV_K - k, 8)
                    slab = shd[r, pl.ds(pl.multiple_of(base + 8 * q, 8), RB), :].reshape(RB // 8, 8, CW)
                    acc = acc + slab * w8[k, :, pl.ds(c0, CW)]
                    q, r = divmod(k + 1, 8)
                    slab = sha[r, pl.ds(pl.multiple_of(base + 8 * q, 8), RB), :].reshape(RB // 8, 8, CW)
                    dw_sc[k, :, pl.ds(c0, CW)] += jnp.sum(dcur * slab, axis=0)
                da_sc[pl.ds(base, RB), pl.ds(c0, CW)] = acc.reshape(RB, CW)
                return carry

            lax.fori_loop(0, TM // RB, rows, 0)
        da = da_sc[...]
        dz_ref[:, 0:D] = (da * sg).astype(BF16)
        dz_ref[:, D:2 * D] = (da * av[...] * (sg * (1.0 - sg))).astype(BF16)
        for g, win in enumerate(POOL_WINDOWS):
            left = win // 2
            right = win - 1 - left
            cols = pl.ds(g * PG, PG)
            s = m_ext[pl.ds(HALO - right, TM), cols]
            for off in range(-right + 1, left + 1):
                s = s + m_ext[pl.ds(HALO + off, TM), cols]
            dz_ref[:, pl.ds(2 * D + g * PG, PG)] = (s - dm_c[:, cols]).astype(BF16)
        dz_ref[:, 3 * D:] = dzg_ref[...]

        @pl.when(i == nt - 1)
        def _():
            for k in range(CONV_K):
                acc_ref[k:k + 1, :] = jnp.sum(dw_sc[k], axis=0, keepdims=True)

        acc_ref[CONV_K:CONV_K + 1, :] += jnp.sum(dac_c[...], axis=0, keepdims=True)

        @pl.when(i == nt - 1)
        def _():
            ex.finish()

    ext = pltpu.VMEM((TM + 2 * HALO, D), F32)
    shs = pltpu.VMEM((8, TM + 24, CW), F32)
    anys = pl.BlockSpec(memory_space=pl.ANY)
    res = pl.pallas_call(
        body, name="seq_bwd", grid=(nt,),
        in_specs=_halo_specs(0, nt) + _halo_specs(0, nt) + _halo_specs(0, nt) + _halo_specs(1, nt)
        + [pl.BlockSpec((TM, 2 * D), lambda i: (i, 0)), pl.BlockSpec((32, D), lambda i: (0, 0))] + [anys] * nq,
        out_specs=[pl.BlockSpec((TM, DIN), lambda i: (i, 0)), pl.BlockSpec((32, D), lambda i: (0, 0))] + [anys] * nq,
        out_shape=[jax.ShapeDtypeStruct((tp, DIN), BF16), jax.ShapeDtypeStruct((32, D), F32)] + ex.out_shape,
        scratch_shapes=[ext, ext, ext, shs, shs, pltpu.VMEM((TM, D), F32), pltpu.VMEM((CONV_K, 8, D), F32),
                        pltpu.VMEM((CONV_K, 8, D), F32)] + ex.scratch,
        compiler_params=_params(("arbitrary",), 48),
    )(dac, dac, dac, dm, dm, dm, z, z, z, z, z, z, dzg, w_dw, *qs)
    return res[:2], res[2:]


def _in_bwd(dz, h0, dh1, g_mix, w_g, seq, qs):
    tp = h0.shape[0]
    tm = _pick(tp, TM_IO)
    nt = tp // tm
    ex = _ChipExchange(qs)
    nq = ex.n

    def body(*refs):
        dz_ref, h_ref, dh1_ref, g_ref, w_hbm = refs[:5]
        gx_ref, gmeta_ref, acc_ref = refs[5 + nq:8 + nq]
        w_vm, sems = refs[8 + 2 * nq:10 + 2 * nq]
        ex.bind(refs[5:5 + nq], refs[8 + nq:8 + 2 * nq], refs[10 + 2 * nq:])
        i = pl.program_id(0)

        @pl.when(i == 0)
        def _():
            ex.issue()
            acc_ref[...] = jnp.zeros_like(acc_ref)

        _load_once(i == 0, _win_pairs(w_hbm, w_vm), sems)

        du = _dot_nt(dz_ref[:, :DIN // 2], w_vm[0]) + _dot_nt(dz_ref[:, DIN // 2:], w_vm[1])
        h = h_ref[...]
        r = lax.rsqrt(jnp.mean(h * h, axis=-1, keepdims=True) + RMS_EPS)
        n0 = h * r
        acc_ref[0:1, :] += jnp.sum(du * n0, axis=0, keepdims=True)
        dn = du * g_ref[...]
        gx_ref[...] = dh1_ref[...] + r * (dn - n0 * jnp.mean(dn * n0, axis=-1, keepdims=True))

        @pl.when(i == nt - 1)
        def _():
            gmeta_ref[...] = gx_ref[pl.ds(tm - N_META, N_META), :]
            ex.finish()

    tile = pl.BlockSpec((tm, D), lambda i: (i, 0))
    anys = pl.BlockSpec(memory_space=pl.ANY)
    res = pl.pallas_call(
        body, name="in_bwd", grid=(nt,),
        in_specs=[pl.BlockSpec((tm, DIN), lambda i: (i, 0)), tile, tile, pl.BlockSpec((1, D), lambda i: (0, 0)), anys]
        + [anys] * nq,
        out_specs=[tile, pl.BlockSpec((N_META, D), lambda i: (0, 0)), pl.BlockSpec((8, D), lambda i: (0, 0))] + [anys] * nq,
        out_shape=[jax.ShapeDtypeStruct((seq, D), F32), jax.ShapeDtypeStruct((N_META, D), F32),
                   jax.ShapeDtypeStruct((8, D), F32)] + ex.out_shape,
        scratch_shapes=[pltpu.VMEM((2, D, DIN // 2), BF16), pltpu.SemaphoreType.DMA((NDEV,))] + ex.scratch,
        compiler_params=_params(("arbitrary",), 58),
    )(dz, h0, dh1, g_mix, w_g, *qs)
    return res[:3], res[3:]


def _wgrad_in(u, dz):
    tp = u.shape[0]
    tm = _pick(tp, TM_WG)
    nt = tp // tm
    half = DIN // 2

    def body(u_ref, dz_ref, o_ref, acc):
        t = pl.program_id(1)

        @pl.when(t == 0)
        def _():
            acc[...] = jnp.zeros_like(acc)

        acc[...] += _dot_tn(u_ref[...], dz_ref[...])

        @pl.when(t == nt - 1)
        def _():
            for d in range(4):
                o_ref[d] = acc[:, INB * d:INB * (d + 1)].astype(BF16)

    return pl.pallas_call(
        body, name="wgrad_in", grid=(2, nt),
        in_specs=[pl.BlockSpec((tm, D), lambda h, t: (t, 0)), pl.BlockSpec((tm, half), lambda h, t: (t, h))],
        out_specs=pl.BlockSpec((4, D, INB), lambda h, t: (h, 0, 0), pipeline_mode=pl.Buffered(1)),
        out_shape=jax.ShapeDtypeStruct((NDEV, D, INB), BF16),
        scratch_shapes=[pltpu.VMEM((D, half), F32)],
        compiler_params=_params(("arbitrary", "arbitrary"), 52),
    )(u, dz)


def _wgrad_mix(s, dyc, q, dyp, merged, dh1, m, dm2):
    tp = s.shape[0]
    tm = _pick(tp, TM_WM)
    nt = tp // tm
    rb = D // NDEV

    def body(s_ref, dyc_ref, q_ref, dyp_ref, mg_ref, dh1_ref, m_ref, dm2_ref, o_ref, op_ref, acc, accp):
        t = pl.program_id(0)

        @pl.when(t == 0)
        def _():
            acc[...] = jnp.zeros_like(acc)
            accp[...] = jnp.zeros_like(accp)

        acc[0] += _dot_tn(s_ref[...], dyc_ref[...])
        acc[1] += _dot_tn(q_ref[...], dyp_ref[...])
        acc[2] += _dot_tn(mg_ref[...], dh1_ref[...].astype(BF16))
        for g in range(4):
            accp[g] += _dot_tn(m_ref[:, g * PG:(g + 1) * PG], dm2_ref[:, g * PG:(g + 1) * PG])

        @pl.when(t == nt - 1)
        def _():
            for d in range(NDEV):
                for k in range(3):
                    o_ref[d, k] = acc[k, rb * d:rb * (d + 1), :].astype(BF16)
                for g in range(4):
                    op_ref[d, g] = accp[g, 32 * d:32 * (d + 1), :].astype(BF16)

    tile = pl.BlockSpec((tm, D), lambda t: (t, 0))
    return pl.pallas_call(
        body, name="wgrad_mix", grid=(nt,),
        in_specs=[tile] * 8,
        out_specs=[pl.BlockSpec((NDEV, 3, rb, D), lambda t: (0, 0, 0, 0), pipeline_mode=pl.Buffered(1)),
                   pl.BlockSpec((NDEV, 4, 32, PG), lambda t: (0, 0, 0, 0), pipeline_mode=pl.Buffered(1))],
        out_shape=[jax.ShapeDtypeStruct((NDEV, 3, rb, D), BF16), jax.ShapeDtypeStruct((NDEV, 4, 32, PG), BF16)],
        scratch_shapes=[pltpu.VMEM((3, D, D), F32), pltpu.VMEM((4, PG, PG), F32)],
        compiler_params=_params(("arbitrary",), 56),
    )(s, dyc, q, dyp, merged, dh1, m, dm2)


def _wgrad_gu(v, dfg, dfu):
    tp = v.shape[0]
    tm = _pick(tp, TM_WG)
    nt = tp // tm

    def body(v_ref, dg_ref, du_ref, o_ref, acc):
        k, t = pl.program_id(0), pl.program_id(2)

        @pl.when(t == 0)
        def _():
            acc[...] = jnp.zeros_like(acc)

        @pl.when(k == 0)
        def _():
            acc[...] += _dot_tn(v_ref[...], dg_ref[...])

        @pl.when(k == 1)
        def _():
            acc[...] += _dot_tn(v_ref[...], du_ref[...])

        @pl.when(t == nt - 1)
        def _():
            for d in range(4):
                o_ref[d] = acc[:, pl.ds(FFB * d, FFB)].astype(BF16)

    return pl.pallas_call(
        body, name="wgrad_gu", grid=(2, 2, nt),
        in_specs=[pl.BlockSpec((tm, D), lambda k, h, t: (t, 0)),
                  pl.BlockSpec((tm, FFC), lambda k, h, t: (t * (1 - k), h * (1 - k))),
                  pl.BlockSpec((tm, FFC), lambda k, h, t: (t * k, h * k))],
        out_specs=pl.BlockSpec((4, None, D, FFB), lambda k, h, t: (h, k, 0, 0), pipeline_mode=pl.Buffered(1)),
        out_shape=jax.ShapeDtypeStruct((NDEV, 2, D, FFB), BF16),
        scratch_shapes=[pltpu.VMEM((D, FFC), F32)],
        compiler_params=_params(("arbitrary",) * 3, 48),
    )(v, dfg, dfu)


def _wgrad_down(f, dh2):
    tp = f.shape[0]
    tm = _pick(tp, TM_WG)
    nt = tp // tm

    def body(f_ref, d_ref, o_ref, acc):
        t = pl.program_id(1)

        @pl.when(t == 0)
        def _():
            acc[...] = jnp.zeros_like(acc)

        acc[...] += _dot_tn(f_ref[...], d_ref[...].astype(BF16))

        @pl.when(t == nt - 1)
        def _():
            for d in range(4):
                o_ref[d] = acc[FFB * d:FFB * (d + 1), :].astype(BF16)

    return pl.pallas_call(
        body, name="wgrad_down", grid=(2, nt),
        in_specs=[pl.BlockSpec((tm, FFC), lambda h, t: (t, h)), pl.BlockSpec((tm, D), lambda h, t: (t, 0))],
        out_specs=pl.BlockSpec((4, FFB, D), lambda h, t: (h, 0, 0), pipeline_mode=pl.Buffered(1)),
        out_shape=jax.ShapeDtypeStruct((NDEV, FFB, D), BF16),
        scratch_shapes=[pltpu.VMEM((FFC, D), F32)],
        compiler_params=_params(("arbitrary", "arbitrary"), 48),
    )(f, dh2)


def kernel(x, meta_tokens, g_mix, w_in, b_gate, w_dw, b_dw, ln_g, ln_b, w_conv_out, w_pool, pool_scale, w_pool_out, w_o, g_ffn, w_ffn_gate, w_ffn_up, w_ffn_down, g_final, loss_target, m_meta_tokens, m_g_mix, m_w_in, m_b_gate, m_w_dw, m_b_dw, m_ln_g, m_ln_b, m_w_conv_out, m_w_pool, m_pool_scale, m_w_pool_out, m_w_o, m_g_ffn, m_w_ffn_gate, m_w_ffn_up, m_w_ffn_down, m_g_final, v_meta_tokens, v_g_mix, v_w_in, v_b_gate, v_w_dw, v_b_dw, v_ln_g, v_ln_b, v_w_conv_out, v_w_pool, v_pool_scale, v_w_pool_out, v_w_o, v_g_ffn, v_w_ffn_gate, v_w_ffn_up, v_w_ffn_down, v_g_final):
    seq = x.shape[1]
    tp = -(-(seq + 2 * HALO) // TM) * TM
    tm_in = _pick(tp, TM_IO)
    nx_last = seq - (tp // tm_in - 1) * tm_in
    assert 0 < nx_last <= tm_in - 2 * HALO and nx_last % 8 == 0 and 0 < seq - (tp // TM - 1) * TM

    whole = (Ellipsis,)
    g_in, g_small = _all_gather(
        [((D, INB), [(w_in, whole, 0)]),
         ((48, D // NDEV), [(meta_tokens, pl.ds(0, N_META), whole), (w_dw, pl.ds(N_META, CONV_K), 0)])], [BF16, F32])
    ag_mix = _Gather([((3, D // NDEV, D), [(w_conv_out, 0, 0), (w_pool_out, 1, 0), (w_o, 2, 0)]),
                      ((4, PG // NDEV, PG), [(w_pool, whole, 0)])], [BF16, BF16])
    ag_ffn = _Gather([((2, D, FFB), [(w_ffn_gate, 0, 0), (w_ffn_up, 1, 0)]),
                      ((FFB, D), [(w_ffn_down, whole, 0)])], [BF16, BF16])
    small_full = g_small.transpose(1, 0, 2).reshape(48, D)
    wdw_full = small_full[N_META:]
    tail = jnp.concatenate([jnp.zeros((tm_in - nx_last - N_META, D), F32), small_full[:N_META]], axis=0)

    (h0, z, u), (g_mixw, g_pool) = _fwd_in(x[0], tail, g_mix, g_in, tp, ag_mix)
    (ac, m), (g_gu, g_down) = _seq_fwd(z, wdw_full, b_dw, seq, ag_ffn)
    w_gu = _repack_gu(g_gu)
    w_dn = g_down.reshape(2, FFC, D)
    h1, s, yc, yp, merged, q = _mix_fwd(ac, m, z, h0, b_gate, ln_g, ln_b, pool_scale, g_mixw, g_pool)
    fg, fu, v, f, dh2, head_acc = _ffn_fwd(h1, loss_target[0], g_ffn, g_final.reshape(1, D), w_gu, w_dn)

    dfg, dfu, dh1, ffn_acc = _ffn_bwd(dh2, fg, fu, h1, g_ffn, w_gu, w_dn)
    own_f, sib_f, q_f = _rs_pair("rs_pair_ffn", [_wgrad_gu(v, dfg, dfu), _wgrad_down(f, dh2)])
    (dac, dm, dzg, dyc, dyp, dm2, mix_acc), rel_f = _mix_bwd(
        dh1, z, yc, yp, ac, m, b_gate, ln_g, ln_b, pool_scale, g_mixw, g_pool, q_f)
    own_m, sib_m, q_m = _rs_pair("rs_pair_mix", list(_wgrad_mix(s, dyc, q, dyp, merged, dh1, m, dm2)))
    (dz, seq_acc), rel_m = _seq_bwd(dac, dm, dzg, z, wdw_full, seq, q_m)
    own_i, sib_i, q_i = _rs_pair("rs_pair_in", [_wgrad_in(u, dz)])
    (grad_x, g_meta, in_acc), rel_i = _in_bwd(dz, h0, dh1, g_mix, g_in, seq, q_i)
    small_g = jnp.concatenate([g_meta, seq_acc[:CONV_K], jnp.zeros((1, D), F32)], axis=0)
    p_small = small_g.reshape(48, NDEV, D // NDEV).transpose(1, 0, 2).astype(BF16)
    rep_g = jnp.concatenate([
        in_acc[0:1], mix_acc[0:1, :D], mix_acc[0:1, D:], seq_acc[CONV_K:CONV_K + 1], mix_acc[1:2, :D], mix_acc[1:2, D:],
        mix_acc[2:3, :D], ffn_acc[0:1], head_acc[1:2], head_acc[0:1], jnp.zeros((REP_ROWS - 10, D), F32)], axis=0)
    own_s, sib_s, rel_s, rep_all = _reduce_scatter([p_small], rep_g)
    owns = [own_i[0], own_s[0], own_m[0], own_m[1], own_f[0], own_f[1]]
    sibs = [sib_i[0], sib_s[0], sib_m[0], sib_m[1], sib_f[0], sib_f[1]]
    rels = [rel_i[0], rel_s[0], rel_m[0], rel_m[1], rel_f[0], rel_f[1]]

    def lead(a):
        return a.reshape(1, *a.shape)

    def stack4(a, lead_dims):
        return a.reshape(*lead_dims, 1, 4 * 32, PG)

    (r_in,) = _adamw_multi("adamw_in", lead(owns[0]), sibs[0][:, None], rels[0][:, None], [w_in], [m_w_in], [v_w_in], 4)
    r_meta, r_dw = _adamw_meta_dw(owns[1], sibs[1], rels[1], (meta_tokens, m_meta_tokens, v_meta_tokens),
                                  (w_dw, m_w_dw, v_w_dw))
    r_conv, r_pout, r_o = _adamw_multi("adamw_mix", owns[2], sibs[2], rels[2], [w_conv_out, w_pool_out, w_o],
                                       [m_w_conv_out, m_w_pool_out, m_w_o], [v_w_conv_out, v_w_pool_out, v_w_o], 1)
    (r_pool,) = _adamw_multi("adamw_pool", stack4(owns[3], ()), stack4(sibs[3], (4,)), stack4(rels[3], (3,)),
                             [w_pool.reshape(1, 128, PG)], [m_w_pool.reshape(1, 128, PG)], [v_w_pool.reshape(1, 128, PG)], 1)
    r_pool = tuple(a.reshape(w_pool.shape) for a in r_pool)
    r_gate, r_up = _adamw_multi("adamw_gu", owns[4], sibs[4], rels[4], [w_ffn_gate, w_ffn_up],
                                [m_w_ffn_gate, m_w_ffn_up], [v_w_ffn_gate, v_w_ffn_up], 4)
    (r_down,) = _adamw_multi("adamw_down", lead(owns[5]), sibs[5][:, None], rels[5][:, None],
                             [w_ffn_down], [m_w_ffn_down], [v_w_ffn_down], 2)
    row = (1, D)
    loss, reps = _adamw_rep(
        rep_all,
        [g_mix, b_gate, b_dw, ln_g, ln_b, pool_scale, g_ffn, g_final.reshape(row)],
        [m_g_mix, m_b_gate, m_b_dw, m_ln_g, m_ln_b, m_pool_scale, m_g_ffn, m_g_final.reshape(row)],
        [v_g_mix, v_b_gate, v_b_dw, v_ln_g, v_ln_b, v_pool_scale, v_g_ffn, v_g_final.reshape(row)])
    r_gmix, r_bg, r_bdw, r_lg, r_lb, r_ps, r_gffn, r_gfin = reps
    r_gfin = tuple(a.reshape(D) for a in r_gfin)

    in_order = [r_meta, r_gmix, r_in, r_bg, r_dw, r_bdw, r_lg, r_lb, r_conv, r_pool, r_ps, r_pout, r_o, r_gffn,
                r_gate, r_up, r_down, r_gfin]
    return (loss.reshape(()), grad_x[None], *[r[0] for r in in_order], *[r[1] for r in in_order],
            *[r[2] for r in in_order], *[r[3] for r in in_order])
```

```python
import math

import jax
import jax.numpy as jnp
from jax import lax
from jax.experimental import pallas as pl
from jax.experimental.pallas import tpu as pltpu

F32, BF16 = jnp.float32, jnp.bfloat16
MESH_ID = pl.DeviceIdType.MESH
NDEV = 8

D = 1024
N_META = 16
CONV_K = 31
HALO = 16
POOL_WINDOWS = (2, 4, 8, 16)
PG = 256
DIN = 5 * D
DFF = 2816
FFB = DFF // NDEV
FFC = DFF // 2
INB = DIN // NDEV
RMS_EPS = 1e-6
LN_EPS = 1e-5
ADAM_LR, ADAM_B1, ADAM_B2, ADAM_EPS, ADAM_WD, ADAM_STEP = 0.001, 0.9, 0.999, 1e-08, 0.01, 10

TM = 384
TMS = 192
TM_IO = 704
TM_WG = 1408
TM_WM = 704
RB, CW = 64, 128
MIB = 2 ** 20


def _sig(x):
    return 1.0 / (1.0 + jnp.exp(-x))


def _dot(a, b):
    return jnp.dot(a, b, preferred_element_type=F32)


def _dot_nt(a, b):
    return lax.dot_general(a, b, (((1,), (1,)), ((), ())), preferred_element_type=F32)


def _dot_tn(a, b):
    return lax.dot_general(a, b, (((0,), (0,)), ((), ())), preferred_element_type=F32)


def _pick(tp, pref):
    return pref if tp % pref == 0 else TM


def _params(sem, vmem_mib):
    return pltpu.CompilerParams(dimension_semantics=sem, vmem_limit_bytes=vmem_mib * MIB)


def _load_once(first, pairs, sems):
    @pl.when(first)
    def _():
        cps = [pltpu.make_async_copy(s, d, sems.at[k]) for k, (s, d) in enumerate(pairs)]
        for cp in cps:
            cp.start()
        for cp in cps:
            cp.wait()


def _place():
    x, y, c = lax.axis_index("x"), lax.axis_index("y"), lax.axis_index("c")
    return x, y, c


class _Gather:
    def __init__(self, groups, dtypes):
        self.groups, self.dtypes, self.n = groups, dtypes, len(groups)
        self.arrays = [a for _, parts in groups for a, _, _ in parts]
        self.out_shape = [jax.ShapeDtypeStruct((NDEV, *s), dt) for (s, _), dt in zip(groups, dtypes)]
        self.scratch = [pltpu.VMEM(s, dt) for (s, _), dt in zip(groups, dtypes)] + [
            pltpu.SemaphoreType.DMA((7 * self.n,)), pltpu.SemaphoreType.DMA((7 * self.n,)),
            pltpu.SemaphoreType.DMA((self.n,))]

    def bind(self, ins, outs, scratch):
        self.ins, self.outs, self.stages = ins, outs, scratch[:self.n]
        self.send_sems, self.recv_sems, self.local_sems = scratch[self.n:]
        return self

    def _copy(self, w, k, block, to, src=None):
        dst = self.outs[w].at[4 * block[0] + 2 * block[1] + block[2]]
        return pltpu.make_async_remote_copy(
            src_ref=dst if src is None else src, dst_ref=dst,
            send_sem=self.send_sems.at[7 * w + k], recv_sem=self.recv_sems.at[7 * w + k],
            device_id=to, device_id_type=MESH_ID)

    def _first(self):
        x, y, c = _place()
        me, sibling = (x, y, c), (x, y, 1 - c)
        chips = [(1 - x, y), (x, 1 - y), (1 - x, 1 - y)]
        mine, first = [], []
        for w in range(self.n):
            mine.append(pltpu.make_async_copy(self.stages[w], self.outs[w].at[4 * x + 2 * y + c], self.local_sems.at[w]))
            first.append(self._copy(w, 0, me, sibling, src=self.stages[w]))
            first += [self._copy(w, 1 + j, me, (*chip, c), src=self.stages[w]) for j, chip in enumerate(chips)]
        return mine, first

    def _passed(self):
        x, y, c = _place()
        chips = [(1 - x, y), (x, 1 - y), (1 - x, 1 - y)]
        return [self._copy(w, 4 + j, (*chip, c), (x, y, 1 - c)) for w in range(self.n) for j, chip in enumerate(chips)]

    def issue(self):
        a = 0
        for w in range(self.n):
            shape, parts = self.groups[w]
            if sum(arr.size for arr, _, _ in parts) < math.prod(shape):
                self.stages[w][...] = jnp.zeros(shape, self.dtypes[w])
            for _, dst, src in parts:
                self.stages[w][dst] = self.ins[a][src].astype(self.dtypes[w])
                a += 1
        mine, first = self._first()
        for cp in mine + first:
            cp.start()

    def forward(self):
        x, y, c = _place()
        chips = [(1 - x, y), (x, 1 - y), (1 - x, 1 - y)]
        passed = self._passed()
        for w in range(self.n):
            for j, chip in enumerate(chips):
                self._copy(w, 1 + j, (*chip, c), (x, y, c)).wait_recv()
                passed[3 * w + j].start()

    def finish(self):
        x, y, c = _place()
        chips = [(1 - x, y), (x, 1 - y), (1 - x, 1 - y)]
        for w in range(self.n):
            self._copy(w, 0, (x, y, 1 - c), (x, y, c)).wait_recv()
            for j, chip in enumerate(chips):
                self._copy(w, 4 + j, (*chip, 1 - c), (x, y, c)).wait_recv()
        mine, first = self._first()
        for cp in first + self._passed():
            cp.wait_send()
        for cp in mine:
            cp.wait()


def _all_gather(groups, dtypes):
    ag = _Gather(groups, dtypes)
    na, n = len(ag.arrays), ag.n

    def body(*refs):
        ag.bind(refs[:na], refs[na:na + n], refs[na + n:])
        ag.issue()
        ag.forward()
        ag.finish()

    return pl.pallas_call(
        body, name="ag_weights", out_shape=ag.out_shape,
        in_specs=[pl.BlockSpec(memory_space=pltpu.VMEM)] * na,
        out_specs=[pl.BlockSpec(memory_space=pl.ANY)] * n,
        scratch_shapes=ag.scratch,
        compiler_params=pltpu.CompilerParams(vmem_limit_bytes=40 * MIB),
    )(*ag.arrays)


class _ChipExchange:
    def __init__(self, qs):
        self.n = len(qs)
        self.out_shape = [jax.ShapeDtypeStruct(q.shape, q.dtype) for q in qs]
        self.scratch = [pltpu.SemaphoreType.DMA((3 * self.n,)), pltpu.SemaphoreType.DMA((3 * self.n,))]

    def bind(self, qs, rels, scratch):
        self.qs, self.rels = qs, rels
        self.send_sems, self.recv_sems = scratch
        return self

    def _copies(self):
        x, y, c = _place()
        chips = [(1 - x, y), (x, 1 - y), (1 - x, 1 - y)]
        return [pltpu.make_async_remote_copy(
            src_ref=self.qs[w].at[j], dst_ref=self.rels[w].at[j],
            send_sem=self.send_sems.at[3 * w + j], recv_sem=self.recv_sems.at[3 * w + j],
            device_id=(*chips[j], c), device_id_type=MESH_ID) for w in range(self.n) for j in range(3)]

    def issue(self):
        for cp in self._copies():
            cp.start()

    def finish(self):
        cps = self._copies()
        for cp in cps:
            cp.wait_recv()
        for cp in cps:
            cp.wait_send()


def _reduce_scatter(parts, small):
    n = len(parts)
    blks = [p.shape[1:] for p in parts]

    def body(*refs):
        ps, small_ref = refs[:n], refs[n]
        o = n + 1
        owns, sibs, rels, small_out = refs[o:o + n], refs[o + n:o + 2 * n], refs[o + 2 * n:o + 3 * n], refs[o + 3 * n]
        o += 3 * n + 1
        pa, pb, qst = refs[o:o + n], refs[o + n:o + 2 * n], refs[o + 2 * n:o + 3 * n]
        s1_send, s1_recv, s2_send, s2_recv, sm_send, sm_recv, lsem = refs[o + 3 * n:]
        x, y, c = _place()
        me = 4 * x + 2 * y + c
        sibling = (x, y, 1 - c)
        chips = [(1 - x, y), (x, 1 - y), (1 - x, 1 - y)]
        all_chips = [(x, y)] + chips

        own_cps = []
        for w in range(n):
            cp = pltpu.make_async_copy(ps[w].at[me], owns[w], lsem.at[w])
            cp.start()
            own_cps.append(cp)
        sm_own = pltpu.make_async_copy(small_ref, small_out.at[me], lsem.at[n])
        sm_own.start()

        def small_copy(r):
            peer = ((x + (r >> 2)) % 2, (y + ((r >> 1) & 1)) % 2, (c + (r & 1)) % 2)
            return pltpu.make_async_remote_copy(
                src_ref=small_ref, dst_ref=small_out.at[me], send_sem=sm_send.at[r - 1], recv_sem=sm_recv.at[r - 1],
                device_id=peer, device_id_type=MESH_ID)

        sm_cps = [small_copy(r) for r in range(1, NDEV)]
        for cp in sm_cps:
            cp.start()

        def pair_copy(w, rel):
            cx, cy = all_chips[rel]
            return pltpu.make_async_remote_copy(
                src_ref=ps[w].at[4 * cx + 2 * cy + (1 - c)], dst_ref=sibs[w].at[rel],
                send_sem=s1_send.at[4 * w + rel], recv_sem=s1_recv.at[4 * w + rel],
                device_id=sibling, device_id_type=MESH_ID)

        def chip_copy(w, j):
            return pltpu.make_async_remote_copy(
                src_ref=qst[w].at[j], dst_ref=rels[w].at[j],
                send_sem=s2_send.at[3 * w + j], recv_sem=s2_recv.at[3 * w + j],
                device_id=(*chips[j], c), device_id_type=MESH_ID)

        pair_cps = [pair_copy(w, rel) for w in range(n) for rel in (1, 2, 3, 0)]
        for cp in pair_cps:
            cp.start()
        chip_cps = []
        for w in range(n):
            for j, (cx, cy) in enumerate(chips):
                pair_copy(w, 1 + j).wait_recv()
                la = pltpu.make_async_copy(ps[w].at[4 * cx + 2 * cy + c], pa[w], lsem.at[n + 1])
                lb = pltpu.make_async_copy(sibs[w].at[1 + j], pb[w], lsem.at[n + 2])
                la.start()
                lb.start()
                la.wait()
                lb.wait()
                qst[w][j] = (pa[w][...].astype(F32) + pb[w][...].astype(F32)).astype(BF16)
                cp = chip_copy(w, j)
                cp.start()
                chip_cps.append(cp)
        for w in range(n):
            pair_copy(w, 0).wait_recv()
            for j in range(3):
                chip_copy(w, j).wait_recv()
        for cp in sm_cps:
            cp.wait_recv()
        for cp in pair_cps + chip_cps + sm_cps:
            cp.wait_send()
        for cp in own_cps:
            cp.wait()
        sm_own.wait()

    any_spec = pl.BlockSpec(memory_space=pl.ANY)
    outs = pl.pallas_call(
        body, name="rs_grads",
        out_shape=[jax.ShapeDtypeStruct(b, BF16) for b in blks]
        + [jax.ShapeDtypeStruct((4, *b), BF16) for b in blks]
        + [jax.ShapeDtypeStruct((3, *b), BF16) for b in blks]
        + [jax.ShapeDtypeStruct((NDEV, *small.shape), F32)],
        in_specs=[any_spec] * (n + 1),
        out_specs=[any_spec] * (3 * n + 1),
        scratch_shapes=[pltpu.VMEM(b, BF16) for b in blks] + [pltpu.VMEM(b, BF16) for b in blks]
        + [pltpu.VMEM((3, *b), BF16) for b in blks]
        + [pltpu.SemaphoreType.DMA((4 * n,)), pltpu.SemaphoreType.DMA((4 * n,)),
           pltpu.SemaphoreType.DMA((3 * n,)), pltpu.SemaphoreType.DMA((3 * n,)),
           pltpu.SemaphoreType.DMA((NDEV - 1,)), pltpu.SemaphoreType.DMA((NDEV - 1,)),
           pltpu.SemaphoreType.DMA((n + 3,))],
        compiler_params=pltpu.CompilerParams(vmem_limit_bytes=40 * MIB),
    )(*parts, small)
    return outs[:n], outs[n:2 * n], outs[2 * n:3 * n], outs[3 * n]


def _rs_pair(name, parts):
    n = len(parts)
    blks = [p.shape[1:] for p in parts]

    def body(*refs):
        ps = refs[:n]
        owns, sibs, qs = refs[n:2 * n], refs[2 * n:3 * n], refs[3 * n:4 * n]
        pa, pb, qst = refs[4 * n:5 * n], refs[5 * n:6 * n], refs[6 * n:7 * n]
        s_send, s_recv, lsem = refs[7 * n:]
        x, y, c = _place()
        chips = [(1 - x, y), (x, 1 - y), (1 - x, 1 - y)]
        all_chips = [(x, y)] + chips

        own_cps = [pltpu.make_async_copy(ps[w].at[4 * x + 2 * y + c], owns[w], lsem.at[w]) for w in range(n)]
        for cp in own_cps:
            cp.start()

        def pair_copy(w, rel):
            cx, cy = all_chips[rel]
            return pltpu.make_async_remote_copy(
                src_ref=ps[w].at[4 * cx + 2 * cy + (1 - c)], dst_ref=sibs[w].at[rel],
                send_sem=s_send.at[4 * w + rel], recv_sem=s_recv.at[4 * w + rel],
                device_id=(x, y, 1 - c), device_id_type=MESH_ID)

        pair_cps = [pair_copy(w, rel) for w in range(n) for rel in (1, 2, 3, 0)]
        for cp in pair_cps:
            cp.start()
        q_cps = []
        for w in range(n):
            for j, (cx, cy) in enumerate(chips):
                la = pltpu.make_async_copy(ps[w].at[4 * cx + 2 * cy + c], pa[w], lsem.at[n])
                lb = pltpu.make_async_copy(sibs[w].at[1 + j], pb[w], lsem.at[n + 1])
                la.start()
                pair_copy(w, 1 + j).wait_recv()
                lb.start()
                la.wait()
                lb.wait()
                qst[w][j] = (pa[w][...].astype(F32) + pb[w][...].astype(F32)).astype(BF16)
            cp = pltpu.make_async_copy(qst[w], qs[w], lsem.at[n + 2 + w])
            cp.start()
            q_cps.append(cp)
        for w in range(n):
            pair_copy(w, 0).wait_recv()
        for cp in pair_cps:
            cp.wait_send()
        for cp in own_cps + q_cps:
            cp.wait()

    any_spec = pl.BlockSpec(memory_space=pl.ANY)
    outs = pl.pallas_call(
        body, name=name,
        out_shape=[jax.ShapeDtypeStruct(b, BF16) for b in blks]
        + [jax.ShapeDtypeStruct((4, *b), BF16) for b in blks]
        + [jax.ShapeDtypeStruct((3, *b), BF16) for b in blks],
        in_specs=[any_spec] * n,
        out_specs=[any_spec] * (3 * n),
        scratch_shapes=[pltpu.VMEM(b, BF16) for b in blks] + [pltpu.VMEM(b, BF16) for b in blks]
        + [pltpu.VMEM((3, *b), BF16) for b in blks]
        + [pltpu.SemaphoreType.DMA((4 * n,)), pltpu.SemaphoreType.DMA((4 * n,)), pltpu.SemaphoreType.DMA((2 * n + 2,))],
        compiler_params=pltpu.CompilerParams(vmem_limit_bytes=40 * MIB),
    )(*parts)
    return outs[:n], outs[n:2 * n], outs[2 * n:3 * n]


def _adamw_math(g, w, m, v):
    m = ADAM_B1 * m + (1.0 - ADAM_B1) * g
    v = ADAM_B2 * v + (1.0 - ADAM_B2) * (g * g)
    m_hat = m / (1.0 - ADAM_B1 ** ADAM_STEP)
    v_hat = v / (1.0 - ADAM_B2 ** ADAM_STEP)
    delta = -ADAM_LR * (m_hat / (jnp.sqrt(v_hat) + ADAM_EPS) + ADAM_WD * w)
    return delta, m, v


def _adamw_multi(name, own, sib, rel, ws, ms, vs, row_grid):
    k_n, r_n, c_n = own.shape
    rbk = r_n // row_grid

    def body(*refs):
        own_ref, sib_ref, r0_ref, r1_ref, r2_ref = refs[:5]
        w_refs, m_refs, v_refs = refs[5:5 + k_n], refs[5 + k_n:5 + 2 * k_n], refs[5 + 2 * k_n:5 + 3 * k_n]
        outs = refs[5 + 3 * k_n:]
        for k in range(k_n):
            g = own_ref[k].astype(F32) + sib_ref[k].astype(F32)
            g = g + r0_ref[k].astype(F32)
            g = g + r1_ref[k].astype(F32)
            g = g + r2_ref[k].astype(F32)
            delta, mm, vv = _adamw_math(g, w_refs[k][0], m_refs[k][0], v_refs[k][0])
            outs[4 * k][0] = g
            outs[4 * k + 1][0] = delta
            outs[4 * k + 2][0] = mm
            outs[4 * k + 3][0] = vv

    def lead(j):
        return pl.BlockSpec((None, k_n, rbk, c_n), lambda g: (j, 0, g, 0))

    wspec = pl.BlockSpec((1, rbk, c_n), lambda g: (0, g, 0))
    shp = jax.ShapeDtypeStruct((1, r_n, c_n), F32)
    res = pl.pallas_call(
        body, name=name, grid=(row_grid,),
        in_specs=[pl.BlockSpec((k_n, rbk, c_n), lambda g: (0, g, 0)), lead(0), lead(0), lead(1), lead(2)] + [wspec] * (3 * k_n),
        out_specs=[wspec] * (4 * k_n), out_shape=[shp] * (4 * k_n),
        compiler_params=_params(("arbitrary",), 40),
    )(own, sib, rel, rel, rel, *ws, *ms, *vs)
    return [tuple(res[4 * k:4 * k + 4]) for k in range(k_n)]


def _adamw_meta_dw(own, sib, rel, meta, dw):
    def body(own_ref, sib_ref, rel_ref, wm, mm, vm, wd, md, vd, *outs):
        def gsum(rows):
            g = own_ref[rows, :].astype(F32) + sib_ref[0, rows, :].astype(F32)
            for j in range(3):
                g = g + rel_ref[j, rows, :].astype(F32)
            return g

        g = gsum(pl.ds(0, N_META))
        delta, m2, v2 = _adamw_math(g, wm[...], mm[...], vm[...])
        for o, val in zip(outs[:4], (g, delta, m2, v2)):
            o[...] = val
        g = gsum(pl.ds(N_META, CONV_K))
        delta, m2, v2 = _adamw_math(g, wd[0], md[0], vd[0])
        for o, val in zip(outs[4:], (g, delta, m2, v2)):
            o[0] = val

    s_meta = jax.ShapeDtypeStruct(meta[0].shape, F32)
    s_dw = jax.ShapeDtypeStruct(dw[0].shape, F32)
    res = pl.pallas_call(body, name="adamw_meta_dw", out_shape=[s_meta] * 4 + [s_dw] * 4)(own, sib, rel, *meta, *dw)
    return tuple(res[:4]), tuple(res[4:])


REP_ROWS = 16


def _adamw_rep(gathered, ws, ms, vs):
    rows = [(0, 1), (1, 2), (3, 1), (4, 1), (5, 1), (6, 1), (7, 1), (8, 1)]

    def body(g_ref, *refs):
        w_refs, m_refs, v_refs = refs[:8], refs[8:16], refs[16:24]
        loss_ref, outs, acc = refs[24], refs[25:57], refs[57]
        g = g_ref[0]
        for d in range(1, NDEV):
            g = g + g_ref[d]
        acc[...] = g
        loss_ref[...] = (0.5 / D) * jnp.sum(acc[pl.ds(9, 1), :], axis=1, keepdims=True)
        for p, (r0, nr) in enumerate(rows):
            for h in range(nr):
                cols = pl.ds(h * D, D)
                gp = acc[pl.ds(r0 + h, 1), :]
                delta, mm, vv = _adamw_math(gp, w_refs[p][:, cols], m_refs[p][:, cols], v_refs[p][:, cols])
                for o, val in zip(outs[4 * p:4 * p + 4], (gp, delta, mm, vv)):
                    o[:, cols] = val

    shapes = [jax.ShapeDtypeStruct(w.shape, F32) for w in ws]
    res = pl.pallas_call(
        body, name="adamw_rep",
        out_shape=[jax.ShapeDtypeStruct((1, 1), F32)] + [s for s in shapes for _ in range(4)],
        scratch_shapes=[pltpu.VMEM((REP_ROWS, D), F32)],
    )(gathered, *ws, *ms, *vs)
    return res[0], [tuple(res[1 + 4 * p:5 + 4 * p]) for p in range(8)]


def _gu_pairs(w_hbm, w_vm):
    half = NDEV // 2
    return [(w_hbm.at[d, i], w_vm.at[i, d // half, pl.ds(FFB * (d % half), FFB), :])
            for i in range(2) for d in range(NDEV)]


def _win_pairs(w_hbm, w_vm):
    half = NDEV // 2
    return [(w_hbm.at[d], w_vm.at[d // half, :, pl.ds(INB * (d % half), INB)]) for d in range(NDEV)]


def _whole(a):
    nd = a.ndim
    return pl.BlockSpec(a.shape, lambda *g: (0,) * nd)


def _fwd_in(x2, tail, g_mix, w_g, tp, ag):
    tm = _pick(tp, TM_IO)
    nt = tp // tm
    nx_last = tm - tail.shape[0]
    na, ng = len(ag.arrays), ag.n
    half = NDEV // 2

    def body(*refs):
        x_ref, tail_ref, g_ref, w_hbm = refs[:4]
        h_ref, z_ref, u_ref = refs[4 + na:7 + na]
        w_vm, sems = refs[7 + na + ng:9 + na + ng]
        ag.bind(refs[4:4 + na], refs[7 + na:7 + na + ng], refs[9 + na + ng:])
        i, j = pl.program_id(0), pl.program_id(1)
        first = (i == 0) & (j == 0)

        @pl.when(first)
        def _():
            ag.issue()

        @pl.when((i == max(nt - 2, 0)) & (j == 0))
        def _():
            ag.forward()

        _load_once(first, _win_pairs(w_hbm, w_vm), sems)

        @pl.when((j == 0) & (i < nt - 1))
        def _():
            h_ref[...] = x_ref[...]

        @pl.when((j == 0) & (i == nt - 1))
        def _():
            h_ref[pl.ds(0, nx_last), :] = x_ref[pl.ds(0, nx_last), :]
            h_ref[pl.ds(nx_last, tm - nx_last), :] = tail_ref[...]

        @pl.when(j == 0)
        def _():
            xv = h_ref[...]
            r = lax.rsqrt(jnp.mean(xv * xv, axis=-1, keepdims=True) + RMS_EPS)
            u_ref[...] = (xv * r * g_ref[...]).astype(BF16)

        z_ref[...] = _dot(u_ref[...], w_vm[j])

        @pl.when((i == nt - 1) & (j == 1))
        def _():
            ag.finish()

    tile = pl.BlockSpec((tm, D), lambda i, j: (i, 0))
    res = pl.pallas_call(
        body, name="fwd_in", grid=(nt, 2),
        in_specs=[tile, pl.BlockSpec(tail.shape, lambda i, j: (0, 0)), pl.BlockSpec((1, D), lambda i, j: (0, 0)),
                  pl.BlockSpec(memory_space=pl.ANY)] + [_whole(a) for a in ag.arrays],
        out_specs=[tile, pl.BlockSpec((tm, DIN // 2), lambda i, j: (i, j)), tile] + [pl.BlockSpec(memory_space=pl.ANY)] * ng,
        out_shape=[jax.ShapeDtypeStruct((tp, D), F32), jax.ShapeDtypeStruct((tp, DIN), F32),
                   jax.ShapeDtypeStruct((tp, D), BF16)] + ag.out_shape,
        scratch_shapes=[pltpu.VMEM((2, D, DIN // 2), BF16), pltpu.SemaphoreType.DMA((NDEV,))] + ag.scratch,
        compiler_params=_params(("arbitrary", "arbitrary"), 56),
    )(x2, tail, g_mix, w_g, *ag.arrays)
    return res[:3], res[3:]


def _halo_specs(col, nt, width=D):
    r = TM // HALO
    nb = nt * r
    return [pl.BlockSpec((HALO, width), lambda i: ((i * r + nb - 1) % nb, col)),
            pl.BlockSpec((TM, width), lambda i: (i, col)),
            pl.BlockSpec((HALO, width), lambda i: (((i + 1) * r) % nb, col))]


def _fill_ext(ext_ref, left, cur, right):
    ext_ref[pl.ds(0, HALO), :] = left
    ext_ref[pl.ds(HALO, TM), :] = cur
    ext_ref[pl.ds(HALO + TM, HALO), :] = right


def _shift_copies(ext_ref, sh_ref, c0):
    for r in range(8):
        sh_ref[r] = ext_ref[pl.ds(r, TM + 24), pl.ds(c0, CW)]


def _tap_rows(w_ref, w8):
    for k in range(CONV_K):
        w8[k] = jnp.broadcast_to(w_ref[pl.ds(k, 1), :], (8, D))


def _pool_cnt(i, seq, tp, left, right, rows, row0):
    b = i * TM + row0 + lax.broadcasted_iota(jnp.int32, (rows, 1), 0)
    b = jnp.where(b < 0, b + tp, b)
    b = jnp.where(b >= tp, b - tp, b)
    t = jnp.where(b < seq, b + N_META, b - (tp - N_META))
    lo = jnp.maximum(t - left, 0)
    hi = jnp.minimum(t + right + 1, seq + N_META)
    return jnp.maximum(hi - lo, 1).astype(F32)


def _seq_fwd(z, w_dw, b_dw, seq, gat):
    tp = z.shape[0]
    nt = tp // TM
    na, ng = len(gat.arrays), gat.n

    def body(*refs):
        av_l, av, av_r, ag_l, ag, ag_r, p_l, p, p_r, w_ref, b_ref = refs[:11]
        ac_ref, m_ref = refs[11 + na:13 + na]
        a_ext, p_ext, sh, w8 = refs[13 + na + ng:17 + na + ng]
        gat.bind(refs[11:11 + na], refs[13 + na:13 + na + ng], refs[17 + na + ng:])
        i = pl.program_id(0)

        @pl.when(i == 0)
        def _():
            gat.issue()
            _tap_rows(w_ref, w8)

        @pl.when(i == max(nt - 2, 0))
        def _():
            gat.forward()

        _fill_ext(a_ext, av_l[...] * _sig(ag_l[...]), av[...] * _sig(ag[...]), av_r[...] * _sig(ag_r[...]))
        _fill_ext(p_ext, p_l[...], p[...], p_r[...])
        for c0 in range(0, D, CW):
            _shift_copies(a_ext, sh, c0)

            def rows(j, carry):
                base = pl.multiple_of(j * RB, RB)
                acc = jnp.broadcast_to(b_ref[:, pl.ds(c0, CW)], (RB // 8, 8, CW))
                for k in range(CONV_K):
                    q, r = divmod(k + 1, 8)
                    slab = sh[r, pl.ds(pl.multiple_of(base + 8 * q, 8), RB), :].reshape(RB // 8, 8, CW)
                    acc = acc + slab * w8[k, :, pl.ds(c0, CW)]
                ac_ref[pl.ds(base, RB), pl.ds(c0, CW)] = acc.reshape(RB, CW)
                return carry

            lax.fori_loop(0, TM // RB, rows, 0)
        for g, win in enumerate(POOL_WINDOWS):
            left = win // 2
            right = win - 1 - left
            cols = pl.ds(g * PG, PG)
            s = p_ext[pl.ds(HALO - left, TM), cols]
            for off in range(-left + 1, right + 1):
                s = s + p_ext[pl.ds(HALO + off, TM), cols]
            cnt = _pool_cnt(i, seq, tp, left, right, TM, 0)
            m_ref[:, cols] = (s / cnt - p_ext[pl.ds(HALO, TM), cols]).astype(BF16)

        @pl.when(i == nt - 1)
        def _():
            gat.finish()

    res = pl.pallas_call(
        body, name="seq_fwd", grid=(nt,),
        in_specs=_halo_specs(0, nt) + _halo_specs(1, nt) + _halo_specs(2, nt)
        + [pl.BlockSpec((32, D), lambda i: (0, 0)), pl.BlockSpec((1, D), lambda i: (0, 0))] + [_whole(a) for a in gat.arrays],
        out_specs=[pl.BlockSpec((TM, D), lambda i: (i, 0))] * 2 + [pl.BlockSpec(memory_space=pl.ANY)] * ng,
        out_shape=[jax.ShapeDtypeStruct((tp, D), F32), jax.ShapeDtypeStruct((tp, D), BF16)] + gat.out_shape,
        scratch_shapes=[pltpu.VMEM((TM + 2 * HALO, D), F32), pltpu.VMEM((TM + 2 * HALO, D), F32),
                        pltpu.VMEM((8, TM + 24, CW), F32), pltpu.VMEM((CONV_K, 8, D), F32)] + gat.scratch,
        compiler_params=_params(("arbitrary",), 52),
    )(z, z, z, z, z, z, z, z, z, w_dw, b_dw, *gat.arrays)
    return res[:2], res[2:]


def _ln_stats(ac):
    mu = jnp.mean(ac, axis=-1, keepdims=True)
    xc = ac - mu
    rl = lax.rsqrt(jnp.mean(xc * xc, axis=-1, keepdims=True) + LN_EPS)
    return xc * rl, rl


def _pool_mix(m, wp_ref):
    return jnp.concatenate(
        [_dot(m[:, g * PG:(g + 1) * PG], wp_ref[:, g].reshape(PG, PG)) for g in range(4)], axis=1)


def _mix_fwd(ac, m, z, h0, b_gate, ln_g, ln_b, pool_scale, g_mixw, g_pool):
    tp = h0.shape[0]
    nt = tp // TMS

    def body(ac_ref, m_ref, zga, zgb, h_ref, bg_ref, lg_ref, lb_ref, ps_ref, wm_hbm, wp_hbm,
             h1_ref, s_ref, yc_ref, yp_ref, mg_ref, q_ref, wm, wp, sems):
        _load_once(pl.program_id(0) == 0, [(wm_hbm, wm), (wp_hbm, wp)], sems)
        n, _ = _ln_stats(ac_ref[...])
        l = n * lg_ref[...] + lb_ref[...]
        s = (l * _sig(l)).astype(BF16)
        s_ref[...] = s
        yc = _dot(s, wm[:, 0].reshape(D, D))
        q = (_pool_mix(m_ref[...], wp) * ps_ref[...]).astype(BF16)
        q_ref[...] = q
        yp = _dot(q, wm[:, 1].reshape(D, D))
        ga = _sig(zga[...] + bg_ref[:, :D])
        gb = _sig(zgb[...] + bg_ref[:, D:])
        merged = (ga * yc + gb * yp).astype(BF16)
        yc_ref[...] = yc
        yp_ref[...] = yp
        mg_ref[...] = merged
        h1_ref[...] = h_ref[...] + _dot(merged, wm[:, 2].reshape(D, D))

    def tile(col=0):
        return pl.BlockSpec((TMS, D), lambda i: (i, col))

    def vec(w):
        return pl.BlockSpec((1, w), lambda i: (0, 0))

    anys = pl.BlockSpec(memory_space=pl.ANY)
    f32o, b16o = jax.ShapeDtypeStruct((tp, D), F32), jax.ShapeDtypeStruct((tp, D), BF16)
    return pl.pallas_call(
        body, name="mix_fwd", grid=(nt,),
        in_specs=[tile(), tile(), tile(3), tile(4), tile(), vec(2 * D), vec(D), vec(D), vec(D), anys, anys],
        out_specs=[tile()] * 6,
        out_shape=[f32o, b16o, f32o, f32o, b16o, b16o],
        scratch_shapes=[pltpu.VMEM((NDEV, 3, D // NDEV, D), BF16), pltpu.VMEM((NDEV, 4, PG // NDEV, PG), BF16),
                        pltpu.SemaphoreType.DMA((2,))],
        compiler_params=_params(("arbitrary",), 48),
    )(ac, m, z, z, h0, b_gate, ln_g, ln_b, pool_scale, g_mixw, g_pool)


def _ffn_fwd(h1, tgt, g_ffn, g_final, w_gu, w_dn):
    tp = h1.shape[0]
    nt = tp // TM
    nx_last = tgt.shape[0] - (nt - 1) * TM

    def body(h_ref, t_ref, gf_ref, gl_ref, wgu_hbm, wdn_hbm,
             fg_ref, fu_ref, v_ref, f_ref, dh2_ref, acc_ref, wgu, wdn, v_sc, h2_sc, diff_sc, sems):
        i, j = pl.program_id(0), pl.program_id(1)
        _load_once((i == 0) & (j == 0), _gu_pairs(wgu_hbm, wgu) + [(wdn_hbm, wdn)], sems)

        @pl.when((i == 0) & (j == 0))
        def _():
            acc_ref[...] = jnp.zeros_like(acc_ref)

        @pl.when(j == 0)
        def _():
            h = h_ref[...]
            r = lax.rsqrt(jnp.mean(h * h, axis=-1, keepdims=True) + RMS_EPS)
            v = (h * r * gf_ref[...]).astype(BF16)
            v_sc[...] = v
            v_ref[...] = v
            h2_sc[...] = h

        v = v_sc[...]
        fg = _dot_nt(v, wgu[0, j])
        fu = _dot_nt(v, wgu[1, j])
        fg_ref[...] = fg
        fu_ref[...] = fu
        f = ((fg * _sig(fg)) * fu).astype(BF16)
        f_ref[...] = f
        h2_sc[...] += _dot(f, wdn[j])

        @pl.when(j == 1)
        def _():
            h2 = h2_sc[...]
            r = lax.rsqrt(jnp.mean(h2 * h2, axis=-1, keepdims=True) + RMS_EPS)
            n2 = h2 * r
            y = n2 * gl_ref[...]

            @pl.when(i < nt - 1)
            def _():
                diff_sc[...] = y - t_ref[...]

            @pl.when(i == nt - 1)
            def _():
                diff_sc[pl.ds(0, nx_last), :] = y[:nx_last] - t_ref[pl.ds(0, nx_last), :]
                diff_sc[pl.ds(nx_last, TM - nx_last), :] = jnp.zeros((TM - nx_last, D), F32)

            diff = diff_sc[...]
            dy = diff * (1.0 / D)
            acc_ref[0:1, :] += jnp.sum(diff * diff, axis=0, keepdims=True)
            acc_ref[1:2, :] += jnp.sum(dy * n2, axis=0, keepdims=True)
            dn = dy * gl_ref[...]
            dh2_ref[...] = r * (dn - n2 * jnp.mean(dn * n2, axis=-1, keepdims=True))

    def tile():
        return pl.BlockSpec((TM, D), lambda i, j: (i, 0))

    def chunk():
        return pl.BlockSpec((TM, FFC), lambda i, j: (i, j))

    def vec():
        return pl.BlockSpec((1, D), lambda i, j: (0, 0))

    anys = pl.BlockSpec(memory_space=pl.ANY)
    hid32, hid16 = jax.ShapeDtypeStruct((tp, DFF), F32), jax.ShapeDtypeStruct((tp, DFF), BF16)
    return pl.pallas_call(
        body, name="ffn_fwd", grid=(nt, 2),
        in_specs=[tile(), tile(), vec(), vec(), anys, anys],
        out_specs=[chunk(), chunk(), tile(), chunk(), tile(), pl.BlockSpec((8, D), lambda i, j: (0, 0))],
        out_shape=[hid32, hid32, jax.ShapeDtypeStruct((tp, D), BF16), hid16, jax.ShapeDtypeStruct((tp, D), F32),
                   jax.ShapeDtypeStruct((8, D), F32)],
        scratch_shapes=[pltpu.VMEM((2, 2, FFC, D), BF16), pltpu.VMEM((2, FFC, D), BF16),
                        pltpu.VMEM((TM, D), BF16), pltpu.VMEM((TM, D), F32), pltpu.VMEM((TM, D), F32),
                        pltpu.SemaphoreType.DMA((2 * NDEV + 1,))],
        compiler_params=_params(("arbitrary", "arbitrary"), 56),
    )(h1, tgt, g_ffn, g_final, w_gu, w_dn)


def _ffn_bwd(dh2, fg, fu, h1, g_ffn, w_gu, w_dn):
    tp = h1.shape[0]
    nt = tp // TM

    def body(dh2_ref, fg_ref, fu_ref, h_ref, gf_ref, wgu_hbm, wdn_hbm,
             dfg_ref, dfu_ref, dh1_ref, acc_ref, wgu, wdn, d_sc, dv_sc, sems):
        i, j = pl.program_id(0), pl.program_id(1)
        _load_once((i == 0) & (j == 0), _gu_pairs(wgu_hbm, wgu) + [(wdn_hbm, wdn)], sems)

        @pl.when((i == 0) & (j == 0))
        def _():
            acc_ref[...] = jnp.zeros_like(acc_ref)

        @pl.when(j == 0)
        def _():
            d_sc[...] = dh2_ref[...].astype(BF16)
            dv_sc[...] = jnp.zeros_like(dv_sc)

        df = _dot_nt(d_sc[...], wdn[j])
        fg = fg_ref[...]
        sg = _sig(fg)
        dfu = (df * (fg * sg)).astype(BF16)
        dfg = (df * fu_ref[...] * (sg * (1.0 + fg * (1.0 - sg)))).astype(BF16)
        dfg_ref[...] = dfg
        dfu_ref[...] = dfu
        dv_sc[...] += _dot(dfg, wgu[0, j]) + _dot(dfu, wgu[1, j])

        @pl.when(j == 1)
        def _():
            h = h_ref[...]
            r = lax.rsqrt(jnp.mean(h * h, axis=-1, keepdims=True) + RMS_EPS)
            n1 = h * r
            dv = dv_sc[...]
            acc_ref[0:1, :] += jnp.sum(dv * n1, axis=0, keepdims=True)
            dn = dv * gf_ref[...]
            dh1_ref[...] = dh2_ref[...] + r * (dn - n1 * jnp.mean(dn * n1, axis=-1, keepdims=True))

    def tile():
        return pl.BlockSpec((TM, D), lambda i, j: (i, 0))

    def chunk():
        return pl.BlockSpec((TM, FFC), lambda i, j: (i, j))

    anys = pl.BlockSpec(memory_space=pl.ANY)
    hid16 = jax.ShapeDtypeStruct((tp, DFF), BF16)
    return pl.pallas_call(
        body, name="ffn_bwd", grid=(nt, 2),
        in_specs=[tile(), chunk(), chunk(), tile(), pl.BlockSpec((1, D), lambda i, j: (0, 0)), anys, anys],
        out_specs=[chunk(), chunk(), tile(), pl.BlockSpec((8, D), lambda i, j: (0, 0))],
        out_shape=[hid16, hid16, jax.ShapeDtypeStruct((tp, D), F32), jax.ShapeDtypeStruct((8, D), F32)],
        scratch_shapes=[pltpu.VMEM((2, 2, FFC, D), BF16), pltpu.VMEM((2, FFC, D), BF16),
                        pltpu.VMEM((TM, D), BF16), pltpu.VMEM((TM, D), F32), pltpu.SemaphoreType.DMA((2 * NDEV + 1,))],
        compiler_params=_params(("arbitrary", "arbitrary"), 56),
    )(dh2, fg, fu, h1, g_ffn, w_gu, w_dn)


def _mix_bwd(dh1, z, yc, yp, ac, m, b_gate, ln_g, ln_b, pool_scale, g_mixw, g_pool, qs):
    tp = dh1.shape[0]
    nt = tp // TMS
    ex = _ChipExchange(qs)
    nq = ex.n

    def body(*refs):
        dh1_ref, zga, zgb, yc_ref, yp_ref, ac_ref, m_ref, bg_ref, lg_ref, lb_ref, ps_ref, wm_hbm, wp_hbm = refs[:13]
        dac_ref, dm_ref, dzg_ref, dyc_ref, dyp_ref, dm2_ref, acc_ref = refs[13 + nq:20 + nq]
        wm, wp, sems = refs[20 + 2 * nq:23 + 2 * nq]
        ex.bind(refs[13:13 + nq], refs[20 + nq:20 + 2 * nq], refs[23 + 2 * nq:])
        first = pl.program_id(0) == 0

        @pl.when(first)
        def _():
            ex.issue()
            acc_ref[...] = jnp.zeros_like(acc_ref)

        _load_once(first, [(wm_hbm, wm), (wp_hbm, wp)], sems)

        dmerged = _dot_nt(dh1_ref[...].astype(BF16), wm[:, 2].reshape(D, D))
        ga = _sig(zga[...] + bg_ref[:, :D])
        gb = _sig(zgb[...] + bg_ref[:, D:])
        dyc = dmerged * ga
        dyp = dmerged * gb
        dza = (dmerged * yc_ref[...]) * (ga * (1.0 - ga))
        dzb = (dmerged * yp_ref[...]) * (gb * (1.0 - gb))
        dzg_ref[:, :D] = dza.astype(BF16)
        dzg_ref[:, D:] = dzb.astype(BF16)
        acc_ref[0:1, :D] += jnp.sum(dza, axis=0, keepdims=True)
        acc_ref[0:1, D:] += jnp.sum(dzb, axis=0, keepdims=True)
        dyc_b = dyc.astype(BF16)
        dyp_b = dyp.astype(BF16)
        dyc_ref[...] = dyc_b
        dyp_ref[...] = dyp_b
        ds = _dot_nt(dyc_b, wm[:, 0].reshape(D, D))
        n, rl = _ln_stats(ac_ref[...])
        l = n * lg_ref[...] + lb_ref[...]
        sg = _sig(l)
        dl = ds * (sg * (1.0 + l * (1.0 - sg)))
        acc_ref[1:2, :D] += jnp.sum(dl * n, axis=0, keepdims=True)
        acc_ref[1:2, D:] += jnp.sum(dl, axis=0, keepdims=True)
        dn = dl * lg_ref[...]
        dac_ref[...] = rl * (dn - jnp.mean(dn, axis=-1, keepdims=True) - n * jnp.mean(dn * n, axis=-1, keepdims=True))
        dq = _dot_nt(dyp_b, wm[:, 1].reshape(D, D))
        mv = m_ref[...]
        acc_ref[2:3, :D] += jnp.sum(dq * _pool_mix(mv, wp), axis=0, keepdims=True)
        dm2 = (dq * ps_ref[...]).astype(BF16)
        dm2_ref[...] = dm2
        dm_ref[...] = jnp.concatenate(
            [_dot_nt(dm2[:, g * PG:(g + 1) * PG], wp[:, g].reshape(PG, PG)) for g in range(4)], axis=1)

        @pl.when(pl.program_id(0) == nt - 1)
        def _():
            ex.finish()

    def tile(col=0):
        return pl.BlockSpec((TMS, D), lambda i: (i, col))

    def vec(w):
        return pl.BlockSpec((1, w), lambda i: (0, 0))

    anys = pl.BlockSpec(memory_space=pl.ANY)
    f32o, b16o = jax.ShapeDtypeStruct((tp, D), F32), jax.ShapeDtypeStruct((tp, D), BF16)
    res = pl.pallas_call(
        body, name="mix_bwd", grid=(nt,),
        in_specs=[tile(), tile(3), tile(4), tile(), tile(), tile(), tile(), vec(2 * D), vec(D), vec(D), vec(D), anys, anys]
        + [anys] * nq,
        out_specs=[tile(), tile(), pl.BlockSpec((TMS, 2 * D), lambda i: (i, 0)), tile(), tile(), tile(),
                   pl.BlockSpec((8, 2 * D), lambda i: (0, 0))] + [anys] * nq,
        out_shape=[f32o, f32o, jax.ShapeDtypeStruct((tp, 2 * D), BF16), b16o, b16o, b16o,
                   jax.ShapeDtypeStruct((8, 2 * D), F32)] + ex.out_shape,
        scratch_shapes=[pltpu.VMEM((NDEV, 3, D // NDEV, D), BF16), pltpu.VMEM((NDEV, 4, PG // NDEV, PG), BF16),
                        pltpu.SemaphoreType.DMA((2,))] + ex.scratch,
        compiler_params=_params(("arbitrary",), 48),
    )(dh1, z, z, yc, yp, ac, m, b_gate, ln_g, ln_b, pool_scale, g_mixw, g_pool, *qs)
    return res[:7], res[7:]


def _seq_bwd(dac, dm, dzg, z, w_dw, seq, qs):
    tp = z.shape[0]
    nt = tp // TM
    ex = _ChipExchange(qs)
    nq = ex.n

    def body(*refs):
        dac_l, dac_c, dac_r, dm_l, dm_c, dm_r, av_l, av, av_r, ag_l, ag, ag_r, dzg_ref, w_ref = refs[:14]
        dz_ref, acc_ref = refs[14 + nq:16 + nq]
        a_ext, d_ext, m_ext, sha, shd, da_sc, dw_sc, w8 = refs[16 + 2 * nq:24 + 2 * nq]
        ex.bind(refs[14:14 + nq], refs[16 + nq:16 + 2 * nq], refs[24 + 2 * nq:])
        i = pl.program_id(0)

        @pl.when(i == 0)
        def _():
            ex.issue()
            dw_sc[...] = jnp.zeros_like(dw_sc)
            acc_ref[...] = jnp.zeros_like(acc_ref)
            _tap_rows(w_ref, w8)

        sg = _sig(ag[...])
        _fill_ext(a_ext, av_l[...] * _sig(ag_l[...]), av[...] * sg, av_r[...] * _sig(ag_r[...]))
        _fill_ext(d_ext, dac_l[...], dac_c[...], dac_r[...])
        for g, win in enumerate(POOL_WINDOWS):
            left = win // 2
            right = win - 1 - left
            cols = pl.ds(g * PG, PG)
            m_ext[pl.ds(0, HALO), cols] = dm_l[:, cols] / _pool_cnt(i, seq, tp, left, right, HALO, -HALO)
            m_ext[pl.ds(HALO, TM), cols] = dm_c[:, cols] / _pool_cnt(i, seq, tp, left, right, TM, 0)
            m_ext[pl.ds(HALO + TM, HALO), cols] = dm_r[:, cols] / _pool_cnt(i, seq, tp, left, right, HALO, TM)
        for c0 in range(0, D, CW):
            _shift_copies(a_ext, sha, c0)
            _shift_copies(d_ext, shd, c0)

            def rows(j, carry):
                base = pl.multiple_of(j * RB, RB)
                dcur = d_ext[pl.ds(pl.multiple_of(base + HALO, 8), RB), pl.ds(c0, CW)].reshape(RB // 8, 8, CW)
                acc = jnp.zeros((RB // 8, 8, CW), F32)
                for k in range(CONV_K):
                    q, r = divmod(CONV_K - k, 8)
                    slab = shd[r, pl.ds(pl.multiple_of(base + 8 * q, 8), RB), :].reshape(RB // 8, 8, CW)
                    acc = acc + slab * w8[k, :, pl.ds(c0, CW)]
                    q, r = divmod(k + 1, 8)
                    slab = sha[r, pl.ds(pl.multiple_of(base + 8 * q, 8), RB), :].reshape(RB // 8, 8, CW)
                    dw_sc[k, :, pl.ds(c0, CW)] += jnp.sum(dcur * slab, axis=0)
                da_sc[pl.ds(base, RB), pl.ds(c0, CW)] = acc.reshape(RB, CW)
                return carry

            lax.fori_loop(0, TM // RB, rows, 0)
        da = da_sc[...]
        dz_ref[:, 0:D] = (da * sg).astype(BF16)
        dz_ref[:, D:2 * D] = (da * av[...] * (sg * (1.0 - sg))).astype(BF16)
        for g, win in enumerate(POOL_WINDOWS):
            left = win // 2
            right = win - 1 - left
            cols = pl.ds(g * PG, PG)
            s = m_ext[pl.ds(HALO - right, TM), cols]
            for off in range(-right + 1, left + 1):
                s = s + m_ext[pl.ds(HALO + off, TM), cols]
            dz_ref[:, pl.ds(2 * D + g * PG, PG)] = (s - dm_c[:, cols]).astype(BF16)
        dz_ref[:, 3 * D:] = dzg_ref[...]

        @pl.when(i == nt - 1)
        def _():
            for k in range(CONV_K):
                acc_ref[k:k + 1, :] = jnp.sum(dw_sc[k], axis=0, keepdims=True)

        acc_ref[CONV_K:CONV_K + 1, :] += jnp.sum(dac_c[...], axis=0, keepdims=True)

        @pl.when(i == nt - 1)
        def _():
            ex.finish()

    ext = pltpu.VMEM((TM + 2 * HALO, D), F32)
    shs = pltpu.VMEM((8, TM + 24, CW), F32)
    anys = pl.BlockSpec(memory_space=pl.ANY)
    res = pl.pallas_call(
        body, name="seq_bwd", grid=(nt,),
        in_specs=_halo_specs(0, nt) + _halo_specs(0, nt) + _halo_specs(0, nt) + _halo_specs(1, nt)
        + [pl.BlockSpec((TM, 2 * D), lambda i: (i, 0)), pl.BlockSpec((32, D), lambda i: (0, 0))] + [anys] * nq,
        out_specs=[pl.BlockSpec((TM, DIN), lambda i: (i, 0)), pl.BlockSpec((32, D), lambda i: (0, 0))] + [anys] * nq,
        out_shape=[jax.ShapeDtypeStruct((tp, DIN), BF16), jax.ShapeDtypeStruct((32, D), F32)] + ex.out_shape,
        scratch_shapes=[ext, ext, ext, shs, shs, pltpu.VMEM((TM, D), F32), pltpu.VMEM((CONV_K, 8, D), F32),
                        pltpu.VMEM((CONV_K, 8, D), F32)] + ex.scratch,
        compiler_params=_params(("arbitrary",), 48),
    )(dac, dac, dac, dm, dm, dm, z, z, z, z, z, z, dzg, w_dw, *qs)
    return res[:2], res[2:]


def _in_bwd(dz, h0, dh1, g_mix, w_g, seq, qs):
    tp = h0.shape[0]
    tm = _pick(tp, TM_IO)
    nt = tp // tm
    ex = _ChipExchange(qs)
    nq = ex.n

    def body(*refs):
        dz_ref, h_ref, dh1_ref, g_ref, w_hbm = refs[:5]
        gx_ref, gmeta_ref, acc_ref = refs[5 + nq:8 + nq]
        w_vm, sems = refs[8 + 2 * nq:10 + 2 * nq]
        ex.bind(refs[5:5 + nq], refs[8 + nq:8 + 2 * nq], refs[10 + 2 * nq:])
        i = pl.program_id(0)

        @pl.when(i == 0)
        def _():
            ex.issue()
            acc_ref[...] = jnp.zeros_like(acc_ref)

        _load_once(i == 0, _win_pairs(w_hbm, w_vm), sems)

        du = _dot_nt(dz_ref[:, :DIN // 2], w_vm[0]) + _dot_nt(dz_ref[:, DIN // 2:], w_vm[1])
        h = h_ref[...]
        r = lax.rsqrt(jnp.mean(h * h, axis=-1, keepdims=True) + RMS_EPS)
        n0 = h * r
        acc_ref[0:1, :] += jnp.sum(du * n0, axis=0, keepdims=True)
        dn = du * g_ref[...]
        gx_ref[...] = dh1_ref[...] + r * (dn - n0 * jnp.mean(dn * n0, axis=-1, keepdims=True))

        @pl.when(i == nt - 1)
        def _():
            gmeta_ref[...] = gx_ref[pl.ds(tm - N_META, N_META), :]
            ex.finish()

    tile = pl.BlockSpec((tm, D), lambda i: (i, 0))
    anys = pl.BlockSpec(memory_space=pl.ANY)
    res = pl.pallas_call(
        body, name="in_bwd", grid=(nt,),
        in_specs=[pl.BlockSpec((tm, DIN), lambda i: (i, 0)), tile, tile, pl.BlockSpec((1, D), lambda i: (0, 0)), anys]
        + [anys] * nq,
        out_specs=[tile, pl.BlockSpec((N_META, D), lambda i: (0, 0)), pl.BlockSpec((8, D), lambda i: (0, 0))] + [anys] * nq,
        out_shape=[jax.ShapeDtypeStruct((seq, D), F32), jax.ShapeDtypeStruct((N_META, D), F32),
                   jax.ShapeDtypeStruct((8, D), F32)] + ex.out_shape,
        scratch_shapes=[pltpu.VMEM((2, D, DIN // 2), BF16), pltpu.SemaphoreType.DMA((NDEV,))] + ex.scratch,
        compiler_params=_params(("arbitrary",), 58),
    )(dz, h0, dh1, g_mix, w_g, *qs)
    return res[:3], res[3:]


def _wgrad_in(u, dz):
    tp = u.shape[0]
    tm = _pick(tp, TM_WG)
    nt = tp // tm
    half = DIN // 2

    def body(u_ref, dz_ref, o_ref, acc):
        t = pl.program_id(1)

        @pl.when(t == 0)
        def _():
            acc[...] = jnp.zeros_like(acc)

        acc[...] += _dot_tn(u_ref[...], dz_ref[...])

        @pl.when(t == nt - 1)
        def _():
            for d in range(4):
                o_ref[d] = acc[:, INB * d:INB * (d + 1)].astype(BF16)

    return pl.pallas_call(
        body, name="wgrad_in", grid=(2, nt),
        in_specs=[pl.BlockSpec((tm, D), lambda h, t: (t, 0)), pl.BlockSpec((tm, half), lambda h, t: (t, h))],
        out_specs=pl.BlockSpec((4, D, INB), lambda h, t: (h, 0, 0), pipeline_mode=pl.Buffered(1)),
        out_shape=jax.ShapeDtypeStruct((NDEV, D, INB), BF16),
        scratch_shapes=[pltpu.VMEM((D, half), F32)],
        compiler_params=_params(("arbitrary", "arbitrary"), 52),
    )(u, dz)


def _wgrad_mix(s, dyc, q, dyp, merged, dh1, m, dm2):
    tp = s.shape[0]
    tm = _pick(tp, TM_WM)
    nt = tp // tm
    rb = D // NDEV

    def body(s_ref, dyc_ref, q_ref, dyp_ref, mg_ref, dh1_ref, m_ref, dm2_ref, o_ref, op_ref, acc, accp):
        t = pl.program_id(0)

        @pl.when(t == 0)
        def _():
            acc[...] = jnp.zeros_like(acc)
            accp[...] = jnp.zeros_like(accp)

        acc[0] += _dot_tn(s_ref[...], dyc_ref[...])
        acc[1] += _dot_tn(q_ref[...], dyp_ref[...])
        acc[2] += _dot_tn(mg_ref[...], dh1_ref[...].astype(BF16))
        for g in range(4):
            accp[g] += _dot_tn(m_ref[:, g * PG:(g + 1) * PG], dm2_ref[:, g * PG:(g + 1) * PG])

        @pl.when(t == nt - 1)
        def _():
            for d in range(NDEV):
                for k in range(3):
                    o_ref[d, k] = acc[k, rb * d:rb * (d + 1), :].astype(BF16)
                for g in range(4):
                    op_ref[d, g] = accp[g, 32 * d:32 * (d + 1), :].astype(BF16)

    tile = pl.BlockSpec((tm, D), lambda t: (t, 0))
    return pl.pallas_call(
        body, name="wgrad_mix", grid=(nt,),
        in_specs=[tile] * 8,
        out_specs=[pl.BlockSpec((NDEV, 3, rb, D), lambda t: (0, 0, 0, 0), pipeline_mode=pl.Buffered(1)),
                   pl.BlockSpec((NDEV, 4, 32, PG), lambda t: (0, 0, 0, 0), pipeline_mode=pl.Buffered(1))],
        out_shape=[jax.ShapeDtypeStruct((NDEV, 3, rb, D), BF16), jax.ShapeDtypeStruct((NDEV, 4, 32, PG), BF16)],
        scratch_shapes=[pltpu.VMEM((3, D, D), F32), pltpu.VMEM((4, PG, PG), F32)],
        compiler_params=_params(("arbitrary",), 56),
    )(s, dyc, q, dyp, merged, dh1, m, dm2)


def _wgrad_gu(v, dfg, dfu):
    tp = v.shape[0]
    tm = _pick(tp, TM_WG)
    nt = tp // tm

    def body(v_ref, dg_ref, du_ref, o_ref, acc):
        k, t = pl.program_id(0), pl.program_id(2)

        @pl.when(t == 0)
        def _():
            acc[...] = jnp.zeros_like(acc)

        @pl.when(k == 0)
        def _():
            acc[...] += _dot_tn(dg_ref[...], v_ref[...])

        @pl.when(k == 1)
        def _():
            acc[...] += _dot_tn(du_ref[...], v_ref[...])

        @pl.when(t == nt - 1)
        def _():
            for d in range(4):
                o_ref[d] = acc[FFB * d:FFB * (d + 1), :].astype(BF16)

    return pl.pallas_call(
        body, name="wgrad_gu", grid=(2, 2, nt),
        in_specs=[pl.BlockSpec((tm, D), lambda k, h, t: (t, 0)),
                  pl.BlockSpec((tm, FFC), lambda k, h, t: (t * (1 - k), h * (1 - k))),
                  pl.BlockSpec((tm, FFC), lambda k, h, t: (t * k, h * k))],
        out_specs=pl.BlockSpec((4, None, FFB, D), lambda k, h, t: (h, k, 0, 0), pipeline_mode=pl.Buffered(1)),
        out_shape=jax.ShapeDtypeStruct((NDEV, 2, FFB, D), BF16),
        scratch_shapes=[pltpu.VMEM((FFC, D), F32)],
        compiler_params=_params(("arbitrary",) * 3, 48),
    )(v, dfg, dfu)


def _wgrad_down(f, dh2):
    tp = f.shape[0]
    tm = _pick(tp, TM_WG)
    nt = tp // tm

    def body(f_ref, d_ref, o_ref, acc):
        t = pl.program_id(1)

        @pl.when(t == 0)
        def _():
            acc[...] = jnp.zeros_like(acc)

        acc[...] += _dot_tn(f_ref[...], d_ref[...].astype(BF16))

        @pl.when(t == nt - 1)
        def _():
            for d in range(4):
                o_ref[d] = acc[FFB * d:FFB * (d + 1), :].astype(BF16)

    return pl.pallas_call(
        body, name="wgrad_down", grid=(2, nt),
        in_specs=[pl.BlockSpec((tm, FFC), lambda h, t: (t, h)), pl.BlockSpec((tm, D), lambda h, t: (t, 0))],
        out_specs=pl.BlockSpec((4, FFB, D), lambda h, t: (h, 0, 0), pipeline_mode=pl.Buffered(1)),
        out_shape=jax.ShapeDtypeStruct((NDEV, FFB, D), BF16),
        scratch_shapes=[pltpu.VMEM((FFC, D), F32)],
        compiler_params=_params(("arbitrary", "arbitrary"), 48),
    )(f, dh2)


def kernel(x, meta_tokens, g_mix, w_in, b_gate, w_dw, b_dw, ln_g, ln_b, w_conv_out, w_pool, pool_scale, w_pool_out, w_o, g_ffn, w_ffn_gate, w_ffn_up, w_ffn_down, g_final, loss_target, m_meta_tokens, m_g_mix, m_w_in, m_b_gate, m_w_dw, m_b_dw, m_ln_g, m_ln_b, m_w_conv_out, m_w_pool, m_pool_scale, m_w_pool_out, m_w_o, m_g_ffn, m_w_ffn_gate, m_w_ffn_up, m_w_ffn_down, m_g_final, v_meta_tokens, v_g_mix, v_w_in, v_b_gate, v_w_dw, v_b_dw, v_ln_g, v_ln_b, v_w_conv_out, v_w_pool, v_pool_scale, v_w_pool_out, v_w_o, v_g_ffn, v_w_ffn_gate, v_w_ffn_up, v_w_ffn_down, v_g_final):
    seq = x.shape[1]
    tp = -(-(seq + 2 * HALO) // TM) * TM
    tm_in = _pick(tp, TM_IO)
    nx_last = seq - (tp // tm_in - 1) * tm_in
    assert 0 < nx_last <= tm_in - 2 * HALO and nx_last % 8 == 0 and 0 < seq - (tp // TM - 1) * TM

    whole = (Ellipsis,)
    g_in, g_small = _all_gather(
        [((D, INB), [(w_in, whole, 0)]),
         ((48, D // NDEV), [(meta_tokens, pl.ds(0, N_META), whole), (w_dw, pl.ds(N_META, CONV_K), 0)])], [BF16, F32])
    ag_mix = _Gather([((3, D // NDEV, D), [(w_conv_out, 0, 0), (w_pool_out, 1, 0), (w_o, 2, 0)]),
                      ((4, PG // NDEV, PG), [(w_pool, whole, 0)])], [BF16, BF16])
    def tr(a):
        return jnp.swapaxes(a, 1, 2)

    ag_ffn = _Gather([((2, FFB, D), [(tr(w_ffn_gate), 0, 0), (tr(w_ffn_up), 1, 0)]),
                      ((FFB, D), [(w_ffn_down, whole, 0)])], [BF16, BF16])
    small_full = g_small.transpose(1, 0, 2).reshape(48, D)
    wdw_full = small_full[N_META:]
    tail = jnp.concatenate([jnp.zeros((tm_in - nx_last - N_META, D), F32), small_full[:N_META]], axis=0)

    (h0, z, u), (g_mixw, g_pool) = _fwd_in(x[0], tail, g_mix, g_in, tp, ag_mix)
    (ac, m), (w_gu, g_down) = _seq_fwd(z, wdw_full, b_dw, seq, ag_ffn)
    w_dn = g_down.reshape(2, FFC, D)
    h1, s, yc, yp, merged, q = _mix_fwd(ac, m, z, h0, b_gate, ln_g, ln_b, pool_scale, g_mixw, g_pool)
    fg, fu, v, f, dh2, head_acc = _ffn_fwd(h1, loss_target[0], g_ffn, g_final.reshape(1, D), w_gu, w_dn)

    dfg, dfu, dh1, ffn_acc = _ffn_bwd(dh2, fg, fu, h1, g_ffn, w_gu, w_dn)
    own_f, sib_f, q_f = _rs_pair("rs_pair_ffn", [_wgrad_gu(v, dfg, dfu), _wgrad_down(f, dh2)])
    (dac, dm, dzg, dyc, dyp, dm2, mix_acc), rel_f = _mix_bwd(
        dh1, z, yc, yp, ac, m, b_gate, ln_g, ln_b, pool_scale, g_mixw, g_pool, q_f)
    own_m, sib_m, q_m = _rs_pair("rs_pair_mix", list(_wgrad_mix(s, dyc, q, dyp, merged, dh1, m, dm2)))
    (dz, seq_acc), rel_m = _seq_bwd(dac, dm, dzg, z, wdw_full, seq, q_m)
    own_i, sib_i, q_i = _rs_pair("rs_pair_in", [_wgrad_in(u, dz)])
    (grad_x, g_meta, in_acc), rel_i = _in_bwd(dz, h0, dh1, g_mix, g_in, seq, q_i)
    small_g = jnp.concatenate([g_meta, seq_acc[:CONV_K], jnp.zeros((1, D), F32)], axis=0)
    p_small = small_g.reshape(48, NDEV, D // NDEV).transpose(1, 0, 2).astype(BF16)
    rep_g = jnp.concatenate([
        in_acc[0:1], mix_acc[0:1, :D], mix_acc[0:1, D:], seq_acc[CONV_K:CONV_K + 1], mix_acc[1:2, :D], mix_acc[1:2, D:],
        mix_acc[2:3, :D], ffn_acc[0:1], head_acc[1:2], head_acc[0:1], jnp.zeros((REP_ROWS - 10, D), F32)], axis=0)
    own_s, sib_s, rel_s, rep_all = _reduce_scatter([p_small], rep_g)
    owns = [own_i[0], own_s[0], own_m[0], own_m[1], own_f[0], own_f[1]]
    sibs = [sib_i[0], sib_s[0], sib_m[0], sib_m[1], sib_f[0], sib_f[1]]
    rels = [rel_i[0], rel_s[0], rel_m[0], rel_m[1], rel_f[0], rel_f[1]]

    def lead(a):
        return a.reshape(1, *a.shape)

    def stack4(a, lead_dims):
        return a.reshape(*lead_dims, 1, 4 * 32, PG)

    (r_in,) = _adamw_multi("adamw_in", lead(owns[0]), sibs[0][:, None], rels[0][:, None], [w_in], [m_w_in], [v_w_in], 4)
    r_meta, r_dw = _adamw_meta_dw(owns[1], sibs[1], rels[1], (meta_tokens, m_meta_tokens, v_meta_tokens),
                                  (w_dw, m_w_dw, v_w_dw))
    r_conv, r_pout, r_o = _adamw_multi("adamw_mix", owns[2], sibs[2], rels[2], [w_conv_out, w_pool_out, w_o],
                                       [m_w_conv_out, m_w_pool_out, m_w_o], [v_w_conv_out, v_w_pool_out, v_w_o], 1)
    (r_pool,) = _adamw_multi("adamw_pool", stack4(owns[3], ()), stack4(sibs[3], (4,)), stack4(rels[3], (3,)),
                             [w_pool.reshape(1, 128, PG)], [m_w_pool.reshape(1, 128, PG)], [v_w_pool.reshape(1, 128, PG)], 1)
    r_pool = tuple(a.reshape(w_pool.shape) for a in r_pool)
    r_gate, r_up = _adamw_multi("adamw_gu", owns[4], sibs[4], rels[4], [tr(w_ffn_gate), tr(w_ffn_up)],
                                [tr(m_w_ffn_gate), tr(m_w_ffn_up)], [tr(v_w_ffn_gate), tr(v_w_ffn_up)], 2)
    r_gate, r_up = tuple(tr(a) for a in r_gate), tuple(tr(a) for a in r_up)
    (r_down,) = _adamw_multi("adamw_down", lead(owns[5]), sibs[5][:, None], rels[5][:, None],
                             [w_ffn_down], [m_w_ffn_down], [v_w_ffn_down], 2)
    row = (1, D)
    loss, reps = _adamw_rep(
        rep_all,
        [g_mix, b_gate, b_dw, ln_g, ln_b, pool_scale, g_ffn, g_final.reshape(row)],
        [m_g_mix, m_b_gate, m_b_dw, m_ln_g, m_ln_b, m_pool_scale, m_g_ffn, m_g_final.reshape(row)],
        [v_g_mix, v_b_gate, v_b_dw, v_ln_g, v_ln_b, v_pool_scale, v_g_ffn, v_g_final.reshape(row)])
    r_gmix, r_bg, r_bdw, r_lg, r_lb, r_ps, r_gffn, r_gfin = reps
    r_gfin = tuple(a.reshape(D) for a in r_gfin)

    in_order = [r_meta, r_gmix, r_in, r_bg, r_dw, r_bdw, r_lg, r_lb, r_conv, r_pool, r_ps, r_pout, r_o, r_gffn,
                r_gate, r_up, r_down, r_gfin]
    return (loss.reshape(()), grad_x[None], *[r[0] for r in in_order], *[r[1] for r in in_order],
            *[r[2] for r in in_order], *[r[3] for r in in_order])
```

```python
import math

import jax
import jax.numpy as jnp
from jax import lax
from jax.experimental import pallas as pl
from jax.experimental.pallas import tpu as pltpu

F32, BF16 = jnp.float32, jnp.bfloat16
MESH_ID = pl.DeviceIdType.MESH
NDEV = 8

D = 1024
N_META = 16
CONV_K = 31
HALO = 16
POOL_WINDOWS = (2, 4, 8, 16)
PG = 256
DIN = 5 * D
DFF = 2816
FFB = DFF // NDEV
FFC = DFF // 2
INB = DIN // NDEV
RMS_EPS = 1e-6
LN_EPS = 1e-5
ADAM_LR, ADAM_B1, ADAM_B2, ADAM_EPS, ADAM_WD, ADAM_STEP = 0.001, 0.9, 0.999, 1e-08, 0.01, 10

TM = 384
TMS = 192
TM_IO = 704
TM_WG = 1408
TM_WM = 704
RB, CW = 64, 128
MIB = 2 ** 20


def _sig(x):
    return 1.0 / (1.0 + jnp.exp(-x))


def _dot(a, b):
    return jnp.dot(a, b, preferred_element_type=F32)


def _dot_nt(a, b):
    return lax.dot_general(a, b, (((1,), (1,)), ((), ())), preferred_element_type=F32)


def _dot_tn(a, b):
    return lax.dot_general(a, b, (((0,), (0,)), ((), ())), preferred_element_type=F32)


def _pick(tp, pref):
    return pref if tp % pref == 0 else TM


def _params(sem, vmem_mib):
    return pltpu.CompilerParams(dimension_semantics=sem, vmem_limit_bytes=vmem_mib * MIB)


def _load_once(first, pairs, sems):
    @pl.when(first)
    def _():
        cps = [pltpu.make_async_copy(s, d, sems.at[k]) for k, (s, d) in enumerate(pairs)]
        for cp in cps:
            cp.start()
        for cp in cps:
            cp.wait()


def _place():
    x, y, c = lax.axis_index("x"), lax.axis_index("y"), lax.axis_index("c")
    return x, y, c


class _Gather:
    def __init__(self, groups, dtypes):
        self.groups, self.dtypes, self.n = groups, dtypes, len(groups)
        self.arrays = [a for _, parts in groups for a, _, _ in parts]
        self.out_shape = [jax.ShapeDtypeStruct((NDEV, *s), dt) for (s, _), dt in zip(groups, dtypes)]
        self.scratch = [pltpu.VMEM(s, dt) for (s, _), dt in zip(groups, dtypes)] + [
            pltpu.SemaphoreType.DMA((7 * self.n,)), pltpu.SemaphoreType.DMA((7 * self.n,)),
            pltpu.SemaphoreType.DMA((self.n,))]

    def bind(self, ins, outs, scratch):
        self.ins, self.outs, self.stages = ins, outs, scratch[:self.n]
        self.send_sems, self.recv_sems, self.local_sems = scratch[self.n:]
        return self

    def _copy(self, w, k, block, to, src=None):
        dst = self.outs[w].at[4 * block[0] + 2 * block[1] + block[2]]
        return pltpu.make_async_remote_copy(
            src_ref=dst if src is None else src, dst_ref=dst,
            send_sem=self.send_sems.at[7 * w + k], recv_sem=self.recv_sems.at[7 * w + k],
            device_id=to, device_id_type=MESH_ID)

    def _first(self):
        x, y, c = _place()
        me, sibling = (x, y, c), (x, y, 1 - c)
        chips = [(1 - x, y), (x, 1 - y), (1 - x, 1 - y)]
        mine, first = [], []
        for w in range(self.n):
            mine.append(pltpu.make_async_copy(self.stages[w], self.outs[w].at[4 * x + 2 * y + c], self.local_sems.at[w]))
            first.append(self._copy(w, 0, me, sibling, src=self.stages[w]))
            first += [self._copy(w, 1 + j, me, (*chip, c), src=self.stages[w]) for j, chip in enumerate(chips)]
        return mine, first

    def _passed(self):
        x, y, c = _place()
        chips = [(1 - x, y), (x, 1 - y), (1 - x, 1 - y)]
        return [self._copy(w, 4 + j, (*chip, c), (x, y, 1 - c)) for w in range(self.n) for j, chip in enumerate(chips)]

    def issue(self):
        a = 0
        for w in range(self.n):
            shape, parts = self.groups[w]
            if sum(arr.size for arr, _, _ in parts) < math.prod(shape):
                self.stages[w][...] = jnp.zeros(shape, self.dtypes[w])
            for _, dst, src in parts:
                self.stages[w][dst] = self.ins[a][src].astype(self.dtypes[w])
                a += 1
        mine, first = self._first()
        for cp in mine + first:
            cp.start()

    def forward(self):
        x, y, c = _place()
        chips = [(1 - x, y), (x, 1 - y), (1 - x, 1 - y)]
        passed = self._passed()
        for w in range(self.n):
            for j, chip in enumerate(chips):
                self._copy(w, 1 + j, (*chip, c), (x, y, c)).wait_recv()
                passed[3 * w + j].start()

    def finish(self):
        x, y, c = _place()
        chips = [(1 - x, y), (x, 1 - y), (1 - x, 1 - y)]
        for w in range(self.n):
            self._copy(w, 0, (x, y, 1 - c), (x, y, c)).wait_recv()
            for j, chip in enumerate(chips):
                self._copy(w, 4 + j, (*chip, 1 - c), (x, y, c)).wait_recv()
        mine, first = self._first()
        for cp in first + self._passed():
            cp.wait_send()
        for cp in mine:
            cp.wait()


def _all_gather(groups, dtypes):
    ag = _Gather(groups, dtypes)
    na, n = len(ag.arrays), ag.n

    def body(*refs):
        ag.bind(refs[:na], refs[na:na + n], refs[na + n:])
        ag.issue()
        ag.forward()
        ag.finish()

    return pl.pallas_call(
        body, name="ag_weights", out_shape=ag.out_shape,
        in_specs=[pl.BlockSpec(memory_space=pltpu.VMEM)] * na,
        out_specs=[pl.BlockSpec(memory_space=pl.ANY)] * n,
        scratch_shapes=ag.scratch,
        compiler_params=pltpu.CompilerParams(vmem_limit_bytes=40 * MIB),
    )(*ag.arrays)


class _ChipExchange:
    def __init__(self, qs):
        self.n = len(qs)
        self.out_shape = [jax.ShapeDtypeStruct(q.shape, q.dtype) for q in qs]
        self.scratch = [pltpu.SemaphoreType.DMA((3 * self.n,)), pltpu.SemaphoreType.DMA((3 * self.n,))]

    def bind(self, qs, rels, scratch):
        self.qs, self.rels = qs, rels
        self.send_sems, self.recv_sems = scratch
        return self

    def _copies(self):
        x, y, c = _place()
        chips = [(1 - x, y), (x, 1 - y), (1 - x, 1 - y)]
        return [pltpu.make_async_remote_copy(
            src_ref=self.qs[w].at[j], dst_ref=self.rels[w].at[j],
            send_sem=self.send_sems.at[3 * w + j], recv_sem=self.recv_sems.at[3 * w + j],
            device_id=(*chips[j], c), device_id_type=MESH_ID) for w in range(self.n) for j in range(3)]

    def issue(self):
        for cp in self._copies():
            cp.start()

    def finish(self):
        cps = self._copies()
        for cp in cps:
            cp.wait_recv()
        for cp in cps:
            cp.wait_send()


def _reduce_scatter(parts, small):
    n = len(parts)
    blks = [p.shape[1:] for p in parts]

    def body(*refs):
        ps, small_ref = refs[:n], refs[n]
        o = n + 1
        owns, sibs, rels, small_out = refs[o:o + n], refs[o + n:o + 2 * n], refs[o + 2 * n:o + 3 * n], refs[o + 3 * n]
        o += 3 * n + 1
        pa, pb, qst = refs[o:o + n], refs[o + n:o + 2 * n], refs[o + 2 * n:o + 3 * n]
        s1_send, s1_recv, s2_send, s2_recv, sm_send, sm_recv, lsem = refs[o + 3 * n:]
        x, y, c = _place()
        me = 4 * x + 2 * y + c
        sibling = (x, y, 1 - c)
        chips = [(1 - x, y), (x, 1 - y), (1 - x, 1 - y)]
        all_chips = [(x, y)] + chips

        own_cps = []
        for w in range(n):
            cp = pltpu.make_async_copy(ps[w].at[me], owns[w], lsem.at[w])
            cp.start()
            own_cps.append(cp)
        sm_own = pltpu.make_async_copy(small_ref, small_out.at[me], lsem.at[n])
        sm_own.start()

        def small_copy(r):
            peer = ((x + (r >> 2)) % 2, (y + ((r >> 1) & 1)) % 2, (c + (r & 1)) % 2)
            return pltpu.make_async_remote_copy(
                src_ref=small_ref, dst_ref=small_out.at[me], send_sem=sm_send.at[r - 1], recv_sem=sm_recv.at[r - 1],
                device_id=peer, device_id_type=MESH_ID)

        sm_cps = [small_copy(r) for r in range(1, NDEV)]
        for cp in sm_cps:
            cp.start()

        def pair_copy(w, rel):
            cx, cy = all_chips[rel]
            return pltpu.make_async_remote_copy(
                src_ref=ps[w].at[4 * cx + 2 * cy + (1 - c)], dst_ref=sibs[w].at[rel],
                send_sem=s1_send.at[4 * w + rel], recv_sem=s1_recv.at[4 * w + rel],
                device_id=sibling, device_id_type=MESH_ID)

        def chip_copy(w, j):
            return pltpu.make_async_remote_copy(
                src_ref=qst[w].at[j], dst_ref=rels[w].at[j],
                send_sem=s2_send.at[3 * w + j], recv_sem=s2_recv.at[3 * w + j],
                device_id=(*chips[j], c), device_id_type=MESH_ID)

        pair_cps = [pair_copy(w, rel) for w in range(n) for rel in (1, 2, 3, 0)]
        for cp in pair_cps:
            cp.start()
        chip_cps = []
        for w in range(n):
            for j, (cx, cy) in enumerate(chips):
                pair_copy(w, 1 + j).wait_recv()
                la = pltpu.make_async_copy(ps[w].at[4 * cx + 2 * cy + c], pa[w], lsem.at[n + 1])
                lb = pltpu.make_async_copy(sibs[w].at[1 + j], pb[w], lsem.at[n + 2])
                la.start()
                lb.start()
                la.wait()
                lb.wait()
                qst[w][j] = (pa[w][...].astype(F32) + pb[w][...].astype(F32)).astype(BF16)
                cp = chip_copy(w, j)
                cp.start()
                chip_cps.append(cp)
        for w in range(n):
            pair_copy(w, 0).wait_recv()
            for j in range(3):
                chip_copy(w, j).wait_recv()
        for cp in sm_cps:
            cp.wait_recv()
        for cp in pair_cps + chip_cps + sm_cps:
            cp.wait_send()
        for cp in own_cps:
            cp.wait()
        sm_own.wait()

    any_spec = pl.BlockSpec(memory_space=pl.ANY)
    outs = pl.pallas_call(
        body, name="rs_grads",
        out_shape=[jax.ShapeDtypeStruct(b, BF16) for b in blks]
        + [jax.ShapeDtypeStruct((4, *b), BF16) for b in blks]
        + [jax.ShapeDtypeStruct((3, *b), BF16) for b in blks]
        + [jax.ShapeDtypeStruct((NDEV, *small.shape), F32)],
        in_specs=[any_spec] * (n + 1),
        out_specs=[any_spec] * (3 * n + 1),
        scratch_shapes=[pltpu.VMEM(b, BF16) for b in blks] + [pltpu.VMEM(b, BF16) for b in blks]
        + [pltpu.VMEM((3, *b), BF16) for b in blks]
        + [pltpu.SemaphoreType.DMA((4 * n,)), pltpu.SemaphoreType.DMA((4 * n,)),
           pltpu.SemaphoreType.DMA((3 * n,)), pltpu.SemaphoreType.DMA((3 * n,)),
           pltpu.SemaphoreType.DMA((NDEV - 1,)), pltpu.SemaphoreType.DMA((NDEV - 1,)),
           pltpu.SemaphoreType.DMA((n + 3,))],
        compiler_params=pltpu.CompilerParams(vmem_limit_bytes=40 * MIB),
    )(*parts, small)
    return outs[:n], outs[n:2 * n], outs[2 * n:3 * n], outs[3 * n]


def _rs_pair(name, parts):
    n = len(parts)
    blks = [p.shape[1:] for p in parts]

    def body(*refs):
        ps = refs[:n]
        owns, sibs, qs = refs[n:2 * n], refs[2 * n:3 * n], refs[3 * n:4 * n]
        pa, pb, qst = refs[4 * n:5 * n], refs[5 * n:6 * n], refs[6 * n:7 * n]
        s_send, s_recv, lsem = refs[7 * n:]
        x, y, c = _place()
        chips = [(1 - x, y), (x, 1 - y), (1 - x, 1 - y)]
        all_chips = [(x, y)] + chips

        own_cps = [pltpu.make_async_copy(ps[w].at[4 * x + 2 * y + c], owns[w], lsem.at[w]) for w in range(n)]
        for cp in own_cps:
            cp.start()

        def pair_copy(w, rel):
            cx, cy = all_chips[rel]
            return pltpu.make_async_remote_copy(
                src_ref=ps[w].at[4 * cx + 2 * cy + (1 - c)], dst_ref=sibs[w].at[rel],
                send_sem=s_send.at[4 * w + rel], recv_sem=s_recv.at[4 * w + rel],
                device_id=(x, y, 1 - c), device_id_type=MESH_ID)

        pair_cps = [pair_copy(w, rel) for w in range(n) for rel in (1, 2, 3, 0)]
        for cp in pair_cps:
            cp.start()
        q_cps = []
        for w in range(n):
            for j, (cx, cy) in enumerate(chips):
                la = pltpu.make_async_copy(ps[w].at[4 * cx + 2 * cy + c], pa[w], lsem.at[n])
                lb = pltpu.make_async_copy(sibs[w].at[1 + j], pb[w], lsem.at[n + 1])
                la.start()
                pair_copy(w, 1 + j).wait_recv()
                lb.start()
                la.wait()
                lb.wait()
                qst[w][j] = (pa[w][...].astype(F32) + pb[w][...].astype(F32)).astype(BF16)
            cp = pltpu.make_async_copy(qst[w], qs[w], lsem.at[n + 2 + w])
            cp.start()
            q_cps.append(cp)
        for w in range(n):
            pair_copy(w, 0).wait_recv()
        for cp in pair_cps:
            cp.wait_send()
        for cp in own_cps + q_cps:
            cp.wait()

    any_spec = pl.BlockSpec(memory_space=pl.ANY)
    outs = pl.pallas_call(
        body, name=name,
        out_shape=[jax.ShapeDtypeStruct(b, BF16) for b in blks]
        + [jax.ShapeDtypeStruct((4, *b), BF16) for b in blks]
        + [jax.ShapeDtypeStruct((3, *b), BF16) for b in blks],
        in_specs=[any_spec] * n,
        out_specs=[any_spec] * (3 * n),
        scratch_shapes=[pltpu.VMEM(b, BF16) for b in blks] + [pltpu.VMEM(b, BF16) for b in blks]
        + [pltpu.VMEM((3, *b), BF16) for b in blks]
        + [pltpu.SemaphoreType.DMA((4 * n,)), pltpu.SemaphoreType.DMA((4 * n,)), pltpu.SemaphoreType.DMA((2 * n + 2,))],
        compiler_params=pltpu.CompilerParams(vmem_limit_bytes=40 * MIB),
    )(*parts)
    return outs[:n], outs[n:2 * n], outs[2 * n:3 * n]


def _adamw_math(g, w, m, v):
    m = ADAM_B1 * m + (1.0 - ADAM_B1) * g
    v = ADAM_B2 * v + (1.0 - ADAM_B2) * (g * g)
    m_hat = m / (1.0 - ADAM_B1 ** ADAM_STEP)
    v_hat = v / (1.0 - ADAM_B2 ** ADAM_STEP)
    delta = -ADAM_LR * (m_hat / (jnp.sqrt(v_hat) + ADAM_EPS) + ADAM_WD * w)
    return delta, m, v


def _adamw_multi(name, own, sib, rel, ws, ms, vs, row_grid):
    k_n, r_n, c_n = own.shape
    rbk = r_n // row_grid

    def body(*refs):
        own_ref, sib_ref, r0_ref, r1_ref, r2_ref = refs[:5]
        w_refs, m_refs, v_refs = refs[5:5 + k_n], refs[5 + k_n:5 + 2 * k_n], refs[5 + 2 * k_n:5 + 3 * k_n]
        outs = refs[5 + 3 * k_n:]
        for k in range(k_n):
            g = own_ref[k].astype(F32) + sib_ref[k].astype(F32)
            g = g + r0_ref[k].astype(F32)
            g = g + r1_ref[k].astype(F32)
            g = g + r2_ref[k].astype(F32)
            delta, mm, vv = _adamw_math(g, w_refs[k][0], m_refs[k][0], v_refs[k][0])
            outs[4 * k][0] = g
            outs[4 * k + 1][0] = delta
            outs[4 * k + 2][0] = mm
            outs[4 * k + 3][0] = vv

    def lead(j):
        return pl.BlockSpec((None, k_n, rbk, c_n), lambda g: (j, 0, g, 0))

    wspec = pl.BlockSpec((1, rbk, c_n), lambda g: (0, g, 0))
    shp = jax.ShapeDtypeStruct((1, r_n, c_n), F32)
    res = pl.pallas_call(
        body, name=name, grid=(row_grid,),
        in_specs=[pl.BlockSpec((k_n, rbk, c_n), lambda g: (0, g, 0)), lead(0), lead(0), lead(1), lead(2)] + [wspec] * (3 * k_n),
        out_specs=[wspec] * (4 * k_n), out_shape=[shp] * (4 * k_n),
        compiler_params=_params(("arbitrary",), 40),
    )(own, sib, rel, rel, rel, *ws, *ms, *vs)
    return [tuple(res[4 * k:4 * k + 4]) for k in range(k_n)]


def _adamw_meta_dw(own, sib, rel, meta, dw):
    def body(own_ref, sib_ref, rel_ref, wm, mm, vm, wd, md, vd, *outs):
        def gsum(rows):
            g = own_ref[rows, :].astype(F32) + sib_ref[0, rows, :].astype(F32)
            for j in range(3):
                g = g + rel_ref[j, rows, :].astype(F32)
            return g

        g = gsum(pl.ds(0, N_META))
        delta, m2, v2 = _adamw_math(g, wm[...], mm[...], vm[...])
        for o, val in zip(outs[:4], (g, delta, m2, v2)):
            o[...] = val
        g = gsum(pl.ds(N_META, CONV_K))
        delta, m2, v2 = _adamw_math(g, wd[0], md[0], vd[0])
        for o, val in zip(outs[4:], (g, delta, m2, v2)):
            o[0] = val

    s_meta = jax.ShapeDtypeStruct(meta[0].shape, F32)
    s_dw = jax.ShapeDtypeStruct(dw[0].shape, F32)
    res = pl.pallas_call(body, name="adamw_meta_dw", out_shape=[s_meta] * 4 + [s_dw] * 4)(own, sib, rel, *meta, *dw)
    return tuple(res[:4]), tuple(res[4:])


REP_ROWS = 16


def _adamw_rep(gathered, ws, ms, vs):
    rows = [(0, 1), (1, 2), (3, 1), (4, 1), (5, 1), (6, 1), (7, 1), (8, 1)]

    def body(g_ref, *refs):
        w_refs, m_refs, v_refs = refs[:8], refs[8:16], refs[16:24]
        loss_ref, outs, acc = refs[24], refs[25:57], refs[57]
        g = g_ref[0]
        for d in range(1, NDEV):
            g = g + g_ref[d]
        acc[...] = g
        loss_ref[...] = (0.5 / D) * jnp.sum(acc[pl.ds(9, 1), :], axis=1, keepdims=True)
        for p, (r0, nr) in enumerate(rows):
            for h in range(nr):
                cols = pl.ds(h * D, D)
                gp = acc[pl.ds(r0 + h, 1), :]
                delta, mm, vv = _adamw_math(gp, w_refs[p][:, cols], m_refs[p][:, cols], v_refs[p][:, cols])
                for o, val in zip(outs[4 * p:4 * p + 4], (gp, delta, mm, vv)):
                    o[:, cols] = val

    shapes = [jax.ShapeDtypeStruct(w.shape, F32) for w in ws]
    res = pl.pallas_call(
        body, name="adamw_rep",
        out_shape=[jax.ShapeDtypeStruct((1, 1), F32)] + [s for s in shapes for _ in range(4)],
        scratch_shapes=[pltpu.VMEM((REP_ROWS, D), F32)],
    )(gathered, *ws, *ms, *vs)
    return res[0], [tuple(res[1 + 4 * p:5 + 4 * p]) for p in range(8)]


def _gu_pairs(w_hbm, w_vm):
    half = NDEV // 2
    return [(w_hbm.at[d, i], w_vm.at[i, d // half, pl.ds(FFB * (d % half), FFB), :])
            for i in range(2) for d in range(NDEV)]


def _win_pairs(w_hbm, w_vm):
    half = NDEV // 2
    return [(w_hbm.at[d], w_vm.at[d // half, :, pl.ds(INB * (d % half), INB)]) for d in range(NDEV)]


def _whole(a):
    nd = a.ndim
    return pl.BlockSpec(a.shape, lambda *g: (0,) * nd)


def _fwd_in(x2, tail, g_mix, w_g, tp, ag):
    tm = _pick(tp, TM_IO)
    nt = tp // tm
    nx_last = tm - tail.shape[0]
    na, ng = len(ag.arrays), ag.n
    half = NDEV // 2

    def body(*refs):
        x_ref, tail_ref, g_ref, w_hbm = refs[:4]
        h_ref, z_ref, u_ref = refs[4 + na:7 + na]
        w_vm, sems = refs[7 + na + ng:9 + na + ng]
        ag.bind(refs[4:4 + na], refs[7 + na:7 + na + ng], refs[9 + na + ng:])
        i, j = pl.program_id(0), pl.program_id(1)
        first = (i == 0) & (j == 0)

        @pl.when(first)
        def _():
            ag.issue()

        @pl.when((i == max(nt - 2, 0)) & (j == 0))
        def _():
            ag.forward()

        _load_once(first, _win_pairs(w_hbm, w_vm), sems)

        @pl.when((j == 0) & (i < nt - 1))
        def _():
            h_ref[...] = x_ref[...]

        @pl.when((j == 0) & (i == nt - 1))
        def _():
            h_ref[pl.ds(0, nx_last), :] = x_ref[pl.ds(0, nx_last), :]
            h_ref[pl.ds(nx_last, tm - nx_last), :] = tail_ref[...]

        @pl.when(j == 0)
        def _():
            xv = h_ref[...]
            r = lax.rsqrt(jnp.mean(xv * xv, axis=-1, keepdims=True) + RMS_EPS)
            u_ref[...] = (xv * r * g_ref[...]).astype(BF16)

        z_ref[...] = _dot(u_ref[...], w_vm[j])

        @pl.when((i == nt - 1) & (j == 1))
        def _():
            ag.finish()

    tile = pl.BlockSpec((tm, D), lambda i, j: (i, 0))
    res = pl.pallas_call(
        body, name="fwd_in", grid=(nt, 2),
        in_specs=[tile, pl.BlockSpec(tail.shape, lambda i, j: (0, 0)), pl.BlockSpec((1, D), lambda i, j: (0, 0)),
                  pl.BlockSpec(memory_space=pl.ANY)] + [_whole(a) for a in ag.arrays],
        out_specs=[tile, pl.BlockSpec((tm, DIN // 2), lambda i, j: (i, j)), tile] + [pl.BlockSpec(memory_space=pl.ANY)] * ng,
        out_shape=[jax.ShapeDtypeStruct((tp, D), F32), jax.ShapeDtypeStruct((tp, DIN), F32),
                   jax.ShapeDtypeStruct((tp, D), BF16)] + ag.out_shape,
        scratch_shapes=[pltpu.VMEM((2, D, DIN // 2), BF16), pltpu.SemaphoreType.DMA((NDEV,))] + ag.scratch,
        compiler_params=_params(("arbitrary", "arbitrary"), 56),
    )(x2, tail, g_mix, w_g, *ag.arrays)
    return res[:3], res[3:]


def _halo_specs(col, nt, width=D):
    r = TM // HALO
    nb = nt * r
    return [pl.BlockSpec((HALO, width), lambda i: ((i * r + nb - 1) % nb, col)),
            pl.BlockSpec((TM, width), lambda i: (i, col)),
            pl.BlockSpec((HALO, width), lambda i: (((i + 1) * r) % nb, col))]


NCB = D // 128
TME = TM + 2 * HALO


def _tm_fill(dst, time0, groups, tile_fn):
    def body(g, c):
        for j in range(NCB):
            dst[pl.ds((time0 + 8 * g) * NCB + j, 8, stride=NCB), :] = tile_fn(pl.multiple_of(8 * g, 8), pl.ds(128 * j, 128))
        return c

    lax.fori_loop(0, groups, body, 0)


def _tm_fill_ext(dst, left, cur, right, fn):
    _tm_fill(dst, 0, HALO // 8, lambda r, l: fn(left, pl.ds(r, 8), l))
    _tm_fill(dst, HALO, TM // 8, lambda r, l: fn(cur, pl.ds(r, 8), l))
    _tm_fill(dst, HALO + TM, HALO // 8, lambda r, l: fn(right, pl.ds(r, 8), l))


def _tm_read(src, groups, store_fn):
    def body(g, c):
        for j in range(NCB):
            store_fn(pl.ds(pl.multiple_of(8 * g, 8), 8), pl.ds(128 * j, 128), src[pl.ds(8 * g * NCB + j, 8, stride=NCB), :])
        return c

    lax.fori_loop(0, groups, body, 0)


def _tm_rows(t):
    return pl.ds(t * NCB if isinstance(t, int) else pl.multiple_of(t * NCB, NCB), NCB)


def _tm_at(ref, t):
    return ref[_tm_rows(t), :]


def _by_group(sub, vals):
    return jnp.where(sub < 2, vals[0], jnp.where(sub < 4, vals[1], jnp.where(sub < 6, vals[2], vals[3])))


def _pool_cnt(b, seq, tp, sub):
    b = jnp.where(b < 0, b + tp, b)
    b = jnp.where(b >= tp, b - tp, b)
    t = jnp.where(b < seq, b + N_META, b - (tp - N_META))
    cnts = []
    for win in POOL_WINDOWS:
        left = win // 2
        lo = jnp.maximum(t - left, 0)
        hi = jnp.minimum(t + win - left, seq + N_META)
        cnts.append(jnp.maximum(hi - lo, 1).astype(F32))
    return _by_group(sub, cnts)


def _nested_windows(at, lo_offs):
    sums, s, have = [], None, set()
    for g, win in enumerate(POOL_WINDOWS):
        for o in range(lo_offs[g], lo_offs[g] + win):
            if o not in have:
                have.add(o)
                s = at(o) if s is None else s + at(o)
        sums.append(s)
    return sums


def _seq_fwd(z, w_dw, b_dw, seq, gat):
    tp = z.shape[0]
    nt = tp // TM
    na, ng = len(gat.arrays), gat.n

    def body(*refs):
        av_l, av, av_r, ag_l, ag, ag_r, p_l, p, p_r, w_ref, b_ref = refs[:11]
        ac_ref, m_ref = refs[11 + na:13 + na]
        a3, p3, o3, m3, w3, b3, m2d = refs[13 + na + ng:20 + na + ng]
        gat.bind(refs[11:11 + na], refs[13 + na:13 + na + ng], refs[20 + na + ng:])
        i = pl.program_id(0)
        sub = lax.broadcasted_iota(jnp.int32, (NCB, 128), 0)

        @pl.when(i == 0)
        def _():
            gat.issue()
            _tm_fill(w3, 0, 4, lambda r, l: w_ref[pl.ds(r, 8), l])
            for j in range(NCB):
                b3[pl.ds(j, 1), :] = b_ref[:, pl.ds(128 * j, 128)]

        @pl.when(i == max(nt - 2, 0))
        def _():
            gat.forward()

        _tm_fill_ext(a3, (av_l, ag_l), (av, ag), (av_r, ag_r), lambda vg, r, l: vg[0][r, l] * _sig(vg[1][r, l]))
        _tm_fill_ext(p3, p_l, p, p_r, lambda ref, r, l: ref[r, l])

        def conv(g, c):
            for t in range(8):
                acc = b3[...]
                for k in range(CONV_K):
                    acc = acc + _tm_at(w3, k) * _tm_at(a3, 8 * g + t + k + 1)
                o3[_tm_rows(8 * g + t), :] = acc
            return c

        lax.fori_loop(0, TM // 8, conv, 0)
        _tm_read(o3, TM // 8, lambda r, l, tile: ac_ref.__setitem__((r, l), tile))

        def pool(scale):
            def grp(g, c):
                for t in range(8):
                    e = 8 * g + t + HALO
                    sums = _nested_windows(lambda o: _tm_at(p3, e + o), [-(w // 2) for w in POOL_WINDOWS])
                    m3[_tm_rows(8 * g + t), :] = scale(_by_group(sub, sums), 8 * g + t) - _tm_at(p3, e)
                return c

            lax.fori_loop(0, TM // 8, grp, 0)

        @pl.when(i < nt - 1)
        def _():
            inv = _by_group(sub, [1.0 / w for w in POOL_WINDOWS])
            pool(lambda s, row: s * inv)

        @pl.when(i == nt - 1)
        def _():
            pool(lambda s, row: s / _pool_cnt(i * TM + row, seq, tp, sub))

        _tm_read(m3, TM // 8, lambda r, l, tile: m2d.__setitem__((r, l), tile))
        m_ref[...] = m2d[...].astype(BF16)

        @pl.when(i == nt - 1)
        def _():
            gat.finish()

    tmaj = pltpu.VMEM((TM * NCB, 128), F32)
    text = pltpu.VMEM((TME * NCB, 128), F32)
    res = pl.pallas_call(
        body, name="seq_fwd", grid=(nt,),
        in_specs=_halo_specs(0, nt) + _halo_specs(1, nt) + _halo_specs(2, nt)
        + [pl.BlockSpec((32, D), lambda i: (0, 0)), pl.BlockSpec((1, D), lambda i: (0, 0))] + [_whole(a) for a in gat.arrays],
        out_specs=[pl.BlockSpec((TM, D), lambda i: (i, 0))] * 2 + [pl.BlockSpec(memory_space=pl.ANY)] * ng,
        out_shape=[jax.ShapeDtypeStruct((tp, D), F32), jax.ShapeDtypeStruct((tp, D), BF16)] + gat.out_shape,
        scratch_shapes=[text, text, tmaj, tmaj, pltpu.VMEM((32 * NCB, 128), F32), pltpu.VMEM((NCB, 128), F32),
                        pltpu.VMEM((TM, D), F32)] + gat.scratch,
        compiler_params=_params(("arbitrary",), 52),
    )(z, z, z, z, z, z, z, z, z, w_dw, b_dw, *gat.arrays)
    return res[:2], res[2:]


def _ln_stats(ac):
    mu = jnp.mean(ac, axis=-1, keepdims=True)
    xc = ac - mu
    rl = lax.rsqrt(jnp.mean(xc * xc, axis=-1, keepdims=True) + LN_EPS)
    return xc * rl, rl


def _pool_mix(m, wp_ref):
    return jnp.concatenate(
        [_dot(m[:, g * PG:(g + 1) * PG], wp_ref[:, g].reshape(PG, PG)) for g in range(4)], axis=1)


def _mix_fwd(ac, m, z, h0, b_gate, ln_g, ln_b, pool_scale, g_mixw, g_pool):
    tp = h0.shape[0]
    nt = tp // TMS

    def body(ac_ref, m_ref, zga, zgb, h_ref, bg_ref, lg_ref, lb_ref, ps_ref, wm_hbm, wp_hbm,
             h1_ref, s_ref, yc_ref, yp_ref, mg_ref, q_ref, wm, wp, sems):
        _load_once(pl.program_id(0) == 0, [(wm_hbm, wm), (wp_hbm, wp)], sems)
        n, _ = _ln_stats(ac_ref[...])
        l = n * lg_ref[...] + lb_ref[...]
        s = (l * _sig(l)).astype(BF16)
        s_ref[...] = s
        yc = _dot(s, wm[:, 0].reshape(D, D))
        q = (_pool_mix(m_ref[...], wp) * ps_ref[...]).astype(BF16)
        q_ref[...] = q
        yp = _dot(q, wm[:, 1].reshape(D, D))
        ga = _sig(zga[...] + bg_ref[:, :D])
        gb = _sig(zgb[...] + bg_ref[:, D:])
        merged = (ga * yc + gb * yp).astype(BF16)
        yc_ref[...] = yc
        yp_ref[...] = yp
        mg_ref[...] = merged
        h1_ref[...] = h_ref[...] + _dot(merged, wm[:, 2].reshape(D, D))

    def tile(col=0):
        return pl.BlockSpec((TMS, D), lambda i: (i, col))

    def vec(w):
        return pl.BlockSpec((1, w), lambda i: (0, 0))

    anys = pl.BlockSpec(memory_space=pl.ANY)
    f32o, b16o = jax.ShapeDtypeStruct((tp, D), F32), jax.ShapeDtypeStruct((tp, D), BF16)
    return pl.pallas_call(
        body, name="mix_fwd", grid=(nt,),
        in_specs=[tile(), tile(), tile(3), tile(4), tile(), vec(2 * D), vec(D), vec(D), vec(D), anys, anys],
        out_specs=[tile()] * 6,
        out_shape=[f32o, b16o, f32o, f32o, b16o, b16o],
        scratch_shapes=[pltpu.VMEM((NDEV, 3, D // NDEV, D), BF16), pltpu.VMEM((NDEV, 4, PG // NDEV, PG), BF16),
                        pltpu.SemaphoreType.DMA((2,))],
        compiler_params=_params(("arbitrary",), 48),
    )(ac, m, z, z, h0, b_gate, ln_g, ln_b, pool_scale, g_mixw, g_pool)


def _ffn_fwd(h1, tgt, g_ffn, g_final, w_gu, w_dn):
    tp = h1.shape[0]
    nt = tp // TM
    nx_last = tgt.shape[0] - (nt - 1) * TM

    def body(h_ref, t_ref, gf_ref, gl_ref, wgu_hbm, wdn_hbm,
             fg_ref, fu_ref, v_ref, f_ref, dh2_ref, acc_ref, wgu, wdn, v_sc, h2_sc, diff_sc, sems):
        i, j = pl.program_id(0), pl.program_id(1)
        _load_once((i == 0) & (j == 0), _gu_pairs(wgu_hbm, wgu) + [(wdn_hbm, wdn)], sems)

        @pl.when((i == 0) & (j == 0))
        def _():
            acc_ref[...] = jnp.zeros_like(acc_ref)

        @pl.when(j == 0)
        def _():
            h = h_ref[...]
            r = lax.rsqrt(jnp.mean(h * h, axis=-1, keepdims=True) + RMS_EPS)
            v = (h * r * gf_ref[...]).astype(BF16)
            v_sc[...] = v
            v_ref[...] = v
            h2_sc[...] = h

        v = v_sc[...]
        fg = _dot_nt(v, wgu[0, j])
        fu = _dot_nt(v, wgu[1, j])
        fg_ref[...] = fg
        fu_ref[...] = fu
        f = ((fg * _sig(fg)) * fu).astype(BF16)
        f_ref[...] = f
        h2_sc[...] += _dot(f, wdn[j])

        @pl.when(j == 1)
        def _():
            h2 = h2_sc[...]
            r = lax.rsqrt(jnp.mean(h2 * h2, axis=-1, keepdims=True) + RMS_EPS)
            n2 = h2 * r
            y = n2 * gl_ref[...]

            @pl.when(i < nt - 1)
            def _():
                diff_sc[...] = y - t_ref[...]

            @pl.when(i == nt - 1)
            def _():
                diff_sc[pl.ds(0, nx_last), :] = y[:nx_last] - t_ref[pl.ds(0, nx_last), :]
                diff_sc[pl.ds(nx_last, TM - nx_last), :] = jnp.zeros((TM - nx_last, D), F32)

            diff = diff_sc[...]
            dy = diff * (1.0 / D)
            acc_ref[0:1, :] += jnp.sum(diff * diff, axis=0, keepdims=True)
            acc_ref[1:2, :] += jnp.sum(dy * n2, axis=0, keepdims=True)
            dn = dy * gl_ref[...]
            dh2_ref[...] = r * (dn - n2 * jnp.mean(dn * n2, axis=-1, keepdims=True))

    def tile():
        return pl.BlockSpec((TM, D), lambda i, j: (i, 0))

    def chunk():
        return pl.BlockSpec((TM, FFC), lambda i, j: (i, j))

    def vec():
        return pl.BlockSpec((1, D), lambda i, j: (0, 0))

    anys = pl.BlockSpec(memory_space=pl.ANY)
    hid32, hid16 = jax.ShapeDtypeStruct((tp, DFF), F32), jax.ShapeDtypeStruct((tp, DFF), BF16)
    return pl.pallas_call(
        body, name="ffn_fwd", grid=(nt, 2),
        in_specs=[tile(), tile(), vec(), vec(), anys, anys],
        out_specs=[chunk(), chunk(), tile(), chunk(), tile(), pl.BlockSpec((8, D), lambda i, j: (0, 0))],
        out_shape=[hid32, hid32, jax.ShapeDtypeStruct((tp, D), BF16), hid16, jax.ShapeDtypeStruct((tp, D), F32),
                   jax.ShapeDtypeStruct((8, D), F32)],
        scratch_shapes=[pltpu.VMEM((2, 2, FFC, D), BF16), pltpu.VMEM((2, FFC, D), BF16),
                        pltpu.VMEM((TM, D), BF16), pltpu.VMEM((TM, D), F32), pltpu.VMEM((TM, D), F32),
                        pltpu.SemaphoreType.DMA((2 * NDEV + 1,))],
        compiler_params=_params(("arbitrary", "arbitrary"), 56),
    )(h1, tgt, g_ffn, g_final, w_gu, w_dn)


def _ffn_bwd(dh2, fg, fu, h1, g_ffn, w_gu, w_dn):
    tp = h1.shape[0]
    nt = tp // TM

    def body(dh2_ref, fg_ref, fu_ref, h_ref, gf_ref, wgu_hbm, wdn_hbm,
             dfg_ref, dfu_ref, dh1_ref, acc_ref, wgu, wdn, d_sc, dv_sc, sems):
        i, j = pl.program_id(0), pl.program_id(1)
        _load_once((i == 0) & (j == 0), _gu_pairs(wgu_hbm, wgu) + [(wdn_hbm, wdn)], sems)

        @pl.when((i == 0) & (j == 0))
        def _():
            acc_ref[...] = jnp.zeros_like(acc_ref)

        @pl.when(j == 0)
        def _():
            d_sc[...] = dh2_ref[...].astype(BF16)
            dv_sc[...] = jnp.zeros_like(dv_sc)

        df = _dot_nt(d_sc[...], wdn[j])
        fg = fg_ref[...]
        sg = _sig(fg)
        dfu = (df * (fg * sg)).astype(BF16)
        dfg = (df * fu_ref[...] * (sg * (1.0 + fg * (1.0 - sg)))).astype(BF16)
        dfg_ref[...] = dfg
        dfu_ref[...] = dfu
        dv_sc[...] += _dot(dfg, wgu[0, j]) + _dot(dfu, wgu[1, j])

        @pl.when(j == 1)
        def _():
            h = h_ref[...]
            r = lax.rsqrt(jnp.mean(h * h, axis=-1, keepdims=True) + RMS_EPS)
            n1 = h * r
            dv = dv_sc[...]
            acc_ref[0:1, :] += jnp.sum(dv * n1, axis=0, keepdims=True)
            dn = dv * gf_ref[...]
            dh1_ref[...] = dh2_ref[...] + r * (dn - n1 * jnp.mean(dn * n1, axis=-1, keepdims=True))

    def tile():
        return pl.BlockSpec((TM, D), lambda i, j: (i, 0))

    def chunk():
        return pl.BlockSpec((TM, FFC), lambda i, j: (i, j))

    anys = pl.BlockSpec(memory_space=pl.ANY)
    hid16 = jax.ShapeDtypeStruct((tp, DFF), BF16)
    return pl.pallas_call(
        body, name="ffn_bwd", grid=(nt, 2),
        in_specs=[tile(), chunk(), chunk(), tile(), pl.BlockSpec((1, D), lambda i, j: (0, 0)), anys, anys],
        out_specs=[chunk(), chunk(), tile(), pl.BlockSpec((8, D), lambda i, j: (0, 0))],
        out_shape=[hid16, hid16, jax.ShapeDtypeStruct((tp, D), F32), jax.ShapeDtypeStruct((8, D), F32)],
        scratch_shapes=[pltpu.VMEM((2, 2, FFC, D), BF16), pltpu.VMEM((2, FFC, D), BF16),
                        pltpu.VMEM((TM, D), BF16), pltpu.VMEM((TM, D), F32), pltpu.SemaphoreType.DMA((2 * NDEV + 1,))],
        compiler_params=_params(("arbitrary", "arbitrary"), 56),
    )(dh2, fg, fu, h1, g_ffn, w_gu, w_dn)


def _mix_bwd(dh1, z, yc, yp, ac, m, b_gate, ln_g, ln_b, pool_scale, g_mixw, g_pool, qs):
    tp = dh1.shape[0]
    nt = tp // TMS
    ex = _ChipExchange(qs)
    nq = ex.n

    def body(*refs):
        dh1_ref, zga, zgb, yc_ref, yp_ref, ac_ref, m_ref, bg_ref, lg_ref, lb_ref, ps_ref, wm_hbm, wp_hbm = refs[:13]
        dac_ref, dm_ref, dzg_ref, dyc_ref, dyp_ref, dm2_ref, acc_ref = refs[13 + nq:20 + nq]
        wm, wp, sems = refs[20 + 2 * nq:23 + 2 * nq]
        ex.bind(refs[13:13 + nq], refs[20 + nq:20 + 2 * nq], refs[23 + 2 * nq:])
        first = pl.program_id(0) == 0

        @pl.when(first)
        def _():
            ex.issue()
            acc_ref[...] = jnp.zeros_like(acc_ref)

        _load_once(first, [(wm_hbm, wm), (wp_hbm, wp)], sems)

        dmerged = _dot_nt(dh1_ref[...].astype(BF16), wm[:, 2].reshape(D, D))
        ga = _sig(zga[...] + bg_ref[:, :D])
        gb = _sig(zgb[...] + bg_ref[:, D:])
        dyc = dmerged * ga
        dyp = dmerged * gb
        dza = (dmerged * yc_ref[...]) * (ga * (1.0 - ga))
        dzb = (dmerged * yp_ref[...]) * (gb * (1.0 - gb))
        dzg_ref[:, :D] = dza.astype(BF16)
        dzg_ref[:, D:] = dzb.astype(BF16)
        acc_ref[0:1, :D] += jnp.sum(dza, axis=0, keepdims=True)
        acc_ref[0:1, D:] += jnp.sum(dzb, axis=0, keepdims=True)
        dyc_b = dyc.astype(BF16)
        dyp_b = dyp.astype(BF16)
        dyc_ref[...] = dyc_b
        dyp_ref[...] = dyp_b
        ds = _dot_nt(dyc_b, wm[:, 0].reshape(D, D))
        n, rl = _ln_stats(ac_ref[...])
        l = n * lg_ref[...] + lb_ref[...]
        sg = _sig(l)
        dl = ds * (sg * (1.0 + l * (1.0 - sg)))
        acc_ref[1:2, :D] += jnp.sum(dl * n, axis=0, keepdims=True)
        acc_ref[1:2, D:] += jnp.sum(dl, axis=0, keepdims=True)
        dn = dl * lg_ref[...]
        dac_ref[...] = rl * (dn - jnp.mean(dn, axis=-1, keepdims=True) - n * jnp.mean(dn * n, axis=-1, keepdims=True))
        dq = _dot_nt(dyp_b, wm[:, 1].reshape(D, D))
        mv = m_ref[...]
        acc_ref[2:3, :D] += jnp.sum(dq * _pool_mix(mv, wp), axis=0, keepdims=True)
        dm2 = (dq * ps_ref[...]).astype(BF16)
        dm2_ref[...] = dm2
        dm_ref[...] = jnp.concatenate(
            [_dot_nt(dm2[:, g * PG:(g + 1) * PG], wp[:, g].reshape(PG, PG)) for g in range(4)], axis=1)

        @pl.when(pl.program_id(0) == nt - 1)
        def _():
            ex.finish()

    def tile(col=0):
        return pl.BlockSpec((TMS, D), lambda i: (i, col))

    def vec(w):
        return pl.BlockSpec((1, w), lambda i: (0, 0))

    anys = pl.BlockSpec(memory_space=pl.ANY)
    f32o, b16o = jax.ShapeDtypeStruct((tp, D), F32), jax.ShapeDtypeStruct((tp, D), BF16)
    res = pl.pallas_call(
        body, name="mix_bwd", grid=(nt,),
        in_specs=[tile(), tile(3), tile(4), tile(), tile(), tile(), tile(), vec(2 * D), vec(D), vec(D), vec(D), anys, anys]
        + [anys] * nq,
        out_specs=[tile(), tile(), pl.BlockSpec((TMS, 2 * D), lambda i: (i, 0)), tile(), tile(), tile(),
                   pl.BlockSpec((8, 2 * D), lambda i: (0, 0))] + [anys] * nq,
        out_shape=[f32o, f32o, jax.ShapeDtypeStruct((tp, 2 * D), BF16), b16o, b16o, b16o,
                   jax.ShapeDtypeStruct((8, 2 * D), F32)] + ex.out_shape,
        scratch_shapes=[pltpu.VMEM((NDEV, 3, D // NDEV, D), BF16), pltpu.VMEM((NDEV, 4, PG // NDEV, PG), BF16),
                        pltpu.SemaphoreType.DMA((2,))] + ex.scratch,
        compiler_params=_params(("arbitrary",), 48),
    )(dh1, z, z, yc, yp, ac, m, b_gate, ln_g, ln_b, pool_scale, g_mixw, g_pool, *qs)
    return res[:7], res[7:]


def _seq_bwd(dac, dm, dzg, z, w_dw, seq, qs):
    tp = z.shape[0]
    nt = tp // TM
    ex = _ChipExchange(qs)
    nq = ex.n

    def body(*refs):
        dac_l, dac_c, dac_r, dm_l, dm_c, dm_r, av_l, av, av_r, ag_l, ag, ag_r, dzg_ref, w_ref = refs[:14]
        dz_ref, acc_ref = refs[14 + nq:16 + nq]
        a3, d3, m3, da3, dp3, w3, dw3, da_sc, dp_sc = refs[16 + 2 * nq:25 + 2 * nq]
        ex.bind(refs[14:14 + nq], refs[16 + nq:16 + 2 * nq], refs[25 + 2 * nq:])
        i = pl.program_id(0)
        sub = lax.broadcasted_iota(jnp.int32, (NCB, 128), 0)

        @pl.when(i == 0)
        def _():
            ex.issue()
            dw3[...] = jnp.zeros_like(dw3)
            _tm_fill(w3, 0, 4, lambda r, l: w_ref[pl.ds(r, 8), l])

        _tm_fill_ext(a3, (av_l, ag_l), (av, ag), (av_r, ag_r), lambda vg, r, l: vg[0][r, l] * _sig(vg[1][r, l]))
        _tm_fill_ext(d3, dac_l, dac_c, dac_r, lambda ref, r, l: ref[r, l])
        _tm_fill_ext(m3, dm_l, dm_c, dm_r, lambda ref, r, l: ref[r, l])

        def conv(g, c):
            dcur = [_tm_at(d3, 8 * g + t + HALO) for t in range(8)]
            accs = [None] * 8
            for k in range(CONV_K):
                wk = _tm_at(w3, k)
                s = None
                for t in range(8):
                    term = wk * _tm_at(d3, 8 * g + t + CONV_K - k)
                    accs[t] = term if accs[t] is None else accs[t] + term
                    pr = dcur[t] * _tm_at(a3, 8 * g + t + k + 1)
                    s = pr if s is None else s + pr
                dw3[_tm_rows(k), :] += s
            s = dcur[0]
            for t in range(1, 8):
                s = s + dcur[t]
            dw3[_tm_rows(CONV_K), :] += s
            for t in range(8):
                da3[_tm_rows(8 * g + t), :] = accs[t]
            return c

        lax.fori_loop(0, TM // 8, conv, 0)

        def pool(scale):
            def grp(g, c):
                for t in range(8):
                    e = 8 * g + t + HALO
                    sums = _nested_windows(lambda o: _tm_at(m3, e + o), [w // 2 + 1 - w for w in POOL_WINDOWS])
                    dp3[_tm_rows(8 * g + t), :] = scale(_by_group(sub, sums))
                return c

            lax.fori_loop(0, TM // 8, grp, 0)

        inner = (i > 0) & (i < nt - 1)

        @pl.when(inner)
        def _():
            inv = _by_group(sub, [1.0 / w for w in POOL_WINDOWS])
            pool(lambda s: s * inv)

        @pl.when(jnp.logical_not(inner))
        def _():
            def rescale(e, c):
                m3[_tm_rows(e), :] = _tm_at(m3, e) / _pool_cnt(i * TM + e - HALO, seq, tp, sub)
                return c

            lax.fori_loop(0, TME, rescale, 0)
            pool(lambda s: s)

        _tm_read(da3, TM // 8, lambda r, l, tile: da_sc.__setitem__((r, l), tile))
        _tm_read(dp3, TM // 8, lambda r, l, tile: dp_sc.__setitem__((r, l), tile))
        sg = _sig(ag[...])
        da = da_sc[...]
        dz_ref[:, 0:D] = (da * sg).astype(BF16)
        dz_ref[:, D:2 * D] = (da * av[...] * (sg * (1.0 - sg))).astype(BF16)
        dz_ref[:, 2 * D:3 * D] = (dp_sc[...] - dm_c[...]).astype(BF16)
        dz_ref[:, 3 * D:] = dzg_ref[...]

        @pl.when(i == nt - 1)
        def _():
            _tm_read(dw3, 4, lambda r, l, tile: acc_ref.__setitem__((r, l), tile))
            ex.finish()

    tmaj = pltpu.VMEM((TM * NCB, 128), F32)
    text = pltpu.VMEM((TME * NCB, 128), F32)
    taps = pltpu.VMEM((32 * NCB, 128), F32)
    anys = pl.BlockSpec(memory_space=pl.ANY)
    res = pl.pallas_call(
        body, name="seq_bwd", grid=(nt,),
        in_specs=_halo_specs(0, nt) + _halo_specs(0, nt) + _halo_specs(0, nt) + _halo_specs(1, nt)
        + [pl.BlockSpec((TM, 2 * D), lambda i: (i, 0)), pl.BlockSpec((32, D), lambda i: (0, 0))] + [anys] * nq,
        out_specs=[pl.BlockSpec((TM, DIN), lambda i: (i, 0)), pl.BlockSpec((32, D), lambda i: (0, 0))] + [anys] * nq,
        out_shape=[jax.ShapeDtypeStruct((tp, DIN), BF16), jax.ShapeDtypeStruct((32, D), F32)] + ex.out_shape,
        scratch_shapes=[text, text, text, tmaj, tmaj, taps, taps, pltpu.VMEM((TM, D), F32), pltpu.VMEM((TM, D), F32)]
        + ex.scratch,
        compiler_params=_params(("arbitrary",), 48),
    )(dac, dac, dac, dm, dm, dm, z, z, z, z, z, z, dzg, w_dw, *qs)
    return res[:2], res[2:]


def _in_bwd(dz, h0, dh1, g_mix, w_g, seq, qs):
    tp = h0.shape[0]
    tm = _pick(tp, TM_IO)
    nt = tp // tm
    ex = _ChipExchange(qs)
    nq = ex.n

    def body(*refs):
        dz_ref, h_ref, dh1_ref, g_ref, w_hbm = refs[:5]
        gx_ref, gmeta_ref, acc_ref = refs[5 + nq:8 + nq]
        w_vm, sems = refs[8 + 2 * nq:10 + 2 * nq]
        ex.bind(refs[5:5 + nq], refs[8 + nq:8 + 2 * nq], refs[10 + 2 * nq:])
        i = pl.program_id(0)

        @pl.when(i == 0)
        def _():
            ex.issue()
            acc_ref[...] = jnp.zeros_like(acc_ref)

        _load_once(i == 0, _win_pairs(w_hbm, w_vm), sems)

        du = _dot_nt(dz_ref[:, :DIN // 2], w_vm[0]) + _dot_nt(dz_ref[:, DIN // 2:], w_vm[1])
        h = h_ref[...]
        r = lax.rsqrt(jnp.mean(h * h, axis=-1, keepdims=True) + RMS_EPS)
        n0 = h * r
        acc_ref[0:1, :] += jnp.sum(du * n0, axis=0, keepdims=True)
        dn = du * g_ref[...]
        gx_ref[...] = dh1_ref[...] + r * (dn - n0 * jnp.mean(dn * n0, axis=-1, keepdims=True))

        @pl.when(i == nt - 1)
        def _():
            gmeta_ref[...] = gx_ref[pl.ds(tm - N_META, N_META), :]
            ex.finish()

    tile = pl.BlockSpec((tm, D), lambda i: (i, 0))
    anys = pl.BlockSpec(memory_space=pl.ANY)
    res = pl.pallas_call(
        body, name="in_bwd", grid=(nt,),
        in_specs=[pl.BlockSpec((tm, DIN), lambda i: (i, 0)), tile, tile, pl.BlockSpec((1, D), lambda i: (0, 0)), anys]
        + [anys] * nq,
        out_specs=[tile, pl.BlockSpec((N_META, D), lambda i: (0, 0)), pl.BlockSpec((8, D), lambda i: (0, 0))] + [anys] * nq,
        out_shape=[jax.ShapeDtypeStruct((seq, D), F32), jax.ShapeDtypeStruct((N_META, D), F32),
                   jax.ShapeDtypeStruct((8, D), F32)] + ex.out_shape,
        scratch_shapes=[pltpu.VMEM((2, D, DIN // 2), BF16), pltpu.SemaphoreType.DMA((NDEV,))] + ex.scratch,
        compiler_params=_params(("arbitrary",), 58),
    )(dz, h0, dh1, g_mix, w_g, *qs)
    return res[:3], res[3:]


def _wgrad_in(u, dz):
    tp = u.shape[0]
    tm = _pick(tp, TM_WG)
    nt = tp // tm
    half = DIN // 2

    def body(u_ref, dz_ref, o_ref, acc):
        t = pl.program_id(1)

        @pl.when(t == 0)
        def _():
            acc[...] = jnp.zeros_like(acc)

        acc[...] += _dot_tn(u_ref[...], dz_ref[...])

        @pl.when(t == nt - 1)
        def _():
            for d in range(4):
                o_ref[d] = acc[:, INB * d:INB * (d + 1)].astype(BF16)

    return pl.pallas_call(
        body, name="wgrad_in", grid=(2, nt),
        in_specs=[pl.BlockSpec((tm, D), lambda h, t: (t, 0)), pl.BlockSpec((tm, half), lambda h, t: (t, h))],
        out_specs=pl.BlockSpec((4, D, INB), lambda h, t: (h, 0, 0), pipeline_mode=pl.Buffered(1)),
        out_shape=jax.ShapeDtypeStruct((NDEV, D, INB), BF16),
        scratch_shapes=[pltpu.VMEM((D, half), F32)],
        compiler_params=_params(("arbitrary", "arbitrary"), 52),
    )(u, dz)


def _wgrad_mix(s, dyc, q, dyp, merged, dh1, m, dm2):
    tp = s.shape[0]
    tm = _pick(tp, TM_WM)
    nt = tp // tm
    rb = D // NDEV

    def body(s_ref, dyc_ref, q_ref, dyp_ref, mg_ref, dh1_ref, m_ref, dm2_ref, o_ref, op_ref, acc, accp):
        t = pl.program_id(0)

        @pl.when(t == 0)
        def _():
            acc[...] = jnp.zeros_like(acc)
            accp[...] = jnp.zeros_like(accp)

        acc[0] += _dot_tn(s_ref[...], dyc_ref[...])
        acc[1] += _dot_tn(q_ref[...], dyp_ref[...])
        acc[2] += _dot_tn(mg_ref[...], dh1_ref[...].astype(BF16))
        for g in range(4):
            accp[g] += _dot_tn(m_ref[:, g * PG:(g + 1) * PG], dm2_ref[:, g * PG:(g + 1) * PG])

        @pl.when(t == nt - 1)
        def _():
            for d in range(NDEV):
                for k in range(3):
                    o_ref[d, k] = acc[k, rb * d:rb * (d + 1), :].astype(BF16)
                for g in range(4):
                    op_ref[d, g] = accp[g, 32 * d:32 * (d + 1), :].astype(BF16)

    tile = pl.BlockSpec((tm, D), lambda t: (t, 0))
    return pl.pallas_call(
        body, name="wgrad_mix", grid=(nt,),
        in_specs=[tile] * 8,
        out_specs=[pl.BlockSpec((NDEV, 3, rb, D), lambda t: (0, 0, 0, 0), pipeline_mode=pl.Buffered(1)),
                   pl.BlockSpec((NDEV, 4, 32, PG), lambda t: (0, 0, 0, 0), pipeline_mode=pl.Buffered(1))],
        out_shape=[jax.ShapeDtypeStruct((NDEV, 3, rb, D), BF16), jax.ShapeDtypeStruct((NDEV, 4, 32, PG), BF16)],
        scratch_shapes=[pltpu.VMEM((3, D, D), F32), pltpu.VMEM((4, PG, PG), F32)],
        compiler_params=_params(("arbitrary",), 56),
    )(s, dyc, q, dyp, merged, dh1, m, dm2)


def _wgrad_gu(v, dfg, dfu):
    tp = v.shape[0]
    tm = _pick(tp, TM_WG)
    nt = tp // tm

    def body(v_ref, dg_ref, du_ref, o_ref, acc):
        k, t = pl.program_id(0), pl.program_id(2)

        @pl.when(t == 0)
        def _():
            acc[...] = jnp.zeros_like(acc)

        @pl.when(k == 0)
        def _():
            acc[...] += _dot_tn(dg_ref[...], v_ref[...])

        @pl.when(k == 1)
        def _():
            acc[...] += _dot_tn(du_ref[...], v_ref[...])

        @pl.when(t == nt - 1)
        def _():
            for d in range(4):
                o_ref[d] = acc[FFB * d:FFB * (d + 1), :].astype(BF16)

    return pl.pallas_call(
        body, name="wgrad_gu", grid=(2, 2, nt),
        in_specs=[pl.BlockSpec((tm, D), lambda k, h, t: (t, 0)),
                  pl.BlockSpec((tm, FFC), lambda k, h, t: (t * (1 - k), h * (1 - k))),
                  pl.BlockSpec((tm, FFC), lambda k, h, t: (t * k, h * k))],
        out_specs=pl.BlockSpec((4, None, FFB, D), lambda k, h, t: (h, k, 0, 0), pipeline_mode=pl.Buffered(1)),
        out_shape=jax.ShapeDtypeStruct((NDEV, 2, FFB, D), BF16),
        scratch_shapes=[pltpu.VMEM((FFC, D), F32)],
        compiler_params=_params(("arbitrary",) * 3, 48),
    )(v, dfg, dfu)


def _wgrad_down(f, dh2):
    tp = f.shape[0]
    tm = _pick(tp, TM_WG)
    nt = tp // tm

    def body(f_ref, d_ref, o_ref, acc):
        t = pl.program_id(1)

        @pl.when(t == 0)
        def _():
            acc[...] = jnp.zeros_like(acc)

        acc[...] += _dot_tn(f_ref[...], d_ref[...].astype(BF16))

        @pl.when(t == nt - 1)
        def _():
            for d in range(4):
                o_ref[d] = acc[FFB * d:FFB * (d + 1), :].astype(BF16)

    return pl.pallas_call(
        body, name="wgrad_down", grid=(2, nt),
        in_specs=[pl.BlockSpec((tm, FFC), lambda h, t: (t, h)), pl.BlockSpec((tm, D), lambda h, t: (t, 0))],
        out_specs=pl.BlockSpec((4, FFB, D), lambda h, t: (h, 0, 0), pipeline_mode=pl.Buffered(1)),
        out_shape=jax.ShapeDtypeStruct((NDEV, FFB, D), BF16),
        scratch_shapes=[pltpu.VMEM((FFC, D), F32)],
        compiler_params=_params(("arbitrary", "arbitrary"), 48),
    )(f, dh2)


def kernel(x, meta_tokens, g_mix, w_in, b_gate, w_dw, b_dw, ln_g, ln_b, w_conv_out, w_pool, pool_scale, w_pool_out, w_o, g_ffn, w_ffn_gate, w_ffn_up, w_ffn_down, g_final, loss_target, m_meta_tokens, m_g_mix, m_w_in, m_b_gate, m_w_dw, m_b_dw, m_ln_g, m_ln_b, m_w_conv_out, m_w_pool, m_pool_scale, m_w_pool_out, m_w_o, m_g_ffn, m_w_ffn_gate, m_w_ffn_up, m_w_ffn_down, m_g_final, v_meta_tokens, v_g_mix, v_w_in, v_b_gate, v_w_dw, v_b_dw, v_ln_g, v_ln_b, v_w_conv_out, v_w_pool, v_pool_scale, v_w_pool_out, v_w_o, v_g_ffn, v_w_ffn_gate, v_w_ffn_up, v_w_ffn_down, v_g_final):
    seq = x.shape[1]
    tp = -(-(seq + 2 * HALO) // TM) * TM
    tm_in = _pick(tp, TM_IO)
    nx_last = seq - (tp // tm_in - 1) * tm_in
    assert 0 < nx_last <= tm_in - 2 * HALO and nx_last % 8 == 0 and 0 < seq - (tp // TM - 1) * TM

    whole = (Ellipsis,)
    g_in, g_small = _all_gather(
        [((D, INB), [(w_in, whole, 0)]),
         ((48, D // NDEV), [(meta_tokens, pl.ds(0, N_META), whole), (w_dw, pl.ds(N_META, CONV_K), 0)])], [BF16, F32])
    ag_mix = _Gather([((3, D // NDEV, D), [(w_conv_out, 0, 0), (w_pool_out, 1, 0), (w_o, 2, 0)]),
                      ((4, PG // NDEV, PG), [(w_pool, whole, 0)])], [BF16, BF16])
    def tr(a):
        return jnp.swapaxes(a, 1, 2)

    ag_ffn = _Gather([((2, FFB, D), [(tr(w_ffn_gate), 0, 0), (tr(w_ffn_up), 1, 0)]),
                      ((FFB, D), [(w_ffn_down, whole, 0)])], [BF16, BF16])
    small_full = g_small.transpose(1, 0, 2).reshape(48, D)
    wdw_full = small_full[N_META:]
    tail = jnp.concatenate([jnp.zeros((tm_in - nx_last - N_META, D), F32), small_full[:N_META]], axis=0)

    (h0, z, u), (g_mixw, g_pool) = _fwd_in(x[0], tail, g_mix, g_in, tp, ag_mix)
    (ac, m), (w_gu, g_down) = _seq_fwd(z, wdw_full, b_dw, seq, ag_ffn)
    w_dn = g_down.reshape(2, FFC, D)
    h1, s, yc, yp, merged, q = _mix_fwd(ac, m, z, h0, b_gate, ln_g, ln_b, pool_scale, g_mixw, g_pool)
    fg, fu, v, f, dh2, head_acc = _ffn_fwd(h1, loss_target[0], g_ffn, g_final.reshape(1, D), w_gu, w_dn)

    dfg, dfu, dh1, ffn_acc = _ffn_bwd(dh2, fg, fu, h1, g_ffn, w_gu, w_dn)
    own_f, sib_f, q_f = _rs_pair("rs_pair_ffn", [_wgrad_gu(v, dfg, dfu), _wgrad_down(f, dh2)])
    (dac, dm, dzg, dyc, dyp, dm2, mix_acc), rel_f = _mix_bwd(
        dh1, z, yc, yp, ac, m, b_gate, ln_g, ln_b, pool_scale, g_mixw, g_pool, q_f)
    own_m, sib_m, q_m = _rs_pair("rs_pair_mix", list(_wgrad_mix(s, dyc, q, dyp, merged, dh1, m, dm2)))
    (dz, seq_acc), rel_m = _seq_bwd(dac, dm, dzg, z, wdw_full, seq, q_m)
    own_i, sib_i, q_i = _rs_pair("rs_pair_in", [_wgrad_in(u, dz)])
    (grad_x, g_meta, in_acc), rel_i = _in_bwd(dz, h0, dh1, g_mix, g_in, seq, q_i)
    small_g = jnp.concatenate([g_meta, seq_acc[:CONV_K], jnp.zeros((1, D), F32)], axis=0)
    p_small = small_g.reshape(48, NDEV, D // NDEV).transpose(1, 0, 2).astype(BF16)
    rep_g = jnp.concatenate([
        in_acc[0:1], mix_acc[0:1, :D], mix_acc[0:1, D:], seq_acc[CONV_K:CONV_K + 1], mix_acc[1:2, :D], mix_acc[1:2, D:],
        mix_acc[2:3, :D], ffn_acc[0:1], head_acc[1:2], head_acc[0:1], jnp.zeros((REP_ROWS - 10, D), F32)], axis=0)
    own_s, sib_s, rel_s, rep_all = _reduce_scatter([p_small], rep_g)
    owns = [own_i[0], own_s[0], own_m[0], own_m[1], own_f[0], own_f[1]]
    sibs = [sib_i[0], sib_s[0], sib_m[0], sib_m[1], sib_f[0], sib_f[1]]
    rels = [rel_i[0], rel_s[0], rel_m[0], rel_m[1], rel_f[0], rel_f[1]]

    def lead(a):
        return a.reshape(1, *a.shape)

    def stack4(a, lead_dims):
        return a.reshape(*lead_dims, 1, 4 * 32, PG)

    (r_in,) = _adamw_multi("adamw_in", lead(owns[0]), sibs[0][:, None], rels[0][:, None], [w_in], [m_w_in], [v_w_in], 4)
    r_meta, r_dw = _adamw_meta_dw(owns[1], sibs[1], rels[1], (meta_tokens, m_meta_tokens, v_meta_tokens),
                                  (w_dw, m_w_dw, v_w_dw))
    r_conv, r_pout, r_o = _adamw_multi("adamw_mix", owns[2], sibs[2], rels[2], [w_conv_out, w_pool_out, w_o],
                                       [m_w_conv_out, m_w_pool_out, m_w_o], [v_w_conv_out, v_w_pool_out, v_w_o], 1)
    (r_pool,) = _adamw_multi("adamw_pool", stack4(owns[3], ()), stack4(sibs[3], (4,)), stack4(rels[3], (3,)),
                             [w_pool.reshape(1, 128, PG)], [m_w_pool.reshape(1, 128, PG)], [v_w_pool.reshape(1, 128, PG)], 1)
    r_pool = tuple(a.reshape(w_pool.shape) for a in r_pool)
    r_gate, r_up = _adamw_multi("adamw_gu", owns[4], sibs[4], rels[4], [tr(w_ffn_gate), tr(w_ffn_up)],
                                [tr(m_w_ffn_gate), tr(m_w_ffn_up)], [tr(v_w_ffn_gate), tr(v_w_ffn_up)], 2)
    r_gate, r_up = tuple(tr(a) for a in r_gate), tuple(tr(a) for a in r_up)
    (r_down,) = _adamw_multi("adamw_down", lead(owns[5]), sibs[5][:, None], rels[5][:, None],
                             [w_ffn_down], [m_w_ffn_down], [v_w_ffn_down], 2)
    row = (1, D)
    loss, reps = _adamw_rep(
        rep_all,
        [g_mix, b_gate, b_dw, ln_g, ln_b, pool_scale, g_ffn, g_final.reshape(row)],
        [m_g_mix, m_b_gate, m_b_dw, m_ln_g, m_ln_b, m_pool_scale, m_g_ffn, m_g_final.reshape(row)],
        [v_g_mix, v_b_gate, v_b_dw, v_ln_g, v_ln_b, v_pool_scale, v_g_ffn, v_g_final.reshape(row)])
    r_gmix, r_bg, r_bdw, r_lg, r_lb, r_ps, r_gffn, r_gfin = reps
    r_gfin = tuple(a.reshape(D) for a in r_gfin)

    in_order = [r_meta, r_gmix, r_in, r_bg, r_dw, r_bdw, r_lg, r_lb, r_conv, r_pool, r_ps, r_pout, r_o, r_gffn,
                r_gate, r_up, r_down, r_gfin]
    return (loss.reshape(()), grad_x[None], *[r[0] for r in in_order], *[r[1] for r in in_order],
            *[r[2] for r in in_order], *[r[3] for r in in_order])
```

```python
import math

import jax
import jax.numpy as jnp
from jax import lax
from jax.experimental import pallas as pl
from jax.experimental.pallas import tpu as pltpu

F32, BF16 = jnp.float32, jnp.bfloat16
MESH_ID = pl.DeviceIdType.MESH
NDEV = 8

D = 1024
N_META = 16
CONV_K = 31
HALO = 16
POOL_WINDOWS = (2, 4, 8, 16)
PG = 256
DIN = 5 * D
DFF = 2816
FFB = DFF // NDEV
FFC = DFF // 2
INB = DIN // NDEV
RMS_EPS = 1e-6
LN_EPS = 1e-5
ADAM_LR, ADAM_B1, ADAM_B2, ADAM_EPS, ADAM_WD, ADAM_STEP = 0.001, 0.9, 0.999, 1e-08, 0.01, 10

TM = 384
TMS = 192
TM_IO = 704
TM_WG = 1408
TM_WM = 704
MIB = 2 ** 20


def _sig(x):
    return 1.0 / (1.0 + jnp.exp(-x))


def _dot(a, b):
    return jnp.dot(a, b, preferred_element_type=F32)


def _dot_nt(a, b):
    return lax.dot_general(a, b, (((1,), (1,)), ((), ())), preferred_element_type=F32)


def _dot_tn(a, b):
    return lax.dot_general(a, b, (((0,), (0,)), ((), ())), preferred_element_type=F32)


def _pick(tp, pref):
    return pref if tp % pref == 0 else TM


def _params(sem, vmem_mib):
    return pltpu.CompilerParams(dimension_semantics=sem, vmem_limit_bytes=vmem_mib * MIB)


def _load_once(first, pairs, sems):
    @pl.when(first)
    def _():
        cps = [pltpu.make_async_copy(s, d, sems.at[k]) for k, (s, d) in enumerate(pairs)]
        for cp in cps:
            cp.start()
        for cp in cps:
            cp.wait()


def _place():
    x, y, c = lax.axis_index("x"), lax.axis_index("y"), lax.axis_index("c")
    return x, y, c


class _Gather:
    def __init__(self, groups, dtypes):
        self.groups, self.dtypes, self.n = groups, dtypes, len(groups)
        self.arrays = [a for _, parts in groups for a, _, _ in parts]
        self.out_shape = [jax.ShapeDtypeStruct((NDEV, *s), dt) for (s, _), dt in zip(groups, dtypes)]
        self.scratch = [pltpu.VMEM(s, dt) for (s, _), dt in zip(groups, dtypes)] + [
            pltpu.SemaphoreType.DMA((7 * self.n,)), pltpu.SemaphoreType.DMA((7 * self.n,)),
            pltpu.SemaphoreType.DMA((self.n,))]

    def bind(self, ins, outs, scratch):
        self.ins, self.outs, self.stages = ins, outs, scratch[:self.n]
        self.send_sems, self.recv_sems, self.local_sems = scratch[self.n:]
        return self

    def _copy(self, w, k, block, to, src=None):
        dst = self.outs[w].at[4 * block[0] + 2 * block[1] + block[2]]
        return pltpu.make_async_remote_copy(
            src_ref=dst if src is None else src, dst_ref=dst,
            send_sem=self.send_sems.at[7 * w + k], recv_sem=self.recv_sems.at[7 * w + k],
            device_id=to, device_id_type=MESH_ID)

    def _first(self):
        x, y, c = _place()
        me, sibling = (x, y, c), (x, y, 1 - c)
        chips = [(1 - x, y), (x, 1 - y), (1 - x, 1 - y)]
        mine, first = [], []
        for w in range(self.n):
            mine.append(pltpu.make_async_copy(self.stages[w], self.outs[w].at[4 * x + 2 * y + c], self.local_sems.at[w]))
            first.append(self._copy(w, 0, me, sibling, src=self.stages[w]))
            first += [self._copy(w, 1 + j, me, (*chip, c), src=self.stages[w]) for j, chip in enumerate(chips)]
        return mine, first

    def _passed(self):
        x, y, c = _place()
        chips = [(1 - x, y), (x, 1 - y), (1 - x, 1 - y)]
        return [self._copy(w, 4 + j, (*chip, c), (x, y, 1 - c)) for w in range(self.n) for j, chip in enumerate(chips)]

    def issue(self):
        a = 0
        for w in range(self.n):
            shape, parts = self.groups[w]
            if sum(arr.size for arr, _, _ in parts) < math.prod(shape):
                self.stages[w][...] = jnp.zeros(shape, self.dtypes[w])
            for _, dst, src in parts:
                self.stages[w][dst] = self.ins[a][src].astype(self.dtypes[w])
                a += 1
        mine, first = self._first()
        for cp in mine + first:
            cp.start()

    def forward(self):
        x, y, c = _place()
        chips = [(1 - x, y), (x, 1 - y), (1 - x, 1 - y)]
        passed = self._passed()
        for w in range(self.n):
            for j, chip in enumerate(chips):
                self._copy(w, 1 + j, (*chip, c), (x, y, c)).wait_recv()
                passed[3 * w + j].start()

    def finish(self):
        x, y, c = _place()
        chips = [(1 - x, y), (x, 1 - y), (1 - x, 1 - y)]
        for w in range(self.n):
            self._copy(w, 0, (x, y, 1 - c), (x, y, c)).wait_recv()
            for j, chip in enumerate(chips):
                self._copy(w, 4 + j, (*chip, 1 - c), (x, y, c)).wait_recv()
        mine, first = self._first()
        for cp in first + self._passed():
            cp.wait_send()
        for cp in mine:
            cp.wait()


def _all_gather(groups, dtypes):
    ag = _Gather(groups, dtypes)
    na, n = len(ag.arrays), ag.n

    def body(*refs):
        ag.bind(refs[:na], refs[na:na + n], refs[na + n:])
        ag.issue()
        ag.forward()
        ag.finish()

    return pl.pallas_call(
        body, name="ag_weights", out_shape=ag.out_shape,
        in_specs=[pl.BlockSpec(memory_space=pltpu.VMEM)] * na,
        out_specs=[pl.BlockSpec(memory_space=pl.ANY)] * n,
        scratch_shapes=ag.scratch,
        compiler_params=pltpu.CompilerParams(vmem_limit_bytes=40 * MIB),
    )(*ag.arrays)


class _ChipExchange:
    def __init__(self, qs):
        self.n = len(qs)
        self.out_shape = [jax.ShapeDtypeStruct(q.shape, q.dtype) for q in qs]
        self.scratch = [pltpu.SemaphoreType.DMA((3 * self.n,)), pltpu.SemaphoreType.DMA((3 * self.n,))]

    def bind(self, qs, rels, scratch):
        self.qs, self.rels = qs, rels
        self.send_sems, self.recv_sems = scratch
        return self

    def _copies(self):
        x, y, c = _place()
        chips = [(1 - x, y), (x, 1 - y), (1 - x, 1 - y)]
        return [pltpu.make_async_remote_copy(
            src_ref=self.qs[w].at[j], dst_ref=self.rels[w].at[j],
            send_sem=self.send_sems.at[3 * w + j], recv_sem=self.recv_sems.at[3 * w + j],
            device_id=(*chips[j], c), device_id_type=MESH_ID) for w in range(self.n) for j in range(3)]

    def issue(self):
        for cp in self._copies():
            cp.start()

    def finish(self):
        cps = self._copies()
        for cp in cps:
            cp.wait_recv()
        for cp in cps:
            cp.wait_send()


def _reduce_scatter(parts, small):
    n = len(parts)
    blks = [p.shape[1:] for p in parts]

    def body(*refs):
        ps, small_ref = refs[:n], refs[n]
        o = n + 1
        owns, sibs, rels, small_out = refs[o:o + n], refs[o + n:o + 2 * n], refs[o + 2 * n:o + 3 * n], refs[o + 3 * n]
        o += 3 * n + 1
        pa, pb, qst = refs[o:o + n], refs[o + n:o + 2 * n], refs[o + 2 * n:o + 3 * n]
        s1_send, s1_recv, s2_send, s2_recv, sm_send, sm_recv, lsem = refs[o + 3 * n:]
        x, y, c = _place()
        me = 4 * x + 2 * y + c
        sibling = (x, y, 1 - c)
        chips = [(1 - x, y), (x, 1 - y), (1 - x, 1 - y)]
        all_chips = [(x, y)] + chips

        own_cps = []
        for w in range(n):
            cp = pltpu.make_async_copy(ps[w].at[me], owns[w], lsem.at[w])
            cp.start()
            own_cps.append(cp)
        sm_own = pltpu.make_async_copy(small_ref, small_out.at[me], lsem.at[n])
        sm_own.start()

        def small_copy(r):
            peer = ((x + (r >> 2)) % 2, (y + ((r >> 1) & 1)) % 2, (c + (r & 1)) % 2)
            return pltpu.make_async_remote_copy(
                src_ref=small_ref, dst_ref=small_out.at[me], send_sem=sm_send.at[r - 1], recv_sem=sm_recv.at[r - 1],
                device_id=peer, device_id_type=MESH_ID)

        sm_cps = [small_copy(r) for r in range(1, NDEV)]
        for cp in sm_cps:
            cp.start()

        def pair_copy(w, rel):
            cx, cy = all_chips[rel]
            return pltpu.make_async_remote_copy(
                src_ref=ps[w].at[4 * cx + 2 * cy + (1 - c)], dst_ref=sibs[w].at[rel],
                send_sem=s1_send.at[4 * w + rel], recv_sem=s1_recv.at[4 * w + rel],
                device_id=sibling, device_id_type=MESH_ID)

        def chip_copy(w, j):
            return pltpu.make_async_remote_copy(
                src_ref=qst[w].at[j], dst_ref=rels[w].at[j],
                send_sem=s2_send.at[3 * w + j], recv_sem=s2_recv.at[3 * w + j],
                device_id=(*chips[j], c), device_id_type=MESH_ID)

        pair_cps = [pair_copy(w, rel) for w in range(n) for rel in (1, 2, 3, 0)]
        for cp in pair_cps:
            cp.start()
        chip_cps = []
        for w in range(n):
            for j, (cx, cy) in enumerate(chips):
                pair_copy(w, 1 + j).wait_recv()
                la = pltpu.make_async_copy(ps[w].at[4 * cx + 2 * cy + c], pa[w], lsem.at[n + 1])
                lb = pltpu.make_async_copy(sibs[w].at[1 + j], pb[w], lsem.at[n + 2])
                la.start()
                lb.start()
                la.wait()
                lb.wait()
                qst[w][j] = (pa[w][...].astype(F32) + pb[w][...].astype(F32)).astype(BF16)
                cp = chip_copy(w, j)
                cp.start()
                chip_cps.append(cp)
        for w in range(n):
            pair_copy(w, 0).wait_recv()
            for j in range(3):
                chip_copy(w, j).wait_recv()
        for cp in sm_cps:
            cp.wait_recv()
        for cp in pair_cps + chip_cps + sm_cps:
            cp.wait_send()
        for cp in own_cps:
            cp.wait()
        sm_own.wait()

    any_spec = pl.BlockSpec(memory_space=pl.ANY)
    outs = pl.pallas_call(
        body, name="rs_grads",
        out_shape=[jax.ShapeDtypeStruct(b, BF16) for b in blks]
        + [jax.ShapeDtypeStruct((4, *b), BF16) for b in blks]
        + [jax.ShapeDtypeStruct((3, *b), BF16) for b in blks]
        + [jax.ShapeDtypeStruct((NDEV, *small.shape), F32)],
        in_specs=[any_spec] * (n + 1),
        out_specs=[any_spec] * (3 * n + 1),
        scratch_shapes=[pltpu.VMEM(b, BF16) for b in blks] + [pltpu.VMEM(b, BF16) for b in blks]
        + [pltpu.VMEM((3, *b), BF16) for b in blks]
        + [pltpu.SemaphoreType.DMA((4 * n,)), pltpu.SemaphoreType.DMA((4 * n,)),
           pltpu.SemaphoreType.DMA((3 * n,)), pltpu.SemaphoreType.DMA((3 * n,)),
           pltpu.SemaphoreType.DMA((NDEV - 1,)), pltpu.SemaphoreType.DMA((NDEV - 1,)),
           pltpu.SemaphoreType.DMA((n + 3,))],
        compiler_params=pltpu.CompilerParams(vmem_limit_bytes=40 * MIB),
    )(*parts, small)
    return outs[:n], outs[n:2 * n], outs[2 * n:3 * n], outs[3 * n]


def _rs_pair(name, parts):
    n = len(parts)
    blks = [p.shape[1:] for p in parts]

    def body(*refs):
        ps = refs[:n]
        owns, sibs, qs = refs[n:2 * n], refs[2 * n:3 * n], refs[3 * n:4 * n]
        pa, pb, qst = refs[4 * n:5 * n], refs[5 * n:6 * n], refs[6 * n:7 * n]
        s_send, s_recv, lsem = refs[7 * n:]
        x, y, c = _place()
        chips = [(1 - x, y), (x, 1 - y), (1 - x, 1 - y)]
        all_chips = [(x, y)] + chips

        own_cps = [pltpu.make_async_copy(ps[w].at[4 * x + 2 * y + c], owns[w], lsem.at[w]) for w in range(n)]
        for cp in own_cps:
            cp.start()

        def pair_copy(w, rel):
            cx, cy = all_chips[rel]
            return pltpu.make_async_remote_copy(
                src_ref=ps[w].at[4 * cx + 2 * cy + (1 - c)], dst_ref=sibs[w].at[rel],
                send_sem=s_send.at[4 * w + rel], recv_sem=s_recv.at[4 * w + rel],
                device_id=(x, y, 1 - c), device_id_type=MESH_ID)

        pair_cps = [pair_copy(w, rel) for w in range(n) for rel in (1, 2, 3, 0)]
        for cp in pair_cps:
            cp.start()
        q_cps = []
        for w in range(n):
            for j, (cx, cy) in enumerate(chips):
                la = pltpu.make_async_copy(ps[w].at[4 * cx + 2 * cy + c], pa[w], lsem.at[n])
                lb = pltpu.make_async_copy(sibs[w].at[1 + j], pb[w], lsem.at[n + 1])
                la.start()
                pair_copy(w, 1 + j).wait_recv()
                lb.start()
                la.wait()
                lb.wait()
                qst[w][j] = (pa[w][...].astype(F32) + pb[w][...].astype(F32)).astype(BF16)
            cp = pltpu.make_async_copy(qst[w], qs[w], lsem.at[n + 2 + w])
            cp.start()
            q_cps.append(cp)
        for w in range(n):
            pair_copy(w, 0).wait_recv()
        for cp in pair_cps:
            cp.wait_send()
        for cp in own_cps + q_cps:
            cp.wait()

    any_spec = pl.BlockSpec(memory_space=pl.ANY)
    outs = pl.pallas_call(
        body, name=name,
        out_shape=[jax.ShapeDtypeStruct(b, BF16) for b in blks]
        + [jax.ShapeDtypeStruct((4, *b), BF16) for b in blks]
        + [jax.ShapeDtypeStruct((3, *b), BF16) for b in blks],
        in_specs=[any_spec] * n,
        out_specs=[any_spec] * (3 * n),
        scratch_shapes=[pltpu.VMEM(b, BF16) for b in blks] + [pltpu.VMEM(b, BF16) for b in blks]
        + [pltpu.VMEM((3, *b), BF16) for b in blks]
        + [pltpu.SemaphoreType.DMA((4 * n,)), pltpu.SemaphoreType.DMA((4 * n,)), pltpu.SemaphoreType.DMA((2 * n + 2,))],
        compiler_params=pltpu.CompilerParams(vmem_limit_bytes=40 * MIB),
    )(*parts)
    return outs[:n], outs[n:2 * n], outs[2 * n:3 * n]


def _adamw_math(g, w, m, v):
    m = ADAM_B1 * m + (1.0 - ADAM_B1) * g
    v = ADAM_B2 * v + (1.0 - ADAM_B2) * (g * g)
    m_hat = m / (1.0 - ADAM_B1 ** ADAM_STEP)
    v_hat = v / (1.0 - ADAM_B2 ** ADAM_STEP)
    delta = -ADAM_LR * (m_hat / (jnp.sqrt(v_hat) + ADAM_EPS) + ADAM_WD * w)
    return delta, m, v


def _adamw_multi(name, own, sib, rel, ws, ms, vs, row_grid):
    k_n, r_n, c_n = own.shape
    rbk = r_n // row_grid

    def body(*refs):
        own_ref, sib_ref, r0_ref, r1_ref, r2_ref = refs[:5]
        w_refs, m_refs, v_refs = refs[5:5 + k_n], refs[5 + k_n:5 + 2 * k_n], refs[5 + 2 * k_n:5 + 3 * k_n]
        outs = refs[5 + 3 * k_n:]
        for k in range(k_n):
            g = own_ref[k].astype(F32) + sib_ref[k].astype(F32)
            g = g + r0_ref[k].astype(F32)
            g = g + r1_ref[k].astype(F32)
            g = g + r2_ref[k].astype(F32)
            delta, mm, vv = _adamw_math(g, w_refs[k][0], m_refs[k][0], v_refs[k][0])
            outs[4 * k][0] = g
            outs[4 * k + 1][0] = delta
            outs[4 * k + 2][0] = mm
            outs[4 * k + 3][0] = vv

    def lead(j):
        return pl.BlockSpec((None, k_n, rbk, c_n), lambda g: (j, 0, g, 0))

    wspec = pl.BlockSpec((1, rbk, c_n), lambda g: (0, g, 0))
    shp = jax.ShapeDtypeStruct((1, r_n, c_n), F32)
    res = pl.pallas_call(
        body, name=name, grid=(row_grid,),
        in_specs=[pl.BlockSpec((k_n, rbk, c_n), lambda g: (0, g, 0)), lead(0), lead(0), lead(1), lead(2)] + [wspec] * (3 * k_n),
        out_specs=[wspec] * (4 * k_n), out_shape=[shp] * (4 * k_n),
        compiler_params=_params(("arbitrary",), 40),
    )(own, sib, rel, rel, rel, *ws, *ms, *vs)
    return [tuple(res[4 * k:4 * k + 4]) for k in range(k_n)]


def _adamw_meta_dw(own, sib, rel, meta, dw):
    def body(own_ref, sib_ref, rel_ref, wm, mm, vm, wd, md, vd, *outs):
        def gsum(rows):
            g = own_ref[rows, :].astype(F32) + sib_ref[0, rows, :].astype(F32)
            for j in range(3):
                g = g + rel_ref[j, rows, :].astype(F32)
            return g

        g = gsum(pl.ds(0, N_META))
        delta, m2, v2 = _adamw_math(g, wm[...], mm[...], vm[...])
        for o, val in zip(outs[:4], (g, delta, m2, v2)):
            o[...] = val
        g = gsum(pl.ds(N_META, CONV_K))
        delta, m2, v2 = _adamw_math(g, wd[0], md[0], vd[0])
        for o, val in zip(outs[4:], (g, delta, m2, v2)):
            o[0] = val

    s_meta = jax.ShapeDtypeStruct(meta[0].shape, F32)
    s_dw = jax.ShapeDtypeStruct(dw[0].shape, F32)
    res = pl.pallas_call(body, name="adamw_meta_dw", out_shape=[s_meta] * 4 + [s_dw] * 4)(own, sib, rel, *meta, *dw)
    return tuple(res[:4]), tuple(res[4:])


REP_ROWS = 16


def _adamw_rep(gathered, ws, ms, vs):
    rows = [(0, 1), (1, 2), (3, 1), (4, 1), (5, 1), (6, 1), (7, 1), (8, 1)]

    def body(g_ref, *refs):
        w_refs, m_refs, v_refs = refs[:8], refs[8:16], refs[16:24]
        loss_ref, outs, acc = refs[24], refs[25:57], refs[57]
        g = g_ref[0]
        for d in range(1, NDEV):
            g = g + g_ref[d]
        acc[...] = g
        loss_ref[...] = (0.5 / D) * jnp.sum(acc[pl.ds(9, 1), :], axis=1, keepdims=True)
        for p, (r0, nr) in enumerate(rows):
            for h in range(nr):
                cols = pl.ds(h * D, D)
                gp = acc[pl.ds(r0 + h, 1), :]
                delta, mm, vv = _adamw_math(gp, w_refs[p][:, cols], m_refs[p][:, cols], v_refs[p][:, cols])
                for o, val in zip(outs[4 * p:4 * p + 4], (gp, delta, mm, vv)):
                    o[:, cols] = val

    shapes = [jax.ShapeDtypeStruct(w.shape, F32) for w in ws]
    res = pl.pallas_call(
        body, name="adamw_rep",
        out_shape=[jax.ShapeDtypeStruct((1, 1), F32)] + [s for s in shapes for _ in range(4)],
        scratch_shapes=[pltpu.VMEM((REP_ROWS, D), F32)],
    )(gathered, *ws, *ms, *vs)
    return res[0], [tuple(res[1 + 4 * p:5 + 4 * p]) for p in range(8)]


def _gu_pairs(w_hbm, w_vm):
    half = NDEV // 2
    return [(w_hbm.at[d, i], w_vm.at[i, d // half, pl.ds(FFB * (d % half), FFB), :])
            for i in range(2) for d in range(NDEV)]


def _win_pairs(w_hbm, w_vm):
    half = NDEV // 2
    return [(w_hbm.at[d], w_vm.at[d // half, :, pl.ds(INB * (d % half), INB)]) for d in range(NDEV)]


def _whole(a):
    nd = a.ndim
    return pl.BlockSpec(a.shape, lambda *g: (0,) * nd)


def _fwd_in(x2, tail, g_mix, w_g, tp, ag):
    tm = _pick(tp, TM_IO)
    nt = tp // tm
    nx_last = tm - tail.shape[0]
    na, ng = len(ag.arrays), ag.n

    def body(*refs):
        x_ref, tail_ref, g_ref, w_hbm = refs[:4]
        h_ref, z_ref, u_ref = refs[4 + na:7 + na]
        w_vm, sems = refs[7 + na + ng:9 + na + ng]
        ag.bind(refs[4:4 + na], refs[7 + na:7 + na + ng], refs[9 + na + ng:])
        i, j = pl.program_id(0), pl.program_id(1)
        first = (i == 0) & (j == 0)

        @pl.when(first)
        def _():
            ag.issue()

        @pl.when((i == max(nt - 2, 0)) & (j == 0))
        def _():
            ag.forward()

        _load_once(first, _win_pairs(w_hbm, w_vm), sems)

        @pl.when((j == 0) & (i < nt - 1))
        def _():
            h_ref[...] = x_ref[...]

        @pl.when((j == 0) & (i == nt - 1))
        def _():
            h_ref[pl.ds(0, nx_last), :] = x_ref[pl.ds(0, nx_last), :]
            h_ref[pl.ds(nx_last, tm - nx_last), :] = tail_ref[...]

        @pl.when(j == 0)
        def _():
            xv = h_ref[...]
            r = lax.rsqrt(jnp.mean(xv * xv, axis=-1, keepdims=True) + RMS_EPS)
            u_ref[...] = (xv * r * g_ref[...]).astype(BF16)

        z_ref[...] = _dot(u_ref[...], w_vm[j])

        @pl.when((i == nt - 1) & (j == 1))
        def _():
            ag.finish()

    tile = pl.BlockSpec((tm, D), lambda i, j: (i, 0))
    res = pl.pallas_call(
        body, name="fwd_in", grid=(nt, 2),
        in_specs=[tile, pl.BlockSpec(tail.shape, lambda i, j: (0, 0)), pl.BlockSpec((1, D), lambda i, j: (0, 0)),
                  pl.BlockSpec(memory_space=pl.ANY)] + [_whole(a) for a in ag.arrays],
        out_specs=[tile, pl.BlockSpec((tm, DIN // 2), lambda i, j: (i, j)), tile] + [pl.BlockSpec(memory_space=pl.ANY)] * ng,
        out_shape=[jax.ShapeDtypeStruct((tp, D), F32), jax.ShapeDtypeStruct((tp, DIN), F32),
                   jax.ShapeDtypeStruct((tp, D), BF16)] + ag.out_shape,
        scratch_shapes=[pltpu.VMEM((2, D, DIN // 2), BF16), pltpu.SemaphoreType.DMA((NDEV,))] + ag.scratch,
        compiler_params=_params(("arbitrary", "arbitrary"), 56),
    )(x2, tail, g_mix, w_g, *ag.arrays)
    return res[:3], res[3:]


def _halo_specs(col, nt, width=D):
    r = TM // HALO
    nb = nt * r
    return [pl.BlockSpec((HALO, width), lambda i: ((i * r + nb - 1) % nb, col)),
            pl.BlockSpec((TM, width), lambda i: (i, col)),
            pl.BlockSpec((HALO, width), lambda i: (((i + 1) * r) % nb, col))]


NCB = D // 128
TME = TM + 2 * HALO


def _tm_fill(dst, time0, groups, tile_fn):
    def body(g, c):
        for j in range(NCB):
            dst[pl.ds((time0 + 8 * g) * NCB + j, 8, stride=NCB), :] = tile_fn(pl.multiple_of(8 * g, 8), pl.ds(128 * j, 128))
        return c

    lax.fori_loop(0, groups, body, 0)


def _tm_fill_ext(dst, left, cur, right, fn):
    _tm_fill(dst, 0, HALO // 8, lambda r, l: fn(left, pl.ds(r, 8), l))
    _tm_fill(dst, HALO, TM // 8, lambda r, l: fn(cur, pl.ds(r, 8), l))
    _tm_fill(dst, HALO + TM, HALO // 8, lambda r, l: fn(right, pl.ds(r, 8), l))


def _tm_read(src, groups, store_fn):
    def body(g, c):
        for j in range(NCB):
            store_fn(pl.ds(pl.multiple_of(8 * g, 8), 8), pl.ds(128 * j, 128), src[pl.ds(8 * g * NCB + j, 8, stride=NCB), :])
        return c

    lax.fori_loop(0, groups, body, 0)


def _tm_rows(t):
    return pl.ds(t * NCB if isinstance(t, int) else pl.multiple_of(t * NCB, NCB), NCB)


def _tm_at(ref, t):
    return ref[_tm_rows(t), :]


def _by_group(sub, vals):
    return jnp.where(sub < 2, vals[0], jnp.where(sub < 4, vals[1], jnp.where(sub < 6, vals[2], vals[3])))


def _pool_cnt(b, seq, tp, sub):
    b = jnp.where(b < 0, b + tp, b)
    b = jnp.where(b >= tp, b - tp, b)
    t = jnp.where(b < seq, b + N_META, b - (tp - N_META))
    cnts = []
    for win in POOL_WINDOWS:
        left = win // 2
        lo = jnp.maximum(t - left, 0)
        hi = jnp.minimum(t + win - left, seq + N_META)
        cnts.append(jnp.maximum(hi - lo, 1).astype(F32))
    return _by_group(sub, cnts)


def _edge_rows(seq, tp):
    reach = max(POOL_WINDOWS) // 2
    return [tp - N_META + t for t in range(reach)] + [seq - reach + 1 + t for t in range(reach - 1)]


def _edge_gain(b, seq, tp, sub):
    return _by_group(sub, [float(w) for w in POOL_WINDOWS]) / _pool_cnt(b, seq, tp, sub)


def _nested_windows(at, lo_offs):
    sums, s, have = [], None, set()
    for g, win in enumerate(POOL_WINDOWS):
        for o in range(lo_offs[g], lo_offs[g] + win):
            if o not in have:
                have.add(o)
                s = at(o) if s is None else s + at(o)
        sums.append(s)
    return sums


def _seq_fwd(z, w_dw, b_dw, seq, gat):
    tp = z.shape[0]
    nt = tp // TM
    na, ng = len(gat.arrays), gat.n

    def body(*refs):
        av_l, av, av_r, ag_l, ag, ag_r, p_l, p, p_r, w_ref, b_ref = refs[:11]
        ac_ref, m_ref = refs[11 + na:13 + na]
        a3, p3, o3, m3, w3, b3, m2d = refs[13 + na + ng:20 + na + ng]
        gat.bind(refs[11:11 + na], refs[13 + na:13 + na + ng], refs[20 + na + ng:])
        i = pl.program_id(0)
        sub = lax.broadcasted_iota(jnp.int32, (NCB, 128), 0)

        @pl.when(i == 0)
        def _():
            gat.issue()
            _tm_fill(w3, 0, 4, lambda r, l: w_ref[pl.ds(r, 8), l])
            for j in range(NCB):
                b3[pl.ds(j, 1), :] = b_ref[:, pl.ds(128 * j, 128)]

        @pl.when(i == max(nt - 2, 0))
        def _():
            gat.forward()

        _tm_fill_ext(a3, (av_l, ag_l), (av, ag), (av_r, ag_r), lambda vg, r, l: vg[0][r, l] * _sig(vg[1][r, l]))
        _tm_fill_ext(p3, p_l, p, p_r, lambda ref, r, l: ref[r, l])

        def conv(g, c):
            for t in range(8):
                acc = b3[...]
                for k in range(CONV_K):
                    acc = acc + _tm_at(w3, k) * _tm_at(a3, 8 * g + t + k + 1)
                o3[_tm_rows(8 * g + t), :] = acc
            return c

        lax.fori_loop(0, TM // 8, conv, 0)
        _tm_read(o3, TM // 8, lambda r, l, tile: ac_ref.__setitem__((r, l), tile))

        inv = _by_group(sub, [1.0 / w for w in POOL_WINDOWS])

        def pool(g, c):
            for t in range(8):
                e = 8 * g + t + HALO
                sums = _nested_windows(lambda o: _tm_at(p3, e + o), [-(w // 2) for w in POOL_WINDOWS])
                m3[_tm_rows(8 * g + t), :] = _by_group(sub, sums) * inv - _tm_at(p3, e)
            return c

        lax.fori_loop(0, TM // 8, pool, 0)
        for b in _edge_rows(seq, tp):
            r = b - i * TM

            @pl.when((r >= 0) & (r < TM))
            def _():
                pv = _tm_at(p3, r + HALO)
                m3[_tm_rows(r), :] = (_tm_at(m3, r) + pv) * _edge_gain(b, seq, tp, sub) - pv

        _tm_read(m3, TM // 8, lambda r, l, tile: m2d.__setitem__((r, l), tile))
        m_ref[...] = m2d[...].astype(BF16)

        @pl.when(i == nt - 1)
        def _():
            gat.finish()

    tmaj = pltpu.VMEM((TM * NCB, 128), F32)
    text = pltpu.VMEM((TME * NCB, 128), F32)
    res = pl.pallas_call(
        body, name="seq_fwd", grid=(nt,),
        in_specs=_halo_specs(0, nt) + _halo_specs(1, nt) + _halo_specs(2, nt)
        + [pl.BlockSpec((32, D), lambda i: (0, 0)), pl.BlockSpec((1, D), lambda i: (0, 0))] + [_whole(a) for a in gat.arrays],
        out_specs=[pl.BlockSpec((TM, D), lambda i: (i, 0))] * 2 + [pl.BlockSpec(memory_space=pl.ANY)] * ng,
        out_shape=[jax.ShapeDtypeStruct((tp, D), F32), jax.ShapeDtypeStruct((tp, D), BF16)] + gat.out_shape,
        scratch_shapes=[text, text, tmaj, tmaj, pltpu.VMEM((32 * NCB, 128), F32), pltpu.VMEM((NCB, 128), F32),
                        pltpu.VMEM((TM, D), F32)] + gat.scratch,
        compiler_params=_params(("arbitrary",), 52),
    )(z, z, z, z, z, z, z, z, z, w_dw, b_dw, *gat.arrays)
    return res[:2], res[2:]


def _ln_stats(ac):
    mu = jnp.mean(ac, axis=-1, keepdims=True)
    xc = ac - mu
    rl = lax.rsqrt(jnp.mean(xc * xc, axis=-1, keepdims=True) + LN_EPS)
    return xc * rl, rl


def _pool_mix(m, wp_ref):
    return jnp.concatenate(
        [_dot(m[:, g * PG:(g + 1) * PG], wp_ref[:, g].reshape(PG, PG)) for g in range(4)], axis=1)


def _mix_fwd(ac, m, z, h0, b_gate, ln_g, ln_b, pool_scale, g_mixw, g_pool, gat):
    tp = h0.shape[0]
    nt = tp // TMS
    na, ng = len(gat.arrays), gat.n

    def body(*refs):
        ac_ref, m_ref, zga, zgb, h_ref, bg_ref, lg_ref, lb_ref, ps_ref, wm_hbm, wp_hbm = refs[:11]
        h1_ref, s_ref, yc_ref, yp_ref, mg_ref, q_ref = refs[11 + na:17 + na]
        wm, wp, sems = refs[17 + na + ng:20 + na + ng]
        gat.bind(refs[11:11 + na], refs[17 + na:17 + na + ng], refs[20 + na + ng:])
        i = pl.program_id(0)

        @pl.when(i == 0)
        def _():
            gat.issue()

        @pl.when(i == max(nt - 4, 0))
        def _():
            gat.forward()

        @pl.when(i == nt - 1)
        def _():
            gat.finish()

        _load_once(i == 0, [(wm_hbm, wm), (wp_hbm, wp)], sems)
        n, _ = _ln_stats(ac_ref[...])
        l = n * lg_ref[...] + lb_ref[...]
        s = (l * _sig(l)).astype(BF16)
        s_ref[...] = s
        yc = _dot(s, wm[:, 0].reshape(D, D))
        q = (_pool_mix(m_ref[...], wp) * ps_ref[...]).astype(BF16)
        q_ref[...] = q
        yp = _dot(q, wm[:, 1].reshape(D, D))
        ga = _sig(zga[...] + bg_ref[:, :D])
        gb = _sig(zgb[...] + bg_ref[:, D:])
        merged = (ga * yc + gb * yp).astype(BF16)
        yc_ref[...] = yc
        yp_ref[...] = yp
        mg_ref[...] = merged
        h1_ref[...] = h_ref[...] + _dot(merged, wm[:, 2].reshape(D, D))

    def tile(col=0):
        return pl.BlockSpec((TMS, D), lambda i: (i, col))

    def vec(w):
        return pl.BlockSpec((1, w), lambda i: (0, 0))

    anys = pl.BlockSpec(memory_space=pl.ANY)
    f32o, b16o = jax.ShapeDtypeStruct((tp, D), F32), jax.ShapeDtypeStruct((tp, D), BF16)
    res = pl.pallas_call(
        body, name="mix_fwd", grid=(nt,),
        in_specs=[tile(), tile(), tile(3), tile(4), tile(), vec(2 * D), vec(D), vec(D), vec(D), anys, anys]
        + [_whole(a) for a in gat.arrays],
        out_specs=[tile()] * 6 + [anys] * ng,
        out_shape=[f32o, b16o, f32o, f32o, b16o, b16o] + gat.out_shape,
        scratch_shapes=[pltpu.VMEM((NDEV, 3, D // NDEV, D), BF16), pltpu.VMEM((NDEV, 4, PG // NDEV, PG), BF16),
                        pltpu.SemaphoreType.DMA((2,))] + gat.scratch,
        compiler_params=_params(("arbitrary",), 52),
    )(ac, m, z, z, h0, b_gate, ln_g, ln_b, pool_scale, g_mixw, g_pool, *gat.arrays)
    return res[:6], res[6:]


def _ffn_fwd(h1, tgt, g_ffn, g_final, w_gu, w_dn):
    tp = h1.shape[0]
    nt = tp // TM
    nx_last = tgt.shape[0] - (nt - 1) * TM

    def body(h_ref, t_ref, gf_ref, gl_ref, wgu_hbm, wdn_hbm,
             fg_ref, fu_ref, v_ref, f_ref, dh2_ref, acc_ref, wgu, wdn, v_sc, h2_sc, diff_sc, sems):
        i, j = pl.program_id(0), pl.program_id(1)
        _load_once((i == 0) & (j == 0), _gu_pairs(wgu_hbm, wgu) + [(wdn_hbm, wdn)], sems)

        @pl.when((i == 0) & (j == 0))
        def _():
            acc_ref[...] = jnp.zeros_like(acc_ref)

        @pl.when(j == 0)
        def _():
            h = h_ref[...]
            r = lax.rsqrt(jnp.mean(h * h, axis=-1, keepdims=True) + RMS_EPS)
            v = (h * r * gf_ref[...]).astype(BF16)
            v_sc[...] = v
            v_ref[...] = v
            h2_sc[...] = h

        v = v_sc[...]
        fg = _dot_nt(v, wgu[0, j])
        fu = _dot_nt(v, wgu[1, j])
        fg_ref[...] = fg
        fu_ref[...] = fu
        f = ((fg * _sig(fg)) * fu).astype(BF16)
        f_ref[...] = f
        h2_sc[...] += _dot(f, wdn[j])

        @pl.when(j == 1)
        def _():
            h2 = h2_sc[...]
            r = lax.rsqrt(jnp.mean(h2 * h2, axis=-1, keepdims=True) + RMS_EPS)
            n2 = h2 * r
            y = n2 * gl_ref[...]

            @pl.when(i < nt - 1)
            def _():
                diff_sc[...] = y - t_ref[...]

            @pl.when(i == nt - 1)
            def _():
                diff_sc[pl.ds(0, nx_last), :] = y[:nx_last] - t_ref[pl.ds(0, nx_last), :]
                diff_sc[pl.ds(nx_last, TM - nx_last), :] = jnp.zeros((TM - nx_last, D), F32)

            diff = diff_sc[...]
            dy = diff * (1.0 / D)
            acc_ref[0:1, :] += jnp.sum(diff * diff, axis=0, keepdims=True)
            acc_ref[1:2, :] += jnp.sum(dy * n2, axis=0, keepdims=True)
            dn = dy * gl_ref[...]
            dh2_ref[...] = r * (dn - n2 * jnp.mean(dn * n2, axis=-1, keepdims=True))

    def tile():
        return pl.BlockSpec((TM, D), lambda i, j: (i, 0))

    def chunk():
        return pl.BlockSpec((TM, FFC), lambda i, j: (i, j))

    def vec():
        return pl.BlockSpec((1, D), lambda i, j: (0, 0))

    anys = pl.BlockSpec(memory_space=pl.ANY)
    hid32, hid16 = jax.ShapeDtypeStruct((tp, DFF), F32), jax.ShapeDtypeStruct((tp, DFF), BF16)
    return pl.pallas_call(
        body, name="ffn_fwd", grid=(nt, 2),
        in_specs=[tile(), tile(), vec(), vec(), anys, anys],
        out_specs=[chunk(), chunk(), tile(), chunk(), tile(), pl.BlockSpec((8, D), lambda i, j: (0, 0))],
        out_shape=[hid32, hid32, jax.ShapeDtypeStruct((tp, D), BF16), hid16, jax.ShapeDtypeStruct((tp, D), F32),
                   jax.ShapeDtypeStruct((8, D), F32)],
        scratch_shapes=[pltpu.VMEM((2, 2, FFC, D), BF16), pltpu.VMEM((2, FFC, D), BF16),
                        pltpu.VMEM((TM, D), BF16), pltpu.VMEM((TM, D), F32), pltpu.VMEM((TM, D), F32),
                        pltpu.SemaphoreType.DMA((2 * NDEV + 1,))],
        compiler_params=_params(("arbitrary", "arbitrary"), 56),
    )(h1, tgt, g_ffn, g_final, w_gu, w_dn)


def _ffn_bwd(dh2, fg, fu, h1, g_ffn, w_gu, w_dn):
    tp = h1.shape[0]
    nt = tp // TM

    def body(dh2_ref, fg_ref, fu_ref, h_ref, gf_ref, wgu_hbm, wdn_hbm,
             dfg_ref, dfu_ref, dh1_ref, acc_ref, wgu, wdn, d_sc, dv_sc, sems):
        i, j = pl.program_id(0), pl.program_id(1)
        _load_once((i == 0) & (j == 0), _gu_pairs(wgu_hbm, wgu) + [(wdn_hbm, wdn)], sems)

        @pl.when((i == 0) & (j == 0))
        def _():
            acc_ref[...] = jnp.zeros_like(acc_ref)

        @pl.when(j == 0)
        def _():
            d_sc[...] = dh2_ref[...].astype(BF16)
            dv_sc[...] = jnp.zeros_like(dv_sc)

        df = _dot_nt(d_sc[...], wdn[j])
        fg = fg_ref[...]
        sg = _sig(fg)
        dfu = (df * (fg * sg)).astype(BF16)
        dfg = (df * fu_ref[...] * (sg * (1.0 + fg * (1.0 - sg)))).astype(BF16)
        dfg_ref[...] = dfg
        dfu_ref[...] = dfu
        dv_sc[...] += _dot(dfg, wgu[0, j]) + _dot(dfu, wgu[1, j])

        @pl.when(j == 1)
        def _():
            h = h_ref[...]
            r = lax.rsqrt(jnp.mean(h * h, axis=-1, keepdims=True) + RMS_EPS)
            n1 = h * r
            dv = dv_sc[...]
            acc_ref[0:1, :] += jnp.sum(dv * n1, axis=0, keepdims=True)
            dn = dv * gf_ref[...]
            dh1_ref[...] = dh2_ref[...] + r * (dn - n1 * jnp.mean(dn * n1, axis=-1, keepdims=True))

    def tile():
        return pl.BlockSpec((TM, D), lambda i, j: (i, 0))

    def chunk():
        return pl.BlockSpec((TM, FFC), lambda i, j: (i, j))

    anys = pl.BlockSpec(memory_space=pl.ANY)
    hid16 = jax.ShapeDtypeStruct((tp, DFF), BF16)
    return pl.pallas_call(
        body, name="ffn_bwd", grid=(nt, 2),
        in_specs=[tile(), chunk(), chunk(), tile(), pl.BlockSpec((1, D), lambda i, j: (0, 0)), anys, anys],
        out_specs=[chunk(), chunk(), tile(), pl.BlockSpec((8, D), lambda i, j: (0, 0))],
        out_shape=[hid16, hid16, jax.ShapeDtypeStruct((tp, D), F32), jax.ShapeDtypeStruct((8, D), F32)],
        scratch_shapes=[pltpu.VMEM((2, 2, FFC, D), BF16), pltpu.VMEM((2, FFC, D), BF16),
                        pltpu.VMEM((TM, D), BF16), pltpu.VMEM((TM, D), F32), pltpu.SemaphoreType.DMA((2 * NDEV + 1,))],
        compiler_params=_params(("arbitrary", "arbitrary"), 56),
    )(dh2, fg, fu, h1, g_ffn, w_gu, w_dn)


def _mix_bwd(dh1, z, yc, yp, ac, m, b_gate, ln_g, ln_b, pool_scale, g_mixw, g_pool, qs):
    tp = dh1.shape[0]
    nt = tp // TMS
    ex = _ChipExchange(qs)
    nq = ex.n

    def body(*refs):
        dh1_ref, zga, zgb, yc_ref, yp_ref, ac_ref, m_ref, bg_ref, lg_ref, lb_ref, ps_ref, wm_hbm, wp_hbm = refs[:13]
        dac_ref, dm_ref, dzg_ref, dyc_ref, dyp_ref, dm2_ref, acc_ref = refs[13 + nq:20 + nq]
        wm, wp, sems = refs[20 + 2 * nq:23 + 2 * nq]
        ex.bind(refs[13:13 + nq], refs[20 + nq:20 + 2 * nq], refs[23 + 2 * nq:])
        first = pl.program_id(0) == 0

        @pl.when(first)
        def _():
            ex.issue()
            acc_ref[...] = jnp.zeros_like(acc_ref)

        _load_once(first, [(wm_hbm, wm), (wp_hbm, wp)], sems)

        dmerged = _dot_nt(dh1_ref[...].astype(BF16), wm[:, 2].reshape(D, D))
        ga = _sig(zga[...] + bg_ref[:, :D])
        gb = _sig(zgb[...] + bg_ref[:, D:])
        dyc = dmerged * ga
        dyp = dmerged * gb
        dza = (dmerged * yc_ref[...]) * (ga * (1.0 - ga))
        dzb = (dmerged * yp_ref[...]) * (gb * (1.0 - gb))
        dzg_ref[:, :D] = dza.astype(BF16)
        dzg_ref[:, D:] = dzb.astype(BF16)
        acc_ref[0:1, :D] += jnp.sum(dza, axis=0, keepdims=True)
        acc_ref[0:1, D:] += jnp.sum(dzb, axis=0, keepdims=True)
        dyc_b = dyc.astype(BF16)
        dyp_b = dyp.astype(BF16)
        dyc_ref[...] = dyc_b
        dyp_ref[...] = dyp_b
        ds = _dot_nt(dyc_b, wm[:, 0].reshape(D, D))
        n, rl = _ln_stats(ac_ref[...])
        l = n * lg_ref[...] + lb_ref[...]
        sg = _sig(l)
        dl = ds * (sg * (1.0 + l * (1.0 - sg)))
        acc_ref[1:2, :D] += jnp.sum(dl * n, axis=0, keepdims=True)
        acc_ref[1:2, D:] += jnp.sum(dl, axis=0, keepdims=True)
        dn = dl * lg_ref[...]
        dac_ref[...] = rl * (dn - jnp.mean(dn, axis=-1, keepdims=True) - n * jnp.mean(dn * n, axis=-1, keepdims=True))
        dq = _dot_nt(dyp_b, wm[:, 1].reshape(D, D))
        mv = m_ref[...]
        acc_ref[2:3, :D] += jnp.sum(dq * _pool_mix(mv, wp), axis=0, keepdims=True)
        dm2 = (dq * ps_ref[...]).astype(BF16)
        dm2_ref[...] = dm2
        dm_ref[...] = jnp.concatenate(
            [_dot_nt(dm2[:, g * PG:(g + 1) * PG], wp[:, g].reshape(PG, PG)) for g in range(4)], axis=1)

        @pl.when(pl.program_id(0) == nt - 1)
        def _():
            ex.finish()

    def tile(col=0):
        return pl.BlockSpec((TMS, D), lambda i: (i, col))

    def vec(w):
        return pl.BlockSpec((1, w), lambda i: (0, 0))

    anys = pl.BlockSpec(memory_space=pl.ANY)
    f32o, b16o = jax.ShapeDtypeStruct((tp, D), F32), jax.ShapeDtypeStruct((tp, D), BF16)
    res = pl.pallas_call(
        body, name="mix_bwd", grid=(nt,),
        in_specs=[tile(), tile(3), tile(4), tile(), tile(), tile(), tile(), vec(2 * D), vec(D), vec(D), vec(D), anys, anys]
        + [anys] * nq,
        out_specs=[tile(), tile(), pl.BlockSpec((TMS, 2 * D), lambda i: (i, 0)), tile(), tile(), tile(),
                   pl.BlockSpec((8, 2 * D), lambda i: (0, 0))] + [anys] * nq,
        out_shape=[f32o, f32o, jax.ShapeDtypeStruct((tp, 2 * D), BF16), b16o, b16o, b16o,
                   jax.ShapeDtypeStruct((8, 2 * D), F32)] + ex.out_shape,
        scratch_shapes=[pltpu.VMEM((NDEV, 3, D // NDEV, D), BF16), pltpu.VMEM((NDEV, 4, PG // NDEV, PG), BF16),
                        pltpu.SemaphoreType.DMA((2,))] + ex.scratch,
        compiler_params=_params(("arbitrary",), 48),
    )(dh1, z, z, yc, yp, ac, m, b_gate, ln_g, ln_b, pool_scale, g_mixw, g_pool, *qs)
    return res[:7], res[7:]


def _seq_bwd(dac, dm, dzg, z, w_dw, seq, qs):
    tp = z.shape[0]
    nt = tp // TM
    ex = _ChipExchange(qs)
    nq = ex.n

    def body(*refs):
        dac_l, dac_c, dac_r, dm_l, dm_c, dm_r, av_l, av, av_r, ag_l, ag, ag_r, dzg_ref, w_ref = refs[:14]
        dz_ref, acc_ref = refs[14 + nq:16 + nq]
        a3, d3, m3, da3, dp3, w3, dw3, da_sc, dp_sc = refs[16 + 2 * nq:25 + 2 * nq]
        ex.bind(refs[14:14 + nq], refs[16 + nq:16 + 2 * nq], refs[25 + 2 * nq:])
        i = pl.program_id(0)
        sub = lax.broadcasted_iota(jnp.int32, (NCB, 128), 0)

        @pl.when(i == 0)
        def _():
            ex.issue()
            dw3[...] = jnp.zeros_like(dw3)
            _tm_fill(w3, 0, 4, lambda r, l: w_ref[pl.ds(r, 8), l])

        _tm_fill_ext(a3, (av_l, ag_l), (av, ag), (av_r, ag_r), lambda vg, r, l: vg[0][r, l] * _sig(vg[1][r, l]))
        _tm_fill_ext(d3, dac_l, dac_c, dac_r, lambda ref, r, l: ref[r, l])
        _tm_fill_ext(m3, dm_l, dm_c, dm_r, lambda ref, r, l: ref[r, l])

        def conv(g, c):
            dcur = [_tm_at(d3, 8 * g + t + HALO) for t in range(8)]
            accs = [None] * 8
            for k in range(CONV_K):
                wk = _tm_at(w3, k)
                s = None
                for t in range(8):
                    term = wk * _tm_at(d3, 8 * g + t + CONV_K - k)
                    accs[t] = term if accs[t] is None else accs[t] + term
                    pr = dcur[t] * _tm_at(a3, 8 * g + t + k + 1)
                    s = pr if s is None else s + pr
                dw3[_tm_rows(k), :] += s
            s = dcur[0]
            for t in range(1, 8):
                s = s + dcur[t]
            dw3[_tm_rows(CONV_K), :] += s
            for t in range(8):
                da3[_tm_rows(8 * g + t), :] = accs[t]
            return c

        lax.fori_loop(0, TM // 8, conv, 0)

        for b in _edge_rows(seq, tp):
            e = lax.rem(b - i * TM + HALO + tp, tp)

            @pl.when(e < TME)
            def _():
                m3[_tm_rows(e), :] = _tm_at(m3, e) * _edge_gain(b, seq, tp, sub)

        inv = _by_group(sub, [1.0 / w for w in POOL_WINDOWS])

        def pool(g, c):
            for t in range(8):
                e = 8 * g + t + HALO
                sums = _nested_windows(lambda o: _tm_at(m3, e + o), [w // 2 + 1 - w for w in POOL_WINDOWS])
                dp3[_tm_rows(8 * g + t), :] = _by_group(sub, sums) * inv
            return c

        lax.fori_loop(0, TM // 8, pool, 0)

        _tm_read(da3, TM // 8, lambda r, l, tile: da_sc.__setitem__((r, l), tile))
        _tm_read(dp3, TM // 8, lambda r, l, tile: dp_sc.__setitem__((r, l), tile))
        sg = _sig(ag[...])
        da = da_sc[...]
        dz_ref[:, 0:D] = (da * sg).astype(BF16)
        dz_ref[:, D:2 * D] = (da * av[...] * (sg * (1.0 - sg))).astype(BF16)
        dz_ref[:, 2 * D:3 * D] = (dp_sc[...] - dm_c[...]).astype(BF16)
        dz_ref[:, 3 * D:] = dzg_ref[...]

        @pl.when(i == nt - 1)
        def _():
            _tm_read(dw3, 4, lambda r, l, tile: acc_ref.__setitem__((r, l), tile))
            ex.finish()

    tmaj = pltpu.VMEM((TM * NCB, 128), F32)
    text = pltpu.VMEM((TME * NCB, 128), F32)
    taps = pltpu.VMEM((32 * NCB, 128), F32)
    anys = pl.BlockSpec(memory_space=pl.ANY)
    res = pl.pallas_call(
        body, name="seq_bwd", grid=(nt,),
        in_specs=_halo_specs(0, nt) + _halo_specs(0, nt) + _halo_specs(0, nt) + _halo_specs(1, nt)
        + [pl.BlockSpec((TM, 2 * D), lambda i: (i, 0)), pl.BlockSpec((32, D), lambda i: (0, 0))] + [anys] * nq,
        out_specs=[pl.BlockSpec((TM, DIN), lambda i: (i, 0)), pl.BlockSpec((32, D), lambda i: (0, 0))] + [anys] * nq,
        out_shape=[jax.ShapeDtypeStruct((tp, DIN), BF16), jax.ShapeDtypeStruct((32, D), F32)] + ex.out_shape,
        scratch_shapes=[text, text, text, tmaj, tmaj, taps, taps, pltpu.VMEM((TM, D), F32), pltpu.VMEM((TM, D), F32)]
        + ex.scratch,
        compiler_params=_params(("arbitrary",), 48),
    )(dac, dac, dac, dm, dm, dm, z, z, z, z, z, z, dzg, w_dw, *qs)
    return res[:2], res[2:]


def _in_bwd(dz, h0, dh1, g_mix, w_g, seq, qs):
    tp = h0.shape[0]
    tm = _pick(tp, TM_IO)
    nt = tp // tm
    ex = _ChipExchange(qs)
    nq = ex.n

    def body(*refs):
        dz_ref, h_ref, dh1_ref, g_ref, w_hbm = refs[:5]
        gx_ref, gmeta_ref, acc_ref = refs[5 + nq:8 + nq]
        w_vm, sems = refs[8 + 2 * nq:10 + 2 * nq]
        ex.bind(refs[5:5 + nq], refs[8 + nq:8 + 2 * nq], refs[10 + 2 * nq:])
        i = pl.program_id(0)

        @pl.when(i == 0)
        def _():
            ex.issue()
            acc_ref[...] = jnp.zeros_like(acc_ref)

        _load_once(i == 0, _win_pairs(w_hbm, w_vm), sems)

        du = _dot_nt(dz_ref[:, :DIN // 2], w_vm[0]) + _dot_nt(dz_ref[:, DIN // 2:], w_vm[1])
        h = h_ref[...]
        r = lax.rsqrt(jnp.mean(h * h, axis=-1, keepdims=True) + RMS_EPS)
        n0 = h * r
        acc_ref[0:1, :] += jnp.sum(du * n0, axis=0, keepdims=True)
        dn = du * g_ref[...]
        gx_ref[...] = dh1_ref[...] + r * (dn - n0 * jnp.mean(dn * n0, axis=-1, keepdims=True))

        @pl.when(i == nt - 1)
        def _():
            gmeta_ref[...] = gx_ref[pl.ds(tm - N_META, N_META), :]
            ex.finish()

    tile = pl.BlockSpec((tm, D), lambda i: (i, 0))
    anys = pl.BlockSpec(memory_space=pl.ANY)
    res = pl.pallas_call(
        body, name="in_bwd", grid=(nt,),
        in_specs=[pl.BlockSpec((tm, DIN), lambda i: (i, 0)), tile, tile, pl.BlockSpec((1, D), lambda i: (0, 0)), anys]
        + [anys] * nq,
        out_specs=[tile, pl.BlockSpec((N_META, D), lambda i: (0, 0)), pl.BlockSpec((8, D), lambda i: (0, 0))] + [anys] * nq,
        out_shape=[jax.ShapeDtypeStruct((seq, D), F32), jax.ShapeDtypeStruct((N_META, D), F32),
                   jax.ShapeDtypeStruct((8, D), F32)] + ex.out_shape,
        scratch_shapes=[pltpu.VMEM((2, D, DIN // 2), BF16), pltpu.SemaphoreType.DMA((NDEV,))] + ex.scratch,
        compiler_params=_params(("arbitrary",), 58),
    )(dz, h0, dh1, g_mix, w_g, *qs)
    return res[:3], res[3:]


def _wgrad_in(u, dz):
    tp = u.shape[0]
    tm = _pick(tp, TM_WG)
    nt = tp // tm
    half = DIN // 2

    def body(u_ref, dz_ref, o_ref, acc):
        t = pl.program_id(1)

        @pl.when(t == 0)
        def _():
            acc[...] = jnp.zeros_like(acc)

        acc[...] += _dot_tn(u_ref[...], dz_ref[...])

        @pl.when(t == nt - 1)
        def _():
            for d in range(4):
                o_ref[d] = acc[:, INB * d:INB * (d + 1)].astype(BF16)

    return pl.pallas_call(
        body, name="wgrad_in", grid=(2, nt),
        in_specs=[pl.BlockSpec((tm, D), lambda h, t: (t, 0)), pl.BlockSpec((tm, half), lambda h, t: (t, h))],
        out_specs=pl.BlockSpec((4, D, INB), lambda h, t: (h, 0, 0), pipeline_mode=pl.Buffered(1)),
        out_shape=jax.ShapeDtypeStruct((NDEV, D, INB), BF16),
        scratch_shapes=[pltpu.VMEM((D, half), F32)],
        compiler_params=_params(("arbitrary", "arbitrary"), 52),
    )(u, dz)


def _wgrad_mix(s, dyc, q, dyp, merged, dh1, m, dm2):
    tp = s.shape[0]
    tm = _pick(tp, TM_WM)
    nt = tp // tm
    rb = D // NDEV

    def body(s_ref, dyc_ref, q_ref, dyp_ref, mg_ref, dh1_ref, m_ref, dm2_ref, o_ref, op_ref, acc, accp):
        t = pl.program_id(0)

        @pl.when(t == 0)
        def _():
            acc[...] = jnp.zeros_like(acc)
            accp[...] = jnp.zeros_like(accp)

        acc[0] += _dot_tn(s_ref[...], dyc_ref[...])
        acc[1] += _dot_tn(q_ref[...], dyp_ref[...])
        acc[2] += _dot_tn(mg_ref[...], dh1_ref[...].astype(BF16))
        for g in range(4):
            accp[g] += _dot_tn(m_ref[:, g * PG:(g + 1) * PG], dm2_ref[:, g * PG:(g + 1) * PG])

        @pl.when(t == nt - 1)
        def _():
            for d in range(NDEV):
                for k in range(3):
                    o_ref[d, k] = acc[k, rb * d:rb * (d + 1), :].astype(BF16)
                for g in range(4):
                    op_ref[d, g] = accp[g, 32 * d:32 * (d + 1), :].astype(BF16)

    tile = pl.BlockSpec((tm, D), lambda t: (t, 0))
    return pl.pallas_call(
        body, name="wgrad_mix", grid=(nt,),
        in_specs=[tile] * 8,
        out_specs=[pl.BlockSpec((NDEV, 3, rb, D), lambda t: (0, 0, 0, 0), pipeline_mode=pl.Buffered(1)),
                   pl.BlockSpec((NDEV, 4, 32, PG), lambda t: (0, 0, 0, 0), pipeline_mode=pl.Buffered(1))],
        out_shape=[jax.ShapeDtypeStruct((NDEV, 3, rb, D), BF16), jax.ShapeDtypeStruct((NDEV, 4, 32, PG), BF16)],
        scratch_shapes=[pltpu.VMEM((3, D, D), F32), pltpu.VMEM((4, PG, PG), F32)],
        compiler_params=_params(("arbitrary",), 56),
    )(s, dyc, q, dyp, merged, dh1, m, dm2)


def _wgrad_gu(v, dfg, dfu):
    tp = v.shape[0]
    tm = _pick(tp, TM_WG)
    nt = tp // tm

    def body(v_ref, dg_ref, du_ref, o_ref, acc):
        k, t = pl.program_id(0), pl.program_id(2)

        @pl.when(t == 0)
        def _():
            acc[...] = jnp.zeros_like(acc)

        @pl.when(k == 0)
        def _():
            acc[...] += _dot_tn(dg_ref[...], v_ref[...])

        @pl.when(k == 1)
        def _():
            acc[...] += _dot_tn(du_ref[...], v_ref[...])

        @pl.when(t == nt - 1)
        def _():
            for d in range(4):
                o_ref[d] = acc[FFB * d:FFB * (d + 1), :].astype(BF16)

    return pl.pallas_call(
        body, name="wgrad_gu", grid=(2, 2, nt),
        in_specs=[pl.BlockSpec((tm, D), lambda k, h, t: (t, 0)),
                  pl.BlockSpec((tm, FFC), lambda k, h, t: (t * (1 - k), h * (1 - k))),
                  pl.BlockSpec((tm, FFC), lambda k, h, t: (t * k, h * k))],
        out_specs=pl.BlockSpec((4, None, FFB, D), lambda k, h, t: (h, k, 0, 0), pipeline_mode=pl.Buffered(1)),
        out_shape=jax.ShapeDtypeStruct((NDEV, 2, FFB, D), BF16),
        scratch_shapes=[pltpu.VMEM((FFC, D), F32)],
        compiler_params=_params(("arbitrary",) * 3, 48),
    )(v, dfg, dfu)


def _wgrad_down(f, dh2):
    tp = f.shape[0]
    tm = _pick(tp, TM_WG)
    nt = tp // tm

    def body(f_ref, d_ref, o_ref, acc):
        t = pl.program_id(1)

        @pl.when(t == 0)
        def _():
            acc[...] = jnp.zeros_like(acc)

        acc[...] += _dot_tn(f_ref[...], d_ref[...].astype(BF16))

        @pl.when(t == nt - 1)
        def _():
            for d in range(4):
                o_ref[d] = acc[FFB * d:FFB * (d + 1), :].astype(BF16)

    return pl.pallas_call(
        body, name="wgrad_down", grid=(2, nt),
        in_specs=[pl.BlockSpec((tm, FFC), lambda h, t: (t, h)), pl.BlockSpec((tm, D), lambda h, t: (t, 0))],
        out_specs=pl.BlockSpec((4, FFB, D), lambda h, t: (h, 0, 0), pipeline_mode=pl.Buffered(1)),
        out_shape=jax.ShapeDtypeStruct((NDEV, FFB, D), BF16),
        scratch_shapes=[pltpu.VMEM((FFC, D), F32)],
        compiler_params=_params(("arbitrary", "arbitrary"), 48),
    )(f, dh2)


def kernel(x, meta_tokens, g_mix, w_in, b_gate, w_dw, b_dw, ln_g, ln_b, w_conv_out, w_pool, pool_scale, w_pool_out, w_o, g_ffn, w_ffn_gate, w_ffn_up, w_ffn_down, g_final, loss_target, m_meta_tokens, m_g_mix, m_w_in, m_b_gate, m_w_dw, m_b_dw, m_ln_g, m_ln_b, m_w_conv_out, m_w_pool, m_pool_scale, m_w_pool_out, m_w_o, m_g_ffn, m_w_ffn_gate, m_w_ffn_up, m_w_ffn_down, m_g_final, v_meta_tokens, v_g_mix, v_w_in, v_b_gate, v_w_dw, v_b_dw, v_ln_g, v_ln_b, v_w_conv_out, v_w_pool, v_pool_scale, v_w_pool_out, v_w_o, v_g_ffn, v_w_ffn_gate, v_w_ffn_up, v_w_ffn_down, v_g_final):
    seq = x.shape[1]
    tp = -(-(seq + 2 * HALO) // TM) * TM
    tm_in = _pick(tp, TM_IO)
    nx_last = seq - (tp // tm_in - 1) * tm_in
    assert 0 < nx_last <= tm_in - 2 * HALO and nx_last % 8 == 0 and 0 < seq - (tp // TM - 1) * TM

    whole = (Ellipsis,)
    g_in, g_small = _all_gather(
        [((D, INB), [(w_in, whole, 0)]),
         ((48, D // NDEV), [(meta_tokens, pl.ds(0, N_META), whole), (w_dw, pl.ds(N_META, CONV_K), 0)])], [BF16, F32])
    ag_mix = _Gather([((3, D // NDEV, D), [(w_conv_out, 0, 0), (w_pool_out, 1, 0), (w_o, 2, 0)]),
                      ((4, PG // NDEV, PG), [(w_pool, whole, 0)])], [BF16, BF16])
    def tr(a):
        return jnp.swapaxes(a, 1, 2)

    ag_gu = _Gather([((2, FFB, D), [(tr(w_ffn_gate), 0, 0), (tr(w_ffn_up), 1, 0)])], [BF16])
    ag_dn = _Gather([((FFB, D), [(w_ffn_down, whole, 0)])], [BF16])
    small_full = g_small.transpose(1, 0, 2).reshape(48, D)
    wdw_full = small_full[N_META:]
    tail = jnp.concatenate([jnp.zeros((tm_in - nx_last - N_META, D), F32), small_full[:N_META]], axis=0)

    (h0, z, u), (g_mixw, g_pool) = _fwd_in(x[0], tail, g_mix, g_in, tp, ag_mix)
    (ac, m), (w_gu,) = _seq_fwd(z, wdw_full, b_dw, seq, ag_gu)
    (h1, s, yc, yp, merged, q), (g_down,) = _mix_fwd(ac, m, z, h0, b_gate, ln_g, ln_b, pool_scale, g_mixw, g_pool, ag_dn)
    w_dn = g_down.reshape(2, FFC, D)
    fg, fu, v, f, dh2, head_acc = _ffn_fwd(h1, loss_target[0], g_ffn, g_final.reshape(1, D), w_gu, w_dn)

    dfg, dfu, dh1, ffn_acc = _ffn_bwd(dh2, fg, fu, h1, g_ffn, w_gu, w_dn)
    own_f, sib_f, q_f = _rs_pair("rs_pair_ffn", [_wgrad_gu(v, dfg, dfu), _wgrad_down(f, dh2)])
    (dac, dm, dzg, dyc, dyp, dm2, mix_acc), rel_f = _mix_bwd(
        dh1, z, yc, yp, ac, m, b_gate, ln_g, ln_b, pool_scale, g_mixw, g_pool, q_f)
    own_m, sib_m, q_m = _rs_pair("rs_pair_mix", list(_wgrad_mix(s, dyc, q, dyp, merged, dh1, m, dm2)))
    (dz, seq_acc), rel_m = _seq_bwd(dac, dm, dzg, z, wdw_full, seq, q_m)
    own_i, sib_i, q_i = _rs_pair("rs_pair_in", [_wgrad_in(u, dz)])
    (grad_x, g_meta, in_acc), rel_i = _in_bwd(dz, h0, dh1, g_mix, g_in, seq, q_i)
    small_g = jnp.concatenate([g_meta, seq_acc[:CONV_K], jnp.zeros((1, D), F32)], axis=0)
    p_small = small_g.reshape(48, NDEV, D // NDEV).transpose(1, 0, 2).astype(BF16)
    rep_g = jnp.concatenate([
        in_acc[0:1], mix_acc[0:1, :D], mix_acc[0:1, D:], seq_acc[CONV_K:CONV_K + 1], mix_acc[1:2, :D], mix_acc[1:2, D:],
        mix_acc[2:3, :D], ffn_acc[0:1], head_acc[1:2], head_acc[0:1], jnp.zeros((REP_ROWS - 10, D), F32)], axis=0)
    own_s, sib_s, rel_s, rep_all = _reduce_scatter([p_small], rep_g)
    owns = [own_i[0], own_s[0], own_m[0], own_m[1], own_f[0], own_f[1]]
    sibs = [sib_i[0], sib_s[0], sib_m[0], sib_m[1], sib_f[0], sib_f[1]]
    rels = [rel_i[0], rel_s[0], rel_m[0], rel_m[1], rel_f[0], rel_f[1]]

    def lead(a):
        return a.reshape(1, *a.shape)

    def stack4(a, lead_dims):
        return a.reshape(*lead_dims, 1, 4 * 32, PG)

    (r_in,) = _adamw_multi("adamw_in", lead(owns[0]), sibs[0][:, None], rels[0][:, None], [w_in], [m_w_in], [v_w_in], 4)
    r_meta, r_dw = _adamw_meta_dw(owns[1], sibs[1], rels[1], (meta_tokens, m_meta_tokens, v_meta_tokens),
                                  (w_dw, m_w_dw, v_w_dw))
    r_conv, r_pout, r_o = _adamw_multi("adamw_mix", owns[2], sibs[2], rels[2], [w_conv_out, w_pool_out, w_o],
                                       [m_w_conv_out, m_w_pool_out, m_w_o], [v_w_conv_out, v_w_pool_out, v_w_o], 1)
    (r_pool,) = _adamw_multi("adamw_pool", stack4(owns[3], ()), stack4(sibs[3], (4,)), stack4(rels[3], (3,)),
                             [w_pool.reshape(1, 128, PG)], [m_w_pool.reshape(1, 128, PG)], [v_w_pool.reshape(1, 128, PG)], 1)
    r_pool = tuple(a.reshape(w_pool.shape) for a in r_pool)
    r_gate, r_up = _adamw_multi("adamw_gu", owns[4], sibs[4], rels[4], [tr(w_ffn_gate), tr(w_ffn_up)],
                                [tr(m_w_ffn_gate), tr(m_w_ffn_up)], [tr(v_w_ffn_gate), tr(v_w_ffn_up)], 2)
    r_gate, r_up = tuple(tr(a) for a in r_gate), tuple(tr(a) for a in r_up)
    (r_down,) = _adamw_multi("adamw_down", lead(owns[5]), sibs[5][:, None], rels[5][:, None],
                             [w_ffn_down], [m_w_ffn_down], [v_w_ffn_down], 2)
    row = (1, D)
    loss, reps = _adamw_rep(
        rep_all,
        [g_mix, b_gate, b_dw, ln_g, ln_b, pool_scale, g_ffn, g_final.reshape(row)],
        [m_g_mix, m_b_gate, m_b_dw, m_ln_g, m_ln_b, m_pool_scale, m_g_ffn, m_g_final.reshape(row)],
        [v_g_mix, v_b_gate, v_b_dw, v_ln_g, v_ln_b, v_pool_scale, v_g_ffn, v_g_final.reshape(row)])
    r_gmix, r_bg, r_bdw, r_lg, r_lb, r_ps, r_gffn, r_gfin = reps
    r_gfin = tuple(a.reshape(D) for a in r_gfin)

    in_order = [r_meta, r_gmix, r_in, r_bg, r_dw, r_bdw, r_lg, r_lb, r_conv, r_pool, r_ps, r_pout, r_o, r_gffn,
                r_gate, r_up, r_down, r_gfin]
    return (loss.reshape(()), grad_x[None], *[r[0] for r in in_order], *[r[1] for r in in_order],
            *[r[2] for r in in_order], *[r[3] for r in in_order])
```

```python
import math

import jax
import jax.numpy as jnp
from jax import lax
from jax.experimental import pallas as pl
from jax.experimental.pallas import tpu as pltpu

F32, BF16 = jnp.float32, jnp.bfloat16
MESH_ID = pl.DeviceIdType.MESH
NDEV = 8

D = 1024
N_META = 16
CONV_K = 31
HALO = 16
POOL_WINDOWS = (2, 4, 8, 16)
PG = 256
DIN = 5 * D
DFF = 2816
FFB = DFF // NDEV
FFC = DFF // 2
INB = DIN // NDEV
RMS_EPS = 1e-6
LN_EPS = 1e-5
ADAM_LR, ADAM_B1, ADAM_B2, ADAM_EPS, ADAM_WD, ADAM_STEP = 0.001, 0.9, 0.999, 1e-08, 0.01, 10

TM = 384
TMS = 192
TM_IO = 704
TM_WG = 1408
TM_WM = 704
MIB = 2 ** 20


def _sig(x):
    return 0.5 * jnp.tanh(0.5 * x) + 0.5


def _dot(a, b):
    return jnp.dot(a, b, preferred_element_type=F32)


def _dot_nt(a, b):
    return lax.dot_general(a, b, (((1,), (1,)), ((), ())), preferred_element_type=F32)


def _dot_tn(a, b):
    return lax.dot_general(a, b, (((0,), (0,)), ((), ())), preferred_element_type=F32)


def _pick(tp, pref):
    return pref if tp % pref == 0 else TM


def _params(sem, vmem_mib):
    return pltpu.CompilerParams(dimension_semantics=sem, vmem_limit_bytes=vmem_mib * MIB)


def _load_once(first, pairs, sems):
    @pl.when(first)
    def _():
        cps = [pltpu.make_async_copy(s, d, sems.at[k]) for k, (s, d) in enumerate(pairs)]
        for cp in cps:
            cp.start()
        for cp in cps:
            cp.wait()


def _place():
    x, y, c = lax.axis_index("x"), lax.axis_index("y"), lax.axis_index("c")
    return x, y, c


class _Gather:
    def __init__(self, groups, dtypes):
        self.groups, self.dtypes, self.n = groups, dtypes, len(groups)
        self.arrays = [a for _, parts in groups for a, _, _ in parts]
        self.out_shape = [jax.ShapeDtypeStruct((NDEV, *s), dt) for (s, _), dt in zip(groups, dtypes)]
        self.scratch = [pltpu.VMEM(s, dt) for (s, _), dt in zip(groups, dtypes)] + [
            pltpu.SemaphoreType.DMA((7 * self.n,)), pltpu.SemaphoreType.DMA((7 * self.n,)),
            pltpu.SemaphoreType.DMA((self.n,))]

    def bind(self, ins, outs, scratch):
        self.ins, self.outs, self.stages = ins, outs, scratch[:self.n]
        self.send_sems, self.recv_sems, self.local_sems = scratch[self.n:]
        return self

    def _copy(self, w, k, block, to, src=None):
        dst = self.outs[w].at[4 * block[0] + 2 * block[1] + block[2]]
        return pltpu.make_async_remote_copy(
            src_ref=dst if src is None else src, dst_ref=dst,
            send_sem=self.send_sems.at[7 * w + k], recv_sem=self.recv_sems.at[7 * w + k],
            device_id=to, device_id_type=MESH_ID)

    def _first(self):
        x, y, c = _place()
        me, sibling = (x, y, c), (x, y, 1 - c)
        chips = [(1 - x, y), (x, 1 - y), (1 - x, 1 - y)]
        mine, first = [], []
        for w in range(self.n):
            mine.append(pltpu.make_async_copy(self.stages[w], self.outs[w].at[4 * x + 2 * y + c], self.local_sems.at[w]))
            first.append(self._copy(w, 0, me, sibling, src=self.stages[w]))
            first += [self._copy(w, 1 + j, me, (*chip, c), src=self.stages[w]) for j, chip in enumerate(chips)]
        return mine, first

    def _passed(self):
        x, y, c = _place()
        chips = [(1 - x, y), (x, 1 - y), (1 - x, 1 - y)]
        return [self._copy(w, 4 + j, (*chip, c), (x, y, 1 - c)) for w in range(self.n) for j, chip in enumerate(chips)]

    def issue(self):
        a = 0
        for w in range(self.n):
            shape, parts = self.groups[w]
            if sum(arr.size for arr, _, _ in parts) < math.prod(shape):
                self.stages[w][...] = jnp.zeros(shape, self.dtypes[w])
            for _, dst, src in parts:
                self.stages[w][dst] = self.ins[a][src].astype(self.dtypes[w])
                a += 1
        mine, first = self._first()
        for cp in mine + first:
            cp.start()

    def forward(self):
        x, y, c = _place()
        chips = [(1 - x, y), (x, 1 - y), (1 - x, 1 - y)]
        passed = self._passed()
        for w in range(self.n):
            for j, chip in enumerate(chips):
                self._copy(w, 1 + j, (*chip, c), (x, y, c)).wait_recv()
                passed[3 * w + j].start()

    def finish(self):
        x, y, c = _place()
        chips = [(1 - x, y), (x, 1 - y), (1 - x, 1 - y)]
        for w in range(self.n):
            self._copy(w, 0, (x, y, 1 - c), (x, y, c)).wait_recv()
            for j, chip in enumerate(chips):
                self._copy(w, 4 + j, (*chip, 1 - c), (x, y, c)).wait_recv()
        mine, first = self._first()
        for cp in first + self._passed():
            cp.wait_send()
        for cp in mine:
            cp.wait()


def _all_gather(groups, dtypes):
    ag = _Gather(groups, dtypes)
    na, n = len(ag.arrays), ag.n

    def body(*refs):
        ag.bind(refs[:na], refs[na:na + n], refs[na + n:])
        ag.issue()
        ag.forward()
        ag.finish()

    return pl.pallas_call(
        body, name="ag_weights", out_shape=ag.out_shape,
        in_specs=[pl.BlockSpec(memory_space=pltpu.VMEM)] * na,
        out_specs=[pl.BlockSpec(memory_space=pl.ANY)] * n,
        scratch_shapes=ag.scratch,
        compiler_params=pltpu.CompilerParams(vmem_limit_bytes=40 * MIB),
    )(*ag.arrays)


class _ChipExchange:
    def __init__(self, qs):
        self.n = len(qs)
        self.out_shape = [jax.ShapeDtypeStruct(q.shape, q.dtype) for q in qs]
        self.scratch = [pltpu.SemaphoreType.DMA((3 * self.n,)), pltpu.SemaphoreType.DMA((3 * self.n,))]

    def bind(self, qs, rels, scratch):
        self.qs, self.rels = qs, rels
        self.send_sems, self.recv_sems = scratch
        return self

    def _copies(self):
        x, y, c = _place()
        chips = [(1 - x, y), (x, 1 - y), (1 - x, 1 - y)]
        return [pltpu.make_async_remote_copy(
            src_ref=self.qs[w].at[j], dst_ref=self.rels[w].at[j],
            send_sem=self.send_sems.at[3 * w + j], recv_sem=self.recv_sems.at[3 * w + j],
            device_id=(*chips[j], c), device_id_type=MESH_ID) for w in range(self.n) for j in range(3)]

    def issue(self):
        for cp in self._copies():
            cp.start()

    def finish(self):
        cps = self._copies()
        for cp in cps:
            cp.wait_recv()
        for cp in cps:
            cp.wait_send()


def _reduce_scatter(parts, small):
    n = len(parts)
    blks = [p.shape[1:] for p in parts]

    def body(*refs):
        ps, small_ref = refs[:n], refs[n]
        o = n + 1
        owns, sibs, rels, small_out = refs[o:o + n], refs[o + n:o + 2 * n], refs[o + 2 * n:o + 3 * n], refs[o + 3 * n]
        o += 3 * n + 1
        pa, pb, qst = refs[o:o + n], refs[o + n:o + 2 * n], refs[o + 2 * n:o + 3 * n]
        s1_send, s1_recv, s2_send, s2_recv, sm_send, sm_recv, lsem = refs[o + 3 * n:]
        x, y, c = _place()
        me = 4 * x + 2 * y + c
        sibling = (x, y, 1 - c)
        chips = [(1 - x, y), (x, 1 - y), (1 - x, 1 - y)]
        all_chips = [(x, y)] + chips

        own_cps = []
        for w in range(n):
            cp = pltpu.make_async_copy(ps[w].at[me], owns[w], lsem.at[w])
            cp.start()
            own_cps.append(cp)
        sm_own = pltpu.make_async_copy(small_ref, small_out.at[me], lsem.at[n])
        sm_own.start()

        def small_copy(r):
            peer = ((x + (r >> 2)) % 2, (y + ((r >> 1) & 1)) % 2, (c + (r & 1)) % 2)
            return pltpu.make_async_remote_copy(
                src_ref=small_ref, dst_ref=small_out.at[me], send_sem=sm_send.at[r - 1], recv_sem=sm_recv.at[r - 1],
                device_id=peer, device_id_type=MESH_ID)

        sm_cps = [small_copy(r) for r in range(1, NDEV)]
        for cp in sm_cps:
            cp.start()

        def pair_copy(w, rel):
            cx, cy = all_chips[rel]
            return pltpu.make_async_remote_copy(
                src_ref=ps[w].at[4 * cx + 2 * cy + (1 - c)], dst_ref=sibs[w].at[rel],
                send_sem=s1_send.at[4 * w + rel], recv_sem=s1_recv.at[4 * w + rel],
                device_id=sibling, device_id_type=MESH_ID)

        def chip_copy(w, j):
            return pltpu.make_async_remote_copy(
                src_ref=qst[w].at[j], dst_ref=rels[w].at[j],
                send_sem=s2_send.at[3 * w + j], recv_sem=s2_recv.at[3 * w + j],
                device_id=(*chips[j], c), device_id_type=MESH_ID)

        pair_cps = [pair_copy(w, rel) for w in range(n) for rel in (1, 2, 3, 0)]
        for cp in pair_cps:
            cp.start()
        chip_cps = []
        for w in range(n):
            for j, (cx, cy) in enumerate(chips):
                pair_copy(w, 1 + j).wait_recv()
                la = pltpu.make_async_copy(ps[w].at[4 * cx + 2 * cy + c], pa[w], lsem.at[n + 1])
                lb = pltpu.make_async_copy(sibs[w].at[1 + j], pb[w], lsem.at[n + 2])
                la.start()
                lb.start()
                la.wait()
                lb.wait()
                qst[w][j] = (pa[w][...].astype(F32) + pb[w][...].astype(F32)).astype(BF16)
                cp = chip_copy(w, j)
                cp.start()
                chip_cps.append(cp)
        for w in range(n):
            pair_copy(w, 0).wait_recv()
            for j in range(3):
                chip_copy(w, j).wait_recv()
        for cp in sm_cps:
            cp.wait_recv()
        for cp in pair_cps + chip_cps + sm_cps:
            cp.wait_send()
        for cp in own_cps:
            cp.wait()
        sm_own.wait()

    any_spec = pl.BlockSpec(memory_space=pl.ANY)
    outs = pl.pallas_call(
        body, name="rs_grads",
        out_shape=[jax.ShapeDtypeStruct(b, BF16) for b in blks]
        + [jax.ShapeDtypeStruct((4, *b), BF16) for b in blks]
        + [jax.ShapeDtypeStruct((3, *b), BF16) for b in blks]
        + [jax.ShapeDtypeStruct((NDEV, *small.shape), F32)],
        in_specs=[any_spec] * (n + 1),
        out_specs=[any_spec] * (3 * n + 1),
        scratch_shapes=[pltpu.VMEM(b, BF16) for b in blks] + [pltpu.VMEM(b, BF16) for b in blks]
        + [pltpu.VMEM((3, *b), BF16) for b in blks]
        + [pltpu.SemaphoreType.DMA((4 * n,)), pltpu.SemaphoreType.DMA((4 * n,)),
           pltpu.SemaphoreType.DMA((3 * n,)), pltpu.SemaphoreType.DMA((3 * n,)),
           pltpu.SemaphoreType.DMA((NDEV - 1,)), pltpu.SemaphoreType.DMA((NDEV - 1,)),
           pltpu.SemaphoreType.DMA((n + 3,))],
        compiler_params=pltpu.CompilerParams(vmem_limit_bytes=40 * MIB),
    )(*parts, small)
    return outs[:n], outs[n:2 * n], outs[2 * n:3 * n], outs[3 * n]


def _rs_pair(name, parts):
    n = len(parts)
    blks = [p.shape[1:] for p in parts]

    def body(*refs):
        ps = refs[:n]
        owns, sibs, qs = refs[n:2 * n], refs[2 * n:3 * n], refs[3 * n:4 * n]
        pa, pb, qst = refs[4 * n:5 * n], refs[5 * n:6 * n], refs[6 * n:7 * n]
        s_send, s_recv, lsem = refs[7 * n:]
        x, y, c = _place()
        chips = [(1 - x, y), (x, 1 - y), (1 - x, 1 - y)]
        all_chips = [(x, y)] + chips

        own_cps = [pltpu.make_async_copy(ps[w].at[4 * x + 2 * y + c], owns[w], lsem.at[w]) for w in range(n)]
        for cp in own_cps:
            cp.start()

        def pair_copy(w, rel):
            cx, cy = all_chips[rel]
            return pltpu.make_async_remote_copy(
                src_ref=ps[w].at[4 * cx + 2 * cy + (1 - c)], dst_ref=sibs[w].at[rel],
                send_sem=s_send.at[4 * w + rel], recv_sem=s_recv.at[4 * w + rel],
                device_id=(x, y, 1 - c), device_id_type=MESH_ID)

        pair_cps = [pair_copy(w, rel) for w in range(n) for rel in (1, 2, 3, 0)]
        for cp in pair_cps:
            cp.start()
        q_cps = []
        for w in range(n):
            for j, (cx, cy) in enumerate(chips):
                la = pltpu.make_async_copy(ps[w].at[4 * cx + 2 * cy + c], pa[w], lsem.at[n])
                lb = pltpu.make_async_copy(sibs[w].at[1 + j], pb[w], lsem.at[n + 1])
                la.start()
                pair_copy(w, 1 + j).wait_recv()
                lb.start()
                la.wait()
                lb.wait()
                qst[w][j] = (pa[w][...].astype(F32) + pb[w][...].astype(F32)).astype(BF16)
            cp = pltpu.make_async_copy(qst[w], qs[w], lsem.at[n + 2 + w])
            cp.start()
            q_cps.append(cp)
        for w in range(n):
            pair_copy(w, 0).wait_recv()
        for cp in pair_cps:
            cp.wait_send()
        for cp in own_cps + q_cps:
            cp.wait()

    any_spec = pl.BlockSpec(memory_space=pl.ANY)
    outs = pl.pallas_call(
        body, name=name,
        out_shape=[jax.ShapeDtypeStruct(b, BF16) for b in blks]
        + [jax.ShapeDtypeStruct((4, *b), BF16) for b in blks]
        + [jax.ShapeDtypeStruct((3, *b), BF16) for b in blks],
        in_specs=[any_spec] * n,
        out_specs=[any_spec] * (3 * n),
        scratch_shapes=[pltpu.VMEM(b, BF16) for b in blks] + [pltpu.VMEM(b, BF16) for b in blks]
        + [pltpu.VMEM((3, *b), BF16) for b in blks]
        + [pltpu.SemaphoreType.DMA((4 * n,)), pltpu.SemaphoreType.DMA((4 * n,)), pltpu.SemaphoreType.DMA((2 * n + 2,))],
        compiler_params=pltpu.CompilerParams(vmem_limit_bytes=40 * MIB),
    )(*parts)
    return outs[:n], outs[n:2 * n], outs[2 * n:3 * n]


def _adamw_math(g, w, m, v):
    m = ADAM_B1 * m + (1.0 - ADAM_B1) * g
    v = ADAM_B2 * v + (1.0 - ADAM_B2) * (g * g)
    m_hat = m / (1.0 - ADAM_B1 ** ADAM_STEP)
    v_hat = v / (1.0 - ADAM_B2 ** ADAM_STEP)
    delta = -ADAM_LR * (m_hat / (jnp.sqrt(v_hat) + ADAM_EPS) + ADAM_WD * w)
    return delta, m, v


def _adamw_multi(name, own, sib, rel, ws, ms, vs, row_grid):
    k_n, r_n, c_n = own.shape
    rbk = r_n // row_grid

    def body(*refs):
        own_ref, sib_ref, r0_ref, r1_ref, r2_ref = refs[:5]
        w_refs, m_refs, v_refs = refs[5:5 + k_n], refs[5 + k_n:5 + 2 * k_n], refs[5 + 2 * k_n:5 + 3 * k_n]
        outs = refs[5 + 3 * k_n:]
        for k in range(k_n):
            g = own_ref[k].astype(F32) + sib_ref[k].astype(F32)
            g = g + r0_ref[k].astype(F32)
            g = g + r1_ref[k].astype(F32)
            g = g + r2_ref[k].astype(F32)
            delta, mm, vv = _adamw_math(g, w_refs[k][0], m_refs[k][0], v_refs[k][0])
            outs[4 * k][0] = g
            outs[4 * k + 1][0] = delta
            outs[4 * k + 2][0] = mm
            outs[4 * k + 3][0] = vv

    def lead(j):
        return pl.BlockSpec((None, k_n, rbk, c_n), lambda g: (j, 0, g, 0))

    wspec = pl.BlockSpec((1, rbk, c_n), lambda g: (0, g, 0))
    shp = jax.ShapeDtypeStruct((1, r_n, c_n), F32)
    res = pl.pallas_call(
        body, name=name, grid=(row_grid,),
        in_specs=[pl.BlockSpec((k_n, rbk, c_n), lambda g: (0, g, 0)), lead(0), lead(0), lead(1), lead(2)] + [wspec] * (3 * k_n),
        out_specs=[wspec] * (4 * k_n), out_shape=[shp] * (4 * k_n),
        compiler_params=_params(("arbitrary",), 40),
    )(own, sib, rel, rel, rel, *ws, *ms, *vs)
    return [tuple(res[4 * k:4 * k + 4]) for k in range(k_n)]


def _adamw_meta_dw(own, sib, rel, meta, dw):
    def body(own_ref, sib_ref, rel_ref, wm, mm, vm, wd, md, vd, *outs):
        def gsum(rows):
            g = own_ref[rows, :].astype(F32) + sib_ref[0, rows, :].astype(F32)
            for j in range(3):
                g = g + rel_ref[j, rows, :].astype(F32)
            return g

        g = gsum(pl.ds(0, N_META))
        delta, m2, v2 = _adamw_math(g, wm[...], mm[...], vm[...])
        for o, val in zip(outs[:4], (g, delta, m2, v2)):
            o[...] = val
        g = gsum(pl.ds(N_META, CONV_K))
        delta, m2, v2 = _adamw_math(g, wd[0], md[0], vd[0])
        for o, val in zip(outs[4:], (g, delta, m2, v2)):
            o[0] = val

    s_meta = jax.ShapeDtypeStruct(meta[0].shape, F32)
    s_dw = jax.ShapeDtypeStruct(dw[0].shape, F32)
    res = pl.pallas_call(body, name="adamw_meta_dw", out_shape=[s_meta] * 4 + [s_dw] * 4)(own, sib, rel, *meta, *dw)
    return tuple(res[:4]), tuple(res[4:])


REP_ROWS = 16


def _adamw_rep(gathered, ws, ms, vs):
    rows = [(0, 1), (1, 2), (3, 1), (4, 1), (5, 1), (6, 1), (7, 1), (8, 1)]

    def body(g_ref, *refs):
        w_refs, m_refs, v_refs = refs[:8], refs[8:16], refs[16:24]
        loss_ref, outs, acc = refs[24], refs[25:57], refs[57]
        g = g_ref[0]
        for d in range(1, NDEV):
            g = g + g_ref[d]
        acc[...] = g
        loss_ref[...] = (0.5 / D) * jnp.sum(acc[pl.ds(9, 1), :], axis=1, keepdims=True)
        for p, (r0, nr) in enumerate(rows):
            for h in range(nr):
                cols = pl.ds(h * D, D)
                gp = acc[pl.ds(r0 + h, 1), :]
                delta, mm, vv = _adamw_math(gp, w_refs[p][:, cols], m_refs[p][:, cols], v_refs[p][:, cols])
                for o, val in zip(outs[4 * p:4 * p + 4], (gp, delta, mm, vv)):
                    o[:, cols] = val

    shapes = [jax.ShapeDtypeStruct(w.shape, F32) for w in ws]
    res = pl.pallas_call(
        body, name="adamw_rep",
        out_shape=[jax.ShapeDtypeStruct((1, 1), F32)] + [s for s in shapes for _ in range(4)],
        scratch_shapes=[pltpu.VMEM((REP_ROWS, D), F32)],
    )(gathered, *ws, *ms, *vs)
    return res[0], [tuple(res[1 + 4 * p:5 + 4 * p]) for p in range(8)]


def _gu_pairs(w_hbm, w_vm):
    half = NDEV // 2
    return [(w_hbm.at[d, i], w_vm.at[i, d // half, pl.ds(FFB * (d % half), FFB), :])
            for i in range(2) for d in range(NDEV)]


def _win_pairs(w_hbm, w_vm):
    return [(w_hbm.at[q], w_vm.at[q // 2, :, pl.ds(2 * INB * (q % 2), 2 * INB)]) for q in range(4)]


def _whole(a):
    nd = a.ndim
    return pl.BlockSpec(a.shape, lambda *g: (0,) * nd)


CHIPW = 2 * INB
PHASE_CHIP = (1, 0, 2)


class _GatherIn:
    scratch = [pltpu.VMEM((D, INB), BF16), pltpu.SemaphoreType.DMA((7,)), pltpu.SemaphoreType.DMA((7,)),
               pltpu.SemaphoreType.DMA((1,))]

    def bind(self, w_ref, w_vm, scratch):
        self.w_ref, self.w_vm = w_ref, w_vm
        self.stage, self.send_sems, self.recv_sems, self.local_sem = scratch
        return self

    def _win(self, chip, core):
        return self.w_vm.at[2 * chip[0] + chip[1], :, pl.ds(INB * core, INB)]

    def _copy(self, k, chip, core, to, src=None):
        dst = self._win(chip, core)
        return pltpu.make_async_remote_copy(
            src_ref=dst if src is None else src, dst_ref=dst, send_sem=self.send_sems.at[k],
            recv_sem=self.recv_sems.at[k], device_id=to, device_id_type=MESH_ID)

    def _mine(self, cs):
        x, y, _ = _place()
        return pltpu.make_async_copy(self.stage, self._win((x, y), cs), self.local_sem.at[0])

    def issue(self, cs):
        x, y, _ = _place()
        chips = [(1 - x, y), (x, 1 - y), (1 - x, 1 - y)]
        self.stage[...] = self.w_ref[0].astype(BF16)
        self._mine(cs).start()
        self._copy(0, (x, y), cs, (x, y, 1 - cs), src=self.stage).start()
        for j, chip in enumerate(chips):
            self._copy(1 + j, (x, y), cs, (*chip, cs), src=self.stage).start()

    def wait_chip(self, phase, cs):
        x, y, _ = _place()
        chips = [(1 - x, y), (x, 1 - y), (1 - x, 1 - y)]
        if phase == 0:
            self._mine(cs).wait()
            self._copy(0, (x, y), 1 - cs, (x, y, cs)).wait_recv()
        else:
            j = PHASE_CHIP[phase - 1]
            self._copy(1 + j, chips[j], cs, (x, y, cs)).wait_recv()
            self._copy(4 + j, chips[j], cs, (x, y, 1 - cs)).start()
            self._copy(4 + j, chips[j], 1 - cs, (x, y, cs)).wait_recv()

    def finish(self, cs):
        x, y, _ = _place()
        for k in range(7):
            self._copy(k, (x, y), cs, (x, y, cs), src=self.stage).wait_send()


def _fwd_in(x2, tail, g_mix, w_in, order, tp, ag):
    tm = _pick(tp, TM_IO)
    nt = tp // tm
    nx_last = tm - tail.shape[0]
    na, ng = len(ag.arrays), ag.n
    gin = _GatherIn()

    def body(order_ref, *refs):
        x_ref, tail_ref, g_ref, w_ref = refs[:4]
        h_ref, z_ref, u_ref, wout_ref = refs[4 + na:8 + na]
        w_vm, u_all, osem = refs[8 + na + ng:11 + na + ng]
        gin.bind(w_ref, w_vm, refs[11 + na + ng:15 + na + ng])
        ag.bind(refs[4:4 + na], refs[8 + na:8 + na + ng], refs[15 + na + ng:])
        ph, i = pl.program_id(0), pl.program_id(1)
        core = lax.axis_index("c")
        first = (ph == 0) & (i == 0)
        last = (ph == 3) & (i == nt - 1)
        for cs in range(2):
            @pl.when(first & (core == cs))
            def _():
                gin.issue(cs)

        @pl.when(first)
        def _():
            ag.issue()

        for cs in range(2):
            for p in range(4):
                @pl.when((ph == p) & (i == 0) & (core == cs))
                def _():
                    gin.wait_chip(p, cs)

        @pl.when((ph == 2) & (i == 0))
        def _():
            ag.forward()

        out_copy = pltpu.make_async_copy(w_vm, wout_ref, osem.at[0])

        @pl.when((ph == 3) & (i == 0))
        def _():
            out_copy.start()

        @pl.when((ph == 0) & (i < nt - 1))
        def _():
            h_ref[...] = x_ref[...]

        @pl.when((ph == 0) & (i == nt - 1))
        def _():
            h_ref[pl.ds(0, nx_last), :] = x_ref[pl.ds(0, nx_last), :]
            h_ref[pl.ds(nx_last, tm - nx_last), :] = tail_ref[...]

        @pl.when(ph == 0)
        def _():
            xv = h_ref[...]
            r = lax.rsqrt(jnp.mean(xv * xv, axis=-1, keepdims=True) + RMS_EPS)
            u = (xv * r * g_ref[...]).astype(BF16)
            u_ref[...] = u
            u_all[i] = u

        z_ref[...] = _dot(u_all[i], w_vm[order_ref[ph]])

        @pl.when(last)
        def _():
            ag.finish()
            out_copy.wait()

        for cs in range(2):
            @pl.when(last & (core == cs))
            def _():
                gin.finish(cs)

    def rows(ph, i, order):
        return (jnp.where(ph == 0, i, nt - 1), 0)

    tile = pl.BlockSpec((tm, D), rows)
    anys = pl.BlockSpec(memory_space=pl.ANY)
    res = pl.pallas_call(
        body, name="fwd_in",
        grid_spec=pltpu.PrefetchScalarGridSpec(
            num_scalar_prefetch=1, grid=(4, nt),
            in_specs=[tile, pl.BlockSpec(tail.shape, lambda ph, i, order: (0, 0)),
                      pl.BlockSpec((1, D), lambda ph, i, order: (0, 0)), _whole(w_in)] + [_whole(a) for a in ag.arrays],
            out_specs=[tile, pl.BlockSpec((tm, CHIPW), lambda ph, i, order: (i, order[ph])), tile, anys] + [anys] * ng,
            scratch_shapes=[pltpu.VMEM((4, D, CHIPW), BF16), pltpu.VMEM((nt, tm, D), BF16), pltpu.SemaphoreType.DMA((1,))]
            + gin.scratch + ag.scratch),
        out_shape=[jax.ShapeDtypeStruct((tp, D), F32), jax.ShapeDtypeStruct((tp, DIN), F32),
                   jax.ShapeDtypeStruct((tp, D), BF16), jax.ShapeDtypeStruct((4, D, CHIPW), BF16)] + ag.out_shape,
        compiler_params=_params(("arbitrary", "arbitrary"), 58),
    )(order, x2, tail, g_mix, w_in, *ag.arrays)
    return res[:4], res[4:]


def _halo_specs(col, nt, width=D):
    r = TM // HALO
    nb = nt * r
    return [pl.BlockSpec((HALO, width), lambda i: ((i * r + nb - 1) % nb, col)),
            pl.BlockSpec((TM, width), lambda i: (i, col)),
            pl.BlockSpec((HALO, width), lambda i: (((i + 1) * r) % nb, col))]


NCB = D // 128
TME = TM + 2 * HALO


def _tm_fill(dst, time0, groups, tile_fn):
    def body(g, c):
        for j in range(NCB):
            dst[pl.ds((time0 + 8 * g) * NCB + j, 8, stride=NCB), :] = tile_fn(pl.multiple_of(8 * g, 8), pl.ds(128 * j, 128))
        return c

    lax.fori_loop(0, groups, body, 0)


def _tm_fill_ext(dst, left, cur, right, fn):
    _tm_fill(dst, 0, HALO // 8, lambda r, l: fn(left, pl.ds(r, 8), l))
    _tm_fill(dst, HALO, TM // 8, lambda r, l: fn(cur, pl.ds(r, 8), l))
    _tm_fill(dst, HALO + TM, HALO // 8, lambda r, l: fn(right, pl.ds(r, 8), l))


def _tm_read(src, groups, store_fn):
    def body(g, c):
        for j in range(NCB):
            store_fn(pl.ds(pl.multiple_of(8 * g, 8), 8), pl.ds(128 * j, 128), src[pl.ds(8 * g * NCB + j, 8, stride=NCB), :])
        return c

    lax.fori_loop(0, groups, body, 0)


def _tm_rows(t):
    return pl.ds(t * NCB if isinstance(t, int) else pl.multiple_of(t * NCB, NCB), NCB)


def _tm_at(ref, t):
    return ref[_tm_rows(t), :]


def _by_group(sub, vals):
    return jnp.where(sub < 2, vals[0], jnp.where(sub < 4, vals[1], jnp.where(sub < 6, vals[2], vals[3])))


def _pool_cnt(b, seq, tp, sub):
    b = jnp.where(b < 0, b + tp, b)
    b = jnp.where(b >= tp, b - tp, b)
    t = jnp.where(b < seq, b + N_META, b - (tp - N_META))
    cnts = []
    for win in POOL_WINDOWS:
        left = win // 2
        lo = jnp.maximum(t - left, 0)
        hi = jnp.minimum(t + win - left, seq + N_META)
        cnts.append(jnp.maximum(hi - lo, 1).astype(F32))
    return _by_group(sub, cnts)


def _edge_rows(seq, tp):
    reach = max(POOL_WINDOWS) // 2
    return [tp - N_META + t for t in range(reach)] + [seq - reach + 1 + t for t in range(reach - 1)]


def _edge_gain(b, seq, tp, sub):
    return _by_group(sub, [float(w) for w in POOL_WINDOWS]) / _pool_cnt(b, seq, tp, sub)


def _nested_windows(at, lo_offs):
    sums, s, have = [], None, set()
    for g, win in enumerate(POOL_WINDOWS):
        for o in range(lo_offs[g], lo_offs[g] + win):
            if o not in have:
                have.add(o)
                s = at(o) if s is None else s + at(o)
        sums.append(s)
    return sums


def _seq_fwd(z, w_dw, b_dw, seq, gat):
    tp = z.shape[0]
    nt = tp // TM
    na, ng = len(gat.arrays), gat.n

    def body(*refs):
        av_l, av, av_r, ag_l, ag, ag_r, p_l, p, p_r, w_ref, b_ref = refs[:11]
        ac_ref, m_ref = refs[11 + na:13 + na]
        a3, p3, o3, m3, w3, b3, m2d = refs[13 + na + ng:20 + na + ng]
        gat.bind(refs[11:11 + na], refs[13 + na:13 + na + ng], refs[20 + na + ng:])
        i = pl.program_id(0)
        sub = lax.broadcasted_iota(jnp.int32, (NCB, 128), 0)

        @pl.when(i == 0)
        def _():
            gat.issue()
            _tm_fill(w3, 0, 4, lambda r, l: w_ref[pl.ds(r, 8), l])
            for j in range(NCB):
                b3[pl.ds(j, 1), :] = b_ref[:, pl.ds(128 * j, 128)]

        @pl.when(i == max(nt - 2, 0))
        def _():
            gat.forward()

        _tm_fill_ext(a3, (av_l, ag_l), (av, ag), (av_r, ag_r), lambda vg, r, l: vg[0][r, l] * _sig(vg[1][r, l]))
        _tm_fill_ext(p3, p_l, p, p_r, lambda ref, r, l: ref[r, l])

        def conv(g, c):
            for t in range(8):
                acc = b3[...]
                for k in range(CONV_K):
                    acc = acc + _tm_at(w3, k) * _tm_at(a3, 8 * g + t + k + 1)
                o3[_tm_rows(8 * g + t), :] = acc
            return c

        lax.fori_loop(0, TM // 8, conv, 0)
        _tm_read(o3, TM // 8, lambda r, l, tile: ac_ref.__setitem__((r, l), tile))

        inv = _by_group(sub, [1.0 / w for w in POOL_WINDOWS])

        def pool(g, c):
            for t in range(8):
                e = 8 * g + t + HALO
                sums = _nested_windows(lambda o: _tm_at(p3, e + o), [-(w // 2) for w in POOL_WINDOWS])
                m3[_tm_rows(8 * g + t), :] = _by_group(sub, sums) * inv - _tm_at(p3, e)
            return c

        lax.fori_loop(0, TM // 8, pool, 0)
        for b in _edge_rows(seq, tp):
            r = b - i * TM

            @pl.when((r >= 0) & (r < TM))
            def _():
                pv = _tm_at(p3, r + HALO)
                m3[_tm_rows(r), :] = (_tm_at(m3, r) + pv) * _edge_gain(b, seq, tp, sub) - pv

        _tm_read(m3, TM // 8, lambda r, l, tile: m2d.__setitem__((r, l), tile))
        m_ref[...] = m2d[...].astype(BF16)

        @pl.when(i == nt - 1)
        def _():
            gat.finish()

    tmaj = pltpu.VMEM((TM * NCB, 128), F32)
    text = pltpu.VMEM((TME * NCB, 128), F32)
    res = pl.pallas_call(
        body, name="seq_fwd", grid=(nt,),
        in_specs=_halo_specs(0, nt) + _halo_specs(1, nt) + _halo_specs(2, nt)
        + [pl.BlockSpec((32, D), lambda i: (0, 0)), pl.BlockSpec((1, D), lambda i: (0, 0))] + [_whole(a) for a in gat.arrays],
        out_specs=[pl.BlockSpec((TM, D), lambda i: (i, 0))] * 2 + [pl.BlockSpec(memory_space=pl.ANY)] * ng,
        out_shape=[jax.ShapeDtypeStruct((tp, D), F32), jax.ShapeDtypeStruct((tp, D), BF16)] + gat.out_shape,
        scratch_shapes=[text, text, tmaj, tmaj, pltpu.VMEM((32 * NCB, 128), F32), pltpu.VMEM((NCB, 128), F32),
                        pltpu.VMEM((TM, D), F32)] + gat.scratch,
        compiler_params=_params(("arbitrary",), 52),
    )(z, z, z, z, z, z, z, z, z, w_dw, b_dw, *gat.arrays)
    return res[:2], res[2:]


def _ln_stats(ac):
    mu = jnp.mean(ac, axis=-1, keepdims=True)
    xc = ac - mu
    rl = lax.rsqrt(jnp.mean(xc * xc, axis=-1, keepdims=True) + LN_EPS)
    return xc * rl, rl


def _pool_mix(m, wp_ref):
    return jnp.concatenate(
        [_dot(m[:, g * PG:(g + 1) * PG], wp_ref[:, g].reshape(PG, PG)) for g in range(4)], axis=1)


def _mix_fwd(ac, m, z, h0, b_gate, ln_g, ln_b, pool_scale, g_mixw, g_pool, gat):
    tp = h0.shape[0]
    nt = tp // TMS
    na, ng = len(gat.arrays), gat.n

    def body(*refs):
        ac_ref, m_ref, zga, zgb, h_ref, bg_ref, lg_ref, lb_ref, ps_ref, wm_hbm, wp_hbm = refs[:11]
        h1_ref, s_ref, yc_ref, yp_ref, mg_ref, q_ref = refs[11 + na:17 + na]
        wm, wp, sems = refs[17 + na + ng:20 + na + ng]
        gat.bind(refs[11:11 + na], refs[17 + na:17 + na + ng], refs[20 + na + ng:])
        i = pl.program_id(0)

        @pl.when(i == 0)
        def _():
            gat.issue()

        @pl.when(i == max(nt - 4, 0))
        def _():
            gat.forward()

        @pl.when(i == nt - 1)
        def _():
            gat.finish()

        _load_once(i == 0, [(wm_hbm, wm), (wp_hbm, wp)], sems)
        n, _ = _ln_stats(ac_ref[...])
        l = n * lg_ref[...] + lb_ref[...]
        s = (l * _sig(l)).astype(BF16)
        s_ref[...] = s
        yc = _dot(s, wm[:, 0].reshape(D, D))
        q = (_pool_mix(m_ref[...], wp) * ps_ref[...]).astype(BF16)
        q_ref[...] = q
        yp = _dot(q, wm[:, 1].reshape(D, D))
        ga = _sig(zga[...] + bg_ref[:, :D])
        gb = _sig(zgb[...] + bg_ref[:, D:])
        merged = (ga * yc + gb * yp).astype(BF16)
        yc_ref[...] = yc
        yp_ref[...] = yp
        mg_ref[...] = merged
        h1_ref[...] = h_ref[...] + _dot(merged, wm[:, 2].reshape(D, D))

    def tile(col=0):
        return pl.BlockSpec((TMS, D), lambda i: (i, col))

    def vec(w):
        return pl.BlockSpec((1, w), lambda i: (0, 0))

    anys = pl.BlockSpec(memory_space=pl.ANY)
    f32o, b16o = jax.ShapeDtypeStruct((tp, D), F32), jax.ShapeDtypeStruct((tp, D), BF16)
    res = pl.pallas_call(
        body, name="mix_fwd", grid=(nt,),
        in_specs=[tile(), tile(), tile(3), tile(4), tile(), vec(2 * D), vec(D), vec(D), vec(D), anys, anys]
        + [_whole(a) for a in gat.arrays],
        out_specs=[tile()] * 6 + [anys] * ng,
        out_shape=[f32o, b16o, f32o, f32o, b16o, b16o] + gat.out_shape,
        scratch_shapes=[pltpu.VMEM((NDEV, 3, D // NDEV, D), BF16), pltpu.VMEM((NDEV, 4, PG // NDEV, PG), BF16),
                        pltpu.SemaphoreType.DMA((2,))] + gat.scratch,
        compiler_params=_params(("arbitrary",), 52),
    )(ac, m, z, z, h0, b_gate, ln_g, ln_b, pool_scale, g_mixw, g_pool, *gat.arrays)
    return res[:6], res[6:]


def _ffn_fwd(h1, tgt, g_ffn, g_final, w_gu, w_dn):
    tp = h1.shape[0]
    nt = tp // TM
    nx_last = tgt.shape[0] - (nt - 1) * TM

    def body(h_ref, t_ref, gf_ref, gl_ref, wgu_hbm, wdn_hbm,
             fg_ref, fu_ref, v_ref, f_ref, dh2_ref, acc_ref, wgu, wdn, v_sc, h2_sc, diff_sc, sems):
        i, j = pl.program_id(0), pl.program_id(1)
        _load_once((i == 0) & (j == 0), _gu_pairs(wgu_hbm, wgu) + [(wdn_hbm, wdn)], sems)

        @pl.when((i == 0) & (j == 0))
        def _():
            acc_ref[...] = jnp.zeros_like(acc_ref)

        @pl.when(j == 0)
        def _():
            h = h_ref[...]
            r = lax.rsqrt(jnp.mean(h * h, axis=-1, keepdims=True) + RMS_EPS)
            v = (h * r * gf_ref[...]).astype(BF16)
            v_sc[...] = v
            v_ref[...] = v
            h2_sc[...] = h

        v = v_sc[...]
        fg = _dot_nt(v, wgu[0, j])
        fu = _dot_nt(v, wgu[1, j])
        fg_ref[...] = fg
        fu_ref[...] = fu
        f = ((fg * _sig(fg)) * fu).astype(BF16)
        f_ref[...] = f
        h2_sc[...] += _dot(f, wdn[j])

        @pl.when(j == 1)
        def _():
            h2 = h2_sc[...]
            r = lax.rsqrt(jnp.mean(h2 * h2, axis=-1, keepdims=True) + RMS_EPS)
            n2 = h2 * r
            y = n2 * gl_ref[...]

            @pl.when(i < nt - 1)
            def _():
                diff_sc[...] = y - t_ref[...]

            @pl.when(i == nt - 1)
            def _():
                diff_sc[pl.ds(0, nx_last), :] = y[:nx_last] - t_ref[pl.ds(0, nx_last), :]
                diff_sc[pl.ds(nx_last, TM - nx_last), :] = jnp.zeros((TM - nx_last, D), F32)

            diff = diff_sc[...]
            dy = diff * (1.0 / D)
            acc_ref[0:1, :] += jnp.sum(diff * diff, axis=0, keepdims=True)
            acc_ref[1:2, :] += jnp.sum(dy * n2, axis=0, keepdims=True)
            dn = dy * gl_ref[...]
            dh2_ref[...] = r * (dn - n2 * jnp.mean(dn * n2, axis=-1, keepdims=True))

    def tile():
        return pl.BlockSpec((TM, D), lambda i, j: (i, 0))

    def chunk():
        return pl.BlockSpec((TM, FFC), lambda i, j: (i, j))

    def vec():
        return pl.BlockSpec((1, D), lambda i, j: (0, 0))

    anys = pl.BlockSpec(memory_space=pl.ANY)
    hid32, hid16 = jax.ShapeDtypeStruct((tp, DFF), F32), jax.ShapeDtypeStruct((tp, DFF), BF16)
    return pl.pallas_call(
        body, name="ffn_fwd", grid=(nt, 2),
        in_specs=[tile(), tile(), vec(), vec(), anys, anys],
        out_specs=[chunk(), chunk(), tile(), chunk(), tile(), pl.BlockSpec((8, D), lambda i, j: (0, 0))],
        out_shape=[hid32, hid32, jax.ShapeDtypeStruct((tp, D), BF16), hid16, jax.ShapeDtypeStruct((tp, D), F32),
                   jax.ShapeDtypeStruct((8, D), F32)],
        scratch_shapes=[pltpu.VMEM((2, 2, FFC, D), BF16), pltpu.VMEM((2, FFC, D), BF16),
                        pltpu.VMEM((TM, D), BF16), pltpu.VMEM((TM, D), F32), pltpu.VMEM((TM, D), F32),
                        pltpu.SemaphoreType.DMA((2 * NDEV + 1,))],
        compiler_params=_params(("arbitrary", "arbitrary"), 56),
    )(h1, tgt, g_ffn, g_final, w_gu, w_dn)


def _ffn_bwd(dh2, fg, fu, h1, g_ffn, w_gu, w_dn):
    tp = h1.shape[0]
    nt = tp // TM

    def body(dh2_ref, fg_ref, fu_ref, h_ref, gf_ref, wgu_hbm, wdn_hbm,
             dfg_ref, dfu_ref, dh1_ref, acc_ref, wgu, wdn, d_sc, dv_sc, sems):
        i, j = pl.program_id(0), pl.program_id(1)
        _load_once((i == 0) & (j == 0), _gu_pairs(wgu_hbm, wgu) + [(wdn_hbm, wdn)], sems)

        @pl.when((i == 0) & (j == 0))
        def _():
            acc_ref[...] = jnp.zeros_like(acc_ref)

        @pl.when(j == 0)
        def _():
            d_sc[...] = dh2_ref[...].astype(BF16)
            dv_sc[...] = jnp.zeros_like(dv_sc)

        df = _dot_nt(d_sc[...], wdn[j])
        fg = fg_ref[...]
        sg = _sig(fg)
        dfu = (df * (fg * sg)).astype(BF16)
        dfg = (df * fu_ref[...] * (sg * (1.0 + fg * (1.0 - sg)))).astype(BF16)
        dfg_ref[...] = dfg
        dfu_ref[...] = dfu
        dv_sc[...] += _dot(dfg, wgu[0, j]) + _dot(dfu, wgu[1, j])

        @pl.when(j == 1)
        def _():
            h = h_ref[...]
            r = lax.rsqrt(jnp.mean(h * h, axis=-1, keepdims=True) + RMS_EPS)
            n1 = h * r
            dv = dv_sc[...]
            acc_ref[0:1, :] += jnp.sum(dv * n1, axis=0, keepdims=True)
            dn = dv * gf_ref[...]
            dh1_ref[...] = dh2_ref[...] + r * (dn - n1 * jnp.mean(dn * n1, axis=-1, keepdims=True))

    def tile():
        return pl.BlockSpec((TM, D), lambda i, j: (i, 0))

    def chunk():
        return pl.BlockSpec((TM, FFC), lambda i, j: (i, j))

    anys = pl.BlockSpec(memory_space=pl.ANY)
    hid16 = jax.ShapeDtypeStruct((tp, DFF), BF16)
    return pl.pallas_call(
        body, name="ffn_bwd", grid=(nt, 2),
        in_specs=[tile(), chunk(), chunk(), tile(), pl.BlockSpec((1, D), lambda i, j: (0, 0)), anys, anys],
        out_specs=[chunk(), chunk(), tile(), pl.BlockSpec((8, D), lambda i, j: (0, 0))],
        out_shape=[hid16, hid16, jax.ShapeDtypeStruct((tp, D), F32), jax.ShapeDtypeStruct((8, D), F32)],
        scratch_shapes=[pltpu.VMEM((2, 2, FFC, D), BF16), pltpu.VMEM((2, FFC, D), BF16),
                        pltpu.VMEM((TM, D), BF16), pltpu.VMEM((TM, D), F32), pltpu.SemaphoreType.DMA((2 * NDEV + 1,))],
        compiler_params=_params(("arbitrary", "arbitrary"), 56),
    )(dh2, fg, fu, h1, g_ffn, w_gu, w_dn)


def _mix_bwd(dh1, z, yc, yp, ac, m, b_gate, ln_g, ln_b, pool_scale, g_mixw, g_pool, qs):
    tp = dh1.shape[0]
    nt = tp // TMS
    ex = _ChipExchange(qs)
    nq = ex.n

    def body(*refs):
        dh1_ref, zga, zgb, yc_ref, yp_ref, ac_ref, m_ref, bg_ref, lg_ref, lb_ref, ps_ref, wm_hbm, wp_hbm = refs[:13]
        dac_ref, dm_ref, dzg_ref, dyc_ref, dyp_ref, dm2_ref, acc_ref = refs[13 + nq:20 + nq]
        wm, wp, sems = refs[20 + 2 * nq:23 + 2 * nq]
        ex.bind(refs[13:13 + nq], refs[20 + nq:20 + 2 * nq], refs[23 + 2 * nq:])
        first = pl.program_id(0) == 0

        @pl.when(first)
        def _():
            ex.issue()
            acc_ref[...] = jnp.zeros_like(acc_ref)

        _load_once(first, [(wm_hbm, wm), (wp_hbm, wp)], sems)

        dmerged = _dot_nt(dh1_ref[...].astype(BF16), wm[:, 2].reshape(D, D))
        ga = _sig(zga[...] + bg_ref[:, :D])
        gb = _sig(zgb[...] + bg_ref[:, D:])
        dyc = dmerged * ga
        dyp = dmerged * gb
        dza = (dmerged * yc_ref[...]) * (ga * (1.0 - ga))
        dzb = (dmerged * yp_ref[...]) * (gb * (1.0 - gb))
        dzg_ref[:, :D] = dza.astype(BF16)
        dzg_ref[:, D:] = dzb.astype(BF16)
        acc_ref[0:1, :D] += jnp.sum(dza, axis=0, keepdims=True)
        acc_ref[0:1, D:] += jnp.sum(dzb, axis=0, keepdims=True)
        dyc_b = dyc.astype(BF16)
        dyp_b = dyp.astype(BF16)
        dyc_ref[...] = dyc_b
        dyp_ref[...] = dyp_b
        ds = _dot_nt(dyc_b, wm[:, 0].reshape(D, D))
        n, rl = _ln_stats(ac_ref[...])
        l = n * lg_ref[...] + lb_ref[...]
        sg = _sig(l)
        dl = ds * (sg * (1.0 + l * (1.0 - sg)))
        acc_ref[1:2, :D] += jnp.sum(dl * n, axis=0, keepdims=True)
        acc_ref[1:2, D:] += jnp.sum(dl, axis=0, keepdims=True)
        dn = dl * lg_ref[...]
        dac_ref[...] = rl * (dn - jnp.mean(dn, axis=-1, keepdims=True) - n * jnp.mean(dn * n, axis=-1, keepdims=True))
        dq = _dot_nt(dyp_b, wm[:, 1].reshape(D, D))
        mv = m_ref[...]
        acc_ref[2:3, :D] += jnp.sum(dq * _pool_mix(mv, wp), axis=0, keepdims=True)
        dm2 = (dq * ps_ref[...]).astype(BF16)
        dm2_ref[...] = dm2
        dm_ref[...] = jnp.concatenate(
            [_dot_nt(dm2[:, g * PG:(g + 1) * PG], wp[:, g].reshape(PG, PG)) for g in range(4)], axis=1)

        @pl.when(pl.program_id(0) == nt - 1)
        def _():
            ex.finish()

    def tile(col=0):
        return pl.BlockSpec((TMS, D), lambda i: (i, col))

    def vec(w):
        return pl.BlockSpec((1, w), lambda i: (0, 0))

    anys = pl.BlockSpec(memory_space=pl.ANY)
    f32o, b16o = jax.ShapeDtypeStruct((tp, D), F32), jax.ShapeDtypeStruct((tp, D), BF16)
    res = pl.pallas_call(
        body, name="mix_bwd", grid=(nt,),
        in_specs=[tile(), tile(3), tile(4), tile(), tile(), tile(), tile(), vec(2 * D), vec(D), vec(D), vec(D), anys, anys]
        + [anys] * nq,
        out_specs=[tile(), tile(), pl.BlockSpec((TMS, 2 * D), lambda i: (i, 0)), tile(), tile(), tile(),
                   pl.BlockSpec((8, 2 * D), lambda i: (0, 0))] + [anys] * nq,
        out_shape=[f32o, f32o, jax.ShapeDtypeStruct((tp, 2 * D), BF16), b16o, b16o, b16o,
                   jax.ShapeDtypeStruct((8, 2 * D), F32)] + ex.out_shape,
        scratch_shapes=[pltpu.VMEM((NDEV, 3, D // NDEV, D), BF16), pltpu.VMEM((NDEV, 4, PG // NDEV, PG), BF16),
                        pltpu.SemaphoreType.DMA((2,))] + ex.scratch,
        compiler_params=_params(("arbitrary",), 48),
    )(dh1, z, z, yc, yp, ac, m, b_gate, ln_g, ln_b, pool_scale, g_mixw, g_pool, *qs)
    return res[:7], res[7:]


def _seq_bwd(dac, dm, dzg, z, w_dw, seq, qs):
    tp = z.shape[0]
    nt = tp // TM
    ex = _ChipExchange(qs)
    nq = ex.n

    def body(*refs):
        dac_l, dac_c, dac_r, dm_l, dm_c, dm_r, av_l, av, av_r, ag_l, ag, ag_r, dzg_ref, w_ref = refs[:14]
        dz_ref, acc_ref = refs[14 + nq:16 + nq]
        a3, d3, m3, da3, dp3, w3, dw3, da_sc, dp_sc = refs[16 + 2 * nq:25 + 2 * nq]
        ex.bind(refs[14:14 + nq], refs[16 + nq:16 + 2 * nq], refs[25 + 2 * nq:])
        i = pl.program_id(0)
        sub = lax.broadcasted_iota(jnp.int32, (NCB, 128), 0)

        @pl.when(i == 0)
        def _():
            ex.issue()
            dw3[...] = jnp.zeros_like(dw3)
            _tm_fill(w3, 0, 4, lambda r, l: w_ref[pl.ds(r, 8), l])

        _tm_fill_ext(a3, (av_l, ag_l), (av, ag), (av_r, ag_r), lambda vg, r, l: vg[0][r, l] * _sig(vg[1][r, l]))
        _tm_fill_ext(d3, dac_l, dac_c, dac_r, lambda ref, r, l: ref[r, l])
        _tm_fill_ext(m3, dm_l, dm_c, dm_r, lambda ref, r, l: ref[r, l])

        def conv(g, c):
            dcur = [_tm_at(d3, 8 * g + t + HALO) for t in range(8)]
            accs = [None] * 8
            for k in range(CONV_K):
                wk = _tm_at(w3, k)
                s = None
                for t in range(8):
                    term = wk * _tm_at(d3, 8 * g + t + CONV_K - k)
                    accs[t] = term if accs[t] is None else accs[t] + term
                    pr = dcur[t] * _tm_at(a3, 8 * g + t + k + 1)
                    s = pr if s is None else s + pr
                dw3[_tm_rows(k), :] += s
            s = dcur[0]
            for t in range(1, 8):
                s = s + dcur[t]
            dw3[_tm_rows(CONV_K), :] += s
            for t in range(8):
                da3[_tm_rows(8 * g + t), :] = accs[t]
            return c

        lax.fori_loop(0, TM // 8, conv, 0)

        for b in _edge_rows(seq, tp):
            e = lax.rem(b - i * TM + HALO + tp, tp)

            @pl.when(e < TME)
            def _():
                m3[_tm_rows(e), :] = _tm_at(m3, e) * _edge_gain(b, seq, tp, sub)

        inv = _by_group(sub, [1.0 / w for w in POOL_WINDOWS])

        def pool(g, c):
            for t in range(8):
                e = 8 * g + t + HALO
                sums = _nested_windows(lambda o: _tm_at(m3, e + o), [w // 2 + 1 - w for w in POOL_WINDOWS])
                dp3[_tm_rows(8 * g + t), :] = _by_group(sub, sums) * inv
            return c

        lax.fori_loop(0, TM // 8, pool, 0)

        _tm_read(da3, TM // 8, lambda r, l, tile: da_sc.__setitem__((r, l), tile))
        _tm_read(dp3, TM // 8, lambda r, l, tile: dp_sc.__setitem__((r, l), tile))
        sg = _sig(ag[...])
        da = da_sc[...]
        dz_ref[:, 0:D] = (da * sg).astype(BF16)
        dz_ref[:, D:2 * D] = (da * av[...] * (sg * (1.0 - sg))).astype(BF16)
        dz_ref[:, 2 * D:3 * D] = (dp_sc[...] - dm_c[...]).astype(BF16)
        dz_ref[:, 3 * D:] = dzg_ref[...]

        @pl.when(i == nt - 1)
        def _():
            _tm_read(dw3, 4, lambda r, l, tile: acc_ref.__setitem__((r, l), tile))
            ex.finish()

    tmaj = pltpu.VMEM((TM * NCB, 128), F32)
    text = pltpu.VMEM((TME * NCB, 128), F32)
    taps = pltpu.VMEM((32 * NCB, 128), F32)
    anys = pl.BlockSpec(memory_space=pl.ANY)
    res = pl.pallas_call(
        body, name="seq_bwd", grid=(nt,),
        in_specs=_halo_specs(0, nt) + _halo_specs(0, nt) + _halo_specs(0, nt) + _halo_specs(1, nt)
        + [pl.BlockSpec((TM, 2 * D), lambda i: (i, 0)), pl.BlockSpec((32, D), lambda i: (0, 0))] + [anys] * nq,
        out_specs=[pl.BlockSpec((TM, DIN), lambda i: (i, 0)), pl.BlockSpec((32, D), lambda i: (0, 0))] + [anys] * nq,
        out_shape=[jax.ShapeDtypeStruct((tp, DIN), BF16), jax.ShapeDtypeStruct((32, D), F32)] + ex.out_shape,
        scratch_shapes=[text, text, text, tmaj, tmaj, taps, taps, pltpu.VMEM((TM, D), F32), pltpu.VMEM((TM, D), F32)]
        + ex.scratch,
        compiler_params=_params(("arbitrary",), 48),
    )(dac, dac, dac, dm, dm, dm, z, z, z, z, z, z, dzg, w_dw, *qs)
    return res[:2], res[2:]


def _in_bwd(dz, h0, dh1, g_mix, w_g, seq, qs):
    tp = h0.shape[0]
    tm = _pick(tp, TM_IO)
    nt = tp // tm
    ex = _ChipExchange(qs)
    nq = ex.n

    def body(*refs):
        dz_ref, h_ref, dh1_ref, g_ref, w_hbm = refs[:5]
        gx_ref, gmeta_ref, acc_ref = refs[5 + nq:8 + nq]
        w_vm, sems = refs[8 + 2 * nq:10 + 2 * nq]
        ex.bind(refs[5:5 + nq], refs[8 + nq:8 + 2 * nq], refs[10 + 2 * nq:])
        i = pl.program_id(0)

        @pl.when(i == 0)
        def _():
            ex.issue()
            acc_ref[...] = jnp.zeros_like(acc_ref)

        _load_once(i == 0, _win_pairs(w_hbm, w_vm), sems)

        du = _dot_nt(dz_ref[:, :DIN // 2], w_vm[0]) + _dot_nt(dz_ref[:, DIN // 2:], w_vm[1])
        h = h_ref[...]
        r = lax.rsqrt(jnp.mean(h * h, axis=-1, keepdims=True) + RMS_EPS)
        n0 = h * r
        acc_ref[0:1, :] += jnp.sum(du * n0, axis=0, keepdims=True)
        dn = du * g_ref[...]
        gx_ref[...] = dh1_ref[...] + r * (dn - n0 * jnp.mean(dn * n0, axis=-1, keepdims=True))

        @pl.when(i == nt - 1)
        def _():
            gmeta_ref[...] = gx_ref[pl.ds(tm - N_META, N_META), :]
            ex.finish()

    tile = pl.BlockSpec((tm, D), lambda i: (i, 0))
    anys = pl.BlockSpec(memory_space=pl.ANY)
    res = pl.pallas_call(
        body, name="in_bwd", grid=(nt,),
        in_specs=[pl.BlockSpec((tm, DIN), lambda i: (i, 0)), tile, tile, pl.BlockSpec((1, D), lambda i: (0, 0)), anys]
        + [anys] * nq,
        out_specs=[tile, pl.BlockSpec((N_META, D), lambda i: (0, 0)), pl.BlockSpec((8, D), lambda i: (0, 0))] + [anys] * nq,
        out_shape=[jax.ShapeDtypeStruct((seq, D), F32), jax.ShapeDtypeStruct((N_META, D), F32),
                   jax.ShapeDtypeStruct((8, D), F32)] + ex.out_shape,
        scratch_shapes=[pltpu.VMEM((2, D, DIN // 2), BF16), pltpu.SemaphoreType.DMA((NDEV,))] + ex.scratch,
        compiler_params=_params(("arbitrary",), 58),
    )(dz, h0, dh1, g_mix, w_g, *qs)
    return res[:3], res[3:]


def _wgrad_in(u, dz):
    tp = u.shape[0]
    tm = _pick(tp, TM_WG)
    nt = tp // tm
    half = DIN // 2

    def body(u_ref, dz_ref, o_ref, acc):
        t = pl.program_id(1)

        @pl.when(t == 0)
        def _():
            acc[...] = jnp.zeros_like(acc)

        acc[...] += _dot_tn(u_ref[...], dz_ref[...])

        @pl.when(t == nt - 1)
        def _():
            for d in range(4):
                o_ref[d] = acc[:, INB * d:INB * (d + 1)].astype(BF16)

    return pl.pallas_call(
        body, name="wgrad_in", grid=(2, nt),
        in_specs=[pl.BlockSpec((tm, D), lambda h, t: (t, 0)), pl.BlockSpec((tm, half), lambda h, t: (t, h))],
        out_specs=pl.BlockSpec((4, D, INB), lambda h, t: (h, 0, 0), pipeline_mode=pl.Buffered(1)),
        out_shape=jax.ShapeDtypeStruct((NDEV, D, INB), BF16),
        scratch_shapes=[pltpu.VMEM((D, half), F32)],
        compiler_params=_params(("arbitrary", "arbitrary"), 52),
    )(u, dz)


def _wgrad_mix(s, dyc, q, dyp, merged, dh1, m, dm2):
    tp = s.shape[0]
    tm = _pick(tp, TM_WM)
    nt = tp // tm
    rb = D // NDEV

    def body(s_ref, dyc_ref, q_ref, dyp_ref, mg_ref, dh1_ref, m_ref, dm2_ref, o_ref, op_ref, acc, accp):
        t = pl.program_id(0)

        @pl.when(t == 0)
        def _():
            acc[...] = jnp.zeros_like(acc)
            accp[...] = jnp.zeros_like(accp)

        acc[0] += _dot_tn(s_ref[...], dyc_ref[...])
        acc[1] += _dot_tn(q_ref[...], dyp_ref[...])
        acc[2] += _dot_tn(mg_ref[...], dh1_ref[...].astype(BF16))
        for g in range(4):
            accp[g] += _dot_tn(m_ref[:, g * PG:(g + 1) * PG], dm2_ref[:, g * PG:(g + 1) * PG])

        @pl.when(t == nt - 1)
        def _():
            for d in range(NDEV):
                for k in range(3):
                    o_ref[d, k] = acc[k, rb * d:rb * (d + 1), :].astype(BF16)
                for g in range(4):
                    op_ref[d, g] = accp[g, 32 * d:32 * (d + 1), :].astype(BF16)

    tile = pl.BlockSpec((tm, D), lambda t: (t, 0))
    return pl.pallas_call(
        body, name="wgrad_mix", grid=(nt,),
        in_specs=[tile] * 8,
        out_specs=[pl.BlockSpec((NDEV, 3, rb, D), lambda t: (0, 0, 0, 0), pipeline_mode=pl.Buffered(1)),
                   pl.BlockSpec((NDEV, 4, 32, PG), lambda t: (0, 0, 0, 0), pipeline_mode=pl.Buffered(1))],
        out_shape=[jax.ShapeDtypeStruct((NDEV, 3, rb, D), BF16), jax.ShapeDtypeStruct((NDEV, 4, 32, PG), BF16)],
        scratch_shapes=[pltpu.VMEM((3, D, D), F32), pltpu.VMEM((4, PG, PG), F32)],
        compiler_params=_params(("arbitrary",), 56),
    )(s, dyc, q, dyp, merged, dh1, m, dm2)


def _wgrad_gu(v, dfg, dfu):
    tp = v.shape[0]
    tm = _pick(tp, TM_WG)
    nt = tp // tm

    def body(v_ref, dg_ref, du_ref, o_ref, acc):
        k, t = pl.program_id(0), pl.program_id(2)

        @pl.when(t == 0)
        def _():
            acc[...] = jnp.zeros_like(acc)

        @pl.when(k == 0)
        def _():
            acc[...] += _dot_tn(dg_ref[...], v_ref[...])

        @pl.when(k == 1)
        def _():
            acc[...] += _dot_tn(du_ref[...], v_ref[...])

        @pl.when(t == nt - 1)
        def _():
            for d in range(4):
                o_ref[d] = acc[FFB * d:FFB * (d + 1), :].astype(BF16)

    return pl.pallas_call(
        body, name="wgrad_gu", grid=(2, 2, nt),
        in_specs=[pl.BlockSpec((tm, D), lambda k, h, t: (t, 0)),
                  pl.BlockSpec((tm, FFC), lambda k, h, t: (t * (1 - k), h * (1 - k))),
                  pl.BlockSpec((tm, FFC), lambda k, h, t: (t * k, h * k))],
        out_specs=pl.BlockSpec((4, None, FFB, D), lambda k, h, t: (h, k, 0, 0), pipeline_mode=pl.Buffered(1)),
        out_shape=jax.ShapeDtypeStruct((NDEV, 2, FFB, D), BF16),
        scratch_shapes=[pltpu.VMEM((FFC, D), F32)],
        compiler_params=_params(("arbitrary",) * 3, 48),
    )(v, dfg, dfu)


def _wgrad_down(f, dh2):
    tp = f.shape[0]
    tm = _pick(tp, TM_WG)
    nt = tp // tm

    def body(f_ref, d_ref, o_ref, acc):
        t = pl.program_id(1)

        @pl.when(t == 0)
        def _():
            acc[...] = jnp.zeros_like(acc)

        acc[...] += _dot_tn(f_ref[...], d_ref[...].astype(BF16))

        @pl.when(t == nt - 1)
        def _():
            for d in range(4):
                o_ref[d] = acc[FFB * d:FFB * (d + 1), :].astype(BF16)

    return pl.pallas_call(
        body, name="wgrad_down", grid=(2, nt),
        in_specs=[pl.BlockSpec((tm, FFC), lambda h, t: (t, h)), pl.BlockSpec((tm, D), lambda h, t: (t, 0))],
        out_specs=pl.BlockSpec((4, FFB, D), lambda h, t: (h, 0, 0), pipeline_mode=pl.Buffered(1)),
        out_shape=jax.ShapeDtypeStruct((NDEV, FFB, D), BF16),
        scratch_shapes=[pltpu.VMEM((FFC, D), F32)],
        compiler_params=_params(("arbitrary", "arbitrary"), 48),
    )(f, dh2)


def kernel(x, meta_tokens, g_mix, w_in, b_gate, w_dw, b_dw, ln_g, ln_b, w_conv_out, w_pool, pool_scale, w_pool_out, w_o, g_ffn, w_ffn_gate, w_ffn_up, w_ffn_down, g_final, loss_target, m_meta_tokens, m_g_mix, m_w_in, m_b_gate, m_w_dw, m_b_dw, m_ln_g, m_ln_b, m_w_conv_out, m_w_pool, m_pool_scale, m_w_pool_out, m_w_o, m_g_ffn, m_w_ffn_gate, m_w_ffn_up, m_w_ffn_down, m_g_final, v_meta_tokens, v_g_mix, v_w_in, v_b_gate, v_w_dw, v_b_dw, v_ln_g, v_ln_b, v_w_conv_out, v_w_pool, v_pool_scale, v_w_pool_out, v_w_o, v_g_ffn, v_w_ffn_gate, v_w_ffn_up, v_w_ffn_down, v_g_final):
    seq = x.shape[1]
    tp = -(-(seq + 2 * HALO) // TM) * TM
    tm_in = _pick(tp, TM_IO)
    nx_last = seq - (tp // tm_in - 1) * tm_in
    assert 0 < nx_last <= tm_in - 2 * HALO and nx_last % 8 == 0 and 0 < seq - (tp // TM - 1) * TM

    whole = (Ellipsis,)
    (g_small,) = _all_gather(
        [((48, D // NDEV), [(meta_tokens, pl.ds(0, N_META), whole), (w_dw, pl.ds(N_META, CONV_K), 0)])], [F32])
    ag_mix = _Gather([((3, D // NDEV, D), [(w_conv_out, 0, 0), (w_pool_out, 1, 0), (w_o, 2, 0)]),
                      ((4, PG // NDEV, PG), [(w_pool, whole, 0)])], [BF16, BF16])
    def tr(a):
        return jnp.swapaxes(a, 1, 2)

    ag_gu = _Gather([((2, FFB, D), [(tr(w_ffn_gate), 0, 0), (tr(w_ffn_up), 1, 0)])], [BF16])
    ag_dn = _Gather([((FFB, D), [(w_ffn_down, whole, 0)])], [BF16])
    small_full = g_small.transpose(1, 0, 2).reshape(48, D)
    wdw_full = small_full[N_META:]
    tail = jnp.concatenate([jnp.zeros((tm_in - nx_last - N_META, D), F32), small_full[:N_META]], axis=0)

    mx, my = lax.axis_index("x"), lax.axis_index("y")
    order = jnp.stack([2 * mx + my, 2 * mx + 1 - my, 2 * (1 - mx) + my, 2 * (1 - mx) + 1 - my]).astype(jnp.int32)
    (h0, z, u, g_in), (g_mixw, g_pool) = _fwd_in(x[0], tail, g_mix, w_in, order, tp, ag_mix)
    (ac, m), (w_gu,) = _seq_fwd(z, wdw_full, b_dw, seq, ag_gu)
    (h1, s, yc, yp, merged, q), (g_down,) = _mix_fwd(ac, m, z, h0, b_gate, ln_g, ln_b, pool_scale, g_mixw, g_pool, ag_dn)
    w_dn = g_down.reshape(2, FFC, D)
    fg, fu, v, f, dh2, head_acc = _ffn_fwd(h1, loss_target[0], g_ffn, g_final.reshape(1, D), w_gu, w_dn)

    dfg, dfu, dh1, ffn_acc = _ffn_bwd(dh2, fg, fu, h1, g_ffn, w_gu, w_dn)
    own_f, sib_f, q_f = _rs_pair("rs_pair_ffn", [_wgrad_gu(v, dfg, dfu), _wgrad_down(f, dh2)])
    (dac, dm, dzg, dyc, dyp, dm2, mix_acc), rel_f = _mix_bwd(
        dh1, z, yc, yp, ac, m, b_gate, ln_g, ln_b, pool_scale, g_mixw, g_pool, q_f)
    own_m, sib_m, q_m = _rs_pair("rs_pair_mix", list(_wgrad_mix(s, dyc, q, dyp, merged, dh1, m, dm2)))
    (dz, seq_acc), rel_m = _seq_bwd(dac, dm, dzg, z, wdw_full, seq, q_m)
    own_i, sib_i, q_i = _rs_pair("rs_pair_in", [_wgrad_in(u, dz)])
    (grad_x, g_meta, in_acc), rel_i = _in_bwd(dz, h0, dh1, g_mix, g_in, seq, q_i)
    small_g = jnp.concatenate([g_meta, seq_acc[:CONV_K], jnp.zeros((1, D), F32)], axis=0)
    p_small = small_g.reshape(48, NDEV, D // NDEV).transpose(1, 0, 2).astype(BF16)
    rep_g = jnp.concatenate([
        in_acc[0:1], mix_acc[0:1, :D], mix_acc[0:1, D:], seq_acc[CONV_K:CONV_K + 1], mix_acc[1:2, :D], mix_acc[1:2, D:],
        mix_acc[2:3, :D], ffn_acc[0:1], head_acc[1:2], head_acc[0:1], jnp.zeros((REP_ROWS - 10, D), F32)], axis=0)
    own_s, sib_s, rel_s, rep_all = _reduce_scatter([p_small], rep_g)
    owns = [own_i[0], own_s[0], own_m[0], own_m[1], own_f[0], own_f[1]]
    sibs = [sib_i[0], sib_s[0], sib_m[0], sib_m[1], sib_f[0], sib_f[1]]
    rels = [rel_i[0], rel_s[0], rel_m[0], rel_m[1], rel_f[0], rel_f[1]]

    def lead(a):
        return a.reshape(1, *a.shape)

    def stack4(a, lead_dims):
        return a.reshape(*lead_dims, 1, 4 * 32, PG)

    (r_in,) = _adamw_multi("adamw_in", lead(owns[0]), sibs[0][:, None], rels[0][:, None], [w_in], [m_w_in], [v_w_in], 4)
    r_meta, r_dw = _adamw_meta_dw(owns[1], sibs[1], rels[1], (meta_tokens, m_meta_tokens, v_meta_tokens),
                                  (w_dw, m_w_dw, v_w_dw))
    r_conv, r_pout, r_o = _adamw_multi("adamw_mix", owns[2], sibs[2], rels[2], [w_conv_out, w_pool_out, w_o],
                                       [m_w_conv_out, m_w_pool_out, m_w_o], [v_w_conv_out, v_w_pool_out, v_w_o], 1)
    (r_pool,) = _adamw_multi("adamw_pool", stack4(owns[3], ()), stack4(sibs[3], (4,)), stack4(rels[3], (3,)),
                             [w_pool.reshape(1, 128, PG)], [m_w_pool.reshape(1, 128, PG)], [v_w_pool.reshape(1, 128, PG)], 1)
    r_pool = tuple(a.reshape(w_pool.shape) for a in r_pool)
    r_gate, r_up = _adamw_multi("adamw_gu", owns[4], sibs[4], rels[4], [tr(w_ffn_gate), tr(w_ffn_up)],
                                [tr(m_w_ffn_gate), tr(m_w_ffn_up)], [tr(v_w_ffn_gate), tr(v_w_ffn_up)], 2)
    r_gate, r_up = tuple(tr(a) for a in r_gate), tuple(tr(a) for a in r_up)
    (r_down,) = _adamw_multi("adamw_down", lead(owns[5]), sibs[5][:, None], rels[5][:, None],
                             [w_ffn_down], [m_w_ffn_down], [v_w_ffn_down], 2)
    row = (1, D)
    loss, reps = _adamw_rep(
        rep_all,
        [g_mix, b_gate, b_dw, ln_g, ln_b, pool_scale, g_ffn, g_final.reshape(row)],
        [m_g_mix, m_b_gate, m_b_dw, m_ln_g, m_ln_b, m_pool_scale, m_g_ffn, m_g_final.reshape(row)],
        [v_g_mix, v_b_gate, v_b_dw, v_ln_g, v_ln_b, v_pool_scale, v_g_ffn, v_g_final.reshape(row)])
    r_gmix, r_bg, r_bdw, r_lg, r_lb, r_ps, r_gffn, r_gfin = reps
    r_gfin = tuple(a.reshape(D) for a in r_gfin)

    in_order = [r_meta, r_gmix, r_in, r_bg, r_dw, r_bdw, r_lg, r_lb, r_conv, r_pool, r_ps, r_pout, r_o, r_gffn,
                r_gate, r_up, r_down, r_gfin]
    return (loss.reshape(()), grad_x[None], *[r[0] for r in in_order], *[r[1] for r in in_order],
            *[r[2] for r in in_order], *[r[3] for r in in_order])
```

```python
import math

import jax
import jax.numpy as jnp
from jax import lax
from jax.experimental import pallas as pl
from jax.experimental.pallas import tpu as pltpu

F32, BF16 = jnp.float32, jnp.bfloat16
MESH_ID = pl.DeviceIdType.MESH
NDEV = 8

D = 1024
N_META = 16
CONV_K = 31
HALO = 16
POOL_WINDOWS = (2, 4, 8, 16)
PG = 256
DIN = 5 * D
DFF = 2816
FFB = DFF // NDEV
FFC = DFF // 2
INB = DIN // NDEV
RMS_EPS = 1e-6
LN_EPS = 1e-5
ADAM_LR, ADAM_B1, ADAM_B2, ADAM_EPS, ADAM_WD, ADAM_STEP = 0.001, 0.9, 0.999, 1e-08, 0.01, 10

TM = 384
TMS = 192
TM_IO = 704
TM_WG = 1408
TM_WM = 704
MIB = 2 ** 20


def _sig(x):
    return 0.5 * jnp.tanh(0.5 * x) + 0.5


def _dot(a, b):
    return jnp.dot(a, b, preferred_element_type=F32)


def _dot_nt(a, b):
    return lax.dot_general(a, b, (((1,), (1,)), ((), ())), preferred_element_type=F32)


def _dot_tn(a, b):
    return lax.dot_general(a, b, (((0,), (0,)), ((), ())), preferred_element_type=F32)


def _pick(tp, pref):
    return pref if tp % pref == 0 else TM


def _params(sem, vmem_mib):
    return pltpu.CompilerParams(dimension_semantics=sem, vmem_limit_bytes=vmem_mib * MIB)


def _load_once(first, pairs, sems):
    @pl.when(first)
    def _():
        cps = [pltpu.make_async_copy(s, d, sems.at[k]) for k, (s, d) in enumerate(pairs)]
        for cp in cps:
            cp.start()
        for cp in cps:
            cp.wait()


def _place():
    x, y, c = lax.axis_index("x"), lax.axis_index("y"), lax.axis_index("c")
    return x, y, c


class _Gather:
    def __init__(self, groups, dtypes):
        self.groups, self.dtypes, self.n = groups, dtypes, len(groups)
        self.arrays = [a for _, parts in groups for a, _, _ in parts]
        self.out_shape = [jax.ShapeDtypeStruct((NDEV, *s), dt) for (s, _), dt in zip(groups, dtypes)]
        self.scratch = [pltpu.VMEM(s, dt) for (s, _), dt in zip(groups, dtypes)] + [
            pltpu.SemaphoreType.DMA((7 * self.n,)), pltpu.SemaphoreType.DMA((7 * self.n,)),
            pltpu.SemaphoreType.DMA((self.n,))]

    def bind(self, ins, outs, scratch):
        self.ins, self.outs, self.stages = ins, outs, scratch[:self.n]
        self.send_sems, self.recv_sems, self.local_sems = scratch[self.n:]
        return self

    def _copy(self, w, k, block, to, src=None):
        dst = self.outs[w].at[4 * block[0] + 2 * block[1] + block[2]]
        return pltpu.make_async_remote_copy(
            src_ref=dst if src is None else src, dst_ref=dst,
            send_sem=self.send_sems.at[7 * w + k], recv_sem=self.recv_sems.at[7 * w + k],
            device_id=to, device_id_type=MESH_ID)

    def _first(self):
        x, y, c = _place()
        me, sibling = (x, y, c), (x, y, 1 - c)
        chips = [(1 - x, y), (x, 1 - y), (1 - x, 1 - y)]
        mine, first = [], []
        for w in range(self.n):
            mine.append(pltpu.make_async_copy(self.stages[w], self.outs[w].at[4 * x + 2 * y + c], self.local_sems.at[w]))
            first.append(self._copy(w, 0, me, sibling, src=self.stages[w]))
            first += [self._copy(w, 1 + j, me, (*chip, c), src=self.stages[w]) for j, chip in enumerate(chips)]
        return mine, first

    def _passed(self):
        x, y, c = _place()
        chips = [(1 - x, y), (x, 1 - y), (1 - x, 1 - y)]
        return [self._copy(w, 4 + j, (*chip, c), (x, y, 1 - c)) for w in range(self.n) for j, chip in enumerate(chips)]

    def issue(self):
        a = 0
        for w in range(self.n):
            shape, parts = self.groups[w]
            if sum(arr.size for arr, _, _ in parts) < math.prod(shape):
                self.stages[w][...] = jnp.zeros(shape, self.dtypes[w])
            for _, dst, src in parts:
                self.stages[w][dst] = self.ins[a][src].astype(self.dtypes[w])
                a += 1
        mine, first = self._first()
        for cp in mine + first:
            cp.start()

    def forward(self):
        x, y, c = _place()
        chips = [(1 - x, y), (x, 1 - y), (1 - x, 1 - y)]
        passed = self._passed()
        for w in range(self.n):
            for j, chip in enumerate(chips):
                self._copy(w, 1 + j, (*chip, c), (x, y, c)).wait_recv()
                passed[3 * w + j].start()

    def finish(self):
        x, y, c = _place()
        chips = [(1 - x, y), (x, 1 - y), (1 - x, 1 - y)]
        for w in range(self.n):
            self._copy(w, 0, (x, y, 1 - c), (x, y, c)).wait_recv()
            for j, chip in enumerate(chips):
                self._copy(w, 4 + j, (*chip, 1 - c), (x, y, c)).wait_recv()
        mine, first = self._first()
        for cp in first + self._passed():
            cp.wait_send()
        for cp in mine:
            cp.wait()


class _ChipExchange:
    def __init__(self, qs):
        self.n = len(qs)
        self.out_shape = [jax.ShapeDtypeStruct(q.shape, q.dtype) for q in qs]
        self.scratch = [pltpu.SemaphoreType.DMA((3 * self.n,)), pltpu.SemaphoreType.DMA((3 * self.n,))]

    def bind(self, qs, rels, scratch):
        self.qs, self.rels = qs, rels
        self.send_sems, self.recv_sems = scratch
        return self

    def _copies(self):
        x, y, c = _place()
        chips = [(1 - x, y), (x, 1 - y), (1 - x, 1 - y)]
        return [pltpu.make_async_remote_copy(
            src_ref=self.qs[w].at[j], dst_ref=self.rels[w].at[j],
            send_sem=self.send_sems.at[3 * w + j], recv_sem=self.recv_sems.at[3 * w + j],
            device_id=(*chips[j], c), device_id_type=MESH_ID) for w in range(self.n) for j in range(3)]

    def issue(self):
        for cp in self._copies():
            cp.start()

    def finish(self):
        cps = self._copies()
        for cp in cps:
            cp.wait_recv()
        for cp in cps:
            cp.wait_send()


def _reduce_scatter(parts, small):
    n = len(parts)
    blks = [p.shape[1:] for p in parts]

    def body(*refs):
        ps, small_ref = refs[:n], refs[n]
        o = n + 1
        owns, sibs, rels, small_out = refs[o:o + n], refs[o + n:o + 2 * n], refs[o + 2 * n:o + 3 * n], refs[o + 3 * n]
        o += 3 * n + 1
        pa, pb, qst = refs[o:o + n], refs[o + n:o + 2 * n], refs[o + 2 * n:o + 3 * n]
        s1_send, s1_recv, s2_send, s2_recv, sm_send, sm_recv, lsem = refs[o + 3 * n:]
        x, y, c = _place()
        me = 4 * x + 2 * y + c
        sibling = (x, y, 1 - c)
        chips = [(1 - x, y), (x, 1 - y), (1 - x, 1 - y)]
        all_chips = [(x, y)] + chips

        own_cps = []
        for w in range(n):
            cp = pltpu.make_async_copy(ps[w].at[me], owns[w], lsem.at[w])
            cp.start()
            own_cps.append(cp)
        sm_own = pltpu.make_async_copy(small_ref, small_out.at[me], lsem.at[n])
        sm_own.start()

        def small_copy(r):
            peer = ((x + (r >> 2)) % 2, (y + ((r >> 1) & 1)) % 2, (c + (r & 1)) % 2)
            return pltpu.make_async_remote_copy(
                src_ref=small_ref, dst_ref=small_out.at[me], send_sem=sm_send.at[r - 1], recv_sem=sm_recv.at[r - 1],
                device_id=peer, device_id_type=MESH_ID)

        sm_cps = [small_copy(r) for r in range(1, NDEV)]
        for cp in sm_cps:
            cp.start()

        def pair_copy(w, rel):
            cx, cy = all_chips[rel]
            return pltpu.make_async_remote_copy(
                src_ref=ps[w].at[4 * cx + 2 * cy + (1 - c)], dst_ref=sibs[w].at[rel],
                send_sem=s1_send.at[4 * w + rel], recv_sem=s1_recv.at[4 * w + rel],
                device_id=sibling, device_id_type=MESH_ID)

        def chip_copy(w, j):
            return pltpu.make_async_remote_copy(
                src_ref=qst[w].at[j], dst_ref=rels[w].at[j],
                send_sem=s2_send.at[3 * w + j], recv_sem=s2_recv.at[3 * w + j],
                device_id=(*chips[j], c), device_id_type=MESH_ID)

        pair_cps = [pair_copy(w, rel) for w in range(n) for rel in (1, 2, 3, 0)]
        for cp in pair_cps:
            cp.start()
        chip_cps = []
        for w in range(n):
            for j, (cx, cy) in enumerate(chips):
                pair_copy(w, 1 + j).wait_recv()
                la = pltpu.make_async_copy(ps[w].at[4 * cx + 2 * cy + c], pa[w], lsem.at[n + 1])
                lb = pltpu.make_async_copy(sibs[w].at[1 + j], pb[w], lsem.at[n + 2])
                la.start()
                lb.start()
                la.wait()
                lb.wait()
                qst[w][j] = (pa[w][...].astype(F32) + pb[w][...].astype(F32)).astype(BF16)
                cp = chip_copy(w, j)
                cp.start()
                chip_cps.append(cp)
        for w in range(n):
            pair_copy(w, 0).wait_recv()
            for j in range(3):
                chip_copy(w, j).wait_recv()
        for cp in sm_cps:
            cp.wait_recv()
        for cp in pair_cps + chip_cps + sm_cps:
            cp.wait_send()
        for cp in own_cps:
            cp.wait()
        sm_own.wait()

    any_spec = pl.BlockSpec(memory_space=pl.ANY)
    outs = pl.pallas_call(
        body, name="rs_grads",
        out_shape=[jax.ShapeDtypeStruct(b, BF16) for b in blks]
        + [jax.ShapeDtypeStruct((4, *b), BF16) for b in blks]
        + [jax.ShapeDtypeStruct((3, *b), BF16) for b in blks]
        + [jax.ShapeDtypeStruct((NDEV, *small.shape), F32)],
        in_specs=[any_spec] * (n + 1),
        out_specs=[any_spec] * (3 * n + 1),
        scratch_shapes=[pltpu.VMEM(b, BF16) for b in blks] + [pltpu.VMEM(b, BF16) for b in blks]
        + [pltpu.VMEM((3, *b), BF16) for b in blks]
        + [pltpu.SemaphoreType.DMA((4 * n,)), pltpu.SemaphoreType.DMA((4 * n,)),
           pltpu.SemaphoreType.DMA((3 * n,)), pltpu.SemaphoreType.DMA((3 * n,)),
           pltpu.SemaphoreType.DMA((NDEV - 1,)), pltpu.SemaphoreType.DMA((NDEV - 1,)),
           pltpu.SemaphoreType.DMA((n + 3,))],
        compiler_params=pltpu.CompilerParams(vmem_limit_bytes=40 * MIB),
    )(*parts, small)
    return outs[:n], outs[n:2 * n], outs[2 * n:3 * n], outs[3 * n]


def _rs_pair(name, parts):
    n = len(parts)
    blks = [p.shape[1:] for p in parts]

    def body(*refs):
        ps = refs[:n]
        owns, sibs, qs = refs[n:2 * n], refs[2 * n:3 * n], refs[3 * n:4 * n]
        pa, pb, qst = refs[4 * n:5 * n], refs[5 * n:6 * n], refs[6 * n:7 * n]
        s_send, s_recv, lsem = refs[7 * n:]
        x, y, c = _place()
        chips = [(1 - x, y), (x, 1 - y), (1 - x, 1 - y)]
        all_chips = [(x, y)] + chips

        own_cps = [pltpu.make_async_copy(ps[w].at[4 * x + 2 * y + c], owns[w], lsem.at[w]) for w in range(n)]
        for cp in own_cps:
            cp.start()

        def pair_copy(w, rel):
            cx, cy = all_chips[rel]
            return pltpu.make_async_remote_copy(
                src_ref=ps[w].at[4 * cx + 2 * cy + (1 - c)], dst_ref=sibs[w].at[rel],
                send_sem=s_send.at[4 * w + rel], recv_sem=s_recv.at[4 * w + rel],
                device_id=(x, y, 1 - c), device_id_type=MESH_ID)

        pair_cps = [pair_copy(w, rel) for w in range(n) for rel in (1, 2, 3, 0)]
        for cp in pair_cps:
            cp.start()
        q_cps = []
        for w in range(n):
            for j, (cx, cy) in enumerate(chips):
                la = pltpu.make_async_copy(ps[w].at[4 * cx + 2 * cy + c], pa[w], lsem.at[n])
                lb = pltpu.make_async_copy(sibs[w].at[1 + j], pb[w], lsem.at[n + 1])
                la.start()
                pair_copy(w, 1 + j).wait_recv()
                lb.start()
                la.wait()
                lb.wait()
                qst[w][j] = (pa[w][...].astype(F32) + pb[w][...].astype(F32)).astype(BF16)
            cp = pltpu.make_async_copy(qst[w], qs[w], lsem.at[n + 2 + w])
            cp.start()
            q_cps.append(cp)
        for w in range(n):
            pair_copy(w, 0).wait_recv()
        for cp in pair_cps:
            cp.wait_send()
        for cp in own_cps + q_cps:
            cp.wait()

    any_spec = pl.BlockSpec(memory_space=pl.ANY)
    outs = pl.pallas_call(
        body, name=name,
        out_shape=[jax.ShapeDtypeStruct(b, BF16) for b in blks]
        + [jax.ShapeDtypeStruct((4, *b), BF16) for b in blks]
        + [jax.ShapeDtypeStruct((3, *b), BF16) for b in blks],
        in_specs=[any_spec] * n,
        out_specs=[any_spec] * (3 * n),
        scratch_shapes=[pltpu.VMEM(b, BF16) for b in blks] + [pltpu.VMEM(b, BF16) for b in blks]
        + [pltpu.VMEM((3, *b), BF16) for b in blks]
        + [pltpu.SemaphoreType.DMA((4 * n,)), pltpu.SemaphoreType.DMA((4 * n,)), pltpu.SemaphoreType.DMA((2 * n + 2,))],
        compiler_params=pltpu.CompilerParams(vmem_limit_bytes=40 * MIB),
    )(*parts)
    return outs[:n], outs[n:2 * n], outs[2 * n:3 * n]


def _adamw_math(g, w, m, v):
    m = ADAM_B1 * m + (1.0 - ADAM_B1) * g
    v = ADAM_B2 * v + (1.0 - ADAM_B2) * (g * g)
    m_hat = m / (1.0 - ADAM_B1 ** ADAM_STEP)
    v_hat = v / (1.0 - ADAM_B2 ** ADAM_STEP)
    delta = -ADAM_LR * (m_hat / (jnp.sqrt(v_hat) + ADAM_EPS) + ADAM_WD * w)
    return delta, m, v


def _adamw_multi(name, own, sib, rel, ws, ms, vs, row_grid):
    k_n, r_n, c_n = own.shape
    rbk = r_n // row_grid

    def body(*refs):
        own_ref, sib_ref, r0_ref, r1_ref, r2_ref = refs[:5]
        w_refs, m_refs, v_refs = refs[5:5 + k_n], refs[5 + k_n:5 + 2 * k_n], refs[5 + 2 * k_n:5 + 3 * k_n]
        outs = refs[5 + 3 * k_n:]
        for k in range(k_n):
            g = own_ref[k].astype(F32) + sib_ref[k].astype(F32)
            g = g + r0_ref[k].astype(F32)
            g = g + r1_ref[k].astype(F32)
            g = g + r2_ref[k].astype(F32)
            delta, mm, vv = _adamw_math(g, w_refs[k][0], m_refs[k][0], v_refs[k][0])
            outs[4 * k][0] = g
            outs[4 * k + 1][0] = delta
            outs[4 * k + 2][0] = mm
            outs[4 * k + 3][0] = vv

    def lead(j):
        return pl.BlockSpec((None, k_n, rbk, c_n), lambda g: (j, 0, g, 0))

    wspec = pl.BlockSpec((1, rbk, c_n), lambda g: (0, g, 0))
    shp = jax.ShapeDtypeStruct((1, r_n, c_n), F32)
    res = pl.pallas_call(
        body, name=name, grid=(row_grid,),
        in_specs=[pl.BlockSpec((k_n, rbk, c_n), lambda g: (0, g, 0)), lead(0), lead(0), lead(1), lead(2)] + [wspec] * (3 * k_n),
        out_specs=[wspec] * (4 * k_n), out_shape=[shp] * (4 * k_n),
        compiler_params=_params(("arbitrary",), 40),
    )(own, sib, rel, rel, rel, *ws, *ms, *vs)
    return [tuple(res[4 * k:4 * k + 4]) for k in range(k_n)]


def _adamw_meta_dw(own, sib, rel, meta, dw):
    def body(own_ref, sib_ref, rel_ref, wm, mm, vm, wd, md, vd, *outs):
        def gsum(rows):
            g = own_ref[rows, :].astype(F32) + sib_ref[0, rows, :].astype(F32)
            for j in range(3):
                g = g + rel_ref[j, rows, :].astype(F32)
            return g

        g = gsum(pl.ds(0, N_META))
        delta, m2, v2 = _adamw_math(g, wm[...], mm[...], vm[...])
        for o, val in zip(outs[:4], (g, delta, m2, v2)):
            o[...] = val
        g = gsum(pl.ds(N_META, CONV_K))
        delta, m2, v2 = _adamw_math(g, wd[0], md[0], vd[0])
        for o, val in zip(outs[4:], (g, delta, m2, v2)):
            o[0] = val

    s_meta = jax.ShapeDtypeStruct(meta[0].shape, F32)
    s_dw = jax.ShapeDtypeStruct(dw[0].shape, F32)
    res = pl.pallas_call(body, name="adamw_meta_dw", out_shape=[s_meta] * 4 + [s_dw] * 4)(own, sib, rel, *meta, *dw)
    return tuple(res[:4]), tuple(res[4:])


REP_ROWS = 16


def _adamw_rep(gathered, ws, ms, vs):
    rows = [(0, 1), (1, 2), (3, 1), (4, 1), (5, 1), (6, 1), (7, 1), (8, 1)]

    def body(g_ref, *refs):
        w_refs, m_refs, v_refs = refs[:8], refs[8:16], refs[16:24]
        loss_ref, outs, acc = refs[24], refs[25:57], refs[57]
        g = g_ref[0]
        for d in range(1, NDEV):
            g = g + g_ref[d]
        acc[...] = g
        loss_ref[...] = (0.5 / D) * jnp.sum(acc[pl.ds(9, 1), :], axis=1, keepdims=True)
        for p, (r0, nr) in enumerate(rows):
            for h in range(nr):
                cols = pl.ds(h * D, D)
                gp = acc[pl.ds(r0 + h, 1), :]
                delta, mm, vv = _adamw_math(gp, w_refs[p][:, cols], m_refs[p][:, cols], v_refs[p][:, cols])
                for o, val in zip(outs[4 * p:4 * p + 4], (gp, delta, mm, vv)):
                    o[:, cols] = val

    shapes = [jax.ShapeDtypeStruct(w.shape, F32) for w in ws]
    res = pl.pallas_call(
        body, name="adamw_rep",
        out_shape=[jax.ShapeDtypeStruct((1, 1), F32)] + [s for s in shapes for _ in range(4)],
        scratch_shapes=[pltpu.VMEM((REP_ROWS, D), F32)],
    )(gathered, *ws, *ms, *vs)
    return res[0], [tuple(res[1 + 4 * p:5 + 4 * p]) for p in range(8)]


def _load_ffn(i, j, wgu_hbm, wgu, wdn_hbm, wdn, sems):
    half = NDEV // 2

    def copies(ch):
        pairs = [(wgu_hbm.at[half * ch + d, g], wgu.at[g, ch, pl.ds(FFB * d, FFB), :]) for g in range(2) for d in range(half)]
        pairs.append((wdn_hbm.at[ch], wdn.at[ch]))
        return [pltpu.make_async_copy(s, t, sems.at[(2 * half + 1) * ch + k]) for k, (s, t) in enumerate(pairs)]

    @pl.when((i == 0) & (j == 0))
    def _():
        for cp in copies(0) + copies(1):
            cp.start()

    for ch in range(2):
        @pl.when((i == 0) & (j == ch))
        def _():
            for cp in copies(ch):
                cp.wait()


def _win_pairs(w_hbm, w_vm):
    return [(w_hbm.at[q], w_vm.at[q // 2, :, pl.ds(2 * INB * (q % 2), 2 * INB)]) for q in range(4)]


def _whole(a):
    nd = a.ndim
    return pl.BlockSpec(a.shape, lambda *g: (0,) * nd)


CHIPW = 2 * INB
PHASE_CHIP = (1, 0, 2)


class _GatherIn:
    scratch = [pltpu.VMEM((D, INB), BF16), pltpu.SemaphoreType.DMA((7,)), pltpu.SemaphoreType.DMA((7,)),
               pltpu.SemaphoreType.DMA((1,))]

    def bind(self, w_ref, w_vm, scratch):
        self.w_ref, self.w_vm = w_ref, w_vm
        self.stage, self.send_sems, self.recv_sems, self.local_sem = scratch
        return self

    def _win(self, chip, core):
        return self.w_vm.at[2 * chip[0] + chip[1], :, pl.ds(INB * core, INB)]

    def _copy(self, k, chip, core, to, src=None):
        dst = self._win(chip, core)
        return pltpu.make_async_remote_copy(
            src_ref=dst if src is None else src, dst_ref=dst, send_sem=self.send_sems.at[k],
            recv_sem=self.recv_sems.at[k], device_id=to, device_id_type=MESH_ID)

    def _mine(self, cs):
        x, y, _ = _place()
        return pltpu.make_async_copy(self.stage, self._win((x, y), cs), self.local_sem.at[0])

    def issue(self, cs):
        x, y, _ = _place()
        chips = [(1 - x, y), (x, 1 - y), (1 - x, 1 - y)]
        self.stage[...] = self.w_ref[0].astype(BF16)
        self._mine(cs).start()
        self._copy(0, (x, y), cs, (x, y, 1 - cs), src=self.stage).start()
        for j, chip in enumerate(chips):
            self._copy(1 + j, (x, y), cs, (*chip, cs), src=self.stage).start()

    def wait_chip(self, phase, cs):
        x, y, _ = _place()
        chips = [(1 - x, y), (x, 1 - y), (1 - x, 1 - y)]
        if phase == 0:
            self._mine(cs).wait()
            self._copy(0, (x, y), 1 - cs, (x, y, cs)).wait_recv()
            return
        if phase == 1:
            for j in PHASE_CHIP:
                self._copy(1 + j, chips[j], cs, (x, y, cs)).wait_recv()
                self._copy(4 + j, chips[j], cs, (x, y, 1 - cs)).start()
        j = PHASE_CHIP[phase - 1]
        self._copy(4 + j, chips[j], 1 - cs, (x, y, cs)).wait_recv()

    def finish(self, cs):
        x, y, _ = _place()
        for k in range(7):
            self._copy(k, (x, y), cs, (x, y, cs), src=self.stage).wait_send()


def _fwd_in(x2, g_mix, w_in, order, tp, ag, ags):
    tm = _pick(tp, TM_IO)
    nt = tp // tm
    nx_last = x2.shape[0] - (nt - 1) * tm
    na, ng, ns = len(ag.arrays), ag.n, len(ags.arrays)
    gin = _GatherIn()

    def body(order_ref, *refs):
        x_ref, g_ref, w_ref = refs[:3]
        o = 3 + na + ns
        h_ref, z_ref, u_ref, wout_ref = refs[o:o + 4]
        s = o + 4 + ng + 1
        w_vm, u_all, osem, sm_vm = refs[s:s + 4]
        gin.bind(w_ref, w_vm, refs[s + 4:s + 8])
        ag.bind(refs[3:3 + na], refs[o + 4:o + 4 + ng], refs[s + 8:s + 8 + len(ag.scratch)])
        ags.bind(refs[3 + na:3 + na + ns], refs[o + 4 + ng:o + 5 + ng], refs[s + 8 + len(ag.scratch):])
        ph, i = pl.program_id(0), pl.program_id(1)
        core = lax.axis_index("c")
        first = (ph == 0) & (i == 0)
        last = (ph == 3) & (i == nt - 1)
        for cs in range(2):
            @pl.when(first & (core == cs))
            def _():
                gin.issue(cs)

        @pl.when(first)
        def _():
            ags.issue()
            ag.issue()

        @pl.when((ph == 0) & (i == max(nt - 2, 0)))
        def _():
            ags.forward()

        for cs in range(2):
            for p in range(4):
                @pl.when((ph == p) & (i == 0) & (core == cs))
                def _():
                    gin.wait_chip(p, cs)

        @pl.when((ph == 2) & (i == 0))
        def _():
            ag.forward()

        out_copy = pltpu.make_async_copy(w_vm, wout_ref, osem.at[0])

        @pl.when((ph == 3) & (i == 0))
        def _():
            out_copy.start()

        @pl.when((ph == 0) & (i < nt - 1))
        def _():
            h_ref[...] = x_ref[...]

        @pl.when((ph == 0) & (i == nt - 1))
        def _():
            ags.finish()
            cp = pltpu.make_async_copy(ags.outs[0], sm_vm, osem.at[1])
            cp.start()
            h_ref[pl.ds(0, nx_last), :] = x_ref[pl.ds(0, nx_last), :]
            h_ref[pl.ds(nx_last, tm - nx_last - N_META), :] = jnp.zeros((tm - nx_last - N_META, D), F32)
            cp.wait()
            for d in range(NDEV):
                h_ref[pl.ds(tm - N_META, N_META), pl.ds(128 * d, 128)] = sm_vm[d, pl.ds(0, N_META), :]

        @pl.when(ph == 0)
        def _():
            xv = h_ref[...]
            r = lax.rsqrt(jnp.mean(xv * xv, axis=-1, keepdims=True) + RMS_EPS)
            u = (xv * r * g_ref[...]).astype(BF16)
            u_ref[...] = u
            u_all[i] = u

        z_ref[...] = _dot(u_all[i], w_vm[order_ref[ph]])

        @pl.when(last)
        def _():
            ag.finish()
            out_copy.wait()

        for cs in range(2):
            @pl.when(last & (core == cs))
            def _():
                gin.finish(cs)

    def rows(ph, i, order):
        return (jnp.where(ph == 0, i, nt - 1), 0)

    tile = pl.BlockSpec((tm, D), rows)
    anys = pl.BlockSpec(memory_space=pl.ANY)
    res = pl.pallas_call(
        body, name="fwd_in",
        grid_spec=pltpu.PrefetchScalarGridSpec(
            num_scalar_prefetch=1, grid=(4, nt),
            in_specs=[tile, pl.BlockSpec((1, D), lambda ph, i, order: (0, 0)), _whole(w_in)]
            + [_whole(a) for a in ag.arrays + ags.arrays],
            out_specs=[tile, pl.BlockSpec((tm, CHIPW), lambda ph, i, order: (i, order[ph])), tile, anys] + [anys] * (ng + 1),
            scratch_shapes=[pltpu.VMEM((4, D, CHIPW), BF16), pltpu.VMEM((nt, tm, D), BF16), pltpu.SemaphoreType.DMA((2,)),
                            pltpu.VMEM(ags.out_shape[0].shape, F32)] + gin.scratch + ag.scratch + ags.scratch),
        out_shape=[jax.ShapeDtypeStruct((tp, D), F32), jax.ShapeDtypeStruct((tp, DIN), F32),
                   jax.ShapeDtypeStruct((tp, D), BF16), jax.ShapeDtypeStruct((4, D, CHIPW), BF16)]
        + ag.out_shape + ags.out_shape,
        compiler_params=_params(("arbitrary", "arbitrary"), 58),
    )(order, x2, g_mix, w_in, *ag.arrays, *ags.arrays)
    return res[:4], res[4:4 + ng], res[4 + ng]


def _halo_specs(col, nt, width=D):
    r = TM // HALO
    nb = nt * r
    return [pl.BlockSpec((HALO, width), lambda i: ((i * r + nb - 1) % nb, col)),
            pl.BlockSpec((TM, width), lambda i: (i, col)),
            pl.BlockSpec((HALO, width), lambda i: (((i + 1) * r) % nb, col))]


NCB = D // 128
TME = TM + 2 * HALO


def _tm_fill(dst, time0, groups, tile_fn):
    def body(g, c):
        for j in range(NCB):
            dst[pl.ds((time0 + 8 * g) * NCB + j, 8, stride=NCB), :] = tile_fn(pl.multiple_of(8 * g, 8), pl.ds(128 * j, 128))
        return c

    lax.fori_loop(0, groups, body, 0)


def _tm_fill_ext(dst, left, cur, right, fn):
    _tm_fill(dst, 0, HALO // 8, lambda r, l: fn(left, pl.ds(r, 8), l))
    _tm_fill(dst, HALO, TM // 8, lambda r, l: fn(cur, pl.ds(r, 8), l))
    _tm_fill(dst, HALO + TM, HALO // 8, lambda r, l: fn(right, pl.ds(r, 8), l))


def _tm_read(src, groups, store_fn):
    def body(g, c):
        for j in range(NCB):
            store_fn(pl.ds(pl.multiple_of(8 * g, 8), 8), pl.ds(128 * j, 128), src[pl.ds(8 * g * NCB + j, 8, stride=NCB), :])
        return c

    lax.fori_loop(0, groups, body, 0)


def _tm_rows(t):
    return pl.ds(t * NCB if isinstance(t, int) else pl.multiple_of(t * NCB, NCB), NCB)


def _tm_at(ref, t):
    return ref[_tm_rows(t), :]


def _by_group(sub, vals):
    return jnp.where(sub < 2, vals[0], jnp.where(sub < 4, vals[1], jnp.where(sub < 6, vals[2], vals[3])))


def _pool_cnt(b, seq, tp, sub):
    b = jnp.where(b < 0, b + tp, b)
    b = jnp.where(b >= tp, b - tp, b)
    t = jnp.where(b < seq, b + N_META, b - (tp - N_META))
    cnts = []
    for win in POOL_WINDOWS:
        left = win // 2
        lo = jnp.maximum(t - left, 0)
        hi = jnp.minimum(t + win - left, seq + N_META)
        cnts.append(jnp.maximum(hi - lo, 1).astype(F32))
    return _by_group(sub, cnts)


def _edge_rows(seq, tp):
    reach = max(POOL_WINDOWS) // 2
    return [tp - N_META + t for t in range(reach)] + [seq - reach + 1 + t for t in range(reach - 1)]


def _edge_gain(b, seq, tp, sub):
    return _by_group(sub, [float(w) for w in POOL_WINDOWS]) / _pool_cnt(b, seq, tp, sub)


def _nested_windows(at, lo_offs):
    sums, s, have = [], None, set()
    for g, win in enumerate(POOL_WINDOWS):
        for o in range(lo_offs[g], lo_offs[g] + win):
            if o not in have:
                have.add(o)
                s = at(o) if s is None else s + at(o)
        sums.append(s)
    return sums


def _seq_fwd(z, w_dw, b_dw, seq, gat):
    tp = z.shape[0]
    nt = tp // TM
    na, ng = len(gat.arrays), gat.n

    def body(*refs):
        av_l, av, av_r, ag_l, ag, ag_r, p_l, p, p_r, w_ref, b_ref = refs[:11]
        ac_ref, m_ref = refs[11 + na:13 + na]
        a3, p3, o3, m3, w3, b3, m2d = refs[13 + na + ng:20 + na + ng]
        gat.bind(refs[11:11 + na], refs[13 + na:13 + na + ng], refs[20 + na + ng:])
        i = pl.program_id(0)
        sub = lax.broadcasted_iota(jnp.int32, (NCB, 128), 0)

        @pl.when(i == 0)
        def _():
            gat.issue()
            _tm_fill(w3, 0, 4, lambda r, l: w_ref[pl.ds(r, 8), l])
            for j in range(NCB):
                b3[pl.ds(j, 1), :] = b_ref[:, pl.ds(128 * j, 128)]

        @pl.when(i == max(nt - 2, 0))
        def _():
            gat.forward()

        _tm_fill_ext(a3, (av_l, ag_l), (av, ag), (av_r, ag_r), lambda vg, r, l: vg[0][r, l] * _sig(vg[1][r, l]))
        _tm_fill_ext(p3, p_l, p, p_r, lambda ref, r, l: ref[r, l])

        def conv(g, c):
            for t in range(8):
                acc = b3[...]
                for k in range(CONV_K):
                    acc = acc + _tm_at(w3, k) * _tm_at(a3, 8 * g + t + k + 1)
                o3[_tm_rows(8 * g + t), :] = acc
            return c

        lax.fori_loop(0, TM // 8, conv, 0)
        _tm_read(o3, TM // 8, lambda r, l, tile: ac_ref.__setitem__((r, l), tile))

        inv = _by_group(sub, [1.0 / w for w in POOL_WINDOWS])

        def pool(g, c):
            for t in range(8):
                e = 8 * g + t + HALO
                sums = _nested_windows(lambda o: _tm_at(p3, e + o), [-(w // 2) for w in POOL_WINDOWS])
                m3[_tm_rows(8 * g + t), :] = _by_group(sub, sums) * inv - _tm_at(p3, e)
            return c

        lax.fori_loop(0, TM // 8, pool, 0)
        for b in _edge_rows(seq, tp):
            r = b - i * TM

            @pl.when((r >= 0) & (r < TM))
            def _():
                pv = _tm_at(p3, r + HALO)
                m3[_tm_rows(r), :] = (_tm_at(m3, r) + pv) * _edge_gain(b, seq, tp, sub) - pv

        _tm_read(m3, TM // 8, lambda r, l, tile: m2d.__setitem__((r, l), tile))
        m_ref[...] = m2d[...].astype(BF16)

        @pl.when(i == nt - 1)
        def _():
            gat.finish()

    tmaj = pltpu.VMEM((TM * NCB, 128), F32)
    text = pltpu.VMEM((TME * NCB, 128), F32)
    res = pl.pallas_call(
        body, name="seq_fwd", grid=(nt,),
        in_specs=_halo_specs(0, nt) + _halo_specs(1, nt) + _halo_specs(2, nt)
        + [pl.BlockSpec((32, D), lambda i: (0, 0)), pl.BlockSpec((1, D), lambda i: (0, 0))] + [_whole(a) for a in gat.arrays],
        out_specs=[pl.BlockSpec((TM, D), lambda i: (i, 0))] * 2 + [pl.BlockSpec(memory_space=pl.ANY)] * ng,
        out_shape=[jax.ShapeDtypeStruct((tp, D), F32), jax.ShapeDtypeStruct((tp, D), BF16)] + gat.out_shape,
        scratch_shapes=[text, text, tmaj, tmaj, pltpu.VMEM((32 * NCB, 128), F32), pltpu.VMEM((NCB, 128), F32),
                        pltpu.VMEM((TM, D), F32)] + gat.scratch,
        compiler_params=_params(("arbitrary",), 52),
    )(z, z, z, z, z, z, z, z, z, w_dw, b_dw, *gat.arrays)
    return res[:2], res[2:]


def _ln_stats(ac):
    mu = jnp.mean(ac, axis=-1, keepdims=True)
    xc = ac - mu
    rl = lax.rsqrt(jnp.mean(xc * xc, axis=-1, keepdims=True) + LN_EPS)
    return xc * rl, rl


def _pool_mix(m, wp_ref):
    return jnp.concatenate(
        [_dot(m[:, g * PG:(g + 1) * PG], wp_ref[:, g].reshape(PG, PG)) for g in range(4)], axis=1)


def _mix_fwd(ac, m, z, h0, b_gate, ln_g, ln_b, pool_scale, g_mixw, g_pool, gat):
    tp = h0.shape[0]
    nt = tp // TMS
    na, ng = len(gat.arrays), gat.n

    def body(*refs):
        ac_ref, m_ref, zga, zgb, h_ref, bg_ref, lg_ref, lb_ref, ps_ref, wm_hbm, wp_hbm = refs[:11]
        h1_ref, s_ref, yc_ref, yp_ref, mg_ref, q_ref = refs[11 + na:17 + na]
        wm, wp, sems = refs[17 + na + ng:20 + na + ng]
        gat.bind(refs[11:11 + na], refs[17 + na:17 + na + ng], refs[20 + na + ng:])
        i = pl.program_id(0)

        @pl.when(i == 0)
        def _():
            gat.issue()

        @pl.when(i == max(nt - 4, 0))
        def _():
            gat.forward()

        @pl.when(i == nt - 1)
        def _():
            gat.finish()

        _load_once(i == 0, [(wm_hbm, wm), (wp_hbm, wp)], sems)
        n, _ = _ln_stats(ac_ref[...])
        l = n * lg_ref[...] + lb_ref[...]
        s = (l * _sig(l)).astype(BF16)
        s_ref[...] = s
        yc = _dot(s, wm[:, 0].reshape(D, D))
        q = (_pool_mix(m_ref[...], wp) * ps_ref[...]).astype(BF16)
        q_ref[...] = q
        yp = _dot(q, wm[:, 1].reshape(D, D))
        ga = _sig(zga[...] + bg_ref[:, :D])
        gb = _sig(zgb[...] + bg_ref[:, D:])
        merged = (ga * yc + gb * yp).astype(BF16)
        yc_ref[...] = yc
        yp_ref[...] = yp
        mg_ref[...] = merged
        h1_ref[...] = h_ref[...] + _dot(merged, wm[:, 2].reshape(D, D))

    def tile(col=0):
        return pl.BlockSpec((TMS, D), lambda i: (i, col))

    def vec(w):
        return pl.BlockSpec((1, w), lambda i: (0, 0))

    anys = pl.BlockSpec(memory_space=pl.ANY)
    f32o, b16o = jax.ShapeDtypeStruct((tp, D), F32), jax.ShapeDtypeStruct((tp, D), BF16)
    res = pl.pallas_call(
        body, name="mix_fwd", grid=(nt,),
        in_specs=[tile(), tile(), tile(3), tile(4), tile(), vec(2 * D), vec(D), vec(D), vec(D), anys, anys]
        + [_whole(a) for a in gat.arrays],
        out_specs=[tile()] * 6 + [anys] * ng,
        out_shape=[f32o, b16o, f32o, f32o, b16o, b16o] + gat.out_shape,
        scratch_shapes=[pltpu.VMEM((NDEV, 3, D // NDEV, D), BF16), pltpu.VMEM((NDEV, 4, PG // NDEV, PG), BF16),
                        pltpu.SemaphoreType.DMA((2,))] + gat.scratch,
        compiler_params=_params(("arbitrary",), 52),
    )(ac, m, z, z, h0, b_gate, ln_g, ln_b, pool_scale, g_mixw, g_pool, *gat.arrays)
    return res[:6], res[6:]


def _ffn_fwd(h1, tgt, g_ffn, g_final, w_gu, w_dn):
    tp = h1.shape[0]
    nt = tp // TM
    nx_last = tgt.shape[0] - (nt - 1) * TM

    def body(h_ref, t_ref, gf_ref, gl_ref, wgu_hbm, wdn_hbm,
             fg_ref, fu_ref, v_ref, f_ref, dh2_ref, acc_ref, wgu, wdn, v_sc, h2_sc, diff_sc, sems):
        i, j = pl.program_id(0), pl.program_id(1)
        _load_ffn(i, j, wgu_hbm, wgu, wdn_hbm, wdn, sems)

        @pl.when((i == 0) & (j == 0))
        def _():
            acc_ref[...] = jnp.zeros_like(acc_ref)

        @pl.when(j == 0)
        def _():
            h = h_ref[...]
            r = lax.rsqrt(jnp.mean(h * h, axis=-1, keepdims=True) + RMS_EPS)
            v = (h * r * gf_ref[...]).astype(BF16)
            v_sc[...] = v
            v_ref[...] = v
            h2_sc[...] = h

        v = v_sc[...]
        fg = _dot_nt(v, wgu[0, j])
        fu = _dot_nt(v, wgu[1, j])
        fg_ref[...] = fg
        fu_ref[...] = fu
        f = ((fg * _sig(fg)) * fu).astype(BF16)
        f_ref[...] = f
        h2_sc[...] += _dot(f, wdn[j])

        @pl.when(j == 1)
        def _():
            h2 = h2_sc[...]
            r = lax.rsqrt(jnp.mean(h2 * h2, axis=-1, keepdims=True) + RMS_EPS)
            n2 = h2 * r
            y = n2 * gl_ref[...]

            @pl.when(i < nt - 1)
            def _():
                diff_sc[...] = y - t_ref[...]

            @pl.when(i == nt - 1)
            def _():
                diff_sc[pl.ds(0, nx_last), :] = y[:nx_last] - t_ref[pl.ds(0, nx_last), :]
                diff_sc[pl.ds(nx_last, TM - nx_last), :] = jnp.zeros((TM - nx_last, D), F32)

            diff = diff_sc[...]
            dy = diff * (1.0 / D)
            acc_ref[0:1, :] += jnp.sum(diff * diff, axis=0, keepdims=True)
            acc_ref[1:2, :] += jnp.sum(dy * n2, axis=0, keepdims=True)
            dn = dy * gl_ref[...]
            dh2_ref[...] = r * (dn - n2 * jnp.mean(dn * n2, axis=-1, keepdims=True))

    def tile():
        return pl.BlockSpec((TM, D), lambda i, j: (i, 0))

    def chunk():
        return pl.BlockSpec((TM, FFC), lambda i, j: (i, j))

    def vec():
        return pl.BlockSpec((1, D), lambda i, j: (0, 0))

    anys = pl.BlockSpec(memory_space=pl.ANY)
    hid32, hid16 = jax.ShapeDtypeStruct((tp, DFF), F32), jax.ShapeDtypeStruct((tp, DFF), BF16)
    return pl.pallas_call(
        body, name="ffn_fwd", grid=(nt, 2),
        in_specs=[tile(), tile(), vec(), vec(), anys, anys],
        out_specs=[chunk(), chunk(), tile(), chunk(), tile(), pl.BlockSpec((8, D), lambda i, j: (0, 0))],
        out_shape=[hid32, hid32, jax.ShapeDtypeStruct((tp, D), BF16), hid16, jax.ShapeDtypeStruct((tp, D), F32),
                   jax.ShapeDtypeStruct((8, D), F32)],
        scratch_shapes=[pltpu.VMEM((2, 2, FFC, D), BF16), pltpu.VMEM((2, FFC, D), BF16),
                        pltpu.VMEM((TM, D), BF16), pltpu.VMEM((TM, D), F32), pltpu.VMEM((TM, D), F32),
                        pltpu.SemaphoreType.DMA((2 * NDEV + 2,))],
        compiler_params=_params(("arbitrary", "arbitrary"), 56),
    )(h1, tgt, g_ffn, g_final, w_gu, w_dn)


def _ffn_bwd(dh2, fg, fu, h1, g_ffn, w_gu, w_dn):
    tp = h1.shape[0]
    nt = tp // TM

    def body(dh2_ref, fg_ref, fu_ref, h_ref, gf_ref, wgu_hbm, wdn_hbm,
             dfg_ref, dfu_ref, dh1_ref, acc_ref, wgu, wdn, d_sc, dv_sc, sems):
        i, j = pl.program_id(0), pl.program_id(1)
        _load_ffn(i, j, wgu_hbm, wgu, wdn_hbm, wdn, sems)

        @pl.when((i == 0) & (j == 0))
        def _():
            acc_ref[...] = jnp.zeros_like(acc_ref)

        @pl.when(j == 0)
        def _():
            d_sc[...] = dh2_ref[...].astype(BF16)
            dv_sc[...] = jnp.zeros_like(dv_sc)

        df = _dot_nt(d_sc[...], wdn[j])
        fg = fg_ref[...]
        sg = _sig(fg)
        dfu = (df * (fg * sg)).astype(BF16)
        dfg = (df * fu_ref[...] * (sg * (1.0 + fg * (1.0 - sg)))).astype(BF16)
        dfg_ref[...] = dfg
        dfu_ref[...] = dfu
        dv_sc[...] += _dot(dfg, wgu[0, j]) + _dot(dfu, wgu[1, j])

        @pl.when(j == 1)
        def _():
            h = h_ref[...]
            r = lax.rsqrt(jnp.mean(h * h, axis=-1, keepdims=True) + RMS_EPS)
            n1 = h * r
            dv = dv_sc[...]
            acc_ref[0:1, :] += jnp.sum(dv * n1, axis=0, keepdims=True)
            dn = dv * gf_ref[...]
            dh1_ref[...] = dh2_ref[...] + r * (dn - n1 * jnp.mean(dn * n1, axis=-1, keepdims=True))

    def tile():
        return pl.BlockSpec((TM, D), lambda i, j: (i, 0))

    def chunk():
        return pl.BlockSpec((TM, FFC), lambda i, j: (i, j))

    anys = pl.BlockSpec(memory_space=pl.ANY)
    hid16 = jax.ShapeDtypeStruct((tp, DFF), BF16)
    return pl.pallas_call(
        body, name="ffn_bwd", grid=(nt, 2),
        in_specs=[tile(), chunk(), chunk(), tile(), pl.BlockSpec((1, D), lambda i, j: (0, 0)), anys, anys],
        out_specs=[chunk(), chunk(), tile(), pl.BlockSpec((8, D), lambda i, j: (0, 0))],
        out_shape=[hid16, hid16, jax.ShapeDtypeStruct((tp, D), F32), jax.ShapeDtypeStruct((8, D), F32)],
        scratch_shapes=[pltpu.VMEM((2, 2, FFC, D), BF16), pltpu.VMEM((2, FFC, D), BF16),
                        pltpu.VMEM((TM, D), BF16), pltpu.VMEM((TM, D), F32), pltpu.SemaphoreType.DMA((2 * NDEV + 2,))],
        compiler_params=_params(("arbitrary", "arbitrary"), 56),
    )(dh2, fg, fu, h1, g_ffn, w_gu, w_dn)


def _mix_bwd(dh1, z, yc, yp, ac, m, b_gate, ln_g, ln_b, pool_scale, g_mixw, g_pool, qs):
    tp = dh1.shape[0]
    nt = tp // TMS
    ex = _ChipExchange(qs)
    nq = ex.n

    def body(*refs):
        dh1_ref, zga, zgb, yc_ref, yp_ref, ac_ref, m_ref, bg_ref, lg_ref, lb_ref, ps_ref, wm_hbm, wp_hbm = refs[:13]
        dac_ref, dm_ref, dzg_ref, dyc_ref, dyp_ref, dm2_ref, acc_ref = refs[13 + nq:20 + nq]
        wm, wp, sems = refs[20 + 2 * nq:23 + 2 * nq]
        ex.bind(refs[13:13 + nq], refs[20 + nq:20 + 2 * nq], refs[23 + 2 * nq:])
        first = pl.program_id(0) == 0

        @pl.when(first)
        def _():
            ex.issue()
            acc_ref[...] = jnp.zeros_like(acc_ref)

        _load_once(first, [(wm_hbm, wm), (wp_hbm, wp)], sems)

        dmerged = _dot_nt(dh1_ref[...].astype(BF16), wm[:, 2].reshape(D, D))
        ga = _sig(zga[...] + bg_ref[:, :D])
        gb = _sig(zgb[...] + bg_ref[:, D:])
        dyc = dmerged * ga
        dyp = dmerged * gb
        dza = (dmerged * yc_ref[...]) * (ga * (1.0 - ga))
        dzb = (dmerged * yp_ref[...]) * (gb * (1.0 - gb))
        dzg_ref[:, :D] = dza.astype(BF16)
        dzg_ref[:, D:] = dzb.astype(BF16)
        acc_ref[0:1, :D] += jnp.sum(dza, axis=0, keepdims=True)
        acc_ref[0:1, D:] += jnp.sum(dzb, axis=0, keepdims=True)
        dyc_b = dyc.astype(BF16)
        dyp_b = dyp.astype(BF16)
        dyc_ref[...] = dyc_b
        dyp_ref[...] = dyp_b
        ds = _dot_nt(dyc_b, wm[:, 0].reshape(D, D))
        n, rl = _ln_stats(ac_ref[...])
        l = n * lg_ref[...] + lb_ref[...]
        sg = _sig(l)
        dl = ds * (sg * (1.0 + l * (1.0 - sg)))
        acc_ref[1:2, :D] += jnp.sum(dl * n, axis=0, keepdims=True)
        acc_ref[1:2, D:] += jnp.sum(dl, axis=0, keepdims=True)
        dn = dl * lg_ref[...]
        dac_ref[...] = rl * (dn - jnp.mean(dn, axis=-1, keepdims=True) - n * jnp.mean(dn * n, axis=-1, keepdims=True))
        dq = _dot_nt(dyp_b, wm[:, 1].reshape(D, D))
        mv = m_ref[...]
        acc_ref[2:3, :D] += jnp.sum(dq * _pool_mix(mv, wp), axis=0, keepdims=True)
        dm2 = (dq * ps_ref[...]).astype(BF16)
        dm2_ref[...] = dm2
        dm_ref[...] = jnp.concatenate(
            [_dot_nt(dm2[:, g * PG:(g + 1) * PG], wp[:, g].reshape(PG, PG)) for g in range(4)], axis=1)

        @pl.when(pl.program_id(0) == nt - 1)
        def _():
            ex.finish()

    def tile(col=0):
        return pl.BlockSpec((TMS, D), lambda i: (i, col))

    def vec(w):
        return pl.BlockSpec((1, w), lambda i: (0, 0))

    anys = pl.BlockSpec(memory_space=pl.ANY)
    f32o, b16o = jax.ShapeDtypeStruct((tp, D), F32), jax.ShapeDtypeStruct((tp, D), BF16)
    res = pl.pallas_call(
        body, name="mix_bwd", grid=(nt,),
        in_specs=[tile(), tile(3), tile(4), tile(), tile(), tile(), tile(), vec(2 * D), vec(D), vec(D), vec(D), anys, anys]
        + [anys] * nq,
        out_specs=[tile(), tile(), pl.BlockSpec((TMS, 2 * D), lambda i: (i, 0)), tile(), tile(), tile(),
                   pl.BlockSpec((8, 2 * D), lambda i: (0, 0))] + [anys] * nq,
        out_shape=[f32o, f32o, jax.ShapeDtypeStruct((tp, 2 * D), BF16), b16o, b16o, b16o,
                   jax.ShapeDtypeStruct((8, 2 * D), F32)] + ex.out_shape,
        scratch_shapes=[pltpu.VMEM((NDEV, 3, D // NDEV, D), BF16), pltpu.VMEM((NDEV, 4, PG // NDEV, PG), BF16),
                        pltpu.SemaphoreType.DMA((2,))] + ex.scratch,
        compiler_params=_params(("arbitrary",), 48),
    )(dh1, z, z, yc, yp, ac, m, b_gate, ln_g, ln_b, pool_scale, g_mixw, g_pool, *qs)
    return res[:7], res[7:]


def _seq_bwd(dac, dm, dzg, z, w_dw, seq, qs):
    tp = z.shape[0]
    nt = tp // TM
    ex = _ChipExchange(qs)
    nq = ex.n

    def body(*refs):
        dac_l, dac_c, dac_r, dm_l, dm_c, dm_r, av_l, av, av_r, ag_l, ag, ag_r, dzg_ref, w_ref = refs[:14]
        dz_ref, acc_ref = refs[14 + nq:16 + nq]
        a3, d3, m3, da3, dp3, w3, dw3, da_sc, dp_sc = refs[16 + 2 * nq:25 + 2 * nq]
        ex.bind(refs[14:14 + nq], refs[16 + nq:16 + 2 * nq], refs[25 + 2 * nq:])
        i = pl.program_id(0)
        sub = lax.broadcasted_iota(jnp.int32, (NCB, 128), 0)

        @pl.when(i == 0)
        def _():
            ex.issue()
            dw3[...] = jnp.zeros_like(dw3)
            _tm_fill(w3, 0, 4, lambda r, l: w_ref[pl.ds(r, 8), l])

        _tm_fill_ext(a3, (av_l, ag_l), (av, ag), (av_r, ag_r), lambda vg, r, l: vg[0][r, l] * _sig(vg[1][r, l]))
        _tm_fill_ext(d3, dac_l, dac_c, dac_r, lambda ref, r, l: ref[r, l])
        _tm_fill_ext(m3, dm_l, dm_c, dm_r, lambda ref, r, l: ref[r, l])

        def conv(g, c):
            dcur = [_tm_at(d3, 8 * g + t + HALO) for t in range(8)]
            accs = [None] * 8
            for k in range(CONV_K):
                wk = _tm_at(w3, k)
                s = None
                for t in range(8):
                    term = wk * _tm_at(d3, 8 * g + t + CONV_K - k)
                    accs[t] = term if accs[t] is None else accs[t] + term
                    pr = dcur[t] * _tm_at(a3, 8 * g + t + k + 1)
                    s = pr if s is None else s + pr
                dw3[_tm_rows(k), :] += s
            s = dcur[0]
            for t in range(1, 8):
                s = s + dcur[t]
            dw3[_tm_rows(CONV_K), :] += s
            for t in range(8):
                da3[_tm_rows(8 * g + t), :] = accs[t]
            return c

        lax.fori_loop(0, TM // 8, conv, 0)

        for b in _edge_rows(seq, tp):
            e = lax.rem(b - i * TM + HALO + tp, tp)

            @pl.when(e < TME)
            def _():
                m3[_tm_rows(e), :] = _tm_at(m3, e) * _edge_gain(b, seq, tp, sub)

        inv = _by_group(sub, [1.0 / w for w in POOL_WINDOWS])

        def pool(g, c):
            for t in range(8):
                e = 8 * g + t + HALO
                sums = _nested_windows(lambda o: _tm_at(m3, e + o), [w // 2 + 1 - w for w in POOL_WINDOWS])
                dp3[_tm_rows(8 * g + t), :] = _by_group(sub, sums) * inv
            return c

        lax.fori_loop(0, TM // 8, pool, 0)

        _tm_read(da3, TM // 8, lambda r, l, tile: da_sc.__setitem__((r, l), tile))
        _tm_read(dp3, TM // 8, lambda r, l, tile: dp_sc.__setitem__((r, l), tile))
        sg = _sig(ag[...])
        da = da_sc[...]
        dz_ref[:, 0:D] = (da * sg).astype(BF16)
        dz_ref[:, D:2 * D] = (da * av[...] * (sg * (1.0 - sg))).astype(BF16)
        dz_ref[:, 2 * D:3 * D] = (dp_sc[...] - dm_c[...]).astype(BF16)
        dz_ref[:, 3 * D:] = dzg_ref[...]

        @pl.when(i == nt - 1)
        def _():
            _tm_read(dw3, 4, lambda r, l, tile: acc_ref.__setitem__((r, l), tile))
            ex.finish()

    tmaj = pltpu.VMEM((TM * NCB, 128), F32)
    text = pltpu.VMEM((TME * NCB, 128), F32)
    taps = pltpu.VMEM((32 * NCB, 128), F32)
    anys = pl.BlockSpec(memory_space=pl.ANY)
    res = pl.pallas_call(
        body, name="seq_bwd", grid=(nt,),
        in_specs=_halo_specs(0, nt) + _halo_specs(0, nt) + _halo_specs(0, nt) + _halo_specs(1, nt)
        + [pl.BlockSpec((TM, 2 * D), lambda i: (i, 0)), pl.BlockSpec((32, D), lambda i: (0, 0))] + [anys] * nq,
        out_specs=[pl.BlockSpec((TM, DIN), lambda i: (i, 0)), pl.BlockSpec((32, D), lambda i: (0, 0))] + [anys] * nq,
        out_shape=[jax.ShapeDtypeStruct((tp, DIN), BF16), jax.ShapeDtypeStruct((32, D), F32)] + ex.out_shape,
        scratch_shapes=[text, text, text, tmaj, tmaj, taps, taps, pltpu.VMEM((TM, D), F32), pltpu.VMEM((TM, D), F32)]
        + ex.scratch,
        compiler_params=_params(("arbitrary",), 48),
    )(dac, dac, dac, dm, dm, dm, z, z, z, z, z, z, dzg, w_dw, *qs)
    return res[:2], res[2:]


def _in_bwd(dz, h0, dh1, g_mix, w_g, seq, qs):
    tp = h0.shape[0]
    tm = _pick(tp, TM_IO)
    nt = tp // tm
    ex = _ChipExchange(qs)
    nq = ex.n

    def body(*refs):
        dz_ref, h_ref, dh1_ref, g_ref, w_hbm = refs[:5]
        gx_ref, gmeta_ref, acc_ref = refs[5 + nq:8 + nq]
        w_vm, sems = refs[8 + 2 * nq:10 + 2 * nq]
        ex.bind(refs[5:5 + nq], refs[8 + nq:8 + 2 * nq], refs[10 + 2 * nq:])
        i = pl.program_id(0)

        @pl.when(i == 0)
        def _():
            ex.issue()
            acc_ref[...] = jnp.zeros_like(acc_ref)

        _load_once(i == 0, _win_pairs(w_hbm, w_vm), sems)

        du = _dot_nt(dz_ref[:, :DIN // 2], w_vm[0]) + _dot_nt(dz_ref[:, DIN // 2:], w_vm[1])
        h = h_ref[...]
        r = lax.rsqrt(jnp.mean(h * h, axis=-1, keepdims=True) + RMS_EPS)
        n0 = h * r
        acc_ref[0:1, :] += jnp.sum(du * n0, axis=0, keepdims=True)
        dn = du * g_ref[...]
        gx_ref[...] = dh1_ref[...] + r * (dn - n0 * jnp.mean(dn * n0, axis=-1, keepdims=True))

        @pl.when(i == nt - 1)
        def _():
            gmeta_ref[...] = gx_ref[pl.ds(tm - N_META, N_META), :]
            ex.finish()

    tile = pl.BlockSpec((tm, D), lambda i: (i, 0))
    anys = pl.BlockSpec(memory_space=pl.ANY)
    res = pl.pallas_call(
        body, name="in_bwd", grid=(nt,),
        in_specs=[pl.BlockSpec((tm, DIN), lambda i: (i, 0)), tile, tile, pl.BlockSpec((1, D), lambda i: (0, 0)), anys]
        + [anys] * nq,
        out_specs=[tile, pl.BlockSpec((N_META, D), lambda i: (0, 0)), pl.BlockSpec((8, D), lambda i: (0, 0))] + [anys] * nq,
        out_shape=[jax.ShapeDtypeStruct((seq, D), F32), jax.ShapeDtypeStruct((N_META, D), F32),
                   jax.ShapeDtypeStruct((8, D), F32)] + ex.out_shape,
        scratch_shapes=[pltpu.VMEM((2, D, DIN // 2), BF16), pltpu.SemaphoreType.DMA((NDEV,))] + ex.scratch,
        compiler_params=_params(("arbitrary",), 58),
    )(dz, h0, dh1, g_mix, w_g, *qs)
    return res[:3], res[3:]


def _wgrad_in(u, dz):
    tp = u.shape[0]
    tm = _pick(tp, TM_WG)
    nt = tp // tm
    half = DIN // 2

    def body(u_ref, dz_ref, o_ref, acc):
        t = pl.program_id(1)

        @pl.when(t == 0)
        def _():
            acc[...] = jnp.zeros_like(acc)

        acc[...] += _dot_tn(u_ref[...], dz_ref[...])

        @pl.when(t == nt - 1)
        def _():
            for d in range(4):
                o_ref[d] = acc[:, INB * d:INB * (d + 1)].astype(BF16)

    return pl.pallas_call(
        body, name="wgrad_in", grid=(2, nt),
        in_specs=[pl.BlockSpec((tm, D), lambda h, t: (t, 0)), pl.BlockSpec((tm, half), lambda h, t: (t, h))],
        out_specs=pl.BlockSpec((4, D, INB), lambda h, t: (h, 0, 0), pipeline_mode=pl.Buffered(1)),
        out_shape=jax.ShapeDtypeStruct((NDEV, D, INB), BF16),
        scratch_shapes=[pltpu.VMEM((D, half), F32)],
        compiler_params=_params(("arbitrary", "arbitrary"), 52),
    )(u, dz)


def _wgrad_mix(s, dyc, q, dyp, merged, dh1, m, dm2):
    tp = s.shape[0]
    tm = _pick(tp, TM_WM)
    nt = tp // tm
    rb = D // NDEV

    def body(s_ref, dyc_ref, q_ref, dyp_ref, mg_ref, dh1_ref, m_ref, dm2_ref, o_ref, op_ref, acc, accp):
        t = pl.program_id(0)

        @pl.when(t == 0)
        def _():
            acc[...] = jnp.zeros_like(acc)
            accp[...] = jnp.zeros_like(accp)

        acc[0] += _dot_tn(s_ref[...], dyc_ref[...])
        acc[1] += _dot_tn(q_ref[...], dyp_ref[...])
        acc[2] += _dot_tn(mg_ref[...], dh1_ref[...].astype(BF16))
        for g in range(4):
            accp[g] += _dot_tn(m_ref[:, g * PG:(g + 1) * PG], dm2_ref[:, g * PG:(g + 1) * PG])

        @pl.when(t == nt - 1)
        def _():
            for d in range(NDEV):
                for k in range(3):
                    o_ref[d, k] = acc[k, rb * d:rb * (d + 1), :].astype(BF16)
                for g in range(4):
                    op_ref[d, g] = accp[g, 32 * d:32 * (d + 1), :].astype(BF16)

    tile = pl.BlockSpec((tm, D), lambda t: (t, 0))
    return pl.pallas_call(
        body, name="wgrad_mix", grid=(nt,),
        in_specs=[tile] * 8,
        out_specs=[pl.BlockSpec((NDEV, 3, rb, D), lambda t: (0, 0, 0, 0), pipeline_mode=pl.Buffered(1)),
                   pl.BlockSpec((NDEV, 4, 32, PG), lambda t: (0, 0, 0, 0), pipeline_mode=pl.Buffered(1))],
        out_shape=[jax.ShapeDtypeStruct((NDEV, 3, rb, D), BF16), jax.ShapeDtypeStruct((NDEV, 4, 32, PG), BF16)],
        scratch_shapes=[pltpu.VMEM((3, D, D), F32), pltpu.VMEM((4, PG, PG), F32)],
        compiler_params=_params(("arbitrary",), 56),
    )(s, dyc, q, dyp, merged, dh1, m, dm2)


def _wgrad_gu(v, dfg, dfu):
    tp = v.shape[0]
    tm = _pick(tp, TM_WG)
    nt = tp // tm

    def body(v_ref, dg_ref, du_ref, o_ref, acc):
        k, t = pl.program_id(0), pl.program_id(2)

        @pl.when(t == 0)
        def _():
            acc[...] = jnp.zeros_like(acc)

        @pl.when(k == 0)
        def _():
            acc[...] += _dot_tn(dg_ref[...], v_ref[...])

        @pl.when(k == 1)
        def _():
            acc[...] += _dot_tn(du_ref[...], v_ref[...])

        @pl.when(t == nt - 1)
        def _():
            for d in range(4):
                o_ref[d] = acc[FFB * d:FFB * (d + 1), :].astype(BF16)

    return pl.pallas_call(
        body, name="wgrad_gu", grid=(2, 2, nt),
        in_specs=[pl.BlockSpec((tm, D), lambda k, h, t: (t, 0)),
                  pl.BlockSpec((tm, FFC), lambda k, h, t: (t * (1 - k), h * (1 - k))),
                  pl.BlockSpec((tm, FFC), lambda k, h, t: (t * k, h * k))],
        out_specs=pl.BlockSpec((4, None, FFB, D), lambda k, h, t: (h, k, 0, 0), pipeline_mode=pl.Buffered(1)),
        out_shape=jax.ShapeDtypeStruct((NDEV, 2, FFB, D), BF16),
        scratch_shapes=[pltpu.VMEM((FFC, D), F32)],
        compiler_params=_params(("arbitrary",) * 3, 48),
    )(v, dfg, dfu)


def _wgrad_down(f, dh2):
    tp = f.shape[0]
    tm = _pick(tp, TM_WG)
    nt = tp // tm

    def body(f_ref, d_ref, o_ref, acc):
        t = pl.program_id(1)

        @pl.when(t == 0)
        def _():
            acc[...] = jnp.zeros_like(acc)

        acc[...] += _dot_tn(f_ref[...], d_ref[...].astype(BF16))

        @pl.when(t == nt - 1)
        def _():
            for d in range(4):
                o_ref[d] = acc[FFB * d:FFB * (d + 1), :].astype(BF16)

    return pl.pallas_call(
        body, name="wgrad_down", grid=(2, nt),
        in_specs=[pl.BlockSpec((tm, FFC), lambda h, t: (t, h)), pl.BlockSpec((tm, D), lambda h, t: (t, 0))],
        out_specs=pl.BlockSpec((4, FFB, D), lambda h, t: (h, 0, 0), pipeline_mode=pl.Buffered(1)),
        out_shape=jax.ShapeDtypeStruct((NDEV, FFB, D), BF16),
        scratch_shapes=[pltpu.VMEM((FFC, D), F32)],
        compiler_params=_params(("arbitrary", "arbitrary"), 48),
    )(f, dh2)


def kernel(x, meta_tokens, g_mix, w_in, b_gate, w_dw, b_dw, ln_g, ln_b, w_conv_out, w_pool, pool_scale, w_pool_out, w_o, g_ffn, w_ffn_gate, w_ffn_up, w_ffn_down, g_final, loss_target, m_meta_tokens, m_g_mix, m_w_in, m_b_gate, m_w_dw, m_b_dw, m_ln_g, m_ln_b, m_w_conv_out, m_w_pool, m_pool_scale, m_w_pool_out, m_w_o, m_g_ffn, m_w_ffn_gate, m_w_ffn_up, m_w_ffn_down, m_g_final, v_meta_tokens, v_g_mix, v_w_in, v_b_gate, v_w_dw, v_b_dw, v_ln_g, v_ln_b, v_w_conv_out, v_w_pool, v_pool_scale, v_w_pool_out, v_w_o, v_g_ffn, v_w_ffn_gate, v_w_ffn_up, v_w_ffn_down, v_g_final):
    seq = x.shape[1]
    tp = -(-(seq + 2 * HALO) // TM) * TM
    tm_in = _pick(tp, TM_IO)
    nx_last = seq - (tp // tm_in - 1) * tm_in
    assert 0 < nx_last <= tm_in - 2 * HALO and nx_last % 8 == 0 and 0 < seq - (tp // TM - 1) * TM

    whole = (Ellipsis,)
    ag_small = _Gather(
        [((48, D // NDEV), [(meta_tokens, pl.ds(0, N_META), whole), (w_dw, pl.ds(N_META, CONV_K), 0)])], [F32])
    ag_mix = _Gather([((3, D // NDEV, D), [(w_conv_out, 0, 0), (w_pool_out, 1, 0), (w_o, 2, 0)]),
                      ((4, PG // NDEV, PG), [(w_pool, whole, 0)])], [BF16, BF16])
    def tr(a):
        return jnp.swapaxes(a, 1, 2)

    ag_gu = _Gather([((2, FFB, D), [(tr(w_ffn_gate), 0, 0), (tr(w_ffn_up), 1, 0)])], [BF16])
    ag_dn = _Gather([((FFB, D), [(w_ffn_down, whole, 0)])], [BF16])

    mx, my = lax.axis_index("x"), lax.axis_index("y")
    order = jnp.stack([2 * mx + my, 2 * mx + 1 - my, 2 * (1 - mx) + my, 2 * (1 - mx) + 1 - my]).astype(jnp.int32)
    (h0, z, u, g_in), (g_mixw, g_pool), g_small = _fwd_in(x[0], g_mix, w_in, order, tp, ag_mix, ag_small)
    wdw_full = g_small.transpose(1, 0, 2).reshape(48, D)[N_META:]
    (ac, m), (w_gu,) = _seq_fwd(z, wdw_full, b_dw, seq, ag_gu)
    (h1, s, yc, yp, merged, q), (g_down,) = _mix_fwd(ac, m, z, h0, b_gate, ln_g, ln_b, pool_scale, g_mixw, g_pool, ag_dn)
    w_dn = g_down.reshape(2, FFC, D)
    fg, fu, v, f, dh2, head_acc = _ffn_fwd(h1, loss_target[0], g_ffn, g_final.reshape(1, D), w_gu, w_dn)

    dfg, dfu, dh1, ffn_acc = _ffn_bwd(dh2, fg, fu, h1, g_ffn, w_gu, w_dn)
    own_f, sib_f, q_f = _rs_pair("rs_pair_ffn", [_wgrad_gu(v, dfg, dfu), _wgrad_down(f, dh2)])
    (dac, dm, dzg, dyc, dyp, dm2, mix_acc), rel_f = _mix_bwd(
        dh1, z, yc, yp, ac, m, b_gate, ln_g, ln_b, pool_scale, g_mixw, g_pool, q_f)
    own_m, sib_m, q_m = _rs_pair("rs_pair_mix", list(_wgrad_mix(s, dyc, q, dyp, merged, dh1, m, dm2)))
    (dz, seq_acc), rel_m = _seq_bwd(dac, dm, dzg, z, wdw_full, seq, q_m)
    own_i, sib_i, q_i = _rs_pair("rs_pair_in", [_wgrad_in(u, dz)])
    (grad_x, g_meta, in_acc), rel_i = _in_bwd(dz, h0, dh1, g_mix, g_in, seq, q_i)
    small_g = jnp.concatenate([g_meta, seq_acc[:CONV_K], jnp.zeros((1, D), F32)], axis=0)
    p_small = small_g.reshape(48, NDEV, D // NDEV).transpose(1, 0, 2).astype(BF16)
    rep_g = jnp.concatenate([
        in_acc[0:1], mix_acc[0:1, :D], mix_acc[0:1, D:], seq_acc[CONV_K:CONV_K + 1], mix_acc[1:2, :D], mix_acc[1:2, D:],
        mix_acc[2:3, :D], ffn_acc[0:1], head_acc[1:2], head_acc[0:1], jnp.zeros((REP_ROWS - 10, D), F32)], axis=0)
    own_s, sib_s, rel_s, rep_all = _reduce_scatter([p_small], rep_g)
    owns = [own_i[0], own_s[0], own_m[0], own_m[1], own_f[0], own_f[1]]
    sibs = [sib_i[0], sib_s[0], sib_m[0], sib_m[1], sib_f[0], sib_f[1]]
    rels = [rel_i[0], rel_s[0], rel_m[0], rel_m[1], rel_f[0], rel_f[1]]

    def lead(a):
        return a.reshape(1, *a.shape)

    def stack4(a, lead_dims):
        return a.reshape(*lead_dims, 1, 4 * 32, PG)

    (r_in,) = _adamw_multi("adamw_in", lead(owns[0]), sibs[0][:, None], rels[0][:, None], [w_in], [m_w_in], [v_w_in], 4)
    r_meta, r_dw = _adamw_meta_dw(owns[1], sibs[1], rels[1], (meta_tokens, m_meta_tokens, v_meta_tokens),
                                  (w_dw, m_w_dw, v_w_dw))
    r_conv, r_pout, r_o = _adamw_multi("adamw_mix", owns[2], sibs[2], rels[2], [w_conv_out, w_pool_out, w_o],
                                       [m_w_conv_out, m_w_pool_out, m_w_o], [v_w_conv_out, v_w_pool_out, v_w_o], 1)
    (r_pool,) = _adamw_multi("adamw_pool", stack4(owns[3], ()), stack4(sibs[3], (4,)), stack4(rels[3], (3,)),
                             [w_pool.reshape(1, 128, PG)], [m_w_pool.reshape(1, 128, PG)], [v_w_pool.reshape(1, 128, PG)], 1)
    r_pool = tuple(a.reshape(w_pool.shape) for a in r_pool)
    r_gate, r_up = _adamw_multi("adamw_gu", owns[4], sibs[4], rels[4], [tr(w_ffn_gate), tr(w_ffn_up)],
                                [tr(m_w_ffn_gate), tr(m_w_ffn_up)], [tr(v_w_ffn_gate), tr(v_w_ffn_up)], 2)
    r_gate, r_up = tuple(tr(a) for a in r_gate), tuple(tr(a) for a in r_up)
    (r_down,) = _adamw_multi("adamw_down", lead(owns[5]), sibs[5][:, None], rels[5][:, None],
                             [w_ffn_down], [m_w_ffn_down], [v_w_ffn_down], 2)
    row = (1, D)
    loss, reps = _adamw_rep(
        rep_all,
        [g_mix, b_gate, b_dw, ln_g, ln_b, pool_scale, g_ffn, g_final.reshape(row)],
        [m_g_mix, m_b_gate, m_b_dw, m_ln_g, m_ln_b, m_pool_scale, m_g_ffn, m_g_final.reshape(row)],
        [v_g_mix, v_b_gate, v_b_dw, v_ln_g, v_ln_b, v_pool_scale, v_g_ffn, v_g_final.reshape(row)])
    r_gmix, r_bg, r_bdw, r_lg, r_lb, r_ps, r_gffn, r_gfin = reps
    r_gfin = tuple(a.reshape(D) for a in r_gfin)

    in_order = [r_meta, r_gmix, r_in, r_bg, r_dw, r_bdw, r_lg, r_lb, r_conv, r_pool, r_ps, r_pout, r_o, r_gffn,
                r_gate, r_up, r_down, r_gfin]
    return (loss.reshape(()), grad_x[None], *[r[0] for r in in_order], *[r[1] for r in in_order],
            *[r[2] for r in in_order], *[r[3] for r in in_order])
```

```python
import math

import jax
import jax.numpy as jnp
from jax import lax
from jax.experimental import pallas as pl
from jax.experimental.pallas import tpu as pltpu

F32, BF16 = jnp.float32, jnp.bfloat16
MESH_ID = pl.DeviceIdType.MESH
NDEV = 8

D = 1024
N_META = 16
CONV_K = 31
HALO = 16
POOL_WINDOWS = (2, 4, 8, 16)
PG = 256
DIN = 5 * D
DFF = 2816
FFB = DFF // NDEV
FFC = DFF // 2
INB = DIN // NDEV
RMS_EPS = 1e-6
LN_EPS = 1e-5
ADAM_LR, ADAM_B1, ADAM_B2, ADAM_EPS, ADAM_WD, ADAM_STEP = 0.001, 0.9, 0.999, 1e-08, 0.01, 10

TM = 384
TMS = 192
TM_IO = 704
TM_WG = 1408
TM_WM = 704
MIB = 2 ** 20


def _sig(x):
    return 0.5 * jnp.tanh(0.5 * x) + 0.5


def _dot(a, b):
    return jnp.dot(a, b, preferred_element_type=F32)


def _dot_nt(a, b):
    return lax.dot_general(a, b, (((1,), (1,)), ((), ())), preferred_element_type=F32)


def _dot_tn(a, b):
    return lax.dot_general(a, b, (((0,), (0,)), ((), ())), preferred_element_type=F32)


def _pick(tp, pref):
    return pref if tp % pref == 0 else TM


def _params(sem, vmem_mib):
    return pltpu.CompilerParams(dimension_semantics=sem, vmem_limit_bytes=vmem_mib * MIB)


def _load_once(first, pairs, sems):
    @pl.when(first)
    def _():
        cps = [pltpu.make_async_copy(s, d, sems.at[k]) for k, (s, d) in enumerate(pairs)]
        for cp in cps:
            cp.start()
        for cp in cps:
            cp.wait()


def _place():
    x, y, c = lax.axis_index("x"), lax.axis_index("y"), lax.axis_index("c")
    return x, y, c


class _Gather:
    def __init__(self, groups, dtypes):
        self.groups, self.dtypes, self.n = groups, dtypes, len(groups)
        self.arrays = [a for _, parts in groups for a, _, _ in parts]
        self.out_shape = [jax.ShapeDtypeStruct((NDEV, *s), dt) for (s, _), dt in zip(groups, dtypes)]
        self.scratch = [pltpu.VMEM(s, dt) for (s, _), dt in zip(groups, dtypes)] + [
            pltpu.SemaphoreType.DMA((7 * self.n,)), pltpu.SemaphoreType.DMA((7 * self.n,)),
            pltpu.SemaphoreType.DMA((self.n,))]

    def bind(self, ins, outs, scratch):
        self.ins, self.outs, self.stages = ins, outs, scratch[:self.n]
        self.send_sems, self.recv_sems, self.local_sems = scratch[self.n:]
        return self

    def _copy(self, w, k, block, to, src=None):
        dst = self.outs[w].at[4 * block[0] + 2 * block[1] + block[2]]
        return pltpu.make_async_remote_copy(
            src_ref=dst if src is None else src, dst_ref=dst,
            send_sem=self.send_sems.at[7 * w + k], recv_sem=self.recv_sems.at[7 * w + k],
            device_id=to, device_id_type=MESH_ID)

    def _first(self):
        x, y, c = _place()
        me, sibling = (x, y, c), (x, y, 1 - c)
        chips = [(1 - x, y), (x, 1 - y), (1 - x, 1 - y)]
        mine, first = [], []
        for w in range(self.n):
            mine.append(pltpu.make_async_copy(self.stages[w], self.outs[w].at[4 * x + 2 * y + c], self.local_sems.at[w]))
            first.append(self._copy(w, 0, me, sibling, src=self.stages[w]))
            first += [self._copy(w, 1 + j, me, (*chip, c), src=self.stages[w]) for j, chip in enumerate(chips)]
        return mine, first

    def _passed(self):
        x, y, c = _place()
        chips = [(1 - x, y), (x, 1 - y), (1 - x, 1 - y)]
        return [self._copy(w, 4 + j, (*chip, c), (x, y, 1 - c)) for w in range(self.n) for j, chip in enumerate(chips)]

    def issue(self):
        a = 0
        for w in range(self.n):
            shape, parts = self.groups[w]
            if sum(arr.size for arr, _, _ in parts) < math.prod(shape):
                self.stages[w][...] = jnp.zeros(shape, self.dtypes[w])
            for _, dst, src in parts:
                self.stages[w][dst] = self.ins[a][src].astype(self.dtypes[w])
                a += 1
        mine, first = self._first()
        for cp in mine + first:
            cp.start()

    def forward(self):
        x, y, c = _place()
        chips = [(1 - x, y), (x, 1 - y), (1 - x, 1 - y)]
        passed = self._passed()
        for w in range(self.n):
            for j, chip in enumerate(chips):
                self._copy(w, 1 + j, (*chip, c), (x, y, c)).wait_recv()
                passed[3 * w + j].start()

    def finish(self):
        x, y, c = _place()
        chips = [(1 - x, y), (x, 1 - y), (1 - x, 1 - y)]
        for w in range(self.n):
            self._copy(w, 0, (x, y, 1 - c), (x, y, c)).wait_recv()
            for j, chip in enumerate(chips):
                self._copy(w, 4 + j, (*chip, 1 - c), (x, y, c)).wait_recv()
        mine, first = self._first()
        for cp in first + self._passed():
            cp.wait_send()
        for cp in mine:
            cp.wait()


class _ChipExchange:
    def __init__(self, qs):
        self.n = len(qs)
        self.out_shape = [jax.ShapeDtypeStruct(q.shape, q.dtype) for q in qs]
        self.scratch = [pltpu.SemaphoreType.DMA((3 * self.n,)), pltpu.SemaphoreType.DMA((3 * self.n,))]

    def bind(self, qs, rels, scratch):
        self.qs, self.rels = qs, rels
        self.send_sems, self.recv_sems = scratch
        return self

    def _copies(self):
        x, y, c = _place()
        chips = [(1 - x, y), (x, 1 - y), (1 - x, 1 - y)]
        return [pltpu.make_async_remote_copy(
            src_ref=self.qs[w].at[j], dst_ref=self.rels[w].at[j],
            send_sem=self.send_sems.at[3 * w + j], recv_sem=self.recv_sems.at[3 * w + j],
            device_id=(*chips[j], c), device_id_type=MESH_ID) for w in range(self.n) for j in range(3)]

    def issue(self):
        for cp in self._copies():
            cp.start()

    def finish(self):
        cps = self._copies()
        for cp in cps:
            cp.wait_recv()
        for cp in cps:
            cp.wait_send()


def _reduce_scatter(parts, small):
    n = len(parts)
    blks = [p.shape[1:] for p in parts]

    def body(*refs):
        ps, small_ref = refs[:n], refs[n]
        o = n + 1
        owns, sibs, rels, small_out = refs[o:o + n], refs[o + n:o + 2 * n], refs[o + 2 * n:o + 3 * n], refs[o + 3 * n]
        o += 3 * n + 1
        pa, pb, qst = refs[o:o + n], refs[o + n:o + 2 * n], refs[o + 2 * n:o + 3 * n]
        s1_send, s1_recv, s2_send, s2_recv, sm_send, sm_recv, lsem = refs[o + 3 * n:]
        x, y, c = _place()
        me = 4 * x + 2 * y + c
        sibling = (x, y, 1 - c)
        chips = [(1 - x, y), (x, 1 - y), (1 - x, 1 - y)]
        all_chips = [(x, y)] + chips

        own_cps = []
        for w in range(n):
            cp = pltpu.make_async_copy(ps[w].at[me], owns[w], lsem.at[w])
            cp.start()
            own_cps.append(cp)
        sm_own = pltpu.make_async_copy(small_ref, small_out.at[me], lsem.at[n])
        sm_own.start()

        def small_copy(r):
            peer = ((x + (r >> 2)) % 2, (y + ((r >> 1) & 1)) % 2, (c + (r & 1)) % 2)
            return pltpu.make_async_remote_copy(
                src_ref=small_ref, dst_ref=small_out.at[me], send_sem=sm_send.at[r - 1], recv_sem=sm_recv.at[r - 1],
                device_id=peer, device_id_type=MESH_ID)

        sm_cps = [small_copy(r) for r in range(1, NDEV)]
        for cp in sm_cps:
            cp.start()

        def pair_copy(w, rel):
            cx, cy = all_chips[rel]
            return pltpu.make_async_remote_copy(
                src_ref=ps[w].at[4 * cx + 2 * cy + (1 - c)], dst_ref=sibs[w].at[rel],
                send_sem=s1_send.at[4 * w + rel], recv_sem=s1_recv.at[4 * w + rel],
                device_id=sibling, device_id_type=MESH_ID)

        def chip_copy(w, j):
            return pltpu.make_async_remote_copy(
                src_ref=qst[w].at[j], dst_ref=rels[w].at[j],
                send_sem=s2_send.at[3 * w + j], recv_sem=s2_recv.at[3 * w + j],
                device_id=(*chips[j], c), device_id_type=MESH_ID)

        pair_cps = [pair_copy(w, rel) for w in range(n) for rel in (1, 2, 3, 0)]
        for cp in pair_cps:
            cp.start()
        chip_cps = []
        for w in range(n):
            for j, (cx, cy) in enumerate(chips):
                pair_copy(w, 1 + j).wait_recv()
                la = pltpu.make_async_copy(ps[w].at[4 * cx + 2 * cy + c], pa[w], lsem.at[n + 1])
                lb = pltpu.make_async_copy(sibs[w].at[1 + j], pb[w], lsem.at[n + 2])
                la.start()
                lb.start()
                la.wait()
                lb.wait()
                qst[w][j] = (pa[w][...].astype(F32) + pb[w][...].astype(F32)).astype(BF16)
                cp = chip_copy(w, j)
                cp.start()
                chip_cps.append(cp)
        for w in range(n):
            pair_copy(w, 0).wait_recv()
            for j in range(3):
                chip_copy(w, j).wait_recv()
        for cp in sm_cps:
            cp.wait_recv()
        for cp in pair_cps + chip_cps + sm_cps:
            cp.wait_send()
        for cp in own_cps:
            cp.wait()
        sm_own.wait()

    any_spec = pl.BlockSpec(memory_space=pl.ANY)
    outs = pl.pallas_call(
        body, name="rs_grads",
        out_shape=[jax.ShapeDtypeStruct(b, BF16) for b in blks]
        + [jax.ShapeDtypeStruct((4, *b), BF16) for b in blks]
        + [jax.ShapeDtypeStruct((3, *b), BF16) for b in blks]
        + [jax.ShapeDtypeStruct((NDEV, *small.shape), F32)],
        in_specs=[any_spec] * (n + 1),
        out_specs=[any_spec] * (3 * n + 1),
        scratch_shapes=[pltpu.VMEM(b, BF16) for b in blks] + [pltpu.VMEM(b, BF16) for b in blks]
        + [pltpu.VMEM((3, *b), BF16) for b in blks]
        + [pltpu.SemaphoreType.DMA((4 * n,)), pltpu.SemaphoreType.DMA((4 * n,)),
           pltpu.SemaphoreType.DMA((3 * n,)), pltpu.SemaphoreType.DMA((3 * n,)),
           pltpu.SemaphoreType.DMA((NDEV - 1,)), pltpu.SemaphoreType.DMA((NDEV - 1,)),
           pltpu.SemaphoreType.DMA((n + 3,))],
        compiler_params=pltpu.CompilerParams(vmem_limit_bytes=40 * MIB),
    )(*parts, small)
    return outs[:n], outs[n:2 * n], outs[2 * n:3 * n], outs[3 * n]


def _rs_pair(name, parts):
    n = len(parts)
    blks = [p.shape[1:] for p in parts]

    def body(*refs):
        ps = refs[:n]
        owns, sibs, qs = refs[n:2 * n], refs[2 * n:3 * n], refs[3 * n:4 * n]
        pa, pb, qst = refs[4 * n:5 * n], refs[5 * n:6 * n], refs[6 * n:7 * n]
        s_send, s_recv, lsem = refs[7 * n:]
        x, y, c = _place()
        chips = [(1 - x, y), (x, 1 - y), (1 - x, 1 - y)]

        own_cps = [pltpu.make_async_copy(ps[w].at[4 * x + 2 * y + c], owns[w], lsem.at[w]) for w in range(n)]
        mine = [[pltpu.make_async_copy(ps[w].at[4 * cx + 2 * cy + c], pa[w].at[j], lsem.at[2 * n + 3 * w + j])
                 for j, (cx, cy) in enumerate(chips)] for w in range(n)]
        for cp in own_cps + [cp for row in mine for cp in row]:
            cp.start()

        def pair_copy(w, rel):
            cx, cy = (x, y) if rel == 0 else chips[rel - 1]
            return pltpu.make_async_remote_copy(
                src_ref=ps[w].at[4 * cx + 2 * cy + (1 - c)], dst_ref=sibs[w].at[0] if rel == 0 else pb[w].at[rel - 1],
                send_sem=s_send.at[4 * w + rel], recv_sem=s_recv.at[4 * w + rel],
                device_id=(x, y, 1 - c), device_id_type=MESH_ID)

        pair_cps = [pair_copy(w, rel) for w in range(n) for rel in (1, 2, 3, 0)]
        for cp in pair_cps:
            cp.start()
        q_cps = []
        for w in range(n):
            for j in range(3):
                pair_copy(w, 1 + j).wait_recv()
                mine[w][j].wait()
                qst[w][j] = (pa[w][j].astype(F32) + pb[w][j].astype(F32)).astype(BF16)
            cp = pltpu.make_async_copy(qst[w], qs[w], lsem.at[n + w])
            cp.start()
            q_cps.append(cp)
        for w in range(n):
            pair_copy(w, 0).wait_recv()
        for cp in pair_cps:
            cp.wait_send()
        for cp in own_cps + q_cps:
            cp.wait()

    any_spec = pl.BlockSpec(memory_space=pl.ANY)
    outs = pl.pallas_call(
        body, name=name,
        out_shape=[jax.ShapeDtypeStruct(b, BF16) for b in blks]
        + [jax.ShapeDtypeStruct((1, *b), BF16) for b in blks]
        + [jax.ShapeDtypeStruct((3, *b), BF16) for b in blks],
        in_specs=[any_spec] * n,
        out_specs=[any_spec] * (3 * n),
        scratch_shapes=[pltpu.VMEM((3, *b), BF16) for b in blks] * 3
        + [pltpu.SemaphoreType.DMA((4 * n,)), pltpu.SemaphoreType.DMA((4 * n,)), pltpu.SemaphoreType.DMA((5 * n,))],
        compiler_params=pltpu.CompilerParams(vmem_limit_bytes=48 * MIB),
    )(*parts)
    return outs[:n], outs[n:2 * n], outs[2 * n:3 * n]


def _adamw_math(g, w, m, v):
    m = ADAM_B1 * m + (1.0 - ADAM_B1) * g
    v = ADAM_B2 * v + (1.0 - ADAM_B2) * (g * g)
    m_hat = m / (1.0 - ADAM_B1 ** ADAM_STEP)
    v_hat = v / (1.0 - ADAM_B2 ** ADAM_STEP)
    delta = -ADAM_LR * (m_hat / (jnp.sqrt(v_hat) + ADAM_EPS) + ADAM_WD * w)
    return delta, m, v


def _adamw_multi(name, own, sib, rel, ws, ms, vs, row_grid):
    k_n, r_n, c_n = own.shape
    rbk = r_n // row_grid

    def body(*refs):
        own_ref, sib_ref, r0_ref, r1_ref, r2_ref = refs[:5]
        w_refs, m_refs, v_refs = refs[5:5 + k_n], refs[5 + k_n:5 + 2 * k_n], refs[5 + 2 * k_n:5 + 3 * k_n]
        outs = refs[5 + 3 * k_n:]
        for k in range(k_n):
            g = own_ref[k].astype(F32) + sib_ref[k].astype(F32)
            g = g + r0_ref[k].astype(F32)
            g = g + r1_ref[k].astype(F32)
            g = g + r2_ref[k].astype(F32)
            delta, mm, vv = _adamw_math(g, w_refs[k][0], m_refs[k][0], v_refs[k][0])
            outs[4 * k][0] = g
            outs[4 * k + 1][0] = delta
            outs[4 * k + 2][0] = mm
            outs[4 * k + 3][0] = vv

    def lead(j):
        return pl.BlockSpec((None, k_n, rbk, c_n), lambda g: (j, 0, g, 0))

    wspec = pl.BlockSpec((1, rbk, c_n), lambda g: (0, g, 0))
    shp = jax.ShapeDtypeStruct((1, r_n, c_n), F32)
    res = pl.pallas_call(
        body, name=name, grid=(row_grid,),
        in_specs=[pl.BlockSpec((k_n, rbk, c_n), lambda g: (0, g, 0)), lead(0), lead(0), lead(1), lead(2)] + [wspec] * (3 * k_n),
        out_specs=[wspec] * (4 * k_n), out_shape=[shp] * (4 * k_n),
        compiler_params=_params(("arbitrary",), 40),
    )(own, sib, rel, rel, rel, *ws, *ms, *vs)
    return [tuple(res[4 * k:4 * k + 4]) for k in range(k_n)]


def _adamw_meta_dw(own, sib, rel, meta, dw):
    def body(own_ref, sib_ref, rel_ref, wm, mm, vm, wd, md, vd, *outs):
        def gsum(rows):
            g = own_ref[rows, :].astype(F32) + sib_ref[0, rows, :].astype(F32)
            for j in range(3):
                g = g + rel_ref[j, rows, :].astype(F32)
            return g

        g = gsum(pl.ds(0, N_META))
        delta, m2, v2 = _adamw_math(g, wm[...], mm[...], vm[...])
        for o, val in zip(outs[:4], (g, delta, m2, v2)):
            o[...] = val
        g = gsum(pl.ds(N_META, CONV_K))
        delta, m2, v2 = _adamw_math(g, wd[0], md[0], vd[0])
        for o, val in zip(outs[4:], (g, delta, m2, v2)):
            o[0] = val

    s_meta = jax.ShapeDtypeStruct(meta[0].shape, F32)
    s_dw = jax.ShapeDtypeStruct(dw[0].shape, F32)
    res = pl.pallas_call(body, name="adamw_meta_dw", out_shape=[s_meta] * 4 + [s_dw] * 4)(own, sib, rel, *meta, *dw)
    return tuple(res[:4]), tuple(res[4:])


REP_ROWS = 16


def _adamw_rep(gathered, ws, ms, vs):
    rows = [(0, 1), (1, 2), (3, 1), (4, 1), (5, 1), (6, 1), (7, 1), (8, 1)]

    def body(g_ref, *refs):
        w_refs, m_refs, v_refs = refs[:8], refs[8:16], refs[16:24]
        loss_ref, outs, acc = refs[24], refs[25:57], refs[57]
        g = g_ref[0]
        for d in range(1, NDEV):
            g = g + g_ref[d]
        acc[...] = g
        loss_ref[...] = (0.5 / D) * jnp.sum(acc[pl.ds(9, 1), :], axis=1, keepdims=True)
        for p, (r0, nr) in enumerate(rows):
            for h in range(nr):
                cols = pl.ds(h * D, D)
                gp = acc[pl.ds(r0 + h, 1), :]
                delta, mm, vv = _adamw_math(gp, w_refs[p][:, cols], m_refs[p][:, cols], v_refs[p][:, cols])
                for o, val in zip(outs[4 * p:4 * p + 4], (gp, delta, mm, vv)):
                    o[:, cols] = val

    shapes = [jax.ShapeDtypeStruct(w.shape, F32) for w in ws]
    res = pl.pallas_call(
        body, name="adamw_rep",
        out_shape=[jax.ShapeDtypeStruct((1, 1), F32)] + [s for s in shapes for _ in range(4)],
        scratch_shapes=[pltpu.VMEM((REP_ROWS, D), F32)],
    )(gathered, *ws, *ms, *vs)
    return res[0], [tuple(res[1 + 4 * p:5 + 4 * p]) for p in range(8)]


def _load_ffn(i, j, wgu_hbm, wgu, wdn_hbm, wdn, sems):
    half = NDEV // 2

    def copies(ch):
        pairs = [(wgu_hbm.at[half * ch + d, g], wgu.at[g, ch, pl.ds(FFB * d, FFB), :]) for g in range(2) for d in range(half)]
        pairs.append((wdn_hbm.at[ch], wdn.at[ch]))
        return [pltpu.make_async_copy(s, t, sems.at[(2 * half + 1) * ch + k]) for k, (s, t) in enumerate(pairs)]

    @pl.when((i == 0) & (j == 0))
    def _():
        for cp in copies(0) + copies(1):
            cp.start()

    for ch in range(2):
        @pl.when((i == 0) & (j == ch))
        def _():
            for cp in copies(ch):
                cp.wait()


def _win_pairs(w_hbm, w_vm):
    return [(w_hbm.at[q], w_vm.at[q // 2, :, pl.ds(2 * INB * (q % 2), 2 * INB)]) for q in range(4)]


def _whole(a):
    nd = a.ndim
    return pl.BlockSpec(a.shape, lambda *g: (0,) * nd)


CHIPW = 2 * INB
PHASE_CHIP = (1, 0, 2)


class _GatherIn:
    scratch = [pltpu.VMEM((D, INB), BF16), pltpu.SemaphoreType.DMA((7,)), pltpu.SemaphoreType.DMA((7,)),
               pltpu.SemaphoreType.DMA((1,))]

    def bind(self, w_ref, w_vm, scratch):
        self.w_ref, self.w_vm = w_ref, w_vm
        self.stage, self.send_sems, self.recv_sems, self.local_sem = scratch
        return self

    def _win(self, chip, core):
        return self.w_vm.at[2 * chip[0] + chip[1], :, pl.ds(INB * core, INB)]

    def _copy(self, k, chip, core, to, src=None):
        dst = self._win(chip, core)
        return pltpu.make_async_remote_copy(
            src_ref=dst if src is None else src, dst_ref=dst, send_sem=self.send_sems.at[k],
            recv_sem=self.recv_sems.at[k], device_id=to, device_id_type=MESH_ID)

    def _mine(self, cs):
        x, y, _ = _place()
        return pltpu.make_async_copy(self.stage, self._win((x, y), cs), self.local_sem.at[0])

    def issue(self, cs):
        x, y, _ = _place()
        chips = [(1 - x, y), (x, 1 - y), (1 - x, 1 - y)]
        self.stage[...] = self.w_ref[0].astype(BF16)
        self._mine(cs).start()
        self._copy(0, (x, y), cs, (x, y, 1 - cs), src=self.stage).start()
        for j, chip in enumerate(chips):
            self._copy(1 + j, (x, y), cs, (*chip, cs), src=self.stage).start()

    def wait_chip(self, phase, cs):
        x, y, _ = _place()
        chips = [(1 - x, y), (x, 1 - y), (1 - x, 1 - y)]
        if phase == 0:
            self._mine(cs).wait()
            self._copy(0, (x, y), 1 - cs, (x, y, cs)).wait_recv()
            return
        if phase == 1:
            for j in PHASE_CHIP:
                self._copy(1 + j, chips[j], cs, (x, y, cs)).wait_recv()
                self._copy(4 + j, chips[j], cs, (x, y, 1 - cs)).start()
        j = PHASE_CHIP[phase - 1]
        self._copy(4 + j, chips[j], 1 - cs, (x, y, cs)).wait_recv()

    def finish(self, cs):
        x, y, _ = _place()
        for k in range(7):
            self._copy(k, (x, y), cs, (x, y, cs), src=self.stage).wait_send()


def _fwd_in(x2, g_mix, w_in, order, tp, ag, ags):
    tm = _pick(tp, TM_IO)
    nt = tp // tm
    nx_last = x2.shape[0] - (nt - 1) * tm
    na, ng, ns = len(ag.arrays), ag.n, len(ags.arrays)
    gin = _GatherIn()

    def body(order_ref, *refs):
        x_ref, g_ref, w_ref = refs[:3]
        o = 3 + na + ns
        h_ref, z_ref, u_ref, wout_ref = refs[o:o + 4]
        s = o + 4 + ng + 1
        w_vm, u_all, osem, sm_vm = refs[s:s + 4]
        gin.bind(w_ref, w_vm, refs[s + 4:s + 8])
        ag.bind(refs[3:3 + na], refs[o + 4:o + 4 + ng], refs[s + 8:s + 8 + len(ag.scratch)])
        ags.bind(refs[3 + na:3 + na + ns], refs[o + 4 + ng:o + 5 + ng], refs[s + 8 + len(ag.scratch):])
        ph, i = pl.program_id(0), pl.program_id(1)
        core = lax.axis_index("c")
        first = (ph == 0) & (i == 0)
        last = (ph == 3) & (i == nt - 1)
        for cs in range(2):
            @pl.when(first & (core == cs))
            def _():
                gin.issue(cs)

        @pl.when(first)
        def _():
            ags.issue()
            ag.issue()

        @pl.when((ph == 0) & (i == max(nt - 2, 0)))
        def _():
            ags.forward()

        for cs in range(2):
            for p in range(4):
                @pl.when((ph == p) & (i == 0) & (core == cs))
                def _():
                    gin.wait_chip(p, cs)

        @pl.when((ph == 3) & (i == max(nt - 2, 0)))
        def _():
            ag.forward()

        out_copy = pltpu.make_async_copy(w_vm, wout_ref, osem.at[0])

        @pl.when((ph == 3) & (i == 0))
        def _():
            out_copy.start()

        @pl.when((ph == 0) & (i < nt - 1))
        def _():
            h_ref[...] = x_ref[...]

        @pl.when((ph == 0) & (i == nt - 1))
        def _():
            ags.finish()
            cp = pltpu.make_async_copy(ags.outs[0], sm_vm, osem.at[1])
            cp.start()
            h_ref[pl.ds(0, nx_last), :] = x_ref[pl.ds(0, nx_last), :]
            h_ref[pl.ds(nx_last, tm - nx_last - N_META), :] = jnp.zeros((tm - nx_last - N_META, D), F32)
            cp.wait()
            for d in range(NDEV):
                h_ref[pl.ds(tm - N_META, N_META), pl.ds(128 * d, 128)] = sm_vm[d, pl.ds(0, N_META), :]

        @pl.when(ph == 0)
        def _():
            xv = h_ref[...]
            r = lax.rsqrt(jnp.mean(xv * xv, axis=-1, keepdims=True) + RMS_EPS)
            u = (xv * r * g_ref[...]).astype(BF16)
            u_ref[...] = u
            u_all[i] = u

        z_ref[...] = _dot(u_all[i], w_vm[order_ref[ph]])

        @pl.when(last)
        def _():
            ag.finish()
            out_copy.wait()

        for cs in range(2):
            @pl.when(last & (core == cs))
            def _():
                gin.finish(cs)

    def rows(ph, i, order):
        return (jnp.where(ph == 0, i, nt - 1), 0)

    tile = pl.BlockSpec((tm, D), rows)
    anys = pl.BlockSpec(memory_space=pl.ANY)
    res = pl.pallas_call(
        body, name="fwd_in",
        grid_spec=pltpu.PrefetchScalarGridSpec(
            num_scalar_prefetch=1, grid=(4, nt),
            in_specs=[tile, pl.BlockSpec((1, D), lambda ph, i, order: (0, 0)), _whole(w_in)]
            + [_whole(a) for a in ag.arrays + ags.arrays],
            out_specs=[tile, pl.BlockSpec((tm, CHIPW), lambda ph, i, order: (i, order[ph])), tile, anys] + [anys] * (ng + 1),
            scratch_shapes=[pltpu.VMEM((4, D, CHIPW), BF16), pltpu.VMEM((nt, tm, D), BF16), pltpu.SemaphoreType.DMA((2,)),
                            pltpu.VMEM(ags.out_shape[0].shape, F32)] + gin.scratch + ag.scratch + ags.scratch),
        out_shape=[jax.ShapeDtypeStruct((tp, D), F32), jax.ShapeDtypeStruct((tp, DIN), F32),
                   jax.ShapeDtypeStruct((tp, D), BF16), jax.ShapeDtypeStruct((4, D, CHIPW), BF16)]
        + ag.out_shape + ags.out_shape,
        compiler_params=_params(("arbitrary", "arbitrary"), 58),
    )(order, x2, g_mix, w_in, *ag.arrays, *ags.arrays)
    return res[:4], res[4:4 + ng], res[4 + ng]


def _halo_specs(col, nt, width=D):
    r = TM // HALO
    nb = nt * r
    return [pl.BlockSpec((HALO, width), lambda i: ((i * r + nb - 1) % nb, col)),
            pl.BlockSpec((TM, width), lambda i: (i, col)),
            pl.BlockSpec((HALO, width), lambda i: (((i + 1) * r) % nb, col))]


NCB = D // 128
TME = TM + 2 * HALO


def _tm_fill(dst, time0, groups, tile_fn):
    def body(g, c):
        for j in range(NCB):
            dst[pl.ds((time0 + 8 * g) * NCB + j, 8, stride=NCB), :] = tile_fn(pl.multiple_of(8 * g, 8), pl.ds(128 * j, 128))
        return c

    lax.fori_loop(0, groups, body, 0)


def _tm_fill_ext(dst, left, cur, right, fn):
    _tm_fill(dst, 0, HALO // 8, lambda r, l: fn(left, pl.ds(r, 8), l))
    _tm_fill(dst, HALO, TM // 8, lambda r, l: fn(cur, pl.ds(r, 8), l))
    _tm_fill(dst, HALO + TM, HALO // 8, lambda r, l: fn(right, pl.ds(r, 8), l))


def _tm_read(src, groups, store_fn):
    def body(g, c):
        for j in range(NCB):
            store_fn(pl.ds(pl.multiple_of(8 * g, 8), 8), pl.ds(128 * j, 128), src[pl.ds(8 * g * NCB + j, 8, stride=NCB), :])
        return c

    lax.fori_loop(0, groups, body, 0)


def _tm_rows(t):
    return pl.ds(t * NCB if isinstance(t, int) else pl.multiple_of(t * NCB, NCB), NCB)


def _tm_at(ref, t):
    return ref[_tm_rows(t), :]


def _by_group(sub, vals):
    return jnp.where(sub < 2, vals[0], jnp.where(sub < 4, vals[1], jnp.where(sub < 6, vals[2], vals[3])))


def _pool_cnt(b, seq, tp, sub):
    b = jnp.where(b < 0, b + tp, b)
    b = jnp.where(b >= tp, b - tp, b)
    t = jnp.where(b < seq, b + N_META, b - (tp - N_META))
    cnts = []
    for win in POOL_WINDOWS:
        left = win // 2
        lo = jnp.maximum(t - left, 0)
        hi = jnp.minimum(t + win - left, seq + N_META)
        cnts.append(jnp.maximum(hi - lo, 1).astype(F32))
    return _by_group(sub, cnts)


def _edge_rows(seq, tp):
    reach = max(POOL_WINDOWS) // 2
    return [tp - N_META + t for t in range(reach)] + [seq - reach + 1 + t for t in range(reach - 1)]


def _edge_gain(b, seq, tp, sub):
    return _by_group(sub, [float(w) for w in POOL_WINDOWS]) / _pool_cnt(b, seq, tp, sub)


def _nested_windows(at, lo_offs):
    sums, s, have = [], None, set()
    for g, win in enumerate(POOL_WINDOWS):
        for o in range(lo_offs[g], lo_offs[g] + win):
            if o not in have:
                have.add(o)
                s = at(o) if s is None else s + at(o)
        sums.append(s)
    return sums


def _seq_fwd(z, w_dw, b_dw, seq, gat):
    tp = z.shape[0]
    nt = tp // TM
    na, ng = len(gat.arrays), gat.n

    def body(*refs):
        av_l, av, av_r, ag_l, ag, ag_r, p_l, p, p_r, w_ref, b_ref = refs[:11]
        ac_ref, m_ref = refs[11 + na:13 + na]
        a3, p3, o3, m3, w3, b3, m2d = refs[13 + na + ng:20 + na + ng]
        gat.bind(refs[11:11 + na], refs[13 + na:13 + na + ng], refs[20 + na + ng:])
        i = pl.program_id(0)
        sub = lax.broadcasted_iota(jnp.int32, (NCB, 128), 0)

        @pl.when(i == 0)
        def _():
            gat.issue()
            _tm_fill(w3, 0, 4, lambda r, l: w_ref[pl.ds(r, 8), l])
            for j in range(NCB):
                b3[pl.ds(j, 1), :] = b_ref[:, pl.ds(128 * j, 128)]

        @pl.when(i == max(nt - 2, 0))
        def _():
            gat.forward()

        _tm_fill_ext(a3, (av_l, ag_l), (av, ag), (av_r, ag_r), lambda vg, r, l: vg[0][r, l] * _sig(vg[1][r, l]))
        _tm_fill_ext(p3, p_l, p, p_r, lambda ref, r, l: ref[r, l])

        def conv(g, c):
            for t in range(8):
                acc = b3[...]
                for k in range(CONV_K):
                    acc = acc + _tm_at(w3, k) * _tm_at(a3, 8 * g + t + k + 1)
                o3[_tm_rows(8 * g + t), :] = acc
            return c

        lax.fori_loop(0, TM // 8, conv, 0)
        _tm_read(o3, TM // 8, lambda r, l, tile: ac_ref.__setitem__((r, l), tile))

        inv = _by_group(sub, [1.0 / w for w in POOL_WINDOWS])

        def pool(g, c):
            for t in range(8):
                e = 8 * g + t + HALO
                sums = _nested_windows(lambda o: _tm_at(p3, e + o), [-(w // 2) for w in POOL_WINDOWS])
                m3[_tm_rows(8 * g + t), :] = _by_group(sub, sums) * inv - _tm_at(p3, e)
            return c

        lax.fori_loop(0, TM // 8, pool, 0)
        for b in _edge_rows(seq, tp):
            r = b - i * TM

            @pl.when((r >= 0) & (r < TM))
            def _():
                pv = _tm_at(p3, r + HALO)
                m3[_tm_rows(r), :] = (_tm_at(m3, r) + pv) * _edge_gain(b, seq, tp, sub) - pv

        _tm_read(m3, TM // 8, lambda r, l, tile: m2d.__setitem__((r, l), tile))
        m_ref[...] = m2d[...].astype(BF16)

        @pl.when(i == nt - 1)
        def _():
            gat.finish()

    tmaj = pltpu.VMEM((TM * NCB, 128), F32)
    text = pltpu.VMEM((TME * NCB, 128), F32)
    res = pl.pallas_call(
        body, name="seq_fwd", grid=(nt,),
        in_specs=_halo_specs(0, nt) + _halo_specs(1, nt) + _halo_specs(2, nt)
        + [pl.BlockSpec((32, D), lambda i: (0, 0)), pl.BlockSpec((1, D), lambda i: (0, 0))] + [_whole(a) for a in gat.arrays],
        out_specs=[pl.BlockSpec((TM, D), lambda i: (i, 0))] * 2 + [pl.BlockSpec(memory_space=pl.ANY)] * ng,
        out_shape=[jax.ShapeDtypeStruct((tp, D), F32), jax.ShapeDtypeStruct((tp, D), BF16)] + gat.out_shape,
        scratch_shapes=[text, text, tmaj, tmaj, pltpu.VMEM((32 * NCB, 128), F32), pltpu.VMEM((NCB, 128), F32),
                        pltpu.VMEM((TM, D), F32)] + gat.scratch,
        compiler_params=_params(("arbitrary",), 52),
    )(z, z, z, z, z, z, z, z, z, w_dw, b_dw, *gat.arrays)
    return res[:2], res[2:]


def _ln_stats(ac):
    mu = jnp.mean(ac, axis=-1, keepdims=True)
    xc = ac - mu
    rl = lax.rsqrt(jnp.mean(xc * xc, axis=-1, keepdims=True) + LN_EPS)
    return xc * rl, rl


def _pool_mix(m, wp_ref):
    return jnp.concatenate(
        [_dot(m[:, g * PG:(g + 1) * PG], wp_ref[:, g].reshape(PG, PG)) for g in range(4)], axis=1)


def _mix_fwd(ac, m, z, h0, b_gate, ln_g, ln_b, pool_scale, g_mixw, g_pool, gat):
    tp = h0.shape[0]
    nt = tp // TMS
    na, ng = len(gat.arrays), gat.n

    def body(*refs):
        ac_ref, m_ref, zga, zgb, h_ref, bg_ref, lg_ref, lb_ref, ps_ref, wm_hbm, wp_hbm = refs[:11]
        h1_ref, s_ref, yc_ref, yp_ref, mg_ref, q_ref = refs[11 + na:17 + na]
        wm, wp, sems = refs[17 + na + ng:20 + na + ng]
        gat.bind(refs[11:11 + na], refs[17 + na:17 + na + ng], refs[20 + na + ng:])
        i = pl.program_id(0)

        @pl.when(i == 0)
        def _():
            gat.issue()

        @pl.when(i == max(nt - 4, 0))
        def _():
            gat.forward()

        @pl.when(i == nt - 1)
        def _():
            gat.finish()

        _load_once(i == 0, [(wm_hbm, wm), (wp_hbm, wp)], sems)
        n, _ = _ln_stats(ac_ref[...])
        l = n * lg_ref[...] + lb_ref[...]
        s = (l * _sig(l)).astype(BF16)
        s_ref[...] = s
        yc = _dot(s, wm[:, 0].reshape(D, D))
        q = (_pool_mix(m_ref[...], wp) * ps_ref[...]).astype(BF16)
        q_ref[...] = q
        yp = _dot(q, wm[:, 1].reshape(D, D))
        ga = _sig(zga[...] + bg_ref[:, :D])
        gb = _sig(zgb[...] + bg_ref[:, D:])
        merged = (ga * yc + gb * yp).astype(BF16)
        yc_ref[...] = yc
        yp_ref[...] = yp
        mg_ref[...] = merged
        h1_ref[...] = h_ref[...] + _dot(merged, wm[:, 2].reshape(D, D))

    def tile(col=0):
        return pl.BlockSpec((TMS, D), lambda i: (i, col))

    def vec(w):
        return pl.BlockSpec((1, w), lambda i: (0, 0))

    anys = pl.BlockSpec(memory_space=pl.ANY)
    f32o, b16o = jax.ShapeDtypeStruct((tp, D), F32), jax.ShapeDtypeStruct((tp, D), BF16)
    res = pl.pallas_call(
        body, name="mix_fwd", grid=(nt,),
        in_specs=[tile(), tile(), tile(3), tile(4), tile(), vec(2 * D), vec(D), vec(D), vec(D), anys, anys]
        + [_whole(a) for a in gat.arrays],
        out_specs=[tile()] * 6 + [anys] * ng,
        out_shape=[f32o, b16o, f32o, f32o, b16o, b16o] + gat.out_shape,
        scratch_shapes=[pltpu.VMEM((NDEV, 3, D // NDEV, D), BF16), pltpu.VMEM((NDEV, 4, PG // NDEV, PG), BF16),
                        pltpu.SemaphoreType.DMA((2,))] + gat.scratch,
        compiler_params=_params(("arbitrary",), 52),
    )(ac, m, z, z, h0, b_gate, ln_g, ln_b, pool_scale, g_mixw, g_pool, *gat.arrays)
    return res[:6], res[6:]


def _ffn_fwd(h1, tgt, g_ffn, g_final, w_gu, w_dn):
    tp = h1.shape[0]
    nt = tp // TM
    nx_last = tgt.shape[0] - (nt - 1) * TM

    def body(h_ref, t_ref, gf_ref, gl_ref, wgu_hbm, wdn_hbm,
             fg_ref, fu_ref, v_ref, f_ref, dh2_ref, acc_ref, wgu, wdn, v_sc, h2_sc, diff_sc, sems):
        i, j = pl.program_id(0), pl.program_id(1)
        _load_ffn(i, j, wgu_hbm, wgu, wdn_hbm, wdn, sems)

        @pl.when((i == 0) & (j == 0))
        def _():
            acc_ref[...] = jnp.zeros_like(acc_ref)

        @pl.when(j == 0)
        def _():
            h = h_ref[...]
            r = lax.rsqrt(jnp.mean(h * h, axis=-1, keepdims=True) + RMS_EPS)
            v = (h * r * gf_ref[...]).astype(BF16)
            v_sc[...] = v
            v_ref[...] = v
            h2_sc[...] = h

        v = v_sc[...]
        fg = _dot_nt(v, wgu[0, j])
        fu = _dot_nt(v, wgu[1, j])
        fg_ref[...] = fg
        fu_ref[...] = fu
        f = ((fg * _sig(fg)) * fu).astype(BF16)
        f_ref[...] = f
        h2_sc[...] += _dot(f, wdn[j])

        @pl.when(j == 1)
        def _():
            h2 = h2_sc[...]
            r = lax.rsqrt(jnp.mean(h2 * h2, axis=-1, keepdims=True) + RMS_EPS)
            n2 = h2 * r
            y = n2 * gl_ref[...]

            @pl.when(i < nt - 1)
            def _():
                diff_sc[...] = y - t_ref[...]

            @pl.when(i == nt - 1)
            def _():
                diff_sc[pl.ds(0, nx_last), :] = y[:nx_last] - t_ref[pl.ds(0, nx_last), :]
                diff_sc[pl.ds(nx_last, TM - nx_last), :] = jnp.zeros((TM - nx_last, D), F32)

            diff = diff_sc[...]
            dy = diff * (1.0 / D)
            acc_ref[0:1, :] += jnp.sum(diff * diff, axis=0, keepdims=True)
            acc_ref[1:2, :] += jnp.sum(dy * n2, axis=0, keepdims=True)
            dn = dy * gl_ref[...]
            dh2_ref[...] = r * (dn - n2 * jnp.mean(dn * n2, axis=-1, keepdims=True))

    def tile():
        return pl.BlockSpec((TM, D), lambda i, j: (i, 0))

    def chunk():
        return pl.BlockSpec((TM, FFC), lambda i, j: (i, j))

    def vec():
        return pl.BlockSpec((1, D), lambda i, j: (0, 0))

    anys = pl.BlockSpec(memory_space=pl.ANY)
    hid32, hid16 = jax.ShapeDtypeStruct((tp, DFF), F32), jax.ShapeDtypeStruct((tp, DFF), BF16)
    return pl.pallas_call(
        body, name="ffn_fwd", grid=(nt, 2),
        in_specs=[tile(), tile(), vec(), vec(), anys, anys],
        out_specs=[chunk(), chunk(), tile(), chunk(), tile(), pl.BlockSpec((8, D), lambda i, j: (0, 0))],
        out_shape=[hid32, hid32, jax.ShapeDtypeStruct((tp, D), BF16), hid16, jax.ShapeDtypeStruct((tp, D), F32),
                   jax.ShapeDtypeStruct((8, D), F32)],
        scratch_shapes=[pltpu.VMEM((2, 2, FFC, D), BF16), pltpu.VMEM((2, FFC, D), BF16),
                        pltpu.VMEM((TM, D), BF16), pltpu.VMEM((TM, D), F32), pltpu.VMEM((TM, D), F32),
                        pltpu.SemaphoreType.DMA((2 * NDEV + 2,))],
        compiler_params=_params(("arbitrary", "arbitrary"), 56),
    )(h1, tgt, g_ffn, g_final, w_gu, w_dn)


def _ffn_bwd(dh2, fg, fu, h1, g_ffn, w_gu, w_dn):
    tp = h1.shape[0]
    nt = tp // TM

    def body(dh2_ref, fg_ref, fu_ref, h_ref, gf_ref, wgu_hbm, wdn_hbm,
             dfg_ref, dfu_ref, dh1_ref, acc_ref, wgu, wdn, d_sc, dv_sc, sems):
        i, j = pl.program_id(0), pl.program_id(1)
        _load_ffn(i, j, wgu_hbm, wgu, wdn_hbm, wdn, sems)

        @pl.when((i == 0) & (j == 0))
        def _():
            acc_ref[...] = jnp.zeros_like(acc_ref)

        @pl.when(j == 0)
        def _():
            d_sc[...] = dh2_ref[...].astype(BF16)
            dv_sc[...] = jnp.zeros_like(dv_sc)

        df = _dot_nt(d_sc[...], wdn[j])
        fg = fg_ref[...]
        sg = _sig(fg)
        dfu = (df * (fg * sg)).astype(BF16)
        dfg = (df * fu_ref[...] * (sg * (1.0 + fg * (1.0 - sg)))).astype(BF16)
        dfg_ref[...] = dfg
        dfu_ref[...] = dfu
        dv_sc[...] += _dot(dfg, wgu[0, j]) + _dot(dfu, wgu[1, j])

        @pl.when(j == 1)
        def _():
            h = h_ref[...]
            r = lax.rsqrt(jnp.mean(h * h, axis=-1, keepdims=True) + RMS_EPS)
            n1 = h * r
            dv = dv_sc[...]
            acc_ref[0:1, :] += jnp.sum(dv * n1, axis=0, keepdims=True)
            dn = dv * gf_ref[...]
            dh1_ref[...] = dh2_ref[...] + r * (dn - n1 * jnp.mean(dn * n1, axis=-1, keepdims=True))

    def tile():
        return pl.BlockSpec((TM, D), lambda i, j: (i, 0))

    def chunk():
        return pl.BlockSpec((TM, FFC), lambda i, j: (i, j))

    anys = pl.BlockSpec(memory_space=pl.ANY)
    hid16 = jax.ShapeDtypeStruct((tp, DFF), BF16)
    return pl.pallas_call(
        body, name="ffn_bwd", grid=(nt, 2),
        in_specs=[tile(), chunk(), chunk(), tile(), pl.BlockSpec((1, D), lambda i, j: (0, 0)), anys, anys],
        out_specs=[chunk(), chunk(), tile(), pl.BlockSpec((8, D), lambda i, j: (0, 0))],
        out_shape=[hid16, hid16, jax.ShapeDtypeStruct((tp, D), F32), jax.ShapeDtypeStruct((8, D), F32)],
        scratch_shapes=[pltpu.VMEM((2, 2, FFC, D), BF16), pltpu.VMEM((2, FFC, D), BF16),
                        pltpu.VMEM((TM, D), BF16), pltpu.VMEM((TM, D), F32), pltpu.SemaphoreType.DMA((2 * NDEV + 2,))],
        compiler_params=_params(("arbitrary", "arbitrary"), 56),
    )(dh2, fg, fu, h1, g_ffn, w_gu, w_dn)


def _mix_bwd(dh1, z, yc, yp, ac, m, b_gate, ln_g, ln_b, pool_scale, g_mixw, g_pool, qs):
    tp = dh1.shape[0]
    nt = tp // TMS
    ex = _ChipExchange(qs)
    nq = ex.n

    def body(*refs):
        dh1_ref, zga, zgb, yc_ref, yp_ref, ac_ref, m_ref, bg_ref, lg_ref, lb_ref, ps_ref, wm_hbm, wp_hbm = refs[:13]
        dac_ref, dm_ref, dzg_ref, dyc_ref, dyp_ref, dm2_ref, acc_ref = refs[13 + nq:20 + nq]
        wm, wp, sems = refs[20 + 2 * nq:23 + 2 * nq]
        ex.bind(refs[13:13 + nq], refs[20 + nq:20 + 2 * nq], refs[23 + 2 * nq:])
        first = pl.program_id(0) == 0

        @pl.when(first)
        def _():
            ex.issue()
            acc_ref[...] = jnp.zeros_like(acc_ref)

        _load_once(first, [(wm_hbm, wm), (wp_hbm, wp)], sems)

        dmerged = _dot_nt(dh1_ref[...].astype(BF16), wm[:, 2].reshape(D, D))
        ga = _sig(zga[...] + bg_ref[:, :D])
        gb = _sig(zgb[...] + bg_ref[:, D:])
        dyc = dmerged * ga
        dyp = dmerged * gb
        dza = (dmerged * yc_ref[...]) * (ga * (1.0 - ga))
        dzb = (dmerged * yp_ref[...]) * (gb * (1.0 - gb))
        dzg_ref[:, :D] = dza.astype(BF16)
        dzg_ref[:, D:] = dzb.astype(BF16)
        acc_ref[0:1, :D] += jnp.sum(dza, axis=0, keepdims=True)
        acc_ref[0:1, D:] += jnp.sum(dzb, axis=0, keepdims=True)
        dyc_b = dyc.astype(BF16)
        dyp_b = dyp.astype(BF16)
        dyc_ref[...] = dyc_b
        dyp_ref[...] = dyp_b
        ds = _dot_nt(dyc_b, wm[:, 0].reshape(D, D))
        n, rl = _ln_stats(ac_ref[...])
        l = n * lg_ref[...] + lb_ref[...]
        sg = _sig(l)
        dl = ds * (sg * (1.0 + l * (1.0 - sg)))
        acc_ref[1:2, :D] += jnp.sum(dl * n, axis=0, keepdims=True)
        acc_ref[1:2, D:] += jnp.sum(dl, axis=0, keepdims=True)
        dn = dl * lg_ref[...]
        dac_ref[...] = rl * (dn - jnp.mean(dn, axis=-1, keepdims=True) - n * jnp.mean(dn * n, axis=-1, keepdims=True))
        dq = _dot_nt(dyp_b, wm[:, 1].reshape(D, D))
        mv = m_ref[...]
        acc_ref[2:3, :D] += jnp.sum(dq * _pool_mix(mv, wp), axis=0, keepdims=True)
        dm2 = (dq * ps_ref[...]).astype(BF16)
        dm2_ref[...] = dm2
        dm_ref[...] = jnp.concatenate(
            [_dot_nt(dm2[:, g * PG:(g + 1) * PG], wp[:, g].reshape(PG, PG)) for g in range(4)], axis=1)

        @pl.when(pl.program_id(0) == nt - 1)
        def _():
            ex.finish()

    def tile(col=0):
        return pl.BlockSpec((TMS, D), lambda i: (i, col))

    def vec(w):
        return pl.BlockSpec((1, w), lambda i: (0, 0))

    anys = pl.BlockSpec(memory_space=pl.ANY)
    f32o, b16o = jax.ShapeDtypeStruct((tp, D), F32), jax.ShapeDtypeStruct((tp, D), BF16)
    res = pl.pallas_call(
        body, name="mix_bwd", grid=(nt,),
        in_specs=[tile(), tile(3), tile(4), tile(), tile(), tile(), tile(), vec(2 * D), vec(D), vec(D), vec(D), anys, anys]
        + [anys] * nq,
        out_specs=[tile(), tile(), pl.BlockSpec((TMS, 2 * D), lambda i: (i, 0)), tile(), tile(), tile(),
                   pl.BlockSpec((8, 2 * D), lambda i: (0, 0))] + [anys] * nq,
        out_shape=[f32o, f32o, jax.ShapeDtypeStruct((tp, 2 * D), BF16), b16o, b16o, b16o,
                   jax.ShapeDtypeStruct((8, 2 * D), F32)] + ex.out_shape,
        scratch_shapes=[pltpu.VMEM((NDEV, 3, D // NDEV, D), BF16), pltpu.VMEM((NDEV, 4, PG // NDEV, PG), BF16),
                        pltpu.SemaphoreType.DMA((2,))] + ex.scratch,
        compiler_params=_params(("arbitrary",), 48),
    )(dh1, z, z, yc, yp, ac, m, b_gate, ln_g, ln_b, pool_scale, g_mixw, g_pool, *qs)
    return res[:7], res[7:]


def _seq_bwd(dac, dm, dzg, z, w_dw, seq, qs):
    tp = z.shape[0]
    nt = tp // TM
    ex = _ChipExchange(qs)
    nq = ex.n

    def body(*refs):
        dac_l, dac_c, dac_r, dm_l, dm_c, dm_r, av_l, av, av_r, ag_l, ag, ag_r, dzg_ref, w_ref = refs[:14]
        dz_ref, acc_ref = refs[14 + nq:16 + nq]
        a3, d3, m3, da3, dp3, w3, dw3, da_sc, dp_sc = refs[16 + 2 * nq:25 + 2 * nq]
        ex.bind(refs[14:14 + nq], refs[16 + nq:16 + 2 * nq], refs[25 + 2 * nq:])
        i = pl.program_id(0)
        sub = lax.broadcasted_iota(jnp.int32, (NCB, 128), 0)

        @pl.when(i == 0)
        def _():
            ex.issue()
            dw3[...] = jnp.zeros_like(dw3)
            _tm_fill(w3, 0, 4, lambda r, l: w_ref[pl.ds(r, 8), l])

        _tm_fill_ext(a3, (av_l, ag_l), (av, ag), (av_r, ag_r), lambda vg, r, l: vg[0][r, l] * _sig(vg[1][r, l]))
        _tm_fill_ext(d3, dac_l, dac_c, dac_r, lambda ref, r, l: ref[r, l])
        _tm_fill_ext(m3, dm_l, dm_c, dm_r, lambda ref, r, l: ref[r, l])

        def conv(g, c):
            dcur = [_tm_at(d3, 8 * g + t + HALO) for t in range(8)]
            accs = [None] * 8
            for k in range(CONV_K):
                wk = _tm_at(w3, k)
                s = None
                for t in range(8):
                    term = wk * _tm_at(d3, 8 * g + t + CONV_K - k)
                    accs[t] = term if accs[t] is None else accs[t] + term
                    pr = dcur[t] * _tm_at(a3, 8 * g + t + k + 1)
                    s = pr if s is None else s + pr
                dw3[_tm_rows(k), :] += s
            s = dcur[0]
            for t in range(1, 8):
                s = s + dcur[t]
            dw3[_tm_rows(CONV_K), :] += s
            for t in range(8):
                da3[_tm_rows(8 * g + t), :] = accs[t]
            return c

        lax.fori_loop(0, TM // 8, conv, 0)

        for b in _edge_rows(seq, tp):
            e = lax.rem(b - i * TM + HALO + tp, tp)

            @pl.when(e < TME)
            def _():
                m3[_tm_rows(e), :] = _tm_at(m3, e) * _edge_gain(b, seq, tp, sub)

        inv = _by_group(sub, [1.0 / w for w in POOL_WINDOWS])

        def pool(g, c):
            for t in range(8):
                e = 8 * g + t + HALO
                sums = _nested_windows(lambda o: _tm_at(m3, e + o), [w // 2 + 1 - w for w in POOL_WINDOWS])
                dp3[_tm_rows(8 * g + t), :] = _by_group(sub, sums) * inv
            return c

        lax.fori_loop(0, TM // 8, pool, 0)

        _tm_read(da3, TM // 8, lambda r, l, tile: da_sc.__setitem__((r, l), tile))
        _tm_read(dp3, TM // 8, lambda r, l, tile: dp_sc.__setitem__((r, l), tile))
        sg = _sig(ag[...])
        da = da_sc[...]
        dz_ref[:, 0:D] = (da * sg).astype(BF16)
        dz_ref[:, D:2 * D] = (da * av[...] * (sg * (1.0 - sg))).astype(BF16)
        dz_ref[:, 2 * D:3 * D] = (dp_sc[...] - dm_c[...]).astype(BF16)
        dz_ref[:, 3 * D:] = dzg_ref[...]

        @pl.when(i == nt - 1)
        def _():
            _tm_read(dw3, 4, lambda r, l, tile: acc_ref.__setitem__((r, l), tile))
            ex.finish()

    tmaj = pltpu.VMEM((TM * NCB, 128), F32)
    text = pltpu.VMEM((TME * NCB, 128), F32)
    taps = pltpu.VMEM((32 * NCB, 128), F32)
    anys = pl.BlockSpec(memory_space=pl.ANY)
    res = pl.pallas_call(
        body, name="seq_bwd", grid=(nt,),
        in_specs=_halo_specs(0, nt) + _halo_specs(0, nt) + _halo_specs(0, nt) + _halo_specs(1, nt)
        + [pl.BlockSpec((TM, 2 * D), lambda i: (i, 0)), pl.BlockSpec((32, D), lambda i: (0, 0))] + [anys] * nq,
        out_specs=[pl.BlockSpec((TM, DIN), lambda i: (i, 0)), pl.BlockSpec((32, D), lambda i: (0, 0))] + [anys] * nq,
        out_shape=[jax.ShapeDtypeStruct((tp, DIN), BF16), jax.ShapeDtypeStruct((32, D), F32)] + ex.out_shape,
        scratch_shapes=[text, text, text, tmaj, tmaj, taps, taps, pltpu.VMEM((TM, D), F32), pltpu.VMEM((TM, D), F32)]
        + ex.scratch,
        compiler_params=_params(("arbitrary",), 48),
    )(dac, dac, dac, dm, dm, dm, z, z, z, z, z, z, dzg, w_dw, *qs)
    return res[:2], res[2:]


def _in_bwd(dz, h0, dh1, g_mix, w_g, seq, qs):
    tp = h0.shape[0]
    tm = _pick(tp, TM_IO)
    nt = tp // tm
    ex = _ChipExchange(qs)
    nq = ex.n

    def body(*refs):
        dz_ref, h_ref, dh1_ref, g_ref, w_hbm = refs[:5]
        gx_ref, gmeta_ref, acc_ref = refs[5 + nq:8 + nq]
        w_vm, sems = refs[8 + 2 * nq:10 + 2 * nq]
        ex.bind(refs[5:5 + nq], refs[8 + nq:8 + 2 * nq], refs[10 + 2 * nq:])
        i = pl.program_id(0)

        @pl.when(i == 0)
        def _():
            ex.issue()
            acc_ref[...] = jnp.zeros_like(acc_ref)

        _load_once(i == 0, _win_pairs(w_hbm, w_vm), sems)

        du = _dot_nt(dz_ref[:, :DIN // 2], w_vm[0]) + _dot_nt(dz_ref[:, DIN // 2:], w_vm[1])
        h = h_ref[...]
        r = lax.rsqrt(jnp.mean(h * h, axis=-1, keepdims=True) + RMS_EPS)
        n0 = h * r
        acc_ref[0:1, :] += jnp.sum(du * n0, axis=0, keepdims=True)
        dn = du * g_ref[...]
        gx_ref[...] = dh1_ref[...] + r * (dn - n0 * jnp.mean(dn * n0, axis=-1, keepdims=True))

        @pl.when(i == nt - 1)
        def _():
            gmeta_ref[...] = gx_ref[pl.ds(tm - N_META, N_META), :]
            ex.finish()

    tile = pl.BlockSpec((tm, D), lambda i: (i, 0))
    anys = pl.BlockSpec(memory_space=pl.ANY)
    res = pl.pallas_call(
        body, name="in_bwd", grid=(nt,),
        in_specs=[pl.BlockSpec((tm, DIN), lambda i: (i, 0)), tile, tile, pl.BlockSpec((1, D), lambda i: (0, 0)), anys]
        + [anys] * nq,
        out_specs=[tile, pl.BlockSpec((N_META, D), lambda i: (0, 0)), pl.BlockSpec((8, D), lambda i: (0, 0))] + [anys] * nq,
        out_shape=[jax.ShapeDtypeStruct((seq, D), F32), jax.ShapeDtypeStruct((N_META, D), F32),
                   jax.ShapeDtypeStruct((8, D), F32)] + ex.out_shape,
        scratch_shapes=[pltpu.VMEM((2, D, DIN // 2), BF16), pltpu.SemaphoreType.DMA((NDEV,))] + ex.scratch,
        compiler_params=_params(("arbitrary",), 58),
    )(dz, h0, dh1, g_mix, w_g, *qs)
    return res[:3], res[3:]


def _wgrad_in(u, dz):
    tp = u.shape[0]
    tm = _pick(tp, TM_WG)
    nt = tp // tm
    half = DIN // 2

    def body(u_ref, dz_ref, o_ref, acc):
        t = pl.program_id(1)

        @pl.when(t == 0)
        def _():
            acc[...] = jnp.zeros_like(acc)

        acc[...] += _dot_tn(u_ref[...], dz_ref[...])

        @pl.when(t == nt - 1)
        def _():
            for d in range(4):
                o_ref[d] = acc[:, INB * d:INB * (d + 1)].astype(BF16)

    return pl.pallas_call(
        body, name="wgrad_in", grid=(2, nt),
        in_specs=[pl.BlockSpec((tm, D), lambda h, t: (t, 0)), pl.BlockSpec((tm, half), lambda h, t: (t, h))],
        out_specs=pl.BlockSpec((4, D, INB), lambda h, t: (h, 0, 0), pipeline_mode=pl.Buffered(1)),
        out_shape=jax.ShapeDtypeStruct((NDEV, D, INB), BF16),
        scratch_shapes=[pltpu.VMEM((D, half), F32)],
        compiler_params=_params(("arbitrary", "arbitrary"), 52),
    )(u, dz)


def _wgrad_mix(s, dyc, q, dyp, merged, dh1, m, dm2):
    tp = s.shape[0]
    tm = _pick(tp, TM_WM)
    nt = tp // tm
    rb = D // NDEV

    def body(s_ref, dyc_ref, q_ref, dyp_ref, mg_ref, dh1_ref, m_ref, dm2_ref, o_ref, op_ref, acc, accp):
        t = pl.program_id(0)

        @pl.when(t == 0)
        def _():
            acc[...] = jnp.zeros_like(acc)
            accp[...] = jnp.zeros_like(accp)

        acc[0] += _dot_tn(s_ref[...], dyc_ref[...])
        acc[1] += _dot_tn(q_ref[...], dyp_ref[...])
        acc[2] += _dot_tn(mg_ref[...], dh1_ref[...].astype(BF16))
        for g in range(4):
            accp[g] += _dot_tn(m_ref[:, g * PG:(g + 1) * PG], dm2_ref[:, g * PG:(g + 1) * PG])

        @pl.when(t == nt - 1)
        def _():
            for d in range(NDEV):
                for k in range(3):
                    o_ref[d, k] = acc[k, rb * d:rb * (d + 1), :].astype(BF16)
                for g in range(4):
                    op_ref[d, g] = accp[g, 32 * d:32 * (d + 1), :].astype(BF16)

    tile = pl.BlockSpec((tm, D), lambda t: (t, 0))
    return pl.pallas_call(
        body, name="wgrad_mix", grid=(nt,),
        in_specs=[tile] * 8,
        out_specs=[pl.BlockSpec((NDEV, 3, rb, D), lambda t: (0, 0, 0, 0), pipeline_mode=pl.Buffered(1)),
                   pl.BlockSpec((NDEV, 4, 32, PG), lambda t: (0, 0, 0, 0), pipeline_mode=pl.Buffered(1))],
        out_shape=[jax.ShapeDtypeStruct((NDEV, 3, rb, D), BF16), jax.ShapeDtypeStruct((NDEV, 4, 32, PG), BF16)],
        scratch_shapes=[pltpu.VMEM((3, D, D), F32), pltpu.VMEM((4, PG, PG), F32)],
        compiler_params=_params(("arbitrary",), 56),
    )(s, dyc, q, dyp, merged, dh1, m, dm2)


def _wgrad_gu(v, dfg, dfu):
    tp = v.shape[0]
    tm = _pick(tp, TM_WG)
    nt = tp // tm

    def body(v_ref, dg_ref, du_ref, o_ref, acc):
        k, t = pl.program_id(0), pl.program_id(2)

        @pl.when(t == 0)
        def _():
            acc[...] = jnp.zeros_like(acc)

        @pl.when(k == 0)
        def _():
            acc[...] += _dot_tn(dg_ref[...], v_ref[...])

        @pl.when(k == 1)
        def _():
            acc[...] += _dot_tn(du_ref[...], v_ref[...])

        @pl.when(t == nt - 1)
        def _():
            for d in range(4):
                o_ref[d] = acc[FFB * d:FFB * (d + 1), :].astype(BF16)

    return pl.pallas_call(
        body, name="wgrad_gu", grid=(2, 2, nt),
        in_specs=[pl.BlockSpec((tm, D), lambda k, h, t: (t, 0)),
                  pl.BlockSpec((tm, FFC), lambda k, h, t: (t * (1 - k), h * (1 - k))),
                  pl.BlockSpec((tm, FFC), lambda k, h, t: (t * k, h * k))],
        out_specs=pl.BlockSpec((4, None, FFB, D), lambda k, h, t: (h, k, 0, 0), pipeline_mode=pl.Buffered(1)),
        out_shape=jax.ShapeDtypeStruct((NDEV, 2, FFB, D), BF16),
        scratch_shapes=[pltpu.VMEM((FFC, D), F32)],
        compiler_params=_params(("arbitrary",) * 3, 48),
    )(v, dfg, dfu)


def _wgrad_down(f, dh2):
    tp = f.shape[0]
    tm = _pick(tp, TM_WG)
    nt = tp // tm

    def body(f_ref, d_ref, o_ref, acc):
        t = pl.program_id(1)

        @pl.when(t == 0)
        def _():
            acc[...] = jnp.zeros_like(acc)

        acc[...] += _dot_tn(f_ref[...], d_ref[...].astype(BF16))

        @pl.when(t == nt - 1)
        def _():
            for d in range(4):
                o_ref[d] = acc[FFB * d:FFB * (d + 1), :].astype(BF16)

    return pl.pallas_call(
        body, name="wgrad_down", grid=(2, nt),
        in_specs=[pl.BlockSpec((tm, FFC), lambda h, t: (t, h)), pl.BlockSpec((tm, D), lambda h, t: (t, 0))],
        out_specs=pl.BlockSpec((4, FFB, D), lambda h, t: (h, 0, 0), pipeline_mode=pl.Buffered(1)),
        out_shape=jax.ShapeDtypeStruct((NDEV, FFB, D), BF16),
        scratch_shapes=[pltpu.VMEM((FFC, D), F32)],
        compiler_params=_params(("arbitrary", "arbitrary"), 48),
    )(f, dh2)


def kernel(x, meta_tokens, g_mix, w_in, b_gate, w_dw, b_dw, ln_g, ln_b, w_conv_out, w_pool, pool_scale, w_pool_out, w_o, g_ffn, w_ffn_gate, w_ffn_up, w_ffn_down, g_final, loss_target, m_meta_tokens, m_g_mix, m_w_in, m_b_gate, m_w_dw, m_b_dw, m_ln_g, m_ln_b, m_w_conv_out, m_w_pool, m_pool_scale, m_w_pool_out, m_w_o, m_g_ffn, m_w_ffn_gate, m_w_ffn_up, m_w_ffn_down, m_g_final, v_meta_tokens, v_g_mix, v_w_in, v_b_gate, v_w_dw, v_b_dw, v_ln_g, v_ln_b, v_w_conv_out, v_w_pool, v_pool_scale, v_w_pool_out, v_w_o, v_g_ffn, v_w_ffn_gate, v_w_ffn_up, v_w_ffn_down, v_g_final):
    seq = x.shape[1]
    tp = -(-(seq + 2 * HALO) // TM) * TM
    tm_in = _pick(tp, TM_IO)
    nx_last = seq - (tp // tm_in - 1) * tm_in
    assert 0 < nx_last <= tm_in - 2 * HALO and nx_last % 8 == 0 and 0 < seq - (tp // TM - 1) * TM

    whole = (Ellipsis,)
    ag_small = _Gather(
        [((48, D // NDEV), [(meta_tokens, pl.ds(0, N_META), whole), (w_dw, pl.ds(N_META, CONV_K), 0)])], [F32])
    ag_mix = _Gather([((3, D // NDEV, D), [(w_conv_out, 0, 0), (w_pool_out, 1, 0), (w_o, 2, 0)]),
                      ((4, PG // NDEV, PG), [(w_pool, whole, 0)])], [BF16, BF16])
    def tr(a):
        return jnp.swapaxes(a, 1, 2)

    ag_gu = _Gather([((2, FFB, D), [(tr(w_ffn_gate), 0, 0), (tr(w_ffn_up), 1, 0)])], [BF16])
    ag_dn = _Gather([((FFB, D), [(w_ffn_down, whole, 0)])], [BF16])

    mx, my = lax.axis_index("x"), lax.axis_index("y")
    order = jnp.stack([2 * mx + my, 2 * mx + 1 - my, 2 * (1 - mx) + my, 2 * (1 - mx) + 1 - my]).astype(jnp.int32)
    (h0, z, u, g_in), (g_mixw, g_pool), g_small = _fwd_in(x[0], g_mix, w_in, order, tp, ag_mix, ag_small)
    wdw_full = g_small.transpose(1, 0, 2).reshape(48, D)[N_META:]
    (ac, m), (w_gu,) = _seq_fwd(z, wdw_full, b_dw, seq, ag_gu)
    (h1, s, yc, yp, merged, q), (g_down,) = _mix_fwd(ac, m, z, h0, b_gate, ln_g, ln_b, pool_scale, g_mixw, g_pool, ag_dn)
    w_dn = g_down.reshape(2, FFC, D)
    fg, fu, v, f, dh2, head_acc = _ffn_fwd(h1, loss_target[0], g_ffn, g_final.reshape(1, D), w_gu, w_dn)

    dfg, dfu, dh1, ffn_acc = _ffn_bwd(dh2, fg, fu, h1, g_ffn, w_gu, w_dn)
    own_f, sib_f, q_f = _rs_pair("rs_pair_ffn", [_wgrad_gu(v, dfg, dfu), _wgrad_down(f, dh2)])
    (dac, dm, dzg, dyc, dyp, dm2, mix_acc), rel_f = _mix_bwd(
        dh1, z, yc, yp, ac, m, b_gate, ln_g, ln_b, pool_scale, g_mixw, g_pool, q_f)
    own_m, sib_m, q_m = _rs_pair("rs_pair_mix", list(_wgrad_mix(s, dyc, q, dyp, merged, dh1, m, dm2)))
    (dz, seq_acc), rel_m = _seq_bwd(dac, dm, dzg, z, wdw_full, seq, q_m)
    own_i, sib_i, q_i = _rs_pair("rs_pair_in", [_wgrad_in(u, dz)])
    (grad_x, g_meta, in_acc), rel_i = _in_bwd(dz, h0, dh1, g_mix, g_in, seq, q_i)
    small_g = jnp.concatenate([g_meta, seq_acc[:CONV_K], jnp.zeros((1, D), F32)], axis=0)
    p_small = small_g.reshape(48, NDEV, D // NDEV).transpose(1, 0, 2).astype(BF16)
    rep_g = jnp.concatenate([
        in_acc[0:1], mix_acc[0:1, :D], mix_acc[0:1, D:], seq_acc[CONV_K:CONV_K + 1], mix_acc[1:2, :D], mix_acc[1:2, D:],
        mix_acc[2:3, :D], ffn_acc[0:1], head_acc[1:2], head_acc[0:1], jnp.zeros((REP_ROWS - 10, D), F32)], axis=0)
    own_s, sib_s, rel_s, rep_all = _reduce_scatter([p_small], rep_g)
    owns = [own_i[0], own_s[0], own_m[0], own_m[1], own_f[0], own_f[1]]
    sibs = [sib_i[0], sib_s[0], sib_m[0], sib_m[1], sib_f[0], sib_f[1]]
    rels = [rel_i[0], rel_s[0], rel_m[0], rel_m[1], rel_f[0], rel_f[1]]

    def lead(a):
        return a.reshape(1, *a.shape)

    def stack4(a, lead_dims):
        return a.reshape(*lead_dims, 1, 4 * 32, PG)

    (r_in,) = _adamw_multi("adamw_in", lead(owns[0]), sibs[0][:, None], rels[0][:, None], [w_in], [m_w_in], [v_w_in], 4)
    r_meta, r_dw = _adamw_meta_dw(owns[1], sibs[1], rels[1], (meta_tokens, m_meta_tokens, v_meta_tokens),
                                  (w_dw, m_w_dw, v_w_dw))
    r_conv, r_pout, r_o = _adamw_multi("adamw_mix", owns[2], sibs[2], rels[2], [w_conv_out, w_pool_out, w_o],
                                       [m_w_conv_out, m_w_pool_out, m_w_o], [v_w_conv_out, v_w_pool_out, v_w_o], 1)
    (r_pool,) = _adamw_multi("adamw_pool", stack4(owns[3], ()), stack4(sibs[3], (1,)), stack4(rels[3], (3,)),
                             [w_pool.reshape(1, 128, PG)], [m_w_pool.reshape(1, 128, PG)], [v_w_pool.reshape(1, 128, PG)], 1)
    r_pool = tuple(a.reshape(w_pool.shape) for a in r_pool)
    r_gate, r_up = _adamw_multi("adamw_gu", owns[4], sibs[4], rels[4], [tr(w_ffn_gate), tr(w_ffn_up)],
                                [tr(m_w_ffn_gate), tr(m_w_ffn_up)], [tr(v_w_ffn_gate), tr(v_w_ffn_up)], 2)
    r_gate, r_up = tuple(tr(a) for a in r_gate), tuple(tr(a) for a in r_up)
    (r_down,) = _adamw_multi("adamw_down", lead(owns[5]), sibs[5][:, None], rels[5][:, None],
                             [w_ffn_down], [m_w_ffn_down], [v_w_ffn_down], 2)
    row = (1, D)
    loss, reps = _adamw_rep(
        rep_all,
        [g_mix, b_gate, b_dw, ln_g, ln_b, pool_scale, g_ffn, g_final.reshape(row)],
        [m_g_mix, m_b_gate, m_b_dw, m_ln_g, m_ln_b, m_pool_scale, m_g_ffn, m_g_final.reshape(row)],
        [v_g_mix, v_b_gate, v_b_dw, v_ln_g, v_ln_b, v_pool_scale, v_g_ffn, v_g_final.reshape(row)])
    r_gmix, r_bg, r_bdw, r_lg, r_lb, r_ps, r_gffn, r_gfin = reps
    r_gfin = tuple(a.reshape(D) for a in r_gfin)

    in_order = [r_meta, r_gmix, r_in, r_bg, r_dw, r_bdw, r_lg, r_lb, r_conv, r_pool, r_ps, r_pout, r_o, r_gffn,
                r_gate, r_up, r_down, r_gfin]
    return (loss.reshape(()), grad_x[None], *[r[0] for r in in_order], *[r[1] for r in in_order],
            *[r[2] for r in in_order], *[r[3] for r in in_order])
```

```python
import math

import jax
import jax.numpy as jnp
from jax import lax
from jax.experimental import pallas as pl
from jax.experimental.pallas import tpu as pltpu

F32, BF16 = jnp.float32, jnp.bfloat16
MESH_ID = pl.DeviceIdType.MESH
NDEV = 8

D = 1024
N_META = 16
CONV_K = 31
HALO = 16
POOL_WINDOWS = (2, 4, 8, 16)
PG = 256
DIN = 5 * D
DFF = 2816
FFB = DFF // NDEV
FFC = DFF // 2
INB = DIN // NDEV
RMS_EPS = 1e-6
LN_EPS = 1e-5
ADAM_LR, ADAM_B1, ADAM_B2, ADAM_EPS, ADAM_WD, ADAM_STEP = 0.001, 0.9, 0.999, 1e-08, 0.01, 10

TM = 384
TMS = 192
TM_IO = 704
TM_WG = 1408
TM_WM = 704
MIB = 2 ** 20


def _sig(x):
    return 0.5 * jnp.tanh(0.5 * x) + 0.5


def _dot(a, b):
    return jnp.dot(a, b, preferred_element_type=F32)


def _dot_nt(a, b):
    return lax.dot_general(a, b, (((1,), (1,)), ((), ())), preferred_element_type=F32)


def _dot_tn(a, b):
    return lax.dot_general(a, b, (((0,), (0,)), ((), ())), preferred_element_type=F32)


def _pick(tp, pref):
    return pref if tp % pref == 0 else TM


def _params(sem, vmem_mib):
    return pltpu.CompilerParams(dimension_semantics=sem, vmem_limit_bytes=vmem_mib * MIB)


def _load_once(first, pairs, sems):
    @pl.when(first)
    def _():
        cps = [pltpu.make_async_copy(s, d, sems.at[k]) for k, (s, d) in enumerate(pairs)]
        for cp in cps:
            cp.start()
        for cp in cps:
            cp.wait()


def _place():
    x, y, c = lax.axis_index("x"), lax.axis_index("y"), lax.axis_index("c")
    return x, y, c


class _Gather:
    def __init__(self, groups, dtypes):
        self.groups, self.dtypes, self.n = groups, dtypes, len(groups)
        self.arrays = [a for _, parts in groups for a, _, _ in parts]
        self.out_shape = [jax.ShapeDtypeStruct((NDEV, *s), dt) for (s, _), dt in zip(groups, dtypes)]
        self.scratch = [pltpu.VMEM(s, dt) for (s, _), dt in zip(groups, dtypes)] + [
            pltpu.SemaphoreType.DMA((7 * self.n,)), pltpu.SemaphoreType.DMA((7 * self.n,)),
            pltpu.SemaphoreType.DMA((self.n,))]

    def bind(self, ins, outs, scratch):
        self.ins, self.outs, self.stages = ins, outs, scratch[:self.n]
        self.send_sems, self.recv_sems, self.local_sems = scratch[self.n:]
        return self

    def _copy(self, w, k, block, to, src=None):
        dst = self.outs[w].at[4 * block[0] + 2 * block[1] + block[2]]
        return pltpu.make_async_remote_copy(
            src_ref=dst if src is None else src, dst_ref=dst,
            send_sem=self.send_sems.at[7 * w + k], recv_sem=self.recv_sems.at[7 * w + k],
            device_id=to, device_id_type=MESH_ID)

    def _first(self):
        x, y, c = _place()
        me, sibling = (x, y, c), (x, y, 1 - c)
        chips = [(1 - x, y), (x, 1 - y), (1 - x, 1 - y)]
        mine, first = [], []
        for w in range(self.n):
            mine.append(pltpu.make_async_copy(self.stages[w], self.outs[w].at[4 * x + 2 * y + c], self.local_sems.at[w]))
            first.append(self._copy(w, 0, me, sibling, src=self.stages[w]))
            first += [self._copy(w, 1 + j, me, (*chip, c), src=self.stages[w]) for j, chip in enumerate(chips)]
        return mine, first

    def _passed(self):
        x, y, c = _place()
        chips = [(1 - x, y), (x, 1 - y), (1 - x, 1 - y)]
        return [self._copy(w, 4 + j, (*chip, c), (x, y, 1 - c)) for w in range(self.n) for j, chip in enumerate(chips)]

    def issue(self):
        a = 0
        for w in range(self.n):
            shape, parts = self.groups[w]
            if sum(arr.size for arr, _, _ in parts) < math.prod(shape):
                self.stages[w][...] = jnp.zeros(shape, self.dtypes[w])
            for _, dst, src in parts:
                self.stages[w][dst] = self.ins[a][src].astype(self.dtypes[w])
                a += 1
        mine, first = self._first()
        for cp in mine + first:
            cp.start()

    def forward(self):
        x, y, c = _place()
        chips = [(1 - x, y), (x, 1 - y), (1 - x, 1 - y)]
        passed = self._passed()
        for w in range(self.n):
            for j, chip in enumerate(chips):
                self._copy(w, 1 + j, (*chip, c), (x, y, c)).wait_recv()
                passed[3 * w + j].start()

    def finish(self):
        x, y, c = _place()
        chips = [(1 - x, y), (x, 1 - y), (1 - x, 1 - y)]
        for w in range(self.n):
            self._copy(w, 0, (x, y, 1 - c), (x, y, c)).wait_recv()
            for j, chip in enumerate(chips):
                self._copy(w, 4 + j, (*chip, 1 - c), (x, y, c)).wait_recv()
        mine, first = self._first()
        for cp in first + self._passed():
            cp.wait_send()
        for cp in mine:
            cp.wait()


class _ChipExchange:
    def __init__(self, qs):
        self.n = len(qs)
        self.out_shape = [jax.ShapeDtypeStruct(q.shape, q.dtype) for q in qs]
        self.scratch = [pltpu.SemaphoreType.DMA((3 * self.n,)), pltpu.SemaphoreType.DMA((3 * self.n,))]

    def bind(self, qs, rels, scratch):
        self.qs, self.rels = qs, rels
        self.send_sems, self.recv_sems = scratch
        return self

    def _copies(self):
        x, y, c = _place()
        chips = [(1 - x, y), (x, 1 - y), (1 - x, 1 - y)]
        return [pltpu.make_async_remote_copy(
            src_ref=self.qs[w].at[j], dst_ref=self.rels[w].at[j],
            send_sem=self.send_sems.at[3 * w + j], recv_sem=self.recv_sems.at[3 * w + j],
            device_id=(*chips[j], c), device_id_type=MESH_ID) for w in range(self.n) for j in range(3)]

    def issue(self):
        for cp in self._copies():
            cp.start()

    def finish(self):
        cps = self._copies()
        for cp in cps:
            cp.wait_recv()
        for cp in cps:
            cp.wait_send()


def _reduce_scatter(parts, small):
    n = len(parts)
    blks = [p.shape[1:] for p in parts]

    def body(*refs):
        ps, small_ref = refs[:n], refs[n]
        o = n + 1
        owns, sibs, rels, small_out = refs[o:o + n], refs[o + n:o + 2 * n], refs[o + 2 * n:o + 3 * n], refs[o + 3 * n]
        o += 3 * n + 1
        pa, pb, qst = refs[o:o + n], refs[o + n:o + 2 * n], refs[o + 2 * n:o + 3 * n]
        s1_send, s1_recv, s2_send, s2_recv, sm_send, sm_recv, lsem = refs[o + 3 * n:]
        x, y, c = _place()
        me = 4 * x + 2 * y + c
        sibling = (x, y, 1 - c)
        chips = [(1 - x, y), (x, 1 - y), (1 - x, 1 - y)]
        all_chips = [(x, y)] + chips

        own_cps = []
        for w in range(n):
            cp = pltpu.make_async_copy(ps[w].at[me], owns[w], lsem.at[w])
            cp.start()
            own_cps.append(cp)
        sm_own = pltpu.make_async_copy(small_ref, small_out.at[me], lsem.at[n])
        sm_own.start()

        def small_copy(r):
            peer = ((x + (r >> 2)) % 2, (y + ((r >> 1) & 1)) % 2, (c + (r & 1)) % 2)
            return pltpu.make_async_remote_copy(
                src_ref=small_ref, dst_ref=small_out.at[me], send_sem=sm_send.at[r - 1], recv_sem=sm_recv.at[r - 1],
                device_id=peer, device_id_type=MESH_ID)

        sm_cps = [small_copy(r) for r in range(1, NDEV)]
        for cp in sm_cps:
            cp.start()

        def pair_copy(w, rel):
            cx, cy = all_chips[rel]
            return pltpu.make_async_remote_copy(
                src_ref=ps[w].at[4 * cx + 2 * cy + (1 - c)], dst_ref=sibs[w].at[rel],
                send_sem=s1_send.at[4 * w + rel], recv_sem=s1_recv.at[4 * w + rel],
                device_id=sibling, device_id_type=MESH_ID)

        def chip_copy(w, j):
            return pltpu.make_async_remote_copy(
                src_ref=qst[w].at[j], dst_ref=rels[w].at[j],
                send_sem=s2_send.at[3 * w + j], recv_sem=s2_recv.at[3 * w + j],
                device_id=(*chips[j], c), device_id_type=MESH_ID)

        pair_cps = [pair_copy(w, rel) for w in range(n) for rel in (1, 2, 3, 0)]
        for cp in pair_cps:
            cp.start()
        chip_cps = []
        for w in range(n):
            for j, (cx, cy) in enumerate(chips):
                pair_copy(w, 1 + j).wait_recv()
                la = pltpu.make_async_copy(ps[w].at[4 * cx + 2 * cy + c], pa[w], lsem.at[n + 1])
                lb = pltpu.make_async_copy(sibs[w].at[1 + j], pb[w], lsem.at[n + 2])
                la.start()
                lb.start()
                la.wait()
                lb.wait()
                qst[w][j] = (pa[w][...].astype(F32) + pb[w][...].astype(F32)).astype(BF16)
                cp = chip_copy(w, j)
                cp.start()
                chip_cps.append(cp)
        for w in range(n):
            pair_copy(w, 0).wait_recv()
            for j in range(3):
                chip_copy(w, j).wait_recv()
        for cp in sm_cps:
            cp.wait_recv()
        for cp in pair_cps + chip_cps + sm_cps:
            cp.wait_send()
        for cp in own_cps:
            cp.wait()
        sm_own.wait()

    any_spec = pl.BlockSpec(memory_space=pl.ANY)
    outs = pl.pallas_call(
        body, name="rs_grads",
        out_shape=[jax.ShapeDtypeStruct(b, BF16) for b in blks]
        + [jax.ShapeDtypeStruct((4, *b), BF16) for b in blks]
        + [jax.ShapeDtypeStruct((3, *b), BF16) for b in blks]
        + [jax.ShapeDtypeStruct((NDEV, *small.shape), F32)],
        in_specs=[any_spec] * (n + 1),
        out_specs=[any_spec] * (3 * n + 1),
        scratch_shapes=[pltpu.VMEM(b, BF16) for b in blks] + [pltpu.VMEM(b, BF16) for b in blks]
        + [pltpu.VMEM((3, *b), BF16) for b in blks]
        + [pltpu.SemaphoreType.DMA((4 * n,)), pltpu.SemaphoreType.DMA((4 * n,)),
           pltpu.SemaphoreType.DMA((3 * n,)), pltpu.SemaphoreType.DMA((3 * n,)),
           pltpu.SemaphoreType.DMA((NDEV - 1,)), pltpu.SemaphoreType.DMA((NDEV - 1,)),
           pltpu.SemaphoreType.DMA((n + 3,))],
        compiler_params=pltpu.CompilerParams(vmem_limit_bytes=40 * MIB),
    )(*parts, small)
    return outs[:n], outs[n:2 * n], outs[2 * n:3 * n], outs[3 * n]


def _rs_pair(name, parts):
    n = len(parts)
    blks = [p.shape[1:] for p in parts]

    def body(*refs):
        ps = refs[:n]
        owns, sibs, qs = refs[n:2 * n], refs[2 * n:3 * n], refs[3 * n:4 * n]
        pa, pb, qst = refs[4 * n:5 * n], refs[5 * n:6 * n], refs[6 * n:7 * n]
        s_send, s_recv, lsem = refs[7 * n:]
        x, y, c = _place()
        chips = [(1 - x, y), (x, 1 - y), (1 - x, 1 - y)]

        own_cps = [pltpu.make_async_copy(ps[w].at[4 * x + 2 * y + c], owns[w], lsem.at[w]) for w in range(n)]
        mine = [[pltpu.make_async_copy(ps[w].at[4 * cx + 2 * cy + c], pa[w].at[j], lsem.at[2 * n + 3 * w + j])
                 for j, (cx, cy) in enumerate(chips)] for w in range(n)]
        for cp in own_cps + [cp for row in mine for cp in row]:
            cp.start()

        def pair_copy(w, rel):
            cx, cy = (x, y) if rel == 0 else chips[rel - 1]
            return pltpu.make_async_remote_copy(
                src_ref=ps[w].at[4 * cx + 2 * cy + (1 - c)], dst_ref=sibs[w].at[0] if rel == 0 else pb[w].at[rel - 1],
                send_sem=s_send.at[4 * w + rel], recv_sem=s_recv.at[4 * w + rel],
                device_id=(x, y, 1 - c), device_id_type=MESH_ID)

        pair_cps = [pair_copy(w, rel) for w in range(n) for rel in (1, 2, 3, 0)]
        for cp in pair_cps:
            cp.start()
        q_cps = []
        for w in range(n):
            for j in range(3):
                pair_copy(w, 1 + j).wait_recv()
                mine[w][j].wait()
                qst[w][j] = (pa[w][j].astype(F32) + pb[w][j].astype(F32)).astype(BF16)
            cp = pltpu.make_async_copy(qst[w], qs[w], lsem.at[n + w])
            cp.start()
            q_cps.append(cp)
        for w in range(n):
            pair_copy(w, 0).wait_recv()
        for cp in pair_cps:
            cp.wait_send()
        for cp in own_cps + q_cps:
            cp.wait()

    any_spec = pl.BlockSpec(memory_space=pl.ANY)
    outs = pl.pallas_call(
        body, name=name,
        out_shape=[jax.ShapeDtypeStruct(b, BF16) for b in blks]
        + [jax.ShapeDtypeStruct((1, *b), BF16) for b in blks]
        + [jax.ShapeDtypeStruct((3, *b), BF16) for b in blks],
        in_specs=[any_spec] * n,
        out_specs=[any_spec] * (3 * n),
        scratch_shapes=[pltpu.VMEM((3, *b), BF16) for b in blks] * 3
        + [pltpu.SemaphoreType.DMA((4 * n,)), pltpu.SemaphoreType.DMA((4 * n,)), pltpu.SemaphoreType.DMA((5 * n,))],
        compiler_params=pltpu.CompilerParams(vmem_limit_bytes=48 * MIB),
    )(*parts)
    return outs[:n], outs[n:2 * n], outs[2 * n:3 * n]


def _adamw_math(g, w, m, v):
    m = ADAM_B1 * m + (1.0 - ADAM_B1) * g
    v = ADAM_B2 * v + (1.0 - ADAM_B2) * (g * g)
    m_hat = m / (1.0 - ADAM_B1 ** ADAM_STEP)
    v_hat = v / (1.0 - ADAM_B2 ** ADAM_STEP)
    delta = -ADAM_LR * (m_hat / (jnp.sqrt(v_hat) + ADAM_EPS) + ADAM_WD * w)
    return delta, m, v


def _adamw_multi(name, own, sib, rel, ws, ms, vs, row_grid):
    k_n, r_n, c_n = own.shape
    rbk = r_n // row_grid

    def body(*refs):
        own_ref, sib_ref, r0_ref, r1_ref, r2_ref = refs[:5]
        w_refs, m_refs, v_refs = refs[5:5 + k_n], refs[5 + k_n:5 + 2 * k_n], refs[5 + 2 * k_n:5 + 3 * k_n]
        outs = refs[5 + 3 * k_n:]
        for k in range(k_n):
            g = own_ref[k].astype(F32) + sib_ref[k].astype(F32)
            g = g + r0_ref[k].astype(F32)
            g = g + r1_ref[k].astype(F32)
            g = g + r2_ref[k].astype(F32)
            delta, mm, vv = _adamw_math(g, w_refs[k][0], m_refs[k][0], v_refs[k][0])
            outs[4 * k][0] = g
            outs[4 * k + 1][0] = delta
            outs[4 * k + 2][0] = mm
            outs[4 * k + 3][0] = vv

    def lead(j):
        return pl.BlockSpec((None, k_n, rbk, c_n), lambda g: (j, 0, g, 0))

    wspec = pl.BlockSpec((1, rbk, c_n), lambda g: (0, g, 0))
    shp = jax.ShapeDtypeStruct((1, r_n, c_n), F32)
    res = pl.pallas_call(
        body, name=name, grid=(row_grid,),
        in_specs=[pl.BlockSpec((k_n, rbk, c_n), lambda g: (0, g, 0)), lead(0), lead(0), lead(1), lead(2)] + [wspec] * (3 * k_n),
        out_specs=[wspec] * (4 * k_n), out_shape=[shp] * (4 * k_n),
        compiler_params=_params(("arbitrary",), 40),
    )(own, sib, rel, rel, rel, *ws, *ms, *vs)
    return [tuple(res[4 * k:4 * k + 4]) for k in range(k_n)]


def _adamw_meta_dw(own, sib, rel, meta, dw):
    def body(own_ref, sib_ref, rel_ref, wm, mm, vm, wd, md, vd, *outs):
        def gsum(rows):
            g = own_ref[rows, :].astype(F32) + sib_ref[0, rows, :].astype(F32)
            for j in range(3):
                g = g + rel_ref[j, rows, :].astype(F32)
            return g

        g = gsum(pl.ds(0, N_META))
        delta, m2, v2 = _adamw_math(g, wm[...], mm[...], vm[...])
        for o, val in zip(outs[:4], (g, delta, m2, v2)):
            o[...] = val
        g = gsum(pl.ds(N_META, CONV_K))
        delta, m2, v2 = _adamw_math(g, wd[0], md[0], vd[0])
        for o, val in zip(outs[4:], (g, delta, m2, v2)):
            o[0] = val

    s_meta = jax.ShapeDtypeStruct(meta[0].shape, F32)
    s_dw = jax.ShapeDtypeStruct(dw[0].shape, F32)
    res = pl.pallas_call(body, name="adamw_meta_dw", out_shape=[s_meta] * 4 + [s_dw] * 4)(own, sib, rel, *meta, *dw)
    return tuple(res[:4]), tuple(res[4:])


REP_ROWS = 16


def _adamw_rep(gathered, ws, ms, vs):
    rows = [(0, 1), (1, 2), (3, 1), (4, 1), (5, 1), (6, 1), (7, 1), (8, 1)]

    def body(g_ref, *refs):
        w_refs, m_refs, v_refs = refs[:8], refs[8:16], refs[16:24]
        loss_ref, outs, acc = refs[24], refs[25:57], refs[57]
        g = g_ref[0]
        for d in range(1, NDEV):
            g = g + g_ref[d]
        acc[...] = g
        loss_ref[...] = (0.5 / D) * jnp.sum(acc[pl.ds(9, 1), :], axis=1, keepdims=True)
        for p, (r0, nr) in enumerate(rows):
            for h in range(nr):
                cols = pl.ds(h * D, D)
                gp = acc[pl.ds(r0 + h, 1), :]
                delta, mm, vv = _adamw_math(gp, w_refs[p][:, cols], m_refs[p][:, cols], v_refs[p][:, cols])
                for o, val in zip(outs[4 * p:4 * p + 4], (gp, delta, mm, vv)):
                    o[:, cols] = val

    shapes = [jax.ShapeDtypeStruct(w.shape, F32) for w in ws]
    res = pl.pallas_call(
        body, name="adamw_rep",
        out_shape=[jax.ShapeDtypeStruct((1, 1), F32)] + [s for s in shapes for _ in range(4)],
        scratch_shapes=[pltpu.VMEM((REP_ROWS, D), F32)],
    )(gathered, *ws, *ms, *vs)
    return res[0], [tuple(res[1 + 4 * p:5 + 4 * p]) for p in range(8)]


def _load_ffn(i, j, wgu_hbm, wgu, wdn_hbm, wdn, sems):
    half = NDEV // 2

    def copies(ch):
        pairs = [(wgu_hbm.at[half * ch + d, g], wgu.at[g, ch, pl.ds(FFB * d, FFB), :]) for g in range(2) for d in range(half)]
        pairs.append((wdn_hbm.at[ch], wdn.at[ch]))
        return [pltpu.make_async_copy(s, t, sems.at[(2 * half + 1) * ch + k]) for k, (s, t) in enumerate(pairs)]

    @pl.when((i == 0) & (j == 0))
    def _():
        for cp in copies(0) + copies(1):
            cp.start()

    for ch in range(2):
        @pl.when((i == 0) & (j == ch))
        def _():
            for cp in copies(ch):
                cp.wait()


def _win_pairs(w_hbm, w_vm):
    return [(w_hbm.at[q], w_vm.at[q // 2, :, pl.ds(2 * INB * (q % 2), 2 * INB)]) for q in range(4)]


def _whole(a):
    nd = a.ndim
    return pl.BlockSpec(a.shape, lambda *g: (0,) * nd)


CHIPW = 2 * INB
PHASE_CHIP = (1, 0, 2)


class _GatherIn:
    scratch = [pltpu.VMEM((D, INB), BF16), pltpu.SemaphoreType.DMA((7,)), pltpu.SemaphoreType.DMA((7,)),
               pltpu.SemaphoreType.DMA((1,))]

    def bind(self, w_ref, w_vm, scratch):
        self.w_ref, self.w_vm = w_ref, w_vm
        self.stage, self.send_sems, self.recv_sems, self.local_sem = scratch
        return self

    def _win(self, chip, core):
        return self.w_vm.at[2 * chip[0] + chip[1], :, pl.ds(INB * core, INB)]

    def _copy(self, k, chip, core, to, src=None):
        dst = self._win(chip, core)
        return pltpu.make_async_remote_copy(
            src_ref=dst if src is None else src, dst_ref=dst, send_sem=self.send_sems.at[k],
            recv_sem=self.recv_sems.at[k], device_id=to, device_id_type=MESH_ID)

    def _mine(self, cs):
        x, y, _ = _place()
        return pltpu.make_async_copy(self.stage, self._win((x, y), cs), self.local_sem.at[0])

    def issue(self, cs):
        x, y, _ = _place()
        chips = [(1 - x, y), (x, 1 - y), (1 - x, 1 - y)]
        self.stage[...] = self.w_ref[0].astype(BF16)
        self._mine(cs).start()
        self._copy(0, (x, y), cs, (x, y, 1 - cs), src=self.stage).start()
        for j, chip in enumerate(chips):
            self._copy(1 + j, (x, y), cs, (*chip, cs), src=self.stage).start()

    def wait_chip(self, phase, cs):
        x, y, _ = _place()
        chips = [(1 - x, y), (x, 1 - y), (1 - x, 1 - y)]
        if phase == 0:
            self._mine(cs).wait()
            self._copy(0, (x, y), 1 - cs, (x, y, cs)).wait_recv()
            return
        if phase == 1:
            for j in PHASE_CHIP:
                self._copy(1 + j, chips[j], cs, (x, y, cs)).wait_recv()
                self._copy(4 + j, chips[j], cs, (x, y, 1 - cs)).start()
        j = PHASE_CHIP[phase - 1]
        self._copy(4 + j, chips[j], 1 - cs, (x, y, cs)).wait_recv()

    def finish(self, cs):
        x, y, _ = _place()
        for k in range(7):
            self._copy(k, (x, y), cs, (x, y, cs), src=self.stage).wait_send()


def _fwd_in(x2, g_mix, w_in, order, tp, ag, ags):
    tm = _pick(tp, TM_IO)
    nt = tp // tm
    nx_last = x2.shape[0] - (nt - 1) * tm
    na, ng, ns = len(ag.arrays), ag.n, len(ags.arrays)
    gin = _GatherIn()

    def body(order_ref, *refs):
        x_ref, g_ref, w_ref = refs[:3]
        o = 3 + na + ns
        h_ref, z_ref, u_ref, wout_ref = refs[o:o + 4]
        s = o + 4 + ng + 1
        w_vm, u_all, osem, sm_vm = refs[s:s + 4]
        gin.bind(w_ref, w_vm, refs[s + 4:s + 8])
        ag.bind(refs[3:3 + na], refs[o + 4:o + 4 + ng], refs[s + 8:s + 8 + len(ag.scratch)])
        ags.bind(refs[3 + na:3 + na + ns], refs[o + 4 + ng:o + 5 + ng], refs[s + 8 + len(ag.scratch):])
        ph, i = pl.program_id(0), pl.program_id(1)
        core = lax.axis_index("c")
        first = (ph == 0) & (i == 0)
        last = (ph == 3) & (i == nt - 1)
        for cs in range(2):
            @pl.when(first & (core == cs))
            def _():
                gin.issue(cs)

        @pl.when(first)
        def _():
            ags.issue()
            ag.issue()

        @pl.when((ph == 0) & (i == max(nt - 2, 0)))
        def _():
            ags.forward()

        for cs in range(2):
            for p in range(4):
                @pl.when((ph == p) & (i == 0) & (core == cs))
                def _():
                    gin.wait_chip(p, cs)

        @pl.when((ph == 3) & (i == max(nt - 2, 0)))
        def _():
            ag.forward()

        out_copy = pltpu.make_async_copy(w_vm, wout_ref, osem.at[0])

        @pl.when((ph == 3) & (i == 0))
        def _():
            out_copy.start()

        @pl.when((ph == 0) & (i < nt - 1))
        def _():
            h_ref[...] = x_ref[...]

        @pl.when((ph == 0) & (i == nt - 1))
        def _():
            ags.finish()
            cp = pltpu.make_async_copy(ags.outs[0], sm_vm, osem.at[1])
            cp.start()
            h_ref[pl.ds(0, nx_last), :] = x_ref[pl.ds(0, nx_last), :]
            h_ref[pl.ds(nx_last, tm - nx_last - N_META), :] = jnp.zeros((tm - nx_last - N_META, D), F32)
            cp.wait()
            for d in range(NDEV):
                h_ref[pl.ds(tm - N_META, N_META), pl.ds(128 * d, 128)] = sm_vm[d, pl.ds(0, N_META), :]

        @pl.when(ph == 0)
        def _():
            xv = h_ref[...]
            r = lax.rsqrt(jnp.mean(xv * xv, axis=-1, keepdims=True) + RMS_EPS)
            u = (xv * r * g_ref[...]).astype(BF16)
            u_ref[...] = u
            u_all[i] = u

        z_ref[...] = _dot(u_all[i], w_vm[order_ref[ph]])

        @pl.when(last)
        def _():
            ag.finish()
            out_copy.wait()

        for cs in range(2):
            @pl.when(last & (core == cs))
            def _():
                gin.finish(cs)

    def rows(ph, i, order):
        return (jnp.where(ph == 0, i, nt - 1), 0)

    tile = pl.BlockSpec((tm, D), rows)
    anys = pl.BlockSpec(memory_space=pl.ANY)
    res = pl.pallas_call(
        body, name="fwd_in",
        grid_spec=pltpu.PrefetchScalarGridSpec(
            num_scalar_prefetch=1, grid=(4, nt),
            in_specs=[tile, pl.BlockSpec((1, D), lambda ph, i, order: (0, 0)), _whole(w_in)]
            + [_whole(a) for a in ag.arrays + ags.arrays],
            out_specs=[tile, pl.BlockSpec((tm, CHIPW), lambda ph, i, order: (i, order[ph])), tile, anys] + [anys] * (ng + 1),
            scratch_shapes=[pltpu.VMEM((4, D, CHIPW), BF16), pltpu.VMEM((nt, tm, D), BF16), pltpu.SemaphoreType.DMA((2,)),
                            pltpu.VMEM(ags.out_shape[0].shape, F32)] + gin.scratch + ag.scratch + ags.scratch),
        out_shape=[jax.ShapeDtypeStruct((tp, D), F32), jax.ShapeDtypeStruct((tp, DIN), F32),
                   jax.ShapeDtypeStruct((tp, D), BF16), jax.ShapeDtypeStruct((4, D, CHIPW), BF16)]
        + ag.out_shape + ags.out_shape,
        compiler_params=_params(("arbitrary", "arbitrary"), 58),
    )(order, x2, g_mix, w_in, *ag.arrays, *ags.arrays)
    return res[:4], res[4:4 + ng], res[4 + ng]


def _halo_specs(col, nt, width=D):
    r = TM // HALO
    nb = nt * r
    return [pl.BlockSpec((HALO, width), lambda i: ((i * r + nb - 1) % nb, col)),
            pl.BlockSpec((TM, width), lambda i: (i, col)),
            pl.BlockSpec((HALO, width), lambda i: (((i + 1) * r) % nb, col))]


NCB = D // 128
TME = TM + 2 * HALO


def _tm_fill(dst, time0, groups, tile_fn):
    def body(g, c):
        for j in range(NCB):
            dst[pl.ds((time0 + 8 * g) * NCB + j, 8, stride=NCB), :] = tile_fn(pl.multiple_of(8 * g, 8), pl.ds(128 * j, 128))
        return c

    lax.fori_loop(0, groups, body, 0)


def _tm_fill_ext(dst, left, cur, right, fn):
    _tm_fill(dst, 0, HALO // 8, lambda r, l: fn(left, pl.ds(r, 8), l))
    _tm_fill(dst, HALO, TM // 8, lambda r, l: fn(cur, pl.ds(r, 8), l))
    _tm_fill(dst, HALO + TM, HALO // 8, lambda r, l: fn(right, pl.ds(r, 8), l))


def _tm_read(src, groups, store_fn):
    def body(g, c):
        for j in range(NCB):
            store_fn(pl.ds(pl.multiple_of(8 * g, 8), 8), pl.ds(128 * j, 128), src[pl.ds(8 * g * NCB + j, 8, stride=NCB), :])
        return c

    lax.fori_loop(0, groups, body, 0)


def _tm_rows(t):
    return pl.ds(t * NCB if isinstance(t, int) else pl.multiple_of(t * NCB, NCB), NCB)


def _tm_at(ref, t):
    return ref[_tm_rows(t), :]


def _by_group(sub, vals):
    return jnp.where(sub < 2, vals[0], jnp.where(sub < 4, vals[1], jnp.where(sub < 6, vals[2], vals[3])))


def _pool_cnt(b, seq, tp, sub):
    b = jnp.where(b < 0, b + tp, b)
    b = jnp.where(b >= tp, b - tp, b)
    t = jnp.where(b < seq, b + N_META, b - (tp - N_META))
    cnts = []
    for win in POOL_WINDOWS:
        left = win // 2
        lo = jnp.maximum(t - left, 0)
        hi = jnp.minimum(t + win - left, seq + N_META)
        cnts.append(jnp.maximum(hi - lo, 1).astype(F32))
    return _by_group(sub, cnts)


def _edge_rows(seq, tp):
    reach = max(POOL_WINDOWS) // 2
    return [tp - N_META + t for t in range(reach)] + [seq - reach + 1 + t for t in range(reach - 1)]


def _edge_gain(b, seq, tp, sub):
    return _by_group(sub, [float(w) for w in POOL_WINDOWS]) / _pool_cnt(b, seq, tp, sub)


def _nested_windows(at, lo_offs):
    sums, s, have = [], None, set()
    for g, win in enumerate(POOL_WINDOWS):
        for o in range(lo_offs[g], lo_offs[g] + win):
            if o not in have:
                have.add(o)
                s = at(o) if s is None else s + at(o)
        sums.append(s)
    return sums


def _seq_fwd(z, w_dw, b_dw, seq, gat):
    tp = z.shape[0]
    nt = tp // TM
    na, ng = len(gat.arrays), gat.n

    def body(*refs):
        av_l, av, av_r, ag_l, ag, ag_r, p_l, p, p_r, w_ref, b_ref = refs[:11]
        ac_ref, m_ref = refs[11 + na:13 + na]
        a3, p3, o3, m3, w3, b3, m2d = refs[13 + na + ng:20 + na + ng]
        gat.bind(refs[11:11 + na], refs[13 + na:13 + na + ng], refs[20 + na + ng:])
        i = pl.program_id(0)
        sub = lax.broadcasted_iota(jnp.int32, (NCB, 128), 0)

        @pl.when(i == 0)
        def _():
            gat.issue()
            _tm_fill(w3, 0, 4, lambda r, l: w_ref[pl.ds(r, 8), l])
            for j in range(NCB):
                b3[pl.ds(j, 1), :] = b_ref[:, pl.ds(128 * j, 128)]

        @pl.when(i == max(nt - 2, 0))
        def _():
            gat.forward()

        _tm_fill_ext(a3, (av_l, ag_l), (av, ag), (av_r, ag_r), lambda vg, r, l: vg[0][r, l] * _sig(vg[1][r, l]))
        _tm_fill_ext(p3, p_l, p, p_r, lambda ref, r, l: ref[r, l])

        def conv(g, c):
            for t in range(8):
                acc = b3[...]
                for k in range(CONV_K):
                    acc = acc + _tm_at(w3, k) * _tm_at(a3, 8 * g + t + k + 1)
                o3[_tm_rows(8 * g + t), :] = acc
            return c

        lax.fori_loop(0, TM // 8, conv, 0)
        _tm_read(o3, TM // 8, lambda r, l, tile: ac_ref.__setitem__((r, l), tile))

        inv = _by_group(sub, [1.0 / w for w in POOL_WINDOWS])

        def pool(g, c):
            for t in range(8):
                e = 8 * g + t + HALO
                sums = _nested_windows(lambda o: _tm_at(p3, e + o), [-(w // 2) for w in POOL_WINDOWS])
                m3[_tm_rows(8 * g + t), :] = _by_group(sub, sums) * inv - _tm_at(p3, e)
            return c

        lax.fori_loop(0, TM // 8, pool, 0)
        for b in _edge_rows(seq, tp):
            r = b - i * TM

            @pl.when((r >= 0) & (r < TM))
            def _():
                pv = _tm_at(p3, r + HALO)
                m3[_tm_rows(r), :] = (_tm_at(m3, r) + pv) * _edge_gain(b, seq, tp, sub) - pv

        _tm_read(m3, TM // 8, lambda r, l, tile: m2d.__setitem__((r, l), tile))
        m_ref[...] = m2d[...].astype(BF16)

        @pl.when(i == nt - 1)
        def _():
            gat.finish()

    tmaj = pltpu.VMEM((TM * NCB, 128), F32)
    text = pltpu.VMEM((TME * NCB, 128), F32)
    res = pl.pallas_call(
        body, name="seq_fwd", grid=(nt,),
        in_specs=_halo_specs(0, nt) + _halo_specs(1, nt) + _halo_specs(2, nt)
        + [pl.BlockSpec((32, D), lambda i: (0, 0)), pl.BlockSpec((1, D), lambda i: (0, 0))] + [_whole(a) for a in gat.arrays],
        out_specs=[pl.BlockSpec((TM, D), lambda i: (i, 0))] * 2 + [pl.BlockSpec(memory_space=pl.ANY)] * ng,
        out_shape=[jax.ShapeDtypeStruct((tp, D), F32), jax.ShapeDtypeStruct((tp, D), BF16)] + gat.out_shape,
        scratch_shapes=[text, text, tmaj, tmaj, pltpu.VMEM((32 * NCB, 128), F32), pltpu.VMEM((NCB, 128), F32),
                        pltpu.VMEM((TM, D), F32)] + gat.scratch,
        compiler_params=_params(("arbitrary",), 52),
    )(z, z, z, z, z, z, z, z, z, w_dw, b_dw, *gat.arrays)
    return res[:2], res[2:]


def _ln_stats(ac):
    mu = jnp.mean(ac, axis=-1, keepdims=True)
    xc = ac - mu
    rl = lax.rsqrt(jnp.mean(xc * xc, axis=-1, keepdims=True) + LN_EPS)
    return xc * rl, rl


def _pool_mix(m, wp_ref):
    return jnp.concatenate(
        [_dot(m[:, g * PG:(g + 1) * PG], wp_ref[:, g].reshape(PG, PG)) for g in range(4)], axis=1)


def _mix_fwd(ac, m, z, h0, b_gate, ln_g, ln_b, pool_scale, g_mixw, g_pool, gat):
    tp = h0.shape[0]
    nt = tp // TMS
    na, ng = len(gat.arrays), gat.n

    def body(*refs):
        ac_ref, m_ref, zga, zgb, h_ref, bg_ref, lg_ref, lb_ref, ps_ref, wm_hbm, wp_hbm = refs[:11]
        h1_ref, s_ref, yc_ref, yp_ref, mg_ref, q_ref = refs[11 + na:17 + na]
        wm, wp, sems = refs[17 + na + ng:20 + na + ng]
        gat.bind(refs[11:11 + na], refs[17 + na:17 + na + ng], refs[20 + na + ng:])
        i = pl.program_id(0)

        @pl.when(i == 0)
        def _():
            gat.issue()

        @pl.when(i == max(nt - 4, 0))
        def _():
            gat.forward()

        @pl.when(i == nt - 1)
        def _():
            gat.finish()

        _load_once(i == 0, [(wm_hbm, wm), (wp_hbm, wp)], sems)
        n, _ = _ln_stats(ac_ref[...])
        l = n * lg_ref[...] + lb_ref[...]
        s = (l * _sig(l)).astype(BF16)
        s_ref[...] = s
        yc = _dot(s, wm[:, 0].reshape(D, D))
        q = (_pool_mix(m_ref[...], wp) * ps_ref[...]).astype(BF16)
        q_ref[...] = q
        yp = _dot(q, wm[:, 1].reshape(D, D))
        ga = _sig(zga[...] + bg_ref[:, :D])
        gb = _sig(zgb[...] + bg_ref[:, D:])
        merged = (ga * yc + gb * yp).astype(BF16)
        yc_ref[...] = yc
        yp_ref[...] = yp
        mg_ref[...] = merged
        h1_ref[...] = h_ref[...] + _dot(merged, wm[:, 2].reshape(D, D))

    def tile(col=0):
        return pl.BlockSpec((TMS, D), lambda i: (i, col))

    def vec(w):
        return pl.BlockSpec((1, w), lambda i: (0, 0))

    anys = pl.BlockSpec(memory_space=pl.ANY)
    f32o, b16o = jax.ShapeDtypeStruct((tp, D), F32), jax.ShapeDtypeStruct((tp, D), BF16)
    res = pl.pallas_call(
        body, name="mix_fwd", grid=(nt,),
        in_specs=[tile(), tile(), tile(3), tile(4), tile(), vec(2 * D), vec(D), vec(D), vec(D), anys, anys]
        + [_whole(a) for a in gat.arrays],
        out_specs=[tile()] * 6 + [anys] * ng,
        out_shape=[f32o, b16o, f32o, f32o, b16o, b16o] + gat.out_shape,
        scratch_shapes=[pltpu.VMEM((NDEV, 3, D // NDEV, D), BF16), pltpu.VMEM((NDEV, 4, PG // NDEV, PG), BF16),
                        pltpu.SemaphoreType.DMA((2,))] + gat.scratch,
        compiler_params=_params(("arbitrary",), 52),
    )(ac, m, z, z, h0, b_gate, ln_g, ln_b, pool_scale, g_mixw, g_pool, *gat.arrays)
    return res[:6], res[6:]


def _ffn_fwd(h1, tgt, g_ffn, g_final, w_gu, w_dn):
    tp = h1.shape[0]
    nt = tp // TM
    nx_last = tgt.shape[0] - (nt - 1) * TM

    def body(h_ref, t_ref, gf_ref, gl_ref, wgu_hbm, wdn_hbm,
             fg_ref, fu_ref, v_ref, f_ref, dh2_ref, acc_ref, wgu, wdn, v_sc, h2_sc, diff_sc, sems):
        i, j = pl.program_id(0), pl.program_id(1)
        _load_ffn(i, j, wgu_hbm, wgu, wdn_hbm, wdn, sems)

        @pl.when((i == 0) & (j == 0))
        def _():
            acc_ref[...] = jnp.zeros_like(acc_ref)

        @pl.when(j == 0)
        def _():
            h = h_ref[...]
            r = lax.rsqrt(jnp.mean(h * h, axis=-1, keepdims=True) + RMS_EPS)
            v = (h * r * gf_ref[...]).astype(BF16)
            v_sc[...] = v
            v_ref[...] = v
            h2_sc[...] = h

        v = v_sc[...]
        fg = _dot_nt(v, wgu[0, j])
        fu = _dot_nt(v, wgu[1, j])
        fg_ref[...] = fg
        fu_ref[...] = fu
        f = ((fg * _sig(fg)) * fu).astype(BF16)
        f_ref[...] = f
        h2_sc[...] += _dot(f, wdn[j])

        @pl.when(j == 1)
        def _():
            h2 = h2_sc[...]
            r = lax.rsqrt(jnp.mean(h2 * h2, axis=-1, keepdims=True) + RMS_EPS)
            n2 = h2 * r
            y = n2 * gl_ref[...]

            @pl.when(i < nt - 1)
            def _():
                diff_sc[...] = y - t_ref[...]

            @pl.when(i == nt - 1)
            def _():
                diff_sc[pl.ds(0, nx_last), :] = y[:nx_last] - t_ref[pl.ds(0, nx_last), :]
                diff_sc[pl.ds(nx_last, TM - nx_last), :] = jnp.zeros((TM - nx_last, D), F32)

            diff = diff_sc[...]
            dy = diff * (1.0 / D)
            acc_ref[0:1, :] += jnp.sum(diff * diff, axis=0, keepdims=True)
            acc_ref[1:2, :] += jnp.sum(dy * n2, axis=0, keepdims=True)
            dn = dy * gl_ref[...]
            dh2_ref[...] = r * (dn - n2 * jnp.mean(dn * n2, axis=-1, keepdims=True))

    def tile():
        return pl.BlockSpec((TM, D), lambda i, j: (i, 0))

    def chunk():
        return pl.BlockSpec((TM, FFC), lambda i, j: (i, j))

    def vec():
        return pl.BlockSpec((1, D), lambda i, j: (0, 0))

    anys = pl.BlockSpec(memory_space=pl.ANY)
    hid32, hid16 = jax.ShapeDtypeStruct((tp, DFF), F32), jax.ShapeDtypeStruct((tp, DFF), BF16)
    return pl.pallas_call(
        body, name="ffn_fwd", grid=(nt, 2),
        in_specs=[tile(), tile(), vec(), vec(), anys, anys],
        out_specs=[chunk(), chunk(), tile(), chunk(), tile(), pl.BlockSpec((8, D), lambda i, j: (0, 0))],
        out_shape=[hid32, hid32, jax.ShapeDtypeStruct((tp, D), BF16), hid16, jax.ShapeDtypeStruct((tp, D), F32),
                   jax.ShapeDtypeStruct((8, D), F32)],
        scratch_shapes=[pltpu.VMEM((2, 2, FFC, D), BF16), pltpu.VMEM((2, FFC, D), BF16),
                        pltpu.VMEM((TM, D), BF16), pltpu.VMEM((TM, D), F32), pltpu.VMEM((TM, D), F32),
                        pltpu.SemaphoreType.DMA((2 * NDEV + 2,))],
        compiler_params=_params(("arbitrary", "arbitrary"), 56),
    )(h1, tgt, g_ffn, g_final, w_gu, w_dn)


def _ffn_bwd(dh2, fg, fu, h1, g_ffn, w_gu, w_dn):
    tp = h1.shape[0]
    nt = tp // TM

    def body(dh2_ref, fg_ref, fu_ref, h_ref, gf_ref, wgu_hbm, wdn_hbm,
             dfg_ref, dfu_ref, dh1_ref, acc_ref, wgu, wdn, d_sc, dv_sc, sems):
        i, j = pl.program_id(0), pl.program_id(1)
        _load_ffn(i, j, wgu_hbm, wgu, wdn_hbm, wdn, sems)

        @pl.when((i == 0) & (j == 0))
        def _():
            acc_ref[...] = jnp.zeros_like(acc_ref)

        @pl.when(j == 0)
        def _():
            d_sc[...] = dh2_ref[...].astype(BF16)
            dv_sc[...] = jnp.zeros_like(dv_sc)

        df = _dot_nt(d_sc[...], wdn[j])
        fg = fg_ref[...]
        sg = _sig(fg)
        dfu = (df * (fg * sg)).astype(BF16)
        dfg = (df * fu_ref[...] * (sg * (1.0 + fg * (1.0 - sg)))).astype(BF16)
        dfg_ref[...] = dfg
        dfu_ref[...] = dfu
        dv_sc[...] += _dot(dfg, wgu[0, j]) + _dot(dfu, wgu[1, j])

        @pl.when(j == 1)
        def _():
            h = h_ref[...]
            r = lax.rsqrt(jnp.mean(h * h, axis=-1, keepdims=True) + RMS_EPS)
            n1 = h * r
            dv = dv_sc[...]
            acc_ref[0:1, :] += jnp.sum(dv * n1, axis=0, keepdims=True)
            dn = dv * gf_ref[...]
            dh1_ref[...] = dh2_ref[...] + r * (dn - n1 * jnp.mean(dn * n1, axis=-1, keepdims=True))

    def tile():
        return pl.BlockSpec((TM, D), lambda i, j: (i, 0))

    def chunk():
        return pl.BlockSpec((TM, FFC), lambda i, j: (i, j))

    anys = pl.BlockSpec(memory_space=pl.ANY)
    hid16 = jax.ShapeDtypeStruct((tp, DFF), BF16)
    return pl.pallas_call(
        body, name="ffn_bwd", grid=(nt, 2),
        in_specs=[tile(), chunk(), chunk(), tile(), pl.BlockSpec((1, D), lambda i, j: (0, 0)), anys, anys],
        out_specs=[chunk(), chunk(), tile(), pl.BlockSpec((8, D), lambda i, j: (0, 0))],
        out_shape=[hid16, hid16, jax.ShapeDtypeStruct((tp, D), F32), jax.ShapeDtypeStruct((8, D), F32)],
        scratch_shapes=[pltpu.VMEM((2, 2, FFC, D), BF16), pltpu.VMEM((2, FFC, D), BF16),
                        pltpu.VMEM((TM, D), BF16), pltpu.VMEM((TM, D), F32), pltpu.SemaphoreType.DMA((2 * NDEV + 2,))],
        compiler_params=_params(("arbitrary", "arbitrary"), 56),
    )(dh2, fg, fu, h1, g_ffn, w_gu, w_dn)


def _mix_bwd(dh1, z, yc, yp, ac, m, b_gate, ln_g, ln_b, pool_scale, g_mixw, g_pool, qs):
    tp = dh1.shape[0]
    nt = tp // TMS
    ex = _ChipExchange(qs)
    nq = ex.n

    def body(*refs):
        dh1_ref, zga, zgb, yc_ref, yp_ref, ac_ref, m_ref, bg_ref, lg_ref, lb_ref, ps_ref, wm_hbm, wp_hbm = refs[:13]
        dac_ref, dm_ref, dzg_ref, dyc_ref, dyp_ref, dm2_ref, acc_ref = refs[13 + nq:20 + nq]
        wm, wp, sems = refs[20 + 2 * nq:23 + 2 * nq]
        ex.bind(refs[13:13 + nq], refs[20 + nq:20 + 2 * nq], refs[23 + 2 * nq:])
        first = pl.program_id(0) == 0

        @pl.when(first)
        def _():
            ex.issue()
            acc_ref[...] = jnp.zeros_like(acc_ref)

        _load_once(first, [(wm_hbm, wm), (wp_hbm, wp)], sems)

        dmerged = _dot_nt(dh1_ref[...].astype(BF16), wm[:, 2].reshape(D, D))
        ga = _sig(zga[...] + bg_ref[:, :D])
        gb = _sig(zgb[...] + bg_ref[:, D:])
        dyc = dmerged * ga
        dyp = dmerged * gb
        dza = (dmerged * yc_ref[...]) * (ga * (1.0 - ga))
        dzb = (dmerged * yp_ref[...]) * (gb * (1.0 - gb))
        dzg_ref[:, :D] = dza.astype(BF16)
        dzg_ref[:, D:] = dzb.astype(BF16)
        acc_ref[0:1, :D] += jnp.sum(dza, axis=0, keepdims=True)
        acc_ref[0:1, D:] += jnp.sum(dzb, axis=0, keepdims=True)
        dyc_b = dyc.astype(BF16)
        dyp_b = dyp.astype(BF16)
        dyc_ref[...] = dyc_b
        dyp_ref[...] = dyp_b
        ds = _dot_nt(dyc_b, wm[:, 0].reshape(D, D))
        n, rl = _ln_stats(ac_ref[...])
        l = n * lg_ref[...] + lb_ref[...]
        sg = _sig(l)
        dl = ds * (sg * (1.0 + l * (1.0 - sg)))
        acc_ref[1:2, :D] += jnp.sum(dl * n, axis=0, keepdims=True)
        acc_ref[1:2, D:] += jnp.sum(dl, axis=0, keepdims=True)
        dn = dl * lg_ref[...]
        dac_ref[...] = rl * (dn - jnp.mean(dn, axis=-1, keepdims=True) - n * jnp.mean(dn * n, axis=-1, keepdims=True))
        dq = _dot_nt(dyp_b, wm[:, 1].reshape(D, D))
        mv = m_ref[...]
        acc_ref[2:3, :D] += jnp.sum(dq * _pool_mix(mv, wp), axis=0, keepdims=True)
        dm2 = (dq * ps_ref[...]).astype(BF16)
        dm2_ref[...] = dm2
        dm_ref[...] = jnp.concatenate(
            [_dot_nt(dm2[:, g * PG:(g + 1) * PG], wp[:, g].reshape(PG, PG)) for g in range(4)], axis=1)

        @pl.when(pl.program_id(0) == nt - 1)
        def _():
            ex.finish()

    def tile(col=0):
        return pl.BlockSpec((TMS, D), lambda i: (i, col))

    def vec(w):
        return pl.BlockSpec((1, w), lambda i: (0, 0))

    anys = pl.BlockSpec(memory_space=pl.ANY)
    f32o, b16o = jax.ShapeDtypeStruct((tp, D), F32), jax.ShapeDtypeStruct((tp, D), BF16)
    res = pl.pallas_call(
        body, name="mix_bwd", grid=(nt,),
        in_specs=[tile(), tile(3), tile(4), tile(), tile(), tile(), tile(), vec(2 * D), vec(D), vec(D), vec(D), anys, anys]
        + [anys] * nq,
        out_specs=[tile(), tile(), pl.BlockSpec((TMS, 2 * D), lambda i: (i, 0)), tile(), tile(), tile(),
                   pl.BlockSpec((8, 2 * D), lambda i: (0, 0))] + [anys] * nq,
        out_shape=[f32o, f32o, jax.ShapeDtypeStruct((tp, 2 * D), BF16), b16o, b16o, b16o,
                   jax.ShapeDtypeStruct((8, 2 * D), F32)] + ex.out_shape,
        scratch_shapes=[pltpu.VMEM((NDEV, 3, D // NDEV, D), BF16), pltpu.VMEM((NDEV, 4, PG // NDEV, PG), BF16),
                        pltpu.SemaphoreType.DMA((2,))] + ex.scratch,
        compiler_params=_params(("arbitrary",), 48),
    )(dh1, z, z, yc, yp, ac, m, b_gate, ln_g, ln_b, pool_scale, g_mixw, g_pool, *qs)
    return res[:7], res[7:]


def _seq_bwd(dac, dm, dzg, z, w_dw, seq, qs):
    tp = z.shape[0]
    nt = tp // TM
    ex = _ChipExchange(qs)
    nq = ex.n

    def body(*refs):
        dac_l, dac_c, dac_r, dm_l, dm_c, dm_r, av_l, av, av_r, ag_l, ag, ag_r, dzg_ref, w_ref = refs[:14]
        dz_ref, acc_ref = refs[14 + nq:16 + nq]
        a3, d3, m3, da3, dp3, w3, dw3, da_sc, dp_sc = refs[16 + 2 * nq:25 + 2 * nq]
        ex.bind(refs[14:14 + nq], refs[16 + nq:16 + 2 * nq], refs[25 + 2 * nq:])
        i = pl.program_id(0)
        sub = lax.broadcasted_iota(jnp.int32, (NCB, 128), 0)

        @pl.when(i == 0)
        def _():
            ex.issue()
            dw3[...] = jnp.zeros_like(dw3)
            _tm_fill(w3, 0, 4, lambda r, l: w_ref[pl.ds(r, 8), l])

        _tm_fill_ext(a3, (av_l, ag_l), (av, ag), (av_r, ag_r), lambda vg, r, l: vg[0][r, l] * _sig(vg[1][r, l]))
        _tm_fill_ext(d3, dac_l, dac_c, dac_r, lambda ref, r, l: ref[r, l])
        _tm_fill_ext(m3, dm_l, dm_c, dm_r, lambda ref, r, l: ref[r, l])

        def conv(g, c):
            dcur = [_tm_at(d3, 8 * g + t + HALO) for t in range(8)]
            accs = [None] * 8
            for k in range(CONV_K):
                wk = _tm_at(w3, k)
                s = None
                for t in range(8):
                    term = wk * _tm_at(d3, 8 * g + t + CONV_K - k)
                    accs[t] = term if accs[t] is None else accs[t] + term
                    pr = dcur[t] * _tm_at(a3, 8 * g + t + k + 1)
                    s = pr if s is None else s + pr
                dw3[_tm_rows(k), :] += s
            s = dcur[0]
            for t in range(1, 8):
                s = s + dcur[t]
            dw3[_tm_rows(CONV_K), :] += s
            for t in range(8):
                da3[_tm_rows(8 * g + t), :] = accs[t]
            return c

        lax.fori_loop(0, TM // 8, conv, 0)

        for b in _edge_rows(seq, tp):
            e = lax.rem(b - i * TM + HALO + tp, tp)

            @pl.when(e < TME)
            def _():
                m3[_tm_rows(e), :] = _tm_at(m3, e) * _edge_gain(b, seq, tp, sub)

        inv = _by_group(sub, [1.0 / w for w in POOL_WINDOWS])

        def pool(g, c):
            for t in range(8):
                e = 8 * g + t + HALO
                sums = _nested_windows(lambda o: _tm_at(m3, e + o), [w // 2 + 1 - w for w in POOL_WINDOWS])
                dp3[_tm_rows(8 * g + t), :] = _by_group(sub, sums) * inv
            return c

        lax.fori_loop(0, TM // 8, pool, 0)

        _tm_read(da3, TM // 8, lambda r, l, tile: da_sc.__setitem__((r, l), tile))
        _tm_read(dp3, TM // 8, lambda r, l, tile: dp_sc.__setitem__((r, l), tile))
        sg = _sig(ag[...])
        da = da_sc[...]
        dz_ref[:, 0:D] = (da * sg).astype(BF16)
        dz_ref[:, D:2 * D] = (da * av[...] * (sg * (1.0 - sg))).astype(BF16)
        dz_ref[:, 2 * D:3 * D] = (dp_sc[...] - dm_c[...]).astype(BF16)
        dz_ref[:, 3 * D:] = dzg_ref[...]

        @pl.when(i == nt - 1)
        def _():
            _tm_read(dw3, 4, lambda r, l, tile: acc_ref.__setitem__((r, l), tile))
            ex.finish()

    tmaj = pltpu.VMEM((TM * NCB, 128), F32)
    text = pltpu.VMEM((TME * NCB, 128), F32)
    taps = pltpu.VMEM((32 * NCB, 128), F32)
    anys = pl.BlockSpec(memory_space=pl.ANY)
    res = pl.pallas_call(
        body, name="seq_bwd", grid=(nt,),
        in_specs=_halo_specs(0, nt) + _halo_specs(0, nt) + _halo_specs(0, nt) + _halo_specs(1, nt)
        + [pl.BlockSpec((TM, 2 * D), lambda i: (i, 0)), pl.BlockSpec((32, D), lambda i: (0, 0))] + [anys] * nq,
        out_specs=[pl.BlockSpec((TM, DIN), lambda i: (i, 0)), pl.BlockSpec((32, D), lambda i: (0, 0))] + [anys] * nq,
        out_shape=[jax.ShapeDtypeStruct((tp, DIN), BF16), jax.ShapeDtypeStruct((32, D), F32)] + ex.out_shape,
        scratch_shapes=[text, text, text, tmaj, tmaj, taps, taps, pltpu.VMEM((TM, D), F32), pltpu.VMEM((TM, D), F32)]
        + ex.scratch,
        compiler_params=_params(("arbitrary",), 48),
    )(dac, dac, dac, dm, dm, dm, z, z, z, z, z, z, dzg, w_dw, *qs)
    return res[:2], res[2:]


def _in_bwd(dz, h0, dh1, g_mix, w_g, seq, qs):
    tp = h0.shape[0]
    tm = _pick(tp, TM_IO)
    nt = tp // tm
    ex = _ChipExchange(qs)
    nq = ex.n

    def body(*refs):
        dz_ref, h_ref, dh1_ref, g_ref, w_hbm = refs[:5]
        gx_ref, gmeta_ref, acc_ref = refs[5 + nq:8 + nq]
        w_vm, sems = refs[8 + 2 * nq:10 + 2 * nq]
        ex.bind(refs[5:5 + nq], refs[8 + nq:8 + 2 * nq], refs[10 + 2 * nq:])
        i = pl.program_id(0)

        @pl.when(i == 0)
        def _():
            ex.issue()
            acc_ref[...] = jnp.zeros_like(acc_ref)

        _load_once(i == 0, _win_pairs(w_hbm, w_vm), sems)

        du = _dot_nt(dz_ref[:, :DIN // 2], w_vm[0]) + _dot_nt(dz_ref[:, DIN // 2:], w_vm[1])
        h = h_ref[...]
        r = lax.rsqrt(jnp.mean(h * h, axis=-1, keepdims=True) + RMS_EPS)
        n0 = h * r
        acc_ref[0:1, :] += jnp.sum(du * n0, axis=0, keepdims=True)
        dn = du * g_ref[...]
        gx_ref[...] = dh1_ref[...] + r * (dn - n0 * jnp.mean(dn * n0, axis=-1, keepdims=True))

        @pl.when(i == nt - 1)
        def _():
            gmeta_ref[...] = gx_ref[pl.ds(tm - N_META, N_META), :]
            ex.finish()

    tile = pl.BlockSpec((tm, D), lambda i: (i, 0))
    anys = pl.BlockSpec(memory_space=pl.ANY)
    res = pl.pallas_call(
        body, name="in_bwd", grid=(nt,),
        in_specs=[pl.BlockSpec((tm, DIN), lambda i: (i, 0)), tile, tile, pl.BlockSpec((1, D), lambda i: (0, 0)), anys]
        + [anys] * nq,
        out_specs=[tile, pl.BlockSpec((N_META, D), lambda i: (0, 0)), pl.BlockSpec((8, D), lambda i: (0, 0))] + [anys] * nq,
        out_shape=[jax.ShapeDtypeStruct((seq, D), F32), jax.ShapeDtypeStruct((N_META, D), F32),
                   jax.ShapeDtypeStruct((8, D), F32)] + ex.out_shape,
        scratch_shapes=[pltpu.VMEM((2, D, DIN // 2), BF16), pltpu.SemaphoreType.DMA((NDEV,))] + ex.scratch,
        compiler_params=_params(("arbitrary",), 58),
    )(dz, h0, dh1, g_mix, w_g, *qs)
    return res[:3], res[3:]


def _wgrad_in(u, dz):
    tp = u.shape[0]
    tm = _pick(tp, TM_WG)
    nt = tp // tm
    half = DIN // 2

    def body(u_ref, dz_ref, o_ref, acc):
        t = pl.program_id(1)

        @pl.when(t == 0)
        def _():
            acc[...] = jnp.zeros_like(acc)

        acc[...] += _dot_tn(u_ref[...], dz_ref[...])

        @pl.when(t == nt - 1)
        def _():
            for d in range(4):
                o_ref[d] = acc[:, INB * d:INB * (d + 1)].astype(BF16)

    return pl.pallas_call(
        body, name="wgrad_in", grid=(2, nt),
        in_specs=[pl.BlockSpec((tm, D), lambda h, t: (t, 0)), pl.BlockSpec((tm, half), lambda h, t: (t, h))],
        out_specs=pl.BlockSpec((4, D, INB), lambda h, t: (h, 0, 0), pipeline_mode=pl.Buffered(1)),
        out_shape=jax.ShapeDtypeStruct((NDEV, D, INB), BF16),
        scratch_shapes=[pltpu.VMEM((D, half), F32)],
        compiler_params=_params(("arbitrary", "arbitrary"), 52),
    )(u, dz)


def _wgrad_mix(s, dyc, q, dyp, merged, dh1, m, dm2, qs):
    tp = s.shape[0]
    tm = _pick(tp, TM_WM)
    nt = tp // tm
    rb = D // NDEV
    ex = _ChipExchange(qs)
    nq = ex.n

    def body(*refs):
        s_ref, dyc_ref, q_ref, dyp_ref, mg_ref, dh1_ref, m_ref, dm2_ref = refs[:8]
        o_ref, op_ref = refs[8 + nq:10 + nq]
        acc, accp = refs[10 + 2 * nq:12 + 2 * nq]
        ex.bind(refs[8:8 + nq], refs[10 + nq:10 + 2 * nq], refs[12 + 2 * nq:])
        t = pl.program_id(0)

        @pl.when(t == 0)
        def _():
            ex.issue()
            acc[...] = jnp.zeros_like(acc)
            accp[...] = jnp.zeros_like(accp)

        acc[0] += _dot_tn(s_ref[...], dyc_ref[...])
        acc[1] += _dot_tn(q_ref[...], dyp_ref[...])
        acc[2] += _dot_tn(mg_ref[...], dh1_ref[...].astype(BF16))
        for g in range(4):
            accp[g] += _dot_tn(m_ref[:, g * PG:(g + 1) * PG], dm2_ref[:, g * PG:(g + 1) * PG])

        @pl.when(t == nt - 1)
        def _():
            for d in range(NDEV):
                for k in range(3):
                    o_ref[d, k] = acc[k, rb * d:rb * (d + 1), :].astype(BF16)
                for g in range(4):
                    op_ref[d, g] = accp[g, 32 * d:32 * (d + 1), :].astype(BF16)
            ex.finish()

    tile = pl.BlockSpec((tm, D), lambda t: (t, 0))
    anys = pl.BlockSpec(memory_space=pl.ANY)
    res = pl.pallas_call(
        body, name="wgrad_mix", grid=(nt,),
        in_specs=[tile] * 8 + [anys] * nq,
        out_specs=[pl.BlockSpec((NDEV, 3, rb, D), lambda t: (0, 0, 0, 0), pipeline_mode=pl.Buffered(1)),
                   pl.BlockSpec((NDEV, 4, 32, PG), lambda t: (0, 0, 0, 0), pipeline_mode=pl.Buffered(1))] + [anys] * nq,
        out_shape=[jax.ShapeDtypeStruct((NDEV, 3, rb, D), BF16), jax.ShapeDtypeStruct((NDEV, 4, 32, PG), BF16)]
        + ex.out_shape,
        scratch_shapes=[pltpu.VMEM((3, D, D), F32), pltpu.VMEM((4, PG, PG), F32)] + ex.scratch,
        compiler_params=_params(("arbitrary",), 56),
    )(s, dyc, q, dyp, merged, dh1, m, dm2, *qs)
    return res[:2], res[2:]


def _wgrad_gu(v, dfg, dfu):
    tp = v.shape[0]
    tm = _pick(tp, TM_WG)
    nt = tp // tm

    def body(v_ref, dg_ref, du_ref, o_ref, acc):
        k, t = pl.program_id(0), pl.program_id(2)

        @pl.when(t == 0)
        def _():
            acc[...] = jnp.zeros_like(acc)

        @pl.when(k == 0)
        def _():
            acc[...] += _dot_tn(dg_ref[...], v_ref[...])

        @pl.when(k == 1)
        def _():
            acc[...] += _dot_tn(du_ref[...], v_ref[...])

        @pl.when(t == nt - 1)
        def _():
            for d in range(4):
                o_ref[d] = acc[FFB * d:FFB * (d + 1), :].astype(BF16)

    return pl.pallas_call(
        body, name="wgrad_gu", grid=(2, 2, nt),
        in_specs=[pl.BlockSpec((tm, D), lambda k, h, t: (t, 0)),
                  pl.BlockSpec((tm, FFC), lambda k, h, t: (t * (1 - k), h * (1 - k))),
                  pl.BlockSpec((tm, FFC), lambda k, h, t: (t * k, h * k))],
        out_specs=pl.BlockSpec((4, None, FFB, D), lambda k, h, t: (h, k, 0, 0), pipeline_mode=pl.Buffered(1)),
        out_shape=jax.ShapeDtypeStruct((NDEV, 2, FFB, D), BF16),
        scratch_shapes=[pltpu.VMEM((FFC, D), F32)],
        compiler_params=_params(("arbitrary",) * 3, 48),
    )(v, dfg, dfu)


def _wgrad_down(f, dh2):
    tp = f.shape[0]
    tm = _pick(tp, TM_WG)
    nt = tp // tm

    def body(f_ref, d_ref, o_ref, acc):
        t = pl.program_id(1)

        @pl.when(t == 0)
        def _():
            acc[...] = jnp.zeros_like(acc)

        acc[...] += _dot_tn(f_ref[...], d_ref[...].astype(BF16))

        @pl.when(t == nt - 1)
        def _():
            for d in range(4):
                o_ref[d] = acc[FFB * d:FFB * (d + 1), :].astype(BF16)

    return pl.pallas_call(
        body, name="wgrad_down", grid=(2, nt),
        in_specs=[pl.BlockSpec((tm, FFC), lambda h, t: (t, h)), pl.BlockSpec((tm, D), lambda h, t: (t, 0))],
        out_specs=pl.BlockSpec((4, FFB, D), lambda h, t: (h, 0, 0), pipeline_mode=pl.Buffered(1)),
        out_shape=jax.ShapeDtypeStruct((NDEV, FFB, D), BF16),
        scratch_shapes=[pltpu.VMEM((FFC, D), F32)],
        compiler_params=_params(("arbitrary", "arbitrary"), 48),
    )(f, dh2)


def kernel(x, meta_tokens, g_mix, w_in, b_gate, w_dw, b_dw, ln_g, ln_b, w_conv_out, w_pool, pool_scale, w_pool_out, w_o, g_ffn, w_ffn_gate, w_ffn_up, w_ffn_down, g_final, loss_target, m_meta_tokens, m_g_mix, m_w_in, m_b_gate, m_w_dw, m_b_dw, m_ln_g, m_ln_b, m_w_conv_out, m_w_pool, m_pool_scale, m_w_pool_out, m_w_o, m_g_ffn, m_w_ffn_gate, m_w_ffn_up, m_w_ffn_down, m_g_final, v_meta_tokens, v_g_mix, v_w_in, v_b_gate, v_w_dw, v_b_dw, v_ln_g, v_ln_b, v_w_conv_out, v_w_pool, v_pool_scale, v_w_pool_out, v_w_o, v_g_ffn, v_w_ffn_gate, v_w_ffn_up, v_w_ffn_down, v_g_final):
    seq = x.shape[1]
    tp = -(-(seq + 2 * HALO) // TM) * TM
    tm_in = _pick(tp, TM_IO)
    nx_last = seq - (tp // tm_in - 1) * tm_in
    assert 0 < nx_last <= tm_in - 2 * HALO and nx_last % 8 == 0 and 0 < seq - (tp // TM - 1) * TM

    whole = (Ellipsis,)
    ag_small = _Gather(
        [((48, D // NDEV), [(meta_tokens, pl.ds(0, N_META), whole), (w_dw, pl.ds(N_META, CONV_K), 0)])], [F32])
    ag_mix = _Gather([((3, D // NDEV, D), [(w_conv_out, 0, 0), (w_pool_out, 1, 0), (w_o, 2, 0)]),
                      ((4, PG // NDEV, PG), [(w_pool, whole, 0)])], [BF16, BF16])
    def tr(a):
        return jnp.swapaxes(a, 1, 2)

    ag_gu = _Gather([((2, FFB, D), [(tr(w_ffn_gate), 0, 0), (tr(w_ffn_up), 1, 0)])], [BF16])
    ag_dn = _Gather([((FFB, D), [(w_ffn_down, whole, 0)])], [BF16])

    mx, my = lax.axis_index("x"), lax.axis_index("y")
    order = jnp.stack([2 * mx + my, 2 * mx + 1 - my, 2 * (1 - mx) + my, 2 * (1 - mx) + 1 - my]).astype(jnp.int32)
    (h0, z, u, g_in), (g_mixw, g_pool), g_small = _fwd_in(x[0], g_mix, w_in, order, tp, ag_mix, ag_small)
    wdw_full = g_small.transpose(1, 0, 2).reshape(48, D)[N_META:]
    (ac, m), (w_gu,) = _seq_fwd(z, wdw_full, b_dw, seq, ag_gu)
    (h1, s, yc, yp, merged, q), (g_down,) = _mix_fwd(ac, m, z, h0, b_gate, ln_g, ln_b, pool_scale, g_mixw, g_pool, ag_dn)
    w_dn = g_down.reshape(2, FFC, D)
    fg, fu, v, f, dh2, head_acc = _ffn_fwd(h1, loss_target[0], g_ffn, g_final.reshape(1, D), w_gu, w_dn)

    dfg, dfu, dh1, ffn_acc = _ffn_bwd(dh2, fg, fu, h1, g_ffn, w_gu, w_dn)
    own_f, sib_f, q_f = _rs_pair("rs_pair_ffn", [_wgrad_gu(v, dfg, dfu), _wgrad_down(f, dh2)])
    (dac, dm, dzg, dyc, dyp, dm2, mix_acc), rel_gu = _mix_bwd(
        dh1, z, yc, yp, ac, m, b_gate, ln_g, ln_b, pool_scale, g_mixw, g_pool, q_f[:1])
    p_mix, rel_dn = _wgrad_mix(s, dyc, q, dyp, merged, dh1, m, dm2, q_f[1:])
    rel_f = [rel_gu[0], rel_dn[0]]
    own_m, sib_m, q_m = _rs_pair("rs_pair_mix", list(p_mix))
    (dz, seq_acc), rel_m = _seq_bwd(dac, dm, dzg, z, wdw_full, seq, q_m)
    own_i, sib_i, q_i = _rs_pair("rs_pair_in", [_wgrad_in(u, dz)])
    (grad_x, g_meta, in_acc), rel_i = _in_bwd(dz, h0, dh1, g_mix, g_in, seq, q_i)
    small_g = jnp.concatenate([g_meta, seq_acc[:CONV_K], jnp.zeros((1, D), F32)], axis=0)
    p_small = small_g.reshape(48, NDEV, D // NDEV).transpose(1, 0, 2).astype(BF16)
    rep_g = jnp.concatenate([
        in_acc[0:1], mix_acc[0:1, :D], mix_acc[0:1, D:], seq_acc[CONV_K:CONV_K + 1], mix_acc[1:2, :D], mix_acc[1:2, D:],
        mix_acc[2:3, :D], ffn_acc[0:1], head_acc[1:2], head_acc[0:1], jnp.zeros((REP_ROWS - 10, D), F32)], axis=0)
    own_s, sib_s, rel_s, rep_all = _reduce_scatter([p_small], rep_g)
    owns = [own_i[0], own_s[0], own_m[0], own_m[1], own_f[0], own_f[1]]
    sibs = [sib_i[0], sib_s[0], sib_m[0], sib_m[1], sib_f[0], sib_f[1]]
    rels = [rel_i[0], rel_s[0], rel_m[0], rel_m[1], rel_f[0], rel_f[1]]

    def lead(a):
        return a.reshape(1, *a.shape)

    def stack4(a, lead_dims):
        return a.reshape(*lead_dims, 1, 4 * 32, PG)

    (r_in,) = _adamw_multi("adamw_in", lead(owns[0]), sibs[0][:, None], rels[0][:, None], [w_in], [m_w_in], [v_w_in], 4)
    r_meta, r_dw = _adamw_meta_dw(owns[1], sibs[1], rels[1], (meta_tokens, m_meta_tokens, v_meta_tokens),
                                  (w_dw, m_w_dw, v_w_dw))
    r_conv, r_pout, r_o = _adamw_multi("adamw_mix", owns[2], sibs[2], rels[2], [w_conv_out, w_pool_out, w_o],
                                       [m_w_conv_out, m_w_pool_out, m_w_o], [v_w_conv_out, v_w_pool_out, v_w_o], 1)
    (r_pool,) = _adamw_multi("adamw_pool", stack4(owns[3], ()), stack4(sibs[3], (1,)), stack4(rels[3], (3,)),
                             [w_pool.reshape(1, 128, PG)], [m_w_pool.reshape(1, 128, PG)], [v_w_pool.reshape(1, 128, PG)], 1)
    r_pool = tuple(a.reshape(w_pool.shape) for a in r_pool)
    r_gate, r_up = _adamw_multi("adamw_gu", owns[4], sibs[4], rels[4], [tr(w_ffn_gate), tr(w_ffn_up)],
                                [tr(m_w_ffn_gate), tr(m_w_ffn_up)], [tr(v_w_ffn_gate), tr(v_w_ffn_up)], 2)
    r_gate, r_up = tuple(tr(a) for a in r_gate), tuple(tr(a) for a in r_up)
    (r_down,) = _adamw_multi("adamw_down", lead(owns[5]), sibs[5][:, None], rels[5][:, None],
                             [w_ffn_down], [m_w_ffn_down], [v_w_ffn_down], 2)
    row = (1, D)
    loss, reps = _adamw_rep(
        rep_all,
        [g_mix, b_gate, b_dw, ln_g, ln_b, pool_scale, g_ffn, g_final.reshape(row)],
        [m_g_mix, m_b_gate, m_b_dw, m_ln_g, m_ln_b, m_pool_scale, m_g_ffn, m_g_final.reshape(row)],
        [v_g_mix, v_b_gate, v_b_dw, v_ln_g, v_ln_b, v_pool_scale, v_g_ffn, v_g_final.reshape(row)])
    r_gmix, r_bg, r_bdw, r_lg, r_lb, r_ps, r_gffn, r_gfin = reps
    r_gfin = tuple(a.reshape(D) for a in r_gfin)

    in_order = [r_meta, r_gmix, r_in, r_bg, r_dw, r_bdw, r_lg, r_lb, r_conv, r_pool, r_ps, r_pout, r_o, r_gffn,
                r_gate, r_up, r_down, r_gfin]
    return (loss.reshape(()), grad_x[None], *[r[0] for r in in_order], *[r[1] for r in in_order],
            *[r[2] for r in in_order], *[r[3] for r in in_order])
```

```python
import math

import jax
import jax.numpy as jnp
from jax import lax
from jax.experimental import pallas as pl
from jax.experimental.pallas import tpu as pltpu

F32, BF16 = jnp.float32, jnp.bfloat16
MESH_ID = pl.DeviceIdType.MESH
NDEV = 8

D = 1024
N_META = 16
CONV_K = 31
HALO = 16
POOL_WINDOWS = (2, 4, 8, 16)
PG = 256
DIN = 5 * D
DFF = 2816
FFB = DFF // NDEV
FFC = DFF // 2
INB = DIN // NDEV
RMS_EPS = 1e-6
LN_EPS = 1e-5
ADAM_LR, ADAM_B1, ADAM_B2, ADAM_EPS, ADAM_WD, ADAM_STEP = 0.001, 0.9, 0.999, 1e-08, 0.01, 10

TM = 384
TMS = 384
TM_IO = 704
TM_WG = 1408
TM_WM = 704
MIB = 2 ** 20


def _sig(x):
    return 0.5 * jnp.tanh(0.5 * x) + 0.5


def _dot(a, b):
    return jnp.dot(a, b, preferred_element_type=F32)


def _dot_nt(a, b):
    return lax.dot_general(a, b, (((1,), (1,)), ((), ())), preferred_element_type=F32)


def _dot_tn(a, b):
    return lax.dot_general(a, b, (((0,), (0,)), ((), ())), preferred_element_type=F32)


def _pick(tp, pref):
    return pref if tp % pref == 0 else TM


def _params(sem, vmem_mib):
    return pltpu.CompilerParams(dimension_semantics=sem, vmem_limit_bytes=vmem_mib * MIB)


def _load_once(first, pairs, sems):
    @pl.when(first)
    def _():
        cps = [pltpu.make_async_copy(s, d, sems.at[k]) for k, (s, d) in enumerate(pairs)]
        for cp in cps:
            cp.start()
        for cp in cps:
            cp.wait()


def _place():
    x, y, c = lax.axis_index("x"), lax.axis_index("y"), lax.axis_index("c")
    return x, y, c


class _Gather:
    def __init__(self, groups, dtypes):
        self.groups, self.dtypes, self.n = groups, dtypes, len(groups)
        self.arrays = [a for _, parts in groups for a, _, _ in parts]
        self.out_shape = [jax.ShapeDtypeStruct((NDEV, *s), dt) for (s, _), dt in zip(groups, dtypes)]
        self.scratch = [pltpu.VMEM(s, dt) for (s, _), dt in zip(groups, dtypes)] + [
            pltpu.SemaphoreType.DMA((7 * self.n,)), pltpu.SemaphoreType.DMA((7 * self.n,)),
            pltpu.SemaphoreType.DMA((self.n,))]

    def bind(self, ins, outs, scratch):
        self.ins, self.outs, self.stages = ins, outs, scratch[:self.n]
        self.send_sems, self.recv_sems, self.local_sems = scratch[self.n:]
        return self

    def _copy(self, w, k, block, to, src=None):
        dst = self.outs[w].at[4 * block[0] + 2 * block[1] + block[2]]
        return pltpu.make_async_remote_copy(
            src_ref=dst if src is None else src, dst_ref=dst,
            send_sem=self.send_sems.at[7 * w + k], recv_sem=self.recv_sems.at[7 * w + k],
            device_id=to, device_id_type=MESH_ID)

    def _first(self):
        x, y, c = _place()
        me, sibling = (x, y, c), (x, y, 1 - c)
        chips = [(1 - x, y), (x, 1 - y), (1 - x, 1 - y)]
        mine, first = [], []
        for w in range(self.n):
            mine.append(pltpu.make_async_copy(self.stages[w], self.outs[w].at[4 * x + 2 * y + c], self.local_sems.at[w]))
            first.append(self._copy(w, 0, me, sibling, src=self.stages[w]))
            first += [self._copy(w, 1 + j, me, (*chip, c), src=self.stages[w]) for j, chip in enumerate(chips)]
        return mine, first

    def _passed(self):
        x, y, c = _place()
        chips = [(1 - x, y), (x, 1 - y), (1 - x, 1 - y)]
        return [self._copy(w, 4 + j, (*chip, c), (x, y, 1 - c)) for w in range(self.n) for j, chip in enumerate(chips)]

    def issue(self):
        a = 0
        for w in range(self.n):
            shape, parts = self.groups[w]
            if sum(arr.size for arr, _, _ in parts) < math.prod(shape):
                self.stages[w][...] = jnp.zeros(shape, self.dtypes[w])
            for _, dst, src in parts:
                self.stages[w][dst] = self.ins[a][src].astype(self.dtypes[w])
                a += 1
        mine, first = self._first()
        for cp in mine + first:
            cp.start()

    def forward(self):
        x, y, c = _place()
        chips = [(1 - x, y), (x, 1 - y), (1 - x, 1 - y)]
        passed = self._passed()
        for w in range(self.n):
            for j, chip in enumerate(chips):
                self._copy(w, 1 + j, (*chip, c), (x, y, c)).wait_recv()
                passed[3 * w + j].start()

    def finish(self):
        x, y, c = _place()
        chips = [(1 - x, y), (x, 1 - y), (1 - x, 1 - y)]
        for w in range(self.n):
            self._copy(w, 0, (x, y, 1 - c), (x, y, c)).wait_recv()
            for j, chip in enumerate(chips):
                self._copy(w, 4 + j, (*chip, 1 - c), (x, y, c)).wait_recv()
        mine, first = self._first()
        for cp in first + self._passed():
            cp.wait_send()
        for cp in mine:
            cp.wait()


class _ChipExchange:
    def __init__(self, qs):
        self.n = len(qs)
        self.out_shape = [jax.ShapeDtypeStruct(q.shape, q.dtype) for q in qs]
        self.scratch = [pltpu.SemaphoreType.DMA((3 * self.n,)), pltpu.SemaphoreType.DMA((3 * self.n,))]

    def bind(self, qs, rels, scratch):
        self.qs, self.rels = qs, rels
        self.send_sems, self.recv_sems = scratch
        return self

    def _copies(self):
        x, y, c = _place()
        chips = [(1 - x, y), (x, 1 - y), (1 - x, 1 - y)]
        return [pltpu.make_async_remote_copy(
            src_ref=self.qs[w].at[j], dst_ref=self.rels[w].at[j],
            send_sem=self.send_sems.at[3 * w + j], recv_sem=self.recv_sems.at[3 * w + j],
            device_id=(*chips[j], c), device_id_type=MESH_ID) for w in range(self.n) for j in range(3)]

    def issue(self):
        for cp in self._copies():
            cp.start()

    def finish(self):
        cps = self._copies()
        for cp in cps:
            cp.wait_recv()
        for cp in cps:
            cp.wait_send()


def _reduce_scatter(parts, small):
    n = len(parts)
    blks = [p.shape[1:] for p in parts]

    def body(*refs):
        ps, small_ref = refs[:n], refs[n]
        o = n + 1
        owns, sibs, rels, small_out = refs[o:o + n], refs[o + n:o + 2 * n], refs[o + 2 * n:o + 3 * n], refs[o + 3 * n]
        o += 3 * n + 1
        pa, pb, qst = refs[o:o + n], refs[o + n:o + 2 * n], refs[o + 2 * n:o + 3 * n]
        s1_send, s1_recv, s2_send, s2_recv, sm_send, sm_recv, lsem = refs[o + 3 * n:]
        x, y, c = _place()
        me = 4 * x + 2 * y + c
        sibling = (x, y, 1 - c)
        chips = [(1 - x, y), (x, 1 - y), (1 - x, 1 - y)]
        all_chips = [(x, y)] + chips

        own_cps = []
        for w in range(n):
            cp = pltpu.make_async_copy(ps[w].at[me], owns[w], lsem.at[w])
            cp.start()
            own_cps.append(cp)
        sm_own = pltpu.make_async_copy(small_ref, small_out.at[me], lsem.at[n])
        sm_own.start()

        def small_copy(r):
            peer = ((x + (r >> 2)) % 2, (y + ((r >> 1) & 1)) % 2, (c + (r & 1)) % 2)
            return pltpu.make_async_remote_copy(
                src_ref=small_ref, dst_ref=small_out.at[me], send_sem=sm_send.at[r - 1], recv_sem=sm_recv.at[r - 1],
                device_id=peer, device_id_type=MESH_ID)

        sm_cps = [small_copy(r) for r in range(1, NDEV)]
        for cp in sm_cps:
            cp.start()

        def pair_copy(w, rel):
            cx, cy = all_chips[rel]
            return pltpu.make_async_remote_copy(
                src_ref=ps[w].at[4 * cx + 2 * cy + (1 - c)], dst_ref=sibs[w].at[rel],
                send_sem=s1_send.at[4 * w + rel], recv_sem=s1_recv.at[4 * w + rel],
                device_id=sibling, device_id_type=MESH_ID)

        def chip_copy(w, j):
            return pltpu.make_async_remote_copy(
                src_ref=qst[w].at[j], dst_ref=rels[w].at[j],
                send_sem=s2_send.at[3 * w + j], recv_sem=s2_recv.at[3 * w + j],
                device_id=(*chips[j], c), device_id_type=MESH_ID)

        pair_cps = [pair_copy(w, rel) for w in range(n) for rel in (1, 2, 3, 0)]
        for cp in pair_cps:
            cp.start()
        chip_cps = []
        for w in range(n):
            for j, (cx, cy) in enumerate(chips):
                pair_copy(w, 1 + j).wait_recv()
                la = pltpu.make_async_copy(ps[w].at[4 * cx + 2 * cy + c], pa[w], lsem.at[n + 1])
                lb = pltpu.make_async_copy(sibs[w].at[1 + j], pb[w], lsem.at[n + 2])
                la.start()
                lb.start()
                la.wait()
                lb.wait()
                qst[w][j] = (pa[w][...].astype(F32) + pb[w][...].astype(F32)).astype(BF16)
                cp = chip_copy(w, j)
                cp.start()
                chip_cps.append(cp)
        for w in range(n):
            pair_copy(w, 0).wait_recv()
            for j in range(3):
                chip_copy(w, j).wait_recv()
        for cp in sm_cps:
            cp.wait_recv()
        for cp in pair_cps + chip_cps + sm_cps:
            cp.wait_send()
        for cp in own_cps:
            cp.wait()
        sm_own.wait()

    any_spec = pl.BlockSpec(memory_space=pl.ANY)
    outs = pl.pallas_call(
        body, name="rs_grads",
        out_shape=[jax.ShapeDtypeStruct(b, BF16) for b in blks]
        + [jax.ShapeDtypeStruct((4, *b), BF16) for b in blks]
        + [jax.ShapeDtypeStruct((3, *b), BF16) for b in blks]
        + [jax.ShapeDtypeStruct((NDEV, *small.shape), F32)],
        in_specs=[any_spec] * (n + 1),
        out_specs=[any_spec] * (3 * n + 1),
        scratch_shapes=[pltpu.VMEM(b, BF16) for b in blks] + [pltpu.VMEM(b, BF16) for b in blks]
        + [pltpu.VMEM((3, *b), BF16) for b in blks]
        + [pltpu.SemaphoreType.DMA((4 * n,)), pltpu.SemaphoreType.DMA((4 * n,)),
           pltpu.SemaphoreType.DMA((3 * n,)), pltpu.SemaphoreType.DMA((3 * n,)),
           pltpu.SemaphoreType.DMA((NDEV - 1,)), pltpu.SemaphoreType.DMA((NDEV - 1,)),
           pltpu.SemaphoreType.DMA((n + 3,))],
        compiler_params=pltpu.CompilerParams(vmem_limit_bytes=40 * MIB),
    )(*parts, small)
    return outs[:n], outs[n:2 * n], outs[2 * n:3 * n], outs[3 * n]


def _rs_pair(name, parts):
    n = len(parts)
    blks = [p.shape[1:] for p in parts]

    def body(*refs):
        ps = refs[:n]
        owns, sibs, qs = refs[n:2 * n], refs[2 * n:3 * n], refs[3 * n:4 * n]
        pa, pb, qst = refs[4 * n:5 * n], refs[5 * n:6 * n], refs[6 * n:7 * n]
        s_send, s_recv, lsem = refs[7 * n:]
        x, y, c = _place()
        chips = [(1 - x, y), (x, 1 - y), (1 - x, 1 - y)]

        own_cps = [pltpu.make_async_copy(ps[w].at[4 * x + 2 * y + c], owns[w], lsem.at[w]) for w in range(n)]
        mine = [[pltpu.make_async_copy(ps[w].at[4 * cx + 2 * cy + c], pa[w].at[j], lsem.at[2 * n + 3 * w + j])
                 for j, (cx, cy) in enumerate(chips)] for w in range(n)]
        for cp in own_cps + [cp for row in mine for cp in row]:
            cp.start()

        def pair_copy(w, rel):
            cx, cy = (x, y) if rel == 0 else chips[rel - 1]
            return pltpu.make_async_remote_copy(
                src_ref=ps[w].at[4 * cx + 2 * cy + (1 - c)], dst_ref=sibs[w].at[0] if rel == 0 else pb[w].at[rel - 1],
                send_sem=s_send.at[4 * w + rel], recv_sem=s_recv.at[4 * w + rel],
                device_id=(x, y, 1 - c), device_id_type=MESH_ID)

        pair_cps = [pair_copy(w, rel) for w in range(n) for rel in (1, 2, 3, 0)]
        for cp in pair_cps:
            cp.start()
        q_cps = []
        for w in range(n):
            for j in range(3):
                pair_copy(w, 1 + j).wait_recv()
                mine[w][j].wait()
                qst[w][j] = (pa[w][j].astype(F32) + pb[w][j].astype(F32)).astype(BF16)
            cp = pltpu.make_async_copy(qst[w], qs[w], lsem.at[n + w])
            cp.start()
            q_cps.append(cp)
        for w in range(n):
            pair_copy(w, 0).wait_recv()
        for cp in pair_cps:
            cp.wait_send()
        for cp in own_cps + q_cps:
            cp.wait()

    any_spec = pl.BlockSpec(memory_space=pl.ANY)
    outs = pl.pallas_call(
        body, name=name,
        out_shape=[jax.ShapeDtypeStruct(b, BF16) for b in blks]
        + [jax.ShapeDtypeStruct((1, *b), BF16) for b in blks]
        + [jax.ShapeDtypeStruct((3, *b), BF16) for b in blks],
        in_specs=[any_spec] * n,
        out_specs=[any_spec] * (3 * n),
        scratch_shapes=[pltpu.VMEM((3, *b), BF16) for b in blks] * 3
        + [pltpu.SemaphoreType.DMA((4 * n,)), pltpu.SemaphoreType.DMA((4 * n,)), pltpu.SemaphoreType.DMA((5 * n,))],
        compiler_params=pltpu.CompilerParams(vmem_limit_bytes=48 * MIB),
    )(*parts)
    return outs[:n], outs[n:2 * n], outs[2 * n:3 * n]


def _adamw_math(g, w, m, v):
    m = ADAM_B1 * m + (1.0 - ADAM_B1) * g
    v = ADAM_B2 * v + (1.0 - ADAM_B2) * (g * g)
    m_hat = m / (1.0 - ADAM_B1 ** ADAM_STEP)
    v_hat = v / (1.0 - ADAM_B2 ** ADAM_STEP)
    delta = -ADAM_LR * (m_hat / (jnp.sqrt(v_hat) + ADAM_EPS) + ADAM_WD * w)
    return delta, m, v


def _adamw_multi(name, own, sib, rel, ws, ms, vs, row_grid):
    k_n, r_n, c_n = own.shape
    rbk = r_n // row_grid

    def body(*refs):
        own_ref, sib_ref, r0_ref, r1_ref, r2_ref = refs[:5]
        w_refs, m_refs, v_refs = refs[5:5 + k_n], refs[5 + k_n:5 + 2 * k_n], refs[5 + 2 * k_n:5 + 3 * k_n]
        outs = refs[5 + 3 * k_n:]
        for k in range(k_n):
            g = own_ref[k].astype(F32) + sib_ref[k].astype(F32)
            g = g + r0_ref[k].astype(F32)
            g = g + r1_ref[k].astype(F32)
            g = g + r2_ref[k].astype(F32)
            delta, mm, vv = _adamw_math(g, w_refs[k][0], m_refs[k][0], v_refs[k][0])
            outs[4 * k][0] = g
            outs[4 * k + 1][0] = delta
            outs[4 * k + 2][0] = mm
            outs[4 * k + 3][0] = vv

    def lead(j):
        return pl.BlockSpec((None, k_n, rbk, c_n), lambda g: (j, 0, g, 0))

    wspec = pl.BlockSpec((1, rbk, c_n), lambda g: (0, g, 0))
    shp = jax.ShapeDtypeStruct((1, r_n, c_n), F32)
    res = pl.pallas_call(
        body, name=name, grid=(row_grid,),
        in_specs=[pl.BlockSpec((k_n, rbk, c_n), lambda g: (0, g, 0)), lead(0), lead(0), lead(1), lead(2)] + [wspec] * (3 * k_n),
        out_specs=[wspec] * (4 * k_n), out_shape=[shp] * (4 * k_n),
        compiler_params=_params(("arbitrary",), 40),
    )(own, sib, rel, rel, rel, *ws, *ms, *vs)
    return [tuple(res[4 * k:4 * k + 4]) for k in range(k_n)]


def _adamw_meta_dw(own, sib, rel, meta, dw):
    def body(own_ref, sib_ref, rel_ref, wm, mm, vm, wd, md, vd, *outs):
        def gsum(rows):
            g = own_ref[rows, :].astype(F32) + sib_ref[0, rows, :].astype(F32)
            for j in range(3):
                g = g + rel_ref[j, rows, :].astype(F32)
            return g

        g = gsum(pl.ds(0, N_META))
        delta, m2, v2 = _adamw_math(g, wm[...], mm[...], vm[...])
        for o, val in zip(outs[:4], (g, delta, m2, v2)):
            o[...] = val
        g = gsum(pl.ds(N_META, CONV_K))
        delta, m2, v2 = _adamw_math(g, wd[0], md[0], vd[0])
        for o, val in zip(outs[4:], (g, delta, m2, v2)):
            o[0] = val

    s_meta = jax.ShapeDtypeStruct(meta[0].shape, F32)
    s_dw = jax.ShapeDtypeStruct(dw[0].shape, F32)
    res = pl.pallas_call(body, name="adamw_meta_dw", out_shape=[s_meta] * 4 + [s_dw] * 4)(own, sib, rel, *meta, *dw)
    return tuple(res[:4]), tuple(res[4:])


REP_ROWS = 16


def _adamw_rep(gathered, ws, ms, vs):
    rows = [(0, 1), (1, 2), (3, 1), (4, 1), (5, 1), (6, 1), (7, 1), (8, 1)]

    def body(g_ref, *refs):
        w_refs, m_refs, v_refs = refs[:8], refs[8:16], refs[16:24]
        loss_ref, outs, acc = refs[24], refs[25:57], refs[57]
        g = g_ref[0]
        for d in range(1, NDEV):
            g = g + g_ref[d]
        acc[...] = g
        loss_ref[...] = (0.5 / D) * jnp.sum(acc[pl.ds(9, 1), :], axis=1, keepdims=True)
        for p, (r0, nr) in enumerate(rows):
            for h in range(nr):
                cols = pl.ds(h * D, D)
                gp = acc[pl.ds(r0 + h, 1), :]
                delta, mm, vv = _adamw_math(gp, w_refs[p][:, cols], m_refs[p][:, cols], v_refs[p][:, cols])
                for o, val in zip(outs[4 * p:4 * p + 4], (gp, delta, mm, vv)):
                    o[:, cols] = val

    shapes = [jax.ShapeDtypeStruct(w.shape, F32) for w in ws]
    res = pl.pallas_call(
        body, name="adamw_rep",
        out_shape=[jax.ShapeDtypeStruct((1, 1), F32)] + [s for s in shapes for _ in range(4)],
        scratch_shapes=[pltpu.VMEM((REP_ROWS, D), F32)],
    )(gathered, *ws, *ms, *vs)
    return res[0], [tuple(res[1 + 4 * p:5 + 4 * p]) for p in range(8)]


def _load_ffn(i, j, wgu_hbm, wgu, wdn_hbm, wdn, sems):
    half = NDEV // 2

    def copies(ch):
        pairs = [(wgu_hbm.at[half * ch + d, g], wgu.at[g, ch, pl.ds(FFB * d, FFB), :]) for g in range(2) for d in range(half)]
        pairs.append((wdn_hbm.at[ch], wdn.at[ch]))
        return [pltpu.make_async_copy(s, t, sems.at[(2 * half + 1) * ch + k]) for k, (s, t) in enumerate(pairs)]

    @pl.when((i == 0) & (j == 0))
    def _():
        for cp in copies(0) + copies(1):
            cp.start()

    for ch in range(2):
        @pl.when((i == 0) & (j == ch))
        def _():
            for cp in copies(ch):
                cp.wait()


def _win_pairs(w_hbm, w_vm):
    return [(w_hbm.at[q], w_vm.at[q // 2, :, pl.ds(2 * INB * (q % 2), 2 * INB)]) for q in range(4)]


def _whole(a):
    nd = a.ndim
    return pl.BlockSpec(a.shape, lambda *g: (0,) * nd)


CHIPW = 2 * INB
PHASE_CHIP = (1, 0, 2)


class _GatherIn:
    scratch = [pltpu.VMEM((D, INB), BF16), pltpu.SemaphoreType.DMA((7,)), pltpu.SemaphoreType.DMA((7,)),
               pltpu.SemaphoreType.DMA((1,))]

    def bind(self, w_ref, w_vm, scratch):
        self.w_ref, self.w_vm = w_ref, w_vm
        self.stage, self.send_sems, self.recv_sems, self.local_sem = scratch
        return self

    def _win(self, chip, core):
        return self.w_vm.at[2 * chip[0] + chip[1], :, pl.ds(INB * core, INB)]

    def _copy(self, k, chip, core, to, src=None):
        dst = self._win(chip, core)
        return pltpu.make_async_remote_copy(
            src_ref=dst if src is None else src, dst_ref=dst, send_sem=self.send_sems.at[k],
            recv_sem=self.recv_sems.at[k], device_id=to, device_id_type=MESH_ID)

    def _mine(self, cs):
        x, y, _ = _place()
        return pltpu.make_async_copy(self.stage, self._win((x, y), cs), self.local_sem.at[0])

    def issue(self, cs):
        x, y, _ = _place()
        chips = [(1 - x, y), (x, 1 - y), (1 - x, 1 - y)]
        self.stage[...] = self.w_ref[0].astype(BF16)
        self._mine(cs).start()
        self._copy(0, (x, y), cs, (x, y, 1 - cs), src=self.stage).start()
        for j, chip in enumerate(chips):
            self._copy(1 + j, (x, y), cs, (*chip, cs), src=self.stage).start()

    def wait_chip(self, phase, cs):
        x, y, _ = _place()
        chips = [(1 - x, y), (x, 1 - y), (1 - x, 1 - y)]
        if phase == 0:
            self._mine(cs).wait()
            self._copy(0, (x, y), 1 - cs, (x, y, cs)).wait_recv()
            return
        if phase == 1:
            for j in PHASE_CHIP:
                self._copy(1 + j, chips[j], cs, (x, y, cs)).wait_recv()
                self._copy(4 + j, chips[j], cs, (x, y, 1 - cs)).start()
        j = PHASE_CHIP[phase - 1]
        self._copy(4 + j, chips[j], 1 - cs, (x, y, cs)).wait_recv()

    def finish(self, cs):
        x, y, _ = _place()
        for k in range(7):
            self._copy(k, (x, y), cs, (x, y, cs), src=self.stage).wait_send()


def _fwd_in(x2, g_mix, w_in, order, tp, ag, ags):
    tm = _pick(tp, TM_IO)
    nt = tp // tm
    nx_last = x2.shape[0] - (nt - 1) * tm
    na, ng, ns = len(ag.arrays), ag.n, len(ags.arrays)
    gin = _GatherIn()

    def body(order_ref, *refs):
        x_ref, g_ref, w_ref = refs[:3]
        o = 3 + na + ns
        h_ref, z_ref, u_ref, wout_ref = refs[o:o + 4]
        s = o + 4 + ng + 1
        w_vm, u_all, osem, sm_vm = refs[s:s + 4]
        gin.bind(w_ref, w_vm, refs[s + 4:s + 8])
        ag.bind(refs[3:3 + na], refs[o + 4:o + 4 + ng], refs[s + 8:s + 8 + len(ag.scratch)])
        ags.bind(refs[3 + na:3 + na + ns], refs[o + 4 + ng:o + 5 + ng], refs[s + 8 + len(ag.scratch):])
        ph, i = pl.program_id(0), pl.program_id(1)
        core = lax.axis_index("c")
        first = (ph == 0) & (i == 0)
        last = (ph == 3) & (i == nt - 1)
        for cs in range(2):
            @pl.when(first & (core == cs))
            def _():
                gin.issue(cs)

        @pl.when(first)
        def _():
            ags.issue()
            ag.issue()

        @pl.when((ph == 0) & (i == max(nt - 2, 0)))
        def _():
            ags.forward()

        for cs in range(2):
            for p in range(4):
                @pl.when((ph == p) & (i == 0) & (core == cs))
                def _():
                    gin.wait_chip(p, cs)

        @pl.when((ph == 3) & (i == max(nt - 2, 0)))
        def _():
            ag.forward()

        out_copy = pltpu.make_async_copy(w_vm, wout_ref, osem.at[0])

        @pl.when((ph == 3) & (i == 0))
        def _():
            out_copy.start()

        @pl.when((ph == 0) & (i < nt - 1))
        def _():
            h_ref[...] = x_ref[...]

        @pl.when((ph == 0) & (i == nt - 1))
        def _():
            ags.finish()
            cp = pltpu.make_async_copy(ags.outs[0], sm_vm, osem.at[1])
            cp.start()
            h_ref[pl.ds(0, nx_last), :] = x_ref[pl.ds(0, nx_last), :]
            h_ref[pl.ds(nx_last, tm - nx_last - N_META), :] = jnp.zeros((tm - nx_last - N_META, D), F32)
            cp.wait()
            for d in range(NDEV):
                h_ref[pl.ds(tm - N_META, N_META), pl.ds(128 * d, 128)] = sm_vm[d, pl.ds(0, N_META), :]

        @pl.when(ph == 0)
        def _():
            xv = h_ref[...]
            r = lax.rsqrt(jnp.mean(xv * xv, axis=-1, keepdims=True) + RMS_EPS)
            u = (xv * r * g_ref[...]).astype(BF16)
            u_ref[...] = u
            u_all[i] = u

        z_ref[...] = _dot(u_all[i], w_vm[order_ref[ph]])

        @pl.when(last)
        def _():
            ag.finish()
            out_copy.wait()

        for cs in range(2):
            @pl.when(last & (core == cs))
            def _():
                gin.finish(cs)

    def rows(ph, i, order):
        return (jnp.where(ph == 0, i, nt - 1), 0)

    tile = pl.BlockSpec((tm, D), rows)
    anys = pl.BlockSpec(memory_space=pl.ANY)
    res = pl.pallas_call(
        body, name="fwd_in",
        grid_spec=pltpu.PrefetchScalarGridSpec(
            num_scalar_prefetch=1, grid=(4, nt),
            in_specs=[tile, pl.BlockSpec((1, D), lambda ph, i, order: (0, 0)), _whole(w_in)]
            + [_whole(a) for a in ag.arrays + ags.arrays],
            out_specs=[tile, pl.BlockSpec((tm, CHIPW), lambda ph, i, order: (i, order[ph])), tile, anys] + [anys] * (ng + 1),
            scratch_shapes=[pltpu.VMEM((4, D, CHIPW), BF16), pltpu.VMEM((nt, tm, D), BF16), pltpu.SemaphoreType.DMA((2,)),
                            pltpu.VMEM(ags.out_shape[0].shape, F32)] + gin.scratch + ag.scratch + ags.scratch),
        out_shape=[jax.ShapeDtypeStruct((tp, D), F32), jax.ShapeDtypeStruct((tp, DIN), F32),
                   jax.ShapeDtypeStruct((tp, D), BF16), jax.ShapeDtypeStruct((4, D, CHIPW), BF16)]
        + ag.out_shape + ags.out_shape,
        compiler_params=_params(("arbitrary", "arbitrary"), 58),
    )(order, x2, g_mix, w_in, *ag.arrays, *ags.arrays)
    return res[:4], res[4:4 + ng], res[4 + ng]


def _halo_specs(col, nt, width=D):
    r = TM // HALO
    nb = nt * r
    return [pl.BlockSpec((HALO, width), lambda i: ((i * r + nb - 1) % nb, col)),
            pl.BlockSpec((TM, width), lambda i: (i, col)),
            pl.BlockSpec((HALO, width), lambda i: (((i + 1) * r) % nb, col))]


NCB = D // 128
TME = TM + 2 * HALO


def _tm_fill(dst, time0, groups, tile_fn):
    def body(g, c):
        for j in range(NCB):
            dst[pl.ds((time0 + 8 * g) * NCB + j, 8, stride=NCB), :] = tile_fn(pl.multiple_of(8 * g, 8), pl.ds(128 * j, 128))
        return c

    lax.fori_loop(0, groups, body, 0)


def _tm_fill_ext(dst, left, cur, right, fn):
    _tm_fill(dst, 0, HALO // 8, lambda r, l: fn(left, pl.ds(r, 8), l))
    _tm_fill(dst, HALO, TM // 8, lambda r, l: fn(cur, pl.ds(r, 8), l))
    _tm_fill(dst, HALO + TM, HALO // 8, lambda r, l: fn(right, pl.ds(r, 8), l))


def _tm_read(src, groups, store_fn):
    def body(g, c):
        for j in range(NCB):
            store_fn(pl.ds(pl.multiple_of(8 * g, 8), 8), pl.ds(128 * j, 128), src[pl.ds(8 * g * NCB + j, 8, stride=NCB), :])
        return c

    lax.fori_loop(0, groups, body, 0)


def _tm_rows(t):
    return pl.ds(t * NCB if isinstance(t, int) else pl.multiple_of(t * NCB, NCB), NCB)


def _tm_at(ref, t):
    return ref[_tm_rows(t), :]


def _by_group(sub, vals):
    return jnp.where(sub < 2, vals[0], jnp.where(sub < 4, vals[1], jnp.where(sub < 6, vals[2], vals[3])))


def _pool_cnt(b, seq, tp, sub):
    b = jnp.where(b < 0, b + tp, b)
    b = jnp.where(b >= tp, b - tp, b)
    t = jnp.where(b < seq, b + N_META, b - (tp - N_META))
    cnts = []
    for win in POOL_WINDOWS:
        left = win // 2
        lo = jnp.maximum(t - left, 0)
        hi = jnp.minimum(t + win - left, seq + N_META)
        cnts.append(jnp.maximum(hi - lo, 1).astype(F32))
    return _by_group(sub, cnts)


def _edge_rows(seq, tp):
    reach = max(POOL_WINDOWS) // 2
    return [tp - N_META + t for t in range(reach)] + [seq - reach + 1 + t for t in range(reach - 1)]


def _edge_gain(b, seq, tp, sub):
    return _by_group(sub, [float(w) for w in POOL_WINDOWS]) / _pool_cnt(b, seq, tp, sub)


def _nested_windows(at, lo_offs):
    sums, s, have = [], None, set()
    for g, win in enumerate(POOL_WINDOWS):
        for o in range(lo_offs[g], lo_offs[g] + win):
            if o not in have:
                have.add(o)
                s = at(o) if s is None else s + at(o)
        sums.append(s)
    return sums


def _seq_fwd(z, w_dw, b_dw, seq, gat):
    tp = z.shape[0]
    nt = tp // TM
    na, ng = len(gat.arrays), gat.n

    def body(*refs):
        av_l, av, av_r, ag_l, ag, ag_r, p_l, p, p_r, w_ref, b_ref = refs[:11]
        ac_ref, m_ref = refs[11 + na:13 + na]
        a3, p3, o3, m3, w3, b3, m2d = refs[13 + na + ng:20 + na + ng]
        gat.bind(refs[11:11 + na], refs[13 + na:13 + na + ng], refs[20 + na + ng:])
        i = pl.program_id(0)
        sub = lax.broadcasted_iota(jnp.int32, (NCB, 128), 0)

        @pl.when(i == 0)
        def _():
            gat.issue()
            _tm_fill(w3, 0, 4, lambda r, l: w_ref[pl.ds(r, 8), l])
            for j in range(NCB):
                b3[pl.ds(j, 1), :] = b_ref[:, pl.ds(128 * j, 128)]

        @pl.when(i == max(nt - 2, 0))
        def _():
            gat.forward()

        _tm_fill_ext(a3, (av_l, ag_l), (av, ag), (av_r, ag_r), lambda vg, r, l: vg[0][r, l] * _sig(vg[1][r, l]))
        _tm_fill_ext(p3, p_l, p, p_r, lambda ref, r, l: ref[r, l])

        def conv(g, c):
            for t in range(8):
                acc = b3[...]
                for k in range(CONV_K):
                    acc = acc + _tm_at(w3, k) * _tm_at(a3, 8 * g + t + k + 1)
                o3[_tm_rows(8 * g + t), :] = acc
            return c

        lax.fori_loop(0, TM // 8, conv, 0)
        _tm_read(o3, TM // 8, lambda r, l, tile: ac_ref.__setitem__((r, l), tile))

        inv = _by_group(sub, [1.0 / w for w in POOL_WINDOWS])

        def pool(g, c):
            for t in range(8):
                e = 8 * g + t + HALO
                sums = _nested_windows(lambda o: _tm_at(p3, e + o), [-(w // 2) for w in POOL_WINDOWS])
                m3[_tm_rows(8 * g + t), :] = _by_group(sub, sums) * inv - _tm_at(p3, e)
            return c

        lax.fori_loop(0, TM // 8, pool, 0)
        for b in _edge_rows(seq, tp):
            r = b - i * TM

            @pl.when((r >= 0) & (r < TM))
            def _():
                pv = _tm_at(p3, r + HALO)
                m3[_tm_rows(r), :] = (_tm_at(m3, r) + pv) * _edge_gain(b, seq, tp, sub) - pv

        _tm_read(m3, TM // 8, lambda r, l, tile: m2d.__setitem__((r, l), tile))
        m_ref[...] = m2d[...].astype(BF16)

        @pl.when(i == nt - 1)
        def _():
            gat.finish()

    tmaj = pltpu.VMEM((TM * NCB, 128), F32)
    text = pltpu.VMEM((TME * NCB, 128), F32)
    res = pl.pallas_call(
        body, name="seq_fwd", grid=(nt,),
        in_specs=_halo_specs(0, nt) + _halo_specs(1, nt) + _halo_specs(2, nt)
        + [pl.BlockSpec((32, D), lambda i: (0, 0)), pl.BlockSpec((1, D), lambda i: (0, 0))] + [_whole(a) for a in gat.arrays],
        out_specs=[pl.BlockSpec((TM, D), lambda i: (i, 0))] * 2 + [pl.BlockSpec(memory_space=pl.ANY)] * ng,
        out_shape=[jax.ShapeDtypeStruct((tp, D), F32), jax.ShapeDtypeStruct((tp, D), BF16)] + gat.out_shape,
        scratch_shapes=[text, text, tmaj, tmaj, pltpu.VMEM((32 * NCB, 128), F32), pltpu.VMEM((NCB, 128), F32),
                        pltpu.VMEM((TM, D), F32)] + gat.scratch,
        compiler_params=_params(("arbitrary",), 52),
    )(z, z, z, z, z, z, z, z, z, w_dw, b_dw, *gat.arrays)
    return res[:2], res[2:]


def _ln_stats(ac):
    mu = jnp.mean(ac, axis=-1, keepdims=True)
    xc = ac - mu
    rl = lax.rsqrt(jnp.mean(xc * xc, axis=-1, keepdims=True) + LN_EPS)
    return xc * rl, rl


def _pool_mix(m, wp_ref):
    return jnp.concatenate(
        [_dot(m[:, g * PG:(g + 1) * PG], wp_ref[:, g].reshape(PG, PG)) for g in range(4)], axis=1)


def _mix_fwd(ac, m, z, h0, b_gate, ln_g, ln_b, pool_scale, g_mixw, g_pool, gat):
    tp = h0.shape[0]
    tms = TM
    nt = tp // tms
    na, ng = len(gat.arrays), gat.n

    def body(*refs):
        ac_ref, m_ref, zga, zgb, h_ref, bg_ref, lg_ref, lb_ref, ps_ref, wm_hbm, wp_hbm = refs[:11]
        h1_ref, s_ref, mg_ref, q_ref = refs[11 + na:15 + na]
        wm, wp, sems = refs[15 + na + ng:18 + na + ng]
        gat.bind(refs[11:11 + na], refs[15 + na:15 + na + ng], refs[18 + na + ng:])
        i = pl.program_id(0)

        @pl.when(i == 0)
        def _():
            gat.issue()

        @pl.when(i == max(nt - 4, 0))
        def _():
            gat.forward()

        @pl.when(i == nt - 1)
        def _():
            gat.finish()

        _load_once(i == 0, [(wm_hbm, wm), (wp_hbm, wp)], sems)
        n, _ = _ln_stats(ac_ref[...])
        l = n * lg_ref[...] + lb_ref[...]
        s = (l * _sig(l)).astype(BF16)
        s_ref[...] = s
        yc = _dot(s, wm[:, 0].reshape(D, D))
        q = (_pool_mix(m_ref[...], wp) * ps_ref[...]).astype(BF16)
        q_ref[...] = q
        yp = _dot(q, wm[:, 1].reshape(D, D))
        ga = _sig(zga[...] + bg_ref[:, :D])
        gb = _sig(zgb[...] + bg_ref[:, D:])
        merged = (ga * yc + gb * yp).astype(BF16)
        mg_ref[...] = merged
        h1_ref[...] = h_ref[...] + _dot(merged, wm[:, 2].reshape(D, D))

    def tile(col=0):
        return pl.BlockSpec((tms, D), lambda i: (i, col))

    def vec(w):
        return pl.BlockSpec((1, w), lambda i: (0, 0))

    anys = pl.BlockSpec(memory_space=pl.ANY)
    f32o, b16o = jax.ShapeDtypeStruct((tp, D), F32), jax.ShapeDtypeStruct((tp, D), BF16)
    res = pl.pallas_call(
        body, name="mix_fwd", grid=(nt,),
        in_specs=[tile(), tile(), tile(3), tile(4), tile(), vec(2 * D), vec(D), vec(D), vec(D), anys, anys]
        + [_whole(a) for a in gat.arrays],
        out_specs=[tile()] * 4 + [anys] * ng,
        out_shape=[f32o, b16o, b16o, b16o] + gat.out_shape,
        scratch_shapes=[pltpu.VMEM((NDEV, 3, D // NDEV, D), BF16), pltpu.VMEM((NDEV, 4, PG // NDEV, PG), BF16),
                        pltpu.SemaphoreType.DMA((2,))] + gat.scratch,
        compiler_params=_params(("arbitrary",), 52),
    )(ac, m, z, z, h0, b_gate, ln_g, ln_b, pool_scale, g_mixw, g_pool, *gat.arrays)
    return res[:4], res[4:]


def _ffn_fwd(h1, tgt, g_ffn, g_final, w_gu, w_dn):
    tp = h1.shape[0]
    nt = tp // TM
    nx_last = tgt.shape[0] - (nt - 1) * TM

    def body(h_ref, t_ref, gf_ref, gl_ref, wgu_hbm, wdn_hbm,
             fg_ref, fu_ref, v_ref, f_ref, dh2_ref, acc_ref, wgu, wdn, v_sc, h2_sc, diff_sc, sems):
        i, j = pl.program_id(0), pl.program_id(1)
        _load_ffn(i, j, wgu_hbm, wgu, wdn_hbm, wdn, sems)

        @pl.when((i == 0) & (j == 0))
        def _():
            acc_ref[...] = jnp.zeros_like(acc_ref)

        @pl.when(j == 0)
        def _():
            h = h_ref[...]
            r = lax.rsqrt(jnp.mean(h * h, axis=-1, keepdims=True) + RMS_EPS)
            v = (h * r * gf_ref[...]).astype(BF16)
            v_sc[...] = v
            v_ref[...] = v
            h2_sc[...] = h

        v = v_sc[...]
        fg = _dot_nt(v, wgu[0, j])
        fu = _dot_nt(v, wgu[1, j])
        fg_ref[...] = fg
        fu_ref[...] = fu
        f = ((fg * _sig(fg)) * fu).astype(BF16)
        f_ref[...] = f
        h2_sc[...] += _dot(f, wdn[j])

        @pl.when(j == 1)
        def _():
            h2 = h2_sc[...]
            r = lax.rsqrt(jnp.mean(h2 * h2, axis=-1, keepdims=True) + RMS_EPS)
            n2 = h2 * r
            y = n2 * gl_ref[...]

            @pl.when(i < nt - 1)
            def _():
                diff_sc[...] = y - t_ref[...]

            @pl.when(i == nt - 1)
            def _():
                diff_sc[pl.ds(0, nx_last), :] = y[:nx_last] - t_ref[pl.ds(0, nx_last), :]
                diff_sc[pl.ds(nx_last, TM - nx_last), :] = jnp.zeros((TM - nx_last, D), F32)

            diff = diff_sc[...]
            dy = diff * (1.0 / D)
            acc_ref[0:1, :] += jnp.sum(diff * diff, axis=0, keepdims=True)
            acc_ref[1:2, :] += jnp.sum(dy * n2, axis=0, keepdims=True)
            dn = dy * gl_ref[...]
            dh2_ref[...] = r * (dn - n2 * jnp.mean(dn * n2, axis=-1, keepdims=True))

    def tile():
        return pl.BlockSpec((TM, D), lambda i, j: (i, 0))

    def chunk():
        return pl.BlockSpec((TM, FFC), lambda i, j: (i, j))

    def vec():
        return pl.BlockSpec((1, D), lambda i, j: (0, 0))

    anys = pl.BlockSpec(memory_space=pl.ANY)
    hid32, hid16 = jax.ShapeDtypeStruct((tp, DFF), F32), jax.ShapeDtypeStruct((tp, DFF), BF16)
    return pl.pallas_call(
        body, name="ffn_fwd", grid=(nt, 2),
        in_specs=[tile(), tile(), vec(), vec(), anys, anys],
        out_specs=[chunk(), chunk(), tile(), chunk(), tile(), pl.BlockSpec((8, D), lambda i, j: (0, 0))],
        out_shape=[hid32, hid32, jax.ShapeDtypeStruct((tp, D), BF16), hid16, jax.ShapeDtypeStruct((tp, D), F32),
                   jax.ShapeDtypeStruct((8, D), F32)],
        scratch_shapes=[pltpu.VMEM((2, 2, FFC, D), BF16), pltpu.VMEM((2, FFC, D), BF16),
                        pltpu.VMEM((TM, D), BF16), pltpu.VMEM((TM, D), F32), pltpu.VMEM((TM, D), F32),
                        pltpu.SemaphoreType.DMA((2 * NDEV + 2,))],
        compiler_params=_params(("arbitrary", "arbitrary"), 56),
    )(h1, tgt, g_ffn, g_final, w_gu, w_dn)


def _ffn_bwd(dh2, fg, fu, h1, g_ffn, w_gu, w_dn):
    tp = h1.shape[0]
    nt = tp // TM

    def body(dh2_ref, fg_ref, fu_ref, h_ref, gf_ref, wgu_hbm, wdn_hbm,
             dfg_ref, dfu_ref, dh1_ref, acc_ref, wgu, wdn, d_sc, dv_sc, sems):
        i, j = pl.program_id(0), pl.program_id(1)
        _load_ffn(i, j, wgu_hbm, wgu, wdn_hbm, wdn, sems)

        @pl.when((i == 0) & (j == 0))
        def _():
            acc_ref[...] = jnp.zeros_like(acc_ref)

        @pl.when(j == 0)
        def _():
            d_sc[...] = dh2_ref[...].astype(BF16)
            dv_sc[...] = jnp.zeros_like(dv_sc)

        df = _dot_nt(d_sc[...], wdn[j])
        fg = fg_ref[...]
        sg = _sig(fg)
        dfu = (df * (fg * sg)).astype(BF16)
        dfg = (df * fu_ref[...] * (sg * (1.0 + fg * (1.0 - sg)))).astype(BF16)
        dfg_ref[...] = dfg
        dfu_ref[...] = dfu
        dv_sc[...] += _dot(dfg, wgu[0, j]) + _dot(dfu, wgu[1, j])

        @pl.when(j == 1)
        def _():
            h = h_ref[...]
            r = lax.rsqrt(jnp.mean(h * h, axis=-1, keepdims=True) + RMS_EPS)
            n1 = h * r
            dv = dv_sc[...]
            acc_ref[0:1, :] += jnp.sum(dv * n1, axis=0, keepdims=True)
            dn = dv * gf_ref[...]
            dh1_ref[...] = dh2_ref[...] + r * (dn - n1 * jnp.mean(dn * n1, axis=-1, keepdims=True))

    def tile():
        return pl.BlockSpec((TM, D), lambda i, j: (i, 0))

    def chunk():
        return pl.BlockSpec((TM, FFC), lambda i, j: (i, j))

    anys = pl.BlockSpec(memory_space=pl.ANY)
    hid16 = jax.ShapeDtypeStruct((tp, DFF), BF16)
    return pl.pallas_call(
        body, name="ffn_bwd", grid=(nt, 2),
        in_specs=[tile(), chunk(), chunk(), tile(), pl.BlockSpec((1, D), lambda i, j: (0, 0)), anys, anys],
        out_specs=[chunk(), chunk(), tile(), pl.BlockSpec((8, D), lambda i, j: (0, 0))],
        out_shape=[hid16, hid16, jax.ShapeDtypeStruct((tp, D), F32), jax.ShapeDtypeStruct((8, D), F32)],
        scratch_shapes=[pltpu.VMEM((2, 2, FFC, D), BF16), pltpu.VMEM((2, FFC, D), BF16),
                        pltpu.VMEM((TM, D), BF16), pltpu.VMEM((TM, D), F32), pltpu.SemaphoreType.DMA((2 * NDEV + 2,))],
        compiler_params=_params(("arbitrary", "arbitrary"), 56),
    )(dh2, fg, fu, h1, g_ffn, w_gu, w_dn)


def _mix_bwd(dh1, z, s, q, ac, m, b_gate, ln_g, ln_b, pool_scale, g_mixw, g_pool, qs):
    tp = dh1.shape[0]
    nt = tp // TMS
    ex = _ChipExchange(qs)
    nq = ex.n

    def body(*refs):
        dh1_ref, zga, zgb, s_ref, q_ref, ac_ref, m_ref, bg_ref, lg_ref, lb_ref, ps_ref, wm_hbm, wp_hbm = refs[:13]
        dac_ref, dm_ref, dzg_ref, dyc_ref, dyp_ref, dm2_ref, acc_ref = refs[13 + nq:20 + nq]
        wm, wp, sems = refs[20 + 2 * nq:23 + 2 * nq]
        ex.bind(refs[13:13 + nq], refs[20 + nq:20 + 2 * nq], refs[23 + 2 * nq:])
        first = pl.program_id(0) == 0

        @pl.when(first)
        def _():
            ex.issue()
            acc_ref[...] = jnp.zeros_like(acc_ref)

        _load_once(first, [(wm_hbm, wm), (wp_hbm, wp)], sems)

        dmerged = _dot_nt(dh1_ref[...].astype(BF16), wm[:, 2].reshape(D, D))
        ga = _sig(zga[...] + bg_ref[:, :D])
        gb = _sig(zgb[...] + bg_ref[:, D:])
        dyc = dmerged * ga
        dyp = dmerged * gb
        dza = (dmerged * _dot(s_ref[...], wm[:, 0].reshape(D, D))) * (ga * (1.0 - ga))
        dzb = (dmerged * _dot(q_ref[...], wm[:, 1].reshape(D, D))) * (gb * (1.0 - gb))
        dzg_ref[:, :D] = dza.astype(BF16)
        dzg_ref[:, D:] = dzb.astype(BF16)
        acc_ref[0:1, :D] += jnp.sum(dza, axis=0, keepdims=True)
        acc_ref[0:1, D:] += jnp.sum(dzb, axis=0, keepdims=True)
        dyc_b = dyc.astype(BF16)
        dyp_b = dyp.astype(BF16)
        dyc_ref[...] = dyc_b
        dyp_ref[...] = dyp_b
        ds = _dot_nt(dyc_b, wm[:, 0].reshape(D, D))
        n, rl = _ln_stats(ac_ref[...])
        l = n * lg_ref[...] + lb_ref[...]
        sg = _sig(l)
        dl = ds * (sg * (1.0 + l * (1.0 - sg)))
        acc_ref[1:2, :D] += jnp.sum(dl * n, axis=0, keepdims=True)
        acc_ref[1:2, D:] += jnp.sum(dl, axis=0, keepdims=True)
        dn = dl * lg_ref[...]
        dac_ref[...] = rl * (dn - jnp.mean(dn, axis=-1, keepdims=True) - n * jnp.mean(dn * n, axis=-1, keepdims=True))
        dq = _dot_nt(dyp_b, wm[:, 1].reshape(D, D))
        mv = m_ref[...]
        acc_ref[2:3, :D] += jnp.sum(dq * _pool_mix(mv, wp), axis=0, keepdims=True)
        dm2 = (dq * ps_ref[...]).astype(BF16)
        dm2_ref[...] = dm2
        dm_ref[...] = jnp.concatenate(
            [_dot_nt(dm2[:, g * PG:(g + 1) * PG], wp[:, g].reshape(PG, PG)) for g in range(4)], axis=1)

        @pl.when(pl.program_id(0) == nt - 1)
        def _():
            ex.finish()

    def tile(col=0):
        return pl.BlockSpec((TMS, D), lambda i: (i, col))

    def vec(w):
        return pl.BlockSpec((1, w), lambda i: (0, 0))

    anys = pl.BlockSpec(memory_space=pl.ANY)
    f32o, b16o = jax.ShapeDtypeStruct((tp, D), F32), jax.ShapeDtypeStruct((tp, D), BF16)
    res = pl.pallas_call(
        body, name="mix_bwd", grid=(nt,),
        in_specs=[tile(), tile(3), tile(4), tile(), tile(), tile(), tile(), vec(2 * D), vec(D), vec(D), vec(D), anys, anys]
        + [anys] * nq,
        out_specs=[tile(), tile(), pl.BlockSpec((TMS, 2 * D), lambda i: (i, 0)), tile(), tile(), tile(),
                   pl.BlockSpec((8, 2 * D), lambda i: (0, 0))] + [anys] * nq,
        out_shape=[f32o, f32o, jax.ShapeDtypeStruct((tp, 2 * D), BF16), b16o, b16o, b16o,
                   jax.ShapeDtypeStruct((8, 2 * D), F32)] + ex.out_shape,
        scratch_shapes=[pltpu.VMEM((NDEV, 3, D // NDEV, D), BF16), pltpu.VMEM((NDEV, 4, PG // NDEV, PG), BF16),
                        pltpu.SemaphoreType.DMA((2,))] + ex.scratch,
        compiler_params=_params(("arbitrary",), 48),
    )(dh1, z, z, s, q, ac, m, b_gate, ln_g, ln_b, pool_scale, g_mixw, g_pool, *qs)
    return res[:7], res[7:]


def _seq_bwd(dac, dm, dzg, z, w_dw, seq, qs):
    tp = z.shape[0]
    nt = tp // TM
    ex = _ChipExchange(qs)
    nq = ex.n

    def body(*refs):
        dac_l, dac_c, dac_r, dm_l, dm_c, dm_r, av_l, av, av_r, ag_l, ag, ag_r, dzg_ref, w_ref = refs[:14]
        dz_ref, acc_ref = refs[14 + nq:16 + nq]
        a3, d3, m3, da3, dp3, w3, dw3, da_sc, dp_sc = refs[16 + 2 * nq:25 + 2 * nq]
        ex.bind(refs[14:14 + nq], refs[16 + nq:16 + 2 * nq], refs[25 + 2 * nq:])
        i = pl.program_id(0)
        sub = lax.broadcasted_iota(jnp.int32, (NCB, 128), 0)

        @pl.when(i == 0)
        def _():
            ex.issue()
            dw3[...] = jnp.zeros_like(dw3)
            _tm_fill(w3, 0, 4, lambda r, l: w_ref[pl.ds(r, 8), l])

        _tm_fill_ext(a3, (av_l, ag_l), (av, ag), (av_r, ag_r), lambda vg, r, l: vg[0][r, l] * _sig(vg[1][r, l]))
        _tm_fill_ext(d3, dac_l, dac_c, dac_r, lambda ref, r, l: ref[r, l])
        _tm_fill_ext(m3, dm_l, dm_c, dm_r, lambda ref, r, l: ref[r, l])

        def conv(g, c):
            dcur = [_tm_at(d3, 8 * g + t + HALO) for t in range(8)]
            accs = [None] * 8
            for k in range(CONV_K):
                wk = _tm_at(w3, k)
                s = None
                for t in range(8):
                    term = wk * _tm_at(d3, 8 * g + t + CONV_K - k)
                    accs[t] = term if accs[t] is None else accs[t] + term
                    pr = dcur[t] * _tm_at(a3, 8 * g + t + k + 1)
                    s = pr if s is None else s + pr
                dw3[_tm_rows(k), :] += s
            s = dcur[0]
            for t in range(1, 8):
                s = s + dcur[t]
            dw3[_tm_rows(CONV_K), :] += s
            for t in range(8):
                da3[_tm_rows(8 * g + t), :] = accs[t]
            return c

        lax.fori_loop(0, TM // 8, conv, 0)

        for b in _edge_rows(seq, tp):
            e = lax.rem(b - i * TM + HALO + tp, tp)

            @pl.when(e < TME)
            def _():
                m3[_tm_rows(e), :] = _tm_at(m3, e) * _edge_gain(b, seq, tp, sub)

        inv = _by_group(sub, [1.0 / w for w in POOL_WINDOWS])

        def pool(g, c):
            for t in range(8):
                e = 8 * g + t + HALO
                sums = _nested_windows(lambda o: _tm_at(m3, e + o), [w // 2 + 1 - w for w in POOL_WINDOWS])
                dp3[_tm_rows(8 * g + t), :] = _by_group(sub, sums) * inv
            return c

        lax.fori_loop(0, TM // 8, pool, 0)

        _tm_read(da3, TM // 8, lambda r, l, tile: da_sc.__setitem__((r, l), tile))
        _tm_read(dp3, TM // 8, lambda r, l, tile: dp_sc.__setitem__((r, l), tile))
        sg = _sig(ag[...])
        da = da_sc[...]
        dz_ref[:, 0:D] = (da * sg).astype(BF16)
        dz_ref[:, D:2 * D] = (da * av[...] * (sg * (1.0 - sg))).astype(BF16)
        dz_ref[:, 2 * D:3 * D] = (dp_sc[...] - dm_c[...]).astype(BF16)
        dz_ref[:, 3 * D:] = dzg_ref[...]

        @pl.when(i == nt - 1)
        def _():
            _tm_read(dw3, 4, lambda r, l, tile: acc_ref.__setitem__((r, l), tile))
            ex.finish()

    tmaj = pltpu.VMEM((TM * NCB, 128), F32)
    text = pltpu.VMEM((TME * NCB, 128), F32)
    taps = pltpu.VMEM((32 * NCB, 128), F32)
    anys = pl.BlockSpec(memory_space=pl.ANY)
    res = pl.pallas_call(
        body, name="seq_bwd", grid=(nt,),
        in_specs=_halo_specs(0, nt) + _halo_specs(0, nt) + _halo_specs(0, nt) + _halo_specs(1, nt)
        + [pl.BlockSpec((TM, 2 * D), lambda i: (i, 0)), pl.BlockSpec((32, D), lambda i: (0, 0))] + [anys] * nq,
        out_specs=[pl.BlockSpec((TM, DIN), lambda i: (i, 0)), pl.BlockSpec((32, D), lambda i: (0, 0))] + [anys] * nq,
        out_shape=[jax.ShapeDtypeStruct((tp, DIN), BF16), jax.ShapeDtypeStruct((32, D), F32)] + ex.out_shape,
        scratch_shapes=[text, text, text, tmaj, tmaj, taps, taps, pltpu.VMEM((TM, D), F32), pltpu.VMEM((TM, D), F32)]
        + ex.scratch,
        compiler_params=_params(("arbitrary",), 48),
    )(dac, dac, dac, dm, dm, dm, z, z, z, z, z, z, dzg, w_dw, *qs)
    return res[:2], res[2:]


def _in_bwd(dz, h0, dh1, g_mix, w_g, seq, qs):
    tp = h0.shape[0]
    tm = _pick(tp, TM_IO)
    nt = tp // tm
    ex = _ChipExchange(qs)
    nq = ex.n

    def body(*refs):
        dz_ref, h_ref, dh1_ref, g_ref, w_hbm = refs[:5]
        gx_ref, gmeta_ref, acc_ref = refs[5 + nq:8 + nq]
        w_vm, sems = refs[8 + 2 * nq:10 + 2 * nq]
        ex.bind(refs[5:5 + nq], refs[8 + nq:8 + 2 * nq], refs[10 + 2 * nq:])
        i = pl.program_id(0)

        @pl.when(i == 0)
        def _():
            ex.issue()
            acc_ref[...] = jnp.zeros_like(acc_ref)

        _load_once(i == 0, _win_pairs(w_hbm, w_vm), sems)

        du = _dot_nt(dz_ref[:, :DIN // 2], w_vm[0]) + _dot_nt(dz_ref[:, DIN // 2:], w_vm[1])
        h = h_ref[...]
        r = lax.rsqrt(jnp.mean(h * h, axis=-1, keepdims=True) + RMS_EPS)
        n0 = h * r
        acc_ref[0:1, :] += jnp.sum(du * n0, axis=0, keepdims=True)
        dn = du * g_ref[...]
        gx_ref[...] = dh1_ref[...] + r * (dn - n0 * jnp.mean(dn * n0, axis=-1, keepdims=True))

        @pl.when(i == nt - 1)
        def _():
            gmeta_ref[...] = gx_ref[pl.ds(tm - N_META, N_META), :]
            ex.finish()

    tile = pl.BlockSpec((tm, D), lambda i: (i, 0))
    anys = pl.BlockSpec(memory_space=pl.ANY)
    res = pl.pallas_call(
        body, name="in_bwd", grid=(nt,),
        in_specs=[pl.BlockSpec((tm, DIN), lambda i: (i, 0)), tile, tile, pl.BlockSpec((1, D), lambda i: (0, 0)), anys]
        + [anys] * nq,
        out_specs=[tile, pl.BlockSpec((N_META, D), lambda i: (0, 0)), pl.BlockSpec((8, D), lambda i: (0, 0))] + [anys] * nq,
        out_shape=[jax.ShapeDtypeStruct((seq, D), F32), jax.ShapeDtypeStruct((N_META, D), F32),
                   jax.ShapeDtypeStruct((8, D), F32)] + ex.out_shape,
        scratch_shapes=[pltpu.VMEM((2, D, DIN // 2), BF16), pltpu.SemaphoreType.DMA((NDEV,))] + ex.scratch,
        compiler_params=_params(("arbitrary",), 58),
    )(dz, h0, dh1, g_mix, w_g, *qs)
    return res[:3], res[3:]


def _wgrad_in(u, dz):
    tp = u.shape[0]
    tm = _pick(tp, TM_WG)
    nt = tp // tm
    half = DIN // 2

    def body(u_ref, dz_ref, o_ref, acc):
        t = pl.program_id(1)

        @pl.when(t == 0)
        def _():
            acc[...] = jnp.zeros_like(acc)

        acc[...] += _dot_tn(u_ref[...], dz_ref[...])

        @pl.when(t == nt - 1)
        def _():
            for d in range(4):
                o_ref[d] = acc[:, INB * d:INB * (d + 1)].astype(BF16)

    return pl.pallas_call(
        body, name="wgrad_in", grid=(2, nt),
        in_specs=[pl.BlockSpec((tm, D), lambda h, t: (t, 0)), pl.BlockSpec((tm, half), lambda h, t: (t, h))],
        out_specs=pl.BlockSpec((4, D, INB), lambda h, t: (h, 0, 0), pipeline_mode=pl.Buffered(1)),
        out_shape=jax.ShapeDtypeStruct((NDEV, D, INB), BF16),
        scratch_shapes=[pltpu.VMEM((D, half), F32)],
        compiler_params=_params(("arbitrary", "arbitrary"), 52),
    )(u, dz)


def _wgrad_mix(s, dyc, q, dyp, merged, dh1, m, dm2, qs):
    tp = s.shape[0]
    tm = _pick(tp, TM_WM)
    nt = tp // tm
    rb = D // NDEV
    ex = _ChipExchange(qs)
    nq = ex.n

    def body(*refs):
        s_ref, dyc_ref, q_ref, dyp_ref, mg_ref, dh1_ref, m_ref, dm2_ref = refs[:8]
        o_ref, op_ref = refs[8 + nq:10 + nq]
        acc, accp = refs[10 + 2 * nq:12 + 2 * nq]
        ex.bind(refs[8:8 + nq], refs[10 + nq:10 + 2 * nq], refs[12 + 2 * nq:])
        t = pl.program_id(0)

        @pl.when(t == 0)
        def _():
            ex.issue()
            acc[...] = jnp.zeros_like(acc)
            accp[...] = jnp.zeros_like(accp)

        acc[0] += _dot_tn(s_ref[...], dyc_ref[...])
        acc[1] += _dot_tn(q_ref[...], dyp_ref[...])
        acc[2] += _dot_tn(mg_ref[...], dh1_ref[...].astype(BF16))
        for g in range(4):
            accp[g] += _dot_tn(m_ref[:, g * PG:(g + 1) * PG], dm2_ref[:, g * PG:(g + 1) * PG])

        @pl.when(t == nt - 1)
        def _():
            for d in range(NDEV):
                for k in range(3):
                    o_ref[d, k] = acc[k, rb * d:rb * (d + 1), :].astype(BF16)
                for g in range(4):
                    op_ref[d, g] = accp[g, 32 * d:32 * (d + 1), :].astype(BF16)
            ex.finish()

    tile = pl.BlockSpec((tm, D), lambda t: (t, 0))
    anys = pl.BlockSpec(memory_space=pl.ANY)
    res = pl.pallas_call(
        body, name="wgrad_mix", grid=(nt,),
        in_specs=[tile] * 8 + [anys] * nq,
        out_specs=[pl.BlockSpec((NDEV, 3, rb, D), lambda t: (0, 0, 0, 0), pipeline_mode=pl.Buffered(1)),
                   pl.BlockSpec((NDEV, 4, 32, PG), lambda t: (0, 0, 0, 0), pipeline_mode=pl.Buffered(1))] + [anys] * nq,
        out_shape=[jax.ShapeDtypeStruct((NDEV, 3, rb, D), BF16), jax.ShapeDtypeStruct((NDEV, 4, 32, PG), BF16)]
        + ex.out_shape,
        scratch_shapes=[pltpu.VMEM((3, D, D), F32), pltpu.VMEM((4, PG, PG), F32)] + ex.scratch,
        compiler_params=_params(("arbitrary",), 56),
    )(s, dyc, q, dyp, merged, dh1, m, dm2, *qs)
    return res[:2], res[2:]


def _wgrad_gu(v, dfg, dfu):
    tp = v.shape[0]
    tm = _pick(tp, TM_WG)
    nt = tp // tm

    def body(v_ref, dg_ref, du_ref, o_ref, acc):
        k, t = pl.program_id(0), pl.program_id(2)

        @pl.when(t == 0)
        def _():
            acc[...] = jnp.zeros_like(acc)

        @pl.when(k == 0)
        def _():
            acc[...] += _dot_tn(dg_ref[...], v_ref[...])

        @pl.when(k == 1)
        def _():
            acc[...] += _dot_tn(du_ref[...], v_ref[...])

        @pl.when(t == nt - 1)
        def _():
            for d in range(4):
                o_ref[d] = acc[FFB * d:FFB * (d + 1), :].astype(BF16)

    return pl.pallas_call(
        body, name="wgrad_gu", grid=(2, 2, nt),
        in_specs=[pl.BlockSpec((tm, D), lambda k, h, t: (t, 0)),
                  pl.BlockSpec((tm, FFC), lambda k, h, t: (t * (1 - k), h * (1 - k))),
                  pl.BlockSpec((tm, FFC), lambda k, h, t: (t * k, h * k))],
        out_specs=pl.BlockSpec((4, None, FFB, D), lambda k, h, t: (h, k, 0, 0), pipeline_mode=pl.Buffered(1)),
        out_shape=jax.ShapeDtypeStruct((NDEV, 2, FFB, D), BF16),
        scratch_shapes=[pltpu.VMEM((FFC, D), F32)],
        compiler_params=_params(("arbitrary",) * 3, 48),
    )(v, dfg, dfu)


def _wgrad_down(f, dh2):
    tp = f.shape[0]
    tm = _pick(tp, TM_WG)
    nt = tp // tm

    def body(f_ref, d_ref, o_ref, acc):
        t = pl.program_id(1)

        @pl.when(t == 0)
        def _():
            acc[...] = jnp.zeros_like(acc)

        acc[...] += _dot_tn(f_ref[...], d_ref[...].astype(BF16))

        @pl.when(t == nt - 1)
        def _():
            for d in range(4):
                o_ref[d] = acc[FFB * d:FFB * (d + 1), :].astype(BF16)

    return pl.pallas_call(
        body, name="wgrad_down", grid=(2, nt),
        in_specs=[pl.BlockSpec((tm, FFC), lambda h, t: (t, h)), pl.BlockSpec((tm, D), lambda h, t: (t, 0))],
        out_specs=pl.BlockSpec((4, FFB, D), lambda h, t: (h, 0, 0), pipeline_mode=pl.Buffered(1)),
        out_shape=jax.ShapeDtypeStruct((NDEV, FFB, D), BF16),
        scratch_shapes=[pltpu.VMEM((FFC, D), F32)],
        compiler_params=_params(("arbitrary", "arbitrary"), 48),
    )(f, dh2)


def kernel(x, meta_tokens, g_mix, w_in, b_gate, w_dw, b_dw, ln_g, ln_b, w_conv_out, w_pool, pool_scale, w_pool_out, w_o, g_ffn, w_ffn_gate, w_ffn_up, w_ffn_down, g_final, loss_target, m_meta_tokens, m_g_mix, m_w_in, m_b_gate, m_w_dw, m_b_dw, m_ln_g, m_ln_b, m_w_conv_out, m_w_pool, m_pool_scale, m_w_pool_out, m_w_o, m_g_ffn, m_w_ffn_gate, m_w_ffn_up, m_w_ffn_down, m_g_final, v_meta_tokens, v_g_mix, v_w_in, v_b_gate, v_w_dw, v_b_dw, v_ln_g, v_ln_b, v_w_conv_out, v_w_pool, v_pool_scale, v_w_pool_out, v_w_o, v_g_ffn, v_w_ffn_gate, v_w_ffn_up, v_w_ffn_down, v_g_final):
    seq = x.shape[1]
    tp = -(-(seq + 2 * HALO) // TM) * TM
    tm_in = _pick(tp, TM_IO)
    nx_last = seq - (tp // tm_in - 1) * tm_in
    assert 0 < nx_last <= tm_in - 2 * HALO and nx_last % 8 == 0 and 0 < seq - (tp // TM - 1) * TM

    whole = (Ellipsis,)
    ag_small = _Gather(
        [((48, D // NDEV), [(meta_tokens, pl.ds(0, N_META), whole), (w_dw, pl.ds(N_META, CONV_K), 0)])], [F32])
    ag_mix = _Gather([((3, D // NDEV, D), [(w_conv_out, 0, 0), (w_pool_out, 1, 0), (w_o, 2, 0)]),
                      ((4, PG // NDEV, PG), [(w_pool, whole, 0)])], [BF16, BF16])
    def tr(a):
        return jnp.swapaxes(a, 1, 2)

    ag_gu = _Gather([((2, FFB, D), [(tr(w_ffn_gate), 0, 0), (tr(w_ffn_up), 1, 0)])], [BF16])
    ag_dn = _Gather([((FFB, D), [(w_ffn_down, whole, 0)])], [BF16])

    mx, my = lax.axis_index("x"), lax.axis_index("y")
    order = jnp.stack([2 * mx + my, 2 * mx + 1 - my, 2 * (1 - mx) + my, 2 * (1 - mx) + 1 - my]).astype(jnp.int32)
    (h0, z, u, g_in), (g_mixw, g_pool), g_small = _fwd_in(x[0], g_mix, w_in, order, tp, ag_mix, ag_small)
    wdw_full = g_small.transpose(1, 0, 2).reshape(48, D)[N_META:]
    (ac, m), (w_gu,) = _seq_fwd(z, wdw_full, b_dw, seq, ag_gu)
    (h1, s, merged, q), (g_down,) = _mix_fwd(ac, m, z, h0, b_gate, ln_g, ln_b, pool_scale, g_mixw, g_pool, ag_dn)
    w_dn = g_down.reshape(2, FFC, D)
    fg, fu, v, f, dh2, head_acc = _ffn_fwd(h1, loss_target[0], g_ffn, g_final.reshape(1, D), w_gu, w_dn)

    dfg, dfu, dh1, ffn_acc = _ffn_bwd(dh2, fg, fu, h1, g_ffn, w_gu, w_dn)
    own_f, sib_f, q_f = _rs_pair("rs_pair_ffn", [_wgrad_gu(v, dfg, dfu), _wgrad_down(f, dh2)])
    (dac, dm, dzg, dyc, dyp, dm2, mix_acc), rel_gu = _mix_bwd(
        dh1, z, s, q, ac, m, b_gate, ln_g, ln_b, pool_scale, g_mixw, g_pool, q_f[:1])
    p_mix, rel_dn = _wgrad_mix(s, dyc, q, dyp, merged, dh1, m, dm2, q_f[1:])
    rel_f = [rel_gu[0], rel_dn[0]]
    own_m, sib_m, q_m = _rs_pair("rs_pair_mix", list(p_mix))
    (dz, seq_acc), rel_m = _seq_bwd(dac, dm, dzg, z, wdw_full, seq, q_m)
    own_i, sib_i, q_i = _rs_pair("rs_pair_in", [_wgrad_in(u, dz)])
    (grad_x, g_meta, in_acc), rel_i = _in_bwd(dz, h0, dh1, g_mix, g_in, seq, q_i)
    small_g = jnp.concatenate([g_meta, seq_acc[:CONV_K], jnp.zeros((1, D), F32)], axis=0)
    p_small = small_g.reshape(48, NDEV, D // NDEV).transpose(1, 0, 2).astype(BF16)
    rep_g = jnp.concatenate([
        in_acc[0:1], mix_acc[0:1, :D], mix_acc[0:1, D:], seq_acc[CONV_K:CONV_K + 1], mix_acc[1:2, :D], mix_acc[1:2, D:],
        mix_acc[2:3, :D], ffn_acc[0:1], head_acc[1:2], head_acc[0:1], jnp.zeros((REP_ROWS - 10, D), F32)], axis=0)
    own_s, sib_s, rel_s, rep_all = _reduce_scatter([p_small], rep_g)
    owns = [own_i[0], own_s[0], own_m[0], own_m[1], own_f[0], own_f[1]]
    sibs = [sib_i[0], sib_s[0], sib_m[0], sib_m[1], sib_f[0], sib_f[1]]
    rels = [rel_i[0], rel_s[0], rel_m[0], rel_m[1], rel_f[0], rel_f[1]]

    def lead(a):
        return a.reshape(1, *a.shape)

    def stack4(a, lead_dims):
        return a.reshape(*lead_dims, 1, 4 * 32, PG)

    (r_in,) = _adamw_multi("adamw_in", lead(owns[0]), sibs[0][:, None], rels[0][:, None], [w_in], [m_w_in], [v_w_in], 4)
    r_meta, r_dw = _adamw_meta_dw(owns[1], sibs[1], rels[1], (meta_tokens, m_meta_tokens, v_meta_tokens),
                                  (w_dw, m_w_dw, v_w_dw))
    r_conv, r_pout, r_o = _adamw_multi("adamw_mix", owns[2], sibs[2], rels[2], [w_conv_out, w_pool_out, w_o],
                                       [m_w_conv_out, m_w_pool_out, m_w_o], [v_w_conv_out, v_w_pool_out, v_w_o], 1)
    (r_pool,) = _adamw_multi("adamw_pool", stack4(owns[3], ()), stack4(sibs[3], (1,)), stack4(rels[3], (3,)),
                             [w_pool.reshape(1, 128, PG)], [m_w_pool.reshape(1, 128, PG)], [v_w_pool.reshape(1, 128, PG)], 1)
    r_pool = tuple(a.reshape(w_pool.shape) for a in r_pool)
    r_gate, r_up = _adamw_multi("adamw_gu", owns[4], sibs[4], rels[4], [tr(w_ffn_gate), tr(w_ffn_up)],
                                [tr(m_w_ffn_gate), tr(m_w_ffn_up)], [tr(v_w_ffn_gate), tr(v_w_ffn_up)], 2)
    r_gate, r_up = tuple(tr(a) for a in r_gate), tuple(tr(a) for a in r_up)
    (r_down,) = _adamw_multi("adamw_down", lead(owns[5]), sibs[5][:, None], rels[5][:, None],
                             [w_ffn_down], [m_w_ffn_down], [v_w_ffn_down], 2)
    row = (1, D)
    loss, reps = _adamw_rep(
        rep_all,
        [g_mix, b_gate, b_dw, ln_g, ln_b, pool_scale, g_ffn, g_final.reshape(row)],
        [m_g_mix, m_b_gate, m_b_dw, m_ln_g, m_ln_b, m_pool_scale, m_g_ffn, m_g_final.reshape(row)],
        [v_g_mix, v_b_gate, v_b_dw, v_ln_g, v_ln_b, v_pool_scale, v_g_ffn, v_g_final.reshape(row)])
    r_gmix, r_bg, r_bdw, r_lg, r_lb, r_ps, r_gffn, r_gfin = reps
    r_gfin = tuple(a.reshape(D) for a in r_gfin)

    in_order = [r_meta, r_gmix, r_in, r_bg, r_dw, r_bdw, r_lg, r_lb, r_conv, r_pool, r_ps, r_pout, r_o, r_gffn,
                r_gate, r_up, r_down, r_gfin]
    return (loss.reshape(()), grad_x[None], *[r[0] for r in in_order], *[r[1] for r in in_order],
            *[r[2] for r in in_order], *[r[3] for r in in_order])
```

```python
import math

import jax
import jax.numpy as jnp
from jax import lax
from jax.experimental import pallas as pl
from jax.experimental.pallas import tpu as pltpu

F32, BF16 = jnp.float32, jnp.bfloat16
MESH_ID = pl.DeviceIdType.MESH
NDEV = 8

D = 1024
N_META = 16
CONV_K = 31
HALO = 16
POOL_WINDOWS = (2, 4, 8, 16)
PG = 256
DIN = 5 * D
DFF = 2816
FFB = DFF // NDEV
FFC = DFF // 2
INB = DIN // NDEV
RMS_EPS = 1e-6
LN_EPS = 1e-5
ADAM_LR, ADAM_B1, ADAM_B2, ADAM_EPS, ADAM_WD, ADAM_STEP = 0.001, 0.9, 0.999, 1e-08, 0.01, 10

TM = 384
TMS = 384
TM_IO = 704
TM_WG = 1408
TM_WM = 704
MIB = 2 ** 20


def _sig(x):
    return 0.5 * jnp.tanh(0.5 * x) + 0.5


def _dot(a, b):
    return jnp.dot(a, b, preferred_element_type=F32)


def _dot_nt(a, b):
    return lax.dot_general(a, b, (((1,), (1,)), ((), ())), preferred_element_type=F32)


def _dot_tn(a, b):
    return lax.dot_general(a, b, (((0,), (0,)), ((), ())), preferred_element_type=F32)


def _pick(tp, pref):
    return pref if tp % pref == 0 else TM


def _params(sem, vmem_mib):
    return pltpu.CompilerParams(dimension_semantics=sem, vmem_limit_bytes=vmem_mib * MIB)


def _load_once(first, pairs, sems):
    @pl.when(first)
    def _():
        cps = [pltpu.make_async_copy(s, d, sems.at[k]) for k, (s, d) in enumerate(pairs)]
        for cp in cps:
            cp.start()
        for cp in cps:
            cp.wait()


def _place():
    x, y, c = lax.axis_index("x"), lax.axis_index("y"), lax.axis_index("c")
    return x, y, c


class _Gather:
    def __init__(self, groups, dtypes):
        self.groups, self.dtypes, self.n = groups, dtypes, len(groups)
        self.arrays = [a for _, parts in groups for a, _, _ in parts]
        self.out_shape = [jax.ShapeDtypeStruct((NDEV, *s), dt) for (s, _), dt in zip(groups, dtypes)]
        self.scratch = [pltpu.VMEM(s, dt) for (s, _), dt in zip(groups, dtypes)] + [
            pltpu.SemaphoreType.DMA((7 * self.n,)), pltpu.SemaphoreType.DMA((7 * self.n,)),
            pltpu.SemaphoreType.DMA((self.n,))]

    def bind(self, ins, outs, scratch):
        self.ins, self.outs, self.stages = ins, outs, scratch[:self.n]
        self.send_sems, self.recv_sems, self.local_sems = scratch[self.n:]
        return self

    def _copy(self, w, k, block, to, src=None):
        dst = self.outs[w].at[4 * block[0] + 2 * block[1] + block[2]]
        return pltpu.make_async_remote_copy(
            src_ref=dst if src is None else src, dst_ref=dst,
            send_sem=self.send_sems.at[7 * w + k], recv_sem=self.recv_sems.at[7 * w + k],
            device_id=to, device_id_type=MESH_ID)

    def _first(self):
        x, y, c = _place()
        me, sibling = (x, y, c), (x, y, 1 - c)
        chips = [(1 - x, y), (x, 1 - y), (1 - x, 1 - y)]
        mine, first = [], []
        for w in range(self.n):
            mine.append(pltpu.make_async_copy(self.stages[w], self.outs[w].at[4 * x + 2 * y + c], self.local_sems.at[w]))
            first.append(self._copy(w, 0, me, sibling, src=self.stages[w]))
            first += [self._copy(w, 1 + j, me, (*chip, c), src=self.stages[w]) for j, chip in enumerate(chips)]
        return mine, first

    def _passed(self):
        x, y, c = _place()
        chips = [(1 - x, y), (x, 1 - y), (1 - x, 1 - y)]
        return [self._copy(w, 4 + j, (*chip, c), (x, y, 1 - c)) for w in range(self.n) for j, chip in enumerate(chips)]

    def issue(self):
        a = 0
        for w in range(self.n):
            shape, parts = self.groups[w]
            if sum(arr.size for arr, _, _ in parts) < math.prod(shape):
                self.stages[w][...] = jnp.zeros(shape, self.dtypes[w])
            for _, dst, src in parts:
                self.stages[w][dst] = self.ins[a][src].astype(self.dtypes[w])
                a += 1
        mine, first = self._first()
        for cp in mine + first:
            cp.start()

    def forward(self):
        x, y, c = _place()
        chips = [(1 - x, y), (x, 1 - y), (1 - x, 1 - y)]
        passed = self._passed()
        for w in range(self.n):
            for j, chip in enumerate(chips):
                self._copy(w, 1 + j, (*chip, c), (x, y, c)).wait_recv()
                passed[3 * w + j].start()

    def finish(self):
        x, y, c = _place()
        chips = [(1 - x, y), (x, 1 - y), (1 - x, 1 - y)]
        for w in range(self.n):
            self._copy(w, 0, (x, y, 1 - c), (x, y, c)).wait_recv()
            for j, chip in enumerate(chips):
                self._copy(w, 4 + j, (*chip, 1 - c), (x, y, c)).wait_recv()
        mine, first = self._first()
        for cp in first + self._passed():
            cp.wait_send()
        for cp in mine:
            cp.wait()


class _ChipExchange:
    def __init__(self, qs):
        self.n = len(qs)
        self.out_shape = [jax.ShapeDtypeStruct(q.shape, q.dtype) for q in qs]
        self.scratch = [pltpu.SemaphoreType.DMA((3 * self.n,)), pltpu.SemaphoreType.DMA((3 * self.n,))]

    def bind(self, qs, rels, scratch):
        self.qs, self.rels = qs, rels
        self.send_sems, self.recv_sems = scratch
        return self

    def _copies(self):
        x, y, c = _place()
        chips = [(1 - x, y), (x, 1 - y), (1 - x, 1 - y)]
        return [pltpu.make_async_remote_copy(
            src_ref=self.qs[w].at[j], dst_ref=self.rels[w].at[j],
            send_sem=self.send_sems.at[3 * w + j], recv_sem=self.recv_sems.at[3 * w + j],
            device_id=(*chips[j], c), device_id_type=MESH_ID) for w in range(self.n) for j in range(3)]

    def issue(self):
        for cp in self._copies():
            cp.start()

    def finish(self):
        cps = self._copies()
        for cp in cps:
            cp.wait_recv()
        for cp in cps:
            cp.wait_send()


def _reduce_scatter(parts, small):
    n = len(parts)
    blks = [p.shape[1:] for p in parts]

    def body(*refs):
        ps, small_ref = refs[:n], refs[n]
        o = n + 1
        owns, sibs, rels, small_out = refs[o:o + n], refs[o + n:o + 2 * n], refs[o + 2 * n:o + 3 * n], refs[o + 3 * n]
        o += 3 * n + 1
        pa, pb, qst = refs[o:o + n], refs[o + n:o + 2 * n], refs[o + 2 * n:o + 3 * n]
        s1_send, s1_recv, s2_send, s2_recv, sm_send, sm_recv, lsem = refs[o + 3 * n:]
        x, y, c = _place()
        me = 4 * x + 2 * y + c
        sibling = (x, y, 1 - c)
        chips = [(1 - x, y), (x, 1 - y), (1 - x, 1 - y)]
        all_chips = [(x, y)] + chips

        own_cps = []
        for w in range(n):
            cp = pltpu.make_async_copy(ps[w].at[me], owns[w], lsem.at[w])
            cp.start()
            own_cps.append(cp)
        sm_own = pltpu.make_async_copy(small_ref, small_out.at[me], lsem.at[n])
        sm_own.start()

        def small_copy(r):
            peer = ((x + (r >> 2)) % 2, (y + ((r >> 1) & 1)) % 2, (c + (r & 1)) % 2)
            return pltpu.make_async_remote_copy(
                src_ref=small_ref, dst_ref=small_out.at[me], send_sem=sm_send.at[r - 1], recv_sem=sm_recv.at[r - 1],
                device_id=peer, device_id_type=MESH_ID)

        sm_cps = [small_copy(r) for r in range(1, NDEV)]
        for cp in sm_cps:
            cp.start()

        def pair_copy(w, rel):
            cx, cy = all_chips[rel]
            return pltpu.make_async_remote_copy(
                src_ref=ps[w].at[4 * cx + 2 * cy + (1 - c)], dst_ref=sibs[w].at[rel],
                send_sem=s1_send.at[4 * w + rel], recv_sem=s1_recv.at[4 * w + rel],
                device_id=sibling, device_id_type=MESH_ID)

        def chip_copy(w, j):
            return pltpu.make_async_remote_copy(
                src_ref=qst[w].at[j], dst_ref=rels[w].at[j],
                send_sem=s2_send.at[3 * w + j], recv_sem=s2_recv.at[3 * w + j],
                device_id=(*chips[j], c), device_id_type=MESH_ID)

        pair_cps = [pair_copy(w, rel) for w in range(n) for rel in (1, 2, 3, 0)]
        for cp in pair_cps:
            cp.start()
        chip_cps = []
        for w in range(n):
            for j, (cx, cy) in enumerate(chips):
                pair_copy(w, 1 + j).wait_recv()
                la = pltpu.make_async_copy(ps[w].at[4 * cx + 2 * cy + c], pa[w], lsem.at[n + 1])
                lb = pltpu.make_async_copy(sibs[w].at[1 + j], pb[w], lsem.at[n + 2])
                la.start()
                lb.start()
                la.wait()
                lb.wait()
                qst[w][j] = (pa[w][...].astype(F32) + pb[w][...].astype(F32)).astype(BF16)
                cp = chip_copy(w, j)
                cp.start()
                chip_cps.append(cp)
        for w in range(n):
            pair_copy(w, 0).wait_recv()
            for j in range(3):
                chip_copy(w, j).wait_recv()
        for cp in sm_cps:
            cp.wait_recv()
        for cp in pair_cps + chip_cps + sm_cps:
            cp.wait_send()
        for cp in own_cps:
            cp.wait()
        sm_own.wait()

    any_spec = pl.BlockSpec(memory_space=pl.ANY)
    outs = pl.pallas_call(
        body, name="rs_grads",
        out_shape=[jax.ShapeDtypeStruct(b, BF16) for b in blks]
        + [jax.ShapeDtypeStruct((4, *b), BF16) for b in blks]
        + [jax.ShapeDtypeStruct((3, *b), BF16) for b in blks]
        + [jax.ShapeDtypeStruct((NDEV, *small.shape), F32)],
        in_specs=[any_spec] * (n + 1),
        out_specs=[any_spec] * (3 * n + 1),
        scratch_shapes=[pltpu.VMEM(b, BF16) for b in blks] + [pltpu.VMEM(b, BF16) for b in blks]
        + [pltpu.VMEM((3, *b), BF16) for b in blks]
        + [pltpu.SemaphoreType.DMA((4 * n,)), pltpu.SemaphoreType.DMA((4 * n,)),
           pltpu.SemaphoreType.DMA((3 * n,)), pltpu.SemaphoreType.DMA((3 * n,)),
           pltpu.SemaphoreType.DMA((NDEV - 1,)), pltpu.SemaphoreType.DMA((NDEV - 1,)),
           pltpu.SemaphoreType.DMA((n + 3,))],
        compiler_params=pltpu.CompilerParams(vmem_limit_bytes=40 * MIB),
    )(*parts, small)
    return outs[:n], outs[n:2 * n], outs[2 * n:3 * n], outs[3 * n]


class _PairSum:
    def __init__(self, parts):
        self.n = n = len(parts)
        blks = [p.shape[1:] for p in parts]
        self.out_shape = ([jax.ShapeDtypeStruct(b, BF16) for b in blks] + [jax.ShapeDtypeStruct((1, *b), BF16) for b in blks]
                          + [jax.ShapeDtypeStruct((3, *b), BF16) for b in blks])
        self.scratch = [pltpu.VMEM((3, *b), BF16) for b in blks] * 3 + [
            pltpu.SemaphoreType.DMA((4 * n,)), pltpu.SemaphoreType.DMA((4 * n,)), pltpu.SemaphoreType.DMA((5 * n,))]

    def bind(self, ps, outs, scratch):
        n = self.n
        self.ps, self.owns, self.sibs, self.qs = ps, outs[:n], outs[n:2 * n], outs[2 * n:]
        self.pa, self.pb, self.qst = scratch[:n], scratch[n:2 * n], scratch[2 * n:3 * n]
        self.s_send, self.s_recv, self.lsem = scratch[3 * n:]
        return self

    def _local(self):
        n = self.n
        x, y, c = _place()
        chips = [(1 - x, y), (x, 1 - y), (1 - x, 1 - y)]
        own = [pltpu.make_async_copy(self.ps[w].at[4 * x + 2 * y + c], self.owns[w], self.lsem.at[w]) for w in range(n)]
        mine = [[pltpu.make_async_copy(self.ps[w].at[4 * cx + 2 * cy + c], self.pa[w].at[j], self.lsem.at[2 * n + 3 * w + j])
                 for j, (cx, cy) in enumerate(chips)] for w in range(n)]
        outq = [pltpu.make_async_copy(self.qst[w], self.qs[w], self.lsem.at[n + w]) for w in range(n)]
        return own, mine, outq

    def _pair(self, w, rel):
        x, y, c = _place()
        cx, cy = [(x, y), (1 - x, y), (x, 1 - y), (1 - x, 1 - y)][rel]
        return pltpu.make_async_remote_copy(
            src_ref=self.ps[w].at[4 * cx + 2 * cy + (1 - c)],
            dst_ref=self.sibs[w].at[0] if rel == 0 else self.pb[w].at[rel - 1],
            send_sem=self.s_send.at[4 * w + rel], recv_sem=self.s_recv.at[4 * w + rel],
            device_id=(x, y, 1 - c), device_id_type=MESH_ID)

    def issue(self):
        own, mine, _ = self._local()
        for cp in own + [cp for row in mine for cp in row]:
            cp.start()
        for w in range(self.n):
            for rel in (1, 2, 3, 0):
                self._pair(w, rel).start()

    def finish(self):
        own, mine, outq = self._local()
        for w in range(self.n):
            for j in range(3):
                self._pair(w, 1 + j).wait_recv()
                mine[w][j].wait()
                self.qst[w][j] = (self.pa[w][j].astype(F32) + self.pb[w][j].astype(F32)).astype(BF16)
            outq[w].start()
        for w in range(self.n):
            self._pair(w, 0).wait_recv()
        for w in range(self.n):
            for rel in range(4):
                self._pair(w, rel).wait_send()
        for cp in own + outq:
            cp.wait()

    def results(self, outs):
        n = self.n
        return outs[:n], outs[n:2 * n], outs[2 * n:3 * n]


def _rs_pair(name, parts):
    ps = _PairSum(parts)
    n = ps.n

    def body(*refs):
        ps.bind(refs[:n], refs[n:4 * n], refs[4 * n:])
        ps.issue()
        ps.finish()

    any_spec = pl.BlockSpec(memory_space=pl.ANY)
    outs = pl.pallas_call(
        body, name=name, out_shape=ps.out_shape,
        in_specs=[any_spec] * n, out_specs=[any_spec] * (3 * n), scratch_shapes=ps.scratch,
        compiler_params=pltpu.CompilerParams(vmem_limit_bytes=48 * MIB),
    )(*parts)
    return ps.results(outs)


def _adamw_math(g, w, m, v):
    m = ADAM_B1 * m + (1.0 - ADAM_B1) * g
    v = ADAM_B2 * v + (1.0 - ADAM_B2) * (g * g)
    m_hat = m / (1.0 - ADAM_B1 ** ADAM_STEP)
    v_hat = v / (1.0 - ADAM_B2 ** ADAM_STEP)
    delta = -ADAM_LR * (m_hat / (jnp.sqrt(v_hat) + ADAM_EPS) + ADAM_WD * w)
    return delta, m, v


def _adamw_multi(name, own, sib, rel, ws, ms, vs, row_grid):
    k_n, r_n, c_n = own.shape
    rbk = r_n // row_grid

    def body(*refs):
        own_ref, sib_ref, r0_ref, r1_ref, r2_ref = refs[:5]
        w_refs, m_refs, v_refs = refs[5:5 + k_n], refs[5 + k_n:5 + 2 * k_n], refs[5 + 2 * k_n:5 + 3 * k_n]
        outs = refs[5 + 3 * k_n:]
        for k in range(k_n):
            g = own_ref[k].astype(F32) + sib_ref[k].astype(F32)
            g = g + r0_ref[k].astype(F32)
            g = g + r1_ref[k].astype(F32)
            g = g + r2_ref[k].astype(F32)
            delta, mm, vv = _adamw_math(g, w_refs[k][0], m_refs[k][0], v_refs[k][0])
            outs[4 * k][0] = g
            outs[4 * k + 1][0] = delta
            outs[4 * k + 2][0] = mm
            outs[4 * k + 3][0] = vv

    def lead(j):
        return pl.BlockSpec((None, k_n, rbk, c_n), lambda g: (j, 0, g, 0))

    wspec = pl.BlockSpec((1, rbk, c_n), lambda g: (0, g, 0))
    shp = jax.ShapeDtypeStruct((1, r_n, c_n), F32)
    res = pl.pallas_call(
        body, name=name, grid=(row_grid,),
        in_specs=[pl.BlockSpec((k_n, rbk, c_n), lambda g: (0, g, 0)), lead(0), lead(0), lead(1), lead(2)] + [wspec] * (3 * k_n),
        out_specs=[wspec] * (4 * k_n), out_shape=[shp] * (4 * k_n),
        compiler_params=_params(("arbitrary",), 40),
    )(own, sib, rel, rel, rel, *ws, *ms, *vs)
    return [tuple(res[4 * k:4 * k + 4]) for k in range(k_n)]


def _adamw_meta_dw(own, sib, rel, meta, dw):
    def body(own_ref, sib_ref, rel_ref, wm, mm, vm, wd, md, vd, *outs):
        def gsum(rows):
            g = own_ref[rows, :].astype(F32) + sib_ref[0, rows, :].astype(F32)
            for j in range(3):
                g = g + rel_ref[j, rows, :].astype(F32)
            return g

        g = gsum(pl.ds(0, N_META))
        delta, m2, v2 = _adamw_math(g, wm[...], mm[...], vm[...])
        for o, val in zip(outs[:4], (g, delta, m2, v2)):
            o[...] = val
        g = gsum(pl.ds(N_META, CONV_K))
        delta, m2, v2 = _adamw_math(g, wd[0], md[0], vd[0])
        for o, val in zip(outs[4:], (g, delta, m2, v2)):
            o[0] = val

    s_meta = jax.ShapeDtypeStruct(meta[0].shape, F32)
    s_dw = jax.ShapeDtypeStruct(dw[0].shape, F32)
    res = pl.pallas_call(body, name="adamw_meta_dw", out_shape=[s_meta] * 4 + [s_dw] * 4)(own, sib, rel, *meta, *dw)
    return tuple(res[:4]), tuple(res[4:])


REP_ROWS = 16


def _adamw_rep(gathered, ws, ms, vs):
    rows = [(0, 1), (1, 2), (3, 1), (4, 1), (5, 1), (6, 1), (7, 1), (8, 1)]

    def body(g_ref, *refs):
        w_refs, m_refs, v_refs = refs[:8], refs[8:16], refs[16:24]
        loss_ref, outs, acc = refs[24], refs[25:57], refs[57]
        g = g_ref[0]
        for d in range(1, NDEV):
            g = g + g_ref[d]
        acc[...] = g
        loss_ref[...] = (0.5 / D) * jnp.sum(acc[pl.ds(9, 1), :], axis=1, keepdims=True)
        for p, (r0, nr) in enumerate(rows):
            for h in range(nr):
                cols = pl.ds(h * D, D)
                gp = acc[pl.ds(r0 + h, 1), :]
                delta, mm, vv = _adamw_math(gp, w_refs[p][:, cols], m_refs[p][:, cols], v_refs[p][:, cols])
                for o, val in zip(outs[4 * p:4 * p + 4], (gp, delta, mm, vv)):
                    o[:, cols] = val

    shapes = [jax.ShapeDtypeStruct(w.shape, F32) for w in ws]
    res = pl.pallas_call(
        body, name="adamw_rep",
        out_shape=[jax.ShapeDtypeStruct((1, 1), F32)] + [s for s in shapes for _ in range(4)],
        scratch_shapes=[pltpu.VMEM((REP_ROWS, D), F32)],
    )(gathered, *ws, *ms, *vs)
    return res[0], [tuple(res[1 + 4 * p:5 + 4 * p]) for p in range(8)]


def _load_ffn(i, j, wgu_hbm, wgu, wdn_hbm, wdn, sems):
    half = NDEV // 2

    def copies(ch):
        pairs = [(wgu_hbm.at[half * ch + d, g], wgu.at[g, ch, pl.ds(FFB * d, FFB), :]) for g in range(2) for d in range(half)]
        pairs.append((wdn_hbm.at[ch], wdn.at[ch]))
        return [pltpu.make_async_copy(s, t, sems.at[(2 * half + 1) * ch + k]) for k, (s, t) in enumerate(pairs)]

    @pl.when((i == 0) & (j == 0))
    def _():
        for cp in copies(0) + copies(1):
            cp.start()

    for ch in range(2):
        @pl.when((i == 0) & (j == ch))
        def _():
            for cp in copies(ch):
                cp.wait()


def _win_pairs(w_hbm, w_vm):
    return [(w_hbm.at[q], w_vm.at[q // 2, :, pl.ds(2 * INB * (q % 2), 2 * INB)]) for q in range(4)]


def _whole(a):
    nd = a.ndim
    return pl.BlockSpec(a.shape, lambda *g: (0,) * nd)


CHIPW = 2 * INB
PHASE_CHIP = (1, 0, 2)


class _GatherIn:
    scratch = [pltpu.VMEM((D, INB), BF16), pltpu.SemaphoreType.DMA((7,)), pltpu.SemaphoreType.DMA((7,)),
               pltpu.SemaphoreType.DMA((1,))]

    def bind(self, w_ref, w_vm, scratch):
        self.w_ref, self.w_vm = w_ref, w_vm
        self.stage, self.send_sems, self.recv_sems, self.local_sem = scratch
        return self

    def _win(self, chip, core):
        return self.w_vm.at[2 * chip[0] + chip[1], :, pl.ds(INB * core, INB)]

    def _copy(self, k, chip, core, to, src=None):
        dst = self._win(chip, core)
        return pltpu.make_async_remote_copy(
            src_ref=dst if src is None else src, dst_ref=dst, send_sem=self.send_sems.at[k],
            recv_sem=self.recv_sems.at[k], device_id=to, device_id_type=MESH_ID)

    def _mine(self, cs):
        x, y, _ = _place()
        return pltpu.make_async_copy(self.stage, self._win((x, y), cs), self.local_sem.at[0])

    def issue(self, cs):
        x, y, _ = _place()
        chips = [(1 - x, y), (x, 1 - y), (1 - x, 1 - y)]
        self.stage[...] = self.w_ref[0].astype(BF16)
        self._mine(cs).start()
        self._copy(0, (x, y), cs, (x, y, 1 - cs), src=self.stage).start()
        for j, chip in enumerate(chips):
            self._copy(1 + j, (x, y), cs, (*chip, cs), src=self.stage).start()

    def wait_chip(self, phase, cs):
        x, y, _ = _place()
        chips = [(1 - x, y), (x, 1 - y), (1 - x, 1 - y)]
        if phase == 0:
            self._mine(cs).wait()
            self._copy(0, (x, y), 1 - cs, (x, y, cs)).wait_recv()
            return
        if phase == 1:
            for j in PHASE_CHIP:
                self._copy(1 + j, chips[j], cs, (x, y, cs)).wait_recv()
                self._copy(4 + j, chips[j], cs, (x, y, 1 - cs)).start()
        j = PHASE_CHIP[phase - 1]
        self._copy(4 + j, chips[j], 1 - cs, (x, y, cs)).wait_recv()

    def finish(self, cs):
        x, y, _ = _place()
        for k in range(7):
            self._copy(k, (x, y), cs, (x, y, cs), src=self.stage).wait_send()


def _fwd_in(x2, g_mix, w_in, order, tp, ag, ags):
    tm = _pick(tp, TM_IO)
    nt = tp // tm
    nx_last = x2.shape[0] - (nt - 1) * tm
    na, ng, ns = len(ag.arrays), ag.n, len(ags.arrays)
    gin = _GatherIn()

    def body(order_ref, *refs):
        x_ref, g_ref, w_ref = refs[:3]
        o = 3 + na + ns
        h_ref, z_ref, u_ref, wout_ref = refs[o:o + 4]
        s = o + 4 + ng + 1
        w_vm, u_all, osem, sm_vm = refs[s:s + 4]
        gin.bind(w_ref, w_vm, refs[s + 4:s + 8])
        ag.bind(refs[3:3 + na], refs[o + 4:o + 4 + ng], refs[s + 8:s + 8 + len(ag.scratch)])
        ags.bind(refs[3 + na:3 + na + ns], refs[o + 4 + ng:o + 5 + ng], refs[s + 8 + len(ag.scratch):])
        ph, i = pl.program_id(0), pl.program_id(1)
        core = lax.axis_index("c")
        first = (ph == 0) & (i == 0)
        last = (ph == 3) & (i == nt - 1)
        for cs in range(2):
            @pl.when(first & (core == cs))
            def _():
                gin.issue(cs)

        @pl.when(first)
        def _():
            ags.issue()
            ag.issue()

        @pl.when((ph == 0) & (i == max(nt - 2, 0)))
        def _():
            ags.forward()

        for cs in range(2):
            for p in range(4):
                @pl.when((ph == p) & (i == 0) & (core == cs))
                def _():
                    gin.wait_chip(p, cs)

        @pl.when((ph == 3) & (i == max(nt - 2, 0)))
        def _():
            ag.forward()

        out_copy = pltpu.make_async_copy(w_vm, wout_ref, osem.at[0])

        @pl.when((ph == 3) & (i == 0))
        def _():
            out_copy.start()

        @pl.when((ph == 0) & (i < nt - 1))
        def _():
            h_ref[...] = x_ref[...]

        @pl.when((ph == 0) & (i == nt - 1))
        def _():
            ags.finish()
            cp = pltpu.make_async_copy(ags.outs[0], sm_vm, osem.at[1])
            cp.start()
            h_ref[pl.ds(0, nx_last), :] = x_ref[pl.ds(0, nx_last), :]
            h_ref[pl.ds(nx_last, tm - nx_last - N_META), :] = jnp.zeros((tm - nx_last - N_META, D), F32)
            cp.wait()
            for d in range(NDEV):
                h_ref[pl.ds(tm - N_META, N_META), pl.ds(128 * d, 128)] = sm_vm[d, pl.ds(0, N_META), :]

        @pl.when(ph == 0)
        def _():
            xv = h_ref[...]
            r = lax.rsqrt(jnp.mean(xv * xv, axis=-1, keepdims=True) + RMS_EPS)
            u = (xv * r * g_ref[...]).astype(BF16)
            u_ref[...] = u
            u_all[i] = u

        z_ref[...] = _dot(u_all[i], w_vm[order_ref[ph]])

        @pl.when(last)
        def _():
            ag.finish()
            out_copy.wait()

        for cs in range(2):
            @pl.when(last & (core == cs))
            def _():
                gin.finish(cs)

    def rows(ph, i, order):
        return (jnp.where(ph == 0, i, nt - 1), 0)

    tile = pl.BlockSpec((tm, D), rows)
    anys = pl.BlockSpec(memory_space=pl.ANY)
    res = pl.pallas_call(
        body, name="fwd_in",
        grid_spec=pltpu.PrefetchScalarGridSpec(
            num_scalar_prefetch=1, grid=(4, nt),
            in_specs=[tile, pl.BlockSpec((1, D), lambda ph, i, order: (0, 0)), _whole(w_in)]
            + [_whole(a) for a in ag.arrays + ags.arrays],
            out_specs=[tile, pl.BlockSpec((tm, CHIPW), lambda ph, i, order: (i, order[ph])), tile, anys] + [anys] * (ng + 1),
            scratch_shapes=[pltpu.VMEM((4, D, CHIPW), BF16), pltpu.VMEM((nt, tm, D), BF16), pltpu.SemaphoreType.DMA((2,)),
                            pltpu.VMEM(ags.out_shape[0].shape, F32)] + gin.scratch + ag.scratch + ags.scratch),
        out_shape=[jax.ShapeDtypeStruct((tp, D), F32), jax.ShapeDtypeStruct((tp, DIN), F32),
                   jax.ShapeDtypeStruct((tp, D), BF16), jax.ShapeDtypeStruct((4, D, CHIPW), BF16)]
        + ag.out_shape + ags.out_shape,
        compiler_params=_params(("arbitrary", "arbitrary"), 58),
    )(order, x2, g_mix, w_in, *ag.arrays, *ags.arrays)
    return res[:4], res[4:4 + ng], res[4 + ng]


def _halo_specs(col, nt, width=D):
    r = TM // HALO
    nb = nt * r
    return [pl.BlockSpec((HALO, width), lambda i: ((i * r + nb - 1) % nb, col)),
            pl.BlockSpec((TM, width), lambda i: (i, col)),
            pl.BlockSpec((HALO, width), lambda i: (((i + 1) * r) % nb, col))]


NCB = D // 128
TME = TM + 2 * HALO


def _tm_fill(dst, time0, groups, tile_fn):
    def body(g, c):
        for j in range(NCB):
            dst[pl.ds((time0 + 8 * g) * NCB + j, 8, stride=NCB), :] = tile_fn(pl.multiple_of(8 * g, 8), pl.ds(128 * j, 128))
        return c

    lax.fori_loop(0, groups, body, 0)


def _tm_fill_ext(dst, left, cur, right, fn):
    _tm_fill(dst, 0, HALO // 8, lambda r, l: fn(left, pl.ds(r, 8), l))
    _tm_fill(dst, HALO, TM // 8, lambda r, l: fn(cur, pl.ds(r, 8), l))
    _tm_fill(dst, HALO + TM, HALO // 8, lambda r, l: fn(right, pl.ds(r, 8), l))


def _tm_read(src, groups, store_fn):
    def body(g, c):
        for j in range(NCB):
            store_fn(pl.ds(pl.multiple_of(8 * g, 8), 8), pl.ds(128 * j, 128), src[pl.ds(8 * g * NCB + j, 8, stride=NCB), :])
        return c

    lax.fori_loop(0, groups, body, 0)


def _tm_rows(t):
    return pl.ds(t * NCB if isinstance(t, int) else pl.multiple_of(t * NCB, NCB), NCB)


def _tm_at(ref, t):
    return ref[_tm_rows(t), :]


def _by_group(sub, vals):
    return jnp.where(sub < 2, vals[0], jnp.where(sub < 4, vals[1], jnp.where(sub < 6, vals[2], vals[3])))


def _pool_cnt(b, seq, tp, sub):
    b = jnp.where(b < 0, b + tp, b)
    b = jnp.where(b >= tp, b - tp, b)
    t = jnp.where(b < seq, b + N_META, b - (tp - N_META))
    cnts = []
    for win in POOL_WINDOWS:
        left = win // 2
        lo = jnp.maximum(t - left, 0)
        hi = jnp.minimum(t + win - left, seq + N_META)
        cnts.append(jnp.maximum(hi - lo, 1).astype(F32))
    return _by_group(sub, cnts)


def _edge_rows(seq, tp):
    reach = max(POOL_WINDOWS) // 2
    return [tp - N_META + t for t in range(reach)] + [seq - reach + 1 + t for t in range(reach - 1)]


def _edge_gain(b, seq, tp, sub):
    return _by_group(sub, [float(w) for w in POOL_WINDOWS]) / _pool_cnt(b, seq, tp, sub)


def _nested_windows(at, lo_offs):
    sums, s, have = [], None, set()
    for g, win in enumerate(POOL_WINDOWS):
        for o in range(lo_offs[g], lo_offs[g] + win):
            if o not in have:
                have.add(o)
                s = at(o) if s is None else s + at(o)
        sums.append(s)
    return sums


def _seq_fwd(z, w_dw, b_dw, seq, gat):
    tp = z.shape[0]
    nt = tp // TM
    na, ng = len(gat.arrays), gat.n

    def body(*refs):
        av_l, av, av_r, ag_l, ag, ag_r, p_l, p, p_r, w_ref, b_ref = refs[:11]
        ac_ref, m_ref = refs[11 + na:13 + na]
        a3, p3, o3, m3, w3, b3, m2d = refs[13 + na + ng:20 + na + ng]
        gat.bind(refs[11:11 + na], refs[13 + na:13 + na + ng], refs[20 + na + ng:])
        i = pl.program_id(0)
        sub = lax.broadcasted_iota(jnp.int32, (NCB, 128), 0)

        @pl.when(i == 0)
        def _():
            gat.issue()
            _tm_fill(w3, 0, 4, lambda r, l: w_ref[pl.ds(r, 8), l])
            for j in range(NCB):
                b3[pl.ds(j, 1), :] = b_ref[:, pl.ds(128 * j, 128)]

        @pl.when(i == max(nt - 2, 0))
        def _():
            gat.forward()

        _tm_fill_ext(a3, (av_l, ag_l), (av, ag), (av_r, ag_r), lambda vg, r, l: vg[0][r, l] * _sig(vg[1][r, l]))
        _tm_fill_ext(p3, p_l, p, p_r, lambda ref, r, l: ref[r, l])

        def conv(g, c):
            for t in range(8):
                acc = b3[...]
                for k in range(CONV_K):
                    acc = acc + _tm_at(w3, k) * _tm_at(a3, 8 * g + t + k + 1)
                o3[_tm_rows(8 * g + t), :] = acc
            return c

        lax.fori_loop(0, TM // 8, conv, 0)
        _tm_read(o3, TM // 8, lambda r, l, tile: ac_ref.__setitem__((r, l), tile))

        inv = _by_group(sub, [1.0 / w for w in POOL_WINDOWS])

        def pool(g, c):
            for t in range(8):
                e = 8 * g + t + HALO
                sums = _nested_windows(lambda o: _tm_at(p3, e + o), [-(w // 2) for w in POOL_WINDOWS])
                m3[_tm_rows(8 * g + t), :] = _by_group(sub, sums) * inv - _tm_at(p3, e)
            return c

        lax.fori_loop(0, TM // 8, pool, 0)
        for b in _edge_rows(seq, tp):
            r = b - i * TM

            @pl.when((r >= 0) & (r < TM))
            def _():
                pv = _tm_at(p3, r + HALO)
                m3[_tm_rows(r), :] = (_tm_at(m3, r) + pv) * _edge_gain(b, seq, tp, sub) - pv

        _tm_read(m3, TM // 8, lambda r, l, tile: m2d.__setitem__((r, l), tile))
        m_ref[...] = m2d[...].astype(BF16)

        @pl.when(i == nt - 1)
        def _():
            gat.finish()

    tmaj = pltpu.VMEM((TM * NCB, 128), F32)
    text = pltpu.VMEM((TME * NCB, 128), F32)
    res = pl.pallas_call(
        body, name="seq_fwd", grid=(nt,),
        in_specs=_halo_specs(0, nt) + _halo_specs(1, nt) + _halo_specs(2, nt)
        + [pl.BlockSpec((32, D), lambda i: (0, 0)), pl.BlockSpec((1, D), lambda i: (0, 0))] + [_whole(a) for a in gat.arrays],
        out_specs=[pl.BlockSpec((TM, D), lambda i: (i, 0))] * 2 + [pl.BlockSpec(memory_space=pl.ANY)] * ng,
        out_shape=[jax.ShapeDtypeStruct((tp, D), F32), jax.ShapeDtypeStruct((tp, D), BF16)] + gat.out_shape,
        scratch_shapes=[text, text, tmaj, tmaj, pltpu.VMEM((32 * NCB, 128), F32), pltpu.VMEM((NCB, 128), F32),
                        pltpu.VMEM((TM, D), F32)] + gat.scratch,
        compiler_params=_params(("arbitrary",), 52),
    )(z, z, z, z, z, z, z, z, z, w_dw, b_dw, *gat.arrays)
    return res[:2], res[2:]


def _ln_stats(ac):
    mu = jnp.mean(ac, axis=-1, keepdims=True)
    xc = ac - mu
    rl = lax.rsqrt(jnp.mean(xc * xc, axis=-1, keepdims=True) + LN_EPS)
    return xc * rl, rl


def _pool_mix(m, wp_ref):
    return jnp.concatenate(
        [_dot(m[:, g * PG:(g + 1) * PG], wp_ref[:, g].reshape(PG, PG)) for g in range(4)], axis=1)


def _mix_fwd(ac, m, z, h0, b_gate, ln_g, ln_b, pool_scale, g_mixw, g_pool, gat):
    tp = h0.shape[0]
    tms = TM
    nt = tp // tms
    na, ng = len(gat.arrays), gat.n

    def body(*refs):
        ac_ref, m_ref, zga, zgb, h_ref, bg_ref, lg_ref, lb_ref, ps_ref, wm_hbm, wp_hbm = refs[:11]
        h1_ref, s_ref, mg_ref, q_ref = refs[11 + na:15 + na]
        wm, wp, sems = refs[15 + na + ng:18 + na + ng]
        gat.bind(refs[11:11 + na], refs[15 + na:15 + na + ng], refs[18 + na + ng:])
        i = pl.program_id(0)

        @pl.when(i == 0)
        def _():
            gat.issue()

        @pl.when(i == max(nt - 4, 0))
        def _():
            gat.forward()

        @pl.when(i == nt - 1)
        def _():
            gat.finish()

        _load_once(i == 0, [(wm_hbm, wm), (wp_hbm, wp)], sems)
        n, _ = _ln_stats(ac_ref[...])
        l = n * lg_ref[...] + lb_ref[...]
        s = (l * _sig(l)).astype(BF16)
        s_ref[...] = s
        yc = _dot(s, wm[:, 0].reshape(D, D))
        q = (_pool_mix(m_ref[...], wp) * ps_ref[...]).astype(BF16)
        q_ref[...] = q
        yp = _dot(q, wm[:, 1].reshape(D, D))
        ga = _sig(zga[...] + bg_ref[:, :D])
        gb = _sig(zgb[...] + bg_ref[:, D:])
        merged = (ga * yc + gb * yp).astype(BF16)
        mg_ref[...] = merged
        h1_ref[...] = h_ref[...] + _dot(merged, wm[:, 2].reshape(D, D))

    def tile(col=0):
        return pl.BlockSpec((tms, D), lambda i: (i, col))

    def vec(w):
        return pl.BlockSpec((1, w), lambda i: (0, 0))

    anys = pl.BlockSpec(memory_space=pl.ANY)
    f32o, b16o = jax.ShapeDtypeStruct((tp, D), F32), jax.ShapeDtypeStruct((tp, D), BF16)
    res = pl.pallas_call(
        body, name="mix_fwd", grid=(nt,),
        in_specs=[tile(), tile(), tile(3), tile(4), tile(), vec(2 * D), vec(D), vec(D), vec(D), anys, anys]
        + [_whole(a) for a in gat.arrays],
        out_specs=[tile()] * 4 + [anys] * ng,
        out_shape=[f32o, b16o, b16o, b16o] + gat.out_shape,
        scratch_shapes=[pltpu.VMEM((NDEV, 3, D // NDEV, D), BF16), pltpu.VMEM((NDEV, 4, PG // NDEV, PG), BF16),
                        pltpu.SemaphoreType.DMA((2,))] + gat.scratch,
        compiler_params=_params(("arbitrary",), 52),
    )(ac, m, z, z, h0, b_gate, ln_g, ln_b, pool_scale, g_mixw, g_pool, *gat.arrays)
    return res[:4], res[4:]


def _ffn_fwd(h1, tgt, g_ffn, g_final, w_gu, w_dn):
    tp = h1.shape[0]
    nt = tp // TM
    nx_last = tgt.shape[0] - (nt - 1) * TM

    def body(h_ref, t_ref, gf_ref, gl_ref, wgu_hbm, wdn_hbm,
             fg_ref, fu_ref, v_ref, f_ref, dh2_ref, acc_ref, wgu, wdn, v_sc, h2_sc, diff_sc, sems):
        i, j = pl.program_id(0), pl.program_id(1)
        _load_ffn(i, j, wgu_hbm, wgu, wdn_hbm, wdn, sems)

        @pl.when((i == 0) & (j == 0))
        def _():
            acc_ref[...] = jnp.zeros_like(acc_ref)

        @pl.when(j == 0)
        def _():
            h = h_ref[...]
            r = lax.rsqrt(jnp.mean(h * h, axis=-1, keepdims=True) + RMS_EPS)
            v = (h * r * gf_ref[...]).astype(BF16)
            v_sc[...] = v
            v_ref[...] = v
            h2_sc[...] = h

        v = v_sc[...]
        fg = _dot_nt(v, wgu[0, j])
        fu = _dot_nt(v, wgu[1, j])
        fg_ref[...] = fg
        fu_ref[...] = fu
        f = ((fg * _sig(fg)) * fu).astype(BF16)
        f_ref[...] = f
        h2_sc[...] += _dot(f, wdn[j])

        @pl.when(j == 1)
        def _():
            h2 = h2_sc[...]
            r = lax.rsqrt(jnp.mean(h2 * h2, axis=-1, keepdims=True) + RMS_EPS)
            n2 = h2 * r
            y = n2 * gl_ref[...]

            @pl.when(i < nt - 1)
            def _():
                diff_sc[...] = y - t_ref[...]

            @pl.when(i == nt - 1)
            def _():
                diff_sc[pl.ds(0, nx_last), :] = y[:nx_last] - t_ref[pl.ds(0, nx_last), :]
                diff_sc[pl.ds(nx_last, TM - nx_last), :] = jnp.zeros((TM - nx_last, D), F32)

            diff = diff_sc[...]
            dy = diff * (1.0 / D)
            acc_ref[0:1, :] += jnp.sum(diff * diff, axis=0, keepdims=True)
            acc_ref[1:2, :] += jnp.sum(dy * n2, axis=0, keepdims=True)
            dn = dy * gl_ref[...]
            dh2_ref[...] = r * (dn - n2 * jnp.mean(dn * n2, axis=-1, keepdims=True))

    def tile():
        return pl.BlockSpec((TM, D), lambda i, j: (i, 0))

    def chunk():
        return pl.BlockSpec((TM, FFC), lambda i, j: (i, j))

    def vec():
        return pl.BlockSpec((1, D), lambda i, j: (0, 0))

    anys = pl.BlockSpec(memory_space=pl.ANY)
    hid32, hid16 = jax.ShapeDtypeStruct((tp, DFF), F32), jax.ShapeDtypeStruct((tp, DFF), BF16)
    return pl.pallas_call(
        body, name="ffn_fwd", grid=(nt, 2),
        in_specs=[tile(), tile(), vec(), vec(), anys, anys],
        out_specs=[chunk(), chunk(), tile(), chunk(), tile(), pl.BlockSpec((8, D), lambda i, j: (0, 0))],
        out_shape=[hid32, hid32, jax.ShapeDtypeStruct((tp, D), BF16), hid16, jax.ShapeDtypeStruct((tp, D), F32),
                   jax.ShapeDtypeStruct((8, D), F32)],
        scratch_shapes=[pltpu.VMEM((2, 2, FFC, D), BF16), pltpu.VMEM((2, FFC, D), BF16),
                        pltpu.VMEM((TM, D), BF16), pltpu.VMEM((TM, D), F32), pltpu.VMEM((TM, D), F32),
                        pltpu.SemaphoreType.DMA((2 * NDEV + 2,))],
        compiler_params=_params(("arbitrary", "arbitrary"), 56),
    )(h1, tgt, g_ffn, g_final, w_gu, w_dn)


def _ffn_bwd(dh2, fg, fu, h1, g_ffn, w_gu, w_dn):
    tp = h1.shape[0]
    nt = tp // TM

    def body(dh2_ref, fg_ref, fu_ref, h_ref, gf_ref, wgu_hbm, wdn_hbm,
             dfg_ref, dfu_ref, dh1_ref, acc_ref, wgu, wdn, d_sc, dv_sc, sems):
        i, j = pl.program_id(0), pl.program_id(1)
        _load_ffn(i, j, wgu_hbm, wgu, wdn_hbm, wdn, sems)

        @pl.when((i == 0) & (j == 0))
        def _():
            acc_ref[...] = jnp.zeros_like(acc_ref)

        @pl.when(j == 0)
        def _():
            d_sc[...] = dh2_ref[...].astype(BF16)
            dv_sc[...] = jnp.zeros_like(dv_sc)

        df = _dot_nt(d_sc[...], wdn[j])
        fg = fg_ref[...]
        sg = _sig(fg)
        dfu = (df * (fg * sg)).astype(BF16)
        dfg = (df * fu_ref[...] * (sg * (1.0 + fg * (1.0 - sg)))).astype(BF16)
        dfg_ref[...] = dfg
        dfu_ref[...] = dfu
        dv_sc[...] += _dot(dfg, wgu[0, j]) + _dot(dfu, wgu[1, j])

        @pl.when(j == 1)
        def _():
            h = h_ref[...]
            r = lax.rsqrt(jnp.mean(h * h, axis=-1, keepdims=True) + RMS_EPS)
            n1 = h * r
            dv = dv_sc[...]
            acc_ref[0:1, :] += jnp.sum(dv * n1, axis=0, keepdims=True)
            dn = dv * gf_ref[...]
            dh1_ref[...] = dh2_ref[...] + r * (dn - n1 * jnp.mean(dn * n1, axis=-1, keepdims=True))

    def tile():
        return pl.BlockSpec((TM, D), lambda i, j: (i, 0))

    def chunk():
        return pl.BlockSpec((TM, FFC), lambda i, j: (i, j))

    anys = pl.BlockSpec(memory_space=pl.ANY)
    hid16 = jax.ShapeDtypeStruct((tp, DFF), BF16)
    return pl.pallas_call(
        body, name="ffn_bwd", grid=(nt, 2),
        in_specs=[tile(), chunk(), chunk(), tile(), pl.BlockSpec((1, D), lambda i, j: (0, 0)), anys, anys],
        out_specs=[chunk(), chunk(), tile(), pl.BlockSpec((8, D), lambda i, j: (0, 0))],
        out_shape=[hid16, hid16, jax.ShapeDtypeStruct((tp, D), F32), jax.ShapeDtypeStruct((8, D), F32)],
        scratch_shapes=[pltpu.VMEM((2, 2, FFC, D), BF16), pltpu.VMEM((2, FFC, D), BF16),
                        pltpu.VMEM((TM, D), BF16), pltpu.VMEM((TM, D), F32), pltpu.SemaphoreType.DMA((2 * NDEV + 2,))],
        compiler_params=_params(("arbitrary", "arbitrary"), 56),
    )(dh2, fg, fu, h1, g_ffn, w_gu, w_dn)


def _mix_bwd(dh1, z, s, q, ac, m, b_gate, ln_g, ln_b, pool_scale, g_mixw, g_pool, qs):
    tp = dh1.shape[0]
    nt = tp // TMS
    ex = _ChipExchange(qs)
    nq = ex.n

    def body(*refs):
        dh1_ref, zga, zgb, s_ref, q_ref, ac_ref, m_ref, bg_ref, lg_ref, lb_ref, ps_ref, wm_hbm, wp_hbm = refs[:13]
        dac_ref, dm_ref, dzg_ref, dyc_ref, dyp_ref, dm2_ref, acc_ref = refs[13 + nq:20 + nq]
        wm, wp, sems = refs[20 + 2 * nq:23 + 2 * nq]
        ex.bind(refs[13:13 + nq], refs[20 + nq:20 + 2 * nq], refs[23 + 2 * nq:])
        first = pl.program_id(0) == 0

        @pl.when(first)
        def _():
            ex.issue()
            acc_ref[...] = jnp.zeros_like(acc_ref)

        _load_once(first, [(wm_hbm, wm), (wp_hbm, wp)], sems)

        dmerged = _dot_nt(dh1_ref[...].astype(BF16), wm[:, 2].reshape(D, D))
        ga = _sig(zga[...] + bg_ref[:, :D])
        gb = _sig(zgb[...] + bg_ref[:, D:])
        dyc = dmerged * ga
        dyp = dmerged * gb
        dza = (dmerged * _dot(s_ref[...], wm[:, 0].reshape(D, D))) * (ga * (1.0 - ga))
        dzb = (dmerged * _dot(q_ref[...], wm[:, 1].reshape(D, D))) * (gb * (1.0 - gb))
        dzg_ref[:, :D] = dza.astype(BF16)
        dzg_ref[:, D:] = dzb.astype(BF16)
        acc_ref[0:1, :D] += jnp.sum(dza, axis=0, keepdims=True)
        acc_ref[0:1, D:] += jnp.sum(dzb, axis=0, keepdims=True)
        dyc_b = dyc.astype(BF16)
        dyp_b = dyp.astype(BF16)
        dyc_ref[...] = dyc_b
        dyp_ref[...] = dyp_b
        ds = _dot_nt(dyc_b, wm[:, 0].reshape(D, D))
        n, rl = _ln_stats(ac_ref[...])
        l = n * lg_ref[...] + lb_ref[...]
        sg = _sig(l)
        dl = ds * (sg * (1.0 + l * (1.0 - sg)))
        acc_ref[1:2, :D] += jnp.sum(dl * n, axis=0, keepdims=True)
        acc_ref[1:2, D:] += jnp.sum(dl, axis=0, keepdims=True)
        dn = dl * lg_ref[...]
        dac_ref[...] = rl * (dn - jnp.mean(dn, axis=-1, keepdims=True) - n * jnp.mean(dn * n, axis=-1, keepdims=True))
        dq = _dot_nt(dyp_b, wm[:, 1].reshape(D, D))
        mv = m_ref[...]
        acc_ref[2:3, :D] += jnp.sum(dq * _pool_mix(mv, wp), axis=0, keepdims=True)
        dm2 = (dq * ps_ref[...]).astype(BF16)
        dm2_ref[...] = dm2
        dm_ref[...] = jnp.concatenate(
            [_dot_nt(dm2[:, g * PG:(g + 1) * PG], wp[:, g].reshape(PG, PG)) for g in range(4)], axis=1)

        @pl.when(pl.program_id(0) == nt - 1)
        def _():
            ex.finish()

    def tile(col=0):
        return pl.BlockSpec((TMS, D), lambda i: (i, col))

    def vec(w):
        return pl.BlockSpec((1, w), lambda i: (0, 0))

    anys = pl.BlockSpec(memory_space=pl.ANY)
    f32o, b16o = jax.ShapeDtypeStruct((tp, D), F32), jax.ShapeDtypeStruct((tp, D), BF16)
    res = pl.pallas_call(
        body, name="mix_bwd", grid=(nt,),
        in_specs=[tile(), tile(3), tile(4), tile(), tile(), tile(), tile(), vec(2 * D), vec(D), vec(D), vec(D), anys, anys]
        + [anys] * nq,
        out_specs=[tile(), tile(), pl.BlockSpec((TMS, 2 * D), lambda i: (i, 0)), tile(), tile(), tile(),
                   pl.BlockSpec((8, 2 * D), lambda i: (0, 0))] + [anys] * nq,
        out_shape=[f32o, f32o, jax.ShapeDtypeStruct((tp, 2 * D), BF16), b16o, b16o, b16o,
                   jax.ShapeDtypeStruct((8, 2 * D), F32)] + ex.out_shape,
        scratch_shapes=[pltpu.VMEM((NDEV, 3, D // NDEV, D), BF16), pltpu.VMEM((NDEV, 4, PG // NDEV, PG), BF16),
                        pltpu.SemaphoreType.DMA((2,))] + ex.scratch,
        compiler_params=_params(("arbitrary",), 48),
    )(dh1, z, z, s, q, ac, m, b_gate, ln_g, ln_b, pool_scale, g_mixw, g_pool, *qs)
    return res[:7], res[7:]


def _seq_bwd(dac, dm, dzg, z, w_dw, seq, qs):
    tp = z.shape[0]
    nt = tp // TM
    ex = _ChipExchange(qs)
    nq = ex.n

    def body(*refs):
        dac_l, dac_c, dac_r, dm_l, dm_c, dm_r, av_l, av, av_r, ag_l, ag, ag_r, dzg_ref, w_ref = refs[:14]
        dz_ref, acc_ref = refs[14 + nq:16 + nq]
        a3, d3, m3, da3, dp3, w3, dw3, da_sc, dp_sc = refs[16 + 2 * nq:25 + 2 * nq]
        ex.bind(refs[14:14 + nq], refs[16 + nq:16 + 2 * nq], refs[25 + 2 * nq:])
        i = pl.program_id(0)
        sub = lax.broadcasted_iota(jnp.int32, (NCB, 128), 0)

        @pl.when(i == 0)
        def _():
            ex.issue()
            dw3[...] = jnp.zeros_like(dw3)
            _tm_fill(w3, 0, 4, lambda r, l: w_ref[pl.ds(r, 8), l])

        _tm_fill_ext(a3, (av_l, ag_l), (av, ag), (av_r, ag_r), lambda vg, r, l: vg[0][r, l] * _sig(vg[1][r, l]))
        _tm_fill_ext(d3, dac_l, dac_c, dac_r, lambda ref, r, l: ref[r, l])
        _tm_fill_ext(m3, dm_l, dm_c, dm_r, lambda ref, r, l: ref[r, l])

        def conv(g, c):
            dcur = [_tm_at(d3, 8 * g + t + HALO) for t in range(8)]
            accs = [None] * 8
            for k in range(CONV_K):
                wk = _tm_at(w3, k)
                s = None
                for t in range(8):
                    term = wk * _tm_at(d3, 8 * g + t + CONV_K - k)
                    accs[t] = term if accs[t] is None else accs[t] + term
                    pr = dcur[t] * _tm_at(a3, 8 * g + t + k + 1)
                    s = pr if s is None else s + pr
                dw3[_tm_rows(k), :] += s
            s = dcur[0]
            for t in range(1, 8):
                s = s + dcur[t]
            dw3[_tm_rows(CONV_K), :] += s
            for t in range(8):
                da3[_tm_rows(8 * g + t), :] = accs[t]
            return c

        lax.fori_loop(0, TM // 8, conv, 0)

        for b in _edge_rows(seq, tp):
            e = lax.rem(b - i * TM + HALO + tp, tp)

            @pl.when(e < TME)
            def _():
                m3[_tm_rows(e), :] = _tm_at(m3, e) * _edge_gain(b, seq, tp, sub)

        inv = _by_group(sub, [1.0 / w for w in POOL_WINDOWS])

        def pool(g, c):
            for t in range(8):
                e = 8 * g + t + HALO
                sums = _nested_windows(lambda o: _tm_at(m3, e + o), [w // 2 + 1 - w for w in POOL_WINDOWS])
                dp3[_tm_rows(8 * g + t), :] = _by_group(sub, sums) * inv
            return c

        lax.fori_loop(0, TM // 8, pool, 0)

        _tm_read(da3, TM // 8, lambda r, l, tile: da_sc.__setitem__((r, l), tile))
        _tm_read(dp3, TM // 8, lambda r, l, tile: dp_sc.__setitem__((r, l), tile))
        sg = _sig(ag[...])
        da = da_sc[...]
        dz_ref[:, 0:D] = (da * sg).astype(BF16)
        dz_ref[:, D:2 * D] = (da * av[...] * (sg * (1.0 - sg))).astype(BF16)
        dz_ref[:, 2 * D:3 * D] = (dp_sc[...] - dm_c[...]).astype(BF16)
        dz_ref[:, 3 * D:] = dzg_ref[...]

        @pl.when(i == nt - 1)
        def _():
            _tm_read(dw3, 4, lambda r, l, tile: acc_ref.__setitem__((r, l), tile))
            ex.finish()

    tmaj = pltpu.VMEM((TM * NCB, 128), F32)
    text = pltpu.VMEM((TME * NCB, 128), F32)
    taps = pltpu.VMEM((32 * NCB, 128), F32)
    anys = pl.BlockSpec(memory_space=pl.ANY)
    res = pl.pallas_call(
        body, name="seq_bwd", grid=(nt,),
        in_specs=_halo_specs(0, nt) + _halo_specs(0, nt) + _halo_specs(0, nt) + _halo_specs(1, nt)
        + [pl.BlockSpec((TM, 2 * D), lambda i: (i, 0)), pl.BlockSpec((32, D), lambda i: (0, 0))] + [anys] * nq,
        out_specs=[pl.BlockSpec((TM, DIN), lambda i: (i, 0)), pl.BlockSpec((32, D), lambda i: (0, 0))] + [anys] * nq,
        out_shape=[jax.ShapeDtypeStruct((tp, DIN), BF16), jax.ShapeDtypeStruct((32, D), F32)] + ex.out_shape,
        scratch_shapes=[text, text, text, tmaj, tmaj, taps, taps, pltpu.VMEM((TM, D), F32), pltpu.VMEM((TM, D), F32)]
        + ex.scratch,
        compiler_params=_params(("arbitrary",), 48),
    )(dac, dac, dac, dm, dm, dm, z, z, z, z, z, z, dzg, w_dw, *qs)
    return res[:2], res[2:]


def _in_bwd(dz, h0, dh1, g_mix, w_g, seq, qs):
    tp = h0.shape[0]
    tm = _pick(tp, TM_IO)
    nt = tp // tm
    ex = _ChipExchange(qs)
    nq = ex.n

    def body(*refs):
        dz_ref, h_ref, dh1_ref, g_ref, w_hbm = refs[:5]
        gx_ref, gmeta_ref, acc_ref = refs[5 + nq:8 + nq]
        w_vm, sems = refs[8 + 2 * nq:10 + 2 * nq]
        ex.bind(refs[5:5 + nq], refs[8 + nq:8 + 2 * nq], refs[10 + 2 * nq:])
        i = pl.program_id(0)

        @pl.when(i == 0)
        def _():
            ex.issue()
            acc_ref[...] = jnp.zeros_like(acc_ref)

        _load_once(i == 0, _win_pairs(w_hbm, w_vm), sems)

        du = _dot_nt(dz_ref[:, :DIN // 2], w_vm[0]) + _dot_nt(dz_ref[:, DIN // 2:], w_vm[1])
        h = h_ref[...]
        r = lax.rsqrt(jnp.mean(h * h, axis=-1, keepdims=True) + RMS_EPS)
        n0 = h * r
        acc_ref[0:1, :] += jnp.sum(du * n0, axis=0, keepdims=True)
        dn = du * g_ref[...]
        gx_ref[...] = dh1_ref[...] + r * (dn - n0 * jnp.mean(dn * n0, axis=-1, keepdims=True))

        @pl.when(i == nt - 1)
        def _():
            gmeta_ref[...] = gx_ref[pl.ds(tm - N_META, N_META), :]
            ex.finish()

    tile = pl.BlockSpec((tm, D), lambda i: (i, 0))
    anys = pl.BlockSpec(memory_space=pl.ANY)
    res = pl.pallas_call(
        body, name="in_bwd", grid=(nt,),
        in_specs=[pl.BlockSpec((tm, DIN), lambda i: (i, 0)), tile, tile, pl.BlockSpec((1, D), lambda i: (0, 0)), anys]
        + [anys] * nq,
        out_specs=[tile, pl.BlockSpec((N_META, D), lambda i: (0, 0)), pl.BlockSpec((8, D), lambda i: (0, 0))] + [anys] * nq,
        out_shape=[jax.ShapeDtypeStruct((seq, D), F32), jax.ShapeDtypeStruct((N_META, D), F32),
                   jax.ShapeDtypeStruct((8, D), F32)] + ex.out_shape,
        scratch_shapes=[pltpu.VMEM((2, D, DIN // 2), BF16), pltpu.SemaphoreType.DMA((NDEV,))] + ex.scratch,
        compiler_params=_params(("arbitrary",), 58),
    )(dz, h0, dh1, g_mix, w_g, *qs)
    return res[:3], res[3:]


def _wgrad_in(u, dz):
    tp = u.shape[0]
    tm = _pick(tp, TM_WG)
    nt = tp // tm
    half = DIN // 2

    def body(u_ref, dz_ref, o_ref, acc):
        t = pl.program_id(1)

        @pl.when(t == 0)
        def _():
            acc[...] = jnp.zeros_like(acc)

        acc[...] += _dot_tn(u_ref[...], dz_ref[...])

        @pl.when(t == nt - 1)
        def _():
            for d in range(4):
                o_ref[d] = acc[:, INB * d:INB * (d + 1)].astype(BF16)

    return pl.pallas_call(
        body, name="wgrad_in", grid=(2, nt),
        in_specs=[pl.BlockSpec((tm, D), lambda h, t: (t, 0)), pl.BlockSpec((tm, half), lambda h, t: (t, h))],
        out_specs=pl.BlockSpec((4, D, INB), lambda h, t: (h, 0, 0), pipeline_mode=pl.Buffered(1)),
        out_shape=jax.ShapeDtypeStruct((NDEV, D, INB), BF16),
        scratch_shapes=[pltpu.VMEM((D, half), F32)],
        compiler_params=_params(("arbitrary", "arbitrary"), 52),
    )(u, dz)


def _wgrad_mix(s, dyc, q, dyp, merged, dh1, m, dm2, qs):
    tp = s.shape[0]
    tm = _pick(tp, TM_WM)
    nt = tp // tm
    rb = D // NDEV
    ex = _ChipExchange(qs)
    nq = ex.n

    def body(*refs):
        s_ref, dyc_ref, q_ref, dyp_ref, mg_ref, dh1_ref, m_ref, dm2_ref = refs[:8]
        o_ref, op_ref = refs[8 + nq:10 + nq]
        acc, accp = refs[10 + 2 * nq:12 + 2 * nq]
        ex.bind(refs[8:8 + nq], refs[10 + nq:10 + 2 * nq], refs[12 + 2 * nq:])
        t = pl.program_id(0)

        @pl.when(t == 0)
        def _():
            ex.issue()
            acc[...] = jnp.zeros_like(acc)
            accp[...] = jnp.zeros_like(accp)

        acc[0] += _dot_tn(s_ref[...], dyc_ref[...])
        acc[1] += _dot_tn(q_ref[...], dyp_ref[...])
        acc[2] += _dot_tn(mg_ref[...], dh1_ref[...].astype(BF16))
        for g in range(4):
            accp[g] += _dot_tn(m_ref[:, g * PG:(g + 1) * PG], dm2_ref[:, g * PG:(g + 1) * PG])

        @pl.when(t == nt - 1)
        def _():
            for d in range(NDEV):
                for k in range(3):
                    o_ref[d, k] = acc[k, rb * d:rb * (d + 1), :].astype(BF16)
                for g in range(4):
                    op_ref[d, g] = accp[g, 32 * d:32 * (d + 1), :].astype(BF16)
            ex.finish()

    tile = pl.BlockSpec((tm, D), lambda t: (t, 0))
    anys = pl.BlockSpec(memory_space=pl.ANY)
    res = pl.pallas_call(
        body, name="wgrad_mix", grid=(nt,),
        in_specs=[tile] * 8 + [anys] * nq,
        out_specs=[pl.BlockSpec((NDEV, 3, rb, D), lambda t: (0, 0, 0, 0), pipeline_mode=pl.Buffered(1)),
                   pl.BlockSpec((NDEV, 4, 32, PG), lambda t: (0, 0, 0, 0), pipeline_mode=pl.Buffered(1))] + [anys] * nq,
        out_shape=[jax.ShapeDtypeStruct((NDEV, 3, rb, D), BF16), jax.ShapeDtypeStruct((NDEV, 4, 32, PG), BF16)]
        + ex.out_shape,
        scratch_shapes=[pltpu.VMEM((3, D, D), F32), pltpu.VMEM((4, PG, PG), F32)] + ex.scratch,
        compiler_params=_params(("arbitrary",), 56),
    )(s, dyc, q, dyp, merged, dh1, m, dm2, *qs)
    return res[:2], res[2:]


def _wgrad_gu(v, dfg, dfu):
    tp = v.shape[0]
    tm = _pick(tp, TM_WG)
    nt = tp // tm

    def body(v_ref, dg_ref, du_ref, o_ref, acc):
        k, t = pl.program_id(0), pl.program_id(2)

        @pl.when(t == 0)
        def _():
            acc[...] = jnp.zeros_like(acc)

        @pl.when(k == 0)
        def _():
            acc[...] += _dot_tn(dg_ref[...], v_ref[...])

        @pl.when(k == 1)
        def _():
            acc[...] += _dot_tn(du_ref[...], v_ref[...])

        @pl.when(t == nt - 1)
        def _():
            for d in range(4):
                o_ref[d] = acc[FFB * d:FFB * (d + 1), :].astype(BF16)

    return pl.pallas_call(
        body, name="wgrad_gu", grid=(2, 2, nt),
        in_specs=[pl.BlockSpec((tm, D), lambda k, h, t: (t, 0)),
                  pl.BlockSpec((tm, FFC), lambda k, h, t: (t * (1 - k), h * (1 - k))),
                  pl.BlockSpec((tm, FFC), lambda k, h, t: (t * k, h * k))],
        out_specs=pl.BlockSpec((4, None, FFB, D), lambda k, h, t: (h, k, 0, 0), pipeline_mode=pl.Buffered(1)),
        out_shape=jax.ShapeDtypeStruct((NDEV, 2, FFB, D), BF16),
        scratch_shapes=[pltpu.VMEM((FFC, D), F32)],
        compiler_params=_params(("arbitrary",) * 3, 48),
    )(v, dfg, dfu)


def _wgrad_down(f, dh2, parts):
    tp = f.shape[0]
    tm = _pick(tp, TM_WG)
    nt = tp // tm
    ps = _PairSum(parts)
    n = ps.n

    def body(*refs):
        f_ref, d_ref = refs[:2]
        o_ref = refs[2 + n]
        acc = refs[3 + 4 * n]
        ps.bind(refs[2:2 + n], refs[3 + n:3 + 4 * n], refs[4 + 4 * n:])
        h, t = pl.program_id(0), pl.program_id(1)

        @pl.when((h == 0) & (t == 0))
        def _():
            ps.issue()

        @pl.when(t == 0)
        def _():
            acc[...] = jnp.zeros_like(acc)

        acc[...] += _dot_tn(f_ref[...], d_ref[...].astype(BF16))

        @pl.when(t == nt - 1)
        def _():
            for d in range(4):
                o_ref[d] = acc[FFB * d:FFB * (d + 1), :].astype(BF16)

        @pl.when((h == 1) & (t == nt - 1))
        def _():
            ps.finish()

    anys = pl.BlockSpec(memory_space=pl.ANY)
    res = pl.pallas_call(
        body, name="wgrad_down", grid=(2, nt),
        in_specs=[pl.BlockSpec((tm, FFC), lambda h, t: (t, h)), pl.BlockSpec((tm, D), lambda h, t: (t, 0))] + [anys] * n,
        out_specs=[pl.BlockSpec((4, FFB, D), lambda h, t: (h, 0, 0), pipeline_mode=pl.Buffered(1))] + [anys] * (3 * n),
        out_shape=[jax.ShapeDtypeStruct((NDEV, FFB, D), BF16)] + ps.out_shape,
        scratch_shapes=[pltpu.VMEM((FFC, D), F32)] + ps.scratch,
        compiler_params=_params(("arbitrary", "arbitrary"), 56),
    )(f, dh2, *parts)
    return res[0], ps.results(res[1:])


def kernel(x, meta_tokens, g_mix, w_in, b_gate, w_dw, b_dw, ln_g, ln_b, w_conv_out, w_pool, pool_scale, w_pool_out, w_o, g_ffn, w_ffn_gate, w_ffn_up, w_ffn_down, g_final, loss_target, m_meta_tokens, m_g_mix, m_w_in, m_b_gate, m_w_dw, m_b_dw, m_ln_g, m_ln_b, m_w_conv_out, m_w_pool, m_pool_scale, m_w_pool_out, m_w_o, m_g_ffn, m_w_ffn_gate, m_w_ffn_up, m_w_ffn_down, m_g_final, v_meta_tokens, v_g_mix, v_w_in, v_b_gate, v_w_dw, v_b_dw, v_ln_g, v_ln_b, v_w_conv_out, v_w_pool, v_pool_scale, v_w_pool_out, v_w_o, v_g_ffn, v_w_ffn_gate, v_w_ffn_up, v_w_ffn_down, v_g_final):
    seq = x.shape[1]
    tp = -(-(seq + 2 * HALO) // TM) * TM
    tm_in = _pick(tp, TM_IO)
    nx_last = seq - (tp // tm_in - 1) * tm_in
    assert 0 < nx_last <= tm_in - 2 * HALO and nx_last % 8 == 0 and 0 < seq - (tp // TM - 1) * TM

    whole = (Ellipsis,)
    ag_small = _Gather(
        [((48, D // NDEV), [(meta_tokens, pl.ds(0, N_META), whole), (w_dw, pl.ds(N_META, CONV_K), 0)])], [F32])
    ag_mix = _Gather([((3, D // NDEV, D), [(w_conv_out, 0, 0), (w_pool_out, 1, 0), (w_o, 2, 0)]),
                      ((4, PG // NDEV, PG), [(w_pool, whole, 0)])], [BF16, BF16])
    def tr(a):
        return jnp.swapaxes(a, 1, 2)

    ag_gu = _Gather([((2, FFB, D), [(tr(w_ffn_gate), 0, 0), (tr(w_ffn_up), 1, 0)])], [BF16])
    ag_dn = _Gather([((FFB, D), [(w_ffn_down, whole, 0)])], [BF16])

    mx, my = lax.axis_index("x"), lax.axis_index("y")
    order = jnp.stack([2 * mx + my, 2 * mx + 1 - my, 2 * (1 - mx) + my, 2 * (1 - mx) + 1 - my]).astype(jnp.int32)
    (h0, z, u, g_in), (g_mixw, g_pool), g_small = _fwd_in(x[0], g_mix, w_in, order, tp, ag_mix, ag_small)
    wdw_full = g_small.transpose(1, 0, 2).reshape(48, D)[N_META:]
    (ac, m), (w_gu,) = _seq_fwd(z, wdw_full, b_dw, seq, ag_gu)
    (h1, s, merged, q), (g_down,) = _mix_fwd(ac, m, z, h0, b_gate, ln_g, ln_b, pool_scale, g_mixw, g_pool, ag_dn)
    w_dn = g_down.reshape(2, FFC, D)
    fg, fu, v, f, dh2, head_acc = _ffn_fwd(h1, loss_target[0], g_ffn, g_final.reshape(1, D), w_gu, w_dn)

    dfg, dfu, dh1, ffn_acc = _ffn_bwd(dh2, fg, fu, h1, g_ffn, w_gu, w_dn)
    p_down, (own_g, sib_g, q_g) = _wgrad_down(f, dh2, [_wgrad_gu(v, dfg, dfu)])
    own_d, sib_d, q_d = _rs_pair("rs_pair_down", [p_down])
    own_f, sib_f, q_f = own_g + own_d, sib_g + sib_d, q_g + q_d
    (dac, dm, dzg, dyc, dyp, dm2, mix_acc), rel_gu = _mix_bwd(
        dh1, z, s, q, ac, m, b_gate, ln_g, ln_b, pool_scale, g_mixw, g_pool, q_f[:1])
    p_mix, rel_dn = _wgrad_mix(s, dyc, q, dyp, merged, dh1, m, dm2, q_f[1:])
    rel_f = [rel_gu[0], rel_dn[0]]
    own_m, sib_m, q_m = _rs_pair("rs_pair_mix", list(p_mix))
    (dz, seq_acc), rel_m = _seq_bwd(dac, dm, dzg, z, wdw_full, seq, q_m)
    own_i, sib_i, q_i = _rs_pair("rs_pair_in", [_wgrad_in(u, dz)])
    (grad_x, g_meta, in_acc), rel_i = _in_bwd(dz, h0, dh1, g_mix, g_in, seq, q_i)
    small_g = jnp.concatenate([g_meta, seq_acc[:CONV_K], jnp.zeros((1, D), F32)], axis=0)
    p_small = small_g.reshape(48, NDEV, D // NDEV).transpose(1, 0, 2).astype(BF16)
    rep_g = jnp.concatenate([
        in_acc[0:1], mix_acc[0:1, :D], mix_acc[0:1, D:], seq_acc[CONV_K:CONV_K + 1], mix_acc[1:2, :D], mix_acc[1:2, D:],
        mix_acc[2:3, :D], ffn_acc[0:1], head_acc[1:2], head_acc[0:1], jnp.zeros((REP_ROWS - 10, D), F32)], axis=0)
    own_s, sib_s, rel_s, rep_all = _reduce_scatter([p_small], rep_g)
    owns = [own_i[0], own_s[0], own_m[0], own_m[1], own_f[0], own_f[1]]
    sibs = [sib_i[0], sib_s[0], sib_m[0], sib_m[1], sib_f[0], sib_f[1]]
    rels = [rel_i[0], rel_s[0], rel_m[0], rel_m[1], rel_f[0], rel_f[1]]

    def lead(a):
        return a.reshape(1, *a.shape)

    def stack4(a, lead_dims):
        return a.reshape(*lead_dims, 1, 4 * 32, PG)

    (r_in,) = _adamw_multi("adamw_in", lead(owns[0]), sibs[0][:, None], rels[0][:, None], [w_in], [m_w_in], [v_w_in], 4)
    r_meta, r_dw = _adamw_meta_dw(owns[1], sibs[1], rels[1], (meta_tokens, m_meta_tokens, v_meta_tokens),
                                  (w_dw, m_w_dw, v_w_dw))
    r_conv, r_pout, r_o = _adamw_multi("adamw_mix", owns[2], sibs[2], rels[2], [w_conv_out, w_pool_out, w_o],
                                       [m_w_conv_out, m_w_pool_out, m_w_o], [v_w_conv_out, v_w_pool_out, v_w_o], 1)
    (r_pool,) = _adamw_multi("adamw_pool", stack4(owns[3], ()), stack4(sibs[3], (1,)), stack4(rels[3], (3,)),
                             [w_pool.reshape(1, 128, PG)], [m_w_pool.reshape(1, 128, PG)], [v_w_pool.reshape(1, 128, PG)], 1)
    r_pool = tuple(a.reshape(w_pool.shape) for a in r_pool)
    r_gate, r_up = _adamw_multi("adamw_gu", owns[4], sibs[4], rels[4], [tr(w_ffn_gate), tr(w_ffn_up)],
                                [tr(m_w_ffn_gate), tr(m_w_ffn_up)], [tr(v_w_ffn_gate), tr(v_w_ffn_up)], 2)
    r_gate, r_up = tuple(tr(a) for a in r_gate), tuple(tr(a) for a in r_up)
    (r_down,) = _adamw_multi("adamw_down", lead(owns[5]), sibs[5][:, None], rels[5][:, None],
                             [w_ffn_down], [m_w_ffn_down], [v_w_ffn_down], 2)
    row = (1, D)
    loss, reps = _adamw_rep(
        rep_all,
        [g_mix, b_gate, b_dw, ln_g, ln_b, pool_scale, g_ffn, g_final.reshape(row)],
        [m_g_mix, m_b_gate, m_b_dw, m_ln_g, m_ln_b, m_pool_scale, m_g_ffn, m_g_final.reshape(row)],
        [v_g_mix, v_b_gate, v_b_dw, v_ln_g, v_ln_b, v_pool_scale, v_g_ffn, v_g_final.reshape(row)])
    r_gmix, r_bg, r_bdw, r_lg, r_lb, r_ps, r_gffn, r_gfin = reps
    r_gfin = tuple(a.reshape(D) for a in r_gfin)

    in_order = [r_meta, r_gmix, r_in, r_bg, r_dw, r_bdw, r_lg, r_lb, r_conv, r_pool, r_ps, r_pout, r_o, r_gffn,
                r_gate, r_up, r_down, r_gfin]
    return (loss.reshape(()), grad_x[None], *[r[0] for r in in_order], *[r[1] for r in in_order],
            *[r[2] for r in in_order], *[r[3] for r in in_order])
```

```python
import math

import jax
import jax.numpy as jnp
from jax import lax
from jax.experimental import pallas as pl
from jax.experimental.pallas import tpu as pltpu

F32, BF16 = jnp.float32, jnp.bfloat16
MESH_ID = pl.DeviceIdType.MESH
NDEV = 8

D = 1024
N_META = 16
CONV_K = 31
HALO = 16
POOL_WINDOWS = (2, 4, 8, 16)
PG = 256
DIN = 5 * D
DFF = 2816
FFB = DFF // NDEV
FFC = DFF // 2
INB = DIN // NDEV
RMS_EPS = 1e-6
LN_EPS = 1e-5
ADAM_LR, ADAM_B1, ADAM_B2, ADAM_EPS, ADAM_WD, ADAM_STEP = 0.001, 0.9, 0.999, 1e-08, 0.01, 10

TM = 384
TMS = 384
TM_IO = 704
TM_WG = 1408
TM_WM = 704
MIB = 2 ** 20


def _sig(x):
    return 0.5 * jnp.tanh(0.5 * x) + 0.5


def _dot(a, b):
    return jnp.dot(a, b, preferred_element_type=F32)


def _dot_nt(a, b):
    return lax.dot_general(a, b, (((1,), (1,)), ((), ())), preferred_element_type=F32)


def _dot_tn(a, b):
    return lax.dot_general(a, b, (((0,), (0,)), ((), ())), preferred_element_type=F32)


def _pick(tp, pref):
    return pref if tp % pref == 0 else TM


def _params(sem, vmem_mib):
    return pltpu.CompilerParams(dimension_semantics=sem, vmem_limit_bytes=vmem_mib * MIB)


def _load_once(first, pairs, sems):
    @pl.when(first)
    def _():
        cps = [pltpu.make_async_copy(s, d, sems.at[k]) for k, (s, d) in enumerate(pairs)]
        for cp in cps:
            cp.start()
        for cp in cps:
            cp.wait()


def _place():
    x, y, c = lax.axis_index("x"), lax.axis_index("y"), lax.axis_index("c")
    return x, y, c


class _Gather:
    def __init__(self, groups, dtypes):
        self.groups, self.dtypes, self.n = groups, dtypes, len(groups)
        self.arrays = [a for _, parts in groups for a, _, _ in parts]
        self.out_shape = [jax.ShapeDtypeStruct((NDEV, *s), dt) for (s, _), dt in zip(groups, dtypes)]
        self.scratch = [pltpu.VMEM(s, dt) for (s, _), dt in zip(groups, dtypes)] + [
            pltpu.SemaphoreType.DMA((7 * self.n,)), pltpu.SemaphoreType.DMA((7 * self.n,)),
            pltpu.SemaphoreType.DMA((self.n,))]

    def bind(self, ins, outs, scratch):
        self.ins, self.outs, self.stages = ins, outs, scratch[:self.n]
        self.send_sems, self.recv_sems, self.local_sems = scratch[self.n:]
        return self

    def _copy(self, w, k, block, to, src=None):
        dst = self.outs[w].at[4 * block[0] + 2 * block[1] + block[2]]
        return pltpu.make_async_remote_copy(
            src_ref=dst if src is None else src, dst_ref=dst,
            send_sem=self.send_sems.at[7 * w + k], recv_sem=self.recv_sems.at[7 * w + k],
            device_id=to, device_id_type=MESH_ID)

    def _first(self):
        x, y, c = _place()
        me, sibling = (x, y, c), (x, y, 1 - c)
        chips = [(1 - x, y), (x, 1 - y), (1 - x, 1 - y)]
        mine, first = [], []
        for w in range(self.n):
            mine.append(pltpu.make_async_copy(self.stages[w], self.outs[w].at[4 * x + 2 * y + c], self.local_sems.at[w]))
            first.append(self._copy(w, 0, me, sibling, src=self.stages[w]))
            first += [self._copy(w, 1 + j, me, (*chip, c), src=self.stages[w]) for j, chip in enumerate(chips)]
        return mine, first

    def _passed(self):
        x, y, c = _place()
        chips = [(1 - x, y), (x, 1 - y), (1 - x, 1 - y)]
        return [self._copy(w, 4 + j, (*chip, c), (x, y, 1 - c)) for w in range(self.n) for j, chip in enumerate(chips)]

    def issue(self):
        a = 0
        for w in range(self.n):
            shape, parts = self.groups[w]
            if sum(arr.size for arr, _, _ in parts) < math.prod(shape):
                self.stages[w][...] = jnp.zeros(shape, self.dtypes[w])
            for _, dst, src in parts:
                self.stages[w][dst] = self.ins[a][src].astype(self.dtypes[w])
                a += 1
        mine, first = self._first()
        for cp in mine + first:
            cp.start()

    def forward(self):
        x, y, c = _place()
        chips = [(1 - x, y), (x, 1 - y), (1 - x, 1 - y)]
        passed = self._passed()
        for w in range(self.n):
            for j, chip in enumerate(chips):
                self._copy(w, 1 + j, (*chip, c), (x, y, c)).wait_recv()
                passed[3 * w + j].start()

    def finish(self):
        x, y, c = _place()
        chips = [(1 - x, y), (x, 1 - y), (1 - x, 1 - y)]
        for w in range(self.n):
            self._copy(w, 0, (x, y, 1 - c), (x, y, c)).wait_recv()
            for j, chip in enumerate(chips):
                self._copy(w, 4 + j, (*chip, 1 - c), (x, y, c)).wait_recv()
        mine, first = self._first()
        for cp in first + self._passed():
            cp.wait_send()
        for cp in mine:
            cp.wait()


class _ChipExchange:
    def __init__(self, qs):
        self.n = len(qs)
        self.out_shape = [jax.ShapeDtypeStruct(q.shape, q.dtype) for q in qs]
        self.scratch = [pltpu.SemaphoreType.DMA((3 * self.n,)), pltpu.SemaphoreType.DMA((3 * self.n,))]

    def bind(self, qs, rels, scratch):
        self.qs, self.rels = qs, rels
        self.send_sems, self.recv_sems = scratch
        return self

    def _copies(self):
        x, y, c = _place()
        chips = [(1 - x, y), (x, 1 - y), (1 - x, 1 - y)]
        return [pltpu.make_async_remote_copy(
            src_ref=self.qs[w].at[j], dst_ref=self.rels[w].at[j],
            send_sem=self.send_sems.at[3 * w + j], recv_sem=self.recv_sems.at[3 * w + j],
            device_id=(*chips[j], c), device_id_type=MESH_ID) for w in range(self.n) for j in range(3)]

    def issue(self):
        for cp in self._copies():
            cp.start()

    def finish(self):
        cps = self._copies()
        for cp in cps:
            cp.wait_recv()
        for cp in cps:
            cp.wait_send()


class _GradReduce:
    def __init__(self, parts):
        self.n = len(parts)
        self.ps = _PairSum(parts, keep_q=False)
        self.ex = _ChipExchange([jax.ShapeDtypeStruct((3, *p.shape[1:]), BF16) for p in parts])
        self.out_shape = self.ps.out_shape + self.ex.out_shape
        self.scratch = self.ps.scratch + self.ex.scratch

    def bind(self, parts, outs, scratch):
        k = len(self.ps.scratch)
        self.ps.bind(parts, outs[:2 * self.n], scratch[:k])
        self.ex.bind(self.ps.qst, outs[2 * self.n:], scratch[k:])
        return self

    def start(self):
        self.ps.issue()

    def middle(self):
        self.ps.finish()
        self.ex.issue()

    def finish(self):
        self.ex.finish()

    def results(self, outs):
        n = self.n
        return outs[:n], outs[n:2 * n], outs[2 * n:3 * n]


def _reduce_scatter(parts, small):
    n = len(parts)
    blks = [p.shape[1:] for p in parts]

    def body(*refs):
        ps, small_ref = refs[:n], refs[n]
        o = n + 1
        owns, sibs, rels, small_out = refs[o:o + n], refs[o + n:o + 2 * n], refs[o + 2 * n:o + 3 * n], refs[o + 3 * n]
        o += 3 * n + 1
        pa, pb, qst = refs[o:o + n], refs[o + n:o + 2 * n], refs[o + 2 * n:o + 3 * n]
        s1_send, s1_recv, s2_send, s2_recv, sm_send, sm_recv, lsem = refs[o + 3 * n:]
        x, y, c = _place()
        me = 4 * x + 2 * y + c
        sibling = (x, y, 1 - c)
        chips = [(1 - x, y), (x, 1 - y), (1 - x, 1 - y)]
        all_chips = [(x, y)] + chips

        own_cps = []
        for w in range(n):
            cp = pltpu.make_async_copy(ps[w].at[me], owns[w], lsem.at[w])
            cp.start()
            own_cps.append(cp)
        sm_own = pltpu.make_async_copy(small_ref, small_out.at[me], lsem.at[n])
        sm_own.start()

        def small_copy(r):
            peer = ((x + (r >> 2)) % 2, (y + ((r >> 1) & 1)) % 2, (c + (r & 1)) % 2)
            return pltpu.make_async_remote_copy(
                src_ref=small_ref, dst_ref=small_out.at[me], send_sem=sm_send.at[r - 1], recv_sem=sm_recv.at[r - 1],
                device_id=peer, device_id_type=MESH_ID)

        sm_cps = [small_copy(r) for r in range(1, NDEV)]
        for cp in sm_cps:
            cp.start()

        def pair_copy(w, rel):
            cx, cy = all_chips[rel]
            return pltpu.make_async_remote_copy(
                src_ref=ps[w].at[4 * cx + 2 * cy + (1 - c)], dst_ref=sibs[w].at[rel],
                send_sem=s1_send.at[4 * w + rel], recv_sem=s1_recv.at[4 * w + rel],
                device_id=sibling, device_id_type=MESH_ID)

        def chip_copy(w, j):
            return pltpu.make_async_remote_copy(
                src_ref=qst[w].at[j], dst_ref=rels[w].at[j],
                send_sem=s2_send.at[3 * w + j], recv_sem=s2_recv.at[3 * w + j],
                device_id=(*chips[j], c), device_id_type=MESH_ID)

        pair_cps = [pair_copy(w, rel) for w in range(n) for rel in (1, 2, 3, 0)]
        for cp in pair_cps:
            cp.start()
        chip_cps = []
        for w in range(n):
            for j, (cx, cy) in enumerate(chips):
                pair_copy(w, 1 + j).wait_recv()
                la = pltpu.make_async_copy(ps[w].at[4 * cx + 2 * cy + c], pa[w], lsem.at[n + 1])
                lb = pltpu.make_async_copy(sibs[w].at[1 + j], pb[w], lsem.at[n + 2])
                la.start()
                lb.start()
                la.wait()
                lb.wait()
                qst[w][j] = (pa[w][...].astype(F32) + pb[w][...].astype(F32)).astype(BF16)
                cp = chip_copy(w, j)
                cp.start()
                chip_cps.append(cp)
        for w in range(n):
            pair_copy(w, 0).wait_recv()
            for j in range(3):
                chip_copy(w, j).wait_recv()
        for cp in sm_cps:
            cp.wait_recv()
        for cp in pair_cps + chip_cps + sm_cps:
            cp.wait_send()
        for cp in own_cps:
            cp.wait()
        sm_own.wait()

    any_spec = pl.BlockSpec(memory_space=pl.ANY)
    outs = pl.pallas_call(
        body, name="rs_grads",
        out_shape=[jax.ShapeDtypeStruct(b, BF16) for b in blks]
        + [jax.ShapeDtypeStruct((4, *b), BF16) for b in blks]
        + [jax.ShapeDtypeStruct((3, *b), BF16) for b in blks]
        + [jax.ShapeDtypeStruct((NDEV, *small.shape), F32)],
        in_specs=[any_spec] * (n + 1),
        out_specs=[any_spec] * (3 * n + 1),
        scratch_shapes=[pltpu.VMEM(b, BF16) for b in blks] + [pltpu.VMEM(b, BF16) for b in blks]
        + [pltpu.VMEM((3, *b), BF16) for b in blks]
        + [pltpu.SemaphoreType.DMA((4 * n,)), pltpu.SemaphoreType.DMA((4 * n,)),
           pltpu.SemaphoreType.DMA((3 * n,)), pltpu.SemaphoreType.DMA((3 * n,)),
           pltpu.SemaphoreType.DMA((NDEV - 1,)), pltpu.SemaphoreType.DMA((NDEV - 1,)),
           pltpu.SemaphoreType.DMA((n + 3,))],
        compiler_params=pltpu.CompilerParams(vmem_limit_bytes=40 * MIB),
    )(*parts, small)
    return outs[:n], outs[n:2 * n], outs[2 * n:3 * n], outs[3 * n]


class _PairSum:
    def __init__(self, parts, keep_q=True):
        self.n = n = len(parts)
        self.keep_q = keep_q
        blks = [p.shape[1:] for p in parts]
        self.out_shape = [jax.ShapeDtypeStruct(b, BF16) for b in blks] + [jax.ShapeDtypeStruct((1, *b), BF16) for b in blks]
        if keep_q:
            self.out_shape += [jax.ShapeDtypeStruct((3, *b), BF16) for b in blks]
        self.scratch = [pltpu.VMEM((3, *b), BF16) for b in blks] * 3 + [
            pltpu.SemaphoreType.DMA((4 * n,)), pltpu.SemaphoreType.DMA((4 * n,)), pltpu.SemaphoreType.DMA((5 * n,))]

    def bind(self, ps, outs, scratch):
        n = self.n
        self.ps, self.owns, self.sibs, self.qs = ps, outs[:n], outs[n:2 * n], outs[2 * n:]
        self.pa, self.pb, self.qst = scratch[:n], scratch[n:2 * n], scratch[2 * n:3 * n]
        self.s_send, self.s_recv, self.lsem = scratch[3 * n:]
        return self

    def _local(self, with_q):
        n = self.n
        x, y, c = _place()
        chips = [(1 - x, y), (x, 1 - y), (1 - x, 1 - y)]
        own = [pltpu.make_async_copy(self.ps[w].at[4 * x + 2 * y + c], self.owns[w], self.lsem.at[w]) for w in range(n)]
        mine = [[pltpu.make_async_copy(self.ps[w].at[4 * cx + 2 * cy + c], self.pa[w].at[j], self.lsem.at[2 * n + 3 * w + j])
                 for j, (cx, cy) in enumerate(chips)] for w in range(n)]
        outq = [pltpu.make_async_copy(self.qst[w], self.qs[w], self.lsem.at[n + w]) for w in range(n)] if with_q else []
        return own, mine, outq

    def _pair(self, w, rel):
        x, y, c = _place()
        cx, cy = [(x, y), (1 - x, y), (x, 1 - y), (1 - x, 1 - y)][rel]
        return pltpu.make_async_remote_copy(
            src_ref=self.ps[w].at[4 * cx + 2 * cy + (1 - c)],
            dst_ref=self.sibs[w].at[0] if rel == 0 else self.pb[w].at[rel - 1],
            send_sem=self.s_send.at[4 * w + rel], recv_sem=self.s_recv.at[4 * w + rel],
            device_id=(x, y, 1 - c), device_id_type=MESH_ID)

    def issue(self):
        own, mine, _ = self._local(False)
        for cp in own + [cp for row in mine for cp in row]:
            cp.start()
        for w in range(self.n):
            for rel in (1, 2, 3, 0):
                self._pair(w, rel).start()

    def finish(self):
        own, mine, outq = self._local(self.keep_q)
        for w in range(self.n):
            for j in range(3):
                self._pair(w, 1 + j).wait_recv()
                mine[w][j].wait()
                self.qst[w][j] = (self.pa[w][j].astype(F32) + self.pb[w][j].astype(F32)).astype(BF16)
            if self.keep_q:
                outq[w].start()
        for w in range(self.n):
            self._pair(w, 0).wait_recv()
        for w in range(self.n):
            for rel in range(4):
                self._pair(w, rel).wait_send()
        for cp in own + outq:
            cp.wait()

    def results(self, outs):
        n = self.n
        return outs[:n], outs[n:2 * n], outs[2 * n:3 * n]


def _rs_pair(name, parts):
    ps = _PairSum(parts)
    n = ps.n

    def body(*refs):
        ps.bind(refs[:n], refs[n:4 * n], refs[4 * n:])
        ps.issue()
        ps.finish()

    any_spec = pl.BlockSpec(memory_space=pl.ANY)
    outs = pl.pallas_call(
        body, name=name, out_shape=ps.out_shape,
        in_specs=[any_spec] * n, out_specs=[any_spec] * (3 * n), scratch_shapes=ps.scratch,
        compiler_params=pltpu.CompilerParams(vmem_limit_bytes=48 * MIB),
    )(*parts)
    return ps.results(outs)


def _adamw_math(g, w, m, v):
    m = ADAM_B1 * m + (1.0 - ADAM_B1) * g
    v = ADAM_B2 * v + (1.0 - ADAM_B2) * (g * g)
    m_hat = m / (1.0 - ADAM_B1 ** ADAM_STEP)
    v_hat = v / (1.0 - ADAM_B2 ** ADAM_STEP)
    delta = -ADAM_LR * (m_hat / (jnp.sqrt(v_hat) + ADAM_EPS) + ADAM_WD * w)
    return delta, m, v


def _adamw_multi(name, own, sib, rel, ws, ms, vs, row_grid):
    k_n, r_n, c_n = own.shape
    rbk = r_n // row_grid

    def body(*refs):
        own_ref, sib_ref, r0_ref, r1_ref, r2_ref = refs[:5]
        w_refs, m_refs, v_refs = refs[5:5 + k_n], refs[5 + k_n:5 + 2 * k_n], refs[5 + 2 * k_n:5 + 3 * k_n]
        outs = refs[5 + 3 * k_n:]
        for k in range(k_n):
            g = own_ref[k].astype(F32) + sib_ref[k].astype(F32)
            g = g + r0_ref[k].astype(F32)
            g = g + r1_ref[k].astype(F32)
            g = g + r2_ref[k].astype(F32)
            delta, mm, vv = _adamw_math(g, w_refs[k][0], m_refs[k][0], v_refs[k][0])
            outs[4 * k][0] = g
            outs[4 * k + 1][0] = delta
            outs[4 * k + 2][0] = mm
            outs[4 * k + 3][0] = vv

    def lead(j):
        return pl.BlockSpec((None, k_n, rbk, c_n), lambda g: (j, 0, g, 0))

    wspec = pl.BlockSpec((1, rbk, c_n), lambda g: (0, g, 0))
    shp = jax.ShapeDtypeStruct((1, r_n, c_n), F32)
    res = pl.pallas_call(
        body, name=name, grid=(row_grid,),
        in_specs=[pl.BlockSpec((k_n, rbk, c_n), lambda g: (0, g, 0)), lead(0), lead(0), lead(1), lead(2)] + [wspec] * (3 * k_n),
        out_specs=[wspec] * (4 * k_n), out_shape=[shp] * (4 * k_n),
        compiler_params=_params(("arbitrary",), 40),
    )(own, sib, rel, rel, rel, *ws, *ms, *vs)
    return [tuple(res[4 * k:4 * k + 4]) for k in range(k_n)]


def _adamw_meta_dw(own, sib, rel, meta, dw):
    def body(own_ref, sib_ref, rel_ref, wm, mm, vm, wd, md, vd, *outs):
        def gsum(rows):
            g = own_ref[rows, :].astype(F32) + sib_ref[0, rows, :].astype(F32)
            for j in range(3):
                g = g + rel_ref[j, rows, :].astype(F32)
            return g

        g = gsum(pl.ds(0, N_META))
        delta, m2, v2 = _adamw_math(g, wm[...], mm[...], vm[...])
        for o, val in zip(outs[:4], (g, delta, m2, v2)):
            o[...] = val
        g = gsum(pl.ds(N_META, CONV_K))
        delta, m2, v2 = _adamw_math(g, wd[0], md[0], vd[0])
        for o, val in zip(outs[4:], (g, delta, m2, v2)):
            o[0] = val

    s_meta = jax.ShapeDtypeStruct(meta[0].shape, F32)
    s_dw = jax.ShapeDtypeStruct(dw[0].shape, F32)
    res = pl.pallas_call(body, name="adamw_meta_dw", out_shape=[s_meta] * 4 + [s_dw] * 4)(own, sib, rel, *meta, *dw)
    return tuple(res[:4]), tuple(res[4:])


REP_ROWS = 16


def _adamw_rep(gathered, ws, ms, vs):
    rows = [(0, 1), (1, 2), (3, 1), (4, 1), (5, 1), (6, 1), (7, 1), (8, 1)]

    def body(g_ref, *refs):
        w_refs, m_refs, v_refs = refs[:8], refs[8:16], refs[16:24]
        loss_ref, outs, acc = refs[24], refs[25:57], refs[57]
        g = g_ref[0]
        for d in range(1, NDEV):
            g = g + g_ref[d]
        acc[...] = g
        loss_ref[...] = (0.5 / D) * jnp.sum(acc[pl.ds(9, 1), :], axis=1, keepdims=True)
        for p, (r0, nr) in enumerate(rows):
            for h in range(nr):
                cols = pl.ds(h * D, D)
                gp = acc[pl.ds(r0 + h, 1), :]
                delta, mm, vv = _adamw_math(gp, w_refs[p][:, cols], m_refs[p][:, cols], v_refs[p][:, cols])
                for o, val in zip(outs[4 * p:4 * p + 4], (gp, delta, mm, vv)):
                    o[:, cols] = val

    shapes = [jax.ShapeDtypeStruct(w.shape, F32) for w in ws]
    res = pl.pallas_call(
        body, name="adamw_rep",
        out_shape=[jax.ShapeDtypeStruct((1, 1), F32)] + [s for s in shapes for _ in range(4)],
        scratch_shapes=[pltpu.VMEM((REP_ROWS, D), F32)],
    )(gathered, *ws, *ms, *vs)
    return res[0], [tuple(res[1 + 4 * p:5 + 4 * p]) for p in range(8)]


def _load_ffn(i, j, wgu_hbm, wgu, wdn_hbm, wdn, sems):
    half = NDEV // 2

    def copies(ch):
        pairs = [(wgu_hbm.at[half * ch + d, g], wgu.at[g, ch, pl.ds(FFB * d, FFB), :]) for g in range(2) for d in range(half)]
        pairs.append((wdn_hbm.at[ch], wdn.at[ch]))
        return [pltpu.make_async_copy(s, t, sems.at[(2 * half + 1) * ch + k]) for k, (s, t) in enumerate(pairs)]

    @pl.when((i == 0) & (j == 0))
    def _():
        for cp in copies(0) + copies(1):
            cp.start()

    for ch in range(2):
        @pl.when((i == 0) & (j == ch))
        def _():
            for cp in copies(ch):
                cp.wait()


def _win_pairs(w_hbm, w_vm):
    return [(w_hbm.at[q], w_vm.at[q // 2, :, pl.ds(2 * INB * (q % 2), 2 * INB)]) for q in range(4)]


def _whole(a):
    nd = a.ndim
    return pl.BlockSpec(a.shape, lambda *g: (0,) * nd)


CHIPW = 2 * INB
PHASE_CHIP = (1, 0, 2)


class _GatherIn:
    scratch = [pltpu.VMEM((D, INB), BF16), pltpu.SemaphoreType.DMA((7,)), pltpu.SemaphoreType.DMA((7,)),
               pltpu.SemaphoreType.DMA((1,))]

    def bind(self, w_ref, w_vm, scratch):
        self.w_ref, self.w_vm = w_ref, w_vm
        self.stage, self.send_sems, self.recv_sems, self.local_sem = scratch
        return self

    def _win(self, chip, core):
        return self.w_vm.at[2 * chip[0] + chip[1], :, pl.ds(INB * core, INB)]

    def _copy(self, k, chip, core, to, src=None):
        dst = self._win(chip, core)
        return pltpu.make_async_remote_copy(
            src_ref=dst if src is None else src, dst_ref=dst, send_sem=self.send_sems.at[k],
            recv_sem=self.recv_sems.at[k], device_id=to, device_id_type=MESH_ID)

    def _mine(self, cs):
        x, y, _ = _place()
        return pltpu.make_async_copy(self.stage, self._win((x, y), cs), self.local_sem.at[0])

    def issue(self, cs):
        x, y, _ = _place()
        chips = [(1 - x, y), (x, 1 - y), (1 - x, 1 - y)]
        self.stage[...] = self.w_ref[0].astype(BF16)
        self._mine(cs).start()
        self._copy(0, (x, y), cs, (x, y, 1 - cs), src=self.stage).start()
        for j, chip in enumerate(chips):
            self._copy(1 + j, (x, y), cs, (*chip, cs), src=self.stage).start()

    def wait_chip(self, phase, cs):
        x, y, _ = _place()
        chips = [(1 - x, y), (x, 1 - y), (1 - x, 1 - y)]
        if phase == 0:
            self._mine(cs).wait()
            self._copy(0, (x, y), 1 - cs, (x, y, cs)).wait_recv()
            return
        if phase == 1:
            for j in PHASE_CHIP:
                self._copy(1 + j, chips[j], cs, (x, y, cs)).wait_recv()
                self._copy(4 + j, chips[j], cs, (x, y, 1 - cs)).start()
        j = PHASE_CHIP[phase - 1]
        self._copy(4 + j, chips[j], 1 - cs, (x, y, cs)).wait_recv()

    def finish(self, cs):
        x, y, _ = _place()
        for k in range(7):
            self._copy(k, (x, y), cs, (x, y, cs), src=self.stage).wait_send()


def _fwd_in(x2, g_mix, w_in, order, tp, ag, ags):
    tm = _pick(tp, TM_IO)
    nt = tp // tm
    nx_last = x2.shape[0] - (nt - 1) * tm
    na, ng, ns = len(ag.arrays), ag.n, len(ags.arrays)
    gin = _GatherIn()

    def body(order_ref, *refs):
        x_ref, g_ref, w_ref = refs[:3]
        o = 3 + na + ns
        h_ref, z_ref, u_ref, wout_ref = refs[o:o + 4]
        s = o + 4 + ng + 1
        w_vm, u_all, osem, sm_vm = refs[s:s + 4]
        gin.bind(w_ref, w_vm, refs[s + 4:s + 8])
        ag.bind(refs[3:3 + na], refs[o + 4:o + 4 + ng], refs[s + 8:s + 8 + len(ag.scratch)])
        ags.bind(refs[3 + na:3 + na + ns], refs[o + 4 + ng:o + 5 + ng], refs[s + 8 + len(ag.scratch):])
        ph, i = pl.program_id(0), pl.program_id(1)
        core = lax.axis_index("c")
        first = (ph == 0) & (i == 0)
        last = (ph == 3) & (i == nt - 1)
        for cs in range(2):
            @pl.when(first & (core == cs))
            def _():
                gin.issue(cs)

        @pl.when(first)
        def _():
            ags.issue()
            ag.issue()

        @pl.when((ph == 0) & (i == max(nt - 2, 0)))
        def _():
            ags.forward()

        for cs in range(2):
            for p in range(4):
                @pl.when((ph == p) & (i == 0) & (core == cs))
                def _():
                    gin.wait_chip(p, cs)

        @pl.when((ph == 3) & (i == max(nt - 2, 0)))
        def _():
            ag.forward()

        out_copy = pltpu.make_async_copy(w_vm, wout_ref, osem.at[0])

        @pl.when((ph == 3) & (i == 0))
        def _():
            out_copy.start()

        @pl.when((ph == 0) & (i < nt - 1))
        def _():
            h_ref[...] = x_ref[...]

        @pl.when((ph == 0) & (i == nt - 1))
        def _():
            ags.finish()
            cp = pltpu.make_async_copy(ags.outs[0], sm_vm, osem.at[1])
            cp.start()
            h_ref[pl.ds(0, nx_last), :] = x_ref[pl.ds(0, nx_last), :]
            h_ref[pl.ds(nx_last, tm - nx_last - N_META), :] = jnp.zeros((tm - nx_last - N_META, D), F32)
            cp.wait()
            for d in range(NDEV):
                h_ref[pl.ds(tm - N_META, N_META), pl.ds(128 * d, 128)] = sm_vm[d, pl.ds(0, N_META), :]

        @pl.when(ph == 0)
        def _():
            xv = h_ref[...]
            r = lax.rsqrt(jnp.mean(xv * xv, axis=-1, keepdims=True) + RMS_EPS)
            u = (xv * r * g_ref[...]).astype(BF16)
            u_ref[...] = u
            u_all[i] = u

        z_ref[...] = _dot(u_all[i], w_vm[order_ref[ph]])

        @pl.when(last)
        def _():
            ag.finish()
            out_copy.wait()

        for cs in range(2):
            @pl.when(last & (core == cs))
            def _():
                gin.finish(cs)

    def rows(ph, i, order):
        return (jnp.where(ph == 0, i, nt - 1), 0)

    tile = pl.BlockSpec((tm, D), rows)
    anys = pl.BlockSpec(memory_space=pl.ANY)
    res = pl.pallas_call(
        body, name="fwd_in",
        grid_spec=pltpu.PrefetchScalarGridSpec(
            num_scalar_prefetch=1, grid=(4, nt),
            in_specs=[tile, pl.BlockSpec((1, D), lambda ph, i, order: (0, 0)), _whole(w_in)]
            + [_whole(a) for a in ag.arrays + ags.arrays],
            out_specs=[tile, pl.BlockSpec((tm, CHIPW), lambda ph, i, order: (i, order[ph])), tile, anys] + [anys] * (ng + 1),
            scratch_shapes=[pltpu.VMEM((4, D, CHIPW), BF16), pltpu.VMEM((nt, tm, D), BF16), pltpu.SemaphoreType.DMA((2,)),
                            pltpu.VMEM(ags.out_shape[0].shape, F32)] + gin.scratch + ag.scratch + ags.scratch),
        out_shape=[jax.ShapeDtypeStruct((tp, D), F32), jax.ShapeDtypeStruct((tp, DIN), F32),
                   jax.ShapeDtypeStruct((tp, D), BF16), jax.ShapeDtypeStruct((4, D, CHIPW), BF16)]
        + ag.out_shape + ags.out_shape,
        compiler_params=_params(("arbitrary", "arbitrary"), 58),
    )(order, x2, g_mix, w_in, *ag.arrays, *ags.arrays)
    return res[:4], res[4:4 + ng], res[4 + ng]


def _halo_specs(col, nt, width=D):
    r = TM // HALO
    nb = nt * r
    return [pl.BlockSpec((HALO, width), lambda i: ((i * r + nb - 1) % nb, col)),
            pl.BlockSpec((TM, width), lambda i: (i, col)),
            pl.BlockSpec((HALO, width), lambda i: (((i + 1) * r) % nb, col))]


NCB = D // 128
TME = TM + 2 * HALO


def _tm_fill(dst, time0, groups, tile_fn):
    def body(g, c):
        for j in range(NCB):
            dst[pl.ds((time0 + 8 * g) * NCB + j, 8, stride=NCB), :] = tile_fn(pl.multiple_of(8 * g, 8), pl.ds(128 * j, 128))
        return c

    lax.fori_loop(0, groups, body, 0)


def _tm_fill_ext(dst, left, cur, right, fn):
    _tm_fill(dst, 0, HALO // 8, lambda r, l: fn(left, pl.ds(r, 8), l))
    _tm_fill(dst, HALO, TM // 8, lambda r, l: fn(cur, pl.ds(r, 8), l))
    _tm_fill(dst, HALO + TM, HALO // 8, lambda r, l: fn(right, pl.ds(r, 8), l))


def _tm_read(src, groups, store_fn):
    def body(g, c):
        for j in range(NCB):
            store_fn(pl.ds(pl.multiple_of(8 * g, 8), 8), pl.ds(128 * j, 128), src[pl.ds(8 * g * NCB + j, 8, stride=NCB), :])
        return c

    lax.fori_loop(0, groups, body, 0)


def _tm_rows(t):
    return pl.ds(t * NCB if isinstance(t, int) else pl.multiple_of(t * NCB, NCB), NCB)


def _tm_at(ref, t):
    return ref[_tm_rows(t), :]


def _by_group(sub, vals):
    return jnp.where(sub < 2, vals[0], jnp.where(sub < 4, vals[1], jnp.where(sub < 6, vals[2], vals[3])))


def _pool_cnt(b, seq, tp, sub):
    b = jnp.where(b < 0, b + tp, b)
    b = jnp.where(b >= tp, b - tp, b)
    t = jnp.where(b < seq, b + N_META, b - (tp - N_META))
    cnts = []
    for win in POOL_WINDOWS:
        left = win // 2
        lo = jnp.maximum(t - left, 0)
        hi = jnp.minimum(t + win - left, seq + N_META)
        cnts.append(jnp.maximum(hi - lo, 1).astype(F32))
    return _by_group(sub, cnts)


def _edge_rows(seq, tp):
    reach = max(POOL_WINDOWS) // 2
    return [tp - N_META + t for t in range(reach)] + [seq - reach + 1 + t for t in range(reach - 1)]


def _edge_gain(b, seq, tp, sub):
    return _by_group(sub, [float(w) for w in POOL_WINDOWS]) / _pool_cnt(b, seq, tp, sub)


def _nested_windows(at, lo_offs):
    sums, s, have = [], None, set()
    for g, win in enumerate(POOL_WINDOWS):
        for o in range(lo_offs[g], lo_offs[g] + win):
            if o not in have:
                have.add(o)
                s = at(o) if s is None else s + at(o)
        sums.append(s)
    return sums


def _seq_fwd(z, w_dw, b_dw, seq, gat):
    tp = z.shape[0]
    nt = tp // TM
    na, ng = len(gat.arrays), gat.n

    def body(*refs):
        av_l, av, av_r, ag_l, ag, ag_r, p_l, p, p_r, w_ref, b_ref = refs[:11]
        ac_ref, m_ref = refs[11 + na:13 + na]
        a3, p3, o3, m3, w3, b3, m2d = refs[13 + na + ng:20 + na + ng]
        gat.bind(refs[11:11 + na], refs[13 + na:13 + na + ng], refs[20 + na + ng:])
        i = pl.program_id(0)
        sub = lax.broadcasted_iota(jnp.int32, (NCB, 128), 0)

        @pl.when(i == 0)
        def _():
            gat.issue()
            _tm_fill(w3, 0, 4, lambda r, l: w_ref[pl.ds(r, 8), l])
            for j in range(NCB):
                b3[pl.ds(j, 1), :] = b_ref[:, pl.ds(128 * j, 128)]

        @pl.when(i == max(nt - 2, 0))
        def _():
            gat.forward()

        _tm_fill_ext(a3, (av_l, ag_l), (av, ag), (av_r, ag_r), lambda vg, r, l: vg[0][r, l] * _sig(vg[1][r, l]))
        _tm_fill_ext(p3, p_l, p, p_r, lambda ref, r, l: ref[r, l])

        def conv(g, c):
            for t in range(8):
                acc = b3[...]
                for k in range(CONV_K):
                    acc = acc + _tm_at(w3, k) * _tm_at(a3, 8 * g + t + k + 1)
                o3[_tm_rows(8 * g + t), :] = acc
            return c

        lax.fori_loop(0, TM // 8, conv, 0)
        _tm_read(o3, TM // 8, lambda r, l, tile: ac_ref.__setitem__((r, l), tile))

        inv = _by_group(sub, [1.0 / w for w in POOL_WINDOWS])

        def pool(g, c):
            for t in range(8):
                e = 8 * g + t + HALO
                sums = _nested_windows(lambda o: _tm_at(p3, e + o), [-(w // 2) for w in POOL_WINDOWS])
                m3[_tm_rows(8 * g + t), :] = _by_group(sub, sums) * inv - _tm_at(p3, e)
            return c

        lax.fori_loop(0, TM // 8, pool, 0)
        for b in _edge_rows(seq, tp):
            r = b - i * TM

            @pl.when((r >= 0) & (r < TM))
            def _():
                pv = _tm_at(p3, r + HALO)
                m3[_tm_rows(r), :] = (_tm_at(m3, r) + pv) * _edge_gain(b, seq, tp, sub) - pv

        _tm_read(m3, TM // 8, lambda r, l, tile: m2d.__setitem__((r, l), tile))
        m_ref[...] = m2d[...].astype(BF16)

        @pl.when(i == nt - 1)
        def _():
            gat.finish()

    tmaj = pltpu.VMEM((TM * NCB, 128), F32)
    text = pltpu.VMEM((TME * NCB, 128), F32)
    res = pl.pallas_call(
        body, name="seq_fwd", grid=(nt,),
        in_specs=_halo_specs(0, nt) + _halo_specs(1, nt) + _halo_specs(2, nt)
        + [pl.BlockSpec((32, D), lambda i: (0, 0)), pl.BlockSpec((1, D), lambda i: (0, 0))] + [_whole(a) for a in gat.arrays],
        out_specs=[pl.BlockSpec((TM, D), lambda i: (i, 0))] * 2 + [pl.BlockSpec(memory_space=pl.ANY)] * ng,
        out_shape=[jax.ShapeDtypeStruct((tp, D), F32), jax.ShapeDtypeStruct((tp, D), BF16)] + gat.out_shape,
        scratch_shapes=[text, text, tmaj, tmaj, pltpu.VMEM((32 * NCB, 128), F32), pltpu.VMEM((NCB, 128), F32),
                        pltpu.VMEM((TM, D), F32)] + gat.scratch,
        compiler_params=_params(("arbitrary",), 52),
    )(z, z, z, z, z, z, z, z, z, w_dw, b_dw, *gat.arrays)
    return res[:2], res[2:]


def _ln_stats(ac):
    mu = jnp.mean(ac, axis=-1, keepdims=True)
    xc = ac - mu
    rl = lax.rsqrt(jnp.mean(xc * xc, axis=-1, keepdims=True) + LN_EPS)
    return xc * rl, rl


def _pool_mix(m, wp_ref):
    return jnp.concatenate(
        [_dot(m[:, g * PG:(g + 1) * PG], wp_ref[:, g].reshape(PG, PG)) for g in range(4)], axis=1)


def _mix_fwd(ac, m, z, h0, b_gate, ln_g, ln_b, pool_scale, g_mixw, g_pool, gat):
    tp = h0.shape[0]
    tms = TM
    nt = tp // tms
    na, ng = len(gat.arrays), gat.n

    def body(*refs):
        ac_ref, m_ref, zga, zgb, h_ref, bg_ref, lg_ref, lb_ref, ps_ref, wm_hbm, wp_hbm = refs[:11]
        h1_ref, s_ref, mg_ref, q_ref = refs[11 + na:15 + na]
        wm, wp, sems = refs[15 + na + ng:18 + na + ng]
        gat.bind(refs[11:11 + na], refs[15 + na:15 + na + ng], refs[18 + na + ng:])
        i = pl.program_id(0)

        @pl.when(i == 0)
        def _():
            gat.issue()

        @pl.when(i == max(nt - 4, 0))
        def _():
            gat.forward()

        @pl.when(i == nt - 1)
        def _():
            gat.finish()

        _load_once(i == 0, [(wm_hbm, wm), (wp_hbm, wp)], sems)
        n, _ = _ln_stats(ac_ref[...])
        l = n * lg_ref[...] + lb_ref[...]
        s = (l * _sig(l)).astype(BF16)
        s_ref[...] = s
        yc = _dot(s, wm[:, 0].reshape(D, D))
        q = (_pool_mix(m_ref[...], wp) * ps_ref[...]).astype(BF16)
        q_ref[...] = q
        yp = _dot(q, wm[:, 1].reshape(D, D))
        ga = _sig(zga[...] + bg_ref[:, :D])
        gb = _sig(zgb[...] + bg_ref[:, D:])
        merged = (ga * yc + gb * yp).astype(BF16)
        mg_ref[...] = merged
        h1_ref[...] = h_ref[...] + _dot(merged, wm[:, 2].reshape(D, D))

    def tile(col=0):
        return pl.BlockSpec((tms, D), lambda i: (i, col))

    def vec(w):
        return pl.BlockSpec((1, w), lambda i: (0, 0))

    anys = pl.BlockSpec(memory_space=pl.ANY)
    f32o, b16o = jax.ShapeDtypeStruct((tp, D), F32), jax.ShapeDtypeStruct((tp, D), BF16)
    res = pl.pallas_call(
        body, name="mix_fwd", grid=(nt,),
        in_specs=[tile(), tile(), tile(3), tile(4), tile(), vec(2 * D), vec(D), vec(D), vec(D), anys, anys]
        + [_whole(a) for a in gat.arrays],
        out_specs=[tile()] * 4 + [anys] * ng,
        out_shape=[f32o, b16o, b16o, b16o] + gat.out_shape,
        scratch_shapes=[pltpu.VMEM((NDEV, 3, D // NDEV, D), BF16), pltpu.VMEM((NDEV, 4, PG // NDEV, PG), BF16),
                        pltpu.SemaphoreType.DMA((2,))] + gat.scratch,
        compiler_params=_params(("arbitrary",), 52),
    )(ac, m, z, z, h0, b_gate, ln_g, ln_b, pool_scale, g_mixw, g_pool, *gat.arrays)
    return res[:4], res[4:]


def _ffn_fwd(h1, tgt, g_ffn, g_final, w_gu, w_dn):
    tp = h1.shape[0]
    nt = tp // TM
    nx_last = tgt.shape[0] - (nt - 1) * TM

    def body(h_ref, t_ref, gf_ref, gl_ref, wgu_hbm, wdn_hbm,
             fg_ref, fu_ref, v_ref, f_ref, dh2_ref, acc_ref, wgu, wdn, v_sc, h2_sc, diff_sc, sems):
        i, j = pl.program_id(0), pl.program_id(1)
        _load_ffn(i, j, wgu_hbm, wgu, wdn_hbm, wdn, sems)

        @pl.when((i == 0) & (j == 0))
        def _():
            acc_ref[...] = jnp.zeros_like(acc_ref)

        @pl.when(j == 0)
        def _():
            h = h_ref[...]
            r = lax.rsqrt(jnp.mean(h * h, axis=-1, keepdims=True) + RMS_EPS)
            v = (h * r * gf_ref[...]).astype(BF16)
            v_sc[...] = v
            v_ref[...] = v
            h2_sc[...] = h

        v = v_sc[...]
        fg = _dot_nt(v, wgu[0, j])
        fu = _dot_nt(v, wgu[1, j])
        fg_ref[...] = fg
        fu_ref[...] = fu
        f = ((fg * _sig(fg)) * fu).astype(BF16)
        f_ref[...] = f
        h2_sc[...] += _dot(f, wdn[j])

        @pl.when(j == 1)
        def _():
            h2 = h2_sc[...]
            r = lax.rsqrt(jnp.mean(h2 * h2, axis=-1, keepdims=True) + RMS_EPS)
            n2 = h2 * r
            y = n2 * gl_ref[...]

            @pl.when(i < nt - 1)
            def _():
                diff_sc[...] = y - t_ref[...]

            @pl.when(i == nt - 1)
            def _():
                diff_sc[pl.ds(0, nx_last), :] = y[:nx_last] - t_ref[pl.ds(0, nx_last), :]
                diff_sc[pl.ds(nx_last, TM - nx_last), :] = jnp.zeros((TM - nx_last, D), F32)

            diff = diff_sc[...]
            dy = diff * (1.0 / D)
            acc_ref[0:1, :] += jnp.sum(diff * diff, axis=0, keepdims=True)
            acc_ref[1:2, :] += jnp.sum(dy * n2, axis=0, keepdims=True)
            dn = dy * gl_ref[...]
            dh2_ref[...] = r * (dn - n2 * jnp.mean(dn * n2, axis=-1, keepdims=True))

    def tile():
        return pl.BlockSpec((TM, D), lambda i, j: (i, 0))

    def chunk():
        return pl.BlockSpec((TM, FFC), lambda i, j: (i, j))

    def vec():
        return pl.BlockSpec((1, D), lambda i, j: (0, 0))

    anys = pl.BlockSpec(memory_space=pl.ANY)
    hid32, hid16 = jax.ShapeDtypeStruct((tp, DFF), F32), jax.ShapeDtypeStruct((tp, DFF), BF16)
    return pl.pallas_call(
        body, name="ffn_fwd", grid=(nt, 2),
        in_specs=[tile(), tile(), vec(), vec(), anys, anys],
        out_specs=[chunk(), chunk(), tile(), chunk(), tile(), pl.BlockSpec((8, D), lambda i, j: (0, 0))],
        out_shape=[hid32, hid32, jax.ShapeDtypeStruct((tp, D), BF16), hid16, jax.ShapeDtypeStruct((tp, D), F32),
                   jax.ShapeDtypeStruct((8, D), F32)],
        scratch_shapes=[pltpu.VMEM((2, 2, FFC, D), BF16), pltpu.VMEM((2, FFC, D), BF16),
                        pltpu.VMEM((TM, D), BF16), pltpu.VMEM((TM, D), F32), pltpu.VMEM((TM, D), F32),
                        pltpu.SemaphoreType.DMA((2 * NDEV + 2,))],
        compiler_params=_params(("arbitrary", "arbitrary"), 56),
    )(h1, tgt, g_ffn, g_final, w_gu, w_dn)


def _ffn_bwd(dh2, fg, fu, h1, g_ffn, w_gu, w_dn):
    tp = h1.shape[0]
    nt = tp // TM

    def body(dh2_ref, fg_ref, fu_ref, h_ref, gf_ref, wgu_hbm, wdn_hbm,
             dfg_ref, dfu_ref, dh1_ref, acc_ref, wgu, wdn, d_sc, dv_sc, sems):
        i, j = pl.program_id(0), pl.program_id(1)
        _load_ffn(i, j, wgu_hbm, wgu, wdn_hbm, wdn, sems)

        @pl.when((i == 0) & (j == 0))
        def _():
            acc_ref[...] = jnp.zeros_like(acc_ref)

        @pl.when(j == 0)
        def _():
            d_sc[...] = dh2_ref[...].astype(BF16)
            dv_sc[...] = jnp.zeros_like(dv_sc)

        df = _dot_nt(d_sc[...], wdn[j])
        fg = fg_ref[...]
        sg = _sig(fg)
        dfu = (df * (fg * sg)).astype(BF16)
        dfg = (df * fu_ref[...] * (sg * (1.0 + fg * (1.0 - sg)))).astype(BF16)
        dfg_ref[...] = dfg
        dfu_ref[...] = dfu
        dv_sc[...] += _dot(dfg, wgu[0, j]) + _dot(dfu, wgu[1, j])

        @pl.when(j == 1)
        def _():
            h = h_ref[...]
            r = lax.rsqrt(jnp.mean(h * h, axis=-1, keepdims=True) + RMS_EPS)
            n1 = h * r
            dv = dv_sc[...]
            acc_ref[0:1, :] += jnp.sum(dv * n1, axis=0, keepdims=True)
            dn = dv * gf_ref[...]
            dh1_ref[...] = dh2_ref[...] + r * (dn - n1 * jnp.mean(dn * n1, axis=-1, keepdims=True))

    def tile():
        return pl.BlockSpec((TM, D), lambda i, j: (i, 0))

    def chunk():
        return pl.BlockSpec((TM, FFC), lambda i, j: (i, j))

    anys = pl.BlockSpec(memory_space=pl.ANY)
    hid16 = jax.ShapeDtypeStruct((tp, DFF), BF16)
    return pl.pallas_call(
        body, name="ffn_bwd", grid=(nt, 2),
        in_specs=[tile(), chunk(), chunk(), tile(), pl.BlockSpec((1, D), lambda i, j: (0, 0)), anys, anys],
        out_specs=[chunk(), chunk(), tile(), pl.BlockSpec((8, D), lambda i, j: (0, 0))],
        out_shape=[hid16, hid16, jax.ShapeDtypeStruct((tp, D), F32), jax.ShapeDtypeStruct((8, D), F32)],
        scratch_shapes=[pltpu.VMEM((2, 2, FFC, D), BF16), pltpu.VMEM((2, FFC, D), BF16),
                        pltpu.VMEM((TM, D), BF16), pltpu.VMEM((TM, D), F32), pltpu.SemaphoreType.DMA((2 * NDEV + 2,))],
        compiler_params=_params(("arbitrary", "arbitrary"), 56),
    )(dh2, fg, fu, h1, g_ffn, w_gu, w_dn)


def _mix_bwd(dh1, z, s, q, ac, m, b_gate, ln_g, ln_b, pool_scale, g_mixw, g_pool, qs):
    tp = dh1.shape[0]
    nt = tp // TMS
    ex = _ChipExchange(qs)
    nq = ex.n

    def body(*refs):
        dh1_ref, zga, zgb, s_ref, q_ref, ac_ref, m_ref, bg_ref, lg_ref, lb_ref, ps_ref, wm_hbm, wp_hbm = refs[:13]
        dac_ref, dm_ref, dzg_ref, dyc_ref, dyp_ref, dm2_ref, acc_ref = refs[13 + nq:20 + nq]
        wm, wp, sems = refs[20 + 2 * nq:23 + 2 * nq]
        ex.bind(refs[13:13 + nq], refs[20 + nq:20 + 2 * nq], refs[23 + 2 * nq:])
        first = pl.program_id(0) == 0

        @pl.when(first)
        def _():
            ex.issue()
            acc_ref[...] = jnp.zeros_like(acc_ref)

        _load_once(first, [(wm_hbm, wm), (wp_hbm, wp)], sems)

        dmerged = _dot_nt(dh1_ref[...].astype(BF16), wm[:, 2].reshape(D, D))
        ga = _sig(zga[...] + bg_ref[:, :D])
        gb = _sig(zgb[...] + bg_ref[:, D:])
        dyc = dmerged * ga
        dyp = dmerged * gb
        dza = (dmerged * _dot(s_ref[...], wm[:, 0].reshape(D, D))) * (ga * (1.0 - ga))
        dzb = (dmerged * _dot(q_ref[...], wm[:, 1].reshape(D, D))) * (gb * (1.0 - gb))
        dzg_ref[:, :D] = dza.astype(BF16)
        dzg_ref[:, D:] = dzb.astype(BF16)
        acc_ref[0:1, :D] += jnp.sum(dza, axis=0, keepdims=True)
        acc_ref[0:1, D:] += jnp.sum(dzb, axis=0, keepdims=True)
        dyc_b = dyc.astype(BF16)
        dyp_b = dyp.astype(BF16)
        dyc_ref[...] = dyc_b
        dyp_ref[...] = dyp_b
        ds = _dot_nt(dyc_b, wm[:, 0].reshape(D, D))
        n, rl = _ln_stats(ac_ref[...])
        l = n * lg_ref[...] + lb_ref[...]
        sg = _sig(l)
        dl = ds * (sg * (1.0 + l * (1.0 - sg)))
        acc_ref[1:2, :D] += jnp.sum(dl * n, axis=0, keepdims=True)
        acc_ref[1:2, D:] += jnp.sum(dl, axis=0, keepdims=True)
        dn = dl * lg_ref[...]
        dac_ref[...] = rl * (dn - jnp.mean(dn, axis=-1, keepdims=True) - n * jnp.mean(dn * n, axis=-1, keepdims=True))
        dq = _dot_nt(dyp_b, wm[:, 1].reshape(D, D))
        mv = m_ref[...]
        acc_ref[2:3, :D] += jnp.sum(dq * _pool_mix(mv, wp), axis=0, keepdims=True)
        dm2 = (dq * ps_ref[...]).astype(BF16)
        dm2_ref[...] = dm2
        dm_ref[...] = jnp.concatenate(
            [_dot_nt(dm2[:, g * PG:(g + 1) * PG], wp[:, g].reshape(PG, PG)) for g in range(4)], axis=1)

        @pl.when(pl.program_id(0) == nt - 1)
        def _():
            ex.finish()

    def tile(col=0):
        return pl.BlockSpec((TMS, D), lambda i: (i, col))

    def vec(w):
        return pl.BlockSpec((1, w), lambda i: (0, 0))

    anys = pl.BlockSpec(memory_space=pl.ANY)
    f32o, b16o = jax.ShapeDtypeStruct((tp, D), F32), jax.ShapeDtypeStruct((tp, D), BF16)
    res = pl.pallas_call(
        body, name="mix_bwd", grid=(nt,),
        in_specs=[tile(), tile(3), tile(4), tile(), tile(), tile(), tile(), vec(2 * D), vec(D), vec(D), vec(D), anys, anys]
        + [anys] * nq,
        out_specs=[tile(), tile(), pl.BlockSpec((TMS, 2 * D), lambda i: (i, 0)), tile(), tile(), tile(),
                   pl.BlockSpec((8, 2 * D), lambda i: (0, 0))] + [anys] * nq,
        out_shape=[f32o, f32o, jax.ShapeDtypeStruct((tp, 2 * D), BF16), b16o, b16o, b16o,
                   jax.ShapeDtypeStruct((8, 2 * D), F32)] + ex.out_shape,
        scratch_shapes=[pltpu.VMEM((NDEV, 3, D // NDEV, D), BF16), pltpu.VMEM((NDEV, 4, PG // NDEV, PG), BF16),
                        pltpu.SemaphoreType.DMA((2,))] + ex.scratch,
        compiler_params=_params(("arbitrary",), 48),
    )(dh1, z, z, s, q, ac, m, b_gate, ln_g, ln_b, pool_scale, g_mixw, g_pool, *qs)
    return res[:7], res[7:]


def _seq_bwd(dac, dm, dzg, z, w_dw, seq, parts):
    tp = z.shape[0]
    nt = tp // TM
    ex = _GradReduce(parts)
    nq, no = ex.n, 3 * ex.n

    def body(*refs):
        dac_l, dac_c, dac_r, dm_l, dm_c, dm_r, av_l, av, av_r, ag_l, ag, ag_r, dzg_ref, w_ref = refs[:14]
        dz_ref, acc_ref = refs[14 + nq:16 + nq]
        a3, d3, m3, da3, dp3, w3, dw3, da_sc, dp_sc = refs[16 + nq + no:25 + nq + no]
        ex.bind(refs[14:14 + nq], refs[16 + nq:16 + nq + no], refs[25 + nq + no:])
        i = pl.program_id(0)
        sub = lax.broadcasted_iota(jnp.int32, (NCB, 128), 0)

        @pl.when(i == min(1, nt - 1))
        def _():
            ex.middle()

        @pl.when(i == 0)
        def _():
            ex.start()
            dw3[...] = jnp.zeros_like(dw3)
            _tm_fill(w3, 0, 4, lambda r, l: w_ref[pl.ds(r, 8), l])

        _tm_fill_ext(a3, (av_l, ag_l), (av, ag), (av_r, ag_r), lambda vg, r, l: vg[0][r, l] * _sig(vg[1][r, l]))
        _tm_fill_ext(d3, dac_l, dac_c, dac_r, lambda ref, r, l: ref[r, l])
        _tm_fill_ext(m3, dm_l, dm_c, dm_r, lambda ref, r, l: ref[r, l])

        def conv(g, c):
            dcur = [_tm_at(d3, 8 * g + t + HALO) for t in range(8)]
            accs = [None] * 8
            for k in range(CONV_K):
                wk = _tm_at(w3, k)
                s = None
                for t in range(8):
                    term = wk * _tm_at(d3, 8 * g + t + CONV_K - k)
                    accs[t] = term if accs[t] is None else accs[t] + term
                    pr = dcur[t] * _tm_at(a3, 8 * g + t + k + 1)
                    s = pr if s is None else s + pr
                dw3[_tm_rows(k), :] += s
            s = dcur[0]
            for t in range(1, 8):
                s = s + dcur[t]
            dw3[_tm_rows(CONV_K), :] += s
            for t in range(8):
                da3[_tm_rows(8 * g + t), :] = accs[t]
            return c

        lax.fori_loop(0, TM // 8, conv, 0)

        for b in _edge_rows(seq, tp):
            e = lax.rem(b - i * TM + HALO + tp, tp)

            @pl.when(e < TME)
            def _():
                m3[_tm_rows(e), :] = _tm_at(m3, e) * _edge_gain(b, seq, tp, sub)

        inv = _by_group(sub, [1.0 / w for w in POOL_WINDOWS])

        def pool(g, c):
            for t in range(8):
                e = 8 * g + t + HALO
                sums = _nested_windows(lambda o: _tm_at(m3, e + o), [w // 2 + 1 - w for w in POOL_WINDOWS])
                dp3[_tm_rows(8 * g + t), :] = _by_group(sub, sums) * inv
            return c

        lax.fori_loop(0, TM // 8, pool, 0)

        _tm_read(da3, TM // 8, lambda r, l, tile: da_sc.__setitem__((r, l), tile))
        _tm_read(dp3, TM // 8, lambda r, l, tile: dp_sc.__setitem__((r, l), tile))
        sg = _sig(ag[...])
        da = da_sc[...]
        dz_ref[:, 0:D] = (da * sg).astype(BF16)
        dz_ref[:, D:2 * D] = (da * av[...] * (sg * (1.0 - sg))).astype(BF16)
        dz_ref[:, 2 * D:3 * D] = (dp_sc[...] - dm_c[...]).astype(BF16)
        dz_ref[:, 3 * D:] = dzg_ref[...]

        @pl.when(i == nt - 1)
        def _():
            _tm_read(dw3, 4, lambda r, l, tile: acc_ref.__setitem__((r, l), tile))
            ex.finish()

    tmaj = pltpu.VMEM((TM * NCB, 128), F32)
    text = pltpu.VMEM((TME * NCB, 128), F32)
    taps = pltpu.VMEM((32 * NCB, 128), F32)
    anys = pl.BlockSpec(memory_space=pl.ANY)
    res = pl.pallas_call(
        body, name="seq_bwd", grid=(nt,),
        in_specs=_halo_specs(0, nt) + _halo_specs(0, nt) + _halo_specs(0, nt) + _halo_specs(1, nt)
        + [pl.BlockSpec((TM, 2 * D), lambda i: (i, 0)), pl.BlockSpec((32, D), lambda i: (0, 0))] + [anys] * nq,
        out_specs=[pl.BlockSpec((TM, DIN), lambda i: (i, 0)), pl.BlockSpec((32, D), lambda i: (0, 0))] + [anys] * no,
        out_shape=[jax.ShapeDtypeStruct((tp, DIN), BF16), jax.ShapeDtypeStruct((32, D), F32)] + ex.out_shape,
        scratch_shapes=[text, text, text, tmaj, tmaj, taps, taps, pltpu.VMEM((TM, D), F32), pltpu.VMEM((TM, D), F32)]
        + ex.scratch,
        compiler_params=_params(("arbitrary",), 56),
    )(dac, dac, dac, dm, dm, dm, z, z, z, z, z, z, dzg, w_dw, *parts)
    return res[:2], ex.results(res[2:])


def _in_bwd(dz, h0, dh1, g_mix, w_g, seq, parts):
    tp = h0.shape[0]
    tm = TM
    nt = tp // tm
    ex = _GradReduce(parts)
    nq, no = ex.n, 3 * ex.n

    def body(*refs):
        dz_ref, h_ref, dh1_ref, g_ref, w_hbm = refs[:5]
        gx_ref, gmeta_ref, acc_ref = refs[5 + nq:8 + nq]
        w_vm, sems = refs[8 + nq + no:10 + nq + no]
        ex.bind(refs[5:5 + nq], refs[8 + nq:8 + nq + no], refs[10 + nq + no:])
        i = pl.program_id(0)

        @pl.when(i == 0)
        def _():
            ex.start()
            acc_ref[...] = jnp.zeros_like(acc_ref)

        @pl.when(i == min(1, nt - 1))
        def _():
            ex.middle()

        _load_once(i == 0, _win_pairs(w_hbm, w_vm), sems)

        du = _dot_nt(dz_ref[:, :DIN // 2], w_vm[0]) + _dot_nt(dz_ref[:, DIN // 2:], w_vm[1])
        h = h_ref[...]
        r = lax.rsqrt(jnp.mean(h * h, axis=-1, keepdims=True) + RMS_EPS)
        n0 = h * r
        acc_ref[0:1, :] += jnp.sum(du * n0, axis=0, keepdims=True)
        dn = du * g_ref[...]
        gx_ref[...] = dh1_ref[...] + r * (dn - n0 * jnp.mean(dn * n0, axis=-1, keepdims=True))

        @pl.when(i == nt - 1)
        def _():
            gmeta_ref[...] = gx_ref[pl.ds(tm - N_META, N_META), :]
            ex.finish()

    tile = pl.BlockSpec((tm, D), lambda i: (i, 0))
    anys = pl.BlockSpec(memory_space=pl.ANY)
    res = pl.pallas_call(
        body, name="in_bwd", grid=(nt,),
        in_specs=[pl.BlockSpec((tm, DIN), lambda i: (i, 0)), tile, tile, pl.BlockSpec((1, D), lambda i: (0, 0)), anys]
        + [anys] * nq,
        out_specs=[tile, pl.BlockSpec((N_META, D), lambda i: (0, 0)), pl.BlockSpec((8, D), lambda i: (0, 0))] + [anys] * no,
        out_shape=[jax.ShapeDtypeStruct((seq, D), F32), jax.ShapeDtypeStruct((N_META, D), F32),
                   jax.ShapeDtypeStruct((8, D), F32)] + ex.out_shape,
        scratch_shapes=[pltpu.VMEM((2, D, DIN // 2), BF16), pltpu.SemaphoreType.DMA((NDEV,))] + ex.scratch,
        compiler_params=_params(("arbitrary",), 58),
    )(dz, h0, dh1, g_mix, w_g, *parts)
    return res[:3], ex.results(res[3:])


def _wgrad_in(u, dz):
    tp = u.shape[0]
    tm = _pick(tp, TM_WG)
    nt = tp // tm
    half = DIN // 2

    def body(u_ref, dz_ref, o_ref, acc):
        t = pl.program_id(1)

        @pl.when(t == 0)
        def _():
            acc[...] = jnp.zeros_like(acc)

        acc[...] += _dot_tn(u_ref[...], dz_ref[...])

        @pl.when(t == nt - 1)
        def _():
            for d in range(4):
                o_ref[d] = acc[:, INB * d:INB * (d + 1)].astype(BF16)

    return pl.pallas_call(
        body, name="wgrad_in", grid=(2, nt),
        in_specs=[pl.BlockSpec((tm, D), lambda h, t: (t, 0)), pl.BlockSpec((tm, half), lambda h, t: (t, h))],
        out_specs=pl.BlockSpec((4, D, INB), lambda h, t: (h, 0, 0), pipeline_mode=pl.Buffered(1)),
        out_shape=jax.ShapeDtypeStruct((NDEV, D, INB), BF16),
        scratch_shapes=[pltpu.VMEM((D, half), F32)],
        compiler_params=_params(("arbitrary", "arbitrary"), 52),
    )(u, dz)


def _wgrad_mix(s, dyc, q, dyp, merged, dh1, m, dm2, qs):
    tp = s.shape[0]
    tm = _pick(tp, TM_WM)
    nt = tp // tm
    rb = D // NDEV
    ex = _ChipExchange(qs)
    nq = ex.n

    def body(*refs):
        s_ref, dyc_ref, q_ref, dyp_ref, mg_ref, dh1_ref, m_ref, dm2_ref = refs[:8]
        o_ref, op_ref = refs[8 + nq:10 + nq]
        acc, accp = refs[10 + 2 * nq:12 + 2 * nq]
        ex.bind(refs[8:8 + nq], refs[10 + nq:10 + 2 * nq], refs[12 + 2 * nq:])
        t = pl.program_id(0)

        @pl.when(t == 0)
        def _():
            ex.issue()
            acc[...] = jnp.zeros_like(acc)
            accp[...] = jnp.zeros_like(accp)

        acc[0] += _dot_tn(s_ref[...], dyc_ref[...])
        acc[1] += _dot_tn(q_ref[...], dyp_ref[...])
        acc[2] += _dot_tn(mg_ref[...], dh1_ref[...].astype(BF16))
        for g in range(4):
            accp[g] += _dot_tn(m_ref[:, g * PG:(g + 1) * PG], dm2_ref[:, g * PG:(g + 1) * PG])

        @pl.when(t == nt - 1)
        def _():
            for d in range(NDEV):
                for k in range(3):
                    o_ref[d, k] = acc[k, rb * d:rb * (d + 1), :].astype(BF16)
                for g in range(4):
                    op_ref[d, g] = accp[g, 32 * d:32 * (d + 1), :].astype(BF16)
            ex.finish()

    tile = pl.BlockSpec((tm, D), lambda t: (t, 0))
    anys = pl.BlockSpec(memory_space=pl.ANY)
    res = pl.pallas_call(
        body, name="wgrad_mix", grid=(nt,),
        in_specs=[tile] * 8 + [anys] * nq,
        out_specs=[pl.BlockSpec((NDEV, 3, rb, D), lambda t: (0, 0, 0, 0), pipeline_mode=pl.Buffered(1)),
                   pl.BlockSpec((NDEV, 4, 32, PG), lambda t: (0, 0, 0, 0), pipeline_mode=pl.Buffered(1))] + [anys] * nq,
        out_shape=[jax.ShapeDtypeStruct((NDEV, 3, rb, D), BF16), jax.ShapeDtypeStruct((NDEV, 4, 32, PG), BF16)]
        + ex.out_shape,
        scratch_shapes=[pltpu.VMEM((3, D, D), F32), pltpu.VMEM((4, PG, PG), F32)] + ex.scratch,
        compiler_params=_params(("arbitrary",), 56),
    )(s, dyc, q, dyp, merged, dh1, m, dm2, *qs)
    return res[:2], res[2:]


def _wgrad_gu(v, dfg, dfu):
    tp = v.shape[0]
    tm = _pick(tp, TM_WG)
    nt = tp // tm

    def body(v_ref, dg_ref, du_ref, o_ref, acc):
        k, t = pl.program_id(0), pl.program_id(2)

        @pl.when(t == 0)
        def _():
            acc[...] = jnp.zeros_like(acc)

        @pl.when(k == 0)
        def _():
            acc[...] += _dot_tn(dg_ref[...], v_ref[...])

        @pl.when(k == 1)
        def _():
            acc[...] += _dot_tn(du_ref[...], v_ref[...])

        @pl.when(t == nt - 1)
        def _():
            for d in range(4):
                o_ref[d] = acc[FFB * d:FFB * (d + 1), :].astype(BF16)

    return pl.pallas_call(
        body, name="wgrad_gu", grid=(2, 2, nt),
        in_specs=[pl.BlockSpec((tm, D), lambda k, h, t: (t, 0)),
                  pl.BlockSpec((tm, FFC), lambda k, h, t: (t * (1 - k), h * (1 - k))),
                  pl.BlockSpec((tm, FFC), lambda k, h, t: (t * k, h * k))],
        out_specs=pl.BlockSpec((4, None, FFB, D), lambda k, h, t: (h, k, 0, 0), pipeline_mode=pl.Buffered(1)),
        out_shape=jax.ShapeDtypeStruct((NDEV, 2, FFB, D), BF16),
        scratch_shapes=[pltpu.VMEM((FFC, D), F32)],
        compiler_params=_params(("arbitrary",) * 3, 48),
    )(v, dfg, dfu)


def _wgrad_down(f, dh2):
    tp = f.shape[0]
    tm = _pick(tp, TM_WG)
    nt = tp // tm

    def body(f_ref, d_ref, o_ref, acc):
        t = pl.program_id(1)

        @pl.when(t == 0)
        def _():
            acc[...] = jnp.zeros_like(acc)

        acc[...] += _dot_tn(f_ref[...], d_ref[...].astype(BF16))

        @pl.when(t == nt - 1)
        def _():
            for d in range(4):
                o_ref[d] = acc[FFB * d:FFB * (d + 1), :].astype(BF16)

    return pl.pallas_call(
        body, name="wgrad_down", grid=(2, nt),
        in_specs=[pl.BlockSpec((tm, FFC), lambda h, t: (t, h)), pl.BlockSpec((tm, D), lambda h, t: (t, 0))],
        out_specs=pl.BlockSpec((4, FFB, D), lambda h, t: (h, 0, 0), pipeline_mode=pl.Buffered(1)),
        out_shape=jax.ShapeDtypeStruct((NDEV, FFB, D), BF16),
        scratch_shapes=[pltpu.VMEM((FFC, D), F32)],
        compiler_params=_params(("arbitrary", "arbitrary"), 48),
    )(f, dh2)


def kernel(x, meta_tokens, g_mix, w_in, b_gate, w_dw, b_dw, ln_g, ln_b, w_conv_out, w_pool, pool_scale, w_pool_out, w_o, g_ffn, w_ffn_gate, w_ffn_up, w_ffn_down, g_final, loss_target, m_meta_tokens, m_g_mix, m_w_in, m_b_gate, m_w_dw, m_b_dw, m_ln_g, m_ln_b, m_w_conv_out, m_w_pool, m_pool_scale, m_w_pool_out, m_w_o, m_g_ffn, m_w_ffn_gate, m_w_ffn_up, m_w_ffn_down, m_g_final, v_meta_tokens, v_g_mix, v_w_in, v_b_gate, v_w_dw, v_b_dw, v_ln_g, v_ln_b, v_w_conv_out, v_w_pool, v_pool_scale, v_w_pool_out, v_w_o, v_g_ffn, v_w_ffn_gate, v_w_ffn_up, v_w_ffn_down, v_g_final):
    seq = x.shape[1]
    tp = -(-(seq + 2 * HALO) // TM) * TM
    tm_in = _pick(tp, TM_IO)
    nx_last = seq - (tp // tm_in - 1) * tm_in
    assert 0 < nx_last <= tm_in - 2 * HALO and nx_last % 8 == 0 and 0 < seq - (tp // TM - 1) * TM

    whole = (Ellipsis,)
    ag_small = _Gather(
        [((48, D // NDEV), [(meta_tokens, pl.ds(0, N_META), whole), (w_dw, pl.ds(N_META, CONV_K), 0)])], [F32])
    ag_mix = _Gather([((3, D // NDEV, D), [(w_conv_out, 0, 0), (w_pool_out, 1, 0), (w_o, 2, 0)]),
                      ((4, PG // NDEV, PG), [(w_pool, whole, 0)])], [BF16, BF16])
    def tr(a):
        return jnp.swapaxes(a, 1, 2)

    ag_gu = _Gather([((2, FFB, D), [(tr(w_ffn_gate), 0, 0), (tr(w_ffn_up), 1, 0)])], [BF16])
    ag_dn = _Gather([((FFB, D), [(w_ffn_down, whole, 0)])], [BF16])

    mx, my = lax.axis_index("x"), lax.axis_index("y")
    order = jnp.stack([2 * mx + my, 2 * mx + 1 - my, 2 * (1 - mx) + my, 2 * (1 - mx) + 1 - my]).astype(jnp.int32)
    (h0, z, u, g_in), (g_mixw, g_pool), g_small = _fwd_in(x[0], g_mix, w_in, order, tp, ag_mix, ag_small)
    wdw_full = g_small.transpose(1, 0, 2).reshape(48, D)[N_META:]
    (ac, m), (w_gu,) = _seq_fwd(z, wdw_full, b_dw, seq, ag_gu)
    (h1, s, merged, q), (g_down,) = _mix_fwd(ac, m, z, h0, b_gate, ln_g, ln_b, pool_scale, g_mixw, g_pool, ag_dn)
    w_dn = g_down.reshape(2, FFC, D)
    fg, fu, v, f, dh2, head_acc = _ffn_fwd(h1, loss_target[0], g_ffn, g_final.reshape(1, D), w_gu, w_dn)

    dfg, dfu, dh1, ffn_acc = _ffn_bwd(dh2, fg, fu, h1, g_ffn, w_gu, w_dn)
    own_f, sib_f, q_f = _rs_pair("rs_pair_ffn", [_wgrad_gu(v, dfg, dfu), _wgrad_down(f, dh2)])
    (dac, dm, dzg, dyc, dyp, dm2, mix_acc), rel_gu = _mix_bwd(
        dh1, z, s, q, ac, m, b_gate, ln_g, ln_b, pool_scale, g_mixw, g_pool, q_f[:1])
    p_mix, rel_dn = _wgrad_mix(s, dyc, q, dyp, merged, dh1, m, dm2, q_f[1:])
    rel_f = [rel_gu[0], rel_dn[0]]
    (dz, seq_acc), (own_m, sib_m, rel_m) = _seq_bwd(dac, dm, dzg, z, wdw_full, seq, list(p_mix))
    (grad_x, g_meta, in_acc), (own_i, sib_i, rel_i) = _in_bwd(dz, h0, dh1, g_mix, g_in, seq, [_wgrad_in(u, dz)])
    small_g = jnp.concatenate([g_meta, seq_acc[:CONV_K], jnp.zeros((1, D), F32)], axis=0)
    p_small = small_g.reshape(48, NDEV, D // NDEV).transpose(1, 0, 2).astype(BF16)
    rep_g = jnp.concatenate([
        in_acc[0:1], mix_acc[0:1, :D], mix_acc[0:1, D:], seq_acc[CONV_K:CONV_K + 1], mix_acc[1:2, :D], mix_acc[1:2, D:],
        mix_acc[2:3, :D], ffn_acc[0:1], head_acc[1:2], head_acc[0:1], jnp.zeros((REP_ROWS - 10, D), F32)], axis=0)
    own_s, sib_s, rel_s, rep_all = _reduce_scatter([p_small], rep_g)
    owns = [own_i[0], own_s[0], own_m[0], own_m[1], own_f[0], own_f[1]]
    sibs = [sib_i[0], sib_s[0], sib_m[0], sib_m[1], sib_f[0], sib_f[1]]
    rels = [rel_i[0], rel_s[0], rel_m[0], rel_m[1], rel_f[0], rel_f[1]]

    def lead(a):
        return a.reshape(1, *a.shape)

    def stack4(a, lead_dims):
        return a.reshape(*lead_dims, 1, 4 * 32, PG)

    (r_in,) = _adamw_multi("adamw_in", lead(owns[0]), sibs[0][:, None], rels[0][:, None], [w_in], [m_w_in], [v_w_in], 4)
    r_meta, r_dw = _adamw_meta_dw(owns[1], sibs[1], rels[1], (meta_tokens, m_meta_tokens, v_meta_tokens),
                                  (w_dw, m_w_dw, v_w_dw))
    r_conv, r_pout, r_o = _adamw_multi("adamw_mix", owns[2], sibs[2], rels[2], [w_conv_out, w_pool_out, w_o],
                                       [m_w_conv_out, m_w_pool_out, m_w_o], [v_w_conv_out, v_w_pool_out, v_w_o], 1)
    (r_pool,) = _adamw_multi("adamw_pool", stack4(owns[3], ()), stack4(sibs[3], (1,)), stack4(rels[3], (3,)),
                             [w_pool.reshape(1, 128, PG)], [m_w_pool.reshape(1, 128, PG)], [v_w_pool.reshape(1, 128, PG)], 1)
    r_pool = tuple(a.reshape(w_pool.shape) for a in r_pool)
    r_gate, r_up = _adamw_multi("adamw_gu", owns[4], sibs[4], rels[4], [tr(w_ffn_gate), tr(w_ffn_up)],
                                [tr(m_w_ffn_gate), tr(m_w_ffn_up)], [tr(v_w_ffn_gate), tr(v_w_ffn_up)], 2)
    r_gate, r_up = tuple(tr(a) for a in r_gate), tuple(tr(a) for a in r_up)
    (r_down,) = _adamw_multi("adamw_down", lead(owns[5]), sibs[5][:, None], rels[5][:, None],
                             [w_ffn_down], [m_w_ffn_down], [v_w_ffn_down], 2)
    row = (1, D)
    loss, reps = _adamw_rep(
        rep_all,
        [g_mix, b_gate, b_dw, ln_g, ln_b, pool_scale, g_ffn, g_final.reshape(row)],
        [m_g_mix, m_b_gate, m_b_dw, m_ln_g, m_ln_b, m_pool_scale, m_g_ffn, m_g_final.reshape(row)],
        [v_g_mix, v_b_gate, v_b_dw, v_ln_g, v_ln_b, v_pool_scale, v_g_ffn, v_g_final.reshape(row)])
    r_gmix, r_bg, r_bdw, r_lg, r_lb, r_ps, r_gffn, r_gfin = reps
    r_gfin = tuple(a.reshape(D) for a in r_gfin)

    in_order = [r_meta, r_gmix, r_in, r_bg, r_dw, r_bdw, r_lg, r_lb, r_conv, r_pool, r_ps, r_pout, r_o, r_gffn,
                r_gate, r_up, r_down, r_gfin]
    return (loss.reshape(()), grad_x[None], *[r[0] for r in in_order], *[r[1] for r in in_order],
            *[r[2] for r in in_order], *[r[3] for r in in_order])
```

```python
import math

import jax
import jax.numpy as jnp
from jax import lax
from jax.experimental import pallas as pl
from jax.experimental.pallas import tpu as pltpu

F32, BF16 = jnp.float32, jnp.bfloat16
MESH_ID = pl.DeviceIdType.MESH
NDEV = 8

D = 1024
N_META = 16
CONV_K = 31
HALO = 16
POOL_WINDOWS = (2, 4, 8, 16)
PG = 256
DIN = 5 * D
DFF = 2816
FFB = DFF // NDEV
FFC = DFF // 2
INB = DIN // NDEV
RMS_EPS = 1e-6
LN_EPS = 1e-5
ADAM_LR, ADAM_B1, ADAM_B2, ADAM_EPS, ADAM_WD, ADAM_STEP = 0.001, 0.9, 0.999, 1e-08, 0.01, 10

TM = 384
TMS = 384
TM_IO = 704
TM_WG = 1408
TM_WM = 704
MIB = 2 ** 20


def _sig(x):
    return 0.5 * jnp.tanh(0.5 * x) + 0.5


def _dot(a, b):
    return jnp.dot(a, b, preferred_element_type=F32)


def _dot_nt(a, b):
    return lax.dot_general(a, b, (((1,), (1,)), ((), ())), preferred_element_type=F32)


def _dot_tn(a, b):
    return lax.dot_general(a, b, (((0,), (0,)), ((), ())), preferred_element_type=F32)


def _pick(tp, pref):
    return pref if tp % pref == 0 else TM


def _params(sem, vmem_mib):
    return pltpu.CompilerParams(dimension_semantics=sem, vmem_limit_bytes=vmem_mib * MIB)


def _load_once(first, pairs, sems):
    @pl.when(first)
    def _():
        cps = [pltpu.make_async_copy(s, d, sems.at[k]) for k, (s, d) in enumerate(pairs)]
        for cp in cps:
            cp.start()
        for cp in cps:
            cp.wait()


def _place():
    x, y, c = lax.axis_index("x"), lax.axis_index("y"), lax.axis_index("c")
    return x, y, c


class _Gather:
    def __init__(self, groups, dtypes):
        self.groups, self.dtypes, self.n = groups, dtypes, len(groups)
        self.arrays = [a for _, parts in groups for a, _, _ in parts]
        self.out_shape = [jax.ShapeDtypeStruct((NDEV, *s), dt) for (s, _), dt in zip(groups, dtypes)]
        self.scratch = [pltpu.VMEM(s, dt) for (s, _), dt in zip(groups, dtypes)] + [
            pltpu.SemaphoreType.DMA((7 * self.n,)), pltpu.SemaphoreType.DMA((7 * self.n,)),
            pltpu.SemaphoreType.DMA((self.n,))]

    def bind(self, ins, outs, scratch):
        self.ins, self.outs, self.stages = ins, outs, scratch[:self.n]
        self.send_sems, self.recv_sems, self.local_sems = scratch[self.n:]
        return self

    def _copy(self, w, k, block, to, src=None):
        dst = self.outs[w].at[4 * block[0] + 2 * block[1] + block[2]]
        return pltpu.make_async_remote_copy(
            src_ref=dst if src is None else src, dst_ref=dst,
            send_sem=self.send_sems.at[7 * w + k], recv_sem=self.recv_sems.at[7 * w + k],
            device_id=to, device_id_type=MESH_ID)

    def _first(self):
        x, y, c = _place()
        me, sibling = (x, y, c), (x, y, 1 - c)
        chips = [(1 - x, y), (x, 1 - y), (1 - x, 1 - y)]
        mine, first = [], []
        for w in range(self.n):
            mine.append(pltpu.make_async_copy(self.stages[w], self.outs[w].at[4 * x + 2 * y + c], self.local_sems.at[w]))
            first.append(self._copy(w, 0, me, sibling, src=self.stages[w]))
            first += [self._copy(w, 1 + j, me, (*chip, c), src=self.stages[w]) for j, chip in enumerate(chips)]
        return mine, first

    def _passed(self):
        x, y, c = _place()
        chips = [(1 - x, y), (x, 1 - y), (1 - x, 1 - y)]
        return [self._copy(w, 4 + j, (*chip, c), (x, y, 1 - c)) for w in range(self.n) for j, chip in enumerate(chips)]

    def issue(self):
        a = 0
        for w in range(self.n):
            shape, parts = self.groups[w]
            if sum(arr.size for arr, _, _ in parts) < math.prod(shape):
                self.stages[w][...] = jnp.zeros(shape, self.dtypes[w])
            for _, dst, src in parts:
                self.stages[w][dst] = self.ins[a][src].astype(self.dtypes[w])
                a += 1
        mine, first = self._first()
        for cp in mine + first:
            cp.start()

    def forward(self):
        x, y, c = _place()
        chips = [(1 - x, y), (x, 1 - y), (1 - x, 1 - y)]
        passed = self._passed()
        for w in range(self.n):
            for j, chip in enumerate(chips):
                self._copy(w, 1 + j, (*chip, c), (x, y, c)).wait_recv()
                passed[3 * w + j].start()

    def finish(self):
        x, y, c = _place()
        chips = [(1 - x, y), (x, 1 - y), (1 - x, 1 - y)]
        for w in range(self.n):
            self._copy(w, 0, (x, y, 1 - c), (x, y, c)).wait_recv()
            for j, chip in enumerate(chips):
                self._copy(w, 4 + j, (*chip, 1 - c), (x, y, c)).wait_recv()
        mine, first = self._first()
        for cp in first + self._passed():
            cp.wait_send()
        for cp in mine:
            cp.wait()


class _ChipExchange:
    def __init__(self, qs):
        self.n = len(qs)
        self.out_shape = [jax.ShapeDtypeStruct(q.shape, q.dtype) for q in qs]
        self.scratch = [pltpu.SemaphoreType.DMA((3 * self.n,)), pltpu.SemaphoreType.DMA((3 * self.n,))]

    def bind(self, qs, rels, scratch):
        self.qs, self.rels = qs, rels
        self.send_sems, self.recv_sems = scratch
        return self

    def _copies(self):
        x, y, c = _place()
        chips = [(1 - x, y), (x, 1 - y), (1 - x, 1 - y)]
        return [pltpu.make_async_remote_copy(
            src_ref=self.qs[w].at[j], dst_ref=self.rels[w].at[j],
            send_sem=self.send_sems.at[3 * w + j], recv_sem=self.recv_sems.at[3 * w + j],
            device_id=(*chips[j], c), device_id_type=MESH_ID) for w in range(self.n) for j in range(3)]

    def issue(self):
        for cp in self._copies():
            cp.start()

    def finish(self):
        cps = self._copies()
        for cp in cps:
            cp.wait_recv()
        for cp in cps:
            cp.wait_send()


def _reduce_scatter(parts, small):
    n = len(parts)
    blks = [p.shape[1:] for p in parts]

    def body(*refs):
        ps, small_ref = refs[:n], refs[n]
        o = n + 1
        owns, sibs, rels, small_out = refs[o:o + n], refs[o + n:o + 2 * n], refs[o + 2 * n:o + 3 * n], refs[o + 3 * n]
        o += 3 * n + 1
        pa, pb, qst = refs[o:o + n], refs[o + n:o + 2 * n], refs[o + 2 * n:o + 3 * n]
        s1_send, s1_recv, s2_send, s2_recv, sm_send, sm_recv, lsem = refs[o + 3 * n:]
        x, y, c = _place()
        me = 4 * x + 2 * y + c
        sibling = (x, y, 1 - c)
        chips = [(1 - x, y), (x, 1 - y), (1 - x, 1 - y)]
        all_chips = [(x, y)] + chips

        own_cps = []
        for w in range(n):
            cp = pltpu.make_async_copy(ps[w].at[me], owns[w], lsem.at[w])
            cp.start()
            own_cps.append(cp)
        sm_own = pltpu.make_async_copy(small_ref, small_out.at[me], lsem.at[n])
        sm_own.start()

        def small_copy(r):
            peer = ((x + (r >> 2)) % 2, (y + ((r >> 1) & 1)) % 2, (c + (r & 1)) % 2)
            return pltpu.make_async_remote_copy(
                src_ref=small_ref, dst_ref=small_out.at[me], send_sem=sm_send.at[r - 1], recv_sem=sm_recv.at[r - 1],
                device_id=peer, device_id_type=MESH_ID)

        sm_cps = [small_copy(r) for r in range(1, NDEV)]
        for cp in sm_cps:
            cp.start()

        def pair_copy(w, rel):
            cx, cy = all_chips[rel]
            return pltpu.make_async_remote_copy(
                src_ref=ps[w].at[4 * cx + 2 * cy + (1 - c)], dst_ref=sibs[w].at[rel],
                send_sem=s1_send.at[4 * w + rel], recv_sem=s1_recv.at[4 * w + rel],
                device_id=sibling, device_id_type=MESH_ID)

        def chip_copy(w, j):
            return pltpu.make_async_remote_copy(
                src_ref=qst[w].at[j], dst_ref=rels[w].at[j],
                send_sem=s2_send.at[3 * w + j], recv_sem=s2_recv.at[3 * w + j],
                device_id=(*chips[j], c), device_id_type=MESH_ID)

        pair_cps = [pair_copy(w, rel) for w in range(n) for rel in (1, 2, 3, 0)]
        for cp in pair_cps:
            cp.start()
        chip_cps = []
        for w in range(n):
            for j, (cx, cy) in enumerate(chips):
                pair_copy(w, 1 + j).wait_recv()
                la = pltpu.make_async_copy(ps[w].at[4 * cx + 2 * cy + c], pa[w], lsem.at[n + 1])
                lb = pltpu.make_async_copy(sibs[w].at[1 + j], pb[w], lsem.at[n + 2])
                la.start()
                lb.start()
                la.wait()
                lb.wait()
                qst[w][j] = (pa[w][...].astype(F32) + pb[w][...].astype(F32)).astype(BF16)
                cp = chip_copy(w, j)
                cp.start()
                chip_cps.append(cp)
        for w in range(n):
            pair_copy(w, 0).wait_recv()
            for j in range(3):
                chip_copy(w, j).wait_recv()
        for cp in sm_cps:
            cp.wait_recv()
        for cp in pair_cps + chip_cps + sm_cps:
            cp.wait_send()
        for cp in own_cps:
            cp.wait()
        sm_own.wait()

    any_spec = pl.BlockSpec(memory_space=pl.ANY)
    outs = pl.pallas_call(
        body, name="rs_grads",
        out_shape=[jax.ShapeDtypeStruct(b, BF16) for b in blks]
        + [jax.ShapeDtypeStruct((4, *b), BF16) for b in blks]
        + [jax.ShapeDtypeStruct((3, *b), BF16) for b in blks]
        + [jax.ShapeDtypeStruct((NDEV, *small.shape), F32)],
        in_specs=[any_spec] * (n + 1),
        out_specs=[any_spec] * (3 * n + 1),
        scratch_shapes=[pltpu.VMEM(b, BF16) for b in blks] + [pltpu.VMEM(b, BF16) for b in blks]
        + [pltpu.VMEM((3, *b), BF16) for b in blks]
        + [pltpu.SemaphoreType.DMA((4 * n,)), pltpu.SemaphoreType.DMA((4 * n,)),
           pltpu.SemaphoreType.DMA((3 * n,)), pltpu.SemaphoreType.DMA((3 * n,)),
           pltpu.SemaphoreType.DMA((NDEV - 1,)), pltpu.SemaphoreType.DMA((NDEV - 1,)),
           pltpu.SemaphoreType.DMA((n + 3,))],
        compiler_params=pltpu.CompilerParams(vmem_limit_bytes=40 * MIB),
    )(*parts, small)
    return outs[:n], outs[n:2 * n], outs[2 * n:3 * n], outs[3 * n]


class _PairSum:
    def __init__(self, parts, keep_q=True):
        self.n = n = len(parts)
        self.keep_q = keep_q
        blks = [p.shape[1:] for p in parts]
        self.out_shape = [jax.ShapeDtypeStruct(b, BF16) for b in blks] + [jax.ShapeDtypeStruct((1, *b), BF16) for b in blks]
        if keep_q:
            self.out_shape += [jax.ShapeDtypeStruct((3, *b), BF16) for b in blks]
        self.scratch = [pltpu.VMEM((3, *b), BF16) for b in blks] * 3 + [
            pltpu.SemaphoreType.DMA((4 * n,)), pltpu.SemaphoreType.DMA((4 * n,)), pltpu.SemaphoreType.DMA((5 * n,))]

    def bind(self, ps, outs, scratch):
        n = self.n
        self.ps, self.owns, self.sibs, self.qs = ps, outs[:n], outs[n:2 * n], outs[2 * n:]
        self.pa, self.pb, self.qst = scratch[:n], scratch[n:2 * n], scratch[2 * n:3 * n]
        self.s_send, self.s_recv, self.lsem = scratch[3 * n:]
        return self

    def _local(self, with_q):
        n = self.n
        x, y, c = _place()
        chips = [(1 - x, y), (x, 1 - y), (1 - x, 1 - y)]
        own = [pltpu.make_async_copy(self.ps[w].at[4 * x + 2 * y + c], self.owns[w], self.lsem.at[w]) for w in range(n)]
        mine = [[pltpu.make_async_copy(self.ps[w].at[4 * cx + 2 * cy + c], self.pa[w].at[j], self.lsem.at[2 * n + 3 * w + j])
                 for j, (cx, cy) in enumerate(chips)] for w in range(n)]
        outq = [pltpu.make_async_copy(self.qst[w], self.qs[w], self.lsem.at[n + w]) for w in range(n)] if with_q else []
        return own, mine, outq

    def _pair(self, w, rel):
        x, y, c = _place()
        cx, cy = [(x, y), (1 - x, y), (x, 1 - y), (1 - x, 1 - y)][rel]
        return pltpu.make_async_remote_copy(
            src_ref=self.ps[w].at[4 * cx + 2 * cy + (1 - c)],
            dst_ref=self.sibs[w].at[0] if rel == 0 else self.pb[w].at[rel - 1],
            send_sem=self.s_send.at[4 * w + rel], recv_sem=self.s_recv.at[4 * w + rel],
            device_id=(x, y, 1 - c), device_id_type=MESH_ID)

    def issue(self):
        own, mine, _ = self._local(False)
        for cp in own + [cp for row in mine for cp in row]:
            cp.start()
        for w in range(self.n):
            for rel in (1, 2, 3, 0):
                self._pair(w, rel).start()

    def finish(self):
        own, mine, outq = self._local(self.keep_q)
        for w in range(self.n):
            for j in range(3):
                self._pair(w, 1 + j).wait_recv()
                mine[w][j].wait()
                self.qst[w][j] = (self.pa[w][j].astype(F32) + self.pb[w][j].astype(F32)).astype(BF16)
            if self.keep_q:
                outq[w].start()
        for w in range(self.n):
            self._pair(w, 0).wait_recv()
        for w in range(self.n):
            for rel in range(4):
                self._pair(w, rel).wait_send()
        for cp in own + outq:
            cp.wait()

    def results(self, outs):
        n = self.n
        return outs[:n], outs[n:2 * n], outs[2 * n:3 * n]


def _rs_pair(name, parts):
    ps = _PairSum(parts)
    n = ps.n

    def body(*refs):
        ps.bind(refs[:n], refs[n:4 * n], refs[4 * n:])
        ps.issue()
        ps.finish()

    any_spec = pl.BlockSpec(memory_space=pl.ANY)
    outs = pl.pallas_call(
        body, name=name, out_shape=ps.out_shape,
        in_specs=[any_spec] * n, out_specs=[any_spec] * (3 * n), scratch_shapes=ps.scratch,
        compiler_params=pltpu.CompilerParams(vmem_limit_bytes=48 * MIB),
    )(*parts)
    return ps.results(outs)


def _adamw_math(g, w, m, v):
    m = ADAM_B1 * m + (1.0 - ADAM_B1) * g
    v = ADAM_B2 * v + (1.0 - ADAM_B2) * (g * g)
    m_hat = m / (1.0 - ADAM_B1 ** ADAM_STEP)
    v_hat = v / (1.0 - ADAM_B2 ** ADAM_STEP)
    delta = -ADAM_LR * (m_hat / (jnp.sqrt(v_hat) + ADAM_EPS) + ADAM_WD * w)
    return delta, m, v


def _adamw_multi(name, own, sib, rel, ws, ms, vs, row_grid):
    k_n, r_n, c_n = own.shape
    rbk = r_n // row_grid

    def body(*refs):
        own_ref, sib_ref, r0_ref, r1_ref, r2_ref = refs[:5]
        w_refs, m_refs, v_refs = refs[5:5 + k_n], refs[5 + k_n:5 + 2 * k_n], refs[5 + 2 * k_n:5 + 3 * k_n]
        outs = refs[5 + 3 * k_n:]
        for k in range(k_n):
            g = own_ref[k].astype(F32) + sib_ref[k].astype(F32)
            g = g + r0_ref[k].astype(F32)
            g = g + r1_ref[k].astype(F32)
            g = g + r2_ref[k].astype(F32)
            delta, mm, vv = _adamw_math(g, w_refs[k][0], m_refs[k][0], v_refs[k][0])
            outs[4 * k][0] = g
            outs[4 * k + 1][0] = delta
            outs[4 * k + 2][0] = mm
            outs[4 * k + 3][0] = vv

    def lead(j):
        return pl.BlockSpec((None, k_n, rbk, c_n), lambda g: (j, 0, g, 0))

    wspec = pl.BlockSpec((1, rbk, c_n), lambda g: (0, g, 0))
    shp = jax.ShapeDtypeStruct((1, r_n, c_n), F32)
    res = pl.pallas_call(
        body, name=name, grid=(row_grid,),
        in_specs=[pl.BlockSpec((k_n, rbk, c_n), lambda g: (0, g, 0)), lead(0), lead(0), lead(1), lead(2)] + [wspec] * (3 * k_n),
        out_specs=[wspec] * (4 * k_n), out_shape=[shp] * (4 * k_n),
        compiler_params=_params(("arbitrary",), 40),
    )(own, sib, rel, rel, rel, *ws, *ms, *vs)
    return [tuple(res[4 * k:4 * k + 4]) for k in range(k_n)]


def _adamw_meta_dw(own, sib, rel, meta, dw):
    def body(own_ref, sib_ref, rel_ref, wm, mm, vm, wd, md, vd, *outs):
        def gsum(rows):
            g = own_ref[rows, :].astype(F32) + sib_ref[0, rows, :].astype(F32)
            for j in range(3):
                g = g + rel_ref[j, rows, :].astype(F32)
            return g

        g = gsum(pl.ds(0, N_META))
        delta, m2, v2 = _adamw_math(g, wm[...], mm[...], vm[...])
        for o, val in zip(outs[:4], (g, delta, m2, v2)):
            o[...] = val
        g = gsum(pl.ds(N_META, CONV_K))
        delta, m2, v2 = _adamw_math(g, wd[0], md[0], vd[0])
        for o, val in zip(outs[4:], (g, delta, m2, v2)):
            o[0] = val

    s_meta = jax.ShapeDtypeStruct(meta[0].shape, F32)
    s_dw = jax.ShapeDtypeStruct(dw[0].shape, F32)
    res = pl.pallas_call(body, name="adamw_meta_dw", out_shape=[s_meta] * 4 + [s_dw] * 4)(own, sib, rel, *meta, *dw)
    return tuple(res[:4]), tuple(res[4:])


REP_ROWS = 16


def _adamw_rep(gathered, ws, ms, vs):
    rows = [(0, 1), (1, 2), (3, 1), (4, 1), (5, 1), (6, 1), (7, 1), (8, 1)]

    def body(g_ref, *refs):
        w_refs, m_refs, v_refs = refs[:8], refs[8:16], refs[16:24]
        loss_ref, outs, acc = refs[24], refs[25:57], refs[57]
        g = g_ref[0]
        for d in range(1, NDEV):
            g = g + g_ref[d]
        acc[...] = g
        loss_ref[...] = (0.5 / D) * jnp.sum(acc[pl.ds(9, 1), :], axis=1, keepdims=True)
        for p, (r0, nr) in enumerate(rows):
            for h in range(nr):
                cols = pl.ds(h * D, D)
                gp = acc[pl.ds(r0 + h, 1), :]
                delta, mm, vv = _adamw_math(gp, w_refs[p][:, cols], m_refs[p][:, cols], v_refs[p][:, cols])
                for o, val in zip(outs[4 * p:4 * p + 4], (gp, delta, mm, vv)):
                    o[:, cols] = val

    shapes = [jax.ShapeDtypeStruct(w.shape, F32) for w in ws]
    res = pl.pallas_call(
        body, name="adamw_rep",
        out_shape=[jax.ShapeDtypeStruct((1, 1), F32)] + [s for s in shapes for _ in range(4)],
        scratch_shapes=[pltpu.VMEM((REP_ROWS, D), F32)],
    )(gathered, *ws, *ms, *vs)
    return res[0], [tuple(res[1 + 4 * p:5 + 4 * p]) for p in range(8)]


def _load_ffn(i, j, wgu_hbm, wgu, wdn_hbm, wdn, sems):
    half = NDEV // 2

    def copies(ch):
        pairs = [(wgu_hbm.at[half * ch + d, g], wgu.at[g, ch, pl.ds(FFB * d, FFB), :]) for g in range(2) for d in range(half)]
        pairs.append((wdn_hbm.at[ch], wdn.at[ch]))
        return [pltpu.make_async_copy(s, t, sems.at[(2 * half + 1) * ch + k]) for k, (s, t) in enumerate(pairs)]

    @pl.when((i == 0) & (j == 0))
    def _():
        for cp in copies(0) + copies(1):
            cp.start()

    for ch in range(2):
        @pl.when((i == 0) & (j == ch))
        def _():
            for cp in copies(ch):
                cp.wait()


def _win_pairs(w_hbm, w_vm):
    return [(w_hbm.at[q], w_vm.at[q // 2, :, pl.ds(2 * INB * (q % 2), 2 * INB)]) for q in range(4)]


def _whole(a):
    nd = a.ndim
    return pl.BlockSpec(a.shape, lambda *g: (0,) * nd)


CHIPW = 2 * INB
PHASE_CHIP = (1, 0, 2)


class _GatherIn:
    scratch = [pltpu.VMEM((D, INB), BF16), pltpu.SemaphoreType.DMA((7,)), pltpu.SemaphoreType.DMA((7,)),
               pltpu.SemaphoreType.DMA((1,))]

    def bind(self, w_ref, w_vm, scratch):
        self.w_ref, self.w_vm = w_ref, w_vm
        self.stage, self.send_sems, self.recv_sems, self.local_sem = scratch
        return self

    def _win(self, chip, core):
        return self.w_vm.at[2 * chip[0] + chip[1], :, pl.ds(INB * core, INB)]

    def _copy(self, k, chip, core, to, src=None):
        dst = self._win(chip, core)
        return pltpu.make_async_remote_copy(
            src_ref=dst if src is None else src, dst_ref=dst, send_sem=self.send_sems.at[k],
            recv_sem=self.recv_sems.at[k], device_id=to, device_id_type=MESH_ID)

    def _mine(self, cs):
        x, y, _ = _place()
        return pltpu.make_async_copy(self.stage, self._win((x, y), cs), self.local_sem.at[0])

    def issue(self, cs):
        x, y, _ = _place()
        chips = [(1 - x, y), (x, 1 - y), (1 - x, 1 - y)]
        self.stage[...] = self.w_ref[0].astype(BF16)
        self._mine(cs).start()
        self._copy(0, (x, y), cs, (x, y, 1 - cs), src=self.stage).start()
        for j, chip in enumerate(chips):
            self._copy(1 + j, (x, y), cs, (*chip, cs), src=self.stage).start()

    def wait_chip(self, phase, cs):
        x, y, _ = _place()
        chips = [(1 - x, y), (x, 1 - y), (1 - x, 1 - y)]
        if phase == 0:
            self._mine(cs).wait()
            self._copy(0, (x, y), 1 - cs, (x, y, cs)).wait_recv()
            return
        if phase == 1:
            for j in PHASE_CHIP:
                self._copy(1 + j, chips[j], cs, (x, y, cs)).wait_recv()
                self._copy(4 + j, chips[j], cs, (x, y, 1 - cs)).start()
        j = PHASE_CHIP[phase - 1]
        self._copy(4 + j, chips[j], 1 - cs, (x, y, cs)).wait_recv()

    def finish(self, cs):
        x, y, _ = _place()
        for k in range(7):
            self._copy(k, (x, y), cs, (x, y, cs), src=self.stage).wait_send()


def _fwd_in(x2, g_mix, w_in, order, tp, ag, ags):
    tm = _pick(tp, TM_IO)
    nt = tp // tm
    nx_last = x2.shape[0] - (nt - 1) * tm
    na, ng, ns = len(ag.arrays), ag.n, len(ags.arrays)
    gin = _GatherIn()

    def body(order_ref, *refs):
        x_ref, g_ref, w_ref = refs[:3]
        o = 3 + na + ns
        h_ref, z_ref, u_ref, wout_ref = refs[o:o + 4]
        s = o + 4 + ng + 1
        w_vm, u_all, osem, sm_vm = refs[s:s + 4]
        gin.bind(w_ref, w_vm, refs[s + 4:s + 8])
        ag.bind(refs[3:3 + na], refs[o + 4:o + 4 + ng], refs[s + 8:s + 8 + len(ag.scratch)])
        ags.bind(refs[3 + na:3 + na + ns], refs[o + 4 + ng:o + 5 + ng], refs[s + 8 + len(ag.scratch):])
        ph, i = pl.program_id(0), pl.program_id(1)
        core = lax.axis_index("c")
        first = (ph == 0) & (i == 0)
        last = (ph == 3) & (i == nt - 1)
        for cs in range(2):
            @pl.when(first & (core == cs))
            def _():
                gin.issue(cs)

        @pl.when(first)
        def _():
            ags.issue()
            ag.issue()

        @pl.when((ph == 0) & (i == max(nt - 2, 0)))
        def _():
            ags.forward()

        for cs in range(2):
            for p in range(4):
                @pl.when((ph == p) & (i == 0) & (core == cs))
                def _():
                    gin.wait_chip(p, cs)

        @pl.when((ph == 3) & (i == max(nt - 2, 0)))
        def _():
            ag.forward()

        out_copy = pltpu.make_async_copy(w_vm, wout_ref, osem.at[0])

        @pl.when((ph == 3) & (i == 0))
        def _():
            out_copy.start()

        @pl.when((ph == 0) & (i < nt - 1))
        def _():
            h_ref[...] = x_ref[...]

        @pl.when((ph == 0) & (i == nt - 1))
        def _():
            ags.finish()
            cp = pltpu.make_async_copy(ags.outs[0], sm_vm, osem.at[1])
            cp.start()
            h_ref[pl.ds(0, nx_last), :] = x_ref[pl.ds(0, nx_last), :]
            h_ref[pl.ds(nx_last, tm - nx_last - N_META), :] = jnp.zeros((tm - nx_last - N_META, D), F32)
            cp.wait()
            for d in range(NDEV):
                h_ref[pl.ds(tm - N_META, N_META), pl.ds(128 * d, 128)] = sm_vm[d, pl.ds(0, N_META), :]

        @pl.when(ph == 0)
        def _():
            xv = h_ref[...]
            r = lax.rsqrt(jnp.mean(xv * xv, axis=-1, keepdims=True) + RMS_EPS)
            u = (xv * r * g_ref[...]).astype(BF16)
            u_ref[...] = u
            u_all[i] = u

        z_ref[...] = _dot(u_all[i], w_vm[order_ref[ph]])

        @pl.when(last)
        def _():
            ag.finish()
            out_copy.wait()

        for cs in range(2):
            @pl.when(last & (core == cs))
            def _():
                gin.finish(cs)

    def rows(ph, i, order):
        return (jnp.where(ph == 0, i, nt - 1), 0)

    tile = pl.BlockSpec((tm, D), rows)
    anys = pl.BlockSpec(memory_space=pl.ANY)
    res = pl.pallas_call(
        body, name="fwd_in",
        grid_spec=pltpu.PrefetchScalarGridSpec(
            num_scalar_prefetch=1, grid=(4, nt),
            in_specs=[tile, pl.BlockSpec((1, D), lambda ph, i, order: (0, 0)), _whole(w_in)]
            + [_whole(a) for a in ag.arrays + ags.arrays],
            out_specs=[tile, pl.BlockSpec((tm, CHIPW), lambda ph, i, order: (i, order[ph])), tile, anys] + [anys] * (ng + 1),
            scratch_shapes=[pltpu.VMEM((4, D, CHIPW), BF16), pltpu.VMEM((nt, tm, D), BF16), pltpu.SemaphoreType.DMA((2,)),
                            pltpu.VMEM(ags.out_shape[0].shape, F32)] + gin.scratch + ag.scratch + ags.scratch),
        out_shape=[jax.ShapeDtypeStruct((tp, D), F32), jax.ShapeDtypeStruct((tp, DIN), F32),
                   jax.ShapeDtypeStruct((tp, D), BF16), jax.ShapeDtypeStruct((4, D, CHIPW), BF16)]
        + ag.out_shape + ags.out_shape,
        compiler_params=_params(("arbitrary", "arbitrary"), 58),
    )(order, x2, g_mix, w_in, *ag.arrays, *ags.arrays)
    return res[:4], res[4:4 + ng], res[4 + ng]


def _halo_specs(col, nt, width=D):
    r = TM // HALO
    nb = nt * r
    return [pl.BlockSpec((HALO, width), lambda i: ((i * r + nb - 1) % nb, col)),
            pl.BlockSpec((TM, width), lambda i: (i, col)),
            pl.BlockSpec((HALO, width), lambda i: (((i + 1) * r) % nb, col))]


NCB = D // 128
TME = TM + 2 * HALO


def _tm_fill(dst, time0, groups, tile_fn):
    def body(g, c):
        for j in range(NCB):
            dst[pl.ds((time0 + 8 * g) * NCB + j, 8, stride=NCB), :] = tile_fn(pl.multiple_of(8 * g, 8), pl.ds(128 * j, 128))
        return c

    lax.fori_loop(0, groups, body, 0)


def _tm_fill_ext(dst, left, cur, right, fn):
    _tm_fill(dst, 0, HALO // 8, lambda r, l: fn(left, pl.ds(r, 8), l))
    _tm_fill(dst, HALO, TM // 8, lambda r, l: fn(cur, pl.ds(r, 8), l))
    _tm_fill(dst, HALO + TM, HALO // 8, lambda r, l: fn(right, pl.ds(r, 8), l))


def _tm_read(src, groups, store_fn):
    def body(g, c):
        for j in range(NCB):
            store_fn(pl.ds(pl.multiple_of(8 * g, 8), 8), pl.ds(128 * j, 128), src[pl.ds(8 * g * NCB + j, 8, stride=NCB), :])
        return c

    lax.fori_loop(0, groups, body, 0)


def _tm_rows(t):
    return pl.ds(t * NCB if isinstance(t, int) else pl.multiple_of(t * NCB, NCB), NCB)


def _tm_at(ref, t):
    return ref[_tm_rows(t), :]


def _by_group(sub, vals):
    return jnp.where(sub < 2, vals[0], jnp.where(sub < 4, vals[1], jnp.where(sub < 6, vals[2], vals[3])))


def _pool_cnt(b, seq, tp, sub):
    b = jnp.where(b < 0, b + tp, b)
    b = jnp.where(b >= tp, b - tp, b)
    t = jnp.where(b < seq, b + N_META, b - (tp - N_META))
    cnts = []
    for win in POOL_WINDOWS:
        left = win // 2
        lo = jnp.maximum(t - left, 0)
        hi = jnp.minimum(t + win - left, seq + N_META)
        cnts.append(jnp.maximum(hi - lo, 1).astype(F32))
    return _by_group(sub, cnts)


def _edge_rows(seq, tp):
    reach = max(POOL_WINDOWS) // 2
    return [tp - N_META + t for t in range(reach)] + [seq - reach + 1 + t for t in range(reach - 1)]


def _edge_gain(b, seq, tp, sub):
    return _by_group(sub, [float(w) for w in POOL_WINDOWS]) / _pool_cnt(b, seq, tp, sub)


def _nested_windows(at, lo_offs):
    sums, s, have = [], None, set()
    for g, win in enumerate(POOL_WINDOWS):
        for o in range(lo_offs[g], lo_offs[g] + win):
            if o not in have:
                have.add(o)
                s = at(o) if s is None else s + at(o)
        sums.append(s)
    return sums


def _seq_fwd(z, w_dw, b_dw, seq, gat):
    tp = z.shape[0]
    nt = tp // TM
    na, ng = len(gat.arrays), gat.n

    def body(*refs):
        av_l, av, av_r, ag_l, ag, ag_r, p_l, p, p_r, w_ref, b_ref = refs[:11]
        ac_ref, m_ref = refs[11 + na:13 + na]
        a3, p3, o3, m3, w3, b3, m2d = refs[13 + na + ng:20 + na + ng]
        gat.bind(refs[11:11 + na], refs[13 + na:13 + na + ng], refs[20 + na + ng:])
        i = pl.program_id(0)
        sub = lax.broadcasted_iota(jnp.int32, (NCB, 128), 0)

        @pl.when(i == 0)
        def _():
            gat.issue()
            _tm_fill(w3, 0, 4, lambda r, l: w_ref[pl.ds(r, 8), l])
            for j in range(NCB):
                b3[pl.ds(j, 1), :] = b_ref[:, pl.ds(128 * j, 128)]

        @pl.when(i == max(nt - 2, 0))
        def _():
            gat.forward()

        _tm_fill_ext(a3, (av_l, ag_l), (av, ag), (av_r, ag_r), lambda vg, r, l: vg[0][r, l] * _sig(vg[1][r, l]))
        _tm_fill_ext(p3, p_l, p, p_r, lambda ref, r, l: ref[r, l])

        def conv(g, c):
            accs = [b3[...]] * 16
            for k in range(CONV_K):
                wk = _tm_at(w3, k)
                for t in range(16):
                    accs[t] = accs[t] + wk * _tm_at(a3, 16 * g + t + k + 1)
            for t in range(16):
                o3[_tm_rows(16 * g + t), :] = accs[t]
            return c

        lax.fori_loop(0, TM // 16, conv, 0)
        _tm_read(o3, TM // 8, lambda r, l, tile: ac_ref.__setitem__((r, l), tile))

        inv = _by_group(sub, [1.0 / w for w in POOL_WINDOWS])

        def pool(g, c):
            for t in range(8):
                e = 8 * g + t + HALO
                sums = _nested_windows(lambda o: _tm_at(p3, e + o), [-(w // 2) for w in POOL_WINDOWS])
                m3[_tm_rows(8 * g + t), :] = _by_group(sub, sums) * inv - _tm_at(p3, e)
            return c

        lax.fori_loop(0, TM // 8, pool, 0)
        for b in _edge_rows(seq, tp):
            r = b - i * TM

            @pl.when((r >= 0) & (r < TM))
            def _():
                pv = _tm_at(p3, r + HALO)
                m3[_tm_rows(r), :] = (_tm_at(m3, r) + pv) * _edge_gain(b, seq, tp, sub) - pv

        _tm_read(m3, TM // 8, lambda r, l, tile: m2d.__setitem__((r, l), tile))
        m_ref[...] = m2d[...].astype(BF16)

        @pl.when(i == nt - 1)
        def _():
            gat.finish()

    tmaj = pltpu.VMEM((TM * NCB, 128), F32)
    text = pltpu.VMEM((TME * NCB, 128), F32)
    res = pl.pallas_call(
        body, name="seq_fwd", grid=(nt,),
        in_specs=_halo_specs(0, nt) + _halo_specs(1, nt) + _halo_specs(2, nt)
        + [pl.BlockSpec((32, D), lambda i: (0, 0)), pl.BlockSpec((1, D), lambda i: (0, 0))] + [_whole(a) for a in gat.arrays],
        out_specs=[pl.BlockSpec((TM, D), lambda i: (i, 0))] * 2 + [pl.BlockSpec(memory_space=pl.ANY)] * ng,
        out_shape=[jax.ShapeDtypeStruct((tp, D), F32), jax.ShapeDtypeStruct((tp, D), BF16)] + gat.out_shape,
        scratch_shapes=[text, text, tmaj, tmaj, pltpu.VMEM((32 * NCB, 128), F32), pltpu.VMEM((NCB, 128), F32),
                        pltpu.VMEM((TM, D), F32)] + gat.scratch,
        compiler_params=_params(("arbitrary",), 52),
    )(z, z, z, z, z, z, z, z, z, w_dw, b_dw, *gat.arrays)
    return res[:2], res[2:]


def _ln_stats(ac):
    mu = jnp.mean(ac, axis=-1, keepdims=True)
    xc = ac - mu
    rl = lax.rsqrt(jnp.mean(xc * xc, axis=-1, keepdims=True) + LN_EPS)
    return xc * rl, rl


def _pool_mix(m, wp_ref):
    return jnp.concatenate(
        [_dot(m[:, g * PG:(g + 1) * PG], wp_ref[:, g].reshape(PG, PG)) for g in range(4)], axis=1)


def _mix_fwd(ac, m, z, h0, b_gate, ln_g, ln_b, pool_scale, g_mixw, g_pool, gat):
    tp = h0.shape[0]
    tms = TM
    nt = tp // tms
    na, ng = len(gat.arrays), gat.n

    def body(*refs):
        ac_ref, m_ref, zga, zgb, h_ref, bg_ref, lg_ref, lb_ref, ps_ref, wm_hbm, wp_hbm = refs[:11]
        h1_ref, s_ref, mg_ref, q_ref = refs[11 + na:15 + na]
        wm, wp, sems = refs[15 + na + ng:18 + na + ng]
        gat.bind(refs[11:11 + na], refs[15 + na:15 + na + ng], refs[18 + na + ng:])
        i = pl.program_id(0)

        @pl.when(i == 0)
        def _():
            gat.issue()

        @pl.when(i == max(nt - 4, 0))
        def _():
            gat.forward()

        @pl.when(i == nt - 1)
        def _():
            gat.finish()

        _load_once(i == 0, [(wm_hbm, wm), (wp_hbm, wp)], sems)
        n, _ = _ln_stats(ac_ref[...])
        l = n * lg_ref[...] + lb_ref[...]
        s = (l * _sig(l)).astype(BF16)
        s_ref[...] = s
        yc = _dot(s, wm[:, 0].reshape(D, D))
        q = (_pool_mix(m_ref[...], wp) * ps_ref[...]).astype(BF16)
        q_ref[...] = q
        yp = _dot(q, wm[:, 1].reshape(D, D))
        ga = _sig(zga[...] + bg_ref[:, :D])
        gb = _sig(zgb[...] + bg_ref[:, D:])
        merged = (ga * yc + gb * yp).astype(BF16)
        mg_ref[...] = merged
        h1_ref[...] = h_ref[...] + _dot(merged, wm[:, 2].reshape(D, D))

    def tile(col=0):
        return pl.BlockSpec((tms, D), lambda i: (i, col))

    def vec(w):
        return pl.BlockSpec((1, w), lambda i: (0, 0))

    anys = pl.BlockSpec(memory_space=pl.ANY)
    f32o, b16o = jax.ShapeDtypeStruct((tp, D), F32), jax.ShapeDtypeStruct((tp, D), BF16)
    res = pl.pallas_call(
        body, name="mix_fwd", grid=(nt,),
        in_specs=[tile(), tile(), tile(3), tile(4), tile(), vec(2 * D), vec(D), vec(D), vec(D), anys, anys]
        + [_whole(a) for a in gat.arrays],
        out_specs=[tile()] * 4 + [anys] * ng,
        out_shape=[f32o, b16o, b16o, b16o] + gat.out_shape,
        scratch_shapes=[pltpu.VMEM((NDEV, 3, D // NDEV, D), BF16), pltpu.VMEM((NDEV, 4, PG // NDEV, PG), BF16),
                        pltpu.SemaphoreType.DMA((2,))] + gat.scratch,
        compiler_params=_params(("arbitrary",), 52),
    )(ac, m, z, z, h0, b_gate, ln_g, ln_b, pool_scale, g_mixw, g_pool, *gat.arrays)
    return res[:4], res[4:]


def _ffn_fwd(h1, tgt, g_ffn, g_final, w_gu, w_dn):
    tp = h1.shape[0]
    nt = tp // TM
    nx_last = tgt.shape[0] - (nt - 1) * TM

    def body(h_ref, t_ref, gf_ref, gl_ref, wgu_hbm, wdn_hbm,
             fg_ref, fu_ref, v_ref, f_ref, dh2_ref, acc_ref, wgu, wdn, v_sc, h2_sc, diff_sc, sems):
        i, j = pl.program_id(0), pl.program_id(1)
        _load_ffn(i, j, wgu_hbm, wgu, wdn_hbm, wdn, sems)

        @pl.when((i == 0) & (j == 0))
        def _():
            acc_ref[...] = jnp.zeros_like(acc_ref)

        @pl.when(j == 0)
        def _():
            h = h_ref[...]
            r = lax.rsqrt(jnp.mean(h * h, axis=-1, keepdims=True) + RMS_EPS)
            v = (h * r * gf_ref[...]).astype(BF16)
            v_sc[...] = v
            v_ref[...] = v
            h2_sc[...] = h

        v = v_sc[...]
        fg = _dot_nt(v, wgu[0, j])
        fu = _dot_nt(v, wgu[1, j])
        fg_ref[...] = fg
        fu_ref[...] = fu
        f = ((fg * _sig(fg)) * fu).astype(BF16)
        f_ref[...] = f
        h2_sc[...] += _dot(f, wdn[j])

        @pl.when(j == 1)
        def _():
            h2 = h2_sc[...]
            r = lax.rsqrt(jnp.mean(h2 * h2, axis=-1, keepdims=True) + RMS_EPS)
            n2 = h2 * r
            y = n2 * gl_ref[...]

            @pl.when(i < nt - 1)
            def _():
                diff_sc[...] = y - t_ref[...]

            @pl.when(i == nt - 1)
            def _():
                diff_sc[pl.ds(0, nx_last), :] = y[:nx_last] - t_ref[pl.ds(0, nx_last), :]
                diff_sc[pl.ds(nx_last, TM - nx_last), :] = jnp.zeros((TM - nx_last, D), F32)

            diff = diff_sc[...]
            dy = diff * (1.0 / D)
            acc_ref[0:1, :] += jnp.sum(diff * diff, axis=0, keepdims=True)
            acc_ref[1:2, :] += jnp.sum(dy * n2, axis=0, keepdims=True)
            dn = dy * gl_ref[...]
            dh2_ref[...] = r * (dn - n2 * jnp.mean(dn * n2, axis=-1, keepdims=True))

    def tile():
        return pl.BlockSpec((TM, D), lambda i, j: (i, 0))

    def chunk():
        return pl.BlockSpec((TM, FFC), lambda i, j: (i, j))

    def vec():
        return pl.BlockSpec((1, D), lambda i, j: (0, 0))

    anys = pl.BlockSpec(memory_space=pl.ANY)
    hid32, hid16 = jax.ShapeDtypeStruct((tp, DFF), F32), jax.ShapeDtypeStruct((tp, DFF), BF16)
    return pl.pallas_call(
        body, name="ffn_fwd", grid=(nt, 2),
        in_specs=[tile(), tile(), vec(), vec(), anys, anys],
        out_specs=[chunk(), chunk(), tile(), chunk(), tile(), pl.BlockSpec((8, D), lambda i, j: (0, 0))],
        out_shape=[hid32, hid32, jax.ShapeDtypeStruct((tp, D), BF16), hid16, jax.ShapeDtypeStruct((tp, D), F32),
                   jax.ShapeDtypeStruct((8, D), F32)],
        scratch_shapes=[pltpu.VMEM((2, 2, FFC, D), BF16), pltpu.VMEM((2, FFC, D), BF16),
                        pltpu.VMEM((TM, D), BF16), pltpu.VMEM((TM, D), F32), pltpu.VMEM((TM, D), F32),
                        pltpu.SemaphoreType.DMA((2 * NDEV + 2,))],
        compiler_params=_params(("arbitrary", "arbitrary"), 56),
    )(h1, tgt, g_ffn, g_final, w_gu, w_dn)


def _ffn_bwd(dh2, fg, fu, h1, g_ffn, w_gu, w_dn):
    tp = h1.shape[0]
    nt = tp // TM

    def body(dh2_ref, fg_ref, fu_ref, h_ref, gf_ref, wgu_hbm, wdn_hbm,
             dfg_ref, dfu_ref, dh1_ref, acc_ref, wgu, wdn, d_sc, dv_sc, sems):
        i, j = pl.program_id(0), pl.program_id(1)
        _load_ffn(i, j, wgu_hbm, wgu, wdn_hbm, wdn, sems)

        @pl.when((i == 0) & (j == 0))
        def _():
            acc_ref[...] = jnp.zeros_like(acc_ref)

        @pl.when(j == 0)
        def _():
            d_sc[...] = dh2_ref[...].astype(BF16)
            dv_sc[...] = jnp.zeros_like(dv_sc)

        df = _dot_nt(d_sc[...], wdn[j])
        fg = fg_ref[...]
        sg = _sig(fg)
        dfu = (df * (fg * sg)).astype(BF16)
        dfg = (df * fu_ref[...] * (sg * (1.0 + fg * (1.0 - sg)))).astype(BF16)
        dfg_ref[...] = dfg
        dfu_ref[...] = dfu
        dv_sc[...] += _dot(dfg, wgu[0, j]) + _dot(dfu, wgu[1, j])

        @pl.when(j == 1)
        def _():
            h = h_ref[...]
            r = lax.rsqrt(jnp.mean(h * h, axis=-1, keepdims=True) + RMS_EPS)
            n1 = h * r
            dv = dv_sc[...]
            acc_ref[0:1, :] += jnp.sum(dv * n1, axis=0, keepdims=True)
            dn = dv * gf_ref[...]
            dh1_ref[...] = dh2_ref[...] + r * (dn - n1 * jnp.mean(dn * n1, axis=-1, keepdims=True))

    def tile():
        return pl.BlockSpec((TM, D), lambda i, j: (i, 0))

    def chunk():
        return pl.BlockSpec((TM, FFC), lambda i, j: (i, j))

    anys = pl.BlockSpec(memory_space=pl.ANY)
    hid16 = jax.ShapeDtypeStruct((tp, DFF), BF16)
    return pl.pallas_call(
        body, name="ffn_bwd", grid=(nt, 2),
        in_specs=[tile(), chunk(), chunk(), tile(), pl.BlockSpec((1, D), lambda i, j: (0, 0)), anys, anys],
        out_specs=[chunk(), chunk(), tile(), pl.BlockSpec((8, D), lambda i, j: (0, 0))],
        out_shape=[hid16, hid16, jax.ShapeDtypeStruct((tp, D), F32), jax.ShapeDtypeStruct((8, D), F32)],
        scratch_shapes=[pltpu.VMEM((2, 2, FFC, D), BF16), pltpu.VMEM((2, FFC, D), BF16),
                        pltpu.VMEM((TM, D), BF16), pltpu.VMEM((TM, D), F32), pltpu.SemaphoreType.DMA((2 * NDEV + 2,))],
        compiler_params=_params(("arbitrary", "arbitrary"), 56),
    )(dh2, fg, fu, h1, g_ffn, w_gu, w_dn)


def _mix_bwd(dh1, z, s, q, ac, m, b_gate, ln_g, ln_b, pool_scale, g_mixw, g_pool, qs):
    tp = dh1.shape[0]
    nt = tp // TMS
    ex = _ChipExchange(qs)
    nq = ex.n

    def body(*refs):
        dh1_ref, zga, zgb, s_ref, q_ref, ac_ref, m_ref, bg_ref, lg_ref, lb_ref, ps_ref, wm_hbm, wp_hbm = refs[:13]
        dac_ref, dm_ref, dzg_ref, dyc_ref, dyp_ref, dm2_ref, acc_ref = refs[13 + nq:20 + nq]
        wm, wp, sems = refs[20 + 2 * nq:23 + 2 * nq]
        ex.bind(refs[13:13 + nq], refs[20 + nq:20 + 2 * nq], refs[23 + 2 * nq:])
        first = pl.program_id(0) == 0

        @pl.when(first)
        def _():
            ex.issue()
            acc_ref[...] = jnp.zeros_like(acc_ref)

        _load_once(first, [(wm_hbm, wm), (wp_hbm, wp)], sems)

        dmerged = _dot_nt(dh1_ref[...].astype(BF16), wm[:, 2].reshape(D, D))
        ga = _sig(zga[...] + bg_ref[:, :D])
        gb = _sig(zgb[...] + bg_ref[:, D:])
        dyc = dmerged * ga
        dyp = dmerged * gb
        dza = (dmerged * _dot(s_ref[...], wm[:, 0].reshape(D, D))) * (ga * (1.0 - ga))
        dzb = (dmerged * _dot(q_ref[...], wm[:, 1].reshape(D, D))) * (gb * (1.0 - gb))
        dzg_ref[:, :D] = dza.astype(BF16)
        dzg_ref[:, D:] = dzb.astype(BF16)
        acc_ref[0:1, :D] += jnp.sum(dza, axis=0, keepdims=True)
        acc_ref[0:1, D:] += jnp.sum(dzb, axis=0, keepdims=True)
        dyc_b = dyc.astype(BF16)
        dyp_b = dyp.astype(BF16)
        dyc_ref[...] = dyc_b
        dyp_ref[...] = dyp_b
        ds = _dot_nt(dyc_b, wm[:, 0].reshape(D, D))
        n, rl = _ln_stats(ac_ref[...])
        l = n * lg_ref[...] + lb_ref[...]
        sg = _sig(l)
        dl = ds * (sg * (1.0 + l * (1.0 - sg)))
        acc_ref[1:2, :D] += jnp.sum(dl * n, axis=0, keepdims=True)
        acc_ref[1:2, D:] += jnp.sum(dl, axis=0, keepdims=True)
        dn = dl * lg_ref[...]
        dac_ref[...] = rl * (dn - jnp.mean(dn, axis=-1, keepdims=True) - n * jnp.mean(dn * n, axis=-1, keepdims=True))
        dq = _dot_nt(dyp_b, wm[:, 1].reshape(D, D))
        mv = m_ref[...]
        acc_ref[2:3, :D] += jnp.sum(dq * _pool_mix(mv, wp), axis=0, keepdims=True)
        dm2 = (dq * ps_ref[...]).astype(BF16)
        dm2_ref[...] = dm2
        dm_ref[...] = jnp.concatenate(
            [_dot_nt(dm2[:, g * PG:(g + 1) * PG], wp[:, g].reshape(PG, PG)) for g in range(4)], axis=1)

        @pl.when(pl.program_id(0) == nt - 1)
        def _():
            ex.finish()

    def tile(col=0):
        return pl.BlockSpec((TMS, D), lambda i: (i, col))

    def vec(w):
        return pl.BlockSpec((1, w), lambda i: (0, 0))

    anys = pl.BlockSpec(memory_space=pl.ANY)
    f32o, b16o = jax.ShapeDtypeStruct((tp, D), F32), jax.ShapeDtypeStruct((tp, D), BF16)
    res = pl.pallas_call(
        body, name="mix_bwd", grid=(nt,),
        in_specs=[tile(), tile(3), tile(4), tile(), tile(), tile(), tile(), vec(2 * D), vec(D), vec(D), vec(D), anys, anys]
        + [anys] * nq,
        out_specs=[tile(), tile(), pl.BlockSpec((TMS, 2 * D), lambda i: (i, 0)), tile(), tile(), tile(),
                   pl.BlockSpec((8, 2 * D), lambda i: (0, 0))] + [anys] * nq,
        out_shape=[f32o, f32o, jax.ShapeDtypeStruct((tp, 2 * D), BF16), b16o, b16o, b16o,
                   jax.ShapeDtypeStruct((8, 2 * D), F32)] + ex.out_shape,
        scratch_shapes=[pltpu.VMEM((NDEV, 3, D // NDEV, D), BF16), pltpu.VMEM((NDEV, 4, PG // NDEV, PG), BF16),
                        pltpu.SemaphoreType.DMA((2,))] + ex.scratch,
        compiler_params=_params(("arbitrary",), 48),
    )(dh1, z, z, s, q, ac, m, b_gate, ln_g, ln_b, pool_scale, g_mixw, g_pool, *qs)
    return res[:7], res[7:]


def _seq_bwd(dac, dm, dzg, z, w_dw, seq, qs):
    tp = z.shape[0]
    nt = tp // TM
    ex = _ChipExchange(qs)
    nq = no = ex.n

    def body(*refs):
        dac_l, dac_c, dac_r, dm_l, dm_c, dm_r, av_l, av, av_r, ag_l, ag, ag_r, dzg_ref, w_ref = refs[:14]
        dz_ref, acc_ref = refs[14 + nq:16 + nq]
        a3, d3, m3, da3, dp3, w3, dw3, da_sc, dp_sc = refs[16 + nq + no:25 + nq + no]
        ex.bind(refs[14:14 + nq], refs[16 + nq:16 + nq + no], refs[25 + nq + no:])
        i = pl.program_id(0)
        sub = lax.broadcasted_iota(jnp.int32, (NCB, 128), 0)

        @pl.when(i == 0)
        def _():
            ex.issue()
            dw3[...] = jnp.zeros_like(dw3)
            _tm_fill(w3, 0, 4, lambda r, l: w_ref[pl.ds(r, 8), l])

        _tm_fill_ext(a3, (av_l, ag_l), (av, ag), (av_r, ag_r), lambda vg, r, l: vg[0][r, l] * _sig(vg[1][r, l]))
        _tm_fill_ext(d3, dac_l, dac_c, dac_r, lambda ref, r, l: ref[r, l])
        _tm_fill_ext(m3, dm_l, dm_c, dm_r, lambda ref, r, l: ref[r, l])

        def conv(g, c):
            dcur = [_tm_at(d3, 8 * g + t + HALO) for t in range(8)]
            accs = [None] * 8
            for k in range(CONV_K):
                wk = _tm_at(w3, k)
                prs = []
                for t in range(8):
                    term = wk * _tm_at(d3, 8 * g + t + CONV_K - k)
                    accs[t] = term if accs[t] is None else accs[t] + term
                    prs.append(dcur[t] * _tm_at(a3, 8 * g + t + k + 1))
                while len(prs) > 1:
                    prs = [prs[j] + prs[j + 1] for j in range(0, len(prs), 2)]
                dw3[_tm_rows(k), :] += prs[0]
            s = dcur[0]
            for t in range(1, 8):
                s = s + dcur[t]
            dw3[_tm_rows(CONV_K), :] += s
            for t in range(8):
                da3[_tm_rows(8 * g + t), :] = accs[t]
            return c

        lax.fori_loop(0, TM // 8, conv, 0)

        for b in _edge_rows(seq, tp):
            e = lax.rem(b - i * TM + HALO + tp, tp)

            @pl.when(e < TME)
            def _():
                m3[_tm_rows(e), :] = _tm_at(m3, e) * _edge_gain(b, seq, tp, sub)

        inv = _by_group(sub, [1.0 / w for w in POOL_WINDOWS])

        def pool(g, c):
            for t in range(8):
                e = 8 * g + t + HALO
                sums = _nested_windows(lambda o: _tm_at(m3, e + o), [w // 2 + 1 - w for w in POOL_WINDOWS])
                dp3[_tm_rows(8 * g + t), :] = _by_group(sub, sums) * inv
            return c

        lax.fori_loop(0, TM // 8, pool, 0)

        _tm_read(da3, TM // 8, lambda r, l, tile: da_sc.__setitem__((r, l), tile))
        _tm_read(dp3, TM // 8, lambda r, l, tile: dp_sc.__setitem__((r, l), tile))
        sg = _sig(ag[...])
        da = da_sc[...]
        dz_ref[:, 0:D] = (da * sg).astype(BF16)
        dz_ref[:, D:2 * D] = (da * av[...] * (sg * (1.0 - sg))).astype(BF16)
        dz_ref[:, 2 * D:3 * D] = (dp_sc[...] - dm_c[...]).astype(BF16)
        dz_ref[:, 3 * D:] = dzg_ref[...]

        @pl.when(i == nt - 1)
        def _():
            _tm_read(dw3, 4, lambda r, l, tile: acc_ref.__setitem__((r, l), tile))
            ex.finish()

    tmaj = pltpu.VMEM((TM * NCB, 128), F32)
    text = pltpu.VMEM((TME * NCB, 128), F32)
    taps = pltpu.VMEM((32 * NCB, 128), F32)
    anys = pl.BlockSpec(memory_space=pl.ANY)
    res = pl.pallas_call(
        body, name="seq_bwd", grid=(nt,),
        in_specs=_halo_specs(0, nt) + _halo_specs(0, nt) + _halo_specs(0, nt) + _halo_specs(1, nt)
        + [pl.BlockSpec((TM, 2 * D), lambda i: (i, 0)), pl.BlockSpec((32, D), lambda i: (0, 0))] + [anys] * nq,
        out_specs=[pl.BlockSpec((TM, DIN), lambda i: (i, 0)), pl.BlockSpec((32, D), lambda i: (0, 0))] + [anys] * no,
        out_shape=[jax.ShapeDtypeStruct((tp, DIN), BF16), jax.ShapeDtypeStruct((32, D), F32)] + ex.out_shape,
        scratch_shapes=[text, text, text, tmaj, tmaj, taps, taps, pltpu.VMEM((TM, D), F32), pltpu.VMEM((TM, D), F32)]
        + ex.scratch,
        compiler_params=_params(("arbitrary",), 48),
    )(dac, dac, dac, dm, dm, dm, z, z, z, z, z, z, dzg, w_dw, *qs)
    return res[:2], res[2:]


def _in_bwd(dz, h0, dh1, g_mix, w_g, seq, qs):
    tp = h0.shape[0]
    tm = _pick(tp, TM_IO)
    nt = tp // tm
    ex = _ChipExchange(qs)
    nq = no = ex.n

    def body(*refs):
        dz_ref, h_ref, dh1_ref, g_ref, w_hbm = refs[:5]
        gx_ref, gmeta_ref, acc_ref = refs[5 + nq:8 + nq]
        w_vm, sems = refs[8 + nq + no:10 + nq + no]
        ex.bind(refs[5:5 + nq], refs[8 + nq:8 + nq + no], refs[10 + nq + no:])
        i = pl.program_id(0)

        @pl.when(i == 0)
        def _():
            ex.issue()
            acc_ref[...] = jnp.zeros_like(acc_ref)

        _load_once(i == 0, _win_pairs(w_hbm, w_vm), sems)

        du = _dot_nt(dz_ref[:, :DIN // 2], w_vm[0]) + _dot_nt(dz_ref[:, DIN // 2:], w_vm[1])
        h = h_ref[...]
        r = lax.rsqrt(jnp.mean(h * h, axis=-1, keepdims=True) + RMS_EPS)
        n0 = h * r
        acc_ref[0:1, :] += jnp.sum(du * n0, axis=0, keepdims=True)
        dn = du * g_ref[...]
        gx_ref[...] = dh1_ref[...] + r * (dn - n0 * jnp.mean(dn * n0, axis=-1, keepdims=True))

        @pl.when(i == nt - 1)
        def _():
            gmeta_ref[...] = gx_ref[pl.ds(tm - N_META, N_META), :]
            ex.finish()

    tile = pl.BlockSpec((tm, D), lambda i: (i, 0))
    anys = pl.BlockSpec(memory_space=pl.ANY)
    res = pl.pallas_call(
        body, name="in_bwd", grid=(nt,),
        in_specs=[pl.BlockSpec((tm, DIN), lambda i: (i, 0)), tile, tile, pl.BlockSpec((1, D), lambda i: (0, 0)), anys]
        + [anys] * nq,
        out_specs=[tile, pl.BlockSpec((N_META, D), lambda i: (0, 0)), pl.BlockSpec((8, D), lambda i: (0, 0))] + [anys] * no,
        out_shape=[jax.ShapeDtypeStruct((seq, D), F32), jax.ShapeDtypeStruct((N_META, D), F32),
                   jax.ShapeDtypeStruct((8, D), F32)] + ex.out_shape,
        scratch_shapes=[pltpu.VMEM((2, D, DIN // 2), BF16), pltpu.SemaphoreType.DMA((NDEV,))] + ex.scratch,
        compiler_params=_params(("arbitrary",), 58),
    )(dz, h0, dh1, g_mix, w_g, *qs)
    return res[:3], res[3:]


def _wgrad_in(u, dz):
    tp = u.shape[0]
    tm = _pick(tp, TM_WG)
    nt = tp // tm
    half = DIN // 2

    def body(u_ref, dz_ref, o_ref, acc):
        t = pl.program_id(1)

        @pl.when(t == 0)
        def _():
            acc[...] = jnp.zeros_like(acc)

        acc[...] += _dot_tn(u_ref[...], dz_ref[...])

        @pl.when(t == nt - 1)
        def _():
            for d in range(4):
                o_ref[d] = acc[:, INB * d:INB * (d + 1)].astype(BF16)

    return pl.pallas_call(
        body, name="wgrad_in", grid=(2, nt),
        in_specs=[pl.BlockSpec((tm, D), lambda h, t: (t, 0)), pl.BlockSpec((tm, half), lambda h, t: (t, h))],
        out_specs=pl.BlockSpec((4, D, INB), lambda h, t: (h, 0, 0), pipeline_mode=pl.Buffered(1)),
        out_shape=jax.ShapeDtypeStruct((NDEV, D, INB), BF16),
        scratch_shapes=[pltpu.VMEM((D, half), F32)],
        compiler_params=_params(("arbitrary", "arbitrary"), 52),
    )(u, dz)


def _wgrad_mix(s, dyc, q, dyp, merged, dh1, m, dm2, qs):
    tp = s.shape[0]
    tm = _pick(tp, TM_WM)
    nt = tp // tm
    rb = D // NDEV
    ex = _ChipExchange(qs)
    nq = ex.n

    def body(*refs):
        s_ref, dyc_ref, q_ref, dyp_ref, mg_ref, dh1_ref, m_ref, dm2_ref = refs[:8]
        o_ref, op_ref = refs[8 + nq:10 + nq]
        acc, accp = refs[10 + 2 * nq:12 + 2 * nq]
        ex.bind(refs[8:8 + nq], refs[10 + nq:10 + 2 * nq], refs[12 + 2 * nq:])
        t = pl.program_id(0)

        @pl.when(t == 0)
        def _():
            ex.issue()
            acc[...] = jnp.zeros_like(acc)
            accp[...] = jnp.zeros_like(accp)

        acc[0] += _dot_tn(s_ref[...], dyc_ref[...])
        acc[1] += _dot_tn(q_ref[...], dyp_ref[...])
        acc[2] += _dot_tn(mg_ref[...], dh1_ref[...].astype(BF16))
        for g in range(4):
            accp[g] += _dot_tn(m_ref[:, g * PG:(g + 1) * PG], dm2_ref[:, g * PG:(g + 1) * PG])

        @pl.when(t == nt - 1)
        def _():
            for d in range(NDEV):
                for k in range(3):
                    o_ref[d, k] = acc[k, rb * d:rb * (d + 1), :].astype(BF16)
                for g in range(4):
                    op_ref[d, g] = accp[g, 32 * d:32 * (d + 1), :].astype(BF16)
            ex.finish()

    tile = pl.BlockSpec((tm, D), lambda t: (t, 0))
    anys = pl.BlockSpec(memory_space=pl.ANY)
    res = pl.pallas_call(
        body, name="wgrad_mix", grid=(nt,),
        in_specs=[tile] * 8 + [anys] * nq,
        out_specs=[pl.BlockSpec((NDEV, 3, rb, D), lambda t: (0, 0, 0, 0), pipeline_mode=pl.Buffered(1)),
                   pl.BlockSpec((NDEV, 4, 32, PG), lambda t: (0, 0, 0, 0), pipeline_mode=pl.Buffered(1))] + [anys] * nq,
        out_shape=[jax.ShapeDtypeStruct((NDEV, 3, rb, D), BF16), jax.ShapeDtypeStruct((NDEV, 4, 32, PG), BF16)]
        + ex.out_shape,
        scratch_shapes=[pltpu.VMEM((3, D, D), F32), pltpu.VMEM((4, PG, PG), F32)] + ex.scratch,
        compiler_params=_params(("arbitrary",), 56),
    )(s, dyc, q, dyp, merged, dh1, m, dm2, *qs)
    return res[:2], res[2:]


def _wgrad_gu(v, dfg, dfu):
    tp = v.shape[0]
    tm = _pick(tp, TM_WG)
    nt = tp // tm

    def body(v_ref, dg_ref, du_ref, o_ref, acc):
        k, t = pl.program_id(0), pl.program_id(2)

        @pl.when(t == 0)
        def _():
            acc[...] = jnp.zeros_like(acc)

        @pl.when(k == 0)
        def _():
            acc[...] += _dot_tn(dg_ref[...], v_ref[...])

        @pl.when(k == 1)
        def _():
            acc[...] += _dot_tn(du_ref[...], v_ref[...])

        @pl.when(t == nt - 1)
        def _():
            for d in range(4):
                o_ref[d] = acc[FFB * d:FFB * (d + 1), :].astype(BF16)

    return pl.pallas_call(
        body, name="wgrad_gu", grid=(2, 2, nt),
        in_specs=[pl.BlockSpec((tm, D), lambda k, h, t: (t, 0)),
                  pl.BlockSpec((tm, FFC), lambda k, h, t: (t * (1 - k), h * (1 - k))),
                  pl.BlockSpec((tm, FFC), lambda k, h, t: (t * k, h * k))],
        out_specs=pl.BlockSpec((4, None, FFB, D), lambda k, h, t: (h, k, 0, 0), pipeline_mode=pl.Buffered(1)),
        out_shape=jax.ShapeDtypeStruct((NDEV, 2, FFB, D), BF16),
        scratch_shapes=[pltpu.VMEM((FFC, D), F32)],
        compiler_params=_params(("arbitrary",) * 3, 48),
    )(v, dfg, dfu)


def _wgrad_down(f, dh2):
    tp = f.shape[0]
    tm = _pick(tp, TM_WG)
    nt = tp // tm

    def body(f_ref, d_ref, o_ref, acc):
        t = pl.program_id(1)

        @pl.when(t == 0)
        def _():
            acc[...] = jnp.zeros_like(acc)

        acc[...] += _dot_tn(f_ref[...], d_ref[...].astype(BF16))

        @pl.when(t == nt - 1)
        def _():
            for d in range(4):
                o_ref[d] = acc[FFB * d:FFB * (d + 1), :].astype(BF16)

    return pl.pallas_call(
        body, name="wgrad_down", grid=(2, nt),
        in_specs=[pl.BlockSpec((tm, FFC), lambda h, t: (t, h)), pl.BlockSpec((tm, D), lambda h, t: (t, 0))],
        out_specs=pl.BlockSpec((4, FFB, D), lambda h, t: (h, 0, 0), pipeline_mode=pl.Buffered(1)),
        out_shape=jax.ShapeDtypeStruct((NDEV, FFB, D), BF16),
        scratch_shapes=[pltpu.VMEM((FFC, D), F32)],
        compiler_params=_params(("arbitrary", "arbitrary"), 48),
    )(f, dh2)


def kernel(x, meta_tokens, g_mix, w_in, b_gate, w_dw, b_dw, ln_g, ln_b, w_conv_out, w_pool, pool_scale, w_pool_out, w_o, g_ffn, w_ffn_gate, w_ffn_up, w_ffn_down, g_final, loss_target, m_meta_tokens, m_g_mix, m_w_in, m_b_gate, m_w_dw, m_b_dw, m_ln_g, m_ln_b, m_w_conv_out, m_w_pool, m_pool_scale, m_w_pool_out, m_w_o, m_g_ffn, m_w_ffn_gate, m_w_ffn_up, m_w_ffn_down, m_g_final, v_meta_tokens, v_g_mix, v_w_in, v_b_gate, v_w_dw, v_b_dw, v_ln_g, v_ln_b, v_w_conv_out, v_w_pool, v_pool_scale, v_w_pool_out, v_w_o, v_g_ffn, v_w_ffn_gate, v_w_ffn_up, v_w_ffn_down, v_g_final):
    seq = x.shape[1]
    tp = -(-(seq + 2 * HALO) // TM) * TM
    tm_in = _pick(tp, TM_IO)
    nx_last = seq - (tp // tm_in - 1) * tm_in
    assert 0 < nx_last <= tm_in - 2 * HALO and nx_last % 8 == 0 and 0 < seq - (tp // TM - 1) * TM

    whole = (Ellipsis,)
    ag_small = _Gather(
        [((48, D // NDEV), [(meta_tokens, pl.ds(0, N_META), whole), (w_dw, pl.ds(N_META, CONV_K), 0)])], [F32])
    ag_mix = _Gather([((3, D // NDEV, D), [(w_conv_out, 0, 0), (w_pool_out, 1, 0), (w_o, 2, 0)]),
                      ((4, PG // NDEV, PG), [(w_pool, whole, 0)])], [BF16, BF16])
    def tr(a):
        return jnp.swapaxes(a, 1, 2)

    ag_gu = _Gather([((2, FFB, D), [(tr(w_ffn_gate), 0, 0), (tr(w_ffn_up), 1, 0)])], [BF16])
    ag_dn = _Gather([((FFB, D), [(w_ffn_down, whole, 0)])], [BF16])

    mx, my = lax.axis_index("x"), lax.axis_index("y")
    order = jnp.stack([2 * mx + my, 2 * mx + 1 - my, 2 * (1 - mx) + my, 2 * (1 - mx) + 1 - my]).astype(jnp.int32)
    (h0, z, u, g_in), (g_mixw, g_pool), g_small = _fwd_in(x[0], g_mix, w_in, order, tp, ag_mix, ag_small)
    wdw_full = g_small.transpose(1, 0, 2).reshape(48, D)[N_META:]
    (ac, m), (w_gu,) = _seq_fwd(z, wdw_full, b_dw, seq, ag_gu)
    (h1, s, merged, q), (g_down,) = _mix_fwd(ac, m, z, h0, b_gate, ln_g, ln_b, pool_scale, g_mixw, g_pool, ag_dn)
    w_dn = g_down.reshape(2, FFC, D)
    fg, fu, v, f, dh2, head_acc = _ffn_fwd(h1, loss_target[0], g_ffn, g_final.reshape(1, D), w_gu, w_dn)

    dfg, dfu, dh1, ffn_acc = _ffn_bwd(dh2, fg, fu, h1, g_ffn, w_gu, w_dn)
    own_f, sib_f, q_f = _rs_pair("rs_pair_ffn", [_wgrad_gu(v, dfg, dfu), _wgrad_down(f, dh2)])
    (dac, dm, dzg, dyc, dyp, dm2, mix_acc), rel_gu = _mix_bwd(
        dh1, z, s, q, ac, m, b_gate, ln_g, ln_b, pool_scale, g_mixw, g_pool, q_f[:1])
    p_mix, rel_dn = _wgrad_mix(s, dyc, q, dyp, merged, dh1, m, dm2, q_f[1:])
    rel_f = [rel_gu[0], rel_dn[0]]
    own_m, sib_m, q_m = _rs_pair("rs_pair_mix", list(p_mix))
    (dz, seq_acc), rel_m = _seq_bwd(dac, dm, dzg, z, wdw_full, seq, q_m)
    own_i, sib_i, q_i = _rs_pair("rs_pair_in", [_wgrad_in(u, dz)])
    (grad_x, g_meta, in_acc), rel_i = _in_bwd(dz, h0, dh1, g_mix, g_in, seq, q_i)
    small_g = jnp.concatenate([g_meta, seq_acc[:CONV_K], jnp.zeros((1, D), F32)], axis=0)
    p_small = small_g.reshape(48, NDEV, D // NDEV).transpose(1, 0, 2).astype(BF16)
    rep_g = jnp.concatenate([
        in_acc[0:1], mix_acc[0:1, :D], mix_acc[0:1, D:], seq_acc[CONV_K:CONV_K + 1], mix_acc[1:2, :D], mix_acc[1:2, D:],
        mix_acc[2:3, :D], ffn_acc[0:1], head_acc[1:2], head_acc[0:1], jnp.zeros((REP_ROWS - 10, D), F32)], axis=0)
    own_s, sib_s, rel_s, rep_all = _reduce_scatter([p_small], rep_g)
    owns = [own_i[0], own_s[0], own_m[0], own_m[1], own_f[0], own_f[1]]
    sibs = [sib_i[0], sib_s[0], sib_m[0], sib_m[1], sib_f[0], sib_f[1]]
    rels = [rel_i[0], rel_s[0], rel_m[0], rel_m[1], rel_f[0], rel_f[1]]

    def lead(a):
        return a.reshape(1, *a.shape)

    def stack4(a, lead_dims):
        return a.reshape(*lead_dims, 1, 4 * 32, PG)

    (r_in,) = _adamw_multi("adamw_in", lead(owns[0]), sibs[0][:, None], rels[0][:, None], [w_in], [m_w_in], [v_w_in], 4)
    r_meta, r_dw = _adamw_meta_dw(owns[1], sibs[1], rels[1], (meta_tokens, m_meta_tokens, v_meta_tokens),
                                  (w_dw, m_w_dw, v_w_dw))
    r_conv, r_pout, r_o = _adamw_multi("adamw_mix", owns[2], sibs[2], rels[2], [w_conv_out, w_pool_out, w_o],
                                       [m_w_conv_out, m_w_pool_out, m_w_o], [v_w_conv_out, v_w_pool_out, v_w_o], 1)
    (r_pool,) = _adamw_multi("adamw_pool", stack4(owns[3], ()), stack4(sibs[3], (1,)), stack4(rels[3], (3,)),
                             [w_pool.reshape(1, 128, PG)], [m_w_pool.reshape(1, 128, PG)], [v_w_pool.reshape(1, 128, PG)], 1)
    r_pool = tuple(a.reshape(w_pool.shape) for a in r_pool)
    r_gate, r_up = _adamw_multi("adamw_gu", owns[4], sibs[4], rels[4], [tr(w_ffn_gate), tr(w_ffn_up)],
                                [tr(m_w_ffn_gate), tr(m_w_ffn_up)], [tr(v_w_ffn_gate), tr(v_w_ffn_up)], 2)
    r_gate, r_up = tuple(tr(a) for a in r_gate), tuple(tr(a) for a in r_up)
    (r_down,) = _adamw_multi("adamw_down", lead(owns[5]), sibs[5][:, None], rels[5][:, None],
                             [w_ffn_down], [m_w_ffn_down], [v_w_ffn_down], 2)
    row = (1, D)
    loss, reps = _adamw_rep(
        rep_all,
        [g_mix, b_gate, b_dw, ln_g, ln_b, pool_scale, g_ffn, g_final.reshape(row)],
        [m_g_mix, m_b_gate, m_b_dw, m_ln_g, m_ln_b, m_pool_scale, m_g_ffn, m_g_final.reshape(row)],
        [v_g_mix, v_b_gate, v_b_dw, v_ln_g, v_ln_b, v_pool_scale, v_g_ffn, v_g_final.reshape(row)])
    r_gmix, r_bg, r_bdw, r_lg, r_lb, r_ps, r_gffn, r_gfin = reps
    r_gfin = tuple(a.reshape(D) for a in r_gfin)

    in_order = [r_meta, r_gmix, r_in, r_bg, r_dw, r_bdw, r_lg, r_lb, r_conv, r_pool, r_ps, r_pout, r_o, r_gffn,
                r_gate, r_up, r_down, r_gfin]
    return (loss.reshape(()), grad_x[None], *[r[0] for r in in_order], *[r[1] for r in in_order],
            *[r[2] for r in in_order], *[r[3] for r in in_order])
```

```python
import math

import jax
import jax.numpy as jnp
from jax import lax
from jax.experimental import pallas as pl
from jax.experimental.pallas import tpu as pltpu

F32, BF16 = jnp.float32, jnp.bfloat16
MESH_ID = pl.DeviceIdType.MESH
NDEV = 8

D = 1024
N_META = 16
CONV_K = 31
HALO = 16
POOL_WINDOWS = (2, 4, 8, 16)
PG = 256
DIN = 5 * D
DFF = 2816
FFB = DFF // NDEV
FFC = DFF // 2
INB = DIN // NDEV
RMS_EPS = 1e-6
LN_EPS = 1e-5
ADAM_LR, ADAM_B1, ADAM_B2, ADAM_EPS, ADAM_WD, ADAM_STEP = 0.001, 0.9, 0.999, 1e-08, 0.01, 10

TM = 384
TMS = 384
TM_IO = 704
TM_WG = 1408
TM_WM = 704
MIB = 2 ** 20


def _sig(x):
    return 0.5 * jnp.tanh(0.5 * x) + 0.5


def _dot(a, b):
    return jnp.dot(a, b, preferred_element_type=F32)


def _dot_nt(a, b):
    return lax.dot_general(a, b, (((1,), (1,)), ((), ())), preferred_element_type=F32)


def _dot_tn(a, b):
    return lax.dot_general(a, b, (((0,), (0,)), ((), ())), preferred_element_type=F32)


def _pick(tp, pref):
    return pref if tp % pref == 0 else TM


def _params(sem, vmem_mib):
    return pltpu.CompilerParams(dimension_semantics=sem, vmem_limit_bytes=vmem_mib * MIB)


def _load_once(first, pairs, sems):
    @pl.when(first)
    def _():
        cps = [pltpu.make_async_copy(s, d, sems.at[k]) for k, (s, d) in enumerate(pairs)]
        for cp in cps:
            cp.start()
        for cp in cps:
            cp.wait()


def _place():
    x, y, c = lax.axis_index("x"), lax.axis_index("y"), lax.axis_index("c")
    return x, y, c


class _Gather:
    def __init__(self, groups, dtypes):
        self.groups, self.dtypes, self.n = groups, dtypes, len(groups)
        self.arrays = [a for _, parts in groups for a, _, _ in parts]
        self.out_shape = [jax.ShapeDtypeStruct((NDEV, *s), dt) for (s, _), dt in zip(groups, dtypes)]
        self.scratch = [pltpu.VMEM(s, dt) for (s, _), dt in zip(groups, dtypes)] + [
            pltpu.SemaphoreType.DMA((7 * self.n,)), pltpu.SemaphoreType.DMA((7 * self.n,)),
            pltpu.SemaphoreType.DMA((self.n,))]

    def bind(self, ins, outs, scratch):
        self.ins, self.outs, self.stages = ins, outs, scratch[:self.n]
        self.send_sems, self.recv_sems, self.local_sems = scratch[self.n:]
        return self

    def _copy(self, w, k, block, to, src=None):
        dst = self.outs[w].at[4 * block[0] + 2 * block[1] + block[2]]
        return pltpu.make_async_remote_copy(
            src_ref=dst if src is None else src, dst_ref=dst,
            send_sem=self.send_sems.at[7 * w + k], recv_sem=self.recv_sems.at[7 * w + k],
            device_id=to, device_id_type=MESH_ID)

    def _first(self):
        x, y, c = _place()
        me, sibling = (x, y, c), (x, y, 1 - c)
        chips = [(1 - x, y), (x, 1 - y), (1 - x, 1 - y)]
        mine, first = [], []
        for w in range(self.n):
            mine.append(pltpu.make_async_copy(self.stages[w], self.outs[w].at[4 * x + 2 * y + c], self.local_sems.at[w]))
            first.append(self._copy(w, 0, me, sibling, src=self.stages[w]))
            first += [self._copy(w, 1 + j, me, (*chip, c), src=self.stages[w]) for j, chip in enumerate(chips)]
        return mine, first

    def _passed(self):
        x, y, c = _place()
        chips = [(1 - x, y), (x, 1 - y), (1 - x, 1 - y)]
        return [self._copy(w, 4 + j, (*chip, c), (x, y, 1 - c)) for w in range(self.n) for j, chip in enumerate(chips)]

    def issue(self):
        a = 0
        for w in range(self.n):
            shape, parts = self.groups[w]
            if sum(arr.size for arr, _, _ in parts) < math.prod(shape):
                self.stages[w][...] = jnp.zeros(shape, self.dtypes[w])
            for _, dst, src in parts:
                self.stages[w][dst] = self.ins[a][src].astype(self.dtypes[w])
                a += 1
        mine, first = self._first()
        for cp in mine + first:
            cp.start()

    def forward(self):
        x, y, c = _place()
        chips = [(1 - x, y), (x, 1 - y), (1 - x, 1 - y)]
        passed = self._passed()
        for w in range(self.n):
            for j, chip in enumerate(chips):
                self._copy(w, 1 + j, (*chip, c), (x, y, c)).wait_recv()
                passed[3 * w + j].start()

    def finish(self):
        x, y, c = _place()
        chips = [(1 - x, y), (x, 1 - y), (1 - x, 1 - y)]
        for w in range(self.n):
            self._copy(w, 0, (x, y, 1 - c), (x, y, c)).wait_recv()
            for j, chip in enumerate(chips):
                self._copy(w, 4 + j, (*chip, 1 - c), (x, y, c)).wait_recv()
        mine, first = self._first()
        for cp in first + self._passed():
            cp.wait_send()
        for cp in mine:
            cp.wait()


class _ChipExchange:
    def __init__(self, qs):
        self.n = len(qs)
        self.out_shape = [jax.ShapeDtypeStruct(q.shape, q.dtype) for q in qs]
        self.scratch = [pltpu.SemaphoreType.DMA((3 * self.n,)), pltpu.SemaphoreType.DMA((3 * self.n,))]

    def bind(self, qs, rels, scratch):
        self.qs, self.rels = qs, rels
        self.send_sems, self.recv_sems = scratch
        return self

    def _copies(self):
        x, y, c = _place()
        chips = [(1 - x, y), (x, 1 - y), (1 - x, 1 - y)]
        return [pltpu.make_async_remote_copy(
            src_ref=self.qs[w].at[j], dst_ref=self.rels[w].at[j],
            send_sem=self.send_sems.at[3 * w + j], recv_sem=self.recv_sems.at[3 * w + j],
            device_id=(*chips[j], c), device_id_type=MESH_ID) for w in range(self.n) for j in range(3)]

    def issue(self):
        for cp in self._copies():
            cp.start()

    def finish(self):
        cps = self._copies()
        for cp in cps:
            cp.wait_recv()
        for cp in cps:
            cp.wait_send()


def _reduce_scatter(parts, small):
    n = len(parts)
    blks = [p.shape[1:] for p in parts]

    def body(*refs):
        ps, small_ref = refs[:n], refs[n]
        o = n + 1
        owns, sibs, rels, small_out = refs[o:o + n], refs[o + n:o + 2 * n], refs[o + 2 * n:o + 3 * n], refs[o + 3 * n]
        o += 3 * n + 1
        pa, pb, qst = refs[o:o + n], refs[o + n:o + 2 * n], refs[o + 2 * n:o + 3 * n]
        s1_send, s1_recv, s2_send, s2_recv, sm_send, sm_recv, lsem = refs[o + 3 * n:]
        x, y, c = _place()
        me = 4 * x + 2 * y + c
        sibling = (x, y, 1 - c)
        chips = [(1 - x, y), (x, 1 - y), (1 - x, 1 - y)]
        all_chips = [(x, y)] + chips

        own_cps = []
        for w in range(n):
            cp = pltpu.make_async_copy(ps[w].at[me], owns[w], lsem.at[w])
            cp.start()
            own_cps.append(cp)
        sm_own = pltpu.make_async_copy(small_ref, small_out.at[me], lsem.at[n])
        sm_own.start()

        def small_copy(r):
            peer = ((x + (r >> 2)) % 2, (y + ((r >> 1) & 1)) % 2, (c + (r & 1)) % 2)
            return pltpu.make_async_remote_copy(
                src_ref=small_ref, dst_ref=small_out.at[me], send_sem=sm_send.at[r - 1], recv_sem=sm_recv.at[r - 1],
                device_id=peer, device_id_type=MESH_ID)

        sm_cps = [small_copy(r) for r in range(1, NDEV)]
        for cp in sm_cps:
            cp.start()

        def pair_copy(w, rel):
            cx, cy = all_chips[rel]
            return pltpu.make_async_remote_copy(
                src_ref=ps[w].at[4 * cx + 2 * cy + (1 - c)], dst_ref=sibs[w].at[rel],
                send_sem=s1_send.at[4 * w + rel], recv_sem=s1_recv.at[4 * w + rel],
                device_id=sibling, device_id_type=MESH_ID)

        def chip_copy(w, j):
            return pltpu.make_async_remote_copy(
                src_ref=qst[w].at[j], dst_ref=rels[w].at[j],
                send_sem=s2_send.at[3 * w + j], recv_sem=s2_recv.at[3 * w + j],
                device_id=(*chips[j], c), device_id_type=MESH_ID)

        pair_cps = [pair_copy(w, rel) for w in range(n) for rel in (1, 2, 3, 0)]
        for cp in pair_cps:
            cp.start()
        chip_cps = []
        for w in range(n):
            for j, (cx, cy) in enumerate(chips):
                pair_copy(w, 1 + j).wait_recv()
                la = pltpu.make_async_copy(ps[w].at[4 * cx + 2 * cy + c], pa[w], lsem.at[n + 1])
                lb = pltpu.make_async_copy(sibs[w].at[1 + j], pb[w], lsem.at[n + 2])
                la.start()
                lb.start()
                la.wait()
                lb.wait()
                qst[w][j] = (pa[w][...].astype(F32) + pb[w][...].astype(F32)).astype(BF16)
                cp = chip_copy(w, j)
                cp.start()
                chip_cps.append(cp)
        for w in range(n):
            pair_copy(w, 0).wait_recv()
            for j in range(3):
                chip_copy(w, j).wait_recv()
        for cp in sm_cps:
            cp.wait_recv()
        for cp in pair_cps + chip_cps + sm_cps:
            cp.wait_send()
        for cp in own_cps:
            cp.wait()
        sm_own.wait()

    any_spec = pl.BlockSpec(memory_space=pl.ANY)
    outs = pl.pallas_call(
        body, name="rs_grads",
        out_shape=[jax.ShapeDtypeStruct(b, BF16) for b in blks]
        + [jax.ShapeDtypeStruct((4, *b), BF16) for b in blks]
        + [jax.ShapeDtypeStruct((3, *b), BF16) for b in blks]
        + [jax.ShapeDtypeStruct((NDEV, *small.shape), F32)],
        in_specs=[any_spec] * (n + 1),
        out_specs=[any_spec] * (3 * n + 1),
        scratch_shapes=[pltpu.VMEM(b, BF16) for b in blks] + [pltpu.VMEM(b, BF16) for b in blks]
        + [pltpu.VMEM((3, *b), BF16) for b in blks]
        + [pltpu.SemaphoreType.DMA((4 * n,)), pltpu.SemaphoreType.DMA((4 * n,)),
           pltpu.SemaphoreType.DMA((3 * n,)), pltpu.SemaphoreType.DMA((3 * n,)),
           pltpu.SemaphoreType.DMA((NDEV - 1,)), pltpu.SemaphoreType.DMA((NDEV - 1,)),
           pltpu.SemaphoreType.DMA((n + 3,))],
        compiler_params=pltpu.CompilerParams(vmem_limit_bytes=40 * MIB),
    )(*parts, small)
    return outs[:n], outs[n:2 * n], outs[2 * n:3 * n], outs[3 * n]


class _PairSum:
    def __init__(self, parts, keep_q=True):
        self.n = n = len(parts)
        self.keep_q = keep_q
        blks = [p.shape[1:] for p in parts]
        self.out_shape = [jax.ShapeDtypeStruct(b, BF16) for b in blks] + [jax.ShapeDtypeStruct((1, *b), BF16) for b in blks]
        if keep_q:
            self.out_shape += [jax.ShapeDtypeStruct((3, *b), BF16) for b in blks]
        self.scratch = [pltpu.VMEM((3, *b), BF16) for b in blks] * 3 + [
            pltpu.SemaphoreType.DMA((4 * n,)), pltpu.SemaphoreType.DMA((4 * n,)), pltpu.SemaphoreType.DMA((5 * n,))]

    def bind(self, ps, outs, scratch):
        n = self.n
        self.ps, self.owns, self.sibs, self.qs = ps, outs[:n], outs[n:2 * n], outs[2 * n:]
        self.pa, self.pb, self.qst = scratch[:n], scratch[n:2 * n], scratch[2 * n:3 * n]
        self.s_send, self.s_recv, self.lsem = scratch[3 * n:]
        return self

    def _local(self, with_q):
        n = self.n
        x, y, c = _place()
        chips = [(1 - x, y), (x, 1 - y), (1 - x, 1 - y)]
        own = [pltpu.make_async_copy(self.ps[w].at[4 * x + 2 * y + c], self.owns[w], self.lsem.at[w]) for w in range(n)]
        mine = [[pltpu.make_async_copy(self.ps[w].at[4 * cx + 2 * cy + c], self.pa[w].at[j], self.lsem.at[2 * n + 3 * w + j])
                 for j, (cx, cy) in enumerate(chips)] for w in range(n)]
        outq = [pltpu.make_async_copy(self.qst[w], self.qs[w], self.lsem.at[n + w]) for w in range(n)] if with_q else []
        return own, mine, outq

    def _pair(self, w, rel):
        x, y, c = _place()
        cx, cy = [(x, y), (1 - x, y), (x, 1 - y), (1 - x, 1 - y)][rel]
        return pltpu.make_async_remote_copy(
            src_ref=self.ps[w].at[4 * cx + 2 * cy + (1 - c)],
            dst_ref=self.sibs[w].at[0] if rel == 0 else self.pb[w].at[rel - 1],
            send_sem=self.s_send.at[4 * w + rel], recv_sem=self.s_recv.at[4 * w + rel],
            device_id=(x, y, 1 - c), device_id_type=MESH_ID)

    def issue(self):
        own, mine, _ = self._local(False)
        for cp in own + [cp for row in mine for cp in row]:
            cp.start()
        for w in range(self.n):
            for rel in (1, 2, 3, 0):
                self._pair(w, rel).start()

    def finish(self):
        own, mine, outq = self._local(self.keep_q)
        for w in range(self.n):
            for j in range(3):
                self._pair(w, 1 + j).wait_recv()
                mine[w][j].wait()
                self.qst[w][j] = (self.pa[w][j].astype(F32) + self.pb[w][j].astype(F32)).astype(BF16)
            if self.keep_q:
                outq[w].start()
        for w in range(self.n):
            self._pair(w, 0).wait_recv()
        for w in range(self.n):
            for rel in range(4):
                self._pair(w, rel).wait_send()
        for cp in own + outq:
            cp.wait()

    def results(self, outs):
        n = self.n
        return outs[:n], outs[n:2 * n], outs[2 * n:3 * n]


def _rs_pair(name, parts):
    ps = _PairSum(parts)
    n = ps.n

    def body(*refs):
        ps.bind(refs[:n], refs[n:4 * n], refs[4 * n:])
        ps.issue()
        ps.finish()

    any_spec = pl.BlockSpec(memory_space=pl.ANY)
    outs = pl.pallas_call(
        body, name=name, out_shape=ps.out_shape,
        in_specs=[any_spec] * n, out_specs=[any_spec] * (3 * n), scratch_shapes=ps.scratch,
        compiler_params=pltpu.CompilerParams(vmem_limit_bytes=48 * MIB),
    )(*parts)
    return ps.results(outs)


def _adamw_math(g, w, m, v):
    m = ADAM_B1 * m + (1.0 - ADAM_B1) * g
    v = ADAM_B2 * v + (1.0 - ADAM_B2) * (g * g)
    m_hat = m / (1.0 - ADAM_B1 ** ADAM_STEP)
    v_hat = v / (1.0 - ADAM_B2 ** ADAM_STEP)
    delta = -ADAM_LR * (m_hat / (jnp.sqrt(v_hat) + ADAM_EPS) + ADAM_WD * w)
    return delta, m, v


def _adamw_multi(name, own, sib, rel, ws, ms, vs, row_grid):
    k_n, r_n, c_n = own.shape
    rbk = r_n // row_grid

    def body(*refs):
        own_ref, sib_ref, r0_ref, r1_ref, r2_ref = refs[:5]
        w_refs, m_refs, v_refs = refs[5:5 + k_n], refs[5 + k_n:5 + 2 * k_n], refs[5 + 2 * k_n:5 + 3 * k_n]
        outs = refs[5 + 3 * k_n:]
        for k in range(k_n):
            g = own_ref[k].astype(F32) + sib_ref[k].astype(F32)
            g = g + r0_ref[k].astype(F32)
            g = g + r1_ref[k].astype(F32)
            g = g + r2_ref[k].astype(F32)
            delta, mm, vv = _adamw_math(g, w_refs[k][0], m_refs[k][0], v_refs[k][0])
            outs[4 * k][0] = g
            outs[4 * k + 1][0] = delta
            outs[4 * k + 2][0] = mm
            outs[4 * k + 3][0] = vv

    def lead(j):
        return pl.BlockSpec((None, k_n, rbk, c_n), lambda g: (j, 0, g, 0))

    wspec = pl.BlockSpec((1, rbk, c_n), lambda g: (0, g, 0))
    shp = jax.ShapeDtypeStruct((1, r_n, c_n), F32)
    res = pl.pallas_call(
        body, name=name, grid=(row_grid,),
        in_specs=[pl.BlockSpec((k_n, rbk, c_n), lambda g: (0, g, 0)), lead(0), lead(0), lead(1), lead(2)] + [wspec] * (3 * k_n),
        out_specs=[wspec] * (4 * k_n), out_shape=[shp] * (4 * k_n),
        compiler_params=_params(("arbitrary",), 40),
    )(own, sib, rel, rel, rel, *ws, *ms, *vs)
    return [tuple(res[4 * k:4 * k + 4]) for k in range(k_n)]


def _adamw_meta_dw(own, sib, rel, meta, dw):
    def body(own_ref, sib_ref, rel_ref, wm, mm, vm, wd, md, vd, *outs):
        def gsum(rows):
            g = own_ref[rows, :].astype(F32) + sib_ref[0, rows, :].astype(F32)
            for j in range(3):
                g = g + rel_ref[j, rows, :].astype(F32)
            return g

        g = gsum(pl.ds(0, N_META))
        delta, m2, v2 = _adamw_math(g, wm[...], mm[...], vm[...])
        for o, val in zip(outs[:4], (g, delta, m2, v2)):
            o[...] = val
        g = gsum(pl.ds(N_META, CONV_K))
        delta, m2, v2 = _adamw_math(g, wd[0], md[0], vd[0])
        for o, val in zip(outs[4:], (g, delta, m2, v2)):
            o[0] = val

    s_meta = jax.ShapeDtypeStruct(meta[0].shape, F32)
    s_dw = jax.ShapeDtypeStruct(dw[0].shape, F32)
    res = pl.pallas_call(body, name="adamw_meta_dw", out_shape=[s_meta] * 4 + [s_dw] * 4)(own, sib, rel, *meta, *dw)
    return tuple(res[:4]), tuple(res[4:])


REP_ROWS = 16


def _adamw_rep(gathered, ws, ms, vs):
    rows = [(0, 1), (1, 2), (3, 1), (4, 1), (5, 1), (6, 1), (7, 1), (8, 1)]

    def body(g_ref, *refs):
        w_refs, m_refs, v_refs = refs[:8], refs[8:16], refs[16:24]
        loss_ref, outs, acc = refs[24], refs[25:57], refs[57]
        g = g_ref[0]
        for d in range(1, NDEV):
            g = g + g_ref[d]
        acc[...] = g
        loss_ref[...] = (0.5 / D) * jnp.sum(acc[pl.ds(9, 1), :], axis=1, keepdims=True)
        for p, (r0, nr) in enumerate(rows):
            for h in range(nr):
                cols = pl.ds(h * D, D)
                gp = acc[pl.ds(r0 + h, 1), :]
                delta, mm, vv = _adamw_math(gp, w_refs[p][:, cols], m_refs[p][:, cols], v_refs[p][:, cols])
                for o, val in zip(outs[4 * p:4 * p + 4], (gp, delta, mm, vv)):
                    o[:, cols] = val

    shapes = [jax.ShapeDtypeStruct(w.shape, F32) for w in ws]
    res = pl.pallas_call(
        body, name="adamw_rep",
        out_shape=[jax.ShapeDtypeStruct((1, 1), F32)] + [s for s in shapes for _ in range(4)],
        scratch_shapes=[pltpu.VMEM((REP_ROWS, D), F32)],
    )(gathered, *ws, *ms, *vs)
    return res[0], [tuple(res[1 + 4 * p:5 + 4 * p]) for p in range(8)]


def _load_ffn(i, j, wgu_hbm, wgu, wdn_hbm, wdn, sems):
    half = NDEV // 2

    def copies(ch):
        pairs = [(wgu_hbm.at[half * ch + d, g], wgu.at[g, ch, pl.ds(FFB * d, FFB), :]) for g in range(2) for d in range(half)]
        pairs.append((wdn_hbm.at[ch], wdn.at[ch]))
        return [pltpu.make_async_copy(s, t, sems.at[(2 * half + 1) * ch + k]) for k, (s, t) in enumerate(pairs)]

    @pl.when((i == 0) & (j == 0))
    def _():
        for cp in copies(0) + copies(1):
            cp.start()

    for ch in range(2):
        @pl.when((i == 0) & (j == ch))
        def _():
            for cp in copies(ch):
                cp.wait()


def _win_pairs(w_hbm, w_vm):
    return [(w_hbm.at[q], w_vm.at[q // 2, :, pl.ds(2 * INB * (q % 2), 2 * INB)]) for q in range(4)]


def _whole(a):
    nd = a.ndim
    return pl.BlockSpec(a.shape, lambda *g: (0,) * nd)


CHIPW = 2 * INB
PHASE_CHIP = (1, 0, 2)


class _GatherIn:
    scratch = [pltpu.VMEM((D, INB), BF16), pltpu.SemaphoreType.DMA((7,)), pltpu.SemaphoreType.DMA((7,)),
               pltpu.SemaphoreType.DMA((1,))]

    def bind(self, w_ref, w_vm, scratch):
        self.w_ref, self.w_vm = w_ref, w_vm
        self.stage, self.send_sems, self.recv_sems, self.local_sem = scratch
        return self

    def _win(self, chip, core):
        return self.w_vm.at[2 * chip[0] + chip[1], :, pl.ds(INB * core, INB)]

    def _copy(self, k, chip, core, to, src=None):
        dst = self._win(chip, core)
        return pltpu.make_async_remote_copy(
            src_ref=dst if src is None else src, dst_ref=dst, send_sem=self.send_sems.at[k],
            recv_sem=self.recv_sems.at[k], device_id=to, device_id_type=MESH_ID)

    def _mine(self, cs):
        x, y, _ = _place()
        return pltpu.make_async_copy(self.stage, self._win((x, y), cs), self.local_sem.at[0])

    def issue(self, cs):
        x, y, _ = _place()
        chips = [(1 - x, y), (x, 1 - y), (1 - x, 1 - y)]
        self.stage[...] = self.w_ref[0].astype(BF16)
        self._mine(cs).start()
        self._copy(0, (x, y), cs, (x, y, 1 - cs), src=self.stage).start()
        for j, chip in enumerate(chips):
            self._copy(1 + j, (x, y), cs, (*chip, cs), src=self.stage).start()

    def wait_chip(self, phase, cs):
        x, y, _ = _place()
        chips = [(1 - x, y), (x, 1 - y), (1 - x, 1 - y)]
        if phase == 0:
            self._mine(cs).wait()
            self._copy(0, (x, y), 1 - cs, (x, y, cs)).wait_recv()
            return
        if phase == 1:
            for j in PHASE_CHIP:
                self._copy(1 + j, chips[j], cs, (x, y, cs)).wait_recv()
                self._copy(4 + j, chips[j], cs, (x, y, 1 - cs)).start()
        j = PHASE_CHIP[phase - 1]
        self._copy(4 + j, chips[j], 1 - cs, (x, y, cs)).wait_recv()

    def finish(self, cs):
        x, y, _ = _place()
        for k in range(7):
            self._copy(k, (x, y), cs, (x, y, cs), src=self.stage).wait_send()


def _fwd_in(x2, g_mix, w_in, order, tp, ag, ags):
    tm = _pick(tp, TM_IO)
    nt = tp // tm
    nx_last = x2.shape[0] - (nt - 1) * tm
    na, ng, ns = len(ag.arrays), ag.n, len(ags.arrays)
    gin = _GatherIn()

    def body(order_ref, *refs):
        x_ref, g_ref, w_ref = refs[:3]
        o = 3 + na + ns
        h_ref, z_ref, u_ref, wout_ref = refs[o:o + 4]
        s = o + 4 + ng + 1
        w_vm, u_all, osem, sm_vm = refs[s:s + 4]
        gin.bind(w_ref, w_vm, refs[s + 4:s + 8])
        ag.bind(refs[3:3 + na], refs[o + 4:o + 4 + ng], refs[s + 8:s + 8 + len(ag.scratch)])
        ags.bind(refs[3 + na:3 + na + ns], refs[o + 4 + ng:o + 5 + ng], refs[s + 8 + len(ag.scratch):])
        ph, i = pl.program_id(0), pl.program_id(1)
        core = lax.axis_index("c")
        first = (ph == 0) & (i == 0)
        last = (ph == 3) & (i == nt - 1)
        for cs in range(2):
            @pl.when(first & (core == cs))
            def _():
                gin.issue(cs)

        @pl.when(first)
        def _():
            ags.issue()
            ag.issue()

        @pl.when((ph == 0) & (i == max(nt - 2, 0)))
        def _():
            ags.forward()

        for cs in range(2):
            for p in range(4):
                @pl.when((ph == p) & (i == 0) & (core == cs))
                def _():
                    gin.wait_chip(p, cs)

        @pl.when((ph == 3) & (i == max(nt - 2, 0)))
        def _():
            ag.forward()

        out_copy = pltpu.make_async_copy(w_vm, wout_ref, osem.at[0])

        @pl.when((ph == 3) & (i == 0))
        def _():
            out_copy.start()

        @pl.when((ph == 0) & (i < nt - 1))
        def _():
            h_ref[...] = x_ref[...]

        @pl.when((ph == 0) & (i == nt - 1))
        def _():
            ags.finish()
            cp = pltpu.make_async_copy(ags.outs[0], sm_vm, osem.at[1])
            cp.start()
            h_ref[pl.ds(0, nx_last), :] = x_ref[pl.ds(0, nx_last), :]
            h_ref[pl.ds(nx_last, tm - nx_last - N_META), :] = jnp.zeros((tm - nx_last - N_META, D), F32)
            cp.wait()
            for d in range(NDEV):
                h_ref[pl.ds(tm - N_META, N_META), pl.ds(128 * d, 128)] = sm_vm[d, pl.ds(0, N_META), :]

        @pl.when(ph == 0)
        def _():
            xv = h_ref[...]
            r = lax.rsqrt(jnp.mean(xv * xv, axis=-1, keepdims=True) + RMS_EPS)
            u = (xv * r * g_ref[...]).astype(BF16)
            u_ref[...] = u
            u_all[i] = u

        z_ref[...] = _dot(u_all[i], w_vm[order_ref[ph]])

        @pl.when(last)
        def _():
            ag.finish()
            out_copy.wait()

        for cs in range(2):
            @pl.when(last & (core == cs))
            def _():
                gin.finish(cs)

    def rows(ph, i, order):
        return (jnp.where(ph == 0, i, nt - 1), 0)

    tile = pl.BlockSpec((tm, D), rows)
    anys = pl.BlockSpec(memory_space=pl.ANY)
    res = pl.pallas_call(
        body, name="fwd_in",
        grid_spec=pltpu.PrefetchScalarGridSpec(
            num_scalar_prefetch=1, grid=(4, nt),
            in_specs=[tile, pl.BlockSpec((1, D), lambda ph, i, order: (0, 0)), _whole(w_in)]
            + [_whole(a) for a in ag.arrays + ags.arrays],
            out_specs=[tile, pl.BlockSpec((tm, CHIPW), lambda ph, i, order: (i, order[ph])), tile, anys] + [anys] * (ng + 1),
            scratch_shapes=[pltpu.VMEM((4, D, CHIPW), BF16), pltpu.VMEM((nt, tm, D), BF16), pltpu.SemaphoreType.DMA((2,)),
                            pltpu.VMEM(ags.out_shape[0].shape, F32)] + gin.scratch + ag.scratch + ags.scratch),
        out_shape=[jax.ShapeDtypeStruct((tp, D), F32), jax.ShapeDtypeStruct((tp, DIN), F32),
                   jax.ShapeDtypeStruct((tp, D), BF16), jax.ShapeDtypeStruct((4, D, CHIPW), BF16)]
        + ag.out_shape + ags.out_shape,
        compiler_params=_params(("arbitrary", "arbitrary"), 58),
    )(order, x2, g_mix, w_in, *ag.arrays, *ags.arrays)
    return res[:4], res[4:4 + ng], res[4 + ng]


def _halo_specs(col, nt, width=D):
    r = TM // HALO
    nb = nt * r
    return [pl.BlockSpec((HALO, width), lambda i: ((i * r + nb - 1) % nb, col)),
            pl.BlockSpec((TM, width), lambda i: (i, col)),
            pl.BlockSpec((HALO, width), lambda i: (((i + 1) * r) % nb, col))]


NCB = D // 128
TME = TM + 2 * HALO


def _tm_fill(dst, time0, groups, tile_fn):
    def body(g, c):
        for j in range(NCB):
            dst[pl.ds((time0 + 8 * g) * NCB + j, 8, stride=NCB), :] = tile_fn(pl.multiple_of(8 * g, 8), pl.ds(128 * j, 128))
        return c

    lax.fori_loop(0, groups, body, 0)


def _tm_fill_ext(dst, left, cur, right, fn):
    _tm_fill(dst, 0, HALO // 8, lambda r, l: fn(left, pl.ds(r, 8), l))
    _tm_fill(dst, HALO, TM // 8, lambda r, l: fn(cur, pl.ds(r, 8), l))
    _tm_fill(dst, HALO + TM, HALO // 8, lambda r, l: fn(right, pl.ds(r, 8), l))


def _tm_read(src, groups, store_fn):
    def body(g, c):
        for j in range(NCB):
            store_fn(pl.ds(pl.multiple_of(8 * g, 8), 8), pl.ds(128 * j, 128), src[pl.ds(8 * g * NCB + j, 8, stride=NCB), :])
        return c

    lax.fori_loop(0, groups, body, 0)


def _tm_rows(t):
    return pl.ds(t * NCB if isinstance(t, int) else pl.multiple_of(t * NCB, NCB), NCB)


def _tm_at(ref, t):
    return ref[_tm_rows(t), :]


def _by_group(sub, vals):
    return jnp.where(sub < 2, vals[0], jnp.where(sub < 4, vals[1], jnp.where(sub < 6, vals[2], vals[3])))


def _pool_cnt(b, seq, tp, sub):
    b = jnp.where(b < 0, b + tp, b)
    b = jnp.where(b >= tp, b - tp, b)
    t = jnp.where(b < seq, b + N_META, b - (tp - N_META))
    cnts = []
    for win in POOL_WINDOWS:
        left = win // 2
        lo = jnp.maximum(t - left, 0)
        hi = jnp.minimum(t + win - left, seq + N_META)
        cnts.append(jnp.maximum(hi - lo, 1).astype(F32))
    return _by_group(sub, cnts)


def _edge_rows(seq, tp):
    reach = max(POOL_WINDOWS) // 2
    return [tp - N_META + t for t in range(reach)] + [seq - reach + 1 + t for t in range(reach - 1)]


def _edge_gain(b, seq, tp, sub):
    return _by_group(sub, [float(w) for w in POOL_WINDOWS]) / _pool_cnt(b, seq, tp, sub)


def _nested_windows(at, lo_offs):
    sums, s, have = [], None, set()
    for g, win in enumerate(POOL_WINDOWS):
        for o in range(lo_offs[g], lo_offs[g] + win):
            if o not in have:
                have.add(o)
                s = at(o) if s is None else s + at(o)
        sums.append(s)
    return sums


def _seq_fwd(z, w_dw, b_dw, b_gate, seq, gat):
    tp = z.shape[0]
    nt = tp // TM
    na, ng = len(gat.arrays), gat.n

    def body(*refs):
        av_l, av, av_r, ag_l, ag, ag_r, p_l, p, p_r, w_ref, b_ref, zga, zgb, bg_ref = refs[:14]
        ac_ref, m_ref, gates_ref = refs[14 + na:17 + na]
        a3, p3, o3, m3, w3, b3, m2d = refs[17 + na + ng:24 + na + ng]
        gat.bind(refs[14:14 + na], refs[17 + na:17 + na + ng], refs[24 + na + ng:])
        i = pl.program_id(0)
        sub = lax.broadcasted_iota(jnp.int32, (NCB, 128), 0)
        gates_ref[:, :D] = _sig(zga[...] + bg_ref[:, :D]).astype(BF16)
        gates_ref[:, D:] = _sig(zgb[...] + bg_ref[:, D:]).astype(BF16)

        @pl.when(i == 0)
        def _():
            gat.issue()
            _tm_fill(w3, 0, 4, lambda r, l: w_ref[pl.ds(r, 8), l])
            for j in range(NCB):
                b3[pl.ds(j, 1), :] = b_ref[:, pl.ds(128 * j, 128)]

        @pl.when(i == max(nt - 2, 0))
        def _():
            gat.forward()

        _tm_fill_ext(a3, (av_l, ag_l), (av, ag), (av_r, ag_r), lambda vg, r, l: vg[0][r, l] * _sig(vg[1][r, l]))
        _tm_fill_ext(p3, p_l, p, p_r, lambda ref, r, l: ref[r, l])

        def conv(g, c):
            accs = [b3[...]] * 16
            for k in range(CONV_K):
                wk = _tm_at(w3, k)
                for t in range(16):
                    accs[t] = accs[t] + wk * _tm_at(a3, 16 * g + t + k + 1)
            for t in range(16):
                o3[_tm_rows(16 * g + t), :] = accs[t]
            return c

        lax.fori_loop(0, TM // 16, conv, 0)
        _tm_read(o3, TM // 8, lambda r, l, tile: ac_ref.__setitem__((r, l), tile))

        inv = _by_group(sub, [1.0 / w for w in POOL_WINDOWS])

        def pool(g, c):
            for t in range(8):
                e = 8 * g + t + HALO
                sums = _nested_windows(lambda o: _tm_at(p3, e + o), [-(w // 2) for w in POOL_WINDOWS])
                m3[_tm_rows(8 * g + t), :] = _by_group(sub, sums) * inv - _tm_at(p3, e)
            return c

        lax.fori_loop(0, TM // 8, pool, 0)
        for b in _edge_rows(seq, tp):
            r = b - i * TM

            @pl.when((r >= 0) & (r < TM))
            def _():
                pv = _tm_at(p3, r + HALO)
                m3[_tm_rows(r), :] = (_tm_at(m3, r) + pv) * _edge_gain(b, seq, tp, sub) - pv

        _tm_read(m3, TM // 8, lambda r, l, tile: m2d.__setitem__((r, l), tile))
        m_ref[...] = m2d[...].astype(BF16)

        @pl.when(i == nt - 1)
        def _():
            gat.finish()

    tmaj = pltpu.VMEM((TM * NCB, 128), F32)
    text = pltpu.VMEM((TME * NCB, 128), F32)
    res = pl.pallas_call(
        body, name="seq_fwd", grid=(nt,),
        in_specs=_halo_specs(0, nt) + _halo_specs(1, nt) + _halo_specs(2, nt)
        + [pl.BlockSpec((32, D), lambda i: (0, 0)), pl.BlockSpec((1, D), lambda i: (0, 0)),
           pl.BlockSpec((TM, D), lambda i: (i, 3)), pl.BlockSpec((TM, D), lambda i: (i, 4)),
           pl.BlockSpec((1, 2 * D), lambda i: (0, 0))] + [_whole(a) for a in gat.arrays],
        out_specs=[pl.BlockSpec((TM, D), lambda i: (i, 0))] * 2 + [pl.BlockSpec((TM, 2 * D), lambda i: (i, 0))]
        + [pl.BlockSpec(memory_space=pl.ANY)] * ng,
        out_shape=[jax.ShapeDtypeStruct((tp, D), F32), jax.ShapeDtypeStruct((tp, D), BF16),
                   jax.ShapeDtypeStruct((tp, 2 * D), BF16)] + gat.out_shape,
        scratch_shapes=[text, text, tmaj, tmaj, pltpu.VMEM((32 * NCB, 128), F32), pltpu.VMEM((NCB, 128), F32),
                        pltpu.VMEM((TM, D), F32)] + gat.scratch,
        compiler_params=_params(("arbitrary",), 56),
    )(z, z, z, z, z, z, z, z, z, w_dw, b_dw, z, z, b_gate, *gat.arrays)
    return res[:3], res[3:]


def _ln_stats(ac):
    mu = jnp.mean(ac, axis=-1, keepdims=True)
    xc = ac - mu
    rl = lax.rsqrt(jnp.mean(xc * xc, axis=-1, keepdims=True) + LN_EPS)
    return xc * rl, rl


def _pool_mix(m, wp_ref):
    return jnp.concatenate(
        [_dot(m[:, g * PG:(g + 1) * PG], wp_ref[:, g].reshape(PG, PG)) for g in range(4)], axis=1)


def _mix_fwd(ac, m, gates, h0, ln_g, ln_b, pool_scale, g_mixw, g_pool, gat):
    tp = h0.shape[0]
    tms = TM
    nt = tp // tms
    na, ng = len(gat.arrays), gat.n

    def body(*refs):
        ac_ref, m_ref, gt_ref, h_ref, lg_ref, lb_ref, ps_ref, wm_hbm, wp_hbm = refs[:9]
        h1_ref, s_ref, mg_ref, q_ref = refs[9 + na:13 + na]
        wm, wp, sems = refs[13 + na + ng:16 + na + ng]
        gat.bind(refs[9:9 + na], refs[13 + na:13 + na + ng], refs[16 + na + ng:])
        i = pl.program_id(0)

        @pl.when(i == 0)
        def _():
            gat.issue()

        @pl.when(i == max(nt - 4, 0))
        def _():
            gat.forward()

        @pl.when(i == nt - 1)
        def _():
            gat.finish()

        _load_once(i == 0, [(wm_hbm, wm), (wp_hbm, wp)], sems)
        n, _ = _ln_stats(ac_ref[...])
        l = n * lg_ref[...] + lb_ref[...]
        s = (l * _sig(l)).astype(BF16)
        s_ref[...] = s
        yc = _dot(s, wm[:, 0].reshape(D, D))
        q = (_pool_mix(m_ref[...], wp) * ps_ref[...]).astype(BF16)
        q_ref[...] = q
        yp = _dot(q, wm[:, 1].reshape(D, D))
        merged = (gt_ref[:, :D].astype(F32) * yc + gt_ref[:, D:].astype(F32) * yp).astype(BF16)
        mg_ref[...] = merged
        h1_ref[...] = h_ref[...] + _dot(merged, wm[:, 2].reshape(D, D))

    def tile(col=0):
        return pl.BlockSpec((tms, D), lambda i: (i, col))

    def vec(w):
        return pl.BlockSpec((1, w), lambda i: (0, 0))

    anys = pl.BlockSpec(memory_space=pl.ANY)
    f32o, b16o = jax.ShapeDtypeStruct((tp, D), F32), jax.ShapeDtypeStruct((tp, D), BF16)
    res = pl.pallas_call(
        body, name="mix_fwd", grid=(nt,),
        in_specs=[tile(), tile(), pl.BlockSpec((tms, 2 * D), lambda i: (i, 0)), tile(), vec(D), vec(D), vec(D), anys, anys]
        + [_whole(a) for a in gat.arrays],
        out_specs=[tile()] * 4 + [anys] * ng,
        out_shape=[f32o, b16o, b16o, b16o] + gat.out_shape,
        scratch_shapes=[pltpu.VMEM((NDEV, 3, D // NDEV, D), BF16), pltpu.VMEM((NDEV, 4, PG // NDEV, PG), BF16),
                        pltpu.SemaphoreType.DMA((2,))] + gat.scratch,
        compiler_params=_params(("arbitrary",), 52),
    )(ac, m, gates, h0, ln_g, ln_b, pool_scale, g_mixw, g_pool, *gat.arrays)
    return res[:4], res[4:]


def _ffn_fwd(h1, tgt, g_ffn, g_final, w_gu, w_dn):
    tp = h1.shape[0]
    nt = tp // TM
    nx_last = tgt.shape[0] - (nt - 1) * TM

    def body(h_ref, t_ref, gf_ref, gl_ref, wgu_hbm, wdn_hbm,
             fg_ref, fu_ref, v_ref, f_ref, dh2_ref, acc_ref, wgu, wdn, v_sc, h2_sc, diff_sc, sems):
        i, j = pl.program_id(0), pl.program_id(1)
        _load_ffn(i, j, wgu_hbm, wgu, wdn_hbm, wdn, sems)

        @pl.when((i == 0) & (j == 0))
        def _():
            acc_ref[...] = jnp.zeros_like(acc_ref)

        @pl.when(j == 0)
        def _():
            h = h_ref[...]
            r = lax.rsqrt(jnp.mean(h * h, axis=-1, keepdims=True) + RMS_EPS)
            v = (h * r * gf_ref[...]).astype(BF16)
            v_sc[...] = v
            v_ref[...] = v
            h2_sc[...] = h

        v = v_sc[...]
        fg = _dot_nt(v, wgu[0, j])
        fu = _dot_nt(v, wgu[1, j])
        fg_ref[...] = fg
        fu_ref[...] = fu
        f = ((fg * _sig(fg)) * fu).astype(BF16)
        f_ref[...] = f
        h2_sc[...] += _dot(f, wdn[j])

        @pl.when(j == 1)
        def _():
            h2 = h2_sc[...]
            r = lax.rsqrt(jnp.mean(h2 * h2, axis=-1, keepdims=True) + RMS_EPS)
            n2 = h2 * r
            y = n2 * gl_ref[...]

            @pl.when(i < nt - 1)
            def _():
                diff_sc[...] = y - t_ref[...]

            @pl.when(i == nt - 1)
            def _():
                diff_sc[pl.ds(0, nx_last), :] = y[:nx_last] - t_ref[pl.ds(0, nx_last), :]
                diff_sc[pl.ds(nx_last, TM - nx_last), :] = jnp.zeros((TM - nx_last, D), F32)

            diff = diff_sc[...]
            dy = diff * (1.0 / D)
            acc_ref[0:1, :] += jnp.sum(diff * diff, axis=0, keepdims=True)
            acc_ref[1:2, :] += jnp.sum(dy * n2, axis=0, keepdims=True)
            dn = dy * gl_ref[...]
            dh2_ref[...] = r * (dn - n2 * jnp.mean(dn * n2, axis=-1, keepdims=True))

    def tile():
        return pl.BlockSpec((TM, D), lambda i, j: (i, 0))

    def chunk():
        return pl.BlockSpec((TM, FFC), lambda i, j: (i, j))

    def vec():
        return pl.BlockSpec((1, D), lambda i, j: (0, 0))

    anys = pl.BlockSpec(memory_space=pl.ANY)
    hid32, hid16 = jax.ShapeDtypeStruct((tp, DFF), F32), jax.ShapeDtypeStruct((tp, DFF), BF16)
    return pl.pallas_call(
        body, name="ffn_fwd", grid=(nt, 2),
        in_specs=[tile(), tile(), vec(), vec(), anys, anys],
        out_specs=[chunk(), chunk(), tile(), chunk(), tile(), pl.BlockSpec((8, D), lambda i, j: (0, 0))],
        out_shape=[hid32, hid32, jax.ShapeDtypeStruct((tp, D), BF16), hid16, jax.ShapeDtypeStruct((tp, D), F32),
                   jax.ShapeDtypeStruct((8, D), F32)],
        scratch_shapes=[pltpu.VMEM((2, 2, FFC, D), BF16), pltpu.VMEM((2, FFC, D), BF16),
                        pltpu.VMEM((TM, D), BF16), pltpu.VMEM((TM, D), F32), pltpu.VMEM((TM, D), F32),
                        pltpu.SemaphoreType.DMA((2 * NDEV + 2,))],
        compiler_params=_params(("arbitrary", "arbitrary"), 56),
    )(h1, tgt, g_ffn, g_final, w_gu, w_dn)


def _ffn_bwd(dh2, fg, fu, h1, g_ffn, w_gu, w_dn):
    tp = h1.shape[0]
    nt = tp // TM

    def body(dh2_ref, fg_ref, fu_ref, h_ref, gf_ref, wgu_hbm, wdn_hbm,
             dfg_ref, dfu_ref, dh1_ref, acc_ref, wgu, wdn, d_sc, dv_sc, sems):
        i, j = pl.program_id(0), pl.program_id(1)
        _load_ffn(i, j, wgu_hbm, wgu, wdn_hbm, wdn, sems)

        @pl.when((i == 0) & (j == 0))
        def _():
            acc_ref[...] = jnp.zeros_like(acc_ref)

        @pl.when(j == 0)
        def _():
            d_sc[...] = dh2_ref[...].astype(BF16)
            dv_sc[...] = jnp.zeros_like(dv_sc)

        df = _dot_nt(d_sc[...], wdn[j])
        fg = fg_ref[...]
        sg = _sig(fg)
        dfu = (df * (fg * sg)).astype(BF16)
        dfg = (df * fu_ref[...] * (sg * (1.0 + fg * (1.0 - sg)))).astype(BF16)
        dfg_ref[...] = dfg
        dfu_ref[...] = dfu
        dv_sc[...] += _dot(dfg, wgu[0, j]) + _dot(dfu, wgu[1, j])

        @pl.when(j == 1)
        def _():
            h = h_ref[...]
            r = lax.rsqrt(jnp.mean(h * h, axis=-1, keepdims=True) + RMS_EPS)
            n1 = h * r
            dv = dv_sc[...]
            acc_ref[0:1, :] += jnp.sum(dv * n1, axis=0, keepdims=True)
            dn = dv * gf_ref[...]
            dh1_ref[...] = dh2_ref[...] + r * (dn - n1 * jnp.mean(dn * n1, axis=-1, keepdims=True))

    def tile():
        return pl.BlockSpec((TM, D), lambda i, j: (i, 0))

    def chunk():
        return pl.BlockSpec((TM, FFC), lambda i, j: (i, j))

    anys = pl.BlockSpec(memory_space=pl.ANY)
    hid16 = jax.ShapeDtypeStruct((tp, DFF), BF16)
    return pl.pallas_call(
        body, name="ffn_bwd", grid=(nt, 2),
        in_specs=[tile(), chunk(), chunk(), tile(), pl.BlockSpec((1, D), lambda i, j: (0, 0)), anys, anys],
        out_specs=[chunk(), chunk(), tile(), pl.BlockSpec((8, D), lambda i, j: (0, 0))],
        out_shape=[hid16, hid16, jax.ShapeDtypeStruct((tp, D), F32), jax.ShapeDtypeStruct((8, D), F32)],
        scratch_shapes=[pltpu.VMEM((2, 2, FFC, D), BF16), pltpu.VMEM((2, FFC, D), BF16),
                        pltpu.VMEM((TM, D), BF16), pltpu.VMEM((TM, D), F32), pltpu.SemaphoreType.DMA((2 * NDEV + 2,))],
        compiler_params=_params(("arbitrary", "arbitrary"), 56),
    )(dh2, fg, fu, h1, g_ffn, w_gu, w_dn)


def _mix_bwd(dh1, gates, s, q, ac, m, ln_g, ln_b, pool_scale, g_mixw, g_pool, qs):
    tp = dh1.shape[0]
    nt = tp // TMS
    ex = _ChipExchange(qs)
    nq = ex.n

    def body(*refs):
        dh1_ref, gt_ref, s_ref, q_ref, ac_ref, m_ref, lg_ref, lb_ref, ps_ref, wm_hbm, wp_hbm = refs[:11]
        dac_ref, dm_ref, dzg_ref, dyc_ref, dyp_ref, dm2_ref, acc_ref = refs[11 + nq:18 + nq]
        wm, wp, sems = refs[18 + 2 * nq:21 + 2 * nq]
        ex.bind(refs[11:11 + nq], refs[18 + nq:18 + 2 * nq], refs[21 + 2 * nq:])
        first = pl.program_id(0) == 0

        @pl.when(first)
        def _():
            ex.issue()
            acc_ref[...] = jnp.zeros_like(acc_ref)

        _load_once(first, [(wm_hbm, wm), (wp_hbm, wp)], sems)

        dmerged = _dot_nt(dh1_ref[...].astype(BF16), wm[:, 2].reshape(D, D))
        ga = gt_ref[:, :D].astype(F32)
        gb = gt_ref[:, D:].astype(F32)
        dyc = dmerged * ga
        dyp = dmerged * gb
        dza = (dmerged * _dot(s_ref[...], wm[:, 0].reshape(D, D))) * (ga * (1.0 - ga))
        dzb = (dmerged * _dot(q_ref[...], wm[:, 1].reshape(D, D))) * (gb * (1.0 - gb))
        dzg_ref[:, :D] = dza.astype(BF16)
        dzg_ref[:, D:] = dzb.astype(BF16)
        acc_ref[0:1, :D] += jnp.sum(dza, axis=0, keepdims=True)
        acc_ref[0:1, D:] += jnp.sum(dzb, axis=0, keepdims=True)
        dyc_b = dyc.astype(BF16)
        dyp_b = dyp.astype(BF16)
        dyc_ref[...] = dyc_b
        dyp_ref[...] = dyp_b
        ds = _dot_nt(dyc_b, wm[:, 0].reshape(D, D))
        n, rl = _ln_stats(ac_ref[...])
        l = n * lg_ref[...] + lb_ref[...]
        sg = _sig(l)
        dl = ds * (sg * (1.0 + l * (1.0 - sg)))
        acc_ref[1:2, :D] += jnp.sum(dl * n, axis=0, keepdims=True)
        acc_ref[1:2, D:] += jnp.sum(dl, axis=0, keepdims=True)
        dn = dl * lg_ref[...]
        dac_ref[...] = rl * (dn - jnp.mean(dn, axis=-1, keepdims=True) - n * jnp.mean(dn * n, axis=-1, keepdims=True))
        dq = _dot_nt(dyp_b, wm[:, 1].reshape(D, D))
        mv = m_ref[...]
        acc_ref[2:3, :D] += jnp.sum(dq * _pool_mix(mv, wp), axis=0, keepdims=True)
        dm2 = (dq * ps_ref[...]).astype(BF16)
        dm2_ref[...] = dm2
        dm_ref[...] = jnp.concatenate(
            [_dot_nt(dm2[:, g * PG:(g + 1) * PG], wp[:, g].reshape(PG, PG)) for g in range(4)], axis=1)

        @pl.when(pl.program_id(0) == nt - 1)
        def _():
            ex.finish()

    def tile(col=0):
        return pl.BlockSpec((TMS, D), lambda i: (i, col))

    def vec(w):
        return pl.BlockSpec((1, w), lambda i: (0, 0))

    anys = pl.BlockSpec(memory_space=pl.ANY)
    f32o, b16o = jax.ShapeDtypeStruct((tp, D), F32), jax.ShapeDtypeStruct((tp, D), BF16)
    res = pl.pallas_call(
        body, name="mix_bwd", grid=(nt,),
        in_specs=[tile(), pl.BlockSpec((TMS, 2 * D), lambda i: (i, 0)), tile(), tile(), tile(), tile(), vec(D), vec(D), vec(D),
                  anys, anys] + [anys] * nq,
        out_specs=[tile(), tile(), pl.BlockSpec((TMS, 2 * D), lambda i: (i, 0)), tile(), tile(), tile(),
                   pl.BlockSpec((8, 2 * D), lambda i: (0, 0))] + [anys] * nq,
        out_shape=[f32o, f32o, jax.ShapeDtypeStruct((tp, 2 * D), BF16), b16o, b16o, b16o,
                   jax.ShapeDtypeStruct((8, 2 * D), F32)] + ex.out_shape,
        scratch_shapes=[pltpu.VMEM((NDEV, 3, D // NDEV, D), BF16), pltpu.VMEM((NDEV, 4, PG // NDEV, PG), BF16),
                        pltpu.SemaphoreType.DMA((2,))] + ex.scratch,
        compiler_params=_params(("arbitrary",), 48),
    )(dh1, gates, s, q, ac, m, ln_g, ln_b, pool_scale, g_mixw, g_pool, *qs)
    return res[:7], res[7:]


def _seq_bwd(dac, dm, dzg, z, w_dw, seq, qs):
    tp = z.shape[0]
    nt = tp // TM
    ex = _ChipExchange(qs)
    nq = no = ex.n

    def body(*refs):
        dac_l, dac_c, dac_r, dm_l, dm_c, dm_r, av_l, av, av_r, ag_l, ag, ag_r, dzg_ref, w_ref = refs[:14]
        dz_ref, acc_ref = refs[14 + nq:16 + nq]
        a3, d3, m3, da3, dp3, w3, dw3, da_sc, dp_sc = refs[16 + nq + no:25 + nq + no]
        ex.bind(refs[14:14 + nq], refs[16 + nq:16 + nq + no], refs[25 + nq + no:])
        i = pl.program_id(0)
        sub = lax.broadcasted_iota(jnp.int32, (NCB, 128), 0)

        @pl.when(i == 0)
        def _():
            ex.issue()
            dw3[...] = jnp.zeros_like(dw3)
            _tm_fill(w3, 0, 4, lambda r, l: w_ref[pl.ds(r, 8), l])

        _tm_fill_ext(a3, (av_l, ag_l), (av, ag), (av_r, ag_r), lambda vg, r, l: vg[0][r, l] * _sig(vg[1][r, l]))
        _tm_fill_ext(d3, dac_l, dac_c, dac_r, lambda ref, r, l: ref[r, l])
        _tm_fill_ext(m3, dm_l, dm_c, dm_r, lambda ref, r, l: ref[r, l])

        def conv(g, c):
            dcur = [_tm_at(d3, 8 * g + t + HALO) for t in range(8)]
            accs = [None] * 8
            for k in range(CONV_K):
                wk = _tm_at(w3, k)
                prs = []
                for t in range(8):
                    term = wk * _tm_at(d3, 8 * g + t + CONV_K - k)
                    accs[t] = term if accs[t] is None else accs[t] + term
                    prs.append(dcur[t] * _tm_at(a3, 8 * g + t + k + 1))
                while len(prs) > 1:
                    prs = [prs[j] + prs[j + 1] for j in range(0, len(prs), 2)]
                dw3[_tm_rows(k), :] += prs[0]
            s = dcur[0]
            for t in range(1, 8):
                s = s + dcur[t]
            dw3[_tm_rows(CONV_K), :] += s
            for t in range(8):
                da3[_tm_rows(8 * g + t), :] = accs[t]
            return c

        lax.fori_loop(0, TM // 8, conv, 0)

        for b in _edge_rows(seq, tp):
            e = lax.rem(b - i * TM + HALO + tp, tp)

            @pl.when(e < TME)
            def _():
                m3[_tm_rows(e), :] = _tm_at(m3, e) * _edge_gain(b, seq, tp, sub)

        inv = _by_group(sub, [1.0 / w for w in POOL_WINDOWS])

        def pool(g, c):
            for t in range(8):
                e = 8 * g + t + HALO
                sums = _nested_windows(lambda o: _tm_at(m3, e + o), [w // 2 + 1 - w for w in POOL_WINDOWS])
                dp3[_tm_rows(8 * g + t), :] = _by_group(sub, sums) * inv
            return c

        lax.fori_loop(0, TM // 8, pool, 0)

        _tm_read(da3, TM // 8, lambda r, l, tile: da_sc.__setitem__((r, l), tile))
        _tm_read(dp3, TM // 8, lambda r, l, tile: dp_sc.__setitem__((r, l), tile))
        sg = _sig(ag[...])
        da = da_sc[...]
        dz_ref[:, 0:D] = (da * sg).astype(BF16)
        dz_ref[:, D:2 * D] = (da * av[...] * (sg * (1.0 - sg))).astype(BF16)
        dz_ref[:, 2 * D:3 * D] = (dp_sc[...] - dm_c[...]).astype(BF16)
        dz_ref[:, 3 * D:] = dzg_ref[...]

        @pl.when(i == nt - 1)
        def _():
            _tm_read(dw3, 4, lambda r, l, tile: acc_ref.__setitem__((r, l), tile))
            ex.finish()

    tmaj = pltpu.VMEM((TM * NCB, 128), F32)
    text = pltpu.VMEM((TME * NCB, 128), F32)
    taps = pltpu.VMEM((32 * NCB, 128), F32)
    anys = pl.BlockSpec(memory_space=pl.ANY)
    res = pl.pallas_call(
        body, name="seq_bwd", grid=(nt,),
        in_specs=_halo_specs(0, nt) + _halo_specs(0, nt) + _halo_specs(0, nt) + _halo_specs(1, nt)
        + [pl.BlockSpec((TM, 2 * D), lambda i: (i, 0)), pl.BlockSpec((32, D), lambda i: (0, 0))] + [anys] * nq,
        out_specs=[pl.BlockSpec((TM, DIN), lambda i: (i, 0)), pl.BlockSpec((32, D), lambda i: (0, 0))] + [anys] * no,
        out_shape=[jax.ShapeDtypeStruct((tp, DIN), BF16), jax.ShapeDtypeStruct((32, D), F32)] + ex.out_shape,
        scratch_shapes=[text, text, text, tmaj, tmaj, taps, taps, pltpu.VMEM((TM, D), F32), pltpu.VMEM((TM, D), F32)]
        + ex.scratch,
        compiler_params=_params(("arbitrary",), 48),
    )(dac, dac, dac, dm, dm, dm, z, z, z, z, z, z, dzg, w_dw, *qs)
    return res[:2], res[2:]


def _in_bwd(dz, h0, dh1, g_mix, w_g, seq, qs):
    tp = h0.shape[0]
    tm = _pick(tp, TM_IO)
    nt = tp // tm
    ex = _ChipExchange(qs)
    nq = no = ex.n

    def body(*refs):
        dz_ref, h_ref, dh1_ref, g_ref, w_hbm = refs[:5]
        gx_ref, gmeta_ref, acc_ref = refs[5 + nq:8 + nq]
        w_vm, sems = refs[8 + nq + no:10 + nq + no]
        ex.bind(refs[5:5 + nq], refs[8 + nq:8 + nq + no], refs[10 + nq + no:])
        i = pl.program_id(0)

        @pl.when(i == 0)
        def _():
            ex.issue()
            acc_ref[...] = jnp.zeros_like(acc_ref)

        _load_once(i == 0, _win_pairs(w_hbm, w_vm), sems)

        du = _dot_nt(dz_ref[:, :DIN // 2], w_vm[0]) + _dot_nt(dz_ref[:, DIN // 2:], w_vm[1])
        h = h_ref[...]
        r = lax.rsqrt(jnp.mean(h * h, axis=-1, keepdims=True) + RMS_EPS)
        n0 = h * r
        acc_ref[0:1, :] += jnp.sum(du * n0, axis=0, keepdims=True)
        dn = du * g_ref[...]
        gx_ref[...] = dh1_ref[...] + r * (dn - n0 * jnp.mean(dn * n0, axis=-1, keepdims=True))

        @pl.when(i == nt - 1)
        def _():
            gmeta_ref[...] = gx_ref[pl.ds(tm - N_META, N_META), :]
            ex.finish()

    tile = pl.BlockSpec((tm, D), lambda i: (i, 0))
    anys = pl.BlockSpec(memory_space=pl.ANY)
    res = pl.pallas_call(
        body, name="in_bwd", grid=(nt,),
        in_specs=[pl.BlockSpec((tm, DIN), lambda i: (i, 0)), tile, tile, pl.BlockSpec((1, D), lambda i: (0, 0)), anys]
        + [anys] * nq,
        out_specs=[tile, pl.BlockSpec((N_META, D), lambda i: (0, 0)), pl.BlockSpec((8, D), lambda i: (0, 0))] + [anys] * no,
        out_shape=[jax.ShapeDtypeStruct((seq, D), F32), jax.ShapeDtypeStruct((N_META, D), F32),
                   jax.ShapeDtypeStruct((8, D), F32)] + ex.out_shape,
        scratch_shapes=[pltpu.VMEM((2, D, DIN // 2), BF16), pltpu.SemaphoreType.DMA((NDEV,))] + ex.scratch,
        compiler_params=_params(("arbitrary",), 58),
    )(dz, h0, dh1, g_mix, w_g, *qs)
    return res[:3], res[3:]


def _wgrad_in(u, dz):
    tp = u.shape[0]
    tm = _pick(tp, TM_WG)
    nt = tp // tm
    half = DIN // 2

    def body(u_ref, dz_ref, o_ref, acc):
        t = pl.program_id(1)

        @pl.when(t == 0)
        def _():
            acc[...] = jnp.zeros_like(acc)

        acc[...] += _dot_tn(u_ref[...], dz_ref[...])

        @pl.when(t == nt - 1)
        def _():
            for d in range(4):
                o_ref[d] = acc[:, INB * d:INB * (d + 1)].astype(BF16)

    return pl.pallas_call(
        body, name="wgrad_in", grid=(2, nt),
        in_specs=[pl.BlockSpec((tm, D), lambda h, t: (t, 0)), pl.BlockSpec((tm, half), lambda h, t: (t, h))],
        out_specs=pl.BlockSpec((4, D, INB), lambda h, t: (h, 0, 0), pipeline_mode=pl.Buffered(1)),
        out_shape=jax.ShapeDtypeStruct((NDEV, D, INB), BF16),
        scratch_shapes=[pltpu.VMEM((D, half), F32)],
        compiler_params=_params(("arbitrary", "arbitrary"), 52),
    )(u, dz)


def _wgrad_mix(s, dyc, q, dyp, merged, dh1, m, dm2, qs):
    tp = s.shape[0]
    tm = _pick(tp, TM_WM)
    nt = tp // tm
    rb = D // NDEV
    ex = _ChipExchange(qs)
    nq = ex.n

    def body(*refs):
        s_ref, dyc_ref, q_ref, dyp_ref, mg_ref, dh1_ref, m_ref, dm2_ref = refs[:8]
        o_ref, op_ref = refs[8 + nq:10 + nq]
        acc, accp = refs[10 + 2 * nq:12 + 2 * nq]
        ex.bind(refs[8:8 + nq], refs[10 + nq:10 + 2 * nq], refs[12 + 2 * nq:])
        t = pl.program_id(0)

        @pl.when(t == 0)
        def _():
            ex.issue()
            acc[...] = jnp.zeros_like(acc)
            accp[...] = jnp.zeros_like(accp)

        acc[0] += _dot_tn(s_ref[...], dyc_ref[...])
        acc[1] += _dot_tn(q_ref[...], dyp_ref[...])
        acc[2] += _dot_tn(mg_ref[...], dh1_ref[...].astype(BF16))
        for g in range(4):
            accp[g] += _dot_tn(m_ref[:, g * PG:(g + 1) * PG], dm2_ref[:, g * PG:(g + 1) * PG])

        @pl.when(t == nt - 1)
        def _():
            for d in range(NDEV):
                for k in range(3):
                    o_ref[d, k] = acc[k, rb * d:rb * (d + 1), :].astype(BF16)
                for g in range(4):
                    op_ref[d, g] = accp[g, 32 * d:32 * (d + 1), :].astype(BF16)
            ex.finish()

    tile = pl.BlockSpec((tm, D), lambda t: (t, 0))
    anys = pl.BlockSpec(memory_space=pl.ANY)
    res = pl.pallas_call(
        body, name="wgrad_mix", grid=(nt,),
        in_specs=[tile] * 8 + [anys] * nq,
        out_specs=[pl.BlockSpec((NDEV, 3, rb, D), lambda t: (0, 0, 0, 0), pipeline_mode=pl.Buffered(1)),
                   pl.BlockSpec((NDEV, 4, 32, PG), lambda t: (0, 0, 0, 0), pipeline_mode=pl.Buffered(1))] + [anys] * nq,
        out_shape=[jax.ShapeDtypeStruct((NDEV, 3, rb, D), BF16), jax.ShapeDtypeStruct((NDEV, 4, 32, PG), BF16)]
        + ex.out_shape,
        scratch_shapes=[pltpu.VMEM((3, D, D), F32), pltpu.VMEM((4, PG, PG), F32)] + ex.scratch,
        compiler_params=_params(("arbitrary",), 56),
    )(s, dyc, q, dyp, merged, dh1, m, dm2, *qs)
    return res[:2], res[2:]


def _wgrad_gu(v, dfg, dfu):
    tp = v.shape[0]
    tm = _pick(tp, TM_WG)
    nt = tp // tm

    def body(v_ref, dg_ref, du_ref, o_ref, acc):
        k, t = pl.program_id(0), pl.program_id(2)

        @pl.when(t == 0)
        def _():
            acc[...] = jnp.zeros_like(acc)

        @pl.when(k == 0)
        def _():
            acc[...] += _dot_tn(dg_ref[...], v_ref[...])

        @pl.when(k == 1)
        def _():
            acc[...] += _dot_tn(du_ref[...], v_ref[...])

        @pl.when(t == nt - 1)
        def _():
            for d in range(4):
                o_ref[d] = acc[FFB * d:FFB * (d + 1), :].astype(BF16)

    return pl.pallas_call(
        body, name="wgrad_gu", grid=(2, 2, nt),
        in_specs=[pl.BlockSpec((tm, D), lambda k, h, t: (t, 0)),
                  pl.BlockSpec((tm, FFC), lambda k, h, t: (t * (1 - k), h * (1 - k))),
                  pl.BlockSpec((tm, FFC), lambda k, h, t: (t * k, h * k))],
        out_specs=pl.BlockSpec((4, None, FFB, D), lambda k, h, t: (h, k, 0, 0), pipeline_mode=pl.Buffered(1)),
        out_shape=jax.ShapeDtypeStruct((NDEV, 2, FFB, D), BF16),
        scratch_shapes=[pltpu.VMEM((FFC, D), F32)],
        compiler_params=_params(("arbitrary",) * 3, 48),
    )(v, dfg, dfu)


def _wgrad_down(f, dh2):
    tp = f.shape[0]
    tm = _pick(tp, TM_WG)
    nt = tp // tm

    def body(f_ref, d_ref, o_ref, acc):
        t = pl.program_id(1)

        @pl.when(t == 0)
        def _():
            acc[...] = jnp.zeros_like(acc)

        acc[...] += _dot_tn(f_ref[...], d_ref[...].astype(BF16))

        @pl.when(t == nt - 1)
        def _():
            for d in range(4):
                o_ref[d] = acc[FFB * d:FFB * (d + 1), :].astype(BF16)

    return pl.pallas_call(
        body, name="wgrad_down", grid=(2, nt),
        in_specs=[pl.BlockSpec((tm, FFC), lambda h, t: (t, h)), pl.BlockSpec((tm, D), lambda h, t: (t, 0))],
        out_specs=pl.BlockSpec((4, FFB, D), lambda h, t: (h, 0, 0), pipeline_mode=pl.Buffered(1)),
        out_shape=jax.ShapeDtypeStruct((NDEV, FFB, D), BF16),
        scratch_shapes=[pltpu.VMEM((FFC, D), F32)],
        compiler_params=_params(("arbitrary", "arbitrary"), 48),
    )(f, dh2)


def kernel(x, meta_tokens, g_mix, w_in, b_gate, w_dw, b_dw, ln_g, ln_b, w_conv_out, w_pool, pool_scale, w_pool_out, w_o, g_ffn, w_ffn_gate, w_ffn_up, w_ffn_down, g_final, loss_target, m_meta_tokens, m_g_mix, m_w_in, m_b_gate, m_w_dw, m_b_dw, m_ln_g, m_ln_b, m_w_conv_out, m_w_pool, m_pool_scale, m_w_pool_out, m_w_o, m_g_ffn, m_w_ffn_gate, m_w_ffn_up, m_w_ffn_down, m_g_final, v_meta_tokens, v_g_mix, v_w_in, v_b_gate, v_w_dw, v_b_dw, v_ln_g, v_ln_b, v_w_conv_out, v_w_pool, v_pool_scale, v_w_pool_out, v_w_o, v_g_ffn, v_w_ffn_gate, v_w_ffn_up, v_w_ffn_down, v_g_final):
    seq = x.shape[1]
    tp = -(-(seq + 2 * HALO) // TM) * TM
    tm_in = _pick(tp, TM_IO)
    nx_last = seq - (tp // tm_in - 1) * tm_in
    assert 0 < nx_last <= tm_in - 2 * HALO and nx_last % 8 == 0 and 0 < seq - (tp // TM - 1) * TM

    whole = (Ellipsis,)
    ag_small = _Gather(
        [((48, D // NDEV), [(meta_tokens, pl.ds(0, N_META), whole), (w_dw, pl.ds(N_META, CONV_K), 0)])], [F32])
    ag_mix = _Gather([((3, D // NDEV, D), [(w_conv_out, 0, 0), (w_pool_out, 1, 0), (w_o, 2, 0)]),
                      ((4, PG // NDEV, PG), [(w_pool, whole, 0)])], [BF16, BF16])
    def tr(a):
        return jnp.swapaxes(a, 1, 2)

    ag_gu = _Gather([((2, FFB, D), [(tr(w_ffn_gate), 0, 0), (tr(w_ffn_up), 1, 0)])], [BF16])
    ag_dn = _Gather([((FFB, D), [(w_ffn_down, whole, 0)])], [BF16])

    mx, my = lax.axis_index("x"), lax.axis_index("y")
    order = jnp.stack([2 * mx + my, 2 * mx + 1 - my, 2 * (1 - mx) + my, 2 * (1 - mx) + 1 - my]).astype(jnp.int32)
    (h0, z, u, g_in), (g_mixw, g_pool), g_small = _fwd_in(x[0], g_mix, w_in, order, tp, ag_mix, ag_small)
    wdw_full = g_small.transpose(1, 0, 2).reshape(48, D)[N_META:]
    (ac, m, gates), (w_gu,) = _seq_fwd(z, wdw_full, b_dw, b_gate, seq, ag_gu)
    (h1, s, merged, q), (g_down,) = _mix_fwd(ac, m, gates, h0, ln_g, ln_b, pool_scale, g_mixw, g_pool, ag_dn)
    w_dn = g_down.reshape(2, FFC, D)
    fg, fu, v, f, dh2, head_acc = _ffn_fwd(h1, loss_target[0], g_ffn, g_final.reshape(1, D), w_gu, w_dn)

    dfg, dfu, dh1, ffn_acc = _ffn_bwd(dh2, fg, fu, h1, g_ffn, w_gu, w_dn)
    own_f, sib_f, q_f = _rs_pair("rs_pair_ffn", [_wgrad_gu(v, dfg, dfu), _wgrad_down(f, dh2)])
    (dac, dm, dzg, dyc, dyp, dm2, mix_acc), rel_gu = _mix_bwd(
        dh1, gates, s, q, ac, m, ln_g, ln_b, pool_scale, g_mixw, g_pool, q_f[:1])
    p_mix, rel_dn = _wgrad_mix(s, dyc, q, dyp, merged, dh1, m, dm2, q_f[1:])
    rel_f = [rel_gu[0], rel_dn[0]]
    own_m, sib_m, q_m = _rs_pair("rs_pair_mix", list(p_mix))
    (dz, seq_acc), rel_m = _seq_bwd(dac, dm, dzg, z, wdw_full, seq, q_m)
    own_i, sib_i, q_i = _rs_pair("rs_pair_in", [_wgrad_in(u, dz)])
    (grad_x, g_meta, in_acc), rel_i = _in_bwd(dz, h0, dh1, g_mix, g_in, seq, q_i)
    small_g = jnp.concatenate([g_meta, seq_acc[:CONV_K], jnp.zeros((1, D), F32)], axis=0)
    p_small = small_g.reshape(48, NDEV, D // NDEV).transpose(1, 0, 2).astype(BF16)
    rep_g = jnp.concatenate([
        in_acc[0:1], mix_acc[0:1, :D], mix_acc[0:1, D:], seq_acc[CONV_K:CONV_K + 1], mix_acc[1:2, :D], mix_acc[1:2, D:],
        mix_acc[2:3, :D], ffn_acc[0:1], head_acc[1:2], head_acc[0:1], jnp.zeros((REP_ROWS - 10, D), F32)], axis=0)
    own_s, sib_s, rel_s, rep_all = _reduce_scatter([p_small], rep_g)
    owns = [own_i[0], own_s[0], own_m[0], own_m[1], own_f[0], own_f[1]]
    sibs = [sib_i[0], sib_s[0], sib_m[0], sib_m[1], sib_f[0], sib_f[1]]
    rels = [rel_i[0], rel_s[0], rel_m[0], rel_m[1], rel_f[0], rel_f[1]]

    def lead(a):
        return a.reshape(1, *a.shape)

    def stack4(a, lead_dims):
        return a.reshape(*lead_dims, 1, 4 * 32, PG)

    (r_in,) = _adamw_multi("adamw_in", lead(owns[0]), sibs[0][:, None], rels[0][:, None], [w_in], [m_w_in], [v_w_in], 4)
    r_meta, r_dw = _adamw_meta_dw(owns[1], sibs[1], rels[1], (meta_tokens, m_meta_tokens, v_meta_tokens),
                                  (w_dw, m_w_dw, v_w_dw))
    r_conv, r_pout, r_o = _adamw_multi("adamw_mix", owns[2], sibs[2], rels[2], [w_conv_out, w_pool_out, w_o],
                                       [m_w_conv_out, m_w_pool_out, m_w_o], [v_w_conv_out, v_w_pool_out, v_w_o], 1)
    (r_pool,) = _adamw_multi("adamw_pool", stack4(owns[3], ()), stack4(sibs[3], (1,)), stack4(rels[3], (3,)),
                             [w_pool.reshape(1, 128, PG)], [m_w_pool.reshape(1, 128, PG)], [v_w_pool.reshape(1, 128, PG)], 1)
    r_pool = tuple(a.reshape(w_pool.shape) for a in r_pool)
    r_gate, r_up = _adamw_multi("adamw_gu", owns[4], sibs[4], rels[4], [tr(w_ffn_gate), tr(w_ffn_up)],
                                [tr(m_w_ffn_gate), tr(m_w_ffn_up)], [tr(v_w_ffn_gate), tr(v_w_ffn_up)], 2)
    r_gate, r_up = tuple(tr(a) for a in r_gate), tuple(tr(a) for a in r_up)
    (r_down,) = _adamw_multi("adamw_down", lead(owns[5]), sibs[5][:, None], rels[5][:, None],
                             [w_ffn_down], [m_w_ffn_down], [v_w_ffn_down], 2)
    row = (1, D)
    loss, reps = _adamw_rep(
        rep_all,
        [g_mix, b_gate, b_dw, ln_g, ln_b, pool_scale, g_ffn, g_final.reshape(row)],
        [m_g_mix, m_b_gate, m_b_dw, m_ln_g, m_ln_b, m_pool_scale, m_g_ffn, m_g_final.reshape(row)],
        [v_g_mix, v_b_gate, v_b_dw, v_ln_g, v_ln_b, v_pool_scale, v_g_ffn, v_g_final.reshape(row)])
    r_gmix, r_bg, r_bdw, r_lg, r_lb, r_ps, r_gffn, r_gfin = reps
    r_gfin = tuple(a.reshape(D) for a in r_gfin)

    in_order = [r_meta, r_gmix, r_in, r_bg, r_dw, r_bdw, r_lg, r_lb, r_conv, r_pool, r_ps, r_pout, r_o, r_gffn,
                r_gate, r_up, r_down, r_gfin]
    return (loss.reshape(()), grad_x[None], *[r[0] for r in in_order], *[r[1] for r in in_order],
            *[r[2] for r in in_order], *[r[3] for r in in_order])
```

```python
import math

import jax
import jax.numpy as jnp
from jax import lax
from jax.experimental import pallas as pl
from jax.experimental.pallas import tpu as pltpu

F32, BF16 = jnp.float32, jnp.bfloat16
MESH_ID = pl.DeviceIdType.MESH
NDEV = 8

D = 1024
N_META = 16
CONV_K = 31
HALO = 16
POOL_WINDOWS = (2, 4, 8, 16)
PG = 256
DIN = 5 * D
DFF = 2816
FFB = DFF // NDEV
FFC = DFF // 2
INB = DIN // NDEV
RMS_EPS = 1e-6
LN_EPS = 1e-5
ADAM_LR, ADAM_B1, ADAM_B2, ADAM_EPS, ADAM_WD, ADAM_STEP = 0.001, 0.9, 0.999, 1e-08, 0.01, 10

TM = 384
TMS = 384
TM_IO = 704
TM_WG = 1408
TM_WM = 704
MIB = 2 ** 20


def _sig(x):
    return 0.5 * jnp.tanh(0.5 * x) + 0.5


def _dot(a, b):
    return jnp.dot(a, b, preferred_element_type=F32)


def _dot_nt(a, b):
    return lax.dot_general(a, b, (((1,), (1,)), ((), ())), preferred_element_type=F32)


def _dot_tn(a, b):
    return lax.dot_general(a, b, (((0,), (0,)), ((), ())), preferred_element_type=F32)


def _pick(tp, pref):
    return pref if tp % pref == 0 else TM


def _params(sem, vmem_mib):
    return pltpu.CompilerParams(dimension_semantics=sem, vmem_limit_bytes=vmem_mib * MIB)


def _load_once(first, pairs, sems):
    @pl.when(first)
    def _():
        cps = [pltpu.make_async_copy(s, d, sems.at[k]) for k, (s, d) in enumerate(pairs)]
        for cp in cps:
            cp.start()
        for cp in cps:
            cp.wait()


def _place():
    x, y, c = lax.axis_index("x"), lax.axis_index("y"), lax.axis_index("c")
    return x, y, c


class _Gather:
    def __init__(self, groups, dtypes):
        self.groups, self.dtypes, self.n = groups, dtypes, len(groups)
        self.arrays = [a for _, parts in groups for a, _, _ in parts]
        self.out_shape = [jax.ShapeDtypeStruct((NDEV, *s), dt) for (s, _), dt in zip(groups, dtypes)]
        self.scratch = [pltpu.VMEM(s, dt) for (s, _), dt in zip(groups, dtypes)] + [
            pltpu.SemaphoreType.DMA((7 * self.n,)), pltpu.SemaphoreType.DMA((7 * self.n,)),
            pltpu.SemaphoreType.DMA((self.n,))]

    def bind(self, ins, outs, scratch):
        self.ins, self.outs, self.stages = ins, outs, scratch[:self.n]
        self.send_sems, self.recv_sems, self.local_sems = scratch[self.n:]
        return self

    def _copy(self, w, k, block, to, src=None):
        dst = self.outs[w].at[4 * block[0] + 2 * block[1] + block[2]]
        return pltpu.make_async_remote_copy(
            src_ref=dst if src is None else src, dst_ref=dst,
            send_sem=self.send_sems.at[7 * w + k], recv_sem=self.recv_sems.at[7 * w + k],
            device_id=to, device_id_type=MESH_ID)

    def _first(self):
        x, y, c = _place()
        me, sibling = (x, y, c), (x, y, 1 - c)
        chips = [(1 - x, y), (x, 1 - y), (1 - x, 1 - y)]
        mine, first = [], []
        for w in range(self.n):
            mine.append(pltpu.make_async_copy(self.stages[w], self.outs[w].at[4 * x + 2 * y + c], self.local_sems.at[w]))
            first.append(self._copy(w, 0, me, sibling, src=self.stages[w]))
            first += [self._copy(w, 1 + j, me, (*chip, c), src=self.stages[w]) for j, chip in enumerate(chips)]
        return mine, first

    def _passed(self):
        x, y, c = _place()
        chips = [(1 - x, y), (x, 1 - y), (1 - x, 1 - y)]
        return [self._copy(w, 4 + j, (*chip, c), (x, y, 1 - c)) for w in range(self.n) for j, chip in enumerate(chips)]

    def issue(self):
        a = 0
        for w in range(self.n):
            shape, parts = self.groups[w]
            if sum(arr.size for arr, _, _ in parts) < math.prod(shape):
                self.stages[w][...] = jnp.zeros(shape, self.dtypes[w])
            for _, dst, src in parts:
                self.stages[w][dst] = self.ins[a][src].astype(self.dtypes[w])
                a += 1
        mine, first = self._first()
        for cp in mine + first:
            cp.start()

    def forward(self):
        x, y, c = _place()
        chips = [(1 - x, y), (x, 1 - y), (1 - x, 1 - y)]
        passed = self._passed()
        for w in range(self.n):
            for j, chip in enumerate(chips):
                self._copy(w, 1 + j, (*chip, c), (x, y, c)).wait_recv()
                passed[3 * w + j].start()

    def finish(self):
        x, y, c = _place()
        chips = [(1 - x, y), (x, 1 - y), (1 - x, 1 - y)]
        for w in range(self.n):
            self._copy(w, 0, (x, y, 1 - c), (x, y, c)).wait_recv()
            for j, chip in enumerate(chips):
                self._copy(w, 4 + j, (*chip, 1 - c), (x, y, c)).wait_recv()
        mine, first = self._first()
        for cp in first + self._passed():
            cp.wait_send()
        for cp in mine:
            cp.wait()


class _ChipExchange:
    def __init__(self, qs):
        self.n = len(qs)
        self.out_shape = [jax.ShapeDtypeStruct(q.shape, q.dtype) for q in qs]
        self.scratch = [pltpu.SemaphoreType.DMA((3 * self.n,)), pltpu.SemaphoreType.DMA((3 * self.n,))]

    def bind(self, qs, rels, scratch):
        self.qs, self.rels = qs, rels
        self.send_sems, self.recv_sems = scratch
        return self

    def _copies(self):
        x, y, c = _place()
        chips = [(1 - x, y), (x, 1 - y), (1 - x, 1 - y)]
        return [pltpu.make_async_remote_copy(
            src_ref=self.qs[w].at[j], dst_ref=self.rels[w].at[j],
            send_sem=self.send_sems.at[3 * w + j], recv_sem=self.recv_sems.at[3 * w + j],
            device_id=(*chips[j], c), device_id_type=MESH_ID) for w in range(self.n) for j in range(3)]

    def issue(self):
        for cp in self._copies():
            cp.start()

    def finish(self):
        cps = self._copies()
        for cp in cps:
            cp.wait_recv()
        for cp in cps:
            cp.wait_send()


def _reduce_scatter(parts, small):
    n = len(parts)
    blks = [p.shape[1:] for p in parts]

    def body(*refs):
        ps, small_ref = refs[:n], refs[n]
        o = n + 1
        owns, sibs, rels, small_out = refs[o:o + n], refs[o + n:o + 2 * n], refs[o + 2 * n:o + 3 * n], refs[o + 3 * n]
        o += 3 * n + 1
        pa, pb, qst = refs[o:o + n], refs[o + n:o + 2 * n], refs[o + 2 * n:o + 3 * n]
        s1_send, s1_recv, s2_send, s2_recv, sm_send, sm_recv, lsem = refs[o + 3 * n:]
        x, y, c = _place()
        me = 4 * x + 2 * y + c
        sibling = (x, y, 1 - c)
        chips = [(1 - x, y), (x, 1 - y), (1 - x, 1 - y)]
        all_chips = [(x, y)] + chips

        own_cps = []
        for w in range(n):
            cp = pltpu.make_async_copy(ps[w].at[me], owns[w], lsem.at[w])
            cp.start()
            own_cps.append(cp)
        sm_own = pltpu.make_async_copy(small_ref, small_out.at[me], lsem.at[n])
        sm_own.start()

        def small_copy(r):
            peer = ((x + (r >> 2)) % 2, (y + ((r >> 1) & 1)) % 2, (c + (r & 1)) % 2)
            return pltpu.make_async_remote_copy(
                src_ref=small_ref, dst_ref=small_out.at[me], send_sem=sm_send.at[r - 1], recv_sem=sm_recv.at[r - 1],
                device_id=peer, device_id_type=MESH_ID)

        sm_cps = [small_copy(r) for r in range(1, NDEV)]
        for cp in sm_cps:
            cp.start()

        def pair_copy(w, rel):
            cx, cy = all_chips[rel]
            return pltpu.make_async_remote_copy(
                src_ref=ps[w].at[4 * cx + 2 * cy + (1 - c)], dst_ref=sibs[w].at[rel],
                send_sem=s1_send.at[4 * w + rel], recv_sem=s1_recv.at[4 * w + rel],
                device_id=sibling, device_id_type=MESH_ID)

        def chip_copy(w, j):
            return pltpu.make_async_remote_copy(
                src_ref=qst[w].at[j], dst_ref=rels[w].at[j],
                send_sem=s2_send.at[3 * w + j], recv_sem=s2_recv.at[3 * w + j],
                device_id=(*chips[j], c), device_id_type=MESH_ID)

        pair_cps = [pair_copy(w, rel) for w in range(n) for rel in (1, 2, 3, 0)]
        for cp in pair_cps:
            cp.start()
        chip_cps = []
        for w in range(n):
            for j, (cx, cy) in enumerate(chips):
                pair_copy(w, 1 + j).wait_recv()
                la = pltpu.make_async_copy(ps[w].at[4 * cx + 2 * cy + c], pa[w], lsem.at[n + 1])
                lb = pltpu.make_async_copy(sibs[w].at[1 + j], pb[w], lsem.at[n + 2])
                la.start()
                lb.start()
                la.wait()
                lb.wait()
                qst[w][j] = (pa[w][...].astype(F32) + pb[w][...].astype(F32)).astype(BF16)
                cp = chip_copy(w, j)
                cp.start()
                chip_cps.append(cp)
        for w in range(n):
            pair_copy(w, 0).wait_recv()
            for j in range(3):
                chip_copy(w, j).wait_recv()
        for cp in sm_cps:
            cp.wait_recv()
        for cp in pair_cps + chip_cps + sm_cps:
            cp.wait_send()
        for cp in own_cps:
            cp.wait()
        sm_own.wait()

    any_spec = pl.BlockSpec(memory_space=pl.ANY)
    outs = pl.pallas_call(
        body, name="rs_grads",
        out_shape=[jax.ShapeDtypeStruct(b, BF16) for b in blks]
        + [jax.ShapeDtypeStruct((4, *b), BF16) for b in blks]
        + [jax.ShapeDtypeStruct((3, *b), BF16) for b in blks]
        + [jax.ShapeDtypeStruct((NDEV, *small.shape), F32)],
        in_specs=[any_spec] * (n + 1),
        out_specs=[any_spec] * (3 * n + 1),
        scratch_shapes=[pltpu.VMEM(b, BF16) for b in blks] + [pltpu.VMEM(b, BF16) for b in blks]
        + [pltpu.VMEM((3, *b), BF16) for b in blks]
        + [pltpu.SemaphoreType.DMA((4 * n,)), pltpu.SemaphoreType.DMA((4 * n,)),
           pltpu.SemaphoreType.DMA((3 * n,)), pltpu.SemaphoreType.DMA((3 * n,)),
           pltpu.SemaphoreType.DMA((NDEV - 1,)), pltpu.SemaphoreType.DMA((NDEV - 1,)),
           pltpu.SemaphoreType.DMA((n + 3,))],
        compiler_params=pltpu.CompilerParams(vmem_limit_bytes=40 * MIB),
    )(*parts, small)
    return outs[:n], outs[n:2 * n], outs[2 * n:3 * n], outs[3 * n]


class _PairSum:
    def __init__(self, parts, keep_q=True):
        self.n = n = len(parts)
        self.keep_q = keep_q
        blks = [p.shape[1:] for p in parts]
        self.out_shape = [jax.ShapeDtypeStruct(b, BF16) for b in blks] + [jax.ShapeDtypeStruct((1, *b), BF16) for b in blks]
        if keep_q:
            self.out_shape += [jax.ShapeDtypeStruct((3, *b), BF16) for b in blks]
        self.scratch = [pltpu.VMEM((3, *b), BF16) for b in blks] * 3 + [
            pltpu.SemaphoreType.DMA((4 * n,)), pltpu.SemaphoreType.DMA((4 * n,)), pltpu.SemaphoreType.DMA((5 * n,))]

    def bind(self, ps, outs, scratch):
        n = self.n
        self.ps, self.owns, self.sibs, self.qs = ps, outs[:n], outs[n:2 * n], outs[2 * n:]
        self.pa, self.pb, self.qst = scratch[:n], scratch[n:2 * n], scratch[2 * n:3 * n]
        self.s_send, self.s_recv, self.lsem = scratch[3 * n:]
        return self

    def _local(self, with_q):
        n = self.n
        x, y, c = _place()
        chips = [(1 - x, y), (x, 1 - y), (1 - x, 1 - y)]
        own = [pltpu.make_async_copy(self.ps[w].at[4 * x + 2 * y + c], self.owns[w], self.lsem.at[w]) for w in range(n)]
        mine = [[pltpu.make_async_copy(self.ps[w].at[4 * cx + 2 * cy + c], self.pa[w].at[j], self.lsem.at[2 * n + 3 * w + j])
                 for j, (cx, cy) in enumerate(chips)] for w in range(n)]
        outq = [pltpu.make_async_copy(self.qst[w], self.qs[w], self.lsem.at[n + w]) for w in range(n)] if with_q else []
        return own, mine, outq

    def _pair(self, w, rel):
        x, y, c = _place()
        cx, cy = [(x, y), (1 - x, y), (x, 1 - y), (1 - x, 1 - y)][rel]
        return pltpu.make_async_remote_copy(
            src_ref=self.ps[w].at[4 * cx + 2 * cy + (1 - c)],
            dst_ref=self.sibs[w].at[0] if rel == 0 else self.pb[w].at[rel - 1],
            send_sem=self.s_send.at[4 * w + rel], recv_sem=self.s_recv.at[4 * w + rel],
            device_id=(x, y, 1 - c), device_id_type=MESH_ID)

    def issue(self):
        own, mine, _ = self._local(False)
        for cp in own + [cp for row in mine for cp in row]:
            cp.start()
        for w in range(self.n):
            for rel in (1, 2, 3, 0):
                self._pair(w, rel).start()

    def finish(self):
        own, mine, outq = self._local(self.keep_q)
        for w in range(self.n):
            for j in range(3):
                self._pair(w, 1 + j).wait_recv()
                mine[w][j].wait()
                self.qst[w][j] = (self.pa[w][j].astype(F32) + self.pb[w][j].astype(F32)).astype(BF16)
            if self.keep_q:
                outq[w].start()
        for w in range(self.n):
            self._pair(w, 0).wait_recv()
        for w in range(self.n):
            for rel in range(4):
                self._pair(w, rel).wait_send()
        for cp in own + outq:
            cp.wait()

    def results(self, outs):
        n = self.n
        return outs[:n], outs[n:2 * n], outs[2 * n:3 * n]


def _rs_pair(name, parts):
    ps = _PairSum(parts)
    n = ps.n

    def body(*refs):
        ps.bind(refs[:n], refs[n:4 * n], refs[4 * n:])
        ps.issue()
        ps.finish()

    any_spec = pl.BlockSpec(memory_space=pl.ANY)
    outs = pl.pallas_call(
        body, name=name, out_shape=ps.out_shape,
        in_specs=[any_spec] * n, out_specs=[any_spec] * (3 * n), scratch_shapes=ps.scratch,
        compiler_params=pltpu.CompilerParams(vmem_limit_bytes=48 * MIB),
    )(*parts)
    return ps.results(outs)


def _adamw_math(g, w, m, v):
    m = ADAM_B1 * m + (1.0 - ADAM_B1) * g
    v = ADAM_B2 * v + (1.0 - ADAM_B2) * (g * g)
    m_hat = m / (1.0 - ADAM_B1 ** ADAM_STEP)
    v_hat = v / (1.0 - ADAM_B2 ** ADAM_STEP)
    delta = -ADAM_LR * (m_hat / (jnp.sqrt(v_hat) + ADAM_EPS) + ADAM_WD * w)
    return delta, m, v


def _adamw_multi(name, own, sib, rel, ws, ms, vs, row_grid):
    k_n, r_n, c_n = own.shape
    rbk = r_n // row_grid

    def body(*refs):
        own_ref, sib_ref, r0_ref, r1_ref, r2_ref = refs[:5]
        w_refs, m_refs, v_refs = refs[5:5 + k_n], refs[5 + k_n:5 + 2 * k_n], refs[5 + 2 * k_n:5 + 3 * k_n]
        outs = refs[5 + 3 * k_n:]
        for k in range(k_n):
            g = own_ref[k].astype(F32) + sib_ref[k].astype(F32)
            g = g + r0_ref[k].astype(F32)
            g = g + r1_ref[k].astype(F32)
            g = g + r2_ref[k].astype(F32)
            delta, mm, vv = _adamw_math(g, w_refs[k][0], m_refs[k][0], v_refs[k][0])
            outs[4 * k][0] = g
            outs[4 * k + 1][0] = delta
            outs[4 * k + 2][0] = mm
            outs[4 * k + 3][0] = vv

    def lead(j):
        return pl.BlockSpec((None, k_n, rbk, c_n), lambda g: (j, 0, g, 0))

    wspec = pl.BlockSpec((1, rbk, c_n), lambda g: (0, g, 0))
    shp = jax.ShapeDtypeStruct((1, r_n, c_n), F32)
    res = pl.pallas_call(
        body, name=name, grid=(row_grid,),
        in_specs=[pl.BlockSpec((k_n, rbk, c_n), lambda g: (0, g, 0)), lead(0), lead(0), lead(1), lead(2)] + [wspec] * (3 * k_n),
        out_specs=[wspec] * (4 * k_n), out_shape=[shp] * (4 * k_n),
        compiler_params=_params(("arbitrary",), 40),
    )(own, sib, rel, rel, rel, *ws, *ms, *vs)
    return [tuple(res[4 * k:4 * k + 4]) for k in range(k_n)]


def _adamw_meta_dw(own, sib, rel, meta, dw):
    def body(own_ref, sib_ref, rel_ref, wm, mm, vm, wd, md, vd, *outs):
        def gsum(rows):
            g = own_ref[rows, :].astype(F32) + sib_ref[0, rows, :].astype(F32)
            for j in range(3):
                g = g + rel_ref[j, rows, :].astype(F32)
            return g

        g = gsum(pl.ds(0, N_META))
        delta, m2, v2 = _adamw_math(g, wm[...], mm[...], vm[...])
        for o, val in zip(outs[:4], (g, delta, m2, v2)):
            o[...] = val
        g = gsum(pl.ds(N_META, CONV_K))
        delta, m2, v2 = _adamw_math(g, wd[0], md[0], vd[0])
        for o, val in zip(outs[4:], (g, delta, m2, v2)):
            o[0] = val

    s_meta = jax.ShapeDtypeStruct(meta[0].shape, F32)
    s_dw = jax.ShapeDtypeStruct(dw[0].shape, F32)
    res = pl.pallas_call(body, name="adamw_meta_dw", out_shape=[s_meta] * 4 + [s_dw] * 4)(own, sib, rel, *meta, *dw)
    return tuple(res[:4]), tuple(res[4:])


REP_ROWS = 16


def _adamw_rep(gathered, ws, ms, vs):
    rows = [(0, 1), (1, 2), (3, 1), (4, 1), (5, 1), (6, 1), (7, 1), (8, 1)]

    def body(g_ref, *refs):
        w_refs, m_refs, v_refs = refs[:8], refs[8:16], refs[16:24]
        loss_ref, outs, acc = refs[24], refs[25:57], refs[57]
        g = g_ref[0]
        for d in range(1, NDEV):
            g = g + g_ref[d]
        acc[...] = g
        loss_ref[...] = (0.5 / D) * jnp.sum(acc[pl.ds(9, 1), :], axis=1, keepdims=True)
        for p, (r0, nr) in enumerate(rows):
            for h in range(nr):
                cols = pl.ds(h * D, D)
                gp = acc[pl.ds(r0 + h, 1), :]
                delta, mm, vv = _adamw_math(gp, w_refs[p][:, cols], m_refs[p][:, cols], v_refs[p][:, cols])
                for o, val in zip(outs[4 * p:4 * p + 4], (gp, delta, mm, vv)):
                    o[:, cols] = val

    shapes = [jax.ShapeDtypeStruct(w.shape, F32) for w in ws]
    res = pl.pallas_call(
        body, name="adamw_rep",
        out_shape=[jax.ShapeDtypeStruct((1, 1), F32)] + [s for s in shapes for _ in range(4)],
        scratch_shapes=[pltpu.VMEM((REP_ROWS, D), F32)],
    )(gathered, *ws, *ms, *vs)
    return res[0], [tuple(res[1 + 4 * p:5 + 4 * p]) for p in range(8)]


def _load_ffn(i, j, wgu_hbm, wgu, wdn_hbm, wdn, sems):
    half = NDEV // 2

    def copies(ch):
        pairs = [(wgu_hbm.at[half * ch + d, g], wgu.at[g, ch, pl.ds(FFB * d, FFB), :]) for g in range(2) for d in range(half)]
        pairs.append((wdn_hbm.at[ch], wdn.at[ch]))
        return [pltpu.make_async_copy(s, t, sems.at[(2 * half + 1) * ch + k]) for k, (s, t) in enumerate(pairs)]

    @pl.when((i == 0) & (j == 0))
    def _():
        for cp in copies(0) + copies(1):
            cp.start()

    for ch in range(2):
        @pl.when((i == 0) & (j == ch))
        def _():
            for cp in copies(ch):
                cp.wait()


def _win_pairs(w_hbm, w_vm):
    return [(w_hbm.at[q], w_vm.at[q // 2, :, pl.ds(2 * INB * (q % 2), 2 * INB)]) for q in range(4)]


def _whole(a):
    nd = a.ndim
    return pl.BlockSpec(a.shape, lambda *g: (0,) * nd)


CHIPW = 2 * INB
PHASE_CHIP = (1, 0, 2)
assert PHASE_CHIP[2] == 2


class _GatherIn:
    scratch = [pltpu.VMEM((D, INB), BF16), pltpu.SemaphoreType.DMA((7,)), pltpu.SemaphoreType.DMA((7,)),
               pltpu.SemaphoreType.DMA((1,))]

    def bind(self, w_ref, w_vm, scratch):
        self.w_ref, self.w_vm = w_ref, w_vm
        self.stage, self.send_sems, self.recv_sems, self.local_sem = scratch
        return self

    def _win(self, chip, core):
        return self.w_vm.at[2 * chip[0] + chip[1], :, pl.ds(INB * core, INB)]

    def _copy(self, k, chip, core, to, src=None):
        dst = self._win(chip, core)
        return pltpu.make_async_remote_copy(
            src_ref=dst if src is None else src, dst_ref=dst, send_sem=self.send_sems.at[k],
            recv_sem=self.recv_sems.at[k], device_id=to, device_id_type=MESH_ID)

    def _mine(self, cs):
        x, y, _ = _place()
        return pltpu.make_async_copy(self.stage, self._win((x, y), cs), self.local_sem.at[0])

    def issue(self, cs):
        x, y, _ = _place()
        chips = [(1 - x, y), (x, 1 - y), (1 - x, 1 - y)]
        self.stage[...] = self.w_ref[0].astype(BF16)
        self._mine(cs).start()
        self._copy(0, (x, y), cs, (x, y, 1 - cs), src=self.stage).start()
        for j in PHASE_CHIP[:2]:
            self._copy(1 + j, (x, y), cs, (*chips[j], cs), src=self.stage).start()

    def wait_chip(self, phase, cs):
        x, y, _ = _place()
        chips = [(1 - x, y), (x, 1 - y), (1 - x, 1 - y)]
        if phase == 0:
            self._mine(cs).wait()
            self._copy(0, (x, y), 1 - cs, (x, y, cs)).wait_recv()
            return
        if phase == 1:
            for j in PHASE_CHIP[:2]:
                self._copy(1 + j, chips[j], cs, (x, y, cs)).wait_recv()
                self._copy(4 + j, chips[j], cs, (x, y, 1 - cs)).start()
            self._copy(3, (x, y), cs, (*chips[2], cs), src=self.stage).start()
        j = PHASE_CHIP[phase - 1]
        if phase == 3:
            self._copy(1 + j, chips[j], cs, (x, y, cs)).wait_recv()
            self._copy(4 + j, chips[j], cs, (x, y, 1 - cs)).start()
        self._copy(4 + j, chips[j], 1 - cs, (x, y, cs)).wait_recv()

    def finish(self, cs):
        x, y, _ = _place()
        for k in range(7):
            self._copy(k, (x, y), cs, (x, y, cs), src=self.stage).wait_send()


def _fwd_in(x2, g_mix, w_in, order, tp, ag, ags):
    tm = _pick(tp, TM_IO)
    nt = tp // tm
    nx_last = x2.shape[0] - (nt - 1) * tm
    na, ng, ns = len(ag.arrays), ag.n, len(ags.arrays)
    gin = _GatherIn()

    def body(order_ref, *refs):
        x_ref, g_ref, w_ref = refs[:3]
        o = 3 + na + ns
        h_ref, z_ref, u_ref, wout_ref = refs[o:o + 4]
        s = o + 4 + ng + 1
        w_vm, u_all, osem, sm_vm = refs[s:s + 4]
        gin.bind(w_ref, w_vm, refs[s + 4:s + 8])
        ag.bind(refs[3:3 + na], refs[o + 4:o + 4 + ng], refs[s + 8:s + 8 + len(ag.scratch)])
        ags.bind(refs[3 + na:3 + na + ns], refs[o + 4 + ng:o + 5 + ng], refs[s + 8 + len(ag.scratch):])
        ph, i = pl.program_id(0), pl.program_id(1)
        core = lax.axis_index("c")
        first = (ph == 0) & (i == 0)
        last = (ph == 3) & (i == nt - 1)
        for cs in range(2):
            @pl.when(first & (core == cs))
            def _():
                gin.issue(cs)

        @pl.when(first)
        def _():
            ags.issue()

        @pl.when((ph == 0) & (i == max(nt - 2, 0)))
        def _():
            ags.forward()

        for cs in range(2):
            for p in range(4):
                @pl.when((ph == p) & (i == 0) & (core == cs))
                def _():
                    gin.wait_chip(p, cs)

        @pl.when((ph == 2) & (i == nt // 2))
        def _():
            ag.issue()

        out_copy = pltpu.make_async_copy(w_vm, wout_ref, osem.at[0])

        @pl.when((ph == 3) & (i == 0))
        def _():
            out_copy.start()

        @pl.when((ph == 0) & (i < nt - 1))
        def _():
            h_ref[...] = x_ref[...]

        @pl.when((ph == 0) & (i == nt - 1))
        def _():
            ags.finish()
            cp = pltpu.make_async_copy(ags.outs[0], sm_vm, osem.at[1])
            cp.start()
            h_ref[pl.ds(0, nx_last), :] = x_ref[pl.ds(0, nx_last), :]
            h_ref[pl.ds(nx_last, tm - nx_last - N_META), :] = jnp.zeros((tm - nx_last - N_META, D), F32)
            cp.wait()
            for d in range(NDEV):
                h_ref[pl.ds(tm - N_META, N_META), pl.ds(128 * d, 128)] = sm_vm[d, pl.ds(0, N_META), :]

        @pl.when(ph == 0)
        def _():
            xv = h_ref[...]
            r = lax.rsqrt(jnp.mean(xv * xv, axis=-1, keepdims=True) + RMS_EPS)
            u = (xv * r * g_ref[...]).astype(BF16)
            u_ref[...] = u
            u_all[i] = u

        z_ref[...] = _dot(u_all[i], w_vm[order_ref[ph]])

        @pl.when(last)
        def _():
            ag.forward()
            ag.finish()
            out_copy.wait()

        for cs in range(2):
            @pl.when(last & (core == cs))
            def _():
                gin.finish(cs)

    def rows(ph, i, order):
        return (jnp.where(ph == 0, i, nt - 1), 0)

    tile = pl.BlockSpec((tm, D), rows)
    anys = pl.BlockSpec(memory_space=pl.ANY)
    res = pl.pallas_call(
        body, name="fwd_in",
        grid_spec=pltpu.PrefetchScalarGridSpec(
            num_scalar_prefetch=1, grid=(4, nt),
            in_specs=[tile, pl.BlockSpec((1, D), lambda ph, i, order: (0, 0)), _whole(w_in)]
            + [_whole(a) for a in ag.arrays + ags.arrays],
            out_specs=[tile, pl.BlockSpec((tm, CHIPW), lambda ph, i, order: (i, order[ph])), tile, anys] + [anys] * (ng + 1),
            scratch_shapes=[pltpu.VMEM((4, D, CHIPW), BF16), pltpu.VMEM((nt, tm, D), BF16), pltpu.SemaphoreType.DMA((2,)),
                            pltpu.VMEM(ags.out_shape[0].shape, F32)] + gin.scratch + ag.scratch + ags.scratch),
        out_shape=[jax.ShapeDtypeStruct((tp, D), F32), jax.ShapeDtypeStruct((tp, DIN), F32),
                   jax.ShapeDtypeStruct((tp, D), BF16), jax.ShapeDtypeStruct((4, D, CHIPW), BF16)]
        + ag.out_shape + ags.out_shape,
        compiler_params=_params(("arbitrary", "arbitrary"), 58),
    )(order, x2, g_mix, w_in, *ag.arrays, *ags.arrays)
    return res[:4], res[4:4 + ng], res[4 + ng]


def _halo_specs(col, nt, width=D):
    r = TM // HALO
    nb = nt * r
    return [pl.BlockSpec((HALO, width), lambda i: ((i * r + nb - 1) % nb, col)),
            pl.BlockSpec((TM, width), lambda i: (i, col)),
            pl.BlockSpec((HALO, width), lambda i: (((i + 1) * r) % nb, col))]


NCB = D // 128
TME = TM + 2 * HALO


def _tm_fill(dst, time0, groups, tile_fn):
    def body(g, c):
        for j in range(NCB):
            dst[pl.ds((time0 + 8 * g) * NCB + j, 8, stride=NCB), :] = tile_fn(pl.multiple_of(8 * g, 8), pl.ds(128 * j, 128))
        return c

    lax.fori_loop(0, groups, body, 0)


def _tm_fill_ext(dst, left, cur, right, fn):
    _tm_fill(dst, 0, HALO // 8, lambda r, l: fn(left, pl.ds(r, 8), l))
    _tm_fill(dst, HALO, TM // 8, lambda r, l: fn(cur, pl.ds(r, 8), l))
    _tm_fill(dst, HALO + TM, HALO // 8, lambda r, l: fn(right, pl.ds(r, 8), l))


def _tm_read(src, groups, store_fn):
    def body(g, c):
        for j in range(NCB):
            store_fn(pl.ds(pl.multiple_of(8 * g, 8), 8), pl.ds(128 * j, 128), src[pl.ds(8 * g * NCB + j, 8, stride=NCB), :])
        return c

    lax.fori_loop(0, groups, body, 0)


def _tm_rows(t):
    return pl.ds(t * NCB if isinstance(t, int) else pl.multiple_of(t * NCB, NCB), NCB)


def _tm_at(ref, t):
    return ref[_tm_rows(t), :]


def _by_group(sub, vals):
    return jnp.where(sub < 2, vals[0], jnp.where(sub < 4, vals[1], jnp.where(sub < 6, vals[2], vals[3])))


def _pool_cnt(b, seq, tp, sub):
    b = jnp.where(b < 0, b + tp, b)
    b = jnp.where(b >= tp, b - tp, b)
    t = jnp.where(b < seq, b + N_META, b - (tp - N_META))
    cnts = []
    for win in POOL_WINDOWS:
        left = win // 2
        lo = jnp.maximum(t - left, 0)
        hi = jnp.minimum(t + win - left, seq + N_META)
        cnts.append(jnp.maximum(hi - lo, 1).astype(F32))
    return _by_group(sub, cnts)


def _edge_rows(seq, tp):
    reach = max(POOL_WINDOWS) // 2
    return [tp - N_META + t for t in range(reach)] + [seq - reach + 1 + t for t in range(reach - 1)]


def _edge_gain(b, seq, tp, sub):
    return _by_group(sub, [float(w) for w in POOL_WINDOWS]) / _pool_cnt(b, seq, tp, sub)


def _nested_windows(at, lo_offs):
    sums, s, have = [], None, set()
    for g, win in enumerate(POOL_WINDOWS):
        for o in range(lo_offs[g], lo_offs[g] + win):
            if o not in have:
                have.add(o)
                s = at(o) if s is None else s + at(o)
        sums.append(s)
    return sums


def _seq_fwd(z, w_dw, b_dw, seq, gat):
    tp = z.shape[0]
    nt = tp // TM
    na, ng = len(gat.arrays), gat.n

    def body(*refs):
        av_l, av, av_r, ag_l, ag, ag_r, p_l, p, p_r, w_ref, b_ref = refs[:11]
        ac_ref, m_ref = refs[11 + na:13 + na]
        a3, p3, o3, m3, w3, b3, m2d = refs[13 + na + ng:20 + na + ng]
        gat.bind(refs[11:11 + na], refs[13 + na:13 + na + ng], refs[20 + na + ng:])
        i = pl.program_id(0)
        sub = lax.broadcasted_iota(jnp.int32, (NCB, 128), 0)

        @pl.when(i == 0)
        def _():
            gat.issue()
            _tm_fill(w3, 0, 4, lambda r, l: w_ref[pl.ds(r, 8), l])
            for j in range(NCB):
                b3[pl.ds(j, 1), :] = b_ref[:, pl.ds(128 * j, 128)]

        @pl.when(i == max(nt - 2, 0))
        def _():
            gat.forward()

        _tm_fill_ext(a3, (av_l, ag_l), (av, ag), (av_r, ag_r), lambda vg, r, l: vg[0][r, l] * _sig(vg[1][r, l]))
        _tm_fill_ext(p3, p_l, p, p_r, lambda ref, r, l: ref[r, l])

        def conv(g, c):
            accs = [b3[...]] * 16
            for k in range(CONV_K):
                wk = _tm_at(w3, k)
                for t in range(16):
                    accs[t] = accs[t] + wk * _tm_at(a3, 16 * g + t + k + 1)
            for t in range(16):
                o3[_tm_rows(16 * g + t), :] = accs[t]
            return c

        lax.fori_loop(0, TM // 16, conv, 0)
        _tm_read(o3, TM // 8, lambda r, l, tile: ac_ref.__setitem__((r, l), tile))

        inv = _by_group(sub, [1.0 / w for w in POOL_WINDOWS])

        def pool(g, c):
            for t in range(8):
                e = 8 * g + t + HALO
                sums = _nested_windows(lambda o: _tm_at(p3, e + o), [-(w // 2) for w in POOL_WINDOWS])
                m3[_tm_rows(8 * g + t), :] = _by_group(sub, sums) * inv - _tm_at(p3, e)
            return c

        lax.fori_loop(0, TM // 8, pool, 0)
        for b in _edge_rows(seq, tp):
            r = b - i * TM

            @pl.when((r >= 0) & (r < TM))
            def _():
                pv = _tm_at(p3, r + HALO)
                m3[_tm_rows(r), :] = (_tm_at(m3, r) + pv) * _edge_gain(b, seq, tp, sub) - pv

        _tm_read(m3, TM // 8, lambda r, l, tile: m2d.__setitem__((r, l), tile))
        m_ref[...] = m2d[...].astype(BF16)

        @pl.when(i == nt - 1)
        def _():
            gat.finish()

    tmaj = pltpu.VMEM((TM * NCB, 128), F32)
    text = pltpu.VMEM((TME * NCB, 128), F32)
    res = pl.pallas_call(
        body, name="seq_fwd", grid=(nt,),
        in_specs=_halo_specs(0, nt) + _halo_specs(1, nt) + _halo_specs(2, nt)
        + [pl.BlockSpec((32, D), lambda i: (0, 0)), pl.BlockSpec((1, D), lambda i: (0, 0))] + [_whole(a) for a in gat.arrays],
        out_specs=[pl.BlockSpec((TM, D), lambda i: (i, 0))] * 2 + [pl.BlockSpec(memory_space=pl.ANY)] * ng,
        out_shape=[jax.ShapeDtypeStruct((tp, D), F32), jax.ShapeDtypeStruct((tp, D), BF16)] + gat.out_shape,
        scratch_shapes=[text, text, tmaj, tmaj, pltpu.VMEM((32 * NCB, 128), F32), pltpu.VMEM((NCB, 128), F32),
                        pltpu.VMEM((TM, D), F32)] + gat.scratch,
        compiler_params=_params(("arbitrary",), 52),
    )(z, z, z, z, z, z, z, z, z, w_dw, b_dw, *gat.arrays)
    return res[:2], res[2:]


def _ln_stats(ac):
    mu = jnp.mean(ac, axis=-1, keepdims=True)
    xc = ac - mu
    rl = lax.rsqrt(jnp.mean(xc * xc, axis=-1, keepdims=True) + LN_EPS)
    return xc * rl, rl


def _pool_mix(m, wp_ref):
    return jnp.concatenate(
        [_dot(m[:, g * PG:(g + 1) * PG], wp_ref[:, g].reshape(PG, PG)) for g in range(4)], axis=1)


def _mix_fwd(ac, m, z, h0, b_gate, ln_g, ln_b, pool_scale, g_mixw, g_pool, gat):
    tp = h0.shape[0]
    tms = TM
    nt = tp // tms
    na, ng = len(gat.arrays), gat.n

    def body(*refs):
        ac_ref, m_ref, zga, zgb, h_ref, bg_ref, lg_ref, lb_ref, ps_ref, wm_hbm, wp_hbm = refs[:11]
        h1_ref, s_ref, mg_ref, q_ref = refs[11 + na:15 + na]
        wm, wp, sems = refs[15 + na + ng:18 + na + ng]
        gat.bind(refs[11:11 + na], refs[15 + na:15 + na + ng], refs[18 + na + ng:])
        i = pl.program_id(0)

        @pl.when(i == 0)
        def _():
            gat.issue()

        @pl.when(i == max(nt - 4, 0))
        def _():
            gat.forward()

        @pl.when(i == nt - 1)
        def _():
            gat.finish()

        _load_once(i == 0, [(wm_hbm, wm), (wp_hbm, wp)], sems)
        n, _ = _ln_stats(ac_ref[...])
        l = n * lg_ref[...] + lb_ref[...]
        s = (l * _sig(l)).astype(BF16)
        s_ref[...] = s
        yc = _dot(s, wm[:, 0].reshape(D, D))
        q = (_pool_mix(m_ref[...], wp) * ps_ref[...]).astype(BF16)
        q_ref[...] = q
        yp = _dot(q, wm[:, 1].reshape(D, D))
        ga = _sig(zga[...] + bg_ref[:, :D])
        gb = _sig(zgb[...] + bg_ref[:, D:])
        merged = (ga * yc + gb * yp).astype(BF16)
        mg_ref[...] = merged
        h1_ref[...] = h_ref[...] + _dot(merged, wm[:, 2].reshape(D, D))

    def tile(col=0):
        return pl.BlockSpec((tms, D), lambda i: (i, col))

    def vec(w):
        return pl.BlockSpec((1, w), lambda i: (0, 0))

    anys = pl.BlockSpec(memory_space=pl.ANY)
    f32o, b16o = jax.ShapeDtypeStruct((tp, D), F32), jax.ShapeDtypeStruct((tp, D), BF16)
    res = pl.pallas_call(
        body, name="mix_fwd", grid=(nt,),
        in_specs=[tile(), tile(), tile(3), tile(4), tile(), vec(2 * D), vec(D), vec(D), vec(D), anys, anys]
        + [_whole(a) for a in gat.arrays],
        out_specs=[tile()] * 4 + [anys] * ng,
        out_shape=[f32o, b16o, b16o, b16o] + gat.out_shape,
        scratch_shapes=[pltpu.VMEM((NDEV, 3, D // NDEV, D), BF16), pltpu.VMEM((NDEV, 4, PG // NDEV, PG), BF16),
                        pltpu.SemaphoreType.DMA((2,))] + gat.scratch,
        compiler_params=_params(("arbitrary",), 52),
    )(ac, m, z, z, h0, b_gate, ln_g, ln_b, pool_scale, g_mixw, g_pool, *gat.arrays)
    return res[:4], res[4:]


def _ffn_fwd(h1, tgt, g_ffn, g_final, w_gu, w_dn):
    tp = h1.shape[0]
    nt = tp // TM
    nx_last = tgt.shape[0] - (nt - 1) * TM

    def body(h_ref, t_ref, gf_ref, gl_ref, wgu_hbm, wdn_hbm,
             fg_ref, fu_ref, v_ref, f_ref, dh2_ref, acc_ref, wgu, wdn, v_sc, h2_sc, diff_sc, sems):
        i, j = pl.program_id(0), pl.program_id(1)
        _load_ffn(i, j, wgu_hbm, wgu, wdn_hbm, wdn, sems)

        @pl.when((i == 0) & (j == 0))
        def _():
            acc_ref[...] = jnp.zeros_like(acc_ref)

        @pl.when(j == 0)
        def _():
            h = h_ref[...]
            r = lax.rsqrt(jnp.mean(h * h, axis=-1, keepdims=True) + RMS_EPS)
            v = (h * r * gf_ref[...]).astype(BF16)
            v_sc[...] = v
            v_ref[...] = v
            h2_sc[...] = h

        v = v_sc[...]
        fg = _dot_nt(v, wgu[0, j])
        fu = _dot_nt(v, wgu[1, j])
        fg_ref[...] = fg
        fu_ref[...] = fu
        f = ((fg * _sig(fg)) * fu).astype(BF16)
        f_ref[...] = f
        h2_sc[...] += _dot(f, wdn[j])

        @pl.when(j == 1)
        def _():
            h2 = h2_sc[...]
            r = lax.rsqrt(jnp.mean(h2 * h2, axis=-1, keepdims=True) + RMS_EPS)
            n2 = h2 * r
            y = n2 * gl_ref[...]

            @pl.when(i < nt - 1)
            def _():
                diff_sc[...] = y - t_ref[...]

            @pl.when(i == nt - 1)
            def _():
                diff_sc[pl.ds(0, nx_last), :] = y[:nx_last] - t_ref[pl.ds(0, nx_last), :]
                diff_sc[pl.ds(nx_last, TM - nx_last), :] = jnp.zeros((TM - nx_last, D), F32)

            diff = diff_sc[...]
            dy = diff * (1.0 / D)
            acc_ref[0:1, :] += jnp.sum(diff * diff, axis=0, keepdims=True)
            acc_ref[1:2, :] += jnp.sum(dy * n2, axis=0, keepdims=True)
            dn = dy * gl_ref[...]
            dh2_ref[...] = r * (dn - n2 * jnp.mean(dn * n2, axis=-1, keepdims=True))

    def tile():
        return pl.BlockSpec((TM, D), lambda i, j: (i, 0))

    def chunk():
        return pl.BlockSpec((TM, FFC), lambda i, j: (i, j))

    def vec():
        return pl.BlockSpec((1, D), lambda i, j: (0, 0))

    anys = pl.BlockSpec(memory_space=pl.ANY)
    hid32, hid16 = jax.ShapeDtypeStruct((tp, DFF), F32), jax.ShapeDtypeStruct((tp, DFF), BF16)
    return pl.pallas_call(
        body, name="ffn_fwd", grid=(nt, 2),
        in_specs=[tile(), tile(), vec(), vec(), anys, anys],
        out_specs=[chunk(), chunk(), tile(), chunk(), tile(), pl.BlockSpec((8, D), lambda i, j: (0, 0))],
        out_shape=[hid32, hid32, jax.ShapeDtypeStruct((tp, D), BF16), hid16, jax.ShapeDtypeStruct((tp, D), F32),
                   jax.ShapeDtypeStruct((8, D), F32)],
        scratch_shapes=[pltpu.VMEM((2, 2, FFC, D), BF16), pltpu.VMEM((2, FFC, D), BF16),
                        pltpu.VMEM((TM, D), BF16), pltpu.VMEM((TM, D), F32), pltpu.VMEM((TM, D), F32),
                        pltpu.SemaphoreType.DMA((2 * NDEV + 2,))],
        compiler_params=_params(("arbitrary", "arbitrary"), 56),
    )(h1, tgt, g_ffn, g_final, w_gu, w_dn)


def _ffn_bwd(dh2, fg, fu, h1, g_ffn, w_gu, w_dn):
    tp = h1.shape[0]
    nt = tp // TM

    def body(dh2_ref, fg_ref, fu_ref, h_ref, gf_ref, wgu_hbm, wdn_hbm,
             dfg_ref, dfu_ref, dh1_ref, acc_ref, wgu, wdn, d_sc, dv_sc, sems):
        i, j = pl.program_id(0), pl.program_id(1)
        _load_ffn(i, j, wgu_hbm, wgu, wdn_hbm, wdn, sems)

        @pl.when((i == 0) & (j == 0))
        def _():
            acc_ref[...] = jnp.zeros_like(acc_ref)

        @pl.when(j == 0)
        def _():
            d_sc[...] = dh2_ref[...].astype(BF16)
            dv_sc[...] = jnp.zeros_like(dv_sc)

        df = _dot_nt(d_sc[...], wdn[j])
        fg = fg_ref[...]
        sg = _sig(fg)
        dfu = (df * (fg * sg)).astype(BF16)
        dfg = (df * fu_ref[...] * (sg * (1.0 + fg * (1.0 - sg)))).astype(BF16)
        dfg_ref[...] = dfg
        dfu_ref[...] = dfu
        dv_sc[...] += _dot(dfg, wgu[0, j]) + _dot(dfu, wgu[1, j])

        @pl.when(j == 1)
        def _():
            h = h_ref[...]
            r = lax.rsqrt(jnp.mean(h * h, axis=-1, keepdims=True) + RMS_EPS)
            n1 = h * r
            dv = dv_sc[...]
            acc_ref[0:1, :] += jnp.sum(dv * n1, axis=0, keepdims=True)
            dn = dv * gf_ref[...]
            dh1_ref[...] = dh2_ref[...] + r * (dn - n1 * jnp.mean(dn * n1, axis=-1, keepdims=True))

    def tile():
        return pl.BlockSpec((TM, D), lambda i, j: (i, 0))

    def chunk():
        return pl.BlockSpec((TM, FFC), lambda i, j: (i, j))

    anys = pl.BlockSpec(memory_space=pl.ANY)
    hid16 = jax.ShapeDtypeStruct((tp, DFF), BF16)
    return pl.pallas_call(
        body, name="ffn_bwd", grid=(nt, 2),
        in_specs=[tile(), chunk(), chunk(), tile(), pl.BlockSpec((1, D), lambda i, j: (0, 0)), anys, anys],
        out_specs=[chunk(), chunk(), tile(), pl.BlockSpec((8, D), lambda i, j: (0, 0))],
        out_shape=[hid16, hid16, jax.ShapeDtypeStruct((tp, D), F32), jax.ShapeDtypeStruct((8, D), F32)],
        scratch_shapes=[pltpu.VMEM((2, 2, FFC, D), BF16), pltpu.VMEM((2, FFC, D), BF16),
                        pltpu.VMEM((TM, D), BF16), pltpu.VMEM((TM, D), F32), pltpu.SemaphoreType.DMA((2 * NDEV + 2,))],
        compiler_params=_params(("arbitrary", "arbitrary"), 56),
    )(dh2, fg, fu, h1, g_ffn, w_gu, w_dn)


def _mix_bwd(dh1, z, s, q, ac, m, b_gate, ln_g, ln_b, pool_scale, g_mixw, g_pool, qs):
    tp = dh1.shape[0]
    nt = tp // TMS
    ex = _ChipExchange(qs)
    nq = ex.n

    def body(*refs):
        dh1_ref, zga, zgb, s_ref, q_ref, ac_ref, m_ref, bg_ref, lg_ref, lb_ref, ps_ref, wm_hbm, wp_hbm = refs[:13]
        dac_ref, dm_ref, dzg_ref, dyc_ref, dyp_ref, dm2_ref, acc_ref = refs[13 + nq:20 + nq]
        wm, wp, sems = refs[20 + 2 * nq:23 + 2 * nq]
        ex.bind(refs[13:13 + nq], refs[20 + nq:20 + 2 * nq], refs[23 + 2 * nq:])
        first = pl.program_id(0) == 0

        @pl.when(first)
        def _():
            ex.issue()
            acc_ref[...] = jnp.zeros_like(acc_ref)

        _load_once(first, [(wm_hbm, wm), (wp_hbm, wp)], sems)

        dmerged = _dot_nt(dh1_ref[...].astype(BF16), wm[:, 2].reshape(D, D))
        ga = _sig(zga[...] + bg_ref[:, :D])
        gb = _sig(zgb[...] + bg_ref[:, D:])
        dyc = dmerged * ga
        dyp = dmerged * gb
        dza = (dmerged * _dot(s_ref[...], wm[:, 0].reshape(D, D))) * (ga * (1.0 - ga))
        dzb = (dmerged * _dot(q_ref[...], wm[:, 1].reshape(D, D))) * (gb * (1.0 - gb))
        dzg_ref[:, :D] = dza.astype(BF16)
        dzg_ref[:, D:] = dzb.astype(BF16)
        acc_ref[0:1, :D] += jnp.sum(dza, axis=0, keepdims=True)
        acc_ref[0:1, D:] += jnp.sum(dzb, axis=0, keepdims=True)
        dyc_b = dyc.astype(BF16)
        dyp_b = dyp.astype(BF16)
        dyc_ref[...] = dyc_b
        dyp_ref[...] = dyp_b
        ds = _dot_nt(dyc_b, wm[:, 0].reshape(D, D))
        n, rl = _ln_stats(ac_ref[...])
        l = n * lg_ref[...] + lb_ref[...]
        sg = _sig(l)
        dl = ds * (sg * (1.0 + l * (1.0 - sg)))
        acc_ref[1:2, :D] += jnp.sum(dl * n, axis=0, keepdims=True)
        acc_ref[1:2, D:] += jnp.sum(dl, axis=0, keepdims=True)
        dn = dl * lg_ref[...]
        dac_ref[...] = rl * (dn - jnp.mean(dn, axis=-1, keepdims=True) - n * jnp.mean(dn * n, axis=-1, keepdims=True))
        dq = _dot_nt(dyp_b, wm[:, 1].reshape(D, D))
        mv = m_ref[...]
        acc_ref[2:3, :D] += jnp.sum(dq * _pool_mix(mv, wp), axis=0, keepdims=True)
        dm2 = (dq * ps_ref[...]).astype(BF16)
        dm2_ref[...] = dm2
        dm_ref[...] = jnp.concatenate(
            [_dot_nt(dm2[:, g * PG:(g + 1) * PG], wp[:, g].reshape(PG, PG)) for g in range(4)], axis=1)

        @pl.when(pl.program_id(0) == nt - 1)
        def _():
            ex.finish()

    def tile(col=0):
        return pl.BlockSpec((TMS, D), lambda i: (i, col))

    def vec(w):
        return pl.BlockSpec((1, w), lambda i: (0, 0))

    anys = pl.BlockSpec(memory_space=pl.ANY)
    f32o, b16o = jax.ShapeDtypeStruct((tp, D), F32), jax.ShapeDtypeStruct((tp, D), BF16)
    res = pl.pallas_call(
        body, name="mix_bwd", grid=(nt,),
        in_specs=[tile(), tile(3), tile(4), tile(), tile(), tile(), tile(), vec(2 * D), vec(D), vec(D), vec(D), anys, anys]
        + [anys] * nq,
        out_specs=[tile(), tile(), pl.BlockSpec((TMS, 2 * D), lambda i: (i, 0)), tile(), tile(), tile(),
                   pl.BlockSpec((8, 2 * D), lambda i: (0, 0))] + [anys] * nq,
        out_shape=[f32o, f32o, jax.ShapeDtypeStruct((tp, 2 * D), BF16), b16o, b16o, b16o,
                   jax.ShapeDtypeStruct((8, 2 * D), F32)] + ex.out_shape,
        scratch_shapes=[pltpu.VMEM((NDEV, 3, D // NDEV, D), BF16), pltpu.VMEM((NDEV, 4, PG // NDEV, PG), BF16),
                        pltpu.SemaphoreType.DMA((2,))] + ex.scratch,
        compiler_params=_params(("arbitrary",), 48),
    )(dh1, z, z, s, q, ac, m, b_gate, ln_g, ln_b, pool_scale, g_mixw, g_pool, *qs)
    return res[:7], res[7:]


def _seq_bwd(dac, dm, dzg, z, w_dw, seq, qs):
    tp = z.shape[0]
    nt = tp // TM
    ex = _ChipExchange(qs)
    nq = no = ex.n

    def body(*refs):
        dac_l, dac_c, dac_r, dm_l, dm_c, dm_r, av_l, av, av_r, ag_l, ag, ag_r, dzg_ref, w_ref = refs[:14]
        dz_ref, acc_ref = refs[14 + nq:16 + nq]
        a3, d3, m3, da3, dp3, w3, dw3, da_sc, dp_sc = refs[16 + nq + no:25 + nq + no]
        ex.bind(refs[14:14 + nq], refs[16 + nq:16 + nq + no], refs[25 + nq + no:])
        i = pl.program_id(0)
        sub = lax.broadcasted_iota(jnp.int32, (NCB, 128), 0)

        @pl.when(i == 0)
        def _():
            ex.issue()
            dw3[...] = jnp.zeros_like(dw3)
            _tm_fill(w3, 0, 4, lambda r, l: w_ref[pl.ds(r, 8), l])

        _tm_fill_ext(a3, (av_l, ag_l), (av, ag), (av_r, ag_r), lambda vg, r, l: vg[0][r, l] * _sig(vg[1][r, l]))
        _tm_fill_ext(d3, dac_l, dac_c, dac_r, lambda ref, r, l: ref[r, l])
        _tm_fill_ext(m3, dm_l, dm_c, dm_r, lambda ref, r, l: ref[r, l])

        def conv(g, c):
            dcur = [_tm_at(d3, 8 * g + t + HALO) for t in range(8)]
            accs = [None] * 8
            for k in range(CONV_K):
                wk = _tm_at(w3, k)
                prs = []
                for t in range(8):
                    term = wk * _tm_at(d3, 8 * g + t + CONV_K - k)
                    accs[t] = term if accs[t] is None else accs[t] + term
                    prs.append(dcur[t] * _tm_at(a3, 8 * g + t + k + 1))
                while len(prs) > 1:
                    prs = [prs[j] + prs[j + 1] for j in range(0, len(prs), 2)]
                dw3[_tm_rows(k), :] += prs[0]
            s = dcur[0]
            for t in range(1, 8):
                s = s + dcur[t]
            dw3[_tm_rows(CONV_K), :] += s
            for t in range(8):
                da3[_tm_rows(8 * g + t), :] = accs[t]
            return c

        lax.fori_loop(0, TM // 8, conv, 0)

        for b in _edge_rows(seq, tp):
            e = lax.rem(b - i * TM + HALO + tp, tp)

            @pl.when(e < TME)
            def _():
                m3[_tm_rows(e), :] = _tm_at(m3, e) * _edge_gain(b, seq, tp, sub)

        inv = _by_group(sub, [1.0 / w for w in POOL_WINDOWS])

        def pool(g, c):
            for t in range(8):
                e = 8 * g + t + HALO
                sums = _nested_windows(lambda o: _tm_at(m3, e + o), [w // 2 + 1 - w for w in POOL_WINDOWS])
                dp3[_tm_rows(8 * g + t), :] = _by_group(sub, sums) * inv
            return c

        lax.fori_loop(0, TM // 8, pool, 0)

        _tm_read(da3, TM // 8, lambda r, l, tile: da_sc.__setitem__((r, l), tile))
        _tm_read(dp3, TM // 8, lambda r, l, tile: dp_sc.__setitem__((r, l), tile))
        sg = _sig(ag[...])
        da = da_sc[...]
        dz_ref[:, 0:D] = (da * sg).astype(BF16)
        dz_ref[:, D:2 * D] = (da * av[...] * (sg * (1.0 - sg))).astype(BF16)
        dz_ref[:, 2 * D:3 * D] = (dp_sc[...] - dm_c[...]).astype(BF16)
        dz_ref[:, 3 * D:] = dzg_ref[...]

        @pl.when(i == nt - 1)
        def _():
            _tm_read(dw3, 4, lambda r, l, tile: acc_ref.__setitem__((r, l), tile))
            ex.finish()

    tmaj = pltpu.VMEM((TM * NCB, 128), F32)
    text = pltpu.VMEM((TME * NCB, 128), F32)
    taps = pltpu.VMEM((32 * NCB, 128), F32)
    anys = pl.BlockSpec(memory_space=pl.ANY)
    res = pl.pallas_call(
        body, name="seq_bwd", grid=(nt,),
        in_specs=_halo_specs(0, nt) + _halo_specs(0, nt) + _halo_specs(0, nt) + _halo_specs(1, nt)
        + [pl.BlockSpec((TM, 2 * D), lambda i: (i, 0)), pl.BlockSpec((32, D), lambda i: (0, 0))] + [anys] * nq,
        out_specs=[pl.BlockSpec((TM, DIN), lambda i: (i, 0)), pl.BlockSpec((32, D), lambda i: (0, 0))] + [anys] * no,
        out_shape=[jax.ShapeDtypeStruct((tp, DIN), BF16), jax.ShapeDtypeStruct((32, D), F32)] + ex.out_shape,
        scratch_shapes=[text, text, text, tmaj, tmaj, taps, taps, pltpu.VMEM((TM, D), F32), pltpu.VMEM((TM, D), F32)]
        + ex.scratch,
        compiler_params=_params(("arbitrary",), 48),
    )(dac, dac, dac, dm, dm, dm, z, z, z, z, z, z, dzg, w_dw, *qs)
    return res[:2], res[2:]


def _in_bwd(dz, h0, dh1, g_mix, w_g, seq, qs):
    tp = h0.shape[0]
    tm = _pick(tp, TM_IO)
    nt = tp // tm
    ex = _ChipExchange(qs)
    nq = no = ex.n

    def body(*refs):
        dz_ref, h_ref, dh1_ref, g_ref, w_hbm = refs[:5]
        gx_ref, gmeta_ref, acc_ref = refs[5 + nq:8 + nq]
        w_vm, sems = refs[8 + nq + no:10 + nq + no]
        ex.bind(refs[5:5 + nq], refs[8 + nq:8 + nq + no], refs[10 + nq + no:])
        i = pl.program_id(0)

        @pl.when(i == 0)
        def _():
            ex.issue()
            acc_ref[...] = jnp.zeros_like(acc_ref)

        _load_once(i == 0, _win_pairs(w_hbm, w_vm), sems)

        du = _dot_nt(dz_ref[:, :DIN // 2], w_vm[0]) + _dot_nt(dz_ref[:, DIN // 2:], w_vm[1])
        h = h_ref[...]
        r = lax.rsqrt(jnp.mean(h * h, axis=-1, keepdims=True) + RMS_EPS)
        n0 = h * r
        acc_ref[0:1, :] += jnp.sum(du * n0, axis=0, keepdims=True)
        dn = du * g_ref[...]
        gx_ref[...] = dh1_ref[...] + r * (dn - n0 * jnp.mean(dn * n0, axis=-1, keepdims=True))

        @pl.when(i == nt - 1)
        def _():
            gmeta_ref[...] = gx_ref[pl.ds(tm - N_META, N_META), :]
            ex.finish()

    tile = pl.BlockSpec((tm, D), lambda i: (i, 0))
    anys = pl.BlockSpec(memory_space=pl.ANY)
    res = pl.pallas_call(
        body, name="in_bwd", grid=(nt,),
        in_specs=[pl.BlockSpec((tm, DIN), lambda i: (i, 0)), tile, tile, pl.BlockSpec((1, D), lambda i: (0, 0)), anys]
        + [anys] * nq,
        out_specs=[tile, pl.BlockSpec((N_META, D), lambda i: (0, 0)), pl.BlockSpec((8, D), lambda i: (0, 0))] + [anys] * no,
        out_shape=[jax.ShapeDtypeStruct((seq, D), F32), jax.ShapeDtypeStruct((N_META, D), F32),
                   jax.ShapeDtypeStruct((8, D), F32)] + ex.out_shape,
        scratch_shapes=[pltpu.VMEM((2, D, DIN // 2), BF16), pltpu.SemaphoreType.DMA((NDEV,))] + ex.scratch,
        compiler_params=_params(("arbitrary",), 58),
    )(dz, h0, dh1, g_mix, w_g, *qs)
    return res[:3], res[3:]


def _wgrad_in(u, dz):
    tp = u.shape[0]
    tm = _pick(tp, TM_WG)
    nt = tp // tm
    half = DIN // 2

    def body(u_ref, dz_ref, o_ref, acc):
        t = pl.program_id(1)

        @pl.when(t == 0)
        def _():
            acc[...] = jnp.zeros_like(acc)

        acc[...] += _dot_tn(u_ref[...], dz_ref[...])

        @pl.when(t == nt - 1)
        def _():
            for d in range(4):
                o_ref[d] = acc[:, INB * d:INB * (d + 1)].astype(BF16)

    return pl.pallas_call(
        body, name="wgrad_in", grid=(2, nt),
        in_specs=[pl.BlockSpec((tm, D), lambda h, t: (t, 0)), pl.BlockSpec((tm, half), lambda h, t: (t, h))],
        out_specs=pl.BlockSpec((4, D, INB), lambda h, t: (h, 0, 0), pipeline_mode=pl.Buffered(1)),
        out_shape=jax.ShapeDtypeStruct((NDEV, D, INB), BF16),
        scratch_shapes=[pltpu.VMEM((D, half), F32)],
        compiler_params=_params(("arbitrary", "arbitrary"), 52),
    )(u, dz)


def _wgrad_mix(s, dyc, q, dyp, merged, dh1, m, dm2, qs):
    tp = s.shape[0]
    tm = _pick(tp, TM_WM)
    nt = tp // tm
    rb = D // NDEV
    ex = _ChipExchange(qs)
    nq = ex.n

    def body(*refs):
        s_ref, dyc_ref, q_ref, dyp_ref, mg_ref, dh1_ref, m_ref, dm2_ref = refs[:8]
        o_ref, op_ref = refs[8 + nq:10 + nq]
        acc, accp = refs[10 + 2 * nq:12 + 2 * nq]
        ex.bind(refs[8:8 + nq], refs[10 + nq:10 + 2 * nq], refs[12 + 2 * nq:])
        t = pl.program_id(0)

        @pl.when(t == 0)
        def _():
            ex.issue()
            acc[...] = jnp.zeros_like(acc)
            accp[...] = jnp.zeros_like(accp)

        acc[0] += _dot_tn(s_ref[...], dyc_ref[...])
        acc[1] += _dot_tn(q_ref[...], dyp_ref[...])
        acc[2] += _dot_tn(mg_ref[...], dh1_ref[...].astype(BF16))
        for g in range(4):
            accp[g] += _dot_tn(m_ref[:, g * PG:(g + 1) * PG], dm2_ref[:, g * PG:(g + 1) * PG])

        @pl.when(t == nt - 1)
        def _():
            for d in range(NDEV):
                for k in range(3):
                    o_ref[d, k] = acc[k, rb * d:rb * (d + 1), :].astype(BF16)
                for g in range(4):
                    op_ref[d, g] = accp[g, 32 * d:32 * (d + 1), :].astype(BF16)
            ex.finish()

    tile = pl.BlockSpec((tm, D), lambda t: (t, 0))
    anys = pl.BlockSpec(memory_space=pl.ANY)
    res = pl.pallas_call(
        body, name="wgrad_mix", grid=(nt,),
        in_specs=[tile] * 8 + [anys] * nq,
        out_specs=[pl.BlockSpec((NDEV, 3, rb, D), lambda t: (0, 0, 0, 0), pipeline_mode=pl.Buffered(1)),
                   pl.BlockSpec((NDEV, 4, 32, PG), lambda t: (0, 0, 0, 0), pipeline_mode=pl.Buffered(1))] + [anys] * nq,
        out_shape=[jax.ShapeDtypeStruct((NDEV, 3, rb, D), BF16), jax.ShapeDtypeStruct((NDEV, 4, 32, PG), BF16)]
        + ex.out_shape,
        scratch_shapes=[pltpu.VMEM((3, D, D), F32), pltpu.VMEM((4, PG, PG), F32)] + ex.scratch,
        compiler_params=_params(("arbitrary",), 56),
    )(s, dyc, q, dyp, merged, dh1, m, dm2, *qs)
    return res[:2], res[2:]


def _wgrad_gu(v, dfg, dfu):
    tp = v.shape[0]
    tm = _pick(tp, TM_WG)
    nt = tp // tm

    def body(v_ref, dg_ref, du_ref, o_ref, acc):
        k, t = pl.program_id(0), pl.program_id(2)

        @pl.when(t == 0)
        def _():
            acc[...] = jnp.zeros_like(acc)

        @pl.when(k == 0)
        def _():
            acc[...] += _dot_tn(dg_ref[...], v_ref[...])

        @pl.when(k == 1)
        def _():
            acc[...] += _dot_tn(du_ref[...], v_ref[...])

        @pl.when(t == nt - 1)
        def _():
            for d in range(4):
                o_ref[d] = acc[FFB * d:FFB * (d + 1), :].astype(BF16)

    return pl.pallas_call(
        body, name="wgrad_gu", grid=(2, 2, nt),
        in_specs=[pl.BlockSpec((tm, D), lambda k, h, t: (t, 0)),
                  pl.BlockSpec((tm, FFC), lambda k, h, t: (t * (1 - k), h * (1 - k))),
                  pl.BlockSpec((tm, FFC), lambda k, h, t: (t * k, h * k))],
        out_specs=pl.BlockSpec((4, None, FFB, D), lambda k, h, t: (h, k, 0, 0), pipeline_mode=pl.Buffered(1)),
        out_shape=jax.ShapeDtypeStruct((NDEV, 2, FFB, D), BF16),
        scratch_shapes=[pltpu.VMEM((FFC, D), F32)],
        compiler_params=_params(("arbitrary",) * 3, 48),
    )(v, dfg, dfu)


def _wgrad_down(f, dh2):
    tp = f.shape[0]
    tm = _pick(tp, TM_WG)
    nt = tp // tm

    def body(f_ref, d_ref, o_ref, acc):
        t = pl.program_id(1)

        @pl.when(t == 0)
        def _():
            acc[...] = jnp.zeros_like(acc)

        acc[...] += _dot_tn(f_ref[...], d_ref[...].astype(BF16))

        @pl.when(t == nt - 1)
        def _():
            for d in range(4):
                o_ref[d] = acc[FFB * d:FFB * (d + 1), :].astype(BF16)

    return pl.pallas_call(
        body, name="wgrad_down", grid=(2, nt),
        in_specs=[pl.BlockSpec((tm, FFC), lambda h, t: (t, h)), pl.BlockSpec((tm, D), lambda h, t: (t, 0))],
        out_specs=pl.BlockSpec((4, FFB, D), lambda h, t: (h, 0, 0), pipeline_mode=pl.Buffered(1)),
        out_shape=jax.ShapeDtypeStruct((NDEV, FFB, D), BF16),
        scratch_shapes=[pltpu.VMEM((FFC, D), F32)],
        compiler_params=_params(("arbitrary", "arbitrary"), 48),
    )(f, dh2)


def kernel(x, meta_tokens, g_mix, w_in, b_gate, w_dw, b_dw, ln_g, ln_b, w_conv_out, w_pool, pool_scale, w_pool_out, w_o, g_ffn, w_ffn_gate, w_ffn_up, w_ffn_down, g_final, loss_target, m_meta_tokens, m_g_mix, m_w_in, m_b_gate, m_w_dw, m_b_dw, m_ln_g, m_ln_b, m_w_conv_out, m_w_pool, m_pool_scale, m_w_pool_out, m_w_o, m_g_ffn, m_w_ffn_gate, m_w_ffn_up, m_w_ffn_down, m_g_final, v_meta_tokens, v_g_mix, v_w_in, v_b_gate, v_w_dw, v_b_dw, v_ln_g, v_ln_b, v_w_conv_out, v_w_pool, v_pool_scale, v_w_pool_out, v_w_o, v_g_ffn, v_w_ffn_gate, v_w_ffn_up, v_w_ffn_down, v_g_final):
    seq = x.shape[1]
    tp = -(-(seq + 2 * HALO) // TM) * TM
    tm_in = _pick(tp, TM_IO)
    nx_last = seq - (tp // tm_in - 1) * tm_in
    assert 0 < nx_last <= tm_in - 2 * HALO and nx_last % 8 == 0 and 0 < seq - (tp // TM - 1) * TM

    whole = (Ellipsis,)
    ag_small = _Gather(
        [((48, D // NDEV), [(meta_tokens, pl.ds(0, N_META), whole), (w_dw, pl.ds(N_META, CONV_K), 0)])], [F32])
    ag_mix = _Gather([((3, D // NDEV, D), [(w_conv_out, 0, 0), (w_pool_out, 1, 0), (w_o, 2, 0)]),
                      ((4, PG // NDEV, PG), [(w_pool, whole, 0)])], [BF16, BF16])
    def tr(a):
        return jnp.swapaxes(a, 1, 2)

    ag_gu = _Gather([((2, FFB, D), [(tr(w_ffn_gate), 0, 0), (tr(w_ffn_up), 1, 0)])], [BF16])
    ag_dn = _Gather([((FFB, D), [(w_ffn_down, whole, 0)])], [BF16])

    mx, my = lax.axis_index("x"), lax.axis_index("y")
    order = jnp.stack([2 * mx + my, 2 * mx + 1 - my, 2 * (1 - mx) + my, 2 * (1 - mx) + 1 - my]).astype(jnp.int32)
    (h0, z, u, g_in), (g_mixw, g_pool), g_small = _fwd_in(x[0], g_mix, w_in, order, tp, ag_mix, ag_small)
    wdw_full = g_small.transpose(1, 0, 2).reshape(48, D)[N_META:]
    (ac, m), (w_gu,) = _seq_fwd(z, wdw_full, b_dw, seq, ag_gu)
    (h1, s, merged, q), (g_down,) = _mix_fwd(ac, m, z, h0, b_gate, ln_g, ln_b, pool_scale, g_mixw, g_pool, ag_dn)
    w_dn = g_down.reshape(2, FFC, D)
    fg, fu, v, f, dh2, head_acc = _ffn_fwd(h1, loss_target[0], g_ffn, g_final.reshape(1, D), w_gu, w_dn)

    dfg, dfu, dh1, ffn_acc = _ffn_bwd(dh2, fg, fu, h1, g_ffn, w_gu, w_dn)
    own_f, sib_f, q_f = _rs_pair("rs_pair_ffn", [_wgrad_gu(v, dfg, dfu), _wgrad_down(f, dh2)])
    (dac, dm, dzg, dyc, dyp, dm2, mix_acc), rel_gu = _mix_bwd(
        dh1, z, s, q, ac, m, b_gate, ln_g, ln_b, pool_scale, g_mixw, g_pool, q_f[:1])
    p_mix, rel_dn = _wgrad_mix(s, dyc, q, dyp, merged, dh1, m, dm2, q_f[1:])
    rel_f = [rel_gu[0], rel_dn[0]]
    own_m, sib_m, q_m = _rs_pair("rs_pair_mix", list(p_mix))
    (dz, seq_acc), rel_m = _seq_bwd(dac, dm, dzg, z, wdw_full, seq, q_m)
    own_i, sib_i, q_i = _rs_pair("rs_pair_in", [_wgrad_in(u, dz)])
    (grad_x, g_meta, in_acc), rel_i = _in_bwd(dz, h0, dh1, g_mix, g_in, seq, q_i)
    small_g = jnp.concatenate([g_meta, seq_acc[:CONV_K], jnp.zeros((1, D), F32)], axis=0)
    p_small = small_g.reshape(48, NDEV, D // NDEV).transpose(1, 0, 2).astype(BF16)
    rep_g = jnp.concatenate([
        in_acc[0:1], mix_acc[0:1, :D], mix_acc[0:1, D:], seq_acc[CONV_K:CONV_K + 1], mix_acc[1:2, :D], mix_acc[1:2, D:],
        mix_acc[2:3, :D], ffn_acc[0:1], head_acc[1:2], head_acc[0:1], jnp.zeros((REP_ROWS - 10, D), F32)], axis=0)
    own_s, sib_s, rel_s, rep_all = _reduce_scatter([p_small], rep_g)
    owns = [own_i[0], own_s[0], own_m[0], own_m[1], own_f[0], own_f[1]]
    sibs = [sib_i[0], sib_s[0], sib_m[0], sib_m[1], sib_f[0], sib_f[1]]
    rels = [rel_i[0], rel_s[0], rel_m[0], rel_m[1], rel_f[0], rel_f[1]]

    def lead(a):
        return a.reshape(1, *a.shape)

    def stack4(a, lead_dims):
        return a.reshape(*lead_dims, 1, 4 * 32, PG)

    (r_in,) = _adamw_multi("adamw_in", lead(owns[0]), sibs[0][:, None], rels[0][:, None], [w_in], [m_w_in], [v_w_in], 4)
    r_meta, r_dw = _adamw_meta_dw(owns[1], sibs[1], rels[1], (meta_tokens, m_meta_tokens, v_meta_tokens),
                                  (w_dw, m_w_dw, v_w_dw))
    r_conv, r_pout, r_o = _adamw_multi("adamw_mix", owns[2], sibs[2], rels[2], [w_conv_out, w_pool_out, w_o],
                                       [m_w_conv_out, m_w_pool_out, m_w_o], [v_w_conv_out, v_w_pool_out, v_w_o], 1)
    (r_pool,) = _adamw_multi("adamw_pool", stack4(owns[3], ()), stack4(sibs[3], (1,)), stack4(rels[3], (3,)),
                             [w_pool.reshape(1, 128, PG)], [m_w_pool.reshape(1, 128, PG)], [v_w_pool.reshape(1, 128, PG)], 1)
    r_pool = tuple(a.reshape(w_pool.shape) for a in r_pool)
    r_gate, r_up = _adamw_multi("adamw_gu", owns[4], sibs[4], rels[4], [tr(w_ffn_gate), tr(w_ffn_up)],
                                [tr(m_w_ffn_gate), tr(m_w_ffn_up)], [tr(v_w_ffn_gate), tr(v_w_ffn_up)], 2)
    r_gate, r_up = tuple(tr(a) for a in r_gate), tuple(tr(a) for a in r_up)
    (r_down,) = _adamw_multi("adamw_down", lead(owns[5]), sibs[5][:, None], rels[5][:, None],
                             [w_ffn_down], [m_w_ffn_down], [v_w_ffn_down], 2)
    row = (1, D)
    loss, reps = _adamw_rep(
        rep_all,
        [g_mix, b_gate, b_dw, ln_g, ln_b, pool_scale, g_ffn, g_final.reshape(row)],
        [m_g_mix, m_b_gate, m_b_dw, m_ln_g, m_ln_b, m_pool_scale, m_g_ffn, m_g_final.reshape(row)],
        [v_g_mix, v_b_gate, v_b_dw, v_ln_g, v_ln_b, v_pool_scale, v_g_ffn, v_g_final.reshape(row)])
    r_gmix, r_bg, r_bdw, r_lg, r_lb, r_ps, r_gffn, r_gfin = reps
    r_gfin = tuple(a.reshape(D) for a in r_gfin)

    in_order = [r_meta, r_gmix, r_in, r_bg, r_dw, r_bdw, r_lg, r_lb, r_conv, r_pool, r_ps, r_pout, r_o, r_gffn,
                r_gate, r_up, r_down, r_gfin]
    return (loss.reshape(()), grad_x[None], *[r[0] for r in in_order], *[r[1] for r in in_order],
            *[r[2] for r in in_order], *[r[3] for r in in_order])
```

```python
import math

import jax
import jax.numpy as jnp
from jax import lax
from jax.experimental import pallas as pl
from jax.experimental.pallas import tpu as pltpu

F32, BF16 = jnp.float32, jnp.bfloat16
MESH_ID = pl.DeviceIdType.MESH
NDEV = 8

D = 1024
N_META = 16
CONV_K = 31
HALO = 16
POOL_WINDOWS = (2, 4, 8, 16)
PG = 256
DIN = 5 * D
DFF = 2816
FFB = DFF // NDEV
FFC = DFF // 2
INB = DIN // NDEV
RMS_EPS = 1e-6
LN_EPS = 1e-5
ADAM_LR, ADAM_B1, ADAM_B2, ADAM_EPS, ADAM_WD, ADAM_STEP = 0.001, 0.9, 0.999, 1e-08, 0.01, 10

TM = 384
TMS = 384
TM_IO = 704
TM_WG = 1408
TM_WM = 704
MIB = 2 ** 20


def _sig(x):
    return 0.5 * jnp.tanh(0.5 * x) + 0.5


def _dot(a, b):
    return jnp.dot(a, b, preferred_element_type=F32)


def _dot_nt(a, b):
    return lax.dot_general(a, b, (((1,), (1,)), ((), ())), preferred_element_type=F32)


def _dot_tn(a, b):
    return lax.dot_general(a, b, (((0,), (0,)), ((), ())), preferred_element_type=F32)


def _pick(tp, pref):
    return pref if tp % pref == 0 else TM


def _params(sem, vmem_mib):
    return pltpu.CompilerParams(dimension_semantics=sem, vmem_limit_bytes=vmem_mib * MIB)


def _load_once(first, pairs, sems):
    @pl.when(first)
    def _():
        cps = [pltpu.make_async_copy(s, d, sems.at[k]) for k, (s, d) in enumerate(pairs)]
        for cp in cps:
            cp.start()
        for cp in cps:
            cp.wait()


def _place():
    x, y, c = lax.axis_index("x"), lax.axis_index("y"), lax.axis_index("c")
    return x, y, c


class _Gather:
    def __init__(self, groups, dtypes):
        self.groups, self.dtypes, self.n = groups, dtypes, len(groups)
        self.arrays = [a for _, parts in groups for a, _, _ in parts]
        self.out_shape = [jax.ShapeDtypeStruct((NDEV, *s), dt) for (s, _), dt in zip(groups, dtypes)]
        self.scratch = [pltpu.VMEM(s, dt) for (s, _), dt in zip(groups, dtypes)] + [
            pltpu.SemaphoreType.DMA((7 * self.n,)), pltpu.SemaphoreType.DMA((7 * self.n,)),
            pltpu.SemaphoreType.DMA((self.n,))]

    def bind(self, ins, outs, scratch):
        self.ins, self.outs, self.stages = ins, outs, scratch[:self.n]
        self.send_sems, self.recv_sems, self.local_sems = scratch[self.n:]
        return self

    def _copy(self, w, k, block, to, src=None):
        dst = self.outs[w].at[4 * block[0] + 2 * block[1] + block[2]]
        return pltpu.make_async_remote_copy(
            src_ref=dst if src is None else src, dst_ref=dst,
            send_sem=self.send_sems.at[7 * w + k], recv_sem=self.recv_sems.at[7 * w + k],
            device_id=to, device_id_type=MESH_ID)

    def _first(self):
        x, y, c = _place()
        me, sibling = (x, y, c), (x, y, 1 - c)
        chips = [(1 - x, y), (x, 1 - y), (1 - x, 1 - y)]
        mine, first = [], []
        for w in range(self.n):
            mine.append(pltpu.make_async_copy(self.stages[w], self.outs[w].at[4 * x + 2 * y + c], self.local_sems.at[w]))
            first.append(self._copy(w, 0, me, sibling, src=self.stages[w]))
            first += [self._copy(w, 1 + j, me, (*chip, c), src=self.stages[w]) for j, chip in enumerate(chips)]
        return mine, first

    def _passed(self):
        x, y, c = _place()
        chips = [(1 - x, y), (x, 1 - y), (1 - x, 1 - y)]
        return [self._copy(w, 4 + j, (*chip, c), (x, y, 1 - c)) for w in range(self.n) for j, chip in enumerate(chips)]

    def issue(self):
        a = 0
        for w in range(self.n):
            shape, parts = self.groups[w]
            if sum(arr.size for arr, _, _ in parts) < math.prod(shape):
                self.stages[w][...] = jnp.zeros(shape, self.dtypes[w])
            for _, dst, src in parts:
                self.stages[w][dst] = self.ins[a][src].astype(self.dtypes[w])
                a += 1
        mine, first = self._first()
        for cp in mine + first:
            cp.start()

    def forward(self):
        x, y, c = _place()
        chips = [(1 - x, y), (x, 1 - y), (1 - x, 1 - y)]
        passed = self._passed()
        for w in range(self.n):
            for j, chip in enumerate(chips):
                self._copy(w, 1 + j, (*chip, c), (x, y, c)).wait_recv()
                passed[3 * w + j].start()

    def finish(self):
        x, y, c = _place()
        chips = [(1 - x, y), (x, 1 - y), (1 - x, 1 - y)]
        for w in range(self.n):
            self._copy(w, 0, (x, y, 1 - c), (x, y, c)).wait_recv()
            for j, chip in enumerate(chips):
                self._copy(w, 4 + j, (*chip, 1 - c), (x, y, c)).wait_recv()
        mine, first = self._first()
        for cp in first + self._passed():
            cp.wait_send()
        for cp in mine:
            cp.wait()


class _ChipExchange:
    def __init__(self, qs):
        self.n = len(qs)
        self.out_shape = [jax.ShapeDtypeStruct(q.shape, q.dtype) for q in qs]
        self.scratch = [pltpu.SemaphoreType.DMA((3 * self.n,)), pltpu.SemaphoreType.DMA((3 * self.n,))]

    def bind(self, qs, rels, scratch):
        self.qs, self.rels = qs, rels
        self.send_sems, self.recv_sems = scratch
        return self

    def _copies(self):
        x, y, c = _place()
        chips = [(1 - x, y), (x, 1 - y), (1 - x, 1 - y)]
        return [pltpu.make_async_remote_copy(
            src_ref=self.qs[w].at[j], dst_ref=self.rels[w].at[j],
            send_sem=self.send_sems.at[3 * w + j], recv_sem=self.recv_sems.at[3 * w + j],
            device_id=(*chips[j], c), device_id_type=MESH_ID) for w in range(self.n) for j in range(3)]

    def issue(self):
        for cp in self._copies():
            cp.start()

    def finish(self):
        cps = self._copies()
        for cp in cps:
            cp.wait_recv()
        for cp in cps:
            cp.wait_send()


def _reduce_scatter(parts, small):
    n = len(parts)
    blks = [p.shape[1:] for p in parts]

    def body(*refs):
        ps, small_ref = refs[:n], refs[n]
        o = n + 1
        owns, sibs, rels, small_out = refs[o:o + n], refs[o + n:o + 2 * n], refs[o + 2 * n:o + 3 * n], refs[o + 3 * n]
        o += 3 * n + 1
        pa, pb, qst = refs[o:o + n], refs[o + n:o + 2 * n], refs[o + 2 * n:o + 3 * n]
        s1_send, s1_recv, s2_send, s2_recv, sm_send, sm_recv, lsem = refs[o + 3 * n:]
        x, y, c = _place()
        me = 4 * x + 2 * y + c
        sibling = (x, y, 1 - c)
        chips = [(1 - x, y), (x, 1 - y), (1 - x, 1 - y)]
        all_chips = [(x, y)] + chips

        own_cps = []
        for w in range(n):
            cp = pltpu.make_async_copy(ps[w].at[me], owns[w], lsem.at[w])
            cp.start()
            own_cps.append(cp)
        sm_own = pltpu.make_async_copy(small_ref, small_out.at[me], lsem.at[n])
        sm_own.start()

        def small_copy(r):
            peer = ((x + (r >> 2)) % 2, (y + ((r >> 1) & 1)) % 2, (c + (r & 1)) % 2)
            return pltpu.make_async_remote_copy(
                src_ref=small_ref, dst_ref=small_out.at[me], send_sem=sm_send.at[r - 1], recv_sem=sm_recv.at[r - 1],
                device_id=peer, device_id_type=MESH_ID)

        sm_cps = [small_copy(r) for r in range(1, NDEV)]
        for cp in sm_cps:
            cp.start()

        def pair_copy(w, rel):
            cx, cy = all_chips[rel]
            return pltpu.make_async_remote_copy(
                src_ref=ps[w].at[4 * cx + 2 * cy + (1 - c)], dst_ref=sibs[w].at[rel],
                send_sem=s1_send.at[4 * w + rel], recv_sem=s1_recv.at[4 * w + rel],
                device_id=sibling, device_id_type=MESH_ID)

        def chip_copy(w, j):
            return pltpu.make_async_remote_copy(
                src_ref=qst[w].at[j], dst_ref=rels[w].at[j],
                send_sem=s2_send.at[3 * w + j], recv_sem=s2_recv.at[3 * w + j],
                device_id=(*chips[j], c), device_id_type=MESH_ID)

        pair_cps = [pair_copy(w, rel) for w in range(n) for rel in (1, 2, 3, 0)]
        for cp in pair_cps:
            cp.start()
        chip_cps = []
        for w in range(n):
            for j, (cx, cy) in enumerate(chips):
                pair_copy(w, 1 + j).wait_recv()
                la = pltpu.make_async_copy(ps[w].at[4 * cx + 2 * cy + c], pa[w], lsem.at[n + 1])
                lb = pltpu.make_async_copy(sibs[w].at[1 + j], pb[w], lsem.at[n + 2])
                la.start()
                lb.start()
                la.wait()
                lb.wait()
                qst[w][j] = (pa[w][...].astype(F32) + pb[w][...].astype(F32)).astype(BF16)
                cp = chip_copy(w, j)
                cp.start()
                chip_cps.append(cp)
        for w in range(n):
            pair_copy(w, 0).wait_recv()
            for j in range(3):
                chip_copy(w, j).wait_recv()
        for cp in sm_cps:
            cp.wait_recv()
        for cp in pair_cps + chip_cps + sm_cps:
            cp.wait_send()
        for cp in own_cps:
            cp.wait()
        sm_own.wait()

    any_spec = pl.BlockSpec(memory_space=pl.ANY)
    outs = pl.pallas_call(
        body, name="rs_grads",
        out_shape=[jax.ShapeDtypeStruct(b, BF16) for b in blks]
        + [jax.ShapeDtypeStruct((4, *b), BF16) for b in blks]
        + [jax.ShapeDtypeStruct((3, *b), BF16) for b in blks]
        + [jax.ShapeDtypeStruct((NDEV, *small.shape), F32)],
        in_specs=[any_spec] * (n + 1),
        out_specs=[any_spec] * (3 * n + 1),
        scratch_shapes=[pltpu.VMEM(b, BF16) for b in blks] + [pltpu.VMEM(b, BF16) for b in blks]
        + [pltpu.VMEM((3, *b), BF16) for b in blks]
        + [pltpu.SemaphoreType.DMA((4 * n,)), pltpu.SemaphoreType.DMA((4 * n,)),
           pltpu.SemaphoreType.DMA((3 * n,)), pltpu.SemaphoreType.DMA((3 * n,)),
           pltpu.SemaphoreType.DMA((NDEV - 1,)), pltpu.SemaphoreType.DMA((NDEV - 1,)),
           pltpu.SemaphoreType.DMA((n + 3,))],
        compiler_params=pltpu.CompilerParams(vmem_limit_bytes=40 * MIB),
    )(*parts, small)
    return outs[:n], outs[n:2 * n], outs[2 * n:3 * n], outs[3 * n]


class _PairSum:
    def __init__(self, parts, keep_q=True):
        self.n = n = len(parts)
        self.keep_q = keep_q
        blks = [p.shape[1:] for p in parts]
        self.out_shape = [jax.ShapeDtypeStruct(b, BF16) for b in blks] + [jax.ShapeDtypeStruct((1, *b), BF16) for b in blks]
        if keep_q:
            self.out_shape += [jax.ShapeDtypeStruct((3, *b), BF16) for b in blks]
        self.scratch = [pltpu.VMEM((3, *b), BF16) for b in blks] * 3 + [
            pltpu.SemaphoreType.DMA((4 * n,)), pltpu.SemaphoreType.DMA((4 * n,)), pltpu.SemaphoreType.DMA((5 * n,))]

    def bind(self, ps, outs, scratch):
        n = self.n
        self.ps, self.owns, self.sibs, self.qs = ps, outs[:n], outs[n:2 * n], outs[2 * n:]
        self.pa, self.pb, self.qst = scratch[:n], scratch[n:2 * n], scratch[2 * n:3 * n]
        self.s_send, self.s_recv, self.lsem = scratch[3 * n:]
        return self

    def _local(self, with_q):
        n = self.n
        x, y, c = _place()
        chips = [(1 - x, y), (x, 1 - y), (1 - x, 1 - y)]
        own = [pltpu.make_async_copy(self.ps[w].at[4 * x + 2 * y + c], self.owns[w], self.lsem.at[w]) for w in range(n)]
        mine = [[pltpu.make_async_copy(self.ps[w].at[4 * cx + 2 * cy + c], self.pa[w].at[j], self.lsem.at[2 * n + 3 * w + j])
                 for j, (cx, cy) in enumerate(chips)] for w in range(n)]
        outq = [pltpu.make_async_copy(self.qst[w], self.qs[w], self.lsem.at[n + w]) for w in range(n)] if with_q else []
        return own, mine, outq

    def _pair(self, w, rel):
        x, y, c = _place()
        cx, cy = [(x, y), (1 - x, y), (x, 1 - y), (1 - x, 1 - y)][rel]
        return pltpu.make_async_remote_copy(
            src_ref=self.ps[w].at[4 * cx + 2 * cy + (1 - c)],
            dst_ref=self.sibs[w].at[0] if rel == 0 else self.pb[w].at[rel - 1],
            send_sem=self.s_send.at[4 * w + rel], recv_sem=self.s_recv.at[4 * w + rel],
            device_id=(x, y, 1 - c), device_id_type=MESH_ID)

    def issue(self):
        own, mine, _ = self._local(False)
        for cp in own + [cp for row in mine for cp in row]:
            cp.start()
        for w in range(self.n):
            for rel in (1, 2, 3, 0):
                self._pair(w, rel).start()

    def finish(self):
        own, mine, outq = self._local(self.keep_q)
        for w in range(self.n):
            for j in range(3):
                self._pair(w, 1 + j).wait_recv()
                mine[w][j].wait()
                self.qst[w][j] = (self.pa[w][j].astype(F32) + self.pb[w][j].astype(F32)).astype(BF16)
            if self.keep_q:
                outq[w].start()
        for w in range(self.n):
            self._pair(w, 0).wait_recv()
        for w in range(self.n):
            for rel in range(4):
                self._pair(w, rel).wait_send()
        for cp in own + outq:
            cp.wait()

    def results(self, outs):
        n = self.n
        return outs[:n], outs[n:2 * n], outs[2 * n:3 * n]


def _rs_pair(name, parts):
    ps = _PairSum(parts)
    n = ps.n

    def body(*refs):
        ps.bind(refs[:n], refs[n:4 * n], refs[4 * n:])
        ps.issue()
        ps.finish()

    any_spec = pl.BlockSpec(memory_space=pl.ANY)
    outs = pl.pallas_call(
        body, name=name, out_shape=ps.out_shape,
        in_specs=[any_spec] * n, out_specs=[any_spec] * (3 * n), scratch_shapes=ps.scratch,
        compiler_params=pltpu.CompilerParams(vmem_limit_bytes=48 * MIB),
    )(*parts)
    return ps.results(outs)


def _adamw_math(g, w, m, v):
    m = ADAM_B1 * m + (1.0 - ADAM_B1) * g
    v = ADAM_B2 * v + (1.0 - ADAM_B2) * (g * g)
    m_hat = m / (1.0 - ADAM_B1 ** ADAM_STEP)
    v_hat = v / (1.0 - ADAM_B2 ** ADAM_STEP)
    delta = -ADAM_LR * (m_hat / (jnp.sqrt(v_hat) + ADAM_EPS) + ADAM_WD * w)
    return delta, m, v


def _adamw_multi(name, own, sib, rel, ws, ms, vs, row_grid):
    k_n, r_n, c_n = own.shape
    rbk = r_n // row_grid

    def body(*refs):
        own_ref, sib_ref, r0_ref, r1_ref, r2_ref = refs[:5]
        w_refs, m_refs, v_refs = refs[5:5 + k_n], refs[5 + k_n:5 + 2 * k_n], refs[5 + 2 * k_n:5 + 3 * k_n]
        outs = refs[5 + 3 * k_n:]
        for k in range(k_n):
            g = own_ref[k].astype(F32) + sib_ref[k].astype(F32)
            g = g + r0_ref[k].astype(F32)
            g = g + r1_ref[k].astype(F32)
            g = g + r2_ref[k].astype(F32)
            delta, mm, vv = _adamw_math(g, w_refs[k][0], m_refs[k][0], v_refs[k][0])
            outs[4 * k][0] = g
            outs[4 * k + 1][0] = delta
            outs[4 * k + 2][0] = mm
            outs[4 * k + 3][0] = vv

    def lead(j):
        return pl.BlockSpec((None, k_n, rbk, c_n), lambda g: (j, 0, g, 0))

    wspec = pl.BlockSpec((1, rbk, c_n), lambda g: (0, g, 0))
    shp = jax.ShapeDtypeStruct((1, r_n, c_n), F32)
    res = pl.pallas_call(
        body, name=name, grid=(row_grid,),
        in_specs=[pl.BlockSpec((k_n, rbk, c_n), lambda g: (0, g, 0)), lead(0), lead(0), lead(1), lead(2)] + [wspec] * (3 * k_n),
        out_specs=[wspec] * (4 * k_n), out_shape=[shp] * (4 * k_n),
        compiler_params=_params(("arbitrary",), 40),
    )(own, sib, rel, rel, rel, *ws, *ms, *vs)
    return [tuple(res[4 * k:4 * k + 4]) for k in range(k_n)]


def _adamw_meta_dw(own, sib, rel, meta, dw):
    def body(own_ref, sib_ref, rel_ref, wm, mm, vm, wd, md, vd, *outs):
        def gsum(rows):
            g = own_ref[rows, :].astype(F32) + sib_ref[0, rows, :].astype(F32)
            for j in range(3):
                g = g + rel_ref[j, rows, :].astype(F32)
            return g

        g = gsum(pl.ds(0, N_META))
        delta, m2, v2 = _adamw_math(g, wm[...], mm[...], vm[...])
        for o, val in zip(outs[:4], (g, delta, m2, v2)):
            o[...] = val
        g = gsum(pl.ds(N_META, CONV_K))
        delta, m2, v2 = _adamw_math(g, wd[0], md[0], vd[0])
        for o, val in zip(outs[4:], (g, delta, m2, v2)):
            o[0] = val

    s_meta = jax.ShapeDtypeStruct(meta[0].shape, F32)
    s_dw = jax.ShapeDtypeStruct(dw[0].shape, F32)
    res = pl.pallas_call(body, name="adamw_meta_dw", out_shape=[s_meta] * 4 + [s_dw] * 4)(own, sib, rel, *meta, *dw)
    return tuple(res[:4]), tuple(res[4:])


REP_ROWS = 16


def _adamw_rep(gathered, ws, ms, vs):
    rows = [(0, 1), (1, 2), (3, 1), (4, 1), (5, 1), (6, 1), (7, 1), (8, 1)]

    def body(g_ref, *refs):
        w_refs, m_refs, v_refs = refs[:8], refs[8:16], refs[16:24]
        loss_ref, outs, acc = refs[24], refs[25:57], refs[57]
        g = g_ref[0]
        for d in range(1, NDEV):
            g = g + g_ref[d]
        acc[...] = g
        loss_ref[...] = (0.5 / D) * jnp.sum(acc[pl.ds(9, 1), :], axis=1, keepdims=True)
        for p, (r0, nr) in enumerate(rows):
            for h in range(nr):
                cols = pl.ds(h * D, D)
                gp = acc[pl.ds(r0 + h, 1), :]
                delta, mm, vv = _adamw_math(gp, w_refs[p][:, cols], m_refs[p][:, cols], v_refs[p][:, cols])
                for o, val in zip(outs[4 * p:4 * p + 4], (gp, delta, mm, vv)):
                    o[:, cols] = val

    shapes = [jax.ShapeDtypeStruct(w.shape, F32) for w in ws]
    res = pl.pallas_call(
        body, name="adamw_rep",
        out_shape=[jax.ShapeDtypeStruct((1, 1), F32)] + [s for s in shapes for _ in range(4)],
        scratch_shapes=[pltpu.VMEM((REP_ROWS, D), F32)],
    )(gathered, *ws, *ms, *vs)
    return res[0], [tuple(res[1 + 4 * p:5 + 4 * p]) for p in range(8)]


def _load_ffn(i, j, wgu_hbm, wgu, wdn_hbm, wdn, sems):
    half = NDEV // 2

    def copies(ch):
        pairs = [(wgu_hbm.at[half * ch + d, g], wgu.at[g, ch, pl.ds(FFB * d, FFB), :]) for g in range(2) for d in range(half)]
        pairs.append((wdn_hbm.at[ch], wdn.at[ch]))
        return [pltpu.make_async_copy(s, t, sems.at[(2 * half + 1) * ch + k]) for k, (s, t) in enumerate(pairs)]

    @pl.when((i == 0) & (j == 0))
    def _():
        for cp in copies(0) + copies(1):
            cp.start()

    for ch in range(2):
        @pl.when((i == 0) & (j == ch))
        def _():
            for cp in copies(ch):
                cp.wait()


def _win_pairs(w_hbm, w_vm):
    return [(w_hbm.at[q], w_vm.at[q // 2, :, pl.ds(2 * INB * (q % 2), 2 * INB)]) for q in range(4)]


def _whole(a):
    nd = a.ndim
    return pl.BlockSpec(a.shape, lambda *g: (0,) * nd)


CHIPW = 2 * INB
PHASE_CHIP = (1, 0, 2)
assert PHASE_CHIP[2] == 2


class _GatherIn:
    scratch = [pltpu.VMEM((D, INB), BF16), pltpu.SemaphoreType.DMA((7,)), pltpu.SemaphoreType.DMA((7,)),
               pltpu.SemaphoreType.DMA((1,))]

    def bind(self, w_ref, w_vm, scratch):
        self.w_ref, self.w_vm = w_ref, w_vm
        self.stage, self.send_sems, self.recv_sems, self.local_sem = scratch
        return self

    def _win(self, chip, core):
        return self.w_vm.at[2 * chip[0] + chip[1], :, pl.ds(INB * core, INB)]

    def _copy(self, k, chip, core, to, src=None):
        dst = self._win(chip, core)
        return pltpu.make_async_remote_copy(
            src_ref=dst if src is None else src, dst_ref=dst, send_sem=self.send_sems.at[k],
            recv_sem=self.recv_sems.at[k], device_id=to, device_id_type=MESH_ID)

    def _mine(self, cs):
        x, y, _ = _place()
        return pltpu.make_async_copy(self.stage, self._win((x, y), cs), self.local_sem.at[0])

    def issue(self, cs):
        x, y, _ = _place()
        chips = [(1 - x, y), (x, 1 - y), (1 - x, 1 - y)]
        self.stage[...] = self.w_ref[0].astype(BF16)
        self._mine(cs).start()
        self._copy(0, (x, y), cs, (x, y, 1 - cs), src=self.stage).start()
        for j in PHASE_CHIP[:2]:
            self._copy(1 + j, (x, y), cs, (*chips[j], cs), src=self.stage).start()

    def wait_chip(self, phase, cs):
        x, y, _ = _place()
        chips = [(1 - x, y), (x, 1 - y), (1 - x, 1 - y)]
        if phase == 0:
            self._mine(cs).wait()
            self._copy(0, (x, y), 1 - cs, (x, y, cs)).wait_recv()
            return
        if phase == 1:
            for j in PHASE_CHIP[:2]:
                self._copy(1 + j, chips[j], cs, (x, y, cs)).wait_recv()
                self._copy(4 + j, chips[j], cs, (x, y, 1 - cs)).start()
            self._copy(3, (x, y), cs, (*chips[2], cs), src=self.stage).start()
        j = PHASE_CHIP[phase - 1]
        if phase == 3:
            self._copy(1 + j, chips[j], cs, (x, y, cs)).wait_recv()
            self._copy(4 + j, chips[j], cs, (x, y, 1 - cs)).start()
        self._copy(4 + j, chips[j], 1 - cs, (x, y, cs)).wait_recv()

    def finish(self, cs):
        x, y, _ = _place()
        for k in range(7):
            self._copy(k, (x, y), cs, (x, y, cs), src=self.stage).wait_send()


def _fwd_in(x2, g_mix, w_in, order, tp, ag, ags):
    tm = _pick(tp, TM_IO)
    nt = tp // tm
    nx_last = x2.shape[0] - (nt - 1) * tm
    na, ng, ns = len(ag.arrays), ag.n, len(ags.arrays)
    gin = _GatherIn()

    def body(order_ref, *refs):
        x_ref, g_ref, w_ref = refs[:3]
        o = 3 + na + ns
        h_ref, z_ref, u_ref, wout_ref = refs[o:o + 4]
        s = o + 4 + ng + 1
        w_vm, u_all, osem, sm_vm = refs[s:s + 4]
        gin.bind(w_ref, w_vm, refs[s + 4:s + 8])
        ag.bind(refs[3:3 + na], refs[o + 4:o + 4 + ng], refs[s + 8:s + 8 + len(ag.scratch)])
        ags.bind(refs[3 + na:3 + na + ns], refs[o + 4 + ng:o + 5 + ng], refs[s + 8 + len(ag.scratch):])
        ph, i = pl.program_id(0), pl.program_id(1)
        core = lax.axis_index("c")
        first = (ph == 0) & (i == 0)
        last = (ph == 3) & (i == nt - 1)
        for cs in range(2):
            @pl.when(first & (core == cs))
            def _():
                gin.issue(cs)

        @pl.when(first)
        def _():
            ags.issue()

        @pl.when((ph == 0) & (i == max(nt - 2, 0)))
        def _():
            ags.forward()

        for cs in range(2):
            for p in range(4):
                @pl.when((ph == p) & (i == 0) & (core == cs))
                def _():
                    gin.wait_chip(p, cs)

        @pl.when((ph == 2) & (i == 0))
        def _():
            ag.issue()

        out_copy = pltpu.make_async_copy(w_vm, wout_ref, osem.at[0])

        @pl.when((ph == 3) & (i == 0))
        def _():
            out_copy.start()

        @pl.when((ph == 0) & (i < nt - 1))
        def _():
            h_ref[...] = x_ref[...]

        @pl.when((ph == 0) & (i == nt - 1))
        def _():
            ags.finish()
            cp = pltpu.make_async_copy(ags.outs[0], sm_vm, osem.at[1])
            cp.start()
            h_ref[pl.ds(0, nx_last), :] = x_ref[pl.ds(0, nx_last), :]
            h_ref[pl.ds(nx_last, tm - nx_last - N_META), :] = jnp.zeros((tm - nx_last - N_META, D), F32)
            cp.wait()
            for d in range(NDEV):
                h_ref[pl.ds(tm - N_META, N_META), pl.ds(128 * d, 128)] = sm_vm[d, pl.ds(0, N_META), :]

        @pl.when(ph == 0)
        def _():
            xv = h_ref[...]
            r = lax.rsqrt(jnp.mean(xv * xv, axis=-1, keepdims=True) + RMS_EPS)
            u = (xv * r * g_ref[...]).astype(BF16)
            u_ref[...] = u
            u_all[i] = u

        z_ref[...] = _dot(u_all[i], w_vm[order_ref[ph]])

        @pl.when(last)
        def _():
            ag.forward()
            ag.finish()
            out_copy.wait()

        for cs in range(2):
            @pl.when(last & (core == cs))
            def _():
                gin.finish(cs)

    def rows(ph, i, order):
        return (jnp.where(ph == 0, i, nt - 1), 0)

    tile = pl.BlockSpec((tm, D), rows)
    anys = pl.BlockSpec(memory_space=pl.ANY)
    res = pl.pallas_call(
        body, name="fwd_in",
        grid_spec=pltpu.PrefetchScalarGridSpec(
            num_scalar_prefetch=1, grid=(4, nt),
            in_specs=[tile, pl.BlockSpec((1, D), lambda ph, i, order: (0, 0)), _whole(w_in)]
            + [_whole(a) for a in ag.arrays + ags.arrays],
            out_specs=[tile, pl.BlockSpec((tm, CHIPW), lambda ph, i, order: (i, order[ph])), tile, anys] + [anys] * (ng + 1),
            scratch_shapes=[pltpu.VMEM((4, D, CHIPW), BF16), pltpu.VMEM((nt, tm, D), BF16), pltpu.SemaphoreType.DMA((2,)),
                            pltpu.VMEM(ags.out_shape[0].shape, F32)] + gin.scratch + ag.scratch + ags.scratch),
        out_shape=[jax.ShapeDtypeStruct((tp, D), F32), jax.ShapeDtypeStruct((tp, DIN), F32),
                   jax.ShapeDtypeStruct((tp, D), BF16), jax.ShapeDtypeStruct((4, D, CHIPW), BF16)]
        + ag.out_shape + ags.out_shape,
        compiler_params=_params(("arbitrary", "arbitrary"), 58),
    )(order, x2, g_mix, w_in, *ag.arrays, *ags.arrays)
    return res[:4], res[4:4 + ng], res[4 + ng]


def _halo_specs(col, nt, width=D):
    r = TM // HALO
    nb = nt * r
    return [pl.BlockSpec((HALO, width), lambda i: ((i * r + nb - 1) % nb, col)),
            pl.BlockSpec((TM, width), lambda i: (i, col)),
            pl.BlockSpec((HALO, width), lambda i: (((i + 1) * r) % nb, col))]


NCB = D // 128
TME = TM + 2 * HALO


def _tm_fill(dst, time0, groups, tile_fn):
    def body(g, c):
        for j in range(NCB):
            dst[pl.ds((time0 + 8 * g) * NCB + j, 8, stride=NCB), :] = tile_fn(pl.multiple_of(8 * g, 8), pl.ds(128 * j, 128))
        return c

    lax.fori_loop(0, groups, body, 0)


def _tm_fill_ext(dst, left, cur, right, fn):
    _tm_fill(dst, 0, HALO // 8, lambda r, l: fn(left, pl.ds(r, 8), l))
    _tm_fill(dst, HALO, TM // 8, lambda r, l: fn(cur, pl.ds(r, 8), l))
    _tm_fill(dst, HALO + TM, HALO // 8, lambda r, l: fn(right, pl.ds(r, 8), l))


def _tm_read(src, groups, store_fn):
    def body(g, c):
        for j in range(NCB):
            store_fn(pl.ds(pl.multiple_of(8 * g, 8), 8), pl.ds(128 * j, 128), src[pl.ds(8 * g * NCB + j, 8, stride=NCB), :])
        return c

    lax.fori_loop(0, groups, body, 0)


def _tm_rows(t):
    return pl.ds(t * NCB if isinstance(t, int) else pl.multiple_of(t * NCB, NCB), NCB)


def _tm_at(ref, t):
    return ref[_tm_rows(t), :]


def _by_group(sub, vals):
    return jnp.where(sub < 2, vals[0], jnp.where(sub < 4, vals[1], jnp.where(sub < 6, vals[2], vals[3])))


def _pool_cnt(b, seq, tp, sub):
    b = jnp.where(b < 0, b + tp, b)
    b = jnp.where(b >= tp, b - tp, b)
    t = jnp.where(b < seq, b + N_META, b - (tp - N_META))
    cnts = []
    for win in POOL_WINDOWS:
        left = win // 2
        lo = jnp.maximum(t - left, 0)
        hi = jnp.minimum(t + win - left, seq + N_META)
        cnts.append(jnp.maximum(hi - lo, 1).astype(F32))
    return _by_group(sub, cnts)


def _edge_rows(seq, tp):
    reach = max(POOL_WINDOWS) // 2
    return [tp - N_META + t for t in range(reach)] + [seq - reach + 1 + t for t in range(reach - 1)]


def _edge_gain(b, seq, tp, sub):
    return _by_group(sub, [float(w) for w in POOL_WINDOWS]) / _pool_cnt(b, seq, tp, sub)


def _nested_windows(at, lo_offs):
    sums, s, have = [], None, set()
    for g, win in enumerate(POOL_WINDOWS):
        for o in range(lo_offs[g], lo_offs[g] + win):
            if o not in have:
                have.add(o)
                s = at(o) if s is None else s + at(o)
        sums.append(s)
    return sums


def _seq_fwd(z, w_dw, b_dw, seq, gat):
    tp = z.shape[0]
    nt = tp // TM
    na, ng = len(gat.arrays), gat.n

    def body(*refs):
        av_l, av, av_r, ag_l, ag, ag_r, p_l, p, p_r, w_ref, b_ref = refs[:11]
        ac_ref, m_ref = refs[11 + na:13 + na]
        a3, p3, o3, m3, w3, b3, m2d = refs[13 + na + ng:20 + na + ng]
        gat.bind(refs[11:11 + na], refs[13 + na:13 + na + ng], refs[20 + na + ng:])
        i = pl.program_id(0)
        sub = lax.broadcasted_iota(jnp.int32, (NCB, 128), 0)

        @pl.when(i == 0)
        def _():
            gat.issue()
            _tm_fill(w3, 0, 4, lambda r, l: w_ref[pl.ds(r, 8), l])
            for j in range(NCB):
                b3[pl.ds(j, 1), :] = b_ref[:, pl.ds(128 * j, 128)]

        @pl.when(i == max(nt - 2, 0))
        def _():
            gat.forward()

        _tm_fill_ext(a3, (av_l, ag_l), (av, ag), (av_r, ag_r), lambda vg, r, l: vg[0][r, l] * _sig(vg[1][r, l]))
        _tm_fill_ext(p3, p_l, p, p_r, lambda ref, r, l: ref[r, l])

        def conv(g, c):
            accs = [b3[...]] * 16
            for k in range(CONV_K):
                wk = _tm_at(w3, k)
                for t in range(16):
                    accs[t] = accs[t] + wk * _tm_at(a3, 16 * g + t + k + 1)
            for t in range(16):
                o3[_tm_rows(16 * g + t), :] = accs[t]
            return c

        lax.fori_loop(0, TM // 16, conv, 0)
        _tm_read(o3, TM // 8, lambda r, l, tile: ac_ref.__setitem__((r, l), tile))

        inv = _by_group(sub, [1.0 / w for w in POOL_WINDOWS])

        def pool(g, c):
            for t in range(8):
                e = 8 * g + t + HALO
                sums = _nested_windows(lambda o: _tm_at(p3, e + o), [-(w // 2) for w in POOL_WINDOWS])
                m3[_tm_rows(8 * g + t), :] = _by_group(sub, sums) * inv - _tm_at(p3, e)
            return c

        lax.fori_loop(0, TM // 8, pool, 0)
        for b in _edge_rows(seq, tp):
            r = b - i * TM

            @pl.when((r >= 0) & (r < TM))
            def _():
                pv = _tm_at(p3, r + HALO)
                m3[_tm_rows(r), :] = (_tm_at(m3, r) + pv) * _edge_gain(b, seq, tp, sub) - pv

        _tm_read(m3, TM // 8, lambda r, l, tile: m2d.__setitem__((r, l), tile))
        m_ref[...] = m2d[...].astype(BF16)

        @pl.when(i == nt - 1)
        def _():
            gat.finish()

    tmaj = pltpu.VMEM((TM * NCB, 128), F32)
    text = pltpu.VMEM((TME * NCB, 128), F32)
    res = pl.pallas_call(
        body, name="seq_fwd", grid=(nt,),
        in_specs=_halo_specs(0, nt) + _halo_specs(1, nt) + _halo_specs(2, nt)
        + [pl.BlockSpec((32, D), lambda i: (0, 0)), pl.BlockSpec((1, D), lambda i: (0, 0))] + [_whole(a) for a in gat.arrays],
        out_specs=[pl.BlockSpec((TM, D), lambda i: (i, 0))] * 2 + [pl.BlockSpec(memory_space=pl.ANY)] * ng,
        out_shape=[jax.ShapeDtypeStruct((tp, D), F32), jax.ShapeDtypeStruct((tp, D), BF16)] + gat.out_shape,
        scratch_shapes=[text, text, tmaj, tmaj, pltpu.VMEM((32 * NCB, 128), F32), pltpu.VMEM((NCB, 128), F32),
                        pltpu.VMEM((TM, D), F32)] + gat.scratch,
        compiler_params=_params(("arbitrary",), 52),
    )(z, z, z, z, z, z, z, z, z, w_dw, b_dw, *gat.arrays)
    return res[:2], res[2:]


def _ln_stats(ac):
    mu = jnp.mean(ac, axis=-1, keepdims=True)
    xc = ac - mu
    rl = lax.rsqrt(jnp.mean(xc * xc, axis=-1, keepdims=True) + LN_EPS)
    return xc * rl, rl


def _pool_mix(m, wp_ref):
    return jnp.concatenate(
        [_dot(m[:, g * PG:(g + 1) * PG], wp_ref[:, g].reshape(PG, PG)) for g in range(4)], axis=1)


def _mix_fwd(ac, m, z, h0, b_gate, ln_g, ln_b, pool_scale, g_mixw, g_pool, gat):
    tp = h0.shape[0]
    tms = TM
    nt = tp // tms
    na, ng = len(gat.arrays), gat.n

    def body(*refs):
        ac_ref, m_ref, zga, zgb, h_ref, bg_ref, lg_ref, lb_ref, ps_ref, wm_hbm, wp_hbm = refs[:11]
        h1_ref, s_ref, mg_ref, q_ref = refs[11 + na:15 + na]
        wm, wp, sems = refs[15 + na + ng:18 + na + ng]
        gat.bind(refs[11:11 + na], refs[15 + na:15 + na + ng], refs[18 + na + ng:])
        i = pl.program_id(0)

        @pl.when(i == 0)
        def _():
            gat.issue()

        @pl.when(i == max(nt - 4, 0))
        def _():
            gat.forward()

        @pl.when(i == nt - 1)
        def _():
            gat.finish()

        _load_once(i == 0, [(wm_hbm, wm), (wp_hbm, wp)], sems)
        n, _ = _ln_stats(ac_ref[...])
        l = n * lg_ref[...] + lb_ref[...]
        s = (l * _sig(l)).astype(BF16)
        s_ref[...] = s
        yc = _dot(s, wm[:, 0].reshape(D, D))
        q = (_pool_mix(m_ref[...], wp) * ps_ref[...]).astype(BF16)
        q_ref[...] = q
        yp = _dot(q, wm[:, 1].reshape(D, D))
        ga = _sig(zga[...] + bg_ref[:, :D])
        gb = _sig(zgb[...] + bg_ref[:, D:])
        merged = (ga * yc + gb * yp).astype(BF16)
        mg_ref[...] = merged
        h1_ref[...] = h_ref[...] + _dot(merged, wm[:, 2].reshape(D, D))

    def tile(col=0):
        return pl.BlockSpec((tms, D), lambda i: (i, col))

    def vec(w):
        return pl.BlockSpec((1, w), lambda i: (0, 0))

    anys = pl.BlockSpec(memory_space=pl.ANY)
    f32o, b16o = jax.ShapeDtypeStruct((tp, D), F32), jax.ShapeDtypeStruct((tp, D), BF16)
    res = pl.pallas_call(
        body, name="mix_fwd", grid=(nt,),
        in_specs=[tile(), tile(), tile(3), tile(4), tile(), vec(2 * D), vec(D), vec(D), vec(D), anys, anys]
        + [_whole(a) for a in gat.arrays],
        out_specs=[tile()] * 4 + [anys] * ng,
        out_shape=[f32o, b16o, b16o, b16o] + gat.out_shape,
        scratch_shapes=[pltpu.VMEM((NDEV, 3, D // NDEV, D), BF16), pltpu.VMEM((NDEV, 4, PG // NDEV, PG), BF16),
                        pltpu.SemaphoreType.DMA((2,))] + gat.scratch,
        compiler_params=_params(("arbitrary",), 52),
    )(ac, m, z, z, h0, b_gate, ln_g, ln_b, pool_scale, g_mixw, g_pool, *gat.arrays)
    return res[:4], res[4:]


def _ffn_fwd(h1, tgt, g_ffn, g_final, w_gu, w_dn):
    tp = h1.shape[0]
    nt = tp // TM
    nx_last = tgt.shape[0] - (nt - 1) * TM

    def body(h_ref, t_ref, gf_ref, gl_ref, wgu_hbm, wdn_hbm,
             fg_ref, fu_ref, v_ref, f_ref, dh2_ref, acc_ref, wgu, wdn, v_sc, h2_sc, diff_sc, sems):
        i, j = pl.program_id(0), pl.program_id(1)
        _load_ffn(i, j, wgu_hbm, wgu, wdn_hbm, wdn, sems)

        @pl.when((i == 0) & (j == 0))
        def _():
            acc_ref[...] = jnp.zeros_like(acc_ref)

        @pl.when(j == 0)
        def _():
            h = h_ref[...]
            r = lax.rsqrt(jnp.mean(h * h, axis=-1, keepdims=True) + RMS_EPS)
            v = (h * r * gf_ref[...]).astype(BF16)
            v_sc[...] = v
            v_ref[...] = v
            h2_sc[...] = h

        v = v_sc[...]
        fg = _dot_nt(v, wgu[0, j])
        fu = _dot_nt(v, wgu[1, j])
        fg_ref[...] = fg
        fu_ref[...] = fu
        f = ((fg * _sig(fg)) * fu).astype(BF16)
        f_ref[...] = f
        h2_sc[...] += _dot(f, wdn[j])

        @pl.when(j == 1)
        def _():
            h2 = h2_sc[...]
            r = lax.rsqrt(jnp.mean(h2 * h2, axis=-1, keepdims=True) + RMS_EPS)
            n2 = h2 * r
            y = n2 * gl_ref[...]

            @pl.when(i < nt - 1)
            def _():
                diff_sc[...] = y - t_ref[...]

            @pl.when(i == nt - 1)
            def _():
                diff_sc[pl.ds(0, nx_last), :] = y[:nx_last] - t_ref[pl.ds(0, nx_last), :]
                diff_sc[pl.ds(nx_last, TM - nx_last), :] = jnp.zeros((TM - nx_last, D), F32)

            diff = diff_sc[...]
            dy = diff * (1.0 / D)
            acc_ref[0:1, :] += jnp.sum(diff * diff, axis=0, keepdims=True)
            acc_ref[1:2, :] += jnp.sum(dy * n2, axis=0, keepdims=True)
            dn = dy * gl_ref[...]
            dh2_ref[...] = r * (dn - n2 * jnp.mean(dn * n2, axis=-1, keepdims=True))

    def tile():
        return pl.BlockSpec((TM, D), lambda i, j: (i, 0))

    def chunk():
        return pl.BlockSpec((TM, FFC), lambda i, j: (i, j))

    def vec():
        return pl.BlockSpec((1, D), lambda i, j: (0, 0))

    anys = pl.BlockSpec(memory_space=pl.ANY)
    hid32, hid16 = jax.ShapeDtypeStruct((tp, DFF), F32), jax.ShapeDtypeStruct((tp, DFF), BF16)
    return pl.pallas_call(
        body, name="ffn_fwd", grid=(nt, 2),
        in_specs=[tile(), tile(), vec(), vec(), anys, anys],
        out_specs=[chunk(), chunk(), tile(), chunk(), tile(), pl.BlockSpec((8, D), lambda i, j: (0, 0))],
        out_shape=[hid32, hid32, jax.ShapeDtypeStruct((tp, D), BF16), hid16, jax.ShapeDtypeStruct((tp, D), F32),
                   jax.ShapeDtypeStruct((8, D), F32)],
        scratch_shapes=[pltpu.VMEM((2, 2, FFC, D), BF16), pltpu.VMEM((2, FFC, D), BF16),
                        pltpu.VMEM((TM, D), BF16), pltpu.VMEM((TM, D), F32), pltpu.VMEM((TM, D), F32),
                        pltpu.SemaphoreType.DMA((2 * NDEV + 2,))],
        compiler_params=_params(("arbitrary", "arbitrary"), 56),
    )(h1, tgt, g_ffn, g_final, w_gu, w_dn)


def _ffn_bwd(dh2, fg, fu, h1, g_ffn, w_gu, w_dn):
    tp = h1.shape[0]
    nt = tp // TM

    def body(dh2_ref, fg_ref, fu_ref, h_ref, gf_ref, wgu_hbm, wdn_hbm,
             dfg_ref, dfu_ref, dh1_ref, acc_ref, wgu, wdn, d_sc, dv_sc, sems):
        i, j = pl.program_id(0), pl.program_id(1)
        _load_ffn(i, j, wgu_hbm, wgu, wdn_hbm, wdn, sems)

        @pl.when((i == 0) & (j == 0))
        def _():
            acc_ref[...] = jnp.zeros_like(acc_ref)

        @pl.when(j == 0)
        def _():
            d_sc[...] = dh2_ref[...].astype(BF16)
            dv_sc[...] = jnp.zeros_like(dv_sc)

        df = _dot_nt(d_sc[...], wdn[j])
        fg = fg_ref[...]
        sg = _sig(fg)
        dfu = (df * (fg * sg)).astype(BF16)
        dfg = (df * fu_ref[...] * (sg * (1.0 + fg * (1.0 - sg)))).astype(BF16)
        dfg_ref[...] = dfg
        dfu_ref[...] = dfu
        dv_sc[...] += _dot(dfg, wgu[0, j]) + _dot(dfu, wgu[1, j])

        @pl.when(j == 1)
        def _():
            h = h_ref[...]
            r = lax.rsqrt(jnp.mean(h * h, axis=-1, keepdims=True) + RMS_EPS)
            n1 = h * r
            dv = dv_sc[...]
            acc_ref[0:1, :] += jnp.sum(dv * n1, axis=0, keepdims=True)
            dn = dv * gf_ref[...]
            dh1_ref[...] = dh2_ref[...] + r * (dn - n1 * jnp.mean(dn * n1, axis=-1, keepdims=True))

    def tile():
        return pl.BlockSpec((TM, D), lambda i, j: (i, 0))

    def chunk():
        return pl.BlockSpec((TM, FFC), lambda i, j: (i, j))

    anys = pl.BlockSpec(memory_space=pl.ANY)
    hid16 = jax.ShapeDtypeStruct((tp, DFF), BF16)
    return pl.pallas_call(
        body, name="ffn_bwd", grid=(nt, 2),
        in_specs=[tile(), chunk(), chunk(), tile(), pl.BlockSpec((1, D), lambda i, j: (0, 0)), anys, anys],
        out_specs=[chunk(), chunk(), tile(), pl.BlockSpec((8, D), lambda i, j: (0, 0))],
        out_shape=[hid16, hid16, jax.ShapeDtypeStruct((tp, D), F32), jax.ShapeDtypeStruct((8, D), F32)],
        scratch_shapes=[pltpu.VMEM((2, 2, FFC, D), BF16), pltpu.VMEM((2, FFC, D), BF16),
                        pltpu.VMEM((TM, D), BF16), pltpu.VMEM((TM, D), F32), pltpu.SemaphoreType.DMA((2 * NDEV + 2,))],
        compiler_params=_params(("arbitrary", "arbitrary"), 56),
    )(dh2, fg, fu, h1, g_ffn, w_gu, w_dn)


def _mix_bwd(dh1, z, s, q, ac, m, b_gate, ln_g, ln_b, pool_scale, g_mixw, g_pool, qs):
    tp = dh1.shape[0]
    nt = tp // TMS
    ex = _ChipExchange(qs)
    nq = ex.n

    def body(*refs):
        dh1_ref, zga, zgb, s_ref, q_ref, ac_ref, m_ref, bg_ref, lg_ref, lb_ref, ps_ref, wm_hbm, wp_hbm = refs[:13]
        dac_ref, dm_ref, dzg_ref, dyc_ref, dyp_ref, dm2_ref, acc_ref = refs[13 + nq:20 + nq]
        wm, wp, sems = refs[20 + 2 * nq:23 + 2 * nq]
        ex.bind(refs[13:13 + nq], refs[20 + nq:20 + 2 * nq], refs[23 + 2 * nq:])
        first = pl.program_id(0) == 0

        @pl.when(first)
        def _():
            ex.issue()
            acc_ref[...] = jnp.zeros_like(acc_ref)

        _load_once(first, [(wm_hbm, wm), (wp_hbm, wp)], sems)

        dmerged = _dot_nt(dh1_ref[...].astype(BF16), wm[:, 2].reshape(D, D))
        ga = _sig(zga[...] + bg_ref[:, :D])
        gb = _sig(zgb[...] + bg_ref[:, D:])
        dyc = dmerged * ga
        dyp = dmerged * gb
        dza = (dmerged * _dot(s_ref[...], wm[:, 0].reshape(D, D))) * (ga * (1.0 - ga))
        dzb = (dmerged * _dot(q_ref[...], wm[:, 1].reshape(D, D))) * (gb * (1.0 - gb))
        dzg_ref[:, :D] = dza.astype(BF16)
        dzg_ref[:, D:] = dzb.astype(BF16)
        acc_ref[0:1, :D] += jnp.sum(dza, axis=0, keepdims=True)
        acc_ref[0:1, D:] += jnp.sum(dzb, axis=0, keepdims=True)
        dyc_b = dyc.astype(BF16)
        dyp_b = dyp.astype(BF16)
        dyc_ref[...] = dyc_b
        dyp_ref[...] = dyp_b
        ds = _dot_nt(dyc_b, wm[:, 0].reshape(D, D))
        n, rl = _ln_stats(ac_ref[...])
        l = n * lg_ref[...] + lb_ref[...]
        sg = _sig(l)
        dl = ds * (sg * (1.0 + l * (1.0 - sg)))
        acc_ref[1:2, :D] += jnp.sum(dl * n, axis=0, keepdims=True)
        acc_ref[1:2, D:] += jnp.sum(dl, axis=0, keepdims=True)
        dn = dl * lg_ref[...]
        dac_ref[...] = rl * (dn - jnp.mean(dn, axis=-1, keepdims=True) - n * jnp.mean(dn * n, axis=-1, keepdims=True))
        dq = _dot_nt(dyp_b, wm[:, 1].reshape(D, D))
        mv = m_ref[...]
        acc_ref[2:3, :D] += jnp.sum(dq * _pool_mix(mv, wp), axis=0, keepdims=True)
        dm2 = (dq * ps_ref[...]).astype(BF16)
        dm2_ref[...] = dm2
        dm_ref[...] = jnp.concatenate(
            [_dot_nt(dm2[:, g * PG:(g + 1) * PG], wp[:, g].reshape(PG, PG)) for g in range(4)], axis=1)

        @pl.when(pl.program_id(0) == nt - 1)
        def _():
            ex.finish()

    def tile(col=0):
        return pl.BlockSpec((TMS, D), lambda i: (i, col))

    def vec(w):
        return pl.BlockSpec((1, w), lambda i: (0, 0))

    anys = pl.BlockSpec(memory_space=pl.ANY)
    f32o, b16o = jax.ShapeDtypeStruct((tp, D), F32), jax.ShapeDtypeStruct((tp, D), BF16)
    res = pl.pallas_call(
        body, name="mix_bwd", grid=(nt,),
        in_specs=[tile(), tile(3), tile(4), tile(), tile(), tile(), tile(), vec(2 * D), vec(D), vec(D), vec(D), anys, anys]
        + [anys] * nq,
        out_specs=[tile(), tile(), pl.BlockSpec((TMS, 2 * D), lambda i: (i, 0)), tile(), tile(), tile(),
                   pl.BlockSpec((8, 2 * D), lambda i: (0, 0))] + [anys] * nq,
        out_shape=[f32o, f32o, jax.ShapeDtypeStruct((tp, 2 * D), BF16), b16o, b16o, b16o,
                   jax.ShapeDtypeStruct((8, 2 * D), F32)] + ex.out_shape,
        scratch_shapes=[pltpu.VMEM((NDEV, 3, D // NDEV, D), BF16), pltpu.VMEM((NDEV, 4, PG // NDEV, PG), BF16),
                        pltpu.SemaphoreType.DMA((2,))] + ex.scratch,
        compiler_params=_params(("arbitrary",), 48),
    )(dh1, z, z, s, q, ac, m, b_gate, ln_g, ln_b, pool_scale, g_mixw, g_pool, *qs)
    return res[:7], res[7:]


def _seq_bwd(dac, dm, dzg, z, w_dw, seq, qs):
    tp = z.shape[0]
    nt = tp // TM
    ex = _ChipExchange(qs)
    nq = no = ex.n

    def body(*refs):
        dac_l, dac_c, dac_r, dm_l, dm_c, dm_r, av_l, av, av_r, ag_l, ag, ag_r, dzg_ref, w_ref = refs[:14]
        dz_ref, acc_ref = refs[14 + nq:16 + nq]
        a3, d3, m3, da3, dp3, w3, dw3, da_sc, dp_sc = refs[16 + nq + no:25 + nq + no]
        ex.bind(refs[14:14 + nq], refs[16 + nq:16 + nq + no], refs[25 + nq + no:])
        i = pl.program_id(0)
        sub = lax.broadcasted_iota(jnp.int32, (NCB, 128), 0)

        @pl.when(i == 0)
        def _():
            ex.issue()
            dw3[...] = jnp.zeros_like(dw3)
            _tm_fill(w3, 0, 4, lambda r, l: w_ref[pl.ds(r, 8), l])

        _tm_fill_ext(a3, (av_l, ag_l), (av, ag), (av_r, ag_r), lambda vg, r, l: vg[0][r, l] * _sig(vg[1][r, l]))
        _tm_fill_ext(d3, dac_l, dac_c, dac_r, lambda ref, r, l: ref[r, l])
        _tm_fill_ext(m3, dm_l, dm_c, dm_r, lambda ref, r, l: ref[r, l])

        def conv(g, c):
            dcur = [_tm_at(d3, 8 * g + t + HALO) for t in range(8)]
            accs = [None] * 8
            for k in range(CONV_K):
                wk = _tm_at(w3, k)
                prs = []
                for t in range(8):
                    term = wk * _tm_at(d3, 8 * g + t + CONV_K - k)
                    accs[t] = term if accs[t] is None else accs[t] + term
                    prs.append(dcur[t] * _tm_at(a3, 8 * g + t + k + 1))
                while len(prs) > 1:
                    prs = [prs[j] + prs[j + 1] for j in range(0, len(prs), 2)]
                dw3[_tm_rows(k), :] += prs[0]
            s = dcur[0]
            for t in range(1, 8):
                s = s + dcur[t]
            dw3[_tm_rows(CONV_K), :] += s
            for t in range(8):
                da3[_tm_rows(8 * g + t), :] = accs[t]
            return c

        lax.fori_loop(0, TM // 8, conv, 0)

        for b in _edge_rows(seq, tp):
            e = lax.rem(b - i * TM + HALO + tp, tp)

            @pl.when(e < TME)
            def _():
                m3[_tm_rows(e), :] = _tm_at(m3, e) * _edge_gain(b, seq, tp, sub)

        inv = _by_group(sub, [1.0 / w for w in POOL_WINDOWS])

        def pool(g, c):
            for t in range(8):
                e = 8 * g + t + HALO
                sums = _nested_windows(lambda o: _tm_at(m3, e + o), [w // 2 + 1 - w for w in POOL_WINDOWS])
                dp3[_tm_rows(8 * g + t), :] = _by_group(sub, sums) * inv
            return c

        lax.fori_loop(0, TM // 8, pool, 0)

        _tm_read(da3, TM // 8, lambda r, l, tile: da_sc.__setitem__((r, l), tile))
        _tm_read(dp3, TM // 8, lambda r, l, tile: dp_sc.__setitem__((r, l), tile))
        sg = _sig(ag[...])
        da = da_sc[...]
        dz_ref[:, 0:D] = (da * sg).astype(BF16)
        dz_ref[:, D:2 * D] = (da * av[...] * (sg * (1.0 - sg))).astype(BF16)
        dz_ref[:, 2 * D:3 * D] = (dp_sc[...] - dm_c[...]).astype(BF16)
        dz_ref[:, 3 * D:] = dzg_ref[...]

        @pl.when(i == nt - 1)
        def _():
            _tm_read(dw3, 4, lambda r, l, tile: acc_ref.__setitem__((r, l), tile))
            ex.finish()

    tmaj = pltpu.VMEM((TM * NCB, 128), F32)
    text = pltpu.VMEM((TME * NCB, 128), F32)
    taps = pltpu.VMEM((32 * NCB, 128), F32)
    anys = pl.BlockSpec(memory_space=pl.ANY)
    res = pl.pallas_call(
        body, name="seq_bwd", grid=(nt,),
        in_specs=_halo_specs(0, nt) + _halo_specs(0, nt) + _halo_specs(0, nt) + _halo_specs(1, nt)
        + [pl.BlockSpec((TM, 2 * D), lambda i: (i, 0)), pl.BlockSpec((32, D), lambda i: (0, 0))] + [anys] * nq,
        out_specs=[pl.BlockSpec((TM, DIN), lambda i: (i, 0)), pl.BlockSpec((32, D), lambda i: (0, 0))] + [anys] * no,
        out_shape=[jax.ShapeDtypeStruct((tp, DIN), BF16), jax.ShapeDtypeStruct((32, D), F32)] + ex.out_shape,
        scratch_shapes=[text, text, text, tmaj, tmaj, taps, taps, pltpu.VMEM((TM, D), F32), pltpu.VMEM((TM, D), F32)]
        + ex.scratch,
        compiler_params=_params(("arbitrary",), 48),
    )(dac, dac, dac, dm, dm, dm, z, z, z, z, z, z, dzg, w_dw, *qs)
    return res[:2], res[2:]


def _in_bwd(dz, h0, dh1, g_mix, w_g, seq, qs):
    tp = h0.shape[0]
    tm = _pick(tp, TM_IO)
    nt = tp // tm
    ex = _ChipExchange(qs)
    nq = no = ex.n

    def body(*refs):
        dz_ref, h_ref, dh1_ref, g_ref, w_hbm = refs[:5]
        gx_ref, gmeta_ref, acc_ref = refs[5 + nq:8 + nq]
        w_vm, sems = refs[8 + nq + no:10 + nq + no]
        ex.bind(refs[5:5 + nq], refs[8 + nq:8 + nq + no], refs[10 + nq + no:])
        i = pl.program_id(0)

        @pl.when(i == 0)
        def _():
            ex.issue()
            acc_ref[...] = jnp.zeros_like(acc_ref)

        _load_once(i == 0, _win_pairs(w_hbm, w_vm), sems)

        du = _dot_nt(dz_ref[:, :DIN // 2], w_vm[0]) + _dot_nt(dz_ref[:, DIN // 2:], w_vm[1])
        h = h_ref[...]
        r = lax.rsqrt(jnp.mean(h * h, axis=-1, keepdims=True) + RMS_EPS)
        n0 = h * r
        acc_ref[0:1, :] += jnp.sum(du * n0, axis=0, keepdims=True)
        dn = du * g_ref[...]
        gx_ref[...] = dh1_ref[...] + r * (dn - n0 * jnp.mean(dn * n0, axis=-1, keepdims=True))

        @pl.when(i == nt - 1)
        def _():
            gmeta_ref[...] = gx_ref[pl.ds(tm - N_META, N_META), :]
            ex.finish()

    tile = pl.BlockSpec((tm, D), lambda i: (i, 0))
    anys = pl.BlockSpec(memory_space=pl.ANY)
    res = pl.pallas_call(
        body, name="in_bwd", grid=(nt,),
        in_specs=[pl.BlockSpec((tm, DIN), lambda i: (i, 0)), tile, tile, pl.BlockSpec((1, D), lambda i: (0, 0)), anys]
        + [anys] * nq,
        out_specs=[tile, pl.BlockSpec((N_META, D), lambda i: (0, 0)), pl.BlockSpec((8, D), lambda i: (0, 0))] + [anys] * no,
        out_shape=[jax.ShapeDtypeStruct((seq, D), F32), jax.ShapeDtypeStruct((N_META, D), F32),
                   jax.ShapeDtypeStruct((8, D), F32)] + ex.out_shape,
        scratch_shapes=[pltpu.VMEM((2, D, DIN // 2), BF16), pltpu.SemaphoreType.DMA((NDEV,))] + ex.scratch,
        compiler_params=_params(("arbitrary",), 58),
    )(dz, h0, dh1, g_mix, w_g, *qs)
    return res[:3], res[3:]


def _wgrad_in(u, dz, qs):
    tp = u.shape[0]
    tm = _pick(tp, TM_WG)
    nt = tp // tm
    half = DIN // 2
    ex = _ChipExchange(qs)
    nq = ex.n

    def body(*refs):
        u_ref, dz_ref = refs[:2]
        o_ref, acc = refs[2 + nq], refs[3 + 2 * nq]
        ex.bind(refs[2:2 + nq], refs[3 + nq:3 + 2 * nq], refs[4 + 2 * nq:])
        h, t = pl.program_id(0), pl.program_id(1)

        @pl.when((h == 0) & (t == 0))
        def _():
            ex.issue()

        @pl.when(t == 0)
        def _():
            acc[...] = jnp.zeros_like(acc)

        acc[...] += _dot_tn(u_ref[...], dz_ref[...])

        @pl.when(t == nt - 1)
        def _():
            for d in range(4):
                o_ref[d] = acc[:, INB * d:INB * (d + 1)].astype(BF16)

        @pl.when((h == 1) & (t == nt - 1))
        def _():
            ex.finish()

    anys = pl.BlockSpec(memory_space=pl.ANY)
    res = pl.pallas_call(
        body, name="wgrad_in", grid=(2, nt),
        in_specs=[pl.BlockSpec((tm, D), lambda h, t: (t, 0)), pl.BlockSpec((tm, half), lambda h, t: (t, h))] + [anys] * nq,
        out_specs=[pl.BlockSpec((4, D, INB), lambda h, t: (h, 0, 0), pipeline_mode=pl.Buffered(1))] + [anys] * nq,
        out_shape=[jax.ShapeDtypeStruct((NDEV, D, INB), BF16)] + ex.out_shape,
        scratch_shapes=[pltpu.VMEM((D, half), F32)] + ex.scratch,
        compiler_params=_params(("arbitrary", "arbitrary"), 52),
    )(u, dz, *qs)
    return res[0], res[1:]


def _wgrad_mix(s, dyc, q, dyp, merged, dh1, m, dm2):
    tp = s.shape[0]
    tm = _pick(tp, TM_WM)
    nt = tp // tm
    rb = D // NDEV

    def body(s_ref, dyc_ref, q_ref, dyp_ref, mg_ref, dh1_ref, m_ref, dm2_ref, o_ref, op_ref, acc, accp):
        t = pl.program_id(0)

        @pl.when(t == 0)
        def _():
            acc[...] = jnp.zeros_like(acc)
            accp[...] = jnp.zeros_like(accp)

        acc[0] += _dot_tn(s_ref[...], dyc_ref[...])
        acc[1] += _dot_tn(q_ref[...], dyp_ref[...])
        acc[2] += _dot_tn(mg_ref[...], dh1_ref[...].astype(BF16))
        for g in range(4):
            accp[g] += _dot_tn(m_ref[:, g * PG:(g + 1) * PG], dm2_ref[:, g * PG:(g + 1) * PG])

        @pl.when(t == nt - 1)
        def _():
            for d in range(NDEV):
                for k in range(3):
                    o_ref[d, k] = acc[k, rb * d:rb * (d + 1), :].astype(BF16)
                for g in range(4):
                    op_ref[d, g] = accp[g, 32 * d:32 * (d + 1), :].astype(BF16)

    tile = pl.BlockSpec((tm, D), lambda t: (t, 0))
    return pl.pallas_call(
        body, name="wgrad_mix", grid=(nt,),
        in_specs=[tile] * 8,
        out_specs=[pl.BlockSpec((NDEV, 3, rb, D), lambda t: (0, 0, 0, 0), pipeline_mode=pl.Buffered(1)),
                   pl.BlockSpec((NDEV, 4, 32, PG), lambda t: (0, 0, 0, 0), pipeline_mode=pl.Buffered(1))],
        out_shape=[jax.ShapeDtypeStruct((NDEV, 3, rb, D), BF16), jax.ShapeDtypeStruct((NDEV, 4, 32, PG), BF16)],
        scratch_shapes=[pltpu.VMEM((3, D, D), F32), pltpu.VMEM((4, PG, PG), F32)],
        compiler_params=_params(("arbitrary",), 56),
    )(s, dyc, q, dyp, merged, dh1, m, dm2)


def _wgrad_gu(v, dfg, dfu):
    tp = v.shape[0]
    tm = _pick(tp, TM_WG)
    nt = tp // tm

    def body(v_ref, dg_ref, du_ref, o_ref, acc):
        k, t = pl.program_id(0), pl.program_id(2)

        @pl.when(t == 0)
        def _():
            acc[...] = jnp.zeros_like(acc)

        @pl.when(k == 0)
        def _():
            acc[...] += _dot_tn(dg_ref[...], v_ref[...])

        @pl.when(k == 1)
        def _():
            acc[...] += _dot_tn(du_ref[...], v_ref[...])

        @pl.when(t == nt - 1)
        def _():
            for d in range(4):
                o_ref[d] = acc[FFB * d:FFB * (d + 1), :].astype(BF16)

    return pl.pallas_call(
        body, name="wgrad_gu", grid=(2, 2, nt),
        in_specs=[pl.BlockSpec((tm, D), lambda k, h, t: (t, 0)),
                  pl.BlockSpec((tm, FFC), lambda k, h, t: (t * (1 - k), h * (1 - k))),
                  pl.BlockSpec((tm, FFC), lambda k, h, t: (t * k, h * k))],
        out_specs=pl.BlockSpec((4, None, FFB, D), lambda k, h, t: (h, k, 0, 0), pipeline_mode=pl.Buffered(1)),
        out_shape=jax.ShapeDtypeStruct((NDEV, 2, FFB, D), BF16),
        scratch_shapes=[pltpu.VMEM((FFC, D), F32)],
        compiler_params=_params(("arbitrary",) * 3, 48),
    )(v, dfg, dfu)


def _wgrad_down(f, dh2):
    tp = f.shape[0]
    tm = _pick(tp, TM_WG)
    nt = tp // tm

    def body(f_ref, d_ref, o_ref, acc):
        t = pl.program_id(1)

        @pl.when(t == 0)
        def _():
            acc[...] = jnp.zeros_like(acc)

        acc[...] += _dot_tn(f_ref[...], d_ref[...].astype(BF16))

        @pl.when(t == nt - 1)
        def _():
            for d in range(4):
                o_ref[d] = acc[FFB * d:FFB * (d + 1), :].astype(BF16)

    return pl.pallas_call(
        body, name="wgrad_down", grid=(2, nt),
        in_specs=[pl.BlockSpec((tm, FFC), lambda h, t: (t, h)), pl.BlockSpec((tm, D), lambda h, t: (t, 0))],
        out_specs=pl.BlockSpec((4, FFB, D), lambda h, t: (h, 0, 0), pipeline_mode=pl.Buffered(1)),
        out_shape=jax.ShapeDtypeStruct((NDEV, FFB, D), BF16),
        scratch_shapes=[pltpu.VMEM((FFC, D), F32)],
        compiler_params=_params(("arbitrary", "arbitrary"), 48),
    )(f, dh2)


def kernel(x, meta_tokens, g_mix, w_in, b_gate, w_dw, b_dw, ln_g, ln_b, w_conv_out, w_pool, pool_scale, w_pool_out, w_o, g_ffn, w_ffn_gate, w_ffn_up, w_ffn_down, g_final, loss_target, m_meta_tokens, m_g_mix, m_w_in, m_b_gate, m_w_dw, m_b_dw, m_ln_g, m_ln_b, m_w_conv_out, m_w_pool, m_pool_scale, m_w_pool_out, m_w_o, m_g_ffn, m_w_ffn_gate, m_w_ffn_up, m_w_ffn_down, m_g_final, v_meta_tokens, v_g_mix, v_w_in, v_b_gate, v_w_dw, v_b_dw, v_ln_g, v_ln_b, v_w_conv_out, v_w_pool, v_pool_scale, v_w_pool_out, v_w_o, v_g_ffn, v_w_ffn_gate, v_w_ffn_up, v_w_ffn_down, v_g_final):
    seq = x.shape[1]
    tp = -(-(seq + 2 * HALO) // TM) * TM
    tm_in = _pick(tp, TM_IO)
    nx_last = seq - (tp // tm_in - 1) * tm_in
    assert 0 < nx_last <= tm_in - 2 * HALO and nx_last % 8 == 0 and 0 < seq - (tp // TM - 1) * TM

    whole = (Ellipsis,)
    ag_small = _Gather(
        [((48, D // NDEV), [(meta_tokens, pl.ds(0, N_META), whole), (w_dw, pl.ds(N_META, CONV_K), 0)])], [F32])
    ag_mix = _Gather([((3, D // NDEV, D), [(w_conv_out, 0, 0), (w_pool_out, 1, 0), (w_o, 2, 0)]),
                      ((4, PG // NDEV, PG), [(w_pool, whole, 0)])], [BF16, BF16])
    def tr(a):
        return jnp.swapaxes(a, 1, 2)

    ag_gu = _Gather([((2, FFB, D), [(tr(w_ffn_gate), 0, 0), (tr(w_ffn_up), 1, 0)])], [BF16])
    ag_dn = _Gather([((FFB, D), [(w_ffn_down, whole, 0)])], [BF16])

    mx, my = lax.axis_index("x"), lax.axis_index("y")
    order = jnp.stack([2 * mx + my, 2 * mx + 1 - my, 2 * (1 - mx) + my, 2 * (1 - mx) + 1 - my]).astype(jnp.int32)
    (h0, z, u, g_in), (g_mixw, g_pool), g_small = _fwd_in(x[0], g_mix, w_in, order, tp, ag_mix, ag_small)
    wdw_full = g_small.transpose(1, 0, 2).reshape(48, D)[N_META:]
    (ac, m), (w_gu,) = _seq_fwd(z, wdw_full, b_dw, seq, ag_gu)
    (h1, s, merged, q), (g_down,) = _mix_fwd(ac, m, z, h0, b_gate, ln_g, ln_b, pool_scale, g_mixw, g_pool, ag_dn)
    w_dn = g_down.reshape(2, FFC, D)
    fg, fu, v, f, dh2, head_acc = _ffn_fwd(h1, loss_target[0], g_ffn, g_final.reshape(1, D), w_gu, w_dn)

    dfg, dfu, dh1, ffn_acc = _ffn_bwd(dh2, fg, fu, h1, g_ffn, w_gu, w_dn)
    own_f, sib_f, q_f = _rs_pair("rs_pair_ffn", [_wgrad_gu(v, dfg, dfu), _wgrad_down(f, dh2)])
    (dac, dm, dzg, dyc, dyp, dm2, mix_acc), rel_dn = _mix_bwd(
        dh1, z, s, q, ac, m, b_gate, ln_g, ln_b, pool_scale, g_mixw, g_pool, q_f[1:])
    p_mix = _wgrad_mix(s, dyc, q, dyp, merged, dh1, m, dm2)
    own_m, sib_m, q_m = _rs_pair("rs_pair_mix", list(p_mix))
    (dz, seq_acc), rel_gu = _seq_bwd(dac, dm, dzg, z, wdw_full, seq, q_f[:1])
    rel_f = [rel_gu[0], rel_dn[0]]
    p_in, rel_m = _wgrad_in(u, dz, q_m)
    own_i, sib_i, q_i = _rs_pair("rs_pair_in", [p_in])
    (grad_x, g_meta, in_acc), rel_i = _in_bwd(dz, h0, dh1, g_mix, g_in, seq, q_i)
    small_g = jnp.concatenate([g_meta, seq_acc[:CONV_K], jnp.zeros((1, D), F32)], axis=0)
    p_small = small_g.reshape(48, NDEV, D // NDEV).transpose(1, 0, 2).astype(BF16)
    rep_g = jnp.concatenate([
        in_acc[0:1], mix_acc[0:1, :D], mix_acc[0:1, D:], seq_acc[CONV_K:CONV_K + 1], mix_acc[1:2, :D], mix_acc[1:2, D:],
        mix_acc[2:3, :D], ffn_acc[0:1], head_acc[1:2], head_acc[0:1], jnp.zeros((REP_ROWS - 10, D), F32)], axis=0)
    own_s, sib_s, rel_s, rep_all = _reduce_scatter([p_small], rep_g)
    owns = [own_i[0], own_s[0], own_m[0], own_m[1], own_f[0], own_f[1]]
    sibs = [sib_i[0], sib_s[0], sib_m[0], sib_m[1], sib_f[0], sib_f[1]]
    rels = [rel_i[0], rel_s[0], rel_m[0], rel_m[1], rel_f[0], rel_f[1]]

    def lead(a):
        return a.reshape(1, *a.shape)

    def stack4(a, lead_dims):
        return a.reshape(*lead_dims, 1, 4 * 32, PG)

    (r_in,) = _adamw_multi("adamw_in", lead(owns[0]), sibs[0][:, None], rels[0][:, None], [w_in], [m_w_in], [v_w_in], 4)
    r_meta, r_dw = _adamw_meta_dw(owns[1], sibs[1], rels[1], (meta_tokens, m_meta_tokens, v_meta_tokens),
                                  (w_dw, m_w_dw, v_w_dw))
    r_conv, r_pout, r_o = _adamw_multi("adamw_mix", owns[2], sibs[2], rels[2], [w_conv_out, w_pool_out, w_o],
                                       [m_w_conv_out, m_w_pool_out, m_w_o], [v_w_conv_out, v_w_pool_out, v_w_o], 1)
    (r_pool,) = _adamw_multi("adamw_pool", stack4(owns[3], ()), stack4(sibs[3], (1,)), stack4(rels[3], (3,)),
                             [w_pool.reshape(1, 128, PG)], [m_w_pool.reshape(1, 128, PG)], [v_w_pool.reshape(1, 128, PG)], 1)
    r_pool = tuple(a.reshape(w_pool.shape) for a in r_pool)
    r_gate, r_up = _adamw_multi("adamw_gu", owns[4], sibs[4], rels[4], [tr(w_ffn_gate), tr(w_ffn_up)],
                                [tr(m_w_ffn_gate), tr(m_w_ffn_up)], [tr(v_w_ffn_gate), tr(v_w_ffn_up)], 2)
    r_gate, r_up = tuple(tr(a) for a in r_gate), tuple(tr(a) for a in r_up)
    (r_down,) = _adamw_multi("adamw_down", lead(owns[5]), sibs[5][:, None], rels[5][:, None],
                             [w_ffn_down], [m_w_ffn_down], [v_w_ffn_down], 2)
    row = (1, D)
    loss, reps = _adamw_rep(
        rep_all,
        [g_mix, b_gate, b_dw, ln_g, ln_b, pool_scale, g_ffn, g_final.reshape(row)],
        [m_g_mix, m_b_gate, m_b_dw, m_ln_g, m_ln_b, m_pool_scale, m_g_ffn, m_g_final.reshape(row)],
        [v_g_mix, v_b_gate, v_b_dw, v_ln_g, v_ln_b, v_pool_scale, v_g_ffn, v_g_final.reshape(row)])
    r_gmix, r_bg, r_bdw, r_lg, r_lb, r_ps, r_gffn, r_gfin = reps
    r_gfin = tuple(a.reshape(D) for a in r_gfin)

    in_order = [r_meta, r_gmix, r_in, r_bg, r_dw, r_bdw, r_lg, r_lb, r_conv, r_pool, r_ps, r_pout, r_o, r_gffn,
                r_gate, r_up, r_down, r_gfin]
    return (loss.reshape(()), grad_x[None], *[r[0] for r in in_order], *[r[1] for r in in_order],
            *[r[2] for r in in_order], *[r[3] for r in in_order])
```

```python
import math

import jax
import jax.numpy as jnp
from jax import lax
from jax.experimental import pallas as pl
from jax.experimental.pallas import tpu as pltpu

F32, BF16 = jnp.float32, jnp.bfloat16
MESH_ID = pl.DeviceIdType.MESH
NDEV = 8

D = 1024
N_META = 16
CONV_K = 31
HALO = 16
POOL_WINDOWS = (2, 4, 8, 16)
PG = 256
DIN = 5 * D
DFF = 2816
FFB = DFF // NDEV
FFC = DFF // 2
INB = DIN // NDEV
RMS_EPS = 1e-6
LN_EPS = 1e-5
ADAM_LR, ADAM_B1, ADAM_B2, ADAM_EPS, ADAM_WD, ADAM_STEP = 0.001, 0.9, 0.999, 1e-08, 0.01, 10

TM = 384
TMS = 384
TM_IO = 704
TM_WG = 1408
TM_WM = 704
MIB = 2 ** 20


def _sig(x):
    return 0.5 * jnp.tanh(0.5 * x) + 0.5


def _dot(a, b):
    return jnp.dot(a, b, preferred_element_type=F32)


def _dot_nt(a, b):
    return lax.dot_general(a, b, (((1,), (1,)), ((), ())), preferred_element_type=F32)


def _dot_tn(a, b):
    return lax.dot_general(a, b, (((0,), (0,)), ((), ())), preferred_element_type=F32)


def _pick(tp, pref):
    return pref if tp % pref == 0 else TM


def _params(sem, vmem_mib):
    return pltpu.CompilerParams(dimension_semantics=sem, vmem_limit_bytes=vmem_mib * MIB)


def _load_once(first, pairs, sems):
    @pl.when(first)
    def _():
        cps = [pltpu.make_async_copy(s, d, sems.at[k]) for k, (s, d) in enumerate(pairs)]
        for cp in cps:
            cp.start()
        for cp in cps:
            cp.wait()


def _place():
    x, y, c = lax.axis_index("x"), lax.axis_index("y"), lax.axis_index("c")
    return x, y, c


class _Gather:
    def __init__(self, groups, dtypes):
        self.groups, self.dtypes, self.n = groups, dtypes, len(groups)
        self.arrays = [a for _, parts in groups for a, _, _ in parts]
        self.out_shape = [jax.ShapeDtypeStruct((NDEV, *s), dt) for (s, _), dt in zip(groups, dtypes)]
        self.scratch = [pltpu.VMEM(s, dt) for (s, _), dt in zip(groups, dtypes)] + [
            pltpu.SemaphoreType.DMA((7 * self.n,)), pltpu.SemaphoreType.DMA((7 * self.n,)),
            pltpu.SemaphoreType.DMA((self.n,))]

    def bind(self, ins, outs, scratch):
        self.ins, self.outs, self.stages = ins, outs, scratch[:self.n]
        self.send_sems, self.recv_sems, self.local_sems = scratch[self.n:]
        return self

    def _copy(self, w, k, block, to, src=None):
        dst = self.outs[w].at[4 * block[0] + 2 * block[1] + block[2]]
        return pltpu.make_async_remote_copy(
            src_ref=dst if src is None else src, dst_ref=dst,
            send_sem=self.send_sems.at[7 * w + k], recv_sem=self.recv_sems.at[7 * w + k],
            device_id=to, device_id_type=MESH_ID)

    def _first(self):
        x, y, c = _place()
        me, sibling = (x, y, c), (x, y, 1 - c)
        chips = [(1 - x, y), (x, 1 - y), (1 - x, 1 - y)]
        mine, first = [], []
        for w in range(self.n):
            mine.append(pltpu.make_async_copy(self.stages[w], self.outs[w].at[4 * x + 2 * y + c], self.local_sems.at[w]))
            first.append(self._copy(w, 0, me, sibling, src=self.stages[w]))
            first += [self._copy(w, 1 + j, me, (*chip, c), src=self.stages[w]) for j, chip in enumerate(chips)]
        return mine, first

    def _passed(self):
        x, y, c = _place()
        chips = [(1 - x, y), (x, 1 - y), (1 - x, 1 - y)]
        return [self._copy(w, 4 + j, (*chip, c), (x, y, 1 - c)) for w in range(self.n) for j, chip in enumerate(chips)]

    def issue(self):
        a = 0
        for w in range(self.n):
            shape, parts = self.groups[w]
            if sum(arr.size for arr, _, _ in parts) < math.prod(shape):
                self.stages[w][...] = jnp.zeros(shape, self.dtypes[w])
            for _, dst, src in parts:
                self.stages[w][dst] = self.ins[a][src].astype(self.dtypes[w])
                a += 1
        mine, first = self._first()
        for cp in mine + first:
            cp.start()

    def forward(self):
        x, y, c = _place()
        chips = [(1 - x, y), (x, 1 - y), (1 - x, 1 - y)]
        passed = self._passed()
        for w in range(self.n):
            for j, chip in enumerate(chips):
                self._copy(w, 1 + j, (*chip, c), (x, y, c)).wait_recv()
                passed[3 * w + j].start()

    def finish(self):
        x, y, c = _place()
        chips = [(1 - x, y), (x, 1 - y), (1 - x, 1 - y)]
        for w in range(self.n):
            self._copy(w, 0, (x, y, 1 - c), (x, y, c)).wait_recv()
            for j, chip in enumerate(chips):
                self._copy(w, 4 + j, (*chip, 1 - c), (x, y, c)).wait_recv()
        mine, first = self._first()
        for cp in first + self._passed():
            cp.wait_send()
        for cp in mine:
            cp.wait()


class _ChipExchange:
    def __init__(self, qs):
        self.n = len(qs)
        self.out_shape = [jax.ShapeDtypeStruct(q.shape, q.dtype) for q in qs]
        self.scratch = [pltpu.SemaphoreType.DMA((3 * self.n,)), pltpu.SemaphoreType.DMA((3 * self.n,))]

    def bind(self, qs, rels, scratch):
        self.qs, self.rels = qs, rels
        self.send_sems, self.recv_sems = scratch
        return self

    def _copies(self):
        x, y, c = _place()
        chips = [(1 - x, y), (x, 1 - y), (1 - x, 1 - y)]
        return [pltpu.make_async_remote_copy(
            src_ref=self.qs[w].at[j], dst_ref=self.rels[w].at[j],
            send_sem=self.send_sems.at[3 * w + j], recv_sem=self.recv_sems.at[3 * w + j],
            device_id=(*chips[j], c), device_id_type=MESH_ID) for w in range(self.n) for j in range(3)]

    def issue(self):
        for cp in self._copies():
            cp.start()

    def finish(self):
        cps = self._copies()
        for cp in cps:
            cp.wait_recv()
        for cp in cps:
            cp.wait_send()


def _reduce_scatter(parts, small):
    n = len(parts)
    blks = [p.shape[1:] for p in parts]

    def body(*refs):
        ps, small_ref = refs[:n], refs[n]
        o = n + 1
        owns, sibs, rels, small_out = refs[o:o + n], refs[o + n:o + 2 * n], refs[o + 2 * n:o + 3 * n], refs[o + 3 * n]
        o += 3 * n + 1
        pa, pb, qst = refs[o:o + n], refs[o + n:o + 2 * n], refs[o + 2 * n:o + 3 * n]
        s1_send, s1_recv, s2_send, s2_recv, sm_send, sm_recv, lsem = refs[o + 3 * n:]
        x, y, c = _place()
        me = 4 * x + 2 * y + c
        sibling = (x, y, 1 - c)
        chips = [(1 - x, y), (x, 1 - y), (1 - x, 1 - y)]
        all_chips = [(x, y)] + chips

        own_cps = []
        for w in range(n):
            cp = pltpu.make_async_copy(ps[w].at[me], owns[w], lsem.at[w])
            cp.start()
            own_cps.append(cp)
        sm_own = pltpu.make_async_copy(small_ref, small_out.at[me], lsem.at[n])
        sm_own.start()

        def small_copy(r):
            peer = ((x + (r >> 2)) % 2, (y + ((r >> 1) & 1)) % 2, (c + (r & 1)) % 2)
            return pltpu.make_async_remote_copy(
                src_ref=small_ref, dst_ref=small_out.at[me], send_sem=sm_send.at[r - 1], recv_sem=sm_recv.at[r - 1],
                device_id=peer, device_id_type=MESH_ID)

        sm_cps = [small_copy(r) for r in range(1, NDEV)]
        for cp in sm_cps:
            cp.start()

        def pair_copy(w, rel):
            cx, cy = all_chips[rel]
            return pltpu.make_async_remote_copy(
                src_ref=ps[w].at[4 * cx + 2 * cy + (1 - c)], dst_ref=sibs[w].at[rel],
                send_sem=s1_send.at[4 * w + rel], recv_sem=s1_recv.at[4 * w + rel],
                device_id=sibling, device_id_type=MESH_ID)

        def chip_copy(w, j):
            return pltpu.make_async_remote_copy(
                src_ref=qst[w].at[j], dst_ref=rels[w].at[j],
                send_sem=s2_send.at[3 * w + j], recv_sem=s2_recv.at[3 * w + j],
                device_id=(*chips[j], c), device_id_type=MESH_ID)

        pair_cps = [pair_copy(w, rel) for w in range(n) for rel in (1, 2, 3, 0)]
        for cp in pair_cps:
            cp.start()
        chip_cps = []
        for w in range(n):
            for j, (cx, cy) in enumerate(chips):
                pair_copy(w, 1 + j).wait_recv()
                la = pltpu.make_async_copy(ps[w].at[4 * cx + 2 * cy + c], pa[w], lsem.at[n + 1])
                lb = pltpu.make_async_copy(sibs[w].at[1 + j], pb[w], lsem.at[n + 2])
                la.start()
                lb.start()
                la.wait()
                lb.wait()
                qst[w][j] = (pa[w][...].astype(F32) + pb[w][...].astype(F32)).astype(BF16)
                cp = chip_copy(w, j)
                cp.start()
                chip_cps.append(cp)
        for w in range(n):
            pair_copy(w, 0).wait_recv()
            for j in range(3):
                chip_copy(w, j).wait_recv()
        for cp in sm_cps:
            cp.wait_recv()
        for cp in pair_cps + chip_cps + sm_cps:
            cp.wait_send()
        for cp in own_cps:
            cp.wait()
        sm_own.wait()

    any_spec = pl.BlockSpec(memory_space=pl.ANY)
    outs = pl.pallas_call(
        body, name="rs_grads",
        out_shape=[jax.ShapeDtypeStruct(b, BF16) for b in blks]
        + [jax.ShapeDtypeStruct((4, *b), BF16) for b in blks]
        + [jax.ShapeDtypeStruct((3, *b), BF16) for b in blks]
        + [jax.ShapeDtypeStruct((NDEV, *small.shape), F32)],
        in_specs=[any_spec] * (n + 1),
        out_specs=[any_spec] * (3 * n + 1),
        scratch_shapes=[pltpu.VMEM(b, BF16) for b in blks] + [pltpu.VMEM(b, BF16) for b in blks]
        + [pltpu.VMEM((3, *b), BF16) for b in blks]
        + [pltpu.SemaphoreType.DMA((4 * n,)), pltpu.SemaphoreType.DMA((4 * n,)),
           pltpu.SemaphoreType.DMA((3 * n,)), pltpu.SemaphoreType.DMA((3 * n,)),
           pltpu.SemaphoreType.DMA((NDEV - 1,)), pltpu.SemaphoreType.DMA((NDEV - 1,)),
           pltpu.SemaphoreType.DMA((n + 3,))],
        compiler_params=pltpu.CompilerParams(vmem_limit_bytes=40 * MIB),
    )(*parts, small)
    return outs[:n], outs[n:2 * n], outs[2 * n:3 * n], outs[3 * n]


class _PairSum:
    def __init__(self, parts, keep_q=True):
        self.n = n = len(parts)
        self.keep_q = keep_q
        blks = [p.shape[1:] for p in parts]
        self.out_shape = [jax.ShapeDtypeStruct(b, BF16) for b in blks] + [jax.ShapeDtypeStruct((1, *b), BF16) for b in blks]
        if keep_q:
            self.out_shape += [jax.ShapeDtypeStruct((3, *b), BF16) for b in blks]
        self.scratch = [pltpu.VMEM((3, *b), BF16) for b in blks] * 3 + [
            pltpu.SemaphoreType.DMA((4 * n,)), pltpu.SemaphoreType.DMA((4 * n,)), pltpu.SemaphoreType.DMA((5 * n,))]

    def bind(self, ps, outs, scratch):
        n = self.n
        self.ps, self.owns, self.sibs, self.qs = ps, outs[:n], outs[n:2 * n], outs[2 * n:]
        self.pa, self.pb, self.qst = scratch[:n], scratch[n:2 * n], scratch[2 * n:3 * n]
        self.s_send, self.s_recv, self.lsem = scratch[3 * n:]
        return self

    def _local(self, with_q):
        n = self.n
        x, y, c = _place()
        chips = [(1 - x, y), (x, 1 - y), (1 - x, 1 - y)]
        own = [pltpu.make_async_copy(self.ps[w].at[4 * x + 2 * y + c], self.owns[w], self.lsem.at[w]) for w in range(n)]
        mine = [[pltpu.make_async_copy(self.ps[w].at[4 * cx + 2 * cy + c], self.pa[w].at[j], self.lsem.at[2 * n + 3 * w + j])
                 for j, (cx, cy) in enumerate(chips)] for w in range(n)]
        outq = [pltpu.make_async_copy(self.qst[w], self.qs[w], self.lsem.at[n + w]) for w in range(n)] if with_q else []
        return own, mine, outq

    def _pair(self, w, rel):
        x, y, c = _place()
        cx, cy = [(x, y), (1 - x, y), (x, 1 - y), (1 - x, 1 - y)][rel]
        return pltpu.make_async_remote_copy(
            src_ref=self.ps[w].at[4 * cx + 2 * cy + (1 - c)],
            dst_ref=self.sibs[w].at[0] if rel == 0 else self.pb[w].at[rel - 1],
            send_sem=self.s_send.at[4 * w + rel], recv_sem=self.s_recv.at[4 * w + rel],
            device_id=(x, y, 1 - c), device_id_type=MESH_ID)

    def issue(self):
        own, mine, _ = self._local(False)
        for cp in own + [cp for row in mine for cp in row]:
            cp.start()
        for w in range(self.n):
            for rel in (1, 2, 3, 0):
                self._pair(w, rel).start()

    def finish(self):
        own, mine, outq = self._local(self.keep_q)
        for w in range(self.n):
            for j in range(3):
                self._pair(w, 1 + j).wait_recv()
                mine[w][j].wait()
                self.qst[w][j] = (self.pa[w][j].astype(F32) + self.pb[w][j].astype(F32)).astype(BF16)
            if self.keep_q:
                outq[w].start()
        for w in range(self.n):
            self._pair(w, 0).wait_recv()
        for w in range(self.n):
            for rel in range(4):
                self._pair(w, rel).wait_send()
        for cp in own + outq:
            cp.wait()

    def results(self, outs):
        n = self.n
        return outs[:n], outs[n:2 * n], outs[2 * n:3 * n]


def _rs_pair(name, parts):
    ps = _PairSum(parts)
    n = ps.n

    def body(*refs):
        ps.bind(refs[:n], refs[n:4 * n], refs[4 * n:])
        ps.issue()
        ps.finish()

    any_spec = pl.BlockSpec(memory_space=pl.ANY)
    outs = pl.pallas_call(
        body, name=name, out_shape=ps.out_shape,
        in_specs=[any_spec] * n, out_specs=[any_spec] * (3 * n), scratch_shapes=ps.scratch,
        compiler_params=pltpu.CompilerParams(vmem_limit_bytes=48 * MIB),
    )(*parts)
    return ps.results(outs)


def _adamw_math(g, w, m, v):
    m = ADAM_B1 * m + (1.0 - ADAM_B1) * g
    v = ADAM_B2 * v + (1.0 - ADAM_B2) * (g * g)
    m_hat = m / (1.0 - ADAM_B1 ** ADAM_STEP)
    v_hat = v / (1.0 - ADAM_B2 ** ADAM_STEP)
    delta = -ADAM_LR * (m_hat / (jnp.sqrt(v_hat) + ADAM_EPS) + ADAM_WD * w)
    return delta, m, v


def _adamw_multi(name, own, sib, rel, ws, ms, vs, row_grid):
    k_n, r_n, c_n = own.shape
    rbk = r_n // row_grid

    def body(*refs):
        own_ref, sib_ref, r0_ref, r1_ref, r2_ref = refs[:5]
        w_refs, m_refs, v_refs = refs[5:5 + k_n], refs[5 + k_n:5 + 2 * k_n], refs[5 + 2 * k_n:5 + 3 * k_n]
        outs = refs[5 + 3 * k_n:]
        for k in range(k_n):
            g = own_ref[k].astype(F32) + sib_ref[k].astype(F32)
            g = g + r0_ref[k].astype(F32)
            g = g + r1_ref[k].astype(F32)
            g = g + r2_ref[k].astype(F32)
            delta, mm, vv = _adamw_math(g, w_refs[k][0], m_refs[k][0], v_refs[k][0])
            outs[4 * k][0] = g
            outs[4 * k + 1][0] = delta
            outs[4 * k + 2][0] = mm
            outs[4 * k + 3][0] = vv

    def lead(j):
        return pl.BlockSpec((None, k_n, rbk, c_n), lambda g: (j, 0, g, 0))

    wspec = pl.BlockSpec((1, rbk, c_n), lambda g: (0, g, 0))
    shp = jax.ShapeDtypeStruct((1, r_n, c_n), F32)
    res = pl.pallas_call(
        body, name=name, grid=(row_grid,),
        in_specs=[pl.BlockSpec((k_n, rbk, c_n), lambda g: (0, g, 0)), lead(0), lead(0), lead(1), lead(2)] + [wspec] * (3 * k_n),
        out_specs=[wspec] * (4 * k_n), out_shape=[shp] * (4 * k_n),
        compiler_params=_params(("arbitrary",), 40),
    )(own, sib, rel, rel, rel, *ws, *ms, *vs)
    return [tuple(res[4 * k:4 * k + 4]) for k in range(k_n)]


def _adamw_meta_dw(own, sib, rel, meta, dw):
    def body(own_ref, sib_ref, rel_ref, wm, mm, vm, wd, md, vd, *outs):
        def gsum(rows):
            g = own_ref[rows, :].astype(F32) + sib_ref[0, rows, :].astype(F32)
            for j in range(3):
                g = g + rel_ref[j, rows, :].astype(F32)
            return g

        g = gsum(pl.ds(0, N_META))
        delta, m2, v2 = _adamw_math(g, wm[...], mm[...], vm[...])
        for o, val in zip(outs[:4], (g, delta, m2, v2)):
            o[...] = val
        g = gsum(pl.ds(N_META, CONV_K))
        delta, m2, v2 = _adamw_math(g, wd[0], md[0], vd[0])
        for o, val in zip(outs[4:], (g, delta, m2, v2)):
            o[0] = val

    s_meta = jax.ShapeDtypeStruct(meta[0].shape, F32)
    s_dw = jax.ShapeDtypeStruct(dw[0].shape, F32)
    res = pl.pallas_call(body, name="adamw_meta_dw", out_shape=[s_meta] * 4 + [s_dw] * 4)(own, sib, rel, *meta, *dw)
    return tuple(res[:4]), tuple(res[4:])


REP_ROWS = 16


def _adamw_rep(gathered, ws, ms, vs):
    rows = [(0, 1), (1, 2), (3, 1), (4, 1), (5, 1), (6, 1), (7, 1), (8, 1)]

    def body(g_ref, *refs):
        w_refs, m_refs, v_refs = refs[:8], refs[8:16], refs[16:24]
        loss_ref, outs, acc = refs[24], refs[25:57], refs[57]
        g = g_ref[0]
        for d in range(1, NDEV):
            g = g + g_ref[d]
        acc[...] = g
        loss_ref[...] = (0.5 / D) * jnp.sum(acc[pl.ds(9, 1), :], axis=1, keepdims=True)
        for p, (r0, nr) in enumerate(rows):
            for h in range(nr):
                cols = pl.ds(h * D, D)
                gp = acc[pl.ds(r0 + h, 1), :]
                delta, mm, vv = _adamw_math(gp, w_refs[p][:, cols], m_refs[p][:, cols], v_refs[p][:, cols])
                for o, val in zip(outs[4 * p:4 * p + 4], (gp, delta, mm, vv)):
                    o[:, cols] = val

    shapes = [jax.ShapeDtypeStruct(w.shape, F32) for w in ws]
    res = pl.pallas_call(
        body, name="adamw_rep",
        out_shape=[jax.ShapeDtypeStruct((1, 1), F32)] + [s for s in shapes for _ in range(4)],
        scratch_shapes=[pltpu.VMEM((REP_ROWS, D), F32)],
    )(gathered, *ws, *ms, *vs)
    return res[0], [tuple(res[1 + 4 * p:5 + 4 * p]) for p in range(8)]


def _load_ffn(i, j, wgu_hbm, wgu, wdn_hbm, wdn, sems):
    half = NDEV // 2

    def copies(ch):
        pairs = [(wgu_hbm.at[half * ch + d, g], wgu.at[g, ch, pl.ds(FFB * d, FFB), :]) for g in range(2) for d in range(half)]
        pairs.append((wdn_hbm.at[ch], wdn.at[ch]))
        return [pltpu.make_async_copy(s, t, sems.at[(2 * half + 1) * ch + k]) for k, (s, t) in enumerate(pairs)]

    @pl.when((i == 0) & (j == 0))
    def _():
        for cp in copies(0) + copies(1):
            cp.start()

    for ch in range(2):
        @pl.when((i == 0) & (j == ch))
        def _():
            for cp in copies(ch):
                cp.wait()


def _win_pairs(w_hbm, w_vm):
    return [(w_hbm.at[q], w_vm.at[q // 2, :, pl.ds(2 * INB * (q % 2), 2 * INB)]) for q in range(4)]


def _whole(a):
    nd = a.ndim
    return pl.BlockSpec(a.shape, lambda *g: (0,) * nd)


CHIPW = 2 * INB
PHASE_CHIP = (1, 0, 2)
assert PHASE_CHIP[2] == 2


class _GatherIn:
    scratch = [pltpu.VMEM((D, INB), BF16), pltpu.SemaphoreType.DMA((7,)), pltpu.SemaphoreType.DMA((7,)),
               pltpu.SemaphoreType.DMA((1,))]

    def bind(self, w_ref, w_vm, scratch):
        self.w_ref, self.w_vm = w_ref, w_vm
        self.stage, self.send_sems, self.recv_sems, self.local_sem = scratch
        return self

    def _win(self, chip, core):
        return self.w_vm.at[2 * chip[0] + chip[1], core]

    def _copy(self, k, chip, core, to, src=None):
        dst = self._win(chip, core)
        return pltpu.make_async_remote_copy(
            src_ref=dst if src is None else src, dst_ref=dst, send_sem=self.send_sems.at[k],
            recv_sem=self.recv_sems.at[k], device_id=to, device_id_type=MESH_ID)

    def _mine(self, cs):
        x, y, _ = _place()
        return pltpu.make_async_copy(self.stage, self._win((x, y), cs), self.local_sem.at[0])

    def issue(self, cs):
        x, y, _ = _place()
        chips = [(1 - x, y), (x, 1 - y), (1 - x, 1 - y)]
        self.stage[...] = self.w_ref[0].astype(BF16)
        self._mine(cs).start()
        self._copy(0, (x, y), cs, (x, y, 1 - cs), src=self.stage).start()
        for j in PHASE_CHIP[:2]:
            self._copy(1 + j, (x, y), cs, (*chips[j], cs), src=self.stage).start()

    def wait_chip(self, phase, cs):
        x, y, _ = _place()
        chips = [(1 - x, y), (x, 1 - y), (1 - x, 1 - y)]
        if phase == 0:
            self._mine(cs).wait()
            self._copy(0, (x, y), 1 - cs, (x, y, cs)).wait_recv()
            return
        if phase == 1:
            for j in PHASE_CHIP[:2]:
                self._copy(1 + j, chips[j], cs, (x, y, cs)).wait_recv()
                self._copy(4 + j, chips[j], cs, (x, y, 1 - cs)).start()
            self._copy(3, (x, y), cs, (*chips[2], cs), src=self.stage).start()
        j = PHASE_CHIP[phase - 1]
        if phase == 3:
            self._copy(1 + j, chips[j], cs, (x, y, cs)).wait_recv()
            self._copy(4 + j, chips[j], cs, (x, y, 1 - cs)).start()
        self._copy(4 + j, chips[j], 1 - cs, (x, y, cs)).wait_recv()

    def finish(self, cs):
        x, y, _ = _place()
        for k in range(7):
            self._copy(k, (x, y), cs, (x, y, cs), src=self.stage).wait_send()


def _fwd_in(x2, g_mix, w_in, order, tp, ag, ags):
    tm = _pick(tp, TM_IO)
    nt = tp // tm
    nx_last = x2.shape[0] - (nt - 1) * tm
    na, ng, ns = len(ag.arrays), ag.n, len(ags.arrays)
    gin = _GatherIn()

    def body(order_ref, *refs):
        x_ref, g_ref, w_ref = refs[:3]
        o = 3 + na + ns
        h_ref, z_ref, u_ref, wout_ref = refs[o:o + 4]
        s = o + 4 + ng + 1
        w_vm, u_all, osem, sm_vm = refs[s:s + 4]
        gin.bind(w_ref, w_vm, refs[s + 4:s + 8])
        ag.bind(refs[3:3 + na], refs[o + 4:o + 4 + ng], refs[s + 8:s + 8 + len(ag.scratch)])
        ags.bind(refs[3 + na:3 + na + ns], refs[o + 4 + ng:o + 5 + ng], refs[s + 8 + len(ag.scratch):])
        ph, i = pl.program_id(0), pl.program_id(1)
        core = lax.axis_index("c")
        first = (ph == 0) & (i == 0)
        last = (ph == 3) & (i == nt - 1)
        @pl.when(first)
        def _():
            ags.issue()

        for cs in range(2):
            @pl.when(first & (core == cs))
            def _():
                gin.issue(cs)

        @pl.when((ph == 0) & (i == max(nt - 2, 0)))
        def _():
            ags.forward()

        for cs in range(2):
            for p in range(4):
                @pl.when((ph == p) & (i == 0) & (core == cs))
                def _():
                    gin.wait_chip(p, cs)

        @pl.when((ph == 2) & (i == 0))
        def _():
            ag.issue()

        out_copies = [pltpu.make_async_copy(w_vm.at[k, c], wout_ref.at[k, :, pl.ds(INB * c, INB)], osem.at[2 * k + c])
                      for k in range(4) for c in range(2)]

        @pl.when((ph == 3) & (i == 0))
        def _():
            for cp in out_copies:
                cp.start()

        @pl.when((ph == 0) & (i < nt - 1))
        def _():
            h_ref[...] = x_ref[...]

        @pl.when((ph == 0) & (i == nt - 1))
        def _():
            ags.finish()
            cp = pltpu.make_async_copy(ags.outs[0], sm_vm, osem.at[8])
            cp.start()
            h_ref[pl.ds(0, nx_last), :] = x_ref[pl.ds(0, nx_last), :]
            h_ref[pl.ds(nx_last, tm - nx_last - N_META), :] = jnp.zeros((tm - nx_last - N_META, D), F32)
            cp.wait()
            for d in range(NDEV):
                h_ref[pl.ds(tm - N_META, N_META), pl.ds(128 * d, 128)] = sm_vm[d, pl.ds(0, N_META), :]

        @pl.when(ph == 0)
        def _():
            xv = h_ref[...]
            r = lax.rsqrt(jnp.mean(xv * xv, axis=-1, keepdims=True) + RMS_EPS)
            u = (xv * r * g_ref[...]).astype(BF16)
            u_ref[...] = u
            u_all[i] = u

        for c in range(2):
            z_ref[:, INB * c:INB * (c + 1)] = _dot(u_all[i], w_vm[order_ref[ph], c])

        @pl.when(last)
        def _():
            ag.forward()
            ag.finish()
            for cp in out_copies:
                cp.wait()

        for cs in range(2):
            @pl.when(last & (core == cs))
            def _():
                gin.finish(cs)

    def rows(ph, i, order):
        return (jnp.where(ph == 0, i, nt - 1), 0)

    tile = pl.BlockSpec((tm, D), rows)
    anys = pl.BlockSpec(memory_space=pl.ANY)
    res = pl.pallas_call(
        body, name="fwd_in",
        grid_spec=pltpu.PrefetchScalarGridSpec(
            num_scalar_prefetch=1, grid=(4, nt),
            in_specs=[tile, pl.BlockSpec((1, D), lambda ph, i, order: (0, 0)), _whole(w_in)]
            + [_whole(a) for a in ag.arrays + ags.arrays],
            out_specs=[tile, pl.BlockSpec((tm, CHIPW), lambda ph, i, order: (i, order[ph])), tile, anys] + [anys] * (ng + 1),
            scratch_shapes=[pltpu.VMEM((4, 2, D, INB), BF16), pltpu.VMEM((nt, tm, D), BF16), pltpu.SemaphoreType.DMA((9,)),
                            pltpu.VMEM(ags.out_shape[0].shape, F32)] + gin.scratch + ag.scratch + ags.scratch),
        out_shape=[jax.ShapeDtypeStruct((tp, D), F32), jax.ShapeDtypeStruct((tp, DIN), F32),
                   jax.ShapeDtypeStruct((tp, D), BF16), jax.ShapeDtypeStruct((4, D, CHIPW), BF16)]
        + ag.out_shape + ags.out_shape,
        compiler_params=_params(("arbitrary", "arbitrary"), 58),
    )(order, x2, g_mix, w_in, *ag.arrays, *ags.arrays)
    return res[:4], res[4:4 + ng], res[4 + ng]


def _halo_specs(col, nt, width=D):
    r = TM // HALO
    nb = nt * r
    return [pl.BlockSpec((HALO, width), lambda i: ((i * r + nb - 1) % nb, col)),
            pl.BlockSpec((TM, width), lambda i: (i, col)),
            pl.BlockSpec((HALO, width), lambda i: (((i + 1) * r) % nb, col))]


NCB = D // 128
TME = TM + 2 * HALO


def _tm_fill(dst, time0, groups, tile_fn):
    def body(g, c):
        for j in range(NCB):
            dst[pl.ds((time0 + 8 * g) * NCB + j, 8, stride=NCB), :] = tile_fn(pl.multiple_of(8 * g, 8), pl.ds(128 * j, 128))
        return c

    lax.fori_loop(0, groups, body, 0)


def _tm_fill_ext(dst, left, cur, right, fn):
    _tm_fill(dst, 0, HALO // 8, lambda r, l: fn(left, pl.ds(r, 8), l))
    _tm_fill(dst, HALO, TM // 8, lambda r, l: fn(cur, pl.ds(r, 8), l))
    _tm_fill(dst, HALO + TM, HALO // 8, lambda r, l: fn(right, pl.ds(r, 8), l))


def _tm_read(src, groups, store_fn):
    def body(g, c):
        for j in range(NCB):
            store_fn(pl.ds(pl.multiple_of(8 * g, 8), 8), pl.ds(128 * j, 128), src[pl.ds(8 * g * NCB + j, 8, stride=NCB), :])
        return c

    lax.fori_loop(0, groups, body, 0)


def _tm_rows(t):
    return pl.ds(t * NCB if isinstance(t, int) else pl.multiple_of(t * NCB, NCB), NCB)


def _tm_at(ref, t):
    return ref[_tm_rows(t), :]


def _by_group(sub, vals):
    return jnp.where(sub < 2, vals[0], jnp.where(sub < 4, vals[1], jnp.where(sub < 6, vals[2], vals[3])))


def _pool_cnt(b, seq, tp, sub):
    b = jnp.where(b < 0, b + tp, b)
    b = jnp.where(b >= tp, b - tp, b)
    t = jnp.where(b < seq, b + N_META, b - (tp - N_META))
    cnts = []
    for win in POOL_WINDOWS:
        left = win // 2
        lo = jnp.maximum(t - left, 0)
        hi = jnp.minimum(t + win - left, seq + N_META)
        cnts.append(jnp.maximum(hi - lo, 1).astype(F32))
    return _by_group(sub, cnts)


def _edge_rows(seq, tp):
    reach = max(POOL_WINDOWS) // 2
    return [tp - N_META + t for t in range(reach)] + [seq - reach + 1 + t for t in range(reach - 1)]


def _edge_gain(b, seq, tp, sub):
    return _by_group(sub, [float(w) for w in POOL_WINDOWS]) / _pool_cnt(b, seq, tp, sub)


def _nested_windows(at, lo_offs):
    sums, s, have = [], None, set()
    for g, win in enumerate(POOL_WINDOWS):
        for o in range(lo_offs[g], lo_offs[g] + win):
            if o not in have:
                have.add(o)
                s = at(o) if s is None else s + at(o)
        sums.append(s)
    return sums


def _seq_fwd(z, w_dw, b_dw, seq, gat):
    tp = z.shape[0]
    nt = tp // TM
    na, ng = len(gat.arrays), gat.n

    def body(*refs):
        av_l, av, av_r, ag_l, ag, ag_r, p_l, p, p_r, w_ref, b_ref = refs[:11]
        ac_ref, m_ref = refs[11 + na:13 + na]
        a3, p3, o3, m3, w3, b3, m2d = refs[13 + na + ng:20 + na + ng]
        gat.bind(refs[11:11 + na], refs[13 + na:13 + na + ng], refs[20 + na + ng:])
        i = pl.program_id(0)
        sub = lax.broadcasted_iota(jnp.int32, (NCB, 128), 0)

        @pl.when(i == 0)
        def _():
            gat.issue()
            _tm_fill(w3, 0, 4, lambda r, l: w_ref[pl.ds(r, 8), l])
            for j in range(NCB):
                b3[pl.ds(j, 1), :] = b_ref[:, pl.ds(128 * j, 128)]

        @pl.when(i == max(nt - 2, 0))
        def _():
            gat.forward()

        _tm_fill_ext(a3, (av_l, ag_l), (av, ag), (av_r, ag_r), lambda vg, r, l: vg[0][r, l] * _sig(vg[1][r, l]))
        _tm_fill_ext(p3, p_l, p, p_r, lambda ref, r, l: ref[r, l])

        def conv(g, c):
            accs = [b3[...]] * 16
            for k in range(CONV_K):
                wk = _tm_at(w3, k)
                for t in range(16):
                    accs[t] = accs[t] + wk * _tm_at(a3, 16 * g + t + k + 1)
            for t in range(16):
                o3[_tm_rows(16 * g + t), :] = accs[t]
            return c

        lax.fori_loop(0, TM // 16, conv, 0)
        _tm_read(o3, TM // 8, lambda r, l, tile: ac_ref.__setitem__((r, l), tile))

        inv = _by_group(sub, [1.0 / w for w in POOL_WINDOWS])

        def pool(g, c):
            for t in range(8):
                e = 8 * g + t + HALO
                sums = _nested_windows(lambda o: _tm_at(p3, e + o), [-(w // 2) for w in POOL_WINDOWS])
                m3[_tm_rows(8 * g + t), :] = _by_group(sub, sums) * inv - _tm_at(p3, e)
            return c

        lax.fori_loop(0, TM // 8, pool, 0)
        for b in _edge_rows(seq, tp):
            r = b - i * TM

            @pl.when((r >= 0) & (r < TM))
            def _():
                pv = _tm_at(p3, r + HALO)
                m3[_tm_rows(r), :] = (_tm_at(m3, r) + pv) * _edge_gain(b, seq, tp, sub) - pv

        _tm_read(m3, TM // 8, lambda r, l, tile: m2d.__setitem__((r, l), tile))
        m_ref[...] = m2d[...].astype(BF16)

        @pl.when(i == nt - 1)
        def _():
            gat.finish()

    tmaj = pltpu.VMEM((TM * NCB, 128), F32)
    text = pltpu.VMEM((TME * NCB, 128), F32)
    res = pl.pallas_call(
        body, name="seq_fwd", grid=(nt,),
        in_specs=_halo_specs(0, nt) + _halo_specs(1, nt) + _halo_specs(2, nt)
        + [pl.BlockSpec((32, D), lambda i: (0, 0)), pl.BlockSpec((1, D), lambda i: (0, 0))] + [_whole(a) for a in gat.arrays],
        out_specs=[pl.BlockSpec((TM, D), lambda i: (i, 0))] * 2 + [pl.BlockSpec(memory_space=pl.ANY)] * ng,
        out_shape=[jax.ShapeDtypeStruct((tp, D), F32), jax.ShapeDtypeStruct((tp, D), BF16)] + gat.out_shape,
        scratch_shapes=[text, text, tmaj, tmaj, pltpu.VMEM((32 * NCB, 128), F32), pltpu.VMEM((NCB, 128), F32),
                        pltpu.VMEM((TM, D), F32)] + gat.scratch,
        compiler_params=_params(("arbitrary",), 52),
    )(z, z, z, z, z, z, z, z, z, w_dw, b_dw, *gat.arrays)
    return res[:2], res[2:]


def _ln_stats(ac):
    mu = jnp.mean(ac, axis=-1, keepdims=True)
    xc = ac - mu
    rl = lax.rsqrt(jnp.mean(xc * xc, axis=-1, keepdims=True) + LN_EPS)
    return xc * rl, rl


def _pool_mix(m, wp_ref):
    return jnp.concatenate(
        [_dot(m[:, g * PG:(g + 1) * PG], wp_ref[:, g].reshape(PG, PG)) for g in range(4)], axis=1)


def _mix_fwd(ac, m, z, h0, b_gate, ln_g, ln_b, pool_scale, g_mixw, g_pool, gat):
    tp = h0.shape[0]
    tms = TM
    nt = tp // tms
    na, ng = len(gat.arrays), gat.n

    def body(*refs):
        ac_ref, m_ref, zga, zgb, h_ref, bg_ref, lg_ref, lb_ref, ps_ref, wm_hbm, wp_hbm = refs[:11]
        h1_ref, s_ref, mg_ref, q_ref = refs[11 + na:15 + na]
        wm, wp, sems = refs[15 + na + ng:18 + na + ng]
        gat.bind(refs[11:11 + na], refs[15 + na:15 + na + ng], refs[18 + na + ng:])
        i = pl.program_id(0)

        @pl.when(i == 0)
        def _():
            gat.issue()

        @pl.when(i == max(nt - 4, 0))
        def _():
            gat.forward()

        @pl.when(i == nt - 1)
        def _():
            gat.finish()

        _load_once(i == 0, [(wm_hbm, wm), (wp_hbm, wp)], sems)
        n, _ = _ln_stats(ac_ref[...])
        l = n * lg_ref[...] + lb_ref[...]
        s = (l * _sig(l)).astype(BF16)
        s_ref[...] = s
        yc = _dot(s, wm[:, 0].reshape(D, D))
        q = (_pool_mix(m_ref[...], wp) * ps_ref[...]).astype(BF16)
        q_ref[...] = q
        yp = _dot(q, wm[:, 1].reshape(D, D))
        ga = _sig(zga[...] + bg_ref[:, :D])
        gb = _sig(zgb[...] + bg_ref[:, D:])
        merged = (ga * yc + gb * yp).astype(BF16)
        mg_ref[...] = merged
        h1_ref[...] = h_ref[...] + _dot(merged, wm[:, 2].reshape(D, D))

    def tile(col=0):
        return pl.BlockSpec((tms, D), lambda i: (i, col))

    def vec(w):
        return pl.BlockSpec((1, w), lambda i: (0, 0))

    anys = pl.BlockSpec(memory_space=pl.ANY)
    f32o, b16o = jax.ShapeDtypeStruct((tp, D), F32), jax.ShapeDtypeStruct((tp, D), BF16)
    res = pl.pallas_call(
        body, name="mix_fwd", grid=(nt,),
        in_specs=[tile(), tile(), tile(3), tile(4), tile(), vec(2 * D), vec(D), vec(D), vec(D), anys, anys]
        + [_whole(a) for a in gat.arrays],
        out_specs=[tile()] * 4 + [anys] * ng,
        out_shape=[f32o, b16o, b16o, b16o] + gat.out_shape,
        scratch_shapes=[pltpu.VMEM((NDEV, 3, D // NDEV, D), BF16), pltpu.VMEM((NDEV, 4, PG // NDEV, PG), BF16),
                        pltpu.SemaphoreType.DMA((2,))] + gat.scratch,
        compiler_params=_params(("arbitrary",), 52),
    )(ac, m, z, z, h0, b_gate, ln_g, ln_b, pool_scale, g_mixw, g_pool, *gat.arrays)
    return res[:4], res[4:]


def _ffn_fwd(h1, tgt, g_ffn, g_final, w_gu, w_dn):
    tp = h1.shape[0]
    nt = tp // TM
    nx_last = tgt.shape[0] - (nt - 1) * TM

    def body(h_ref, t_ref, gf_ref, gl_ref, wgu_hbm, wdn_hbm,
             fg_ref, fu_ref, v_ref, f_ref, dh2_ref, acc_ref, wgu, wdn, v_sc, h2_sc, diff_sc, sems):
        i, j = pl.program_id(0), pl.program_id(1)
        _load_ffn(i, j, wgu_hbm, wgu, wdn_hbm, wdn, sems)

        @pl.when((i == 0) & (j == 0))
        def _():
            acc_ref[...] = jnp.zeros_like(acc_ref)

        @pl.when(j == 0)
        def _():
            h = h_ref[...]
            r = lax.rsqrt(jnp.mean(h * h, axis=-1, keepdims=True) + RMS_EPS)
            v = (h * r * gf_ref[...]).astype(BF16)
            v_sc[...] = v
            v_ref[...] = v
            h2_sc[...] = h

        v = v_sc[...]
        fg = _dot_nt(v, wgu[0, j])
        fu = _dot_nt(v, wgu[1, j])
        fg_ref[...] = fg
        fu_ref[...] = fu
        f = ((fg * _sig(fg)) * fu).astype(BF16)
        f_ref[...] = f
        h2_sc[...] += _dot(f, wdn[j])

        @pl.when(j == 1)
        def _():
            h2 = h2_sc[...]
            r = lax.rsqrt(jnp.mean(h2 * h2, axis=-1, keepdims=True) + RMS_EPS)
            n2 = h2 * r
            y = n2 * gl_ref[...]

            @pl.when(i < nt - 1)
            def _():
                diff_sc[...] = y - t_ref[...]

            @pl.when(i == nt - 1)
            def _():
                diff_sc[pl.ds(0, nx_last), :] = y[:nx_last] - t_ref[pl.ds(0, nx_last), :]
                diff_sc[pl.ds(nx_last, TM - nx_last), :] = jnp.zeros((TM - nx_last, D), F32)

            diff = diff_sc[...]
            dy = diff * (1.0 / D)
            acc_ref[0:1, :] += jnp.sum(diff * diff, axis=0, keepdims=True)
            acc_ref[1:2, :] += jnp.sum(dy * n2, axis=0, keepdims=True)
            dn = dy * gl_ref[...]
            dh2_ref[...] = r * (dn - n2 * jnp.mean(dn * n2, axis=-1, keepdims=True))

    def tile():
        return pl.BlockSpec((TM, D), lambda i, j: (i, 0))

    def chunk():
        return pl.BlockSpec((TM, FFC), lambda i, j: (i, j))

    def vec():
        return pl.BlockSpec((1, D), lambda i, j: (0, 0))

    anys = pl.BlockSpec(memory_space=pl.ANY)
    hid32, hid16 = jax.ShapeDtypeStruct((tp, DFF), F32), jax.ShapeDtypeStruct((tp, DFF), BF16)
    return pl.pallas_call(
        body, name="ffn_fwd", grid=(nt, 2),
        in_specs=[tile(), tile(), vec(), vec(), anys, anys],
        out_specs=[chunk(), chunk(), tile(), chunk(), tile(), pl.BlockSpec((8, D), lambda i, j: (0, 0))],
        out_shape=[hid32, hid32, jax.ShapeDtypeStruct((tp, D), BF16), hid16, jax.ShapeDtypeStruct((tp, D), F32),
                   jax.ShapeDtypeStruct((8, D), F32)],
        scratch_shapes=[pltpu.VMEM((2, 2, FFC, D), BF16), pltpu.VMEM((2, FFC, D), BF16),
                        pltpu.VMEM((TM, D), BF16), pltpu.VMEM((TM, D), F32), pltpu.VMEM((TM, D), F32),
                        pltpu.SemaphoreType.DMA((2 * NDEV + 2,))],
        compiler_params=_params(("arbitrary", "arbitrary"), 56),
    )(h1, tgt, g_ffn, g_final, w_gu, w_dn)


def _ffn_bwd(dh2, fg, fu, h1, g_ffn, w_gu, w_dn):
    tp = h1.shape[0]
    nt = tp // TM

    def body(dh2_ref, fg_ref, fu_ref, h_ref, gf_ref, wgu_hbm, wdn_hbm,
             dfg_ref, dfu_ref, dh1_ref, acc_ref, wgu, wdn, d_sc, dv_sc, sems):
        i, j = pl.program_id(0), pl.program_id(1)
        _load_ffn(i, j, wgu_hbm, wgu, wdn_hbm, wdn, sems)

        @pl.when((i == 0) & (j == 0))
        def _():
            acc_ref[...] = jnp.zeros_like(acc_ref)

        @pl.when(j == 0)
        def _():
            d_sc[...] = dh2_ref[...].astype(BF16)
            dv_sc[...] = jnp.zeros_like(dv_sc)

        df = _dot_nt(d_sc[...], wdn[j])
        fg = fg_ref[...]
        sg = _sig(fg)
        dfu = (df * (fg * sg)).astype(BF16)
        dfg = (df * fu_ref[...] * (sg * (1.0 + fg * (1.0 - sg)))).astype(BF16)
        dfg_ref[...] = dfg
        dfu_ref[...] = dfu
        dv_sc[...] += _dot(dfg, wgu[0, j]) + _dot(dfu, wgu[1, j])

        @pl.when(j == 1)
        def _():
            h = h_ref[...]
            r = lax.rsqrt(jnp.mean(h * h, axis=-1, keepdims=True) + RMS_EPS)
            n1 = h * r
            dv = dv_sc[...]
            acc_ref[0:1, :] += jnp.sum(dv * n1, axis=0, keepdims=True)
            dn = dv * gf_ref[...]
            dh1_ref[...] = dh2_ref[...] + r * (dn - n1 * jnp.mean(dn * n1, axis=-1, keepdims=True))

    def tile():
        return pl.BlockSpec((TM, D), lambda i, j: (i, 0))

    def chunk():
        return pl.BlockSpec((TM, FFC), lambda i, j: (i, j))

    anys = pl.BlockSpec(memory_space=pl.ANY)
    hid16 = jax.ShapeDtypeStruct((tp, DFF), BF16)
    return pl.pallas_call(
        body, name="ffn_bwd", grid=(nt, 2),
        in_specs=[tile(), chunk(), chunk(), tile(), pl.BlockSpec((1, D), lambda i, j: (0, 0)), anys, anys],
        out_specs=[chunk(), chunk(), tile(), pl.BlockSpec((8, D), lambda i, j: (0, 0))],
        out_shape=[hid16, hid16, jax.ShapeDtypeStruct((tp, D), F32), jax.ShapeDtypeStruct((8, D), F32)],
        scratch_shapes=[pltpu.VMEM((2, 2, FFC, D), BF16), pltpu.VMEM((2, FFC, D), BF16),
                        pltpu.VMEM((TM, D), BF16), pltpu.VMEM((TM, D), F32), pltpu.SemaphoreType.DMA((2 * NDEV + 2,))],
        compiler_params=_params(("arbitrary", "arbitrary"), 56),
    )(dh2, fg, fu, h1, g_ffn, w_gu, w_dn)


def _mix_bwd(dh1, z, s, q, ac, m, b_gate, ln_g, ln_b, pool_scale, g_mixw, g_pool, qs):
    tp = dh1.shape[0]
    nt = tp // TMS
    ex = _ChipExchange(qs)
    nq = ex.n

    def body(*refs):
        dh1_ref, zga, zgb, s_ref, q_ref, ac_ref, m_ref, bg_ref, lg_ref, lb_ref, ps_ref, wm_hbm, wp_hbm = refs[:13]
        dac_ref, dm_ref, dzg_ref, dyc_ref, dyp_ref, dm2_ref, acc_ref = refs[13 + nq:20 + nq]
        wm, wp, sems = refs[20 + 2 * nq:23 + 2 * nq]
        ex.bind(refs[13:13 + nq], refs[20 + nq:20 + 2 * nq], refs[23 + 2 * nq:])
        first = pl.program_id(0) == 0

        @pl.when(first)
        def _():
            ex.issue()
            acc_ref[...] = jnp.zeros_like(acc_ref)

        _load_once(first, [(wm_hbm, wm), (wp_hbm, wp)], sems)

        dmerged = _dot_nt(dh1_ref[...].astype(BF16), wm[:, 2].reshape(D, D))
        ga = _sig(zga[...] + bg_ref[:, :D])
        gb = _sig(zgb[...] + bg_ref[:, D:])
        dyc = dmerged * ga
        dyp = dmerged * gb
        dza = (dmerged * _dot(s_ref[...], wm[:, 0].reshape(D, D))) * (ga * (1.0 - ga))
        dzb = (dmerged * _dot(q_ref[...], wm[:, 1].reshape(D, D))) * (gb * (1.0 - gb))
        dzg_ref[:, :D] = dza.astype(BF16)
        dzg_ref[:, D:] = dzb.astype(BF16)
        acc_ref[0:1, :D] += jnp.sum(dza, axis=0, keepdims=True)
        acc_ref[0:1, D:] += jnp.sum(dzb, axis=0, keepdims=True)
        dyc_b = dyc.astype(BF16)
        dyp_b = dyp.astype(BF16)
        dyc_ref[...] = dyc_b
        dyp_ref[...] = dyp_b
        ds = _dot_nt(dyc_b, wm[:, 0].reshape(D, D))
        n, rl = _ln_stats(ac_ref[...])
        l = n * lg_ref[...] + lb_ref[...]
        sg = _sig(l)
        dl = ds * (sg * (1.0 + l * (1.0 - sg)))
        acc_ref[1:2, :D] += jnp.sum(dl * n, axis=0, keepdims=True)
        acc_ref[1:2, D:] += jnp.sum(dl, axis=0, keepdims=True)
        dn = dl * lg_ref[...]
        dac_ref[...] = rl * (dn - jnp.mean(dn, axis=-1, keepdims=True) - n * jnp.mean(dn * n, axis=-1, keepdims=True))
        dq = _dot_nt(dyp_b, wm[:, 1].reshape(D, D))
        mv = m_ref[...]
        acc_ref[2:3, :D] += jnp.sum(dq * _pool_mix(mv, wp), axis=0, keepdims=True)
        dm2 = (dq * ps_ref[...]).astype(BF16)
        dm2_ref[...] = dm2
        dm_ref[...] = jnp.concatenate(
            [_dot_nt(dm2[:, g * PG:(g + 1) * PG], wp[:, g].reshape(PG, PG)) for g in range(4)], axis=1)

        @pl.when(pl.program_id(0) == nt - 1)
        def _():
            ex.finish()

    def tile(col=0):
        return pl.BlockSpec((TMS, D), lambda i: (i, col))

    def vec(w):
        return pl.BlockSpec((1, w), lambda i: (0, 0))

    anys = pl.BlockSpec(memory_space=pl.ANY)
    f32o, b16o = jax.ShapeDtypeStruct((tp, D), F32), jax.ShapeDtypeStruct((tp, D), BF16)
    res = pl.pallas_call(
        body, name="mix_bwd", grid=(nt,),
        in_specs=[tile(), tile(3), tile(4), tile(), tile(), tile(), tile(), vec(2 * D), vec(D), vec(D), vec(D), anys, anys]
        + [anys] * nq,
        out_specs=[tile(), tile(), pl.BlockSpec((TMS, 2 * D), lambda i: (i, 0)), tile(), tile(), tile(),
                   pl.BlockSpec((8, 2 * D), lambda i: (0, 0))] + [anys] * nq,
        out_shape=[f32o, f32o, jax.ShapeDtypeStruct((tp, 2 * D), BF16), b16o, b16o, b16o,
                   jax.ShapeDtypeStruct((8, 2 * D), F32)] + ex.out_shape,
        scratch_shapes=[pltpu.VMEM((NDEV, 3, D // NDEV, D), BF16), pltpu.VMEM((NDEV, 4, PG // NDEV, PG), BF16),
                        pltpu.SemaphoreType.DMA((2,))] + ex.scratch,
        compiler_params=_params(("arbitrary",), 48),
    )(dh1, z, z, s, q, ac, m, b_gate, ln_g, ln_b, pool_scale, g_mixw, g_pool, *qs)
    return res[:7], res[7:]


def _seq_bwd(dac, dm, dzg, z, w_dw, seq, qs):
    tp = z.shape[0]
    nt = tp // TM
    ex = _ChipExchange(qs)
    nq = no = ex.n

    def body(*refs):
        dac_l, dac_c, dac_r, dm_l, dm_c, dm_r, av_l, av, av_r, ag_l, ag, ag_r, dzg_ref, w_ref = refs[:14]
        dz_ref, acc_ref = refs[14 + nq:16 + nq]
        a3, d3, m3, da3, dp3, w3, dw3, da_sc, dp_sc = refs[16 + nq + no:25 + nq + no]
        ex.bind(refs[14:14 + nq], refs[16 + nq:16 + nq + no], refs[25 + nq + no:])
        i = pl.program_id(0)
        sub = lax.broadcasted_iota(jnp.int32, (NCB, 128), 0)

        @pl.when(i == 0)
        def _():
            ex.issue()
            dw3[...] = jnp.zeros_like(dw3)
            _tm_fill(w3, 0, 4, lambda r, l: w_ref[pl.ds(r, 8), l])

        _tm_fill_ext(a3, (av_l, ag_l), (av, ag), (av_r, ag_r), lambda vg, r, l: vg[0][r, l] * _sig(vg[1][r, l]))
        _tm_fill_ext(d3, dac_l, dac_c, dac_r, lambda ref, r, l: ref[r, l])
        _tm_fill_ext(m3, dm_l, dm_c, dm_r, lambda ref, r, l: ref[r, l])

        def conv(g, c):
            dcur = [_tm_at(d3, 8 * g + t + HALO) for t in range(8)]
            accs = [None] * 8
            for k in range(CONV_K):
                wk = _tm_at(w3, k)
                prs = []
                for t in range(8):
                    term = wk * _tm_at(d3, 8 * g + t + CONV_K - k)
                    accs[t] = term if accs[t] is None else accs[t] + term
                    prs.append(dcur[t] * _tm_at(a3, 8 * g + t + k + 1))
                while len(prs) > 1:
                    prs = [prs[j] + prs[j + 1] for j in range(0, len(prs), 2)]
                dw3[_tm_rows(k), :] += prs[0]
            s = dcur[0]
            for t in range(1, 8):
                s = s + dcur[t]
            dw3[_tm_rows(CONV_K), :] += s
            for t in range(8):
                da3[_tm_rows(8 * g + t), :] = accs[t]
            return c

        lax.fori_loop(0, TM // 8, conv, 0)

        for b in _edge_rows(seq, tp):
            e = lax.rem(b - i * TM + HALO + tp, tp)

            @pl.when(e < TME)
            def _():
                m3[_tm_rows(e), :] = _tm_at(m3, e) * _edge_gain(b, seq, tp, sub)

        inv = _by_group(sub, [1.0 / w for w in POOL_WINDOWS])

        def pool(g, c):
            for t in range(8):
                e = 8 * g + t + HALO
                sums = _nested_windows(lambda o: _tm_at(m3, e + o), [w // 2 + 1 - w for w in POOL_WINDOWS])
                dp3[_tm_rows(8 * g + t), :] = _by_group(sub, sums) * inv
            return c

        lax.fori_loop(0, TM // 8, pool, 0)

        _tm_read(da3, TM // 8, lambda r, l, tile: da_sc.__setitem__((r, l), tile))
        _tm_read(dp3, TM // 8, lambda r, l, tile: dp_sc.__setitem__((r, l), tile))
        sg = _sig(ag[...])
        da = da_sc[...]
        dz_ref[:, 0:D] = (da * sg).astype(BF16)
        dz_ref[:, D:2 * D] = (da * av[...] * (sg * (1.0 - sg))).astype(BF16)
        dz_ref[:, 2 * D:3 * D] = (dp_sc[...] - dm_c[...]).astype(BF16)
        dz_ref[:, 3 * D:] = dzg_ref[...]

        @pl.when(i == nt - 1)
        def _():
            _tm_read(dw3, 4, lambda r, l, tile: acc_ref.__setitem__((r, l), tile))
            ex.finish()

    tmaj = pltpu.VMEM((TM * NCB, 128), F32)
    text = pltpu.VMEM((TME * NCB, 128), F32)
    taps = pltpu.VMEM((32 * NCB, 128), F32)
    anys = pl.BlockSpec(memory_space=pl.ANY)
    res = pl.pallas_call(
        body, name="seq_bwd", grid=(nt,),
        in_specs=_halo_specs(0, nt) + _halo_specs(0, nt) + _halo_specs(0, nt) + _halo_specs(1, nt)
        + [pl.BlockSpec((TM, 2 * D), lambda i: (i, 0)), pl.BlockSpec((32, D), lambda i: (0, 0))] + [anys] * nq,
        out_specs=[pl.BlockSpec((TM, DIN), lambda i: (i, 0)), pl.BlockSpec((32, D), lambda i: (0, 0))] + [anys] * no,
        out_shape=[jax.ShapeDtypeStruct((tp, DIN), BF16), jax.ShapeDtypeStruct((32, D), F32)] + ex.out_shape,
        scratch_shapes=[text, text, text, tmaj, tmaj, taps, taps, pltpu.VMEM((TM, D), F32), pltpu.VMEM((TM, D), F32)]
        + ex.scratch,
        compiler_params=_params(("arbitrary",), 48),
    )(dac, dac, dac, dm, dm, dm, z, z, z, z, z, z, dzg, w_dw, *qs)
    return res[:2], res[2:]


def _in_bwd(dz, h0, dh1, g_mix, w_g, seq, qs):
    tp = h0.shape[0]
    tm = _pick(tp, TM_IO)
    nt = tp // tm
    ex = _ChipExchange(qs)
    nq = no = ex.n

    def body(*refs):
        dz_ref, h_ref, dh1_ref, g_ref, w_hbm = refs[:5]
        gx_ref, gmeta_ref, acc_ref = refs[5 + nq:8 + nq]
        w_vm, sems = refs[8 + nq + no:10 + nq + no]
        ex.bind(refs[5:5 + nq], refs[8 + nq:8 + nq + no], refs[10 + nq + no:])
        i = pl.program_id(0)

        @pl.when(i == 0)
        def _():
            ex.issue()
            acc_ref[...] = jnp.zeros_like(acc_ref)

        _load_once(i == 0, _win_pairs(w_hbm, w_vm), sems)

        du = _dot_nt(dz_ref[:, :DIN // 2], w_vm[0]) + _dot_nt(dz_ref[:, DIN // 2:], w_vm[1])
        h = h_ref[...]
        r = lax.rsqrt(jnp.mean(h * h, axis=-1, keepdims=True) + RMS_EPS)
        n0 = h * r
        acc_ref[0:1, :] += jnp.sum(du * n0, axis=0, keepdims=True)
        dn = du * g_ref[...]
        gx_ref[...] = dh1_ref[...] + r * (dn - n0 * jnp.mean(dn * n0, axis=-1, keepdims=True))

        @pl.when(i == nt - 1)
        def _():
            gmeta_ref[...] = gx_ref[pl.ds(tm - N_META, N_META), :]
            ex.finish()

    tile = pl.BlockSpec((tm, D), lambda i: (i, 0))
    anys = pl.BlockSpec(memory_space=pl.ANY)
    res = pl.pallas_call(
        body, name="in_bwd", grid=(nt,),
        in_specs=[pl.BlockSpec((tm, DIN), lambda i: (i, 0)), tile, tile, pl.BlockSpec((1, D), lambda i: (0, 0)), anys]
        + [anys] * nq,
        out_specs=[tile, pl.BlockSpec((N_META, D), lambda i: (0, 0)), pl.BlockSpec((8, D), lambda i: (0, 0))] + [anys] * no,
        out_shape=[jax.ShapeDtypeStruct((seq, D), F32), jax.ShapeDtypeStruct((N_META, D), F32),
                   jax.ShapeDtypeStruct((8, D), F32)] + ex.out_shape,
        scratch_shapes=[pltpu.VMEM((2, D, DIN // 2), BF16), pltpu.SemaphoreType.DMA((NDEV,))] + ex.scratch,
        compiler_params=_params(("arbitrary",), 58),
    )(dz, h0, dh1, g_mix, w_g, *qs)
    return res[:3], res[3:]


def _wgrad_in(u, dz, qs):
    tp = u.shape[0]
    tm = _pick(tp, TM_WG)
    nt = tp // tm
    half = DIN // 2
    ex = _ChipExchange(qs)
    nq = ex.n

    def body(*refs):
        u_ref, dz_ref = refs[:2]
        o_ref, acc = refs[2 + nq], refs[3 + 2 * nq]
        ex.bind(refs[2:2 + nq], refs[3 + nq:3 + 2 * nq], refs[4 + 2 * nq:])
        h, t = pl.program_id(0), pl.program_id(1)

        @pl.when((h == 0) & (t == 0))
        def _():
            ex.issue()

        @pl.when(t == 0)
        def _():
            acc[...] = jnp.zeros_like(acc)

        acc[...] += _dot_tn(u_ref[...], dz_ref[...])

        @pl.when(t == nt - 1)
        def _():
            for d in range(4):
                o_ref[d] = acc[:, INB * d:INB * (d + 1)].astype(BF16)

        @pl.when((h == 1) & (t == nt - 1))
        def _():
            ex.finish()

    anys = pl.BlockSpec(memory_space=pl.ANY)
    res = pl.pallas_call(
        body, name="wgrad_in", grid=(2, nt),
        in_specs=[pl.BlockSpec((tm, D), lambda h, t: (t, 0)), pl.BlockSpec((tm, half), lambda h, t: (t, h))] + [anys] * nq,
        out_specs=[pl.BlockSpec((4, D, INB), lambda h, t: (h, 0, 0), pipeline_mode=pl.Buffered(1))] + [anys] * nq,
        out_shape=[jax.ShapeDtypeStruct((NDEV, D, INB), BF16)] + ex.out_shape,
        scratch_shapes=[pltpu.VMEM((D, half), F32)] + ex.scratch,
        compiler_params=_params(("arbitrary", "arbitrary"), 52),
    )(u, dz, *qs)
    return res[0], res[1:]


def _wgrad_mix(s, dyc, q, dyp, merged, dh1, m, dm2):
    tp = s.shape[0]
    tm = _pick(tp, TM_WM)
    nt = tp // tm
    rb = D // NDEV

    def body(s_ref, dyc_ref, q_ref, dyp_ref, mg_ref, dh1_ref, m_ref, dm2_ref, o_ref, op_ref, acc, accp):
        t = pl.program_id(0)

        @pl.when(t == 0)
        def _():
            acc[...] = jnp.zeros_like(acc)
            accp[...] = jnp.zeros_like(accp)

        acc[0] += _dot_tn(s_ref[...], dyc_ref[...])
        acc[1] += _dot_tn(q_ref[...], dyp_ref[...])
        acc[2] += _dot_tn(mg_ref[...], dh1_ref[...].astype(BF16))
        for g in range(4):
            accp[g] += _dot_tn(m_ref[:, g * PG:(g + 1) * PG], dm2_ref[:, g * PG:(g + 1) * PG])

        @pl.when(t == nt - 1)
        def _():
            for d in range(NDEV):
                for k in range(3):
                    o_ref[d, k] = acc[k, rb * d:rb * (d + 1), :].astype(BF16)
                for g in range(4):
                    op_ref[d, g] = accp[g, 32 * d:32 * (d + 1), :].astype(BF16)

    tile = pl.BlockSpec((tm, D), lambda t: (t, 0))
    return pl.pallas_call(
        body, name="wgrad_mix", grid=(nt,),
        in_specs=[tile] * 8,
        out_specs=[pl.BlockSpec((NDEV, 3, rb, D), lambda t: (0, 0, 0, 0), pipeline_mode=pl.Buffered(1)),
                   pl.BlockSpec((NDEV, 4, 32, PG), lambda t: (0, 0, 0, 0), pipeline_mode=pl.Buffered(1))],
        out_shape=[jax.ShapeDtypeStruct((NDEV, 3, rb, D), BF16), jax.ShapeDtypeStruct((NDEV, 4, 32, PG), BF16)],
        scratch_shapes=[pltpu.VMEM((3, D, D), F32), pltpu.VMEM((4, PG, PG), F32)],
        compiler_params=_params(("arbitrary",), 56),
    )(s, dyc, q, dyp, merged, dh1, m, dm2)


def _wgrad_gu(v, dfg, dfu):
    tp = v.shape[0]
    tm = _pick(tp, TM_WG)
    nt = tp // tm

    def body(v_ref, dg_ref, du_ref, o_ref, acc):
        k, t = pl.program_id(0), pl.program_id(2)

        @pl.when(t == 0)
        def _():
            acc[...] = jnp.zeros_like(acc)

        @pl.when(k == 0)
        def _():
            acc[...] += _dot_tn(dg_ref[...], v_ref[...])

        @pl.when(k == 1)
        def _():
            acc[...] += _dot_tn(du_ref[...], v_ref[...])

        @pl.when(t == nt - 1)
        def _():
            for d in range(4):
                o_ref[d] = acc[FFB * d:FFB * (d + 1), :].astype(BF16)

    return pl.pallas_call(
        body, name="wgrad_gu", grid=(2, 2, nt),
        in_specs=[pl.BlockSpec((tm, D), lambda k, h, t: (t, 0)),
                  pl.BlockSpec((tm, FFC), lambda k, h, t: (t * (1 - k), h * (1 - k))),
                  pl.BlockSpec((tm, FFC), lambda k, h, t: (t * k, h * k))],
        out_specs=pl.BlockSpec((4, None, FFB, D), lambda k, h, t: (h, k, 0, 0), pipeline_mode=pl.Buffered(1)),
        out_shape=jax.ShapeDtypeStruct((NDEV, 2, FFB, D), BF16),
        scratch_shapes=[pltpu.VMEM((FFC, D), F32)],
        compiler_params=_params(("arbitrary",) * 3, 48),
    )(v, dfg, dfu)


def _wgrad_down(f, dh2):
    tp = f.shape[0]
    tm = _pick(tp, TM_WG)
    nt = tp // tm

    def body(f_ref, d_ref, o_ref, acc):
        t = pl.program_id(1)

        @pl.when(t == 0)
        def _():
            acc[...] = jnp.zeros_like(acc)

        acc[...] += _dot_tn(f_ref[...], d_ref[...].astype(BF16))

        @pl.when(t == nt - 1)
        def _():
            for d in range(4):
                o_ref[d] = acc[FFB * d:FFB * (d + 1), :].astype(BF16)

    return pl.pallas_call(
        body, name="wgrad_down", grid=(2, nt),
        in_specs=[pl.BlockSpec((tm, FFC), lambda h, t: (t, h)), pl.BlockSpec((tm, D), lambda h, t: (t, 0))],
        out_specs=pl.BlockSpec((4, FFB, D), lambda h, t: (h, 0, 0), pipeline_mode=pl.Buffered(1)),
        out_shape=jax.ShapeDtypeStruct((NDEV, FFB, D), BF16),
        scratch_shapes=[pltpu.VMEM((FFC, D), F32)],
        compiler_params=_params(("arbitrary", "arbitrary"), 48),
    )(f, dh2)


def kernel(x, meta_tokens, g_mix, w_in, b_gate, w_dw, b_dw, ln_g, ln_b, w_conv_out, w_pool, pool_scale, w_pool_out, w_o, g_ffn, w_ffn_gate, w_ffn_up, w_ffn_down, g_final, loss_target, m_meta_tokens, m_g_mix, m_w_in, m_b_gate, m_w_dw, m_b_dw, m_ln_g, m_ln_b, m_w_conv_out, m_w_pool, m_pool_scale, m_w_pool_out, m_w_o, m_g_ffn, m_w_ffn_gate, m_w_ffn_up, m_w_ffn_down, m_g_final, v_meta_tokens, v_g_mix, v_w_in, v_b_gate, v_w_dw, v_b_dw, v_ln_g, v_ln_b, v_w_conv_out, v_w_pool, v_pool_scale, v_w_pool_out, v_w_o, v_g_ffn, v_w_ffn_gate, v_w_ffn_up, v_w_ffn_down, v_g_final):
    seq = x.shape[1]
    tp = -(-(seq + 2 * HALO) // TM) * TM
    tm_in = _pick(tp, TM_IO)
    nx_last = seq - (tp // tm_in - 1) * tm_in
    assert 0 < nx_last <= tm_in - 2 * HALO and nx_last % 8 == 0 and 0 < seq - (tp // TM - 1) * TM

    whole = (Ellipsis,)
    ag_small = _Gather(
        [((48, D // NDEV), [(meta_tokens, pl.ds(0, N_META), whole), (w_dw, pl.ds(N_META, CONV_K), 0)])], [F32])
    ag_mix = _Gather([((3, D // NDEV, D), [(w_conv_out, 0, 0), (w_pool_out, 1, 0), (w_o, 2, 0)]),
                      ((4, PG // NDEV, PG), [(w_pool, whole, 0)])], [BF16, BF16])
    def tr(a):
        return jnp.swapaxes(a, 1, 2)

    ag_gu = _Gather([((2, FFB, D), [(tr(w_ffn_gate), 0, 0), (tr(w_ffn_up), 1, 0)])], [BF16])
    ag_dn = _Gather([((FFB, D), [(w_ffn_down, whole, 0)])], [BF16])

    mx, my = lax.axis_index("x"), lax.axis_index("y")
    order = jnp.stack([2 * mx + my, 2 * mx + 1 - my, 2 * (1 - mx) + my, 2 * (1 - mx) + 1 - my]).astype(jnp.int32)
    (h0, z, u, g_in), (g_mixw, g_pool), g_small = _fwd_in(x[0], g_mix, w_in, order, tp, ag_mix, ag_small)
    wdw_full = g_small.transpose(1, 0, 2).reshape(48, D)[N_META:]
    (ac, m), (w_gu,) = _seq_fwd(z, wdw_full, b_dw, seq, ag_gu)
    (h1, s, merged, q), (g_down,) = _mix_fwd(ac, m, z, h0, b_gate, ln_g, ln_b, pool_scale, g_mixw, g_pool, ag_dn)
    w_dn = g_down.reshape(2, FFC, D)
    fg, fu, v, f, dh2, head_acc = _ffn_fwd(h1, loss_target[0], g_ffn, g_final.reshape(1, D), w_gu, w_dn)

    dfg, dfu, dh1, ffn_acc = _ffn_bwd(dh2, fg, fu, h1, g_ffn, w_gu, w_dn)
    own_f, sib_f, q_f = _rs_pair("rs_pair_ffn", [_wgrad_gu(v, dfg, dfu), _wgrad_down(f, dh2)])
    (dac, dm, dzg, dyc, dyp, dm2, mix_acc), rel_dn = _mix_bwd(
        dh1, z, s, q, ac, m, b_gate, ln_g, ln_b, pool_scale, g_mixw, g_pool, q_f[1:])
    p_mix = _wgrad_mix(s, dyc, q, dyp, merged, dh1, m, dm2)
    own_m, sib_m, q_m = _rs_pair("rs_pair_mix", list(p_mix))
    (dz, seq_acc), rel_gu = _seq_bwd(dac, dm, dzg, z, wdw_full, seq, q_f[:1])
    rel_f = [rel_gu[0], rel_dn[0]]
    p_in, rel_m = _wgrad_in(u, dz, q_m)
    own_i, sib_i, q_i = _rs_pair("rs_pair_in", [p_in])
    (grad_x, g_meta, in_acc), rel_i = _in_bwd(dz, h0, dh1, g_mix, g_in, seq, q_i)
    small_g = jnp.concatenate([g_meta, seq_acc[:CONV_K], jnp.zeros((1, D), F32)], axis=0)
    p_small = small_g.reshape(48, NDEV, D // NDEV).transpose(1, 0, 2).astype(BF16)
    rep_g = jnp.concatenate([
        in_acc[0:1], mix_acc[0:1, :D], mix_acc[0:1, D:], seq_acc[CONV_K:CONV_K + 1], mix_acc[1:2, :D], mix_acc[1:2, D:],
        mix_acc[2:3, :D], ffn_acc[0:1], head_acc[1:2], head_acc[0:1], jnp.zeros((REP_ROWS - 10, D), F32)], axis=0)
    own_s, sib_s, rel_s, rep_all = _reduce_scatter([p_small], rep_g)
    owns = [own_i[0], own_s[0], own_m[0], own_m[1], own_f[0], own_f[1]]
    sibs = [sib_i[0], sib_s[0], sib_m[0], sib_m[1], sib_f[0], sib_f[1]]
    rels = [rel_i[0], rel_s[0], rel_m[0], rel_m[1], rel_f[0], rel_f[1]]

    def lead(a):
        return a.reshape(1, *a.shape)

    def stack4(a, lead_dims):
        return a.reshape(*lead_dims, 1, 4 * 32, PG)

    (r_in,) = _adamw_multi("adamw_in", lead(owns[0]), sibs[0][:, None], rels[0][:, None], [w_in], [m_w_in], [v_w_in], 4)
    r_meta, r_dw = _adamw_meta_dw(owns[1], sibs[1], rels[1], (meta_tokens, m_meta_tokens, v_meta_tokens),
                                  (w_dw, m_w_dw, v_w_dw))
    r_conv, r_pout, r_o = _adamw_multi("adamw_mix", owns[2], sibs[2], rels[2], [w_conv_out, w_pool_out, w_o],
                                       [m_w_conv_out, m_w_pool_out, m_w_o], [v_w_conv_out, v_w_pool_out, v_w_o], 1)
    (r_pool,) = _adamw_multi("adamw_pool", stack4(owns[3], ()), stack4(sibs[3], (1,)), stack4(rels[3], (3,)),
                             [w_pool.reshape(1, 128, PG)], [m_w_pool.reshape(1, 128, PG)], [v_w_pool.reshape(1, 128, PG)], 1)
    r_pool = tuple(a.reshape(w_pool.shape) for a in r_pool)
    r_gate, r_up = _adamw_multi("adamw_gu", owns[4], sibs[4], rels[4], [tr(w_ffn_gate), tr(w_ffn_up)],
                                [tr(m_w_ffn_gate), tr(m_w_ffn_up)], [tr(v_w_ffn_gate), tr(v_w_ffn_up)], 2)
    r_gate, r_up = tuple(tr(a) for a in r_gate), tuple(tr(a) for a in r_up)
    (r_down,) = _adamw_multi("adamw_down", lead(owns[5]), sibs[5][:, None], rels[5][:, None],
                             [w_ffn_down], [m_w_ffn_down], [v_w_ffn_down], 2)
    row = (1, D)
    loss, reps = _adamw_rep(
        rep_all,
        [g_mix, b_gate, b_dw, ln_g, ln_b, pool_scale, g_ffn, g_final.reshape(row)],
        [m_g_mix, m_b_gate, m_b_dw, m_ln_g, m_ln_b, m_pool_scale, m_g_ffn, m_g_final.reshape(row)],
        [v_g_mix, v_b_gate, v_b_dw, v_ln_g, v_ln_b, v_pool_scale, v_g_ffn, v_g_final.reshape(row)])
    r_gmix, r_bg, r_bdw, r_lg, r_lb, r_ps, r_gffn, r_gfin = reps
    r_gfin = tuple(a.reshape(D) for a in r_gfin)

    in_order = [r_meta, r_gmix, r_in, r_bg, r_dw, r_bdw, r_lg, r_lb, r_conv, r_pool, r_ps, r_pout, r_o, r_gffn,
                r_gate, r_up, r_down, r_gfin]
    return (loss.reshape(()), grad_x[None], *[r[0] for r in in_order], *[r[1] for r in in_order],
            *[r[2] for r in in_order], *[r[3] for r in in_order])
```

```python
import math

import jax
import jax.numpy as jnp
from jax import lax
from jax.experimental import pallas as pl
from jax.experimental.pallas import tpu as pltpu

F32, BF16 = jnp.float32, jnp.bfloat16
MESH_ID = pl.DeviceIdType.MESH
NDEV = 8

D = 1024
N_META = 16
CONV_K = 31
HALO = 16
POOL_WINDOWS = (2, 4, 8, 16)
PG = 256
DIN = 5 * D
DFF = 2816
FFB = DFF // NDEV
FFC = DFF // 2
INB = DIN // NDEV
RMS_EPS = 1e-6
LN_EPS = 1e-5
ADAM_LR, ADAM_B1, ADAM_B2, ADAM_EPS, ADAM_WD, ADAM_STEP = 0.001, 0.9, 0.999, 1e-08, 0.01, 10

TM = 384
TMS = 384
TM_IO = 704
TM_WG = 1408
TM_WM = 704
MIB = 2 ** 20


def _sig(x):
    return 0.5 * jnp.tanh(0.5 * x) + 0.5


def _dot(a, b):
    return jnp.dot(a, b, preferred_element_type=F32)


def _dot_nt(a, b):
    return lax.dot_general(a, b, (((1,), (1,)), ((), ())), preferred_element_type=F32)


def _dot_tn(a, b):
    return lax.dot_general(a, b, (((0,), (0,)), ((), ())), preferred_element_type=F32)


def _pick(tp, pref):
    return pref if tp % pref == 0 else TM


def _params(sem, vmem_mib):
    return pltpu.CompilerParams(dimension_semantics=sem, vmem_limit_bytes=vmem_mib * MIB)


def _load_once(first, pairs, sems):
    @pl.when(first)
    def _():
        cps = [pltpu.make_async_copy(s, d, sems.at[k]) for k, (s, d) in enumerate(pairs)]
        for cp in cps:
            cp.start()
        for cp in cps:
            cp.wait()


def _place():
    x, y, c = lax.axis_index("x"), lax.axis_index("y"), lax.axis_index("c")
    return x, y, c


class _Gather:
    def __init__(self, groups, dtypes):
        self.groups, self.dtypes, self.n = groups, dtypes, len(groups)
        self.arrays = [a for _, parts in groups for a, _, _ in parts]
        self.out_shape = [jax.ShapeDtypeStruct((NDEV, *s), dt) for (s, _), dt in zip(groups, dtypes)]
        self.scratch = [pltpu.VMEM(s, dt) for (s, _), dt in zip(groups, dtypes)] + [
            pltpu.SemaphoreType.DMA((7 * self.n,)), pltpu.SemaphoreType.DMA((7 * self.n,)),
            pltpu.SemaphoreType.DMA((self.n,))]

    def bind(self, ins, outs, scratch):
        self.ins, self.outs, self.stages = ins, outs, scratch[:self.n]
        self.send_sems, self.recv_sems, self.local_sems = scratch[self.n:]
        return self

    def _copy(self, w, k, block, to, src=None):
        dst = self.outs[w].at[4 * block[0] + 2 * block[1] + block[2]]
        return pltpu.make_async_remote_copy(
            src_ref=dst if src is None else src, dst_ref=dst,
            send_sem=self.send_sems.at[7 * w + k], recv_sem=self.recv_sems.at[7 * w + k],
            device_id=to, device_id_type=MESH_ID)

    def _first(self):
        x, y, c = _place()
        me, sibling = (x, y, c), (x, y, 1 - c)
        chips = [(1 - x, y), (x, 1 - y), (1 - x, 1 - y)]
        mine, first = [], []
        for w in range(self.n):
            mine.append(pltpu.make_async_copy(self.stages[w], self.outs[w].at[4 * x + 2 * y + c], self.local_sems.at[w]))
            first.append(self._copy(w, 0, me, sibling, src=self.stages[w]))
            first += [self._copy(w, 1 + j, me, (*chip, c), src=self.stages[w]) for j, chip in enumerate(chips)]
        return mine, first

    def _passed(self):
        x, y, c = _place()
        chips = [(1 - x, y), (x, 1 - y), (1 - x, 1 - y)]
        return [self._copy(w, 4 + j, (*chip, c), (x, y, 1 - c)) for w in range(self.n) for j, chip in enumerate(chips)]

    def issue(self):
        a = 0
        for w in range(self.n):
            shape, parts = self.groups[w]
            if sum(arr.size for arr, _, _ in parts) < math.prod(shape):
                self.stages[w][...] = jnp.zeros(shape, self.dtypes[w])
            for _, dst, src in parts:
                self.stages[w][dst] = self.ins[a][src].astype(self.dtypes[w])
                a += 1
        mine, first = self._first()
        for cp in mine + first:
            cp.start()

    def forward(self):
        x, y, c = _place()
        chips = [(1 - x, y), (x, 1 - y), (1 - x, 1 - y)]
        passed = self._passed()
        for w in range(self.n):
            for j, chip in enumerate(chips):
                self._copy(w, 1 + j, (*chip, c), (x, y, c)).wait_recv()
                passed[3 * w + j].start()

    def finish(self):
        x, y, c = _place()
        chips = [(1 - x, y), (x, 1 - y), (1 - x, 1 - y)]
        for w in range(self.n):
            self._copy(w, 0, (x, y, 1 - c), (x, y, c)).wait_recv()
            for j, chip in enumerate(chips):
                self._copy(w, 4 + j, (*chip, 1 - c), (x, y, c)).wait_recv()
        mine, first = self._first()
        for cp in first + self._passed():
            cp.wait_send()
        for cp in mine:
            cp.wait()


class _ChipExchange:
    def __init__(self, qs):
        self.n = len(qs)
        self.out_shape = [jax.ShapeDtypeStruct(q.shape, q.dtype) for q in qs]
        self.scratch = [pltpu.SemaphoreType.DMA((3 * self.n,)), pltpu.SemaphoreType.DMA((3 * self.n,))]

    def bind(self, qs, rels, scratch):
        self.qs, self.rels = qs, rels
        self.send_sems, self.recv_sems = scratch
        return self

    def _copies(self):
        x, y, c = _place()
        chips = [(1 - x, y), (x, 1 - y), (1 - x, 1 - y)]
        return [pltpu.make_async_remote_copy(
            src_ref=self.qs[w].at[j], dst_ref=self.rels[w].at[j],
            send_sem=self.send_sems.at[3 * w + j], recv_sem=self.recv_sems.at[3 * w + j],
            device_id=(*chips[j], c), device_id_type=MESH_ID) for w in range(self.n) for j in range(3)]

    def issue(self):
        for cp in self._copies():
            cp.start()

    def finish(self):
        cps = self._copies()
        for cp in cps:
            cp.wait_recv()
        for cp in cps:
            cp.wait_send()


def _reduce_scatter(parts, small):
    n = len(parts)
    blks = [p.shape[1:] for p in parts]

    def body(*refs):
        ps, small_ref = refs[:n], refs[n]
        o = n + 1
        owns, sibs, rels, small_out = refs[o:o + n], refs[o + n:o + 2 * n], refs[o + 2 * n:o + 3 * n], refs[o + 3 * n]
        o += 3 * n + 1
        pa, pb, qst = refs[o:o + n], refs[o + n:o + 2 * n], refs[o + 2 * n:o + 3 * n]
        s1_send, s1_recv, s2_send, s2_recv, sm_send, sm_recv, lsem = refs[o + 3 * n:]
        x, y, c = _place()
        me = 4 * x + 2 * y + c
        sibling = (x, y, 1 - c)
        chips = [(1 - x, y), (x, 1 - y), (1 - x, 1 - y)]
        all_chips = [(x, y)] + chips

        own_cps = []
        for w in range(n):
            cp = pltpu.make_async_copy(ps[w].at[me], owns[w], lsem.at[w])
            cp.start()
            own_cps.append(cp)
        sm_own = pltpu.make_async_copy(small_ref, small_out.at[me], lsem.at[n])
        sm_own.start()

        def small_copy(r):
            peer = ((x + (r >> 2)) % 2, (y + ((r >> 1) & 1)) % 2, (c + (r & 1)) % 2)
            return pltpu.make_async_remote_copy(
                src_ref=small_ref, dst_ref=small_out.at[me], send_sem=sm_send.at[r - 1], recv_sem=sm_recv.at[r - 1],
                device_id=peer, device_id_type=MESH_ID)

        sm_cps = [small_copy(r) for r in range(1, NDEV)]
        for cp in sm_cps:
            cp.start()

        def pair_copy(w, rel):
            cx, cy = all_chips[rel]
            return pltpu.make_async_remote_copy(
                src_ref=ps[w].at[4 * cx + 2 * cy + (1 - c)], dst_ref=sibs[w].at[rel],
                send_sem=s1_send.at[4 * w + rel], recv_sem=s1_recv.at[4 * w + rel],
                device_id=sibling, device_id_type=MESH_ID)

        def chip_copy(w, j):
            return pltpu.make_async_remote_copy(
                src_ref=qst[w].at[j], dst_ref=rels[w].at[j],
                send_sem=s2_send.at[3 * w + j], recv_sem=s2_recv.at[3 * w + j],
                device_id=(*chips[j], c), device_id_type=MESH_ID)

        pair_cps = [pair_copy(w, rel) for w in range(n) for rel in (1, 2, 3, 0)]
        for cp in pair_cps:
            cp.start()
        chip_cps = []
        for w in range(n):
            for j, (cx, cy) in enumerate(chips):
                pair_copy(w, 1 + j).wait_recv()
                la = pltpu.make_async_copy(ps[w].at[4 * cx + 2 * cy + c], pa[w], lsem.at[n + 1])
                lb = pltpu.make_async_copy(sibs[w].at[1 + j], pb[w], lsem.at[n + 2])
                la.start()
                lb.start()
                la.wait()
                lb.wait()
                qst[w][j] = (pa[w][...].astype(F32) + pb[w][...].astype(F32)).astype(BF16)
                cp = chip_copy(w, j)
                cp.start()
                chip_cps.append(cp)
        for w in range(n):
            pair_copy(w, 0).wait_recv()
            for j in range(3):
                chip_copy(w, j).wait_recv()
        for cp in sm_cps:
            cp.wait_recv()
        for cp in pair_cps + chip_cps + sm_cps:
            cp.wait_send()
        for cp in own_cps:
            cp.wait()
        sm_own.wait()

    any_spec = pl.BlockSpec(memory_space=pl.ANY)
    outs = pl.pallas_call(
        body, name="rs_grads",
        out_shape=[jax.ShapeDtypeStruct(b, BF16) for b in blks]
        + [jax.ShapeDtypeStruct((4, *b), BF16) for b in blks]
        + [jax.ShapeDtypeStruct((3, *b), BF16) for b in blks]
        + [jax.ShapeDtypeStruct((NDEV, *small.shape), F32)],
        in_specs=[any_spec] * (n + 1),
        out_specs=[any_spec] * (3 * n + 1),
        scratch_shapes=[pltpu.VMEM(b, BF16) for b in blks] + [pltpu.VMEM(b, BF16) for b in blks]
        + [pltpu.VMEM((3, *b), BF16) for b in blks]
        + [pltpu.SemaphoreType.DMA((4 * n,)), pltpu.SemaphoreType.DMA((4 * n,)),
           pltpu.SemaphoreType.DMA((3 * n,)), pltpu.SemaphoreType.DMA((3 * n,)),
           pltpu.SemaphoreType.DMA((NDEV - 1,)), pltpu.SemaphoreType.DMA((NDEV - 1,)),
           pltpu.SemaphoreType.DMA((n + 3,))],
        compiler_params=pltpu.CompilerParams(vmem_limit_bytes=40 * MIB),
    )(*parts, small)
    return outs[:n], outs[n:2 * n], outs[2 * n:3 * n], outs[3 * n]


class _PairSum:
    def __init__(self, parts, keep_q=True):
        self.n = n = len(parts)
        self.keep_q = keep_q
        blks = [p.shape[1:] for p in parts]
        self.out_shape = [jax.ShapeDtypeStruct(b, BF16) for b in blks] + [jax.ShapeDtypeStruct((1, *b), BF16) for b in blks]
        if keep_q:
            self.out_shape += [jax.ShapeDtypeStruct((3, *b), BF16) for b in blks]
        self.scratch = [pltpu.VMEM((3, *b), BF16) for b in blks] * 3 + [
            pltpu.SemaphoreType.DMA((4 * n,)), pltpu.SemaphoreType.DMA((4 * n,)), pltpu.SemaphoreType.DMA((5 * n,))]

    def bind(self, ps, outs, scratch):
        n = self.n
        self.ps, self.owns, self.sibs, self.qs = ps, outs[:n], outs[n:2 * n], outs[2 * n:]
        self.pa, self.pb, self.qst = scratch[:n], scratch[n:2 * n], scratch[2 * n:3 * n]
        self.s_send, self.s_recv, self.lsem = scratch[3 * n:]
        return self

    def _local(self, with_q):
        n = self.n
        x, y, c = _place()
        chips = [(1 - x, y), (x, 1 - y), (1 - x, 1 - y)]
        own = [pltpu.make_async_copy(self.ps[w].at[4 * x + 2 * y + c], self.owns[w], self.lsem.at[w]) for w in range(n)]
        mine = [[pltpu.make_async_copy(self.ps[w].at[4 * cx + 2 * cy + c], self.pa[w].at[j], self.lsem.at[2 * n + 3 * w + j])
                 for j, (cx, cy) in enumerate(chips)] for w in range(n)]
        outq = [pltpu.make_async_copy(self.qst[w], self.qs[w], self.lsem.at[n + w]) for w in range(n)] if with_q else []
        return own, mine, outq

    def _pair(self, w, rel):
        x, y, c = _place()
        cx, cy = [(x, y), (1 - x, y), (x, 1 - y), (1 - x, 1 - y)][rel]
        return pltpu.make_async_remote_copy(
            src_ref=self.ps[w].at[4 * cx + 2 * cy + (1 - c)],
            dst_ref=self.sibs[w].at[0] if rel == 0 else self.pb[w].at[rel - 1],
            send_sem=self.s_send.at[4 * w + rel], recv_sem=self.s_recv.at[4 * w + rel],
            device_id=(x, y, 1 - c), device_id_type=MESH_ID)

    def issue(self):
        own, mine, _ = self._local(False)
        for cp in own + [cp for row in mine for cp in row]:
            cp.start()
        for w in range(self.n):
            for rel in (1, 2, 3, 0):
                self._pair(w, rel).start()

    def finish(self):
        own, mine, outq = self._local(self.keep_q)
        for w in range(self.n):
            for j in range(3):
                self._pair(w, 1 + j).wait_recv()
                mine[w][j].wait()
                self.qst[w][j] = (self.pa[w][j].astype(F32) + self.pb[w][j].astype(F32)).astype(BF16)
            if self.keep_q:
                outq[w].start()
        for w in range(self.n):
            self._pair(w, 0).wait_recv()
        for w in range(self.n):
            for rel in range(4):
                self._pair(w, rel).wait_send()
        for cp in own + outq:
            cp.wait()

    def results(self, outs):
        n = self.n
        return outs[:n], outs[n:2 * n], outs[2 * n:3 * n]


def _rs_pair(name, parts):
    ps = _PairSum(parts)
    n = ps.n

    def body(*refs):
        ps.bind(refs[:n], refs[n:4 * n], refs[4 * n:])
        ps.issue()
        ps.finish()

    any_spec = pl.BlockSpec(memory_space=pl.ANY)
    outs = pl.pallas_call(
        body, name=name, out_shape=ps.out_shape,
        in_specs=[any_spec] * n, out_specs=[any_spec] * (3 * n), scratch_shapes=ps.scratch,
        compiler_params=pltpu.CompilerParams(vmem_limit_bytes=48 * MIB),
    )(*parts)
    return ps.results(outs)


def _adamw_math(g, w, m, v):
    m = ADAM_B1 * m + (1.0 - ADAM_B1) * g
    v = ADAM_B2 * v + (1.0 - ADAM_B2) * (g * g)
    m_hat = m / (1.0 - ADAM_B1 ** ADAM_STEP)
    v_hat = v / (1.0 - ADAM_B2 ** ADAM_STEP)
    delta = -ADAM_LR * (m_hat / (jnp.sqrt(v_hat) + ADAM_EPS) + ADAM_WD * w)
    return delta, m, v


def _adamw_multi(name, own, sib, rel, ws, ms, vs, row_grid):
    k_n, r_n, c_n = own.shape
    rbk = r_n // row_grid

    def body(*refs):
        own_ref, sib_ref, r0_ref, r1_ref, r2_ref = refs[:5]
        w_refs, m_refs, v_refs = refs[5:5 + k_n], refs[5 + k_n:5 + 2 * k_n], refs[5 + 2 * k_n:5 + 3 * k_n]
        outs = refs[5 + 3 * k_n:]
        for k in range(k_n):
            g = own_ref[k].astype(F32) + sib_ref[k].astype(F32)
            g = g + r0_ref[k].astype(F32)
            g = g + r1_ref[k].astype(F32)
            g = g + r2_ref[k].astype(F32)
            delta, mm, vv = _adamw_math(g, w_refs[k][0], m_refs[k][0], v_refs[k][0])
            outs[4 * k][0] = g
            outs[4 * k + 1][0] = delta
            outs[4 * k + 2][0] = mm
            outs[4 * k + 3][0] = vv

    def lead(j):
        return pl.BlockSpec((None, k_n, rbk, c_n), lambda g: (j, 0, g, 0))

    wspec = pl.BlockSpec((1, rbk, c_n), lambda g: (0, g, 0))
    shp = jax.ShapeDtypeStruct((1, r_n, c_n), F32)
    res = pl.pallas_call(
        body, name=name, grid=(row_grid,),
        in_specs=[pl.BlockSpec((k_n, rbk, c_n), lambda g: (0, g, 0)), lead(0), lead(0), lead(1), lead(2)] + [wspec] * (3 * k_n),
        out_specs=[wspec] * (4 * k_n), out_shape=[shp] * (4 * k_n),
        compiler_params=_params(("arbitrary",), 40),
    )(own, sib, rel, rel, rel, *ws, *ms, *vs)
    return [tuple(res[4 * k:4 * k + 4]) for k in range(k_n)]


def _adamw_meta_dw(own, sib, rel, meta, dw):
    def body(own_ref, sib_ref, rel_ref, wm, mm, vm, wd, md, vd, *outs):
        def gsum(rows):
            g = own_ref[rows, :].astype(F32) + sib_ref[0, rows, :].astype(F32)
            for j in range(3):
                g = g + rel_ref[j, rows, :].astype(F32)
            return g

        g = gsum(pl.ds(0, N_META))
        delta, m2, v2 = _adamw_math(g, wm[...], mm[...], vm[...])
        for o, val in zip(outs[:4], (g, delta, m2, v2)):
            o[...] = val
        g = gsum(pl.ds(N_META, CONV_K))
        delta, m2, v2 = _adamw_math(g, wd[0], md[0], vd[0])
        for o, val in zip(outs[4:], (g, delta, m2, v2)):
            o[0] = val

    s_meta = jax.ShapeDtypeStruct(meta[0].shape, F32)
    s_dw = jax.ShapeDtypeStruct(dw[0].shape, F32)
    res = pl.pallas_call(body, name="adamw_meta_dw", out_shape=[s_meta] * 4 + [s_dw] * 4)(own, sib, rel, *meta, *dw)
    return tuple(res[:4]), tuple(res[4:])


REP_ROWS = 16


def _adamw_rep(gathered, ws, ms, vs):
    rows = [(0, 1), (1, 2), (3, 1), (4, 1), (5, 1), (6, 1), (7, 1), (8, 1)]

    def body(g_ref, *refs):
        w_refs, m_refs, v_refs = refs[:8], refs[8:16], refs[16:24]
        loss_ref, outs, acc = refs[24], refs[25:57], refs[57]
        g = g_ref[0]
        for d in range(1, NDEV):
            g = g + g_ref[d]
        acc[...] = g
        loss_ref[...] = (0.5 / D) * jnp.sum(acc[pl.ds(9, 1), :], axis=1, keepdims=True)
        for p, (r0, nr) in enumerate(rows):
            for h in range(nr):
                cols = pl.ds(h * D, D)
                gp = acc[pl.ds(r0 + h, 1), :]
                delta, mm, vv = _adamw_math(gp, w_refs[p][:, cols], m_refs[p][:, cols], v_refs[p][:, cols])
                for o, val in zip(outs[4 * p:4 * p + 4], (gp, delta, mm, vv)):
                    o[:, cols] = val

    shapes = [jax.ShapeDtypeStruct(w.shape, F32) for w in ws]
    res = pl.pallas_call(
        body, name="adamw_rep",
        out_shape=[jax.ShapeDtypeStruct((1, 1), F32)] + [s for s in shapes for _ in range(4)],
        scratch_shapes=[pltpu.VMEM((REP_ROWS, D), F32)],
    )(gathered, *ws, *ms, *vs)
    return res[0], [tuple(res[1 + 4 * p:5 + 4 * p]) for p in range(8)]


def _load_ffn(i, j, wgu_hbm, wgu, wdn_hbm, wdn, sems):
    half = NDEV // 2

    def copies(ch):
        pairs = [(wgu_hbm.at[half * ch + d, g], wgu.at[g, ch, pl.ds(FFB * d, FFB), :]) for g in range(2) for d in range(half)]
        pairs.append((wdn_hbm.at[ch], wdn.at[ch]))
        return [pltpu.make_async_copy(s, t, sems.at[(2 * half + 1) * ch + k]) for k, (s, t) in enumerate(pairs)]

    @pl.when((i == 0) & (j == 0))
    def _():
        for cp in copies(0) + copies(1):
            cp.start()

    for ch in range(2):
        @pl.when((i == 0) & (j == ch))
        def _():
            for cp in copies(ch):
                cp.wait()


def _win_pairs(w_hbm, w_vm):
    return [(w_hbm.at[q], w_vm.at[q // 2, :, pl.ds(2 * INB * (q % 2), 2 * INB)]) for q in range(4)]


def _whole(a):
    nd = a.ndim
    return pl.BlockSpec(a.shape, lambda *g: (0,) * nd)


CHIPW = 2 * INB
PHASE_CHIP = (1, 0, 2)
assert PHASE_CHIP[2] == 2


class _GatherIn:
    scratch = [pltpu.VMEM((D, INB), BF16), pltpu.SemaphoreType.DMA((7,)), pltpu.SemaphoreType.DMA((7,)),
               pltpu.SemaphoreType.DMA((1,))]

    def bind(self, w_ref, w_vm, scratch):
        self.w_ref, self.w_vm = w_ref, w_vm
        self.stage, self.send_sems, self.recv_sems, self.local_sem = scratch
        return self

    def _win(self, chip, core):
        return self.w_vm.at[2 * chip[0] + chip[1], core]

    def _copy(self, k, chip, core, to, src=None):
        dst = self._win(chip, core)
        return pltpu.make_async_remote_copy(
            src_ref=dst if src is None else src, dst_ref=dst, send_sem=self.send_sems.at[k],
            recv_sem=self.recv_sems.at[k], device_id=to, device_id_type=MESH_ID)

    def _mine(self, cs):
        x, y, _ = _place()
        return pltpu.make_async_copy(self.stage, self._win((x, y), cs), self.local_sem.at[0])

    def issue(self, cs):
        x, y, _ = _place()
        chips = [(1 - x, y), (x, 1 - y), (1 - x, 1 - y)]
        self.stage[...] = self.w_ref[0].astype(BF16)
        self._mine(cs).start()
        self._copy(0, (x, y), cs, (x, y, 1 - cs), src=self.stage).start()
        for j in PHASE_CHIP[:2]:
            self._copy(1 + j, (x, y), cs, (*chips[j], cs), src=self.stage).start()

    def wait_chip(self, phase, cs):
        x, y, _ = _place()
        chips = [(1 - x, y), (x, 1 - y), (1 - x, 1 - y)]
        if phase == 0:
            self._mine(cs).wait()
            self._copy(0, (x, y), 1 - cs, (x, y, cs)).wait_recv()
            return
        j = PHASE_CHIP[phase - 1]
        self._copy(1 + j, chips[j], cs, (x, y, cs)).wait_recv()
        self._copy(4 + j, chips[j], cs, (x, y, 1 - cs)).start()
        if phase == 1:
            self._copy(3, (x, y), cs, (*chips[2], cs), src=self.stage).start()
        self._copy(4 + j, chips[j], 1 - cs, (x, y, cs)).wait_recv()

    def finish(self, cs):
        x, y, _ = _place()
        for k in range(7):
            self._copy(k, (x, y), cs, (x, y, cs), src=self.stage).wait_send()


def _fwd_in(x2, g_mix, w_in, order, tp, ag, ags):
    tm = _pick(tp, TM_IO)
    nt = tp // tm
    nx_last = x2.shape[0] - (nt - 1) * tm
    na, ng, ns = len(ag.arrays), ag.n, len(ags.arrays)
    gin = _GatherIn()

    def body(order_ref, *refs):
        x_ref, g_ref, w_ref = refs[:3]
        o = 3 + na + ns
        h_ref, z_ref, u_ref, wout_ref = refs[o:o + 4]
        s = o + 4 + ng + 1
        w_vm, u_all, osem, sm_vm = refs[s:s + 4]
        gin.bind(w_ref, w_vm, refs[s + 4:s + 8])
        ag.bind(refs[3:3 + na], refs[o + 4:o + 4 + ng], refs[s + 8:s + 8 + len(ag.scratch)])
        ags.bind(refs[3 + na:3 + na + ns], refs[o + 4 + ng:o + 5 + ng], refs[s + 8 + len(ag.scratch):])
        ph, i = pl.program_id(0), pl.program_id(1)
        core = lax.axis_index("c")
        first = (ph == 0) & (i == 0)
        last = (ph == 3) & (i == nt - 1)
        @pl.when(first)
        def _():
            ags.issue()

        for cs in range(2):
            @pl.when(first & (core == cs))
            def _():
                gin.issue(cs)

        @pl.when((ph == 0) & (i == max(nt - 2, 0)))
        def _():
            ags.forward()

        for cs in range(2):
            for p in range(4):
                @pl.when((ph == p) & (i == 0) & (core == cs))
                def _():
                    gin.wait_chip(p, cs)

        @pl.when((ph == 2) & (i == 0))
        def _():
            ag.issue()

        out_copies = [pltpu.make_async_copy(w_vm.at[k, c], wout_ref.at[k, :, pl.ds(INB * c, INB)], osem.at[2 * k + c])
                      for k in range(4) for c in range(2)]

        @pl.when((ph == 3) & (i == 0))
        def _():
            for cp in out_copies:
                cp.start()

        @pl.when((ph == 0) & (i < nt - 1))
        def _():
            h_ref[...] = x_ref[...]

        @pl.when((ph == 0) & (i == nt - 1))
        def _():
            ags.finish()
            cp = pltpu.make_async_copy(ags.outs[0], sm_vm, osem.at[8])
            cp.start()
            h_ref[pl.ds(0, nx_last), :] = x_ref[pl.ds(0, nx_last), :]
            h_ref[pl.ds(nx_last, tm - nx_last - N_META), :] = jnp.zeros((tm - nx_last - N_META, D), F32)
            cp.wait()
            for d in range(NDEV):
                h_ref[pl.ds(tm - N_META, N_META), pl.ds(128 * d, 128)] = sm_vm[d, pl.ds(0, N_META), :]

        @pl.when(ph == 0)
        def _():
            xv = h_ref[...]
            r = lax.rsqrt(jnp.mean(xv * xv, axis=-1, keepdims=True) + RMS_EPS)
            u = (xv * r * g_ref[...]).astype(BF16)
            u_ref[...] = u
            u_all[i] = u

        for c in range(2):
            z_ref[:, INB * c:INB * (c + 1)] = _dot(u_all[i], w_vm[order_ref[ph], c])

        @pl.when(last)
        def _():
            ag.forward()
            ag.finish()
            for cp in out_copies:
                cp.wait()

        for cs in range(2):
            @pl.when(last & (core == cs))
            def _():
                gin.finish(cs)

    def rows(ph, i, order):
        return (jnp.where(ph == 0, i, nt - 1), 0)

    tile = pl.BlockSpec((tm, D), rows)
    anys = pl.BlockSpec(memory_space=pl.ANY)
    res = pl.pallas_call(
        body, name="fwd_in",
        grid_spec=pltpu.PrefetchScalarGridSpec(
            num_scalar_prefetch=1, grid=(4, nt),
            in_specs=[tile, pl.BlockSpec((1, D), lambda ph, i, order: (0, 0)), _whole(w_in)]
            + [_whole(a) for a in ag.arrays + ags.arrays],
            out_specs=[tile, pl.BlockSpec((tm, CHIPW), lambda ph, i, order: (i, order[ph])), tile, anys] + [anys] * (ng + 1),
            scratch_shapes=[pltpu.VMEM((4, 2, D, INB), BF16), pltpu.VMEM((nt, tm, D), BF16), pltpu.SemaphoreType.DMA((9,)),
                            pltpu.VMEM(ags.out_shape[0].shape, F32)] + gin.scratch + ag.scratch + ags.scratch),
        out_shape=[jax.ShapeDtypeStruct((tp, D), F32), jax.ShapeDtypeStruct((tp, DIN), F32),
                   jax.ShapeDtypeStruct((tp, D), BF16), jax.ShapeDtypeStruct((4, D, CHIPW), BF16)]
        + ag.out_shape + ags.out_shape,
        compiler_params=_params(("arbitrary", "arbitrary"), 58),
    )(order, x2, g_mix, w_in, *ag.arrays, *ags.arrays)
    return res[:4], res[4:4 + ng], res[4 + ng]


def _halo_specs(col, nt, width=D):
    r = TM // HALO
    nb = nt * r
    return [pl.BlockSpec((HALO, width), lambda i: ((i * r + nb - 1) % nb, col)),
            pl.BlockSpec((TM, width), lambda i: (i, col)),
            pl.BlockSpec((HALO, width), lambda i: (((i + 1) * r) % nb, col))]


NCB = D // 128
TME = TM + 2 * HALO


def _tm_fill(dst, time0, groups, tile_fn, unroll=1):
    def body(g, c):
        for j in range(NCB):
            dst[pl.ds((time0 + 8 * g) * NCB + j, 8, stride=NCB), :] = tile_fn(pl.multiple_of(8 * g, 8), pl.ds(128 * j, 128))
        return c

    lax.fori_loop(0, groups, body, 0, unroll=unroll)


def _tm_fill_ext(dst, left, cur, right, fn, unroll=1):
    _tm_fill(dst, 0, HALO // 8, lambda r, l: fn(left, pl.ds(r, 8), l), unroll)
    _tm_fill(dst, HALO, TM // 8, lambda r, l: fn(cur, pl.ds(r, 8), l), unroll)
    _tm_fill(dst, HALO + TM, HALO // 8, lambda r, l: fn(right, pl.ds(r, 8), l), unroll)


def _tm_read(src, groups, store_fn):
    def body(g, c):
        for j in range(NCB):
            store_fn(pl.ds(pl.multiple_of(8 * g, 8), 8), pl.ds(128 * j, 128), src[pl.ds(8 * g * NCB + j, 8, stride=NCB), :])
        return c

    lax.fori_loop(0, groups, body, 0, unroll=2)


def _tm_rows(t):
    return pl.ds(t * NCB if isinstance(t, int) else pl.multiple_of(t * NCB, NCB), NCB)


def _tm_at(ref, t):
    return ref[_tm_rows(t), :]


def _by_group(sub, vals):
    return jnp.where(sub < 2, vals[0], jnp.where(sub < 4, vals[1], jnp.where(sub < 6, vals[2], vals[3])))


def _pool_cnt(b, seq, tp, sub):
    b = jnp.where(b < 0, b + tp, b)
    b = jnp.where(b >= tp, b - tp, b)
    t = jnp.where(b < seq, b + N_META, b - (tp - N_META))
    cnts = []
    for win in POOL_WINDOWS:
        left = win // 2
        lo = jnp.maximum(t - left, 0)
        hi = jnp.minimum(t + win - left, seq + N_META)
        cnts.append(jnp.maximum(hi - lo, 1).astype(F32))
    return _by_group(sub, cnts)


def _edge_rows(seq, tp):
    reach = max(POOL_WINDOWS) // 2
    return [tp - N_META + t for t in range(reach)] + [seq - reach + 1 + t for t in range(reach - 1)]


def _edge_gain(b, seq, tp, sub):
    return _by_group(sub, [float(w) for w in POOL_WINDOWS]) / _pool_cnt(b, seq, tp, sub)


def _nested_windows(at, lo_offs):
    sums, s, have = [], None, set()
    for g, win in enumerate(POOL_WINDOWS):
        for o in range(lo_offs[g], lo_offs[g] + win):
            if o not in have:
                have.add(o)
                s = at(o) if s is None else s + at(o)
        sums.append(s)
    return sums


def _seq_fwd(z, w_dw, b_dw, seq, gat):
    tp = z.shape[0]
    nt = tp // TM
    na, ng = len(gat.arrays), gat.n

    def body(*refs):
        av_l, av, av_r, ag_l, ag, ag_r, p_l, p, p_r, w_ref, b_ref = refs[:11]
        ac_ref, m_ref = refs[11 + na:13 + na]
        a3, p3, o3, m3, w3, b3, m2d = refs[13 + na + ng:20 + na + ng]
        gat.bind(refs[11:11 + na], refs[13 + na:13 + na + ng], refs[20 + na + ng:])
        i = pl.program_id(0)
        sub = lax.broadcasted_iota(jnp.int32, (NCB, 128), 0)

        @pl.when(i == 0)
        def _():
            gat.issue()
            _tm_fill(w3, 0, 4, lambda r, l: w_ref[pl.ds(r, 8), l])
            for j in range(NCB):
                b3[pl.ds(j, 1), :] = b_ref[:, pl.ds(128 * j, 128)]

        @pl.when(i == max(nt - 2, 0))
        def _():
            gat.forward()

        _tm_fill_ext(a3, (av_l, ag_l), (av, ag), (av_r, ag_r), lambda vg, r, l: vg[0][r, l] * _sig(vg[1][r, l]), unroll=2)
        _tm_fill_ext(p3, p_l, p, p_r, lambda ref, r, l: ref[r, l])

        def conv(g, c):
            accs = [b3[...]] * 16
            for k in range(CONV_K):
                wk = _tm_at(w3, k)
                for t in range(16):
                    accs[t] = accs[t] + wk * _tm_at(a3, 16 * g + t + k + 1)
            for t in range(16):
                o3[_tm_rows(16 * g + t), :] = accs[t]
            return c

        lax.fori_loop(0, TM // 16, conv, 0)
        _tm_read(o3, TM // 8, lambda r, l, tile: ac_ref.__setitem__((r, l), tile))

        inv = _by_group(sub, [1.0 / w for w in POOL_WINDOWS])

        def pool(g, c):
            for t in range(8):
                e = 8 * g + t + HALO
                sums = _nested_windows(lambda o: _tm_at(p3, e + o), [-(w // 2) for w in POOL_WINDOWS])
                m3[_tm_rows(8 * g + t), :] = _by_group(sub, sums) * inv - _tm_at(p3, e)
            return c

        lax.fori_loop(0, TM // 8, pool, 0)
        for b in _edge_rows(seq, tp):
            r = b - i * TM

            @pl.when((r >= 0) & (r < TM))
            def _():
                pv = _tm_at(p3, r + HALO)
                m3[_tm_rows(r), :] = (_tm_at(m3, r) + pv) * _edge_gain(b, seq, tp, sub) - pv

        _tm_read(m3, TM // 8, lambda r, l, tile: m2d.__setitem__((r, l), tile))
        m_ref[...] = m2d[...].astype(BF16)

        @pl.when(i == nt - 1)
        def _():
            gat.finish()

    tmaj = pltpu.VMEM((TM * NCB, 128), F32)
    text = pltpu.VMEM((TME * NCB, 128), F32)
    res = pl.pallas_call(
        body, name="seq_fwd", grid=(nt,),
        in_specs=_halo_specs(0, nt) + _halo_specs(1, nt) + _halo_specs(2, nt)
        + [pl.BlockSpec((32, D), lambda i: (0, 0)), pl.BlockSpec((1, D), lambda i: (0, 0))] + [_whole(a) for a in gat.arrays],
        out_specs=[pl.BlockSpec((TM, D), lambda i: (i, 0))] * 2 + [pl.BlockSpec(memory_space=pl.ANY)] * ng,
        out_shape=[jax.ShapeDtypeStruct((tp, D), F32), jax.ShapeDtypeStruct((tp, D), BF16)] + gat.out_shape,
        scratch_shapes=[text, text, tmaj, tmaj, pltpu.VMEM((32 * NCB, 128), F32), pltpu.VMEM((NCB, 128), F32),
                        pltpu.VMEM((TM, D), F32)] + gat.scratch,
        compiler_params=_params(("arbitrary",), 52),
    )(z, z, z, z, z, z, z, z, z, w_dw, b_dw, *gat.arrays)
    return res[:2], res[2:]


def _ln_stats(ac):
    mu = jnp.mean(ac, axis=-1, keepdims=True)
    xc = ac - mu
    rl = lax.rsqrt(jnp.mean(xc * xc, axis=-1, keepdims=True) + LN_EPS)
    return xc * rl, rl


def _pool_mix(m, wp_ref):
    return jnp.concatenate(
        [_dot(m[:, g * PG:(g + 1) * PG], wp_ref[:, g].reshape(PG, PG)) for g in range(4)], axis=1)


def _mix_fwd(ac, m, z, h0, b_gate, ln_g, ln_b, pool_scale, g_mixw, g_pool, gat):
    tp = h0.shape[0]
    tms = TM
    nt = tp // tms
    na, ng = len(gat.arrays), gat.n

    def body(*refs):
        ac_ref, m_ref, zga, zgb, h_ref, bg_ref, lg_ref, lb_ref, ps_ref, wm_hbm, wp_hbm = refs[:11]
        h1_ref, s_ref, mg_ref, q_ref = refs[11 + na:15 + na]
        wm, wp, sems = refs[15 + na + ng:18 + na + ng]
        gat.bind(refs[11:11 + na], refs[15 + na:15 + na + ng], refs[18 + na + ng:])
        i = pl.program_id(0)

        @pl.when(i == 0)
        def _():
            gat.issue()

        @pl.when(i == max(nt - 4, 0))
        def _():
            gat.forward()

        @pl.when(i == nt - 1)
        def _():
            gat.finish()

        _load_once(i == 0, [(wm_hbm, wm), (wp_hbm, wp)], sems)
        n, _ = _ln_stats(ac_ref[...])
        l = n * lg_ref[...] + lb_ref[...]
        s = (l * _sig(l)).astype(BF16)
        s_ref[...] = s
        yc = _dot(s, wm[:, 0].reshape(D, D))
        q = (_pool_mix(m_ref[...], wp) * ps_ref[...]).astype(BF16)
        q_ref[...] = q
        yp = _dot(q, wm[:, 1].reshape(D, D))
        ga = _sig(zga[...] + bg_ref[:, :D])
        gb = _sig(zgb[...] + bg_ref[:, D:])
        merged = (ga * yc + gb * yp).astype(BF16)
        mg_ref[...] = merged
        h1_ref[...] = h_ref[...] + _dot(merged, wm[:, 2].reshape(D, D))

    def tile(col=0):
        return pl.BlockSpec((tms, D), lambda i: (i, col))

    def vec(w):
        return pl.BlockSpec((1, w), lambda i: (0, 0))

    anys = pl.BlockSpec(memory_space=pl.ANY)
    f32o, b16o = jax.ShapeDtypeStruct((tp, D), F32), jax.ShapeDtypeStruct((tp, D), BF16)
    res = pl.pallas_call(
        body, name="mix_fwd", grid=(nt,),
        in_specs=[tile(), tile(), tile(3), tile(4), tile(), vec(2 * D), vec(D), vec(D), vec(D), anys, anys]
        + [_whole(a) for a in gat.arrays],
        out_specs=[tile()] * 4 + [anys] * ng,
        out_shape=[f32o, b16o, b16o, b16o] + gat.out_shape,
        scratch_shapes=[pltpu.VMEM((NDEV, 3, D // NDEV, D), BF16), pltpu.VMEM((NDEV, 4, PG // NDEV, PG), BF16),
                        pltpu.SemaphoreType.DMA((2,))] + gat.scratch,
        compiler_params=_params(("arbitrary",), 52),
    )(ac, m, z, z, h0, b_gate, ln_g, ln_b, pool_scale, g_mixw, g_pool, *gat.arrays)
    return res[:4], res[4:]


def _ffn_fwd(h1, tgt, g_ffn, g_final, w_gu, w_dn):
    tp = h1.shape[0]
    nt = tp // TM
    nx_last = tgt.shape[0] - (nt - 1) * TM

    def body(h_ref, t_ref, gf_ref, gl_ref, wgu_hbm, wdn_hbm,
             fg_ref, fu_ref, v_ref, f_ref, dh2_ref, acc_ref, wgu, wdn, v_sc, h2_sc, diff_sc, sems):
        i, j = pl.program_id(0), pl.program_id(1)
        _load_ffn(i, j, wgu_hbm, wgu, wdn_hbm, wdn, sems)

        @pl.when((i == 0) & (j == 0))
        def _():
            acc_ref[...] = jnp.zeros_like(acc_ref)

        @pl.when(j == 0)
        def _():
            h = h_ref[...]
            r = lax.rsqrt(jnp.mean(h * h, axis=-1, keepdims=True) + RMS_EPS)
            v = (h * r * gf_ref[...]).astype(BF16)
            v_sc[...] = v
            v_ref[...] = v
            h2_sc[...] = h

        v = v_sc[...]
        fg = _dot_nt(v, wgu[0, j])
        fu = _dot_nt(v, wgu[1, j])
        fg_ref[...] = fg
        fu_ref[...] = fu
        f = ((fg * _sig(fg)) * fu).astype(BF16)
        f_ref[...] = f
        h2_sc[...] += _dot(f, wdn[j])

        @pl.when(j == 1)
        def _():
            h2 = h2_sc[...]
            r = lax.rsqrt(jnp.mean(h2 * h2, axis=-1, keepdims=True) + RMS_EPS)
            n2 = h2 * r
            y = n2 * gl_ref[...]

            @pl.when(i < nt - 1)
            def _():
                diff_sc[...] = y - t_ref[...]

            @pl.when(i == nt - 1)
            def _():
                diff_sc[pl.ds(0, nx_last), :] = y[:nx_last] - t_ref[pl.ds(0, nx_last), :]
                diff_sc[pl.ds(nx_last, TM - nx_last), :] = jnp.zeros((TM - nx_last, D), F32)

            diff = diff_sc[...]
            dy = diff * (1.0 / D)
            acc_ref[0:1, :] += jnp.sum(diff * diff, axis=0, keepdims=True)
            acc_ref[1:2, :] += jnp.sum(dy * n2, axis=0, keepdims=True)
            dn = dy * gl_ref[...]
            dh2_ref[...] = r * (dn - n2 * jnp.mean(dn * n2, axis=-1, keepdims=True))

    def tile():
        return pl.BlockSpec((TM, D), lambda i, j: (i, 0))

    def chunk():
        return pl.BlockSpec((TM, FFC), lambda i, j: (i, j))

    def vec():
        return pl.BlockSpec((1, D), lambda i, j: (0, 0))

    anys = pl.BlockSpec(memory_space=pl.ANY)
    hid32, hid16 = jax.ShapeDtypeStruct((tp, DFF), F32), jax.ShapeDtypeStruct((tp, DFF), BF16)
    return pl.pallas_call(
        body, name="ffn_fwd", grid=(nt, 2),
        in_specs=[tile(), tile(), vec(), vec(), anys, anys],
        out_specs=[chunk(), chunk(), tile(), chunk(), tile(), pl.BlockSpec((8, D), lambda i, j: (0, 0))],
        out_shape=[hid32, hid32, jax.ShapeDtypeStruct((tp, D), BF16), hid16, jax.ShapeDtypeStruct((tp, D), F32),
                   jax.ShapeDtypeStruct((8, D), F32)],
        scratch_shapes=[pltpu.VMEM((2, 2, FFC, D), BF16), pltpu.VMEM((2, FFC, D), BF16),
                        pltpu.VMEM((TM, D), BF16), pltpu.VMEM((TM, D), F32), pltpu.VMEM((TM, D), F32),
                        pltpu.SemaphoreType.DMA((2 * NDEV + 2,))],
        compiler_params=_params(("arbitrary", "arbitrary"), 56),
    )(h1, tgt, g_ffn, g_final, w_gu, w_dn)


def _ffn_bwd(dh2, fg, fu, h1, g_ffn, w_gu, w_dn):
    tp = h1.shape[0]
    nt = tp // TM

    def body(dh2_ref, fg_ref, fu_ref, h_ref, gf_ref, wgu_hbm, wdn_hbm,
             dfg_ref, dfu_ref, dh1_ref, acc_ref, wgu, wdn, d_sc, dv_sc, sems):
        i, j = pl.program_id(0), pl.program_id(1)
        _load_ffn(i, j, wgu_hbm, wgu, wdn_hbm, wdn, sems)

        @pl.when((i == 0) & (j == 0))
        def _():
            acc_ref[...] = jnp.zeros_like(acc_ref)

        @pl.when(j == 0)
        def _():
            d_sc[...] = dh2_ref[...].astype(BF16)
            dv_sc[...] = jnp.zeros_like(dv_sc)

        df = _dot_nt(d_sc[...], wdn[j])
        fg = fg_ref[...]
        sg = _sig(fg)
        dfu = (df * (fg * sg)).astype(BF16)
        dfg = (df * fu_ref[...] * (sg * (1.0 + fg * (1.0 - sg)))).astype(BF16)
        dfg_ref[...] = dfg
        dfu_ref[...] = dfu
        dv_sc[...] += _dot(dfg, wgu[0, j]) + _dot(dfu, wgu[1, j])

        @pl.when(j == 1)
        def _():
            h = h_ref[...]
            r = lax.rsqrt(jnp.mean(h * h, axis=-1, keepdims=True) + RMS_EPS)
            n1 = h * r
            dv = dv_sc[...]
            acc_ref[0:1, :] += jnp.sum(dv * n1, axis=0, keepdims=True)
            dn = dv * gf_ref[...]
            dh1_ref[...] = dh2_ref[...] + r * (dn - n1 * jnp.mean(dn * n1, axis=-1, keepdims=True))

    def tile():
        return pl.BlockSpec((TM, D), lambda i, j: (i, 0))

    def chunk():
        return pl.BlockSpec((TM, FFC), lambda i, j: (i, j))

    anys = pl.BlockSpec(memory_space=pl.ANY)
    hid16 = jax.ShapeDtypeStruct((tp, DFF), BF16)
    return pl.pallas_call(
        body, name="ffn_bwd", grid=(nt, 2),
        in_specs=[tile(), chunk(), chunk(), tile(), pl.BlockSpec((1, D), lambda i, j: (0, 0)), anys, anys],
        out_specs=[chunk(), chunk(), tile(), pl.BlockSpec((8, D), lambda i, j: (0, 0))],
        out_shape=[hid16, hid16, jax.ShapeDtypeStruct((tp, D), F32), jax.ShapeDtypeStruct((8, D), F32)],
        scratch_shapes=[pltpu.VMEM((2, 2, FFC, D), BF16), pltpu.VMEM((2, FFC, D), BF16),
                        pltpu.VMEM((TM, D), BF16), pltpu.VMEM((TM, D), F32), pltpu.SemaphoreType.DMA((2 * NDEV + 2,))],
        compiler_params=_params(("arbitrary", "arbitrary"), 56),
    )(dh2, fg, fu, h1, g_ffn, w_gu, w_dn)


def _mix_bwd(dh1, z, s, q, ac, m, b_gate, ln_g, ln_b, pool_scale, g_mixw, g_pool, qs):
    tp = dh1.shape[0]
    nt = tp // TMS
    ex = _ChipExchange(qs)
    nq = ex.n

    def body(*refs):
        dh1_ref, zga, zgb, s_ref, q_ref, ac_ref, m_ref, bg_ref, lg_ref, lb_ref, ps_ref, wm_hbm, wp_hbm = refs[:13]
        dac_ref, dm_ref, dzg_ref, dyc_ref, dyp_ref, dm2_ref, acc_ref = refs[13 + nq:20 + nq]
        wm, wp, sems = refs[20 + 2 * nq:23 + 2 * nq]
        ex.bind(refs[13:13 + nq], refs[20 + nq:20 + 2 * nq], refs[23 + 2 * nq:])
        first = pl.program_id(0) == 0

        @pl.when(first)
        def _():
            ex.issue()
            acc_ref[...] = jnp.zeros_like(acc_ref)

        _load_once(first, [(wm_hbm, wm), (wp_hbm, wp)], sems)

        dmerged = _dot_nt(dh1_ref[...].astype(BF16), wm[:, 2].reshape(D, D))
        ga = _sig(zga[...] + bg_ref[:, :D])
        gb = _sig(zgb[...] + bg_ref[:, D:])
        dyc = dmerged * ga
        dyp = dmerged * gb
        dza = (dmerged * _dot(s_ref[...], wm[:, 0].reshape(D, D))) * (ga * (1.0 - ga))
        dzb = (dmerged * _dot(q_ref[...], wm[:, 1].reshape(D, D))) * (gb * (1.0 - gb))
        dzg_ref[:, :D] = dza.astype(BF16)
        dzg_ref[:, D:] = dzb.astype(BF16)
        acc_ref[0:1, :D] += jnp.sum(dza, axis=0, keepdims=True)
        acc_ref[0:1, D:] += jnp.sum(dzb, axis=0, keepdims=True)
        dyc_b = dyc.astype(BF16)
        dyp_b = dyp.astype(BF16)
        dyc_ref[...] = dyc_b
        dyp_ref[...] = dyp_b
        ds = _dot_nt(dyc_b, wm[:, 0].reshape(D, D))
        n, rl = _ln_stats(ac_ref[...])
        l = n * lg_ref[...] + lb_ref[...]
        sg = _sig(l)
        dl = ds * (sg * (1.0 + l * (1.0 - sg)))
        acc_ref[1:2, :D] += jnp.sum(dl * n, axis=0, keepdims=True)
        acc_ref[1:2, D:] += jnp.sum(dl, axis=0, keepdims=True)
        dn = dl * lg_ref[...]
        dac_ref[...] = rl * (dn - jnp.mean(dn, axis=-1, keepdims=True) - n * jnp.mean(dn * n, axis=-1, keepdims=True))
        dq = _dot_nt(dyp_b, wm[:, 1].reshape(D, D))
        mv = m_ref[...]
        acc_ref[2:3, :D] += jnp.sum(dq * _pool_mix(mv, wp), axis=0, keepdims=True)
        dm2 = (dq * ps_ref[...]).astype(BF16)
        dm2_ref[...] = dm2
        dm_ref[...] = jnp.concatenate(
            [_dot_nt(dm2[:, g * PG:(g + 1) * PG], wp[:, g].reshape(PG, PG)) for g in range(4)], axis=1)

        @pl.when(pl.program_id(0) == nt - 1)
        def _():
            ex.finish()

    def tile(col=0):
        return pl.BlockSpec((TMS, D), lambda i: (i, col))

    def vec(w):
        return pl.BlockSpec((1, w), lambda i: (0, 0))

    anys = pl.BlockSpec(memory_space=pl.ANY)
    f32o, b16o = jax.ShapeDtypeStruct((tp, D), F32), jax.ShapeDtypeStruct((tp, D), BF16)
    res = pl.pallas_call(
        body, name="mix_bwd", grid=(nt,),
        in_specs=[tile(), tile(3), tile(4), tile(), tile(), tile(), tile(), vec(2 * D), vec(D), vec(D), vec(D), anys, anys]
        + [anys] * nq,
        out_specs=[tile(), tile(), pl.BlockSpec((TMS, 2 * D), lambda i: (i, 0)), tile(), tile(), tile(),
                   pl.BlockSpec((8, 2 * D), lambda i: (0, 0))] + [anys] * nq,
        out_shape=[f32o, f32o, jax.ShapeDtypeStruct((tp, 2 * D), BF16), b16o, b16o, b16o,
                   jax.ShapeDtypeStruct((8, 2 * D), F32)] + ex.out_shape,
        scratch_shapes=[pltpu.VMEM((NDEV, 3, D // NDEV, D), BF16), pltpu.VMEM((NDEV, 4, PG // NDEV, PG), BF16),
                        pltpu.SemaphoreType.DMA((2,))] + ex.scratch,
        compiler_params=_params(("arbitrary",), 48),
    )(dh1, z, z, s, q, ac, m, b_gate, ln_g, ln_b, pool_scale, g_mixw, g_pool, *qs)
    return res[:7], res[7:]


def _seq_bwd(dac, dm, dzg, z, w_dw, seq, qs):
    tp = z.shape[0]
    nt = tp // TM
    ex = _ChipExchange(qs)
    nq = no = ex.n

    def body(*refs):
        dac_l, dac_c, dac_r, dm_l, dm_c, dm_r, av_l, av, av_r, ag_l, ag, ag_r, dzg_ref, w_ref = refs[:14]
        dz_ref, acc_ref = refs[14 + nq:16 + nq]
        a3, d3, m3, da3, dp3, w3, dw3, da_sc, dp_sc = refs[16 + nq + no:25 + nq + no]
        ex.bind(refs[14:14 + nq], refs[16 + nq:16 + nq + no], refs[25 + nq + no:])
        i = pl.program_id(0)
        sub = lax.broadcasted_iota(jnp.int32, (NCB, 128), 0)

        @pl.when(i == 0)
        def _():
            ex.issue()
            dw3[...] = jnp.zeros_like(dw3)
            _tm_fill(w3, 0, 4, lambda r, l: w_ref[pl.ds(r, 8), l])

        _tm_fill_ext(a3, (av_l, ag_l), (av, ag), (av_r, ag_r), lambda vg, r, l: vg[0][r, l] * _sig(vg[1][r, l]), unroll=2)
        _tm_fill_ext(d3, dac_l, dac_c, dac_r, lambda ref, r, l: ref[r, l])
        _tm_fill_ext(m3, dm_l, dm_c, dm_r, lambda ref, r, l: ref[r, l])

        def conv(g, c):
            dcur = [_tm_at(d3, 8 * g + t + HALO) for t in range(8)]
            accs = [None] * 8
            for k in range(CONV_K):
                wk = _tm_at(w3, k)
                prs = []
                for t in range(8):
                    term = wk * _tm_at(d3, 8 * g + t + CONV_K - k)
                    accs[t] = term if accs[t] is None else accs[t] + term
                    prs.append(dcur[t] * _tm_at(a3, 8 * g + t + k + 1))
                while len(prs) > 1:
                    prs = [prs[j] + prs[j + 1] for j in range(0, len(prs), 2)]
                dw3[_tm_rows(k), :] += prs[0]
            s = dcur[0]
            for t in range(1, 8):
                s = s + dcur[t]
            dw3[_tm_rows(CONV_K), :] += s
            for t in range(8):
                da3[_tm_rows(8 * g + t), :] = accs[t]
            return c

        lax.fori_loop(0, TM // 8, conv, 0)

        for b in _edge_rows(seq, tp):
            e = lax.rem(b - i * TM + HALO + tp, tp)

            @pl.when(e < TME)
            def _():
                m3[_tm_rows(e), :] = _tm_at(m3, e) * _edge_gain(b, seq, tp, sub)

        inv = _by_group(sub, [1.0 / w for w in POOL_WINDOWS])

        def pool(g, c):
            for t in range(8):
                e = 8 * g + t + HALO
                sums = _nested_windows(lambda o: _tm_at(m3, e + o), [w // 2 + 1 - w for w in POOL_WINDOWS])
                dp3[_tm_rows(8 * g + t), :] = _by_group(sub, sums) * inv
            return c

        lax.fori_loop(0, TM // 8, pool, 0)

        _tm_read(da3, TM // 8, lambda r, l, tile: da_sc.__setitem__((r, l), tile))
        _tm_read(dp3, TM // 8, lambda r, l, tile: dp_sc.__setitem__((r, l), tile))
        sg = _sig(ag[...])
        da = da_sc[...]
        dz_ref[:, 0:D] = (da * sg).astype(BF16)
        dz_ref[:, D:2 * D] = (da * av[...] * (sg * (1.0 - sg))).astype(BF16)
        dz_ref[:, 2 * D:3 * D] = (dp_sc[...] - dm_c[...]).astype(BF16)
        dz_ref[:, 3 * D:] = dzg_ref[...]

        @pl.when(i == nt - 1)
        def _():
            _tm_read(dw3, 4, lambda r, l, tile: acc_ref.__setitem__((r, l), tile))
            ex.finish()

    tmaj = pltpu.VMEM((TM * NCB, 128), F32)
    text = pltpu.VMEM((TME * NCB, 128), F32)
    taps = pltpu.VMEM((32 * NCB, 128), F32)
    anys = pl.BlockSpec(memory_space=pl.ANY)
    res = pl.pallas_call(
        body, name="seq_bwd", grid=(nt,),
        in_specs=_halo_specs(0, nt) + _halo_specs(0, nt) + _halo_specs(0, nt) + _halo_specs(1, nt)
        + [pl.BlockSpec((TM, 2 * D), lambda i: (i, 0)), pl.BlockSpec((32, D), lambda i: (0, 0))] + [anys] * nq,
        out_specs=[pl.BlockSpec((TM, DIN), lambda i: (i, 0)), pl.BlockSpec((32, D), lambda i: (0, 0))] + [anys] * no,
        out_shape=[jax.ShapeDtypeStruct((tp, DIN), BF16), jax.ShapeDtypeStruct((32, D), F32)] + ex.out_shape,
        scratch_shapes=[text, text, text, tmaj, tmaj, taps, taps, pltpu.VMEM((TM, D), F32), pltpu.VMEM((TM, D), F32)]
        + ex.scratch,
        compiler_params=_params(("arbitrary",), 48),
    )(dac, dac, dac, dm, dm, dm, z, z, z, z, z, z, dzg, w_dw, *qs)
    return res[:2], res[2:]


def _in_bwd(dz, h0, dh1, g_mix, w_g, seq, qs):
    tp = h0.shape[0]
    tm = _pick(tp, TM_IO)
    nt = tp // tm
    ex = _ChipExchange(qs)
    nq = no = ex.n

    def body(*refs):
        dz_ref, h_ref, dh1_ref, g_ref, w_hbm = refs[:5]
        gx_ref, gmeta_ref, acc_ref = refs[5 + nq:8 + nq]
        w_vm, sems = refs[8 + nq + no:10 + nq + no]
        ex.bind(refs[5:5 + nq], refs[8 + nq:8 + nq + no], refs[10 + nq + no:])
        i = pl.program_id(0)

        @pl.when(i == 0)
        def _():
            ex.issue()
            acc_ref[...] = jnp.zeros_like(acc_ref)

        _load_once(i == 0, _win_pairs(w_hbm, w_vm), sems)

        du = _dot_nt(dz_ref[:, :DIN // 2], w_vm[0]) + _dot_nt(dz_ref[:, DIN // 2:], w_vm[1])
        h = h_ref[...]
        r = lax.rsqrt(jnp.mean(h * h, axis=-1, keepdims=True) + RMS_EPS)
        n0 = h * r
        acc_ref[0:1, :] += jnp.sum(du * n0, axis=0, keepdims=True)
        dn = du * g_ref[...]
        gx_ref[...] = dh1_ref[...] + r * (dn - n0 * jnp.mean(dn * n0, axis=-1, keepdims=True))

        @pl.when(i == nt - 1)
        def _():
            gmeta_ref[...] = gx_ref[pl.ds(tm - N_META, N_META), :]
            ex.finish()

    tile = pl.BlockSpec((tm, D), lambda i: (i, 0))
    anys = pl.BlockSpec(memory_space=pl.ANY)
    res = pl.pallas_call(
        body, name="in_bwd", grid=(nt,),
        in_specs=[pl.BlockSpec((tm, DIN), lambda i: (i, 0)), tile, tile, pl.BlockSpec((1, D), lambda i: (0, 0)), anys]
        + [anys] * nq,
        out_specs=[tile, pl.BlockSpec((N_META, D), lambda i: (0, 0)), pl.BlockSpec((8, D), lambda i: (0, 0))] + [anys] * no,
        out_shape=[jax.ShapeDtypeStruct((seq, D), F32), jax.ShapeDtypeStruct((N_META, D), F32),
                   jax.ShapeDtypeStruct((8, D), F32)] + ex.out_shape,
        scratch_shapes=[pltpu.VMEM((2, D, DIN // 2), BF16), pltpu.SemaphoreType.DMA((NDEV,))] + ex.scratch,
        compiler_params=_params(("arbitrary",), 58),
    )(dz, h0, dh1, g_mix, w_g, *qs)
    return res[:3], res[3:]


def _wgrad_in(u, dz, qs):
    tp = u.shape[0]
    tm = _pick(tp, TM_WG)
    nt = tp // tm
    half = DIN // 2
    ex = _ChipExchange(qs)
    nq = ex.n

    def body(*refs):
        u_ref, dz_ref = refs[:2]
        o_ref, acc = refs[2 + nq], refs[3 + 2 * nq]
        ex.bind(refs[2:2 + nq], refs[3 + nq:3 + 2 * nq], refs[4 + 2 * nq:])
        h, t = pl.program_id(0), pl.program_id(1)

        @pl.when((h == 0) & (t == 0))
        def _():
            ex.issue()

        @pl.when(t == 0)
        def _():
            acc[...] = jnp.zeros_like(acc)

        acc[...] += _dot_tn(u_ref[...], dz_ref[...])

        @pl.when(t == nt - 1)
        def _():
            for d in range(4):
                o_ref[d] = acc[:, INB * d:INB * (d + 1)].astype(BF16)

        @pl.when((h == 1) & (t == nt - 1))
        def _():
            ex.finish()

    anys = pl.BlockSpec(memory_space=pl.ANY)
    res = pl.pallas_call(
        body, name="wgrad_in", grid=(2, nt),
        in_specs=[pl.BlockSpec((tm, D), lambda h, t: (t, 0)), pl.BlockSpec((tm, half), lambda h, t: (t, h))] + [anys] * nq,
        out_specs=[pl.BlockSpec((4, D, INB), lambda h, t: (h, 0, 0), pipeline_mode=pl.Buffered(1))] + [anys] * nq,
        out_shape=[jax.ShapeDtypeStruct((NDEV, D, INB), BF16)] + ex.out_shape,
        scratch_shapes=[pltpu.VMEM((D, half), F32)] + ex.scratch,
        compiler_params=_params(("arbitrary", "arbitrary"), 52),
    )(u, dz, *qs)
    return res[0], res[1:]


def _wgrad_mix(s, dyc, q, dyp, merged, dh1, m, dm2):
    tp = s.shape[0]
    tm = _pick(tp, TM_WM)
    nt = tp // tm
    rb = D // NDEV

    def body(s_ref, dyc_ref, q_ref, dyp_ref, mg_ref, dh1_ref, m_ref, dm2_ref, o_ref, op_ref, acc, accp):
        t = pl.program_id(0)

        @pl.when(t == 0)
        def _():
            acc[...] = jnp.zeros_like(acc)
            accp[...] = jnp.zeros_like(accp)

        acc[0] += _dot_tn(s_ref[...], dyc_ref[...])
        acc[1] += _dot_tn(q_ref[...], dyp_ref[...])
        acc[2] += _dot_tn(mg_ref[...], dh1_ref[...].astype(BF16))
        for g in range(4):
            accp[g] += _dot_tn(m_ref[:, g * PG:(g + 1) * PG], dm2_ref[:, g * PG:(g + 1) * PG])

        @pl.when(t == nt - 1)
        def _():
            for d in range(NDEV):
                for k in range(3):
                    o_ref[d, k] = acc[k, rb * d:rb * (d + 1), :].astype(BF16)
                for g in range(4):
                    op_ref[d, g] = accp[g, 32 * d:32 * (d + 1), :].astype(BF16)

    tile = pl.BlockSpec((tm, D), lambda t: (t, 0))
    return pl.pallas_call(
        body, name="wgrad_mix", grid=(nt,),
        in_specs=[tile] * 8,
        out_specs=[pl.BlockSpec((NDEV, 3, rb, D), lambda t: (0, 0, 0, 0), pipeline_mode=pl.Buffered(1)),
                   pl.BlockSpec((NDEV, 4, 32, PG), lambda t: (0, 0, 0, 0), pipeline_mode=pl.Buffered(1))],
        out_shape=[jax.ShapeDtypeStruct((NDEV, 3, rb, D), BF16), jax.ShapeDtypeStruct((NDEV, 4, 32, PG), BF16)],
        scratch_shapes=[pltpu.VMEM((3, D, D), F32), pltpu.VMEM((4, PG, PG), F32)],
        compiler_params=_params(("arbitrary",), 56),
    )(s, dyc, q, dyp, merged, dh1, m, dm2)


def _wgrad_gu(v, dfg, dfu):
    tp = v.shape[0]
    tm = _pick(tp, TM_WG)
    nt = tp // tm

    def body(v_ref, dg_ref, du_ref, o_ref, acc):
        k, t = pl.program_id(0), pl.program_id(2)

        @pl.when(t == 0)
        def _():
            acc[...] = jnp.zeros_like(acc)

        @pl.when(k == 0)
        def _():
            acc[...] += _dot_tn(dg_ref[...], v_ref[...])

        @pl.when(k == 1)
        def _():
            acc[...] += _dot_tn(du_ref[...], v_ref[...])

        @pl.when(t == nt - 1)
        def _():
            for d in range(4):
                o_ref[d] = acc[FFB * d:FFB * (d + 1), :].astype(BF16)

    return pl.pallas_call(
        body, name="wgrad_gu", grid=(2, 2, nt),
        in_specs=[pl.BlockSpec((tm, D), lambda k, h, t: (t, 0)),
                  pl.BlockSpec((tm, FFC), lambda k, h, t: (t * (1 - k), h * (1 - k))),
                  pl.BlockSpec((tm, FFC), lambda k, h, t: (t * k, h * k))],
        out_specs=pl.BlockSpec((4, None, FFB, D), lambda k, h, t: (h, k, 0, 0), pipeline_mode=pl.Buffered(1)),
        out_shape=jax.ShapeDtypeStruct((NDEV, 2, FFB, D), BF16),
        scratch_shapes=[pltpu.VMEM((FFC, D), F32)],
        compiler_params=_params(("arbitrary",) * 3, 48),
    )(v, dfg, dfu)


def _wgrad_down(f, dh2):
    tp = f.shape[0]
    tm = _pick(tp, TM_WG)
    nt = tp // tm

    def body(f_ref, d_ref, o_ref, acc):
        t = pl.program_id(1)

        @pl.when(t == 0)
        def _():
            acc[...] = jnp.zeros_like(acc)

        acc[...] += _dot_tn(f_ref[...], d_ref[...].astype(BF16))

        @pl.when(t == nt - 1)
        def _():
            for d in range(4):
                o_ref[d] = acc[FFB * d:FFB * (d + 1), :].astype(BF16)

    return pl.pallas_call(
        body, name="wgrad_down", grid=(2, nt),
        in_specs=[pl.BlockSpec((tm, FFC), lambda h, t: (t, h)), pl.BlockSpec((tm, D), lambda h, t: (t, 0))],
        out_specs=pl.BlockSpec((4, FFB, D), lambda h, t: (h, 0, 0), pipeline_mode=pl.Buffered(1)),
        out_shape=jax.ShapeDtypeStruct((NDEV, FFB, D), BF16),
        scratch_shapes=[pltpu.VMEM((FFC, D), F32)],
        compiler_params=_params(("arbitrary", "arbitrary"), 48),
    )(f, dh2)


def kernel(x, meta_tokens, g_mix, w_in, b_gate, w_dw, b_dw, ln_g, ln_b, w_conv_out, w_pool, pool_scale, w_pool_out, w_o, g_ffn, w_ffn_gate, w_ffn_up, w_ffn_down, g_final, loss_target, m_meta_tokens, m_g_mix, m_w_in, m_b_gate, m_w_dw, m_b_dw, m_ln_g, m_ln_b, m_w_conv_out, m_w_pool, m_pool_scale, m_w_pool_out, m_w_o, m_g_ffn, m_w_ffn_gate, m_w_ffn_up, m_w_ffn_down, m_g_final, v_meta_tokens, v_g_mix, v_w_in, v_b_gate, v_w_dw, v_b_dw, v_ln_g, v_ln_b, v_w_conv_out, v_w_pool, v_pool_scale, v_w_pool_out, v_w_o, v_g_ffn, v_w_ffn_gate, v_w_ffn_up, v_w_ffn_down, v_g_final):
    seq = x.shape[1]
    tp = -(-(seq + 2 * HALO) // TM) * TM
    tm_in = _pick(tp, TM_IO)
    nx_last = seq - (tp // tm_in - 1) * tm_in
    assert 0 < nx_last <= tm_in - 2 * HALO and nx_last % 8 == 0 and 0 < seq - (tp // TM - 1) * TM

    whole = (Ellipsis,)
    ag_small = _Gather(
        [((48, D // NDEV), [(meta_tokens, pl.ds(0, N_META), whole), (w_dw, pl.ds(N_META, CONV_K), 0)])], [F32])
    ag_mix = _Gather([((3, D // NDEV, D), [(w_conv_out, 0, 0), (w_pool_out, 1, 0), (w_o, 2, 0)]),
                      ((4, PG // NDEV, PG), [(w_pool, whole, 0)])], [BF16, BF16])
    def tr(a):
        return jnp.swapaxes(a, 1, 2)

    ag_gu = _Gather([((2, FFB, D), [(tr(w_ffn_gate), 0, 0), (tr(w_ffn_up), 1, 0)])], [BF16])
    ag_dn = _Gather([((FFB, D), [(w_ffn_down, whole, 0)])], [BF16])

    mx, my = lax.axis_index("x"), lax.axis_index("y")
    order = jnp.stack([2 * mx + my, 2 * mx + 1 - my, 2 * (1 - mx) + my, 2 * (1 - mx) + 1 - my]).astype(jnp.int32)
    (h0, z, u, g_in), (g_mixw, g_pool), g_small = _fwd_in(x[0], g_mix, w_in, order, tp, ag_mix, ag_small)
    wdw_full = g_small.transpose(1, 0, 2).reshape(48, D)[N_META:]
    (ac, m), (w_gu,) = _seq_fwd(z, wdw_full, b_dw, seq, ag_gu)
    (h1, s, merged, q), (g_down,) = _mix_fwd(ac, m, z, h0, b_gate, ln_g, ln_b, pool_scale, g_mixw, g_pool, ag_dn)
    w_dn = g_down.reshape(2, FFC, D)
    fg, fu, v, f, dh2, head_acc = _ffn_fwd(h1, loss_target[0], g_ffn, g_final.reshape(1, D), w_gu, w_dn)

    dfg, dfu, dh1, ffn_acc = _ffn_bwd(dh2, fg, fu, h1, g_ffn, w_gu, w_dn)
    own_f, sib_f, q_f = _rs_pair("rs_pair_ffn", [_wgrad_gu(v, dfg, dfu), _wgrad_down(f, dh2)])
    (dac, dm, dzg, dyc, dyp, dm2, mix_acc), rel_dn = _mix_bwd(
        dh1, z, s, q, ac, m, b_gate, ln_g, ln_b, pool_scale, g_mixw, g_pool, q_f[1:])
    p_mix = _wgrad_mix(s, dyc, q, dyp, merged, dh1, m, dm2)
    own_m, sib_m, q_m = _rs_pair("rs_pair_mix", list(p_mix))
    (dz, seq_acc), rel_gu = _seq_bwd(dac, dm, dzg, z, wdw_full, seq, q_f[:1])
    rel_f = [rel_gu[0], rel_dn[0]]
    p_in, rel_m = _wgrad_in(u, dz, q_m)
    own_i, sib_i, q_i = _rs_pair("rs_pair_in", [p_in])
    (grad_x, g_meta, in_acc), rel_i = _in_bwd(dz, h0, dh1, g_mix, g_in, seq, q_i)
    small_g = jnp.concatenate([g_meta, seq_acc[:CONV_K], jnp.zeros((1, D), F32)], axis=0)
    p_small = small_g.reshape(48, NDEV, D // NDEV).transpose(1, 0, 2).astype(BF16)
    rep_g = jnp.concatenate([
        in_acc[0:1], mix_acc[0:1, :D], mix_acc[0:1, D:], seq_acc[CONV_K:CONV_K + 1], mix_acc[1:2, :D], mix_acc[1:2, D:],
        mix_acc[2:3, :D], ffn_acc[0:1], head_acc[1:2], head_acc[0:1], jnp.zeros((REP_ROWS - 10, D), F32)], axis=0)
    own_s, sib_s, rel_s, rep_all = _reduce_scatter([p_small], rep_g)
    owns = [own_i[0], own_s[0], own_m[0], own_m[1], own_f[0], own_f[1]]
    sibs = [sib_i[0], sib_s[0], sib_m[0], sib_m[1], sib_f[0], sib_f[1]]
    rels = [rel_i[0], rel_s[0], rel_m[0], rel_m[1], rel_f[0], rel_f[1]]

    def lead(a):
        return a.reshape(1, *a.shape)

    def stack4(a, lead_dims):
        return a.reshape(*lead_dims, 1, 4 * 32, PG)

    (r_in,) = _adamw_multi("adamw_in", lead(owns[0]), sibs[0][:, None], rels[0][:, None], [w_in], [m_w_in], [v_w_in], 4)
    r_meta, r_dw = _adamw_meta_dw(owns[1], sibs[1], rels[1], (meta_tokens, m_meta_tokens, v_meta_tokens),
                                  (w_dw, m_w_dw, v_w_dw))
    r_conv, r_pout, r_o = _adamw_multi("adamw_mix", owns[2], sibs[2], rels[2], [w_conv_out, w_pool_out, w_o],
                                       [m_w_conv_out, m_w_pool_out, m_w_o], [v_w_conv_out, v_w_pool_out, v_w_o], 1)
    (r_pool,) = _adamw_multi("adamw_pool", stack4(owns[3], ()), stack4(sibs[3], (1,)), stack4(rels[3], (3,)),
                             [w_pool.reshape(1, 128, PG)], [m_w_pool.reshape(1, 128, PG)], [v_w_pool.reshape(1, 128, PG)], 1)
    r_pool = tuple(a.reshape(w_pool.shape) for a in r_pool)
    r_gate, r_up = _adamw_multi("adamw_gu", owns[4], sibs[4], rels[4], [tr(w_ffn_gate), tr(w_ffn_up)],
                                [tr(m_w_ffn_gate), tr(m_w_ffn_up)], [tr(v_w_ffn_gate), tr(v_w_ffn_up)], 2)
    r_gate, r_up = tuple(tr(a) for a in r_gate), tuple(tr(a) for a in r_up)
    (r_down,) = _adamw_multi("adamw_down", lead(owns[5]), sibs[5][:, None], rels[5][:, None],
                             [w_ffn_down], [m_w_ffn_down], [v_w_ffn_down], 2)
    row = (1, D)
    loss, reps = _adamw_rep(
        rep_all,
        [g_mix, b_gate, b_dw, ln_g, ln_b, pool_scale, g_ffn, g_final.reshape(row)],
        [m_g_mix, m_b_gate, m_b_dw, m_ln_g, m_ln_b, m_pool_scale, m_g_ffn, m_g_final.reshape(row)],
        [v_g_mix, v_b_gate, v_b_dw, v_ln_g, v_ln_b, v_pool_scale, v_g_ffn, v_g_final.reshape(row)])
    r_gmix, r_bg, r_bdw, r_lg, r_lb, r_ps, r_gffn, r_gfin = reps
    r_gfin = tuple(a.reshape(D) for a in r_gfin)

    in_order = [r_meta, r_gmix, r_in, r_bg, r_dw, r_bdw, r_lg, r_lb, r_conv, r_pool, r_ps, r_pout, r_o, r_gffn,
                r_gate, r_up, r_down, r_gfin]
    return (loss.reshape(()), grad_x[None], *[r[0] for r in in_order], *[r[1] for r in in_order],
            *[r[2] for r in in_order], *[r[3] for r in in_order])
```

```python
import math

import jax
import jax.numpy as jnp
from jax import lax
from jax.experimental import pallas as pl
from jax.experimental.pallas import tpu as pltpu

F32, BF16 = jnp.float32, jnp.bfloat16
MESH_ID = pl.DeviceIdType.MESH
NDEV = 8

D = 1024
N_META = 16
CONV_K = 31
HALO = 16
POOL_WINDOWS = (2, 4, 8, 16)
PG = 256
DIN = 5 * D
DFF = 2816
FFB = DFF // NDEV
FFC = DFF // 2
INB = DIN // NDEV
RMS_EPS = 1e-6
LN_EPS = 1e-5
ADAM_LR, ADAM_B1, ADAM_B2, ADAM_EPS, ADAM_WD, ADAM_STEP = 0.001, 0.9, 0.999, 1e-08, 0.01, 10

TM = 384
TMS = 384
TM_IO = 704
TM_WG = 1408
TM_WM = 704
MIB = 2 ** 20


def _sig(x):
    return 0.5 * jnp.tanh(0.5 * x) + 0.5


def _dot(a, b):
    return jnp.dot(a, b, preferred_element_type=F32)


def _dot_nt(a, b):
    return lax.dot_general(a, b, (((1,), (1,)), ((), ())), preferred_element_type=F32)


def _dot_tn(a, b):
    return lax.dot_general(a, b, (((0,), (0,)), ((), ())), preferred_element_type=F32)


def _pick(tp, pref):
    return pref if tp % pref == 0 else TM


def _params(sem, vmem_mib):
    return pltpu.CompilerParams(dimension_semantics=sem, vmem_limit_bytes=vmem_mib * MIB)


def _load_once(first, pairs, sems):
    @pl.when(first)
    def _():
        cps = [pltpu.make_async_copy(s, d, sems.at[k]) for k, (s, d) in enumerate(pairs)]
        for cp in cps:
            cp.start()
        for cp in cps:
            cp.wait()


def _place():
    x, y, c = lax.axis_index("x"), lax.axis_index("y"), lax.axis_index("c")
    return x, y, c


class _Gather:
    def __init__(self, groups, dtypes):
        self.groups, self.dtypes, self.n = groups, dtypes, len(groups)
        self.arrays = [a for _, parts in groups for a, _, _ in parts]
        self.out_shape = [jax.ShapeDtypeStruct((NDEV, *s), dt) for (s, _), dt in zip(groups, dtypes)]
        self.scratch = [pltpu.VMEM(s, dt) for (s, _), dt in zip(groups, dtypes)] + [
            pltpu.SemaphoreType.DMA((7 * self.n,)), pltpu.SemaphoreType.DMA((7 * self.n,)),
            pltpu.SemaphoreType.DMA((self.n,))]

    def bind(self, ins, outs, scratch):
        self.ins, self.outs, self.stages = ins, outs, scratch[:self.n]
        self.send_sems, self.recv_sems, self.local_sems = scratch[self.n:]
        return self

    def _copy(self, w, k, block, to, src=None):
        dst = self.outs[w].at[4 * block[0] + 2 * block[1] + block[2]]
        return pltpu.make_async_remote_copy(
            src_ref=dst if src is None else src, dst_ref=dst,
            send_sem=self.send_sems.at[7 * w + k], recv_sem=self.recv_sems.at[7 * w + k],
            device_id=to, device_id_type=MESH_ID)

    def _first(self):
        x, y, c = _place()
        me, sibling = (x, y, c), (x, y, 1 - c)
        chips = [(1 - x, y), (x, 1 - y), (1 - x, 1 - y)]
        mine, first = [], []
        for w in range(self.n):
            mine.append(pltpu.make_async_copy(self.stages[w], self.outs[w].at[4 * x + 2 * y + c], self.local_sems.at[w]))
            first.append(self._copy(w, 0, me, sibling, src=self.stages[w]))
            first += [self._copy(w, 1 + j, me, (*chip, c), src=self.stages[w]) for j, chip in enumerate(chips)]
        return mine, first

    def _passed(self):
        x, y, c = _place()
        chips = [(1 - x, y), (x, 1 - y), (1 - x, 1 - y)]
        return [self._copy(w, 4 + j, (*chip, c), (x, y, 1 - c)) for w in range(self.n) for j, chip in enumerate(chips)]

    def issue(self):
        a = 0
        for w in range(self.n):
            shape, parts = self.groups[w]
            if sum(arr.size for arr, _, _ in parts) < math.prod(shape):
                self.stages[w][...] = jnp.zeros(shape, self.dtypes[w])
            for _, dst, src in parts:
                self.stages[w][dst] = self.ins[a][src].astype(self.dtypes[w])
                a += 1
        mine, first = self._first()
        for cp in mine + first:
            cp.start()

    def forward(self):
        x, y, c = _place()
        chips = [(1 - x, y), (x, 1 - y), (1 - x, 1 - y)]
        passed = self._passed()
        for w in range(self.n):
            for j, chip in enumerate(chips):
                self._copy(w, 1 + j, (*chip, c), (x, y, c)).wait_recv()
                passed[3 * w + j].start()

    def finish(self):
        x, y, c = _place()
        chips = [(1 - x, y), (x, 1 - y), (1 - x, 1 - y)]
        for w in range(self.n):
            self._copy(w, 0, (x, y, 1 - c), (x, y, c)).wait_recv()
            for j, chip in enumerate(chips):
                self._copy(w, 4 + j, (*chip, 1 - c), (x, y, c)).wait_recv()
        mine, first = self._first()
        for cp in first + self._passed():
            cp.wait_send()
        for cp in mine:
            cp.wait()


class _ChipExchange:
    def __init__(self, qs):
        self.n = len(qs)
        self.out_shape = [jax.ShapeDtypeStruct(q.shape, q.dtype) for q in qs]
        self.scratch = [pltpu.SemaphoreType.DMA((3 * self.n,)), pltpu.SemaphoreType.DMA((3 * self.n,))]

    def bind(self, qs, rels, scratch):
        self.qs, self.rels = qs, rels
        self.send_sems, self.recv_sems = scratch
        return self

    def _copies(self):
        x, y, c = _place()
        chips = [(1 - x, y), (x, 1 - y), (1 - x, 1 - y)]
        return [pltpu.make_async_remote_copy(
            src_ref=self.qs[w].at[j], dst_ref=self.rels[w].at[j],
            send_sem=self.send_sems.at[3 * w + j], recv_sem=self.recv_sems.at[3 * w + j],
            device_id=(*chips[j], c), device_id_type=MESH_ID) for w in range(self.n) for j in range(3)]

    def issue(self):
        for cp in self._copies():
            cp.start()

    def finish(self):
        cps = self._copies()
        for cp in cps:
            cp.wait_recv()
        for cp in cps:
            cp.wait_send()


def _reduce_scatter(parts, small):
    n = len(parts)
    blks = [p.shape[1:] for p in parts]

    def body(*refs):
        ps, small_ref = refs[:n], refs[n]
        o = n + 1
        owns, sibs, rels, small_out = refs[o:o + n], refs[o + n:o + 2 * n], refs[o + 2 * n:o + 3 * n], refs[o + 3 * n]
        o += 3 * n + 1
        pa, pb, qst = refs[o:o + n], refs[o + n:o + 2 * n], refs[o + 2 * n:o + 3 * n]
        s1_send, s1_recv, s2_send, s2_recv, sm_send, sm_recv, lsem = refs[o + 3 * n:]
        x, y, c = _place()
        me = 4 * x + 2 * y + c
        sibling = (x, y, 1 - c)
        chips = [(1 - x, y), (x, 1 - y), (1 - x, 1 - y)]
        all_chips = [(x, y)] + chips

        own_cps = []
        for w in range(n):
            cp = pltpu.make_async_copy(ps[w].at[me], owns[w], lsem.at[w])
            cp.start()
            own_cps.append(cp)
        sm_own = pltpu.make_async_copy(small_ref, small_out.at[me], lsem.at[n])
        sm_own.start()

        def small_copy(r):
            peer = ((x + (r >> 2)) % 2, (y + ((r >> 1) & 1)) % 2, (c + (r & 1)) % 2)
            return pltpu.make_async_remote_copy(
                src_ref=small_ref, dst_ref=small_out.at[me], send_sem=sm_send.at[r - 1], recv_sem=sm_recv.at[r - 1],
                device_id=peer, device_id_type=MESH_ID)

        sm_cps = [small_copy(r) for r in range(1, NDEV)]
        for cp in sm_cps:
            cp.start()

        def pair_copy(w, rel):
            cx, cy = all_chips[rel]
            return pltpu.make_async_remote_copy(
                src_ref=ps[w].at[4 * cx + 2 * cy + (1 - c)], dst_ref=sibs[w].at[rel],
                send_sem=s1_send.at[4 * w + rel], recv_sem=s1_recv.at[4 * w + rel],
                device_id=sibling, device_id_type=MESH_ID)

        def chip_copy(w, j):
            return pltpu.make_async_remote_copy(
                src_ref=qst[w].at[j], dst_ref=rels[w].at[j],
                send_sem=s2_send.at[3 * w + j], recv_sem=s2_recv.at[3 * w + j],
                device_id=(*chips[j], c), device_id_type=MESH_ID)

        pair_cps = [pair_copy(w, rel) for w in range(n) for rel in (1, 2, 3, 0)]
        for cp in pair_cps:
            cp.start()
        chip_cps = []
        for w in range(n):
            for j, (cx, cy) in enumerate(chips):
                pair_copy(w, 1 + j).wait_recv()
                la = pltpu.make_async_copy(ps[w].at[4 * cx + 2 * cy + c], pa[w], lsem.at[n + 1])
                lb = pltpu.make_async_copy(sibs[w].at[1 + j], pb[w], lsem.at[n + 2])
                la.start()
                lb.start()
                la.wait()
                lb.wait()
                qst[w][j] = (pa[w][...].astype(F32) + pb[w][...].astype(F32)).astype(BF16)
                cp = chip_copy(w, j)
                cp.start()
                chip_cps.append(cp)
        for w in range(n):
            pair_copy(w, 0).wait_recv()
            for j in range(3):
                chip_copy(w, j).wait_recv()
        for cp in sm_cps:
            cp.wait_recv()
        for cp in pair_cps + chip_cps + sm_cps:
            cp.wait_send()
        for cp in own_cps:
            cp.wait()
        sm_own.wait()

    any_spec = pl.BlockSpec(memory_space=pl.ANY)
    outs = pl.pallas_call(
        body, name="rs_grads",
        out_shape=[jax.ShapeDtypeStruct(b, BF16) for b in blks]
        + [jax.ShapeDtypeStruct((4, *b), BF16) for b in blks]
        + [jax.ShapeDtypeStruct((3, *b), BF16) for b in blks]
        + [jax.ShapeDtypeStruct((NDEV, *small.shape), F32)],
        in_specs=[any_spec] * (n + 1),
        out_specs=[any_spec] * (3 * n + 1),
        scratch_shapes=[pltpu.VMEM(b, BF16) for b in blks] + [pltpu.VMEM(b, BF16) for b in blks]
        + [pltpu.VMEM((3, *b), BF16) for b in blks]
        + [pltpu.SemaphoreType.DMA((4 * n,)), pltpu.SemaphoreType.DMA((4 * n,)),
           pltpu.SemaphoreType.DMA((3 * n,)), pltpu.SemaphoreType.DMA((3 * n,)),
           pltpu.SemaphoreType.DMA((NDEV - 1,)), pltpu.SemaphoreType.DMA((NDEV - 1,)),
           pltpu.SemaphoreType.DMA((n + 3,))],
        compiler_params=pltpu.CompilerParams(vmem_limit_bytes=40 * MIB),
    )(*parts, small)
    return outs[:n], outs[n:2 * n], outs[2 * n:3 * n], outs[3 * n]


class _PairSum:
    def __init__(self, parts, keep_q=True):
        self.n = n = len(parts)
        self.keep_q = keep_q
        blks = [p.shape[1:] for p in parts]
        self.out_shape = [jax.ShapeDtypeStruct(b, BF16) for b in blks] + [jax.ShapeDtypeStruct((1, *b), BF16) for b in blks]
        if keep_q:
            self.out_shape += [jax.ShapeDtypeStruct((3, *b), BF16) for b in blks]
        self.scratch = [pltpu.VMEM((3, *b), BF16) for b in blks] * 3 + [
            pltpu.SemaphoreType.DMA((4 * n,)), pltpu.SemaphoreType.DMA((4 * n,)), pltpu.SemaphoreType.DMA((5 * n,))]

    def bind(self, ps, outs, scratch):
        n = self.n
        self.ps, self.owns, self.sibs, self.qs = ps, outs[:n], outs[n:2 * n], outs[2 * n:]
        self.pa, self.pb, self.qst = scratch[:n], scratch[n:2 * n], scratch[2 * n:3 * n]
        self.s_send, self.s_recv, self.lsem = scratch[3 * n:]
        return self

    def _local(self, with_q):
        n = self.n
        x, y, c = _place()
        chips = [(1 - x, y), (x, 1 - y), (1 - x, 1 - y)]
        own = [pltpu.make_async_copy(self.ps[w].at[4 * x + 2 * y + c], self.owns[w], self.lsem.at[w]) for w in range(n)]
        mine = [[pltpu.make_async_copy(self.ps[w].at[4 * cx + 2 * cy + c], self.pa[w].at[j], self.lsem.at[2 * n + 3 * w + j])
                 for j, (cx, cy) in enumerate(chips)] for w in range(n)]
        outq = [pltpu.make_async_copy(self.qst[w], self.qs[w], self.lsem.at[n + w]) for w in range(n)] if with_q else []
        return own, mine, outq

    def _pair(self, w, rel):
        x, y, c = _place()
        cx, cy = [(x, y), (1 - x, y), (x, 1 - y), (1 - x, 1 - y)][rel]
        return pltpu.make_async_remote_copy(
            src_ref=self.ps[w].at[4 * cx + 2 * cy + (1 - c)],
            dst_ref=self.sibs[w].at[0] if rel == 0 else self.pb[w].at[rel - 1],
            send_sem=self.s_send.at[4 * w + rel], recv_sem=self.s_recv.at[4 * w + rel],
            device_id=(x, y, 1 - c), device_id_type=MESH_ID)

    def issue(self):
        own, mine, _ = self._local(False)
        for cp in own + [cp for row in mine for cp in row]:
            cp.start()
        for w in range(self.n):
            for rel in (1, 2, 3, 0):
                self._pair(w, rel).start()

    def finish(self):
        own, mine, outq = self._local(self.keep_q)
        for w in range(self.n):
            for j in range(3):
                self._pair(w, 1 + j).wait_recv()
                mine[w][j].wait()
                self.qst[w][j] = (self.pa[w][j].astype(F32) + self.pb[w][j].astype(F32)).astype(BF16)
            if self.keep_q:
                outq[w].start()
        for w in range(self.n):
            self._pair(w, 0).wait_recv()
        for w in range(self.n):
            for rel in range(4):
                self._pair(w, rel).wait_send()
        for cp in own + outq:
            cp.wait()

    def results(self, outs):
        n = self.n
        return outs[:n], outs[n:2 * n], outs[2 * n:3 * n]


def _rs_pair(name, parts):
    ps = _PairSum(parts)
    n = ps.n

    def body(*refs):
        ps.bind(refs[:n], refs[n:4 * n], refs[4 * n:])
        ps.issue()
        ps.finish()

    any_spec = pl.BlockSpec(memory_space=pl.ANY)
    outs = pl.pallas_call(
        body, name=name, out_shape=ps.out_shape,
        in_specs=[any_spec] * n, out_specs=[any_spec] * (3 * n), scratch_shapes=ps.scratch,
        compiler_params=pltpu.CompilerParams(vmem_limit_bytes=48 * MIB),
    )(*parts)
    return ps.results(outs)


def _adamw_math(g, w, m, v):
    m = ADAM_B1 * m + (1.0 - ADAM_B1) * g
    v = ADAM_B2 * v + (1.0 - ADAM_B2) * (g * g)
    m_hat = m / (1.0 - ADAM_B1 ** ADAM_STEP)
    v_hat = v / (1.0 - ADAM_B2 ** ADAM_STEP)
    delta = -ADAM_LR * (m_hat / (jnp.sqrt(v_hat) + ADAM_EPS) + ADAM_WD * w)
    return delta, m, v


def _adamw_multi(name, own, sib, rel, ws, ms, vs, row_grid):
    k_n, r_n, c_n = own.shape
    rbk = r_n // row_grid

    def body(*refs):
        own_ref, sib_ref, r0_ref, r1_ref, r2_ref = refs[:5]
        w_refs, m_refs, v_refs = refs[5:5 + k_n], refs[5 + k_n:5 + 2 * k_n], refs[5 + 2 * k_n:5 + 3 * k_n]
        outs = refs[5 + 3 * k_n:]
        for k in range(k_n):
            g = own_ref[k].astype(F32) + sib_ref[k].astype(F32)
            g = g + r0_ref[k].astype(F32)
            g = g + r1_ref[k].astype(F32)
            g = g + r2_ref[k].astype(F32)
            delta, mm, vv = _adamw_math(g, w_refs[k][0], m_refs[k][0], v_refs[k][0])
            outs[4 * k][0] = g
            outs[4 * k + 1][0] = delta
            outs[4 * k + 2][0] = mm
            outs[4 * k + 3][0] = vv

    def lead(j):
        return pl.BlockSpec((None, k_n, rbk, c_n), lambda g: (j, 0, g, 0))

    wspec = pl.BlockSpec((1, rbk, c_n), lambda g: (0, g, 0))
    shp = jax.ShapeDtypeStruct((1, r_n, c_n), F32)
    res = pl.pallas_call(
        body, name=name, grid=(row_grid,),
        in_specs=[pl.BlockSpec((k_n, rbk, c_n), lambda g: (0, g, 0)), lead(0), lead(0), lead(1), lead(2)] + [wspec] * (3 * k_n),
        out_specs=[wspec] * (4 * k_n), out_shape=[shp] * (4 * k_n),
        compiler_params=_params(("arbitrary",), 40),
    )(own, sib, rel, rel, rel, *ws, *ms, *vs)
    return [tuple(res[4 * k:4 * k + 4]) for k in range(k_n)]


def _adamw_meta_dw(own, sib, rel, meta, dw):
    def body(own_ref, sib_ref, rel_ref, wm, mm, vm, wd, md, vd, *outs):
        def gsum(rows):
            g = own_ref[rows, :].astype(F32) + sib_ref[0, rows, :].astype(F32)
            for j in range(3):
                g = g + rel_ref[j, rows, :].astype(F32)
            return g

        g = gsum(pl.ds(0, N_META))
        delta, m2, v2 = _adamw_math(g, wm[...], mm[...], vm[...])
        for o, val in zip(outs[:4], (g, delta, m2, v2)):
            o[...] = val
        g = gsum(pl.ds(N_META, CONV_K))
        delta, m2, v2 = _adamw_math(g, wd[0], md[0], vd[0])
        for o, val in zip(outs[4:], (g, delta, m2, v2)):
            o[0] = val

    s_meta = jax.ShapeDtypeStruct(meta[0].shape, F32)
    s_dw = jax.ShapeDtypeStruct(dw[0].shape, F32)
    res = pl.pallas_call(body, name="adamw_meta_dw", out_shape=[s_meta] * 4 + [s_dw] * 4)(own, sib, rel, *meta, *dw)
    return tuple(res[:4]), tuple(res[4:])


REP_ROWS = 16


def _adamw_rep(gathered, ws, ms, vs):
    rows = [(0, 1), (1, 2), (3, 1), (4, 1), (5, 1), (6, 1), (7, 1), (8, 1)]

    def body(g_ref, *refs):
        w_refs, m_refs, v_refs = refs[:8], refs[8:16], refs[16:24]
        loss_ref, outs, acc = refs[24], refs[25:57], refs[57]
        g = g_ref[0]
        for d in range(1, NDEV):
            g = g + g_ref[d]
        acc[...] = g
        loss_ref[...] = (0.5 / D) * jnp.sum(acc[pl.ds(9, 1), :], axis=1, keepdims=True)
        for p, (r0, nr) in enumerate(rows):
            for h in range(nr):
                cols = pl.ds(h * D, D)
                gp = acc[pl.ds(r0 + h, 1), :]
                delta, mm, vv = _adamw_math(gp, w_refs[p][:, cols], m_refs[p][:, cols], v_refs[p][:, cols])
                for o, val in zip(outs[4 * p:4 * p + 4], (gp, delta, mm, vv)):
                    o[:, cols] = val

    shapes = [jax.ShapeDtypeStruct(w.shape, F32) for w in ws]
    res = pl.pallas_call(
        body, name="adamw_rep",
        out_shape=[jax.ShapeDtypeStruct((1, 1), F32)] + [s for s in shapes for _ in range(4)],
        scratch_shapes=[pltpu.VMEM((REP_ROWS, D), F32)],
    )(gathered, *ws, *ms, *vs)
    return res[0], [tuple(res[1 + 4 * p:5 + 4 * p]) for p in range(8)]


def _load_ffn(i, j, wgu_hbm, wgu, wdn_hbm, wdn, sems):
    half = NDEV // 2

    def copies(ch):
        pairs = [(wgu_hbm.at[half * ch + d, g], wgu.at[g, ch, pl.ds(FFB * d, FFB), :]) for g in range(2) for d in range(half)]
        pairs.append((wdn_hbm.at[ch], wdn.at[ch]))
        return [pltpu.make_async_copy(s, t, sems.at[(2 * half + 1) * ch + k]) for k, (s, t) in enumerate(pairs)]

    @pl.when((i == 0) & (j == 0))
    def _():
        for cp in copies(0) + copies(1):
            cp.start()

    for ch in range(2):
        @pl.when((i == 0) & (j == ch))
        def _():
            for cp in copies(ch):
                cp.wait()


def _win_pairs(w_hbm, w_vm):
    return [(w_hbm.at[q], w_vm.at[q // 2, :, pl.ds(2 * INB * (q % 2), 2 * INB)]) for q in range(4)]


def _whole(a):
    nd = a.ndim
    return pl.BlockSpec(a.shape, lambda *g: (0,) * nd)


CHIPW = 2 * INB
PHASE_CHIP = (1, 0, 2)
assert PHASE_CHIP[2] == 2
Z_PARTS = 4


class _GatherIn:
    scratch = [pltpu.VMEM((D, INB), BF16), pltpu.SemaphoreType.DMA((7,)), pltpu.SemaphoreType.DMA((7,)),
               pltpu.SemaphoreType.DMA((1,))]

    def bind(self, w_ref, w_vm, scratch):
        self.w_ref, self.w_vm = w_ref, w_vm
        self.stage, self.send_sems, self.recv_sems, self.local_sem = scratch
        return self

    def _win(self, chip, core):
        return self.w_vm.at[2 * chip[0] + chip[1], core]

    def _copy(self, k, chip, core, to, src=None):
        dst = self._win(chip, core)
        return pltpu.make_async_remote_copy(
            src_ref=dst if src is None else src, dst_ref=dst, send_sem=self.send_sems.at[k],
            recv_sem=self.recv_sems.at[k], device_id=to, device_id_type=MESH_ID)

    def _mine(self, cs):
        x, y, _ = _place()
        return pltpu.make_async_copy(self.stage, self._win((x, y), cs), self.local_sem.at[0])

    def issue(self, cs):
        x, y, _ = _place()
        chips = [(1 - x, y), (x, 1 - y), (1 - x, 1 - y)]
        self.stage[...] = self.w_ref[0].astype(BF16)
        self._mine(cs).start()
        self._copy(0, (x, y), cs, (x, y, 1 - cs), src=self.stage).start()
        for j in PHASE_CHIP[:2]:
            self._copy(1 + j, (x, y), cs, (*chips[j], cs), src=self.stage).start()

    def wait_chip(self, phase, cs):
        x, y, _ = _place()
        chips = [(1 - x, y), (x, 1 - y), (1 - x, 1 - y)]
        if phase == 0:
            self._mine(cs).wait()
            self._copy(0, (x, y), 1 - cs, (x, y, cs)).wait_recv()
            return
        j = PHASE_CHIP[phase - 1]
        self._copy(1 + j, chips[j], cs, (x, y, cs)).wait_recv()
        self._copy(4 + j, chips[j], cs, (x, y, 1 - cs)).start()
        if phase == 1:
            self._copy(3, (x, y), cs, (*chips[2], cs), src=self.stage).start()
        self._copy(4 + j, chips[j], 1 - cs, (x, y, cs)).wait_recv()

    def finish(self, cs):
        x, y, _ = _place()
        for k in range(7):
            self._copy(k, (x, y), cs, (x, y, cs), src=self.stage).wait_send()


def _fwd_in(x2, g_mix, w_in, order, tp, ag, ags):
    tm = _pick(tp, TM_IO)
    nt = tp // tm
    nx_last = x2.shape[0] - (nt - 1) * tm
    na, ng, ns = len(ag.arrays), ag.n, len(ags.arrays)
    gin = _GatherIn()
    assert tm % (8 * Z_PARTS) == 0
    zr = tm // Z_PARTS

    def body(order_ref, *refs):
        x_ref, g_ref, w_ref = refs[:3]
        o = 3 + na + ns
        h_ref, z_hbm, u_ref, wout_ref = refs[o:o + 4]
        s = o + 4 + ng + 1
        w_vm, u_all, osem, sm_vm, zbuf, zsem = refs[s:s + 6]
        gin.bind(w_ref, w_vm, refs[s + 6:s + 10])
        ag.bind(refs[3:3 + na], refs[o + 4:o + 4 + ng], refs[s + 10:s + 10 + len(ag.scratch)])
        ags.bind(refs[3 + na:3 + na + ns], refs[o + 4 + ng:o + 5 + ng], refs[s + 10 + len(ag.scratch):])
        ph, i = pl.program_id(0), pl.program_id(1)
        core = lax.axis_index("c")
        first = (ph == 0) & (i == 0)
        last = (ph == 3) & (i == nt - 1)
        @pl.when(first)
        def _():
            ags.issue()

        for cs in range(2):
            @pl.when(first & (core == cs))
            def _():
                gin.issue(cs)

        @pl.when((ph == 0) & (i == max(nt - 2, 0)))
        def _():
            ags.forward()

        for cs in range(2):
            for p in range(4):
                @pl.when((ph == p) & (i == 0) & (core == cs))
                def _():
                    gin.wait_chip(p, cs)

        @pl.when((ph == 2) & (i == 0))
        def _():
            ag.issue()

        out_copies = [pltpu.make_async_copy(w_vm.at[k, c], wout_ref.at[k, :, pl.ds(INB * c, INB)], osem.at[2 * k + c])
                      for k in range(4) for c in range(2)]

        @pl.when((ph == 3) & (i == 0))
        def _():
            for cp in out_copies:
                cp.start()

        @pl.when((ph == 0) & (i < nt - 1))
        def _():
            h_ref[...] = x_ref[...]

        @pl.when((ph == 0) & (i == nt - 1))
        def _():
            ags.finish()
            cp = pltpu.make_async_copy(ags.outs[0], sm_vm, osem.at[8])
            cp.start()
            h_ref[pl.ds(0, nx_last), :] = x_ref[pl.ds(0, nx_last), :]
            h_ref[pl.ds(nx_last, tm - nx_last - N_META), :] = jnp.zeros((tm - nx_last - N_META, D), F32)
            cp.wait()
            for d in range(NDEV):
                h_ref[pl.ds(tm - N_META, N_META), pl.ds(128 * d, 128)] = sm_vm[d, pl.ds(0, N_META), :]

        @pl.when(ph == 0)
        def _():
            xv = h_ref[...]
            r = lax.rsqrt(jnp.mean(xv * xv, axis=-1, keepdims=True) + RMS_EPS)
            u = (xv * r * g_ref[...]).astype(BF16)
            u_ref[...] = u
            u_all[i] = u

        step = ph * nt + i
        slot = lax.rem(step, 2)

        def z_copies(sl):
            col = pl.multiple_of(order_ref[ph] * CHIPW, 128)
            return [pltpu.make_async_copy(zbuf.at[sl, pl.ds(zr * r, zr)],
                                          z_hbm.at[pl.ds(pl.multiple_of(i * tm + zr * r, 8), zr), pl.ds(col, CHIPW)],
                                          zsem.at[sl, r]) for r in range(Z_PARTS)]

        @pl.when(step >= 2)
        def _():
            for cp in z_copies(slot):
                cp.wait()

        for c in range(2):
            zbuf[slot, :, INB * c:INB * (c + 1)] = _dot(u_all[i], w_vm[order_ref[ph], c])
        for cp in z_copies(slot):
            cp.start()

        @pl.when(last)
        def _():
            ag.forward()
            ag.finish()
            for cp in out_copies + z_copies(1 - slot) + z_copies(slot):
                cp.wait()

        for cs in range(2):
            @pl.when(last & (core == cs))
            def _():
                gin.finish(cs)

    def rows(ph, i, order):
        return (jnp.where(ph == 0, i, nt - 1), 0)

    tile = pl.BlockSpec((tm, D), rows)
    anys = pl.BlockSpec(memory_space=pl.ANY)
    res = pl.pallas_call(
        body, name="fwd_in",
        grid_spec=pltpu.PrefetchScalarGridSpec(
            num_scalar_prefetch=1, grid=(4, nt),
            in_specs=[tile, pl.BlockSpec((1, D), lambda ph, i, order: (0, 0)), _whole(w_in)]
            + [_whole(a) for a in ag.arrays + ags.arrays],
            out_specs=[tile, anys, tile, anys] + [anys] * (ng + 1),
            scratch_shapes=[pltpu.VMEM((4, 2, D, INB), BF16), pltpu.VMEM((nt, tm, D), BF16), pltpu.SemaphoreType.DMA((9,)),
                            pltpu.VMEM(ags.out_shape[0].shape, F32), pltpu.VMEM((2, tm, CHIPW), F32),
                            pltpu.SemaphoreType.DMA((2, Z_PARTS))] + gin.scratch + ag.scratch + ags.scratch),
        out_shape=[jax.ShapeDtypeStruct((tp, D), F32), jax.ShapeDtypeStruct((tp, DIN), F32),
                   jax.ShapeDtypeStruct((tp, D), BF16), jax.ShapeDtypeStruct((4, D, CHIPW), BF16)]
        + ag.out_shape + ags.out_shape,
        compiler_params=_params(("arbitrary", "arbitrary"), 58),
    )(order, x2, g_mix, w_in, *ag.arrays, *ags.arrays)
    return res[:4], res[4:4 + ng], res[4 + ng]


def _halo_specs(col, nt, width=D):
    r = TM // HALO
    nb = nt * r
    return [pl.BlockSpec((HALO, width), lambda i: ((i * r + nb - 1) % nb, col)),
            pl.BlockSpec((TM, width), lambda i: (i, col)),
            pl.BlockSpec((HALO, width), lambda i: (((i + 1) * r) % nb, col))]


NCB = D // 128
TME = TM + 2 * HALO


def _tm_fill(dst, time0, groups, tile_fn, unroll=1):
    def body(g, c):
        for j in range(NCB):
            dst[pl.ds((time0 + 8 * g) * NCB + j, 8, stride=NCB), :] = tile_fn(pl.multiple_of(8 * g, 8), pl.ds(128 * j, 128))
        return c

    lax.fori_loop(0, groups, body, 0, unroll=unroll)


def _tm_fill_ext(dst, left, cur, right, fn, unroll=1):
    _tm_fill(dst, 0, HALO // 8, lambda r, l: fn(left, pl.ds(r, 8), l), unroll)
    _tm_fill(dst, HALO, TM // 8, lambda r, l: fn(cur, pl.ds(r, 8), l), unroll)
    _tm_fill(dst, HALO + TM, HALO // 8, lambda r, l: fn(right, pl.ds(r, 8), l), unroll)


def _tm_read(src, groups, store_fn):
    def body(g, c):
        for j in range(NCB):
            store_fn(pl.ds(pl.multiple_of(8 * g, 8), 8), pl.ds(128 * j, 128), src[pl.ds(8 * g * NCB + j, 8, stride=NCB), :])
        return c

    lax.fori_loop(0, groups, body, 0, unroll=2)


def _tm_rows(t):
    return pl.ds(t * NCB if isinstance(t, int) else pl.multiple_of(t * NCB, NCB), NCB)


def _tm_at(ref, t):
    return ref[_tm_rows(t), :]


def _by_group(sub, vals):
    return jnp.where(sub < 2, vals[0], jnp.where(sub < 4, vals[1], jnp.where(sub < 6, vals[2], vals[3])))


def _pool_cnt(b, seq, tp, sub):
    b = jnp.where(b < 0, b + tp, b)
    b = jnp.where(b >= tp, b - tp, b)
    t = jnp.where(b < seq, b + N_META, b - (tp - N_META))
    cnts = []
    for win in POOL_WINDOWS:
        left = win // 2
        lo = jnp.maximum(t - left, 0)
        hi = jnp.minimum(t + win - left, seq + N_META)
        cnts.append(jnp.maximum(hi - lo, 1).astype(F32))
    return _by_group(sub, cnts)


def _edge_rows(seq, tp):
    reach = max(POOL_WINDOWS) // 2
    return [tp - N_META + t for t in range(reach)] + [seq - reach + 1 + t for t in range(reach - 1)]


def _edge_gain(b, seq, tp, sub):
    return _by_group(sub, [float(w) for w in POOL_WINDOWS]) / _pool_cnt(b, seq, tp, sub)


def _nested_windows(at, lo_offs):
    sums, s, have = [], None, set()
    for g, win in enumerate(POOL_WINDOWS):
        for o in range(lo_offs[g], lo_offs[g] + win):
            if o not in have:
                have.add(o)
                s = at(o) if s is None else s + at(o)
        sums.append(s)
    return sums


def _seq_fwd(z, w_dw, b_dw, seq, gat):
    tp = z.shape[0]
    nt = tp // TM
    na, ng = len(gat.arrays), gat.n

    def body(*refs):
        av_l, av, av_r, ag_l, ag, ag_r, p_l, p, p_r, w_ref, b_ref = refs[:11]
        ac_ref, m_ref = refs[11 + na:13 + na]
        a3, p3, o3, m3, w3, b3, m2d = refs[13 + na + ng:20 + na + ng]
        gat.bind(refs[11:11 + na], refs[13 + na:13 + na + ng], refs[20 + na + ng:])
        i = pl.program_id(0)
        sub = lax.broadcasted_iota(jnp.int32, (NCB, 128), 0)

        @pl.when(i == 0)
        def _():
            gat.issue()
            _tm_fill(w3, 0, 4, lambda r, l: w_ref[pl.ds(r, 8), l])
            for j in range(NCB):
                b3[pl.ds(j, 1), :] = b_ref[:, pl.ds(128 * j, 128)]

        @pl.when(i == max(nt - 2, 0))
        def _():
            gat.forward()

        _tm_fill_ext(a3, (av_l, ag_l), (av, ag), (av_r, ag_r), lambda vg, r, l: vg[0][r, l] * _sig(vg[1][r, l]), unroll=2)
        _tm_fill_ext(p3, p_l, p, p_r, lambda ref, r, l: ref[r, l])

        def conv(g, c):
            accs = [b3[...]] * 16
            for k in range(CONV_K):
                wk = _tm_at(w3, k)
                for t in range(16):
                    accs[t] = accs[t] + wk * _tm_at(a3, 16 * g + t + k + 1)
            for t in range(16):
                o3[_tm_rows(16 * g + t), :] = accs[t]
            return c

        lax.fori_loop(0, TM // 16, conv, 0)
        _tm_read(o3, TM // 8, lambda r, l, tile: ac_ref.__setitem__((r, l), tile))

        inv = _by_group(sub, [1.0 / w for w in POOL_WINDOWS])

        def pool(g, c):
            for t in range(8):
                e = 8 * g + t + HALO
                sums = _nested_windows(lambda o: _tm_at(p3, e + o), [-(w // 2) for w in POOL_WINDOWS])
                m3[_tm_rows(8 * g + t), :] = _by_group(sub, sums) * inv - _tm_at(p3, e)
            return c

        lax.fori_loop(0, TM // 8, pool, 0)
        for b in _edge_rows(seq, tp):
            r = b - i * TM

            @pl.when((r >= 0) & (r < TM))
            def _():
                pv = _tm_at(p3, r + HALO)
                m3[_tm_rows(r), :] = (_tm_at(m3, r) + pv) * _edge_gain(b, seq, tp, sub) - pv

        _tm_read(m3, TM // 8, lambda r, l, tile: m2d.__setitem__((r, l), tile))
        m_ref[...] = m2d[...].astype(BF16)

        @pl.when(i == nt - 1)
        def _():
            gat.finish()

    tmaj = pltpu.VMEM((TM * NCB, 128), F32)
    text = pltpu.VMEM((TME * NCB, 128), F32)
    res = pl.pallas_call(
        body, name="seq_fwd", grid=(nt,),
        in_specs=_halo_specs(0, nt) + _halo_specs(1, nt) + _halo_specs(2, nt)
        + [pl.BlockSpec((32, D), lambda i: (0, 0)), pl.BlockSpec((1, D), lambda i: (0, 0))] + [_whole(a) for a in gat.arrays],
        out_specs=[pl.BlockSpec((TM, D), lambda i: (i, 0))] * 2 + [pl.BlockSpec(memory_space=pl.ANY)] * ng,
        out_shape=[jax.ShapeDtypeStruct((tp, D), F32), jax.ShapeDtypeStruct((tp, D), BF16)] + gat.out_shape,
        scratch_shapes=[text, text, tmaj, tmaj, pltpu.VMEM((32 * NCB, 128), F32), pltpu.VMEM((NCB, 128), F32),
                        pltpu.VMEM((TM, D), F32)] + gat.scratch,
        compiler_params=_params(("arbitrary",), 52),
    )(z, z, z, z, z, z, z, z, z, w_dw, b_dw, *gat.arrays)
    return res[:2], res[2:]


def _ln_stats(ac):
    mu = jnp.mean(ac, axis=-1, keepdims=True)
    xc = ac - mu
    rl = lax.rsqrt(jnp.mean(xc * xc, axis=-1, keepdims=True) + LN_EPS)
    return xc * rl, rl


def _pool_mix(m, wp_ref):
    return jnp.concatenate(
        [_dot(m[:, g * PG:(g + 1) * PG], wp_ref[:, g].reshape(PG, PG)) for g in range(4)], axis=1)


def _mix_fwd(ac, m, z, h0, b_gate, ln_g, ln_b, pool_scale, g_mixw, g_pool, gat):
    tp = h0.shape[0]
    tms = TM
    nt = tp // tms
    na, ng = len(gat.arrays), gat.n

    def body(*refs):
        ac_ref, m_ref, zga, zgb, h_ref, bg_ref, lg_ref, lb_ref, ps_ref, wm_hbm, wp_hbm = refs[:11]
        h1_ref, s_ref, mg_ref, q_ref = refs[11 + na:15 + na]
        wm, wp, sems = refs[15 + na + ng:18 + na + ng]
        gat.bind(refs[11:11 + na], refs[15 + na:15 + na + ng], refs[18 + na + ng:])
        i = pl.program_id(0)

        @pl.when(i == 0)
        def _():
            gat.issue()

        @pl.when(i == max(nt - 4, 0))
        def _():
            gat.forward()

        @pl.when(i == nt - 1)
        def _():
            gat.finish()

        _load_once(i == 0, [(wm_hbm, wm), (wp_hbm, wp)], sems)
        n, _ = _ln_stats(ac_ref[...])
        l = n * lg_ref[...] + lb_ref[...]
        s = (l * _sig(l)).astype(BF16)
        s_ref[...] = s
        yc = _dot(s, wm[:, 0].reshape(D, D))
        q = (_pool_mix(m_ref[...], wp) * ps_ref[...]).astype(BF16)
        q_ref[...] = q
        yp = _dot(q, wm[:, 1].reshape(D, D))
        ga = _sig(zga[...] + bg_ref[:, :D])
        gb = _sig(zgb[...] + bg_ref[:, D:])
        merged = (ga * yc + gb * yp).astype(BF16)
        mg_ref[...] = merged
        h1_ref[...] = h_ref[...] + _dot(merged, wm[:, 2].reshape(D, D))

    def tile(col=0):
        return pl.BlockSpec((tms, D), lambda i: (i, col))

    def vec(w):
        return pl.BlockSpec((1, w), lambda i: (0, 0))

    anys = pl.BlockSpec(memory_space=pl.ANY)
    f32o, b16o = jax.ShapeDtypeStruct((tp, D), F32), jax.ShapeDtypeStruct((tp, D), BF16)
    res = pl.pallas_call(
        body, name="mix_fwd", grid=(nt,),
        in_specs=[tile(), tile(), tile(3), tile(4), tile(), vec(2 * D), vec(D), vec(D), vec(D), anys, anys]
        + [_whole(a) for a in gat.arrays],
        out_specs=[tile()] * 4 + [anys] * ng,
        out_shape=[f32o, b16o, b16o, b16o] + gat.out_shape,
        scratch_shapes=[pltpu.VMEM((NDEV, 3, D // NDEV, D), BF16), pltpu.VMEM((NDEV, 4, PG // NDEV, PG), BF16),
                        pltpu.SemaphoreType.DMA((2,))] + gat.scratch,
        compiler_params=_params(("arbitrary",), 52),
    )(ac, m, z, z, h0, b_gate, ln_g, ln_b, pool_scale, g_mixw, g_pool, *gat.arrays)
    return res[:4], res[4:]


def _ffn_fwd(h1, tgt, g_ffn, g_final, w_gu, w_dn):
    tp = h1.shape[0]
    nt = tp // TM
    nx_last = tgt.shape[0] - (nt - 1) * TM

    def body(h_ref, t_ref, gf_ref, gl_ref, wgu_hbm, wdn_hbm,
             fg_ref, fu_ref, v_ref, f_ref, dh2_ref, acc_ref, wgu, wdn, v_sc, h2_sc, diff_sc, sems):
        i, j = pl.program_id(0), pl.program_id(1)
        _load_ffn(i, j, wgu_hbm, wgu, wdn_hbm, wdn, sems)

        @pl.when((i == 0) & (j == 0))
        def _():
            acc_ref[...] = jnp.zeros_like(acc_ref)

        @pl.when(j == 0)
        def _():
            h = h_ref[...]
            r = lax.rsqrt(jnp.mean(h * h, axis=-1, keepdims=True) + RMS_EPS)
            v = (h * r * gf_ref[...]).astype(BF16)
            v_sc[...] = v
            v_ref[...] = v
            h2_sc[...] = h

        v = v_sc[...]
        fg = _dot_nt(v, wgu[0, j])
        fu = _dot_nt(v, wgu[1, j])
        fg_ref[...] = fg
        fu_ref[...] = fu
        f = ((fg * _sig(fg)) * fu).astype(BF16)
        f_ref[...] = f
        h2_sc[...] += _dot(f, wdn[j])

        @pl.when(j == 1)
        def _():
            h2 = h2_sc[...]
            r = lax.rsqrt(jnp.mean(h2 * h2, axis=-1, keepdims=True) + RMS_EPS)
            n2 = h2 * r
            y = n2 * gl_ref[...]

            @pl.when(i < nt - 1)
            def _():
                diff_sc[...] = y - t_ref[...]

            @pl.when(i == nt - 1)
            def _():
                diff_sc[pl.ds(0, nx_last), :] = y[:nx_last] - t_ref[pl.ds(0, nx_last), :]
                diff_sc[pl.ds(nx_last, TM - nx_last), :] = jnp.zeros((TM - nx_last, D), F32)

            diff = diff_sc[...]
            dy = diff * (1.0 / D)
            acc_ref[0:1, :] += jnp.sum(diff * diff, axis=0, keepdims=True)
            acc_ref[1:2, :] += jnp.sum(dy * n2, axis=0, keepdims=True)
            dn = dy * gl_ref[...]
            dh2_ref[...] = r * (dn - n2 * jnp.mean(dn * n2, axis=-1, keepdims=True))

    def tile():
        return pl.BlockSpec((TM, D), lambda i, j: (i, 0))

    def chunk():
        return pl.BlockSpec((TM, FFC), lambda i, j: (i, j))

    def vec():
        return pl.BlockSpec((1, D), lambda i, j: (0, 0))

    anys = pl.BlockSpec(memory_space=pl.ANY)
    hid32, hid16 = jax.ShapeDtypeStruct((tp, DFF), F32), jax.ShapeDtypeStruct((tp, DFF), BF16)
    return pl.pallas_call(
        body, name="ffn_fwd", grid=(nt, 2),
        in_specs=[tile(), tile(), vec(), vec(), anys, anys],
        out_specs=[chunk(), chunk(), tile(), chunk(), tile(), pl.BlockSpec((8, D), lambda i, j: (0, 0))],
        out_shape=[hid32, hid32, jax.ShapeDtypeStruct((tp, D), BF16), hid16, jax.ShapeDtypeStruct((tp, D), F32),
                   jax.ShapeDtypeStruct((8, D), F32)],
        scratch_shapes=[pltpu.VMEM((2, 2, FFC, D), BF16), pltpu.VMEM((2, FFC, D), BF16),
                        pltpu.VMEM((TM, D), BF16), pltpu.VMEM((TM, D), F32), pltpu.VMEM((TM, D), F32),
                        pltpu.SemaphoreType.DMA((2 * NDEV + 2,))],
        compiler_params=_params(("arbitrary", "arbitrary"), 56),
    )(h1, tgt, g_ffn, g_final, w_gu, w_dn)


def _ffn_bwd(dh2, fg, fu, h1, g_ffn, w_gu, w_dn):
    tp = h1.shape[0]
    nt = tp // TM

    def body(dh2_ref, fg_ref, fu_ref, h_ref, gf_ref, wgu_hbm, wdn_hbm,
             dfg_ref, dfu_ref, dh1_ref, acc_ref, wgu, wdn, d_sc, dv_sc, sems):
        i, j = pl.program_id(0), pl.program_id(1)
        _load_ffn(i, j, wgu_hbm, wgu, wdn_hbm, wdn, sems)

        @pl.when((i == 0) & (j == 0))
        def _():
            acc_ref[...] = jnp.zeros_like(acc_ref)

        @pl.when(j == 0)
        def _():
            d_sc[...] = dh2_ref[...].astype(BF16)
            dv_sc[...] = jnp.zeros_like(dv_sc)

        df = _dot_nt(d_sc[...], wdn[j])
        fg = fg_ref[...]
        sg = _sig(fg)
        dfu = (df * (fg * sg)).astype(BF16)
        dfg = (df * fu_ref[...] * (sg * (1.0 + fg * (1.0 - sg)))).astype(BF16)
        dfg_ref[...] = dfg
        dfu_ref[...] = dfu
        dv_sc[...] += _dot(dfg, wgu[0, j]) + _dot(dfu, wgu[1, j])

        @pl.when(j == 1)
        def _():
            h = h_ref[...]
            r = lax.rsqrt(jnp.mean(h * h, axis=-1, keepdims=True) + RMS_EPS)
            n1 = h * r
            dv = dv_sc[...]
            acc_ref[0:1, :] += jnp.sum(dv * n1, axis=0, keepdims=True)
            dn = dv * gf_ref[...]
            dh1_ref[...] = dh2_ref[...] + r * (dn - n1 * jnp.mean(dn * n1, axis=-1, keepdims=True))

    def tile():
        return pl.BlockSpec((TM, D), lambda i, j: (i, 0))

    def chunk():
        return pl.BlockSpec((TM, FFC), lambda i, j: (i, j))

    anys = pl.BlockSpec(memory_space=pl.ANY)
    hid16 = jax.ShapeDtypeStruct((tp, DFF), BF16)
    return pl.pallas_call(
        body, name="ffn_bwd", grid=(nt, 2),
        in_specs=[tile(), chunk(), chunk(), tile(), pl.BlockSpec((1, D), lambda i, j: (0, 0)), anys, anys],
        out_specs=[chunk(), chunk(), tile(), pl.BlockSpec((8, D), lambda i, j: (0, 0))],
        out_shape=[hid16, hid16, jax.ShapeDtypeStruct((tp, D), F32), jax.ShapeDtypeStruct((8, D), F32)],
        scratch_shapes=[pltpu.VMEM((2, 2, FFC, D), BF16), pltpu.VMEM((2, FFC, D), BF16),
                        pltpu.VMEM((TM, D), BF16), pltpu.VMEM((TM, D), F32), pltpu.SemaphoreType.DMA((2 * NDEV + 2,))],
        compiler_params=_params(("arbitrary", "arbitrary"), 56),
    )(dh2, fg, fu, h1, g_ffn, w_gu, w_dn)


def _mix_bwd(dh1, z, s, q, ac, m, b_gate, ln_g, ln_b, pool_scale, g_mixw, g_pool, qs):
    tp = dh1.shape[0]
    nt = tp // TMS
    ex = _ChipExchange(qs)
    nq = ex.n

    def body(*refs):
        dh1_ref, zga, zgb, s_ref, q_ref, ac_ref, m_ref, bg_ref, lg_ref, lb_ref, ps_ref, wm_hbm, wp_hbm = refs[:13]
        dac_ref, dm_ref, dzg_ref, dyc_ref, dyp_ref, dm2_ref, acc_ref = refs[13 + nq:20 + nq]
        wm, wp, sems = refs[20 + 2 * nq:23 + 2 * nq]
        ex.bind(refs[13:13 + nq], refs[20 + nq:20 + 2 * nq], refs[23 + 2 * nq:])
        first = pl.program_id(0) == 0

        @pl.when(first)
        def _():
            ex.issue()
            acc_ref[...] = jnp.zeros_like(acc_ref)

        _load_once(first, [(wm_hbm, wm), (wp_hbm, wp)], sems)

        dmerged = _dot_nt(dh1_ref[...].astype(BF16), wm[:, 2].reshape(D, D))
        ga = _sig(zga[...] + bg_ref[:, :D])
        gb = _sig(zgb[...] + bg_ref[:, D:])
        dyc = dmerged * ga
        dyp = dmerged * gb
        dza = (dmerged * _dot(s_ref[...], wm[:, 0].reshape(D, D))) * (ga * (1.0 - ga))
        dzb = (dmerged * _dot(q_ref[...], wm[:, 1].reshape(D, D))) * (gb * (1.0 - gb))
        dzg_ref[:, :D] = dza.astype(BF16)
        dzg_ref[:, D:] = dzb.astype(BF16)
        acc_ref[0:1, :D] += jnp.sum(dza, axis=0, keepdims=True)
        acc_ref[0:1, D:] += jnp.sum(dzb, axis=0, keepdims=True)
        dyc_b = dyc.astype(BF16)
        dyp_b = dyp.astype(BF16)
        dyc_ref[...] = dyc_b
        dyp_ref[...] = dyp_b
        ds = _dot_nt(dyc_b, wm[:, 0].reshape(D, D))
        n, rl = _ln_stats(ac_ref[...])
        l = n * lg_ref[...] + lb_ref[...]
        sg = _sig(l)
        dl = ds * (sg * (1.0 + l * (1.0 - sg)))
        acc_ref[1:2, :D] += jnp.sum(dl * n, axis=0, keepdims=True)
        acc_ref[1:2, D:] += jnp.sum(dl, axis=0, keepdims=True)
        dn = dl * lg_ref[...]
        dac_ref[...] = rl * (dn - jnp.mean(dn, axis=-1, keepdims=True) - n * jnp.mean(dn * n, axis=-1, keepdims=True))
        dq = _dot_nt(dyp_b, wm[:, 1].reshape(D, D))
        mv = m_ref[...]
        acc_ref[2:3, :D] += jnp.sum(dq * _pool_mix(mv, wp), axis=0, keepdims=True)
        dm2 = (dq * ps_ref[...]).astype(BF16)
        dm2_ref[...] = dm2
        dm_ref[...] = jnp.concatenate(
            [_dot_nt(dm2[:, g * PG:(g + 1) * PG], wp[:, g].reshape(PG, PG)) for g in range(4)], axis=1)

        @pl.when(pl.program_id(0) == nt - 1)
        def _():
            ex.finish()

    def tile(col=0):
        return pl.BlockSpec((TMS, D), lambda i: (i, col))

    def vec(w):
        return pl.BlockSpec((1, w), lambda i: (0, 0))

    anys = pl.BlockSpec(memory_space=pl.ANY)
    f32o, b16o = jax.ShapeDtypeStruct((tp, D), F32), jax.ShapeDtypeStruct((tp, D), BF16)
    res = pl.pallas_call(
        body, name="mix_bwd", grid=(nt,),
        in_specs=[tile(), tile(3), tile(4), tile(), tile(), tile(), tile(), vec(2 * D), vec(D), vec(D), vec(D), anys, anys]
        + [anys] * nq,
        out_specs=[tile(), tile(), pl.BlockSpec((TMS, 2 * D), lambda i: (i, 0)), tile(), tile(), tile(),
                   pl.BlockSpec((8, 2 * D), lambda i: (0, 0))] + [anys] * nq,
        out_shape=[f32o, f32o, jax.ShapeDtypeStruct((tp, 2 * D), BF16), b16o, b16o, b16o,
                   jax.ShapeDtypeStruct((8, 2 * D), F32)] + ex.out_shape,
        scratch_shapes=[pltpu.VMEM((NDEV, 3, D // NDEV, D), BF16), pltpu.VMEM((NDEV, 4, PG // NDEV, PG), BF16),
                        pltpu.SemaphoreType.DMA((2,))] + ex.scratch,
        compiler_params=_params(("arbitrary",), 48),
    )(dh1, z, z, s, q, ac, m, b_gate, ln_g, ln_b, pool_scale, g_mixw, g_pool, *qs)
    return res[:7], res[7:]


def _seq_bwd(dac, dm, dzg, z, w_dw, seq, qs):
    tp = z.shape[0]
    nt = tp // TM
    ex = _ChipExchange(qs)
    nq = no = ex.n

    def body(*refs):
        dac_l, dac_c, dac_r, dm_l, dm_c, dm_r, av_l, av, av_r, ag_l, ag, ag_r, dzg_ref, w_ref = refs[:14]
        dz_ref, acc_ref = refs[14 + nq:16 + nq]
        a3, d3, m3, da3, dp3, w3, dw3, da_sc, dp_sc = refs[16 + nq + no:25 + nq + no]
        ex.bind(refs[14:14 + nq], refs[16 + nq:16 + nq + no], refs[25 + nq + no:])
        i = pl.program_id(0)
        sub = lax.broadcasted_iota(jnp.int32, (NCB, 128), 0)

        @pl.when(i == 0)
        def _():
            ex.issue()
            dw3[...] = jnp.zeros_like(dw3)
            _tm_fill(w3, 0, 4, lambda r, l: w_ref[pl.ds(r, 8), l])

        _tm_fill_ext(a3, (av_l, ag_l), (av, ag), (av_r, ag_r), lambda vg, r, l: vg[0][r, l] * _sig(vg[1][r, l]), unroll=2)
        _tm_fill_ext(d3, dac_l, dac_c, dac_r, lambda ref, r, l: ref[r, l])
        _tm_fill_ext(m3, dm_l, dm_c, dm_r, lambda ref, r, l: ref[r, l])

        def conv(g, c):
            dcur = [_tm_at(d3, 8 * g + t + HALO) for t in range(8)]
            accs = [None] * 8
            for k in range(CONV_K):
                wk = _tm_at(w3, k)
                prs = []
                for t in range(8):
                    term = wk * _tm_at(d3, 8 * g + t + CONV_K - k)
                    accs[t] = term if accs[t] is None else accs[t] + term
                    prs.append(dcur[t] * _tm_at(a3, 8 * g + t + k + 1))
                while len(prs) > 1:
                    prs = [prs[j] + prs[j + 1] for j in range(0, len(prs), 2)]
                dw3[_tm_rows(k), :] += prs[0]
            s = dcur[0]
            for t in range(1, 8):
                s = s + dcur[t]
            dw3[_tm_rows(CONV_K), :] += s
            for t in range(8):
                da3[_tm_rows(8 * g + t), :] = accs[t]
            return c

        lax.fori_loop(0, TM // 8, conv, 0)

        for b in _edge_rows(seq, tp):
            e = lax.rem(b - i * TM + HALO + tp, tp)

            @pl.when(e < TME)
            def _():
                m3[_tm_rows(e), :] = _tm_at(m3, e) * _edge_gain(b, seq, tp, sub)

        inv = _by_group(sub, [1.0 / w for w in POOL_WINDOWS])

        def pool(g, c):
            for t in range(8):
                e = 8 * g + t + HALO
                sums = _nested_windows(lambda o: _tm_at(m3, e + o), [w // 2 + 1 - w for w in POOL_WINDOWS])
                dp3[_tm_rows(8 * g + t), :] = _by_group(sub, sums) * inv
            return c

        lax.fori_loop(0, TM // 8, pool, 0)

        _tm_read(da3, TM // 8, lambda r, l, tile: da_sc.__setitem__((r, l), tile))
        _tm_read(dp3, TM // 8, lambda r, l, tile: dp_sc.__setitem__((r, l), tile))
        sg = _sig(ag[...])
        da = da_sc[...]
        dz_ref[:, 0:D] = (da * sg).astype(BF16)
        dz_ref[:, D:2 * D] = (da * av[...] * (sg * (1.0 - sg))).astype(BF16)
        dz_ref[:, 2 * D:3 * D] = (dp_sc[...] - dm_c[...]).astype(BF16)
        dz_ref[:, 3 * D:] = dzg_ref[...]

        @pl.when(i == nt - 1)
        def _():
            _tm_read(dw3, 4, lambda r, l, tile: acc_ref.__setitem__((r, l), tile))
            ex.finish()

    tmaj = pltpu.VMEM((TM * NCB, 128), F32)
    text = pltpu.VMEM((TME * NCB, 128), F32)
    taps = pltpu.VMEM((32 * NCB, 128), F32)
    anys = pl.BlockSpec(memory_space=pl.ANY)
    res = pl.pallas_call(
        body, name="seq_bwd", grid=(nt,),
        in_specs=_halo_specs(0, nt) + _halo_specs(0, nt) + _halo_specs(0, nt) + _halo_specs(1, nt)
        + [pl.BlockSpec((TM, 2 * D), lambda i: (i, 0)), pl.BlockSpec((32, D), lambda i: (0, 0))] + [anys] * nq,
        out_specs=[pl.BlockSpec((TM, DIN), lambda i: (i, 0)), pl.BlockSpec((32, D), lambda i: (0, 0))] + [anys] * no,
        out_shape=[jax.ShapeDtypeStruct((tp, DIN), BF16), jax.ShapeDtypeStruct((32, D), F32)] + ex.out_shape,
        scratch_shapes=[text, text, text, tmaj, tmaj, taps, taps, pltpu.VMEM((TM, D), F32), pltpu.VMEM((TM, D), F32)]
        + ex.scratch,
        compiler_params=_params(("arbitrary",), 48),
    )(dac, dac, dac, dm, dm, dm, z, z, z, z, z, z, dzg, w_dw, *qs)
    return res[:2], res[2:]


def _in_bwd(dz, h0, dh1, g_mix, w_g, seq, qs):
    tp = h0.shape[0]
    tm = _pick(tp, TM_IO)
    nt = tp // tm
    ex = _ChipExchange(qs)
    nq = no = ex.n

    def body(*refs):
        dz_ref, h_ref, dh1_ref, g_ref, w_hbm = refs[:5]
        gx_ref, gmeta_ref, acc_ref = refs[5 + nq:8 + nq]
        w_vm, sems = refs[8 + nq + no:10 + nq + no]
        ex.bind(refs[5:5 + nq], refs[8 + nq:8 + nq + no], refs[10 + nq + no:])
        i = pl.program_id(0)

        @pl.when(i == 0)
        def _():
            ex.issue()
            acc_ref[...] = jnp.zeros_like(acc_ref)

        _load_once(i == 0, _win_pairs(w_hbm, w_vm), sems)

        du = _dot_nt(dz_ref[:, :DIN // 2], w_vm[0]) + _dot_nt(dz_ref[:, DIN // 2:], w_vm[1])
        h = h_ref[...]
        r = lax.rsqrt(jnp.mean(h * h, axis=-1, keepdims=True) + RMS_EPS)
        n0 = h * r
        acc_ref[0:1, :] += jnp.sum(du * n0, axis=0, keepdims=True)
        dn = du * g_ref[...]
        gx_ref[...] = dh1_ref[...] + r * (dn - n0 * jnp.mean(dn * n0, axis=-1, keepdims=True))

        @pl.when(i == nt - 1)
        def _():
            gmeta_ref[...] = gx_ref[pl.ds(tm - N_META, N_META), :]
            ex.finish()

    tile = pl.BlockSpec((tm, D), lambda i: (i, 0))
    anys = pl.BlockSpec(memory_space=pl.ANY)
    res = pl.pallas_call(
        body, name="in_bwd", grid=(nt,),
        in_specs=[pl.BlockSpec((tm, DIN), lambda i: (i, 0)), tile, tile, pl.BlockSpec((1, D), lambda i: (0, 0)), anys]
        + [anys] * nq,
        out_specs=[tile, pl.BlockSpec((N_META, D), lambda i: (0, 0)), pl.BlockSpec((8, D), lambda i: (0, 0))] + [anys] * no,
        out_shape=[jax.ShapeDtypeStruct((seq, D), F32), jax.ShapeDtypeStruct((N_META, D), F32),
                   jax.ShapeDtypeStruct((8, D), F32)] + ex.out_shape,
        scratch_shapes=[pltpu.VMEM((2, D, DIN // 2), BF16), pltpu.SemaphoreType.DMA((NDEV,))] + ex.scratch,
        compiler_params=_params(("arbitrary",), 58),
    )(dz, h0, dh1, g_mix, w_g, *qs)
    return res[:3], res[3:]


def _wgrad_in(u, dz, qs):
    tp = u.shape[0]
    tm = _pick(tp, TM_WG)
    nt = tp // tm
    half = DIN // 2
    ex = _ChipExchange(qs)
    nq = ex.n

    def body(*refs):
        u_ref, dz_ref = refs[:2]
        o_ref, acc = refs[2 + nq], refs[3 + 2 * nq]
        ex.bind(refs[2:2 + nq], refs[3 + nq:3 + 2 * nq], refs[4 + 2 * nq:])
        h, t = pl.program_id(0), pl.program_id(1)

        @pl.when((h == 0) & (t == 0))
        def _():
            ex.issue()

        @pl.when(t == 0)
        def _():
            acc[...] = jnp.zeros_like(acc)

        acc[...] += _dot_tn(u_ref[...], dz_ref[...])

        @pl.when(t == nt - 1)
        def _():
            for d in range(4):
                o_ref[d] = acc[:, INB * d:INB * (d + 1)].astype(BF16)

        @pl.when((h == 1) & (t == nt - 1))
        def _():
            ex.finish()

    anys = pl.BlockSpec(memory_space=pl.ANY)
    res = pl.pallas_call(
        body, name="wgrad_in", grid=(2, nt),
        in_specs=[pl.BlockSpec((tm, D), lambda h, t: (t, 0)), pl.BlockSpec((tm, half), lambda h, t: (t, h))] + [anys] * nq,
        out_specs=[pl.BlockSpec((4, D, INB), lambda h, t: (h, 0, 0), pipeline_mode=pl.Buffered(1))] + [anys] * nq,
        out_shape=[jax.ShapeDtypeStruct((NDEV, D, INB), BF16)] + ex.out_shape,
        scratch_shapes=[pltpu.VMEM((D, half), F32)] + ex.scratch,
        compiler_params=_params(("arbitrary", "arbitrary"), 52),
    )(u, dz, *qs)
    return res[0], res[1:]


def _wgrad_mix(s, dyc, q, dyp, merged, dh1, m, dm2):
    tp = s.shape[0]
    tm = _pick(tp, TM_WM)
    nt = tp // tm
    rb = D // NDEV

    def body(s_ref, dyc_ref, q_ref, dyp_ref, mg_ref, dh1_ref, m_ref, dm2_ref, o_ref, op_ref, acc, accp):
        t = pl.program_id(0)

        @pl.when(t == 0)
        def _():
            acc[...] = jnp.zeros_like(acc)
            accp[...] = jnp.zeros_like(accp)

        acc[0] += _dot_tn(s_ref[...], dyc_ref[...])
        acc[1] += _dot_tn(q_ref[...], dyp_ref[...])
        acc[2] += _dot_tn(mg_ref[...], dh1_ref[...].astype(BF16))
        for g in range(4):
            accp[g] += _dot_tn(m_ref[:, g * PG:(g + 1) * PG], dm2_ref[:, g * PG:(g + 1) * PG])

        @pl.when(t == nt - 1)
        def _():
            for d in range(NDEV):
                for k in range(3):
                    o_ref[d, k] = acc[k, rb * d:rb * (d + 1), :].astype(BF16)
                for g in range(4):
                    op_ref[d, g] = accp[g, 32 * d:32 * (d + 1), :].astype(BF16)

    tile = pl.BlockSpec((tm, D), lambda t: (t, 0))
    return pl.pallas_call(
        body, name="wgrad_mix", grid=(nt,),
        in_specs=[tile] * 8,
        out_specs=[pl.BlockSpec((NDEV, 3, rb, D), lambda t: (0, 0, 0, 0), pipeline_mode=pl.Buffered(1)),
                   pl.BlockSpec((NDEV, 4, 32, PG), lambda t: (0, 0, 0, 0), pipeline_mode=pl.Buffered(1))],
        out_shape=[jax.ShapeDtypeStruct((NDEV, 3, rb, D), BF16), jax.ShapeDtypeStruct((NDEV, 4, 32, PG), BF16)],
        scratch_shapes=[pltpu.VMEM((3, D, D), F32), pltpu.VMEM((4, PG, PG), F32)],
        compiler_params=_params(("arbitrary",), 56),
    )(s, dyc, q, dyp, merged, dh1, m, dm2)


def _wgrad_gu(v, dfg, dfu):
    tp = v.shape[0]
    tm = _pick(tp, TM_WG)
    nt = tp // tm

    def body(v_ref, dg_ref, du_ref, o_ref, acc):
        k, t = pl.program_id(0), pl.program_id(2)

        @pl.when(t == 0)
        def _():
            acc[...] = jnp.zeros_like(acc)

        @pl.when(k == 0)
        def _():
            acc[...] += _dot_tn(dg_ref[...], v_ref[...])

        @pl.when(k == 1)
        def _():
            acc[...] += _dot_tn(du_ref[...], v_ref[...])

        @pl.when(t == nt - 1)
        def _():
            for d in range(4):
                o_ref[d] = acc[FFB * d:FFB * (d + 1), :].astype(BF16)

    return pl.pallas_call(
        body, name="wgrad_gu", grid=(2, 2, nt),
        in_specs=[pl.BlockSpec((tm, D), lambda k, h, t: (t, 0)),
                  pl.BlockSpec((tm, FFC), lambda k, h, t: (t * (1 - k), h * (1 - k))),
                  pl.BlockSpec((tm, FFC), lambda k, h, t: (t * k, h * k))],
        out_specs=pl.BlockSpec((4, None, FFB, D), lambda k, h, t: (h, k, 0, 0), pipeline_mode=pl.Buffered(1)),
        out_shape=jax.ShapeDtypeStruct((NDEV, 2, FFB, D), BF16),
        scratch_shapes=[pltpu.VMEM((FFC, D), F32)],
        compiler_params=_params(("arbitrary",) * 3, 48),
    )(v, dfg, dfu)


def _wgrad_down(f, dh2):
    tp = f.shape[0]
    tm = _pick(tp, TM_WG)
    nt = tp // tm

    def body(f_ref, d_ref, o_ref, acc):
        t = pl.program_id(1)

        @pl.when(t == 0)
        def _():
            acc[...] = jnp.zeros_like(acc)

        acc[...] += _dot_tn(f_ref[...], d_ref[...].astype(BF16))

        @pl.when(t == nt - 1)
        def _():
            for d in range(4):
                o_ref[d] = acc[FFB * d:FFB * (d + 1), :].astype(BF16)

    return pl.pallas_call(
        body, name="wgrad_down", grid=(2, nt),
        in_specs=[pl.BlockSpec((tm, FFC), lambda h, t: (t, h)), pl.BlockSpec((tm, D), lambda h, t: (t, 0))],
        out_specs=pl.BlockSpec((4, FFB, D), lambda h, t: (h, 0, 0), pipeline_mode=pl.Buffered(1)),
        out_shape=jax.ShapeDtypeStruct((NDEV, FFB, D), BF16),
        scratch_shapes=[pltpu.VMEM((FFC, D), F32)],
        compiler_params=_params(("arbitrary", "arbitrary"), 48),
    )(f, dh2)


def kernel(x, meta_tokens, g_mix, w_in, b_gate, w_dw, b_dw, ln_g, ln_b, w_conv_out, w_pool, pool_scale, w_pool_out, w_o, g_ffn, w_ffn_gate, w_ffn_up, w_ffn_down, g_final, loss_target, m_meta_tokens, m_g_mix, m_w_in, m_b_gate, m_w_dw, m_b_dw, m_ln_g, m_ln_b, m_w_conv_out, m_w_pool, m_pool_scale, m_w_pool_out, m_w_o, m_g_ffn, m_w_ffn_gate, m_w_ffn_up, m_w_ffn_down, m_g_final, v_meta_tokens, v_g_mix, v_w_in, v_b_gate, v_w_dw, v_b_dw, v_ln_g, v_ln_b, v_w_conv_out, v_w_pool, v_pool_scale, v_w_pool_out, v_w_o, v_g_ffn, v_w_ffn_gate, v_w_ffn_up, v_w_ffn_down, v_g_final):
    seq = x.shape[1]
    tp = -(-(seq + 2 * HALO) // TM) * TM
    tm_in = _pick(tp, TM_IO)
    nx_last = seq - (tp // tm_in - 1) * tm_in
    assert 0 < nx_last <= tm_in - 2 * HALO and nx_last % 8 == 0 and 0 < seq - (tp // TM - 1) * TM

    whole = (Ellipsis,)
    ag_small = _Gather(
        [((48, D // NDEV), [(meta_tokens, pl.ds(0, N_META), whole), (w_dw, pl.ds(N_META, CONV_K), 0)])], [F32])
    ag_mix = _Gather([((3, D // NDEV, D), [(w_conv_out, 0, 0), (w_pool_out, 1, 0), (w_o, 2, 0)]),
                      ((4, PG // NDEV, PG), [(w_pool, whole, 0)])], [BF16, BF16])
    def tr(a):
        return jnp.swapaxes(a, 1, 2)

    ag_gu = _Gather([((2, FFB, D), [(tr(w_ffn_gate), 0, 0), (tr(w_ffn_up), 1, 0)])], [BF16])
    ag_dn = _Gather([((FFB, D), [(w_ffn_down, whole, 0)])], [BF16])

    mx, my = lax.axis_index("x"), lax.axis_index("y")
    order = jnp.stack([2 * mx + my, 2 * mx + 1 - my, 2 * (1 - mx) + my, 2 * (1 - mx) + 1 - my]).astype(jnp.int32)
    (h0, z, u, g_in), (g_mixw, g_pool), g_small = _fwd_in(x[0], g_mix, w_in, order, tp, ag_mix, ag_small)
    wdw_full = g_small.transpose(1, 0, 2).reshape(48, D)[N_META:]
    (ac, m), (w_gu,) = _seq_fwd(z, wdw_full, b_dw, seq, ag_gu)
    (h1, s, merged, q), (g_down,) = _mix_fwd(ac, m, z, h0, b_gate, ln_g, ln_b, pool_scale, g_mixw, g_pool, ag_dn)
    w_dn = g_down.reshape(2, FFC, D)
    fg, fu, v, f, dh2, head_acc = _ffn_fwd(h1, loss_target[0], g_ffn, g_final.reshape(1, D), w_gu, w_dn)

    dfg, dfu, dh1, ffn_acc = _ffn_bwd(dh2, fg, fu, h1, g_ffn, w_gu, w_dn)
    own_f, sib_f, q_f = _rs_pair("rs_pair_ffn", [_wgrad_gu(v, dfg, dfu), _wgrad_down(f, dh2)])
    (dac, dm, dzg, dyc, dyp, dm2, mix_acc), rel_dn = _mix_bwd(
        dh1, z, s, q, ac, m, b_gate, ln_g, ln_b, pool_scale, g_mixw, g_pool, q_f[1:])
    p_mix = _wgrad_mix(s, dyc, q, dyp, merged, dh1, m, dm2)
    own_m, sib_m, q_m = _rs_pair("rs_pair_mix", list(p_mix))
    (dz, seq_acc), rel_gu = _seq_bwd(dac, dm, dzg, z, wdw_full, seq, q_f[:1])
    rel_f = [rel_gu[0], rel_dn[0]]
    p_in, rel_m = _wgrad_in(u, dz, q_m)
    own_i, sib_i, q_i = _rs_pair("rs_pair_in", [p_in])
    (grad_x, g_meta, in_acc), rel_i = _in_bwd(dz, h0, dh1, g_mix, g_in, seq, q_i)
    small_g = jnp.concatenate([g_meta, seq_acc[:CONV_K], jnp.zeros((1, D), F32)], axis=0)
    p_small = small_g.reshape(48, NDEV, D // NDEV).transpose(1, 0, 2).astype(BF16)
    rep_g = jnp.concatenate([
        in_acc[0:1], mix_acc[0:1, :D], mix_acc[0:1, D:], seq_acc[CONV_K:CONV_K + 1], mix_acc[1:2, :D], mix_acc[1:2, D:],
        mix_acc[2:3, :D], ffn_acc[0:1], head_acc[1:2], head_acc[0:1], jnp.zeros((REP_ROWS - 10, D), F32)], axis=0)
    own_s, sib_s, rel_s, rep_all = _reduce_scatter([p_small], rep_g)
    owns = [own_i[0], own_s[0], own_m[0], own_m[1], own_f[0], own_f[1]]
    sibs = [sib_i[0], sib_s[0], sib_m[0], sib_m[1], sib_f[0], sib_f[1]]
    rels = [rel_i[0], rel_s[0], rel_m[0], rel_m[1], rel_f[0], rel_f[1]]

    def lead(a):
        return a.reshape(1, *a.shape)

    def stack4(a, lead_dims):
        return a.reshape(*lead_dims, 1, 4 * 32, PG)

    (r_in,) = _adamw_multi("adamw_in", lead(owns[0]), sibs[0][:, None], rels[0][:, None], [w_in], [m_w_in], [v_w_in], 4)
    r_meta, r_dw = _adamw_meta_dw(owns[1], sibs[1], rels[1], (meta_tokens, m_meta_tokens, v_meta_tokens),
                                  (w_dw, m_w_dw, v_w_dw))
    r_conv, r_pout, r_o = _adamw_multi("adamw_mix", owns[2], sibs[2], rels[2], [w_conv_out, w_pool_out, w_o],
                                       [m_w_conv_out, m_w_pool_out, m_w_o], [v_w_conv_out, v_w_pool_out, v_w_o], 1)
    (r_pool,) = _adamw_multi("adamw_pool", stack4(owns[3], ()), stack4(sibs[3], (1,)), stack4(rels[3], (3,)),
                             [w_pool.reshape(1, 128, PG)], [m_w_pool.reshape(1, 128, PG)], [v_w_pool.reshape(1, 128, PG)], 1)
    r_pool = tuple(a.reshape(w_pool.shape) for a in r_pool)
    r_gate, r_up = _adamw_multi("adamw_gu", owns[4], sibs[4], rels[4], [tr(w_ffn_gate), tr(w_ffn_up)],
                                [tr(m_w_ffn_gate), tr(m_w_ffn_up)], [tr(v_w_ffn_gate), tr(v_w_ffn_up)], 2)
    r_gate, r_up = tuple(tr(a) for a in r_gate), tuple(tr(a) for a in r_up)
    (r_down,) = _adamw_multi("adamw_down", lead(owns[5]), sibs[5][:, None], rels[5][:, None],
                             [w_ffn_down], [m_w_ffn_down], [v_w_ffn_down], 2)
    row = (1, D)
    loss, reps = _adamw_rep(
        rep_all,
        [g_mix, b_gate, b_dw, ln_g, ln_b, pool_scale, g_ffn, g_final.reshape(row)],
        [m_g_mix, m_b_gate, m_b_dw, m_ln_g, m_ln_b, m_pool_scale, m_g_ffn, m_g_final.reshape(row)],
        [v_g_mix, v_b_gate, v_b_dw, v_ln_g, v_ln_b, v_pool_scale, v_g_ffn, v_g_final.reshape(row)])
    r_gmix, r_bg, r_bdw, r_lg, r_lb, r_ps, r_gffn, r_gfin = reps
    r_gfin = tuple(a.reshape(D) for a in r_gfin)

    in_order = [r_meta, r_gmix, r_in, r_bg, r_dw, r_bdw, r_lg, r_lb, r_conv, r_pool, r_ps, r_pout, r_o, r_gffn,
                r_gate, r_up, r_down, r_gfin]
    return (loss.reshape(()), grad_x[None], *[r[0] for r in in_order], *[r[1] for r in in_order],
            *[r[2] for r in in_order], *[r[3] for r in in_order])
```

```python
import math

import jax
import jax.numpy as jnp
from jax import lax
from jax.experimental import pallas as pl
from jax.experimental.pallas import tpu as pltpu

F32, BF16 = jnp.float32, jnp.bfloat16
MESH_ID = pl.DeviceIdType.MESH
NDEV = 8

D = 1024
N_META = 16
CONV_K = 31
HALO = 16
POOL_WINDOWS = (2, 4, 8, 16)
PG = 256
DIN = 5 * D
DFF = 2816
FFB = DFF // NDEV
FFC = DFF // 2
INB = DIN // NDEV
RMS_EPS = 1e-6
LN_EPS = 1e-5
ADAM_LR, ADAM_B1, ADAM_B2, ADAM_EPS, ADAM_WD, ADAM_STEP = 0.001, 0.9, 0.999, 1e-08, 0.01, 10

TM = 384
TMS = 384
TM_IO = 704
TM_WG = 1408
TM_WM = 704
MIB = 2 ** 20


def _sig(x):
    return 0.5 * jnp.tanh(0.5 * x) + 0.5


def _dot(a, b):
    return jnp.dot(a, b, preferred_element_type=F32)


def _dot_nt(a, b):
    return lax.dot_general(a, b, (((1,), (1,)), ((), ())), preferred_element_type=F32)


def _dot_tn(a, b):
    return lax.dot_general(a, b, (((0,), (0,)), ((), ())), preferred_element_type=F32)


def _pick(tp, pref):
    return pref if tp % pref == 0 else TM


def _params(sem, vmem_mib):
    return pltpu.CompilerParams(dimension_semantics=sem, vmem_limit_bytes=vmem_mib * MIB)


def _load_once(first, pairs, sems):
    @pl.when(first)
    def _():
        cps = [pltpu.make_async_copy(s, d, sems.at[k]) for k, (s, d) in enumerate(pairs)]
        for cp in cps:
            cp.start()
        for cp in cps:
            cp.wait()


def _place():
    x, y, c = lax.axis_index("x"), lax.axis_index("y"), lax.axis_index("c")
    return x, y, c


class _Gather:
    def __init__(self, groups, dtypes):
        self.groups, self.dtypes, self.n = groups, dtypes, len(groups)
        self.arrays = [a for _, parts in groups for a, _, _ in parts]
        self.out_shape = [jax.ShapeDtypeStruct((NDEV, *s), dt) for (s, _), dt in zip(groups, dtypes)]
        self.scratch = [pltpu.VMEM(s, dt) for (s, _), dt in zip(groups, dtypes)] + [
            pltpu.SemaphoreType.DMA((7 * self.n,)), pltpu.SemaphoreType.DMA((7 * self.n,)),
            pltpu.SemaphoreType.DMA((self.n,))]

    def bind(self, ins, outs, scratch):
        self.ins, self.outs, self.stages = ins, outs, scratch[:self.n]
        self.send_sems, self.recv_sems, self.local_sems = scratch[self.n:]
        return self

    def _copy(self, w, k, block, to, src=None):
        dst = self.outs[w].at[4 * block[0] + 2 * block[1] + block[2]]
        return pltpu.make_async_remote_copy(
            src_ref=dst if src is None else src, dst_ref=dst,
            send_sem=self.send_sems.at[7 * w + k], recv_sem=self.recv_sems.at[7 * w + k],
            device_id=to, device_id_type=MESH_ID)

    def _first(self):
        x, y, c = _place()
        me, sibling = (x, y, c), (x, y, 1 - c)
        chips = [(1 - x, y), (x, 1 - y), (1 - x, 1 - y)]
        mine, first = [], []
        for w in range(self.n):
            mine.append(pltpu.make_async_copy(self.stages[w], self.outs[w].at[4 * x + 2 * y + c], self.local_sems.at[w]))
            first.append(self._copy(w, 0, me, sibling, src=self.stages[w]))
            first += [self._copy(w, 1 + j, me, (*chip, c), src=self.stages[w]) for j, chip in enumerate(chips)]
        return mine, first

    def _passed(self):
        x, y, c = _place()
        chips = [(1 - x, y), (x, 1 - y), (1 - x, 1 - y)]
        return [self._copy(w, 4 + j, (*chip, c), (x, y, 1 - c)) for w in range(self.n) for j, chip in enumerate(chips)]

    def issue(self):
        a = 0
        for w in range(self.n):
            shape, parts = self.groups[w]
            if sum(arr.size for arr, _, _ in parts) < math.prod(shape):
                self.stages[w][...] = jnp.zeros(shape, self.dtypes[w])
            for _, dst, src in parts:
                self.stages[w][dst] = self.ins[a][src].astype(self.dtypes[w])
                a += 1
        mine, first = self._first()
        for cp in mine + first:
            cp.start()

    def forward(self):
        x, y, c = _place()
        chips = [(1 - x, y), (x, 1 - y), (1 - x, 1 - y)]
        passed = self._passed()
        for w in range(self.n):
            for j, chip in enumerate(chips):
                self._copy(w, 1 + j, (*chip, c), (x, y, c)).wait_recv()
                passed[3 * w + j].start()

    def finish(self):
        x, y, c = _place()
        chips = [(1 - x, y), (x, 1 - y), (1 - x, 1 - y)]
        for w in range(self.n):
            self._copy(w, 0, (x, y, 1 - c), (x, y, c)).wait_recv()
            for j, chip in enumerate(chips):
                self._copy(w, 4 + j, (*chip, 1 - c), (x, y, c)).wait_recv()
        mine, first = self._first()
        for cp in first + self._passed():
            cp.wait_send()
        for cp in mine:
            cp.wait()


class _ChipExchange:
    def __init__(self, qs):
        self.n = len(qs)
        self.out_shape = [jax.ShapeDtypeStruct(q.shape, q.dtype) for q in qs]
        self.scratch = [pltpu.SemaphoreType.DMA((3 * self.n,)), pltpu.SemaphoreType.DMA((3 * self.n,))]

    def bind(self, qs, rels, scratch):
        self.qs, self.rels = qs, rels
        self.send_sems, self.recv_sems = scratch
        return self

    def _copies(self):
        x, y, c = _place()
        chips = [(1 - x, y), (x, 1 - y), (1 - x, 1 - y)]
        return [pltpu.make_async_remote_copy(
            src_ref=self.qs[w].at[j], dst_ref=self.rels[w].at[j],
            send_sem=self.send_sems.at[3 * w + j], recv_sem=self.recv_sems.at[3 * w + j],
            device_id=(*chips[j], c), device_id_type=MESH_ID) for w in range(self.n) for j in range(3)]

    def issue(self):
        for cp in self._copies():
            cp.start()

    def finish(self):
        cps = self._copies()
        for cp in cps:
            cp.wait_recv()
        for cp in cps:
            cp.wait_send()


def _reduce_scatter(parts, small):
    n = len(parts)
    blks = [p.shape[1:] for p in parts]

    def body(*refs):
        ps, small_ref = refs[:n], refs[n]
        o = n + 1
        owns, sibs, rels, small_out = refs[o:o + n], refs[o + n:o + 2 * n], refs[o + 2 * n:o + 3 * n], refs[o + 3 * n]
        o += 3 * n + 1
        pa, pb, qst = refs[o:o + n], refs[o + n:o + 2 * n], refs[o + 2 * n:o + 3 * n]
        s1_send, s1_recv, s2_send, s2_recv, sm_send, sm_recv, lsem = refs[o + 3 * n:]
        x, y, c = _place()
        me = 4 * x + 2 * y + c
        sibling = (x, y, 1 - c)
        chips = [(1 - x, y), (x, 1 - y), (1 - x, 1 - y)]
        all_chips = [(x, y)] + chips

        own_cps = []
        for w in range(n):
            cp = pltpu.make_async_copy(ps[w].at[me], owns[w], lsem.at[w])
            cp.start()
            own_cps.append(cp)
        sm_own = pltpu.make_async_copy(small_ref, small_out.at[me], lsem.at[n])
        sm_own.start()

        def small_copy(r):
            peer = ((x + (r >> 2)) % 2, (y + ((r >> 1) & 1)) % 2, (c + (r & 1)) % 2)
            return pltpu.make_async_remote_copy(
                src_ref=small_ref, dst_ref=small_out.at[me], send_sem=sm_send.at[r - 1], recv_sem=sm_recv.at[r - 1],
                device_id=peer, device_id_type=MESH_ID)

        sm_cps = [small_copy(r) for r in range(1, NDEV)]
        for cp in sm_cps:
            cp.start()

        def pair_copy(w, rel):
            cx, cy = all_chips[rel]
            return pltpu.make_async_remote_copy(
                src_ref=ps[w].at[4 * cx + 2 * cy + (1 - c)], dst_ref=sibs[w].at[rel],
                send_sem=s1_send.at[4 * w + rel], recv_sem=s1_recv.at[4 * w + rel],
                device_id=sibling, device_id_type=MESH_ID)

        def chip_copy(w, j):
            return pltpu.make_async_remote_copy(
                src_ref=qst[w].at[j], dst_ref=rels[w].at[j],
                send_sem=s2_send.at[3 * w + j], recv_sem=s2_recv.at[3 * w + j],
                device_id=(*chips[j], c), device_id_type=MESH_ID)

        pair_cps = [pair_copy(w, rel) for w in range(n) for rel in (1, 2, 3, 0)]
        for cp in pair_cps:
            cp.start()
        chip_cps = []
        for w in range(n):
            for j, (cx, cy) in enumerate(chips):
                pair_copy(w, 1 + j).wait_recv()
                la = pltpu.make_async_copy(ps[w].at[4 * cx + 2 * cy + c], pa[w], lsem.at[n + 1])
                lb = pltpu.make_async_copy(sibs[w].at[1 + j], pb[w], lsem.at[n + 2])
                la.start()
                lb.start()
                la.wait()
                lb.wait()
                qst[w][j] = (pa[w][...].astype(F32) + pb[w][...].astype(F32)).astype(BF16)
                cp = chip_copy(w, j)
                cp.start()
                chip_cps.append(cp)
        for w in range(n):
            pair_copy(w, 0).wait_recv()
            for j in range(3):
                chip_copy(w, j).wait_recv()
        for cp in sm_cps:
            cp.wait_recv()
        for cp in pair_cps + chip_cps + sm_cps:
            cp.wait_send()
        for cp in own_cps:
            cp.wait()
        sm_own.wait()

    any_spec = pl.BlockSpec(memory_space=pl.ANY)
    outs = pl.pallas_call(
        body, name="rs_grads",
        out_shape=[jax.ShapeDtypeStruct(b, BF16) for b in blks]
        + [jax.ShapeDtypeStruct((4, *b), BF16) for b in blks]
        + [jax.ShapeDtypeStruct((3, *b), BF16) for b in blks]
        + [jax.ShapeDtypeStruct((NDEV, *small.shape), F32)],
        in_specs=[any_spec] * (n + 1),
        out_specs=[any_spec] * (3 * n + 1),
        scratch_shapes=[pltpu.VMEM(b, BF16) for b in blks] + [pltpu.VMEM(b, BF16) for b in blks]
        + [pltpu.VMEM((3, *b), BF16) for b in blks]
        + [pltpu.SemaphoreType.DMA((4 * n,)), pltpu.SemaphoreType.DMA((4 * n,)),
           pltpu.SemaphoreType.DMA((3 * n,)), pltpu.SemaphoreType.DMA((3 * n,)),
           pltpu.SemaphoreType.DMA((NDEV - 1,)), pltpu.SemaphoreType.DMA((NDEV - 1,)),
           pltpu.SemaphoreType.DMA((n + 3,))],
        compiler_params=pltpu.CompilerParams(vmem_limit_bytes=40 * MIB),
    )(*parts, small)
    return outs[:n], outs[n:2 * n], outs[2 * n:3 * n], outs[3 * n]


class _PairSum:
    def __init__(self, parts, keep_q=True):
        self.n = n = len(parts)
        self.keep_q = keep_q
        blks = [p.shape[1:] for p in parts]
        self.out_shape = [jax.ShapeDtypeStruct(b, BF16) for b in blks] + [jax.ShapeDtypeStruct((1, *b), BF16) for b in blks]
        if keep_q:
            self.out_shape += [jax.ShapeDtypeStruct((3, *b), BF16) for b in blks]
        self.scratch = [pltpu.VMEM((3, *b), BF16) for b in blks] * 3 + [
            pltpu.SemaphoreType.DMA((4 * n,)), pltpu.SemaphoreType.DMA((4 * n,)), pltpu.SemaphoreType.DMA((5 * n,))]

    def bind(self, ps, outs, scratch):
        n = self.n
        self.ps, self.owns, self.sibs, self.qs = ps, outs[:n], outs[n:2 * n], outs[2 * n:]
        self.pa, self.pb, self.qst = scratch[:n], scratch[n:2 * n], scratch[2 * n:3 * n]
        self.s_send, self.s_recv, self.lsem = scratch[3 * n:]
        return self

    def _local(self, with_q):
        n = self.n
        x, y, c = _place()
        chips = [(1 - x, y), (x, 1 - y), (1 - x, 1 - y)]
        own = [pltpu.make_async_copy(self.ps[w].at[4 * x + 2 * y + c], self.owns[w], self.lsem.at[w]) for w in range(n)]
        mine = [[pltpu.make_async_copy(self.ps[w].at[4 * cx + 2 * cy + c], self.pa[w].at[j], self.lsem.at[2 * n + 3 * w + j])
                 for j, (cx, cy) in enumerate(chips)] for w in range(n)]
        outq = [pltpu.make_async_copy(self.qst[w], self.qs[w], self.lsem.at[n + w]) for w in range(n)] if with_q else []
        return own, mine, outq

    def _pair(self, w, rel):
        x, y, c = _place()
        cx, cy = [(x, y), (1 - x, y), (x, 1 - y), (1 - x, 1 - y)][rel]
        return pltpu.make_async_remote_copy(
            src_ref=self.ps[w].at[4 * cx + 2 * cy + (1 - c)],
            dst_ref=self.sibs[w].at[0] if rel == 0 else self.pb[w].at[rel - 1],
            send_sem=self.s_send.at[4 * w + rel], recv_sem=self.s_recv.at[4 * w + rel],
            device_id=(x, y, 1 - c), device_id_type=MESH_ID)

    def issue(self):
        own, mine, _ = self._local(False)
        for cp in own + [cp for row in mine for cp in row]:
            cp.start()
        for w in range(self.n):
            for rel in (1, 2, 3, 0):
                self._pair(w, rel).start()

    def finish(self):
        own, mine, outq = self._local(self.keep_q)
        for w in range(self.n):
            for j in range(3):
                self._pair(w, 1 + j).wait_recv()
                mine[w][j].wait()
                self.qst[w][j] = (self.pa[w][j].astype(F32) + self.pb[w][j].astype(F32)).astype(BF16)
            if self.keep_q:
                outq[w].start()
        for w in range(self.n):
            self._pair(w, 0).wait_recv()
        for w in range(self.n):
            for rel in range(4):
                self._pair(w, rel).wait_send()
        for cp in own + outq:
            cp.wait()

    def results(self, outs):
        n = self.n
        return outs[:n], outs[n:2 * n], outs[2 * n:3 * n]


def _rs_pair(name, parts):
    ps = _PairSum(parts)
    n = ps.n

    def body(*refs):
        ps.bind(refs[:n], refs[n:4 * n], refs[4 * n:])
        ps.issue()
        ps.finish()

    any_spec = pl.BlockSpec(memory_space=pl.ANY)
    outs = pl.pallas_call(
        body, name=name, out_shape=ps.out_shape,
        in_specs=[any_spec] * n, out_specs=[any_spec] * (3 * n), scratch_shapes=ps.scratch,
        compiler_params=pltpu.CompilerParams(vmem_limit_bytes=48 * MIB),
    )(*parts)
    return ps.results(outs)


def _adamw_math(g, w, m, v):
    m = ADAM_B1 * m + (1.0 - ADAM_B1) * g
    v = ADAM_B2 * v + (1.0 - ADAM_B2) * (g * g)
    m_hat = m / (1.0 - ADAM_B1 ** ADAM_STEP)
    v_hat = v / (1.0 - ADAM_B2 ** ADAM_STEP)
    delta = -ADAM_LR * (m_hat / (jnp.sqrt(v_hat) + ADAM_EPS) + ADAM_WD * w)
    return delta, m, v


def _adamw_multi(name, own, sib, rel, ws, ms, vs, row_grid):
    k_n, r_n, c_n = own.shape
    rbk = r_n // row_grid

    def body(*refs):
        own_ref, sib_ref, r0_ref, r1_ref, r2_ref = refs[:5]
        w_refs, m_refs, v_refs = refs[5:5 + k_n], refs[5 + k_n:5 + 2 * k_n], refs[5 + 2 * k_n:5 + 3 * k_n]
        outs = refs[5 + 3 * k_n:]
        for k in range(k_n):
            g = own_ref[k].astype(F32) + sib_ref[k].astype(F32)
            g = g + r0_ref[k].astype(F32)
            g = g + r1_ref[k].astype(F32)
            g = g + r2_ref[k].astype(F32)
            delta, mm, vv = _adamw_math(g, w_refs[k][0], m_refs[k][0], v_refs[k][0])
            outs[4 * k][0] = g
            outs[4 * k + 1][0] = delta
            outs[4 * k + 2][0] = mm
            outs[4 * k + 3][0] = vv

    def lead(j):
        return pl.BlockSpec((None, k_n, rbk, c_n), lambda g: (j, 0, g, 0))

    wspec = pl.BlockSpec((1, rbk, c_n), lambda g: (0, g, 0))
    shp = jax.ShapeDtypeStruct((1, r_n, c_n), F32)
    res = pl.pallas_call(
        body, name=name, grid=(row_grid,),
        in_specs=[pl.BlockSpec((k_n, rbk, c_n), lambda g: (0, g, 0)), lead(0), lead(0), lead(1), lead(2)] + [wspec] * (3 * k_n),
        out_specs=[wspec] * (4 * k_n), out_shape=[shp] * (4 * k_n),
        compiler_params=_params(("arbitrary",), 40),
    )(own, sib, rel, rel, rel, *ws, *ms, *vs)
    return [tuple(res[4 * k:4 * k + 4]) for k in range(k_n)]


def _adamw_meta_dw(own, sib, rel, meta, dw):
    def body(own_ref, sib_ref, rel_ref, wm, mm, vm, wd, md, vd, *outs):
        def gsum(rows):
            g = own_ref[rows, :].astype(F32) + sib_ref[0, rows, :].astype(F32)
            for j in range(3):
                g = g + rel_ref[j, rows, :].astype(F32)
            return g

        g = gsum(pl.ds(0, N_META))
        delta, m2, v2 = _adamw_math(g, wm[...], mm[...], vm[...])
        for o, val in zip(outs[:4], (g, delta, m2, v2)):
            o[...] = val
        g = gsum(pl.ds(N_META, CONV_K))
        delta, m2, v2 = _adamw_math(g, wd[0], md[0], vd[0])
        for o, val in zip(outs[4:], (g, delta, m2, v2)):
            o[0] = val

    s_meta = jax.ShapeDtypeStruct(meta[0].shape, F32)
    s_dw = jax.ShapeDtypeStruct(dw[0].shape, F32)
    res = pl.pallas_call(body, name="adamw_meta_dw", out_shape=[s_meta] * 4 + [s_dw] * 4)(own, sib, rel, *meta, *dw)
    return tuple(res[:4]), tuple(res[4:])


REP_ROWS = 16


def _adamw_rep(gathered, ws, ms, vs):
    rows = [(0, 1), (1, 2), (3, 1), (4, 1), (5, 1), (6, 1), (7, 1), (8, 1)]

    def body(g_ref, *refs):
        w_refs, m_refs, v_refs = refs[:8], refs[8:16], refs[16:24]
        loss_ref, outs, acc = refs[24], refs[25:57], refs[57]
        g = g_ref[0]
        for d in range(1, NDEV):
            g = g + g_ref[d]
        acc[...] = g
        loss_ref[...] = (0.5 / D) * jnp.sum(acc[pl.ds(9, 1), :], axis=1, keepdims=True)
        for p, (r0, nr) in enumerate(rows):
            for h in range(nr):
                cols = pl.ds(h * D, D)
                gp = acc[pl.ds(r0 + h, 1), :]
                delta, mm, vv = _adamw_math(gp, w_refs[p][:, cols], m_refs[p][:, cols], v_refs[p][:, cols])
                for o, val in zip(outs[4 * p:4 * p + 4], (gp, delta, mm, vv)):
                    o[:, cols] = val

    shapes = [jax.ShapeDtypeStruct(w.shape, F32) for w in ws]
    res = pl.pallas_call(
        body, name="adamw_rep",
        out_shape=[jax.ShapeDtypeStruct((1, 1), F32)] + [s for s in shapes for _ in range(4)],
        scratch_shapes=[pltpu.VMEM((REP_ROWS, D), F32)],
    )(gathered, *ws, *ms, *vs)
    return res[0], [tuple(res[1 + 4 * p:5 + 4 * p]) for p in range(8)]


def _load_ffn(i, j, wgu_hbm, wgu, wdn_hbm, wdn, sems):
    half = NDEV // 2

    def copies(ch):
        pairs = [(wgu_hbm.at[half * ch + d, g], wgu.at[g, ch, pl.ds(FFB * d, FFB), :]) for g in range(2) for d in range(half)]
        pairs.append((wdn_hbm.at[ch], wdn.at[ch]))
        return [pltpu.make_async_copy(s, t, sems.at[(2 * half + 1) * ch + k]) for k, (s, t) in enumerate(pairs)]

    @pl.when((i == 0) & (j == 0))
    def _():
        for cp in copies(0) + copies(1):
            cp.start()

    for ch in range(2):
        @pl.when((i == 0) & (j == ch))
        def _():
            for cp in copies(ch):
                cp.wait()


def _win_pairs(w_hbm, w_vm):
    return [(w_hbm.at[q], w_vm.at[q // 2, :, pl.ds(2 * INB * (q % 2), 2 * INB)]) for q in range(4)]


def _whole(a):
    nd = a.ndim
    return pl.BlockSpec(a.shape, lambda *g: (0,) * nd)


CHIPW = 2 * INB
PHASE_CHIP = (1, 0, 2)
assert PHASE_CHIP[2] == 2
W_PARTS = 4
assert D % (16 * W_PARTS) == 0


class _Parts(list):
    def start(self):
        for cp in self:
            cp.start()

    def wait_recv(self):
        for cp in self:
            cp.wait_recv()

    def wait_send(self):
        for cp in self:
            cp.wait_send()


class _GatherIn:
    scratch = [pltpu.VMEM((D, INB), BF16), pltpu.SemaphoreType.DMA((7 * W_PARTS,)), pltpu.SemaphoreType.DMA((7 * W_PARTS,)),
               pltpu.SemaphoreType.DMA((1,))]

    def bind(self, w_ref, w_vm, scratch):
        self.w_ref, self.w_vm = w_ref, w_vm
        self.stage, self.send_sems, self.recv_sems, self.local_sem = scratch
        return self

    def _win(self, chip, core):
        return self.w_vm.at[2 * chip[0] + chip[1], core]

    def _copy(self, k, chip, core, to, src=None):
        dst = self._win(chip, core)
        src = dst if src is None else src
        rows = D // W_PARTS
        return _Parts(pltpu.make_async_remote_copy(
            src_ref=src.at[pl.ds(rows * r, rows)], dst_ref=dst.at[pl.ds(rows * r, rows)],
            send_sem=self.send_sems.at[W_PARTS * k + r], recv_sem=self.recv_sems.at[W_PARTS * k + r],
            device_id=to, device_id_type=MESH_ID) for r in range(W_PARTS))

    def _mine(self, cs):
        x, y, _ = _place()
        return pltpu.make_async_copy(self.stage, self._win((x, y), cs), self.local_sem.at[0])

    def issue(self, cs):
        x, y, _ = _place()
        chips = [(1 - x, y), (x, 1 - y), (1 - x, 1 - y)]
        self.stage[...] = self.w_ref[0].astype(BF16)
        self._mine(cs).start()
        self._copy(0, (x, y), cs, (x, y, 1 - cs), src=self.stage).start()
        for j in PHASE_CHIP[:2]:
            self._copy(1 + j, (x, y), cs, (*chips[j], cs), src=self.stage).start()

    def wait_chip(self, phase, cs):
        x, y, _ = _place()
        chips = [(1 - x, y), (x, 1 - y), (1 - x, 1 - y)]
        if phase == 0:
            self._mine(cs).wait()
            self._copy(0, (x, y), 1 - cs, (x, y, cs)).wait_recv()
            return
        j = PHASE_CHIP[phase - 1]
        self._copy(1 + j, chips[j], cs, (x, y, cs)).wait_recv()
        self._copy(4 + j, chips[j], cs, (x, y, 1 - cs)).start()
        if phase == 1:
            self._copy(3, (x, y), cs, (*chips[2], cs), src=self.stage).start()
        self._copy(4 + j, chips[j], 1 - cs, (x, y, cs)).wait_recv()

    def finish(self, cs):
        x, y, _ = _place()
        for k in range(7):
            self._copy(k, (x, y), cs, (x, y, cs), src=self.stage).wait_send()


def _fwd_in(x2, g_mix, w_in, order, tp, ag, ags):
    tm = _pick(tp, TM_IO)
    nt = tp // tm
    nx_last = x2.shape[0] - (nt - 1) * tm
    na, ng, ns = len(ag.arrays), ag.n, len(ags.arrays)
    gin = _GatherIn()

    def body(order_ref, *refs):
        x_ref, g_ref, w_ref = refs[:3]
        o = 3 + na + ns
        h_ref, z_ref, u_ref, wout_ref = refs[o:o + 4]
        s = o + 4 + ng + 1
        w_vm, u_all, osem, sm_vm = refs[s:s + 4]
        gin.bind(w_ref, w_vm, refs[s + 4:s + 8])
        ag.bind(refs[3:3 + na], refs[o + 4:o + 4 + ng], refs[s + 8:s + 8 + len(ag.scratch)])
        ags.bind(refs[3 + na:3 + na + ns], refs[o + 4 + ng:o + 5 + ng], refs[s + 8 + len(ag.scratch):])
        ph, i = pl.program_id(0), pl.program_id(1)
        core = lax.axis_index("c")
        first = (ph == 0) & (i == 0)
        last = (ph == 3) & (i == nt - 1)
        @pl.when(first)
        def _():
            ags.issue()

        for cs in range(2):
            @pl.when(first & (core == cs))
            def _():
                gin.issue(cs)

        @pl.when((ph == 0) & (i == max(nt - 2, 0)))
        def _():
            ags.forward()

        for cs in range(2):
            for p in range(4):
                @pl.when((ph == p) & (i == 0) & (core == cs))
                def _():
                    gin.wait_chip(p, cs)

        @pl.when((ph == 2) & (i == 0))
        def _():
            ag.issue()

        out_copies = [pltpu.make_async_copy(w_vm.at[k, c], wout_ref.at[k, :, pl.ds(INB * c, INB)], osem.at[2 * k + c])
                      for k in range(4) for c in range(2)]

        @pl.when((ph == 3) & (i == 0))
        def _():
            for cp in out_copies:
                cp.start()

        @pl.when((ph == 0) & (i < nt - 1))
        def _():
            h_ref[...] = x_ref[...]

        @pl.when((ph == 0) & (i == nt - 1))
        def _():
            ags.finish()
            cp = pltpu.make_async_copy(ags.outs[0], sm_vm, osem.at[8])
            cp.start()
            h_ref[pl.ds(0, nx_last), :] = x_ref[pl.ds(0, nx_last), :]
            h_ref[pl.ds(nx_last, tm - nx_last - N_META), :] = jnp.zeros((tm - nx_last - N_META, D), F32)
            cp.wait()
            for d in range(NDEV):
                h_ref[pl.ds(tm - N_META, N_META), pl.ds(128 * d, 128)] = sm_vm[d, pl.ds(0, N_META), :]

        @pl.when(ph == 0)
        def _():
            xv = h_ref[...]
            r = lax.rsqrt(jnp.mean(xv * xv, axis=-1, keepdims=True) + RMS_EPS)
            u = (xv * r * g_ref[...]).astype(BF16)
            u_ref[...] = u
            u_all[i] = u

        for c in range(2):
            z_ref[:, INB * c:INB * (c + 1)] = _dot(u_all[i], w_vm[order_ref[ph], c])

        @pl.when(last)
        def _():
            ag.forward()
            ag.finish()
            for cp in out_copies:
                cp.wait()

        for cs in range(2):
            @pl.when(last & (core == cs))
            def _():
                gin.finish(cs)

    def rows(ph, i, order):
        return (jnp.where(ph == 0, i, nt - 1), 0)

    tile = pl.BlockSpec((tm, D), rows)
    anys = pl.BlockSpec(memory_space=pl.ANY)
    res = pl.pallas_call(
        body, name="fwd_in",
        grid_spec=pltpu.PrefetchScalarGridSpec(
            num_scalar_prefetch=1, grid=(4, nt),
            in_specs=[tile, pl.BlockSpec((1, D), lambda ph, i, order: (0, 0)), _whole(w_in)]
            + [_whole(a) for a in ag.arrays + ags.arrays],
            out_specs=[tile, pl.BlockSpec((tm, CHIPW), lambda ph, i, order: (i, order[ph])), tile, anys] + [anys] * (ng + 1),
            scratch_shapes=[pltpu.VMEM((4, 2, D, INB), BF16), pltpu.VMEM((nt, tm, D), BF16), pltpu.SemaphoreType.DMA((9,)),
                            pltpu.VMEM(ags.out_shape[0].shape, F32)] + gin.scratch + ag.scratch + ags.scratch),
        out_shape=[jax.ShapeDtypeStruct((tp, D), F32), jax.ShapeDtypeStruct((tp, DIN), F32),
                   jax.ShapeDtypeStruct((tp, D), BF16), jax.ShapeDtypeStruct((4, D, CHIPW), BF16)]
        + ag.out_shape + ags.out_shape,
        compiler_params=_params(("arbitrary", "arbitrary"), 58),
    )(order, x2, g_mix, w_in, *ag.arrays, *ags.arrays)
    return res[:4], res[4:4 + ng], res[4 + ng]


def _halo_specs(col, nt, width=D):
    r = TM // HALO
    nb = nt * r
    return [pl.BlockSpec((HALO, width), lambda i: ((i * r + nb - 1) % nb, col)),
            pl.BlockSpec((TM, width), lambda i: (i, col)),
            pl.BlockSpec((HALO, width), lambda i: (((i + 1) * r) % nb, col))]


NCB = D // 128
TME = TM + 2 * HALO


def _tm_fill(dst, time0, groups, tile_fn, unroll=1):
    def body(g, c):
        for j in range(NCB):
            dst[pl.ds((time0 + 8 * g) * NCB + j, 8, stride=NCB), :] = tile_fn(pl.multiple_of(8 * g, 8), pl.ds(128 * j, 128))
        return c

    lax.fori_loop(0, groups, body, 0, unroll=unroll)


def _tm_fill_ext(dst, left, cur, right, fn, unroll=1):
    _tm_fill(dst, 0, HALO // 8, lambda r, l: fn(left, pl.ds(r, 8), l), unroll)
    _tm_fill(dst, HALO, TM // 8, lambda r, l: fn(cur, pl.ds(r, 8), l), unroll)
    _tm_fill(dst, HALO + TM, HALO // 8, lambda r, l: fn(right, pl.ds(r, 8), l), unroll)


def _tm_read(src, groups, store_fn):
    def body(g, c):
        for j in range(NCB):
            store_fn(pl.ds(pl.multiple_of(8 * g, 8), 8), pl.ds(128 * j, 128), src[pl.ds(8 * g * NCB + j, 8, stride=NCB), :])
        return c

    lax.fori_loop(0, groups, body, 0, unroll=2)


def _tm_rows(t):
    return pl.ds(t * NCB if isinstance(t, int) else pl.multiple_of(t * NCB, NCB), NCB)


def _tm_at(ref, t):
    return ref[_tm_rows(t), :]


def _by_group(sub, vals):
    return jnp.where(sub < 2, vals[0], jnp.where(sub < 4, vals[1], jnp.where(sub < 6, vals[2], vals[3])))


def _pool_cnt(b, seq, tp, sub):
    b = jnp.where(b < 0, b + tp, b)
    b = jnp.where(b >= tp, b - tp, b)
    t = jnp.where(b < seq, b + N_META, b - (tp - N_META))
    cnts = []
    for win in POOL_WINDOWS:
        left = win // 2
        lo = jnp.maximum(t - left, 0)
        hi = jnp.minimum(t + win - left, seq + N_META)
        cnts.append(jnp.maximum(hi - lo, 1).astype(F32))
    return _by_group(sub, cnts)


def _edge_rows(seq, tp):
    reach = max(POOL_WINDOWS) // 2
    return [tp - N_META + t for t in range(reach)] + [seq - reach + 1 + t for t in range(reach - 1)]


def _edge_gain(b, seq, tp, sub):
    return _by_group(sub, [float(w) for w in POOL_WINDOWS]) / _pool_cnt(b, seq, tp, sub)


def _nested_windows(at, lo_offs):
    sums, s, have = [], None, set()
    for g, win in enumerate(POOL_WINDOWS):
        for o in range(lo_offs[g], lo_offs[g] + win):
            if o not in have:
                have.add(o)
                s = at(o) if s is None else s + at(o)
        sums.append(s)
    return sums


def _seq_fwd(z, w_dw, b_dw, seq, gat):
    tp = z.shape[0]
    nt = tp // TM
    na, ng = len(gat.arrays), gat.n

    def body(*refs):
        av_l, av, av_r, ag_l, ag, ag_r, p_l, p, p_r, w_ref, b_ref = refs[:11]
        ac_ref, m_ref = refs[11 + na:13 + na]
        a3, p3, o3, m3, w3, b3, m2d = refs[13 + na + ng:20 + na + ng]
        gat.bind(refs[11:11 + na], refs[13 + na:13 + na + ng], refs[20 + na + ng:])
        i = pl.program_id(0)
        sub = lax.broadcasted_iota(jnp.int32, (NCB, 128), 0)

        @pl.when(i == 0)
        def _():
            gat.issue()
            _tm_fill(w3, 0, 4, lambda r, l: w_ref[pl.ds(r, 8), l])
            for j in range(NCB):
                b3[pl.ds(j, 1), :] = b_ref[:, pl.ds(128 * j, 128)]

        @pl.when(i == max(nt - 2, 0))
        def _():
            gat.forward()

        _tm_fill_ext(a3, (av_l, ag_l), (av, ag), (av_r, ag_r), lambda vg, r, l: vg[0][r, l] * _sig(vg[1][r, l]), unroll=2)
        _tm_fill_ext(p3, p_l, p, p_r, lambda ref, r, l: ref[r, l])

        def conv(g, c):
            accs = [b3[...]] * 16
            for k in range(CONV_K):
                wk = _tm_at(w3, k)
                for t in range(16):
                    accs[t] = accs[t] + wk * _tm_at(a3, 16 * g + t + k + 1)
            for t in range(16):
                o3[_tm_rows(16 * g + t), :] = accs[t]
            return c

        lax.fori_loop(0, TM // 16, conv, 0)
        _tm_read(o3, TM // 8, lambda r, l, tile: ac_ref.__setitem__((r, l), tile))

        inv = _by_group(sub, [1.0 / w for w in POOL_WINDOWS])

        def pool(g, c):
            for t in range(8):
                e = 8 * g + t + HALO
                sums = _nested_windows(lambda o: _tm_at(p3, e + o), [-(w // 2) for w in POOL_WINDOWS])
                m3[_tm_rows(8 * g + t), :] = _by_group(sub, sums) * inv - _tm_at(p3, e)
            return c

        lax.fori_loop(0, TM // 8, pool, 0)
        for b in _edge_rows(seq, tp):
            r = b - i * TM

            @pl.when((r >= 0) & (r < TM))
            def _():
                pv = _tm_at(p3, r + HALO)
                m3[_tm_rows(r), :] = (_tm_at(m3, r) + pv) * _edge_gain(b, seq, tp, sub) - pv

        _tm_read(m3, TM // 8, lambda r, l, tile: m2d.__setitem__((r, l), tile))
        m_ref[...] = m2d[...].astype(BF16)

        @pl.when(i == nt - 1)
        def _():
            gat.finish()

    tmaj = pltpu.VMEM((TM * NCB, 128), F32)
    text = pltpu.VMEM((TME * NCB, 128), F32)
    res = pl.pallas_call(
        body, name="seq_fwd", grid=(nt,),
        in_specs=_halo_specs(0, nt) + _halo_specs(1, nt) + _halo_specs(2, nt)
        + [pl.BlockSpec((32, D), lambda i: (0, 0)), pl.BlockSpec((1, D), lambda i: (0, 0))] + [_whole(a) for a in gat.arrays],
        out_specs=[pl.BlockSpec((TM, D), lambda i: (i, 0))] * 2 + [pl.BlockSpec(memory_space=pl.ANY)] * ng,
        out_shape=[jax.ShapeDtypeStruct((tp, D), F32), jax.ShapeDtypeStruct((tp, D), BF16)] + gat.out_shape,
        scratch_shapes=[text, text, tmaj, tmaj, pltpu.VMEM((32 * NCB, 128), F32), pltpu.VMEM((NCB, 128), F32),
                        pltpu.VMEM((TM, D), F32)] + gat.scratch,
        compiler_params=_params(("arbitrary",), 52),
    )(z, z, z, z, z, z, z, z, z, w_dw, b_dw, *gat.arrays)
    return res[:2], res[2:]


def _ln_stats(ac):
    mu = jnp.mean(ac, axis=-1, keepdims=True)
    xc = ac - mu
    rl = lax.rsqrt(jnp.mean(xc * xc, axis=-1, keepdims=True) + LN_EPS)
    return xc * rl, rl


def _pool_mix(m, wp_ref):
    return jnp.concatenate(
        [_dot(m[:, g * PG:(g + 1) * PG], wp_ref[:, g].reshape(PG, PG)) for g in range(4)], axis=1)


def _mix_fwd(ac, m, z, h0, b_gate, ln_g, ln_b, pool_scale, g_mixw, g_pool, gat):
    tp = h0.shape[0]
    tms = TM
    nt = tp // tms
    na, ng = len(gat.arrays), gat.n

    def body(*refs):
        ac_ref, m_ref, zga, zgb, h_ref, bg_ref, lg_ref, lb_ref, ps_ref, wm_hbm, wp_hbm = refs[:11]
        h1_ref, s_ref, mg_ref, q_ref = refs[11 + na:15 + na]
        wm, wp, sems = refs[15 + na + ng:18 + na + ng]
        gat.bind(refs[11:11 + na], refs[15 + na:15 + na + ng], refs[18 + na + ng:])
        i = pl.program_id(0)

        @pl.when(i == 0)
        def _():
            gat.issue()

        @pl.when(i == max(nt - 4, 0))
        def _():
            gat.forward()

        @pl.when(i == nt - 1)
        def _():
            gat.finish()

        _load_once(i == 0, [(wm_hbm, wm), (wp_hbm, wp)], sems)
        n, _ = _ln_stats(ac_ref[...])
        l = n * lg_ref[...] + lb_ref[...]
        s = (l * _sig(l)).astype(BF16)
        s_ref[...] = s
        yc = _dot(s, wm[:, 0].reshape(D, D))
        q = (_pool_mix(m_ref[...], wp) * ps_ref[...]).astype(BF16)
        q_ref[...] = q
        yp = _dot(q, wm[:, 1].reshape(D, D))
        ga = _sig(zga[...] + bg_ref[:, :D])
        gb = _sig(zgb[...] + bg_ref[:, D:])
        merged = (ga * yc + gb * yp).astype(BF16)
        mg_ref[...] = merged
        h1_ref[...] = h_ref[...] + _dot(merged, wm[:, 2].reshape(D, D))

    def tile(col=0):
        return pl.BlockSpec((tms, D), lambda i: (i, col))

    def vec(w):
        return pl.BlockSpec((1, w), lambda i: (0, 0))

    anys = pl.BlockSpec(memory_space=pl.ANY)
    f32o, b16o = jax.ShapeDtypeStruct((tp, D), F32), jax.ShapeDtypeStruct((tp, D), BF16)
    res = pl.pallas_call(
        body, name="mix_fwd", grid=(nt,),
        in_specs=[tile(), tile(), tile(3), tile(4), tile(), vec(2 * D), vec(D), vec(D), vec(D), anys, anys]
        + [_whole(a) for a in gat.arrays],
        out_specs=[tile()] * 4 + [anys] * ng,
        out_shape=[f32o, b16o, b16o, b16o] + gat.out_shape,
        scratch_shapes=[pltpu.VMEM((NDEV, 3, D // NDEV, D), BF16), pltpu.VMEM((NDEV, 4, PG // NDEV, PG), BF16),
                        pltpu.SemaphoreType.DMA((2,))] + gat.scratch,
        compiler_params=_params(("arbitrary",), 52),
    )(ac, m, z, z, h0, b_gate, ln_g, ln_b, pool_scale, g_mixw, g_pool, *gat.arrays)
    return res[:4], res[4:]


def _ffn_fwd(h1, tgt, g_ffn, g_final, w_gu, w_dn):
    tp = h1.shape[0]
    nt = tp // TM
    nx_last = tgt.shape[0] - (nt - 1) * TM

    def body(h_ref, t_ref, gf_ref, gl_ref, wgu_hbm, wdn_hbm,
             fg_ref, fu_ref, v_ref, f_ref, dh2_ref, acc_ref, wgu, wdn, v_sc, h2_sc, diff_sc, sems):
        i, j = pl.program_id(0), pl.program_id(1)
        _load_ffn(i, j, wgu_hbm, wgu, wdn_hbm, wdn, sems)

        @pl.when((i == 0) & (j == 0))
        def _():
            acc_ref[...] = jnp.zeros_like(acc_ref)

        @pl.when(j == 0)
        def _():
            h = h_ref[...]
            r = lax.rsqrt(jnp.mean(h * h, axis=-1, keepdims=True) + RMS_EPS)
            v = (h * r * gf_ref[...]).astype(BF16)
            v_sc[...] = v
            v_ref[...] = v
            h2_sc[...] = h

        v = v_sc[...]
        fg = _dot_nt(v, wgu[0, j])
        fu = _dot_nt(v, wgu[1, j])
        fg_ref[...] = fg
        fu_ref[...] = fu
        f = ((fg * _sig(fg)) * fu).astype(BF16)
        f_ref[...] = f
        h2_sc[...] += _dot(f, wdn[j])

        @pl.when(j == 1)
        def _():
            h2 = h2_sc[...]
            r = lax.rsqrt(jnp.mean(h2 * h2, axis=-1, keepdims=True) + RMS_EPS)
            n2 = h2 * r
            y = n2 * gl_ref[...]

            @pl.when(i < nt - 1)
            def _():
                diff_sc[...] = y - t_ref[...]

            @pl.when(i == nt - 1)
            def _():
                diff_sc[pl.ds(0, nx_last), :] = y[:nx_last] - t_ref[pl.ds(0, nx_last), :]
                diff_sc[pl.ds(nx_last, TM - nx_last), :] = jnp.zeros((TM - nx_last, D), F32)

            diff = diff_sc[...]
            dy = diff * (1.0 / D)
            acc_ref[0:1, :] += jnp.sum(diff * diff, axis=0, keepdims=True)
            acc_ref[1:2, :] += jnp.sum(dy * n2, axis=0, keepdims=True)
            dn = dy * gl_ref[...]
            dh2_ref[...] = r * (dn - n2 * jnp.mean(dn * n2, axis=-1, keepdims=True))

    def tile():
        return pl.BlockSpec((TM, D), lambda i, j: (i, 0))

    def chunk():
        return pl.BlockSpec((TM, FFC), lambda i, j: (i, j))

    def vec():
        return pl.BlockSpec((1, D), lambda i, j: (0, 0))

    anys = pl.BlockSpec(memory_space=pl.ANY)
    hid32, hid16 = jax.ShapeDtypeStruct((tp, DFF), F32), jax.ShapeDtypeStruct((tp, DFF), BF16)
    return pl.pallas_call(
        body, name="ffn_fwd", grid=(nt, 2),
        in_specs=[tile(), tile(), vec(), vec(), anys, anys],
        out_specs=[chunk(), chunk(), tile(), chunk(), tile(), pl.BlockSpec((8, D), lambda i, j: (0, 0))],
        out_shape=[hid32, hid32, jax.ShapeDtypeStruct((tp, D), BF16), hid16, jax.ShapeDtypeStruct((tp, D), F32),
                   jax.ShapeDtypeStruct((8, D), F32)],
        scratch_shapes=[pltpu.VMEM((2, 2, FFC, D), BF16), pltpu.VMEM((2, FFC, D), BF16),
                        pltpu.VMEM((TM, D), BF16), pltpu.VMEM((TM, D), F32), pltpu.VMEM((TM, D), F32),
                        pltpu.SemaphoreType.DMA((2 * NDEV + 2,))],
        compiler_params=_params(("arbitrary", "arbitrary"), 56),
    )(h1, tgt, g_ffn, g_final, w_gu, w_dn)


def _ffn_bwd(dh2, fg, fu, h1, g_ffn, w_gu, w_dn):
    tp = h1.shape[0]
    nt = tp // TM

    def body(dh2_ref, fg_ref, fu_ref, h_ref, gf_ref, wgu_hbm, wdn_hbm,
             dfg_ref, dfu_ref, dh1_ref, acc_ref, wgu, wdn, d_sc, dv_sc, sems):
        i, j = pl.program_id(0), pl.program_id(1)
        _load_ffn(i, j, wgu_hbm, wgu, wdn_hbm, wdn, sems)

        @pl.when((i == 0) & (j == 0))
        def _():
            acc_ref[...] = jnp.zeros_like(acc_ref)

        @pl.when(j == 0)
        def _():
            d_sc[...] = dh2_ref[...].astype(BF16)
            dv_sc[...] = jnp.zeros_like(dv_sc)

        df = _dot_nt(d_sc[...], wdn[j])
        fg = fg_ref[...]
        sg = _sig(fg)
        dfu = (df * (fg * sg)).astype(BF16)
        dfg = (df * fu_ref[...] * (sg * (1.0 + fg * (1.0 - sg)))).astype(BF16)
        dfg_ref[...] = dfg
        dfu_ref[...] = dfu
        dv_sc[...] += _dot(dfg, wgu[0, j]) + _dot(dfu, wgu[1, j])

        @pl.when(j == 1)
        def _():
            h = h_ref[...]
            r = lax.rsqrt(jnp.mean(h * h, axis=-1, keepdims=True) + RMS_EPS)
            n1 = h * r
            dv = dv_sc[...]
            acc_ref[0:1, :] += jnp.sum(dv * n1, axis=0, keepdims=True)
            dn = dv * gf_ref[...]
            dh1_ref[...] = dh2_ref[...] + r * (dn - n1 * jnp.mean(dn * n1, axis=-1, keepdims=True))

    def tile():
        return pl.BlockSpec((TM, D), lambda i, j: (i, 0))

    def chunk():
        return pl.BlockSpec((TM, FFC), lambda i, j: (i, j))

    anys = pl.BlockSpec(memory_space=pl.ANY)
    hid16 = jax.ShapeDtypeStruct((tp, DFF), BF16)
    return pl.pallas_call(
        body, name="ffn_bwd", grid=(nt, 2),
        in_specs=[tile(), chunk(), chunk(), tile(), pl.BlockSpec((1, D), lambda i, j: (0, 0)), anys, anys],
        out_specs=[chunk(), chunk(), tile(), pl.BlockSpec((8, D), lambda i, j: (0, 0))],
        out_shape=[hid16, hid16, jax.ShapeDtypeStruct((tp, D), F32), jax.ShapeDtypeStruct((8, D), F32)],
        scratch_shapes=[pltpu.VMEM((2, 2, FFC, D), BF16), pltpu.VMEM((2, FFC, D), BF16),
                        pltpu.VMEM((TM, D), BF16), pltpu.VMEM((TM, D), F32), pltpu.SemaphoreType.DMA((2 * NDEV + 2,))],
        compiler_params=_params(("arbitrary", "arbitrary"), 56),
    )(dh2, fg, fu, h1, g_ffn, w_gu, w_dn)


def _mix_bwd(dh1, z, s, q, ac, m, b_gate, ln_g, ln_b, pool_scale, g_mixw, g_pool, qs):
    tp = dh1.shape[0]
    nt = tp // TMS
    ex = _ChipExchange(qs)
    nq = ex.n

    def body(*refs):
        dh1_ref, zga, zgb, s_ref, q_ref, ac_ref, m_ref, bg_ref, lg_ref, lb_ref, ps_ref, wm_hbm, wp_hbm = refs[:13]
        dac_ref, dm_ref, dzg_ref, dyc_ref, dyp_ref, dm2_ref, acc_ref = refs[13 + nq:20 + nq]
        wm, wp, sems = refs[20 + 2 * nq:23 + 2 * nq]
        ex.bind(refs[13:13 + nq], refs[20 + nq:20 + 2 * nq], refs[23 + 2 * nq:])
        first = pl.program_id(0) == 0

        @pl.when(first)
        def _():
            ex.issue()
            acc_ref[...] = jnp.zeros_like(acc_ref)

        _load_once(first, [(wm_hbm, wm), (wp_hbm, wp)], sems)

        dmerged = _dot_nt(dh1_ref[...].astype(BF16), wm[:, 2].reshape(D, D))
        ga = _sig(zga[...] + bg_ref[:, :D])
        gb = _sig(zgb[...] + bg_ref[:, D:])
        dyc = dmerged * ga
        dyp = dmerged * gb
        dza = (dmerged * _dot(s_ref[...], wm[:, 0].reshape(D, D))) * (ga * (1.0 - ga))
        dzb = (dmerged * _dot(q_ref[...], wm[:, 1].reshape(D, D))) * (gb * (1.0 - gb))
        dzg_ref[:, :D] = dza.astype(BF16)
        dzg_ref[:, D:] = dzb.astype(BF16)
        acc_ref[0:1, :D] += jnp.sum(dza, axis=0, keepdims=True)
        acc_ref[0:1, D:] += jnp.sum(dzb, axis=0, keepdims=True)
        dyc_b = dyc.astype(BF16)
        dyp_b = dyp.astype(BF16)
        dyc_ref[...] = dyc_b
        dyp_ref[...] = dyp_b
        ds = _dot_nt(dyc_b, wm[:, 0].reshape(D, D))
        n, rl = _ln_stats(ac_ref[...])
        l = n * lg_ref[...] + lb_ref[...]
        sg = _sig(l)
        dl = ds * (sg * (1.0 + l * (1.0 - sg)))
        acc_ref[1:2, :D] += jnp.sum(dl * n, axis=0, keepdims=True)
        acc_ref[1:2, D:] += jnp.sum(dl, axis=0, keepdims=True)
        dn = dl * lg_ref[...]
        dac_ref[...] = rl * (dn - jnp.mean(dn, axis=-1, keepdims=True) - n * jnp.mean(dn * n, axis=-1, keepdims=True))
        dq = _dot_nt(dyp_b, wm[:, 1].reshape(D, D))
        mv = m_ref[...]
        acc_ref[2:3, :D] += jnp.sum(dq * _pool_mix(mv, wp), axis=0, keepdims=True)
        dm2 = (dq * ps_ref[...]).astype(BF16)
        dm2_ref[...] = dm2
        dm_ref[...] = jnp.concatenate(
            [_dot_nt(dm2[:, g * PG:(g + 1) * PG], wp[:, g].reshape(PG, PG)) for g in range(4)], axis=1)

        @pl.when(pl.program_id(0) == nt - 1)
        def _():
            ex.finish()

    def tile(col=0):
        return pl.BlockSpec((TMS, D), lambda i: (i, col))

    def vec(w):
        return pl.BlockSpec((1, w), lambda i: (0, 0))

    anys = pl.BlockSpec(memory_space=pl.ANY)
    f32o, b16o = jax.ShapeDtypeStruct((tp, D), F32), jax.ShapeDtypeStruct((tp, D), BF16)
    res = pl.pallas_call(
        body, name="mix_bwd", grid=(nt,),
        in_specs=[tile(), tile(3), tile(4), tile(), tile(), tile(), tile(), vec(2 * D), vec(D), vec(D), vec(D), anys, anys]
        + [anys] * nq,
        out_specs=[tile(), tile(), pl.BlockSpec((TMS, 2 * D), lambda i: (i, 0)), tile(), tile(), tile(),
                   pl.BlockSpec((8, 2 * D), lambda i: (0, 0))] + [anys] * nq,
        out_shape=[f32o, f32o, jax.ShapeDtypeStruct((tp, 2 * D), BF16), b16o, b16o, b16o,
                   jax.ShapeDtypeStruct((8, 2 * D), F32)] + ex.out_shape,
        scratch_shapes=[pltpu.VMEM((NDEV, 3, D // NDEV, D), BF16), pltpu.VMEM((NDEV, 4, PG // NDEV, PG), BF16),
                        pltpu.SemaphoreType.DMA((2,))] + ex.scratch,
        compiler_params=_params(("arbitrary",), 48),
    )(dh1, z, z, s, q, ac, m, b_gate, ln_g, ln_b, pool_scale, g_mixw, g_pool, *qs)
    return res[:7], res[7:]


def _seq_bwd(dac, dm, dzg, z, w_dw, seq, qs):
    tp = z.shape[0]
    nt = tp // TM
    ex = _ChipExchange(qs)
    nq = no = ex.n

    def body(*refs):
        dac_l, dac_c, dac_r, dm_l, dm_c, dm_r, av_l, av, av_r, ag_l, ag, ag_r, dzg_ref, w_ref = refs[:14]
        dz_ref, acc_ref = refs[14 + nq:16 + nq]
        a3, d3, m3, da3, dp3, w3, dw3, da_sc, dp_sc = refs[16 + nq + no:25 + nq + no]
        ex.bind(refs[14:14 + nq], refs[16 + nq:16 + nq + no], refs[25 + nq + no:])
        i = pl.program_id(0)
        sub = lax.broadcasted_iota(jnp.int32, (NCB, 128), 0)

        @pl.when(i == 0)
        def _():
            ex.issue()
            dw3[...] = jnp.zeros_like(dw3)
            _tm_fill(w3, 0, 4, lambda r, l: w_ref[pl.ds(r, 8), l])

        _tm_fill_ext(a3, (av_l, ag_l), (av, ag), (av_r, ag_r), lambda vg, r, l: vg[0][r, l] * _sig(vg[1][r, l]), unroll=2)
        _tm_fill_ext(d3, dac_l, dac_c, dac_r, lambda ref, r, l: ref[r, l])
        _tm_fill_ext(m3, dm_l, dm_c, dm_r, lambda ref, r, l: ref[r, l])

        def conv(g, c):
            dcur = [_tm_at(d3, 8 * g + t + HALO) for t in range(8)]
            accs = [None] * 8
            for k in range(CONV_K):
                wk = _tm_at(w3, k)
                prs = []
                for t in range(8):
                    term = wk * _tm_at(d3, 8 * g + t + CONV_K - k)
                    accs[t] = term if accs[t] is None else accs[t] + term
                    prs.append(dcur[t] * _tm_at(a3, 8 * g + t + k + 1))
                while len(prs) > 1:
                    prs = [prs[j] + prs[j + 1] for j in range(0, len(prs), 2)]
                dw3[_tm_rows(k), :] += prs[0]
            s = dcur[0]
            for t in range(1, 8):
                s = s + dcur[t]
            dw3[_tm_rows(CONV_K), :] += s
            for t in range(8):
                da3[_tm_rows(8 * g + t), :] = accs[t]
            return c

        lax.fori_loop(0, TM // 8, conv, 0)

        for b in _edge_rows(seq, tp):
            e = lax.rem(b - i * TM + HALO + tp, tp)

            @pl.when(e < TME)
            def _():
                m3[_tm_rows(e), :] = _tm_at(m3, e) * _edge_gain(b, seq, tp, sub)

        inv = _by_group(sub, [1.0 / w for w in POOL_WINDOWS])

        def pool(g, c):
            for t in range(8):
                e = 8 * g + t + HALO
                sums = _nested_windows(lambda o: _tm_at(m3, e + o), [w // 2 + 1 - w for w in POOL_WINDOWS])
                dp3[_tm_rows(8 * g + t), :] = _by_group(sub, sums) * inv
            return c

        lax.fori_loop(0, TM // 8, pool, 0)

        _tm_read(da3, TM // 8, lambda r, l, tile: da_sc.__setitem__((r, l), tile))
        _tm_read(dp3, TM // 8, lambda r, l, tile: dp_sc.__setitem__((r, l), tile))
        sg = _sig(ag[...])
        da = da_sc[...]
        dz_ref[:, 0:D] = (da * sg).astype(BF16)
        dz_ref[:, D:2 * D] = (da * av[...] * (sg * (1.0 - sg))).astype(BF16)
        dz_ref[:, 2 * D:3 * D] = (dp_sc[...] - dm_c[...]).astype(BF16)
        dz_ref[:, 3 * D:] = dzg_ref[...]

        @pl.when(i == nt - 1)
        def _():
            _tm_read(dw3, 4, lambda r, l, tile: acc_ref.__setitem__((r, l), tile))
            ex.finish()

    tmaj = pltpu.VMEM((TM * NCB, 128), F32)
    text = pltpu.VMEM((TME * NCB, 128), F32)
    taps = pltpu.VMEM((32 * NCB, 128), F32)
    anys = pl.BlockSpec(memory_space=pl.ANY)
    res = pl.pallas_call(
        body, name="seq_bwd", grid=(nt,),
        in_specs=_halo_specs(0, nt) + _halo_specs(0, nt) + _halo_specs(0, nt) + _halo_specs(1, nt)
        + [pl.BlockSpec((TM, 2 * D), lambda i: (i, 0)), pl.BlockSpec((32, D), lambda i: (0, 0))] + [anys] * nq,
        out_specs=[pl.BlockSpec((TM, DIN), lambda i: (i, 0)), pl.BlockSpec((32, D), lambda i: (0, 0))] + [anys] * no,
        out_shape=[jax.ShapeDtypeStruct((tp, DIN), BF16), jax.ShapeDtypeStruct((32, D), F32)] + ex.out_shape,
        scratch_shapes=[text, text, text, tmaj, tmaj, taps, taps, pltpu.VMEM((TM, D), F32), pltpu.VMEM((TM, D), F32)]
        + ex.scratch,
        compiler_params=_params(("arbitrary",), 48),
    )(dac, dac, dac, dm, dm, dm, z, z, z, z, z, z, dzg, w_dw, *qs)
    return res[:2], res[2:]


def _in_bwd(dz, h0, dh1, g_mix, w_g, seq, qs):
    tp = h0.shape[0]
    tm = _pick(tp, TM_IO)
    nt = tp // tm
    ex = _ChipExchange(qs)
    nq = no = ex.n

    def body(*refs):
        dz_ref, h_ref, dh1_ref, g_ref, w_hbm = refs[:5]
        gx_ref, gmeta_ref, acc_ref = refs[5 + nq:8 + nq]
        w_vm, sems = refs[8 + nq + no:10 + nq + no]
        ex.bind(refs[5:5 + nq], refs[8 + nq:8 + nq + no], refs[10 + nq + no:])
        i = pl.program_id(0)

        @pl.when(i == 0)
        def _():
            ex.issue()
            acc_ref[...] = jnp.zeros_like(acc_ref)

        _load_once(i == 0, _win_pairs(w_hbm, w_vm), sems)

        du = _dot_nt(dz_ref[:, :DIN // 2], w_vm[0]) + _dot_nt(dz_ref[:, DIN // 2:], w_vm[1])
        h = h_ref[...]
        r = lax.rsqrt(jnp.mean(h * h, axis=-1, keepdims=True) + RMS_EPS)
        n0 = h * r
        acc_ref[0:1, :] += jnp.sum(du * n0, axis=0, keepdims=True)
        dn = du * g_ref[...]
        gx_ref[...] = dh1_ref[...] + r * (dn - n0 * jnp.mean(dn * n0, axis=-1, keepdims=True))

        @pl.when(i == nt - 1)
        def _():
            gmeta_ref[...] = gx_ref[pl.ds(tm - N_META, N_META), :]
            ex.finish()

    tile = pl.BlockSpec((tm, D), lambda i: (i, 0))
    anys = pl.BlockSpec(memory_space=pl.ANY)
    res = pl.pallas_call(
        body, name="in_bwd", grid=(nt,),
        in_specs=[pl.BlockSpec((tm, DIN), lambda i: (i, 0)), tile, tile, pl.BlockSpec((1, D), lambda i: (0, 0)), anys]
        + [anys] * nq,
        out_specs=[tile, pl.BlockSpec((N_META, D), lambda i: (0, 0)), pl.BlockSpec((8, D), lambda i: (0, 0))] + [anys] * no,
        out_shape=[jax.ShapeDtypeStruct((seq, D), F32), jax.ShapeDtypeStruct((N_META, D), F32),
                   jax.ShapeDtypeStruct((8, D), F32)] + ex.out_shape,
        scratch_shapes=[pltpu.VMEM((2, D, DIN // 2), BF16), pltpu.SemaphoreType.DMA((NDEV,))] + ex.scratch,
        compiler_params=_params(("arbitrary",), 58),
    )(dz, h0, dh1, g_mix, w_g, *qs)
    return res[:3], res[3:]


def _wgrad_in(u, dz, qs):
    tp = u.shape[0]
    tm = _pick(tp, TM_WG)
    nt = tp // tm
    half = DIN // 2
    ex = _ChipExchange(qs)
    nq = ex.n

    def body(*refs):
        u_ref, dz_ref = refs[:2]
        o_ref, acc = refs[2 + nq], refs[3 + 2 * nq]
        ex.bind(refs[2:2 + nq], refs[3 + nq:3 + 2 * nq], refs[4 + 2 * nq:])
        h, t = pl.program_id(0), pl.program_id(1)

        @pl.when((h == 0) & (t == 0))
        def _():
            ex.issue()

        @pl.when(t == 0)
        def _():
            acc[...] = jnp.zeros_like(acc)

        acc[...] += _dot_tn(u_ref[...], dz_ref[...])

        @pl.when(t == nt - 1)
        def _():
            for d in range(4):
                o_ref[d] = acc[:, INB * d:INB * (d + 1)].astype(BF16)

        @pl.when((h == 1) & (t == nt - 1))
        def _():
            ex.finish()

    anys = pl.BlockSpec(memory_space=pl.ANY)
    res = pl.pallas_call(
        body, name="wgrad_in", grid=(2, nt),
        in_specs=[pl.BlockSpec((tm, D), lambda h, t: (t, 0)), pl.BlockSpec((tm, half), lambda h, t: (t, h))] + [anys] * nq,
        out_specs=[pl.BlockSpec((4, D, INB), lambda h, t: (h, 0, 0), pipeline_mode=pl.Buffered(1))] + [anys] * nq,
        out_shape=[jax.ShapeDtypeStruct((NDEV, D, INB), BF16)] + ex.out_shape,
        scratch_shapes=[pltpu.VMEM((D, half), F32)] + ex.scratch,
        compiler_params=_params(("arbitrary", "arbitrary"), 52),
    )(u, dz, *qs)
    return res[0], res[1:]


def _wgrad_mix(s, dyc, q, dyp, merged, dh1, m, dm2):
    tp = s.shape[0]
    tm = _pick(tp, TM_WM)
    nt = tp // tm
    rb = D // NDEV

    def body(s_ref, dyc_ref, q_ref, dyp_ref, mg_ref, dh1_ref, m_ref, dm2_ref, o_ref, op_ref, acc, accp):
        t = pl.program_id(0)

        @pl.when(t == 0)
        def _():
            acc[...] = jnp.zeros_like(acc)
            accp[...] = jnp.zeros_like(accp)

        acc[0] += _dot_tn(s_ref[...], dyc_ref[...])
        acc[1] += _dot_tn(q_ref[...], dyp_ref[...])
        acc[2] += _dot_tn(mg_ref[...], dh1_ref[...].astype(BF16))
        for g in range(4):
            accp[g] += _dot_tn(m_ref[:, g * PG:(g + 1) * PG], dm2_ref[:, g * PG:(g + 1) * PG])

        @pl.when(t == nt - 1)
        def _():
            for d in range(NDEV):
                for k in range(3):
                    o_ref[d, k] = acc[k, rb * d:rb * (d + 1), :].astype(BF16)
                for g in range(4):
                    op_ref[d, g] = accp[g, 32 * d:32 * (d + 1), :].astype(BF16)

    tile = pl.BlockSpec((tm, D), lambda t: (t, 0))
    return pl.pallas_call(
        body, name="wgrad_mix", grid=(nt,),
        in_specs=[tile] * 8,
        out_specs=[pl.BlockSpec((NDEV, 3, rb, D), lambda t: (0, 0, 0, 0), pipeline_mode=pl.Buffered(1)),
                   pl.BlockSpec((NDEV, 4, 32, PG), lambda t: (0, 0, 0, 0), pipeline_mode=pl.Buffered(1))],
        out_shape=[jax.ShapeDtypeStruct((NDEV, 3, rb, D), BF16), jax.ShapeDtypeStruct((NDEV, 4, 32, PG), BF16)],
        scratch_shapes=[pltpu.VMEM((3, D, D), F32), pltpu.VMEM((4, PG, PG), F32)],
        compiler_params=_params(("arbitrary",), 56),
    )(s, dyc, q, dyp, merged, dh1, m, dm2)


def _wgrad_gu(v, dfg, dfu):
    tp = v.shape[0]
    tm = _pick(tp, TM_WG)
    nt = tp // tm

    def body(v_ref, dg_ref, du_ref, o_ref, acc):
        k, t = pl.program_id(0), pl.program_id(2)

        @pl.when(t == 0)
        def _():
            acc[...] = jnp.zeros_like(acc)

        @pl.when(k == 0)
        def _():
            acc[...] += _dot_tn(dg_ref[...], v_ref[...])

        @pl.when(k == 1)
        def _():
            acc[...] += _dot_tn(du_ref[...], v_ref[...])

        @pl.when(t == nt - 1)
        def _():
            for d in range(4):
                o_ref[d] = acc[FFB * d:FFB * (d + 1), :].astype(BF16)

    return pl.pallas_call(
        body, name="wgrad_gu", grid=(2, 2, nt),
        in_specs=[pl.BlockSpec((tm, D), lambda k, h, t: (t, 0)),
                  pl.BlockSpec((tm, FFC), lambda k, h, t: (t * (1 - k), h * (1 - k))),
                  pl.BlockSpec((tm, FFC), lambda k, h, t: (t * k, h * k))],
        out_specs=pl.BlockSpec((4, None, FFB, D), lambda k, h, t: (h, k, 0, 0), pipeline_mode=pl.Buffered(1)),
        out_shape=jax.ShapeDtypeStruct((NDEV, 2, FFB, D), BF16),
        scratch_shapes=[pltpu.VMEM((FFC, D), F32)],
        compiler_params=_params(("arbitrary",) * 3, 48),
    )(v, dfg, dfu)


def _wgrad_down(f, dh2):
    tp = f.shape[0]
    tm = _pick(tp, TM_WG)
    nt = tp // tm

    def body(f_ref, d_ref, o_ref, acc):
        t = pl.program_id(1)

        @pl.when(t == 0)
        def _():
            acc[...] = jnp.zeros_like(acc)

        acc[...] += _dot_tn(f_ref[...], d_ref[...].astype(BF16))

        @pl.when(t == nt - 1)
        def _():
            for d in range(4):
                o_ref[d] = acc[FFB * d:FFB * (d + 1), :].astype(BF16)

    return pl.pallas_call(
        body, name="wgrad_down", grid=(2, nt),
        in_specs=[pl.BlockSpec((tm, FFC), lambda h, t: (t, h)), pl.BlockSpec((tm, D), lambda h, t: (t, 0))],
        out_specs=pl.BlockSpec((4, FFB, D), lambda h, t: (h, 0, 0), pipeline_mode=pl.Buffered(1)),
        out_shape=jax.ShapeDtypeStruct((NDEV, FFB, D), BF16),
        scratch_shapes=[pltpu.VMEM((FFC, D), F32)],
        compiler_params=_params(("arbitrary", "arbitrary"), 48),
    )(f, dh2)


def kernel(x, meta_tokens, g_mix, w_in, b_gate, w_dw, b_dw, ln_g, ln_b, w_conv_out, w_pool, pool_scale, w_pool_out, w_o, g_ffn, w_ffn_gate, w_ffn_up, w_ffn_down, g_final, loss_target, m_meta_tokens, m_g_mix, m_w_in, m_b_gate, m_w_dw, m_b_dw, m_ln_g, m_ln_b, m_w_conv_out, m_w_pool, m_pool_scale, m_w_pool_out, m_w_o, m_g_ffn, m_w_ffn_gate, m_w_ffn_up, m_w_ffn_down, m_g_final, v_meta_tokens, v_g_mix, v_w_in, v_b_gate, v_w_dw, v_b_dw, v_ln_g, v_ln_b, v_w_conv_out, v_w_pool, v_pool_scale, v_w_pool_out, v_w_o, v_g_ffn, v_w_ffn_gate, v_w_ffn_up, v_w_ffn_down, v_g_final):
    seq = x.shape[1]
    tp = -(-(seq + 2 * HALO) // TM) * TM
    tm_in = _pick(tp, TM_IO)
    nx_last = seq - (tp // tm_in - 1) * tm_in
    assert 0 < nx_last <= tm_in - 2 * HALO and nx_last % 8 == 0 and 0 < seq - (tp // TM - 1) * TM

    whole = (Ellipsis,)
    ag_small = _Gather(
        [((48, D // NDEV), [(meta_tokens, pl.ds(0, N_META), whole), (w_dw, pl.ds(N_META, CONV_K), 0)])], [F32])
    ag_mix = _Gather([((3, D // NDEV, D), [(w_conv_out, 0, 0), (w_pool_out, 1, 0), (w_o, 2, 0)]),
                      ((4, PG // NDEV, PG), [(w_pool, whole, 0)])], [BF16, BF16])
    def tr(a):
        return jnp.swapaxes(a, 1, 2)

    ag_gu = _Gather([((2, FFB, D), [(tr(w_ffn_gate), 0, 0), (tr(w_ffn_up), 1, 0)])], [BF16])
    ag_dn = _Gather([((FFB, D), [(w_ffn_down, whole, 0)])], [BF16])

    mx, my = lax.axis_index("x"), lax.axis_index("y")
    order = jnp.stack([2 * mx + my, 2 * mx + 1 - my, 2 * (1 - mx) + my, 2 * (1 - mx) + 1 - my]).astype(jnp.int32)
    (h0, z, u, g_in), (g_mixw, g_pool), g_small = _fwd_in(x[0], g_mix, w_in, order, tp, ag_mix, ag_small)
    wdw_full = g_small.transpose(1, 0, 2).reshape(48, D)[N_META:]
    (ac, m), (w_gu,) = _seq_fwd(z, wdw_full, b_dw, seq, ag_gu)
    (h1, s, merged, q), (g_down,) = _mix_fwd(ac, m, z, h0, b_gate, ln_g, ln_b, pool_scale, g_mixw, g_pool, ag_dn)
    w_dn = g_down.reshape(2, FFC, D)
    fg, fu, v, f, dh2, head_acc = _ffn_fwd(h1, loss_target[0], g_ffn, g_final.reshape(1, D), w_gu, w_dn)

    dfg, dfu, dh1, ffn_acc = _ffn_bwd(dh2, fg, fu, h1, g_ffn, w_gu, w_dn)
    own_f, sib_f, q_f = _rs_pair("rs_pair_ffn", [_wgrad_gu(v, dfg, dfu), _wgrad_down(f, dh2)])
    (dac, dm, dzg, dyc, dyp, dm2, mix_acc), rel_dn = _mix_bwd(
        dh1, z, s, q, ac, m, b_gate, ln_g, ln_b, pool_scale, g_mixw, g_pool, q_f[1:])
    p_mix = _wgrad_mix(s, dyc, q, dyp, merged, dh1, m, dm2)
    own_m, sib_m, q_m = _rs_pair("rs_pair_mix", list(p_mix))
    (dz, seq_acc), rel_gu = _seq_bwd(dac, dm, dzg, z, wdw_full, seq, q_f[:1])
    rel_f = [rel_gu[0], rel_dn[0]]
    p_in, rel_m = _wgrad_in(u, dz, q_m)
    own_i, sib_i, q_i = _rs_pair("rs_pair_in", [p_in])
    (grad_x, g_meta, in_acc), rel_i = _in_bwd(dz, h0, dh1, g_mix, g_in, seq, q_i)
    small_g = jnp.concatenate([g_meta, seq_acc[:CONV_K], jnp.zeros((1, D), F32)], axis=0)
    p_small = small_g.reshape(48, NDEV, D // NDEV).transpose(1, 0, 2).astype(BF16)
    rep_g = jnp.concatenate([
        in_acc[0:1], mix_acc[0:1, :D], mix_acc[0:1, D:], seq_acc[CONV_K:CONV_K + 1], mix_acc[1:2, :D], mix_acc[1:2, D:],
        mix_acc[2:3, :D], ffn_acc[0:1], head_acc[1:2], head_acc[0:1], jnp.zeros((REP_ROWS - 10, D), F32)], axis=0)
    own_s, sib_s, rel_s, rep_all = _reduce_scatter([p_small], rep_g)
    owns = [own_i[0], own_s[0], own_m[0], own_m[1], own_f[0], own_f[1]]
    sibs = [sib_i[0], sib_s[0], sib_m[0], sib_m[1], sib_f[0], sib_f[1]]
    rels = [rel_i[0], rel_s[0], rel_m[0], rel_m[1], rel_f[0], rel_f[1]]

    def lead(a):
        return a.reshape(1, *a.shape)

    def stack4(a, lead_dims):
        return a.reshape(*lead_dims, 1, 4 * 32, PG)

    (r_in,) = _adamw_multi("adamw_in", lead(owns[0]), sibs[0][:, None], rels[0][:, None], [w_in], [m_w_in], [v_w_in], 4)
    r_meta, r_dw = _adamw_meta_dw(owns[1], sibs[1], rels[1], (meta_tokens, m_meta_tokens, v_meta_tokens),
                                  (w_dw, m_w_dw, v_w_dw))
    r_conv, r_pout, r_o = _adamw_multi("adamw_mix", owns[2], sibs[2], rels[2], [w_conv_out, w_pool_out, w_o],
                                       [m_w_conv_out, m_w_pool_out, m_w_o], [v_w_conv_out, v_w_pool_out, v_w_o], 1)
    (r_pool,) = _adamw_multi("adamw_pool", stack4(owns[3], ()), stack4(sibs[3], (1,)), stack4(rels[3], (3,)),
                             [w_pool.reshape(1, 128, PG)], [m_w_pool.reshape(1, 128, PG)], [v_w_pool.reshape(1, 128, PG)], 1)
    r_pool = tuple(a.reshape(w_pool.shape) for a in r_pool)
    r_gate, r_up = _adamw_multi("adamw_gu", owns[4], sibs[4], rels[4], [tr(w_ffn_gate), tr(w_ffn_up)],
                                [tr(m_w_ffn_gate), tr(m_w_ffn_up)], [tr(v_w_ffn_gate), tr(v_w_ffn_up)], 2)
    r_gate, r_up = tuple(tr(a) for a in r_gate), tuple(tr(a) for a in r_up)
    (r_down,) = _adamw_multi("adamw_down", lead(owns[5]), sibs[5][:, None], rels[5][:, None],
                             [w_ffn_down], [m_w_ffn_down], [v_w_ffn_down], 2)
    row = (1, D)
    loss, reps = _adamw_rep(
        rep_all,
        [g_mix, b_gate, b_dw, ln_g, ln_b, pool_scale, g_ffn, g_final.reshape(row)],
        [m_g_mix, m_b_gate, m_b_dw, m_ln_g, m_ln_b, m_pool_scale, m_g_ffn, m_g_final.reshape(row)],
        [v_g_mix, v_b_gate, v_b_dw, v_ln_g, v_ln_b, v_pool_scale, v_g_ffn, v_g_final.reshape(row)])
    r_gmix, r_bg, r_bdw, r_lg, r_lb, r_ps, r_gffn, r_gfin = reps
    r_gfin = tuple(a.reshape(D) for a in r_gfin)

    in_order = [r_meta, r_gmix, r_in, r_bg, r_dw, r_bdw, r_lg, r_lb, r_conv, r_pool, r_ps, r_pout, r_o, r_gffn,
                r_gate, r_up, r_down, r_gfin]
    return (loss.reshape(()), grad_x[None], *[r[0] for r in in_order], *[r[1] for r in in_order],
            *[r[2] for r in in_order], *[r[3] for r in in_order])
```

```python
import math

import jax
import jax.numpy as jnp
from jax import lax
from jax.experimental import pallas as pl
from jax.experimental.pallas import tpu as pltpu

F32, BF16 = jnp.float32, jnp.bfloat16
MESH_ID = pl.DeviceIdType.MESH
NDEV = 8

D = 1024
N_META = 16
CONV_K = 31
HALO = 16
POOL_WINDOWS = (2, 4, 8, 16)
PG = 256
DIN = 5 * D
DFF = 2816
FFB = DFF // NDEV
FFC = DFF // 2
INB = DIN // NDEV
RMS_EPS = 1e-6
LN_EPS = 1e-5
ADAM_LR, ADAM_B1, ADAM_B2, ADAM_EPS, ADAM_WD, ADAM_STEP = 0.001, 0.9, 0.999, 1e-08, 0.01, 10

TM = 384
TMS = 384
TM_IO = 704
TM_WG = 1408
TM_WM = 704
MIB = 2 ** 20


def _sig(x):
    return 0.5 * jnp.tanh(0.5 * x) + 0.5


def _dot(a, b):
    return jnp.dot(a, b, preferred_element_type=F32)


def _dot_nt(a, b):
    return lax.dot_general(a, b, (((1,), (1,)), ((), ())), preferred_element_type=F32)


def _dot_tn(a, b):
    return lax.dot_general(a, b, (((0,), (0,)), ((), ())), preferred_element_type=F32)


def _pick(tp, pref):
    return pref if tp % pref == 0 else TM


def _params(sem, vmem_mib):
    return pltpu.CompilerParams(dimension_semantics=sem, vmem_limit_bytes=vmem_mib * MIB)


def _load_once(first, pairs, sems):
    @pl.when(first)
    def _():
        cps = [pltpu.make_async_copy(s, d, sems.at[k]) for k, (s, d) in enumerate(pairs)]
        for cp in cps:
            cp.start()
        for cp in cps:
            cp.wait()


def _place():
    x, y, c = lax.axis_index("x"), lax.axis_index("y"), lax.axis_index("c")
    return x, y, c


class _Gather:
    def __init__(self, groups, dtypes):
        self.groups, self.dtypes, self.n = groups, dtypes, len(groups)
        self.arrays = [a for _, parts in groups for a, _, _ in parts]
        self.out_shape = [jax.ShapeDtypeStruct((NDEV, *s), dt) for (s, _), dt in zip(groups, dtypes)]
        self.scratch = [pltpu.VMEM(s, dt) for (s, _), dt in zip(groups, dtypes)] + [
            pltpu.SemaphoreType.DMA((7 * self.n,)), pltpu.SemaphoreType.DMA((7 * self.n,)),
            pltpu.SemaphoreType.DMA((self.n,))]

    def bind(self, ins, outs, scratch):
        self.ins, self.outs, self.stages = ins, outs, scratch[:self.n]
        self.send_sems, self.recv_sems, self.local_sems = scratch[self.n:]
        return self

    def _copy(self, w, k, block, to, src=None):
        dst = self.outs[w].at[4 * block[0] + 2 * block[1] + block[2]]
        return pltpu.make_async_remote_copy(
            src_ref=dst if src is None else src, dst_ref=dst,
            send_sem=self.send_sems.at[7 * w + k], recv_sem=self.recv_sems.at[7 * w + k],
            device_id=to, device_id_type=MESH_ID)

    def _first(self):
        x, y, c = _place()
        me, sibling = (x, y, c), (x, y, 1 - c)
        chips = [(1 - x, y), (x, 1 - y), (1 - x, 1 - y)]
        mine, first = [], []
        for w in range(self.n):
            mine.append(pltpu.make_async_copy(self.stages[w], self.outs[w].at[4 * x + 2 * y + c], self.local_sems.at[w]))
            first.append(self._copy(w, 0, me, sibling, src=self.stages[w]))
            first += [self._copy(w, 1 + j, me, (*chip, c), src=self.stages[w]) for j, chip in enumerate(chips)]
        return mine, first

    def _passed(self):
        x, y, c = _place()
        chips = [(1 - x, y), (x, 1 - y), (1 - x, 1 - y)]
        return [self._copy(w, 4 + j, (*chip, c), (x, y, 1 - c)) for w in range(self.n) for j, chip in enumerate(chips)]

    def issue(self):
        a = 0
        for w in range(self.n):
            shape, parts = self.groups[w]
            if sum(arr.size for arr, _, _ in parts) < math.prod(shape):
                self.stages[w][...] = jnp.zeros(shape, self.dtypes[w])
            for _, dst, src in parts:
                self.stages[w][dst] = self.ins[a][src].astype(self.dtypes[w])
                a += 1
        mine, first = self._first()
        for cp in mine + first:
            cp.start()

    def forward(self):
        x, y, c = _place()
        chips = [(1 - x, y), (x, 1 - y), (1 - x, 1 - y)]
        passed = self._passed()
        for w in range(self.n):
            for j, chip in enumerate(chips):
                self._copy(w, 1 + j, (*chip, c), (x, y, c)).wait_recv()
                passed[3 * w + j].start()

    def finish(self):
        x, y, c = _place()
        chips = [(1 - x, y), (x, 1 - y), (1 - x, 1 - y)]
        for w in range(self.n):
            self._copy(w, 0, (x, y, 1 - c), (x, y, c)).wait_recv()
            for j, chip in enumerate(chips):
                self._copy(w, 4 + j, (*chip, 1 - c), (x, y, c)).wait_recv()
        mine, first = self._first()
        for cp in first + self._passed():
            cp.wait_send()
        for cp in mine:
            cp.wait()


class _ChipExchange:
    def __init__(self, qs):
        self.n = len(qs)
        self.out_shape = [jax.ShapeDtypeStruct(q.shape, q.dtype) for q in qs]
        self.scratch = [pltpu.SemaphoreType.DMA((3 * self.n,)), pltpu.SemaphoreType.DMA((3 * self.n,))]

    def bind(self, qs, rels, scratch):
        self.qs, self.rels = qs, rels
        self.send_sems, self.recv_sems = scratch
        return self

    def _copies(self):
        x, y, c = _place()
        chips = [(1 - x, y), (x, 1 - y), (1 - x, 1 - y)]
        return [pltpu.make_async_remote_copy(
            src_ref=self.qs[w].at[j], dst_ref=self.rels[w].at[j],
            send_sem=self.send_sems.at[3 * w + j], recv_sem=self.recv_sems.at[3 * w + j],
            device_id=(*chips[j], c), device_id_type=MESH_ID) for w in range(self.n) for j in range(3)]

    def issue(self):
        for cp in self._copies():
            cp.start()

    def finish(self):
        cps = self._copies()
        for cp in cps:
            cp.wait_recv()
        for cp in cps:
            cp.wait_send()


def _reduce_scatter(parts, small):
    n = len(parts)
    blks = [p.shape[1:] for p in parts]

    def body(*refs):
        ps, small_ref = refs[:n], refs[n]
        o = n + 1
        owns, sibs, rels, small_out = refs[o:o + n], refs[o + n:o + 2 * n], refs[o + 2 * n:o + 3 * n], refs[o + 3 * n]
        o += 3 * n + 1
        pa, pb, qst = refs[o:o + n], refs[o + n:o + 2 * n], refs[o + 2 * n:o + 3 * n]
        s1_send, s1_recv, s2_send, s2_recv, sm_send, sm_recv, lsem = refs[o + 3 * n:]
        x, y, c = _place()
        me = 4 * x + 2 * y + c
        sibling = (x, y, 1 - c)
        chips = [(1 - x, y), (x, 1 - y), (1 - x, 1 - y)]
        all_chips = [(x, y)] + chips

        own_cps = []
        for w in range(n):
            cp = pltpu.make_async_copy(ps[w].at[me], owns[w], lsem.at[w])
            cp.start()
            own_cps.append(cp)
        sm_own = pltpu.make_async_copy(small_ref, small_out.at[me], lsem.at[n])
        sm_own.start()

        def small_copy(r):
            peer = ((x + (r >> 2)) % 2, (y + ((r >> 1) & 1)) % 2, (c + (r & 1)) % 2)
            return pltpu.make_async_remote_copy(
                src_ref=small_ref, dst_ref=small_out.at[me], send_sem=sm_send.at[r - 1], recv_sem=sm_recv.at[r - 1],
                device_id=peer, device_id_type=MESH_ID)

        sm_cps = [small_copy(r) for r in range(1, NDEV)]
        for cp in sm_cps:
            cp.start()

        def pair_copy(w, rel):
            cx, cy = all_chips[rel]
            return pltpu.make_async_remote_copy(
                src_ref=ps[w].at[4 * cx + 2 * cy + (1 - c)], dst_ref=sibs[w].at[rel],
                send_sem=s1_send.at[4 * w + rel], recv_sem=s1_recv.at[4 * w + rel],
                device_id=sibling, device_id_type=MESH_ID)

        def chip_copy(w, j):
            return pltpu.make_async_remote_copy(
                src_ref=qst[w].at[j], dst_ref=rels[w].at[j],
                send_sem=s2_send.at[3 * w + j], recv_sem=s2_recv.at[3 * w + j],
                device_id=(*chips[j], c), device_id_type=MESH_ID)

        pair_cps = [pair_copy(w, rel) for w in range(n) for rel in (1, 2, 3, 0)]
        for cp in pair_cps:
            cp.start()
        chip_cps = []
        for w in range(n):
            for j, (cx, cy) in enumerate(chips):
                pair_copy(w, 1 + j).wait_recv()
                la = pltpu.make_async_copy(ps[w].at[4 * cx + 2 * cy + c], pa[w], lsem.at[n + 1])
                lb = pltpu.make_async_copy(sibs[w].at[1 + j], pb[w], lsem.at[n + 2])
                la.start()
                lb.start()
                la.wait()
                lb.wait()
                qst[w][j] = (pa[w][...].astype(F32) + pb[w][...].astype(F32)).astype(BF16)
                cp = chip_copy(w, j)
                cp.start()
                chip_cps.append(cp)
        for w in range(n):
            pair_copy(w, 0).wait_recv()
            for j in range(3):
                chip_copy(w, j).wait_recv()
        for cp in sm_cps:
            cp.wait_recv()
        for cp in pair_cps + chip_cps + sm_cps:
            cp.wait_send()
        for cp in own_cps:
            cp.wait()
        sm_own.wait()

    any_spec = pl.BlockSpec(memory_space=pl.ANY)
    outs = pl.pallas_call(
        body, name="rs_grads",
        out_shape=[jax.ShapeDtypeStruct(b, BF16) for b in blks]
        + [jax.ShapeDtypeStruct((4, *b), BF16) for b in blks]
        + [jax.ShapeDtypeStruct((3, *b), BF16) for b in blks]
        + [jax.ShapeDtypeStruct((NDEV, *small.shape), F32)],
        in_specs=[any_spec] * (n + 1),
        out_specs=[any_spec] * (3 * n + 1),
        scratch_shapes=[pltpu.VMEM(b, BF16) for b in blks] + [pltpu.VMEM(b, BF16) for b in blks]
        + [pltpu.VMEM((3, *b), BF16) for b in blks]
        + [pltpu.SemaphoreType.DMA((4 * n,)), pltpu.SemaphoreType.DMA((4 * n,)),
           pltpu.SemaphoreType.DMA((3 * n,)), pltpu.SemaphoreType.DMA((3 * n,)),
           pltpu.SemaphoreType.DMA((NDEV - 1,)), pltpu.SemaphoreType.DMA((NDEV - 1,)),
           pltpu.SemaphoreType.DMA((n + 3,))],
        compiler_params=pltpu.CompilerParams(vmem_limit_bytes=40 * MIB),
    )(*parts, small)
    return outs[:n], outs[n:2 * n], outs[2 * n:3 * n], outs[3 * n]


class _PairSum:
    def __init__(self, parts, keep_q=True):
        self.n = n = len(parts)
        self.keep_q = keep_q
        blks = [p.shape[1:] for p in parts]
        self.out_shape = [jax.ShapeDtypeStruct(b, BF16) for b in blks] + [jax.ShapeDtypeStruct((1, *b), BF16) for b in blks]
        if keep_q:
            self.out_shape += [jax.ShapeDtypeStruct((3, *b), BF16) for b in blks]
        self.scratch = [pltpu.VMEM((3, *b), BF16) for b in blks] * 3 + [
            pltpu.SemaphoreType.DMA((4 * n,)), pltpu.SemaphoreType.DMA((4 * n,)), pltpu.SemaphoreType.DMA((5 * n,))]

    def bind(self, ps, outs, scratch):
        n = self.n
        self.ps, self.owns, self.sibs, self.qs = ps, outs[:n], outs[n:2 * n], outs[2 * n:]
        self.pa, self.pb, self.qst = scratch[:n], scratch[n:2 * n], scratch[2 * n:3 * n]
        self.s_send, self.s_recv, self.lsem = scratch[3 * n:]
        return self

    def _local(self, with_q):
        n = self.n
        x, y, c = _place()
        chips = [(1 - x, y), (x, 1 - y), (1 - x, 1 - y)]
        own = [pltpu.make_async_copy(self.ps[w].at[4 * x + 2 * y + c], self.owns[w], self.lsem.at[w]) for w in range(n)]
        mine = [[pltpu.make_async_copy(self.ps[w].at[4 * cx + 2 * cy + c], self.pa[w].at[j], self.lsem.at[2 * n + 3 * w + j])
                 for j, (cx, cy) in enumerate(chips)] for w in range(n)]
        outq = [pltpu.make_async_copy(self.qst[w], self.qs[w], self.lsem.at[n + w]) for w in range(n)] if with_q else []
        return own, mine, outq

    def _pair(self, w, rel):
        x, y, c = _place()
        cx, cy = [(x, y), (1 - x, y), (x, 1 - y), (1 - x, 1 - y)][rel]
        return pltpu.make_async_remote_copy(
            src_ref=self.ps[w].at[4 * cx + 2 * cy + (1 - c)],
            dst_ref=self.sibs[w].at[0] if rel == 0 else self.pb[w].at[rel - 1],
            send_sem=self.s_send.at[4 * w + rel], recv_sem=self.s_recv.at[4 * w + rel],
            device_id=(x, y, 1 - c), device_id_type=MESH_ID)

    def issue(self):
        own, mine, _ = self._local(False)
        for cp in own + [cp for row in mine for cp in row]:
            cp.start()
        for w in range(self.n):
            for rel in (1, 2, 3, 0):
                self._pair(w, rel).start()

    def finish(self):
        own, mine, outq = self._local(self.keep_q)
        for w in range(self.n):
            for j in range(3):
                self._pair(w, 1 + j).wait_recv()
                mine[w][j].wait()
                self.qst[w][j] = (self.pa[w][j].astype(F32) + self.pb[w][j].astype(F32)).astype(BF16)
            if self.keep_q:
                outq[w].start()
        for w in range(self.n):
            self._pair(w, 0).wait_recv()
        for w in range(self.n):
            for rel in range(4):
                self._pair(w, rel).wait_send()
        for cp in own + outq:
            cp.wait()

    def results(self, outs):
        n = self.n
        return outs[:n], outs[n:2 * n], outs[2 * n:3 * n]


def _rs_pair(name, parts):
    ps = _PairSum(parts)
    n = ps.n

    def body(*refs):
        ps.bind(refs[:n], refs[n:4 * n], refs[4 * n:])
        ps.issue()
        ps.finish()

    any_spec = pl.BlockSpec(memory_space=pl.ANY)
    outs = pl.pallas_call(
        body, name=name, out_shape=ps.out_shape,
        in_specs=[any_spec] * n, out_specs=[any_spec] * (3 * n), scratch_shapes=ps.scratch,
        compiler_params=pltpu.CompilerParams(vmem_limit_bytes=48 * MIB),
    )(*parts)
    return ps.results(outs)


def _adamw_math(g, w, m, v):
    m = ADAM_B1 * m + (1.0 - ADAM_B1) * g
    v = ADAM_B2 * v + (1.0 - ADAM_B2) * (g * g)
    m_hat = m / (1.0 - ADAM_B1 ** ADAM_STEP)
    v_hat = v / (1.0 - ADAM_B2 ** ADAM_STEP)
    delta = -ADAM_LR * (m_hat / (jnp.sqrt(v_hat) + ADAM_EPS) + ADAM_WD * w)
    return delta, m, v


def _adamw_multi(name, own, sib, rel, ws, ms, vs, row_grid):
    k_n, r_n, c_n = own.shape
    rbk = r_n // row_grid

    def body(*refs):
        own_ref, sib_ref, r0_ref, r1_ref, r2_ref = refs[:5]
        w_refs, m_refs, v_refs = refs[5:5 + k_n], refs[5 + k_n:5 + 2 * k_n], refs[5 + 2 * k_n:5 + 3 * k_n]
        outs = refs[5 + 3 * k_n:]
        for k in range(k_n):
            g = own_ref[k].astype(F32) + sib_ref[k].astype(F32)
            g = g + r0_ref[k].astype(F32)
            g = g + r1_ref[k].astype(F32)
            g = g + r2_ref[k].astype(F32)
            delta, mm, vv = _adamw_math(g, w_refs[k][0], m_refs[k][0], v_refs[k][0])
            outs[4 * k][0] = g
            outs[4 * k + 1][0] = delta
            outs[4 * k + 2][0] = mm
            outs[4 * k + 3][0] = vv

    def lead(j):
        return pl.BlockSpec((None, k_n, rbk, c_n), lambda g: (j, 0, g, 0))

    wspec = pl.BlockSpec((1, rbk, c_n), lambda g: (0, g, 0))
    shp = jax.ShapeDtypeStruct((1, r_n, c_n), F32)
    res = pl.pallas_call(
        body, name=name, grid=(row_grid,),
        in_specs=[pl.BlockSpec((k_n, rbk, c_n), lambda g: (0, g, 0)), lead(0), lead(0), lead(1), lead(2)] + [wspec] * (3 * k_n),
        out_specs=[wspec] * (4 * k_n), out_shape=[shp] * (4 * k_n),
        compiler_params=_params(("arbitrary",), 40),
    )(own, sib, rel, rel, rel, *ws, *ms, *vs)
    return [tuple(res[4 * k:4 * k + 4]) for k in range(k_n)]


def _adamw_meta_dw(own, sib, rel, meta, dw):
    def body(own_ref, sib_ref, rel_ref, wm, mm, vm, wd, md, vd, *outs):
        def gsum(rows):
            g = own_ref[rows, :].astype(F32) + sib_ref[0, rows, :].astype(F32)
            for j in range(3):
                g = g + rel_ref[j, rows, :].astype(F32)
            return g

        g = gsum(pl.ds(0, N_META))
        delta, m2, v2 = _adamw_math(g, wm[...], mm[...], vm[...])
        for o, val in zip(outs[:4], (g, delta, m2, v2)):
            o[...] = val
        g = gsum(pl.ds(N_META, CONV_K))
        delta, m2, v2 = _adamw_math(g, wd[0], md[0], vd[0])
        for o, val in zip(outs[4:], (g, delta, m2, v2)):
            o[0] = val

    s_meta = jax.ShapeDtypeStruct(meta[0].shape, F32)
    s_dw = jax.ShapeDtypeStruct(dw[0].shape, F32)
    res = pl.pallas_call(body, name="adamw_meta_dw", out_shape=[s_meta] * 4 + [s_dw] * 4)(own, sib, rel, *meta, *dw)
    return tuple(res[:4]), tuple(res[4:])


REP_ROWS = 16


def _adamw_rep(gathered, ws, ms, vs):
    rows = [(0, 1), (1, 2), (3, 1), (4, 1), (5, 1), (6, 1), (7, 1), (8, 1)]

    def body(g_ref, *refs):
        w_refs, m_refs, v_refs = refs[:8], refs[8:16], refs[16:24]
        loss_ref, outs, acc = refs[24], refs[25:57], refs[57]
        g = g_ref[0]
        for d in range(1, NDEV):
            g = g + g_ref[d]
        acc[...] = g
        loss_ref[...] = (0.5 / D) * jnp.sum(acc[pl.ds(9, 1), :], axis=1, keepdims=True)
        for p, (r0, nr) in enumerate(rows):
            for h in range(nr):
                cols = pl.ds(h * D, D)
                gp = acc[pl.ds(r0 + h, 1), :]
                delta, mm, vv = _adamw_math(gp, w_refs[p][:, cols], m_refs[p][:, cols], v_refs[p][:, cols])
                for o, val in zip(outs[4 * p:4 * p + 4], (gp, delta, mm, vv)):
                    o[:, cols] = val

    shapes = [jax.ShapeDtypeStruct(w.shape, F32) for w in ws]
    res = pl.pallas_call(
        body, name="adamw_rep",
        out_shape=[jax.ShapeDtypeStruct((1, 1), F32)] + [s for s in shapes for _ in range(4)],
        scratch_shapes=[pltpu.VMEM((REP_ROWS, D), F32)],
    )(gathered, *ws, *ms, *vs)
    return res[0], [tuple(res[1 + 4 * p:5 + 4 * p]) for p in range(8)]


def _load_ffn(i, j, wgu_hbm, wgu, wdn_hbm, wdn, sems):
    half = NDEV // 2

    def copies(ch):
        pairs = [(wgu_hbm.at[half * ch + d, g], wgu.at[g, ch, pl.ds(FFB * d, FFB), :]) for g in range(2) for d in range(half)]
        pairs.append((wdn_hbm.at[ch], wdn.at[ch]))
        return [pltpu.make_async_copy(s, t, sems.at[(2 * half + 1) * ch + k]) for k, (s, t) in enumerate(pairs)]

    @pl.when((i == 0) & (j == 0))
    def _():
        for cp in copies(0) + copies(1):
            cp.start()

    for ch in range(2):
        @pl.when((i == 0) & (j == ch))
        def _():
            for cp in copies(ch):
                cp.wait()


def _win_pairs(w_hbm, w_vm):
    return [(w_hbm.at[q], w_vm.at[q // 2, :, pl.ds(2 * INB * (q % 2), 2 * INB)]) for q in range(4)]


def _whole(a):
    nd = a.ndim
    return pl.BlockSpec(a.shape, lambda *g: (0,) * nd)


CHIPW = 2 * INB
PHASE_CHIP = (1, 0, 2)
assert PHASE_CHIP[2] == 2


class _GatherIn:
    scratch = [pltpu.VMEM((D, INB), BF16), pltpu.SemaphoreType.DMA((7,)), pltpu.SemaphoreType.DMA((7,)),
               pltpu.SemaphoreType.DMA((1,))]

    def bind(self, w_ref, w_vm, scratch):
        self.w_ref, self.w_vm = w_ref, w_vm
        self.stage, self.send_sems, self.recv_sems, self.local_sem = scratch
        return self

    def _win(self, chip, core):
        return self.w_vm.at[2 * chip[0] + chip[1], core]

    def _copy(self, k, chip, core, to, src=None):
        dst = self._win(chip, core)
        return pltpu.make_async_remote_copy(
            src_ref=dst if src is None else src, dst_ref=dst, send_sem=self.send_sems.at[k],
            recv_sem=self.recv_sems.at[k], device_id=to, device_id_type=MESH_ID)

    def _mine(self, cs):
        x, y, _ = _place()
        return pltpu.make_async_copy(self.stage, self._win((x, y), cs), self.local_sem.at[0])

    def issue(self, cs):
        x, y, _ = _place()
        chips = [(1 - x, y), (x, 1 - y), (1 - x, 1 - y)]
        self.stage[...] = self.w_ref[0].astype(BF16)
        self._mine(cs).start()
        self._copy(0, (x, y), cs, (x, y, 1 - cs), src=self.stage).start()
        for j in PHASE_CHIP[:2]:
            self._copy(1 + j, (x, y), cs, (*chips[j], cs), src=self.stage).start()

    def wait_chip(self, phase, cs):
        x, y, _ = _place()
        chips = [(1 - x, y), (x, 1 - y), (1 - x, 1 - y)]
        if phase == 0:
            self._mine(cs).wait()
            self._copy(0, (x, y), 1 - cs, (x, y, cs)).wait_recv()
            return
        j = PHASE_CHIP[phase - 1]
        self._copy(1 + j, chips[j], cs, (x, y, cs)).wait_recv()
        self._copy(4 + j, chips[j], cs, (x, y, 1 - cs)).start()
        if phase == 1:
            self._copy(3, (x, y), cs, (*chips[2], cs), src=self.stage).start()
        self._copy(4 + j, chips[j], 1 - cs, (x, y, cs)).wait_recv()

    def finish(self, cs):
        x, y, _ = _place()
        for k in range(7):
            self._copy(k, (x, y), cs, (x, y, cs), src=self.stage).wait_send()


def _fwd_in(x2, g_mix, w_in, order, tp, ag, ags):
    tm = _pick(tp, TM_IO)
    nt = tp // tm
    nx_last = x2.shape[0] - (nt - 1) * tm
    na, ng, ns = len(ag.arrays), ag.n, len(ags.arrays)
    gin = _GatherIn()

    def body(order_ref, *refs):
        x_ref, g_ref, w_ref = refs[:3]
        o = 3 + na + ns
        h_ref, z_ref, u_ref, wout_ref = refs[o:o + 4]
        s = o + 4 + ng + 1
        w_vm, u_all, osem, sm_vm = refs[s:s + 4]
        gin.bind(w_ref, w_vm, refs[s + 4:s + 8])
        ag.bind(refs[3:3 + na], refs[o + 4:o + 4 + ng], refs[s + 8:s + 8 + len(ag.scratch)])
        ags.bind(refs[3 + na:3 + na + ns], refs[o + 4 + ng:o + 5 + ng], refs[s + 8 + len(ag.scratch):])
        ph, i = pl.program_id(0), pl.program_id(1)
        core = lax.axis_index("c")
        first = (ph == 0) & (i == 0)
        last = (ph == 3) & (i == nt - 1)
        @pl.when(first)
        def _():
            ags.issue()

        for cs in range(2):
            @pl.when(first & (core == cs))
            def _():
                gin.issue(cs)

        @pl.when((ph == 0) & (i == max(nt - 2, 0)))
        def _():
            ags.forward()

        for cs in range(2):
            for p in range(4):
                @pl.when((ph == p) & (i == 0) & (core == cs))
                def _():
                    gin.wait_chip(p, cs)

        @pl.when((ph == 2) & (i == 0))
        def _():
            ag.issue()

        out_copies = [pltpu.make_async_copy(w_vm.at[k, c], wout_ref.at[k, :, pl.ds(INB * c, INB)], osem.at[2 * k + c])
                      for k in range(4) for c in range(2)]

        @pl.when((ph == 3) & (i == 0))
        def _():
            for cp in out_copies:
                cp.start()

        @pl.when((ph == 0) & (i < nt - 1))
        def _():
            h_ref[...] = x_ref[...]

        @pl.when((ph == 0) & (i == nt - 1))
        def _():
            ags.finish()
            cp = pltpu.make_async_copy(ags.outs[0], sm_vm, osem.at[8])
            cp.start()
            h_ref[pl.ds(0, nx_last), :] = x_ref[pl.ds(0, nx_last), :]
            h_ref[pl.ds(nx_last, tm - nx_last - N_META), :] = jnp.zeros((tm - nx_last - N_META, D), F32)
            cp.wait()
            for d in range(NDEV):
                h_ref[pl.ds(tm - N_META, N_META), pl.ds(128 * d, 128)] = sm_vm[d, pl.ds(0, N_META), :]

        @pl.when(ph == 0)
        def _():
            xv = h_ref[...]
            r = lax.rsqrt(jnp.mean(xv * xv, axis=-1, keepdims=True) + RMS_EPS)
            u = (xv * r * g_ref[...]).astype(BF16)
            u_ref[...] = u
            u_all[i] = u

        for c in range(2):
            z_ref[:, INB * c:INB * (c + 1)] = _dot(u_all[i], w_vm[order_ref[ph], c])

        @pl.when(last)
        def _():
            ag.forward()
            ag.finish()
            for cp in out_copies:
                cp.wait()

        for cs in range(2):
            @pl.when(last & (core == cs))
            def _():
                gin.finish(cs)

    def rows(ph, i, order):
        return (jnp.where(ph == 0, i, nt - 1), 0)

    tile = pl.BlockSpec((tm, D), rows)
    anys = pl.BlockSpec(memory_space=pl.ANY)
    res = pl.pallas_call(
        body, name="fwd_in",
        grid_spec=pltpu.PrefetchScalarGridSpec(
            num_scalar_prefetch=1, grid=(4, nt),
            in_specs=[tile, pl.BlockSpec((1, D), lambda ph, i, order: (0, 0)), _whole(w_in)]
            + [_whole(a) for a in ag.arrays + ags.arrays],
            out_specs=[tile, pl.BlockSpec((tm, CHIPW), lambda ph, i, order: (i, order[ph])), tile, anys] + [anys] * (ng + 1),
            scratch_shapes=[pltpu.VMEM((4, 2, D, INB), BF16), pltpu.VMEM((nt, tm, D), BF16), pltpu.SemaphoreType.DMA((9,)),
                            pltpu.VMEM(ags.out_shape[0].shape, F32)] + gin.scratch + ag.scratch + ags.scratch),
        out_shape=[jax.ShapeDtypeStruct((tp, D), F32), jax.ShapeDtypeStruct((tp, DIN), F32),
                   jax.ShapeDtypeStruct((tp, D), BF16), jax.ShapeDtypeStruct((4, D, CHIPW), BF16)]
        + ag.out_shape + ags.out_shape,
        compiler_params=_params(("arbitrary", "arbitrary"), 58),
    )(order, x2, g_mix, w_in, *ag.arrays, *ags.arrays)
    return res[:4], res[4:4 + ng], res[4 + ng]


def _halo_specs(col, nt, width=D):
    r = TM // HALO
    nb = nt * r
    return [pl.BlockSpec((HALO, width), lambda i: ((i * r + nb - 1) % nb, col)),
            pl.BlockSpec((TM, width), lambda i: (i, col)),
            pl.BlockSpec((HALO, width), lambda i: (((i + 1) * r) % nb, col))]


NCB = D // 128
TME = TM + 2 * HALO
CONV_STEPS = 16
assert TM % CONV_STEPS == 0


def _tm_fill(dst, time0, groups, tile_fn, unroll=1):
    def body(g, c):
        for j in range(NCB):
            dst[pl.ds((time0 + 8 * g) * NCB + j, 8, stride=NCB), :] = tile_fn(pl.multiple_of(8 * g, 8), pl.ds(128 * j, 128))
        return c

    lax.fori_loop(0, groups, body, 0, unroll=unroll)


def _tm_fill_ext(dst, left, cur, right, fn, unroll=1):
    _tm_fill(dst, 0, HALO // 8, lambda r, l: fn(left, pl.ds(r, 8), l), unroll)
    _tm_fill(dst, HALO, TM // 8, lambda r, l: fn(cur, pl.ds(r, 8), l), unroll)
    _tm_fill(dst, HALO + TM, HALO // 8, lambda r, l: fn(right, pl.ds(r, 8), l), unroll)


def _tm_read(src, groups, store_fn):
    def body(g, c):
        for j in range(NCB):
            store_fn(pl.ds(pl.multiple_of(8 * g, 8), 8), pl.ds(128 * j, 128), src[pl.ds(8 * g * NCB + j, 8, stride=NCB), :])
        return c

    lax.fori_loop(0, groups, body, 0, unroll=2)


def _tm_rows(t):
    return pl.ds(t * NCB if isinstance(t, int) else pl.multiple_of(t * NCB, NCB), NCB)


def _tm_at(ref, t):
    return ref[_tm_rows(t), :]


def _by_group(sub, vals):
    return jnp.where(sub < 2, vals[0], jnp.where(sub < 4, vals[1], jnp.where(sub < 6, vals[2], vals[3])))


def _pool_cnt(b, seq, tp, sub):
    b = jnp.where(b < 0, b + tp, b)
    b = jnp.where(b >= tp, b - tp, b)
    t = jnp.where(b < seq, b + N_META, b - (tp - N_META))
    cnts = []
    for win in POOL_WINDOWS:
        left = win // 2
        lo = jnp.maximum(t - left, 0)
        hi = jnp.minimum(t + win - left, seq + N_META)
        cnts.append(jnp.maximum(hi - lo, 1).astype(F32))
    return _by_group(sub, cnts)


def _edge_rows(seq, tp):
    reach = max(POOL_WINDOWS) // 2
    return [tp - N_META + t for t in range(reach)] + [seq - reach + 1 + t for t in range(reach - 1)]


def _edge_gain(b, seq, tp, sub):
    return _by_group(sub, [float(w) for w in POOL_WINDOWS]) / _pool_cnt(b, seq, tp, sub)


def _nested_windows(at, lo_offs):
    sums, s, have = [], None, set()
    for g, win in enumerate(POOL_WINDOWS):
        for o in range(lo_offs[g], lo_offs[g] + win):
            if o not in have:
                have.add(o)
                s = at(o) if s is None else s + at(o)
        sums.append(s)
    return sums


def _seq_fwd(z, w_dw, b_dw, seq, gat):
    tp = z.shape[0]
    nt = tp // TM
    na, ng = len(gat.arrays), gat.n

    def body(*refs):
        av_l, av, av_r, ag_l, ag, ag_r, p_l, p, p_r, w_ref, b_ref = refs[:11]
        ac_ref, m_ref = refs[11 + na:13 + na]
        a3, p3, o3, m3, w3, b3, m2d = refs[13 + na + ng:20 + na + ng]
        gat.bind(refs[11:11 + na], refs[13 + na:13 + na + ng], refs[20 + na + ng:])
        i = pl.program_id(0)
        sub = lax.broadcasted_iota(jnp.int32, (NCB, 128), 0)

        @pl.when(i == 0)
        def _():
            gat.issue()
            _tm_fill(w3, 0, 4, lambda r, l: w_ref[pl.ds(r, 8), l])
            for j in range(NCB):
                b3[pl.ds(j, 1), :] = b_ref[:, pl.ds(128 * j, 128)]

        @pl.when(i == max(nt - 2, 0))
        def _():
            gat.forward()

        _tm_fill_ext(a3, (av_l, ag_l), (av, ag), (av_r, ag_r), lambda vg, r, l: vg[0][r, l] * _sig(vg[1][r, l]), unroll=2)
        _tm_fill_ext(p3, p_l, p, p_r, lambda ref, r, l: ref[r, l])

        def conv(g, c):
            accs = [b3[...]] * 16
            for k in range(CONV_K):
                wk = _tm_at(w3, k)
                for t in range(16):
                    accs[t] = accs[t] + wk * _tm_at(a3, 16 * g + t + k + 1)
            for t in range(16):
                o3[_tm_rows(16 * g + t), :] = accs[t]
            return c

        lax.fori_loop(0, TM // 16, conv, 0)
        _tm_read(o3, TM // 8, lambda r, l, tile: ac_ref.__setitem__((r, l), tile))

        inv = _by_group(sub, [1.0 / w for w in POOL_WINDOWS])

        def pool(g, c):
            for t in range(8):
                e = 8 * g + t + HALO
                sums = _nested_windows(lambda o: _tm_at(p3, e + o), [-(w // 2) for w in POOL_WINDOWS])
                m3[_tm_rows(8 * g + t), :] = _by_group(sub, sums) * inv - _tm_at(p3, e)
            return c

        lax.fori_loop(0, TM // 8, pool, 0)
        for b in _edge_rows(seq, tp):
            r = b - i * TM

            @pl.when((r >= 0) & (r < TM))
            def _():
                pv = _tm_at(p3, r + HALO)
                m3[_tm_rows(r), :] = (_tm_at(m3, r) + pv) * _edge_gain(b, seq, tp, sub) - pv

        _tm_read(m3, TM // 8, lambda r, l, tile: m2d.__setitem__((r, l), tile))
        m_ref[...] = m2d[...].astype(BF16)

        @pl.when(i == nt - 1)
        def _():
            gat.finish()

    tmaj = pltpu.VMEM((TM * NCB, 128), F32)
    text = pltpu.VMEM((TME * NCB, 128), F32)
    res = pl.pallas_call(
        body, name="seq_fwd", grid=(nt,),
        in_specs=_halo_specs(0, nt) + _halo_specs(1, nt) + _halo_specs(2, nt)
        + [pl.BlockSpec((32, D), lambda i: (0, 0)), pl.BlockSpec((1, D), lambda i: (0, 0))] + [_whole(a) for a in gat.arrays],
        out_specs=[pl.BlockSpec((TM, D), lambda i: (i, 0))] * 2 + [pl.BlockSpec(memory_space=pl.ANY)] * ng,
        out_shape=[jax.ShapeDtypeStruct((tp, D), F32), jax.ShapeDtypeStruct((tp, D), BF16)] + gat.out_shape,
        scratch_shapes=[text, text, tmaj, tmaj, pltpu.VMEM((32 * NCB, 128), F32), pltpu.VMEM((NCB, 128), F32),
                        pltpu.VMEM((TM, D), F32)] + gat.scratch,
        compiler_params=_params(("arbitrary",), 52),
    )(z, z, z, z, z, z, z, z, z, w_dw, b_dw, *gat.arrays)
    return res[:2], res[2:]


def _ln_stats(ac):
    mu = jnp.mean(ac, axis=-1, keepdims=True)
    xc = ac - mu
    rl = lax.rsqrt(jnp.mean(xc * xc, axis=-1, keepdims=True) + LN_EPS)
    return xc * rl, rl


def _pool_mix(m, wp_ref):
    return jnp.concatenate(
        [_dot(m[:, g * PG:(g + 1) * PG], wp_ref[:, g].reshape(PG, PG)) for g in range(4)], axis=1)


def _mix_fwd(ac, m, z, h0, b_gate, ln_g, ln_b, pool_scale, g_mixw, g_pool, gat):
    tp = h0.shape[0]
    tms = TM
    nt = tp // tms
    na, ng = len(gat.arrays), gat.n

    def body(*refs):
        ac_ref, m_ref, zga, zgb, h_ref, bg_ref, lg_ref, lb_ref, ps_ref, wm_hbm, wp_hbm = refs[:11]
        h1_ref, s_ref, mg_ref, q_ref = refs[11 + na:15 + na]
        wm, wp, sems = refs[15 + na + ng:18 + na + ng]
        gat.bind(refs[11:11 + na], refs[15 + na:15 + na + ng], refs[18 + na + ng:])
        i = pl.program_id(0)

        @pl.when(i == 0)
        def _():
            gat.issue()

        @pl.when(i == max(nt - 4, 0))
        def _():
            gat.forward()

        @pl.when(i == nt - 1)
        def _():
            gat.finish()

        _load_once(i == 0, [(wm_hbm, wm), (wp_hbm, wp)], sems)
        n, _ = _ln_stats(ac_ref[...])
        l = n * lg_ref[...] + lb_ref[...]
        s = (l * _sig(l)).astype(BF16)
        s_ref[...] = s
        yc = _dot(s, wm[:, 0].reshape(D, D))
        q = (_pool_mix(m_ref[...], wp) * ps_ref[...]).astype(BF16)
        q_ref[...] = q
        yp = _dot(q, wm[:, 1].reshape(D, D))
        ga = _sig(zga[...] + bg_ref[:, :D])
        gb = _sig(zgb[...] + bg_ref[:, D:])
        merged = (ga * yc + gb * yp).astype(BF16)
        mg_ref[...] = merged
        h1_ref[...] = h_ref[...] + _dot(merged, wm[:, 2].reshape(D, D))

    def tile(col=0):
        return pl.BlockSpec((tms, D), lambda i: (i, col))

    def vec(w):
        return pl.BlockSpec((1, w), lambda i: (0, 0))

    anys = pl.BlockSpec(memory_space=pl.ANY)
    f32o, b16o = jax.ShapeDtypeStruct((tp, D), F32), jax.ShapeDtypeStruct((tp, D), BF16)
    res = pl.pallas_call(
        body, name="mix_fwd", grid=(nt,),
        in_specs=[tile(), tile(), tile(3), tile(4), tile(), vec(2 * D), vec(D), vec(D), vec(D), anys, anys]
        + [_whole(a) for a in gat.arrays],
        out_specs=[tile()] * 4 + [anys] * ng,
        out_shape=[f32o, b16o, b16o, b16o] + gat.out_shape,
        scratch_shapes=[pltpu.VMEM((NDEV, 3, D // NDEV, D), BF16), pltpu.VMEM((NDEV, 4, PG // NDEV, PG), BF16),
                        pltpu.SemaphoreType.DMA((2,))] + gat.scratch,
        compiler_params=_params(("arbitrary",), 52),
    )(ac, m, z, z, h0, b_gate, ln_g, ln_b, pool_scale, g_mixw, g_pool, *gat.arrays)
    return res[:4], res[4:]


def _ffn_fwd(h1, tgt, g_ffn, g_final, w_gu, w_dn):
    tp = h1.shape[0]
    nt = tp // TM
    nx_last = tgt.shape[0] - (nt - 1) * TM

    def body(h_ref, t_ref, gf_ref, gl_ref, wgu_hbm, wdn_hbm,
             fg_ref, fu_ref, v_ref, f_ref, dh2_ref, acc_ref, wgu, wdn, v_sc, h2_sc, diff_sc, sems):
        i, j = pl.program_id(0), pl.program_id(1)
        _load_ffn(i, j, wgu_hbm, wgu, wdn_hbm, wdn, sems)

        @pl.when((i == 0) & (j == 0))
        def _():
            acc_ref[...] = jnp.zeros_like(acc_ref)

        @pl.when(j == 0)
        def _():
            h = h_ref[...]
            r = lax.rsqrt(jnp.mean(h * h, axis=-1, keepdims=True) + RMS_EPS)
            v = (h * r * gf_ref[...]).astype(BF16)
            v_sc[...] = v
            v_ref[...] = v
            h2_sc[...] = h

        v = v_sc[...]
        fg = _dot_nt(v, wgu[0, j])
        fu = _dot_nt(v, wgu[1, j])
        fg_ref[...] = fg
        fu_ref[...] = fu
        f = ((fg * _sig(fg)) * fu).astype(BF16)
        f_ref[...] = f
        h2_sc[...] += _dot(f, wdn[j])

        @pl.when(j == 1)
        def _():
            h2 = h2_sc[...]
            r = lax.rsqrt(jnp.mean(h2 * h2, axis=-1, keepdims=True) + RMS_EPS)
            n2 = h2 * r
            y = n2 * gl_ref[...]

            @pl.when(i < nt - 1)
            def _():
                diff_sc[...] = y - t_ref[...]

            @pl.when(i == nt - 1)
            def _():
                diff_sc[pl.ds(0, nx_last), :] = y[:nx_last] - t_ref[pl.ds(0, nx_last), :]
                diff_sc[pl.ds(nx_last, TM - nx_last), :] = jnp.zeros((TM - nx_last, D), F32)

            diff = diff_sc[...]
            dy = diff * (1.0 / D)
            acc_ref[0:1, :] += jnp.sum(diff * diff, axis=0, keepdims=True)
            acc_ref[1:2, :] += jnp.sum(dy * n2, axis=0, keepdims=True)
            dn = dy * gl_ref[...]
            dh2_ref[...] = r * (dn - n2 * jnp.mean(dn * n2, axis=-1, keepdims=True))

    def tile():
        return pl.BlockSpec((TM, D), lambda i, j: (i, 0))

    def chunk():
        return pl.BlockSpec((TM, FFC), lambda i, j: (i, j))

    def vec():
        return pl.BlockSpec((1, D), lambda i, j: (0, 0))

    anys = pl.BlockSpec(memory_space=pl.ANY)
    hid32, hid16 = jax.ShapeDtypeStruct((tp, DFF), F32), jax.ShapeDtypeStruct((tp, DFF), BF16)
    return pl.pallas_call(
        body, name="ffn_fwd", grid=(nt, 2),
        in_specs=[tile(), tile(), vec(), vec(), anys, anys],
        out_specs=[chunk(), chunk(), tile(), chunk(), tile(), pl.BlockSpec((8, D), lambda i, j: (0, 0))],
        out_shape=[hid32, hid32, jax.ShapeDtypeStruct((tp, D), BF16), hid16, jax.ShapeDtypeStruct((tp, D), F32),
                   jax.ShapeDtypeStruct((8, D), F32)],
        scratch_shapes=[pltpu.VMEM((2, 2, FFC, D), BF16), pltpu.VMEM((2, FFC, D), BF16),
                        pltpu.VMEM((TM, D), BF16), pltpu.VMEM((TM, D), F32), pltpu.VMEM((TM, D), F32),
                        pltpu.SemaphoreType.DMA((2 * NDEV + 2,))],
        compiler_params=_params(("arbitrary", "arbitrary"), 56),
    )(h1, tgt, g_ffn, g_final, w_gu, w_dn)


def _ffn_bwd(dh2, fg, fu, h1, g_ffn, w_gu, w_dn):
    tp = h1.shape[0]
    nt = tp // TM

    def body(dh2_ref, fg_ref, fu_ref, h_ref, gf_ref, wgu_hbm, wdn_hbm,
             dfg_ref, dfu_ref, dh1_ref, acc_ref, wgu, wdn, d_sc, dv_sc, sems):
        i, j = pl.program_id(0), pl.program_id(1)
        _load_ffn(i, j, wgu_hbm, wgu, wdn_hbm, wdn, sems)

        @pl.when((i == 0) & (j == 0))
        def _():
            acc_ref[...] = jnp.zeros_like(acc_ref)

        @pl.when(j == 0)
        def _():
            d_sc[...] = dh2_ref[...].astype(BF16)
            dv_sc[...] = jnp.zeros_like(dv_sc)

        df = _dot_nt(d_sc[...], wdn[j])
        fg = fg_ref[...]
        sg = _sig(fg)
        dfu = (df * (fg * sg)).astype(BF16)
        dfg = (df * fu_ref[...] * (sg * (1.0 + fg * (1.0 - sg)))).astype(BF16)
        dfg_ref[...] = dfg
        dfu_ref[...] = dfu
        dv_sc[...] += _dot(dfg, wgu[0, j]) + _dot(dfu, wgu[1, j])

        @pl.when(j == 1)
        def _():
            h = h_ref[...]
            r = lax.rsqrt(jnp.mean(h * h, axis=-1, keepdims=True) + RMS_EPS)
            n1 = h * r
            dv = dv_sc[...]
            acc_ref[0:1, :] += jnp.sum(dv * n1, axis=0, keepdims=True)
            dn = dv * gf_ref[...]
            dh1_ref[...] = dh2_ref[...] + r * (dn - n1 * jnp.mean(dn * n1, axis=-1, keepdims=True))

    def tile():
        return pl.BlockSpec((TM, D), lambda i, j: (i, 0))

    def chunk():
        return pl.BlockSpec((TM, FFC), lambda i, j: (i, j))

    anys = pl.BlockSpec(memory_space=pl.ANY)
    hid16 = jax.ShapeDtypeStruct((tp, DFF), BF16)
    return pl.pallas_call(
        body, name="ffn_bwd", grid=(nt, 2),
        in_specs=[tile(), chunk(), chunk(), tile(), pl.BlockSpec((1, D), lambda i, j: (0, 0)), anys, anys],
        out_specs=[chunk(), chunk(), tile(), pl.BlockSpec((8, D), lambda i, j: (0, 0))],
        out_shape=[hid16, hid16, jax.ShapeDtypeStruct((tp, D), F32), jax.ShapeDtypeStruct((8, D), F32)],
        scratch_shapes=[pltpu.VMEM((2, 2, FFC, D), BF16), pltpu.VMEM((2, FFC, D), BF16),
                        pltpu.VMEM((TM, D), BF16), pltpu.VMEM((TM, D), F32), pltpu.SemaphoreType.DMA((2 * NDEV + 2,))],
        compiler_params=_params(("arbitrary", "arbitrary"), 56),
    )(dh2, fg, fu, h1, g_ffn, w_gu, w_dn)


def _mix_bwd(dh1, z, s, q, ac, m, b_gate, ln_g, ln_b, pool_scale, g_mixw, g_pool, qs):
    tp = dh1.shape[0]
    nt = tp // TMS
    ex = _ChipExchange(qs)
    nq = ex.n

    def body(*refs):
        dh1_ref, zga, zgb, s_ref, q_ref, ac_ref, m_ref, bg_ref, lg_ref, lb_ref, ps_ref, wm_hbm, wp_hbm = refs[:13]
        dac_ref, dm_ref, dzg_ref, dyc_ref, dyp_ref, dm2_ref, acc_ref = refs[13 + nq:20 + nq]
        wm, wp, sems = refs[20 + 2 * nq:23 + 2 * nq]
        ex.bind(refs[13:13 + nq], refs[20 + nq:20 + 2 * nq], refs[23 + 2 * nq:])
        first = pl.program_id(0) == 0

        @pl.when(first)
        def _():
            ex.issue()
            acc_ref[...] = jnp.zeros_like(acc_ref)

        _load_once(first, [(wm_hbm, wm), (wp_hbm, wp)], sems)

        dmerged = _dot_nt(dh1_ref[...].astype(BF16), wm[:, 2].reshape(D, D))
        ga = _sig(zga[...] + bg_ref[:, :D])
        gb = _sig(zgb[...] + bg_ref[:, D:])
        dyc = dmerged * ga
        dyp = dmerged * gb
        dza = (dmerged * _dot(s_ref[...], wm[:, 0].reshape(D, D))) * (ga * (1.0 - ga))
        dzb = (dmerged * _dot(q_ref[...], wm[:, 1].reshape(D, D))) * (gb * (1.0 - gb))
        dzg_ref[:, :D] = dza.astype(BF16)
        dzg_ref[:, D:] = dzb.astype(BF16)
        acc_ref[0:1, :D] += jnp.sum(dza, axis=0, keepdims=True)
        acc_ref[0:1, D:] += jnp.sum(dzb, axis=0, keepdims=True)
        dyc_b = dyc.astype(BF16)
        dyp_b = dyp.astype(BF16)
        dyc_ref[...] = dyc_b
        dyp_ref[...] = dyp_b
        ds = _dot_nt(dyc_b, wm[:, 0].reshape(D, D))
        n, rl = _ln_stats(ac_ref[...])
        l = n * lg_ref[...] + lb_ref[...]
        sg = _sig(l)
        dl = ds * (sg * (1.0 + l * (1.0 - sg)))
        acc_ref[1:2, :D] += jnp.sum(dl * n, axis=0, keepdims=True)
        acc_ref[1:2, D:] += jnp.sum(dl, axis=0, keepdims=True)
        dn = dl * lg_ref[...]
        dac_ref[...] = rl * (dn - jnp.mean(dn, axis=-1, keepdims=True) - n * jnp.mean(dn * n, axis=-1, keepdims=True))
        dq = _dot_nt(dyp_b, wm[:, 1].reshape(D, D))
        mv = m_ref[...]
        acc_ref[2:3, :D] += jnp.sum(dq * _pool_mix(mv, wp), axis=0, keepdims=True)
        dm2 = (dq * ps_ref[...]).astype(BF16)
        dm2_ref[...] = dm2
        dm_ref[...] = jnp.concatenate(
            [_dot_nt(dm2[:, g * PG:(g + 1) * PG], wp[:, g].reshape(PG, PG)) for g in range(4)], axis=1)

        @pl.when(pl.program_id(0) == nt - 1)
        def _():
            ex.finish()

    def tile(col=0):
        return pl.BlockSpec((TMS, D), lambda i: (i, col))

    def vec(w):
        return pl.BlockSpec((1, w), lambda i: (0, 0))

    anys = pl.BlockSpec(memory_space=pl.ANY)
    f32o, b16o = jax.ShapeDtypeStruct((tp, D), F32), jax.ShapeDtypeStruct((tp, D), BF16)
    res = pl.pallas_call(
        body, name="mix_bwd", grid=(nt,),
        in_specs=[tile(), tile(3), tile(4), tile(), tile(), tile(), tile(), vec(2 * D), vec(D), vec(D), vec(D), anys, anys]
        + [anys] * nq,
        out_specs=[tile(), tile(), pl.BlockSpec((TMS, 2 * D), lambda i: (i, 0)), tile(), tile(), tile(),
                   pl.BlockSpec((8, 2 * D), lambda i: (0, 0))] + [anys] * nq,
        out_shape=[f32o, f32o, jax.ShapeDtypeStruct((tp, 2 * D), BF16), b16o, b16o, b16o,
                   jax.ShapeDtypeStruct((8, 2 * D), F32)] + ex.out_shape,
        scratch_shapes=[pltpu.VMEM((NDEV, 3, D // NDEV, D), BF16), pltpu.VMEM((NDEV, 4, PG // NDEV, PG), BF16),
                        pltpu.SemaphoreType.DMA((2,))] + ex.scratch,
        compiler_params=_params(("arbitrary",), 48),
    )(dh1, z, z, s, q, ac, m, b_gate, ln_g, ln_b, pool_scale, g_mixw, g_pool, *qs)
    return res[:7], res[7:]


def _seq_bwd(dac, dm, dzg, z, w_dw, seq, qs):
    tp = z.shape[0]
    nt = tp // TM
    ex = _ChipExchange(qs)
    nq = no = ex.n

    def body(*refs):
        dac_l, dac_c, dac_r, dm_l, dm_c, dm_r, av_l, av, av_r, ag_l, ag, ag_r, dzg_ref, w_ref = refs[:14]
        dz_ref, acc_ref = refs[14 + nq:16 + nq]
        a3, d3, m3, da3, dp3, w3, dw3, da_sc, dp_sc = refs[16 + nq + no:25 + nq + no]
        ex.bind(refs[14:14 + nq], refs[16 + nq:16 + nq + no], refs[25 + nq + no:])
        i = pl.program_id(0)
        sub = lax.broadcasted_iota(jnp.int32, (NCB, 128), 0)

        @pl.when(i == 0)
        def _():
            ex.issue()
            dw3[...] = jnp.zeros_like(dw3)
            _tm_fill(w3, 0, 4, lambda r, l: w_ref[pl.ds(r, 8), l])

        _tm_fill_ext(a3, (av_l, ag_l), (av, ag), (av_r, ag_r), lambda vg, r, l: vg[0][r, l] * _sig(vg[1][r, l]), unroll=2)
        _tm_fill_ext(d3, dac_l, dac_c, dac_r, lambda ref, r, l: ref[r, l])
        _tm_fill_ext(m3, dm_l, dm_c, dm_r, lambda ref, r, l: ref[r, l])

        def conv(g, c):
            t0 = CONV_STEPS * g
            dcur = [_tm_at(d3, t0 + t + HALO) for t in range(CONV_STEPS)]
            accs = [None] * CONV_STEPS
            for k in range(CONV_K):
                wk = _tm_at(w3, k)
                prs = []
                for t in range(CONV_STEPS):
                    term = wk * _tm_at(d3, t0 + t + CONV_K - k)
                    accs[t] = term if accs[t] is None else accs[t] + term
                    prs.append(dcur[t] * _tm_at(a3, t0 + t + k + 1))
                while len(prs) > 1:
                    prs = [prs[j] + prs[j + 1] for j in range(0, len(prs) - 1, 2)] + prs[len(prs) - len(prs) % 2:]
                dw3[_tm_rows(k), :] += prs[0]
            s = dcur[0]
            for t in range(1, CONV_STEPS):
                s = s + dcur[t]
            dw3[_tm_rows(CONV_K), :] += s
            for t in range(CONV_STEPS):
                da3[_tm_rows(t0 + t), :] = accs[t]
            return c

        lax.fori_loop(0, TM // CONV_STEPS, conv, 0)

        for b in _edge_rows(seq, tp):
            e = lax.rem(b - i * TM + HALO + tp, tp)

            @pl.when(e < TME)
            def _():
                m3[_tm_rows(e), :] = _tm_at(m3, e) * _edge_gain(b, seq, tp, sub)

        inv = _by_group(sub, [1.0 / w for w in POOL_WINDOWS])

        def pool(g, c):
            for t in range(8):
                e = 8 * g + t + HALO
                sums = _nested_windows(lambda o: _tm_at(m3, e + o), [w // 2 + 1 - w for w in POOL_WINDOWS])
                dp3[_tm_rows(8 * g + t), :] = _by_group(sub, sums) * inv
            return c

        lax.fori_loop(0, TM // 8, pool, 0)

        _tm_read(da3, TM // 8, lambda r, l, tile: da_sc.__setitem__((r, l), tile))
        _tm_read(dp3, TM // 8, lambda r, l, tile: dp_sc.__setitem__((r, l), tile))
        sg = _sig(ag[...])
        da = da_sc[...]
        dz_ref[:, 0:D] = (da * sg).astype(BF16)
        dz_ref[:, D:2 * D] = (da * av[...] * (sg * (1.0 - sg))).astype(BF16)
        dz_ref[:, 2 * D:3 * D] = (dp_sc[...] - dm_c[...]).astype(BF16)
        dz_ref[:, 3 * D:] = dzg_ref[...]

        @pl.when(i == nt - 1)
        def _():
            _tm_read(dw3, 4, lambda r, l, tile: acc_ref.__setitem__((r, l), tile))
            ex.finish()

    tmaj = pltpu.VMEM((TM * NCB, 128), F32)
    text = pltpu.VMEM((TME * NCB, 128), F32)
    taps = pltpu.VMEM((32 * NCB, 128), F32)
    anys = pl.BlockSpec(memory_space=pl.ANY)
    res = pl.pallas_call(
        body, name="seq_bwd", grid=(nt,),
        in_specs=_halo_specs(0, nt) + _halo_specs(0, nt) + _halo_specs(0, nt) + _halo_specs(1, nt)
        + [pl.BlockSpec((TM, 2 * D), lambda i: (i, 0)), pl.BlockSpec((32, D), lambda i: (0, 0))] + [anys] * nq,
        out_specs=[pl.BlockSpec((TM, DIN), lambda i: (i, 0)), pl.BlockSpec((32, D), lambda i: (0, 0))] + [anys] * no,
        out_shape=[jax.ShapeDtypeStruct((tp, DIN), BF16), jax.ShapeDtypeStruct((32, D), F32)] + ex.out_shape,
        scratch_shapes=[text, text, text, tmaj, tmaj, taps, taps, pltpu.VMEM((TM, D), F32), pltpu.VMEM((TM, D), F32)]
        + ex.scratch,
        compiler_params=_params(("arbitrary",), 48),
    )(dac, dac, dac, dm, dm, dm, z, z, z, z, z, z, dzg, w_dw, *qs)
    return res[:2], res[2:]


def _in_bwd(dz, h0, dh1, g_mix, w_g, seq, qs):
    tp = h0.shape[0]
    tm = _pick(tp, TM_IO)
    nt = tp // tm
    ex = _ChipExchange(qs)
    nq = no = ex.n

    def body(*refs):
        dz_ref, h_ref, dh1_ref, g_ref, w_hbm = refs[:5]
        gx_ref, gmeta_ref, acc_ref = refs[5 + nq:8 + nq]
        w_vm, sems = refs[8 + nq + no:10 + nq + no]
        ex.bind(refs[5:5 + nq], refs[8 + nq:8 + nq + no], refs[10 + nq + no:])
        i = pl.program_id(0)

        @pl.when(i == 0)
        def _():
            ex.issue()
            acc_ref[...] = jnp.zeros_like(acc_ref)

        _load_once(i == 0, _win_pairs(w_hbm, w_vm), sems)

        du = _dot_nt(dz_ref[:, :DIN // 2], w_vm[0]) + _dot_nt(dz_ref[:, DIN // 2:], w_vm[1])
        h = h_ref[...]
        r = lax.rsqrt(jnp.mean(h * h, axis=-1, keepdims=True) + RMS_EPS)
        n0 = h * r
        acc_ref[0:1, :] += jnp.sum(du * n0, axis=0, keepdims=True)
        dn = du * g_ref[...]
        gx_ref[...] = dh1_ref[...] + r * (dn - n0 * jnp.mean(dn * n0, axis=-1, keepdims=True))

        @pl.when(i == nt - 1)
        def _():
            gmeta_ref[...] = gx_ref[pl.ds(tm - N_META, N_META), :]
            ex.finish()

    tile = pl.BlockSpec((tm, D), lambda i: (i, 0))
    anys = pl.BlockSpec(memory_space=pl.ANY)
    res = pl.pallas_call(
        body, name="in_bwd", grid=(nt,),
        in_specs=[pl.BlockSpec((tm, DIN), lambda i: (i, 0)), tile, tile, pl.BlockSpec((1, D), lambda i: (0, 0)), anys]
        + [anys] * nq,
        out_specs=[tile, pl.BlockSpec((N_META, D), lambda i: (0, 0)), pl.BlockSpec((8, D), lambda i: (0, 0))] + [anys] * no,
        out_shape=[jax.ShapeDtypeStruct((seq, D), F32), jax.ShapeDtypeStruct((N_META, D), F32),
                   jax.ShapeDtypeStruct((8, D), F32)] + ex.out_shape,
        scratch_shapes=[pltpu.VMEM((2, D, DIN // 2), BF16), pltpu.SemaphoreType.DMA((NDEV,))] + ex.scratch,
        compiler_params=_params(("arbitrary",), 58),
    )(dz, h0, dh1, g_mix, w_g, *qs)
    return res[:3], res[3:]


def _wgrad_in(u, dz, qs):
    tp = u.shape[0]
    tm = _pick(tp, TM_WG)
    nt = tp // tm
    half = DIN // 2
    ex = _ChipExchange(qs)
    nq = ex.n

    def body(*refs):
        u_ref, dz_ref = refs[:2]
        o_ref, acc = refs[2 + nq], refs[3 + 2 * nq]
        ex.bind(refs[2:2 + nq], refs[3 + nq:3 + 2 * nq], refs[4 + 2 * nq:])
        h, t = pl.program_id(0), pl.program_id(1)

        @pl.when((h == 0) & (t == 0))
        def _():
            ex.issue()

        @pl.when(t == 0)
        def _():
            acc[...] = jnp.zeros_like(acc)

        acc[...] += _dot_tn(u_ref[...], dz_ref[...])

        @pl.when(t == nt - 1)
        def _():
            for d in range(4):
                o_ref[d] = acc[:, INB * d:INB * (d + 1)].astype(BF16)

        @pl.when((h == 1) & (t == nt - 1))
        def _():
            ex.finish()

    anys = pl.BlockSpec(memory_space=pl.ANY)
    res = pl.pallas_call(
        body, name="wgrad_in", grid=(2, nt),
        in_specs=[pl.BlockSpec((tm, D), lambda h, t: (t, 0)), pl.BlockSpec((tm, half), lambda h, t: (t, h))] + [anys] * nq,
        out_specs=[pl.BlockSpec((4, D, INB), lambda h, t: (h, 0, 0), pipeline_mode=pl.Buffered(1))] + [anys] * nq,
        out_shape=[jax.ShapeDtypeStruct((NDEV, D, INB), BF16)] + ex.out_shape,
        scratch_shapes=[pltpu.VMEM((D, half), F32)] + ex.scratch,
        compiler_params=_params(("arbitrary", "arbitrary"), 52),
    )(u, dz, *qs)
    return res[0], res[1:]


def _wgrad_mix(s, dyc, q, dyp, merged, dh1, m, dm2):
    tp = s.shape[0]
    tm = _pick(tp, TM_WM)
    nt = tp // tm
    rb = D // NDEV

    def body(s_ref, dyc_ref, q_ref, dyp_ref, mg_ref, dh1_ref, m_ref, dm2_ref, o_ref, op_ref, acc, accp):
        t = pl.program_id(0)

        @pl.when(t == 0)
        def _():
            acc[...] = jnp.zeros_like(acc)
            accp[...] = jnp.zeros_like(accp)

        acc[0] += _dot_tn(s_ref[...], dyc_ref[...])
        acc[1] += _dot_tn(q_ref[...], dyp_ref[...])
        acc[2] += _dot_tn(mg_ref[...], dh1_ref[...].astype(BF16))
        for g in range(4):
            accp[g] += _dot_tn(m_ref[:, g * PG:(g + 1) * PG], dm2_ref[:, g * PG:(g + 1) * PG])

        @pl.when(t == nt - 1)
        def _():
            for d in range(NDEV):
                for k in range(3):
                    o_ref[d, k] = acc[k, rb * d:rb * (d + 1), :].astype(BF16)
                for g in range(4):
                    op_ref[d, g] = accp[g, 32 * d:32 * (d + 1), :].astype(BF16)

    tile = pl.BlockSpec((tm, D), lambda t: (t, 0))
    return pl.pallas_call(
        body, name="wgrad_mix", grid=(nt,),
        in_specs=[tile] * 8,
        out_specs=[pl.BlockSpec((NDEV, 3, rb, D), lambda t: (0, 0, 0, 0), pipeline_mode=pl.Buffered(1)),
                   pl.BlockSpec((NDEV, 4, 32, PG), lambda t: (0, 0, 0, 0), pipeline_mode=pl.Buffered(1))],
        out_shape=[jax.ShapeDtypeStruct((NDEV, 3, rb, D), BF16), jax.ShapeDtypeStruct((NDEV, 4, 32, PG), BF16)],
        scratch_shapes=[pltpu.VMEM((3, D, D), F32), pltpu.VMEM((4, PG, PG), F32)],
        compiler_params=_params(("arbitrary",), 56),
    )(s, dyc, q, dyp, merged, dh1, m, dm2)


def _wgrad_gu(v, dfg, dfu):
    tp = v.shape[0]
    tm = _pick(tp, TM_WG)
    nt = tp // tm

    def body(v_ref, dg_ref, du_ref, o_ref, acc):
        k, t = pl.program_id(0), pl.program_id(2)

        @pl.when(t == 0)
        def _():
            acc[...] = jnp.zeros_like(acc)

        @pl.when(k == 0)
        def _():
            acc[...] += _dot_tn(dg_ref[...], v_ref[...])

        @pl.when(k == 1)
        def _():
            acc[...] += _dot_tn(du_ref[...], v_ref[...])

        @pl.when(t == nt - 1)
        def _():
            for d in range(4):
                o_ref[d] = acc[FFB * d:FFB * (d + 1), :].astype(BF16)

    return pl.pallas_call(
        body, name="wgrad_gu", grid=(2, 2, nt),
        in_specs=[pl.BlockSpec((tm, D), lambda k, h, t: (t, 0)),
                  pl.BlockSpec((tm, FFC), lambda k, h, t: (t * (1 - k), h * (1 - k))),
                  pl.BlockSpec((tm, FFC), lambda k, h, t: (t * k, h * k))],
        out_specs=pl.BlockSpec((4, None, FFB, D), lambda k, h, t: (h, k, 0, 0), pipeline_mode=pl.Buffered(1)),
        out_shape=jax.ShapeDtypeStruct((NDEV, 2, FFB, D), BF16),
        scratch_shapes=[pltpu.VMEM((FFC, D), F32)],
        compiler_params=_params(("arbitrary",) * 3, 48),
    )(v, dfg, dfu)


def _wgrad_down(f, dh2):
    tp = f.shape[0]
    tm = _pick(tp, TM_WG)
    nt = tp // tm

    def body(f_ref, d_ref, o_ref, acc):
        t = pl.program_id(1)

        @pl.when(t == 0)
        def _():
            acc[...] = jnp.zeros_like(acc)

        acc[...] += _dot_tn(f_ref[...], d_ref[...].astype(BF16))

        @pl.when(t == nt - 1)
        def _():
            for d in range(4):
                o_ref[d] = acc[FFB * d:FFB * (d + 1), :].astype(BF16)

    return pl.pallas_call(
        body, name="wgrad_down", grid=(2, nt),
        in_specs=[pl.BlockSpec((tm, FFC), lambda h, t: (t, h)), pl.BlockSpec((tm, D), lambda h, t: (t, 0))],
        out_specs=pl.BlockSpec((4, FFB, D), lambda h, t: (h, 0, 0), pipeline_mode=pl.Buffered(1)),
        out_shape=jax.ShapeDtypeStruct((NDEV, FFB, D), BF16),
        scratch_shapes=[pltpu.VMEM((FFC, D), F32)],
        compiler_params=_params(("arbitrary", "arbitrary"), 48),
    )(f, dh2)


def kernel(x, meta_tokens, g_mix, w_in, b_gate, w_dw, b_dw, ln_g, ln_b, w_conv_out, w_pool, pool_scale, w_pool_out, w_o, g_ffn, w_ffn_gate, w_ffn_up, w_ffn_down, g_final, loss_target, m_meta_tokens, m_g_mix, m_w_in, m_b_gate, m_w_dw, m_b_dw, m_ln_g, m_ln_b, m_w_conv_out, m_w_pool, m_pool_scale, m_w_pool_out, m_w_o, m_g_ffn, m_w_ffn_gate, m_w_ffn_up, m_w_ffn_down, m_g_final, v_meta_tokens, v_g_mix, v_w_in, v_b_gate, v_w_dw, v_b_dw, v_ln_g, v_ln_b, v_w_conv_out, v_w_pool, v_pool_scale, v_w_pool_out, v_w_o, v_g_ffn, v_w_ffn_gate, v_w_ffn_up, v_w_ffn_down, v_g_final):
    seq = x.shape[1]
    tp = -(-(seq + 2 * HALO) // TM) * TM
    tm_in = _pick(tp, TM_IO)
    nx_last = seq - (tp // tm_in - 1) * tm_in
    assert 0 < nx_last <= tm_in - 2 * HALO and nx_last % 8 == 0 and 0 < seq - (tp // TM - 1) * TM

    whole = (Ellipsis,)
    ag_small = _Gather(
        [((48, D // NDEV), [(meta_tokens, pl.ds(0, N_META), whole), (w_dw, pl.ds(N_META, CONV_K), 0)])], [F32])
    ag_mix = _Gather([((3, D // NDEV, D), [(w_conv_out, 0, 0), (w_pool_out, 1, 0), (w_o, 2, 0)]),
                      ((4, PG // NDEV, PG), [(w_pool, whole, 0)])], [BF16, BF16])
    def tr(a):
        return jnp.swapaxes(a, 1, 2)

    ag_gu = _Gather([((2, FFB, D), [(tr(w_ffn_gate), 0, 0), (tr(w_ffn_up), 1, 0)])], [BF16])
    ag_dn = _Gather([((FFB, D), [(w_ffn_down, whole, 0)])], [BF16])

    mx, my = lax.axis_index("x"), lax.axis_index("y")
    order = jnp.stack([2 * mx + my, 2 * mx + 1 - my, 2 * (1 - mx) + my, 2 * (1 - mx) + 1 - my]).astype(jnp.int32)
    (h0, z, u, g_in), (g_mixw, g_pool), g_small = _fwd_in(x[0], g_mix, w_in, order, tp, ag_mix, ag_small)
    wdw_full = g_small.transpose(1, 0, 2).reshape(48, D)[N_META:]
    (ac, m), (w_gu,) = _seq_fwd(z, wdw_full, b_dw, seq, ag_gu)
    (h1, s, merged, q), (g_down,) = _mix_fwd(ac, m, z, h0, b_gate, ln_g, ln_b, pool_scale, g_mixw, g_pool, ag_dn)
    w_dn = g_down.reshape(2, FFC, D)
    fg, fu, v, f, dh2, head_acc = _ffn_fwd(h1, loss_target[0], g_ffn, g_final.reshape(1, D), w_gu, w_dn)

    dfg, dfu, dh1, ffn_acc = _ffn_bwd(dh2, fg, fu, h1, g_ffn, w_gu, w_dn)
    own_f, sib_f, q_f = _rs_pair("rs_pair_ffn", [_wgrad_gu(v, dfg, dfu), _wgrad_down(f, dh2)])
    (dac, dm, dzg, dyc, dyp, dm2, mix_acc), rel_dn = _mix_bwd(
        dh1, z, s, q, ac, m, b_gate, ln_g, ln_b, pool_scale, g_mixw, g_pool, q_f[1:])
    p_mix = _wgrad_mix(s, dyc, q, dyp, merged, dh1, m, dm2)
    own_m, sib_m, q_m = _rs_pair("rs_pair_mix", list(p_mix))
    (dz, seq_acc), rel_gu = _seq_bwd(dac, dm, dzg, z, wdw_full, seq, q_f[:1])
    rel_f = [rel_gu[0], rel_dn[0]]
    p_in, rel_m = _wgrad_in(u, dz, q_m)
    own_i, sib_i, q_i = _rs_pair("rs_pair_in", [p_in])
    (grad_x, g_meta, in_acc), rel_i = _in_bwd(dz, h0, dh1, g_mix, g_in, seq, q_i)
    small_g = jnp.concatenate([g_meta, seq_acc[:CONV_K], jnp.zeros((1, D), F32)], axis=0)
    p_small = small_g.reshape(48, NDEV, D // NDEV).transpose(1, 0, 2).astype(BF16)
    rep_g = jnp.concatenate([
        in_acc[0:1], mix_acc[0:1, :D], mix_acc[0:1, D:], seq_acc[CONV_K:CONV_K + 1], mix_acc[1:2, :D], mix_acc[1:2, D:],
        mix_acc[2:3, :D], ffn_acc[0:1], head_acc[1:2], head_acc[0:1], jnp.zeros((REP_ROWS - 10, D), F32)], axis=0)
    own_s, sib_s, rel_s, rep_all = _reduce_scatter([p_small], rep_g)
    owns = [own_i[0], own_s[0], own_m[0], own_m[1], own_f[0], own_f[1]]
    sibs = [sib_i[0], sib_s[0], sib_m[0], sib_m[1], sib_f[0], sib_f[1]]
    rels = [rel_i[0], rel_s[0], rel_m[0], rel_m[1], rel_f[0], rel_f[1]]

    def lead(a):
        return a.reshape(1, *a.shape)

    def stack4(a, lead_dims):
        return a.reshape(*lead_dims, 1, 4 * 32, PG)

    (r_in,) = _adamw_multi("adamw_in", lead(owns[0]), sibs[0][:, None], rels[0][:, None], [w_in], [m_w_in], [v_w_in], 4)
    r_meta, r_dw = _adamw_meta_dw(owns[1], sibs[1], rels[1], (meta_tokens, m_meta_tokens, v_meta_tokens),
                                  (w_dw, m_w_dw, v_w_dw))
    r_conv, r_pout, r_o = _adamw_multi("adamw_mix", owns[2], sibs[2], rels[2], [w_conv_out, w_pool_out, w_o],
                                       [m_w_conv_out, m_w_pool_out, m_w_o], [v_w_conv_out, v_w_pool_out, v_w_o], 1)
    (r_pool,) = _adamw_multi("adamw_pool", stack4(owns[3], ()), stack4(sibs[3], (1,)), stack4(rels[3], (3,)),
                             [w_pool.reshape(1, 128, PG)], [m_w_pool.reshape(1, 128, PG)], [v_w_pool.reshape(1, 128, PG)], 1)
    r_pool = tuple(a.reshape(w_pool.shape) for a in r_pool)
    r_gate, r_up = _adamw_multi("adamw_gu", owns[4], sibs[4], rels[4], [tr(w_ffn_gate), tr(w_ffn_up)],
                                [tr(m_w_ffn_gate), tr(m_w_ffn_up)], [tr(v_w_ffn_gate), tr(v_w_ffn_up)], 2)
    r_gate, r_up = tuple(tr(a) for a in r_gate), tuple(tr(a) for a in r_up)
    (r_down,) = _adamw_multi("adamw_down", lead(owns[5]), sibs[5][:, None], rels[5][:, None],
                             [w_ffn_down], [m_w_ffn_down], [v_w_ffn_down], 2)
    row = (1, D)
    loss, reps = _adamw_rep(
        rep_all,
        [g_mix, b_gate, b_dw, ln_g, ln_b, pool_scale, g_ffn, g_final.reshape(row)],
        [m_g_mix, m_b_gate, m_b_dw, m_ln_g, m_ln_b, m_pool_scale, m_g_ffn, m_g_final.reshape(row)],
        [v_g_mix, v_b_gate, v_b_dw, v_ln_g, v_ln_b, v_pool_scale, v_g_ffn, v_g_final.reshape(row)])
    r_gmix, r_bg, r_bdw, r_lg, r_lb, r_ps, r_gffn, r_gfin = reps
    r_gfin = tuple(a.reshape(D) for a in r_gfin)

    in_order = [r_meta, r_gmix, r_in, r_bg, r_dw, r_bdw, r_lg, r_lb, r_conv, r_pool, r_ps, r_pout, r_o, r_gffn,
                r_gate, r_up, r_down, r_gfin]
    return (loss.reshape(()), grad_x[None], *[r[0] for r in in_order], *[r[1] for r in in_order],
            *[r[2] for r in in_order], *[r[3] for r in in_order])
```

```python
import math

import jax
import jax.numpy as jnp
from jax import lax
from jax.experimental import pallas as pl
from jax.experimental.pallas import tpu as pltpu

F32, BF16 = jnp.float32, jnp.bfloat16
MESH_ID = pl.DeviceIdType.MESH
NDEV = 8

D = 1024
N_META = 16
CONV_K = 31
HALO = 16
POOL_WINDOWS = (2, 4, 8, 16)
PG = 256
DIN = 5 * D
DFF = 2816
FFB = DFF // NDEV
FFC = DFF // 2
INB = DIN // NDEV
RMS_EPS = 1e-6
LN_EPS = 1e-5
ADAM_LR, ADAM_B1, ADAM_B2, ADAM_EPS, ADAM_WD, ADAM_STEP = 0.001, 0.9, 0.999, 1e-08, 0.01, 10

TM = 384
TMS = 384
TM_IO = 704
TM_WG = 1408
TM_WM = 704
MIB = 2 ** 20


def _sig(x):
    return 0.5 * jnp.tanh(0.5 * x) + 0.5


def _dot(a, b):
    return jnp.dot(a, b, preferred_element_type=F32)


def _dot_nt(a, b):
    return lax.dot_general(a, b, (((1,), (1,)), ((), ())), preferred_element_type=F32)


def _dot_tn(a, b):
    return lax.dot_general(a, b, (((0,), (0,)), ((), ())), preferred_element_type=F32)


def _pick(tp, pref):
    return pref if tp % pref == 0 else TM


def _params(sem, vmem_mib):
    return pltpu.CompilerParams(dimension_semantics=sem, vmem_limit_bytes=vmem_mib * MIB)


def _start_on(first, cps):
    @pl.when(first)
    def _():
        for cp in cps:
            cp.start()


def _wait_on(first, cps):
    @pl.when(first)
    def _():
        for cp in cps:
            cp.wait()


def _mixer_copies(wm_hbm, wm, wp_hbm, wp, sems):
    return ([pltpu.make_async_copy(wm_hbm.at[:, k], wm.at[:, k], sems.at[k]) for k in range(3)],
            pltpu.make_async_copy(wp_hbm, wp, sems.at[3]))


def _place():
    x, y, c = lax.axis_index("x"), lax.axis_index("y"), lax.axis_index("c")
    return x, y, c


class _Gather:
    def __init__(self, groups, dtypes):
        self.groups, self.dtypes, self.n = groups, dtypes, len(groups)
        self.arrays = [a for _, parts in groups for a, _, _ in parts]
        self.out_shape = [jax.ShapeDtypeStruct((NDEV, *s), dt) for (s, _), dt in zip(groups, dtypes)]
        self.scratch = [pltpu.VMEM(s, dt) for (s, _), dt in zip(groups, dtypes)] + [
            pltpu.SemaphoreType.DMA((7 * self.n,)), pltpu.SemaphoreType.DMA((7 * self.n,)),
            pltpu.SemaphoreType.DMA((self.n,))]

    def bind(self, ins, outs, scratch):
        self.ins, self.outs, self.stages = ins, outs, scratch[:self.n]
        self.send_sems, self.recv_sems, self.local_sems = scratch[self.n:]
        return self

    def _copy(self, w, k, block, to, src=None):
        dst = self.outs[w].at[4 * block[0] + 2 * block[1] + block[2]]
        return pltpu.make_async_remote_copy(
            src_ref=dst if src is None else src, dst_ref=dst,
            send_sem=self.send_sems.at[7 * w + k], recv_sem=self.recv_sems.at[7 * w + k],
            device_id=to, device_id_type=MESH_ID)

    def _first(self):
        x, y, c = _place()
        me, sibling = (x, y, c), (x, y, 1 - c)
        chips = [(1 - x, y), (x, 1 - y), (1 - x, 1 - y)]
        mine, first = [], []
        for w in range(self.n):
            mine.append(pltpu.make_async_copy(self.stages[w], self.outs[w].at[4 * x + 2 * y + c], self.local_sems.at[w]))
            first.append(self._copy(w, 0, me, sibling, src=self.stages[w]))
            first += [self._copy(w, 1 + j, me, (*chip, c), src=self.stages[w]) for j, chip in enumerate(chips)]
        return mine, first

    def _passed(self):
        x, y, c = _place()
        chips = [(1 - x, y), (x, 1 - y), (1 - x, 1 - y)]
        return [self._copy(w, 4 + j, (*chip, c), (x, y, 1 - c)) for w in range(self.n) for j, chip in enumerate(chips)]

    def issue(self):
        a = 0
        for w in range(self.n):
            shape, parts = self.groups[w]
            if sum(arr.size for arr, _, _ in parts) < math.prod(shape):
                self.stages[w][...] = jnp.zeros(shape, self.dtypes[w])
            for _, dst, src in parts:
                self.stages[w][dst] = self.ins[a][src].astype(self.dtypes[w])
                a += 1
        mine, first = self._first()
        for cp in mine + first:
            cp.start()

    def forward(self):
        x, y, c = _place()
        chips = [(1 - x, y), (x, 1 - y), (1 - x, 1 - y)]
        passed = self._passed()
        for w in range(self.n):
            for j, chip in enumerate(chips):
                self._copy(w, 1 + j, (*chip, c), (x, y, c)).wait_recv()
                passed[3 * w + j].start()

    def finish(self):
        x, y, c = _place()
        chips = [(1 - x, y), (x, 1 - y), (1 - x, 1 - y)]
        for w in range(self.n):
            self._copy(w, 0, (x, y, 1 - c), (x, y, c)).wait_recv()
            for j, chip in enumerate(chips):
                self._copy(w, 4 + j, (*chip, 1 - c), (x, y, c)).wait_recv()
        mine, first = self._first()
        for cp in first + self._passed():
            cp.wait_send()
        for cp in mine:
            cp.wait()


class _ChipExchange:
    def __init__(self, qs):
        self.n = len(qs)
        self.out_shape = [jax.ShapeDtypeStruct(q.shape, q.dtype) for q in qs]
        self.scratch = [pltpu.SemaphoreType.DMA((3 * self.n,)), pltpu.SemaphoreType.DMA((3 * self.n,))]

    def bind(self, qs, rels, scratch):
        self.qs, self.rels = qs, rels
        self.send_sems, self.recv_sems = scratch
        return self

    def _copies(self):
        x, y, c = _place()
        chips = [(1 - x, y), (x, 1 - y), (1 - x, 1 - y)]
        return [pltpu.make_async_remote_copy(
            src_ref=self.qs[w].at[j], dst_ref=self.rels[w].at[j],
            send_sem=self.send_sems.at[3 * w + j], recv_sem=self.recv_sems.at[3 * w + j],
            device_id=(*chips[j], c), device_id_type=MESH_ID) for w in range(self.n) for j in range(3)]

    def issue(self):
        for cp in self._copies():
            cp.start()

    def finish(self):
        cps = self._copies()
        for cp in cps:
            cp.wait_recv()
        for cp in cps:
            cp.wait_send()


def _reduce_scatter(parts, small):
    n = len(parts)
    blks = [p.shape[1:] for p in parts]

    def body(*refs):
        ps, small_ref = refs[:n], refs[n]
        o = n + 1
        owns, sibs, rels, small_out = refs[o:o + n], refs[o + n:o + 2 * n], refs[o + 2 * n:o + 3 * n], refs[o + 3 * n]
        o += 3 * n + 1
        pa, pb, qst = refs[o:o + n], refs[o + n:o + 2 * n], refs[o + 2 * n:o + 3 * n]
        s1_send, s1_recv, s2_send, s2_recv, sm_send, sm_recv, lsem = refs[o + 3 * n:]
        x, y, c = _place()
        me = 4 * x + 2 * y + c
        sibling = (x, y, 1 - c)
        chips = [(1 - x, y), (x, 1 - y), (1 - x, 1 - y)]
        all_chips = [(x, y)] + chips

        own_cps = []
        for w in range(n):
            cp = pltpu.make_async_copy(ps[w].at[me], owns[w], lsem.at[w])
            cp.start()
            own_cps.append(cp)
        sm_own = pltpu.make_async_copy(small_ref, small_out.at[me], lsem.at[n])
        sm_own.start()

        def small_copy(r):
            peer = ((x + (r >> 2)) % 2, (y + ((r >> 1) & 1)) % 2, (c + (r & 1)) % 2)
            return pltpu.make_async_remote_copy(
                src_ref=small_ref, dst_ref=small_out.at[me], send_sem=sm_send.at[r - 1], recv_sem=sm_recv.at[r - 1],
                device_id=peer, device_id_type=MESH_ID)

        sm_cps = [small_copy(r) for r in range(1, NDEV)]
        for cp in sm_cps:
            cp.start()

        def pair_copy(w, rel):
            cx, cy = all_chips[rel]
            return pltpu.make_async_remote_copy(
                src_ref=ps[w].at[4 * cx + 2 * cy + (1 - c)], dst_ref=sibs[w].at[rel],
                send_sem=s1_send.at[4 * w + rel], recv_sem=s1_recv.at[4 * w + rel],
                device_id=sibling, device_id_type=MESH_ID)

        def chip_copy(w, j):
            return pltpu.make_async_remote_copy(
                src_ref=qst[w].at[j], dst_ref=rels[w].at[j],
                send_sem=s2_send.at[3 * w + j], recv_sem=s2_recv.at[3 * w + j],
                device_id=(*chips[j], c), device_id_type=MESH_ID)

        pair_cps = [pair_copy(w, rel) for w in range(n) for rel in (1, 2, 3, 0)]
        for cp in pair_cps:
            cp.start()
        chip_cps = []
        for w in range(n):
            for j, (cx, cy) in enumerate(chips):
                pair_copy(w, 1 + j).wait_recv()
                la = pltpu.make_async_copy(ps[w].at[4 * cx + 2 * cy + c], pa[w], lsem.at[n + 1])
                lb = pltpu.make_async_copy(sibs[w].at[1 + j], pb[w], lsem.at[n + 2])
                la.start()
                lb.start()
                la.wait()
                lb.wait()
                qst[w][j] = (pa[w][...].astype(F32) + pb[w][...].astype(F32)).astype(BF16)
                cp = chip_copy(w, j)
                cp.start()
                chip_cps.append(cp)
        for w in range(n):
            pair_copy(w, 0).wait_recv()
            for j in range(3):
                chip_copy(w, j).wait_recv()
        for cp in sm_cps:
            cp.wait_recv()
        for cp in pair_cps + chip_cps + sm_cps:
            cp.wait_send()
        for cp in own_cps:
            cp.wait()
        sm_own.wait()

    any_spec = pl.BlockSpec(memory_space=pl.ANY)
    outs = pl.pallas_call(
        body, name="rs_grads",
        out_shape=[jax.ShapeDtypeStruct(b, BF16) for b in blks]
        + [jax.ShapeDtypeStruct((4, *b), BF16) for b in blks]
        + [jax.ShapeDtypeStruct((3, *b), BF16) for b in blks]
        + [jax.ShapeDtypeStruct((NDEV, *small.shape), F32)],
        in_specs=[any_spec] * (n + 1),
        out_specs=[any_spec] * (3 * n + 1),
        scratch_shapes=[pltpu.VMEM(b, BF16) for b in blks] + [pltpu.VMEM(b, BF16) for b in blks]
        + [pltpu.VMEM((3, *b), BF16) for b in blks]
        + [pltpu.SemaphoreType.DMA((4 * n,)), pltpu.SemaphoreType.DMA((4 * n,)),
           pltpu.SemaphoreType.DMA((3 * n,)), pltpu.SemaphoreType.DMA((3 * n,)),
           pltpu.SemaphoreType.DMA((NDEV - 1,)), pltpu.SemaphoreType.DMA((NDEV - 1,)),
           pltpu.SemaphoreType.DMA((n + 3,))],
        compiler_params=pltpu.CompilerParams(vmem_limit_bytes=40 * MIB),
    )(*parts, small)
    return outs[:n], outs[n:2 * n], outs[2 * n:3 * n], outs[3 * n]


class _PairSum:
    def __init__(self, parts, keep_q=True):
        self.n = n = len(parts)
        self.keep_q = keep_q
        blks = [p.shape[1:] for p in parts]
        self.out_shape = [jax.ShapeDtypeStruct(b, BF16) for b in blks] + [jax.ShapeDtypeStruct((1, *b), BF16) for b in blks]
        if keep_q:
            self.out_shape += [jax.ShapeDtypeStruct((3, *b), BF16) for b in blks]
        self.scratch = [pltpu.VMEM((3, *b), BF16) for b in blks] * 3 + [
            pltpu.SemaphoreType.DMA((4 * n,)), pltpu.SemaphoreType.DMA((4 * n,)), pltpu.SemaphoreType.DMA((5 * n,))]

    def bind(self, ps, outs, scratch):
        n = self.n
        self.ps, self.owns, self.sibs, self.qs = ps, outs[:n], outs[n:2 * n], outs[2 * n:]
        self.pa, self.pb, self.qst = scratch[:n], scratch[n:2 * n], scratch[2 * n:3 * n]
        self.s_send, self.s_recv, self.lsem = scratch[3 * n:]
        return self

    def _local(self, with_q):
        n = self.n
        x, y, c = _place()
        chips = [(1 - x, y), (x, 1 - y), (1 - x, 1 - y)]
        own = [pltpu.make_async_copy(self.ps[w].at[4 * x + 2 * y + c], self.owns[w], self.lsem.at[w]) for w in range(n)]
        mine = [[pltpu.make_async_copy(self.ps[w].at[4 * cx + 2 * cy + c], self.pa[w].at[j], self.lsem.at[2 * n + 3 * w + j])
                 for j, (cx, cy) in enumerate(chips)] for w in range(n)]
        outq = [pltpu.make_async_copy(self.qst[w], self.qs[w], self.lsem.at[n + w]) for w in range(n)] if with_q else []
        return own, mine, outq

    def _pair(self, w, rel):
        x, y, c = _place()
        cx, cy = [(x, y), (1 - x, y), (x, 1 - y), (1 - x, 1 - y)][rel]
        return pltpu.make_async_remote_copy(
            src_ref=self.ps[w].at[4 * cx + 2 * cy + (1 - c)],
            dst_ref=self.sibs[w].at[0] if rel == 0 else self.pb[w].at[rel - 1],
            send_sem=self.s_send.at[4 * w + rel], recv_sem=self.s_recv.at[4 * w + rel],
            device_id=(x, y, 1 - c), device_id_type=MESH_ID)

    def issue(self):
        own, mine, _ = self._local(False)
        for cp in own + [cp for row in mine for cp in row]:
            cp.start()
        for w in range(self.n):
            for rel in (1, 2, 3, 0):
                self._pair(w, rel).start()

    def finish(self):
        own, mine, outq = self._local(self.keep_q)
        for w in range(self.n):
            for j in range(3):
                self._pair(w, 1 + j).wait_recv()
                mine[w][j].wait()
                self.qst[w][j] = (self.pa[w][j].astype(F32) + self.pb[w][j].astype(F32)).astype(BF16)
            if self.keep_q:
                outq[w].start()
        for w in range(self.n):
            self._pair(w, 0).wait_recv()
        for w in range(self.n):
            for rel in range(4):
                self._pair(w, rel).wait_send()
        for cp in own + outq:
            cp.wait()

    def results(self, outs):
        n = self.n
        return outs[:n], outs[n:2 * n], outs[2 * n:3 * n]


def _rs_pair(name, parts):
    ps = _PairSum(parts)
    n = ps.n

    def body(*refs):
        ps.bind(refs[:n], refs[n:4 * n], refs[4 * n:])
        ps.issue()
        ps.finish()

    any_spec = pl.BlockSpec(memory_space=pl.ANY)
    outs = pl.pallas_call(
        body, name=name, out_shape=ps.out_shape,
        in_specs=[any_spec] * n, out_specs=[any_spec] * (3 * n), scratch_shapes=ps.scratch,
        compiler_params=pltpu.CompilerParams(vmem_limit_bytes=48 * MIB),
    )(*parts)
    return ps.results(outs)


def _adamw_math(g, w, m, v):
    m = ADAM_B1 * m + (1.0 - ADAM_B1) * g
    v = ADAM_B2 * v + (1.0 - ADAM_B2) * (g * g)
    m_hat = m / (1.0 - ADAM_B1 ** ADAM_STEP)
    v_hat = v / (1.0 - ADAM_B2 ** ADAM_STEP)
    delta = -ADAM_LR * (m_hat / (jnp.sqrt(v_hat) + ADAM_EPS) + ADAM_WD * w)
    return delta, m, v


def _adamw_multi(name, own, sib, rel, ws, ms, vs, row_grid):
    k_n, r_n, c_n = own.shape
    rbk = r_n // row_grid

    def body(*refs):
        own_ref, sib_ref, r0_ref, r1_ref, r2_ref = refs[:5]
        w_refs, m_refs, v_refs = refs[5:5 + k_n], refs[5 + k_n:5 + 2 * k_n], refs[5 + 2 * k_n:5 + 3 * k_n]
        outs = refs[5 + 3 * k_n:]
        for k in range(k_n):
            g = own_ref[k].astype(F32) + sib_ref[k].astype(F32)
            g = g + r0_ref[k].astype(F32)
            g = g + r1_ref[k].astype(F32)
            g = g + r2_ref[k].astype(F32)
            delta, mm, vv = _adamw_math(g, w_refs[k][0], m_refs[k][0], v_refs[k][0])
            outs[4 * k][0] = g
            outs[4 * k + 1][0] = delta
            outs[4 * k + 2][0] = mm
            outs[4 * k + 3][0] = vv

    def lead(j):
        return pl.BlockSpec((None, k_n, rbk, c_n), lambda g: (j, 0, g, 0))

    wspec = pl.BlockSpec((1, rbk, c_n), lambda g: (0, g, 0))
    shp = jax.ShapeDtypeStruct((1, r_n, c_n), F32)
    res = pl.pallas_call(
        body, name=name, grid=(row_grid,),
        in_specs=[pl.BlockSpec((k_n, rbk, c_n), lambda g: (0, g, 0)), lead(0), lead(0), lead(1), lead(2)] + [wspec] * (3 * k_n),
        out_specs=[wspec] * (4 * k_n), out_shape=[shp] * (4 * k_n),
        compiler_params=_params(("arbitrary",), 40),
    )(own, sib, rel, rel, rel, *ws, *ms, *vs)
    return [tuple(res[4 * k:4 * k + 4]) for k in range(k_n)]


def _adamw_meta_dw(own, sib, rel, meta, dw):
    def body(own_ref, sib_ref, rel_ref, wm, mm, vm, wd, md, vd, *outs):
        def gsum(rows):
            g = own_ref[rows, :].astype(F32) + sib_ref[0, rows, :].astype(F32)
            for j in range(3):
                g = g + rel_ref[j, rows, :].astype(F32)
            return g

        g = gsum(pl.ds(0, N_META))
        delta, m2, v2 = _adamw_math(g, wm[...], mm[...], vm[...])
        for o, val in zip(outs[:4], (g, delta, m2, v2)):
            o[...] = val
        g = gsum(pl.ds(N_META, CONV_K))
        delta, m2, v2 = _adamw_math(g, wd[0], md[0], vd[0])
        for o, val in zip(outs[4:], (g, delta, m2, v2)):
            o[0] = val

    s_meta = jax.ShapeDtypeStruct(meta[0].shape, F32)
    s_dw = jax.ShapeDtypeStruct(dw[0].shape, F32)
    res = pl.pallas_call(body, name="adamw_meta_dw", out_shape=[s_meta] * 4 + [s_dw] * 4)(own, sib, rel, *meta, *dw)
    return tuple(res[:4]), tuple(res[4:])


REP_ROWS = 16


def _adamw_rep(gathered, ws, ms, vs):
    rows = [(0, 1), (1, 2), (3, 1), (4, 1), (5, 1), (6, 1), (7, 1), (8, 1)]

    def body(g_ref, *refs):
        w_refs, m_refs, v_refs = refs[:8], refs[8:16], refs[16:24]
        loss_ref, outs, acc = refs[24], refs[25:57], refs[57]
        g = g_ref[0]
        for d in range(1, NDEV):
            g = g + g_ref[d]
        acc[...] = g
        loss_ref[...] = (0.5 / D) * jnp.sum(acc[pl.ds(9, 1), :], axis=1, keepdims=True)
        for p, (r0, nr) in enumerate(rows):
            for h in range(nr):
                cols = pl.ds(h * D, D)
                gp = acc[pl.ds(r0 + h, 1), :]
                delta, mm, vv = _adamw_math(gp, w_refs[p][:, cols], m_refs[p][:, cols], v_refs[p][:, cols])
                for o, val in zip(outs[4 * p:4 * p + 4], (gp, delta, mm, vv)):
                    o[:, cols] = val

    shapes = [jax.ShapeDtypeStruct(w.shape, F32) for w in ws]
    res = pl.pallas_call(
        body, name="adamw_rep",
        out_shape=[jax.ShapeDtypeStruct((1, 1), F32)] + [s for s in shapes for _ in range(4)],
        scratch_shapes=[pltpu.VMEM((REP_ROWS, D), F32)],
    )(gathered, *ws, *ms, *vs)
    return res[0], [tuple(res[1 + 4 * p:5 + 4 * p]) for p in range(8)]


def _load_ffn(i, j, wgu_hbm, wgu, wdn_hbm, wdn, sems):
    half = NDEV // 2

    def copies(ch):
        pairs = [(wgu_hbm.at[half * ch + d, g], wgu.at[g, ch, pl.ds(FFB * d, FFB), :]) for g in range(2) for d in range(half)]
        pairs.append((wdn_hbm.at[ch], wdn.at[ch]))
        return [pltpu.make_async_copy(s, t, sems.at[(2 * half + 1) * ch + k]) for k, (s, t) in enumerate(pairs)]

    @pl.when((i == 0) & (j == 0))
    def _():
        for cp in copies(0) + copies(1):
            cp.start()

    for ch in range(2):
        @pl.when((i == 0) & (j == ch))
        def _():
            for cp in copies(ch):
                cp.wait()


def _win_pairs(w_hbm, w_vm):
    return [(w_hbm.at[q], w_vm.at[q // 2, :, pl.ds(2 * INB * (q % 2), 2 * INB)]) for q in range(4)]


def _whole(a):
    nd = a.ndim
    return pl.BlockSpec(a.shape, lambda *g: (0,) * nd)


CHIPW = 2 * INB
PHASE_CHIP = (1, 0, 2)
assert PHASE_CHIP[2] == 2


class _GatherIn:
    scratch = [pltpu.VMEM((D, INB), BF16), pltpu.SemaphoreType.DMA((7,)), pltpu.SemaphoreType.DMA((7,)),
               pltpu.SemaphoreType.DMA((1,))]

    def bind(self, w_ref, w_vm, scratch):
        self.w_ref, self.w_vm = w_ref, w_vm
        self.stage, self.send_sems, self.recv_sems, self.local_sem = scratch
        return self

    def _win(self, chip, core):
        return self.w_vm.at[2 * chip[0] + chip[1], core]

    def _copy(self, k, chip, core, to, src=None):
        dst = self._win(chip, core)
        return pltpu.make_async_remote_copy(
            src_ref=dst if src is None else src, dst_ref=dst, send_sem=self.send_sems.at[k],
            recv_sem=self.recv_sems.at[k], device_id=to, device_id_type=MESH_ID)

    def _mine(self, cs):
        x, y, _ = _place()
        return pltpu.make_async_copy(self.stage, self._win((x, y), cs), self.local_sem.at[0])

    def issue(self, cs):
        x, y, _ = _place()
        chips = [(1 - x, y), (x, 1 - y), (1 - x, 1 - y)]
        self.stage[...] = self.w_ref[0].astype(BF16)
        self._mine(cs).start()
        self._copy(0, (x, y), cs, (x, y, 1 - cs), src=self.stage).start()
        for j in PHASE_CHIP[:2]:
            self._copy(1 + j, (x, y), cs, (*chips[j], cs), src=self.stage).start()

    def wait_chip(self, phase, cs):
        x, y, _ = _place()
        chips = [(1 - x, y), (x, 1 - y), (1 - x, 1 - y)]
        if phase == 0:
            self._mine(cs).wait()
            self._copy(0, (x, y), 1 - cs, (x, y, cs)).wait_recv()
            return
        j = PHASE_CHIP[phase - 1]
        self._copy(1 + j, chips[j], cs, (x, y, cs)).wait_recv()
        self._copy(4 + j, chips[j], cs, (x, y, 1 - cs)).start()
        if phase == 1:
            self._copy(3, (x, y), cs, (*chips[2], cs), src=self.stage).start()
        self._copy(4 + j, chips[j], 1 - cs, (x, y, cs)).wait_recv()

    def finish(self, cs):
        x, y, _ = _place()
        for k in range(7):
            self._copy(k, (x, y), cs, (x, y, cs), src=self.stage).wait_send()


def _fwd_in(x2, g_mix, w_in, order, tp, ag, ags):
    tm = _pick(tp, TM_IO)
    nt = tp // tm
    nx_last = x2.shape[0] - (nt - 1) * tm
    na, ng, ns = len(ag.arrays), ag.n, len(ags.arrays)
    gin = _GatherIn()

    def body(order_ref, *refs):
        x_ref, g_ref, w_ref = refs[:3]
        o = 3 + na + ns
        h_ref, z_ref, u_ref, wout_ref = refs[o:o + 4]
        s = o + 4 + ng + 1
        w_vm, u_all, osem, sm_vm = refs[s:s + 4]
        gin.bind(w_ref, w_vm, refs[s + 4:s + 8])
        ag.bind(refs[3:3 + na], refs[o + 4:o + 4 + ng], refs[s + 8:s + 8 + len(ag.scratch)])
        ags.bind(refs[3 + na:3 + na + ns], refs[o + 4 + ng:o + 5 + ng], refs[s + 8 + len(ag.scratch):])
        ph, i = pl.program_id(0), pl.program_id(1)
        core = lax.axis_index("c")
        first = (ph == 0) & (i == 0)
        last = (ph == 3) & (i == nt - 1)
        @pl.when(first)
        def _():
            ags.issue()

        for cs in range(2):
            @pl.when(first & (core == cs))
            def _():
                gin.issue(cs)

        @pl.when((ph == 0) & (i == max(nt - 2, 0)))
        def _():
            ags.forward()

        for cs in range(2):
            for p in range(4):
                @pl.when((ph == p) & (i == 0) & (core == cs))
                def _():
                    gin.wait_chip(p, cs)

        @pl.when((ph == 2) & (i == 0))
        def _():
            ag.issue()

        out_copies = [pltpu.make_async_copy(w_vm.at[k, c], wout_ref.at[k, :, pl.ds(INB * c, INB)], osem.at[2 * k + c])
                      for k in range(4) for c in range(2)]

        @pl.when((ph == 3) & (i == 0))
        def _():
            for cp in out_copies:
                cp.start()

        @pl.when((ph == 0) & (i < nt - 1))
        def _():
            h_ref[...] = x_ref[...]

        @pl.when((ph == 0) & (i == nt - 1))
        def _():
            ags.finish()
            cp = pltpu.make_async_copy(ags.outs[0], sm_vm, osem.at[8])
            cp.start()
            h_ref[pl.ds(0, nx_last), :] = x_ref[pl.ds(0, nx_last), :]
            h_ref[pl.ds(nx_last, tm - nx_last - N_META), :] = jnp.zeros((tm - nx_last - N_META, D), F32)
            cp.wait()
            for d in range(NDEV):
                h_ref[pl.ds(tm - N_META, N_META), pl.ds(128 * d, 128)] = sm_vm[d, pl.ds(0, N_META), :]

        @pl.when(ph == 0)
        def _():
            xv = h_ref[...]
            r = lax.rsqrt(jnp.mean(xv * xv, axis=-1, keepdims=True) + RMS_EPS)
            u = (xv * r * g_ref[...]).astype(BF16)
            u_ref[...] = u
            u_all[i] = u

        for c in range(2):
            z_ref[:, INB * c:INB * (c + 1)] = _dot(u_all[i], w_vm[order_ref[ph], c])

        @pl.when(last)
        def _():
            ag.forward()
            ag.finish()
            for cp in out_copies:
                cp.wait()

        for cs in range(2):
            @pl.when(last & (core == cs))
            def _():
                gin.finish(cs)

    def rows(ph, i, order):
        return (jnp.where(ph == 0, i, nt - 1), 0)

    tile = pl.BlockSpec((tm, D), rows)
    anys = pl.BlockSpec(memory_space=pl.ANY)
    res = pl.pallas_call(
        body, name="fwd_in",
        grid_spec=pltpu.PrefetchScalarGridSpec(
            num_scalar_prefetch=1, grid=(4, nt),
            in_specs=[tile, pl.BlockSpec((1, D), lambda ph, i, order: (0, 0)), _whole(w_in)]
            + [_whole(a) for a in ag.arrays + ags.arrays],
            out_specs=[tile, pl.BlockSpec((tm, CHIPW), lambda ph, i, order: (i, order[ph])), tile, anys] + [anys] * (ng + 1),
            scratch_shapes=[pltpu.VMEM((4, 2, D, INB), BF16), pltpu.VMEM((nt, tm, D), BF16), pltpu.SemaphoreType.DMA((9,)),
                            pltpu.VMEM(ags.out_shape[0].shape, F32)] + gin.scratch + ag.scratch + ags.scratch),
        out_shape=[jax.ShapeDtypeStruct((tp, D), F32), jax.ShapeDtypeStruct((tp, DIN), F32),
                   jax.ShapeDtypeStruct((tp, D), BF16), jax.ShapeDtypeStruct((4, D, CHIPW), BF16)]
        + ag.out_shape + ags.out_shape,
        compiler_params=_params(("arbitrary", "arbitrary"), 58),
    )(order, x2, g_mix, w_in, *ag.arrays, *ags.arrays)
    return res[:4], res[4:4 + ng], res[4 + ng]


def _halo_specs(col, nt, width=D):
    r = TM // HALO
    nb = nt * r
    return [pl.BlockSpec((HALO, width), lambda i: ((i * r + nb - 1) % nb, col)),
            pl.BlockSpec((TM, width), lambda i: (i, col)),
            pl.BlockSpec((HALO, width), lambda i: (((i + 1) * r) % nb, col))]


NCB = D // 128
TME = TM + 2 * HALO
CONV_STEPS = 16
assert TM % CONV_STEPS == 0


def _tm_fill(dst, time0, groups, tile_fn, unroll=1):
    def body(g, c):
        for j in range(NCB):
            dst[pl.ds((time0 + 8 * g) * NCB + j, 8, stride=NCB), :] = tile_fn(pl.multiple_of(8 * g, 8), pl.ds(128 * j, 128))
        return c

    lax.fori_loop(0, groups, body, 0, unroll=unroll)


def _tm_fill_ext(dst, left, cur, right, fn, unroll=1):
    _tm_fill(dst, 0, HALO // 8, lambda r, l: fn(left, pl.ds(r, 8), l), unroll)
    _tm_fill(dst, HALO, TM // 8, lambda r, l: fn(cur, pl.ds(r, 8), l), unroll)
    _tm_fill(dst, HALO + TM, HALO // 8, lambda r, l: fn(right, pl.ds(r, 8), l), unroll)


def _tm_read(src, groups, store_fn):
    def body(g, c):
        for j in range(NCB):
            store_fn(pl.ds(pl.multiple_of(8 * g, 8), 8), pl.ds(128 * j, 128), src[pl.ds(8 * g * NCB + j, 8, stride=NCB), :])
        return c

    lax.fori_loop(0, groups, body, 0, unroll=2)


def _tm_rows(t):
    return pl.ds(t * NCB if isinstance(t, int) else pl.multiple_of(t * NCB, NCB), NCB)


def _tm_at(ref, t):
    return ref[_tm_rows(t), :]


def _by_group(sub, vals):
    return jnp.where(sub < 2, vals[0], jnp.where(sub < 4, vals[1], jnp.where(sub < 6, vals[2], vals[3])))


def _pool_cnt(b, seq, tp, sub):
    b = jnp.where(b < 0, b + tp, b)
    b = jnp.where(b >= tp, b - tp, b)
    t = jnp.where(b < seq, b + N_META, b - (tp - N_META))
    cnts = []
    for win in POOL_WINDOWS:
        left = win // 2
        lo = jnp.maximum(t - left, 0)
        hi = jnp.minimum(t + win - left, seq + N_META)
        cnts.append(jnp.maximum(hi - lo, 1).astype(F32))
    return _by_group(sub, cnts)


def _edge_rows(seq, tp):
    reach = max(POOL_WINDOWS) // 2
    return [tp - N_META + t for t in range(reach)] + [seq - reach + 1 + t for t in range(reach - 1)]


def _edge_gain(b, seq, tp, sub):
    return _by_group(sub, [float(w) for w in POOL_WINDOWS]) / _pool_cnt(b, seq, tp, sub)


def _nested_windows(at, lo_offs):
    sums, s, have = [], None, set()
    for g, win in enumerate(POOL_WINDOWS):
        for o in range(lo_offs[g], lo_offs[g] + win):
            if o not in have:
                have.add(o)
                s = at(o) if s is None else s + at(o)
        sums.append(s)
    return sums


def _seq_fwd(z, w_dw, b_dw, seq, gat):
    tp = z.shape[0]
    nt = tp // TM
    na, ng = len(gat.arrays), gat.n

    def body(*refs):
        av_l, av, av_r, ag_l, ag, ag_r, p_l, p, p_r, w_ref, b_ref = refs[:11]
        ac_ref, m_ref = refs[11 + na:13 + na]
        a3, p3, o3, m3, w3, b3, m2d = refs[13 + na + ng:20 + na + ng]
        gat.bind(refs[11:11 + na], refs[13 + na:13 + na + ng], refs[20 + na + ng:])
        i = pl.program_id(0)
        sub = lax.broadcasted_iota(jnp.int32, (NCB, 128), 0)

        @pl.when(i == 0)
        def _():
            gat.issue()
            _tm_fill(w3, 0, 4, lambda r, l: w_ref[pl.ds(r, 8), l])
            for j in range(NCB):
                b3[pl.ds(j, 1), :] = b_ref[:, pl.ds(128 * j, 128)]

        @pl.when(i == max(nt - 2, 0))
        def _():
            gat.forward()

        _tm_fill_ext(a3, (av_l, ag_l), (av, ag), (av_r, ag_r), lambda vg, r, l: vg[0][r, l] * _sig(vg[1][r, l]), unroll=2)
        _tm_fill_ext(p3, p_l, p, p_r, lambda ref, r, l: ref[r, l])

        def conv(g, c):
            accs = [b3[...]] * 16
            for k in range(CONV_K):
                wk = _tm_at(w3, k)
                for t in range(16):
                    accs[t] = accs[t] + wk * _tm_at(a3, 16 * g + t + k + 1)
            for t in range(16):
                o3[_tm_rows(16 * g + t), :] = accs[t]
            return c

        lax.fori_loop(0, TM // 16, conv, 0)
        _tm_read(o3, TM // 8, lambda r, l, tile: ac_ref.__setitem__((r, l), tile))

        inv = _by_group(sub, [1.0 / w for w in POOL_WINDOWS])

        def pool(g, c):
            for t in range(8):
                e = 8 * g + t + HALO
                sums = _nested_windows(lambda o: _tm_at(p3, e + o), [-(w // 2) for w in POOL_WINDOWS])
                m3[_tm_rows(8 * g + t), :] = _by_group(sub, sums) * inv - _tm_at(p3, e)
            return c

        lax.fori_loop(0, TM // 8, pool, 0)
        for b in _edge_rows(seq, tp):
            r = b - i * TM

            @pl.when((r >= 0) & (r < TM))
            def _():
                pv = _tm_at(p3, r + HALO)
                m3[_tm_rows(r), :] = (_tm_at(m3, r) + pv) * _edge_gain(b, seq, tp, sub) - pv

        _tm_read(m3, TM // 8, lambda r, l, tile: m2d.__setitem__((r, l), tile))
        m_ref[...] = m2d[...].astype(BF16)

        @pl.when(i == nt - 1)
        def _():
            gat.finish()

    tmaj = pltpu.VMEM((TM * NCB, 128), F32)
    text = pltpu.VMEM((TME * NCB, 128), F32)
    res = pl.pallas_call(
        body, name="seq_fwd", grid=(nt,),
        in_specs=_halo_specs(0, nt) + _halo_specs(1, nt) + _halo_specs(2, nt)
        + [pl.BlockSpec((32, D), lambda i: (0, 0)), pl.BlockSpec((1, D), lambda i: (0, 0))] + [_whole(a) for a in gat.arrays],
        out_specs=[pl.BlockSpec((TM, D), lambda i: (i, 0))] * 2 + [pl.BlockSpec(memory_space=pl.ANY)] * ng,
        out_shape=[jax.ShapeDtypeStruct((tp, D), F32), jax.ShapeDtypeStruct((tp, D), BF16)] + gat.out_shape,
        scratch_shapes=[text, text, tmaj, tmaj, pltpu.VMEM((32 * NCB, 128), F32), pltpu.VMEM((NCB, 128), F32),
                        pltpu.VMEM((TM, D), F32)] + gat.scratch,
        compiler_params=_params(("arbitrary",), 52),
    )(z, z, z, z, z, z, z, z, z, w_dw, b_dw, *gat.arrays)
    return res[:2], res[2:]


def _ln_stats(ac):
    mu = jnp.mean(ac, axis=-1, keepdims=True)
    xc = ac - mu
    rl = lax.rsqrt(jnp.mean(xc * xc, axis=-1, keepdims=True) + LN_EPS)
    return xc * rl, rl


def _pool_mix(m, wp_ref):
    return jnp.concatenate(
        [_dot(m[:, g * PG:(g + 1) * PG], wp_ref[:, g].reshape(PG, PG)) for g in range(4)], axis=1)


def _mix_fwd(ac, m, z, h0, b_gate, ln_g, ln_b, pool_scale, g_mixw, g_pool, gat):
    tp = h0.shape[0]
    tms = TM
    nt = tp // tms
    na, ng = len(gat.arrays), gat.n

    def body(*refs):
        ac_ref, m_ref, zga, zgb, h_ref, bg_ref, lg_ref, lb_ref, ps_ref, wm_hbm, wp_hbm = refs[:11]
        h1_ref, s_ref, mg_ref, q_ref = refs[11 + na:15 + na]
        wm, wp, sems = refs[15 + na + ng:18 + na + ng]
        gat.bind(refs[11:11 + na], refs[15 + na:15 + na + ng], refs[18 + na + ng:])
        i = pl.program_id(0)

        @pl.when(i == 0)
        def _():
            gat.issue()

        @pl.when(i == max(nt - 4, 0))
        def _():
            gat.forward()

        @pl.when(i == nt - 1)
        def _():
            gat.finish()

        cw, cpool = _mixer_copies(wm_hbm, wm, wp_hbm, wp, sems)
        _start_on(i == 0, [cw[0], cpool, cw[1], cw[2]])
        n, _ = _ln_stats(ac_ref[...])
        l = n * lg_ref[...] + lb_ref[...]
        s = (l * _sig(l)).astype(BF16)
        s_ref[...] = s
        _wait_on(i == 0, [cw[0]])
        yc = _dot(s, wm[:, 0].reshape(D, D))
        _wait_on(i == 0, [cpool])
        q = (_pool_mix(m_ref[...], wp) * ps_ref[...]).astype(BF16)
        q_ref[...] = q
        _wait_on(i == 0, [cw[1]])
        yp = _dot(q, wm[:, 1].reshape(D, D))
        ga = _sig(zga[...] + bg_ref[:, :D])
        gb = _sig(zgb[...] + bg_ref[:, D:])
        merged = (ga * yc + gb * yp).astype(BF16)
        mg_ref[...] = merged
        _wait_on(i == 0, [cw[2]])
        h1_ref[...] = h_ref[...] + _dot(merged, wm[:, 2].reshape(D, D))

    def tile(col=0):
        return pl.BlockSpec((tms, D), lambda i: (i, col))

    def vec(w):
        return pl.BlockSpec((1, w), lambda i: (0, 0))

    anys = pl.BlockSpec(memory_space=pl.ANY)
    f32o, b16o = jax.ShapeDtypeStruct((tp, D), F32), jax.ShapeDtypeStruct((tp, D), BF16)
    res = pl.pallas_call(
        body, name="mix_fwd", grid=(nt,),
        in_specs=[tile(), tile(), tile(3), tile(4), tile(), vec(2 * D), vec(D), vec(D), vec(D), anys, anys]
        + [_whole(a) for a in gat.arrays],
        out_specs=[tile()] * 4 + [anys] * ng,
        out_shape=[f32o, b16o, b16o, b16o] + gat.out_shape,
        scratch_shapes=[pltpu.VMEM((NDEV, 3, D // NDEV, D), BF16), pltpu.VMEM((NDEV, 4, PG // NDEV, PG), BF16),
                        pltpu.SemaphoreType.DMA((4,))] + gat.scratch,
        compiler_params=_params(("arbitrary",), 52),
    )(ac, m, z, z, h0, b_gate, ln_g, ln_b, pool_scale, g_mixw, g_pool, *gat.arrays)
    return res[:4], res[4:]


def _ffn_fwd(h1, tgt, g_ffn, g_final, w_gu, w_dn):
    tp = h1.shape[0]
    nt = tp // TM
    nx_last = tgt.shape[0] - (nt - 1) * TM

    def body(h_ref, t_ref, gf_ref, gl_ref, wgu_hbm, wdn_hbm,
             fg_ref, fu_ref, v_ref, f_ref, dh2_ref, acc_ref, wgu, wdn, v_sc, h2_sc, diff_sc, sems):
        i, j = pl.program_id(0), pl.program_id(1)
        _load_ffn(i, j, wgu_hbm, wgu, wdn_hbm, wdn, sems)

        @pl.when((i == 0) & (j == 0))
        def _():
            acc_ref[...] = jnp.zeros_like(acc_ref)

        @pl.when(j == 0)
        def _():
            h = h_ref[...]
            r = lax.rsqrt(jnp.mean(h * h, axis=-1, keepdims=True) + RMS_EPS)
            v = (h * r * gf_ref[...]).astype(BF16)
            v_sc[...] = v
            v_ref[...] = v
            h2_sc[...] = h

        v = v_sc[...]
        fg = _dot_nt(v, wgu[0, j])
        fu = _dot_nt(v, wgu[1, j])
        fg_ref[...] = fg
        fu_ref[...] = fu
        f = ((fg * _sig(fg)) * fu).astype(BF16)
        f_ref[...] = f
        h2_sc[...] += _dot(f, wdn[j])

        @pl.when(j == 1)
        def _():
            h2 = h2_sc[...]
            r = lax.rsqrt(jnp.mean(h2 * h2, axis=-1, keepdims=True) + RMS_EPS)
            n2 = h2 * r
            y = n2 * gl_ref[...]

            @pl.when(i < nt - 1)
            def _():
                diff_sc[...] = y - t_ref[...]

            @pl.when(i == nt - 1)
            def _():
                diff_sc[pl.ds(0, nx_last), :] = y[:nx_last] - t_ref[pl.ds(0, nx_last), :]
                diff_sc[pl.ds(nx_last, TM - nx_last), :] = jnp.zeros((TM - nx_last, D), F32)

            diff = diff_sc[...]
            dy = diff * (1.0 / D)
            acc_ref[0:1, :] += jnp.sum(diff * diff, axis=0, keepdims=True)
            acc_ref[1:2, :] += jnp.sum(dy * n2, axis=0, keepdims=True)
            dn = dy * gl_ref[...]
            dh2_ref[...] = r * (dn - n2 * jnp.mean(dn * n2, axis=-1, keepdims=True))

    def tile():
        return pl.BlockSpec((TM, D), lambda i, j: (i, 0))

    def chunk():
        return pl.BlockSpec((TM, FFC), lambda i, j: (i, j))

    def vec():
        return pl.BlockSpec((1, D), lambda i, j: (0, 0))

    anys = pl.BlockSpec(memory_space=pl.ANY)
    hid32, hid16 = jax.ShapeDtypeStruct((tp, DFF), F32), jax.ShapeDtypeStruct((tp, DFF), BF16)
    return pl.pallas_call(
        body, name="ffn_fwd", grid=(nt, 2),
        in_specs=[tile(), tile(), vec(), vec(), anys, anys],
        out_specs=[chunk(), chunk(), tile(), chunk(), tile(), pl.BlockSpec((8, D), lambda i, j: (0, 0))],
        out_shape=[hid32, hid32, jax.ShapeDtypeStruct((tp, D), BF16), hid16, jax.ShapeDtypeStruct((tp, D), F32),
                   jax.ShapeDtypeStruct((8, D), F32)],
        scratch_shapes=[pltpu.VMEM((2, 2, FFC, D), BF16), pltpu.VMEM((2, FFC, D), BF16),
                        pltpu.VMEM((TM, D), BF16), pltpu.VMEM((TM, D), F32), pltpu.VMEM((TM, D), F32),
                        pltpu.SemaphoreType.DMA((2 * NDEV + 2,))],
        compiler_params=_params(("arbitrary", "arbitrary"), 56),
    )(h1, tgt, g_ffn, g_final, w_gu, w_dn)


def _ffn_bwd(dh2, fg, fu, h1, g_ffn, w_gu, w_dn):
    tp = h1.shape[0]
    nt = tp // TM

    def body(dh2_ref, fg_ref, fu_ref, h_ref, gf_ref, wgu_hbm, wdn_hbm,
             dfg_ref, dfu_ref, dh1_ref, acc_ref, wgu, wdn, d_sc, dv_sc, sems):
        i, j = pl.program_id(0), pl.program_id(1)
        _load_ffn(i, j, wgu_hbm, wgu, wdn_hbm, wdn, sems)

        @pl.when((i == 0) & (j == 0))
        def _():
            acc_ref[...] = jnp.zeros_like(acc_ref)

        @pl.when(j == 0)
        def _():
            d_sc[...] = dh2_ref[...].astype(BF16)
            dv_sc[...] = jnp.zeros_like(dv_sc)

        df = _dot_nt(d_sc[...], wdn[j])
        fg = fg_ref[...]
        sg = _sig(fg)
        dfu = (df * (fg * sg)).astype(BF16)
        dfg = (df * fu_ref[...] * (sg * (1.0 + fg * (1.0 - sg)))).astype(BF16)
        dfg_ref[...] = dfg
        dfu_ref[...] = dfu
        dv_sc[...] += _dot(dfg, wgu[0, j]) + _dot(dfu, wgu[1, j])

        @pl.when(j == 1)
        def _():
            h = h_ref[...]
            r = lax.rsqrt(jnp.mean(h * h, axis=-1, keepdims=True) + RMS_EPS)
            n1 = h * r
            dv = dv_sc[...]
            acc_ref[0:1, :] += jnp.sum(dv * n1, axis=0, keepdims=True)
            dn = dv * gf_ref[...]
            dh1_ref[...] = dh2_ref[...] + r * (dn - n1 * jnp.mean(dn * n1, axis=-1, keepdims=True))

    def tile():
        return pl.BlockSpec((TM, D), lambda i, j: (i, 0))

    def chunk():
        return pl.BlockSpec((TM, FFC), lambda i, j: (i, j))

    anys = pl.BlockSpec(memory_space=pl.ANY)
    hid16 = jax.ShapeDtypeStruct((tp, DFF), BF16)
    return pl.pallas_call(
        body, name="ffn_bwd", grid=(nt, 2),
        in_specs=[tile(), chunk(), chunk(), tile(), pl.BlockSpec((1, D), lambda i, j: (0, 0)), anys, anys],
        out_specs=[chunk(), chunk(), tile(), pl.BlockSpec((8, D), lambda i, j: (0, 0))],
        out_shape=[hid16, hid16, jax.ShapeDtypeStruct((tp, D), F32), jax.ShapeDtypeStruct((8, D), F32)],
        scratch_shapes=[pltpu.VMEM((2, 2, FFC, D), BF16), pltpu.VMEM((2, FFC, D), BF16),
                        pltpu.VMEM((TM, D), BF16), pltpu.VMEM((TM, D), F32), pltpu.SemaphoreType.DMA((2 * NDEV + 2,))],
        compiler_params=_params(("arbitrary", "arbitrary"), 56),
    )(dh2, fg, fu, h1, g_ffn, w_gu, w_dn)


def _mix_bwd(dh1, z, s, q, ac, m, b_gate, ln_g, ln_b, pool_scale, g_mixw, g_pool, qs):
    tp = dh1.shape[0]
    nt = tp // TMS
    ex = _ChipExchange(qs)
    nq = ex.n

    def body(*refs):
        dh1_ref, zga, zgb, s_ref, q_ref, ac_ref, m_ref, bg_ref, lg_ref, lb_ref, ps_ref, wm_hbm, wp_hbm = refs[:13]
        dac_ref, dm_ref, dzg_ref, dyc_ref, dyp_ref, dm2_ref, acc_ref = refs[13 + nq:20 + nq]
        wm, wp, sems = refs[20 + 2 * nq:23 + 2 * nq]
        ex.bind(refs[13:13 + nq], refs[20 + nq:20 + 2 * nq], refs[23 + 2 * nq:])
        first = pl.program_id(0) == 0

        @pl.when(first)
        def _():
            ex.issue()
            acc_ref[...] = jnp.zeros_like(acc_ref)

        cw, cpool = _mixer_copies(wm_hbm, wm, wp_hbm, wp, sems)
        _start_on(first, [cw[2], cw[0], cw[1], cpool])
        _wait_on(first, [cw[2]])
        dmerged = _dot_nt(dh1_ref[...].astype(BF16), wm[:, 2].reshape(D, D))
        ga = _sig(zga[...] + bg_ref[:, :D])
        gb = _sig(zgb[...] + bg_ref[:, D:])
        dyc = dmerged * ga
        dyp = dmerged * gb
        _wait_on(first, [cw[0]])
        dza = (dmerged * _dot(s_ref[...], wm[:, 0].reshape(D, D))) * (ga * (1.0 - ga))
        _wait_on(first, [cw[1]])
        dzb = (dmerged * _dot(q_ref[...], wm[:, 1].reshape(D, D))) * (gb * (1.0 - gb))
        dzg_ref[:, :D] = dza.astype(BF16)
        dzg_ref[:, D:] = dzb.astype(BF16)
        acc_ref[0:1, :D] += jnp.sum(dza, axis=0, keepdims=True)
        acc_ref[0:1, D:] += jnp.sum(dzb, axis=0, keepdims=True)
        dyc_b = dyc.astype(BF16)
        dyp_b = dyp.astype(BF16)
        dyc_ref[...] = dyc_b
        dyp_ref[...] = dyp_b
        ds = _dot_nt(dyc_b, wm[:, 0].reshape(D, D))
        n, rl = _ln_stats(ac_ref[...])
        l = n * lg_ref[...] + lb_ref[...]
        sg = _sig(l)
        dl = ds * (sg * (1.0 + l * (1.0 - sg)))
        acc_ref[1:2, :D] += jnp.sum(dl * n, axis=0, keepdims=True)
        acc_ref[1:2, D:] += jnp.sum(dl, axis=0, keepdims=True)
        dn = dl * lg_ref[...]
        dac_ref[...] = rl * (dn - jnp.mean(dn, axis=-1, keepdims=True) - n * jnp.mean(dn * n, axis=-1, keepdims=True))
        dq = _dot_nt(dyp_b, wm[:, 1].reshape(D, D))
        mv = m_ref[...]
        _wait_on(first, [cpool])
        acc_ref[2:3, :D] += jnp.sum(dq * _pool_mix(mv, wp), axis=0, keepdims=True)
        dm2 = (dq * ps_ref[...]).astype(BF16)
        dm2_ref[...] = dm2
        dm_ref[...] = jnp.concatenate(
            [_dot_nt(dm2[:, g * PG:(g + 1) * PG], wp[:, g].reshape(PG, PG)) for g in range(4)], axis=1)

        @pl.when(pl.program_id(0) == nt - 1)
        def _():
            ex.finish()

    def tile(col=0):
        return pl.BlockSpec((TMS, D), lambda i: (i, col))

    def vec(w):
        return pl.BlockSpec((1, w), lambda i: (0, 0))

    anys = pl.BlockSpec(memory_space=pl.ANY)
    f32o, b16o = jax.ShapeDtypeStruct((tp, D), F32), jax.ShapeDtypeStruct((tp, D), BF16)
    res = pl.pallas_call(
        body, name="mix_bwd", grid=(nt,),
        in_specs=[tile(), tile(3), tile(4), tile(), tile(), tile(), tile(), vec(2 * D), vec(D), vec(D), vec(D), anys, anys]
        + [anys] * nq,
        out_specs=[tile(), tile(), pl.BlockSpec((TMS, 2 * D), lambda i: (i, 0)), tile(), tile(), tile(),
                   pl.BlockSpec((8, 2 * D), lambda i: (0, 0))] + [anys] * nq,
        out_shape=[f32o, f32o, jax.ShapeDtypeStruct((tp, 2 * D), BF16), b16o, b16o, b16o,
                   jax.ShapeDtypeStruct((8, 2 * D), F32)] + ex.out_shape,
        scratch_shapes=[pltpu.VMEM((NDEV, 3, D // NDEV, D), BF16), pltpu.VMEM((NDEV, 4, PG // NDEV, PG), BF16),
                        pltpu.SemaphoreType.DMA((4,))] + ex.scratch,
        compiler_params=_params(("arbitrary",), 48),
    )(dh1, z, z, s, q, ac, m, b_gate, ln_g, ln_b, pool_scale, g_mixw, g_pool, *qs)
    return res[:7], res[7:]


def _seq_bwd(dac, dm, dzg, z, w_dw, seq, qs):
    tp = z.shape[0]
    nt = tp // TM
    ex = _ChipExchange(qs)
    nq = no = ex.n

    def body(*refs):
        dac_l, dac_c, dac_r, dm_l, dm_c, dm_r, av_l, av, av_r, ag_l, ag, ag_r, dzg_ref, w_ref = refs[:14]
        dz_ref, acc_ref = refs[14 + nq:16 + nq]
        a3, d3, m3, da3, dp3, w3, dw3, da_sc, dp_sc = refs[16 + nq + no:25 + nq + no]
        ex.bind(refs[14:14 + nq], refs[16 + nq:16 + nq + no], refs[25 + nq + no:])
        i = pl.program_id(0)
        sub = lax.broadcasted_iota(jnp.int32, (NCB, 128), 0)

        @pl.when(i == 0)
        def _():
            ex.issue()
            dw3[...] = jnp.zeros_like(dw3)
            _tm_fill(w3, 0, 4, lambda r, l: w_ref[pl.ds(r, 8), l])

        _tm_fill_ext(a3, (av_l, ag_l), (av, ag), (av_r, ag_r), lambda vg, r, l: vg[0][r, l] * _sig(vg[1][r, l]), unroll=2)
        _tm_fill_ext(d3, dac_l, dac_c, dac_r, lambda ref, r, l: ref[r, l])
        _tm_fill_ext(m3, dm_l, dm_c, dm_r, lambda ref, r, l: ref[r, l])

        def conv(g, c):
            t0 = CONV_STEPS * g
            dcur = [_tm_at(d3, t0 + t + HALO) for t in range(CONV_STEPS)]
            accs = [None] * CONV_STEPS
            for k in range(CONV_K):
                wk = _tm_at(w3, k)
                prs = []
                for t in range(CONV_STEPS):
                    term = wk * _tm_at(d3, t0 + t + CONV_K - k)
                    accs[t] = term if accs[t] is None else accs[t] + term
                    prs.append(dcur[t] * _tm_at(a3, t0 + t + k + 1))
                while len(prs) > 1:
                    prs = [prs[j] + prs[j + 1] for j in range(0, len(prs) - 1, 2)] + prs[len(prs) - len(prs) % 2:]
                dw3[_tm_rows(k), :] += prs[0]
            s = dcur[0]
            for t in range(1, CONV_STEPS):
                s = s + dcur[t]
            dw3[_tm_rows(CONV_K), :] += s
            for t in range(CONV_STEPS):
                da3[_tm_rows(t0 + t), :] = accs[t]
            return c

        lax.fori_loop(0, TM // CONV_STEPS, conv, 0)

        for b in _edge_rows(seq, tp):
            e = lax.rem(b - i * TM + HALO + tp, tp)

            @pl.when(e < TME)
            def _():
                m3[_tm_rows(e), :] = _tm_at(m3, e) * _edge_gain(b, seq, tp, sub)

        inv = _by_group(sub, [1.0 / w for w in POOL_WINDOWS])

        def pool(g, c):
            for t in range(8):
                e = 8 * g + t + HALO
                sums = _nested_windows(lambda o: _tm_at(m3, e + o), [w // 2 + 1 - w for w in POOL_WINDOWS])
                dp3[_tm_rows(8 * g + t), :] = _by_group(sub, sums) * inv
            return c

        lax.fori_loop(0, TM // 8, pool, 0, unroll=2)

        _tm_read(da3, TM // 8, lambda r, l, tile: da_sc.__setitem__((r, l), tile))
        _tm_read(dp3, TM // 8, lambda r, l, tile: dp_sc.__setitem__((r, l), tile))
        sg = _sig(ag[...])
        da = da_sc[...]
        dz_ref[:, 0:D] = (da * sg).astype(BF16)
        dz_ref[:, D:2 * D] = (da * av[...] * (sg * (1.0 - sg))).astype(BF16)
        dz_ref[:, 2 * D:3 * D] = (dp_sc[...] - dm_c[...]).astype(BF16)
        dz_ref[:, 3 * D:] = dzg_ref[...]

        @pl.when(i == nt - 1)
        def _():
            _tm_read(dw3, 4, lambda r, l, tile: acc_ref.__setitem__((r, l), tile))
            ex.finish()

    tmaj = pltpu.VMEM((TM * NCB, 128), F32)
    text = pltpu.VMEM((TME * NCB, 128), F32)
    taps = pltpu.VMEM((32 * NCB, 128), F32)
    anys = pl.BlockSpec(memory_space=pl.ANY)
    res = pl.pallas_call(
        body, name="seq_bwd", grid=(nt,),
        in_specs=_halo_specs(0, nt) + _halo_specs(0, nt) + _halo_specs(0, nt) + _halo_specs(1, nt)
        + [pl.BlockSpec((TM, 2 * D), lambda i: (i, 0)), pl.BlockSpec((32, D), lambda i: (0, 0))] + [anys] * nq,
        out_specs=[pl.BlockSpec((TM, DIN), lambda i: (i, 0)), pl.BlockSpec((32, D), lambda i: (0, 0))] + [anys] * no,
        out_shape=[jax.ShapeDtypeStruct((tp, DIN), BF16), jax.ShapeDtypeStruct((32, D), F32)] + ex.out_shape,
        scratch_shapes=[text, text, text, tmaj, tmaj, taps, taps, pltpu.VMEM((TM, D), F32), pltpu.VMEM((TM, D), F32)]
        + ex.scratch,
        compiler_params=_params(("arbitrary",), 48),
    )(dac, dac, dac, dm, dm, dm, z, z, z, z, z, z, dzg, w_dw, *qs)
    return res[:2], res[2:]


def _in_bwd(dz, h0, dh1, g_mix, w_g, seq, qs):
    tp = h0.shape[0]
    tm = _pick(tp, TM_IO)
    nt = tp // tm
    ex = _ChipExchange(qs)
    nq = no = ex.n

    def body(*refs):
        dz_ref, h_ref, dh1_ref, g_ref, w_hbm = refs[:5]
        gx_ref, gmeta_ref, acc_ref = refs[5 + nq:8 + nq]
        w_vm, sems = refs[8 + nq + no:10 + nq + no]
        ex.bind(refs[5:5 + nq], refs[8 + nq:8 + nq + no], refs[10 + nq + no:])
        i = pl.program_id(0)

        @pl.when(i == 0)
        def _():
            ex.issue()
            acc_ref[...] = jnp.zeros_like(acc_ref)

        cps = [pltpu.make_async_copy(a, b, sems.at[k]) for k, (a, b) in enumerate(_win_pairs(w_hbm, w_vm))]
        _start_on(i == 0, cps)
        _wait_on(i == 0, cps[:2])
        du = _dot_nt(dz_ref[:, :DIN // 2], w_vm[0])
        _wait_on(i == 0, cps[2:])
        du = du + _dot_nt(dz_ref[:, DIN // 2:], w_vm[1])
        h = h_ref[...]
        r = lax.rsqrt(jnp.mean(h * h, axis=-1, keepdims=True) + RMS_EPS)
        n0 = h * r
        acc_ref[0:1, :] += jnp.sum(du * n0, axis=0, keepdims=True)
        dn = du * g_ref[...]
        gx_ref[...] = dh1_ref[...] + r * (dn - n0 * jnp.mean(dn * n0, axis=-1, keepdims=True))

        @pl.when(i == nt - 1)
        def _():
            gmeta_ref[...] = gx_ref[pl.ds(tm - N_META, N_META), :]
            ex.finish()

    tile = pl.BlockSpec((tm, D), lambda i: (i, 0))
    anys = pl.BlockSpec(memory_space=pl.ANY)
    res = pl.pallas_call(
        body, name="in_bwd", grid=(nt,),
        in_specs=[pl.BlockSpec((tm, DIN), lambda i: (i, 0)), tile, tile, pl.BlockSpec((1, D), lambda i: (0, 0)), anys]
        + [anys] * nq,
        out_specs=[tile, pl.BlockSpec((N_META, D), lambda i: (0, 0)), pl.BlockSpec((8, D), lambda i: (0, 0))] + [anys] * no,
        out_shape=[jax.ShapeDtypeStruct((seq, D), F32), jax.ShapeDtypeStruct((N_META, D), F32),
                   jax.ShapeDtypeStruct((8, D), F32)] + ex.out_shape,
        scratch_shapes=[pltpu.VMEM((2, D, DIN // 2), BF16), pltpu.SemaphoreType.DMA((NDEV,))] + ex.scratch,
        compiler_params=_params(("arbitrary",), 58),
    )(dz, h0, dh1, g_mix, w_g, *qs)
    return res[:3], res[3:]


def _wgrad_in(u, dz, qs):
    tp = u.shape[0]
    tm = _pick(tp, TM_WG)
    nt = tp // tm
    half = DIN // 2
    ex = _ChipExchange(qs)
    nq = ex.n

    def body(*refs):
        u_ref, dz_ref = refs[:2]
        o_ref, acc = refs[2 + nq], refs[3 + 2 * nq]
        ex.bind(refs[2:2 + nq], refs[3 + nq:3 + 2 * nq], refs[4 + 2 * nq:])
        h, t = pl.program_id(0), pl.program_id(1)

        @pl.when((h == 0) & (t == 0))
        def _():
            ex.issue()

        @pl.when(t == 0)
        def _():
            acc[...] = jnp.zeros_like(acc)

        acc[...] += _dot_tn(u_ref[...], dz_ref[...])

        @pl.when(t == nt - 1)
        def _():
            for d in range(4):
                o_ref[d] = acc[:, INB * d:INB * (d + 1)].astype(BF16)

        @pl.when((h == 1) & (t == nt - 1))
        def _():
            ex.finish()

    anys = pl.BlockSpec(memory_space=pl.ANY)
    res = pl.pallas_call(
        body, name="wgrad_in", grid=(2, nt),
        in_specs=[pl.BlockSpec((tm, D), lambda h, t: (t, 0)), pl.BlockSpec((tm, half), lambda h, t: (t, h))] + [anys] * nq,
        out_specs=[pl.BlockSpec((4, D, INB), lambda h, t: (h, 0, 0), pipeline_mode=pl.Buffered(1))] + [anys] * nq,
        out_shape=[jax.ShapeDtypeStruct((NDEV, D, INB), BF16)] + ex.out_shape,
        scratch_shapes=[pltpu.VMEM((D, half), F32)] + ex.scratch,
        compiler_params=_params(("arbitrary", "arbitrary"), 52),
    )(u, dz, *qs)
    return res[0], res[1:]


def _wgrad_mix(s, dyc, q, dyp, merged, dh1, m, dm2):
    tp = s.shape[0]
    tm = _pick(tp, TM_WM)
    nt = tp // tm
    rb = D // NDEV

    def body(s_ref, dyc_ref, q_ref, dyp_ref, mg_ref, dh1_ref, m_ref, dm2_ref, o_ref, op_ref, acc, accp):
        t = pl.program_id(0)

        @pl.when(t == 0)
        def _():
            acc[...] = jnp.zeros_like(acc)
            accp[...] = jnp.zeros_like(accp)

        acc[0] += _dot_tn(s_ref[...], dyc_ref[...])
        acc[1] += _dot_tn(q_ref[...], dyp_ref[...])
        acc[2] += _dot_tn(mg_ref[...], dh1_ref[...].astype(BF16))
        for g in range(4):
            accp[g] += _dot_tn(m_ref[:, g * PG:(g + 1) * PG], dm2_ref[:, g * PG:(g + 1) * PG])

        @pl.when(t == nt - 1)
        def _():
            for d in range(NDEV):
                for k in range(3):
                    o_ref[d, k] = acc[k, rb * d:rb * (d + 1), :].astype(BF16)
                for g in range(4):
                    op_ref[d, g] = accp[g, 32 * d:32 * (d + 1), :].astype(BF16)

    tile = pl.BlockSpec((tm, D), lambda t: (t, 0))
    return pl.pallas_call(
        body, name="wgrad_mix", grid=(nt,),
        in_specs=[tile] * 8,
        out_specs=[pl.BlockSpec((NDEV, 3, rb, D), lambda t: (0, 0, 0, 0), pipeline_mode=pl.Buffered(1)),
                   pl.BlockSpec((NDEV, 4, 32, PG), lambda t: (0, 0, 0, 0), pipeline_mode=pl.Buffered(1))],
        out_shape=[jax.ShapeDtypeStruct((NDEV, 3, rb, D), BF16), jax.ShapeDtypeStruct((NDEV, 4, 32, PG), BF16)],
        scratch_shapes=[pltpu.VMEM((3, D, D), F32), pltpu.VMEM((4, PG, PG), F32)],
        compiler_params=_params(("arbitrary",), 56),
    )(s, dyc, q, dyp, merged, dh1, m, dm2)


def _wgrad_gu(v, dfg, dfu):
    tp = v.shape[0]
    tm = _pick(tp, TM_WG)
    nt = tp // tm

    def body(v_ref, dg_ref, du_ref, o_ref, acc):
        k, t = pl.program_id(0), pl.program_id(2)

        @pl.when(t == 0)
        def _():
            acc[...] = jnp.zeros_like(acc)

        @pl.when(k == 0)
        def _():
            acc[...] += _dot_tn(dg_ref[...], v_ref[...])

        @pl.when(k == 1)
        def _():
            acc[...] += _dot_tn(du_ref[...], v_ref[...])

        @pl.when(t == nt - 1)
        def _():
            for d in range(4):
                o_ref[d] = acc[FFB * d:FFB * (d + 1), :].astype(BF16)

    return pl.pallas_call(
        body, name="wgrad_gu", grid=(2, 2, nt),
        in_specs=[pl.BlockSpec((tm, D), lambda k, h, t: (t, 0)),
                  pl.BlockSpec((tm, FFC), lambda k, h, t: (t * (1 - k), h * (1 - k))),
                  pl.BlockSpec((tm, FFC), lambda k, h, t: (t * k, h * k))],
        out_specs=pl.BlockSpec((4, None, FFB, D), lambda k, h, t: (h, k, 0, 0), pipeline_mode=pl.Buffered(1)),
        out_shape=jax.ShapeDtypeStruct((NDEV, 2, FFB, D), BF16),
        scratch_shapes=[pltpu.VMEM((FFC, D), F32)],
        compiler_params=_params(("arbitrary",) * 3, 48),
    )(v, dfg, dfu)


def _wgrad_down(f, dh2):
    tp = f.shape[0]
    tm = _pick(tp, TM_WG)
    nt = tp // tm

    def body(f_ref, d_ref, o_ref, acc):
        t = pl.program_id(1)

        @pl.when(t == 0)
        def _():
            acc[...] = jnp.zeros_like(acc)

        acc[...] += _dot_tn(f_ref[...], d_ref[...].astype(BF16))

        @pl.when(t == nt - 1)
        def _():
            for d in range(4):
                o_ref[d] = acc[FFB * d:FFB * (d + 1), :].astype(BF16)

    return pl.pallas_call(
        body, name="wgrad_down", grid=(2, nt),
        in_specs=[pl.BlockSpec((tm, FFC), lambda h, t: (t, h)), pl.BlockSpec((tm, D), lambda h, t: (t, 0))],
        out_specs=pl.BlockSpec((4, FFB, D), lambda h, t: (h, 0, 0), pipeline_mode=pl.Buffered(1)),
        out_shape=jax.ShapeDtypeStruct((NDEV, FFB, D), BF16),
        scratch_shapes=[pltpu.VMEM((FFC, D), F32)],
        compiler_params=_params(("arbitrary", "arbitrary"), 48),
    )(f, dh2)


def kernel(x, meta_tokens, g_mix, w_in, b_gate, w_dw, b_dw, ln_g, ln_b, w_conv_out, w_pool, pool_scale, w_pool_out, w_o, g_ffn, w_ffn_gate, w_ffn_up, w_ffn_down, g_final, loss_target, m_meta_tokens, m_g_mix, m_w_in, m_b_gate, m_w_dw, m_b_dw, m_ln_g, m_ln_b, m_w_conv_out, m_w_pool, m_pool_scale, m_w_pool_out, m_w_o, m_g_ffn, m_w_ffn_gate, m_w_ffn_up, m_w_ffn_down, m_g_final, v_meta_tokens, v_g_mix, v_w_in, v_b_gate, v_w_dw, v_b_dw, v_ln_g, v_ln_b, v_w_conv_out, v_w_pool, v_pool_scale, v_w_pool_out, v_w_o, v_g_ffn, v_w_ffn_gate, v_w_ffn_up, v_w_ffn_down, v_g_final):
    seq = x.shape[1]
    tp = -(-(seq + 2 * HALO) // TM) * TM
    tm_in = _pick(tp, TM_IO)
    nx_last = seq - (tp // tm_in - 1) * tm_in
    assert 0 < nx_last <= tm_in - 2 * HALO and nx_last % 8 == 0 and 0 < seq - (tp // TM - 1) * TM

    whole = (Ellipsis,)
    ag_small = _Gather(
        [((48, D // NDEV), [(meta_tokens, pl.ds(0, N_META), whole), (w_dw, pl.ds(N_META, CONV_K), 0)])], [F32])
    ag_mix = _Gather([((3, D // NDEV, D), [(w_conv_out, 0, 0), (w_pool_out, 1, 0), (w_o, 2, 0)]),
                      ((4, PG // NDEV, PG), [(w_pool, whole, 0)])], [BF16, BF16])
    def tr(a):
        return jnp.swapaxes(a, 1, 2)

    ag_gu = _Gather([((2, FFB, D), [(tr(w_ffn_gate), 0, 0), (tr(w_ffn_up), 1, 0)])], [BF16])
    ag_dn = _Gather([((FFB, D), [(w_ffn_down, whole, 0)])], [BF16])

    mx, my = lax.axis_index("x"), lax.axis_index("y")
    order = jnp.stack([2 * mx + my, 2 * mx + 1 - my, 2 * (1 - mx) + my, 2 * (1 - mx) + 1 - my]).astype(jnp.int32)
    (h0, z, u, g_in), (g_mixw, g_pool), g_small = _fwd_in(x[0], g_mix, w_in, order, tp, ag_mix, ag_small)
    wdw_full = g_small.transpose(1, 0, 2).reshape(48, D)[N_META:]
    (ac, m), (w_gu,) = _seq_fwd(z, wdw_full, b_dw, seq, ag_gu)
    (h1, s, merged, q), (g_down,) = _mix_fwd(ac, m, z, h0, b_gate, ln_g, ln_b, pool_scale, g_mixw, g_pool, ag_dn)
    w_dn = g_down.reshape(2, FFC, D)
    fg, fu, v, f, dh2, head_acc = _ffn_fwd(h1, loss_target[0], g_ffn, g_final.reshape(1, D), w_gu, w_dn)

    dfg, dfu, dh1, ffn_acc = _ffn_bwd(dh2, fg, fu, h1, g_ffn, w_gu, w_dn)
    own_f, sib_f, q_f = _rs_pair("rs_pair_ffn", [_wgrad_gu(v, dfg, dfu), _wgrad_down(f, dh2)])
    (dac, dm, dzg, dyc, dyp, dm2, mix_acc), rel_dn = _mix_bwd(
        dh1, z, s, q, ac, m, b_gate, ln_g, ln_b, pool_scale, g_mixw, g_pool, q_f[1:])
    p_mix = _wgrad_mix(s, dyc, q, dyp, merged, dh1, m, dm2)
    own_m, sib_m, q_m = _rs_pair("rs_pair_mix", list(p_mix))
    (dz, seq_acc), rel_gu = _seq_bwd(dac, dm, dzg, z, wdw_full, seq, q_f[:1])
    rel_f = [rel_gu[0], rel_dn[0]]
    p_in, rel_m = _wgrad_in(u, dz, q_m)
    own_i, sib_i, q_i = _rs_pair("rs_pair_in", [p_in])
    (grad_x, g_meta, in_acc), rel_i = _in_bwd(dz, h0, dh1, g_mix, g_in, seq, q_i)
    small_g = jnp.concatenate([g_meta, seq_acc[:CONV_K], jnp.zeros((1, D), F32)], axis=0)
    p_small = small_g.reshape(48, NDEV, D // NDEV).transpose(1, 0, 2).astype(BF16)
    rep_g = jnp.concatenate([
        in_acc[0:1], mix_acc[0:1, :D], mix_acc[0:1, D:], seq_acc[CONV_K:CONV_K + 1], mix_acc[1:2, :D], mix_acc[1:2, D:],
        mix_acc[2:3, :D], ffn_acc[0:1], head_acc[1:2], head_acc[0:1], jnp.zeros((REP_ROWS - 10, D), F32)], axis=0)
    own_s, sib_s, rel_s, rep_all = _reduce_scatter([p_small], rep_g)
    owns = [own_i[0], own_s[0], own_m[0], own_m[1], own_f[0], own_f[1]]
    sibs = [sib_i[0], sib_s[0], sib_m[0], sib_m[1], sib_f[0], sib_f[1]]
    rels = [rel_i[0], rel_s[0], rel_m[0], rel_m[1], rel_f[0], rel_f[1]]

    def lead(a):
        return a.reshape(1, *a.shape)

    def stack4(a, lead_dims):
        return a.reshape(*lead_dims, 1, 4 * 32, PG)

    (r_in,) = _adamw_multi("adamw_in", lead(owns[0]), sibs[0][:, None], rels[0][:, None], [w_in], [m_w_in], [v_w_in], 4)
    r_meta, r_dw = _adamw_meta_dw(owns[1], sibs[1], rels[1], (meta_tokens, m_meta_tokens, v_meta_tokens),
                                  (w_dw, m_w_dw, v_w_dw))
    r_conv, r_pout, r_o = _adamw_multi("adamw_mix", owns[2], sibs[2], rels[2], [w_conv_out, w_pool_out, w_o],
                                       [m_w_conv_out, m_w_pool_out, m_w_o], [v_w_conv_out, v_w_pool_out, v_w_o], 1)
    (r_pool,) = _adamw_multi("adamw_pool", stack4(owns[3], ()), stack4(sibs[3], (1,)), stack4(rels[3], (3,)),
                             [w_pool.reshape(1, 128, PG)], [m_w_pool.reshape(1, 128, PG)], [v_w_pool.reshape(1, 128, PG)], 1)
    r_pool = tuple(a.reshape(w_pool.shape) for a in r_pool)
    r_gate, r_up = _adamw_multi("adamw_gu", owns[4], sibs[4], rels[4], [tr(w_ffn_gate), tr(w_ffn_up)],
                                [tr(m_w_ffn_gate), tr(m_w_ffn_up)], [tr(v_w_ffn_gate), tr(v_w_ffn_up)], 2)
    r_gate, r_up = tuple(tr(a) for a in r_gate), tuple(tr(a) for a in r_up)
    (r_down,) = _adamw_multi("adamw_down", lead(owns[5]), sibs[5][:, None], rels[5][:, None],
                             [w_ffn_down], [m_w_ffn_down], [v_w_ffn_down], 2)
    row = (1, D)
    loss, reps = _adamw_rep(
        rep_all,
        [g_mix, b_gate, b_dw, ln_g, ln_b, pool_scale, g_ffn, g_final.reshape(row)],
        [m_g_mix, m_b_gate, m_b_dw, m_ln_g, m_ln_b, m_pool_scale, m_g_ffn, m_g_final.reshape(row)],
        [v_g_mix, v_b_gate, v_b_dw, v_ln_g, v_ln_b, v_pool_scale, v_g_ffn, v_g_final.reshape(row)])
    r_gmix, r_bg, r_bdw, r_lg, r_lb, r_ps, r_gffn, r_gfin = reps
    r_gfin = tuple(a.reshape(D) for a in r_gfin)

    in_order = [r_meta, r_gmix, r_in, r_bg, r_dw, r_bdw, r_lg, r_lb, r_conv, r_pool, r_ps, r_pout, r_o, r_gffn,
                r_gate, r_up, r_down, r_gfin]
    return (loss.reshape(()), grad_x[None], *[r[0] for r in in_order], *[r[1] for r in in_order],
            *[r[2] for r in in_order], *[r[3] for r in in_order])
```

```python
import math

import jax
import jax.numpy as jnp
from jax import lax
from jax.experimental import pallas as pl
from jax.experimental.pallas import tpu as pltpu

F32, BF16 = jnp.float32, jnp.bfloat16
MESH_ID = pl.DeviceIdType.MESH
NDEV = 8

D = 1024
N_META = 16
CONV_K = 31
HALO = 16
POOL_WINDOWS = (2, 4, 8, 16)
PG = 256
DIN = 5 * D
DFF = 2816
FFB = DFF // NDEV
FFC = DFF // 2
INB = DIN // NDEV
RMS_EPS = 1e-6
LN_EPS = 1e-5
ADAM_LR, ADAM_B1, ADAM_B2, ADAM_EPS, ADAM_WD, ADAM_STEP = 0.001, 0.9, 0.999, 1e-08, 0.01, 10

TM = 384
TMS = 384
TM_IO = 704
TM_WG = 1408
TM_WM = 704
MIB = 2 ** 20


def _sig(x):
    return 0.5 * jnp.tanh(0.5 * x) + 0.5


def _dot(a, b):
    return jnp.dot(a, b, preferred_element_type=F32)


def _dot_nt(a, b):
    return lax.dot_general(a, b, (((1,), (1,)), ((), ())), preferred_element_type=F32)


def _dot_tn(a, b):
    return lax.dot_general(a, b, (((0,), (0,)), ((), ())), preferred_element_type=F32)


def _pick(tp, pref):
    return pref if tp % pref == 0 else TM


def _params(sem, vmem_mib):
    return pltpu.CompilerParams(dimension_semantics=sem, vmem_limit_bytes=vmem_mib * MIB)


def _load_once(first, pairs, sems):
    @pl.when(first)
    def _():
        cps = [pltpu.make_async_copy(s, d, sems.at[k]) for k, (s, d) in enumerate(pairs)]
        for cp in cps:
            cp.start()
        for cp in cps:
            cp.wait()


def _place():
    x, y, c = lax.axis_index("x"), lax.axis_index("y"), lax.axis_index("c")
    return x, y, c


class _Gather:
    def __init__(self, groups, dtypes):
        self.groups, self.dtypes, self.n = groups, dtypes, len(groups)
        self.arrays = [a for _, parts in groups for a, _, _ in parts]
        self.out_shape = [jax.ShapeDtypeStruct((NDEV, *s), dt) for (s, _), dt in zip(groups, dtypes)]
        self.scratch = [pltpu.VMEM(s, dt) for (s, _), dt in zip(groups, dtypes)] + [
            pltpu.SemaphoreType.DMA((7 * self.n,)), pltpu.SemaphoreType.DMA((7 * self.n,)),
            pltpu.SemaphoreType.DMA((self.n,))]

    def bind(self, ins, outs, scratch):
        self.ins, self.outs, self.stages = ins, outs, scratch[:self.n]
        self.send_sems, self.recv_sems, self.local_sems = scratch[self.n:]
        return self

    def _copy(self, w, k, block, to, src=None):
        dst = self.outs[w].at[4 * block[0] + 2 * block[1] + block[2]]
        return pltpu.make_async_remote_copy(
            src_ref=dst if src is None else src, dst_ref=dst,
            send_sem=self.send_sems.at[7 * w + k], recv_sem=self.recv_sems.at[7 * w + k],
            device_id=to, device_id_type=MESH_ID)

    def _first(self):
        x, y, c = _place()
        me, sibling = (x, y, c), (x, y, 1 - c)
        chips = [(1 - x, y), (x, 1 - y), (1 - x, 1 - y)]
        mine, first = [], []
        for w in range(self.n):
            mine.append(pltpu.make_async_copy(self.stages[w], self.outs[w].at[4 * x + 2 * y + c], self.local_sems.at[w]))
            first.append(self._copy(w, 0, me, sibling, src=self.stages[w]))
            first += [self._copy(w, 1 + j, me, (*chip, c), src=self.stages[w]) for j, chip in enumerate(chips)]
        return mine, first

    def _passed(self):
        x, y, c = _place()
        chips = [(1 - x, y), (x, 1 - y), (1 - x, 1 - y)]
        return [self._copy(w, 4 + j, (*chip, c), (x, y, 1 - c)) for w in range(self.n) for j, chip in enumerate(chips)]

    def issue(self):
        a = 0
        for w in range(self.n):
            shape, parts = self.groups[w]
            if sum(arr.size for arr, _, _ in parts) < math.prod(shape):
                self.stages[w][...] = jnp.zeros(shape, self.dtypes[w])
            for _, dst, src in parts:
                self.stages[w][dst] = self.ins[a][src].astype(self.dtypes[w])
                a += 1
        mine, first = self._first()
        for cp in mine + first:
            cp.start()

    def forward(self):
        x, y, c = _place()
        chips = [(1 - x, y), (x, 1 - y), (1 - x, 1 - y)]
        passed = self._passed()
        for w in range(self.n):
            for j, chip in enumerate(chips):
                self._copy(w, 1 + j, (*chip, c), (x, y, c)).wait_recv()
                passed[3 * w + j].start()

    def finish(self):
        x, y, c = _place()
        chips = [(1 - x, y), (x, 1 - y), (1 - x, 1 - y)]
        for w in range(self.n):
            self._copy(w, 0, (x, y, 1 - c), (x, y, c)).wait_recv()
            for j, chip in enumerate(chips):
                self._copy(w, 4 + j, (*chip, 1 - c), (x, y, c)).wait_recv()
        mine, first = self._first()
        for cp in first + self._passed():
            cp.wait_send()
        for cp in mine:
            cp.wait()


class _ChipExchange:
    def __init__(self, qs):
        self.n = len(qs)
        self.out_shape = [jax.ShapeDtypeStruct(q.shape, q.dtype) for q in qs]
        self.scratch = [pltpu.SemaphoreType.DMA((3 * self.n,)), pltpu.SemaphoreType.DMA((3 * self.n,))]

    def bind(self, qs, rels, scratch):
        self.qs, self.rels = qs, rels
        self.send_sems, self.recv_sems = scratch
        return self

    def _copies(self):
        x, y, c = _place()
        chips = [(1 - x, y), (x, 1 - y), (1 - x, 1 - y)]
        return [pltpu.make_async_remote_copy(
            src_ref=self.qs[w].at[j], dst_ref=self.rels[w].at[j],
            send_sem=self.send_sems.at[3 * w + j], recv_sem=self.recv_sems.at[3 * w + j],
            device_id=(*chips[j], c), device_id_type=MESH_ID) for w in range(self.n) for j in range(3)]

    def issue(self):
        for cp in self._copies():
            cp.start()

    def finish(self):
        cps = self._copies()
        for cp in cps:
            cp.wait_recv()
        for cp in cps:
            cp.wait_send()


def _reduce_scatter(parts, small):
    n = len(parts)
    blks = [p.shape[1:] for p in parts]

    def body(*refs):
        ps, small_ref = refs[:n], refs[n]
        o = n + 1
        owns, sibs, rels, small_out = refs[o:o + n], refs[o + n:o + 2 * n], refs[o + 2 * n:o + 3 * n], refs[o + 3 * n]
        o += 3 * n + 1
        pa, pb, qst = refs[o:o + n], refs[o + n:o + 2 * n], refs[o + 2 * n:o + 3 * n]
        s1_send, s1_recv, s2_send, s2_recv, sm_send, sm_recv, lsem = refs[o + 3 * n:]
        x, y, c = _place()
        me = 4 * x + 2 * y + c
        sibling = (x, y, 1 - c)
        chips = [(1 - x, y), (x, 1 - y), (1 - x, 1 - y)]
        all_chips = [(x, y)] + chips

        own_cps = []
        for w in range(n):
            cp = pltpu.make_async_copy(ps[w].at[me], owns[w], lsem.at[w])
            cp.start()
            own_cps.append(cp)
        sm_own = pltpu.make_async_copy(small_ref, small_out.at[me], lsem.at[n])
        sm_own.start()

        def small_copy(r):
            peer = ((x + (r >> 2)) % 2, (y + ((r >> 1) & 1)) % 2, (c + (r & 1)) % 2)
            return pltpu.make_async_remote_copy(
                src_ref=small_ref, dst_ref=small_out.at[me], send_sem=sm_send.at[r - 1], recv_sem=sm_recv.at[r - 1],
                device_id=peer, device_id_type=MESH_ID)

        sm_cps = [small_copy(r) for r in range(1, NDEV)]
        for cp in sm_cps:
            cp.start()

        def pair_copy(w, rel):
            cx, cy = all_chips[rel]
            return pltpu.make_async_remote_copy(
                src_ref=ps[w].at[4 * cx + 2 * cy + (1 - c)], dst_ref=sibs[w].at[rel],
                send_sem=s1_send.at[4 * w + rel], recv_sem=s1_recv.at[4 * w + rel],
                device_id=sibling, device_id_type=MESH_ID)

        def chip_copy(w, j):
            return pltpu.make_async_remote_copy(
                src_ref=qst[w].at[j], dst_ref=rels[w].at[j],
                send_sem=s2_send.at[3 * w + j], recv_sem=s2_recv.at[3 * w + j],
                device_id=(*chips[j], c), device_id_type=MESH_ID)

        pair_cps = [pair_copy(w, rel) for w in range(n) for rel in (1, 2, 3, 0)]
        for cp in pair_cps:
            cp.start()
        chip_cps = []
        for w in range(n):
            for j, (cx, cy) in enumerate(chips):
                pair_copy(w, 1 + j).wait_recv()
                la = pltpu.make_async_copy(ps[w].at[4 * cx + 2 * cy + c], pa[w], lsem.at[n + 1])
                lb = pltpu.make_async_copy(sibs[w].at[1 + j], pb[w], lsem.at[n + 2])
                la.start()
                lb.start()
                la.wait()
                lb.wait()
                qst[w][j] = (pa[w][...].astype(F32) + pb[w][...].astype(F32)).astype(BF16)
                cp = chip_copy(w, j)
                cp.start()
                chip_cps.append(cp)
        for w in range(n):
            pair_copy(w, 0).wait_recv()
            for j in range(3):
                chip_copy(w, j).wait_recv()
        for cp in sm_cps:
            cp.wait_recv()
        for cp in pair_cps + chip_cps + sm_cps:
            cp.wait_send()
        for cp in own_cps:
            cp.wait()
        sm_own.wait()

    any_spec = pl.BlockSpec(memory_space=pl.ANY)
    outs = pl.pallas_call(
        body, name="rs_grads",
        out_shape=[jax.ShapeDtypeStruct(b, BF16) for b in blks]
        + [jax.ShapeDtypeStruct((4, *b), BF16) for b in blks]
        + [jax.ShapeDtypeStruct((3, *b), BF16) for b in blks]
        + [jax.ShapeDtypeStruct((NDEV, *small.shape), F32)],
        in_specs=[any_spec] * (n + 1),
        out_specs=[any_spec] * (3 * n + 1),
        scratch_shapes=[pltpu.VMEM(b, BF16) for b in blks] + [pltpu.VMEM(b, BF16) for b in blks]
        + [pltpu.VMEM((3, *b), BF16) for b in blks]
        + [pltpu.SemaphoreType.DMA((4 * n,)), pltpu.SemaphoreType.DMA((4 * n,)),
           pltpu.SemaphoreType.DMA((3 * n,)), pltpu.SemaphoreType.DMA((3 * n,)),
           pltpu.SemaphoreType.DMA((NDEV - 1,)), pltpu.SemaphoreType.DMA((NDEV - 1,)),
           pltpu.SemaphoreType.DMA((n + 3,))],
        compiler_params=pltpu.CompilerParams(vmem_limit_bytes=40 * MIB),
    )(*parts, small)
    return outs[:n], outs[n:2 * n], outs[2 * n:3 * n], outs[3 * n]


class _PairSum:
    def __init__(self, parts, keep_q=True):
        self.n = n = len(parts)
        self.keep_q = keep_q
        blks = [p.shape[1:] for p in parts]
        self.out_shape = [jax.ShapeDtypeStruct(b, BF16) for b in blks] + [jax.ShapeDtypeStruct((1, *b), BF16) for b in blks]
        if keep_q:
            self.out_shape += [jax.ShapeDtypeStruct((3, *b), BF16) for b in blks]
        self.scratch = [pltpu.VMEM((3, *b), BF16) for b in blks] * 3 + [
            pltpu.SemaphoreType.DMA((4 * n,)), pltpu.SemaphoreType.DMA((4 * n,)), pltpu.SemaphoreType.DMA((5 * n,))]

    def bind(self, ps, outs, scratch):
        n = self.n
        self.ps, self.owns, self.sibs, self.qs = ps, outs[:n], outs[n:2 * n], outs[2 * n:]
        self.pa, self.pb, self.qst = scratch[:n], scratch[n:2 * n], scratch[2 * n:3 * n]
        self.s_send, self.s_recv, self.lsem = scratch[3 * n:]
        return self

    def _local(self, with_q):
        n = self.n
        x, y, c = _place()
        chips = [(1 - x, y), (x, 1 - y), (1 - x, 1 - y)]
        own = [pltpu.make_async_copy(self.ps[w].at[4 * x + 2 * y + c], self.owns[w], self.lsem.at[w]) for w in range(n)]
        mine = [[pltpu.make_async_copy(self.ps[w].at[4 * cx + 2 * cy + c], self.pa[w].at[j], self.lsem.at[2 * n + 3 * w + j])
                 for j, (cx, cy) in enumerate(chips)] for w in range(n)]
        outq = [pltpu.make_async_copy(self.qst[w], self.qs[w], self.lsem.at[n + w]) for w in range(n)] if with_q else []
        return own, mine, outq

    def _pair(self, w, rel):
        x, y, c = _place()
        cx, cy = [(x, y), (1 - x, y), (x, 1 - y), (1 - x, 1 - y)][rel]
        return pltpu.make_async_remote_copy(
            src_ref=self.ps[w].at[4 * cx + 2 * cy + (1 - c)],
            dst_ref=self.sibs[w].at[0] if rel == 0 else self.pb[w].at[rel - 1],
            send_sem=self.s_send.at[4 * w + rel], recv_sem=self.s_recv.at[4 * w + rel],
            device_id=(x, y, 1 - c), device_id_type=MESH_ID)

    def issue(self):
        own, mine, _ = self._local(False)
        for cp in own + [cp for row in mine for cp in row]:
            cp.start()
        for w in range(self.n):
            for rel in (1, 2, 3, 0):
                self._pair(w, rel).start()

    def finish(self):
        own, mine, outq = self._local(self.keep_q)
        for w in range(self.n):
            for j in range(3):
                self._pair(w, 1 + j).wait_recv()
                mine[w][j].wait()
                self.qst[w][j] = (self.pa[w][j].astype(F32) + self.pb[w][j].astype(F32)).astype(BF16)
            if self.keep_q:
                outq[w].start()
        for w in range(self.n):
            self._pair(w, 0).wait_recv()
        for w in range(self.n):
            for rel in range(4):
                self._pair(w, rel).wait_send()
        for cp in own + outq:
            cp.wait()

    def results(self, outs):
        n = self.n
        return outs[:n], outs[n:2 * n], outs[2 * n:3 * n]


def _rs_pair(name, parts):
    ps = _PairSum(parts)
    n = ps.n

    def body(*refs):
        ps.bind(refs[:n], refs[n:4 * n], refs[4 * n:])
        ps.issue()
        ps.finish()

    any_spec = pl.BlockSpec(memory_space=pl.ANY)
    outs = pl.pallas_call(
        body, name=name, out_shape=ps.out_shape,
        in_specs=[any_spec] * n, out_specs=[any_spec] * (3 * n), scratch_shapes=ps.scratch,
        compiler_params=pltpu.CompilerParams(vmem_limit_bytes=48 * MIB),
    )(*parts)
    return ps.results(outs)


def _adamw_math(g, w, m, v):
    m = ADAM_B1 * m + (1.0 - ADAM_B1) * g
    v = ADAM_B2 * v + (1.0 - ADAM_B2) * (g * g)
    m_hat = m / (1.0 - ADAM_B1 ** ADAM_STEP)
    v_hat = v / (1.0 - ADAM_B2 ** ADAM_STEP)
    delta = -ADAM_LR * (m_hat / (jnp.sqrt(v_hat) + ADAM_EPS) + ADAM_WD * w)
    return delta, m, v


def _adamw_multi(name, own, sib, rel, ws, ms, vs, row_grid):
    k_n, r_n, c_n = own.shape
    rbk = r_n // row_grid

    def body(*refs):
        own_ref, sib_ref, r0_ref, r1_ref, r2_ref = refs[:5]
        w_refs, m_refs, v_refs = refs[5:5 + k_n], refs[5 + k_n:5 + 2 * k_n], refs[5 + 2 * k_n:5 + 3 * k_n]
        outs = refs[5 + 3 * k_n:]
        for k in range(k_n):
            g = own_ref[k].astype(F32) + sib_ref[k].astype(F32)
            g = g + r0_ref[k].astype(F32)
            g = g + r1_ref[k].astype(F32)
            g = g + r2_ref[k].astype(F32)
            delta, mm, vv = _adamw_math(g, w_refs[k][0], m_refs[k][0], v_refs[k][0])
            outs[4 * k][0] = g
            outs[4 * k + 1][0] = delta
            outs[4 * k + 2][0] = mm
            outs[4 * k + 3][0] = vv

    def lead(j):
        return pl.BlockSpec((None, k_n, rbk, c_n), lambda g: (j, 0, g, 0))

    wspec = pl.BlockSpec((1, rbk, c_n), lambda g: (0, g, 0))
    shp = jax.ShapeDtypeStruct((1, r_n, c_n), F32)
    res = pl.pallas_call(
        body, name=name, grid=(row_grid,),
        in_specs=[pl.BlockSpec((k_n, rbk, c_n), lambda g: (0, g, 0)), lead(0), lead(0), lead(1), lead(2)] + [wspec] * (3 * k_n),
        out_specs=[wspec] * (4 * k_n), out_shape=[shp] * (4 * k_n),
        compiler_params=_params(("arbitrary",), 40),
    )(own, sib, rel, rel, rel, *ws, *ms, *vs)
    return [tuple(res[4 * k:4 * k + 4]) for k in range(k_n)]


def _adamw_meta_dw(own, sib, rel, meta, dw):
    def body(own_ref, sib_ref, rel_ref, wm, mm, vm, wd, md, vd, *outs):
        def gsum(rows):
            g = own_ref[rows, :].astype(F32) + sib_ref[0, rows, :].astype(F32)
            for j in range(3):
                g = g + rel_ref[j, rows, :].astype(F32)
            return g

        g = gsum(pl.ds(0, N_META))
        delta, m2, v2 = _adamw_math(g, wm[...], mm[...], vm[...])
        for o, val in zip(outs[:4], (g, delta, m2, v2)):
            o[...] = val
        g = gsum(pl.ds(N_META, CONV_K))
        delta, m2, v2 = _adamw_math(g, wd[0], md[0], vd[0])
        for o, val in zip(outs[4:], (g, delta, m2, v2)):
            o[0] = val

    s_meta = jax.ShapeDtypeStruct(meta[0].shape, F32)
    s_dw = jax.ShapeDtypeStruct(dw[0].shape, F32)
    res = pl.pallas_call(body, name="adamw_meta_dw", out_shape=[s_meta] * 4 + [s_dw] * 4)(own, sib, rel, *meta, *dw)
    return tuple(res[:4]), tuple(res[4:])


REP_ROWS = 16


def _adamw_rep(gathered, ws, ms, vs):
    rows = [(0, 1), (1, 2), (3, 1), (4, 1), (5, 1), (6, 1), (7, 1), (8, 1)]

    def body(g_ref, *refs):
        w_refs, m_refs, v_refs = refs[:8], refs[8:16], refs[16:24]
        loss_ref, outs, acc = refs[24], refs[25:57], refs[57]
        g = g_ref[0]
        for d in range(1, NDEV):
            g = g + g_ref[d]
        acc[...] = g
        loss_ref[...] = (0.5 / D) * jnp.sum(acc[pl.ds(9, 1), :], axis=1, keepdims=True)
        for p, (r0, nr) in enumerate(rows):
            for h in range(nr):
                cols = pl.ds(h * D, D)
                gp = acc[pl.ds(r0 + h, 1), :]
                delta, mm, vv = _adamw_math(gp, w_refs[p][:, cols], m_refs[p][:, cols], v_refs[p][:, cols])
                for o, val in zip(outs[4 * p:4 * p + 4], (gp, delta, mm, vv)):
                    o[:, cols] = val

    shapes = [jax.ShapeDtypeStruct(w.shape, F32) for w in ws]
    res = pl.pallas_call(
        body, name="adamw_rep",
        out_shape=[jax.ShapeDtypeStruct((1, 1), F32)] + [s for s in shapes for _ in range(4)],
        scratch_shapes=[pltpu.VMEM((REP_ROWS, D), F32)],
    )(gathered, *ws, *ms, *vs)
    return res[0], [tuple(res[1 + 4 * p:5 + 4 * p]) for p in range(8)]


def _load_ffn(i, j, wgu_hbm, wgu, wdn_hbm, wdn, sems):
    half = NDEV // 2

    def copies(ch):
        pairs = [(wgu_hbm.at[half * ch + d, g], wgu.at[g, ch, pl.ds(FFB * d, FFB), :]) for g in range(2) for d in range(half)]
        pairs.append((wdn_hbm.at[ch], wdn.at[ch]))
        return [pltpu.make_async_copy(s, t, sems.at[(2 * half + 1) * ch + k]) for k, (s, t) in enumerate(pairs)]

    @pl.when((i == 0) & (j == 0))
    def _():
        for cp in copies(0) + copies(1):
            cp.start()

    for ch in range(2):
        @pl.when((i == 0) & (j == ch))
        def _():
            for cp in copies(ch):
                cp.wait()


def _win_pairs(w_hbm, w_vm):
    return [(w_hbm.at[q], w_vm.at[q // 2, :, pl.ds(2 * INB * (q % 2), 2 * INB)]) for q in range(4)]


def _whole(a):
    nd = a.ndim
    return pl.BlockSpec(a.shape, lambda *g: (0,) * nd)


CHIPW = 2 * INB
PHASE_CHIP = (1, 0, 2)
assert PHASE_CHIP[2] == 2


class _GatherIn:
    scratch = [pltpu.VMEM((D, INB), BF16), pltpu.SemaphoreType.DMA((7,)), pltpu.SemaphoreType.DMA((7,)),
               pltpu.SemaphoreType.DMA((1,))]

    def bind(self, w_ref, w_vm, scratch):
        self.w_ref, self.w_vm = w_ref, w_vm
        self.stage, self.send_sems, self.recv_sems, self.local_sem = scratch
        return self

    def _win(self, chip, core):
        return self.w_vm.at[2 * chip[0] + chip[1], core]

    def _copy(self, k, chip, core, to, src=None):
        dst = self._win(chip, core)
        return pltpu.make_async_remote_copy(
            src_ref=dst if src is None else src, dst_ref=dst, send_sem=self.send_sems.at[k],
            recv_sem=self.recv_sems.at[k], device_id=to, device_id_type=MESH_ID)

    def _mine(self, cs):
        x, y, _ = _place()
        return pltpu.make_async_copy(self.stage, self._win((x, y), cs), self.local_sem.at[0])

    def issue(self, cs):
        x, y, _ = _place()
        chips = [(1 - x, y), (x, 1 - y), (1 - x, 1 - y)]
        self.stage[...] = self.w_ref[0].astype(BF16)
        self._mine(cs).start()
        self._copy(0, (x, y), cs, (x, y, 1 - cs), src=self.stage).start()
        for j in PHASE_CHIP[:2]:
            self._copy(1 + j, (x, y), cs, (*chips[j], cs), src=self.stage).start()

    def wait_chip(self, phase, cs):
        x, y, _ = _place()
        chips = [(1 - x, y), (x, 1 - y), (1 - x, 1 - y)]
        if phase == 0:
            self._mine(cs).wait()
            self._copy(0, (x, y), 1 - cs, (x, y, cs)).wait_recv()
            return
        j = PHASE_CHIP[phase - 1]
        self._copy(1 + j, chips[j], cs, (x, y, cs)).wait_recv()
        self._copy(4 + j, chips[j], cs, (x, y, 1 - cs)).start()
        if phase == 1:
            self._copy(3, (x, y), cs, (*chips[2], cs), src=self.stage).start()
        self._copy(4 + j, chips[j], 1 - cs, (x, y, cs)).wait_recv()

    def finish(self, cs):
        x, y, _ = _place()
        for k in range(7):
            self._copy(k, (x, y), cs, (x, y, cs), src=self.stage).wait_send()


def _fwd_in(x2, g_mix, w_in, order, tp, ag, ags):
    tm = _pick(tp, TM_IO)
    nt = tp // tm
    nx_last = x2.shape[0] - (nt - 1) * tm
    na, ng, ns = len(ag.arrays), ag.n, len(ags.arrays)
    gin = _GatherIn()

    def body(order_ref, *refs):
        x_ref, g_ref, w_ref = refs[:3]
        o = 3 + na + ns
        h_ref, z_ref, u_ref, wout_ref = refs[o:o + 4]
        s = o + 4 + ng + 1
        w_vm, u_all, osem, sm_vm = refs[s:s + 4]
        gin.bind(w_ref, w_vm, refs[s + 4:s + 8])
        ag.bind(refs[3:3 + na], refs[o + 4:o + 4 + ng], refs[s + 8:s + 8 + len(ag.scratch)])
        ags.bind(refs[3 + na:3 + na + ns], refs[o + 4 + ng:o + 5 + ng], refs[s + 8 + len(ag.scratch):])
        ph, i = pl.program_id(0), pl.program_id(1)
        core = lax.axis_index("c")
        first = (ph == 0) & (i == 0)
        last = (ph == 3) & (i == nt - 1)
        @pl.when(first)
        def _():
            ags.issue()

        for cs in range(2):
            @pl.when(first & (core == cs))
            def _():
                gin.issue(cs)

        @pl.when((ph == 0) & (i == max(nt - 2, 0)))
        def _():
            ags.forward()

        for cs in range(2):
            for p in range(4):
                @pl.when((ph == p) & (i == 0) & (core == cs))
                def _():
                    gin.wait_chip(p, cs)

        @pl.when((ph == 2) & (i == 0))
        def _():
            ag.issue()

        out_copies = [pltpu.make_async_copy(w_vm.at[k, c], wout_ref.at[k, :, pl.ds(INB * c, INB)], osem.at[2 * k + c])
                      for k in range(4) for c in range(2)]

        @pl.when((ph == 3) & (i == 0))
        def _():
            for cp in out_copies:
                cp.start()

        @pl.when((ph == 0) & (i < nt - 1))
        def _():
            h_ref[...] = x_ref[...]

        @pl.when((ph == 0) & (i == nt - 1))
        def _():
            ags.finish()
            cp = pltpu.make_async_copy(ags.outs[0], sm_vm, osem.at[8])
            cp.start()
            h_ref[pl.ds(0, nx_last), :] = x_ref[pl.ds(0, nx_last), :]
            h_ref[pl.ds(nx_last, tm - nx_last - N_META), :] = jnp.zeros((tm - nx_last - N_META, D), F32)
            cp.wait()
            for d in range(NDEV):
                h_ref[pl.ds(tm - N_META, N_META), pl.ds(128 * d, 128)] = sm_vm[d, pl.ds(0, N_META), :]

        @pl.when(ph == 0)
        def _():
            xv = h_ref[...]
            r = lax.rsqrt(jnp.mean(xv * xv, axis=-1, keepdims=True) + RMS_EPS)
            u = (xv * r * g_ref[...]).astype(BF16)
            u_ref[...] = u
            u_all[i] = u

        for c in range(2):
            z_ref[:, INB * c:INB * (c + 1)] = _dot(u_all[i], w_vm[order_ref[ph], c])

        @pl.when(last)
        def _():
            ag.forward()
            ag.finish()
            for cp in out_copies:
                cp.wait()

        for cs in range(2):
            @pl.when(last & (core == cs))
            def _():
                gin.finish(cs)

    def rows(ph, i, order):
        return (jnp.where(ph == 0, i, nt - 1), 0)

    tile = pl.BlockSpec((tm, D), rows)
    anys = pl.BlockSpec(memory_space=pl.ANY)
    res = pl.pallas_call(
        body, name="fwd_in",
        grid_spec=pltpu.PrefetchScalarGridSpec(
            num_scalar_prefetch=1, grid=(4, nt),
            in_specs=[tile, pl.BlockSpec((1, D), lambda ph, i, order: (0, 0)), _whole(w_in)]
            + [_whole(a) for a in ag.arrays + ags.arrays],
            out_specs=[tile, pl.BlockSpec((tm, CHIPW), lambda ph, i, order: (i, order[ph])), tile, anys] + [anys] * (ng + 1),
            scratch_shapes=[pltpu.VMEM((4, 2, D, INB), BF16), pltpu.VMEM((nt, tm, D), BF16), pltpu.SemaphoreType.DMA((9,)),
                            pltpu.VMEM(ags.out_shape[0].shape, F32)] + gin.scratch + ag.scratch + ags.scratch),
        out_shape=[jax.ShapeDtypeStruct((tp, D), F32), jax.ShapeDtypeStruct((tp, DIN), F32),
                   jax.ShapeDtypeStruct((tp, D), BF16), jax.ShapeDtypeStruct((4, D, CHIPW), BF16)]
        + ag.out_shape + ags.out_shape,
        compiler_params=_params(("arbitrary", "arbitrary"), 58),
    )(order, x2, g_mix, w_in, *ag.arrays, *ags.arrays)
    return res[:4], res[4:4 + ng], res[4 + ng]


def _halo_specs(col, nt, width=D):
    r = TM // HALO
    nb = nt * r
    return [pl.BlockSpec((HALO, width), lambda i: ((i * r + nb - 1) % nb, col)),
            pl.BlockSpec((TM, width), lambda i: (i, col)),
            pl.BlockSpec((HALO, width), lambda i: (((i + 1) * r) % nb, col))]


NCB = D // 128
TME = TM + 2 * HALO
CONV_STEPS = 16
assert TM % CONV_STEPS == 0


def _tm_fill(dst, time0, groups, tile_fn, unroll=1):
    def body(g, c):
        for j in range(NCB):
            dst[pl.ds((time0 + 8 * g) * NCB + j, 8, stride=NCB), :] = tile_fn(pl.multiple_of(8 * g, 8), pl.ds(128 * j, 128))
        return c

    lax.fori_loop(0, groups, body, 0, unroll=unroll)


def _tm_fill_ext(dst, left, cur, right, fn, unroll=1):
    _tm_fill(dst, 0, HALO // 8, lambda r, l: fn(left, pl.ds(r, 8), l), unroll)
    _tm_fill(dst, HALO, TM // 8, lambda r, l: fn(cur, pl.ds(r, 8), l), unroll)
    _tm_fill(dst, HALO + TM, HALO // 8, lambda r, l: fn(right, pl.ds(r, 8), l), unroll)


def _tm_read(src, groups, store_fn):
    def body(g, c):
        for j in range(NCB):
            store_fn(pl.ds(pl.multiple_of(8 * g, 8), 8), pl.ds(128 * j, 128), src[pl.ds(8 * g * NCB + j, 8, stride=NCB), :])
        return c

    lax.fori_loop(0, groups, body, 0, unroll=2)


def _tm_rows(t):
    return pl.ds(t * NCB if isinstance(t, int) else pl.multiple_of(t * NCB, NCB), NCB)


def _tm_at(ref, t):
    return ref[_tm_rows(t), :]


def _by_group(sub, vals):
    return jnp.where(sub < 2, vals[0], jnp.where(sub < 4, vals[1], jnp.where(sub < 6, vals[2], vals[3])))


def _pool_cnt(b, seq, tp, sub):
    b = jnp.where(b < 0, b + tp, b)
    b = jnp.where(b >= tp, b - tp, b)
    t = jnp.where(b < seq, b + N_META, b - (tp - N_META))
    cnts = []
    for win in POOL_WINDOWS:
        left = win // 2
        lo = jnp.maximum(t - left, 0)
        hi = jnp.minimum(t + win - left, seq + N_META)
        cnts.append(jnp.maximum(hi - lo, 1).astype(F32))
    return _by_group(sub, cnts)


def _edge_rows(seq, tp):
    reach = max(POOL_WINDOWS) // 2
    return [tp - N_META + t for t in range(reach)] + [seq - reach + 1 + t for t in range(reach - 1)]


def _edge_gain(b, seq, tp, sub):
    return _by_group(sub, [float(w) for w in POOL_WINDOWS]) / _pool_cnt(b, seq, tp, sub)


def _nested_windows(at, lo_offs):
    sums, s, have = [], None, set()
    for g, win in enumerate(POOL_WINDOWS):
        for o in range(lo_offs[g], lo_offs[g] + win):
            if o not in have:
                have.add(o)
                s = at(o) if s is None else s + at(o)
        sums.append(s)
    return sums


def _seq_fwd(z, w_dw, b_dw, seq, gat):
    tp = z.shape[0]
    nt = tp // TM
    na, ng = len(gat.arrays), gat.n

    def body(*refs):
        av_l, av, av_r, ag_l, ag, ag_r, p_l, p, p_r, w_ref, b_ref = refs[:11]
        ac_ref, m_ref = refs[11 + na:13 + na]
        a3, p3, o3, m3, w3, b3, m2d = refs[13 + na + ng:20 + na + ng]
        gat.bind(refs[11:11 + na], refs[13 + na:13 + na + ng], refs[20 + na + ng:])
        i = pl.program_id(0)
        sub = lax.broadcasted_iota(jnp.int32, (NCB, 128), 0)

        @pl.when(i == 0)
        def _():
            gat.issue()
            _tm_fill(w3, 0, 4, lambda r, l: w_ref[pl.ds(r, 8), l])
            for j in range(NCB):
                b3[pl.ds(j, 1), :] = b_ref[:, pl.ds(128 * j, 128)]

        @pl.when(i == max(nt - 2, 0))
        def _():
            gat.forward()

        _tm_fill_ext(a3, (av_l, ag_l), (av, ag), (av_r, ag_r), lambda vg, r, l: vg[0][r, l] * _sig(vg[1][r, l]), unroll=2)
        _tm_fill_ext(p3, p_l, p, p_r, lambda ref, r, l: ref[r, l])

        def conv(g, c):
            accs = [b3[...]] * 16
            for k in range(CONV_K):
                wk = _tm_at(w3, k)
                for t in range(16):
                    accs[t] = accs[t] + wk * _tm_at(a3, 16 * g + t + k + 1)
            for t in range(16):
                o3[_tm_rows(16 * g + t), :] = accs[t]
            return c

        lax.fori_loop(0, TM // 16, conv, 0)
        _tm_read(o3, TM // 8, lambda r, l, tile: ac_ref.__setitem__((r, l), tile))

        inv = _by_group(sub, [1.0 / w for w in POOL_WINDOWS])

        def pool(g, c):
            for t in range(8):
                e = 8 * g + t + HALO
                sums = _nested_windows(lambda o: _tm_at(p3, e + o), [-(w // 2) for w in POOL_WINDOWS])
                m3[_tm_rows(8 * g + t), :] = _by_group(sub, sums) * inv - _tm_at(p3, e)
            return c

        lax.fori_loop(0, TM // 8, pool, 0)
        for b in _edge_rows(seq, tp):
            r = b - i * TM

            @pl.when((r >= 0) & (r < TM))
            def _():
                pv = _tm_at(p3, r + HALO)
                m3[_tm_rows(r), :] = (_tm_at(m3, r) + pv) * _edge_gain(b, seq, tp, sub) - pv

        _tm_read(m3, TM // 8, lambda r, l, tile: m2d.__setitem__((r, l), tile))
        m_ref[...] = m2d[...].astype(BF16)

        @pl.when(i == nt - 1)
        def _():
            gat.finish()

    tmaj = pltpu.VMEM((TM * NCB, 128), F32)
    text = pltpu.VMEM((TME * NCB, 128), F32)
    res = pl.pallas_call(
        body, name="seq_fwd", grid=(nt,),
        in_specs=_halo_specs(0, nt) + _halo_specs(1, nt) + _halo_specs(2, nt)
        + [pl.BlockSpec((32, D), lambda i: (0, 0)), pl.BlockSpec((1, D), lambda i: (0, 0))] + [_whole(a) for a in gat.arrays],
        out_specs=[pl.BlockSpec((TM, D), lambda i: (i, 0))] * 2 + [pl.BlockSpec(memory_space=pl.ANY)] * ng,
        out_shape=[jax.ShapeDtypeStruct((tp, D), F32), jax.ShapeDtypeStruct((tp, D), BF16)] + gat.out_shape,
        scratch_shapes=[text, text, tmaj, tmaj, pltpu.VMEM((32 * NCB, 128), F32), pltpu.VMEM((NCB, 128), F32),
                        pltpu.VMEM((TM, D), F32)] + gat.scratch,
        compiler_params=_params(("arbitrary",), 52),
    )(z, z, z, z, z, z, z, z, z, w_dw, b_dw, *gat.arrays)
    return res[:2], res[2:]


def _ln_stats(ac):
    mu = jnp.mean(ac, axis=-1, keepdims=True)
    xc = ac - mu
    rl = lax.rsqrt(jnp.mean(xc * xc, axis=-1, keepdims=True) + LN_EPS)
    return xc * rl, rl


def _pool_mix(m, wp_ref):
    return jnp.concatenate(
        [_dot(m[:, g * PG:(g + 1) * PG], wp_ref[:, g].reshape(PG, PG)) for g in range(4)], axis=1)


def _mix_fwd(ac, m, z, h0, b_gate, ln_g, ln_b, pool_scale, g_mixw, g_pool, gat):
    tp = h0.shape[0]
    tms = TM
    nt = tp // tms
    na, ng = len(gat.arrays), gat.n

    def body(*refs):
        ac_ref, m_ref, zga, zgb, h_ref, bg_ref, lg_ref, lb_ref, ps_ref, wm_hbm, wp_hbm = refs[:11]
        h1_ref, s_ref, mg_ref, q_ref = refs[11 + na:15 + na]
        wm, wp, sems = refs[15 + na + ng:18 + na + ng]
        gat.bind(refs[11:11 + na], refs[15 + na:15 + na + ng], refs[18 + na + ng:])
        i = pl.program_id(0)

        @pl.when(i == 0)
        def _():
            gat.issue()

        @pl.when(i == max(nt - 4, 0))
        def _():
            gat.forward()

        @pl.when(i == nt - 1)
        def _():
            gat.finish()

        _load_once(i == 0, [(wm_hbm, wm), (wp_hbm, wp)], sems)
        n, _ = _ln_stats(ac_ref[...])
        l = n * lg_ref[...] + lb_ref[...]
        s = (l * _sig(l)).astype(BF16)
        s_ref[...] = s
        yc = _dot(s, wm[:, 0].reshape(D, D))
        q = (_pool_mix(m_ref[...], wp) * ps_ref[...]).astype(BF16)
        q_ref[...] = q
        yp = _dot(q, wm[:, 1].reshape(D, D))
        ga = _sig(zga[...] + bg_ref[:, :D])
        gb = _sig(zgb[...] + bg_ref[:, D:])
        merged = (ga * yc + gb * yp).astype(BF16)
        mg_ref[...] = merged
        h1_ref[...] = h_ref[...] + _dot(merged, wm[:, 2].reshape(D, D))

    def tile(col=0):
        return pl.BlockSpec((tms, D), lambda i: (i, col))

    def vec(w):
        return pl.BlockSpec((1, w), lambda i: (0, 0))

    anys = pl.BlockSpec(memory_space=pl.ANY)
    f32o, b16o = jax.ShapeDtypeStruct((tp, D), F32), jax.ShapeDtypeStruct((tp, D), BF16)
    res = pl.pallas_call(
        body, name="mix_fwd", grid=(nt,),
        in_specs=[tile(), tile(), tile(3), tile(4), tile(), vec(2 * D), vec(D), vec(D), vec(D), anys, anys]
        + [_whole(a) for a in gat.arrays],
        out_specs=[tile()] * 4 + [anys] * ng,
        out_shape=[f32o, b16o, b16o, b16o] + gat.out_shape,
        scratch_shapes=[pltpu.VMEM((NDEV, 3, D // NDEV, D), BF16), pltpu.VMEM((NDEV, 4, PG // NDEV, PG), BF16),
                        pltpu.SemaphoreType.DMA((2,))] + gat.scratch,
        compiler_params=_params(("arbitrary",), 52),
    )(ac, m, z, z, h0, b_gate, ln_g, ln_b, pool_scale, g_mixw, g_pool, *gat.arrays)
    return res[:4], res[4:]


def _ffn_fwd(h1, tgt, g_ffn, g_final, w_gu, w_dn):
    tp = h1.shape[0]
    nt = tp // TM
    nx_last = tgt.shape[0] - (nt - 1) * TM

    def body(h_ref, t_ref, gf_ref, gl_ref, wgu_hbm, wdn_hbm,
             fg_ref, fu_ref, v_ref, f_ref, dh2_ref, acc_ref, wgu, wdn, v_sc, h2_sc, diff_sc, sems):
        i, j = pl.program_id(0), pl.program_id(1)
        _load_ffn(i, j, wgu_hbm, wgu, wdn_hbm, wdn, sems)

        @pl.when((i == 0) & (j == 0))
        def _():
            acc_ref[...] = jnp.zeros_like(acc_ref)

        @pl.when(j == 0)
        def _():
            h = h_ref[...]
            r = lax.rsqrt(jnp.mean(h * h, axis=-1, keepdims=True) + RMS_EPS)
            v = (h * r * gf_ref[...]).astype(BF16)
            v_sc[...] = v
            v_ref[...] = v
            h2_sc[...] = h

        v = v_sc[...]
        fg = _dot_nt(v, wgu[0, j])
        fu = _dot_nt(v, wgu[1, j])
        fg_ref[...] = fg
        fu_ref[...] = fu
        f = ((fg * _sig(fg)) * fu).astype(BF16)
        f_ref[...] = f
        h2_sc[...] += _dot(f, wdn[j])

        @pl.when(j == 1)
        def _():
            h2 = h2_sc[...]
            r = lax.rsqrt(jnp.mean(h2 * h2, axis=-1, keepdims=True) + RMS_EPS)
            n2 = h2 * r
            y = n2 * gl_ref[...]

            @pl.when(i < nt - 1)
            def _():
                diff_sc[...] = y - t_ref[...]

            @pl.when(i == nt - 1)
            def _():
                diff_sc[pl.ds(0, nx_last), :] = y[:nx_last] - t_ref[pl.ds(0, nx_last), :]
                diff_sc[pl.ds(nx_last, TM - nx_last), :] = jnp.zeros((TM - nx_last, D), F32)

            diff = diff_sc[...]
            dy = diff * (1.0 / D)
            acc_ref[0:1, :] += jnp.sum(diff * diff, axis=0, keepdims=True)
            acc_ref[1:2, :] += jnp.sum(dy * n2, axis=0, keepdims=True)
            dn = dy * gl_ref[...]
            dh2_ref[...] = r * (dn - n2 * jnp.mean(dn * n2, axis=-1, keepdims=True))

    def tile():
        return pl.BlockSpec((TM, D), lambda i, j: (i, 0))

    def chunk():
        return pl.BlockSpec((TM, FFC), lambda i, j: (i, j))

    def vec():
        return pl.BlockSpec((1, D), lambda i, j: (0, 0))

    anys = pl.BlockSpec(memory_space=pl.ANY)
    hid32, hid16 = jax.ShapeDtypeStruct((tp, DFF), F32), jax.ShapeDtypeStruct((tp, DFF), BF16)
    return pl.pallas_call(
        body, name="ffn_fwd", grid=(nt, 2),
        in_specs=[tile(), tile(), vec(), vec(), anys, anys],
        out_specs=[chunk(), chunk(), tile(), chunk(), tile(), pl.BlockSpec((8, D), lambda i, j: (0, 0))],
        out_shape=[hid32, hid32, jax.ShapeDtypeStruct((tp, D), BF16), hid16, jax.ShapeDtypeStruct((tp, D), F32),
                   jax.ShapeDtypeStruct((8, D), F32)],
        scratch_shapes=[pltpu.VMEM((2, 2, FFC, D), BF16), pltpu.VMEM((2, FFC, D), BF16),
                        pltpu.VMEM((TM, D), BF16), pltpu.VMEM((TM, D), F32), pltpu.VMEM((TM, D), F32),
                        pltpu.SemaphoreType.DMA((2 * NDEV + 2,))],
        compiler_params=_params(("arbitrary", "arbitrary"), 56),
    )(h1, tgt, g_ffn, g_final, w_gu, w_dn)


def _ffn_bwd(dh2, fg, fu, h1, g_ffn, w_gu, w_dn):
    tp = h1.shape[0]
    nt = tp // TM

    def body(dh2_ref, fg_ref, fu_ref, h_ref, gf_ref, wgu_hbm, wdn_hbm,
             dfg_ref, dfu_ref, dh1_ref, acc_ref, wgu, wdn, d_sc, dv_sc, sems):
        i, j = pl.program_id(0), pl.program_id(1)
        _load_ffn(i, j, wgu_hbm, wgu, wdn_hbm, wdn, sems)

        @pl.when((i == 0) & (j == 0))
        def _():
            acc_ref[...] = jnp.zeros_like(acc_ref)

        @pl.when(j == 0)
        def _():
            d_sc[...] = dh2_ref[...].astype(BF16)
            dv_sc[...] = jnp.zeros_like(dv_sc)

        df = _dot_nt(d_sc[...], wdn[j])
        fg = fg_ref[...]
        sg = _sig(fg)
        dfu = (df * (fg * sg)).astype(BF16)
        dfg = (df * fu_ref[...] * (sg * (1.0 + fg * (1.0 - sg)))).astype(BF16)
        dfg_ref[...] = dfg
        dfu_ref[...] = dfu
        dv_sc[...] += _dot(dfg, wgu[0, j]) + _dot(dfu, wgu[1, j])

        @pl.when(j == 1)
        def _():
            h = h_ref[...]
            r = lax.rsqrt(jnp.mean(h * h, axis=-1, keepdims=True) + RMS_EPS)
            n1 = h * r
            dv = dv_sc[...]
            acc_ref[0:1, :] += jnp.sum(dv * n1, axis=0, keepdims=True)
            dn = dv * gf_ref[...]
            dh1_ref[...] = dh2_ref[...] + r * (dn - n1 * jnp.mean(dn * n1, axis=-1, keepdims=True))

    def tile():
        return pl.BlockSpec((TM, D), lambda i, j: (i, 0))

    def chunk():
        return pl.BlockSpec((TM, FFC), lambda i, j: (i, j))

    anys = pl.BlockSpec(memory_space=pl.ANY)
    hid16 = jax.ShapeDtypeStruct((tp, DFF), BF16)
    return pl.pallas_call(
        body, name="ffn_bwd", grid=(nt, 2),
        in_specs=[tile(), chunk(), chunk(), tile(), pl.BlockSpec((1, D), lambda i, j: (0, 0)), anys, anys],
        out_specs=[chunk(), chunk(), tile(), pl.BlockSpec((8, D), lambda i, j: (0, 0))],
        out_shape=[hid16, hid16, jax.ShapeDtypeStruct((tp, D), F32), jax.ShapeDtypeStruct((8, D), F32)],
        scratch_shapes=[pltpu.VMEM((2, 2, FFC, D), BF16), pltpu.VMEM((2, FFC, D), BF16),
                        pltpu.VMEM((TM, D), BF16), pltpu.VMEM((TM, D), F32), pltpu.SemaphoreType.DMA((2 * NDEV + 2,))],
        compiler_params=_params(("arbitrary", "arbitrary"), 56),
    )(dh2, fg, fu, h1, g_ffn, w_gu, w_dn)


def _mix_bwd(dh1, z, s, q, ac, m, b_gate, ln_g, ln_b, pool_scale, g_mixw, g_pool, qs):
    tp = dh1.shape[0]
    nt = tp // TMS
    ex = _ChipExchange(qs)
    nq = ex.n

    def body(*refs):
        dh1_ref, zga, zgb, s_ref, q_ref, ac_ref, m_ref, bg_ref, lg_ref, lb_ref, ps_ref, wm_hbm, wp_hbm = refs[:13]
        dac_ref, dm_ref, dzg_ref, dyc_ref, dyp_ref, dm2_ref, acc_ref = refs[13 + nq:20 + nq]
        wm, wp, sems = refs[20 + 2 * nq:23 + 2 * nq]
        ex.bind(refs[13:13 + nq], refs[20 + nq:20 + 2 * nq], refs[23 + 2 * nq:])
        first = pl.program_id(0) == 0

        @pl.when(first)
        def _():
            ex.issue()
            acc_ref[...] = jnp.zeros_like(acc_ref)

        _load_once(first, [(wm_hbm, wm), (wp_hbm, wp)], sems)

        dmerged = _dot_nt(dh1_ref[...].astype(BF16), wm[:, 2].reshape(D, D))
        ga = _sig(zga[...] + bg_ref[:, :D])
        gb = _sig(zgb[...] + bg_ref[:, D:])
        dyc = dmerged * ga
        dyp = dmerged * gb
        dza = (dmerged * _dot(s_ref[...], wm[:, 0].reshape(D, D))) * (ga * (1.0 - ga))
        dzb = (dmerged * _dot(q_ref[...], wm[:, 1].reshape(D, D))) * (gb * (1.0 - gb))
        dzg_ref[:, :D] = dza.astype(BF16)
        dzg_ref[:, D:] = dzb.astype(BF16)
        acc_ref[0:1, :D] += jnp.sum(dza, axis=0, keepdims=True)
        acc_ref[0:1, D:] += jnp.sum(dzb, axis=0, keepdims=True)
        dyc_b = dyc.astype(BF16)
        dyp_b = dyp.astype(BF16)
        dyc_ref[...] = dyc_b
        dyp_ref[...] = dyp_b
        ds = _dot_nt(dyc_b, wm[:, 0].reshape(D, D))
        n, rl = _ln_stats(ac_ref[...])
        l = n * lg_ref[...] + lb_ref[...]
        sg = _sig(l)
        dl = ds * (sg * (1.0 + l * (1.0 - sg)))
        acc_ref[1:2, :D] += jnp.sum(dl * n, axis=0, keepdims=True)
        acc_ref[1:2, D:] += jnp.sum(dl, axis=0, keepdims=True)
        dn = dl * lg_ref[...]
        dac_ref[...] = rl * (dn - jnp.mean(dn, axis=-1, keepdims=True) - n * jnp.mean(dn * n, axis=-1, keepdims=True))
        dq = _dot_nt(dyp_b, wm[:, 1].reshape(D, D))
        mv = m_ref[...]
        acc_ref[2:3, :D] += jnp.sum(dq * _pool_mix(mv, wp), axis=0, keepdims=True)
        dm2 = (dq * ps_ref[...]).astype(BF16)
        dm2_ref[...] = dm2
        dm_ref[...] = jnp.concatenate(
            [_dot_nt(dm2[:, g * PG:(g + 1) * PG], wp[:, g].reshape(PG, PG)) for g in range(4)], axis=1)

        @pl.when(pl.program_id(0) == nt - 1)
        def _():
            ex.finish()

    def tile(col=0):
        return pl.BlockSpec((TMS, D), lambda i: (i, col))

    def vec(w):
        return pl.BlockSpec((1, w), lambda i: (0, 0))

    anys = pl.BlockSpec(memory_space=pl.ANY)
    f32o, b16o = jax.ShapeDtypeStruct((tp, D), F32), jax.ShapeDtypeStruct((tp, D), BF16)
    res = pl.pallas_call(
        body, name="mix_bwd", grid=(nt,),
        in_specs=[tile(), tile(3), tile(4), tile(), tile(), tile(), tile(), vec(2 * D), vec(D), vec(D), vec(D), anys, anys]
        + [anys] * nq,
        out_specs=[tile(), tile(), pl.BlockSpec((TMS, 2 * D), lambda i: (i, 0)), tile(), tile(), tile(),
                   pl.BlockSpec((8, 2 * D), lambda i: (0, 0))] + [anys] * nq,
        out_shape=[f32o, f32o, jax.ShapeDtypeStruct((tp, 2 * D), BF16), b16o, b16o, b16o,
                   jax.ShapeDtypeStruct((8, 2 * D), F32)] + ex.out_shape,
        scratch_shapes=[pltpu.VMEM((NDEV, 3, D // NDEV, D), BF16), pltpu.VMEM((NDEV, 4, PG // NDEV, PG), BF16),
                        pltpu.SemaphoreType.DMA((2,))] + ex.scratch,
        compiler_params=_params(("arbitrary",), 48),
    )(dh1, z, z, s, q, ac, m, b_gate, ln_g, ln_b, pool_scale, g_mixw, g_pool, *qs)
    return res[:7], res[7:]


def _seq_bwd(dac, dm, dzg, z, w_dw, seq, qs):
    tp = z.shape[0]
    nt = tp // TM
    ex = _ChipExchange(qs)
    nq = no = ex.n

    def body(*refs):
        dac_l, dac_c, dac_r, dm_l, dm_c, dm_r, av_l, av, av_r, ag_l, ag, ag_r, dzg_ref, w_ref = refs[:14]
        dz_ref, acc_ref = refs[14 + nq:16 + nq]
        a3, d3, m3, da3, dp3, w3, dw3, da_sc, dp_sc = refs[16 + nq + no:25 + nq + no]
        ex.bind(refs[14:14 + nq], refs[16 + nq:16 + nq + no], refs[25 + nq + no:])
        i = pl.program_id(0)
        sub = lax.broadcasted_iota(jnp.int32, (NCB, 128), 0)

        @pl.when(i == 0)
        def _():
            ex.issue()
            dw3[...] = jnp.zeros_like(dw3)
            _tm_fill(w3, 0, 4, lambda r, l: w_ref[pl.ds(r, 8), l])

        _tm_fill_ext(a3, (av_l, ag_l), (av, ag), (av_r, ag_r), lambda vg, r, l: vg[0][r, l] * _sig(vg[1][r, l]), unroll=2)
        _tm_fill_ext(d3, dac_l, dac_c, dac_r, lambda ref, r, l: ref[r, l])
        _tm_fill_ext(m3, dm_l, dm_c, dm_r, lambda ref, r, l: ref[r, l])

        def conv(g, c):
            t0 = CONV_STEPS * g
            dcur = [_tm_at(d3, t0 + t + HALO) for t in range(CONV_STEPS)]
            accs = [None] * CONV_STEPS
            for k in range(CONV_K):
                wk = _tm_at(w3, k)
                prs = []
                for t in range(CONV_STEPS):
                    term = wk * _tm_at(d3, t0 + t + CONV_K - k)
                    accs[t] = term if accs[t] is None else accs[t] + term
                    prs.append(dcur[t] * _tm_at(a3, t0 + t + k + 1))
                while len(prs) > 1:
                    prs = [prs[j] + prs[j + 1] for j in range(0, len(prs) - 1, 2)] + prs[len(prs) - len(prs) % 2:]
                dw3[_tm_rows(k), :] += prs[0]
            s = dcur[0]
            for t in range(1, CONV_STEPS):
                s = s + dcur[t]
            dw3[_tm_rows(CONV_K), :] += s
            for t in range(CONV_STEPS):
                da3[_tm_rows(t0 + t), :] = accs[t]
            return c

        lax.fori_loop(0, TM // CONV_STEPS, conv, 0)

        for b in _edge_rows(seq, tp):
            e = lax.rem(b - i * TM + HALO + tp, tp)

            @pl.when(e < TME)
            def _():
                m3[_tm_rows(e), :] = _tm_at(m3, e) * _edge_gain(b, seq, tp, sub)

        inv = _by_group(sub, [1.0 / w for w in POOL_WINDOWS])

        def pool(g, c):
            for t in range(8):
                e = 8 * g + t + HALO
                sums = _nested_windows(lambda o: _tm_at(m3, e + o), [w // 2 + 1 - w for w in POOL_WINDOWS])
                dp3[_tm_rows(8 * g + t), :] = _by_group(sub, sums) * inv
            return c

        lax.fori_loop(0, TM // 8, pool, 0, unroll=2)

        _tm_read(da3, TM // 8, lambda r, l, tile: da_sc.__setitem__((r, l), tile))
        _tm_read(dp3, TM // 8, lambda r, l, tile: dp_sc.__setitem__((r, l), tile))
        sg = _sig(ag[...])
        da = da_sc[...]
        dz_ref[:, 0:D] = (da * sg).astype(BF16)
        dz_ref[:, D:2 * D] = (da * av[...] * (sg * (1.0 - sg))).astype(BF16)
        dz_ref[:, 2 * D:3 * D] = (dp_sc[...] - dm_c[...]).astype(BF16)
        dz_ref[:, 3 * D:] = dzg_ref[...]

        @pl.when(i == nt - 1)
        def _():
            _tm_read(dw3, 4, lambda r, l, tile: acc_ref.__setitem__((r, l), tile))
            ex.finish()

    tmaj = pltpu.VMEM((TM * NCB, 128), F32)
    text = pltpu.VMEM((TME * NCB, 128), F32)
    taps = pltpu.VMEM((32 * NCB, 128), F32)
    anys = pl.BlockSpec(memory_space=pl.ANY)
    res = pl.pallas_call(
        body, name="seq_bwd", grid=(nt,),
        in_specs=_halo_specs(0, nt) + _halo_specs(0, nt) + _halo_specs(0, nt) + _halo_specs(1, nt)
        + [pl.BlockSpec((TM, 2 * D), lambda i: (i, 0)), pl.BlockSpec((32, D), lambda i: (0, 0))] + [anys] * nq,
        out_specs=[pl.BlockSpec((TM, DIN), lambda i: (i, 0)), pl.BlockSpec((32, D), lambda i: (0, 0))] + [anys] * no,
        out_shape=[jax.ShapeDtypeStruct((tp, DIN), BF16), jax.ShapeDtypeStruct((32, D), F32)] + ex.out_shape,
        scratch_shapes=[text, text, text, tmaj, tmaj, taps, taps, pltpu.VMEM((TM, D), F32), pltpu.VMEM((TM, D), F32)]
        + ex.scratch,
        compiler_params=_params(("arbitrary",), 48),
    )(dac, dac, dac, dm, dm, dm, z, z, z, z, z, z, dzg, w_dw, *qs)
    return res[:2], res[2:]


def _in_bwd(dz, h0, dh1, g_mix, w_g, seq, qs):
    tp = h0.shape[0]
    tm = _pick(tp, TM_IO)
    nt = tp // tm
    ex = _ChipExchange(qs)
    nq = no = ex.n

    def body(*refs):
        dz_ref, h_ref, dh1_ref, g_ref, w_hbm = refs[:5]
        gx_ref, gmeta_ref, acc_ref = refs[5 + nq:8 + nq]
        w_vm, sems = refs[8 + nq + no:10 + nq + no]
        ex.bind(refs[5:5 + nq], refs[8 + nq:8 + nq + no], refs[10 + nq + no:])
        i = pl.program_id(0)

        @pl.when(i == 0)
        def _():
            ex.issue()
            acc_ref[...] = jnp.zeros_like(acc_ref)

        _load_once(i == 0, _win_pairs(w_hbm, w_vm), sems)

        du = _dot_nt(dz_ref[:, :DIN // 2], w_vm[0]) + _dot_nt(dz_ref[:, DIN // 2:], w_vm[1])
        h = h_ref[...]
        r = lax.rsqrt(jnp.mean(h * h, axis=-1, keepdims=True) + RMS_EPS)
        n0 = h * r
        acc_ref[0:1, :] += jnp.sum(du * n0, axis=0, keepdims=True)
        dn = du * g_ref[...]
        gx_ref[...] = dh1_ref[...] + r * (dn - n0 * jnp.mean(dn * n0, axis=-1, keepdims=True))

        @pl.when(i == nt - 1)
        def _():
            gmeta_ref[...] = gx_ref[pl.ds(tm - N_META, N_META), :]
            ex.finish()

    tile = pl.BlockSpec((tm, D), lambda i: (i, 0))
    anys = pl.BlockSpec(memory_space=pl.ANY)
    res = pl.pallas_call(
        body, name="in_bwd", grid=(nt,),
        in_specs=[pl.BlockSpec((tm, DIN), lambda i: (i, 0)), tile, tile, pl.BlockSpec((1, D), lambda i: (0, 0)), anys]
        + [anys] * nq,
        out_specs=[tile, pl.BlockSpec((N_META, D), lambda i: (0, 0)), pl.BlockSpec((8, D), lambda i: (0, 0))] + [anys] * no,
        out_shape=[jax.ShapeDtypeStruct((seq, D), F32), jax.ShapeDtypeStruct((N_META, D), F32),
                   jax.ShapeDtypeStruct((8, D), F32)] + ex.out_shape,
        scratch_shapes=[pltpu.VMEM((2, D, DIN // 2), BF16), pltpu.SemaphoreType.DMA((NDEV,))] + ex.scratch,
        compiler_params=_params(("arbitrary",), 58),
    )(dz, h0, dh1, g_mix, w_g, *qs)
    return res[:3], res[3:]


def _wgrad_in(u, dz, qs):
    tp = u.shape[0]
    tm = _pick(tp, TM_WG)
    nt = tp // tm
    half = DIN // 2
    ex = _ChipExchange(qs)
    nq = ex.n

    def body(*refs):
        u_ref, dz_ref = refs[:2]
        o_ref, acc = refs[2 + nq], refs[3 + 2 * nq]
        ex.bind(refs[2:2 + nq], refs[3 + nq:3 + 2 * nq], refs[4 + 2 * nq:])
        h, t = pl.program_id(0), pl.program_id(1)

        @pl.when((h == 0) & (t == 0))
        def _():
            ex.issue()

        @pl.when(t == 0)
        def _():
            acc[...] = jnp.zeros_like(acc)

        acc[...] += _dot_tn(u_ref[...], dz_ref[...])

        @pl.when(t == nt - 1)
        def _():
            for d in range(4):
                o_ref[d] = acc[:, INB * d:INB * (d + 1)].astype(BF16)

        @pl.when((h == 1) & (t == nt - 1))
        def _():
            ex.finish()

    anys = pl.BlockSpec(memory_space=pl.ANY)
    res = pl.pallas_call(
        body, name="wgrad_in", grid=(2, nt),
        in_specs=[pl.BlockSpec((tm, D), lambda h, t: (t, 0)), pl.BlockSpec((tm, half), lambda h, t: (t, h))] + [anys] * nq,
        out_specs=[pl.BlockSpec((4, D, INB), lambda h, t: (h, 0, 0), pipeline_mode=pl.Buffered(1))] + [anys] * nq,
        out_shape=[jax.ShapeDtypeStruct((NDEV, D, INB), BF16)] + ex.out_shape,
        scratch_shapes=[pltpu.VMEM((D, half), F32)] + ex.scratch,
        compiler_params=_params(("arbitrary", "arbitrary"), 52),
    )(u, dz, *qs)
    return res[0], res[1:]


def _wgrad_mix(s, dyc, q, dyp, merged, dh1, m, dm2):
    tp = s.shape[0]
    tm = _pick(tp, TM_WM)
    nt = tp // tm
    rb = D // NDEV

    def body(s_ref, dyc_ref, q_ref, dyp_ref, mg_ref, dh1_ref, m_ref, dm2_ref, o_ref, op_ref, acc, accp):
        t = pl.program_id(0)

        @pl.when(t == 0)
        def _():
            acc[...] = jnp.zeros_like(acc)
            accp[...] = jnp.zeros_like(accp)

        acc[0] += _dot_tn(s_ref[...], dyc_ref[...])
        acc[1] += _dot_tn(q_ref[...], dyp_ref[...])
        acc[2] += _dot_tn(mg_ref[...], dh1_ref[...].astype(BF16))
        for g in range(4):
            accp[g] += _dot_tn(m_ref[:, g * PG:(g + 1) * PG], dm2_ref[:, g * PG:(g + 1) * PG])

        @pl.when(t == nt - 1)
        def _():
            for d in range(NDEV):
                for k in range(3):
                    o_ref[d, k] = acc[k, rb * d:rb * (d + 1), :].astype(BF16)
                for g in range(4):
                    op_ref[d, g] = accp[g, 32 * d:32 * (d + 1), :].astype(BF16)

    tile = pl.BlockSpec((tm, D), lambda t: (t, 0))
    return pl.pallas_call(
        body, name="wgrad_mix", grid=(nt,),
        in_specs=[tile] * 8,
        out_specs=[pl.BlockSpec((NDEV, 3, rb, D), lambda t: (0, 0, 0, 0), pipeline_mode=pl.Buffered(1)),
                   pl.BlockSpec((NDEV, 4, 32, PG), lambda t: (0, 0, 0, 0), pipeline_mode=pl.Buffered(1))],
        out_shape=[jax.ShapeDtypeStruct((NDEV, 3, rb, D), BF16), jax.ShapeDtypeStruct((NDEV, 4, 32, PG), BF16)],
        scratch_shapes=[pltpu.VMEM((3, D, D), F32), pltpu.VMEM((4, PG, PG), F32)],
        compiler_params=_params(("arbitrary",), 56),
    )(s, dyc, q, dyp, merged, dh1, m, dm2)


def _wgrad_gu(v, dfg, dfu):
    tp = v.shape[0]
    tm = _pick(tp, TM_WG)
    nt = tp // tm

    def body(v_ref, dg_ref, du_ref, o_ref, acc):
        k, t = pl.program_id(0), pl.program_id(2)

        @pl.when(t == 0)
        def _():
            acc[...] = jnp.zeros_like(acc)

        @pl.when(k == 0)
        def _():
            acc[...] += _dot_tn(dg_ref[...], v_ref[...])

        @pl.when(k == 1)
        def _():
            acc[...] += _dot_tn(du_ref[...], v_ref[...])

        @pl.when(t == nt - 1)
        def _():
            for d in range(4):
                o_ref[d] = acc[FFB * d:FFB * (d + 1), :].astype(BF16)

    return pl.pallas_call(
        body, name="wgrad_gu", grid=(2, 2, nt),
        in_specs=[pl.BlockSpec((tm, D), lambda k, h, t: (t, 0)),
                  pl.BlockSpec((tm, FFC), lambda k, h, t: (t * (1 - k), h * (1 - k))),
                  pl.BlockSpec((tm, FFC), lambda k, h, t: (t * k, h * k))],
        out_specs=pl.BlockSpec((4, None, FFB, D), lambda k, h, t: (h, k, 0, 0), pipeline_mode=pl.Buffered(1)),
        out_shape=jax.ShapeDtypeStruct((NDEV, 2, FFB, D), BF16),
        scratch_shapes=[pltpu.VMEM((FFC, D), F32)],
        compiler_params=_params(("arbitrary",) * 3, 48),
    )(v, dfg, dfu)


def _wgrad_down(f, dh2):
    tp = f.shape[0]
    tm = _pick(tp, TM_WG)
    nt = tp // tm

    def body(f_ref, d_ref, o_ref, acc):
        t = pl.program_id(1)

        @pl.when(t == 0)
        def _():
            acc[...] = jnp.zeros_like(acc)

        acc[...] += _dot_tn(f_ref[...], d_ref[...].astype(BF16))

        @pl.when(t == nt - 1)
        def _():
            for d in range(4):
                o_ref[d] = acc[FFB * d:FFB * (d + 1), :].astype(BF16)

    return pl.pallas_call(
        body, name="wgrad_down", grid=(2, nt),
        in_specs=[pl.BlockSpec((tm, FFC), lambda h, t: (t, h)), pl.BlockSpec((tm, D), lambda h, t: (t, 0))],
        out_specs=pl.BlockSpec((4, FFB, D), lambda h, t: (h, 0, 0), pipeline_mode=pl.Buffered(1)),
        out_shape=jax.ShapeDtypeStruct((NDEV, FFB, D), BF16),
        scratch_shapes=[pltpu.VMEM((FFC, D), F32)],
        compiler_params=_params(("arbitrary", "arbitrary"), 48),
    )(f, dh2)


def kernel(x, meta_tokens, g_mix, w_in, b_gate, w_dw, b_dw, ln_g, ln_b, w_conv_out, w_pool, pool_scale, w_pool_out, w_o, g_ffn, w_ffn_gate, w_ffn_up, w_ffn_down, g_final, loss_target, m_meta_tokens, m_g_mix, m_w_in, m_b_gate, m_w_dw, m_b_dw, m_ln_g, m_ln_b, m_w_conv_out, m_w_pool, m_pool_scale, m_w_pool_out, m_w_o, m_g_ffn, m_w_ffn_gate, m_w_ffn_up, m_w_ffn_down, m_g_final, v_meta_tokens, v_g_mix, v_w_in, v_b_gate, v_w_dw, v_b_dw, v_ln_g, v_ln_b, v_w_conv_out, v_w_pool, v_pool_scale, v_w_pool_out, v_w_o, v_g_ffn, v_w_ffn_gate, v_w_ffn_up, v_w_ffn_down, v_g_final):
    seq = x.shape[1]
    tp = -(-(seq + 2 * HALO) // TM) * TM
    tm_in = _pick(tp, TM_IO)
    nx_last = seq - (tp // tm_in - 1) * tm_in
    assert 0 < nx_last <= tm_in - 2 * HALO and nx_last % 8 == 0 and 0 < seq - (tp // TM - 1) * TM

    whole = (Ellipsis,)
    ag_small = _Gather(
        [((48, D // NDEV), [(meta_tokens, pl.ds(0, N_META), whole), (w_dw, pl.ds(N_META, CONV_K), 0)])], [F32])
    ag_mix = _Gather([((3, D // NDEV, D), [(w_conv_out, 0, 0), (w_pool_out, 1, 0), (w_o, 2, 0)]),
                      ((4, PG // NDEV, PG), [(w_pool, whole, 0)])], [BF16, BF16])
    def tr(a):
        return jnp.swapaxes(a, 1, 2)

    ag_gu = _Gather([((2, FFB, D), [(tr(w_ffn_gate), 0, 0), (tr(w_ffn_up), 1, 0)])], [BF16])
    ag_dn = _Gather([((FFB, D), [(w_ffn_down, whole, 0)])], [BF16])

    mx, my = lax.axis_index("x"), lax.axis_index("y")
    order = jnp.stack([2 * mx + my, 2 * mx + 1 - my, 2 * (1 - mx) + my, 2 * (1 - mx) + 1 - my]).astype(jnp.int32)
    (h0, z, u, g_in), (g_mixw, g_pool), g_small = _fwd_in(x[0], g_mix, w_in, order, tp, ag_mix, ag_small)
    wdw_full = g_small.transpose(1, 0, 2).reshape(48, D)[N_META:]
    (ac, m), (w_gu,) = _seq_fwd(z, wdw_full, b_dw, seq, ag_gu)
    (h1, s, merged, q), (g_down,) = _mix_fwd(ac, m, z, h0, b_gate, ln_g, ln_b, pool_scale, g_mixw, g_pool, ag_dn)
    w_dn = g_down.reshape(2, FFC, D)
    fg, fu, v, f, dh2, head_acc = _ffn_fwd(h1, loss_target[0], g_ffn, g_final.reshape(1, D), w_gu, w_dn)

    dfg, dfu, dh1, ffn_acc = _ffn_bwd(dh2, fg, fu, h1, g_ffn, w_gu, w_dn)
    own_f, sib_f, q_f = _rs_pair("rs_pair_ffn", [_wgrad_gu(v, dfg, dfu), _wgrad_down(f, dh2)])
    (dac, dm, dzg, dyc, dyp, dm2, mix_acc), rel_dn = _mix_bwd(
        dh1, z, s, q, ac, m, b_gate, ln_g, ln_b, pool_scale, g_mixw, g_pool, q_f[1:])
    p_mix = _wgrad_mix(s, dyc, q, dyp, merged, dh1, m, dm2)
    own_m, sib_m, q_m = _rs_pair("rs_pair_mix", list(p_mix))
    (dz, seq_acc), rel_gu = _seq_bwd(dac, dm, dzg, z, wdw_full, seq, q_f[:1])
    rel_f = [rel_gu[0], rel_dn[0]]
    p_in, rel_m = _wgrad_in(u, dz, q_m)
    own_i, sib_i, q_i = _rs_pair("rs_pair_in", [p_in])
    (grad_x, g_meta, in_acc), rel_i = _in_bwd(dz, h0, dh1, g_mix, g_in, seq, q_i)
    small_g = jnp.concatenate([g_meta, seq_acc[:CONV_K], jnp.zeros((1, D), F32)], axis=0)
    p_small = small_g.reshape(48, NDEV, D // NDEV).transpose(1, 0, 2).astype(BF16)
    rep_g = jnp.concatenate([
        in_acc[0:1], mix_acc[0:1, :D], mix_acc[0:1, D:], seq_acc[CONV_K:CONV_K + 1], mix_acc[1:2, :D], mix_acc[1:2, D:],
        mix_acc[2:3, :D], ffn_acc[0:1], head_acc[1:2], head_acc[0:1], jnp.zeros((REP_ROWS - 10, D), F32)], axis=0)
    own_s, sib_s, rel_s, rep_all = _reduce_scatter([p_small], rep_g)
    owns = [own_i[0], own_s[0], own_m[0], own_m[1], own_f[0], own_f[1]]
    sibs = [sib_i[0], sib_s[0], sib_m[0], sib_m[1], sib_f[0], sib_f[1]]
    rels = [rel_i[0], rel_s[0], rel_m[0], rel_m[1], rel_f[0], rel_f[1]]

    def lead(a):
        return a.reshape(1, *a.shape)

    def stack4(a, lead_dims):
        return a.reshape(*lead_dims, 1, 4 * 32, PG)

    (r_in,) = _adamw_multi("adamw_in", lead(owns[0]), sibs[0][:, None], rels[0][:, None], [w_in], [m_w_in], [v_w_in], 4)
    r_meta, r_dw = _adamw_meta_dw(owns[1], sibs[1], rels[1], (meta_tokens, m_meta_tokens, v_meta_tokens),
                                  (w_dw, m_w_dw, v_w_dw))
    r_conv, r_pout, r_o = _adamw_multi("adamw_mix", owns[2], sibs[2], rels[2], [w_conv_out, w_pool_out, w_o],
                                       [m_w_conv_out, m_w_pool_out, m_w_o], [v_w_conv_out, v_w_pool_out, v_w_o], 1)
    (r_pool,) = _adamw_multi("adamw_pool", stack4(owns[3], ()), stack4(sibs[3], (1,)), stack4(rels[3], (3,)),
                             [w_pool.reshape(1, 128, PG)], [m_w_pool.reshape(1, 128, PG)], [v_w_pool.reshape(1, 128, PG)], 1)
    r_pool = tuple(a.reshape(w_pool.shape) for a in r_pool)
    r_gate, r_up = _adamw_multi("adamw_gu", owns[4], sibs[4], rels[4], [tr(w_ffn_gate), tr(w_ffn_up)],
                                [tr(m_w_ffn_gate), tr(m_w_ffn_up)], [tr(v_w_ffn_gate), tr(v_w_ffn_up)], 2)
    r_gate, r_up = tuple(tr(a) for a in r_gate), tuple(tr(a) for a in r_up)
    (r_down,) = _adamw_multi("adamw_down", lead(owns[5]), sibs[5][:, None], rels[5][:, None],
                             [w_ffn_down], [m_w_ffn_down], [v_w_ffn_down], 2)
    row = (1, D)
    loss, reps = _adamw_rep(
        rep_all,
        [g_mix, b_gate, b_dw, ln_g, ln_b, pool_scale, g_ffn, g_final.reshape(row)],
        [m_g_mix, m_b_gate, m_b_dw, m_ln_g, m_ln_b, m_pool_scale, m_g_ffn, m_g_final.reshape(row)],
        [v_g_mix, v_b_gate, v_b_dw, v_ln_g, v_ln_b, v_pool_scale, v_g_ffn, v_g_final.reshape(row)])
    r_gmix, r_bg, r_bdw, r_lg, r_lb, r_ps, r_gffn, r_gfin = reps
    r_gfin = tuple(a.reshape(D) for a in r_gfin)

    in_order = [r_meta, r_gmix, r_in, r_bg, r_dw, r_bdw, r_lg, r_lb, r_conv, r_pool, r_ps, r_pout, r_o, r_gffn,
                r_gate, r_up, r_down, r_gfin]
    return (loss.reshape(()), grad_x[None], *[r[0] for r in in_order], *[r[1] for r in in_order],
            *[r[2] for r in in_order], *[r[3] for r in in_order])
```

```python
import math

import jax
import jax.numpy as jnp
from jax import lax
from jax.experimental import pallas as pl
from jax.experimental.pallas import tpu as pltpu

F32, BF16 = jnp.float32, jnp.bfloat16
MESH_ID = pl.DeviceIdType.MESH
NDEV = 8

D = 1024
N_META = 16
CONV_K = 31
HALO = 16
POOL_WINDOWS = (2, 4, 8, 16)
PG = 256
DIN = 5 * D
DFF = 2816
FFB = DFF // NDEV
FFC = DFF // 2
INB = DIN // NDEV
RMS_EPS = 1e-6
LN_EPS = 1e-5
ADAM_LR, ADAM_B1, ADAM_B2, ADAM_EPS, ADAM_WD, ADAM_STEP = 0.001, 0.9, 0.999, 1e-08, 0.01, 10

TM = 384
TMS = 384
TM_IO = 704
TM_WG = 1408
TM_WM = 704
MIB = 2 ** 20


def _sig(x):
    return 0.5 * jnp.tanh(0.5 * x) + 0.5


def _dot(a, b):
    return jnp.dot(a, b, preferred_element_type=F32)


def _dot_nt(a, b):
    return lax.dot_general(a, b, (((1,), (1,)), ((), ())), preferred_element_type=F32)


def _dot_tn(a, b):
    return lax.dot_general(a, b, (((0,), (0,)), ((), ())), preferred_element_type=F32)


def _pick(tp, pref):
    return pref if tp % pref == 0 else TM


def _params(sem, vmem_mib):
    return pltpu.CompilerParams(dimension_semantics=sem, vmem_limit_bytes=vmem_mib * MIB)


def _load_once(first, pairs, sems):
    @pl.when(first)
    def _():
        cps = [pltpu.make_async_copy(s, d, sems.at[k]) for k, (s, d) in enumerate(pairs)]
        for cp in cps:
            cp.start()
        for cp in cps:
            cp.wait()


def _place():
    x, y, c = lax.axis_index("x"), lax.axis_index("y"), lax.axis_index("c")
    return x, y, c


class _Gather:
    def __init__(self, groups, dtypes):
        self.groups, self.dtypes, self.n = groups, dtypes, len(groups)
        self.arrays = [a for _, parts in groups for a, _, _ in parts]
        self.out_shape = [jax.ShapeDtypeStruct((NDEV, *s), dt) for (s, _), dt in zip(groups, dtypes)]
        self.scratch = [pltpu.VMEM(s, dt) for (s, _), dt in zip(groups, dtypes)] + [
            pltpu.SemaphoreType.DMA((7 * self.n,)), pltpu.SemaphoreType.DMA((7 * self.n,)),
            pltpu.SemaphoreType.DMA((self.n,))]

    def bind(self, ins, outs, scratch):
        self.ins, self.outs, self.stages = ins, outs, scratch[:self.n]
        self.send_sems, self.recv_sems, self.local_sems = scratch[self.n:]
        return self

    def _copy(self, w, k, block, to, src=None):
        dst = self.outs[w].at[4 * block[0] + 2 * block[1] + block[2]]
        return pltpu.make_async_remote_copy(
            src_ref=dst if src is None else src, dst_ref=dst,
            send_sem=self.send_sems.at[7 * w + k], recv_sem=self.recv_sems.at[7 * w + k],
            device_id=to, device_id_type=MESH_ID)

    def _first(self):
        x, y, c = _place()
        me, sibling = (x, y, c), (x, y, 1 - c)
        chips = [(1 - x, y), (x, 1 - y), (1 - x, 1 - y)]
        mine, first = [], []
        for w in range(self.n):
            mine.append(pltpu.make_async_copy(self.stages[w], self.outs[w].at[4 * x + 2 * y + c], self.local_sems.at[w]))
            first.append(self._copy(w, 0, me, sibling, src=self.stages[w]))
            first += [self._copy(w, 1 + j, me, (*chip, c), src=self.stages[w]) for j, chip in enumerate(chips)]
        return mine, first

    def _passed(self):
        x, y, c = _place()
        chips = [(1 - x, y), (x, 1 - y), (1 - x, 1 - y)]
        return [self._copy(w, 4 + j, (*chip, c), (x, y, 1 - c)) for w in range(self.n) for j, chip in enumerate(chips)]

    def issue(self):
        a = 0
        for w in range(self.n):
            shape, parts = self.groups[w]
            if sum(arr.size for arr, _, _ in parts) < math.prod(shape):
                self.stages[w][...] = jnp.zeros(shape, self.dtypes[w])
            for _, dst, src in parts:
                self.stages[w][dst] = self.ins[a][src].astype(self.dtypes[w])
                a += 1
        mine, first = self._first()
        for cp in mine + first:
            cp.start()

    def forward(self):
        x, y, c = _place()
        chips = [(1 - x, y), (x, 1 - y), (1 - x, 1 - y)]
        passed = self._passed()
        for w in range(self.n):
            for j, chip in enumerate(chips):
                self._copy(w, 1 + j, (*chip, c), (x, y, c)).wait_recv()
                passed[3 * w + j].start()

    def finish(self):
        x, y, c = _place()
        chips = [(1 - x, y), (x, 1 - y), (1 - x, 1 - y)]
        for w in range(self.n):
            self._copy(w, 0, (x, y, 1 - c), (x, y, c)).wait_recv()
            for j, chip in enumerate(chips):
                self._copy(w, 4 + j, (*chip, 1 - c), (x, y, c)).wait_recv()
        mine, first = self._first()
        for cp in first + self._passed():
            cp.wait_send()
        for cp in mine:
            cp.wait()


class _ChipExchange:
    def __init__(self, qs):
        self.n = len(qs)
        self.out_shape = [jax.ShapeDtypeStruct(q.shape, q.dtype) for q in qs]
        self.scratch = [pltpu.SemaphoreType.DMA((3 * self.n,)), pltpu.SemaphoreType.DMA((3 * self.n,))]

    def bind(self, qs, rels, scratch):
        self.qs, self.rels = qs, rels
        self.send_sems, self.recv_sems = scratch
        return self

    def _copies(self):
        x, y, c = _place()
        chips = [(1 - x, y), (x, 1 - y), (1 - x, 1 - y)]
        return [pltpu.make_async_remote_copy(
            src_ref=self.qs[w].at[j], dst_ref=self.rels[w].at[j],
            send_sem=self.send_sems.at[3 * w + j], recv_sem=self.recv_sems.at[3 * w + j],
            device_id=(*chips[j], c), device_id_type=MESH_ID) for w in range(self.n) for j in range(3)]

    def issue(self):
        for cp in self._copies():
            cp.start()

    def finish(self):
        cps = self._copies()
        for cp in cps:
            cp.wait_recv()
        for cp in cps:
            cp.wait_send()


def _reduce_scatter(parts, small):
    n = len(parts)
    blks = [p.shape[1:] for p in parts]

    def body(*refs):
        ps, small_ref = refs[:n], refs[n]
        o = n + 1
        owns, sibs, rels, small_out = refs[o:o + n], refs[o + n:o + 2 * n], refs[o + 2 * n:o + 3 * n], refs[o + 3 * n]
        o += 3 * n + 1
        pa, pb, qst = refs[o:o + n], refs[o + n:o + 2 * n], refs[o + 2 * n:o + 3 * n]
        s1_send, s1_recv, s2_send, s2_recv, sm_send, sm_recv, lsem = refs[o + 3 * n:]
        x, y, c = _place()
        me = 4 * x + 2 * y + c
        sibling = (x, y, 1 - c)
        chips = [(1 - x, y), (x, 1 - y), (1 - x, 1 - y)]
        all_chips = [(x, y)] + chips

        own_cps = []
        for w in range(n):
            cp = pltpu.make_async_copy(ps[w].at[me], owns[w], lsem.at[w])
            cp.start()
            own_cps.append(cp)
        sm_own = pltpu.make_async_copy(small_ref, small_out.at[me], lsem.at[n])
        sm_own.start()

        def small_copy(r):
            peer = ((x + (r >> 2)) % 2, (y + ((r >> 1) & 1)) % 2, (c + (r & 1)) % 2)
            return pltpu.make_async_remote_copy(
                src_ref=small_ref, dst_ref=small_out.at[me], send_sem=sm_send.at[r - 1], recv_sem=sm_recv.at[r - 1],
                device_id=peer, device_id_type=MESH_ID)

        sm_cps = [small_copy(r) for r in range(1, NDEV)]
        for cp in sm_cps:
            cp.start()

        def pair_copy(w, rel):
            cx, cy = all_chips[rel]
            return pltpu.make_async_remote_copy(
                src_ref=ps[w].at[4 * cx + 2 * cy + (1 - c)], dst_ref=sibs[w].at[rel],
                send_sem=s1_send.at[4 * w + rel], recv_sem=s1_recv.at[4 * w + rel],
                device_id=sibling, device_id_type=MESH_ID)

        def chip_copy(w, j):
            return pltpu.make_async_remote_copy(
                src_ref=qst[w].at[j], dst_ref=rels[w].at[j],
                send_sem=s2_send.at[3 * w + j], recv_sem=s2_recv.at[3 * w + j],
                device_id=(*chips[j], c), device_id_type=MESH_ID)

        pair_cps = [pair_copy(w, rel) for w in range(n) for rel in (1, 2, 3, 0)]
        for cp in pair_cps:
            cp.start()
        chip_cps = []
        for w in range(n):
            for j, (cx, cy) in enumerate(chips):
                pair_copy(w, 1 + j).wait_recv()
                la = pltpu.make_async_copy(ps[w].at[4 * cx + 2 * cy + c], pa[w], lsem.at[n + 1])
                lb = pltpu.make_async_copy(sibs[w].at[1 + j], pb[w], lsem.at[n + 2])
                la.start()
                lb.start()
                la.wait()
                lb.wait()
                qst[w][j] = (pa[w][...].astype(F32) + pb[w][...].astype(F32)).astype(BF16)
                cp = chip_copy(w, j)
                cp.start()
                chip_cps.append(cp)
        for w in range(n):
            pair_copy(w, 0).wait_recv()
            for j in range(3):
                chip_copy(w, j).wait_recv()
        for cp in sm_cps:
            cp.wait_recv()
        for cp in pair_cps + chip_cps + sm_cps:
            cp.wait_send()
        for cp in own_cps:
            cp.wait()
        sm_own.wait()

    any_spec = pl.BlockSpec(memory_space=pl.ANY)
    outs = pl.pallas_call(
        body, name="rs_grads",
        out_shape=[jax.ShapeDtypeStruct(b, BF16) for b in blks]
        + [jax.ShapeDtypeStruct((4, *b), BF16) for b in blks]
        + [jax.ShapeDtypeStruct((3, *b), BF16) for b in blks]
        + [jax.ShapeDtypeStruct((NDEV, *small.shape), F32)],
        in_specs=[any_spec] * (n + 1),
        out_specs=[any_spec] * (3 * n + 1),
        scratch_shapes=[pltpu.VMEM(b, BF16) for b in blks] + [pltpu.VMEM(b, BF16) for b in blks]
        + [pltpu.VMEM((3, *b), BF16) for b in blks]
        + [pltpu.SemaphoreType.DMA((4 * n,)), pltpu.SemaphoreType.DMA((4 * n,)),
           pltpu.SemaphoreType.DMA((3 * n,)), pltpu.SemaphoreType.DMA((3 * n,)),
           pltpu.SemaphoreType.DMA((NDEV - 1,)), pltpu.SemaphoreType.DMA((NDEV - 1,)),
           pltpu.SemaphoreType.DMA((n + 3,))],
        compiler_params=pltpu.CompilerParams(vmem_limit_bytes=40 * MIB),
    )(*parts, small)
    return outs[:n], outs[n:2 * n], outs[2 * n:3 * n], outs[3 * n]


class _PairSum:
    def __init__(self, parts, keep_q=True):
        self.n = n = len(parts)
        self.keep_q = keep_q
        blks = [p.shape[1:] for p in parts]
        self.out_shape = [jax.ShapeDtypeStruct(b, BF16) for b in blks] + [jax.ShapeDtypeStruct((1, *b), BF16) for b in blks]
        if keep_q:
            self.out_shape += [jax.ShapeDtypeStruct((3, *b), BF16) for b in blks]
        self.scratch = [pltpu.VMEM((3, *b), BF16) for b in blks] * 3 + [
            pltpu.SemaphoreType.DMA((4 * n,)), pltpu.SemaphoreType.DMA((4 * n,)), pltpu.SemaphoreType.DMA((5 * n,))]

    def bind(self, ps, outs, scratch):
        n = self.n
        self.ps, self.owns, self.sibs, self.qs = ps, outs[:n], outs[n:2 * n], outs[2 * n:]
        self.pa, self.pb, self.qst = scratch[:n], scratch[n:2 * n], scratch[2 * n:3 * n]
        self.s_send, self.s_recv, self.lsem = scratch[3 * n:]
        return self

    def _local(self, with_q):
        n = self.n
        x, y, c = _place()
        chips = [(1 - x, y), (x, 1 - y), (1 - x, 1 - y)]
        own = [pltpu.make_async_copy(self.ps[w].at[4 * x + 2 * y + c], self.owns[w], self.lsem.at[w]) for w in range(n)]
        mine = [[pltpu.make_async_copy(self.ps[w].at[4 * cx + 2 * cy + c], self.pa[w].at[j], self.lsem.at[2 * n + 3 * w + j])
                 for j, (cx, cy) in enumerate(chips)] for w in range(n)]
        outq = [pltpu.make_async_copy(self.qst[w], self.qs[w], self.lsem.at[n + w]) for w in range(n)] if with_q else []
        return own, mine, outq

    def _pair(self, w, rel):
        x, y, c = _place()
        cx, cy = [(x, y), (1 - x, y), (x, 1 - y), (1 - x, 1 - y)][rel]
        return pltpu.make_async_remote_copy(
            src_ref=self.ps[w].at[4 * cx + 2 * cy + (1 - c)],
            dst_ref=self.sibs[w].at[0] if rel == 0 else self.pb[w].at[rel - 1],
            send_sem=self.s_send.at[4 * w + rel], recv_sem=self.s_recv.at[4 * w + rel],
            device_id=(x, y, 1 - c), device_id_type=MESH_ID)

    def issue(self):
        own, mine, _ = self._local(False)
        for cp in own + [cp for row in mine for cp in row]:
            cp.start()
        for w in range(self.n):
            for rel in (1, 2, 3, 0):
                self._pair(w, rel).start()

    def finish(self):
        own, mine, outq = self._local(self.keep_q)
        for w in range(self.n):
            for j in range(3):
                self._pair(w, 1 + j).wait_recv()
                mine[w][j].wait()
                self.qst[w][j] = (self.pa[w][j].astype(F32) + self.pb[w][j].astype(F32)).astype(BF16)
            if self.keep_q:
                outq[w].start()
        for w in range(self.n):
            self._pair(w, 0).wait_recv()
        for w in range(self.n):
            for rel in range(4):
                self._pair(w, rel).wait_send()
        for cp in own + outq:
            cp.wait()

    def results(self, outs):
        n = self.n
        return outs[:n], outs[n:2 * n], outs[2 * n:3 * n]


def _rs_pair(name, parts):
    ps = _PairSum(parts)
    n = ps.n

    def body(*refs):
        ps.bind(refs[:n], refs[n:4 * n], refs[4 * n:])
        ps.issue()
        ps.finish()

    any_spec = pl.BlockSpec(memory_space=pl.ANY)
    outs = pl.pallas_call(
        body, name=name, out_shape=ps.out_shape,
        in_specs=[any_spec] * n, out_specs=[any_spec] * (3 * n), scratch_shapes=ps.scratch,
        compiler_params=pltpu.CompilerParams(vmem_limit_bytes=48 * MIB),
    )(*parts)
    return ps.results(outs)


def _adamw_math(g, w, m, v):
    m = ADAM_B1 * m + (1.0 - ADAM_B1) * g
    v = ADAM_B2 * v + (1.0 - ADAM_B2) * (g * g)
    m_hat = m / (1.0 - ADAM_B1 ** ADAM_STEP)
    v_hat = v / (1.0 - ADAM_B2 ** ADAM_STEP)
    delta = -ADAM_LR * (m_hat / (jnp.sqrt(v_hat) + ADAM_EPS) + ADAM_WD * w)
    return delta, m, v


def _adamw_multi(name, own, sib, rel, ws, ms, vs, row_grid):
    k_n, r_n, c_n = own.shape
    rbk = r_n // row_grid

    def body(*refs):
        own_ref, sib_ref, r0_ref, r1_ref, r2_ref = refs[:5]
        w_refs, m_refs, v_refs = refs[5:5 + k_n], refs[5 + k_n:5 + 2 * k_n], refs[5 + 2 * k_n:5 + 3 * k_n]
        outs = refs[5 + 3 * k_n:]
        for k in range(k_n):
            g = own_ref[k].astype(F32) + sib_ref[k].astype(F32)
            g = g + r0_ref[k].astype(F32)
            g = g + r1_ref[k].astype(F32)
            g = g + r2_ref[k].astype(F32)
            delta, mm, vv = _adamw_math(g, w_refs[k][0], m_refs[k][0], v_refs[k][0])
            outs[4 * k][0] = g
            outs[4 * k + 1][0] = delta
            outs[4 * k + 2][0] = mm
            outs[4 * k + 3][0] = vv

    def lead(j):
        return pl.BlockSpec((None, k_n, rbk, c_n), lambda g: (j, 0, g, 0))

    wspec = pl.BlockSpec((1, rbk, c_n), lambda g: (0, g, 0))
    shp = jax.ShapeDtypeStruct((1, r_n, c_n), F32)
    res = pl.pallas_call(
        body, name=name, grid=(row_grid,),
        in_specs=[pl.BlockSpec((k_n, rbk, c_n), lambda g: (0, g, 0)), lead(0), lead(0), lead(1), lead(2)] + [wspec] * (3 * k_n),
        out_specs=[wspec] * (4 * k_n), out_shape=[shp] * (4 * k_n),
        compiler_params=_params(("arbitrary",), 40),
    )(own, sib, rel, rel, rel, *ws, *ms, *vs)
    return [tuple(res[4 * k:4 * k + 4]) for k in range(k_n)]


def _adamw_meta_dw(own, sib, rel, meta, dw):
    def body(own_ref, sib_ref, rel_ref, wm, mm, vm, wd, md, vd, *outs):
        def gsum(rows):
            g = own_ref[rows, :].astype(F32) + sib_ref[0, rows, :].astype(F32)
            for j in range(3):
                g = g + rel_ref[j, rows, :].astype(F32)
            return g

        g = gsum(pl.ds(0, N_META))
        delta, m2, v2 = _adamw_math(g, wm[...], mm[...], vm[...])
        for o, val in zip(outs[:4], (g, delta, m2, v2)):
            o[...] = val
        g = gsum(pl.ds(N_META, CONV_K))
        delta, m2, v2 = _adamw_math(g, wd[0], md[0], vd[0])
        for o, val in zip(outs[4:], (g, delta, m2, v2)):
            o[0] = val

    s_meta = jax.ShapeDtypeStruct(meta[0].shape, F32)
    s_dw = jax.ShapeDtypeStruct(dw[0].shape, F32)
    res = pl.pallas_call(body, name="adamw_meta_dw", out_shape=[s_meta] * 4 + [s_dw] * 4)(own, sib, rel, *meta, *dw)
    return tuple(res[:4]), tuple(res[4:])


REP_ROWS = 16


def _adamw_rep(gathered, ws, ms, vs):
    rows = [(0, 1), (1, 2), (3, 1), (4, 1), (5, 1), (6, 1), (7, 1), (8, 1)]

    def body(g_ref, *refs):
        w_refs, m_refs, v_refs = refs[:8], refs[8:16], refs[16:24]
        loss_ref, outs, acc = refs[24], refs[25:57], refs[57]
        g = g_ref[0]
        for d in range(1, NDEV):
            g = g + g_ref[d]
        acc[...] = g
        loss_ref[...] = (0.5 / D) * jnp.sum(acc[pl.ds(9, 1), :], axis=1, keepdims=True)
        for p, (r0, nr) in enumerate(rows):
            for h in range(nr):
                cols = pl.ds(h * D, D)
                gp = acc[pl.ds(r0 + h, 1), :]
                delta, mm, vv = _adamw_math(gp, w_refs[p][:, cols], m_refs[p][:, cols], v_refs[p][:, cols])
                for o, val in zip(outs[4 * p:4 * p + 4], (gp, delta, mm, vv)):
                    o[:, cols] = val

    shapes = [jax.ShapeDtypeStruct(w.shape, F32) for w in ws]
    res = pl.pallas_call(
        body, name="adamw_rep",
        out_shape=[jax.ShapeDtypeStruct((1, 1), F32)] + [s for s in shapes for _ in range(4)],
        scratch_shapes=[pltpu.VMEM((REP_ROWS, D), F32)],
    )(gathered, *ws, *ms, *vs)
    return res[0], [tuple(res[1 + 4 * p:5 + 4 * p]) for p in range(8)]


def _load_ffn(i, j, wgu_hbm, wgu, wdn_hbm, wdn, sems):
    half = NDEV // 2

    def copies(ch):
        pairs = [(wgu_hbm.at[half * ch + d, g], wgu.at[g, ch, pl.ds(FFB * d, FFB), :]) for g in range(2) for d in range(half)]
        pairs.append((wdn_hbm.at[ch], wdn.at[ch]))
        return [pltpu.make_async_copy(s, t, sems.at[(2 * half + 1) * ch + k]) for k, (s, t) in enumerate(pairs)]

    @pl.when((i == 0) & (j == 0))
    def _():
        for cp in copies(0) + copies(1):
            cp.start()

    for ch in range(2):
        @pl.when((i == 0) & (j == ch))
        def _():
            for cp in copies(ch):
                cp.wait()


def _win_pairs(w_hbm, w_vm):
    return [(w_hbm.at[q], w_vm.at[q // 2, :, pl.ds(2 * INB * (q % 2), 2 * INB)]) for q in range(4)]


def _whole(a):
    nd = a.ndim
    return pl.BlockSpec(a.shape, lambda *g: (0,) * nd)


CHIPW = 2 * INB
PHASE_CHIP = (1, 0, 2)
assert PHASE_CHIP[2] == 2


class _GatherIn:
    scratch = [pltpu.VMEM((D, INB), BF16), pltpu.SemaphoreType.DMA((7,)), pltpu.SemaphoreType.DMA((7,)),
               pltpu.SemaphoreType.DMA((1,))]

    def bind(self, w_ref, w_vm, scratch):
        self.w_ref, self.w_vm = w_ref, w_vm
        self.stage, self.send_sems, self.recv_sems, self.local_sem = scratch
        return self

    def _win(self, chip, core):
        return self.w_vm.at[2 * chip[0] + chip[1], core]

    def _copy(self, k, chip, core, to, src=None):
        dst = self._win(chip, core)
        return pltpu.make_async_remote_copy(
            src_ref=dst if src is None else src, dst_ref=dst, send_sem=self.send_sems.at[k],
            recv_sem=self.recv_sems.at[k], device_id=to, device_id_type=MESH_ID)

    def _mine(self, cs):
        x, y, _ = _place()
        return pltpu.make_async_copy(self.stage, self._win((x, y), cs), self.local_sem.at[0])

    def issue(self, cs):
        x, y, _ = _place()
        chips = [(1 - x, y), (x, 1 - y), (1 - x, 1 - y)]
        self.stage[...] = self.w_ref[0].astype(BF16)
        self._mine(cs).start()
        self._copy(0, (x, y), cs, (x, y, 1 - cs), src=self.stage).start()
        for j in PHASE_CHIP[:2]:
            self._copy(1 + j, (x, y), cs, (*chips[j], cs), src=self.stage).start()

    def wait_chip(self, phase, cs):
        x, y, _ = _place()
        chips = [(1 - x, y), (x, 1 - y), (1 - x, 1 - y)]
        if phase == 0:
            self._mine(cs).wait()
            self._copy(0, (x, y), 1 - cs, (x, y, cs)).wait_recv()
            return
        j = PHASE_CHIP[phase - 1]
        self._copy(1 + j, chips[j], cs, (x, y, cs)).wait_recv()
        self._copy(4 + j, chips[j], cs, (x, y, 1 - cs)).start()
        if phase == 1:
            self._copy(3, (x, y), cs, (*chips[2], cs), src=self.stage).start()
        self._copy(4 + j, chips[j], 1 - cs, (x, y, cs)).wait_recv()

    def finish(self, cs):
        x, y, _ = _place()
        for k in range(7):
            self._copy(k, (x, y), cs, (x, y, cs), src=self.stage).wait_send()


def _fwd_in(x2, g_mix, w_in, order, tp, ag, ags):
    tm = _pick(tp, TM_IO)
    nt = tp // tm
    nx_last = x2.shape[0] - (nt - 1) * tm
    na, ng, ns = len(ag.arrays), ag.n, len(ags.arrays)
    gin = _GatherIn()

    def body(order_ref, *refs):
        x_ref, g_ref, w_ref = refs[:3]
        o = 3 + na + ns
        h_ref, z_ref, u_ref, wout_ref = refs[o:o + 4]
        s = o + 4 + ng + 1
        w_vm, u_all, osem, sm_vm = refs[s:s + 4]
        gin.bind(w_ref, w_vm, refs[s + 4:s + 8])
        ag.bind(refs[3:3 + na], refs[o + 4:o + 4 + ng], refs[s + 8:s + 8 + len(ag.scratch)])
        ags.bind(refs[3 + na:3 + na + ns], refs[o + 4 + ng:o + 5 + ng], refs[s + 8 + len(ag.scratch):])
        ph, i = pl.program_id(0), pl.program_id(1)
        core = lax.axis_index("c")
        first = (ph == 0) & (i == 0)
        last = (ph == 3) & (i == nt - 1)
        @pl.when(first)
        def _():
            ags.issue()

        for cs in range(2):
            @pl.when(first & (core == cs))
            def _():
                gin.issue(cs)

        @pl.when((ph == 0) & (i == max(nt - 2, 0)))
        def _():
            ags.forward()

        for cs in range(2):
            for p in range(4):
                @pl.when((ph == p) & (i == 0) & (core == cs))
                def _():
                    gin.wait_chip(p, cs)

        @pl.when((ph == 2) & (i == 0))
        def _():
            ag.issue()

        out_copies = [pltpu.make_async_copy(w_vm.at[k, c], wout_ref.at[k, :, pl.ds(INB * c, INB)], osem.at[2 * k + c])
                      for k in range(4) for c in range(2)]

        @pl.when((ph == 3) & (i == 0))
        def _():
            for cp in out_copies:
                cp.start()

        @pl.when((ph == 0) & (i < nt - 1))
        def _():
            h_ref[...] = x_ref[...]

        @pl.when((ph == 0) & (i == nt - 1))
        def _():
            ags.finish()
            cp = pltpu.make_async_copy(ags.outs[0], sm_vm, osem.at[8])
            cp.start()
            h_ref[pl.ds(0, nx_last), :] = x_ref[pl.ds(0, nx_last), :]
            h_ref[pl.ds(nx_last, tm - nx_last - N_META), :] = jnp.zeros((tm - nx_last - N_META, D), F32)
            cp.wait()
            for d in range(NDEV):
                h_ref[pl.ds(tm - N_META, N_META), pl.ds(128 * d, 128)] = sm_vm[d, pl.ds(0, N_META), :]

        @pl.when(ph == 0)
        def _():
            xv = h_ref[...]
            r = lax.rsqrt(jnp.mean(xv * xv, axis=-1, keepdims=True) + RMS_EPS)
            u = (xv * r * g_ref[...]).astype(BF16)
            u_ref[...] = u
            u_all[i] = u

        for c in range(2):
            z_ref[:, INB * c:INB * (c + 1)] = _dot(u_all[i], w_vm[order_ref[ph], c])

        @pl.when(last)
        def _():
            ag.forward()
            ag.finish()
            for cp in out_copies:
                cp.wait()

        for cs in range(2):
            @pl.when(last & (core == cs))
            def _():
                gin.finish(cs)

    def rows(ph, i, order):
        return (jnp.where(ph == 0, i, nt - 1), 0)

    tile = pl.BlockSpec((tm, D), rows)
    anys = pl.BlockSpec(memory_space=pl.ANY)
    res = pl.pallas_call(
        body, name="fwd_in",
        grid_spec=pltpu.PrefetchScalarGridSpec(
            num_scalar_prefetch=1, grid=(4, nt),
            in_specs=[tile, pl.BlockSpec((1, D), lambda ph, i, order: (0, 0)), _whole(w_in)]
            + [_whole(a) for a in ag.arrays + ags.arrays],
            out_specs=[tile, pl.BlockSpec((tm, CHIPW), lambda ph, i, order: (i, order[ph])), tile, anys] + [anys] * (ng + 1),
            scratch_shapes=[pltpu.VMEM((4, 2, D, INB), BF16), pltpu.VMEM((nt, tm, D), BF16), pltpu.SemaphoreType.DMA((9,)),
                            pltpu.VMEM(ags.out_shape[0].shape, F32)] + gin.scratch + ag.scratch + ags.scratch),
        out_shape=[jax.ShapeDtypeStruct((tp, D), F32), jax.ShapeDtypeStruct((tp, DIN), F32),
                   jax.ShapeDtypeStruct((tp, D), BF16), jax.ShapeDtypeStruct((4, D, CHIPW), BF16)]
        + ag.out_shape + ags.out_shape,
        compiler_params=_params(("arbitrary", "arbitrary"), 58),
    )(order, x2, g_mix, w_in, *ag.arrays, *ags.arrays)
    return res[:4], res[4:4 + ng], res[4 + ng]


def _halo_specs(col, nt, width=D):
    r = TM // HALO
    nb = nt * r
    return [pl.BlockSpec((HALO, width), lambda i: ((i * r + nb - 1) % nb, col)),
            pl.BlockSpec((TM, width), lambda i: (i, col)),
            pl.BlockSpec((HALO, width), lambda i: (((i + 1) * r) % nb, col))]


NCB = D // 128
TME = TM + 2 * HALO
CONV_STEPS = 16
assert TM % CONV_STEPS == 0


def _tm_fill(dst, time0, groups, tile_fn, unroll=1):
    def body(g, c):
        for j in range(NCB):
            dst[pl.ds((time0 + 8 * g) * NCB + j, 8, stride=NCB), :] = tile_fn(pl.multiple_of(8 * g, 8), pl.ds(128 * j, 128))
        return c

    lax.fori_loop(0, groups, body, 0, unroll=unroll)


def _tm_fill_ext(dst, left, cur, right, fn, unroll=1):
    _tm_fill(dst, 0, HALO // 8, lambda r, l: fn(left, pl.ds(r, 8), l), unroll)
    _tm_fill(dst, HALO, TM // 8, lambda r, l: fn(cur, pl.ds(r, 8), l), unroll)
    _tm_fill(dst, HALO + TM, HALO // 8, lambda r, l: fn(right, pl.ds(r, 8), l), unroll)


def _tm_read(src, groups, store_fn):
    def body(g, c):
        for j in range(NCB):
            store_fn(pl.ds(pl.multiple_of(8 * g, 8), 8), pl.ds(128 * j, 128), src[pl.ds(8 * g * NCB + j, 8, stride=NCB), :])
        return c

    lax.fori_loop(0, groups, body, 0, unroll=2)


def _tm_rows(t):
    return pl.ds(t * NCB if isinstance(t, int) else pl.multiple_of(t * NCB, NCB), NCB)


def _tm_at(ref, t):
    return ref[_tm_rows(t), :]


def _by_group(sub, vals):
    return jnp.where(sub < 2, vals[0], jnp.where(sub < 4, vals[1], jnp.where(sub < 6, vals[2], vals[3])))


def _pool_cnt(b, seq, tp, sub):
    b = jnp.where(b < 0, b + tp, b)
    b = jnp.where(b >= tp, b - tp, b)
    t = jnp.where(b < seq, b + N_META, b - (tp - N_META))
    cnts = []
    for win in POOL_WINDOWS:
        left = win // 2
        lo = jnp.maximum(t - left, 0)
        hi = jnp.minimum(t + win - left, seq + N_META)
        cnts.append(jnp.maximum(hi - lo, 1).astype(F32))
    return _by_group(sub, cnts)


def _edge_rows(seq, tp):
    reach = max(POOL_WINDOWS) // 2
    return [tp - N_META + t for t in range(reach)] + [seq - reach + 1 + t for t in range(reach - 1)]


def _edge_gain(b, seq, tp, sub):
    return _by_group(sub, [float(w) for w in POOL_WINDOWS]) / _pool_cnt(b, seq, tp, sub)


def _nested_windows(at, lo_offs):
    sums, s, have = [], None, set()
    for g, win in enumerate(POOL_WINDOWS):
        for o in range(lo_offs[g], lo_offs[g] + win):
            if o not in have:
                have.add(o)
                s = at(o) if s is None else s + at(o)
        sums.append(s)
    return sums


def _seq_fwd(z, w_dw, b_dw, seq, gat):
    tp = z.shape[0]
    nt = tp // TM
    na, ng = len(gat.arrays), gat.n

    def body(*refs):
        av_l, av, av_r, ag_l, ag, ag_r, p_l, p, p_r, w_ref, b_ref = refs[:11]
        ac_ref, m_ref = refs[11 + na:13 + na]
        a3, p3, o3, m3, w3, b3, m2d = refs[13 + na + ng:20 + na + ng]
        gat.bind(refs[11:11 + na], refs[13 + na:13 + na + ng], refs[20 + na + ng:])
        i = pl.program_id(0)
        sub = lax.broadcasted_iota(jnp.int32, (NCB, 128), 0)

        @pl.when(i == 0)
        def _():
            gat.issue()
            _tm_fill(w3, 0, 4, lambda r, l: w_ref[pl.ds(r, 8), l])
            for j in range(NCB):
                b3[pl.ds(j, 1), :] = b_ref[:, pl.ds(128 * j, 128)]

        @pl.when(i == max(nt - 2, 0))
        def _():
            gat.forward()

        _tm_fill_ext(a3, (av_l, ag_l), (av, ag), (av_r, ag_r), lambda vg, r, l: vg[0][r, l] * _sig(vg[1][r, l]), unroll=2)
        _tm_fill_ext(p3, p_l, p, p_r, lambda ref, r, l: ref[r, l])

        def conv(g, c):
            accs = [b3[...]] * 16
            for k in range(CONV_K):
                wk = _tm_at(w3, k)
                for t in range(16):
                    accs[t] = accs[t] + wk * _tm_at(a3, 16 * g + t + k + 1)
            for t in range(16):
                o3[_tm_rows(16 * g + t), :] = accs[t]
            return c

        lax.fori_loop(0, TM // 16, conv, 0)
        _tm_read(o3, TM // 8, lambda r, l, tile: ac_ref.__setitem__((r, l), tile))

        inv = _by_group(sub, [1.0 / w for w in POOL_WINDOWS])

        def pool(g, c):
            for t in range(8):
                e = 8 * g + t + HALO
                sums = _nested_windows(lambda o: _tm_at(p3, e + o), [-(w // 2) for w in POOL_WINDOWS])
                m3[_tm_rows(8 * g + t), :] = _by_group(sub, sums) * inv - _tm_at(p3, e)
            return c

        lax.fori_loop(0, TM // 8, pool, 0)
        for b in _edge_rows(seq, tp):
            r = b - i * TM

            @pl.when((r >= 0) & (r < TM))
            def _():
                pv = _tm_at(p3, r + HALO)
                m3[_tm_rows(r), :] = (_tm_at(m3, r) + pv) * _edge_gain(b, seq, tp, sub) - pv

        _tm_read(m3, TM // 8, lambda r, l, tile: m2d.__setitem__((r, l), tile))
        m_ref[...] = m2d[...].astype(BF16)

        @pl.when(i == nt - 1)
        def _():
            gat.finish()

    tmaj = pltpu.VMEM((TM * NCB, 128), F32)
    text = pltpu.VMEM((TME * NCB, 128), F32)
    res = pl.pallas_call(
        body, name="seq_fwd", grid=(nt,),
        in_specs=_halo_specs(0, nt) + _halo_specs(1, nt) + _halo_specs(2, nt)
        + [pl.BlockSpec((32, D), lambda i: (0, 0)), pl.BlockSpec((1, D), lambda i: (0, 0))] + [_whole(a) for a in gat.arrays],
        out_specs=[pl.BlockSpec((TM, D), lambda i: (i, 0))] * 2 + [pl.BlockSpec(memory_space=pl.ANY)] * ng,
        out_shape=[jax.ShapeDtypeStruct((tp, D), F32), jax.ShapeDtypeStruct((tp, D), BF16)] + gat.out_shape,
        scratch_shapes=[text, text, tmaj, tmaj, pltpu.VMEM((32 * NCB, 128), F32), pltpu.VMEM((NCB, 128), F32),
                        pltpu.VMEM((TM, D), F32)] + gat.scratch,
        compiler_params=_params(("arbitrary",), 52),
    )(z, z, z, z, z, z, z, z, z, w_dw, b_dw, *gat.arrays)
    return res[:2], res[2:]


def _ln_stats(ac):
    mu = jnp.mean(ac, axis=-1, keepdims=True)
    xc = ac - mu
    rl = lax.rsqrt(jnp.mean(xc * xc, axis=-1, keepdims=True) + LN_EPS)
    return xc * rl, rl


def _pool_mix(m, wp_ref):
    return jnp.concatenate(
        [_dot(m[:, g * PG:(g + 1) * PG], wp_ref[:, g].reshape(PG, PG)) for g in range(4)], axis=1)


def _mix_fwd(ac, m, z, h0, b_gate, ln_g, ln_b, pool_scale, g_mixw, g_pool, gat):
    tp = h0.shape[0]
    tms = TM
    nt = tp // tms
    na, ng = len(gat.arrays), gat.n

    def body(*refs):
        ac_ref, m_ref, zga, zgb, h_ref, bg_ref, lg_ref, lb_ref, ps_ref, wm_hbm, wp_hbm = refs[:11]
        h1_ref, s_ref, mg_ref, q_ref = refs[11 + na:15 + na]
        wm, wp, sems = refs[15 + na + ng:18 + na + ng]
        gat.bind(refs[11:11 + na], refs[15 + na:15 + na + ng], refs[18 + na + ng:])
        i = pl.program_id(0)

        @pl.when(i == 0)
        def _():
            gat.issue()

        @pl.when(i == max(nt - 4, 0))
        def _():
            gat.forward()

        @pl.when(i == nt - 1)
        def _():
            gat.finish()

        _load_once(i == 0, [(wm_hbm, wm), (wp_hbm, wp)], sems)
        n, _ = _ln_stats(ac_ref[...])
        l = n * lg_ref[...] + lb_ref[...]
        s = (l * _sig(l)).astype(BF16)
        s_ref[...] = s
        yc = _dot(s, wm[:, 0].reshape(D, D))
        q = (_pool_mix(m_ref[...], wp) * ps_ref[...]).astype(BF16)
        q_ref[...] = q
        yp = _dot(q, wm[:, 1].reshape(D, D))
        ga = _sig(zga[...] + bg_ref[:, :D])
        gb = _sig(zgb[...] + bg_ref[:, D:])
        merged = (ga * yc + gb * yp).astype(BF16)
        mg_ref[...] = merged
        h1_ref[...] = h_ref[...] + _dot(merged, wm[:, 2].reshape(D, D))

    def tile(col=0):
        return pl.BlockSpec((tms, D), lambda i: (i, col))

    def vec(w):
        return pl.BlockSpec((1, w), lambda i: (0, 0))

    anys = pl.BlockSpec(memory_space=pl.ANY)
    f32o, b16o = jax.ShapeDtypeStruct((tp, D), F32), jax.ShapeDtypeStruct((tp, D), BF16)
    res = pl.pallas_call(
        body, name="mix_fwd", grid=(nt,),
        in_specs=[tile(), tile(), tile(3), tile(4), tile(), vec(2 * D), vec(D), vec(D), vec(D), anys, anys]
        + [_whole(a) for a in gat.arrays],
        out_specs=[tile()] * 4 + [anys] * ng,
        out_shape=[f32o, b16o, b16o, b16o] + gat.out_shape,
        scratch_shapes=[pltpu.VMEM((NDEV, 3, D // NDEV, D), BF16), pltpu.VMEM((NDEV, 4, PG // NDEV, PG), BF16),
                        pltpu.SemaphoreType.DMA((2,))] + gat.scratch,
        compiler_params=_params(("arbitrary",), 52),
    )(ac, m, z, z, h0, b_gate, ln_g, ln_b, pool_scale, g_mixw, g_pool, *gat.arrays)
    return res[:4], res[4:]


def _ffn_fwd(h1, tgt, g_ffn, g_final, w_gu, w_dn):
    tp = h1.shape[0]
    nt = tp // TM
    nx_last = tgt.shape[0] - (nt - 1) * TM

    def body(h_ref, t_ref, gf_ref, gl_ref, wgu_hbm, wdn_hbm,
             fg_ref, fu_ref, v_ref, f_ref, dh2_ref, acc_ref, wgu, wdn, v_sc, h2_sc, diff_sc, sems):
        i, j = pl.program_id(0), pl.program_id(1)
        _load_ffn(i, j, wgu_hbm, wgu, wdn_hbm, wdn, sems)

        @pl.when((i == 0) & (j == 0))
        def _():
            acc_ref[...] = jnp.zeros_like(acc_ref)

        @pl.when(j == 0)
        def _():
            h = h_ref[...]
            r = lax.rsqrt(jnp.mean(h * h, axis=-1, keepdims=True) + RMS_EPS)
            v = (h * r * gf_ref[...]).astype(BF16)
            v_sc[...] = v
            v_ref[...] = v
            h2_sc[...] = h

        v = v_sc[...]
        fg = _dot_nt(v, wgu[0, j])
        fu = _dot_nt(v, wgu[1, j])
        fg_ref[...] = fg
        fu_ref[...] = fu
        f = ((fg * _sig(fg)) * fu).astype(BF16)
        f_ref[...] = f
        h2_sc[...] += _dot(f, wdn[j])

        @pl.when(j == 1)
        def _():
            h2 = h2_sc[...]
            r = lax.rsqrt(jnp.mean(h2 * h2, axis=-1, keepdims=True) + RMS_EPS)
            n2 = h2 * r
            y = n2 * gl_ref[...]

            @pl.when(i < nt - 1)
            def _():
                diff_sc[...] = y - t_ref[...]

            @pl.when(i == nt - 1)
            def _():
                diff_sc[pl.ds(0, nx_last), :] = y[:nx_last] - t_ref[pl.ds(0, nx_last), :]
                diff_sc[pl.ds(nx_last, TM - nx_last), :] = jnp.zeros((TM - nx_last, D), F32)

            diff = diff_sc[...]
            dy = diff * (1.0 / D)
            acc_ref[0:1, :] += jnp.sum(diff * diff, axis=0, keepdims=True)
            acc_ref[1:2, :] += jnp.sum(dy * n2, axis=0, keepdims=True)
            dn = dy * gl_ref[...]
            dh2_ref[...] = r * (dn - n2 * jnp.mean(dn * n2, axis=-1, keepdims=True))

    def tile():
        return pl.BlockSpec((TM, D), lambda i, j: (i, 0))

    def chunk():
        return pl.BlockSpec((TM, FFC), lambda i, j: (i, j))

    def vec():
        return pl.BlockSpec((1, D), lambda i, j: (0, 0))

    anys = pl.BlockSpec(memory_space=pl.ANY)
    hid32, hid16 = jax.ShapeDtypeStruct((tp, DFF), F32), jax.ShapeDtypeStruct((tp, DFF), BF16)
    return pl.pallas_call(
        body, name="ffn_fwd", grid=(nt, 2),
        in_specs=[tile(), tile(), vec(), vec(), anys, anys],
        out_specs=[chunk(), chunk(), tile(), chunk(), tile(), pl.BlockSpec((8, D), lambda i, j: (0, 0))],
        out_shape=[hid32, hid32, jax.ShapeDtypeStruct((tp, D), BF16), hid16, jax.ShapeDtypeStruct((tp, D), F32),
                   jax.ShapeDtypeStruct((8, D), F32)],
        scratch_shapes=[pltpu.VMEM((2, 2, FFC, D), BF16), pltpu.VMEM((2, FFC, D), BF16),
                        pltpu.VMEM((TM, D), BF16), pltpu.VMEM((TM, D), F32), pltpu.VMEM((TM, D), F32),
                        pltpu.SemaphoreType.DMA((2 * NDEV + 2,))],
        compiler_params=_params(("arbitrary", "arbitrary"), 56),
    )(h1, tgt, g_ffn, g_final, w_gu, w_dn)


def _ffn_bwd(dh2, fg, fu, h1, g_ffn, w_gu, w_dn):
    tp = h1.shape[0]
    nt = tp // TM

    def body(dh2_ref, fg_ref, fu_ref, h_ref, gf_ref, wgu_hbm, wdn_hbm,
             dfg_ref, dfu_ref, dh1_ref, acc_ref, wgu, wdn, d_sc, dv_sc, sems):
        i, j = pl.program_id(0), pl.program_id(1)
        _load_ffn(i, j, wgu_hbm, wgu, wdn_hbm, wdn, sems)

        @pl.when((i == 0) & (j == 0))
        def _():
            acc_ref[...] = jnp.zeros_like(acc_ref)

        @pl.when(j == 0)
        def _():
            d_sc[...] = dh2_ref[...].astype(BF16)
            dv_sc[...] = jnp.zeros_like(dv_sc)

        df = _dot_nt(d_sc[...], wdn[j])
        fg = fg_ref[...]
        sg = _sig(fg)
        dfu = (df * (fg * sg)).astype(BF16)
        dfg = (df * fu_ref[...] * (sg * (1.0 + fg * (1.0 - sg)))).astype(BF16)
        dfg_ref[...] = dfg
        dfu_ref[...] = dfu
        dv_sc[...] += _dot(dfg, wgu[0, j]) + _dot(dfu, wgu[1, j])

        @pl.when(j == 1)
        def _():
            h = h_ref[...]
            r = lax.rsqrt(jnp.mean(h * h, axis=-1, keepdims=True) + RMS_EPS)
            n1 = h * r
            dv = dv_sc[...]
            acc_ref[0:1, :] += jnp.sum(dv * n1, axis=0, keepdims=True)
            dn = dv * gf_ref[...]
            dh1_ref[...] = dh2_ref[...] + r * (dn - n1 * jnp.mean(dn * n1, axis=-1, keepdims=True))

    def tile():
        return pl.BlockSpec((TM, D), lambda i, j: (i, 0))

    def chunk():
        return pl.BlockSpec((TM, FFC), lambda i, j: (i, j))

    anys = pl.BlockSpec(memory_space=pl.ANY)
    hid16 = jax.ShapeDtypeStruct((tp, DFF), BF16)
    return pl.pallas_call(
        body, name="ffn_bwd", grid=(nt, 2),
        in_specs=[tile(), chunk(), chunk(), tile(), pl.BlockSpec((1, D), lambda i, j: (0, 0)), anys, anys],
        out_specs=[chunk(), chunk(), tile(), pl.BlockSpec((8, D), lambda i, j: (0, 0))],
        out_shape=[hid16, hid16, jax.ShapeDtypeStruct((tp, D), F32), jax.ShapeDtypeStruct((8, D), F32)],
        scratch_shapes=[pltpu.VMEM((2, 2, FFC, D), BF16), pltpu.VMEM((2, FFC, D), BF16),
                        pltpu.VMEM((TM, D), BF16), pltpu.VMEM((TM, D), F32), pltpu.SemaphoreType.DMA((2 * NDEV + 2,))],
        compiler_params=_params(("arbitrary", "arbitrary"), 56),
    )(dh2, fg, fu, h1, g_ffn, w_gu, w_dn)


def _mix_bwd(dh1, z, s, q, ac, m, b_gate, ln_g, ln_b, pool_scale, g_mixw, g_pool, qs):
    tp = dh1.shape[0]
    nt = tp // TMS
    ex = _ChipExchange(qs)
    nq = ex.n

    def body(*refs):
        dh1_ref, zga, zgb, s_ref, q_ref, ac_ref, m_ref, bg_ref, lg_ref, lb_ref, ps_ref, wm_hbm, wp_hbm = refs[:13]
        dac_ref, dm_ref, dzg_ref, dyc_ref, dyp_ref, dm2_ref, acc_ref = refs[13 + nq:20 + nq]
        wm, wp, sems = refs[20 + 2 * nq:23 + 2 * nq]
        ex.bind(refs[13:13 + nq], refs[20 + nq:20 + 2 * nq], refs[23 + 2 * nq:])
        first = pl.program_id(0) == 0

        @pl.when(first)
        def _():
            ex.issue()
            acc_ref[...] = jnp.zeros_like(acc_ref)

        _load_once(first, [(wm_hbm, wm), (wp_hbm, wp)], sems)

        dmerged = _dot_nt(dh1_ref[...].astype(BF16), wm[:, 2].reshape(D, D))
        ga = _sig(zga[...] + bg_ref[:, :D])
        gb = _sig(zgb[...] + bg_ref[:, D:])
        dyc = dmerged * ga
        dyp = dmerged * gb
        dza = (dmerged * _dot(s_ref[...], wm[:, 0].reshape(D, D))) * (ga * (1.0 - ga))
        dzb = (dmerged * _dot(q_ref[...], wm[:, 1].reshape(D, D))) * (gb * (1.0 - gb))
        dzg_ref[:, :D] = dza.astype(BF16)
        dzg_ref[:, D:] = dzb.astype(BF16)
        acc_ref[0:1, :D] += jnp.sum(dza, axis=0, keepdims=True)
        acc_ref[0:1, D:] += jnp.sum(dzb, axis=0, keepdims=True)
        dyc_b = dyc.astype(BF16)
        dyp_b = dyp.astype(BF16)
        dyc_ref[...] = dyc_b
        dyp_ref[...] = dyp_b
        ds = _dot_nt(dyc_b, wm[:, 0].reshape(D, D))
        n, rl = _ln_stats(ac_ref[...])
        l = n * lg_ref[...] + lb_ref[...]
        sg = _sig(l)
        dl = ds * (sg * (1.0 + l * (1.0 - sg)))
        acc_ref[1:2, :D] += jnp.sum(dl * n, axis=0, keepdims=True)
        acc_ref[1:2, D:] += jnp.sum(dl, axis=0, keepdims=True)
        dn = dl * lg_ref[...]
        dac_ref[...] = rl * (dn - jnp.mean(dn, axis=-1, keepdims=True) - n * jnp.mean(dn * n, axis=-1, keepdims=True))
        dq = _dot_nt(dyp_b, wm[:, 1].reshape(D, D))
        mv = m_ref[...]
        acc_ref[2:3, :D] += jnp.sum(dq * _pool_mix(mv, wp), axis=0, keepdims=True)
        dm2 = (dq * ps_ref[...]).astype(BF16)
        dm2_ref[...] = dm2
        dm_ref[...] = jnp.concatenate(
            [_dot_nt(dm2[:, g * PG:(g + 1) * PG], wp[:, g].reshape(PG, PG)) for g in range(4)], axis=1)

        @pl.when(pl.program_id(0) == nt - 1)
        def _():
            ex.finish()

    def tile(col=0):
        return pl.BlockSpec((TMS, D), lambda i: (i, col))

    def vec(w):
        return pl.BlockSpec((1, w), lambda i: (0, 0))

    anys = pl.BlockSpec(memory_space=pl.ANY)
    f32o, b16o = jax.ShapeDtypeStruct((tp, D), F32), jax.ShapeDtypeStruct((tp, D), BF16)
    res = pl.pallas_call(
        body, name="mix_bwd", grid=(nt,),
        in_specs=[tile(), tile(3), tile(4), tile(), tile(), tile(), tile(), vec(2 * D), vec(D), vec(D), vec(D), anys, anys]
        + [anys] * nq,
        out_specs=[tile(), tile(), pl.BlockSpec((TMS, 2 * D), lambda i: (i, 0)), tile(), tile(), tile(),
                   pl.BlockSpec((8, 2 * D), lambda i: (0, 0))] + [anys] * nq,
        out_shape=[f32o, f32o, jax.ShapeDtypeStruct((tp, 2 * D), BF16), b16o, b16o, b16o,
                   jax.ShapeDtypeStruct((8, 2 * D), F32)] + ex.out_shape,
        scratch_shapes=[pltpu.VMEM((NDEV, 3, D // NDEV, D), BF16), pltpu.VMEM((NDEV, 4, PG // NDEV, PG), BF16),
                        pltpu.SemaphoreType.DMA((2,))] + ex.scratch,
        compiler_params=_params(("arbitrary",), 48),
    )(dh1, z, z, s, q, ac, m, b_gate, ln_g, ln_b, pool_scale, g_mixw, g_pool, *qs)
    return res[:7], res[7:]


def _seq_bwd(dac, dm, dzg, z, w_dw, seq, qs):
    tp = z.shape[0]
    nt = tp // TM
    ex = _ChipExchange(qs)
    nq = no = ex.n

    def body(*refs):
        dac_l, dac_c, dac_r, dm_l, dm_c, dm_r, av_l, av, av_r, ag_l, ag, ag_r, dzg_ref, w_ref = refs[:14]
        dz_ref, acc_ref = refs[14 + nq:16 + nq]
        a3, d3, m3, da3, dp3, w3, dw3, da_sc, dp_sc = refs[16 + nq + no:25 + nq + no]
        ex.bind(refs[14:14 + nq], refs[16 + nq:16 + nq + no], refs[25 + nq + no:])
        i = pl.program_id(0)
        sub = lax.broadcasted_iota(jnp.int32, (NCB, 128), 0)

        @pl.when(i == 0)
        def _():
            ex.issue()
            dw3[...] = jnp.zeros_like(dw3)
            _tm_fill(w3, 0, 4, lambda r, l: w_ref[pl.ds(r, 8), l])

        _tm_fill_ext(a3, (av_l, ag_l), (av, ag), (av_r, ag_r), lambda vg, r, l: vg[0][r, l] * _sig(vg[1][r, l]), unroll=2)
        _tm_fill_ext(d3, dac_l, dac_c, dac_r, lambda ref, r, l: ref[r, l])
        _tm_fill_ext(m3, dm_l, dm_c, dm_r, lambda ref, r, l: ref[r, l])

        def conv(g, c):
            t0 = CONV_STEPS * g
            dcur = [_tm_at(d3, t0 + t + HALO) for t in range(CONV_STEPS)]
            accs = [None] * CONV_STEPS
            for k in range(CONV_K):
                wk = _tm_at(w3, k)
                prs = []
                for t in range(CONV_STEPS):
                    term = wk * _tm_at(d3, t0 + t + CONV_K - k)
                    accs[t] = term if accs[t] is None else accs[t] + term
                    prs.append(dcur[t] * _tm_at(a3, t0 + t + k + 1))
                while len(prs) > 1:
                    prs = [prs[j] + prs[j + 1] for j in range(0, len(prs) - 1, 2)] + prs[len(prs) - len(prs) % 2:]
                dw3[_tm_rows(k), :] += prs[0]
            s = dcur[0]
            for t in range(1, CONV_STEPS):
                s = s + dcur[t]
            dw3[_tm_rows(CONV_K), :] += s
            for t in range(CONV_STEPS):
                da3[_tm_rows(t0 + t), :] = accs[t]
            return c

        lax.fori_loop(0, TM // CONV_STEPS, conv, 0)

        for b in _edge_rows(seq, tp):
            e = lax.rem(b - i * TM + HALO + tp, tp)

            @pl.when(e < TME)
            def _():
                m3[_tm_rows(e), :] = _tm_at(m3, e) * _edge_gain(b, seq, tp, sub)

        inv = _by_group(sub, [1.0 / w for w in POOL_WINDOWS])

        def pool(g, c):
            for t in range(8):
                e = 8 * g + t + HALO
                sums = _nested_windows(lambda o: _tm_at(m3, e + o), [w // 2 + 1 - w for w in POOL_WINDOWS])
                dp3[_tm_rows(8 * g + t), :] = _by_group(sub, sums) * inv
            return c

        lax.fori_loop(0, TM // 8, pool, 0, unroll=2)

        _tm_read(da3, TM // 8, lambda r, l, tile: da_sc.__setitem__((r, l), tile))
        _tm_read(dp3, TM // 8, lambda r, l, tile: dp_sc.__setitem__((r, l), tile))
        sg = _sig(ag[...])
        da = da_sc[...]
        dz_ref[:, 0:D] = (da * sg).astype(BF16)
        dz_ref[:, D:2 * D] = (da * av[...] * (sg * (1.0 - sg))).astype(BF16)
        dz_ref[:, 2 * D:3 * D] = (dp_sc[...] - dm_c[...]).astype(BF16)
        dz_ref[:, 3 * D:] = dzg_ref[...]

        @pl.when(i == nt - 1)
        def _():
            _tm_read(dw3, 4, lambda r, l, tile: acc_ref.__setitem__((r, l), tile))
            ex.finish()

    tmaj = pltpu.VMEM((TM * NCB, 128), F32)
    text = pltpu.VMEM((TME * NCB, 128), F32)
    taps = pltpu.VMEM((32 * NCB, 128), F32)
    anys = pl.BlockSpec(memory_space=pl.ANY)
    res = pl.pallas_call(
        body, name="seq_bwd", grid=(nt,),
        in_specs=_halo_specs(0, nt) + _halo_specs(0, nt) + _halo_specs(0, nt) + _halo_specs(1, nt)
        + [pl.BlockSpec((TM, 2 * D), lambda i: (i, 0)), pl.BlockSpec((32, D), lambda i: (0, 0))] + [anys] * nq,
        out_specs=[pl.BlockSpec((TM, DIN), lambda i: (i, 0)), pl.BlockSpec((32, D), lambda i: (0, 0))] + [anys] * no,
        out_shape=[jax.ShapeDtypeStruct((tp, DIN), BF16), jax.ShapeDtypeStruct((32, D), F32)] + ex.out_shape,
        scratch_shapes=[text, text, text, tmaj, tmaj, taps, taps, pltpu.VMEM((TM, D), F32), pltpu.VMEM((TM, D), F32)]
        + ex.scratch,
        compiler_params=_params(("arbitrary",), 48),
    )(dac, dac, dac, dm, dm, dm, z, z, z, z, z, z, dzg, w_dw, *qs)
    return res[:2], res[2:]


def _in_bwd(dz, h0, dh1, g_mix, w_g, seq, qs):
    tp = h0.shape[0]
    tm = _pick(tp, TM_IO)
    nt = tp // tm
    ex = _ChipExchange(qs)
    nq = no = ex.n

    def body(*refs):
        dz_ref, h_ref, dh1_ref, g_ref, w_hbm = refs[:5]
        gx_ref, gmeta_ref, acc_ref = refs[5 + nq:8 + nq]
        w_vm, sems = refs[8 + nq + no:10 + nq + no]
        ex.bind(refs[5:5 + nq], refs[8 + nq:8 + nq + no], refs[10 + nq + no:])
        i = pl.program_id(0)

        @pl.when(i == 0)
        def _():
            ex.issue()
            acc_ref[...] = jnp.zeros_like(acc_ref)

        _load_once(i == 0, _win_pairs(w_hbm, w_vm), sems)

        du = _dot_nt(dz_ref[:, :DIN // 2], w_vm[0]) + _dot_nt(dz_ref[:, DIN // 2:], w_vm[1])
        h = h_ref[...]
        r = lax.rsqrt(jnp.mean(h * h, axis=-1, keepdims=True) + RMS_EPS)
        n0 = h * r
        acc_ref[0:1, :] += jnp.sum(du * n0, axis=0, keepdims=True)
        dn = du * g_ref[...]
        gx_ref[...] = dh1_ref[...] + r * (dn - n0 * jnp.mean(dn * n0, axis=-1, keepdims=True))

        @pl.when(i == nt - 1)
        def _():
            gmeta_ref[...] = gx_ref[pl.ds(tm - N_META, N_META), :]
            ex.finish()

    tile = pl.BlockSpec((tm, D), lambda i: (i, 0))
    anys = pl.BlockSpec(memory_space=pl.ANY)
    res = pl.pallas_call(
        body, name="in_bwd", grid=(nt,),
        in_specs=[pl.BlockSpec((tm, DIN), lambda i: (i, 0)), tile, tile, pl.BlockSpec((1, D), lambda i: (0, 0)), anys]
        + [anys] * nq,
        out_specs=[tile, pl.BlockSpec((N_META, D), lambda i: (0, 0)), pl.BlockSpec((8, D), lambda i: (0, 0))] + [anys] * no,
        out_shape=[jax.ShapeDtypeStruct((seq, D), F32), jax.ShapeDtypeStruct((N_META, D), F32),
                   jax.ShapeDtypeStruct((8, D), F32)] + ex.out_shape,
        scratch_shapes=[pltpu.VMEM((2, D, DIN // 2), BF16), pltpu.SemaphoreType.DMA((NDEV,))] + ex.scratch,
        compiler_params=_params(("arbitrary",), 58),
    )(dz, h0, dh1, g_mix, w_g, *qs)
    return res[:3], res[3:]


def _wgrad_in(u, dz):
    tp = u.shape[0]
    tm = _pick(tp, TM_WG)
    nt = tp // tm
    half = DIN // 2

    def body(u_ref, dz_ref, o_ref, acc):
        t = pl.program_id(1)

        @pl.when(t == 0)
        def _():
            acc[...] = jnp.zeros_like(acc)

        acc[...] += _dot_tn(u_ref[...], dz_ref[...])

        @pl.when(t == nt - 1)
        def _():
            for d in range(4):
                o_ref[d] = acc[:, INB * d:INB * (d + 1)].astype(BF16)

    return pl.pallas_call(
        body, name="wgrad_in", grid=(2, nt),
        in_specs=[pl.BlockSpec((tm, D), lambda h, t: (t, 0)), pl.BlockSpec((tm, half), lambda h, t: (t, h))],
        out_specs=pl.BlockSpec((4, D, INB), lambda h, t: (h, 0, 0), pipeline_mode=pl.Buffered(1)),
        out_shape=jax.ShapeDtypeStruct((NDEV, D, INB), BF16),
        scratch_shapes=[pltpu.VMEM((D, half), F32)],
        compiler_params=_params(("arbitrary", "arbitrary"), 52),
    )(u, dz)


def _wgrad_mix(s, dyc, q, dyp, merged, dh1, m, dm2):
    tp = s.shape[0]
    tm = _pick(tp, TM_WM)
    nt = tp // tm
    rb = D // NDEV

    def body(s_ref, dyc_ref, q_ref, dyp_ref, mg_ref, dh1_ref, m_ref, dm2_ref, o_ref, op_ref, acc, accp):
        t = pl.program_id(0)

        @pl.when(t == 0)
        def _():
            acc[...] = jnp.zeros_like(acc)
            accp[...] = jnp.zeros_like(accp)

        acc[0] += _dot_tn(s_ref[...], dyc_ref[...])
        acc[1] += _dot_tn(q_ref[...], dyp_ref[...])
        acc[2] += _dot_tn(mg_ref[...], dh1_ref[...].astype(BF16))
        for g in range(4):
            accp[g] += _dot_tn(m_ref[:, g * PG:(g + 1) * PG], dm2_ref[:, g * PG:(g + 1) * PG])

        @pl.when(t == nt - 1)
        def _():
            for d in range(NDEV):
                for k in range(3):
                    o_ref[d, k] = acc[k, rb * d:rb * (d + 1), :].astype(BF16)
                for g in range(4):
                    op_ref[d, g] = accp[g, 32 * d:32 * (d + 1), :].astype(BF16)

    tile = pl.BlockSpec((tm, D), lambda t: (t, 0))
    return pl.pallas_call(
        body, name="wgrad_mix", grid=(nt,),
        in_specs=[tile] * 8,
        out_specs=[pl.BlockSpec((NDEV, 3, rb, D), lambda t: (0, 0, 0, 0), pipeline_mode=pl.Buffered(1)),
                   pl.BlockSpec((NDEV, 4, 32, PG), lambda t: (0, 0, 0, 0), pipeline_mode=pl.Buffered(1))],
        out_shape=[jax.ShapeDtypeStruct((NDEV, 3, rb, D), BF16), jax.ShapeDtypeStruct((NDEV, 4, 32, PG), BF16)],
        scratch_shapes=[pltpu.VMEM((3, D, D), F32), pltpu.VMEM((4, PG, PG), F32)],
        compiler_params=_params(("arbitrary",), 56),
    )(s, dyc, q, dyp, merged, dh1, m, dm2)


def _wgrad_gu(v, dfg, dfu):
    tp = v.shape[0]
    tm = _pick(tp, TM_WG)
    nt = tp // tm

    def body(v_ref, dg_ref, du_ref, o_ref, acc):
        k, t = pl.program_id(0), pl.program_id(2)

        @pl.when(t == 0)
        def _():
            acc[...] = jnp.zeros_like(acc)

        @pl.when(k == 0)
        def _():
            acc[...] += _dot_tn(dg_ref[...], v_ref[...])

        @pl.when(k == 1)
        def _():
            acc[...] += _dot_tn(du_ref[...], v_ref[...])

        @pl.when(t == nt - 1)
        def _():
            for d in range(4):
                o_ref[d] = acc[FFB * d:FFB * (d + 1), :].astype(BF16)

    return pl.pallas_call(
        body, name="wgrad_gu", grid=(2, 2, nt),
        in_specs=[pl.BlockSpec((tm, D), lambda k, h, t: (t, 0)),
                  pl.BlockSpec((tm, FFC), lambda k, h, t: (t * (1 - k), h * (1 - k))),
                  pl.BlockSpec((tm, FFC), lambda k, h, t: (t * k, h * k))],
        out_specs=pl.BlockSpec((4, None, FFB, D), lambda k, h, t: (h, k, 0, 0), pipeline_mode=pl.Buffered(1)),
        out_shape=jax.ShapeDtypeStruct((NDEV, 2, FFB, D), BF16),
        scratch_shapes=[pltpu.VMEM((FFC, D), F32)],
        compiler_params=_params(("arbitrary",) * 3, 48),
    )(v, dfg, dfu)


def _wgrad_down(f, dh2):
    tp = f.shape[0]
    tm = _pick(tp, TM_WG)
    nt = tp // tm

    def body(f_ref, d_ref, o_ref, acc):
        t = pl.program_id(1)

        @pl.when(t == 0)
        def _():
            acc[...] = jnp.zeros_like(acc)

        acc[...] += _dot_tn(f_ref[...], d_ref[...].astype(BF16))

        @pl.when(t == nt - 1)
        def _():
            for d in range(4):
                o_ref[d] = acc[FFB * d:FFB * (d + 1), :].astype(BF16)

    return pl.pallas_call(
        body, name="wgrad_down", grid=(2, nt),
        in_specs=[pl.BlockSpec((tm, FFC), lambda h, t: (t, h)), pl.BlockSpec((tm, D), lambda h, t: (t, 0))],
        out_specs=pl.BlockSpec((4, FFB, D), lambda h, t: (h, 0, 0), pipeline_mode=pl.Buffered(1)),
        out_shape=jax.ShapeDtypeStruct((NDEV, FFB, D), BF16),
        scratch_shapes=[pltpu.VMEM((FFC, D), F32)],
        compiler_params=_params(("arbitrary", "arbitrary"), 48),
    )(f, dh2)


def kernel(x, meta_tokens, g_mix, w_in, b_gate, w_dw, b_dw, ln_g, ln_b, w_conv_out, w_pool, pool_scale, w_pool_out, w_o, g_ffn, w_ffn_gate, w_ffn_up, w_ffn_down, g_final, loss_target, m_meta_tokens, m_g_mix, m_w_in, m_b_gate, m_w_dw, m_b_dw, m_ln_g, m_ln_b, m_w_conv_out, m_w_pool, m_pool_scale, m_w_pool_out, m_w_o, m_g_ffn, m_w_ffn_gate, m_w_ffn_up, m_w_ffn_down, m_g_final, v_meta_tokens, v_g_mix, v_w_in, v_b_gate, v_w_dw, v_b_dw, v_ln_g, v_ln_b, v_w_conv_out, v_w_pool, v_pool_scale, v_w_pool_out, v_w_o, v_g_ffn, v_w_ffn_gate, v_w_ffn_up, v_w_ffn_down, v_g_final):
    seq = x.shape[1]
    tp = -(-(seq + 2 * HALO) // TM) * TM
    tm_in = _pick(tp, TM_IO)
    nx_last = seq - (tp // tm_in - 1) * tm_in
    assert 0 < nx_last <= tm_in - 2 * HALO and nx_last % 8 == 0 and 0 < seq - (tp // TM - 1) * TM

    whole = (Ellipsis,)
    ag_small = _Gather(
        [((48, D // NDEV), [(meta_tokens, pl.ds(0, N_META), whole), (w_dw, pl.ds(N_META, CONV_K), 0)])], [F32])
    ag_mix = _Gather([((3, D // NDEV, D), [(w_conv_out, 0, 0), (w_pool_out, 1, 0), (w_o, 2, 0)]),
                      ((4, PG // NDEV, PG), [(w_pool, whole, 0)])], [BF16, BF16])
    def tr(a):
        return jnp.swapaxes(a, 1, 2)

    ag_gu = _Gather([((2, FFB, D), [(tr(w_ffn_gate), 0, 0), (tr(w_ffn_up), 1, 0)])], [BF16])
    ag_dn = _Gather([((FFB, D), [(w_ffn_down, whole, 0)])], [BF16])

    mx, my = lax.axis_index("x"), lax.axis_index("y")
    order = jnp.stack([2 * mx + my, 2 * mx + 1 - my, 2 * (1 - mx) + my, 2 * (1 - mx) + 1 - my]).astype(jnp.int32)
    (h0, z, u, g_in), (g_mixw, g_pool), g_small = _fwd_in(x[0], g_mix, w_in, order, tp, ag_mix, ag_small)
    wdw_full = g_small.transpose(1, 0, 2).reshape(48, D)[N_META:]
    (ac, m), (w_gu,) = _seq_fwd(z, wdw_full, b_dw, seq, ag_gu)
    (h1, s, merged, q), (g_down,) = _mix_fwd(ac, m, z, h0, b_gate, ln_g, ln_b, pool_scale, g_mixw, g_pool, ag_dn)
    w_dn = g_down.reshape(2, FFC, D)
    fg, fu, v, f, dh2, head_acc = _ffn_fwd(h1, loss_target[0], g_ffn, g_final.reshape(1, D), w_gu, w_dn)

    dfg, dfu, dh1, ffn_acc = _ffn_bwd(dh2, fg, fu, h1, g_ffn, w_gu, w_dn)
    own_f, sib_f, q_f = _rs_pair("rs_pair_ffn", [_wgrad_gu(v, dfg, dfu), _wgrad_down(f, dh2)])
    (dac, dm, dzg, dyc, dyp, dm2, mix_acc), rel_dn = _mix_bwd(
        dh1, z, s, q, ac, m, b_gate, ln_g, ln_b, pool_scale, g_mixw, g_pool, q_f[1:])
    p_mix = _wgrad_mix(s, dyc, q, dyp, merged, dh1, m, dm2)
    own_m, sib_m, q_m = _rs_pair("rs_pair_mix", list(p_mix))
    (dz, seq_acc), rel_gm = _seq_bwd(dac, dm, dzg, z, wdw_full, seq, list(q_f[:1]) + list(q_m))
    rel_f, rel_m = [rel_gm[0], rel_dn[0]], rel_gm[1:]
    own_i, sib_i, q_i = _rs_pair("rs_pair_in", [_wgrad_in(u, dz)])
    (grad_x, g_meta, in_acc), rel_i = _in_bwd(dz, h0, dh1, g_mix, g_in, seq, q_i)
    small_g = jnp.concatenate([g_meta, seq_acc[:CONV_K], jnp.zeros((1, D), F32)], axis=0)
    p_small = small_g.reshape(48, NDEV, D // NDEV).transpose(1, 0, 2).astype(BF16)
    rep_g = jnp.concatenate([
        in_acc[0:1], mix_acc[0:1, :D], mix_acc[0:1, D:], seq_acc[CONV_K:CONV_K + 1], mix_acc[1:2, :D], mix_acc[1:2, D:],
        mix_acc[2:3, :D], ffn_acc[0:1], head_acc[1:2], head_acc[0:1], jnp.zeros((REP_ROWS - 10, D), F32)], axis=0)
    own_s, sib_s, rel_s, rep_all = _reduce_scatter([p_small], rep_g)
    owns = [own_i[0], own_s[0], own_m[0], own_m[1], own_f[0], own_f[1]]
    sibs = [sib_i[0], sib_s[0], sib_m[0], sib_m[1], sib_f[0], sib_f[1]]
    rels = [rel_i[0], rel_s[0], rel_m[0], rel_m[1], rel_f[0], rel_f[1]]

    def lead(a):
        return a.reshape(1, *a.shape)

    def stack4(a, lead_dims):
        return a.reshape(*lead_dims, 1, 4 * 32, PG)

    (r_in,) = _adamw_multi("adamw_in", lead(owns[0]), sibs[0][:, None], rels[0][:, None], [w_in], [m_w_in], [v_w_in], 4)
    r_meta, r_dw = _adamw_meta_dw(owns[1], sibs[1], rels[1], (meta_tokens, m_meta_tokens, v_meta_tokens),
                                  (w_dw, m_w_dw, v_w_dw))
    r_conv, r_pout, r_o = _adamw_multi("adamw_mix", owns[2], sibs[2], rels[2], [w_conv_out, w_pool_out, w_o],
                                       [m_w_conv_out, m_w_pool_out, m_w_o], [v_w_conv_out, v_w_pool_out, v_w_o], 1)
    (r_pool,) = _adamw_multi("adamw_pool", stack4(owns[3], ()), stack4(sibs[3], (1,)), stack4(rels[3], (3,)),
                             [w_pool.reshape(1, 128, PG)], [m_w_pool.reshape(1, 128, PG)], [v_w_pool.reshape(1, 128, PG)], 1)
    r_pool = tuple(a.reshape(w_pool.shape) for a in r_pool)
    r_gate, r_up = _adamw_multi("adamw_gu", owns[4], sibs[4], rels[4], [tr(w_ffn_gate), tr(w_ffn_up)],
                                [tr(m_w_ffn_gate), tr(m_w_ffn_up)], [tr(v_w_ffn_gate), tr(v_w_ffn_up)], 2)
    r_gate, r_up = tuple(tr(a) for a in r_gate), tuple(tr(a) for a in r_up)
    (r_down,) = _adamw_multi("adamw_down", lead(owns[5]), sibs[5][:, None], rels[5][:, None],
                             [w_ffn_down], [m_w_ffn_down], [v_w_ffn_down], 2)
    row = (1, D)
    loss, reps = _adamw_rep(
        rep_all,
        [g_mix, b_gate, b_dw, ln_g, ln_b, pool_scale, g_ffn, g_final.reshape(row)],
        [m_g_mix, m_b_gate, m_b_dw, m_ln_g, m_ln_b, m_pool_scale, m_g_ffn, m_g_final.reshape(row)],
        [v_g_mix, v_b_gate, v_b_dw, v_ln_g, v_ln_b, v_pool_scale, v_g_ffn, v_g_final.reshape(row)])
    r_gmix, r_bg, r_bdw, r_lg, r_lb, r_ps, r_gffn, r_gfin = reps
    r_gfin = tuple(a.reshape(D) for a in r_gfin)

    in_order = [r_meta, r_gmix, r_in, r_bg, r_dw, r_bdw, r_lg, r_lb, r_conv, r_pool, r_ps, r_pout, r_o, r_gffn,
                r_gate, r_up, r_down, r_gfin]
    return (loss.reshape(()), grad_x[None], *[r[0] for r in in_order], *[r[1] for r in in_order],
            *[r[2] for r in in_order], *[r[3] for r in in_order])
```

```python
import math

import jax
import jax.numpy as jnp
from jax import lax
from jax.experimental import pallas as pl
from jax.experimental.pallas import tpu as pltpu

F32, BF16 = jnp.float32, jnp.bfloat16
MESH_ID = pl.DeviceIdType.MESH
NDEV = 8

D = 1024
N_META = 16
CONV_K = 31
HALO = 16
POOL_WINDOWS = (2, 4, 8, 16)
PG = 256
DIN = 5 * D
DFF = 2816
FFB = DFF // NDEV
FFC = DFF // 2
INB = DIN // NDEV
RMS_EPS = 1e-6
LN_EPS = 1e-5
ADAM_LR, ADAM_B1, ADAM_B2, ADAM_EPS, ADAM_WD, ADAM_STEP = 0.001, 0.9, 0.999, 1e-08, 0.01, 10

TM = 384
TMS = 384
TM_IO = 704
TM_WG = 1408
TM_WM = 704
MIB = 2 ** 20


def _sig(x):
    return 0.5 * jnp.tanh(0.5 * x) + 0.5


def _dot(a, b):
    return jnp.dot(a, b, preferred_element_type=F32)


def _dot_nt(a, b):
    return lax.dot_general(a, b, (((1,), (1,)), ((), ())), preferred_element_type=F32)


def _dot_tn(a, b):
    return lax.dot_general(a, b, (((0,), (0,)), ((), ())), preferred_element_type=F32)


def _pick(tp, pref):
    return pref if tp % pref == 0 else TM


def _params(sem, vmem_mib):
    return pltpu.CompilerParams(dimension_semantics=sem, vmem_limit_bytes=vmem_mib * MIB)


def _load_once(first, pairs, sems):
    @pl.when(first)
    def _():
        cps = [pltpu.make_async_copy(s, d, sems.at[k]) for k, (s, d) in enumerate(pairs)]
        for cp in cps:
            cp.start()
        for cp in cps:
            cp.wait()


def _place():
    x, y, c = lax.axis_index("x"), lax.axis_index("y"), lax.axis_index("c")
    return x, y, c


class _Gather:
    def __init__(self, groups, dtypes):
        self.groups, self.dtypes, self.n = groups, dtypes, len(groups)
        self.arrays = [a for _, parts in groups for a, _, _ in parts]
        self.out_shape = [jax.ShapeDtypeStruct((NDEV, *s), dt) for (s, _), dt in zip(groups, dtypes)]
        self.scratch = [pltpu.VMEM(s, dt) for (s, _), dt in zip(groups, dtypes)] + [
            pltpu.SemaphoreType.DMA((7 * self.n,)), pltpu.SemaphoreType.DMA((7 * self.n,)),
            pltpu.SemaphoreType.DMA((self.n,))]

    def bind(self, ins, outs, scratch):
        self.ins, self.outs, self.stages = ins, outs, scratch[:self.n]
        self.send_sems, self.recv_sems, self.local_sems = scratch[self.n:]
        return self

    def _copy(self, w, k, block, to, src=None):
        dst = self.outs[w].at[4 * block[0] + 2 * block[1] + block[2]]
        return pltpu.make_async_remote_copy(
            src_ref=dst if src is None else src, dst_ref=dst,
            send_sem=self.send_sems.at[7 * w + k], recv_sem=self.recv_sems.at[7 * w + k],
            device_id=to, device_id_type=MESH_ID)

    def _first(self):
        x, y, c = _place()
        me, sibling = (x, y, c), (x, y, 1 - c)
        chips = [(1 - x, y), (x, 1 - y), (1 - x, 1 - y)]
        mine, first = [], []
        for w in range(self.n):
            mine.append(pltpu.make_async_copy(self.stages[w], self.outs[w].at[4 * x + 2 * y + c], self.local_sems.at[w]))
            first.append(self._copy(w, 0, me, sibling, src=self.stages[w]))
            first += [self._copy(w, 1 + j, me, (*chip, c), src=self.stages[w]) for j, chip in enumerate(chips)]
        return mine, first

    def _passed(self):
        x, y, c = _place()
        chips = [(1 - x, y), (x, 1 - y), (1 - x, 1 - y)]
        return [self._copy(w, 4 + j, (*chip, c), (x, y, 1 - c)) for w in range(self.n) for j, chip in enumerate(chips)]

    def issue(self):
        a = 0
        for w in range(self.n):
            shape, parts = self.groups[w]
            if sum(arr.size for arr, _, _ in parts) < math.prod(shape):
                self.stages[w][...] = jnp.zeros(shape, self.dtypes[w])
            for _, dst, src in parts:
                self.stages[w][dst] = self.ins[a][src].astype(self.dtypes[w])
                a += 1
        mine, first = self._first()
        for cp in mine + first:
            cp.start()

    def forward(self):
        x, y, c = _place()
        chips = [(1 - x, y), (x, 1 - y), (1 - x, 1 - y)]
        passed = self._passed()
        for w in range(self.n):
            for j, chip in enumerate(chips):
                self._copy(w, 1 + j, (*chip, c), (x, y, c)).wait_recv()
                passed[3 * w + j].start()

    def finish(self):
        x, y, c = _place()
        chips = [(1 - x, y), (x, 1 - y), (1 - x, 1 - y)]
        for w in range(self.n):
            self._copy(w, 0, (x, y, 1 - c), (x, y, c)).wait_recv()
            for j, chip in enumerate(chips):
                self._copy(w, 4 + j, (*chip, 1 - c), (x, y, c)).wait_recv()
        mine, first = self._first()
        for cp in first + self._passed():
            cp.wait_send()
        for cp in mine:
            cp.wait()


class _ChipExchange:
    def __init__(self, qs):
        self.n = len(qs)
        self.out_shape = [jax.ShapeDtypeStruct(q.shape, q.dtype) for q in qs]
        self.scratch = [pltpu.SemaphoreType.DMA((3 * self.n,)), pltpu.SemaphoreType.DMA((3 * self.n,))]

    def bind(self, qs, rels, scratch):
        self.qs, self.rels = qs, rels
        self.send_sems, self.recv_sems = scratch
        return self

    def _copies(self):
        x, y, c = _place()
        chips = [(1 - x, y), (x, 1 - y), (1 - x, 1 - y)]
        return [pltpu.make_async_remote_copy(
            src_ref=self.qs[w].at[j], dst_ref=self.rels[w].at[j],
            send_sem=self.send_sems.at[3 * w + j], recv_sem=self.recv_sems.at[3 * w + j],
            device_id=(*chips[j], c), device_id_type=MESH_ID) for w in range(self.n) for j in range(3)]

    def issue(self):
        for cp in self._copies():
            cp.start()

    def finish(self):
        cps = self._copies()
        for cp in cps:
            cp.wait_recv()
        for cp in cps:
            cp.wait_send()


def _reduce_scatter(parts, small):
    n = len(parts)
    blks = [p.shape[1:] for p in parts]

    def body(*refs):
        ps, small_ref = refs[:n], refs[n]
        o = n + 1
        owns, sibs, rels, small_out = refs[o:o + n], refs[o + n:o + 2 * n], refs[o + 2 * n:o + 3 * n], refs[o + 3 * n]
        o += 3 * n + 1
        pa, pb, qst = refs[o:o + n], refs[o + n:o + 2 * n], refs[o + 2 * n:o + 3 * n]
        s1_send, s1_recv, s2_send, s2_recv, sm_send, sm_recv, lsem = refs[o + 3 * n:]
        x, y, c = _place()
        me = 4 * x + 2 * y + c
        sibling = (x, y, 1 - c)
        chips = [(1 - x, y), (x, 1 - y), (1 - x, 1 - y)]
        all_chips = [(x, y)] + chips

        own_cps = []
        for w in range(n):
            cp = pltpu.make_async_copy(ps[w].at[me], owns[w], lsem.at[w])
            cp.start()
            own_cps.append(cp)
        sm_own = pltpu.make_async_copy(small_ref, small_out.at[me], lsem.at[n])
        sm_own.start()

        def small_copy(r):
            peer = ((x + (r >> 2)) % 2, (y + ((r >> 1) & 1)) % 2, (c + (r & 1)) % 2)
            return pltpu.make_async_remote_copy(
                src_ref=small_ref, dst_ref=small_out.at[me], send_sem=sm_send.at[r - 1], recv_sem=sm_recv.at[r - 1],
                device_id=peer, device_id_type=MESH_ID)

        sm_cps = [small_copy(r) for r in range(1, NDEV)]
        for cp in sm_cps:
            cp.start()

        def pair_copy(w, rel):
            cx, cy = all_chips[rel]
            return pltpu.make_async_remote_copy(
                src_ref=ps[w].at[4 * cx + 2 * cy + (1 - c)], dst_ref=sibs[w].at[rel],
                send_sem=s1_send.at[4 * w + rel], recv_sem=s1_recv.at[4 * w + rel],
                device_id=sibling, device_id_type=MESH_ID)

        def chip_copy(w, j):
            return pltpu.make_async_remote_copy(
                src_ref=qst[w].at[j], dst_ref=rels[w].at[j],
                send_sem=s2_send.at[3 * w + j], recv_sem=s2_recv.at[3 * w + j],
                device_id=(*chips[j], c), device_id_type=MESH_ID)

        pair_cps = [pair_copy(w, rel) for w in range(n) for rel in (1, 2, 3, 0)]
        for cp in pair_cps:
            cp.start()
        chip_cps = []
        for w in range(n):
            for j, (cx, cy) in enumerate(chips):
                pair_copy(w, 1 + j).wait_recv()
                la = pltpu.make_async_copy(ps[w].at[4 * cx + 2 * cy + c], pa[w], lsem.at[n + 1])
                lb = pltpu.make_async_copy(sibs[w].at[1 + j], pb[w], lsem.at[n + 2])
                la.start()
                lb.start()
                la.wait()
                lb.wait()
                qst[w][j] = (pa[w][...].astype(F32) + pb[w][...].astype(F32)).astype(BF16)
                cp = chip_copy(w, j)
                cp.start()
                chip_cps.append(cp)
        for w in range(n):
            pair_copy(w, 0).wait_recv()
            for j in range(3):
                chip_copy(w, j).wait_recv()
        for cp in sm_cps:
            cp.wait_recv()
        for cp in pair_cps + chip_cps + sm_cps:
            cp.wait_send()
        for cp in own_cps:
            cp.wait()
        sm_own.wait()

    any_spec = pl.BlockSpec(memory_space=pl.ANY)
    outs = pl.pallas_call(
        body, name="rs_grads",
        out_shape=[jax.ShapeDtypeStruct(b, BF16) for b in blks]
        + [jax.ShapeDtypeStruct((4, *b), BF16) for b in blks]
        + [jax.ShapeDtypeStruct((3, *b), BF16) for b in blks]
        + [jax.ShapeDtypeStruct((NDEV, *small.shape), F32)],
        in_specs=[any_spec] * (n + 1),
        out_specs=[any_spec] * (3 * n + 1),
        scratch_shapes=[pltpu.VMEM(b, BF16) for b in blks] + [pltpu.VMEM(b, BF16) for b in blks]
        + [pltpu.VMEM((3, *b), BF16) for b in blks]
        + [pltpu.SemaphoreType.DMA((4 * n,)), pltpu.SemaphoreType.DMA((4 * n,)),
           pltpu.SemaphoreType.DMA((3 * n,)), pltpu.SemaphoreType.DMA((3 * n,)),
           pltpu.SemaphoreType.DMA((NDEV - 1,)), pltpu.SemaphoreType.DMA((NDEV - 1,)),
           pltpu.SemaphoreType.DMA((n + 3,))],
        compiler_params=pltpu.CompilerParams(vmem_limit_bytes=40 * MIB),
    )(*parts, small)
    return outs[:n], outs[n:2 * n], outs[2 * n:3 * n], outs[3 * n]


class _PairSum:
    def __init__(self, parts, keep_q=True):
        self.n = n = len(parts)
        self.keep_q = keep_q
        blks = [p.shape[1:] for p in parts]
        self.out_shape = [jax.ShapeDtypeStruct(b, BF16) for b in blks] + [jax.ShapeDtypeStruct((1, *b), BF16) for b in blks]
        if keep_q:
            self.out_shape += [jax.ShapeDtypeStruct((3, *b), BF16) for b in blks]
        self.scratch = [pltpu.VMEM((3, *b), BF16) for b in blks] * 3 + [
            pltpu.SemaphoreType.DMA((4 * n,)), pltpu.SemaphoreType.DMA((4 * n,)), pltpu.SemaphoreType.DMA((5 * n,))]

    def bind(self, ps, outs, scratch):
        n = self.n
        self.ps, self.owns, self.sibs, self.qs = ps, outs[:n], outs[n:2 * n], outs[2 * n:]
        self.pa, self.pb, self.qst = scratch[:n], scratch[n:2 * n], scratch[2 * n:3 * n]
        self.s_send, self.s_recv, self.lsem = scratch[3 * n:]
        return self

    def _local(self, with_q):
        n = self.n
        x, y, c = _place()
        chips = [(1 - x, y), (x, 1 - y), (1 - x, 1 - y)]
        own = [pltpu.make_async_copy(self.ps[w].at[4 * x + 2 * y + c], self.owns[w], self.lsem.at[w]) for w in range(n)]
        mine = [[pltpu.make_async_copy(self.ps[w].at[4 * cx + 2 * cy + c], self.pa[w].at[j], self.lsem.at[2 * n + 3 * w + j])
                 for j, (cx, cy) in enumerate(chips)] for w in range(n)]
        outq = [pltpu.make_async_copy(self.qst[w], self.qs[w], self.lsem.at[n + w]) for w in range(n)] if with_q else []
        return own, mine, outq

    def _pair(self, w, rel):
        x, y, c = _place()
        cx, cy = [(x, y), (1 - x, y), (x, 1 - y), (1 - x, 1 - y)][rel]
        return pltpu.make_async_remote_copy(
            src_ref=self.ps[w].at[4 * cx + 2 * cy + (1 - c)],
            dst_ref=self.sibs[w].at[0] if rel == 0 else self.pb[w].at[rel - 1],
            send_sem=self.s_send.at[4 * w + rel], recv_sem=self.s_recv.at[4 * w + rel],
            device_id=(x, y, 1 - c), device_id_type=MESH_ID)

    def issue(self):
        own, mine, _ = self._local(False)
        for cp in own + [cp for row in mine for cp in row]:
            cp.start()
        for w in range(self.n):
            for rel in (1, 2, 3, 0):
                self._pair(w, rel).start()

    def finish(self):
        own, mine, outq = self._local(self.keep_q)
        for w in range(self.n):
            for j in range(3):
                self._pair(w, 1 + j).wait_recv()
                mine[w][j].wait()
                self.qst[w][j] = (self.pa[w][j].astype(F32) + self.pb[w][j].astype(F32)).astype(BF16)
            if self.keep_q:
                outq[w].start()
        for w in range(self.n):
            self._pair(w, 0).wait_recv()
        for w in range(self.n):
            for rel in range(4):
                self._pair(w, rel).wait_send()
        for cp in own + outq:
            cp.wait()

    def results(self, outs):
        n = self.n
        return outs[:n], outs[n:2 * n], outs[2 * n:3 * n]


def _rs_pair(name, parts):
    ps = _PairSum(parts)
    n = ps.n

    def body(*refs):
        ps.bind(refs[:n], refs[n:4 * n], refs[4 * n:])
        ps.issue()
        ps.finish()

    any_spec = pl.BlockSpec(memory_space=pl.ANY)
    outs = pl.pallas_call(
        body, name=name, out_shape=ps.out_shape,
        in_specs=[any_spec] * n, out_specs=[any_spec] * (3 * n), scratch_shapes=ps.scratch,
        compiler_params=pltpu.CompilerParams(vmem_limit_bytes=48 * MIB),
    )(*parts)
    return ps.results(outs)


def _adamw_math(g, w, m, v):
    m = ADAM_B1 * m + (1.0 - ADAM_B1) * g
    v = ADAM_B2 * v + (1.0 - ADAM_B2) * (g * g)
    m_hat = m / (1.0 - ADAM_B1 ** ADAM_STEP)
    v_hat = v / (1.0 - ADAM_B2 ** ADAM_STEP)
    delta = -ADAM_LR * (m_hat / (jnp.sqrt(v_hat) + ADAM_EPS) + ADAM_WD * w)
    return delta, m, v


def _adamw_multi(name, own, sib, rel, ws, ms, vs, row_grid):
    k_n, r_n, c_n = own.shape
    rbk = r_n // row_grid

    def body(*refs):
        own_ref, sib_ref, r0_ref, r1_ref, r2_ref = refs[:5]
        w_refs, m_refs, v_refs = refs[5:5 + k_n], refs[5 + k_n:5 + 2 * k_n], refs[5 + 2 * k_n:5 + 3 * k_n]
        outs = refs[5 + 3 * k_n:]
        for k in range(k_n):
            g = own_ref[k].astype(F32) + sib_ref[k].astype(F32)
            g = g + r0_ref[k].astype(F32)
            g = g + r1_ref[k].astype(F32)
            g = g + r2_ref[k].astype(F32)
            delta, mm, vv = _adamw_math(g, w_refs[k][0], m_refs[k][0], v_refs[k][0])
            outs[4 * k][0] = g
            outs[4 * k + 1][0] = delta
            outs[4 * k + 2][0] = mm
            outs[4 * k + 3][0] = vv

    def lead(j):
        return pl.BlockSpec((None, k_n, rbk, c_n), lambda g: (j, 0, g, 0))

    wspec = pl.BlockSpec((1, rbk, c_n), lambda g: (0, g, 0))
    shp = jax.ShapeDtypeStruct((1, r_n, c_n), F32)
    res = pl.pallas_call(
        body, name=name, grid=(row_grid,),
        in_specs=[pl.BlockSpec((k_n, rbk, c_n), lambda g: (0, g, 0)), lead(0), lead(0), lead(1), lead(2)] + [wspec] * (3 * k_n),
        out_specs=[wspec] * (4 * k_n), out_shape=[shp] * (4 * k_n),
        compiler_params=_params(("arbitrary",), 40),
    )(own, sib, rel, rel, rel, *ws, *ms, *vs)
    return [tuple(res[4 * k:4 * k + 4]) for k in range(k_n)]


def _adamw_meta_dw(own, sib, rel, meta, dw):
    def body(own_ref, sib_ref, rel_ref, wm, mm, vm, wd, md, vd, *outs):
        def gsum(rows):
            g = own_ref[rows, :].astype(F32) + sib_ref[0, rows, :].astype(F32)
            for j in range(3):
                g = g + rel_ref[j, rows, :].astype(F32)
            return g

        g = gsum(pl.ds(0, N_META))
        delta, m2, v2 = _adamw_math(g, wm[...], mm[...], vm[...])
        for o, val in zip(outs[:4], (g, delta, m2, v2)):
            o[...] = val
        g = gsum(pl.ds(N_META, CONV_K))
        delta, m2, v2 = _adamw_math(g, wd[0], md[0], vd[0])
        for o, val in zip(outs[4:], (g, delta, m2, v2)):
            o[0] = val

    s_meta = jax.ShapeDtypeStruct(meta[0].shape, F32)
    s_dw = jax.ShapeDtypeStruct(dw[0].shape, F32)
    res = pl.pallas_call(body, name="adamw_meta_dw", out_shape=[s_meta] * 4 + [s_dw] * 4)(own, sib, rel, *meta, *dw)
    return tuple(res[:4]), tuple(res[4:])


REP_ROWS = 16


def _adamw_rep(gathered, ws, ms, vs):
    rows = [(0, 1), (1, 2), (3, 1), (4, 1), (5, 1), (6, 1), (7, 1), (8, 1)]

    def body(g_ref, *refs):
        w_refs, m_refs, v_refs = refs[:8], refs[8:16], refs[16:24]
        loss_ref, outs, acc = refs[24], refs[25:57], refs[57]
        g = g_ref[0]
        for d in range(1, NDEV):
            g = g + g_ref[d]
        acc[...] = g
        loss_ref[...] = (0.5 / D) * jnp.sum(acc[pl.ds(9, 1), :], axis=1, keepdims=True)
        for p, (r0, nr) in enumerate(rows):
            for h in range(nr):
                cols = pl.ds(h * D, D)
                gp = acc[pl.ds(r0 + h, 1), :]
                delta, mm, vv = _adamw_math(gp, w_refs[p][:, cols], m_refs[p][:, cols], v_refs[p][:, cols])
                for o, val in zip(outs[4 * p:4 * p + 4], (gp, delta, mm, vv)):
                    o[:, cols] = val

    shapes = [jax.ShapeDtypeStruct(w.shape, F32) for w in ws]
    res = pl.pallas_call(
        body, name="adamw_rep",
        out_shape=[jax.ShapeDtypeStruct((1, 1), F32)] + [s for s in shapes for _ in range(4)],
        scratch_shapes=[pltpu.VMEM((REP_ROWS, D), F32)],
    )(gathered, *ws, *ms, *vs)
    return res[0], [tuple(res[1 + 4 * p:5 + 4 * p]) for p in range(8)]


def _load_ffn(i, j, wgu_hbm, wgu, wdn_hbm, wdn, sems):
    half = NDEV // 2

    def copies(ch):
        pairs = [(wgu_hbm.at[half * ch + d, g], wgu.at[g, ch, pl.ds(FFB * d, FFB), :]) for g in range(2) for d in range(half)]
        pairs.append((wdn_hbm.at[ch], wdn.at[ch]))
        return [pltpu.make_async_copy(s, t, sems.at[(2 * half + 1) * ch + k]) for k, (s, t) in enumerate(pairs)]

    @pl.when((i == 0) & (j == 0))
    def _():
        for cp in copies(0) + copies(1):
            cp.start()

    for ch in range(2):
        @pl.when((i == 0) & (j == ch))
        def _():
            for cp in copies(ch):
                cp.wait()


def _win_pairs(w_hbm, w_vm):
    return [(w_hbm.at[q], w_vm.at[q // 2, :, pl.ds(2 * INB * (q % 2), 2 * INB)]) for q in range(4)]


def _whole(a):
    nd = a.ndim
    return pl.BlockSpec(a.shape, lambda *g: (0,) * nd)


CHIPW = 2 * INB
PHASE_CHIP = (1, 0, 2)
assert PHASE_CHIP[2] == 2


class _GatherIn:
    scratch = [pltpu.VMEM((D, INB), BF16), pltpu.SemaphoreType.DMA((7,)), pltpu.SemaphoreType.DMA((7,)),
               pltpu.SemaphoreType.DMA((1,))]

    def bind(self, w_ref, w_vm, scratch):
        self.w_ref, self.w_vm = w_ref, w_vm
        self.stage, self.send_sems, self.recv_sems, self.local_sem = scratch
        return self

    def _win(self, chip, core):
        return self.w_vm.at[2 * chip[0] + chip[1], core]

    def _copy(self, k, chip, core, to, src=None):
        dst = self._win(chip, core)
        return pltpu.make_async_remote_copy(
            src_ref=dst if src is None else src, dst_ref=dst, send_sem=self.send_sems.at[k],
            recv_sem=self.recv_sems.at[k], device_id=to, device_id_type=MESH_ID)

    def _mine(self, cs):
        x, y, _ = _place()
        return pltpu.make_async_copy(self.stage, self._win((x, y), cs), self.local_sem.at[0])

    def issue(self, cs):
        x, y, _ = _place()
        chips = [(1 - x, y), (x, 1 - y), (1 - x, 1 - y)]
        self.stage[...] = self.w_ref[0].astype(BF16)
        self._mine(cs).start()
        self._copy(0, (x, y), cs, (x, y, 1 - cs), src=self.stage).start()
        for j in PHASE_CHIP[:2]:
            self._copy(1 + j, (x, y), cs, (*chips[j], cs), src=self.stage).start()

    def wait_chip(self, phase, cs):
        x, y, _ = _place()
        chips = [(1 - x, y), (x, 1 - y), (1 - x, 1 - y)]
        if phase == 0:
            self._mine(cs).wait()
            self._copy(0, (x, y), 1 - cs, (x, y, cs)).wait_recv()
            return
        j = PHASE_CHIP[phase - 1]
        self._copy(1 + j, chips[j], cs, (x, y, cs)).wait_recv()
        self._copy(4 + j, chips[j], cs, (x, y, 1 - cs)).start()
        if phase == 1:
            self._copy(3, (x, y), cs, (*chips[2], cs), src=self.stage).start()
        self._copy(4 + j, chips[j], 1 - cs, (x, y, cs)).wait_recv()

    def finish(self, cs):
        x, y, _ = _place()
        for k in range(7):
            self._copy(k, (x, y), cs, (x, y, cs), src=self.stage).wait_send()


def _fwd_in(x2, g_mix, w_in, order, tp, ag, ags):
    tm = _pick(tp, TM_IO)
    nt = tp // tm
    nx_last = x2.shape[0] - (nt - 1) * tm
    na, ng, ns = len(ag.arrays), ag.n, len(ags.arrays)
    gin = _GatherIn()

    def body(order_ref, *refs):
        x_ref, g_ref, w_ref = refs[:3]
        o = 3 + na + ns
        h_ref, z_ref, u_ref, wout_ref = refs[o:o + 4]
        s = o + 4 + ng + 1
        w_vm, u_all, osem, sm_vm = refs[s:s + 4]
        gin.bind(w_ref, w_vm, refs[s + 4:s + 8])
        ag.bind(refs[3:3 + na], refs[o + 4:o + 4 + ng], refs[s + 8:s + 8 + len(ag.scratch)])
        ags.bind(refs[3 + na:3 + na + ns], refs[o + 4 + ng:o + 5 + ng], refs[s + 8 + len(ag.scratch):])
        ph, i = pl.program_id(0), pl.program_id(1)
        core = lax.axis_index("c")
        first = (ph == 0) & (i == 0)
        last = (ph == 3) & (i == nt - 1)
        @pl.when(first)
        def _():
            ags.issue()

        for cs in range(2):
            @pl.when(first & (core == cs))
            def _():
                gin.issue(cs)

        @pl.when((ph == 0) & (i == max(nt - 2, 0)))
        def _():
            ags.forward()

        for cs in range(2):
            for p in range(4):
                @pl.when((ph == p) & (i == 0) & (core == cs))
                def _():
                    gin.wait_chip(p, cs)

        @pl.when((ph == 2) & (i == 0))
        def _():
            ag.issue()

        out_copies = [pltpu.make_async_copy(w_vm.at[k, c], wout_ref.at[k, :, pl.ds(INB * c, INB)], osem.at[2 * k + c])
                      for k in range(4) for c in range(2)]

        @pl.when((ph == 3) & (i == 0))
        def _():
            for cp in out_copies:
                cp.start()

        @pl.when((ph == 0) & (i < nt - 1))
        def _():
            h_ref[...] = x_ref[...]

        @pl.when((ph == 0) & (i == nt - 1))
        def _():
            ags.finish()
            cp = pltpu.make_async_copy(ags.outs[0], sm_vm, osem.at[8])
            cp.start()
            h_ref[pl.ds(0, nx_last), :] = x_ref[pl.ds(0, nx_last), :]
            h_ref[pl.ds(nx_last, tm - nx_last - N_META), :] = jnp.zeros((tm - nx_last - N_META, D), F32)
            cp.wait()
            for d in range(NDEV):
                h_ref[pl.ds(tm - N_META, N_META), pl.ds(128 * d, 128)] = sm_vm[d, pl.ds(0, N_META), :]

        @pl.when(ph == 0)
        def _():
            xv = h_ref[...]
            r = lax.rsqrt(jnp.mean(xv * xv, axis=-1, keepdims=True) + RMS_EPS)
            u = (xv * r * g_ref[...]).astype(BF16)
            u_ref[...] = u
            u_all[i] = u

        for c in range(2):
            z_ref[:, INB * c:INB * (c + 1)] = _dot(u_all[i], w_vm[order_ref[ph], c])

        @pl.when(last)
        def _():
            ag.forward()
            ag.finish()
            for cp in out_copies:
                cp.wait()

        for cs in range(2):
            @pl.when(last & (core == cs))
            def _():
                gin.finish(cs)

    def rows(ph, i, order):
        return (jnp.where(ph == 0, i, nt - 1), 0)

    tile = pl.BlockSpec((tm, D), rows)
    anys = pl.BlockSpec(memory_space=pl.ANY)
    res = pl.pallas_call(
        body, name="fwd_in",
        grid_spec=pltpu.PrefetchScalarGridSpec(
            num_scalar_prefetch=1, grid=(4, nt),
            in_specs=[tile, pl.BlockSpec((1, D), lambda ph, i, order: (0, 0)), _whole(w_in)]
            + [_whole(a) for a in ag.arrays + ags.arrays],
            out_specs=[tile, pl.BlockSpec((tm, CHIPW), lambda ph, i, order: (i, order[ph])), tile, anys] + [anys] * (ng + 1),
            scratch_shapes=[pltpu.VMEM((4, 2, D, INB), BF16), pltpu.VMEM((nt, tm, D), BF16), pltpu.SemaphoreType.DMA((9,)),
                            pltpu.VMEM(ags.out_shape[0].shape, F32)] + gin.scratch + ag.scratch + ags.scratch),
        out_shape=[jax.ShapeDtypeStruct((tp, D), F32), jax.ShapeDtypeStruct((tp, DIN), F32),
                   jax.ShapeDtypeStruct((tp, D), BF16), jax.ShapeDtypeStruct((4, D, CHIPW), BF16)]
        + ag.out_shape + ags.out_shape,
        compiler_params=_params(("arbitrary", "arbitrary"), 58),
    )(order, x2, g_mix, w_in, *ag.arrays, *ags.arrays)
    return res[:4], res[4:4 + ng], res[4 + ng]


def _halo_specs(col, nt, width=D):
    r = TM // HALO
    nb = nt * r
    return [pl.BlockSpec((HALO, width), lambda i: ((i * r + nb - 1) % nb, col)),
            pl.BlockSpec((TM, width), lambda i: (i, col)),
            pl.BlockSpec((HALO, width), lambda i: (((i + 1) * r) % nb, col))]


NCB = D // 128
TME = TM + 2 * HALO
CONV_STEPS = 16
assert TM % CONV_STEPS == 0


def _tm_fill(dst, time0, groups, tile_fn, unroll=1):
    def body(g, c):
        for j in range(NCB):
            dst[pl.ds((time0 + 8 * g) * NCB + j, 8, stride=NCB), :] = tile_fn(pl.multiple_of(8 * g, 8), pl.ds(128 * j, 128))
        return c

    lax.fori_loop(0, groups, body, 0, unroll=unroll)


def _tm_fill_ext(dst, left, cur, right, fn, unroll=1):
    _tm_fill(dst, 0, HALO // 8, lambda r, l: fn(left, pl.ds(r, 8), l), unroll)
    _tm_fill(dst, HALO, TM // 8, lambda r, l: fn(cur, pl.ds(r, 8), l), unroll)
    _tm_fill(dst, HALO + TM, HALO // 8, lambda r, l: fn(right, pl.ds(r, 8), l), unroll)


def _tm_read(src, groups, store_fn):
    def body(g, c):
        for j in range(NCB):
            store_fn(pl.ds(pl.multiple_of(8 * g, 8), 8), pl.ds(128 * j, 128), src[pl.ds(8 * g * NCB + j, 8, stride=NCB), :])
        return c

    lax.fori_loop(0, groups, body, 0, unroll=2)


def _tm_rows(t):
    return pl.ds(t * NCB if isinstance(t, int) else pl.multiple_of(t * NCB, NCB), NCB)


def _tm_at(ref, t):
    return ref[_tm_rows(t), :]


def _by_group(sub, vals):
    return jnp.where(sub < 2, vals[0], jnp.where(sub < 4, vals[1], jnp.where(sub < 6, vals[2], vals[3])))


def _pool_cnt(b, seq, tp, sub):
    b = jnp.where(b < 0, b + tp, b)
    b = jnp.where(b >= tp, b - tp, b)
    t = jnp.where(b < seq, b + N_META, b - (tp - N_META))
    cnts = []
    for win in POOL_WINDOWS:
        left = win // 2
        lo = jnp.maximum(t - left, 0)
        hi = jnp.minimum(t + win - left, seq + N_META)
        cnts.append(jnp.maximum(hi - lo, 1).astype(F32))
    return _by_group(sub, cnts)


def _edge_rows(seq, tp):
    reach = max(POOL_WINDOWS) // 2
    return [tp - N_META + t for t in range(reach)] + [seq - reach + 1 + t for t in range(reach - 1)]


def _edge_gain(b, seq, tp, sub):
    return _by_group(sub, [float(w) for w in POOL_WINDOWS]) / _pool_cnt(b, seq, tp, sub)


def _nested_windows(at, lo_offs):
    sums, s, have = [], None, set()
    for g, win in enumerate(POOL_WINDOWS):
        for o in range(lo_offs[g], lo_offs[g] + win):
            if o not in have:
                have.add(o)
                s = at(o) if s is None else s + at(o)
        sums.append(s)
    return sums


def _seq_fwd(z, w_dw, b_dw, seq, gat):
    tp = z.shape[0]
    nt = tp // TM
    na, ng = len(gat.arrays), gat.n

    def body(*refs):
        av_l, av, av_r, ag_l, ag, ag_r, p_l, p, p_r, w_ref, b_ref = refs[:11]
        ac_ref, m_ref = refs[11 + na:13 + na]
        a3, p3, o3, m3, w3, b3, m2d = refs[13 + na + ng:20 + na + ng]
        gat.bind(refs[11:11 + na], refs[13 + na:13 + na + ng], refs[20 + na + ng:])
        i = pl.program_id(0)
        sub = lax.broadcasted_iota(jnp.int32, (NCB, 128), 0)

        @pl.when(i == 0)
        def _():
            gat.issue()
            _tm_fill(w3, 0, 4, lambda r, l: w_ref[pl.ds(r, 8), l])
            for j in range(NCB):
                b3[pl.ds(j, 1), :] = b_ref[:, pl.ds(128 * j, 128)]

        @pl.when(i == max(nt - 2, 0))
        def _():
            gat.forward()

        _tm_fill_ext(a3, (av_l, ag_l), (av, ag), (av_r, ag_r), lambda vg, r, l: vg[0][r, l] * _sig(vg[1][r, l]), unroll=2)
        _tm_fill_ext(p3, p_l, p, p_r, lambda ref, r, l: ref[r, l])

        def conv(g, c):
            accs = [b3[...]] * 16
            for k in range(CONV_K):
                wk = _tm_at(w3, k)
                for t in range(16):
                    accs[t] = accs[t] + wk * _tm_at(a3, 16 * g + t + k + 1)
            for t in range(16):
                o3[_tm_rows(16 * g + t), :] = accs[t]
            return c

        lax.fori_loop(0, TM // 16, conv, 0)
        _tm_read(o3, TM // 8, lambda r, l, tile: ac_ref.__setitem__((r, l), tile))

        inv = _by_group(sub, [1.0 / w for w in POOL_WINDOWS])

        def pool(g, c):
            for t in range(8):
                e = 8 * g + t + HALO
                sums = _nested_windows(lambda o: _tm_at(p3, e + o), [-(w // 2) for w in POOL_WINDOWS])
                m3[_tm_rows(8 * g + t), :] = _by_group(sub, sums) * inv - _tm_at(p3, e)
            return c

        lax.fori_loop(0, TM // 8, pool, 0)
        for b in _edge_rows(seq, tp):
            r = b - i * TM

            @pl.when((r >= 0) & (r < TM))
            def _():
                pv = _tm_at(p3, r + HALO)
                m3[_tm_rows(r), :] = (_tm_at(m3, r) + pv) * _edge_gain(b, seq, tp, sub) - pv

        _tm_read(m3, TM // 8, lambda r, l, tile: m2d.__setitem__((r, l), tile))
        m_ref[...] = m2d[...].astype(BF16)

        @pl.when(i == nt - 1)
        def _():
            gat.finish()

    tmaj = pltpu.VMEM((TM * NCB, 128), F32)
    text = pltpu.VMEM((TME * NCB, 128), F32)
    res = pl.pallas_call(
        body, name="seq_fwd", grid=(nt,),
        in_specs=_halo_specs(0, nt) + _halo_specs(1, nt) + _halo_specs(2, nt)
        + [pl.BlockSpec((32, D), lambda i: (0, 0)), pl.BlockSpec((1, D), lambda i: (0, 0))] + [_whole(a) for a in gat.arrays],
        out_specs=[pl.BlockSpec((TM, D), lambda i: (i, 0))] * 2 + [pl.BlockSpec(memory_space=pl.ANY)] * ng,
        out_shape=[jax.ShapeDtypeStruct((tp, D), F32), jax.ShapeDtypeStruct((tp, D), BF16)] + gat.out_shape,
        scratch_shapes=[text, text, tmaj, tmaj, pltpu.VMEM((32 * NCB, 128), F32), pltpu.VMEM((NCB, 128), F32),
                        pltpu.VMEM((TM, D), F32)] + gat.scratch,
        compiler_params=_params(("arbitrary",), 52),
    )(z, z, z, z, z, z, z, z, z, w_dw, b_dw, *gat.arrays)
    return res[:2], res[2:]


def _ln_stats(ac):
    mu = jnp.mean(ac, axis=-1, keepdims=True)
    xc = ac - mu
    rl = lax.rsqrt(jnp.mean(xc * xc, axis=-1, keepdims=True) + LN_EPS)
    return xc * rl, rl


def _pool_mix(m, wp_ref):
    return jnp.concatenate(
        [_dot(m[:, g * PG:(g + 1) * PG], wp_ref[:, g].reshape(PG, PG)) for g in range(4)], axis=1)


def _mix_fwd(ac, m, z, h0, b_gate, ln_g, ln_b, pool_scale, g_mixw, g_pool, gat):
    tp = h0.shape[0]
    tms = TM
    nt = tp // tms
    na, ng = len(gat.arrays), gat.n

    def body(*refs):
        ac_ref, m_ref, zga, zgb, h_ref, bg_ref, lg_ref, lb_ref, ps_ref, wm_hbm, wp_hbm = refs[:11]
        h1_ref, s_ref, mg_ref, q_ref = refs[11 + na:15 + na]
        wm, wp, sems = refs[15 + na + ng:18 + na + ng]
        gat.bind(refs[11:11 + na], refs[15 + na:15 + na + ng], refs[18 + na + ng:])
        i = pl.program_id(0)

        @pl.when(i == 0)
        def _():
            gat.issue()

        @pl.when(i == max(nt - 4, 0))
        def _():
            gat.forward()

        @pl.when(i == nt - 1)
        def _():
            gat.finish()

        _load_once(i == 0, [(wm_hbm, wm), (wp_hbm, wp)], sems)
        n, _ = _ln_stats(ac_ref[...])
        l = n * lg_ref[...] + lb_ref[...]
        s = (l * _sig(l)).astype(BF16)
        s_ref[...] = s
        yc = _dot(s, wm[:, 0].reshape(D, D))
        q = (_pool_mix(m_ref[...], wp) * ps_ref[...]).astype(BF16)
        q_ref[...] = q
        yp = _dot(q, wm[:, 1].reshape(D, D))
        ga = _sig(zga[...] + bg_ref[:, :D])
        gb = _sig(zgb[...] + bg_ref[:, D:])
        merged = (ga * yc + gb * yp).astype(BF16)
        mg_ref[...] = merged
        h1_ref[...] = h_ref[...] + _dot(merged, wm[:, 2].reshape(D, D))

    def tile(col=0):
        return pl.BlockSpec((tms, D), lambda i: (i, col))

    def vec(w):
        return pl.BlockSpec((1, w), lambda i: (0, 0))

    anys = pl.BlockSpec(memory_space=pl.ANY)
    f32o, b16o = jax.ShapeDtypeStruct((tp, D), F32), jax.ShapeDtypeStruct((tp, D), BF16)
    res = pl.pallas_call(
        body, name="mix_fwd", grid=(nt,),
        in_specs=[tile(), tile(), tile(3), tile(4), tile(), vec(2 * D), vec(D), vec(D), vec(D), anys, anys]
        + [_whole(a) for a in gat.arrays],
        out_specs=[tile()] * 4 + [anys] * ng,
        out_shape=[f32o, b16o, b16o, b16o] + gat.out_shape,
        scratch_shapes=[pltpu.VMEM((NDEV, 3, D // NDEV, D), BF16), pltpu.VMEM((NDEV, 4, PG // NDEV, PG), BF16),
                        pltpu.SemaphoreType.DMA((2,))] + gat.scratch,
        compiler_params=_params(("arbitrary",), 52),
    )(ac, m, z, z, h0, b_gate, ln_g, ln_b, pool_scale, g_mixw, g_pool, *gat.arrays)
    return res[:4], res[4:]


def _ffn_fwd(h1, tgt, g_ffn, g_final, w_gu, w_dn):
    tp = h1.shape[0]
    nt = tp // TM
    nx_last = tgt.shape[0] - (nt - 1) * TM

    def body(h_ref, t_ref, gf_ref, gl_ref, wgu_hbm, wdn_hbm,
             fg_ref, fu_ref, v_ref, f_ref, dh2_ref, acc_ref, wgu, wdn, v_sc, h2_sc, diff_sc, sems):
        i, j = pl.program_id(0), pl.program_id(1)
        _load_ffn(i, j, wgu_hbm, wgu, wdn_hbm, wdn, sems)

        @pl.when((i == 0) & (j == 0))
        def _():
            acc_ref[...] = jnp.zeros_like(acc_ref)

        @pl.when(j == 0)
        def _():
            h = h_ref[...]
            r = lax.rsqrt(jnp.mean(h * h, axis=-1, keepdims=True) + RMS_EPS)
            v = (h * r * gf_ref[...]).astype(BF16)
            v_sc[...] = v
            v_ref[...] = v
            h2_sc[...] = h

        v = v_sc[...]
        fg = _dot_nt(v, wgu[0, j])
        fu = _dot_nt(v, wgu[1, j])
        fg_ref[...] = fg
        fu_ref[...] = fu
        f = ((fg * _sig(fg)) * fu).astype(BF16)
        f_ref[...] = f
        h2_sc[...] += _dot(f, wdn[j])

        @pl.when(j == 1)
        def _():
            h2 = h2_sc[...]
            r = lax.rsqrt(jnp.mean(h2 * h2, axis=-1, keepdims=True) + RMS_EPS)
            n2 = h2 * r
            y = n2 * gl_ref[...]

            @pl.when(i < nt - 1)
            def _():
                diff_sc[...] = y - t_ref[...]

            @pl.when(i == nt - 1)
            def _():
                diff_sc[pl.ds(0, nx_last), :] = y[:nx_last] - t_ref[pl.ds(0, nx_last), :]
                diff_sc[pl.ds(nx_last, TM - nx_last), :] = jnp.zeros((TM - nx_last, D), F32)

            diff = diff_sc[...]
            dy = diff * (1.0 / D)
            acc_ref[0:1, :] += jnp.sum(diff * diff, axis=0, keepdims=True)
            acc_ref[1:2, :] += jnp.sum(dy * n2, axis=0, keepdims=True)
            dn = dy * gl_ref[...]
            dh2_ref[...] = r * (dn - n2 * jnp.mean(dn * n2, axis=-1, keepdims=True))

    def tile():
        return pl.BlockSpec((TM, D), lambda i, j: (i, 0))

    def chunk():
        return pl.BlockSpec((TM, FFC), lambda i, j: (i, j))

    def vec():
        return pl.BlockSpec((1, D), lambda i, j: (0, 0))

    anys = pl.BlockSpec(memory_space=pl.ANY)
    hid32, hid16 = jax.ShapeDtypeStruct((tp, DFF), F32), jax.ShapeDtypeStruct((tp, DFF), BF16)
    return pl.pallas_call(
        body, name="ffn_fwd", grid=(nt, 2),
        in_specs=[tile(), tile(), vec(), vec(), anys, anys],
        out_specs=[chunk(), chunk(), tile(), chunk(), tile(), pl.BlockSpec((8, D), lambda i, j: (0, 0))],
        out_shape=[hid32, hid32, jax.ShapeDtypeStruct((tp, D), BF16), hid16, jax.ShapeDtypeStruct((tp, D), F32),
                   jax.ShapeDtypeStruct((8, D), F32)],
        scratch_shapes=[pltpu.VMEM((2, 2, FFC, D), BF16), pltpu.VMEM((2, FFC, D), BF16),
                        pltpu.VMEM((TM, D), BF16), pltpu.VMEM((TM, D), F32), pltpu.VMEM((TM, D), F32),
                        pltpu.SemaphoreType.DMA((2 * NDEV + 2,))],
        compiler_params=_params(("arbitrary", "arbitrary"), 56),
    )(h1, tgt, g_ffn, g_final, w_gu, w_dn)


def _ffn_bwd(dh2, fg, fu, h1, g_ffn, w_gu, w_dn):
    tp = h1.shape[0]
    nt = tp // TM

    def body(dh2_ref, fg_ref, fu_ref, h_ref, gf_ref, wgu_hbm, wdn_hbm,
             dfg_ref, dfu_ref, dh1_ref, acc_ref, wgu, wdn, d_sc, dv_sc, sems):
        i, j = pl.program_id(0), pl.program_id(1)
        _load_ffn(i, j, wgu_hbm, wgu, wdn_hbm, wdn, sems)

        @pl.when((i == 0) & (j == 0))
        def _():
            acc_ref[...] = jnp.zeros_like(acc_ref)

        @pl.when(j == 0)
        def _():
            d_sc[...] = dh2_ref[...].astype(BF16)
            dv_sc[...] = jnp.zeros_like(dv_sc)

        df = _dot_nt(d_sc[...], wdn[j])
        fg = fg_ref[...]
        sg = _sig(fg)
        dfu = (df * (fg * sg)).astype(BF16)
        dfg = (df * fu_ref[...] * (sg * (1.0 + fg * (1.0 - sg)))).astype(BF16)
        dfg_ref[...] = dfg
        dfu_ref[...] = dfu
        dv_sc[...] += _dot(dfg, wgu[0, j]) + _dot(dfu, wgu[1, j])

        @pl.when(j == 1)
        def _():
            h = h_ref[...]
            r = lax.rsqrt(jnp.mean(h * h, axis=-1, keepdims=True) + RMS_EPS)
            n1 = h * r
            dv = dv_sc[...]
            acc_ref[0:1, :] += jnp.sum(dv * n1, axis=0, keepdims=True)
            dn = dv * gf_ref[...]
            dh1_ref[...] = dh2_ref[...] + r * (dn - n1 * jnp.mean(dn * n1, axis=-1, keepdims=True))

    def tile():
        return pl.BlockSpec((TM, D), lambda i, j: (i, 0))

    def chunk():
        return pl.BlockSpec((TM, FFC), lambda i, j: (i, j))

    anys = pl.BlockSpec(memory_space=pl.ANY)
    hid16 = jax.ShapeDtypeStruct((tp, DFF), BF16)
    return pl.pallas_call(
        body, name="ffn_bwd", grid=(nt, 2),
        in_specs=[tile(), chunk(), chunk(), tile(), pl.BlockSpec((1, D), lambda i, j: (0, 0)), anys, anys],
        out_specs=[chunk(), chunk(), tile(), pl.BlockSpec((8, D), lambda i, j: (0, 0))],
        out_shape=[hid16, hid16, jax.ShapeDtypeStruct((tp, D), F32), jax.ShapeDtypeStruct((8, D), F32)],
        scratch_shapes=[pltpu.VMEM((2, 2, FFC, D), BF16), pltpu.VMEM((2, FFC, D), BF16),
                        pltpu.VMEM((TM, D), BF16), pltpu.VMEM((TM, D), F32), pltpu.SemaphoreType.DMA((2 * NDEV + 2,))],
        compiler_params=_params(("arbitrary", "arbitrary"), 56),
    )(dh2, fg, fu, h1, g_ffn, w_gu, w_dn)


def _mix_bwd(dh1, z, s, q, ac, m, b_gate, ln_g, ln_b, pool_scale, g_mixw, g_pool, qs):
    tp = dh1.shape[0]
    nt = tp // TMS
    ex = _ChipExchange(qs)
    nq = ex.n

    def body(*refs):
        dh1_ref, z_hbm, s_ref, q_ref, ac_ref, m_ref, bg_ref, lg_ref, lb_ref, ps_ref, wm_hbm, wp_hbm = refs[:12]
        dac_ref, dm_ref, dzg_ref, dyc_ref, dyp_ref, dm2_ref, acc_ref = refs[12 + nq:19 + nq]
        wm, wp, sems, zbuf, zsem = refs[19 + 2 * nq:24 + 2 * nq]
        ex.bind(refs[12:12 + nq], refs[19 + nq:19 + 2 * nq], refs[24 + 2 * nq:])
        i = pl.program_id(0)
        first = i == 0

        def zcopy(t):
            slot = lax.rem(t, 3)
            rows = pl.ds(pl.multiple_of(t * TMS, TMS), TMS)
            return pltpu.make_async_copy(z_hbm.at[rows, pl.ds(3 * D, 2 * D)], zbuf.at[slot], zsem.at[slot])

        @pl.when(first)
        def _():
            ex.issue()
            acc_ref[...] = jnp.zeros_like(acc_ref)
            zcopy(0).start()
            zcopy(1).start()

        @pl.when(i + 2 < nt)
        def _():
            zcopy(i + 2).start()

        _load_once(first, [(wm_hbm, wm), (wp_hbm, wp)], sems)

        dmerged = _dot_nt(dh1_ref[...].astype(BF16), wm[:, 2].reshape(D, D))
        zcopy(i).wait()
        zg = zbuf.at[lax.rem(i, 3)]
        ga = _sig(zg[:, :D] + bg_ref[:, :D])
        gb = _sig(zg[:, D:] + bg_ref[:, D:])
        dyc = dmerged * ga
        dyp = dmerged * gb
        dza = (dmerged * _dot(s_ref[...], wm[:, 0].reshape(D, D))) * (ga * (1.0 - ga))
        dzb = (dmerged * _dot(q_ref[...], wm[:, 1].reshape(D, D))) * (gb * (1.0 - gb))
        dzg_ref[:, :D] = dza.astype(BF16)
        dzg_ref[:, D:] = dzb.astype(BF16)
        acc_ref[0:1, :D] += jnp.sum(dza, axis=0, keepdims=True)
        acc_ref[0:1, D:] += jnp.sum(dzb, axis=0, keepdims=True)
        dyc_b = dyc.astype(BF16)
        dyp_b = dyp.astype(BF16)
        dyc_ref[...] = dyc_b
        dyp_ref[...] = dyp_b
        ds = _dot_nt(dyc_b, wm[:, 0].reshape(D, D))
        n, rl = _ln_stats(ac_ref[...])
        l = n * lg_ref[...] + lb_ref[...]
        sg = _sig(l)
        dl = ds * (sg * (1.0 + l * (1.0 - sg)))
        acc_ref[1:2, :D] += jnp.sum(dl * n, axis=0, keepdims=True)
        acc_ref[1:2, D:] += jnp.sum(dl, axis=0, keepdims=True)
        dn = dl * lg_ref[...]
        dac_ref[...] = rl * (dn - jnp.mean(dn, axis=-1, keepdims=True) - n * jnp.mean(dn * n, axis=-1, keepdims=True))
        dq = _dot_nt(dyp_b, wm[:, 1].reshape(D, D))
        mv = m_ref[...]
        acc_ref[2:3, :D] += jnp.sum(dq * _pool_mix(mv, wp), axis=0, keepdims=True)
        dm2 = (dq * ps_ref[...]).astype(BF16)
        dm2_ref[...] = dm2
        dm_ref[...] = jnp.concatenate(
            [_dot_nt(dm2[:, g * PG:(g + 1) * PG], wp[:, g].reshape(PG, PG)) for g in range(4)], axis=1)

        @pl.when(pl.program_id(0) == nt - 1)
        def _():
            ex.finish()

    def tile(col=0):
        return pl.BlockSpec((TMS, D), lambda i: (i, col))

    def vec(w):
        return pl.BlockSpec((1, w), lambda i: (0, 0))

    anys = pl.BlockSpec(memory_space=pl.ANY)
    f32o, b16o = jax.ShapeDtypeStruct((tp, D), F32), jax.ShapeDtypeStruct((tp, D), BF16)
    res = pl.pallas_call(
        body, name="mix_bwd", grid=(nt,),
        in_specs=[tile(), anys, tile(), tile(), tile(), tile(), vec(2 * D), vec(D), vec(D), vec(D), anys, anys]
        + [anys] * nq,
        out_specs=[tile(), tile(), pl.BlockSpec((TMS, 2 * D), lambda i: (i, 0)), tile(), tile(), tile(),
                   pl.BlockSpec((8, 2 * D), lambda i: (0, 0))] + [anys] * nq,
        out_shape=[f32o, f32o, jax.ShapeDtypeStruct((tp, 2 * D), BF16), b16o, b16o, b16o,
                   jax.ShapeDtypeStruct((8, 2 * D), F32)] + ex.out_shape,
        scratch_shapes=[pltpu.VMEM((NDEV, 3, D // NDEV, D), BF16), pltpu.VMEM((NDEV, 4, PG // NDEV, PG), BF16),
                        pltpu.SemaphoreType.DMA((2,)), pltpu.VMEM((3, TMS, 2 * D), F32), pltpu.SemaphoreType.DMA((3,))]
        + ex.scratch,
        compiler_params=_params(("arbitrary",), 52),
    )(dh1, z, s, q, ac, m, b_gate, ln_g, ln_b, pool_scale, g_mixw, g_pool, *qs)
    return res[:7], res[7:]


def _seq_bwd(dac, dm, dzg, z, w_dw, seq, qs):
    tp = z.shape[0]
    nt = tp // TM
    ex = _ChipExchange(qs)
    nq = no = ex.n

    def body(*refs):
        dac_l, dac_c, dac_r, dm_l, dm_c, dm_r, av_l, av, av_r, ag_l, ag, ag_r, dzg_ref, w_ref = refs[:14]
        dz_ref, acc_ref = refs[14 + nq:16 + nq]
        a3, d3, m3, da3, dp3, w3, dw3, da_sc, dp_sc = refs[16 + nq + no:25 + nq + no]
        ex.bind(refs[14:14 + nq], refs[16 + nq:16 + nq + no], refs[25 + nq + no:])
        i = pl.program_id(0)
        sub = lax.broadcasted_iota(jnp.int32, (NCB, 128), 0)

        @pl.when(i == 0)
        def _():
            ex.issue()
            dw3[...] = jnp.zeros_like(dw3)
            _tm_fill(w3, 0, 4, lambda r, l: w_ref[pl.ds(r, 8), l])

        _tm_fill_ext(a3, (av_l, ag_l), (av, ag), (av_r, ag_r), lambda vg, r, l: vg[0][r, l] * _sig(vg[1][r, l]), unroll=2)
        _tm_fill_ext(d3, dac_l, dac_c, dac_r, lambda ref, r, l: ref[r, l])
        _tm_fill_ext(m3, dm_l, dm_c, dm_r, lambda ref, r, l: ref[r, l])

        def conv(g, c):
            t0 = CONV_STEPS * g
            dcur = [_tm_at(d3, t0 + t + HALO) for t in range(CONV_STEPS)]
            accs = [None] * CONV_STEPS
            for k in range(CONV_K):
                wk = _tm_at(w3, k)
                prs = []
                for t in range(CONV_STEPS):
                    term = wk * _tm_at(d3, t0 + t + CONV_K - k)
                    accs[t] = term if accs[t] is None else accs[t] + term
                    prs.append(dcur[t] * _tm_at(a3, t0 + t + k + 1))
                while len(prs) > 1:
                    prs = [prs[j] + prs[j + 1] for j in range(0, len(prs) - 1, 2)] + prs[len(prs) - len(prs) % 2:]
                dw3[_tm_rows(k), :] += prs[0]
            s = dcur[0]
            for t in range(1, CONV_STEPS):
                s = s + dcur[t]
            dw3[_tm_rows(CONV_K), :] += s
            for t in range(CONV_STEPS):
                da3[_tm_rows(t0 + t), :] = accs[t]
            return c

        lax.fori_loop(0, TM // CONV_STEPS, conv, 0)

        for b in _edge_rows(seq, tp):
            e = lax.rem(b - i * TM + HALO + tp, tp)

            @pl.when(e < TME)
            def _():
                m3[_tm_rows(e), :] = _tm_at(m3, e) * _edge_gain(b, seq, tp, sub)

        inv = _by_group(sub, [1.0 / w for w in POOL_WINDOWS])

        def pool(g, c):
            for t in range(8):
                e = 8 * g + t + HALO
                sums = _nested_windows(lambda o: _tm_at(m3, e + o), [w // 2 + 1 - w for w in POOL_WINDOWS])
                dp3[_tm_rows(8 * g + t), :] = _by_group(sub, sums) * inv
            return c

        lax.fori_loop(0, TM // 8, pool, 0, unroll=2)

        _tm_read(da3, TM // 8, lambda r, l, tile: da_sc.__setitem__((r, l), tile))
        _tm_read(dp3, TM // 8, lambda r, l, tile: dp_sc.__setitem__((r, l), tile))
        sg = _sig(ag[...])
        da = da_sc[...]
        dz_ref[:, 0:D] = (da * sg).astype(BF16)
        dz_ref[:, D:2 * D] = (da * av[...] * (sg * (1.0 - sg))).astype(BF16)
        dz_ref[:, 2 * D:3 * D] = (dp_sc[...] - dm_c[...]).astype(BF16)
        dz_ref[:, 3 * D:] = dzg_ref[...]

        @pl.when(i == nt - 1)
        def _():
            _tm_read(dw3, 4, lambda r, l, tile: acc_ref.__setitem__((r, l), tile))
            ex.finish()

    tmaj = pltpu.VMEM((TM * NCB, 128), F32)
    text = pltpu.VMEM((TME * NCB, 128), F32)
    taps = pltpu.VMEM((32 * NCB, 128), F32)
    anys = pl.BlockSpec(memory_space=pl.ANY)
    res = pl.pallas_call(
        body, name="seq_bwd", grid=(nt,),
        in_specs=_halo_specs(0, nt) + _halo_specs(0, nt) + _halo_specs(0, nt) + _halo_specs(1, nt)
        + [pl.BlockSpec((TM, 2 * D), lambda i: (i, 0)), pl.BlockSpec((32, D), lambda i: (0, 0))] + [anys] * nq,
        out_specs=[pl.BlockSpec((TM, DIN), lambda i: (i, 0)), pl.BlockSpec((32, D), lambda i: (0, 0))] + [anys] * no,
        out_shape=[jax.ShapeDtypeStruct((tp, DIN), BF16), jax.ShapeDtypeStruct((32, D), F32)] + ex.out_shape,
        scratch_shapes=[text, text, text, tmaj, tmaj, taps, taps, pltpu.VMEM((TM, D), F32), pltpu.VMEM((TM, D), F32)]
        + ex.scratch,
        compiler_params=_params(("arbitrary",), 48),
    )(dac, dac, dac, dm, dm, dm, z, z, z, z, z, z, dzg, w_dw, *qs)
    return res[:2], res[2:]


def _in_bwd(dz, h0, dh1, g_mix, w_g, seq, qs):
    tp = h0.shape[0]
    tm = _pick(tp, TM_IO)
    nt = tp // tm
    ex = _ChipExchange(qs)
    nq = no = ex.n

    def body(*refs):
        dz_ref, h_ref, dh1_ref, g_ref, w_hbm = refs[:5]
        gx_ref, gmeta_ref, acc_ref = refs[5 + nq:8 + nq]
        w_vm, sems = refs[8 + nq + no:10 + nq + no]
        ex.bind(refs[5:5 + nq], refs[8 + nq:8 + nq + no], refs[10 + nq + no:])
        i = pl.program_id(0)

        @pl.when(i == 0)
        def _():
            ex.issue()
            acc_ref[...] = jnp.zeros_like(acc_ref)

        _load_once(i == 0, _win_pairs(w_hbm, w_vm), sems)

        du = _dot_nt(dz_ref[:, :DIN // 2], w_vm[0]) + _dot_nt(dz_ref[:, DIN // 2:], w_vm[1])
        h = h_ref[...]
        r = lax.rsqrt(jnp.mean(h * h, axis=-1, keepdims=True) + RMS_EPS)
        n0 = h * r
        acc_ref[0:1, :] += jnp.sum(du * n0, axis=0, keepdims=True)
        dn = du * g_ref[...]
        gx_ref[...] = dh1_ref[...] + r * (dn - n0 * jnp.mean(dn * n0, axis=-1, keepdims=True))

        @pl.when(i == nt - 1)
        def _():
            gmeta_ref[...] = gx_ref[pl.ds(tm - N_META, N_META), :]
            ex.finish()

    tile = pl.BlockSpec((tm, D), lambda i: (i, 0))
    anys = pl.BlockSpec(memory_space=pl.ANY)
    res = pl.pallas_call(
        body, name="in_bwd", grid=(nt,),
        in_specs=[pl.BlockSpec((tm, DIN), lambda i: (i, 0)), tile, tile, pl.BlockSpec((1, D), lambda i: (0, 0)), anys]
        + [anys] * nq,
        out_specs=[tile, pl.BlockSpec((N_META, D), lambda i: (0, 0)), pl.BlockSpec((8, D), lambda i: (0, 0))] + [anys] * no,
        out_shape=[jax.ShapeDtypeStruct((seq, D), F32), jax.ShapeDtypeStruct((N_META, D), F32),
                   jax.ShapeDtypeStruct((8, D), F32)] + ex.out_shape,
        scratch_shapes=[pltpu.VMEM((2, D, DIN // 2), BF16), pltpu.SemaphoreType.DMA((NDEV,))] + ex.scratch,
        compiler_params=_params(("arbitrary",), 58),
    )(dz, h0, dh1, g_mix, w_g, *qs)
    return res[:3], res[3:]


def _wgrad_in(u, dz):
    tp = u.shape[0]
    tm = _pick(tp, TM_WG)
    nt = tp // tm
    half = DIN // 2

    def body(u_ref, dz_ref, o_ref, acc):
        t = pl.program_id(1)

        @pl.when(t == 0)
        def _():
            acc[...] = jnp.zeros_like(acc)

        acc[...] += _dot_tn(u_ref[...], dz_ref[...])

        @pl.when(t == nt - 1)
        def _():
            for d in range(4):
                o_ref[d] = acc[:, INB * d:INB * (d + 1)].astype(BF16)

    return pl.pallas_call(
        body, name="wgrad_in", grid=(2, nt),
        in_specs=[pl.BlockSpec((tm, D), lambda h, t: (t, 0)), pl.BlockSpec((tm, half), lambda h, t: (t, h))],
        out_specs=pl.BlockSpec((4, D, INB), lambda h, t: (h, 0, 0), pipeline_mode=pl.Buffered(1)),
        out_shape=jax.ShapeDtypeStruct((NDEV, D, INB), BF16),
        scratch_shapes=[pltpu.VMEM((D, half), F32)],
        compiler_params=_params(("arbitrary", "arbitrary"), 52),
    )(u, dz)


def _wgrad_mix(s, dyc, q, dyp, merged, dh1, m, dm2):
    tp = s.shape[0]
    tm = _pick(tp, TM_WM)
    nt = tp // tm
    rb = D // NDEV

    def body(s_ref, dyc_ref, q_ref, dyp_ref, mg_ref, dh1_ref, m_ref, dm2_ref, o_ref, op_ref, acc, accp):
        t = pl.program_id(0)

        @pl.when(t == 0)
        def _():
            acc[...] = jnp.zeros_like(acc)
            accp[...] = jnp.zeros_like(accp)

        acc[0] += _dot_tn(s_ref[...], dyc_ref[...])
        acc[1] += _dot_tn(q_ref[...], dyp_ref[...])
        acc[2] += _dot_tn(mg_ref[...], dh1_ref[...].astype(BF16))
        for g in range(4):
            accp[g] += _dot_tn(m_ref[:, g * PG:(g + 1) * PG], dm2_ref[:, g * PG:(g + 1) * PG])

        @pl.when(t == nt - 1)
        def _():
            for d in range(NDEV):
                for k in range(3):
                    o_ref[d, k] = acc[k, rb * d:rb * (d + 1), :].astype(BF16)
                for g in range(4):
                    op_ref[d, g] = accp[g, 32 * d:32 * (d + 1), :].astype(BF16)

    tile = pl.BlockSpec((tm, D), lambda t: (t, 0))
    return pl.pallas_call(
        body, name="wgrad_mix", grid=(nt,),
        in_specs=[tile] * 8,
        out_specs=[pl.BlockSpec((NDEV, 3, rb, D), lambda t: (0, 0, 0, 0), pipeline_mode=pl.Buffered(1)),
                   pl.BlockSpec((NDEV, 4, 32, PG), lambda t: (0, 0, 0, 0), pipeline_mode=pl.Buffered(1))],
        out_shape=[jax.ShapeDtypeStruct((NDEV, 3, rb, D), BF16), jax.ShapeDtypeStruct((NDEV, 4, 32, PG), BF16)],
        scratch_shapes=[pltpu.VMEM((3, D, D), F32), pltpu.VMEM((4, PG, PG), F32)],
        compiler_params=_params(("arbitrary",), 56),
    )(s, dyc, q, dyp, merged, dh1, m, dm2)


def _wgrad_gu(v, dfg, dfu):
    tp = v.shape[0]
    tm = _pick(tp, TM_WG)
    nt = tp // tm

    def body(v_ref, dg_ref, du_ref, o_ref, acc):
        k, t = pl.program_id(0), pl.program_id(2)

        @pl.when(t == 0)
        def _():
            acc[...] = jnp.zeros_like(acc)

        @pl.when(k == 0)
        def _():
            acc[...] += _dot_tn(dg_ref[...], v_ref[...])

        @pl.when(k == 1)
        def _():
            acc[...] += _dot_tn(du_ref[...], v_ref[...])

        @pl.when(t == nt - 1)
        def _():
            for d in range(4):
                o_ref[d] = acc[FFB * d:FFB * (d + 1), :].astype(BF16)

    return pl.pallas_call(
        body, name="wgrad_gu", grid=(2, 2, nt),
        in_specs=[pl.BlockSpec((tm, D), lambda k, h, t: (t, 0)),
                  pl.BlockSpec((tm, FFC), lambda k, h, t: (t * (1 - k), h * (1 - k))),
                  pl.BlockSpec((tm, FFC), lambda k, h, t: (t * k, h * k))],
        out_specs=pl.BlockSpec((4, None, FFB, D), lambda k, h, t: (h, k, 0, 0), pipeline_mode=pl.Buffered(1)),
        out_shape=jax.ShapeDtypeStruct((NDEV, 2, FFB, D), BF16),
        scratch_shapes=[pltpu.VMEM((FFC, D), F32)],
        compiler_params=_params(("arbitrary",) * 3, 48),
    )(v, dfg, dfu)


def _wgrad_down(f, dh2):
    tp = f.shape[0]
    tm = _pick(tp, TM_WG)
    nt = tp // tm

    def body(f_ref, d_ref, o_ref, acc):
        t = pl.program_id(1)

        @pl.when(t == 0)
        def _():
            acc[...] = jnp.zeros_like(acc)

        acc[...] += _dot_tn(f_ref[...], d_ref[...].astype(BF16))

        @pl.when(t == nt - 1)
        def _():
            for d in range(4):
                o_ref[d] = acc[FFB * d:FFB * (d + 1), :].astype(BF16)

    return pl.pallas_call(
        body, name="wgrad_down", grid=(2, nt),
        in_specs=[pl.BlockSpec((tm, FFC), lambda h, t: (t, h)), pl.BlockSpec((tm, D), lambda h, t: (t, 0))],
        out_specs=pl.BlockSpec((4, FFB, D), lambda h, t: (h, 0, 0), pipeline_mode=pl.Buffered(1)),
        out_shape=jax.ShapeDtypeStruct((NDEV, FFB, D), BF16),
        scratch_shapes=[pltpu.VMEM((FFC, D), F32)],
        compiler_params=_params(("arbitrary", "arbitrary"), 48),
    )(f, dh2)


def kernel(x, meta_tokens, g_mix, w_in, b_gate, w_dw, b_dw, ln_g, ln_b, w_conv_out, w_pool, pool_scale, w_pool_out, w_o, g_ffn, w_ffn_gate, w_ffn_up, w_ffn_down, g_final, loss_target, m_meta_tokens, m_g_mix, m_w_in, m_b_gate, m_w_dw, m_b_dw, m_ln_g, m_ln_b, m_w_conv_out, m_w_pool, m_pool_scale, m_w_pool_out, m_w_o, m_g_ffn, m_w_ffn_gate, m_w_ffn_up, m_w_ffn_down, m_g_final, v_meta_tokens, v_g_mix, v_w_in, v_b_gate, v_w_dw, v_b_dw, v_ln_g, v_ln_b, v_w_conv_out, v_w_pool, v_pool_scale, v_w_pool_out, v_w_o, v_g_ffn, v_w_ffn_gate, v_w_ffn_up, v_w_ffn_down, v_g_final):
    seq = x.shape[1]
    tp = -(-(seq + 2 * HALO) // TM) * TM
    tm_in = _pick(tp, TM_IO)
    nx_last = seq - (tp // tm_in - 1) * tm_in
    assert 0 < nx_last <= tm_in - 2 * HALO and nx_last % 8 == 0 and 0 < seq - (tp // TM - 1) * TM

    whole = (Ellipsis,)
    ag_small = _Gather(
        [((48, D // NDEV), [(meta_tokens, pl.ds(0, N_META), whole), (w_dw, pl.ds(N_META, CONV_K), 0)])], [F32])
    ag_mix = _Gather([((3, D // NDEV, D), [(w_conv_out, 0, 0), (w_pool_out, 1, 0), (w_o, 2, 0)]),
                      ((4, PG // NDEV, PG), [(w_pool, whole, 0)])], [BF16, BF16])
    def tr(a):
        return jnp.swapaxes(a, 1, 2)

    ag_gu = _Gather([((2, FFB, D), [(tr(w_ffn_gate), 0, 0), (tr(w_ffn_up), 1, 0)])], [BF16])
    ag_dn = _Gather([((FFB, D), [(w_ffn_down, whole, 0)])], [BF16])

    mx, my = lax.axis_index("x"), lax.axis_index("y")
    order = jnp.stack([2 * mx + my, 2 * mx + 1 - my, 2 * (1 - mx) + my, 2 * (1 - mx) + 1 - my]).astype(jnp.int32)
    (h0, z, u, g_in), (g_mixw, g_pool), g_small = _fwd_in(x[0], g_mix, w_in, order, tp, ag_mix, ag_small)
    wdw_full = g_small.transpose(1, 0, 2).reshape(48, D)[N_META:]
    (ac, m), (w_gu,) = _seq_fwd(z, wdw_full, b_dw, seq, ag_gu)
    (h1, s, merged, q), (g_down,) = _mix_fwd(ac, m, z, h0, b_gate, ln_g, ln_b, pool_scale, g_mixw, g_pool, ag_dn)
    w_dn = g_down.reshape(2, FFC, D)
    fg, fu, v, f, dh2, head_acc = _ffn_fwd(h1, loss_target[0], g_ffn, g_final.reshape(1, D), w_gu, w_dn)

    dfg, dfu, dh1, ffn_acc = _ffn_bwd(dh2, fg, fu, h1, g_ffn, w_gu, w_dn)
    own_f, sib_f, q_f = _rs_pair("rs_pair_ffn", [_wgrad_gu(v, dfg, dfu), _wgrad_down(f, dh2)])
    (dac, dm, dzg, dyc, dyp, dm2, mix_acc), rel_dn = _mix_bwd(
        dh1, z, s, q, ac, m, b_gate, ln_g, ln_b, pool_scale, g_mixw, g_pool, q_f[1:])
    p_mix = _wgrad_mix(s, dyc, q, dyp, merged, dh1, m, dm2)
    own_m, sib_m, q_m = _rs_pair("rs_pair_mix", list(p_mix))
    (dz, seq_acc), rel_gm = _seq_bwd(dac, dm, dzg, z, wdw_full, seq, list(q_f[:1]) + list(q_m))
    rel_f, rel_m = [rel_gm[0], rel_dn[0]], rel_gm[1:]
    own_i, sib_i, q_i = _rs_pair("rs_pair_in", [_wgrad_in(u, dz)])
    (grad_x, g_meta, in_acc), rel_i = _in_bwd(dz, h0, dh1, g_mix, g_in, seq, q_i)
    small_g = jnp.concatenate([g_meta, seq_acc[:CONV_K], jnp.zeros((1, D), F32)], axis=0)
    p_small = small_g.reshape(48, NDEV, D // NDEV).transpose(1, 0, 2).astype(BF16)
    rep_g = jnp.concatenate([
        in_acc[0:1], mix_acc[0:1, :D], mix_acc[0:1, D:], seq_acc[CONV_K:CONV_K + 1], mix_acc[1:2, :D], mix_acc[1:2, D:],
        mix_acc[2:3, :D], ffn_acc[0:1], head_acc[1:2], head_acc[0:1], jnp.zeros((REP_ROWS - 10, D), F32)], axis=0)
    own_s, sib_s, rel_s, rep_all = _reduce_scatter([p_small], rep_g)
    owns = [own_i[0], own_s[0], own_m[0], own_m[1], own_f[0], own_f[1]]
    sibs = [sib_i[0], sib_s[0], sib_m[0], sib_m[1], sib_f[0], sib_f[1]]
    rels = [rel_i[0], rel_s[0], rel_m[0], rel_m[1], rel_f[0], rel_f[1]]

    def lead(a):
        return a.reshape(1, *a.shape)

    def stack4(a, lead_dims):
        return a.reshape(*lead_dims, 1, 4 * 32, PG)

    (r_in,) = _adamw_multi("adamw_in", lead(owns[0]), sibs[0][:, None], rels[0][:, None], [w_in], [m_w_in], [v_w_in], 4)
    r_meta, r_dw = _adamw_meta_dw(owns[1], sibs[1], rels[1], (meta_tokens, m_meta_tokens, v_meta_tokens),
                                  (w_dw, m_w_dw, v_w_dw))
    r_conv, r_pout, r_o = _adamw_multi("adamw_mix", owns[2], sibs[2], rels[2], [w_conv_out, w_pool_out, w_o],
                                       [m_w_conv_out, m_w_pool_out, m_w_o], [v_w_conv_out, v_w_pool_out, v_w_o], 1)
    (r_pool,) = _adamw_multi("adamw_pool", stack4(owns[3], ()), stack4(sibs[3], (1,)), stack4(rels[3], (3,)),
                             [w_pool.reshape(1, 128, PG)], [m_w_pool.reshape(1, 128, PG)], [v_w_pool.reshape(1, 128, PG)], 1)
    r_pool = tuple(a.reshape(w_pool.shape) for a in r_pool)
    r_gate, r_up = _adamw_multi("adamw_gu", owns[4], sibs[4], rels[4], [tr(w_ffn_gate), tr(w_ffn_up)],
                                [tr(m_w_ffn_gate), tr(m_w_ffn_up)], [tr(v_w_ffn_gate), tr(v_w_ffn_up)], 2)
    r_gate, r_up = tuple(tr(a) for a in r_gate), tuple(tr(a) for a in r_up)
    (r_down,) = _adamw_multi("adamw_down", lead(owns[5]), sibs[5][:, None], rels[5][:, None],
                             [w_ffn_down], [m_w_ffn_down], [v_w_ffn_down], 2)
    row = (1, D)
    loss, reps = _adamw_rep(
        rep_all,
        [g_mix, b_gate, b_dw, ln_g, ln_b, pool_scale, g_ffn, g_final.reshape(row)],
        [m_g_mix, m_b_gate, m_b_dw, m_ln_g, m_ln_b, m_pool_scale, m_g_ffn, m_g_final.reshape(row)],
        [v_g_mix, v_b_gate, v_b_dw, v_ln_g, v_ln_b, v_pool_scale, v_g_ffn, v_g_final.reshape(row)])
    r_gmix, r_bg, r_bdw, r_lg, r_lb, r_ps, r_gffn, r_gfin = reps
    r_gfin = tuple(a.reshape(D) for a in r_gfin)

    in_order = [r_meta, r_gmix, r_in, r_bg, r_dw, r_bdw, r_lg, r_lb, r_conv, r_pool, r_ps, r_pout, r_o, r_gffn,
                r_gate, r_up, r_down, r_gfin]
    return (loss.reshape(()), grad_x[None], *[r[0] for r in in_order], *[r[1] for r in in_order],
            *[r[2] for r in in_order], *[r[3] for r in in_order])
```

```python
import math

import jax
import jax.numpy as jnp
from jax import lax
from jax.experimental import pallas as pl
from jax.experimental.pallas import tpu as pltpu

F32, BF16 = jnp.float32, jnp.bfloat16
MESH_ID = pl.DeviceIdType.MESH
NDEV = 8

D = 1024
N_META = 16
CONV_K = 31
HALO = 16
POOL_WINDOWS = (2, 4, 8, 16)
PG = 256
DIN = 5 * D
DFF = 2816
FFB = DFF // NDEV
FFC = DFF // 2
INB = DIN // NDEV
RMS_EPS = 1e-6
LN_EPS = 1e-5
ADAM_LR, ADAM_B1, ADAM_B2, ADAM_EPS, ADAM_WD, ADAM_STEP = 0.001, 0.9, 0.999, 1e-08, 0.01, 10

TM = 384
TMS = 384
TM_IO = 704
TM_WG = 1408
TM_WM = 704
MIB = 2 ** 20


def _sig(x):
    return 0.5 * jnp.tanh(0.5 * x) + 0.5


def _dot(a, b):
    return jnp.dot(a, b, preferred_element_type=F32)


def _dot_nt(a, b):
    return lax.dot_general(a, b, (((1,), (1,)), ((), ())), preferred_element_type=F32)


def _dot_tn(a, b):
    return lax.dot_general(a, b, (((0,), (0,)), ((), ())), preferred_element_type=F32)


def _pick(tp, pref):
    return pref if tp % pref == 0 else TM


def _params(sem, vmem_mib):
    return pltpu.CompilerParams(dimension_semantics=sem, vmem_limit_bytes=vmem_mib * MIB)


def _load_once(first, pairs, sems):
    @pl.when(first)
    def _():
        cps = [pltpu.make_async_copy(s, d, sems.at[k]) for k, (s, d) in enumerate(pairs)]
        for cp in cps:
            cp.start()
        for cp in cps:
            cp.wait()


def _place():
    x, y, c = lax.axis_index("x"), lax.axis_index("y"), lax.axis_index("c")
    return x, y, c


class _Gather:
    def __init__(self, groups, dtypes):
        self.groups, self.dtypes, self.n = groups, dtypes, len(groups)
        self.arrays = [a for _, parts in groups for a, _, _ in parts]
        self.out_shape = [jax.ShapeDtypeStruct((NDEV, *s), dt) for (s, _), dt in zip(groups, dtypes)]
        self.scratch = [pltpu.VMEM(s, dt) for (s, _), dt in zip(groups, dtypes)] + [
            pltpu.SemaphoreType.DMA((7 * self.n,)), pltpu.SemaphoreType.DMA((7 * self.n,)),
            pltpu.SemaphoreType.DMA((self.n,))]

    def bind(self, ins, outs, scratch):
        self.ins, self.outs, self.stages = ins, outs, scratch[:self.n]
        self.send_sems, self.recv_sems, self.local_sems = scratch[self.n:]
        return self

    def _copy(self, w, k, block, to, src=None):
        dst = self.outs[w].at[4 * block[0] + 2 * block[1] + block[2]]
        return pltpu.make_async_remote_copy(
            src_ref=dst if src is None else src, dst_ref=dst,
            send_sem=self.send_sems.at[7 * w + k], recv_sem=self.recv_sems.at[7 * w + k],
            device_id=to, device_id_type=MESH_ID)

    def _first(self):
        x, y, c = _place()
        me, sibling = (x, y, c), (x, y, 1 - c)
        chips = [(1 - x, y), (x, 1 - y), (1 - x, 1 - y)]
        mine, first = [], []
        for w in range(self.n):
            mine.append(pltpu.make_async_copy(self.stages[w], self.outs[w].at[4 * x + 2 * y + c], self.local_sems.at[w]))
            first.append(self._copy(w, 0, me, sibling, src=self.stages[w]))
            first += [self._copy(w, 1 + j, me, (*chip, c), src=self.stages[w]) for j, chip in enumerate(chips)]
        return mine, first

    def _passed(self):
        x, y, c = _place()
        chips = [(1 - x, y), (x, 1 - y), (1 - x, 1 - y)]
        return [self._copy(w, 4 + j, (*chip, c), (x, y, 1 - c)) for w in range(self.n) for j, chip in enumerate(chips)]

    def issue(self):
        a = 0
        for w in range(self.n):
            shape, parts = self.groups[w]
            if sum(arr.size for arr, _, _ in parts) < math.prod(shape):
                self.stages[w][...] = jnp.zeros(shape, self.dtypes[w])
            for _, dst, src in parts:
                self.stages[w][dst] = self.ins[a][src].astype(self.dtypes[w])
                a += 1
        mine, first = self._first()
        for cp in mine + first:
            cp.start()

    def forward(self):
        x, y, c = _place()
        chips = [(1 - x, y), (x, 1 - y), (1 - x, 1 - y)]
        passed = self._passed()
        for w in range(self.n):
            for j, chip in enumerate(chips):
                self._copy(w, 1 + j, (*chip, c), (x, y, c)).wait_recv()
                passed[3 * w + j].start()

    def finish(self):
        x, y, c = _place()
        chips = [(1 - x, y), (x, 1 - y), (1 - x, 1 - y)]
        for w in range(self.n):
            self._copy(w, 0, (x, y, 1 - c), (x, y, c)).wait_recv()
            for j, chip in enumerate(chips):
                self._copy(w, 4 + j, (*chip, 1 - c), (x, y, c)).wait_recv()
        mine, first = self._first()
        for cp in first + self._passed():
            cp.wait_send()
        for cp in mine:
            cp.wait()


class _ChipExchange:
    def __init__(self, qs):
        self.n = len(qs)
        self.out_shape = [jax.ShapeDtypeStruct(q.shape, q.dtype) for q in qs]
        self.scratch = [pltpu.SemaphoreType.DMA((3 * self.n,)), pltpu.SemaphoreType.DMA((3 * self.n,))] if qs else []

    def bind(self, qs, rels, scratch):
        self.qs, self.rels = qs, rels
        self.send_sems, self.recv_sems = scratch if self.n else (None, None)
        return self

    def _copies(self):
        x, y, c = _place()
        chips = [(1 - x, y), (x, 1 - y), (1 - x, 1 - y)]
        return [pltpu.make_async_remote_copy(
            src_ref=self.qs[w].at[j], dst_ref=self.rels[w].at[j],
            send_sem=self.send_sems.at[3 * w + j], recv_sem=self.recv_sems.at[3 * w + j],
            device_id=(*chips[j], c), device_id_type=MESH_ID) for w in range(self.n) for j in range(3)]

    def issue(self):
        for cp in self._copies():
            cp.start()

    def finish(self):
        cps = self._copies()
        for cp in cps:
            cp.wait_recv()
        for cp in cps:
            cp.wait_send()


def _reduce_scatter(parts, small):
    n = len(parts)
    blks = [p.shape[1:] for p in parts]

    def body(*refs):
        ps, small_ref = refs[:n], refs[n]
        o = n + 1
        owns, sibs, rels, small_out = refs[o:o + n], refs[o + n:o + 2 * n], refs[o + 2 * n:o + 3 * n], refs[o + 3 * n]
        o += 3 * n + 1
        pa, pb, qst = refs[o:o + n], refs[o + n:o + 2 * n], refs[o + 2 * n:o + 3 * n]
        s1_send, s1_recv, s2_send, s2_recv, sm_send, sm_recv, lsem = refs[o + 3 * n:]
        x, y, c = _place()
        me = 4 * x + 2 * y + c
        sibling = (x, y, 1 - c)
        chips = [(1 - x, y), (x, 1 - y), (1 - x, 1 - y)]
        all_chips = [(x, y)] + chips

        own_cps = []
        for w in range(n):
            cp = pltpu.make_async_copy(ps[w].at[me], owns[w], lsem.at[w])
            cp.start()
            own_cps.append(cp)
        sm_own = pltpu.make_async_copy(small_ref, small_out.at[me], lsem.at[n])
        sm_own.start()

        def small_copy(r):
            peer = ((x + (r >> 2)) % 2, (y + ((r >> 1) & 1)) % 2, (c + (r & 1)) % 2)
            return pltpu.make_async_remote_copy(
                src_ref=small_ref, dst_ref=small_out.at[me], send_sem=sm_send.at[r - 1], recv_sem=sm_recv.at[r - 1],
                device_id=peer, device_id_type=MESH_ID)

        sm_cps = [small_copy(r) for r in range(1, NDEV)]
        for cp in sm_cps:
            cp.start()

        def pair_copy(w, rel):
            cx, cy = all_chips[rel]
            return pltpu.make_async_remote_copy(
                src_ref=ps[w].at[4 * cx + 2 * cy + (1 - c)], dst_ref=sibs[w].at[rel],
                send_sem=s1_send.at[4 * w + rel], recv_sem=s1_recv.at[4 * w + rel],
                device_id=sibling, device_id_type=MESH_ID)

        def chip_copy(w, j):
            return pltpu.make_async_remote_copy(
                src_ref=qst[w].at[j], dst_ref=rels[w].at[j],
                send_sem=s2_send.at[3 * w + j], recv_sem=s2_recv.at[3 * w + j],
                device_id=(*chips[j], c), device_id_type=MESH_ID)

        pair_cps = [pair_copy(w, rel) for w in range(n) for rel in (1, 2, 3, 0)]
        for cp in pair_cps:
            cp.start()
        chip_cps = []
        for w in range(n):
            for j, (cx, cy) in enumerate(chips):
                pair_copy(w, 1 + j).wait_recv()
                la = pltpu.make_async_copy(ps[w].at[4 * cx + 2 * cy + c], pa[w], lsem.at[n + 1])
                lb = pltpu.make_async_copy(sibs[w].at[1 + j], pb[w], lsem.at[n + 2])
                la.start()
                lb.start()
                la.wait()
                lb.wait()
                qst[w][j] = (pa[w][...].astype(F32) + pb[w][...].astype(F32)).astype(BF16)
                cp = chip_copy(w, j)
                cp.start()
                chip_cps.append(cp)
        for w in range(n):
            pair_copy(w, 0).wait_recv()
            for j in range(3):
                chip_copy(w, j).wait_recv()
        for cp in sm_cps:
            cp.wait_recv()
        for cp in pair_cps + chip_cps + sm_cps:
            cp.wait_send()
        for cp in own_cps:
            cp.wait()
        sm_own.wait()

    any_spec = pl.BlockSpec(memory_space=pl.ANY)
    outs = pl.pallas_call(
        body, name="rs_grads",
        out_shape=[jax.ShapeDtypeStruct(b, BF16) for b in blks]
        + [jax.ShapeDtypeStruct((4, *b), BF16) for b in blks]
        + [jax.ShapeDtypeStruct((3, *b), BF16) for b in blks]
        + [jax.ShapeDtypeStruct((NDEV, *small.shape), F32)],
        in_specs=[any_spec] * (n + 1),
        out_specs=[any_spec] * (3 * n + 1),
        scratch_shapes=[pltpu.VMEM(b, BF16) for b in blks] + [pltpu.VMEM(b, BF16) for b in blks]
        + [pltpu.VMEM((3, *b), BF16) for b in blks]
        + [pltpu.SemaphoreType.DMA((4 * n,)), pltpu.SemaphoreType.DMA((4 * n,)),
           pltpu.SemaphoreType.DMA((3 * n,)), pltpu.SemaphoreType.DMA((3 * n,)),
           pltpu.SemaphoreType.DMA((NDEV - 1,)), pltpu.SemaphoreType.DMA((NDEV - 1,)),
           pltpu.SemaphoreType.DMA((n + 3,))],
        compiler_params=pltpu.CompilerParams(vmem_limit_bytes=40 * MIB),
    )(*parts, small)
    return outs[:n], outs[n:2 * n], outs[2 * n:3 * n], outs[3 * n]


class _PairSum:
    def __init__(self, parts, keep_q=True):
        self.n = n = len(parts)
        self.keep_q = keep_q
        blks = [p.shape[1:] for p in parts]
        self.out_shape = [jax.ShapeDtypeStruct(b, BF16) for b in blks] + [jax.ShapeDtypeStruct((1, *b), BF16) for b in blks]
        if keep_q:
            self.out_shape += [jax.ShapeDtypeStruct((3, *b), BF16) for b in blks]
        self.scratch = [pltpu.VMEM((3, *b), BF16) for b in blks] * 3 + [
            pltpu.SemaphoreType.DMA((4 * n,)), pltpu.SemaphoreType.DMA((4 * n,)), pltpu.SemaphoreType.DMA((5 * n,))]

    def bind(self, ps, outs, scratch):
        n = self.n
        self.ps, self.owns, self.sibs, self.qs = ps, outs[:n], outs[n:2 * n], outs[2 * n:]
        self.pa, self.pb, self.qst = scratch[:n], scratch[n:2 * n], scratch[2 * n:3 * n]
        self.s_send, self.s_recv, self.lsem = scratch[3 * n:]
        return self

    def _local(self, with_q):
        n = self.n
        x, y, c = _place()
        chips = [(1 - x, y), (x, 1 - y), (1 - x, 1 - y)]
        own = [pltpu.make_async_copy(self.ps[w].at[4 * x + 2 * y + c], self.owns[w], self.lsem.at[w]) for w in range(n)]
        mine = [[pltpu.make_async_copy(self.ps[w].at[4 * cx + 2 * cy + c], self.pa[w].at[j], self.lsem.at[2 * n + 3 * w + j])
                 for j, (cx, cy) in enumerate(chips)] for w in range(n)]
        outq = [pltpu.make_async_copy(self.qst[w], self.qs[w], self.lsem.at[n + w]) for w in range(n)] if with_q else []
        return own, mine, outq

    def _pair(self, w, rel):
        x, y, c = _place()
        cx, cy = [(x, y), (1 - x, y), (x, 1 - y), (1 - x, 1 - y)][rel]
        return pltpu.make_async_remote_copy(
            src_ref=self.ps[w].at[4 * cx + 2 * cy + (1 - c)],
            dst_ref=self.sibs[w].at[0] if rel == 0 else self.pb[w].at[rel - 1],
            send_sem=self.s_send.at[4 * w + rel], recv_sem=self.s_recv.at[4 * w + rel],
            device_id=(x, y, 1 - c), device_id_type=MESH_ID)

    def issue(self):
        own, mine, _ = self._local(False)
        for cp in own + [cp for row in mine for cp in row]:
            cp.start()
        for w in range(self.n):
            for rel in (1, 2, 3, 0):
                self._pair(w, rel).start()

    def finish(self):
        own, mine, outq = self._local(self.keep_q)
        for w in range(self.n):
            for j in range(3):
                self._pair(w, 1 + j).wait_recv()
                mine[w][j].wait()
                self.qst[w][j] = (self.pa[w][j].astype(F32) + self.pb[w][j].astype(F32)).astype(BF16)
            if self.keep_q:
                outq[w].start()
        for w in range(self.n):
            self._pair(w, 0).wait_recv()
        for w in range(self.n):
            for rel in range(4):
                self._pair(w, rel).wait_send()
        for cp in own + outq:
            cp.wait()

    def results(self, outs):
        n = self.n
        return outs[:n], outs[n:2 * n], outs[2 * n:3 * n]


def _rs_pair(name, parts):
    ps = _PairSum(parts)
    n = ps.n

    def body(*refs):
        ps.bind(refs[:n], refs[n:4 * n], refs[4 * n:])
        ps.issue()
        ps.finish()

    any_spec = pl.BlockSpec(memory_space=pl.ANY)
    outs = pl.pallas_call(
        body, name=name, out_shape=ps.out_shape,
        in_specs=[any_spec] * n, out_specs=[any_spec] * (3 * n), scratch_shapes=ps.scratch,
        compiler_params=pltpu.CompilerParams(vmem_limit_bytes=48 * MIB),
    )(*parts)
    return ps.results(outs)


def _adamw_math(g, w, m, v):
    m = ADAM_B1 * m + (1.0 - ADAM_B1) * g
    v = ADAM_B2 * v + (1.0 - ADAM_B2) * (g * g)
    m_hat = m / (1.0 - ADAM_B1 ** ADAM_STEP)
    v_hat = v / (1.0 - ADAM_B2 ** ADAM_STEP)
    delta = -ADAM_LR * (m_hat / (jnp.sqrt(v_hat) + ADAM_EPS) + ADAM_WD * w)
    return delta, m, v


def _adamw_multi(name, own, sib, rel, ws, ms, vs, row_grid):
    k_n, r_n, c_n = own.shape
    rbk = r_n // row_grid

    def body(*refs):
        own_ref, sib_ref, r0_ref, r1_ref, r2_ref = refs[:5]
        w_refs, m_refs, v_refs = refs[5:5 + k_n], refs[5 + k_n:5 + 2 * k_n], refs[5 + 2 * k_n:5 + 3 * k_n]
        outs = refs[5 + 3 * k_n:]
        for k in range(k_n):
            g = own_ref[k].astype(F32) + sib_ref[k].astype(F32)
            g = g + r0_ref[k].astype(F32)
            g = g + r1_ref[k].astype(F32)
            g = g + r2_ref[k].astype(F32)
            delta, mm, vv = _adamw_math(g, w_refs[k][0], m_refs[k][0], v_refs[k][0])
            outs[4 * k][0] = g
            outs[4 * k + 1][0] = delta
            outs[4 * k + 2][0] = mm
            outs[4 * k + 3][0] = vv

    def lead(j):
        return pl.BlockSpec((None, k_n, rbk, c_n), lambda g: (j, 0, g, 0))

    wspec = pl.BlockSpec((1, rbk, c_n), lambda g: (0, g, 0))
    shp = jax.ShapeDtypeStruct((1, r_n, c_n), F32)
    res = pl.pallas_call(
        body, name=name, grid=(row_grid,),
        in_specs=[pl.BlockSpec((k_n, rbk, c_n), lambda g: (0, g, 0)), lead(0), lead(0), lead(1), lead(2)] + [wspec] * (3 * k_n),
        out_specs=[wspec] * (4 * k_n), out_shape=[shp] * (4 * k_n),
        compiler_params=_params(("arbitrary",), 40),
    )(own, sib, rel, rel, rel, *ws, *ms, *vs)
    return [tuple(res[4 * k:4 * k + 4]) for k in range(k_n)]


def _adamw_meta_dw(own, sib, rel, meta, dw):
    def body(own_ref, sib_ref, rel_ref, wm, mm, vm, wd, md, vd, *outs):
        def gsum(rows):
            g = own_ref[rows, :].astype(F32) + sib_ref[0, rows, :].astype(F32)
            for j in range(3):
                g = g + rel_ref[j, rows, :].astype(F32)
            return g

        g = gsum(pl.ds(0, N_META))
        delta, m2, v2 = _adamw_math(g, wm[...], mm[...], vm[...])
        for o, val in zip(outs[:4], (g, delta, m2, v2)):
            o[...] = val
        g = gsum(pl.ds(N_META, CONV_K))
        delta, m2, v2 = _adamw_math(g, wd[0], md[0], vd[0])
        for o, val in zip(outs[4:], (g, delta, m2, v2)):
            o[0] = val

    s_meta = jax.ShapeDtypeStruct(meta[0].shape, F32)
    s_dw = jax.ShapeDtypeStruct(dw[0].shape, F32)
    res = pl.pallas_call(body, name="adamw_meta_dw", out_shape=[s_meta] * 4 + [s_dw] * 4)(own, sib, rel, *meta, *dw)
    return tuple(res[:4]), tuple(res[4:])


REP_ROWS = 16


def _adamw_rep(gathered, ws, ms, vs):
    rows = [(0, 1), (1, 2), (3, 1), (4, 1), (5, 1), (6, 1), (7, 1), (8, 1)]

    def body(g_ref, *refs):
        w_refs, m_refs, v_refs = refs[:8], refs[8:16], refs[16:24]
        loss_ref, outs, acc = refs[24], refs[25:57], refs[57]
        g = g_ref[0]
        for d in range(1, NDEV):
            g = g + g_ref[d]
        acc[...] = g
        loss_ref[...] = (0.5 / D) * jnp.sum(acc[pl.ds(9, 1), :], axis=1, keepdims=True)
        for p, (r0, nr) in enumerate(rows):
            for h in range(nr):
                cols = pl.ds(h * D, D)
                gp = acc[pl.ds(r0 + h, 1), :]
                delta, mm, vv = _adamw_math(gp, w_refs[p][:, cols], m_refs[p][:, cols], v_refs[p][:, cols])
                for o, val in zip(outs[4 * p:4 * p + 4], (gp, delta, mm, vv)):
                    o[:, cols] = val

    shapes = [jax.ShapeDtypeStruct(w.shape, F32) for w in ws]
    res = pl.pallas_call(
        body, name="adamw_rep",
        out_shape=[jax.ShapeDtypeStruct((1, 1), F32)] + [s for s in shapes for _ in range(4)],
        scratch_shapes=[pltpu.VMEM((REP_ROWS, D), F32)],
    )(gathered, *ws, *ms, *vs)
    return res[0], [tuple(res[1 + 4 * p:5 + 4 * p]) for p in range(8)]


def _load_ffn(i, j, wgu_hbm, wgu, wdn_hbm, wdn, sems):
    half = NDEV // 2

    def copies(ch):
        pairs = [(wgu_hbm.at[half * ch + d, g], wgu.at[g, ch, pl.ds(FFB * d, FFB), :]) for g in range(2) for d in range(half)]
        pairs.append((wdn_hbm.at[ch], wdn.at[ch]))
        return [pltpu.make_async_copy(s, t, sems.at[(2 * half + 1) * ch + k]) for k, (s, t) in enumerate(pairs)]

    @pl.when((i == 0) & (j == 0))
    def _():
        for cp in copies(0) + copies(1):
            cp.start()

    for ch in range(2):
        @pl.when((i == 0) & (j == ch))
        def _():
            for cp in copies(ch):
                cp.wait()


def _win_pairs(w_hbm, w_vm):
    return [(w_hbm.at[q], w_vm.at[q // 2, :, pl.ds(2 * INB * (q % 2), 2 * INB)]) for q in range(4)]


def _whole(a):
    nd = a.ndim
    return pl.BlockSpec(a.shape, lambda *g: (0,) * nd)


CHIPW = 2 * INB
PHASE_CHIP = (1, 0, 2)
assert PHASE_CHIP[2] == 2


class _GatherIn:
    scratch = [pltpu.VMEM((D, INB), BF16), pltpu.SemaphoreType.DMA((7,)), pltpu.SemaphoreType.DMA((7,)),
               pltpu.SemaphoreType.DMA((1,))]

    def bind(self, w_ref, w_vm, scratch):
        self.w_ref, self.w_vm = w_ref, w_vm
        self.stage, self.send_sems, self.recv_sems, self.local_sem = scratch
        return self

    def _win(self, chip, core):
        return self.w_vm.at[2 * chip[0] + chip[1], core]

    def _copy(self, k, chip, core, to, src=None):
        dst = self._win(chip, core)
        return pltpu.make_async_remote_copy(
            src_ref=dst if src is None else src, dst_ref=dst, send_sem=self.send_sems.at[k],
            recv_sem=self.recv_sems.at[k], device_id=to, device_id_type=MESH_ID)

    def _mine(self, cs):
        x, y, _ = _place()
        return pltpu.make_async_copy(self.stage, self._win((x, y), cs), self.local_sem.at[0])

    def issue(self, cs):
        x, y, _ = _place()
        chips = [(1 - x, y), (x, 1 - y), (1 - x, 1 - y)]
        self.stage[...] = self.w_ref[0].astype(BF16)
        self._mine(cs).start()
        self._copy(0, (x, y), cs, (x, y, 1 - cs), src=self.stage).start()
        for j in PHASE_CHIP[:2]:
            self._copy(1 + j, (x, y), cs, (*chips[j], cs), src=self.stage).start()

    def wait_chip(self, phase, cs):
        x, y, _ = _place()
        chips = [(1 - x, y), (x, 1 - y), (1 - x, 1 - y)]
        if phase == 0:
            self._mine(cs).wait()
            self._copy(0, (x, y), 1 - cs, (x, y, cs)).wait_recv()
            return
        j = PHASE_CHIP[phase - 1]
        self._copy(1 + j, chips[j], cs, (x, y, cs)).wait_recv()
        self._copy(4 + j, chips[j], cs, (x, y, 1 - cs)).start()
        if phase == 1:
            self._copy(3, (x, y), cs, (*chips[2], cs), src=self.stage).start()
        self._copy(4 + j, chips[j], 1 - cs, (x, y, cs)).wait_recv()

    def finish(self, cs):
        x, y, _ = _place()
        for k in range(7):
            self._copy(k, (x, y), cs, (x, y, cs), src=self.stage).wait_send()


def _fwd_in(x2, g_mix, w_in, order, tp, ag, ags):
    tm = _pick(tp, TM_IO)
    nt = tp // tm
    nx_last = x2.shape[0] - (nt - 1) * tm
    na, ng, ns = len(ag.arrays), ag.n, len(ags.arrays)
    gin = _GatherIn()

    def body(order_ref, *refs):
        x_ref, g_ref, w_ref = refs[:3]
        o = 3 + na + ns
        h_ref, z_ref, u_ref, wout_ref = refs[o:o + 4]
        s = o + 4 + ng + 1
        w_vm, u_all, osem, sm_vm = refs[s:s + 4]
        gin.bind(w_ref, w_vm, refs[s + 4:s + 8])
        ag.bind(refs[3:3 + na], refs[o + 4:o + 4 + ng], refs[s + 8:s + 8 + len(ag.scratch)])
        ags.bind(refs[3 + na:3 + na + ns], refs[o + 4 + ng:o + 5 + ng], refs[s + 8 + len(ag.scratch):])
        ph, i = pl.program_id(0), pl.program_id(1)
        core = lax.axis_index("c")
        first = (ph == 0) & (i == 0)
        last = (ph == 3) & (i == nt - 1)
        @pl.when(first)
        def _():
            ags.issue()

        for cs in range(2):
            @pl.when(first & (core == cs))
            def _():
                gin.issue(cs)

        @pl.when((ph == 0) & (i == max(nt - 2, 0)))
        def _():
            ags.forward()

        for cs in range(2):
            for p in range(4):
                @pl.when((ph == p) & (i == 0) & (core == cs))
                def _():
                    gin.wait_chip(p, cs)

        @pl.when((ph == 2) & (i == 0))
        def _():
            ag.issue()

        out_copies = [pltpu.make_async_copy(w_vm.at[k, c], wout_ref.at[k, :, pl.ds(INB * c, INB)], osem.at[2 * k + c])
                      for k in range(4) for c in range(2)]

        @pl.when((ph == 3) & (i == 0))
        def _():
            for cp in out_copies:
                cp.start()

        @pl.when((ph == 0) & (i < nt - 1))
        def _():
            h_ref[...] = x_ref[...]

        @pl.when((ph == 0) & (i == nt - 1))
        def _():
            ags.finish()
            cp = pltpu.make_async_copy(ags.outs[0], sm_vm, osem.at[8])
            cp.start()
            h_ref[pl.ds(0, nx_last), :] = x_ref[pl.ds(0, nx_last), :]
            h_ref[pl.ds(nx_last, tm - nx_last - N_META), :] = jnp.zeros((tm - nx_last - N_META, D), F32)
            cp.wait()
            for d in range(NDEV):
                h_ref[pl.ds(tm - N_META, N_META), pl.ds(128 * d, 128)] = sm_vm[d, pl.ds(0, N_META), :]

        @pl.when(ph == 0)
        def _():
            xv = h_ref[...]
            r = lax.rsqrt(jnp.mean(xv * xv, axis=-1, keepdims=True) + RMS_EPS)
            u = (xv * r * g_ref[...]).astype(BF16)
            u_ref[...] = u
            u_all[i] = u

        for c in range(2):
            z_ref[:, INB * c:INB * (c + 1)] = _dot(u_all[i], w_vm[order_ref[ph], c])

        @pl.when(last)
        def _():
            ag.forward()
            ag.finish()
            for cp in out_copies:
                cp.wait()

        for cs in range(2):
            @pl.when(last & (core == cs))
            def _():
                gin.finish(cs)

    def rows(ph, i, order):
        return (jnp.where(ph == 0, i, nt - 1), 0)

    tile = pl.BlockSpec((tm, D), rows)
    anys = pl.BlockSpec(memory_space=pl.ANY)
    res = pl.pallas_call(
        body, name="fwd_in",
        grid_spec=pltpu.PrefetchScalarGridSpec(
            num_scalar_prefetch=1, grid=(4, nt),
            in_specs=[tile, pl.BlockSpec((1, D), lambda ph, i, order: (0, 0)), _whole(w_in)]
            + [_whole(a) for a in ag.arrays + ags.arrays],
            out_specs=[tile, pl.BlockSpec((tm, CHIPW), lambda ph, i, order: (i, order[ph])), tile, anys] + [anys] * (ng + 1),
            scratch_shapes=[pltpu.VMEM((4, 2, D, INB), BF16), pltpu.VMEM((nt, tm, D), BF16), pltpu.SemaphoreType.DMA((9,)),
                            pltpu.VMEM(ags.out_shape[0].shape, F32)] + gin.scratch + ag.scratch + ags.scratch),
        out_shape=[jax.ShapeDtypeStruct((tp, D), F32), jax.ShapeDtypeStruct((tp, DIN), F32),
                   jax.ShapeDtypeStruct((tp, D), BF16), jax.ShapeDtypeStruct((4, D, CHIPW), BF16)]
        + ag.out_shape + ags.out_shape,
        compiler_params=_params(("arbitrary", "arbitrary"), 58),
    )(order, x2, g_mix, w_in, *ag.arrays, *ags.arrays)
    return res[:4], res[4:4 + ng], res[4 + ng]


def _halo_specs(col, nt, width=D):
    r = TM // HALO
    nb = nt * r
    return [pl.BlockSpec((HALO, width), lambda i: ((i * r + nb - 1) % nb, col)),
            pl.BlockSpec((TM, width), lambda i: (i, col)),
            pl.BlockSpec((HALO, width), lambda i: (((i + 1) * r) % nb, col))]


NCB = D // 128
TME = TM + 2 * HALO
CONV_STEPS = 16
assert TM % CONV_STEPS == 0


def _tm_fill(dst, time0, groups, tile_fn, unroll=1):
    def body(g, c):
        for j in range(NCB):
            dst[pl.ds((time0 + 8 * g) * NCB + j, 8, stride=NCB), :] = tile_fn(pl.multiple_of(8 * g, 8), pl.ds(128 * j, 128))
        return c

    lax.fori_loop(0, groups, body, 0, unroll=unroll)


def _tm_fill_ext(dst, left, cur, right, fn, unroll=1):
    _tm_fill(dst, 0, HALO // 8, lambda r, l: fn(left, pl.ds(r, 8), l), unroll)
    _tm_fill(dst, HALO, TM // 8, lambda r, l: fn(cur, pl.ds(r, 8), l), unroll)
    _tm_fill(dst, HALO + TM, HALO // 8, lambda r, l: fn(right, pl.ds(r, 8), l), unroll)


def _tm_read(src, groups, store_fn):
    def body(g, c):
        for j in range(NCB):
            store_fn(pl.ds(pl.multiple_of(8 * g, 8), 8), pl.ds(128 * j, 128), src[pl.ds(8 * g * NCB + j, 8, stride=NCB), :])
        return c

    lax.fori_loop(0, groups, body, 0, unroll=2)


def _tm_rows(t):
    return pl.ds(t * NCB if isinstance(t, int) else pl.multiple_of(t * NCB, NCB), NCB)


def _tm_at(ref, t):
    return ref[_tm_rows(t), :]


def _by_group(sub, vals):
    return jnp.where(sub < 2, vals[0], jnp.where(sub < 4, vals[1], jnp.where(sub < 6, vals[2], vals[3])))


def _pool_cnt(b, seq, tp, sub):
    b = jnp.where(b < 0, b + tp, b)
    b = jnp.where(b >= tp, b - tp, b)
    t = jnp.where(b < seq, b + N_META, b - (tp - N_META))
    cnts = []
    for win in POOL_WINDOWS:
        left = win // 2
        lo = jnp.maximum(t - left, 0)
        hi = jnp.minimum(t + win - left, seq + N_META)
        cnts.append(jnp.maximum(hi - lo, 1).astype(F32))
    return _by_group(sub, cnts)


def _edge_rows(seq, tp):
    reach = max(POOL_WINDOWS) // 2
    return [tp - N_META + t for t in range(reach)] + [seq - reach + 1 + t for t in range(reach - 1)]


def _edge_gain(b, seq, tp, sub):
    return _by_group(sub, [float(w) for w in POOL_WINDOWS]) / _pool_cnt(b, seq, tp, sub)


def _nested_windows(at, lo_offs):
    sums, s, have = [], None, set()
    for g, win in enumerate(POOL_WINDOWS):
        for o in range(lo_offs[g], lo_offs[g] + win):
            if o not in have:
                have.add(o)
                s = at(o) if s is None else s + at(o)
        sums.append(s)
    return sums


def _seq_fwd(z, w_dw, b_dw, seq, gat):
    tp = z.shape[0]
    nt = tp // TM
    na, ng = len(gat.arrays), gat.n

    def body(*refs):
        av_l, av, av_r, ag_l, ag, ag_r, p_l, p, p_r, w_ref, b_ref = refs[:11]
        ac_ref, m_ref = refs[11 + na:13 + na]
        a3, p3, o3, m3, w3, b3, m2d = refs[13 + na + ng:20 + na + ng]
        gat.bind(refs[11:11 + na], refs[13 + na:13 + na + ng], refs[20 + na + ng:])
        i = pl.program_id(0)
        sub = lax.broadcasted_iota(jnp.int32, (NCB, 128), 0)

        @pl.when(i == 0)
        def _():
            gat.issue()
            _tm_fill(w3, 0, 4, lambda r, l: w_ref[pl.ds(r, 8), l])
            for j in range(NCB):
                b3[pl.ds(j, 1), :] = b_ref[:, pl.ds(128 * j, 128)]

        @pl.when(i == max(nt - 2, 0))
        def _():
            gat.forward()

        _tm_fill_ext(a3, (av_l, ag_l), (av, ag), (av_r, ag_r), lambda vg, r, l: vg[0][r, l] * _sig(vg[1][r, l]), unroll=2)
        _tm_fill_ext(p3, p_l, p, p_r, lambda ref, r, l: ref[r, l])

        def conv(g, c):
            accs = [b3[...]] * 16
            for k in range(CONV_K):
                wk = _tm_at(w3, k)
                for t in range(16):
                    accs[t] = accs[t] + wk * _tm_at(a3, 16 * g + t + k + 1)
            for t in range(16):
                o3[_tm_rows(16 * g + t), :] = accs[t]
            return c

        lax.fori_loop(0, TM // 16, conv, 0)
        _tm_read(o3, TM // 8, lambda r, l, tile: ac_ref.__setitem__((r, l), tile))

        inv = _by_group(sub, [1.0 / w for w in POOL_WINDOWS])

        def pool(g, c):
            for t in range(8):
                e = 8 * g + t + HALO
                sums = _nested_windows(lambda o: _tm_at(p3, e + o), [-(w // 2) for w in POOL_WINDOWS])
                m3[_tm_rows(8 * g + t), :] = _by_group(sub, sums) * inv - _tm_at(p3, e)
            return c

        lax.fori_loop(0, TM // 8, pool, 0)
        for b in _edge_rows(seq, tp):
            r = b - i * TM

            @pl.when((r >= 0) & (r < TM))
            def _():
                pv = _tm_at(p3, r + HALO)
                m3[_tm_rows(r), :] = (_tm_at(m3, r) + pv) * _edge_gain(b, seq, tp, sub) - pv

        _tm_read(m3, TM // 8, lambda r, l, tile: m2d.__setitem__((r, l), tile))
        m_ref[...] = m2d[...].astype(BF16)

        @pl.when(i == nt - 1)
        def _():
            gat.finish()

    tmaj = pltpu.VMEM((TM * NCB, 128), F32)
    text = pltpu.VMEM((TME * NCB, 128), F32)
    res = pl.pallas_call(
        body, name="seq_fwd", grid=(nt,),
        in_specs=_halo_specs(0, nt) + _halo_specs(1, nt) + _halo_specs(2, nt)
        + [pl.BlockSpec((32, D), lambda i: (0, 0)), pl.BlockSpec((1, D), lambda i: (0, 0))] + [_whole(a) for a in gat.arrays],
        out_specs=[pl.BlockSpec((TM, D), lambda i: (i, 0))] * 2 + [pl.BlockSpec(memory_space=pl.ANY)] * ng,
        out_shape=[jax.ShapeDtypeStruct((tp, D), F32), jax.ShapeDtypeStruct((tp, D), BF16)] + gat.out_shape,
        scratch_shapes=[text, text, tmaj, tmaj, pltpu.VMEM((32 * NCB, 128), F32), pltpu.VMEM((NCB, 128), F32),
                        pltpu.VMEM((TM, D), F32)] + gat.scratch,
        compiler_params=_params(("arbitrary",), 52),
    )(z, z, z, z, z, z, z, z, z, w_dw, b_dw, *gat.arrays)
    return res[:2], res[2:]


def _ln_stats(ac):
    mu = jnp.mean(ac, axis=-1, keepdims=True)
    xc = ac - mu
    rl = lax.rsqrt(jnp.mean(xc * xc, axis=-1, keepdims=True) + LN_EPS)
    return xc * rl, rl


def _pool_mix(m, wp_ref):
    return jnp.concatenate(
        [_dot(m[:, g * PG:(g + 1) * PG], wp_ref[:, g].reshape(PG, PG)) for g in range(4)], axis=1)


def _mix_fwd(ac, m, z, h0, b_gate, ln_g, ln_b, pool_scale, g_mixw, g_pool, gat):
    tp = h0.shape[0]
    tms = TM
    nt = tp // tms
    na, ng = len(gat.arrays), gat.n

    def body(*refs):
        ac_ref, m_ref, zga, zgb, h_ref, bg_ref, lg_ref, lb_ref, ps_ref, wm_hbm, wp_hbm = refs[:11]
        h1_ref, s_ref, mg_ref, q_ref = refs[11 + na:15 + na]
        wm, wp, sems = refs[15 + na + ng:18 + na + ng]
        gat.bind(refs[11:11 + na], refs[15 + na:15 + na + ng], refs[18 + na + ng:])
        i = pl.program_id(0)

        @pl.when(i == 0)
        def _():
            gat.issue()

        @pl.when(i == max(nt - 4, 0))
        def _():
            gat.forward()

        @pl.when(i == nt - 1)
        def _():
            gat.finish()

        _load_once(i == 0, [(wm_hbm, wm), (wp_hbm, wp)], sems)
        n, _ = _ln_stats(ac_ref[...])
        l = n * lg_ref[...] + lb_ref[...]
        s = (l * _sig(l)).astype(BF16)
        s_ref[...] = s
        yc = _dot(s, wm[:, 0].reshape(D, D))
        q = (_pool_mix(m_ref[...], wp) * ps_ref[...]).astype(BF16)
        q_ref[...] = q
        yp = _dot(q, wm[:, 1].reshape(D, D))
        ga = _sig(zga[...] + bg_ref[:, :D])
        gb = _sig(zgb[...] + bg_ref[:, D:])
        merged = (ga * yc + gb * yp).astype(BF16)
        mg_ref[...] = merged
        h1_ref[...] = h_ref[...] + _dot(merged, wm[:, 2].reshape(D, D))

    def tile(col=0):
        return pl.BlockSpec((tms, D), lambda i: (i, col))

    def vec(w):
        return pl.BlockSpec((1, w), lambda i: (0, 0))

    anys = pl.BlockSpec(memory_space=pl.ANY)
    f32o, b16o = jax.ShapeDtypeStruct((tp, D), F32), jax.ShapeDtypeStruct((tp, D), BF16)
    res = pl.pallas_call(
        body, name="mix_fwd", grid=(nt,),
        in_specs=[tile(), tile(), tile(3), tile(4), tile(), vec(2 * D), vec(D), vec(D), vec(D), anys, anys]
        + [_whole(a) for a in gat.arrays],
        out_specs=[tile()] * 4 + [anys] * ng,
        out_shape=[f32o, b16o, b16o, b16o] + gat.out_shape,
        scratch_shapes=[pltpu.VMEM((NDEV, 3, D // NDEV, D), BF16), pltpu.VMEM((NDEV, 4, PG // NDEV, PG), BF16),
                        pltpu.SemaphoreType.DMA((2,))] + gat.scratch,
        compiler_params=_params(("arbitrary",), 52),
    )(ac, m, z, z, h0, b_gate, ln_g, ln_b, pool_scale, g_mixw, g_pool, *gat.arrays)
    return res[:4], res[4:]


def _ffn_fwd(h1, tgt, g_ffn, g_final, w_gu, w_dn):
    tp = h1.shape[0]
    nt = tp // TM
    nx_last = tgt.shape[0] - (nt - 1) * TM

    def body(h_ref, t_ref, gf_ref, gl_ref, wgu_hbm, wdn_hbm,
             fg_ref, fu_ref, v_ref, f_ref, dh2_ref, acc_ref, wgu, wdn, v_sc, h2_sc, diff_sc, sems):
        i, j = pl.program_id(0), pl.program_id(1)
        _load_ffn(i, j, wgu_hbm, wgu, wdn_hbm, wdn, sems)

        @pl.when((i == 0) & (j == 0))
        def _():
            acc_ref[...] = jnp.zeros_like(acc_ref)

        @pl.when(j == 0)
        def _():
            h = h_ref[...]
            r = lax.rsqrt(jnp.mean(h * h, axis=-1, keepdims=True) + RMS_EPS)
            v = (h * r * gf_ref[...]).astype(BF16)
            v_sc[...] = v
            v_ref[...] = v
            h2_sc[...] = h

        v = v_sc[...]
        fg = _dot_nt(v, wgu[0, j])
        fu = _dot_nt(v, wgu[1, j])
        fg_ref[...] = fg
        fu_ref[...] = fu
        f = ((fg * _sig(fg)) * fu).astype(BF16)
        f_ref[...] = f
        h2_sc[...] += _dot(f, wdn[j])

        @pl.when(j == 1)
        def _():
            h2 = h2_sc[...]
            r = lax.rsqrt(jnp.mean(h2 * h2, axis=-1, keepdims=True) + RMS_EPS)
            n2 = h2 * r
            y = n2 * gl_ref[...]

            @pl.when(i < nt - 1)
            def _():
                diff_sc[...] = y - t_ref[...]

            @pl.when(i == nt - 1)
            def _():
                diff_sc[pl.ds(0, nx_last), :] = y[:nx_last] - t_ref[pl.ds(0, nx_last), :]
                diff_sc[pl.ds(nx_last, TM - nx_last), :] = jnp.zeros((TM - nx_last, D), F32)

            diff = diff_sc[...]
            dy = diff * (1.0 / D)
            acc_ref[0:1, :] += jnp.sum(diff * diff, axis=0, keepdims=True)
            acc_ref[1:2, :] += jnp.sum(dy * n2, axis=0, keepdims=True)
            dn = dy * gl_ref[...]
            dh2_ref[...] = r * (dn - n2 * jnp.mean(dn * n2, axis=-1, keepdims=True))

    def tile():
        return pl.BlockSpec((TM, D), lambda i, j: (i, 0))

    def chunk():
        return pl.BlockSpec((TM, FFC), lambda i, j: (i, j))

    def vec():
        return pl.BlockSpec((1, D), lambda i, j: (0, 0))

    anys = pl.BlockSpec(memory_space=pl.ANY)
    hid32, hid16 = jax.ShapeDtypeStruct((tp, DFF), F32), jax.ShapeDtypeStruct((tp, DFF), BF16)
    return pl.pallas_call(
        body, name="ffn_fwd", grid=(nt, 2),
        in_specs=[tile(), tile(), vec(), vec(), anys, anys],
        out_specs=[chunk(), chunk(), tile(), chunk(), tile(), pl.BlockSpec((8, D), lambda i, j: (0, 0))],
        out_shape=[hid32, hid32, jax.ShapeDtypeStruct((tp, D), BF16), hid16, jax.ShapeDtypeStruct((tp, D), F32),
                   jax.ShapeDtypeStruct((8, D), F32)],
        scratch_shapes=[pltpu.VMEM((2, 2, FFC, D), BF16), pltpu.VMEM((2, FFC, D), BF16),
                        pltpu.VMEM((TM, D), BF16), pltpu.VMEM((TM, D), F32), pltpu.VMEM((TM, D), F32),
                        pltpu.SemaphoreType.DMA((2 * NDEV + 2,))],
        compiler_params=_params(("arbitrary", "arbitrary"), 56),
    )(h1, tgt, g_ffn, g_final, w_gu, w_dn)


def _ffn_bwd(dh2, fg, fu, h1, g_ffn, w_gu, w_dn):
    tp = h1.shape[0]
    nt = tp // TM

    def body(dh2_ref, fg_ref, fu_ref, h_ref, gf_ref, wgu_hbm, wdn_hbm,
             dfg_ref, dfu_ref, dh1_ref, acc_ref, wgu, wdn, d_sc, dv_sc, sems):
        i, j = pl.program_id(0), pl.program_id(1)
        _load_ffn(i, j, wgu_hbm, wgu, wdn_hbm, wdn, sems)

        @pl.when((i == 0) & (j == 0))
        def _():
            acc_ref[...] = jnp.zeros_like(acc_ref)

        @pl.when(j == 0)
        def _():
            d_sc[...] = dh2_ref[...].astype(BF16)
            dv_sc[...] = jnp.zeros_like(dv_sc)

        df = _dot_nt(d_sc[...], wdn[j])
        fg = fg_ref[...]
        sg = _sig(fg)
        dfu = (df * (fg * sg)).astype(BF16)
        dfg = (df * fu_ref[...] * (sg * (1.0 + fg * (1.0 - sg)))).astype(BF16)
        dfg_ref[...] = dfg
        dfu_ref[...] = dfu
        dv_sc[...] += _dot(dfg, wgu[0, j]) + _dot(dfu, wgu[1, j])

        @pl.when(j == 1)
        def _():
            h = h_ref[...]
            r = lax.rsqrt(jnp.mean(h * h, axis=-1, keepdims=True) + RMS_EPS)
            n1 = h * r
            dv = dv_sc[...]
            acc_ref[0:1, :] += jnp.sum(dv * n1, axis=0, keepdims=True)
            dn = dv * gf_ref[...]
            dh1_ref[...] = dh2_ref[...] + r * (dn - n1 * jnp.mean(dn * n1, axis=-1, keepdims=True))

    def tile():
        return pl.BlockSpec((TM, D), lambda i, j: (i, 0))

    def chunk():
        return pl.BlockSpec((TM, FFC), lambda i, j: (i, j))

    anys = pl.BlockSpec(memory_space=pl.ANY)
    hid16 = jax.ShapeDtypeStruct((tp, DFF), BF16)
    return pl.pallas_call(
        body, name="ffn_bwd", grid=(nt, 2),
        in_specs=[tile(), chunk(), chunk(), tile(), pl.BlockSpec((1, D), lambda i, j: (0, 0)), anys, anys],
        out_specs=[chunk(), chunk(), tile(), pl.BlockSpec((8, D), lambda i, j: (0, 0))],
        out_shape=[hid16, hid16, jax.ShapeDtypeStruct((tp, D), F32), jax.ShapeDtypeStruct((8, D), F32)],
        scratch_shapes=[pltpu.VMEM((2, 2, FFC, D), BF16), pltpu.VMEM((2, FFC, D), BF16),
                        pltpu.VMEM((TM, D), BF16), pltpu.VMEM((TM, D), F32), pltpu.SemaphoreType.DMA((2 * NDEV + 2,))],
        compiler_params=_params(("arbitrary", "arbitrary"), 56),
    )(dh2, fg, fu, h1, g_ffn, w_gu, w_dn)


def _mix_bwd(dh1, z, s, q, ac, m, b_gate, ln_g, ln_b, pool_scale, g_mixw, g_pool, qs):
    tp = dh1.shape[0]
    nt = tp // TMS
    ex = _ChipExchange(qs)
    nq = ex.n

    def body(*refs):
        dh1_ref, zga, zgb, s_ref, q_ref, ac_ref, m_ref, bg_ref, lg_ref, lb_ref, ps_ref, wm_hbm, wp_hbm = refs[:13]
        dac_ref, dm_ref, dzg_ref, dyc_ref, dyp_ref, dm2_ref, acc_ref = refs[13 + nq:20 + nq]
        wm, wp, sems = refs[20 + 2 * nq:23 + 2 * nq]
        ex.bind(refs[13:13 + nq], refs[20 + nq:20 + 2 * nq], refs[23 + 2 * nq:])
        first = pl.program_id(0) == 0

        @pl.when(first)
        def _():
            ex.issue()
            acc_ref[...] = jnp.zeros_like(acc_ref)

        _load_once(first, [(wm_hbm, wm), (wp_hbm, wp)], sems)

        dmerged = _dot_nt(dh1_ref[...].astype(BF16), wm[:, 2].reshape(D, D))
        ga = _sig(zga[...] + bg_ref[:, :D])
        gb = _sig(zgb[...] + bg_ref[:, D:])
        dyc = dmerged * ga
        dyp = dmerged * gb
        dza = (dmerged * _dot(s_ref[...], wm[:, 0].reshape(D, D))) * (ga * (1.0 - ga))
        dzb = (dmerged * _dot(q_ref[...], wm[:, 1].reshape(D, D))) * (gb * (1.0 - gb))
        dzg_ref[:, :D] = dza.astype(BF16)
        dzg_ref[:, D:] = dzb.astype(BF16)
        acc_ref[0:1, :D] += jnp.sum(dza, axis=0, keepdims=True)
        acc_ref[0:1, D:] += jnp.sum(dzb, axis=0, keepdims=True)
        dyc_b = dyc.astype(BF16)
        dyp_b = dyp.astype(BF16)
        dyc_ref[...] = dyc_b
        dyp_ref[...] = dyp_b
        ds = _dot_nt(dyc_b, wm[:, 0].reshape(D, D))
        n, rl = _ln_stats(ac_ref[...])
        l = n * lg_ref[...] + lb_ref[...]
        sg = _sig(l)
        dl = ds * (sg * (1.0 + l * (1.0 - sg)))
        acc_ref[1:2, :D] += jnp.sum(dl * n, axis=0, keepdims=True)
        acc_ref[1:2, D:] += jnp.sum(dl, axis=0, keepdims=True)
        dn = dl * lg_ref[...]
        dac_ref[...] = rl * (dn - jnp.mean(dn, axis=-1, keepdims=True) - n * jnp.mean(dn * n, axis=-1, keepdims=True))
        dq = _dot_nt(dyp_b, wm[:, 1].reshape(D, D))
        mv = m_ref[...]
        acc_ref[2:3, :D] += jnp.sum(dq * _pool_mix(mv, wp), axis=0, keepdims=True)
        dm2 = (dq * ps_ref[...]).astype(BF16)
        dm2_ref[...] = dm2
        dm_ref[...] = jnp.concatenate(
            [_dot_nt(dm2[:, g * PG:(g + 1) * PG], wp[:, g].reshape(PG, PG)) for g in range(4)], axis=1)

        @pl.when(pl.program_id(0) == nt - 1)
        def _():
            ex.finish()

    def tile(col=0):
        return pl.BlockSpec((TMS, D), lambda i: (i, col))

    def vec(w):
        return pl.BlockSpec((1, w), lambda i: (0, 0))

    anys = pl.BlockSpec(memory_space=pl.ANY)
    f32o, b16o = jax.ShapeDtypeStruct((tp, D), F32), jax.ShapeDtypeStruct((tp, D), BF16)
    res = pl.pallas_call(
        body, name="mix_bwd", grid=(nt,),
        in_specs=[tile(), tile(3), tile(4), tile(), tile(), tile(), tile(), vec(2 * D), vec(D), vec(D), vec(D), anys, anys]
        + [anys] * nq,
        out_specs=[tile(), tile(), pl.BlockSpec((TMS, 2 * D), lambda i: (i, 0)), tile(), tile(), tile(),
                   pl.BlockSpec((8, 2 * D), lambda i: (0, 0))] + [anys] * nq,
        out_shape=[f32o, f32o, jax.ShapeDtypeStruct((tp, 2 * D), BF16), b16o, b16o, b16o,
                   jax.ShapeDtypeStruct((8, 2 * D), F32)] + ex.out_shape,
        scratch_shapes=[pltpu.VMEM((NDEV, 3, D // NDEV, D), BF16), pltpu.VMEM((NDEV, 4, PG // NDEV, PG), BF16),
                        pltpu.SemaphoreType.DMA((2,))] + ex.scratch,
        compiler_params=_params(("arbitrary",), 48),
    )(dh1, z, z, s, q, ac, m, b_gate, ln_g, ln_b, pool_scale, g_mixw, g_pool, *qs)
    return res[:7], res[7:]


def _seq_bwd(dac, dm, dzg, z, w_dw, seq, qs):
    tp = z.shape[0]
    nt = tp // TM
    ex = _ChipExchange(qs)
    nq = no = ex.n

    def body(*refs):
        dac_l, dac_c, dac_r, dm_l, dm_c, dm_r, av_l, av, av_r, ag_l, ag, ag_r, dzg_ref, w_ref = refs[:14]
        dz_ref, acc_ref = refs[14 + nq:16 + nq]
        a3, d3, m3, da3, dp3, w3, dw3, da_sc, dp_sc = refs[16 + nq + no:25 + nq + no]
        ex.bind(refs[14:14 + nq], refs[16 + nq:16 + nq + no], refs[25 + nq + no:])
        i = pl.program_id(0)
        sub = lax.broadcasted_iota(jnp.int32, (NCB, 128), 0)

        @pl.when(i == 0)
        def _():
            ex.issue()
            dw3[...] = jnp.zeros_like(dw3)
            _tm_fill(w3, 0, 4, lambda r, l: w_ref[pl.ds(r, 8), l])

        _tm_fill_ext(a3, (av_l, ag_l), (av, ag), (av_r, ag_r), lambda vg, r, l: vg[0][r, l] * _sig(vg[1][r, l]), unroll=2)
        _tm_fill_ext(d3, dac_l, dac_c, dac_r, lambda ref, r, l: ref[r, l])
        _tm_fill_ext(m3, dm_l, dm_c, dm_r, lambda ref, r, l: ref[r, l])

        def conv(g, c):
            t0 = CONV_STEPS * g
            dcur = [_tm_at(d3, t0 + t + HALO) for t in range(CONV_STEPS)]
            accs = [None] * CONV_STEPS
            for k in range(CONV_K):
                wk = _tm_at(w3, k)
                prs = []
                for t in range(CONV_STEPS):
                    term = wk * _tm_at(d3, t0 + t + CONV_K - k)
                    accs[t] = term if accs[t] is None else accs[t] + term
                    prs.append(dcur[t] * _tm_at(a3, t0 + t + k + 1))
                while len(prs) > 1:
                    prs = [prs[j] + prs[j + 1] for j in range(0, len(prs) - 1, 2)] + prs[len(prs) - len(prs) % 2:]
                dw3[_tm_rows(k), :] += prs[0]
            s = dcur[0]
            for t in range(1, CONV_STEPS):
                s = s + dcur[t]
            dw3[_tm_rows(CONV_K), :] += s
            for t in range(CONV_STEPS):
                da3[_tm_rows(t0 + t), :] = accs[t]
            return c

        lax.fori_loop(0, TM // CONV_STEPS, conv, 0)

        for b in _edge_rows(seq, tp):
            e = lax.rem(b - i * TM + HALO + tp, tp)

            @pl.when(e < TME)
            def _():
                m3[_tm_rows(e), :] = _tm_at(m3, e) * _edge_gain(b, seq, tp, sub)

        inv = _by_group(sub, [1.0 / w for w in POOL_WINDOWS])

        def pool(g, c):
            for t in range(8):
                e = 8 * g + t + HALO
                sums = _nested_windows(lambda o: _tm_at(m3, e + o), [w // 2 + 1 - w for w in POOL_WINDOWS])
                dp3[_tm_rows(8 * g + t), :] = _by_group(sub, sums) * inv
            return c

        lax.fori_loop(0, TM // 8, pool, 0, unroll=2)

        _tm_read(da3, TM // 8, lambda r, l, tile: da_sc.__setitem__((r, l), tile))
        _tm_read(dp3, TM // 8, lambda r, l, tile: dp_sc.__setitem__((r, l), tile))
        sg = _sig(ag[...])
        da = da_sc[...]
        dz_ref[:, 0:D] = (da * sg).astype(BF16)
        dz_ref[:, D:2 * D] = (da * av[...] * (sg * (1.0 - sg))).astype(BF16)
        dz_ref[:, 2 * D:3 * D] = (dp_sc[...] - dm_c[...]).astype(BF16)
        dz_ref[:, 3 * D:] = dzg_ref[...]

        @pl.when(i == nt - 1)
        def _():
            _tm_read(dw3, 4, lambda r, l, tile: acc_ref.__setitem__((r, l), tile))
            ex.finish()

    tmaj = pltpu.VMEM((TM * NCB, 128), F32)
    text = pltpu.VMEM((TME * NCB, 128), F32)
    taps = pltpu.VMEM((32 * NCB, 128), F32)
    anys = pl.BlockSpec(memory_space=pl.ANY)
    res = pl.pallas_call(
        body, name="seq_bwd", grid=(nt,),
        in_specs=_halo_specs(0, nt) + _halo_specs(0, nt) + _halo_specs(0, nt) + _halo_specs(1, nt)
        + [pl.BlockSpec((TM, 2 * D), lambda i: (i, 0)), pl.BlockSpec((32, D), lambda i: (0, 0))] + [anys] * nq,
        out_specs=[pl.BlockSpec((TM, DIN), lambda i: (i, 0)), pl.BlockSpec((32, D), lambda i: (0, 0))] + [anys] * no,
        out_shape=[jax.ShapeDtypeStruct((tp, DIN), BF16), jax.ShapeDtypeStruct((32, D), F32)] + ex.out_shape,
        scratch_shapes=[text, text, text, tmaj, tmaj, taps, taps, pltpu.VMEM((TM, D), F32), pltpu.VMEM((TM, D), F32)]
        + ex.scratch,
        compiler_params=_params(("arbitrary",), 48),
    )(dac, dac, dac, dm, dm, dm, z, z, z, z, z, z, dzg, w_dw, *qs)
    return res[:2], res[2:]


def _in_bwd(dz, h0, dh1, g_mix, w_g, seq, qs):
    tp = h0.shape[0]
    tm = _pick(tp, TM_IO)
    nt = tp // tm
    ex = _ChipExchange(qs)
    nq = no = ex.n

    def body(*refs):
        dz_ref, h_ref, dh1_ref, g_ref, w_hbm = refs[:5]
        gx_ref, gmeta_ref, acc_ref = refs[5 + nq:8 + nq]
        w_vm, sems = refs[8 + nq + no:10 + nq + no]
        ex.bind(refs[5:5 + nq], refs[8 + nq:8 + nq + no], refs[10 + nq + no:])
        i = pl.program_id(0)

        @pl.when(i == 0)
        def _():
            ex.issue()
            acc_ref[...] = jnp.zeros_like(acc_ref)

        _load_once(i == 0, _win_pairs(w_hbm, w_vm), sems)

        du = _dot_nt(dz_ref[:, :DIN // 2], w_vm[0]) + _dot_nt(dz_ref[:, DIN // 2:], w_vm[1])
        h = h_ref[...]
        r = lax.rsqrt(jnp.mean(h * h, axis=-1, keepdims=True) + RMS_EPS)
        n0 = h * r
        acc_ref[0:1, :] += jnp.sum(du * n0, axis=0, keepdims=True)
        dn = du * g_ref[...]
        gx_ref[...] = dh1_ref[...] + r * (dn - n0 * jnp.mean(dn * n0, axis=-1, keepdims=True))

        @pl.when(i == nt - 1)
        def _():
            gmeta_ref[...] = gx_ref[pl.ds(tm - N_META, N_META), :]
            ex.finish()

    tile = pl.BlockSpec((tm, D), lambda i: (i, 0))
    anys = pl.BlockSpec(memory_space=pl.ANY)
    res = pl.pallas_call(
        body, name="in_bwd", grid=(nt,),
        in_specs=[pl.BlockSpec((tm, DIN), lambda i: (i, 0)), tile, tile, pl.BlockSpec((1, D), lambda i: (0, 0)), anys]
        + [anys] * nq,
        out_specs=[tile, pl.BlockSpec((N_META, D), lambda i: (0, 0)), pl.BlockSpec((8, D), lambda i: (0, 0))] + [anys] * no,
        out_shape=[jax.ShapeDtypeStruct((seq, D), F32), jax.ShapeDtypeStruct((N_META, D), F32),
                   jax.ShapeDtypeStruct((8, D), F32)] + ex.out_shape,
        scratch_shapes=[pltpu.VMEM((2, D, DIN // 2), BF16), pltpu.SemaphoreType.DMA((NDEV,))] + ex.scratch,
        compiler_params=_params(("arbitrary",), 58),
    )(dz, h0, dh1, g_mix, w_g, *qs)
    return res[:3], res[3:]


def _wgrad_in(u, dz, qs):
    tp = u.shape[0]
    tm = _pick(tp, TM_WG)
    nt = tp // tm
    half = DIN // 2
    ex = _ChipExchange(qs)
    nq = ex.n

    def body(*refs):
        u_ref, dz_ref = refs[:2]
        o_ref, acc = refs[2 + nq], refs[3 + 2 * nq]
        ex.bind(refs[2:2 + nq], refs[3 + nq:3 + 2 * nq], refs[4 + 2 * nq:])
        h, t = pl.program_id(0), pl.program_id(1)

        @pl.when((h == 0) & (t == 0))
        def _():
            ex.issue()

        @pl.when(t == 0)
        def _():
            acc[...] = jnp.zeros_like(acc)

        acc[...] += _dot_tn(u_ref[...], dz_ref[...])

        @pl.when(t == nt - 1)
        def _():
            for d in range(4):
                o_ref[d] = acc[:, INB * d:INB * (d + 1)].astype(BF16)

        @pl.when((h == 1) & (t == nt - 1))
        def _():
            ex.finish()

    anys = pl.BlockSpec(memory_space=pl.ANY)
    res = pl.pallas_call(
        body, name="wgrad_in", grid=(2, nt),
        in_specs=[pl.BlockSpec((tm, D), lambda h, t: (t, 0)), pl.BlockSpec((tm, half), lambda h, t: (t, h))] + [anys] * nq,
        out_specs=[pl.BlockSpec((4, D, INB), lambda h, t: (h, 0, 0), pipeline_mode=pl.Buffered(1))] + [anys] * nq,
        out_shape=[jax.ShapeDtypeStruct((NDEV, D, INB), BF16)] + ex.out_shape,
        scratch_shapes=[pltpu.VMEM((D, half), F32)] + ex.scratch,
        compiler_params=_params(("arbitrary", "arbitrary"), 52),
    )(u, dz, *qs)
    return res[0], res[1:]


def _wgrad_mix(s, dyc, q, dyp, merged, dh1, m, dm2):
    tp = s.shape[0]
    tm = _pick(tp, TM_WM)
    nt = tp // tm
    rb = D // NDEV

    def body(s_ref, dyc_ref, q_ref, dyp_ref, mg_ref, dh1_ref, m_ref, dm2_ref, o_ref, op_ref, acc, accp):
        t = pl.program_id(0)

        @pl.when(t == 0)
        def _():
            acc[...] = jnp.zeros_like(acc)
            accp[...] = jnp.zeros_like(accp)

        acc[0] += _dot_tn(s_ref[...], dyc_ref[...])
        acc[1] += _dot_tn(q_ref[...], dyp_ref[...])
        acc[2] += _dot_tn(mg_ref[...], dh1_ref[...].astype(BF16))
        for g in range(4):
            accp[g] += _dot_tn(m_ref[:, g * PG:(g + 1) * PG], dm2_ref[:, g * PG:(g + 1) * PG])

        @pl.when(t == nt - 1)
        def _():
            for d in range(NDEV):
                for k in range(3):
                    o_ref[d, k] = acc[k, rb * d:rb * (d + 1), :].astype(BF16)
                for g in range(4):
                    op_ref[d, g] = accp[g, 32 * d:32 * (d + 1), :].astype(BF16)

    tile = pl.BlockSpec((tm, D), lambda t: (t, 0))
    return pl.pallas_call(
        body, name="wgrad_mix", grid=(nt,),
        in_specs=[tile] * 8,
        out_specs=[pl.BlockSpec((NDEV, 3, rb, D), lambda t: (0, 0, 0, 0), pipeline_mode=pl.Buffered(1)),
                   pl.BlockSpec((NDEV, 4, 32, PG), lambda t: (0, 0, 0, 0), pipeline_mode=pl.Buffered(1))],
        out_shape=[jax.ShapeDtypeStruct((NDEV, 3, rb, D), BF16), jax.ShapeDtypeStruct((NDEV, 4, 32, PG), BF16)],
        scratch_shapes=[pltpu.VMEM((3, D, D), F32), pltpu.VMEM((4, PG, PG), F32)],
        compiler_params=_params(("arbitrary",), 56),
    )(s, dyc, q, dyp, merged, dh1, m, dm2)


def _wgrad_gu(v, dfg, dfu):
    tp = v.shape[0]
    tm = _pick(tp, TM_WG)
    nt = tp // tm

    def body(v_ref, dg_ref, du_ref, o_ref, acc):
        k, t = pl.program_id(0), pl.program_id(2)

        @pl.when(t == 0)
        def _():
            acc[...] = jnp.zeros_like(acc)

        @pl.when(k == 0)
        def _():
            acc[...] += _dot_tn(dg_ref[...], v_ref[...])

        @pl.when(k == 1)
        def _():
            acc[...] += _dot_tn(du_ref[...], v_ref[...])

        @pl.when(t == nt - 1)
        def _():
            for d in range(4):
                o_ref[d] = acc[FFB * d:FFB * (d + 1), :].astype(BF16)

    return pl.pallas_call(
        body, name="wgrad_gu", grid=(2, 2, nt),
        in_specs=[pl.BlockSpec((tm, D), lambda k, h, t: (t, 0)),
                  pl.BlockSpec((tm, FFC), lambda k, h, t: (t * (1 - k), h * (1 - k))),
                  pl.BlockSpec((tm, FFC), lambda k, h, t: (t * k, h * k))],
        out_specs=pl.BlockSpec((4, None, FFB, D), lambda k, h, t: (h, k, 0, 0), pipeline_mode=pl.Buffered(1)),
        out_shape=jax.ShapeDtypeStruct((NDEV, 2, FFB, D), BF16),
        scratch_shapes=[pltpu.VMEM((FFC, D), F32)],
        compiler_params=_params(("arbitrary",) * 3, 48),
    )(v, dfg, dfu)


def _wgrad_down(f, dh2):
    tp = f.shape[0]
    tm = _pick(tp, TM_WG)
    nt = tp // tm

    def body(f_ref, d_ref, o_ref, acc):
        t = pl.program_id(1)

        @pl.when(t == 0)
        def _():
            acc[...] = jnp.zeros_like(acc)

        acc[...] += _dot_tn(f_ref[...], d_ref[...].astype(BF16))

        @pl.when(t == nt - 1)
        def _():
            for d in range(4):
                o_ref[d] = acc[FFB * d:FFB * (d + 1), :].astype(BF16)

    return pl.pallas_call(
        body, name="wgrad_down", grid=(2, nt),
        in_specs=[pl.BlockSpec((tm, FFC), lambda h, t: (t, h)), pl.BlockSpec((tm, D), lambda h, t: (t, 0))],
        out_specs=pl.BlockSpec((4, FFB, D), lambda h, t: (h, 0, 0), pipeline_mode=pl.Buffered(1)),
        out_shape=jax.ShapeDtypeStruct((NDEV, FFB, D), BF16),
        scratch_shapes=[pltpu.VMEM((FFC, D), F32)],
        compiler_params=_params(("arbitrary", "arbitrary"), 48),
    )(f, dh2)


def kernel(x, meta_tokens, g_mix, w_in, b_gate, w_dw, b_dw, ln_g, ln_b, w_conv_out, w_pool, pool_scale, w_pool_out, w_o, g_ffn, w_ffn_gate, w_ffn_up, w_ffn_down, g_final, loss_target, m_meta_tokens, m_g_mix, m_w_in, m_b_gate, m_w_dw, m_b_dw, m_ln_g, m_ln_b, m_w_conv_out, m_w_pool, m_pool_scale, m_w_pool_out, m_w_o, m_g_ffn, m_w_ffn_gate, m_w_ffn_up, m_w_ffn_down, m_g_final, v_meta_tokens, v_g_mix, v_w_in, v_b_gate, v_w_dw, v_b_dw, v_ln_g, v_ln_b, v_w_conv_out, v_w_pool, v_pool_scale, v_w_pool_out, v_w_o, v_g_ffn, v_w_ffn_gate, v_w_ffn_up, v_w_ffn_down, v_g_final):
    seq = x.shape[1]
    tp = -(-(seq + 2 * HALO) // TM) * TM
    tm_in = _pick(tp, TM_IO)
    nx_last = seq - (tp // tm_in - 1) * tm_in
    assert 0 < nx_last <= tm_in - 2 * HALO and nx_last % 8 == 0 and 0 < seq - (tp // TM - 1) * TM

    whole = (Ellipsis,)
    ag_small = _Gather(
        [((48, D // NDEV), [(meta_tokens, pl.ds(0, N_META), whole), (w_dw, pl.ds(N_META, CONV_K), 0)])], [F32])
    ag_mix = _Gather([((3, D // NDEV, D), [(w_conv_out, 0, 0), (w_pool_out, 1, 0), (w_o, 2, 0)]),
                      ((4, PG // NDEV, PG), [(w_pool, whole, 0)])], [BF16, BF16])
    def tr(a):
        return jnp.swapaxes(a, 1, 2)

    ag_gu = _Gather([((2, FFB, D), [(tr(w_ffn_gate), 0, 0), (tr(w_ffn_up), 1, 0)])], [BF16])
    ag_dn = _Gather([((FFB, D), [(w_ffn_down, whole, 0)])], [BF16])

    mx, my = lax.axis_index("x"), lax.axis_index("y")
    order = jnp.stack([2 * mx + my, 2 * mx + 1 - my, 2 * (1 - mx) + my, 2 * (1 - mx) + 1 - my]).astype(jnp.int32)
    (h0, z, u, g_in), (g_mixw, g_pool), g_small = _fwd_in(x[0], g_mix, w_in, order, tp, ag_mix, ag_small)
    wdw_full = g_small.transpose(1, 0, 2).reshape(48, D)[N_META:]
    (ac, m), (w_gu,) = _seq_fwd(z, wdw_full, b_dw, seq, ag_gu)
    (h1, s, merged, q), (g_down,) = _mix_fwd(ac, m, z, h0, b_gate, ln_g, ln_b, pool_scale, g_mixw, g_pool, ag_dn)
    w_dn = g_down.reshape(2, FFC, D)
    fg, fu, v, f, dh2, head_acc = _ffn_fwd(h1, loss_target[0], g_ffn, g_final.reshape(1, D), w_gu, w_dn)

    dfg, dfu, dh1, ffn_acc = _ffn_bwd(dh2, fg, fu, h1, g_ffn, w_gu, w_dn)
    p_gu, p_dn = _wgrad_gu(v, dfg, dfu), _wgrad_down(f, dh2)
    (dac, dm, dzg, dyc, dyp, dm2, mix_acc), _ = _mix_bwd(
        dh1, z, s, q, ac, m, b_gate, ln_g, ln_b, pool_scale, g_mixw, g_pool, [])
    p_mix = _wgrad_mix(s, dyc, q, dyp, merged, dh1, m, dm2)
    own_p, sib_p, q_p = _rs_pair("rs_pair_ffn_mix", [p_gu, p_dn, *p_mix])
    own_f, sib_f, own_m, sib_m = own_p[:2], sib_p[:2], own_p[2:], sib_p[2:]
    (dz, seq_acc), rel_gm = _seq_bwd(dac, dm, dzg, z, wdw_full, seq, [q_p[0], q_p[2], q_p[3]])
    p_in, rel_dn = _wgrad_in(u, dz, [q_p[1]])
    rel_f, rel_m = [rel_gm[0], rel_dn[0]], rel_gm[1:]
    own_i, sib_i, q_i = _rs_pair("rs_pair_in", [p_in])
    (grad_x, g_meta, in_acc), rel_i = _in_bwd(dz, h0, dh1, g_mix, g_in, seq, q_i)
    small_g = jnp.concatenate([g_meta, seq_acc[:CONV_K], jnp.zeros((1, D), F32)], axis=0)
    p_small = small_g.reshape(48, NDEV, D // NDEV).transpose(1, 0, 2).astype(BF16)
    rep_g = jnp.concatenate([
        in_acc[0:1], mix_acc[0:1, :D], mix_acc[0:1, D:], seq_acc[CONV_K:CONV_K + 1], mix_acc[1:2, :D], mix_acc[1:2, D:],
        mix_acc[2:3, :D], ffn_acc[0:1], head_acc[1:2], head_acc[0:1], jnp.zeros((REP_ROWS - 10, D), F32)], axis=0)
    own_s, sib_s, rel_s, rep_all = _reduce_scatter([p_small], rep_g)
    owns = [own_i[0], own_s[0], own_m[0], own_m[1], own_f[0], own_f[1]]
    sibs = [sib_i[0], sib_s[0], sib_m[0], sib_m[1], sib_f[0], sib_f[1]]
    rels = [rel_i[0], rel_s[0], rel_m[0], rel_m[1], rel_f[0], rel_f[1]]

    def lead(a):
        return a.reshape(1, *a.shape)

    def stack4(a, lead_dims):
        return a.reshape(*lead_dims, 1, 4 * 32, PG)

    (r_in,) = _adamw_multi("adamw_in", lead(owns[0]), sibs[0][:, None], rels[0][:, None], [w_in], [m_w_in], [v_w_in], 4)
    r_meta, r_dw = _adamw_meta_dw(owns[1], sibs[1], rels[1], (meta_tokens, m_meta_tokens, v_meta_tokens),
                                  (w_dw, m_w_dw, v_w_dw))
    r_conv, r_pout, r_o = _adamw_multi("adamw_mix", owns[2], sibs[2], rels[2], [w_conv_out, w_pool_out, w_o],
                                       [m_w_conv_out, m_w_pool_out, m_w_o], [v_w_conv_out, v_w_pool_out, v_w_o], 1)
    (r_pool,) = _adamw_multi("adamw_pool", stack4(owns[3], ()), stack4(sibs[3], (1,)), stack4(rels[3], (3,)),
                             [w_pool.reshape(1, 128, PG)], [m_w_pool.reshape(1, 128, PG)], [v_w_pool.reshape(1, 128, PG)], 1)
    r_pool = tuple(a.reshape(w_pool.shape) for a in r_pool)
    r_gate, r_up = _adamw_multi("adamw_gu", owns[4], sibs[4], rels[4], [tr(w_ffn_gate), tr(w_ffn_up)],
                                [tr(m_w_ffn_gate), tr(m_w_ffn_up)], [tr(v_w_ffn_gate), tr(v_w_ffn_up)], 2)
    r_gate, r_up = tuple(tr(a) for a in r_gate), tuple(tr(a) for a in r_up)
    (r_down,) = _adamw_multi("adamw_down", lead(owns[5]), sibs[5][:, None], rels[5][:, None],
                             [w_ffn_down], [m_w_ffn_down], [v_w_ffn_down], 2)
    row = (1, D)
    loss, reps = _adamw_rep(
        rep_all,
        [g_mix, b_gate, b_dw, ln_g, ln_b, pool_scale, g_ffn, g_final.reshape(row)],
        [m_g_mix, m_b_gate, m_b_dw, m_ln_g, m_ln_b, m_pool_scale, m_g_ffn, m_g_final.reshape(row)],
        [v_g_mix, v_b_gate, v_b_dw, v_ln_g, v_ln_b, v_pool_scale, v_g_ffn, v_g_final.reshape(row)])
    r_gmix, r_bg, r_bdw, r_lg, r_lb, r_ps, r_gffn, r_gfin = reps
    r_gfin = tuple(a.reshape(D) for a in r_gfin)

    in_order = [r_meta, r_gmix, r_in, r_bg, r_dw, r_bdw, r_lg, r_lb, r_conv, r_pool, r_ps, r_pout, r_o, r_gffn,
                r_gate, r_up, r_down, r_gfin]
    return (loss.reshape(()), grad_x[None], *[r[0] for r in in_order], *[r[1] for r in in_order],
            *[r[2] for r in in_order], *[r[3] for r in in_order])
```
